```python
import jax, jax.numpy as jnp
from jax import lax
import numpy as np

D_MODEL = 1024
BATCH = 8
SEQ = 8192
DEPTH = 2

HEAD_DIM = 64
POOL_WIDTH = D_MODEL // 4
POOL_WINDOWS = (2, 4, 8, 16)
N_POOL_GROUPS = len(POOL_WINDOWS)
POOL_GC = POOL_WIDTH // N_POOL_GROUPS
ATTN_WIDTH = D_MODEL - POOL_WIDTH
N_ATTN_HEADS = ATTN_WIDTH // HEAD_DIM
DIL_PATTERNS = ((128, 1), (512, 4), (2048, 16))
HEADS_PER_PATTERN = N_ATTN_HEADS // len(DIL_PATTERNS)
ROT_DIM = HEAD_DIM // 4
ROPE_THETA = 500000.0
BLK = 128
D_FF = 4 * D_MODEL
PLE_DIM = 256
EPS = 1e-6

kernel_name = "hybrid_pool_dilated_attn_block"


def rmsnorm(x, g):
    xf = x.astype(jnp.float32)
    y = xf * lax.rsqrt(jnp.mean(xf * xf, axis=-1, keepdims=True) + EPS)
    return (y * g.astype(jnp.float32)).astype(x.dtype)


def rotary_tables(positions, dtype):
    inv_freq = ROPE_THETA ** (-jnp.arange(0, ROT_DIM, 2, dtype=jnp.float32) / ROT_DIM)
    ang = positions.astype(jnp.float32)[..., None] * inv_freq
    return jnp.cos(ang)[:, :, None, :].astype(dtype), jnp.sin(ang)[:, :, None, :].astype(dtype)


def apply_partial_rotary(x, cos, sin):
    half = ROT_DIM // 2
    x1 = x[..., :half]
    x2 = x[..., half:ROT_DIM]
    rot = jnp.concatenate([x1 * cos - x2 * sin, x2 * cos + x1 * sin], axis=-1)
    return jnp.concatenate([rot, x[..., ROT_DIM:]], axis=-1)


def pool_mixer(u, w, scale):
    B, S, _ = u.shape
    ug = u.reshape(B, S, N_POOL_GROUPS, POOL_GC).astype(jnp.float32)
    c = lax.cumsum(ug, axis=1)
    c0 = jnp.pad(c, ((0, 0), (1, 0), (0, 0), (0, 0)))
    t = jnp.arange(S, dtype=jnp.int32)
    win = jnp.array(POOL_WINDOWS, dtype=jnp.int32)
    lo = jnp.maximum(t[:, None] + 1 - win[None, :], 0)
    c_lo = jnp.take_along_axis(c0, lo[None, :, :, None], axis=1)
    cnt = (t[:, None] + 1 - lo).astype(jnp.float32)
    y = ((c - c_lo) / cnt[None, :, :, None] - ug).astype(u.dtype)
    y = jnp.einsum('bsgc,gcd->bsgd', y, w).reshape(B, S, POOL_WIDTH)
    return y * scale


def dilated_window_attention(q, k, v, window, dil):
    B, S, H, Dh = q.shape
    steps = window // dil
    L = -(-S // (dil * BLK)) * BLK
    pad = L * dil - S
    nb = L // BLK

    def to_strided(a):
        a = jnp.pad(a, ((0, 0), (0, pad), (0, 0), (0, 0)))
        a = a.reshape(B, L, dil, H, Dh).transpose(0, 2, 1, 3, 4)
        return a.reshape(B, dil, nb, BLK, H, Dh)

    def with_prev(a):
        prev = jnp.pad(a[:, :, :-1], ((0, 0), (0, 0), (1, 0), (0, 0), (0, 0), (0, 0)))
        return jnp.concatenate([prev, a], axis=3)

    qs = to_strided(q)
    kb = with_prev(to_strided(k))
    vb = with_prev(to_strided(v))
    s = jnp.einsum('brnqhd,brnkhd->brnhqk', qs, kb,
                   preferred_element_type=jnp.float32) * (HEAD_DIM ** -0.5)
    qi = jnp.arange(nb)[:, None] * BLK + jnp.arange(BLK)[None, :]
    ki = jnp.arange(nb)[:, None] * BLK - BLK + jnp.arange(2 * BLK)[None, :]
    dist = qi[:, :, None] - ki[:, None, :]
    valid = (dist >= 0) & (dist <= steps) & (ki[:, None, :] >= 0)
    s = jnp.where(valid[None, None, :, None], s, -jnp.inf)
    m = jnp.max(s, axis=-1, keepdims=True)
    e = jnp.exp(s - m)
    l = jnp.sum(e, axis=-1, keepdims=True)
    o = jnp.einsum('brnhqk,brnkhd->brnqhd', e / l, vb.astype(jnp.float32))
    lse = (m + jnp.log(l))[..., 0]
    o = o.reshape(B, dil, L, H, Dh).transpose(0, 2, 1, 3, 4).reshape(B, L * dil, H, Dh)[:, :S]
    lse = lse.transpose(0, 1, 2, 4, 3).reshape(B, dil, L, H).transpose(0, 2, 1, 3)
    lse = lse.reshape(B, L * dil, H)[:, :S]
    return o, lse


def dilated_mixer(q, k, v):
    outs, lses = [], []
    for g, (window, dil) in enumerate(DIL_PATTERNS):
        sl = slice(g * HEADS_PER_PATTERN, (g + 1) * HEADS_PER_PATTERN)
        o, lse = dilated_window_attention(q[:, :, sl], k[:, :, sl], v[:, :, sl], window, dil)
        outs.append(o)
        lses.append(lse)
    wts = jax.nn.softmax(jnp.stack(lses, axis=0), axis=0)
    o = jnp.concatenate([outs[g] * wts[g][..., None] for g in range(len(DIL_PATTERNS))], axis=2)
    B, S = q.shape[0], q.shape[1]
    return o.reshape(B, S, ATTN_WIDTH).astype(q.dtype)


def _fwd_setup_inputs(seed: int = 0) -> dict:
    key = jax.random.key(seed)
    ks = jax.random.split(key, 16)
    f32 = jnp.float32
    n_in = POOL_WIDTH + 3 * ATTN_WIDTH
    return {
        "x": jax.random.normal(ks[0], (BATCH, SEQ, D_MODEL), f32),
        "p": jax.random.normal(ks[1], (DEPTH, BATCH, SEQ, PLE_DIM), f32),
        "positions": jnp.broadcast_to(jnp.arange(SEQ, dtype=jnp.int32), (BATCH, SEQ)),
        "norm1": 1.0 + 0.02 * jax.random.normal(ks[2], (DEPTH, D_MODEL), f32),
        "w_in": jax.random.normal(ks[3], (DEPTH, D_MODEL, n_in), f32) * D_MODEL ** -0.5,
        "pool_w": jax.random.normal(ks[4], (DEPTH, N_POOL_GROUPS, POOL_GC, POOL_GC), f32) * POOL_GC ** -0.5,
        "pool_scale": 1.0 + 0.02 * jax.random.normal(ks[5], (DEPTH, POOL_WIDTH), f32),
        "w_out": jax.random.normal(ks[6], (DEPTH, POOL_WIDTH + ATTN_WIDTH, D_MODEL), f32) * (POOL_WIDTH + ATTN_WIDTH) ** -0.5,
        "norm2": 1.0 + 0.02 * jax.random.normal(ks[7], (DEPTH, D_MODEL), f32),
        "w_up": jax.random.normal(ks[8], (DEPTH, D_MODEL, D_FF), f32) * D_MODEL ** -0.5,
        "w_down": jax.random.normal(ks[9], (DEPTH, D_FF, D_MODEL), f32) * D_FF ** -0.5,
        "norm3": 1.0 + 0.02 * jax.random.normal(ks[10], (DEPTH, D_MODEL), f32),
        "w_gate": jax.random.normal(ks[11], (DEPTH, D_MODEL, D_MODEL), f32) * D_MODEL ** -0.5,
        "w_ple": jax.random.normal(ks[12], (DEPTH, PLE_DIM, D_MODEL), f32) * PLE_DIM ** -0.5,
        "final_norm": 1.0 + 0.02 * jax.random.normal(ks[13], (D_MODEL,), f32),
    }


def _fwd_reference(x, p, positions, norm1, w_in, pool_w, pool_scale, w_out, norm2, w_up, w_down,
              norm3, w_gate, w_ple, final_norm):
    B, S, _ = x.shape
    cos, sin = rotary_tables(positions, x.dtype)
    h = x
    for i in range(DEPTH):
        hn = rmsnorm(h, norm1[i])
        z = hn @ w_in[i]
        u = z[..., :POOL_WIDTH]
        q, k, v = jnp.split(z[..., POOL_WIDTH:], 3, axis=-1)
        q = apply_partial_rotary(q.reshape(B, S, N_ATTN_HEADS, HEAD_DIM), cos, sin)
        k = apply_partial_rotary(k.reshape(B, S, N_ATTN_HEADS, HEAD_DIM), cos, sin)
        v = v.reshape(B, S, N_ATTN_HEADS, HEAD_DIM)
        pool_out = pool_mixer(u, pool_w[i], pool_scale[i])
        attn_out = dilated_mixer(q, k, v)
        h = h + jnp.concatenate([pool_out, attn_out], axis=-1) @ w_out[i]
        hn = rmsnorm(h, norm2[i])
        h = h + jnp.square(jax.nn.relu(hn @ w_up[i])) @ w_down[i]
        gate = jax.nn.sigmoid(rmsnorm(h, norm3[i]) @ w_gate[i])
        h = h + gate * (p[i] @ w_ple[i])
    return rmsnorm(h, final_norm)


import jax as _jax
import jax.numpy as _jnp

TWIN_FORMAT = 'train_step'
FWD_PARAMS = ['x', 'p', 'positions', 'norm1', 'w_in', 'pool_w', 'pool_scale', 'w_out', 'norm2', 'w_up', 'w_down', 'norm3', 'w_gate', 'w_ple', 'final_norm']
TWIN_WEIGHTS = ['norm1', 'w_in', 'pool_w', 'pool_scale', 'w_out', 'norm2', 'w_up', 'w_down', 'norm3', 'w_gate', 'w_ple', 'final_norm']
TWIN_DIFF_INPUT = 'x'
TWIN_INPUTS = ['x', 'p', 'positions', 'norm1', 'w_in', 'pool_w', 'pool_scale', 'w_out', 'norm2', 'w_up', 'w_down', 'norm3', 'w_gate', 'w_ple', 'final_norm', 'loss_target', 'm_norm1', 'm_w_in', 'm_pool_w', 'm_pool_scale', 'm_w_out', 'm_norm2', 'm_w_up', 'm_w_down', 'm_norm3', 'm_w_gate', 'm_w_ple', 'm_final_norm', 'v_norm1', 'v_w_in', 'v_pool_w', 'v_pool_scale', 'v_w_out', 'v_norm2', 'v_w_up', 'v_w_down', 'v_norm3', 'v_w_gate', 'v_w_ple', 'v_final_norm']
TWIN_OUTPUTS = ['loss', 'grad_x', 'grad_norm1', 'grad_w_in', 'grad_pool_w', 'grad_pool_scale', 'grad_w_out', 'grad_norm2', 'grad_w_up', 'grad_w_down', 'grad_norm3', 'grad_w_gate', 'grad_w_ple', 'grad_final_norm', 'delta_norm1', 'delta_w_in', 'delta_pool_w', 'delta_pool_scale', 'delta_w_out', 'delta_norm2', 'delta_w_up', 'delta_w_down', 'delta_norm3', 'delta_w_gate', 'delta_w_ple', 'delta_final_norm', 'new_m_norm1', 'new_m_w_in', 'new_m_pool_w', 'new_m_pool_scale', 'new_m_w_out', 'new_m_norm2', 'new_m_w_up', 'new_m_w_down', 'new_m_norm3', 'new_m_w_gate', 'new_m_w_ple', 'new_m_final_norm', 'new_v_norm1', 'new_v_w_in', 'new_v_pool_w', 'new_v_pool_scale', 'new_v_w_out', 'new_v_norm2', 'new_v_w_up', 'new_v_w_down', 'new_v_norm3', 'new_v_w_gate', 'new_v_w_ple', 'new_v_final_norm']
TWIN_LEAF_KINDS = {'loss': 'loss', 'grad_x': 'grad_x', 'grad_norm1': 'grad_w', 'grad_w_in': 'grad_w', 'grad_pool_w': 'grad_w', 'grad_pool_scale': 'grad_w', 'grad_w_out': 'grad_w', 'grad_norm2': 'grad_w', 'grad_w_up': 'grad_w', 'grad_w_down': 'grad_w', 'grad_norm3': 'grad_w', 'grad_w_gate': 'grad_w', 'grad_w_ple': 'grad_w', 'grad_final_norm': 'grad_w', 'delta_norm1': 'delta_w', 'delta_w_in': 'delta_w', 'delta_pool_w': 'delta_w', 'delta_pool_scale': 'delta_w', 'delta_w_out': 'delta_w', 'delta_norm2': 'delta_w', 'delta_w_up': 'delta_w', 'delta_w_down': 'delta_w', 'delta_norm3': 'delta_w', 'delta_w_gate': 'delta_w', 'delta_w_ple': 'delta_w', 'delta_final_norm': 'delta_w', 'new_m_norm1': 'new_m', 'new_m_w_in': 'new_m', 'new_m_pool_w': 'new_m', 'new_m_pool_scale': 'new_m', 'new_m_w_out': 'new_m', 'new_m_norm2': 'new_m', 'new_m_w_up': 'new_m', 'new_m_w_down': 'new_m', 'new_m_norm3': 'new_m', 'new_m_w_gate': 'new_m', 'new_m_w_ple': 'new_m', 'new_m_final_norm': 'new_m', 'new_v_norm1': 'new_v', 'new_v_w_in': 'new_v', 'new_v_pool_w': 'new_v', 'new_v_pool_scale': 'new_v', 'new_v_w_out': 'new_v', 'new_v_norm2': 'new_v', 'new_v_w_up': 'new_v', 'new_v_w_down': 'new_v', 'new_v_norm3': 'new_v', 'new_v_w_gate': 'new_v', 'new_v_w_ple': 'new_v', 'new_v_final_norm': 'new_v'}


def _forward(args):
    return _fwd_reference(*[args[k] for k in FWD_PARAMS])


def _output_shape():
    def fwd():
        inp = _fwd_setup_inputs(0)
        return _fwd_reference(*[inp[k] for k in FWD_PARAMS])
    out = _jax.eval_shape(fwd)
    return out.shape, out.dtype

N_MICROBATCH = 1
ADAM_LR = 0.001
ADAM_B1 = 0.9
ADAM_B2 = 0.999
ADAM_EPS = 1e-08
ADAM_WD = 0.01
ADAM_STEP = 10
PER_EXAMPLE_BATCH_AXIS = {'x': 0, 'p': 1, 'positions': 0, 'loss_target': 0}
SHARED_INPUTS = []
_WEIGHT_DTYPES = {'norm1': _jnp.float32, 'w_in': _jnp.float32, 'pool_w': _jnp.float32, 'pool_scale': _jnp.float32, 'w_out': _jnp.float32, 'norm2': _jnp.float32, 'w_up': _jnp.float32, 'w_down': _jnp.float32, 'norm3': _jnp.float32, 'w_gate': _jnp.float32, 'w_ple': _jnp.float32, 'final_norm': _jnp.float32}
MOMENT_SCALE = {'norm1': 9.495852e-02, 'w_in': 5.876835e-02, 'pool_w': 1.787005e-01, 'pool_scale': 1.928137e-01, 'w_out': 8.998696e-02, 'norm2': 1.914469e-01, 'w_up': 9.580857e-02, 'w_down': 2.063273e-01, 'norm3': 2.827731e-02, 'w_gate': 2.921951e-02, 'w_ple': 7.155831e-02, 'final_norm': 6.483060e+01}


def _to_microbatches(a, axis):
    t = _jnp.moveaxis(a, axis, 0)
    t = t.reshape((N_MICROBATCH, t.shape[0] // N_MICROBATCH) + t.shape[1:])
    return _jnp.moveaxis(t, 1, axis + 1)


def setup_inputs(seed: int = 0) -> dict:
    inp = _fwd_setup_inputs(seed)
    key = _jax.random.fold_in(_jax.random.key(seed), 7919)
    shape, _ = _output_shape()
    out = dict(inp)
    out["loss_target"] = _jax.random.normal(_jax.random.fold_in(key, 0), shape, _jnp.float32)
    for i, name in enumerate(TWIN_WEIGHTS):
        w = inp[name].astype(_jnp.float32)
        if MOMENT_SCALE is None:
            s = _jnp.sqrt(_jnp.mean(_jnp.square(w)) + 1e-30)
        else:
            s = MOMENT_SCALE[name]
        km, kv = _jax.random.split(_jax.random.fold_in(key, i + 1))
        out[name] = w
        out["m_" + name] = s * _jax.random.normal(km, w.shape, _jnp.float32)
        out["v_" + name] = (s * s) * _jax.random.uniform(kv, w.shape, _jnp.float32, 0.5, 1.5)
    if N_MICROBATCH > 1:
        for name, axis in PER_EXAMPLE_BATCH_AXIS.items():
            out[name] = _to_microbatches(out[name], axis)
    return {'x': out['x'], 'p': out['p'], 'positions': out['positions'], 'norm1': out['norm1'], 'w_in': out['w_in'], 'pool_w': out['pool_w'], 'pool_scale': out['pool_scale'], 'w_out': out['w_out'], 'norm2': out['norm2'], 'w_up': out['w_up'], 'w_down': out['w_down'], 'norm3': out['norm3'], 'w_gate': out['w_gate'], 'w_ple': out['w_ple'], 'final_norm': out['final_norm'], 'loss_target': out['loss_target'], 'm_norm1': out['m_norm1'], 'm_w_in': out['m_w_in'], 'm_pool_w': out['m_pool_w'], 'm_pool_scale': out['m_pool_scale'], 'm_w_out': out['m_w_out'], 'm_norm2': out['m_norm2'], 'm_w_up': out['m_w_up'], 'm_w_down': out['m_w_down'], 'm_norm3': out['m_norm3'], 'm_w_gate': out['m_w_gate'], 'm_w_ple': out['m_w_ple'], 'm_final_norm': out['m_final_norm'], 'v_norm1': out['v_norm1'], 'v_w_in': out['v_w_in'], 'v_pool_w': out['v_pool_w'], 'v_pool_scale': out['v_pool_scale'], 'v_w_out': out['v_w_out'], 'v_norm2': out['v_norm2'], 'v_w_up': out['v_w_up'], 'v_w_down': out['v_w_down'], 'v_norm3': out['v_norm3'], 'v_w_gate': out['v_w_gate'], 'v_w_ple': out['v_w_ple'], 'v_final_norm': out['v_final_norm']}


def _loss(weights, diff, rest, loss_target):
    with _jax.named_scope("forward"):
        args = {**rest, TWIN_DIFF_INPUT: diff, **{k: w.astype(_WEIGHT_DTYPES[k]) for k, w in weights.items()}}
        y = _forward(args)
    with _jax.named_scope("loss_head"):
        err = _jnp.square(y.astype(_jnp.float32) - loss_target)
        return 0.5 * _jnp.sum(_jnp.mean(err, axis=-1)) if err.ndim else 0.5 * err


def _adamw(w, g, m, v):
    m = ADAM_B1 * m + (1.0 - ADAM_B1) * g
    v = ADAM_B2 * v + (1.0 - ADAM_B2) * _jnp.square(g)
    m_hat = m / (1.0 - ADAM_B1 ** ADAM_STEP)
    v_hat = v / (1.0 - ADAM_B2 ** ADAM_STEP)
    delta = -ADAM_LR * (m_hat / (_jnp.sqrt(v_hat) + ADAM_EPS) + ADAM_WD * w)
    return delta, m, v


def reference(x, p, positions, norm1, w_in, pool_w, pool_scale, w_out, norm2, w_up, w_down, norm3, w_gate, w_ple, final_norm, loss_target, m_norm1, m_w_in, m_pool_w, m_pool_scale, m_w_out, m_norm2, m_w_up, m_w_down, m_norm3, m_w_gate, m_w_ple, m_final_norm, v_norm1, v_w_in, v_pool_w, v_pool_scale, v_w_out, v_norm2, v_w_up, v_w_down, v_norm3, v_w_gate, v_w_ple, v_final_norm):
    given = dict(x=x, p=p, positions=positions, norm1=norm1, w_in=w_in, pool_w=pool_w, pool_scale=pool_scale, w_out=w_out, norm2=norm2, w_up=w_up, w_down=w_down, norm3=norm3, w_gate=w_gate, w_ple=w_ple, final_norm=final_norm, loss_target=loss_target, m_norm1=m_norm1, m_w_in=m_w_in, m_pool_w=m_pool_w, m_pool_scale=m_pool_scale, m_w_out=m_w_out, m_norm2=m_norm2, m_w_up=m_w_up, m_w_down=m_w_down, m_norm3=m_norm3, m_w_gate=m_w_gate, m_w_ple=m_w_ple, m_final_norm=m_final_norm, v_norm1=v_norm1, v_w_in=v_w_in, v_pool_w=v_pool_w, v_pool_scale=v_pool_scale, v_w_out=v_w_out, v_norm2=v_norm2, v_w_up=v_w_up, v_w_down=v_w_down, v_norm3=v_norm3, v_w_gate=v_w_gate, v_w_ple=v_w_ple, v_final_norm=v_final_norm)
    weights = {n: given[n] for n in TWIN_WEIGHTS}
    shared = {n: given[n] for n in SHARED_INPUTS}
    per_example = {n: given[n] for n in ['x', 'p', 'positions']}
    grad_fn = _jax.value_and_grad(_loss, argnums=(0, 1))

    def one_microbatch(ex, loss_target):
        ex = dict(ex)
        diff = ex.pop(TWIN_DIFF_INPUT)
        return grad_fn(weights, diff, {**shared, **ex}, loss_target)

    if N_MICROBATCH == 1:
        loss, (grad_w, grad_x) = one_microbatch(per_example, given["loss_target"])
    else:
        def body(carry, xs):
            loss_sum, grad_sum = carry
            l_k, (gw_k, gx_k) = one_microbatch(xs[0], xs[1])
            with _jax.named_scope("update"):
                return (loss_sum + l_k, _jax.tree.map(_jnp.add, grad_sum, gw_k)), gx_k

        init = (_jnp.zeros((), _jnp.float32), _jax.tree.map(_jnp.zeros_like, weights))
        (loss, grad_w), grad_x = _jax.lax.scan(body, init, (per_example, given["loss_target"]))
    with _jax.named_scope("update"):
        delta_w, new_m, new_v = {}, {}, {}
        for n in TWIN_WEIGHTS:
            delta_w[n], new_m[n], new_v[n] = _adamw(weights[n], grad_w[n], given["m_" + n], given["v_" + n])
    return (loss, grad_x, *[grad_w[n] for n in TWIN_WEIGHTS], *[delta_w[n] for n in TWIN_WEIGHTS],
            *[new_m[n] for n in TWIN_WEIGHTS], *[new_v[n] for n in TWIN_WEIGHTS])
```

```python
import functools

import jax
import jax.numpy as jnp
from jax import lax
from jax.experimental import pallas as pl
from jax.experimental.pallas import tpu as pltpu

F32 = jnp.float32
BF16 = jnp.bfloat16

D_MODEL = 1024
HEAD_DIM = 64
POOL_WIDTH = 256
POOL_WINDOWS = (2, 4, 8, 16)
POOL_HALO = 16
GROUP_WIDTH = 256
DILATIONS = (1, 4, 16)
ATTN_BLOCK = 128
ROT_SHIFT = 8
ROPE_THETA = 500000.0
D_FF = 4096
FF_BLOCK = 512
N_DEV = 8
N_IN = POOL_WIDTH + 3 * 768
PLE_DIM = 256
EPS = 1e-6
NEG_BIG = -1e30

ADAM_LR = 0.001
ADAM_B1 = 0.9
ADAM_B2 = 0.999
ADAM_EPS = 1e-08
ADAM_WD = 0.01
ADAM_STEP = 10

LANES = 128
VMEM_LIMIT = 56 * 1024 * 1024
MESH = pl.DeviceIdType.MESH


def _params(n_grid):
    return pltpu.CompilerParams(dimension_semantics=("arbitrary",) * n_grid, vmem_limit_bytes=VMEM_LIMIT)


def _dot(a, b):
    return jnp.dot(a, b, preferred_element_type=F32)


def _dot_nt(a, b):
    return lax.dot_general(a, b, (((1,), (1,)), ((), ())), preferred_element_type=F32)


def _dot_tn(a, b):
    return lax.dot_general(a, b, (((0,), (0,)), ((), ())), preferred_element_type=F32)


def _rms(x, g):
    rstd = lax.rsqrt(jnp.mean(x * x, axis=-1, keepdims=True) + EPS)
    n = x * rstd
    return n, rstd, n * g


def _rms_bwd(dy, n, rstd, g):
    dyn = dy * g
    dx = rstd * (dyn - n * jnp.mean(dyn * n, axis=-1, keepdims=True))
    return dx, jnp.sum(dy * n, axis=0, keepdims=True)


def _row_tile(s, t):
    t = min(s, t)
    assert s % t == 0
    return t


def _rot(z, c, sa, sb):
    return z * c + pltpu.roll(z, ROT_SHIFT, 1) * sa + pltpu.roll(z, LANES - ROT_SHIFT, 1) * sb


def _rot_t(dz, c, sa, sb):
    return dz * c + pltpu.roll(dz * sa, LANES - ROT_SHIFT, 1) + pltpu.roll(dz * sb, ROT_SHIFT, 1)


def _normproj_fwd(h, g, w_in, rc, rsa, rsb, name):
    s = h.shape[0]
    t = _row_tile(s, 512)

    def body(h_ref, g_ref, w_ref, c_ref, sa_ref, sb_ref, hn_ref, u_ref, q_ref, k_ref, v_ref):
        _, _, hn = _rms(h_ref[...], g_ref[...])
        hb = hn.astype(BF16)
        hn_ref[...] = hb
        c, sa, sb = c_ref[...], sa_ref[...], sb_ref[...]
        u_ref[...] = _dot(hb, w_ref[:, 0:POOL_WIDTH])
        for grp in range(3):
            lo = POOL_WIDTH + grp * GROUP_WIDTH
            zq = _dot(hb, w_ref[:, lo:lo + GROUP_WIDTH])
            zk = _dot(hb, w_ref[:, lo + 768:lo + 768 + GROUP_WIDTH])
            zv = _dot(hb, w_ref[:, lo + 1536:lo + 1536 + GROUP_WIDTH])
            for half in range(2):
                sl = slice(half * LANES, (half + 1) * LANES)
                q_ref[grp, :, sl] = (_rot(zq[:, sl], c, sa, sb) * (HEAD_DIM ** -0.5)).astype(BF16)
                k_ref[grp, :, sl] = _rot(zk[:, sl], c, sa, sb).astype(BF16)
            v_ref[grp] = zv.astype(BF16)

    row = lambda w: pl.BlockSpec((t, w), lambda i: (i, 0))
    grp_spec = pl.BlockSpec((3, t, GROUP_WIDTH), lambda i: (0, i, 0))
    return pl.pallas_call(
        body, name=name, grid=(s // t,),
        in_specs=[row(D_MODEL), pl.BlockSpec((1, D_MODEL), lambda i: (0, 0)),
                  pl.BlockSpec((D_MODEL, N_IN), lambda i: (0, 0)), row(LANES), row(LANES), row(LANES)],
        out_specs=[row(D_MODEL), row(POOL_WIDTH), grp_spec, grp_spec, grp_spec],
        out_shape=[jax.ShapeDtypeStruct((s, D_MODEL), BF16), jax.ShapeDtypeStruct((s, POOL_WIDTH), F32)]
        + [jax.ShapeDtypeStruct((3, s, GROUP_WIDTH), BF16)] * 3,
        compiler_params=_params(1),
    )(h, g, w_in, rc, rsa, rsb)


def _pool_lane_window():
    lane = lax.broadcasted_iota(jnp.int32, (1, POOL_WIDTH), 1)
    return jnp.left_shift(2, lane // (POOL_WIDTH // len(POOL_WINDOWS)))


def _pool_fwd(u, w_bd, scale, name):
    s = u.shape[0]
    t = _row_tile(s, 512)

    def body(u_ref, w_ref, sc_ref, out_ref, y_ref, ext):
        i = pl.program_id(0)

        @pl.when(i == 0)
        def _():
            ext[0:POOL_HALO, :] = jnp.zeros((POOL_HALO, POOL_WIDTH), F32)

        x = u_ref[...]
        ext[POOL_HALO:, :] = x
        win = _pool_lane_window()
        acc = x
        wsum = jnp.zeros_like(x)
        for k in range(1, POOL_HALO):
            acc = acc + ext[POOL_HALO - k:POOL_HALO - k + t, :]
            if k + 1 in POOL_WINDOWS:
                wsum = jnp.where(win == k + 1, acc, wsum)
        pos = i * t + lax.broadcasted_iota(jnp.int32, (t, POOL_WIDTH), 0)
        cnt = jnp.minimum(pos + 1, win).astype(F32)
        y = wsum / cnt - x
        yb = y.astype(BF16)
        y_ref[...] = yb
        out_ref[...] = _dot(yb, w_ref[...]) * sc_ref[...]
        ext[0:POOL_HALO, :] = x[t - POOL_HALO:, :]

    row = pl.BlockSpec((t, POOL_WIDTH), lambda i: (i, 0))
    return pl.pallas_call(
        body, name=name, grid=(s // t,),
        in_specs=[row, pl.BlockSpec((POOL_WIDTH, POOL_WIDTH), lambda i: (0, 0)),
                  pl.BlockSpec((1, POOL_WIDTH), lambda i: (0, 0))],
        out_specs=[row, row],
        out_shape=[jax.ShapeDtypeStruct((s, POOL_WIDTH), F32), jax.ShapeDtypeStruct((s, POOL_WIDTH), BF16)],
        scratch_shapes=[pltpu.VMEM((t + POOL_HALO, POOL_WIDTH), F32)],
        compiler_params=_params(1),
    )(u, w_bd, scale)


def _head_masks():
    lane = lax.broadcasted_iota(jnp.int32, (ATTN_BLOCK, GROUP_WIDTH), 1)
    return [lane // HEAD_DIM == hd for hd in range(GROUP_WIDTH // HEAD_DIM)]


def _stack_heads(a, masks):
    zero = jnp.zeros_like(a)
    return jnp.concatenate([jnp.where(m, a, zero) for m in masks], axis=0)


def _band_mask(first_block):
    rows = ATTN_BLOCK * (GROUP_WIDTH // HEAD_DIM)
    i = lax.broadcasted_iota(jnp.int32, (rows, 2 * ATTN_BLOCK), 0) & (ATTN_BLOCK - 1)
    j = lax.broadcasted_iota(jnp.int32, (rows, 2 * ATTN_BLOCK), 1)
    return (j >= i) & (j <= i + ATTN_BLOCK) & ((j >= ATTN_BLOCK) | jnp.logical_not(first_block))


def _column_per_head(a):
    return jnp.concatenate([a[:, hd * HEAD_DIM:hd * HEAD_DIM + 1] for hd in range(GROUP_WIDTH // HEAD_DIM)], axis=0)


def _attn_fwd(q3, k3, v3, grp, name):
    dil = DILATIONS[grp]
    s = q3.shape[1]
    length = s // dil
    nb = length // ATTN_BLOCK
    shape3 = (3, length, dil * GROUP_WIDTH)

    def body(q_ref, kp_ref, kc_ref, vp_ref, vc_ref, o_ref, lse_ref):
        b = pl.program_id(1)
        masks = _head_masks()
        qs = _stack_heads(q_ref[...], masks)
        kcat = jnp.concatenate([kp_ref[...], kc_ref[...]], axis=0)
        vcat = jnp.concatenate([vp_ref[...], vc_ref[...]], axis=0)
        sc = jnp.where(_band_mask(b == 0), _dot_nt(qs, kcat), NEG_BIG)
        m = jnp.max(sc, axis=1, keepdims=True)
        e = jnp.exp(sc - m)
        l = jnp.sum(e, axis=1, keepdims=True)
        p = (e / l).astype(BF16)
        lse = m + jnp.log(l)
        o = jnp.zeros((ATTN_BLOCK, GROUP_WIDTH), F32)
        lse_full = jnp.zeros((ATTN_BLOCK, GROUP_WIDTH), F32)
        for hd, msk in enumerate(masks):
            rows = slice(hd * ATTN_BLOCK, (hd + 1) * ATTN_BLOCK)
            o = jnp.where(msk, _dot(p[rows], vcat), o)
            lse_full = jnp.where(msk, lse[rows], lse_full)
        o_ref[...] = o
        lse_ref[...] = lse_full

    cur = pl.BlockSpec((None, ATTN_BLOCK, GROUP_WIDTH), lambda r, b: (grp, b, r))
    prev = pl.BlockSpec((None, ATTN_BLOCK, GROUP_WIDTH), lambda r, b: (grp, jnp.maximum(b - 1, 0), r))
    out = pl.BlockSpec((ATTN_BLOCK, GROUP_WIDTH), lambda r, b: (b, r))
    o, lse = pl.pallas_call(
        body, name=name, grid=(dil, nb),
        in_specs=[cur, prev, cur, prev, cur], out_specs=[out, out],
        out_shape=[jax.ShapeDtypeStruct((length, dil * GROUP_WIDTH), F32)] * 2,
        compiler_params=_params(2),
    )(q3.reshape(shape3), k3.reshape(shape3), k3.reshape(shape3), v3.reshape(shape3), v3.reshape(shape3))
    return o.reshape(s, GROUP_WIDTH), lse.reshape(s, GROUP_WIDTH)


def _group_weights(l0, l1, l2):
    m = jnp.maximum(jnp.maximum(l0, l1), l2)
    e0, e1, e2 = jnp.exp(l0 - m), jnp.exp(l1 - m), jnp.exp(l2 - m)
    den = e0 + e1 + e2
    return e0 / den, e1 / den, e2 / den


def _outproj_fwd(h, pool_out, o, lse, w_out, name):
    s = h.shape[0]
    t = _row_tile(s, 512)

    def body(h_ref, po_ref, o0, o1, o2, l0, l1, l2, w_ref, out_ref, a_ref):
        w0, w1, w2 = _group_weights(l0[...], l1[...], l2[...])
        a = jnp.concatenate([po_ref[...], o0[...] * w0, o1[...] * w1, o2[...] * w2], axis=1).astype(BF16)
        a_ref[...] = a
        out_ref[...] = h_ref[...] + _dot(a, w_ref[...])

    row = lambda w: pl.BlockSpec((t, w), lambda i: (i, 0))
    return pl.pallas_call(
        body, name=name, grid=(s // t,),
        in_specs=[row(D_MODEL)] + [row(GROUP_WIDTH)] * 7 + [pl.BlockSpec((D_MODEL, D_MODEL), lambda i: (0, 0))],
        out_specs=[row(D_MODEL), row(D_MODEL)],
        out_shape=[jax.ShapeDtypeStruct((s, D_MODEL), F32), jax.ShapeDtypeStruct((s, D_MODEL), BF16)],
        compiler_params=_params(1),
    )(h, pool_out, *o, *lse, w_out)


def _mlp_fwd(h, g, w_up, w_down, layer, name):
    s = h.shape[0]
    t = _row_tile(s, 1024)
    nblk = D_FF // FF_BLOCK

    def body(h_ref, g_ref, wu_ref, wd_ref, out_ref, hn_ref, r_ref, hb_s, acc):
        j = pl.program_id(1)

        @pl.when(j == 0)
        def _():
            _, _, hn = _rms(h_ref[...], g_ref[...])
            hb = hn.astype(BF16)
            hb_s[...] = hb
            hn_ref[...] = hb
            acc[...] = jnp.zeros_like(acc)

        r = jnp.maximum(_dot(hb_s[...], wu_ref[...]), 0.0)
        r_ref[...] = r.astype(BF16)
        acc[...] += _dot((r * r).astype(BF16), wd_ref[...])

        @pl.when(j == nblk - 1)
        def _():
            out_ref[...] = h_ref[...] + acc[...]

    row = pl.BlockSpec((t, D_MODEL), lambda i, j: (i, 0))
    return pl.pallas_call(
        body, name=name, grid=(s // t, nblk),
        in_specs=[row, pl.BlockSpec((1, D_MODEL), lambda i, j: (0, 0)),
                  pl.BlockSpec((None, None, D_MODEL, FF_BLOCK), lambda i, j: (j, layer, 0, 0)),
                  pl.BlockSpec((None, None, FF_BLOCK, D_MODEL), lambda i, j: (j, layer, 0, 0))],
        out_specs=[row, row, pl.BlockSpec((t, FF_BLOCK), lambda i, j: (i, j))],
        out_shape=[jax.ShapeDtypeStruct((s, D_MODEL), F32), jax.ShapeDtypeStruct((s, D_MODEL), BF16),
                   jax.ShapeDtypeStruct((s, D_FF), BF16)],
        scratch_shapes=[pltpu.VMEM((t, D_MODEL), BF16), pltpu.VMEM((t, D_MODEL), F32)],
        compiler_params=_params(2),
    )(h, g, w_up, w_down)


def _gate_fwd(h, g, w_gate, p, w_ple, name):
    s = h.shape[0]
    t = _row_tile(s, 512)

    def body(h_ref, g_ref, wg_ref, p_ref, wp_ref, out_ref, hn_ref, gate_ref, e_ref):
        x = h_ref[...]
        _, _, hn = _rms(x, g_ref[...])
        hb = hn.astype(BF16)
        hn_ref[...] = hb
        gate = 1.0 / (1.0 + jnp.exp(-_dot(hb, wg_ref[...])))
        e = _dot(p_ref[...].astype(BF16), wp_ref[...])
        gate_ref[...] = gate
        e_ref[...] = e
        out_ref[...] = x + gate * e

    row = lambda w: pl.BlockSpec((t, w), lambda i: (i, 0))
    full = lambda a, b: pl.BlockSpec((a, b), lambda i: (0, 0))
    return pl.pallas_call(
        body, name=name, grid=(s // t,),
        in_specs=[row(D_MODEL), full(1, D_MODEL), full(D_MODEL, D_MODEL), row(PLE_DIM), full(PLE_DIM, D_MODEL)],
        out_specs=[row(D_MODEL)] * 4,
        out_shape=[jax.ShapeDtypeStruct((s, D_MODEL), F32), jax.ShapeDtypeStruct((s, D_MODEL), BF16),
                   jax.ShapeDtypeStruct((s, D_MODEL), F32), jax.ShapeDtypeStruct((s, D_MODEL), F32)],
        compiler_params=_params(1),
    )(h, g, w_gate, p, w_ple)


def _loss_head(h, g, target, name):
    s = h.shape[0]
    t = _row_tile(s, 512)

    def body(h_ref, g_ref, t_ref, loss_ref, dh_ref, dg_ref):
        i = pl.program_id(0)

        @pl.when(i == 0)
        def _():
            loss_ref[...] = jnp.zeros_like(loss_ref)
            dg_ref[...] = jnp.zeros_like(dg_ref)

        gv = g_ref[...]
        n, rstd, y = _rms(h_ref[...], gv)
        err = y - t_ref[...]
        loss_ref[...] += jnp.sum(err * err) * (0.5 / D_MODEL)
        dx, dg = _rms_bwd(err * (1.0 / D_MODEL), n, rstd, gv)
        dh_ref[...] = dx
        dg_ref[...] += dg

    row = pl.BlockSpec((t, D_MODEL), lambda i: (i, 0))
    vec = pl.BlockSpec((1, D_MODEL), lambda i: (0, 0))
    return pl.pallas_call(
        body, name=name, grid=(s // t,),
        in_specs=[row, vec, row],
        out_specs=[pl.BlockSpec((1, LANES), lambda i: (0, 0)), row, vec],
        out_shape=[jax.ShapeDtypeStruct((1, LANES), F32), jax.ShapeDtypeStruct((s, D_MODEL), F32),
                   jax.ShapeDtypeStruct((1, D_MODEL), F32)],
        compiler_params=_params(1),
    )(h, g, target)


def _gate_bwd(dh, gate, e, h, g, w_gate, name):
    s = h.shape[0]
    t = _row_tile(s, 512)

    def body(dh_ref, gate_ref, e_ref, h_ref, g_ref, wg_ref, out_ref, dgl_ref, de_ref, dg_ref):
        @pl.when(pl.program_id(0) == 0)
        def _():
            dg_ref[...] = jnp.zeros_like(dg_ref)

        d = dh_ref[...]
        gate = gate_ref[...]
        dgl = (d * e_ref[...] * gate * (1.0 - gate)).astype(BF16)
        dgl_ref[...] = dgl
        de_ref[...] = (d * gate).astype(BF16)
        gv = g_ref[...]
        n, rstd, _ = _rms(h_ref[...], gv)
        dx, dg = _rms_bwd(_dot_nt(dgl, wg_ref[...]), n, rstd, gv)
        out_ref[...] = d + dx
        dg_ref[...] += dg

    row = pl.BlockSpec((t, D_MODEL), lambda i: (i, 0))
    vec = pl.BlockSpec((1, D_MODEL), lambda i: (0, 0))
    return pl.pallas_call(
        body, name=name, grid=(s // t,),
        in_specs=[row, row, row, row, vec, pl.BlockSpec((D_MODEL, D_MODEL), lambda i: (0, 0))],
        out_specs=[row, row, row, vec],
        out_shape=[jax.ShapeDtypeStruct((s, D_MODEL), F32), jax.ShapeDtypeStruct((s, D_MODEL), BF16),
                   jax.ShapeDtypeStruct((s, D_MODEL), BF16), jax.ShapeDtypeStruct((1, D_MODEL), F32)],
        compiler_params=_params(1),
    )(dh, gate, e, h, g, w_gate)


def _mlp_bwd(dh, r, h, g, w_up, w_down, layer, name):
    s = h.shape[0]
    t = _row_tile(s, 1024)
    nblk = D_FF // FF_BLOCK

    def body(dh_ref, r_ref, h_ref, g_ref, wu_ref, wd_ref, out_ref, dup_ref, dg_ref, db_s, acc):
        i, j = pl.program_id(0), pl.program_id(1)

        @pl.when((i == 0) & (j == 0))
        def _():
            dg_ref[...] = jnp.zeros_like(dg_ref)

        @pl.when(j == 0)
        def _():
            db_s[...] = dh_ref[...].astype(BF16)
            acc[...] = jnp.zeros_like(acc)

        dup = (_dot_nt(db_s[...], wd_ref[...]) * (2.0 * r_ref[...].astype(F32))).astype(BF16)
        dup_ref[...] = dup
        acc[...] += _dot_nt(dup, wu_ref[...])

        @pl.when(j == nblk - 1)
        def _():
            gv = g_ref[...]
            n, rstd, _ = _rms(h_ref[...], gv)
            dx, dg = _rms_bwd(acc[...], n, rstd, gv)
            out_ref[...] = dh_ref[...] + dx
            dg_ref[...] += dg

    row = pl.BlockSpec((t, D_MODEL), lambda i, j: (i, 0))
    vec = pl.BlockSpec((1, D_MODEL), lambda i, j: (0, 0))
    blk = pl.BlockSpec((t, FF_BLOCK), lambda i, j: (i, j))
    return pl.pallas_call(
        body, name=name, grid=(s // t, nblk),
        in_specs=[row, blk, row, vec,
                  pl.BlockSpec((None, None, D_MODEL, FF_BLOCK), lambda i, j: (j, layer, 0, 0)),
                  pl.BlockSpec((None, None, FF_BLOCK, D_MODEL), lambda i, j: (j, layer, 0, 0))],
        out_specs=[row, blk, vec],
        out_shape=[jax.ShapeDtypeStruct((s, D_MODEL), F32), jax.ShapeDtypeStruct((s, D_FF), BF16),
                   jax.ShapeDtypeStruct((1, D_MODEL), F32)],
        scratch_shapes=[pltpu.VMEM((t, D_MODEL), BF16), pltpu.VMEM((t, D_MODEL), F32)],
        compiler_params=_params(2),
    )(dh, r, h, g, w_up, w_down)


def _outproj_bwd(dh, w_out, o, lse, ones_bd, name):
    s = dh.shape[0]
    t = _row_tile(s, 512)

    def body(dh_ref, w_ref, o0, o1, o2, l0, l1, l2, bd_ref, dp_ref, do0, do1, do2, de0, de1, de2):
        da = _dot_nt(dh_ref[...].astype(BF16), w_ref[...])
        dp_ref[...] = da[:, 0:POOL_WIDTH]
        wts = _group_weights(l0[...], l1[...], l2[...])
        bd = bd_ref[...]
        cbar = jnp.zeros((t, GROUP_WIDTH), F32)
        for grp, (o_ref, do_ref) in enumerate(((o0, do0), (o1, do1), (o2, do2))):
            lo = POOL_WIDTH + grp * GROUP_WIDTH
            dag = da[:, lo:lo + GROUP_WIDTH]
            do_ref[...] = (dag * wts[grp]).astype(BF16)
            prod = dag * o_ref[...]
            hi = prod.astype(BF16)
            low = (prod - hi.astype(F32)).astype(BF16)
            cbar = cbar + wts[grp] * (_dot(hi, bd) + _dot(low, bd))
        for grp, de_ref in enumerate((de0, de1, de2)):
            de_ref[...] = wts[grp] * cbar

    row = lambda w: pl.BlockSpec((t, w), lambda i: (i, 0))
    full = lambda a, b: pl.BlockSpec((a, b), lambda i: (0, 0))
    grp_shape = lambda dt: jax.ShapeDtypeStruct((s, GROUP_WIDTH), dt)
    return pl.pallas_call(
        body, name=name, grid=(s // t,),
        in_specs=[row(D_MODEL), full(D_MODEL, D_MODEL)] + [row(GROUP_WIDTH)] * 6 + [full(GROUP_WIDTH, GROUP_WIDTH)],
        out_specs=[row(GROUP_WIDTH)] * 7,
        out_shape=[grp_shape(F32)] + [grp_shape(BF16)] * 3 + [grp_shape(F32)] * 3,
        compiler_params=_params(1),
    )(dh, w_out, *o, *lse, ones_bd)


def _attn_bwd(q3, k3, v3, do, lse, deff, grp, name):
    dil = DILATIONS[grp]
    s = q3.shape[1]
    length = s // dil
    nb = length // ATTN_BLOCK
    shape3 = (3, length, dil * GROUP_WIDTH)
    shape2 = (length, dil * GROUP_WIDTH)

    def body(q_ref, kp_ref, kc_ref, vp_ref, vc_ref, do_ref, lse_ref, de_ref, dq_ref, dk_ref, dv_ref, ck, cv):
        b = pl.program_id(1)

        @pl.when(b < nb)
        def _():
            masks = _head_masks()
            qs = _stack_heads(q_ref[...], masks)
            dos = _stack_heads(do_ref[...], masks)
            kcat = jnp.concatenate([kp_ref[...], kc_ref[...]], axis=0)
            vcat = jnp.concatenate([vp_ref[...], vc_ref[...]], axis=0)
            sc = jnp.where(_band_mask(b == 0), _dot_nt(qs, kcat), NEG_BIG)
            p = jnp.exp(sc - _column_per_head(lse_ref[...]))
            ds = (p * (_dot_nt(dos, vcat) - _column_per_head(de_ref[...]))).astype(BF16)
            dq = jnp.zeros((ATTN_BLOCK, GROUP_WIDTH), F32)
            for hd, msk in enumerate(masks):
                dq = jnp.where(msk, _dot(ds[hd * ATTN_BLOCK:(hd + 1) * ATTN_BLOCK], kcat), dq)
            dq_ref[...] = dq
            dkc = _dot_tn(ds, qs)
            dvc = _dot_tn(p.astype(BF16), dos)

            @pl.when(b > 0)
            def _():
                dk_ref[...] = ck[...] + dkc[0:ATTN_BLOCK]
                dv_ref[...] = cv[...] + dvc[0:ATTN_BLOCK]

            ck[...] = dkc[ATTN_BLOCK:]
            cv[...] = dvc[ATTN_BLOCK:]

        @pl.when(b == nb)
        def _():
            dk_ref[...] = ck[...]
            dv_ref[...] = cv[...]

    qb = lambda b: jnp.minimum(b, nb - 1)
    cur3 = pl.BlockSpec((None, ATTN_BLOCK, GROUP_WIDTH), lambda r, b: (grp, qb(b), r))
    prev3 = pl.BlockSpec((None, ATTN_BLOCK, GROUP_WIDTH), lambda r, b: (grp, jnp.maximum(qb(b) - 1, 0), r))
    cur2 = pl.BlockSpec((ATTN_BLOCK, GROUP_WIDTH), lambda r, b: (qb(b), r))
    late2 = pl.BlockSpec((ATTN_BLOCK, GROUP_WIDTH), lambda r, b: (jnp.maximum(b - 1, 0), r))
    dq, dk, dv = pl.pallas_call(
        body, name=name, grid=(dil, nb + 1),
        in_specs=[cur3, prev3, cur3, prev3, cur3, cur2, cur2, cur2],
        out_specs=[cur2, late2, late2],
        out_shape=[jax.ShapeDtypeStruct(shape2, F32)] * 3,
        scratch_shapes=[pltpu.VMEM((ATTN_BLOCK, GROUP_WIDTH), F32)] * 2,
        compiler_params=_params(2),
    )(q3.reshape(shape3), k3.reshape(shape3), k3.reshape(shape3), v3.reshape(shape3), v3.reshape(shape3),
      do.reshape(shape2), lse.reshape(shape2), deff.reshape(shape2))
    return [a.reshape(s, GROUP_WIDTH) for a in (dq, dk, dv)]


def _pool_bwd(dpool, y, w_bd, scale, name):
    s = dpool.shape[0]
    t = _row_tile(s, 512)
    nt = s // t

    def body(dp_ref, y_ref, w_ref, sc_ref, du_ref, dw_ref, dsc_ref, ext):
        i = pl.program_id(0)

        @pl.when(i == 0)
        def _():
            ext[t:, :] = jnp.zeros((POOL_HALO, POOL_WIDTH), F32)
            dw_ref[...] = jnp.zeros_like(dw_ref)
            dsc_ref[...] = jnp.zeros_like(dsc_ref)

        dp = dp_ref[...]
        yb = y_ref[...]
        w = w_ref[...]
        dsc_ref[...] += jnp.sum(dp * _dot(yb, w), axis=0, keepdims=True)
        dyo = (dp * sc_ref[...]).astype(BF16)
        dw_ref[...] += _dot_tn(yb, dyo)
        dy = _dot_nt(dyo, w)
        win = _pool_lane_window()
        pos = (nt - 1 - i) * t + lax.broadcasted_iota(jnp.int32, (t, POOL_WIDTH), 0)
        gq = dy / jnp.minimum(pos + 1, win).astype(F32)
        ext[0:t, :] = gq
        acc = gq
        wsum = jnp.zeros_like(gq)
        for k in range(1, POOL_HALO):
            acc = acc + ext[k:k + t, :]
            if k + 1 in POOL_WINDOWS:
                wsum = jnp.where(win == k + 1, acc, wsum)
        du_ref[...] = wsum - dy
        ext[t:, :] = gq[0:POOL_HALO, :]

    rev = pl.BlockSpec((t, POOL_WIDTH), lambda i: (nt - 1 - i, 0))
    full = lambda a, b: pl.BlockSpec((a, b), lambda i: (0, 0))
    return pl.pallas_call(
        body, name=name, grid=(nt,),
        in_specs=[rev, rev, full(POOL_WIDTH, POOL_WIDTH), full(1, POOL_WIDTH)],
        out_specs=[rev, full(POOL_WIDTH, POOL_WIDTH), full(1, POOL_WIDTH)],
        out_shape=[jax.ShapeDtypeStruct((s, POOL_WIDTH), F32), jax.ShapeDtypeStruct((POOL_WIDTH, POOL_WIDTH), F32),
                   jax.ShapeDtypeStruct((1, POOL_WIDTH), F32)],
        scratch_shapes=[pltpu.VMEM((t + POOL_HALO, POOL_WIDTH), F32)],
        compiler_params=_params(1),
    )(dpool, y, w_bd, scale)


def _normproj_bwd(dh, du, dq, dk, dv, rc, rsa, rsb, w_in, h, g, name):
    s = h.shape[0]
    t = _row_tile(s, 512)

    def body(dh_ref, du_ref, q0, q1, q2, k0, k1, k2, v0, v1, v2, c_ref, sa_ref, sb_ref, w_ref, h_ref, g_ref,
             out_ref, dz_ref, dg_ref):
        @pl.when(pl.program_id(0) == 0)
        def _():
            dg_ref[...] = jnp.zeros_like(dg_ref)

        c, sa, sb = c_ref[...], sa_ref[...], sb_ref[...]

        def unrot(a, scale):
            halves = [_rot_t(a[:, hf * LANES:(hf + 1) * LANES] * scale, c, sa, sb) for hf in range(2)]
            return jnp.concatenate(halves, axis=1)

        chunks = [du_ref[...]]
        chunks += [unrot(r[...], HEAD_DIM ** -0.5) for r in (q0, q1, q2)]
        chunks += [unrot(r[...], 1.0) for r in (k0, k1, k2)]
        chunks += [r[...] for r in (v0, v1, v2)]
        acc = jnp.zeros((t, D_MODEL), F32)
        for ci, ch in enumerate(chunks):
            cols = slice(ci * GROUP_WIDTH, (ci + 1) * GROUP_WIDTH)
            cb = ch.astype(BF16)
            dz_ref[:, cols] = cb
            acc = acc + _dot_nt(cb, w_ref[:, cols])
        gv = g_ref[...]
        n, rstd, _ = _rms(h_ref[...], gv)
        dx, dg = _rms_bwd(acc, n, rstd, gv)
        out_ref[...] = dh_ref[...] + dx
        dg_ref[...] += dg

    row = lambda w: pl.BlockSpec((t, w), lambda i: (i, 0))
    vec = pl.BlockSpec((1, D_MODEL), lambda i: (0, 0))
    return pl.pallas_call(
        body, name=name, grid=(s // t,),
        in_specs=[row(D_MODEL)] + [row(GROUP_WIDTH)] * 10 + [row(LANES)] * 3
        + [pl.BlockSpec((D_MODEL, N_IN), lambda i: (0, 0)), row(D_MODEL), vec],
        out_specs=[row(D_MODEL), row(N_IN), vec],
        out_shape=[jax.ShapeDtypeStruct((s, D_MODEL), F32), jax.ShapeDtypeStruct((s, N_IN), BF16),
                   jax.ShapeDtypeStruct((1, D_MODEL), F32)],
        compiler_params=_params(1),
    )(dh, du, *dq, *dk, *dv, rc, rsa, rsb, w_in, h, g)


def _matmul_tn(a, b, name, *, square_a=False, tn=None, blocked_out=False):
    s, m = a.shape
    n = b.shape[1]
    tk = _row_tile(s, 2048)
    tm = min(m, 512)
    tn = tn or min(n, 1024)
    assert m % tm == 0 and n % tn == 0
    nk = s // tk

    def body(a_ref, b_ref, o_ref, ob_ref, acc):
        k = pl.program_id(2)

        @pl.when(k == 0)
        def _():
            acc[...] = jnp.zeros_like(acc)

        av = a_ref[...]
        if square_a:
            av = av.astype(F32)
            av = av * av
        acc[...] += _dot_tn(av.astype(BF16), b_ref[...].astype(BF16))

        @pl.when(k == nk - 1)
        def _():
            o_ref[...] = acc[...]
            ob_ref[...] = acc[...].astype(BF16)

    if blocked_out:
        shape = (n // tn, m, tn)
        out_spec = pl.BlockSpec((None, tm, tn), lambda i, j, k: (j, i, 0))
    else:
        shape = (m, n)
        out_spec = pl.BlockSpec((tm, tn), lambda i, j, k: (i, j))
    return pl.pallas_call(
        body, name=name, grid=(m // tm, n // tn, nk),
        in_specs=[pl.BlockSpec((tk, tm), lambda i, j, k: (k, i)), pl.BlockSpec((tk, tn), lambda i, j, k: (k, j))],
        out_specs=[out_spec, out_spec],
        out_shape=[jax.ShapeDtypeStruct(shape, F32), jax.ShapeDtypeStruct(shape, BF16)],
        scratch_shapes=[pltpu.VMEM((tm, tn), F32)],
        compiler_params=_params(3),
    )(a, b)


def _adamw_math(w, g, m, v):
    m = ADAM_B1 * m + (1.0 - ADAM_B1) * g
    v = ADAM_B2 * v + (1.0 - ADAM_B2) * (g * g)
    m_hat = m / (1.0 - ADAM_B1 ** ADAM_STEP)
    v_hat = v / (1.0 - ADAM_B2 ** ADAM_STEP)
    delta = -ADAM_LR * (m_hat / (jnp.sqrt(v_hat) + ADAM_EPS) + ADAM_WD * w)
    return delta, m, v


def _adamw_sharded(w, m, v, own, recv0, recv1, name):
    _, rows, cols = w.shape
    t = _row_tile(rows, 256)

    def body(w_ref, m_ref, v_ref, own_ref, r0_ref, r1_ref, g_ref, d_ref, nm_ref, nv_ref):
        layer0 = pl.program_id(0) == 0
        g = own_ref[...]
        for k in range(N_DEV - 1):
            g = g + jnp.where(layer0, r0_ref[k], r1_ref[k]).astype(F32)
        g_ref[...] = g
        d_ref[...], nm_ref[...], nv_ref[...] = _adamw_math(w_ref[...], g, m_ref[...], v_ref[...])

    blk = pl.BlockSpec((None, t, cols), lambda l, i: (l, i, 0))
    recv = lambda layer: pl.BlockSpec((N_DEV - 1, t, cols), lambda l, i: (0, jnp.where(l == layer, i, 0), 0))
    return pl.pallas_call(
        body, name=name, grid=(2, rows // t),
        in_specs=[blk, blk, blk, blk, recv(0), recv(1)], out_specs=[blk] * 4,
        out_shape=[jax.ShapeDtypeStruct(w.shape, F32)] * 4,
        compiler_params=_params(2),
    )(w, m, v, own, recv0, recv1)


def _adamw_packed(w, g, m, v, name):
    def body(w_ref, g_ref, m_ref, v_ref, d_ref, nm_ref, nv_ref):
        d_ref[...], nm_ref[...], nv_ref[...] = _adamw_math(w_ref[...], g_ref[...], m_ref[...], v_ref[...])

    return pl.pallas_call(
        body, name=name, out_shape=[jax.ShapeDtypeStruct(w.shape, F32)] * 3,
        compiler_params=pltpu.CompilerParams(vmem_limit_bytes=VMEM_LIMIT),
    )(w, g, m, v)


def _peer(k):
    x, y, c = lax.axis_index("x"), lax.axis_index("y"), lax.axis_index("c")
    return (1 - x if k & 4 else x, 1 - y if k & 2 else y, 1 - c if k & 1 else c)


def _linear(dev):
    return 4 * dev[0] + 2 * dev[1] + dev[2]


def _allgather_weights(shards):
    n = len(shards)
    any_spec = pl.BlockSpec(memory_space=pl.ANY)

    def body(*refs):
        src, dst = refs[:n], refs[n:2 * n]
        send_sems, recv_sems, local_sems = refs[2 * n:]
        me = _peer(0)
        sibling = _peer(1)
        chips = [_peer(2), _peer(4), _peer(6)]

        def copy(w, slot, origin, to, from_input):
            return pltpu.make_async_remote_copy(
                src_ref=src[w] if from_input else dst[w].at[_linear(origin)], dst_ref=dst[w].at[_linear(origin)],
                send_sem=send_sems.at[7 * w + slot], recv_sem=recv_sems.at[7 * w + slot], device_id=to,
                device_id_type=MESH)

        local = [pltpu.make_async_copy(src[w], dst[w].at[_linear(me)], local_sems.at[w]) for w in range(n)]
        for cp in local:
            cp.start()
        first = []
        for w in range(n):
            first.append(copy(w, 0, me, sibling, True))
            first += [copy(w, 1 + j, me, chip, True) for j, chip in enumerate(chips)]
        for cp in first:
            cp.start()
        passed = []
        for j, chip in enumerate(chips):
            for w in range(n):
                copy(w, 1 + j, chip, me, False).wait_recv()
                fwd = copy(w, 4 + j, chip, sibling, False)
                fwd.start()
                passed.append(fwd)
        sib_chips = [(ch[0], ch[1], sibling[2]) for ch in chips]
        for w in range(n):
            copy(w, 0, sibling, me, False).wait_recv()
            for j, chip in enumerate(sib_chips):
                copy(w, 4 + j, chip, me, False).wait_recv()
        for cp in first + passed:
            cp.wait_send()
        for cp in local:
            cp.wait()

    return pl.pallas_call(
        body, name="allgather_weights",
        in_specs=[any_spec] * n, out_specs=[any_spec] * n,
        out_shape=[jax.ShapeDtypeStruct((N_DEV,) + a.shape, a.dtype) for a in shards],
        scratch_shapes=[pltpu.SemaphoreType.DMA((7 * n,)), pltpu.SemaphoreType.DMA((7 * n,)),
                        pltpu.SemaphoreType.DMA((n,))],
    )(*shards)


def _alltoall_grads(chunks, name):
    n = len(chunks)
    any_spec = pl.BlockSpec(memory_space=pl.ANY)

    def body(*refs):
        src, dst = refs[:n], refs[n:2 * n]
        send_sems, recv_sems = refs[2 * n:]
        copies = []
        for k in range(1, N_DEV):
            to = _peer(k)
            for w in range(n):
                copies.append(pltpu.make_async_remote_copy(
                    src_ref=src[w].at[_linear(to)], dst_ref=dst[w].at[k - 1],
                    send_sem=send_sems.at[7 * w + k - 1], recv_sem=recv_sems.at[7 * w + k - 1], device_id=to,
                    device_id_type=MESH))
        for cp in copies:
            cp.start()
        for cp in copies:
            cp.wait_recv()
        for cp in copies:
            cp.wait_send()

    return pl.pallas_call(
        body, name=name,
        in_specs=[any_spec] * n, out_specs=[any_spec] * n,
        out_shape=[jax.ShapeDtypeStruct((N_DEV - 1,) + a.shape[1:], a.dtype) for a in chunks],
        scratch_shapes=[pltpu.SemaphoreType.DMA((7 * n,)), pltpu.SemaphoreType.DMA((7 * n,))],
    )(*chunks)


def _allreduce_packed(g):
    rows = g.shape[0]

    def body(g_ref, out_ref, buf, send_sems, recv_sems):
        me = _linear(_peer(0))
        buf[me] = g_ref[...]
        copies = []
        for k in range(1, N_DEV):
            copies.append(pltpu.make_async_remote_copy(
                src_ref=g_ref, dst_ref=buf.at[me], send_sem=send_sems.at[k - 1], recv_sem=recv_sems.at[k - 1],
                device_id=_peer(k), device_id_type=MESH))
        for cp in copies:
            cp.start()
        for k in range(1, N_DEV):
            pltpu.make_async_remote_copy(
                src_ref=g_ref, dst_ref=buf.at[_linear(_peer(k))], send_sem=send_sems.at[k - 1],
                recv_sem=recv_sems.at[k - 1], device_id=_peer(k), device_id_type=MESH).wait_recv()
        for cp in copies:
            cp.wait_send()
        total = buf[0]
        for d in range(1, N_DEV):
            total = total + buf[d]
        out_ref[...] = total

    return pl.pallas_call(
        body, name="allreduce_small",
        in_specs=[pl.BlockSpec(memory_space=pltpu.VMEM)], out_specs=pl.BlockSpec(memory_space=pltpu.VMEM),
        out_shape=jax.ShapeDtypeStruct(g.shape, F32),
        scratch_shapes=[pltpu.VMEM((N_DEV, rows, g.shape[1]), F32), pltpu.SemaphoreType.DMA((7,)),
                        pltpu.SemaphoreType.DMA((7,))],
        compiler_params=pltpu.CompilerParams(vmem_limit_bytes=VMEM_LIMIT),
    )(g)


def _rotary_tables(positions):
    rot_dim = HEAD_DIM // 4
    inv_freq = ROPE_THETA ** (-jnp.arange(0, rot_dim, 2, dtype=F32) / rot_dim)
    ang = positions.astype(F32)[:, None] * inv_freq
    cos, sin = jnp.cos(ang), jnp.sin(ang)
    s = positions.shape[0]
    rest = HEAD_DIM - rot_dim
    zeros8 = jnp.zeros((s, ROT_SHIFT), F32)
    pad = lambda fill: jnp.full((s, rest), fill, F32)
    c = jnp.concatenate([cos, cos, pad(1.0)], axis=1)
    sa = jnp.concatenate([zeros8, sin, pad(0.0)], axis=1)
    sb = jnp.concatenate([-sin, zeros8, pad(0.0)], axis=1)
    return [jnp.concatenate([a, a], axis=1) for a in (c, sa, sb)]


def _block_diag(pool_w):
    gc = pool_w.shape[-1]
    out = jnp.zeros((POOL_WIDTH, POOL_WIDTH), pool_w.dtype)
    for grp in range(pool_w.shape[0]):
        out = lax.dynamic_update_slice(out, pool_w[grp], (grp * gc, grp * gc))
    return out


def _diag_blocks(a):
    gc = POOL_WIDTH // len(POOL_WINDOWS)
    return jnp.stack([a[grp * gc:(grp + 1) * gc, grp * gc:(grp + 1) * gc] for grp in range(len(POOL_WINDOWS))])


def _local_step(x, p, positions, loss_target, norm1, pool_w, pool_scale, norm2, norm3, final_norm,
                w_in, w_out, w_up, w_down, w_gate, w_ple):
    rc, rsa, rsb = _rotary_tables(positions)
    ones_bd = _block_diag(jnp.ones((4, HEAD_DIM, HEAD_DIM), BF16))
    saved = []
    h = x
    for i in range(2):
        tag = f"_l{i}"
        g1, g2, g3 = norm1[i:i + 1], norm2[i:i + 1], norm3[i:i + 1]
        w_bd = _block_diag(pool_w[i]).astype(BF16)
        scale = pool_scale[i:i + 1]
        hn1, u, q3, k3, v3 = _normproj_fwd(h, g1, w_in[i], rc, rsa, rsb, "normproj_fwd" + tag)
        pool_out, y = _pool_fwd(u, w_bd, scale, "pool_fwd" + tag)
        o, lse = zip(*[_attn_fwd(q3, k3, v3, grp, f"attn_fwd{tag}_g{grp}") for grp in range(3)])
        h1, a = _outproj_fwd(h, pool_out, o, lse, w_out[i], "outproj_fwd" + tag)
        h2, hn2, r = _mlp_fwd(h1, g2, w_up, w_down, i, "mlp_fwd" + tag)
        h3, hn3, gate, e = _gate_fwd(h2, g3, w_gate[i], p[i], w_ple[i], "gate_fwd" + tag)
        saved.append(dict(h0=h, hn1=hn1, q3=q3, k3=k3, v3=v3, y=y, o=o, lse=lse, a=a, h1=h1, hn2=hn2, r=r, h2=h2,
                          hn3=hn3, gate=gate, e=e, w_bd=w_bd, scale=scale, g1=g1, g2=g2, g3=g3))
        h = h3
    loss, dh, d_final = _loss_head(h, final_norm.reshape(1, D_MODEL), loss_target, "loss_head")

    grads = [None, None]
    for i in (1, 0):
        tag = f"_l{i}"
        sv = saved[i]
        dh2, dgl, de, dg3 = _gate_bwd(dh, sv["gate"], sv["e"], sv["h2"], sv["g3"], w_gate[i], "gate_bwd" + tag)
        dw_gate = _matmul_tn(sv["hn3"], dgl, "dw_gate" + tag)
        dw_ple = _matmul_tn(p[i], de, "dw_ple" + tag)
        dh1, dup, dg2 = _mlp_bwd(dh2, sv["r"], sv["h1"], sv["g2"], w_up, w_down, i, "mlp_bwd" + tag)
        dw_down = _matmul_tn(sv["r"], dh2, "dw_down" + tag, square_a=True)
        dw_up = _matmul_tn(sv["hn2"], dup, "dw_up" + tag, tn=FF_BLOCK, blocked_out=True)
        dpool, do0, do1, do2, de0, de1, de2 = _outproj_bwd(dh1, w_out[i], sv["o"], sv["lse"], ones_bd,
                                                           "outproj_bwd" + tag)
        dw_out = _matmul_tn(sv["a"], dh1, "dw_out" + tag)
        dqkv = [_attn_bwd(sv["q3"], sv["k3"], sv["v3"], do_g, sv["lse"][grp], de_g, grp, f"attn_bwd{tag}_g{grp}")
                for grp, (do_g, de_g) in enumerate(((do0, de0), (do1, de1), (do2, de2)))]
        dq, dk, dv = zip(*dqkv)
        du, dw_bd, dscale = _pool_bwd(dpool, sv["y"], sv["w_bd"], sv["scale"], "pool_bwd" + tag)
        dh, dz, dg1 = _normproj_bwd(dh1, du, dq, dk, dv, rc, rsa, rsb, w_in[i], sv["h0"], sv["g1"],
                                    "normproj_bwd" + tag)
        dw_in = _matmul_tn(sv["hn1"], dz, "dw_in" + tag, tn=512)
        grads[i] = dict(norm1=dg1, norm2=dg2, norm3=dg3, pool_w=_diag_blocks(dw_bd), pool_scale=dscale,
                        w_in=dw_in, w_out=dw_out, w_up=dw_up, w_down=dw_down, w_gate=dw_gate, w_ple=dw_ple)
    return loss, dh, grads, d_final


def _pack_small(norm1, norm2, norm3, final_norm, pool_scale, pool_w):
    scale_row = jnp.concatenate([pool_scale.reshape(1, 2 * POOL_WIDTH), jnp.zeros((1, D_MODEL - 2 * POOL_WIDTH), F32)], axis=1)
    return jnp.concatenate([norm1, norm2, norm3, final_norm.reshape(1, D_MODEL), scale_row,
                            pool_w.reshape(32, D_MODEL)], axis=0)


def _unpack_small(a):
    return dict(norm1=a[0:2], norm2=a[2:4], norm3=a[4:6], final_norm=a[6], pool_scale=a[7, 0:2 * POOL_WIDTH].reshape(2, POOL_WIDTH),
                pool_w=a[8:40].reshape(2, 4, HEAD_DIM, HEAD_DIM))


def _chunks_cols(a, cols):
    return a.reshape(a.shape[0], N_DEV, cols).transpose(1, 0, 2)


def _chunks_rows(a, rows):
    return a.reshape(N_DEV, rows, a.shape[1])


BIG = ("w_in", "w_out", "w_up", "w_down", "w_gate", "w_ple")
SMALL = ("norm1", "norm2", "norm3", "final_norm", "pool_scale", "pool_w")
ORDER = ("norm1", "w_in", "pool_w", "pool_scale", "w_out", "norm2", "w_up", "w_down", "norm3", "w_gate", "w_ple",
         "final_norm")


def kernel(x, p, positions, norm1, w_in, pool_w, pool_scale, w_out, norm2, w_up, w_down, norm3, w_gate, w_ple, final_norm, loss_target, m_norm1, m_w_in, m_pool_w, m_pool_scale, m_w_out, m_norm2, m_w_up, m_w_down, m_norm3, m_w_gate, m_w_ple, m_final_norm, v_norm1, v_w_in, v_pool_w, v_pool_scale, v_w_out, v_norm2, v_w_up, v_w_down, v_norm3, v_w_gate, v_w_ple, v_final_norm):
    w = dict(norm1=norm1, w_in=w_in, pool_w=pool_w, pool_scale=pool_scale, w_out=w_out, norm2=norm2, w_up=w_up,
             w_down=w_down, norm3=norm3, w_gate=w_gate, w_ple=w_ple, final_norm=final_norm)
    m = dict(norm1=m_norm1, w_in=m_w_in, pool_w=m_pool_w, pool_scale=m_pool_scale, w_out=m_w_out, norm2=m_norm2,
             w_up=m_w_up, w_down=m_w_down, norm3=m_norm3, w_gate=m_w_gate, w_ple=m_w_ple, final_norm=m_final_norm)
    v = dict(norm1=v_norm1, w_in=v_w_in, pool_w=v_pool_w, pool_scale=v_pool_scale, w_out=v_w_out, norm2=v_norm2,
             w_up=v_w_up, w_down=v_w_down, norm3=v_norm3, w_gate=v_w_gate, w_ple=v_w_ple, final_norm=v_final_norm)
    seq = x.shape[1]

    gathered = dict(zip(BIG, _allgather_weights([w[n].astype(BF16) for n in BIG])))
    full = dict(
        w_in=gathered["w_in"].transpose(1, 2, 0, 3).reshape(2, D_MODEL, N_IN),
        w_out=gathered["w_out"].transpose(1, 0, 2, 3).reshape(2, D_MODEL, D_MODEL),
        w_up=gathered["w_up"], w_down=gathered["w_down"],
        w_gate=gathered["w_gate"].transpose(1, 0, 2, 3).reshape(2, D_MODEL, D_MODEL),
        w_ple=gathered["w_ple"].transpose(1, 2, 0, 3).reshape(2, PLE_DIM, D_MODEL))

    loss, dx, grads, d_final = _local_step(
        x.reshape(seq, D_MODEL), p.reshape(2, seq, PLE_DIM), positions.reshape(seq), loss_target.reshape(seq, D_MODEL),
        norm1, pool_w, pool_scale, norm2, norm3, final_norm, **full)

    me = 4 * lax.axis_index("x") + 2 * lax.axis_index("y") + lax.axis_index("c")
    own, recv = {n: [] for n in BIG}, {n: [] for n in BIG}
    for i in (1, 0):
        gr = grads[i]
        f32c = dict(
            w_in=_chunks_cols(gr["w_in"][0], N_IN // N_DEV), w_out=_chunks_rows(gr["w_out"][0], D_MODEL // N_DEV),
            w_up=gr["w_up"][0], w_down=_chunks_rows(gr["w_down"][0], FF_BLOCK),
            w_gate=_chunks_rows(gr["w_gate"][0], D_MODEL // N_DEV), w_ple=_chunks_cols(gr["w_ple"][0], D_MODEL // N_DEV))
        bf16c = dict(
            w_in=_chunks_cols(gr["w_in"][1], N_IN // N_DEV), w_out=_chunks_rows(gr["w_out"][1], D_MODEL // N_DEV),
            w_up=gr["w_up"][1], w_down=_chunks_rows(gr["w_down"][1], FF_BLOCK),
            w_gate=_chunks_rows(gr["w_gate"][1], D_MODEL // N_DEV), w_ple=_chunks_cols(gr["w_ple"][1], D_MODEL // N_DEV))
        got = _alltoall_grads([bf16c[n] for n in BIG], f"alltoall_grads_l{i}")
        for n, r in zip(BIG, got):
            recv[n].insert(0, r)
            own[n].insert(0, lax.dynamic_index_in_dim(f32c[n], me, axis=0, keepdims=False))
    small_g = _pack_small(
        *[jnp.concatenate([grads[0][n], grads[1][n]], axis=0) for n in ("norm1", "norm2", "norm3")], d_final.reshape(D_MODEL),
        jnp.concatenate([grads[0]["pool_scale"], grads[1]["pool_scale"]], axis=0),
        jnp.stack([grads[0]["pool_w"], grads[1]["pool_w"]]))
    small_g = _allreduce_packed(small_g)

    g_out, d_out, m_out, v_out = {}, {}, {}, {}
    for n in BIG:
        g_out[n], d_out[n], m_out[n], v_out[n] = _adamw_sharded(
            w[n], m[n], v[n], jnp.stack(own[n]), recv[n][0], recv[n][1], "adamw_" + n)
    pack = lambda t: _pack_small(*[t[n] for n in SMALL])
    d_small, m_small, v_small = _adamw_packed(pack(w), small_g, pack(m), pack(v), "adamw_small")
    for dst, a in ((g_out, small_g), (d_out, d_small), (m_out, m_small), (v_out, v_small)):
        dst.update(_unpack_small(a))

    loss = lax.psum(loss[0, 0], ("x", "y", "c"))
    return (loss, dx.reshape(1, seq, D_MODEL), *[g_out[n] for n in ORDER], *[d_out[n] for n in ORDER],
            *[m_out[n] for n in ORDER], *[v_out[n] for n in ORDER])
```

```python
import functools

import jax
import jax.numpy as jnp
from jax import lax
from jax.experimental import pallas as pl
from jax.experimental.pallas import tpu as pltpu

F32 = jnp.float32
BF16 = jnp.bfloat16

D_MODEL = 1024
HEAD_DIM = 64
POOL_WIDTH = 256
POOL_WINDOWS = (2, 4, 8, 16)
POOL_HALO = 16
GROUP_WIDTH = 256
DILATIONS = (1, 4, 16)
ATTN_BLOCK = 128
ROT_SHIFT = 8
ROPE_THETA = 500000.0
D_FF = 4096
FF_BLOCK = 512
N_DEV = 8
N_IN = POOL_WIDTH + 3 * 768
PLE_DIM = 256
EPS = 1e-6
NEG_BIG = -1e30

ADAM_LR = 0.001
ADAM_B1 = 0.9
ADAM_B2 = 0.999
ADAM_EPS = 1e-08
ADAM_WD = 0.01
ADAM_STEP = 10

LANES = 128
VMEM_LIMIT = 56 * 1024 * 1024
MESH = pl.DeviceIdType.MESH


def _params(n_grid):
    return pltpu.CompilerParams(dimension_semantics=("arbitrary",) * n_grid, vmem_limit_bytes=VMEM_LIMIT)


def _dot(a, b):
    return jnp.dot(a, b, preferred_element_type=F32)


def _dot_nt(a, b):
    return lax.dot_general(a, b, (((1,), (1,)), ((), ())), preferred_element_type=F32)


def _dot_tn(a, b):
    return lax.dot_general(a, b, (((0,), (0,)), ((), ())), preferred_element_type=F32)


def _rms(x, g):
    rstd = lax.rsqrt(jnp.mean(x * x, axis=-1, keepdims=True) + EPS)
    n = x * rstd
    return n, rstd, n * g


def _rms_bwd(dy, n, rstd, g):
    dyn = dy * g
    dx = rstd * (dyn - n * jnp.mean(dyn * n, axis=-1, keepdims=True))
    return dx, jnp.sum(dy * n, axis=0, keepdims=True)


def _row_tile(s, t):
    t = min(s, t)
    assert s % t == 0
    return t


def _rot(z, c, sa, sb):
    return z * c + pltpu.roll(z, ROT_SHIFT, 1) * sa + pltpu.roll(z, LANES - ROT_SHIFT, 1) * sb


def _rot_t(dz, c, sa, sb):
    return dz * c + pltpu.roll(dz * sa, LANES - ROT_SHIFT, 1) + pltpu.roll(dz * sb, ROT_SHIFT, 1)


def _to_residues(value, stage, out_ref, dil):
    if dil == 1:
        out_ref[0] = value.astype(out_ref.dtype)
        return
    rows = value.shape[0] // dil
    for hf in range(GROUP_WIDTH // LANES):
        lanes = slice(hf * LANES, (hf + 1) * LANES)
        stage[hf][...] = value[:, lanes]
        for r in range(dil):
            out_ref[r, :, lanes] = stage[hf][pl.ds(r, rows, stride=dil), :].astype(out_ref.dtype)


def _from_residues(in_ref, stage, dil):
    if dil == 1:
        return in_ref[0]
    rows = in_ref.shape[1]
    for hf in range(GROUP_WIDTH // LANES):
        for r in range(dil):
            stage[hf][pl.ds(r, rows, stride=dil), :] = in_ref[r, :, hf * LANES:(hf + 1) * LANES]
    return jnp.concatenate([stage[0][...], stage[1][...]], axis=1)


def _residue_spec(dil, t):
    return pl.BlockSpec((dil, t // dil, GROUP_WIDTH), lambda i: (0, i, 0))


def _residue_shape(dil, s, dtype):
    return jax.ShapeDtypeStruct((dil, s // dil, GROUP_WIDTH), dtype)


def _stages(t, n):
    return [pltpu.VMEM((t, LANES), F32)] * (n * (GROUP_WIDTH // LANES))


def _pair_stages(refs):
    return [refs[i:i + 2] for i in range(0, len(refs), 2)]


def _normproj_fwd(h, g, w_in, rc, rsa, rsb, name):
    s = h.shape[0]
    t = _row_tile(s, 512)

    def body(h_ref, g_ref, w_ref, c_ref, sa_ref, sb_ref, hn_ref, u_ref, *rest):
        qkv_refs, stages = rest[:9], _pair_stages(rest[9:])
        _, _, hn = _rms(h_ref[...], g_ref[...])
        hb = hn.astype(BF16)
        hn_ref[...] = hb
        c, sa, sb = c_ref[...], sa_ref[...], sb_ref[...]

        def rot(z, scale):
            halves = [_rot(z[:, hf * LANES:(hf + 1) * LANES], c, sa, sb) * scale for hf in range(2)]
            return jnp.concatenate(halves, axis=1)

        u_ref[...] = _dot(hb, w_ref[:, 0:POOL_WIDTH])
        for grp, dil in enumerate(DILATIONS):
            lo = POOL_WIDTH + grp * GROUP_WIDTH
            q_ref, k_ref, v_ref = qkv_refs[3 * grp:3 * grp + 3]
            _to_residues(rot(_dot(hb, w_ref[:, lo:lo + GROUP_WIDTH]), HEAD_DIM ** -0.5), stages[0], q_ref, dil)
            _to_residues(rot(_dot(hb, w_ref[:, lo + 768:lo + 768 + GROUP_WIDTH]), 1.0), stages[1], k_ref, dil)
            _to_residues(_dot(hb, w_ref[:, lo + 1536:lo + 1536 + GROUP_WIDTH]), stages[2], v_ref, dil)

    row = lambda w: pl.BlockSpec((t, w), lambda i: (i, 0))
    return pl.pallas_call(
        body, name=name, grid=(s // t,),
        in_specs=[row(D_MODEL), pl.BlockSpec((1, D_MODEL), lambda i: (0, 0)),
                  pl.BlockSpec((D_MODEL, N_IN), lambda i: (0, 0)), row(LANES), row(LANES), row(LANES)],
        out_specs=[row(D_MODEL), row(POOL_WIDTH)] + [_residue_spec(dil, t) for dil in DILATIONS for _ in range(3)],
        out_shape=[jax.ShapeDtypeStruct((s, D_MODEL), BF16), jax.ShapeDtypeStruct((s, POOL_WIDTH), F32)]
        + [_residue_shape(dil, s, BF16) for dil in DILATIONS for _ in range(3)],
        scratch_shapes=_stages(t, 3),
        compiler_params=_params(1),
    )(h, g, w_in, rc, rsa, rsb)


def _pool_lane_window():
    lane = lax.broadcasted_iota(jnp.int32, (1, POOL_WIDTH), 1)
    return jnp.left_shift(2, lane // (POOL_WIDTH // len(POOL_WINDOWS)))


def _pool_fwd(u, w_bd, scale, name):
    s = u.shape[0]
    t = _row_tile(s, 512)

    def body(u_ref, w_ref, sc_ref, out_ref, y_ref, ext):
        i = pl.program_id(0)

        @pl.when(i == 0)
        def _():
            ext[0:POOL_HALO, :] = jnp.zeros((POOL_HALO, POOL_WIDTH), F32)

        x = u_ref[...]
        ext[POOL_HALO:, :] = x
        win = _pool_lane_window()
        acc = x
        wsum = jnp.zeros_like(x)
        for k in range(1, POOL_HALO):
            acc = acc + ext[POOL_HALO - k:POOL_HALO - k + t, :]
            if k + 1 in POOL_WINDOWS:
                wsum = jnp.where(win == k + 1, acc, wsum)
        pos = i * t + lax.broadcasted_iota(jnp.int32, (t, POOL_WIDTH), 0)
        cnt = jnp.minimum(pos + 1, win).astype(F32)
        y = wsum / cnt - x
        yb = y.astype(BF16)
        y_ref[...] = yb
        out_ref[...] = _dot(yb, w_ref[...]) * sc_ref[...]
        ext[0:POOL_HALO, :] = x[t - POOL_HALO:, :]

    row = pl.BlockSpec((t, POOL_WIDTH), lambda i: (i, 0))
    return pl.pallas_call(
        body, name=name, grid=(s // t,),
        in_specs=[row, pl.BlockSpec((POOL_WIDTH, POOL_WIDTH), lambda i: (0, 0)),
                  pl.BlockSpec((1, POOL_WIDTH), lambda i: (0, 0))],
        out_specs=[row, row],
        out_shape=[jax.ShapeDtypeStruct((s, POOL_WIDTH), F32), jax.ShapeDtypeStruct((s, POOL_WIDTH), BF16)],
        scratch_shapes=[pltpu.VMEM((t + POOL_HALO, POOL_WIDTH), F32)],
        compiler_params=_params(1),
    )(u, w_bd, scale)


def _head_masks():
    lane = lax.broadcasted_iota(jnp.int32, (ATTN_BLOCK, GROUP_WIDTH), 1)
    return [lane // HEAD_DIM == hd for hd in range(GROUP_WIDTH // HEAD_DIM)]


def _stack_heads(a, masks):
    zero = jnp.zeros_like(a)
    return jnp.concatenate([jnp.where(m, a, zero) for m in masks], axis=0)


def _band_mask(first_block):
    rows = ATTN_BLOCK * (GROUP_WIDTH // HEAD_DIM)
    i = lax.broadcasted_iota(jnp.int32, (rows, 2 * ATTN_BLOCK), 0) & (ATTN_BLOCK - 1)
    j = lax.broadcasted_iota(jnp.int32, (rows, 2 * ATTN_BLOCK), 1)
    return (j >= i) & (j <= i + ATTN_BLOCK) & ((j >= ATTN_BLOCK) | jnp.logical_not(first_block))


def _column_per_head(a):
    return jnp.concatenate([a[:, hd * HEAD_DIM:hd * HEAD_DIM + 1] for hd in range(GROUP_WIDTH // HEAD_DIM)], axis=0)


def _blocks_per_step(nb):
    return 2 if nb % 2 == 0 else 1


def _attn_fwd(q, k, v, name):
    dil, length, _ = q.shape
    nb = length // ATTN_BLOCK
    qb = _blocks_per_step(nb)

    def body(q_ref, kp_ref, kc_ref, vp_ref, vc_ref, o_ref, lse_ref):
        j = pl.program_id(1)
        masks = _head_masks()
        for qi in range(qb):
            here = slice(qi * ATTN_BLOCK, (qi + 1) * ATTN_BLOCK)
            before = slice((qi - 1) * ATTN_BLOCK, qi * ATTN_BLOCK)
            kcat = jnp.concatenate([kp_ref[...] if qi == 0 else kc_ref[before], kc_ref[here]], axis=0)
            vcat = jnp.concatenate([vp_ref[...] if qi == 0 else vc_ref[before], vc_ref[here]], axis=0)
            first = (j == 0) if qi == 0 else False
            qs = _stack_heads(q_ref[here], masks)
            sc = jnp.where(_band_mask(first), _dot_nt(qs, kcat), NEG_BIG)
            m = jnp.max(sc, axis=1, keepdims=True)
            e = jnp.exp(sc - m)
            l = jnp.sum(e, axis=1, keepdims=True)
            p = (e / l).astype(BF16)
            lse = m + jnp.log(l)
            o = jnp.zeros((ATTN_BLOCK, GROUP_WIDTH), F32)
            lse_full = jnp.zeros((ATTN_BLOCK, GROUP_WIDTH), F32)
            for hd, msk in enumerate(masks):
                rows = slice(hd * ATTN_BLOCK, (hd + 1) * ATTN_BLOCK)
                o = jnp.where(msk, _dot(p[rows], vcat), o)
                lse_full = jnp.where(msk, lse[rows], lse_full)
            o_ref[here] = o
            lse_ref[here] = lse_full

    cur = pl.BlockSpec((None, qb * ATTN_BLOCK, GROUP_WIDTH), lambda r, j: (r, j, 0))
    prev = pl.BlockSpec((None, ATTN_BLOCK, GROUP_WIDTH), lambda r, j: (r, jnp.maximum(qb * j - 1, 0), 0))
    return pl.pallas_call(
        body, name=name, grid=(dil, nb // qb),
        in_specs=[cur, prev, cur, prev, cur], out_specs=[cur, cur],
        out_shape=[jax.ShapeDtypeStruct(q.shape, F32)] * 2,
        compiler_params=_params(2),
    )(q, k, k, v, v)


def _group_weights(l0, l1, l2):
    m = jnp.maximum(jnp.maximum(l0, l1), l2)
    e0, e1, e2 = jnp.exp(l0 - m), jnp.exp(l1 - m), jnp.exp(l2 - m)
    den = e0 + e1 + e2
    return e0 / den, e1 / den, e2 / den


def _outproj_fwd(h, pool_out, o, lse, w_out, name):
    s = h.shape[0]
    t = _row_tile(s, 512)

    def body(h_ref, po_ref, o0, o1, o2, l0, l1, l2, w_ref, out_ref, a_ref, *stages):
        stages = _pair_stages(stages)
        ov =[_from_residues(r, stages[i], DILATIONS[i]) for i, r in enumerate((o0, o1, o2))]
        lv = [_from_residues(r, stages[3 + i], DILATIONS[i]) for i, r in enumerate((l0, l1, l2))]
        wts = _group_weights(*lv)
        a = jnp.concatenate([po_ref[...]] + [ov[i] * wts[i] for i in range(3)], axis=1).astype(BF16)
        a_ref[...] = a
        out_ref[...] = h_ref[...] + _dot(a, w_ref[...])

    row = lambda w: pl.BlockSpec((t, w), lambda i: (i, 0))
    res = [_residue_spec(dil, t) for dil in DILATIONS]
    return pl.pallas_call(
        body, name=name, grid=(s // t,),
        in_specs=[row(D_MODEL), row(POOL_WIDTH)] + res + res + [pl.BlockSpec((D_MODEL, D_MODEL), lambda i: (0, 0))],
        out_specs=[row(D_MODEL), row(D_MODEL)],
        out_shape=[jax.ShapeDtypeStruct((s, D_MODEL), F32), jax.ShapeDtypeStruct((s, D_MODEL), BF16)],
        scratch_shapes=_stages(t, 6),
        compiler_params=_params(1),
    )(h, pool_out, *o, *lse, w_out)


def _mlp_fwd(h, g, w_up, w_down, layer, name):
    s = h.shape[0]
    t = _row_tile(s, 1024)
    nblk = D_FF // FF_BLOCK

    def body(h_ref, g_ref, wu_ref, wd_ref, out_ref, hn_ref, r_ref, hb_s, acc):
        j = pl.program_id(1)

        @pl.when(j == 0)
        def _():
            _, _, hn = _rms(h_ref[...], g_ref[...])
            hb = hn.astype(BF16)
            hb_s[...] = hb
            hn_ref[...] = hb
            acc[...] = jnp.zeros_like(acc)

        r = jnp.maximum(_dot(hb_s[...], wu_ref[...]), 0.0)
        r_ref[...] = r.astype(BF16)
        acc[...] += _dot((r * r).astype(BF16), wd_ref[...])

        @pl.when(j == nblk - 1)
        def _():
            out_ref[...] = h_ref[...] + acc[...]

    row = pl.BlockSpec((t, D_MODEL), lambda i, j: (i, 0))
    return pl.pallas_call(
        body, name=name, grid=(s // t, nblk),
        in_specs=[row, pl.BlockSpec((1, D_MODEL), lambda i, j: (0, 0)),
                  pl.BlockSpec((None, None, D_MODEL, FF_BLOCK), lambda i, j: (j, layer, 0, 0)),
                  pl.BlockSpec((None, None, FF_BLOCK, D_MODEL), lambda i, j: (j, layer, 0, 0))],
        out_specs=[row, row, pl.BlockSpec((t, FF_BLOCK), lambda i, j: (i, j))],
        out_shape=[jax.ShapeDtypeStruct((s, D_MODEL), F32), jax.ShapeDtypeStruct((s, D_MODEL), BF16),
                   jax.ShapeDtypeStruct((s, D_FF), BF16)],
        scratch_shapes=[pltpu.VMEM((t, D_MODEL), BF16), pltpu.VMEM((t, D_MODEL), F32)],
        compiler_params=_params(2),
    )(h, g, w_up, w_down)


def _gate_fwd(h, g, w_gate, p, w_ple, name):
    s = h.shape[0]
    t = _row_tile(s, 512)

    def body(h_ref, g_ref, wg_ref, p_ref, wp_ref, out_ref, hn_ref, gate_ref, e_ref):
        x = h_ref[...]
        _, _, hn = _rms(x, g_ref[...])
        hb = hn.astype(BF16)
        hn_ref[...] = hb
        gate = 1.0 / (1.0 + jnp.exp(-_dot(hb, wg_ref[...])))
        e = _dot(p_ref[...].astype(BF16), wp_ref[...])
        gate_ref[...] = gate
        e_ref[...] = e
        out_ref[...] = x + gate * e

    row = lambda w: pl.BlockSpec((t, w), lambda i: (i, 0))
    full = lambda a, b: pl.BlockSpec((a, b), lambda i: (0, 0))
    return pl.pallas_call(
        body, name=name, grid=(s // t,),
        in_specs=[row(D_MODEL), full(1, D_MODEL), full(D_MODEL, D_MODEL), row(PLE_DIM), full(PLE_DIM, D_MODEL)],
        out_specs=[row(D_MODEL)] * 4,
        out_shape=[jax.ShapeDtypeStruct((s, D_MODEL), F32), jax.ShapeDtypeStruct((s, D_MODEL), BF16),
                   jax.ShapeDtypeStruct((s, D_MODEL), F32), jax.ShapeDtypeStruct((s, D_MODEL), F32)],
        compiler_params=_params(1),
    )(h, g, w_gate, p, w_ple)


def _loss_head(h, g, target, name):
    s = h.shape[0]
    t = _row_tile(s, 512)

    def body(h_ref, g_ref, t_ref, loss_ref, dh_ref, dg_ref):
        i = pl.program_id(0)

        @pl.when(i == 0)
        def _():
            loss_ref[...] = jnp.zeros_like(loss_ref)
            dg_ref[...] = jnp.zeros_like(dg_ref)

        gv = g_ref[...]
        n, rstd, y = _rms(h_ref[...], gv)
        err = y - t_ref[...]
        loss_ref[...] += jnp.sum(err * err) * (0.5 / D_MODEL)
        dx, dg = _rms_bwd(err * (1.0 / D_MODEL), n, rstd, gv)
        dh_ref[...] = dx
        dg_ref[...] += dg

    row = pl.BlockSpec((t, D_MODEL), lambda i: (i, 0))
    vec = pl.BlockSpec((1, D_MODEL), lambda i: (0, 0))
    return pl.pallas_call(
        body, name=name, grid=(s // t,),
        in_specs=[row, vec, row],
        out_specs=[pl.BlockSpec((1, LANES), lambda i: (0, 0)), row, vec],
        out_shape=[jax.ShapeDtypeStruct((1, LANES), F32), jax.ShapeDtypeStruct((s, D_MODEL), F32),
                   jax.ShapeDtypeStruct((1, D_MODEL), F32)],
        compiler_params=_params(1),
    )(h, g, target)


def _gate_bwd(dh, gate, e, h, g, w_gate, name):
    s = h.shape[0]
    t = _row_tile(s, 512)

    def body(dh_ref, gate_ref, e_ref, h_ref, g_ref, wg_ref, out_ref, dgl_ref, de_ref, dg_ref):
        @pl.when(pl.program_id(0) == 0)
        def _():
            dg_ref[...] = jnp.zeros_like(dg_ref)

        d = dh_ref[...]
        gate = gate_ref[...]
        dgl = (d * e_ref[...] * gate * (1.0 - gate)).astype(BF16)
        dgl_ref[...] = dgl
        de_ref[...] = (d * gate).astype(BF16)
        gv = g_ref[...]
        n, rstd, _ = _rms(h_ref[...], gv)
        dx, dg = _rms_bwd(_dot_nt(dgl, wg_ref[...]), n, rstd, gv)
        out_ref[...] = d + dx
        dg_ref[...] += dg

    row = pl.BlockSpec((t, D_MODEL), lambda i: (i, 0))
    vec = pl.BlockSpec((1, D_MODEL), lambda i: (0, 0))
    return pl.pallas_call(
        body, name=name, grid=(s // t,),
        in_specs=[row, row, row, row, vec, pl.BlockSpec((D_MODEL, D_MODEL), lambda i: (0, 0))],
        out_specs=[row, row, row, vec],
        out_shape=[jax.ShapeDtypeStruct((s, D_MODEL), F32), jax.ShapeDtypeStruct((s, D_MODEL), BF16),
                   jax.ShapeDtypeStruct((s, D_MODEL), BF16), jax.ShapeDtypeStruct((1, D_MODEL), F32)],
        compiler_params=_params(1),
    )(dh, gate, e, h, g, w_gate)


def _mlp_bwd(dh, r, h, g, w_up, w_down, layer, name):
    s = h.shape[0]
    t = _row_tile(s, 1024)
    nblk = D_FF // FF_BLOCK

    def body(dh_ref, r_ref, h_ref, g_ref, wu_ref, wd_ref, out_ref, dup_ref, dg_ref, db_s, acc):
        i, j = pl.program_id(0), pl.program_id(1)

        @pl.when((i == 0) & (j == 0))
        def _():
            dg_ref[...] = jnp.zeros_like(dg_ref)

        @pl.when(j == 0)
        def _():
            db_s[...] = dh_ref[...].astype(BF16)
            acc[...] = jnp.zeros_like(acc)

        dup = (_dot_nt(db_s[...], wd_ref[...]) * (2.0 * r_ref[...].astype(F32))).astype(BF16)
        dup_ref[...] = dup
        acc[...] += _dot_nt(dup, wu_ref[...])

        @pl.when(j == nblk - 1)
        def _():
            gv = g_ref[...]
            n, rstd, _ = _rms(h_ref[...], gv)
            dx, dg = _rms_bwd(acc[...], n, rstd, gv)
            out_ref[...] = dh_ref[...] + dx
            dg_ref[...] += dg

    row = pl.BlockSpec((t, D_MODEL), lambda i, j: (i, 0))
    vec = pl.BlockSpec((1, D_MODEL), lambda i, j: (0, 0))
    blk = pl.BlockSpec((t, FF_BLOCK), lambda i, j: (i, j))
    return pl.pallas_call(
        body, name=name, grid=(s // t, nblk),
        in_specs=[row, blk, row, vec,
                  pl.BlockSpec((None, None, D_MODEL, FF_BLOCK), lambda i, j: (j, layer, 0, 0)),
                  pl.BlockSpec((None, None, FF_BLOCK, D_MODEL), lambda i, j: (j, layer, 0, 0))],
        out_specs=[row, blk, vec],
        out_shape=[jax.ShapeDtypeStruct((s, D_MODEL), F32), jax.ShapeDtypeStruct((s, D_FF), BF16),
                   jax.ShapeDtypeStruct((1, D_MODEL), F32)],
        scratch_shapes=[pltpu.VMEM((t, D_MODEL), BF16), pltpu.VMEM((t, D_MODEL), F32)],
        compiler_params=_params(2),
    )(dh, r, h, g, w_up, w_down)


def _outproj_bwd(dh, w_out, o, lse, ones_bd, name):
    s = dh.shape[0]
    t = _row_tile(s, 512)

    def body(dh_ref, w_ref, o0, o1, o2, l0, l1, l2, bd_ref, dp_ref, do0, do1, do2, de0, de1, de2, *stages):
        stages = _pair_stages(stages)
        da = _dot_nt(dh_ref[...].astype(BF16), w_ref[...])
        dp_ref[...] = da[:, 0:POOL_WIDTH]
        ov =[_from_residues(r, stages[i], DILATIONS[i]) for i, r in enumerate((o0, o1, o2))]
        lv = [_from_residues(r, stages[3 + i], DILATIONS[i]) for i, r in enumerate((l0, l1, l2))]
        wts = _group_weights(*lv)
        bd = bd_ref[...]
        cbar = jnp.zeros((t, GROUP_WIDTH), F32)
        for grp, do_ref in enumerate((do0, do1, do2)):
            lo = POOL_WIDTH + grp * GROUP_WIDTH
            dag = da[:, lo:lo + GROUP_WIDTH]
            _to_residues(dag * wts[grp], stages[6 + grp], do_ref, DILATIONS[grp])
            prod = dag * ov[grp]
            hi = prod.astype(BF16)
            low = (prod - hi.astype(F32)).astype(BF16)
            cbar = cbar + wts[grp] * (_dot(hi, bd) + _dot(low, bd))
        for grp, de_ref in enumerate((de0, de1, de2)):
            _to_residues(wts[grp] * cbar, stages[9 + grp], de_ref, DILATIONS[grp])

    row = lambda w: pl.BlockSpec((t, w), lambda i: (i, 0))
    full = lambda a, b: pl.BlockSpec((a, b), lambda i: (0, 0))
    res = [_residue_spec(dil, t) for dil in DILATIONS]
    return pl.pallas_call(
        body, name=name, grid=(s // t,),
        in_specs=[row(D_MODEL), full(D_MODEL, D_MODEL)] + res + res + [full(GROUP_WIDTH, GROUP_WIDTH)],
        out_specs=[row(POOL_WIDTH)] + res + res,
        out_shape=[jax.ShapeDtypeStruct((s, POOL_WIDTH), F32)] + [_residue_shape(dil, s, BF16) for dil in DILATIONS]
        + [_residue_shape(dil, s, F32) for dil in DILATIONS],
        scratch_shapes=_stages(t, 12),
        compiler_params=_params(1),
    )(dh, w_out, *o, *lse, ones_bd)


def _attn_bwd(q, k, v, do, lse, deff, name):
    dil, length, _ = q.shape
    nb = length // ATTN_BLOCK
    qb = _blocks_per_step(nb)
    nj = nb // qb
    tail = slice((qb - 1) * ATTN_BLOCK, qb * ATTN_BLOCK)

    def body(q_ref, kp_ref, kc_ref, vp_ref, vc_ref, do_ref, lse_ref, de_ref, dq_ref, dk_ref, dv_ref, ck, cv):
        j = pl.program_id(1)

        @pl.when(j < nj)
        def _():
            masks = _head_masks()
            dkc, dvc = [], []
            for qi in range(qb):
                here = slice(qi * ATTN_BLOCK, (qi + 1) * ATTN_BLOCK)
                before = slice((qi - 1) * ATTN_BLOCK, qi * ATTN_BLOCK)
                kcat = jnp.concatenate([kp_ref[...] if qi == 0 else kc_ref[before], kc_ref[here]], axis=0)
                vcat = jnp.concatenate([vp_ref[...] if qi == 0 else vc_ref[before], vc_ref[here]], axis=0)
                first = (j == 0) if qi == 0 else False
                qs = _stack_heads(q_ref[here], masks)
                dos = _stack_heads(do_ref[here], masks)
                sc = jnp.where(_band_mask(first), _dot_nt(qs, kcat), NEG_BIG)
                p = jnp.exp(sc - _column_per_head(lse_ref[here]))
                ds = (p * (_dot_nt(dos, vcat) - _column_per_head(de_ref[here]))).astype(BF16)
                dq = jnp.zeros((ATTN_BLOCK, GROUP_WIDTH), F32)
                for hd, msk in enumerate(masks):
                    dq = jnp.where(msk, _dot(ds[hd * ATTN_BLOCK:(hd + 1) * ATTN_BLOCK], kcat), dq)
                dq_ref[here] = dq
                dkc.append(_dot_tn(ds, qs))
                dvc.append(_dot_tn(p.astype(BF16), dos))

            for out_ref, carry, parts in ((dk_ref, ck, dkc), (dv_ref, cv, dvc)):
                @pl.when(j > 0)
                def _():
                    if qb > 1:
                        out_ref[0:(qb - 1) * ATTN_BLOCK] = carry[0:(qb - 1) * ATTN_BLOCK]
                    out_ref[tail] = carry[tail] + parts[0][0:ATTN_BLOCK]

                for qi in range(qb - 1):
                    carry[qi * ATTN_BLOCK:(qi + 1) * ATTN_BLOCK] = parts[qi][ATTN_BLOCK:] + parts[qi + 1][0:ATTN_BLOCK]
                carry[tail] = parts[qb - 1][ATTN_BLOCK:]

        @pl.when(j == nj)
        def _():
            dk_ref[...] = ck[...]
            dv_ref[...] = cv[...]

    step = lambda j: jnp.minimum(j, nj - 1)
    cur = pl.BlockSpec((None, qb * ATTN_BLOCK, GROUP_WIDTH), lambda r, j: (r, step(j), 0))
    prev = pl.BlockSpec((None, ATTN_BLOCK, GROUP_WIDTH), lambda r, j: (r, jnp.maximum(qb * step(j) - 1, 0), 0))
    late = pl.BlockSpec((None, qb * ATTN_BLOCK, GROUP_WIDTH), lambda r, j: (r, jnp.maximum(j - 1, 0), 0))
    return pl.pallas_call(
        body, name=name, grid=(dil, nj + 1),
        in_specs=[cur, prev, cur, prev, cur, cur, cur, cur],
        out_specs=[cur, late, late],
        out_shape=[jax.ShapeDtypeStruct(q.shape, F32)] * 3,
        scratch_shapes=[pltpu.VMEM((qb * ATTN_BLOCK, GROUP_WIDTH), F32)] * 2,
        compiler_params=_params(2),
    )(q, k, k, v, v, do, lse, deff)


def _pool_bwd(dpool, y, w_bd, scale, name):
    s = dpool.shape[0]
    t = _row_tile(s, 512)
    nt = s // t

    def body(dp_ref, y_ref, w_ref, sc_ref, du_ref, dw_ref, dsc_ref, ext):
        i = pl.program_id(0)

        @pl.when(i == 0)
        def _():
            ext[t:, :] = jnp.zeros((POOL_HALO, POOL_WIDTH), F32)
            dw_ref[...] = jnp.zeros_like(dw_ref)
            dsc_ref[...] = jnp.zeros_like(dsc_ref)

        dp = dp_ref[...]
        yb = y_ref[...]
        w = w_ref[...]
        dsc_ref[...] += jnp.sum(dp * _dot(yb, w), axis=0, keepdims=True)
        dyo = (dp * sc_ref[...]).astype(BF16)
        dw_ref[...] += _dot_tn(yb, dyo)
        dy = _dot_nt(dyo, w)
        win = _pool_lane_window()
        pos = (nt - 1 - i) * t + lax.broadcasted_iota(jnp.int32, (t, POOL_WIDTH), 0)
        gq = dy / jnp.minimum(pos + 1, win).astype(F32)
        ext[0:t, :] = gq
        acc = gq
        wsum = jnp.zeros_like(gq)
        for k in range(1, POOL_HALO):
            acc = acc + ext[k:k + t, :]
            if k + 1 in POOL_WINDOWS:
                wsum = jnp.where(win == k + 1, acc, wsum)
        du_ref[...] = wsum - dy
        ext[t:, :] = gq[0:POOL_HALO, :]

    rev = pl.BlockSpec((t, POOL_WIDTH), lambda i: (nt - 1 - i, 0))
    full = lambda a, b: pl.BlockSpec((a, b), lambda i: (0, 0))
    return pl.pallas_call(
        body, name=name, grid=(nt,),
        in_specs=[rev, rev, full(POOL_WIDTH, POOL_WIDTH), full(1, POOL_WIDTH)],
        out_specs=[rev, full(POOL_WIDTH, POOL_WIDTH), full(1, POOL_WIDTH)],
        out_shape=[jax.ShapeDtypeStruct((s, POOL_WIDTH), F32), jax.ShapeDtypeStruct((POOL_WIDTH, POOL_WIDTH), F32),
                   jax.ShapeDtypeStruct((1, POOL_WIDTH), F32)],
        scratch_shapes=[pltpu.VMEM((t + POOL_HALO, POOL_WIDTH), F32)],
        compiler_params=_params(1),
    )(dpool, y, w_bd, scale)


def _normproj_bwd(dh, du, dq, dk, dv, rc, rsa, rsb, w_in, h, g, name):
    s = h.shape[0]
    t = _row_tile(s, 512)

    def body(dh_ref, du_ref, q0, q1, q2, k0, k1, k2, v0, v1, v2, c_ref, sa_ref, sb_ref, w_ref, h_ref, g_ref,
             out_ref, dz_ref, dg_ref, *stages):
        @pl.when(pl.program_id(0) == 0)
        def _():
            dg_ref[...] = jnp.zeros_like(dg_ref)

        c, sa, sb = c_ref[...], sa_ref[...], sb_ref[...]

        def unrot(a, scale):
            halves = [_rot_t(a[:, hf * LANES:(hf + 1) * LANES] * scale, c, sa, sb) for hf in range(2)]
            return jnp.concatenate(halves, axis=1)

        staged = _pair_stages(stages)
        tok = lambda refs, base: [_from_residues(r, staged[base + i], DILATIONS[i]) for i, r in enumerate(refs)]
        chunks = [du_ref[...]]
        chunks += [unrot(a, HEAD_DIM ** -0.5) for a in tok((q0, q1, q2), 0)]
        chunks += [unrot(a, 1.0) for a in tok((k0, k1, k2), 3)]
        chunks += tok((v0, v1, v2), 6)
        acc = jnp.zeros((t, D_MODEL), F32)
        for ci, ch in enumerate(chunks):
            cols = slice(ci * GROUP_WIDTH, (ci + 1) * GROUP_WIDTH)
            cb = ch.astype(BF16)
            dz_ref[:, cols] = cb
            acc = acc + _dot_nt(cb, w_ref[:, cols])
        gv = g_ref[...]
        n, rstd, _ = _rms(h_ref[...], gv)
        dx, dg = _rms_bwd(acc, n, rstd, gv)
        out_ref[...] = dh_ref[...] + dx
        dg_ref[...] += dg

    row = lambda w: pl.BlockSpec((t, w), lambda i: (i, 0))
    vec = pl.BlockSpec((1, D_MODEL), lambda i: (0, 0))
    res = [_residue_spec(dil, t) for dil in DILATIONS]
    return pl.pallas_call(
        body, name=name, grid=(s // t,),
        in_specs=[row(D_MODEL), row(POOL_WIDTH)] + res * 3 + [row(LANES)] * 3
        + [pl.BlockSpec((D_MODEL, N_IN), lambda i: (0, 0)), row(D_MODEL), vec],
        out_specs=[row(D_MODEL), row(N_IN), vec],
        out_shape=[jax.ShapeDtypeStruct((s, D_MODEL), F32), jax.ShapeDtypeStruct((s, N_IN), BF16),
                   jax.ShapeDtypeStruct((1, D_MODEL), F32)],
        scratch_shapes=_stages(t, 9),
        compiler_params=_params(1),
    )(dh, du, *dq, *dk, *dv, rc, rsa, rsb, w_in, h, g)


def _matmul_tn(a, b, name, *, square_a=False, tn=None, blocked_out=False):
    s, m = a.shape
    n = b.shape[1]
    tk = _row_tile(s, 2048)
    tm = min(m, 512)
    tn = tn or min(n, 1024)
    assert m % tm == 0 and n % tn == 0
    nk = s // tk

    def body(a_ref, b_ref, o_ref, ob_ref, acc):
        k = pl.program_id(2)

        @pl.when(k == 0)
        def _():
            acc[...] = jnp.zeros_like(acc)

        av = a_ref[...]
        if square_a:
            av = av.astype(F32)
            av = av * av
        acc[...] += _dot_tn(av.astype(BF16), b_ref[...].astype(BF16))

        @pl.when(k == nk - 1)
        def _():
            o_ref[...] = acc[...]
            ob_ref[...] = acc[...].astype(BF16)

    if blocked_out:
        shape = (n // tn, m, tn)
        out_spec = pl.BlockSpec((None, tm, tn), lambda i, j, k: (j, i, 0))
    else:
        shape = (m, n)
        out_spec = pl.BlockSpec((tm, tn), lambda i, j, k: (i, j))
    return pl.pallas_call(
        body, name=name, grid=(m // tm, n // tn, nk),
        in_specs=[pl.BlockSpec((tk, tm), lambda i, j, k: (k, i)), pl.BlockSpec((tk, tn), lambda i, j, k: (k, j))],
        out_specs=[out_spec, out_spec],
        out_shape=[jax.ShapeDtypeStruct(shape, F32), jax.ShapeDtypeStruct(shape, BF16)],
        scratch_shapes=[pltpu.VMEM((tm, tn), F32)],
        compiler_params=_params(3),
    )(a, b)


def _adamw_math(w, g, m, v):
    m = ADAM_B1 * m + (1.0 - ADAM_B1) * g
    v = ADAM_B2 * v + (1.0 - ADAM_B2) * (g * g)
    m_hat = m / (1.0 - ADAM_B1 ** ADAM_STEP)
    v_hat = v / (1.0 - ADAM_B2 ** ADAM_STEP)
    delta = -ADAM_LR * (m_hat / (jnp.sqrt(v_hat) + ADAM_EPS) + ADAM_WD * w)
    return delta, m, v


def _adamw_sharded(w, m, v, own, recv0, recv1, name):
    _, rows, cols = w.shape
    t = _row_tile(rows, 256)

    def body(w_ref, m_ref, v_ref, own_ref, r0_ref, r1_ref, g_ref, d_ref, nm_ref, nv_ref):
        layer0 = pl.program_id(0) == 0
        g = own_ref[...]
        for k in range(N_DEV - 1):
            g = g + jnp.where(layer0, r0_ref[k], r1_ref[k]).astype(F32)
        g_ref[...] = g
        d_ref[...], nm_ref[...], nv_ref[...] = _adamw_math(w_ref[...], g, m_ref[...], v_ref[...])

    blk = pl.BlockSpec((None, t, cols), lambda l, i: (l, i, 0))
    recv = lambda layer: pl.BlockSpec((N_DEV - 1, t, cols), lambda l, i: (0, jnp.where(l == layer, i, 0), 0))
    return pl.pallas_call(
        body, name=name, grid=(2, rows // t),
        in_specs=[blk, blk, blk, blk, recv(0), recv(1)], out_specs=[blk] * 4,
        out_shape=[jax.ShapeDtypeStruct(w.shape, F32)] * 4,
        compiler_params=_params(2),
    )(w, m, v, own, recv0, recv1)


def _adamw_packed(w, g, m, v, name):
    def body(w_ref, g_ref, m_ref, v_ref, d_ref, nm_ref, nv_ref):
        d_ref[...], nm_ref[...], nv_ref[...] = _adamw_math(w_ref[...], g_ref[...], m_ref[...], v_ref[...])

    return pl.pallas_call(
        body, name=name, out_shape=[jax.ShapeDtypeStruct(w.shape, F32)] * 3,
        compiler_params=pltpu.CompilerParams(vmem_limit_bytes=VMEM_LIMIT),
    )(w, g, m, v)


def _peer(k):
    x, y, c = lax.axis_index("x"), lax.axis_index("y"), lax.axis_index("c")
    return (1 - x if k & 4 else x, 1 - y if k & 2 else y, 1 - c if k & 1 else c)


def _linear(dev):
    return 4 * dev[0] + 2 * dev[1] + dev[2]


def _allgather_weights(shards):
    n = len(shards)
    any_spec = pl.BlockSpec(memory_space=pl.ANY)

    def body(*refs):
        src, dst = refs[:n], refs[n:2 * n]
        send_sems, recv_sems, local_sems = refs[2 * n:]
        me = _peer(0)
        sibling = _peer(1)
        chips = [_peer(2), _peer(4), _peer(6)]

        def copy(w, slot, origin, to, from_input):
            return pltpu.make_async_remote_copy(
                src_ref=src[w] if from_input else dst[w].at[_linear(origin)], dst_ref=dst[w].at[_linear(origin)],
                send_sem=send_sems.at[7 * w + slot], recv_sem=recv_sems.at[7 * w + slot], device_id=to,
                device_id_type=MESH)

        local = [pltpu.make_async_copy(src[w], dst[w].at[_linear(me)], local_sems.at[w]) for w in range(n)]
        for cp in local:
            cp.start()
        first = []
        for w in range(n):
            first.append(copy(w, 0, me, sibling, True))
            first += [copy(w, 1 + j, me, chip, True) for j, chip in enumerate(chips)]
        for cp in first:
            cp.start()
        passed = []
        for j, chip in enumerate(chips):
            for w in range(n):
                copy(w, 1 + j, chip, me, False).wait_recv()
                fwd = copy(w, 4 + j, chip, sibling, False)
                fwd.start()
                passed.append(fwd)
        sib_chips = [(ch[0], ch[1], sibling[2]) for ch in chips]
        for w in range(n):
            copy(w, 0, sibling, me, False).wait_recv()
            for j, chip in enumerate(sib_chips):
                copy(w, 4 + j, chip, me, False).wait_recv()
        for cp in first + passed:
            cp.wait_send()
        for cp in local:
            cp.wait()

    return pl.pallas_call(
        body, name="allgather_weights",
        in_specs=[any_spec] * n, out_specs=[any_spec] * n,
        out_shape=[jax.ShapeDtypeStruct((N_DEV,) + a.shape, a.dtype) for a in shards],
        scratch_shapes=[pltpu.SemaphoreType.DMA((7 * n,)), pltpu.SemaphoreType.DMA((7 * n,)),
                        pltpu.SemaphoreType.DMA((n,))],
    )(*shards)


def _alltoall_grads(chunks, name):
    n = len(chunks)
    any_spec = pl.BlockSpec(memory_space=pl.ANY)

    def body(*refs):
        src, dst = refs[:n], refs[n:2 * n]
        send_sems, recv_sems = refs[2 * n:]
        copies = []
        for k in range(1, N_DEV):
            to = _peer(k)
            for w in range(n):
                copies.append(pltpu.make_async_remote_copy(
                    src_ref=src[w].at[_linear(to)], dst_ref=dst[w].at[k - 1],
                    send_sem=send_sems.at[7 * w + k - 1], recv_sem=recv_sems.at[7 * w + k - 1], device_id=to,
                    device_id_type=MESH))
        for cp in copies:
            cp.start()
        for cp in copies:
            cp.wait_recv()
        for cp in copies:
            cp.wait_send()

    return pl.pallas_call(
        body, name=name,
        in_specs=[any_spec] * n, out_specs=[any_spec] * n,
        out_shape=[jax.ShapeDtypeStruct((N_DEV - 1,) + a.shape[1:], a.dtype) for a in chunks],
        scratch_shapes=[pltpu.SemaphoreType.DMA((7 * n,)), pltpu.SemaphoreType.DMA((7 * n,))],
    )(*chunks)


def _allreduce_packed(g):
    rows = g.shape[0]

    def body(g_ref, out_ref, buf, send_sems, recv_sems):
        me = _linear(_peer(0))
        buf[me] = g_ref[...]
        copies = []
        for k in range(1, N_DEV):
            copies.append(pltpu.make_async_remote_copy(
                src_ref=g_ref, dst_ref=buf.at[me], send_sem=send_sems.at[k - 1], recv_sem=recv_sems.at[k - 1],
                device_id=_peer(k), device_id_type=MESH))
        for cp in copies:
            cp.start()
        for k in range(1, N_DEV):
            pltpu.make_async_remote_copy(
                src_ref=g_ref, dst_ref=buf.at[_linear(_peer(k))], send_sem=send_sems.at[k - 1],
                recv_sem=recv_sems.at[k - 1], device_id=_peer(k), device_id_type=MESH).wait_recv()
        for cp in copies:
            cp.wait_send()
        total = buf[0]
        for d in range(1, N_DEV):
            total = total + buf[d]
        out_ref[...] = total

    return pl.pallas_call(
        body, name="allreduce_small",
        in_specs=[pl.BlockSpec(memory_space=pltpu.VMEM)], out_specs=pl.BlockSpec(memory_space=pltpu.VMEM),
        out_shape=jax.ShapeDtypeStruct(g.shape, F32),
        scratch_shapes=[pltpu.VMEM((N_DEV, rows, g.shape[1]), F32), pltpu.SemaphoreType.DMA((7,)),
                        pltpu.SemaphoreType.DMA((7,))],
        compiler_params=pltpu.CompilerParams(vmem_limit_bytes=VMEM_LIMIT),
    )(g)


def _rotary_tables(positions):
    rot_dim = HEAD_DIM // 4
    inv_freq = ROPE_THETA ** (-jnp.arange(0, rot_dim, 2, dtype=F32) / rot_dim)
    dim = jnp.arange(LANES) % HEAD_DIM
    ang = positions.astype(F32)[:, None] * inv_freq[dim % ROT_SHIFT][None, :]
    cos, sin = jnp.cos(ang), jnp.sin(ang)
    first, second = dim < ROT_SHIFT, (dim >= ROT_SHIFT) & (dim < rot_dim)
    c = jnp.where(first | second, cos, 1.0)
    sa = jnp.where(second, sin, 0.0)
    sb = jnp.where(first, -sin, 0.0)
    return [c, sa, sb]


def _block_diag(pool_w):
    gc = pool_w.shape[-1]
    out = jnp.zeros((POOL_WIDTH, POOL_WIDTH), pool_w.dtype)
    for grp in range(pool_w.shape[0]):
        out = lax.dynamic_update_slice(out, pool_w[grp], (grp * gc, grp * gc))
    return out


def _diag_blocks(a):
    gc = POOL_WIDTH // len(POOL_WINDOWS)
    return jnp.stack([a[grp * gc:(grp + 1) * gc, grp * gc:(grp + 1) * gc] for grp in range(len(POOL_WINDOWS))])


def _local_step(x, p, positions, loss_target, norm1, pool_w, pool_scale, norm2, norm3, final_norm,
                w_in, w_out, w_up, w_down, w_gate, w_ple):
    rc, rsa, rsb = _rotary_tables(positions)
    ones_bd = _block_diag(jnp.ones((4, HEAD_DIM, HEAD_DIM), BF16))
    saved = []
    h = x
    for i in range(2):
        tag = f"_l{i}"
        g1, g2, g3 = norm1[i:i + 1], norm2[i:i + 1], norm3[i:i + 1]
        w_bd = _block_diag(pool_w[i]).astype(BF16)
        scale = pool_scale[i:i + 1]
        hn1, u, *qkv = _normproj_fwd(h, g1, w_in[i], rc, rsa, rsb, "normproj_fwd" + tag)
        qkv = [qkv[3 * grp:3 * grp + 3] for grp in range(3)]
        pool_out, y = _pool_fwd(u, w_bd, scale, "pool_fwd" + tag)
        o, lse = zip(*[_attn_fwd(*qkv[grp], f"attn_fwd{tag}_g{grp}") for grp in range(3)])
        h1, a = _outproj_fwd(h, pool_out, o, lse, w_out[i], "outproj_fwd" + tag)
        h2, hn2, r = _mlp_fwd(h1, g2, w_up, w_down, i, "mlp_fwd" + tag)
        h3, hn3, gate, e = _gate_fwd(h2, g3, w_gate[i], p[i], w_ple[i], "gate_fwd" + tag)
        saved.append(dict(h0=h, hn1=hn1, qkv=qkv, y=y, o=o, lse=lse, a=a, h1=h1, hn2=hn2, r=r, h2=h2,
                          hn3=hn3, gate=gate, e=e, w_bd=w_bd, scale=scale, g1=g1, g2=g2, g3=g3))
        h = h3
    loss, dh, d_final = _loss_head(h, final_norm.reshape(1, D_MODEL), loss_target, "loss_head")

    grads = [None, None]
    for i in (1, 0):
        tag = f"_l{i}"
        sv = saved[i]
        dh2, dgl, de, dg3 = _gate_bwd(dh, sv["gate"], sv["e"], sv["h2"], sv["g3"], w_gate[i], "gate_bwd" + tag)
        dw_gate = _matmul_tn(sv["hn3"], dgl, "dw_gate" + tag)
        dw_ple = _matmul_tn(p[i], de, "dw_ple" + tag)
        dh1, dup, dg2 = _mlp_bwd(dh2, sv["r"], sv["h1"], sv["g2"], w_up, w_down, i, "mlp_bwd" + tag)
        dw_down = _matmul_tn(sv["r"], dh2, "dw_down" + tag, square_a=True)
        dw_up = _matmul_tn(sv["hn2"], dup, "dw_up" + tag, tn=FF_BLOCK, blocked_out=True)
        dpool, do0, do1, do2, de0, de1, de2 = _outproj_bwd(dh1, w_out[i], sv["o"], sv["lse"], ones_bd,
                                                           "outproj_bwd" + tag)
        dw_out = _matmul_tn(sv["a"], dh1, "dw_out" + tag)
        dqkv = [_attn_bwd(*sv["qkv"][grp], do_g, sv["lse"][grp], de_g, f"attn_bwd{tag}_g{grp}")
                for grp, (do_g, de_g) in enumerate(((do0, de0), (do1, de1), (do2, de2)))]
        dq, dk, dv = zip(*dqkv)
        du, dw_bd, dscale = _pool_bwd(dpool, sv["y"], sv["w_bd"], sv["scale"], "pool_bwd" + tag)
        dh, dz, dg1 = _normproj_bwd(dh1, du, dq, dk, dv, rc, rsa, rsb, w_in[i], sv["h0"], sv["g1"],
                                    "normproj_bwd" + tag)
        dw_in = _matmul_tn(sv["hn1"], dz, "dw_in" + tag, tn=512)
        grads[i] = dict(norm1=dg1, norm2=dg2, norm3=dg3, pool_w=_diag_blocks(dw_bd), pool_scale=dscale,
                        w_in=dw_in, w_out=dw_out, w_up=dw_up, w_down=dw_down, w_gate=dw_gate, w_ple=dw_ple)
    return loss, dh, grads, d_final


def _pack_small(norm1, norm2, norm3, final_norm, pool_scale, pool_w):
    scale_row = jnp.concatenate([pool_scale.reshape(1, 2 * POOL_WIDTH), jnp.zeros((1, D_MODEL - 2 * POOL_WIDTH), F32)], axis=1)
    return jnp.concatenate([norm1, norm2, norm3, final_norm.reshape(1, D_MODEL), scale_row,
                            pool_w.reshape(32, D_MODEL)], axis=0)


def _unpack_small(a):
    return dict(norm1=a[0:2], norm2=a[2:4], norm3=a[4:6], final_norm=a[6], pool_scale=a[7, 0:2 * POOL_WIDTH].reshape(2, POOL_WIDTH),
                pool_w=a[8:40].reshape(2, 4, HEAD_DIM, HEAD_DIM))


def _chunks_cols(a, cols):
    return a.reshape(a.shape[0], N_DEV, cols).transpose(1, 0, 2)


def _chunks_rows(a, rows):
    return a.reshape(N_DEV, rows, a.shape[1])


BIG = ("w_in", "w_out", "w_up", "w_down", "w_gate", "w_ple")
SMALL = ("norm1", "norm2", "norm3", "final_norm", "pool_scale", "pool_w")
ORDER = ("norm1", "w_in", "pool_w", "pool_scale", "w_out", "norm2", "w_up", "w_down", "norm3", "w_gate", "w_ple",
         "final_norm")


def kernel(x, p, positions, norm1, w_in, pool_w, pool_scale, w_out, norm2, w_up, w_down, norm3, w_gate, w_ple, final_norm, loss_target, m_norm1, m_w_in, m_pool_w, m_pool_scale, m_w_out, m_norm2, m_w_up, m_w_down, m_norm3, m_w_gate, m_w_ple, m_final_norm, v_norm1, v_w_in, v_pool_w, v_pool_scale, v_w_out, v_norm2, v_w_up, v_w_down, v_norm3, v_w_gate, v_w_ple, v_final_norm):
    w = dict(norm1=norm1, w_in=w_in, pool_w=pool_w, pool_scale=pool_scale, w_out=w_out, norm2=norm2, w_up=w_up,
             w_down=w_down, norm3=norm3, w_gate=w_gate, w_ple=w_ple, final_norm=final_norm)
    m = dict(norm1=m_norm1, w_in=m_w_in, pool_w=m_pool_w, pool_scale=m_pool_scale, w_out=m_w_out, norm2=m_norm2,
             w_up=m_w_up, w_down=m_w_down, norm3=m_norm3, w_gate=m_w_gate, w_ple=m_w_ple, final_norm=m_final_norm)
    v = dict(norm1=v_norm1, w_in=v_w_in, pool_w=v_pool_w, pool_scale=v_pool_scale, w_out=v_w_out, norm2=v_norm2,
             w_up=v_w_up, w_down=v_w_down, norm3=v_norm3, w_gate=v_w_gate, w_ple=v_w_ple, final_norm=v_final_norm)
    seq = x.shape[1]

    gathered = dict(zip(BIG, _allgather_weights([w[n].astype(BF16) for n in BIG])))
    full = dict(
        w_in=gathered["w_in"].transpose(1, 2, 0, 3).reshape(2, D_MODEL, N_IN),
        w_out=gathered["w_out"].transpose(1, 0, 2, 3).reshape(2, D_MODEL, D_MODEL),
        w_up=gathered["w_up"], w_down=gathered["w_down"],
        w_gate=gathered["w_gate"].transpose(1, 0, 2, 3).reshape(2, D_MODEL, D_MODEL),
        w_ple=gathered["w_ple"].transpose(1, 2, 0, 3).reshape(2, PLE_DIM, D_MODEL))

    loss, dx, grads, d_final = _local_step(
        x.reshape(seq, D_MODEL), p.reshape(2, seq, PLE_DIM), positions.reshape(seq), loss_target.reshape(seq, D_MODEL),
        norm1, pool_w, pool_scale, norm2, norm3, final_norm, **full)

    me = 4 * lax.axis_index("x") + 2 * lax.axis_index("y") + lax.axis_index("c")
    own, recv = {n: [] for n in BIG}, {n: [] for n in BIG}
    for i in (1, 0):
        gr = grads[i]
        f32c = dict(
            w_in=_chunks_cols(gr["w_in"][0], N_IN // N_DEV), w_out=_chunks_rows(gr["w_out"][0], D_MODEL // N_DEV),
            w_up=gr["w_up"][0], w_down=_chunks_rows(gr["w_down"][0], FF_BLOCK),
            w_gate=_chunks_rows(gr["w_gate"][0], D_MODEL // N_DEV), w_ple=_chunks_cols(gr["w_ple"][0], D_MODEL // N_DEV))
        bf16c = dict(
            w_in=_chunks_cols(gr["w_in"][1], N_IN // N_DEV), w_out=_chunks_rows(gr["w_out"][1], D_MODEL // N_DEV),
            w_up=gr["w_up"][1], w_down=_chunks_rows(gr["w_down"][1], FF_BLOCK),
            w_gate=_chunks_rows(gr["w_gate"][1], D_MODEL // N_DEV), w_ple=_chunks_cols(gr["w_ple"][1], D_MODEL // N_DEV))
        got = _alltoall_grads([bf16c[n] for n in BIG], f"alltoall_grads_l{i}")
        for n, r in zip(BIG, got):
            recv[n].insert(0, r)
            own[n].insert(0, lax.dynamic_index_in_dim(f32c[n], me, axis=0, keepdims=False))
    small_g = _pack_small(
        *[jnp.concatenate([grads[0][n], grads[1][n]], axis=0) for n in ("norm1", "norm2", "norm3")], d_final.reshape(D_MODEL),
        jnp.concatenate([grads[0]["pool_scale"], grads[1]["pool_scale"]], axis=0),
        jnp.stack([grads[0]["pool_w"], grads[1]["pool_w"]]))
    small_g = _allreduce_packed(small_g)

    g_out, d_out, m_out, v_out = {}, {}, {}, {}
    for n in BIG:
        g_out[n], d_out[n], m_out[n], v_out[n] = _adamw_sharded(
            w[n], m[n], v[n], jnp.stack(own[n]), recv[n][0], recv[n][1], "adamw_" + n)
    pack = lambda t: _pack_small(*[t[n] for n in SMALL])
    d_small, m_small, v_small = _adamw_packed(pack(w), small_g, pack(m), pack(v), "adamw_small")
    for dst, a in ((g_out, small_g), (d_out, d_small), (m_out, m_small), (v_out, v_small)):
        dst.update(_unpack_small(a))

    loss = lax.psum(loss[0, 0], ("x", "y", "c"))
    return (loss, dx.reshape(1, seq, D_MODEL), *[g_out[n] for n in ORDER], *[d_out[n] for n in ORDER],
            *[m_out[n] for n in ORDER], *[v_out[n] for n in ORDER])
```

```python
import functools

import jax
import jax.numpy as jnp
from jax import lax
from jax.experimental import pallas as pl
from jax.experimental.pallas import tpu as pltpu

F32 = jnp.float32
BF16 = jnp.bfloat16

D_MODEL = 1024
HEAD_DIM = 64
POOL_WIDTH = 256
POOL_WINDOWS = (2, 4, 8, 16)
POOL_HALO = 16
GROUP_WIDTH = 256
DILATIONS = (1, 4, 16)
ATTN_BLOCK = 128
ROT_SHIFT = 8
ROPE_THETA = 500000.0
D_FF = 4096
FF_BLOCK = 512
N_DEV = 8
N_IN = POOL_WIDTH + 3 * 768
PLE_DIM = 256
EPS = 1e-6
NEG_BIG = -1e30

ADAM_LR = 0.001
ADAM_B1 = 0.9
ADAM_B2 = 0.999
ADAM_EPS = 1e-08
ADAM_WD = 0.01
ADAM_STEP = 10

LANES = 128
VMEM_LIMIT = 56 * 1024 * 1024
MESH = pl.DeviceIdType.MESH


def _params(n_grid):
    return pltpu.CompilerParams(dimension_semantics=("arbitrary",) * n_grid, vmem_limit_bytes=VMEM_LIMIT)


def _dot(a, b):
    return jnp.dot(a, b, preferred_element_type=F32)


def _dot_nt(a, b):
    return lax.dot_general(a, b, (((1,), (1,)), ((), ())), preferred_element_type=F32)


def _dot_tn(a, b):
    return lax.dot_general(a, b, (((0,), (0,)), ((), ())), preferred_element_type=F32)


def _rms(x, g):
    rstd = lax.rsqrt(jnp.mean(x * x, axis=-1, keepdims=True) + EPS)
    n = x * rstd
    return n, rstd, n * g


def _rms_bwd(dy, n, rstd, g):
    dyn = dy * g
    dx = rstd * (dyn - n * jnp.mean(dyn * n, axis=-1, keepdims=True))
    return dx, jnp.sum(dy * n, axis=0, keepdims=True)


def _ordered_after(body, n_in, after):
    if not after:
        return body
    return lambda *refs: body(*refs[:n_in], *refs[n_in + len(after):])


def _row_tile(s, t):
    t = min(s, t)
    assert s % t == 0
    return t


def _rot(z, c, sa, sb):
    return z * c + pltpu.roll(z, ROT_SHIFT, 1) * sa + pltpu.roll(z, LANES - ROT_SHIFT, 1) * sb


def _rot_t(dz, c, sa, sb):
    return dz * c + pltpu.roll(dz * sa, LANES - ROT_SHIFT, 1) + pltpu.roll(dz * sb, ROT_SHIFT, 1)


def _to_residues(value, stage, out_ref, dil):
    if dil == 1:
        out_ref[0] = value.astype(out_ref.dtype)
        return
    rows = value.shape[0] // dil
    for hf in range(GROUP_WIDTH // LANES):
        lanes = slice(hf * LANES, (hf + 1) * LANES)
        stage[hf][...] = value[:, lanes]
        for r in range(dil):
            out_ref[r, :, lanes] = stage[hf][pl.ds(r, rows, stride=dil), :].astype(out_ref.dtype)


def _from_residues(in_ref, stage, dil):
    if dil == 1:
        return in_ref[0]
    rows = in_ref.shape[1]
    for hf in range(GROUP_WIDTH // LANES):
        for r in range(dil):
            stage[hf][pl.ds(r, rows, stride=dil), :] = in_ref[r, :, hf * LANES:(hf + 1) * LANES]
    return jnp.concatenate([stage[0][...], stage[1][...]], axis=1)


def _residue_spec(dil, t):
    return pl.BlockSpec((dil, t // dil, GROUP_WIDTH), lambda i: (0, i, 0))


def _residue_shape(dil, s, dtype):
    return jax.ShapeDtypeStruct((dil, s // dil, GROUP_WIDTH), dtype)


def _stages(t, n):
    return [pltpu.VMEM((t, LANES), F32)] * (n * (GROUP_WIDTH // LANES))


def _pair_stages(refs):
    return [refs[i:i + 2] for i in range(0, len(refs), 2)]


def _normproj_fwd(h, g, w_in, rc, rsa, rsb, name):
    s = h.shape[0]
    t = _row_tile(s, 512)

    def body(h_ref, g_ref, w_ref, c_ref, sa_ref, sb_ref, hn_ref, u_ref, *rest):
        qkv_refs, stages = rest[:9], _pair_stages(rest[9:])
        _, _, hn = _rms(h_ref[...], g_ref[...])
        hb = hn.astype(BF16)
        hn_ref[...] = hb
        c, sa, sb = c_ref[...], sa_ref[...], sb_ref[...]

        def rot(z, scale):
            halves = [_rot(z[:, hf * LANES:(hf + 1) * LANES], c, sa, sb) * scale for hf in range(2)]
            return jnp.concatenate(halves, axis=1)

        u_ref[...] = _dot(hb, w_ref[:, 0:POOL_WIDTH])
        for grp, dil in enumerate(DILATIONS):
            lo = POOL_WIDTH + grp * GROUP_WIDTH
            q_ref, k_ref, v_ref = qkv_refs[3 * grp:3 * grp + 3]
            _to_residues(rot(_dot(hb, w_ref[:, lo:lo + GROUP_WIDTH]), HEAD_DIM ** -0.5), stages[0], q_ref, dil)
            _to_residues(rot(_dot(hb, w_ref[:, lo + 768:lo + 768 + GROUP_WIDTH]), 1.0), stages[1], k_ref, dil)
            _to_residues(_dot(hb, w_ref[:, lo + 1536:lo + 1536 + GROUP_WIDTH]), stages[2], v_ref, dil)

    row = lambda w: pl.BlockSpec((t, w), lambda i: (i, 0))
    return pl.pallas_call(
        body, name=name, grid=(s // t,),
        in_specs=[row(D_MODEL), pl.BlockSpec((1, D_MODEL), lambda i: (0, 0)),
                  pl.BlockSpec((D_MODEL, N_IN), lambda i: (0, 0)), row(LANES), row(LANES), row(LANES)],
        out_specs=[row(D_MODEL), row(POOL_WIDTH)] + [_residue_spec(dil, t) for dil in DILATIONS for _ in range(3)],
        out_shape=[jax.ShapeDtypeStruct((s, D_MODEL), BF16), jax.ShapeDtypeStruct((s, POOL_WIDTH), F32)]
        + [_residue_shape(dil, s, BF16) for dil in DILATIONS for _ in range(3)],
        scratch_shapes=_stages(t, 3),
        compiler_params=_params(1),
    )(h, g, w_in, rc, rsa, rsb)


def _pool_lane_window():
    lane = lax.broadcasted_iota(jnp.int32, (1, POOL_WIDTH), 1)
    return jnp.left_shift(2, lane // (POOL_WIDTH // len(POOL_WINDOWS)))


def _pool_fwd(u, w_bd, scale, name):
    s = u.shape[0]
    t = _row_tile(s, 512)

    def body(u_ref, w_ref, sc_ref, out_ref, y_ref, ext):
        i = pl.program_id(0)

        @pl.when(i == 0)
        def _():
            ext[0:POOL_HALO, :] = jnp.zeros((POOL_HALO, POOL_WIDTH), F32)

        x = u_ref[...]
        ext[POOL_HALO:, :] = x
        win = _pool_lane_window()
        acc = x
        wsum = jnp.zeros_like(x)
        for k in range(1, POOL_HALO):
            acc = acc + ext[POOL_HALO - k:POOL_HALO - k + t, :]
            if k + 1 in POOL_WINDOWS:
                wsum = jnp.where(win == k + 1, acc, wsum)
        pos = i * t + lax.broadcasted_iota(jnp.int32, (t, POOL_WIDTH), 0)
        cnt = jnp.minimum(pos + 1, win).astype(F32)
        y = wsum / cnt - x
        yb = y.astype(BF16)
        y_ref[...] = yb
        out_ref[...] = _dot(yb, w_ref[...]) * sc_ref[...]
        ext[0:POOL_HALO, :] = x[t - POOL_HALO:, :]

    row = pl.BlockSpec((t, POOL_WIDTH), lambda i: (i, 0))
    return pl.pallas_call(
        body, name=name, grid=(s // t,),
        in_specs=[row, pl.BlockSpec((POOL_WIDTH, POOL_WIDTH), lambda i: (0, 0)),
                  pl.BlockSpec((1, POOL_WIDTH), lambda i: (0, 0))],
        out_specs=[row, row],
        out_shape=[jax.ShapeDtypeStruct((s, POOL_WIDTH), F32), jax.ShapeDtypeStruct((s, POOL_WIDTH), BF16)],
        scratch_shapes=[pltpu.VMEM((t + POOL_HALO, POOL_WIDTH), F32)],
        compiler_params=_params(1),
    )(u, w_bd, scale)


def _head_masks():
    lane = lax.broadcasted_iota(jnp.int32, (ATTN_BLOCK, GROUP_WIDTH), 1)
    return [lane // HEAD_DIM == hd for hd in range(GROUP_WIDTH // HEAD_DIM)]


def _stack_heads(a, masks):
    zero = jnp.zeros_like(a)
    return jnp.concatenate([jnp.where(m, a, zero) for m in masks], axis=0)


def _band_mask(first_block):
    rows = ATTN_BLOCK * (GROUP_WIDTH // HEAD_DIM)
    i = lax.broadcasted_iota(jnp.int32, (rows, 2 * ATTN_BLOCK), 0) & (ATTN_BLOCK - 1)
    j = lax.broadcasted_iota(jnp.int32, (rows, 2 * ATTN_BLOCK), 1)
    return (j >= i) & (j <= i + ATTN_BLOCK) & ((j >= ATTN_BLOCK) | jnp.logical_not(first_block))


def _column_per_head(a):
    return jnp.concatenate([a[:, hd * HEAD_DIM:hd * HEAD_DIM + 1] for hd in range(GROUP_WIDTH // HEAD_DIM)], axis=0)


def _blocks_per_step(nb):
    return 2 if nb % 2 == 0 else 1


def _attn_fwd(q, k, v, name):
    dil, length, _ = q.shape
    nb = length // ATTN_BLOCK
    qb = _blocks_per_step(nb)

    def body(q_ref, kp_ref, kc_ref, vp_ref, vc_ref, o_ref, lse_ref):
        j = pl.program_id(1)
        masks = _head_masks()
        for qi in range(qb):
            here = slice(qi * ATTN_BLOCK, (qi + 1) * ATTN_BLOCK)
            before = slice((qi - 1) * ATTN_BLOCK, qi * ATTN_BLOCK)
            kcat = jnp.concatenate([kp_ref[...] if qi == 0 else kc_ref[before], kc_ref[here]], axis=0)
            vcat = jnp.concatenate([vp_ref[...] if qi == 0 else vc_ref[before], vc_ref[here]], axis=0)
            first = (j == 0) if qi == 0 else False
            qs = _stack_heads(q_ref[here], masks)
            sc = jnp.where(_band_mask(first), _dot_nt(qs, kcat), NEG_BIG)
            m = jnp.max(sc, axis=1, keepdims=True)
            e = jnp.exp(sc - m)
            l = jnp.sum(e, axis=1, keepdims=True)
            p = (e / l).astype(BF16)
            lse = m + jnp.log(l)
            o = jnp.zeros((ATTN_BLOCK, GROUP_WIDTH), F32)
            lse_full = jnp.zeros((ATTN_BLOCK, GROUP_WIDTH), F32)
            for hd, msk in enumerate(masks):
                rows = slice(hd * ATTN_BLOCK, (hd + 1) * ATTN_BLOCK)
                o = jnp.where(msk, _dot(p[rows], vcat), o)
                lse_full = jnp.where(msk, lse[rows], lse_full)
            o_ref[here] = o
            lse_ref[here] = lse_full

    cur = pl.BlockSpec((None, qb * ATTN_BLOCK, GROUP_WIDTH), lambda r, j: (r, j, 0))
    prev = pl.BlockSpec((None, ATTN_BLOCK, GROUP_WIDTH), lambda r, j: (r, jnp.maximum(qb * j - 1, 0), 0))
    return pl.pallas_call(
        body, name=name, grid=(dil, nb // qb),
        in_specs=[cur, prev, cur, prev, cur], out_specs=[cur, cur],
        out_shape=[jax.ShapeDtypeStruct(q.shape, F32)] * 2,
        compiler_params=_params(2),
    )(q, k, k, v, v)


def _group_weights(l0, l1, l2):
    m = jnp.maximum(jnp.maximum(l0, l1), l2)
    e0, e1, e2 = jnp.exp(l0 - m), jnp.exp(l1 - m), jnp.exp(l2 - m)
    den = e0 + e1 + e2
    return e0 / den, e1 / den, e2 / den


def _outproj_fwd(h, pool_out, o, lse, w_out, name):
    s = h.shape[0]
    t = _row_tile(s, 512)

    def body(h_ref, po_ref, o0, o1, o2, l0, l1, l2, w_ref, out_ref, a_ref, *stages):
        stages = _pair_stages(stages)
        ov =[_from_residues(r, stages[i], DILATIONS[i]) for i, r in enumerate((o0, o1, o2))]
        lv = [_from_residues(r, stages[3 + i], DILATIONS[i]) for i, r in enumerate((l0, l1, l2))]
        wts = _group_weights(*lv)
        a = jnp.concatenate([po_ref[...]] + [ov[i] * wts[i] for i in range(3)], axis=1).astype(BF16)
        a_ref[...] = a
        out_ref[...] = h_ref[...] + _dot(a, w_ref[...])

    row = lambda w: pl.BlockSpec((t, w), lambda i: (i, 0))
    res = [_residue_spec(dil, t) for dil in DILATIONS]
    return pl.pallas_call(
        body, name=name, grid=(s // t,),
        in_specs=[row(D_MODEL), row(POOL_WIDTH)] + res + res + [pl.BlockSpec((D_MODEL, D_MODEL), lambda i: (0, 0))],
        out_specs=[row(D_MODEL), row(D_MODEL)],
        out_shape=[jax.ShapeDtypeStruct((s, D_MODEL), F32), jax.ShapeDtypeStruct((s, D_MODEL), BF16)],
        scratch_shapes=_stages(t, 6),
        compiler_params=_params(1),
    )(h, pool_out, *o, *lse, w_out)


def _mlp_fwd(h, g, w_up, w_down, name):
    s = h.shape[0]
    t = _row_tile(s, 1024)
    nblk = D_FF // FF_BLOCK

    def body(h_ref, g_ref, wu_ref, wd_ref, out_ref, hn_ref, r_ref, hb_s, acc):
        j = pl.program_id(1)

        @pl.when(j == 0)
        def _():
            _, _, hn = _rms(h_ref[...], g_ref[...])
            hb = hn.astype(BF16)
            hb_s[...] = hb
            hn_ref[...] = hb
            acc[...] = jnp.zeros_like(acc)

        r = jnp.maximum(_dot(hb_s[...], wu_ref[...]), 0.0)
        r_ref[...] = r.astype(BF16)
        acc[...] += _dot((r * r).astype(BF16), wd_ref[...])

        @pl.when(j == nblk - 1)
        def _():
            out_ref[...] = h_ref[...] + acc[...]

    row = pl.BlockSpec((t, D_MODEL), lambda i, j: (i, 0))
    return pl.pallas_call(
        body, name=name, grid=(s // t, nblk),
        in_specs=[row, pl.BlockSpec((1, D_MODEL), lambda i, j: (0, 0)),
                  pl.BlockSpec((None, D_MODEL, FF_BLOCK), lambda i, j: (j, 0, 0)),
                  pl.BlockSpec((None, FF_BLOCK, D_MODEL), lambda i, j: (j, 0, 0))],
        out_specs=[row, row, pl.BlockSpec((t, FF_BLOCK), lambda i, j: (i, j))],
        out_shape=[jax.ShapeDtypeStruct((s, D_MODEL), F32), jax.ShapeDtypeStruct((s, D_MODEL), BF16),
                   jax.ShapeDtypeStruct((s, D_FF), BF16)],
        scratch_shapes=[pltpu.VMEM((t, D_MODEL), BF16), pltpu.VMEM((t, D_MODEL), F32)],
        compiler_params=_params(2),
    )(h, g, w_up, w_down)


def _gate_fwd(h, g, w_gate, p, w_ple, name):
    s = h.shape[0]
    t = _row_tile(s, 512)

    def body(h_ref, g_ref, wg_ref, p_ref, wp_ref, out_ref, hn_ref, gate_ref, e_ref):
        x = h_ref[...]
        _, _, hn = _rms(x, g_ref[...])
        hb = hn.astype(BF16)
        hn_ref[...] = hb
        gate = 1.0 / (1.0 + jnp.exp(-_dot(hb, wg_ref[...])))
        e = _dot(p_ref[...].astype(BF16), wp_ref[...])
        gate_ref[...] = gate
        e_ref[...] = e
        out_ref[...] = x + gate * e

    row = lambda w: pl.BlockSpec((t, w), lambda i: (i, 0))
    full = lambda a, b: pl.BlockSpec((a, b), lambda i: (0, 0))
    return pl.pallas_call(
        body, name=name, grid=(s // t,),
        in_specs=[row(D_MODEL), full(1, D_MODEL), full(D_MODEL, D_MODEL), row(PLE_DIM), full(PLE_DIM, D_MODEL)],
        out_specs=[row(D_MODEL)] * 4,
        out_shape=[jax.ShapeDtypeStruct((s, D_MODEL), F32), jax.ShapeDtypeStruct((s, D_MODEL), BF16),
                   jax.ShapeDtypeStruct((s, D_MODEL), F32), jax.ShapeDtypeStruct((s, D_MODEL), F32)],
        compiler_params=_params(1),
    )(h, g, w_gate, p, w_ple)


def _loss_head(h, g, target, name):
    s = h.shape[0]
    t = _row_tile(s, 512)

    def body(h_ref, g_ref, t_ref, loss_ref, dh_ref, dg_ref):
        i = pl.program_id(0)

        @pl.when(i == 0)
        def _():
            loss_ref[...] = jnp.zeros_like(loss_ref)
            dg_ref[...] = jnp.zeros_like(dg_ref)

        gv = g_ref[...]
        n, rstd, y = _rms(h_ref[...], gv)
        err = y - t_ref[...]
        loss_ref[...] += jnp.sum(err * err) * (0.5 / D_MODEL)
        dx, dg = _rms_bwd(err * (1.0 / D_MODEL), n, rstd, gv)
        dh_ref[...] = dx
        dg_ref[...] += dg

    row = pl.BlockSpec((t, D_MODEL), lambda i: (i, 0))
    vec = pl.BlockSpec((1, D_MODEL), lambda i: (0, 0))
    return pl.pallas_call(
        body, name=name, grid=(s // t,),
        in_specs=[row, vec, row],
        out_specs=[pl.BlockSpec((1, LANES), lambda i: (0, 0)), row, vec],
        out_shape=[jax.ShapeDtypeStruct((1, LANES), F32), jax.ShapeDtypeStruct((s, D_MODEL), F32),
                   jax.ShapeDtypeStruct((1, D_MODEL), F32)],
        compiler_params=_params(1),
    )(h, g, target)


def _gate_bwd(dh, gate, e, h, g, w_gate, name, after=()):
    s = h.shape[0]
    t = _row_tile(s, 512)

    def body(dh_ref, gate_ref, e_ref, h_ref, g_ref, wg_ref, out_ref, dgl_ref, de_ref, dg_ref):
        @pl.when(pl.program_id(0) == 0)
        def _():
            dg_ref[...] = jnp.zeros_like(dg_ref)

        d = dh_ref[...]
        gate = gate_ref[...]
        dgl = (d * e_ref[...] * gate * (1.0 - gate)).astype(BF16)
        dgl_ref[...] = dgl
        de_ref[...] = (d * gate).astype(BF16)
        gv = g_ref[...]
        n, rstd, _ = _rms(h_ref[...], gv)
        dx, dg = _rms_bwd(_dot_nt(dgl, wg_ref[...]), n, rstd, gv)
        out_ref[...] = d + dx
        dg_ref[...] += dg

    row = pl.BlockSpec((t, D_MODEL), lambda i: (i, 0))
    vec = pl.BlockSpec((1, D_MODEL), lambda i: (0, 0))
    return pl.pallas_call(
        _ordered_after(body, 6, after), name=name, grid=(s // t,),
        in_specs=[row, row, row, row, vec, pl.BlockSpec((D_MODEL, D_MODEL), lambda i: (0, 0))]
        + [pl.BlockSpec(memory_space=pl.ANY)] * len(after),
        out_specs=[row, row, row, vec],
        out_shape=[jax.ShapeDtypeStruct((s, D_MODEL), F32), jax.ShapeDtypeStruct((s, D_MODEL), BF16),
                   jax.ShapeDtypeStruct((s, D_MODEL), BF16), jax.ShapeDtypeStruct((1, D_MODEL), F32)],
        compiler_params=_params(1),
    )(dh, gate, e, h, g, w_gate, *after)


def _mlp_bwd(dh, r, h, g, w_up, w_down, name):
    s = h.shape[0]
    t = _row_tile(s, 1024)
    nblk = D_FF // FF_BLOCK

    def body(dh_ref, r_ref, h_ref, g_ref, wu_ref, wd_ref, out_ref, dup_ref, dg_ref, db_s, acc):
        i, j = pl.program_id(0), pl.program_id(1)

        @pl.when((i == 0) & (j == 0))
        def _():
            dg_ref[...] = jnp.zeros_like(dg_ref)

        @pl.when(j == 0)
        def _():
            db_s[...] = dh_ref[...].astype(BF16)
            acc[...] = jnp.zeros_like(acc)

        dup = (_dot_nt(db_s[...], wd_ref[...]) * (2.0 * r_ref[...].astype(F32))).astype(BF16)
        dup_ref[...] = dup
        acc[...] += _dot_nt(dup, wu_ref[...])

        @pl.when(j == nblk - 1)
        def _():
            gv = g_ref[...]
            n, rstd, _ = _rms(h_ref[...], gv)
            dx, dg = _rms_bwd(acc[...], n, rstd, gv)
            out_ref[...] = dh_ref[...] + dx
            dg_ref[...] += dg

    row = pl.BlockSpec((t, D_MODEL), lambda i, j: (i, 0))
    vec = pl.BlockSpec((1, D_MODEL), lambda i, j: (0, 0))
    blk = pl.BlockSpec((t, FF_BLOCK), lambda i, j: (i, j))
    return pl.pallas_call(
        body, name=name, grid=(s // t, nblk),
        in_specs=[row, blk, row, vec,
                  pl.BlockSpec((None, D_MODEL, FF_BLOCK), lambda i, j: (j, 0, 0)),
                  pl.BlockSpec((None, FF_BLOCK, D_MODEL), lambda i, j: (j, 0, 0))],
        out_specs=[row, blk, vec],
        out_shape=[jax.ShapeDtypeStruct((s, D_MODEL), F32), jax.ShapeDtypeStruct((s, D_FF), BF16),
                   jax.ShapeDtypeStruct((1, D_MODEL), F32)],
        scratch_shapes=[pltpu.VMEM((t, D_MODEL), BF16), pltpu.VMEM((t, D_MODEL), F32)],
        compiler_params=_params(2),
    )(dh, r, h, g, w_up, w_down)


def _outproj_bwd(dh, w_out, o, lse, ones_bd, name):
    s = dh.shape[0]
    t = _row_tile(s, 512)

    def body(dh_ref, w_ref, o0, o1, o2, l0, l1, l2, bd_ref, dp_ref, do0, do1, do2, de0, de1, de2, *stages):
        stages = _pair_stages(stages)
        da = _dot_nt(dh_ref[...].astype(BF16), w_ref[...])
        dp_ref[...] = da[:, 0:POOL_WIDTH]
        ov =[_from_residues(r, stages[i], DILATIONS[i]) for i, r in enumerate((o0, o1, o2))]
        lv = [_from_residues(r, stages[3 + i], DILATIONS[i]) for i, r in enumerate((l0, l1, l2))]
        wts = _group_weights(*lv)
        bd = bd_ref[...]
        cbar = jnp.zeros((t, GROUP_WIDTH), F32)
        for grp, do_ref in enumerate((do0, do1, do2)):
            lo = POOL_WIDTH + grp * GROUP_WIDTH
            dag = da[:, lo:lo + GROUP_WIDTH]
            _to_residues(dag * wts[grp], stages[6 + grp], do_ref, DILATIONS[grp])
            prod = dag * ov[grp]
            hi = prod.astype(BF16)
            low = (prod - hi.astype(F32)).astype(BF16)
            cbar = cbar + wts[grp] * (_dot(hi, bd) + _dot(low, bd))
        for grp, de_ref in enumerate((de0, de1, de2)):
            _to_residues(wts[grp] * cbar, stages[9 + grp], de_ref, DILATIONS[grp])

    row = lambda w: pl.BlockSpec((t, w), lambda i: (i, 0))
    full = lambda a, b: pl.BlockSpec((a, b), lambda i: (0, 0))
    res = [_residue_spec(dil, t) for dil in DILATIONS]
    return pl.pallas_call(
        body, name=name, grid=(s // t,),
        in_specs=[row(D_MODEL), full(D_MODEL, D_MODEL)] + res + res + [full(GROUP_WIDTH, GROUP_WIDTH)],
        out_specs=[row(POOL_WIDTH)] + res + res,
        out_shape=[jax.ShapeDtypeStruct((s, POOL_WIDTH), F32)] + [_residue_shape(dil, s, BF16) for dil in DILATIONS]
        + [_residue_shape(dil, s, F32) for dil in DILATIONS],
        scratch_shapes=_stages(t, 12),
        compiler_params=_params(1),
    )(dh, w_out, *o, *lse, ones_bd)


def _attn_bwd(q, k, v, do, lse, deff, name, after=()):
    dil, length, _ = q.shape
    nb = length // ATTN_BLOCK
    qb = _blocks_per_step(nb)
    nj = nb // qb
    tail = slice((qb - 1) * ATTN_BLOCK, qb * ATTN_BLOCK)

    def body(q_ref, kp_ref, kc_ref, vp_ref, vc_ref, do_ref, lse_ref, de_ref, dq_ref, dk_ref, dv_ref, ck, cv):
        j = pl.program_id(1)

        @pl.when(j < nj)
        def _():
            masks = _head_masks()
            dkc, dvc = [], []
            for qi in range(qb):
                here = slice(qi * ATTN_BLOCK, (qi + 1) * ATTN_BLOCK)
                before = slice((qi - 1) * ATTN_BLOCK, qi * ATTN_BLOCK)
                kcat = jnp.concatenate([kp_ref[...] if qi == 0 else kc_ref[before], kc_ref[here]], axis=0)
                vcat = jnp.concatenate([vp_ref[...] if qi == 0 else vc_ref[before], vc_ref[here]], axis=0)
                first = (j == 0) if qi == 0 else False
                qs = _stack_heads(q_ref[here], masks)
                dos = _stack_heads(do_ref[here], masks)
                sc = jnp.where(_band_mask(first), _dot_nt(qs, kcat), NEG_BIG)
                p = jnp.exp(sc - _column_per_head(lse_ref[here]))
                ds = (p * (_dot_nt(dos, vcat) - _column_per_head(de_ref[here]))).astype(BF16)
                dq = jnp.zeros((ATTN_BLOCK, GROUP_WIDTH), F32)
                for hd, msk in enumerate(masks):
                    dq = jnp.where(msk, _dot(ds[hd * ATTN_BLOCK:(hd + 1) * ATTN_BLOCK], kcat), dq)
                dq_ref[here] = dq
                dkc.append(_dot_tn(ds, qs))
                dvc.append(_dot_tn(p.astype(BF16), dos))

            for out_ref, carry, parts in ((dk_ref, ck, dkc), (dv_ref, cv, dvc)):
                @pl.when(j > 0)
                def _():
                    if qb > 1:
                        out_ref[0:(qb - 1) * ATTN_BLOCK] = carry[0:(qb - 1) * ATTN_BLOCK]
                    out_ref[tail] = carry[tail] + parts[0][0:ATTN_BLOCK]

                for qi in range(qb - 1):
                    carry[qi * ATTN_BLOCK:(qi + 1) * ATTN_BLOCK] = parts[qi][ATTN_BLOCK:] + parts[qi + 1][0:ATTN_BLOCK]
                carry[tail] = parts[qb - 1][ATTN_BLOCK:]

        @pl.when(j == nj)
        def _():
            dk_ref[...] = ck[...]
            dv_ref[...] = cv[...]

    step = lambda j: jnp.minimum(j, nj - 1)
    cur = pl.BlockSpec((None, qb * ATTN_BLOCK, GROUP_WIDTH), lambda r, j: (r, step(j), 0))
    prev = pl.BlockSpec((None, ATTN_BLOCK, GROUP_WIDTH), lambda r, j: (r, jnp.maximum(qb * step(j) - 1, 0), 0))
    late = pl.BlockSpec((None, qb * ATTN_BLOCK, GROUP_WIDTH), lambda r, j: (r, jnp.maximum(j - 1, 0), 0))
    return pl.pallas_call(
        _ordered_after(body, 8, after), name=name, grid=(dil, nj + 1),
        in_specs=[cur, prev, cur, prev, cur, cur, cur, cur] + [pl.BlockSpec(memory_space=pl.ANY)] * len(after),
        out_specs=[cur, late, late],
        out_shape=[jax.ShapeDtypeStruct(q.shape, F32)] * 3,
        scratch_shapes=[pltpu.VMEM((qb * ATTN_BLOCK, GROUP_WIDTH), F32)] * 2,
        compiler_params=_params(2),
    )(q, k, k, v, v, do, lse, deff, *after)


def _pool_bwd(dpool, y, w_bd, scale, name, after=()):
    s = dpool.shape[0]
    t = _row_tile(s, 512)
    nt = s // t

    def body(dp_ref, y_ref, w_ref, sc_ref, du_ref, dw_ref, dsc_ref, ext):
        i = pl.program_id(0)

        @pl.when(i == 0)
        def _():
            ext[t:, :] = jnp.zeros((POOL_HALO, POOL_WIDTH), F32)
            dw_ref[...] = jnp.zeros_like(dw_ref)
            dsc_ref[...] = jnp.zeros_like(dsc_ref)

        dp = dp_ref[...]
        yb = y_ref[...]
        w = w_ref[...]
        dsc_ref[...] += jnp.sum(dp * _dot(yb, w), axis=0, keepdims=True)
        dyo = (dp * sc_ref[...]).astype(BF16)
        dw_ref[...] += _dot_tn(yb, dyo)
        dy = _dot_nt(dyo, w)
        win = _pool_lane_window()
        pos = (nt - 1 - i) * t + lax.broadcasted_iota(jnp.int32, (t, POOL_WIDTH), 0)
        gq = dy / jnp.minimum(pos + 1, win).astype(F32)
        ext[0:t, :] = gq
        acc = gq
        wsum = jnp.zeros_like(gq)
        for k in range(1, POOL_HALO):
            acc = acc + ext[k:k + t, :]
            if k + 1 in POOL_WINDOWS:
                wsum = jnp.where(win == k + 1, acc, wsum)
        du_ref[...] = wsum - dy
        ext[t:, :] = gq[0:POOL_HALO, :]

    rev = pl.BlockSpec((t, POOL_WIDTH), lambda i: (nt - 1 - i, 0))
    full = lambda a, b: pl.BlockSpec((a, b), lambda i: (0, 0))
    return pl.pallas_call(
        _ordered_after(body, 4, after), name=name, grid=(nt,),
        in_specs=[rev, rev, full(POOL_WIDTH, POOL_WIDTH), full(1, POOL_WIDTH)]
        + [pl.BlockSpec(memory_space=pl.ANY)] * len(after),
        out_specs=[rev, full(POOL_WIDTH, POOL_WIDTH), full(1, POOL_WIDTH)],
        out_shape=[jax.ShapeDtypeStruct((s, POOL_WIDTH), F32), jax.ShapeDtypeStruct((POOL_WIDTH, POOL_WIDTH), F32),
                   jax.ShapeDtypeStruct((1, POOL_WIDTH), F32)],
        scratch_shapes=[pltpu.VMEM((t + POOL_HALO, POOL_WIDTH), F32)],
        compiler_params=_params(1),
    )(dpool, y, w_bd, scale, *after)


def _normproj_bwd(dh, du, dq, dk, dv, rc, rsa, rsb, w_in, h, g, name):
    s = h.shape[0]
    t = _row_tile(s, 512)

    def body(dh_ref, du_ref, q0, q1, q2, k0, k1, k2, v0, v1, v2, c_ref, sa_ref, sb_ref, w_ref, h_ref, g_ref,
             out_ref, dz_ref, dg_ref, *stages):
        @pl.when(pl.program_id(0) == 0)
        def _():
            dg_ref[...] = jnp.zeros_like(dg_ref)

        c, sa, sb = c_ref[...], sa_ref[...], sb_ref[...]

        def unrot(a, scale):
            halves = [_rot_t(a[:, hf * LANES:(hf + 1) * LANES] * scale, c, sa, sb) for hf in range(2)]
            return jnp.concatenate(halves, axis=1)

        staged = _pair_stages(stages)
        tok = lambda refs, base: [_from_residues(r, staged[base + i], DILATIONS[i]) for i, r in enumerate(refs)]
        chunks = [du_ref[...]]
        chunks += [unrot(a, HEAD_DIM ** -0.5) for a in tok((q0, q1, q2), 0)]
        chunks += [unrot(a, 1.0) for a in tok((k0, k1, k2), 3)]
        chunks += tok((v0, v1, v2), 6)
        acc = jnp.zeros((t, D_MODEL), F32)
        for ci, ch in enumerate(chunks):
            cols = slice(ci * GROUP_WIDTH, (ci + 1) * GROUP_WIDTH)
            cb = ch.astype(BF16)
            dz_ref[:, cols] = cb
            acc = acc + _dot_nt(cb, w_ref[:, cols])
        gv = g_ref[...]
        n, rstd, _ = _rms(h_ref[...], gv)
        dx, dg = _rms_bwd(acc, n, rstd, gv)
        out_ref[...] = dh_ref[...] + dx
        dg_ref[...] += dg

    row = lambda w: pl.BlockSpec((t, w), lambda i: (i, 0))
    vec = pl.BlockSpec((1, D_MODEL), lambda i: (0, 0))
    res = [_residue_spec(dil, t) for dil in DILATIONS]
    return pl.pallas_call(
        body, name=name, grid=(s // t,),
        in_specs=[row(D_MODEL), row(POOL_WIDTH)] + res * 3 + [row(LANES)] * 3
        + [pl.BlockSpec((D_MODEL, N_IN), lambda i: (0, 0)), row(D_MODEL), vec],
        out_specs=[row(D_MODEL), row(N_IN), vec],
        out_shape=[jax.ShapeDtypeStruct((s, D_MODEL), F32), jax.ShapeDtypeStruct((s, N_IN), BF16),
                   jax.ShapeDtypeStruct((1, D_MODEL), F32)],
        scratch_shapes=_stages(t, 9),
        compiler_params=_params(1),
    )(dh, du, *dq, *dk, *dv, rc, rsa, rsb, w_in, h, g)


def _matmul_tn(a, b, name, *, square_a=False, tn=None, blocked_out=False):
    s, m = a.shape
    n = b.shape[1]
    tk = _row_tile(s, 2048)
    tm = min(m, 512)
    tn = tn or min(n, 1024)
    assert m % tm == 0 and n % tn == 0
    nk = s // tk

    def body(a_ref, b_ref, o_ref, ob_ref, acc):
        k = pl.program_id(2)

        @pl.when(k == 0)
        def _():
            acc[...] = jnp.zeros_like(acc)

        av = a_ref[...]
        if square_a:
            av = av.astype(F32)
            av = av * av
        acc[...] += _dot_tn(av.astype(BF16), b_ref[...].astype(BF16))

        @pl.when(k == nk - 1)
        def _():
            o_ref[...] = acc[...]
            ob_ref[...] = acc[...].astype(BF16)

    if blocked_out:
        shape = (n // tn, m, tn)
        out_spec = pl.BlockSpec((None, tm, tn), lambda i, j, k: (j, i, 0))
    else:
        shape = (m, n)
        out_spec = pl.BlockSpec((tm, tn), lambda i, j, k: (i, j))
    return pl.pallas_call(
        body, name=name, grid=(m // tm, n // tn, nk),
        in_specs=[pl.BlockSpec((tk, tm), lambda i, j, k: (k, i)), pl.BlockSpec((tk, tn), lambda i, j, k: (k, j))],
        out_specs=[out_spec, out_spec],
        out_shape=[jax.ShapeDtypeStruct(shape, F32), jax.ShapeDtypeStruct(shape, BF16)],
        scratch_shapes=[pltpu.VMEM((tm, tn), F32)],
        compiler_params=_params(3),
    )(a, b)


def _adamw_math(w, g, m, v):
    m = ADAM_B1 * m + (1.0 - ADAM_B1) * g
    v = ADAM_B2 * v + (1.0 - ADAM_B2) * (g * g)
    m_hat = m / (1.0 - ADAM_B1 ** ADAM_STEP)
    v_hat = v / (1.0 - ADAM_B2 ** ADAM_STEP)
    delta = -ADAM_LR * (m_hat / (jnp.sqrt(v_hat) + ADAM_EPS) + ADAM_WD * w)
    return delta, m, v


def _adamw_sharded(w, m, v, own, recv0, recv1, name):
    _, rows, cols = w.shape
    t = _row_tile(rows, 256)

    def body(w_ref, m_ref, v_ref, own_ref, r0_ref, r1_ref, g_ref, d_ref, nm_ref, nv_ref):
        layer0 = pl.program_id(0) == 0
        g = own_ref[...]
        for k in range(N_DEV - 1):
            g = g + jnp.where(layer0, r0_ref[k], r1_ref[k]).astype(F32)
        g_ref[...] = g
        d_ref[...], nm_ref[...], nv_ref[...] = _adamw_math(w_ref[...], g, m_ref[...], v_ref[...])

    blk = pl.BlockSpec((None, t, cols), lambda l, i: (l, i, 0))
    recv = lambda layer: pl.BlockSpec((N_DEV - 1, t, cols), lambda l, i: (0, jnp.where(l == layer, i, 0), 0))
    return pl.pallas_call(
        body, name=name, grid=(2, rows // t),
        in_specs=[blk, blk, blk, blk, recv(0), recv(1)], out_specs=[blk] * 4,
        out_shape=[jax.ShapeDtypeStruct(w.shape, F32)] * 4,
        compiler_params=_params(2),
    )(w, m, v, own, recv0, recv1)


def _adamw_packed(w, g, m, v, name):
    def body(w_ref, g_ref, m_ref, v_ref, d_ref, nm_ref, nv_ref):
        d_ref[...], nm_ref[...], nv_ref[...] = _adamw_math(w_ref[...], g_ref[...], m_ref[...], v_ref[...])

    return pl.pallas_call(
        body, name=name, out_shape=[jax.ShapeDtypeStruct(w.shape, F32)] * 3,
        compiler_params=pltpu.CompilerParams(vmem_limit_bytes=VMEM_LIMIT),
    )(w, g, m, v)


def _peer(k):
    x, y, c = lax.axis_index("x"), lax.axis_index("y"), lax.axis_index("c")
    return (1 - x if k & 4 else x, 1 - y if k & 2 else y, 1 - c if k & 1 else c)


def _linear(dev):
    return 4 * dev[0] + 2 * dev[1] + dev[2]


HBM_SPEC = pl.BlockSpec(memory_space=pltpu.HBM)
SEM_SPEC = pl.BlockSpec(memory_space=pltpu.SEMAPHORE)
ANY_SPEC = pl.BlockSpec(memory_space=pl.ANY)
EFFECT = pltpu.SideEffectType.DATAFLOW_SIDE_EFFECTING


def _in_hbm(a):
    return pltpu.with_memory_space_constraint(a, pltpu.HBM)


class _Exchange:
    def __init__(self, name, groups, scatter):
        self.name, self.scatter = name, scatter
        self.sizes = sizes = [len(g) for g in groups]
        srcs = [a for g in groups for a in g]
        n, ng = len(srcs), len(groups)
        lead = (N_DEV - 1,) if scatter else (N_DEV,)
        shapes = [lead + (a.shape[1:] if scatter else a.shape) for a in srcs]
        lands = [lax.empty(sh, a.dtype) for sh, a in zip(shapes, srcs)]
        offsets = [sum(sizes[:gi]) for gi in range(ng)]
        copy = self._copy

        def body(*refs):
            src, land = refs[:n], refs[n:2 * n]
            sems = refs[2 * n:2 * n + 2 * ng]
            token, local_sems = refs[-2], refs[-1]
            me = _linear(_peer(0))
            local = []
            if not scatter:
                local = [pltpu.make_async_copy(src[w], land[w].at[me], local_sems.at[w]) for w in range(n)]
            for cp in local:
                cp.start()
            for gi in range(ng):
                for wi in range(sizes[gi]):
                    w = offsets[gi] + wi
                    for k in range(1, N_DEV):
                        copy(src[w], land[w], sems[2 * gi], sems[2 * gi + 1], wi, k).start()
            for cp in local:
                cp.wait()
            token[...] = jnp.zeros_like(token)

        sem_shapes = [pltpu.SemaphoreType.DMA((7 * sz,)) for sz in sizes for _ in range(2)]
        outs = pl.pallas_call(
            body, name=name + "_start",
            in_specs=[HBM_SPEC] * (2 * n),
            out_specs=[SEM_SPEC] * (2 * ng) + [HBM_SPEC] * (2 * n) + [pl.BlockSpec(memory_space=pltpu.VMEM)],
            out_shape=sem_shapes + [pltpu.HBM(a.shape, a.dtype) for a in srcs + lands]
            + [jax.ShapeDtypeStruct((8, LANES), F32)],
            input_output_aliases={i: 2 * ng + i for i in range(2 * n)},
            scratch_shapes=[pltpu.SemaphoreType.DMA((n,))],
            compiler_params=pltpu.CompilerParams(has_side_effects=EFFECT),
        )(*[_in_hbm(a) for a in srcs + lands])
        self.sems = [outs[2 * gi:2 * gi + 2] for gi in range(ng)]
        thru = outs[2 * ng:2 * ng + 2 * n]
        self.srcs = [thru[offsets[gi]:offsets[gi] + sizes[gi]] for gi in range(ng)]
        self.lands = [thru[n + offsets[gi]:n + offsets[gi] + sizes[gi]] for gi in range(ng)]
        self.token = outs[-1]

    def _copy(self, src, land, send_sems, recv_sems, wi, k):
        to = _peer(k)
        if self.scatter:
            src_ref, dst_ref = src.at[_linear(to)], land.at[k - 1]
        else:
            src_ref, dst_ref = src, land.at[_linear(_peer(0))]
        return pltpu.make_async_remote_copy(
            src_ref=src_ref, dst_ref=dst_ref, send_sem=send_sems.at[7 * wi + k - 1],
            recv_sem=recv_sems.at[7 * wi + k - 1], device_id=to, device_id_type=MESH)

    def wait(self, gi, after):
        n = self.sizes[gi]
        copy = self._copy

        def body(*refs):
            src, land = refs[:n], refs[n:2 * n]
            send_sems, recv_sems = refs[2 * n], refs[2 * n + 1]
            for wi in range(n):
                for k in range(1, N_DEV):
                    cp = copy(src[wi], land[wi], send_sems, recv_sems, wi, k)
                    cp.wait_send()
                    cp.wait_recv()

        arrays = list(self.srcs[gi]) + list(self.lands[gi])
        outs = pl.pallas_call(
            body, name=f"{self.name}_wait{gi}",
            in_specs=[HBM_SPEC] * (2 * n) + [SEM_SPEC, SEM_SPEC] + [ANY_SPEC] * len(after),
            out_specs=[HBM_SPEC] * (2 * n),
            out_shape=[pltpu.HBM(a.shape, a.dtype) for a in arrays],
            input_output_aliases={i: i for i in range(2 * n)},
            compiler_params=pltpu.CompilerParams(has_side_effects=EFFECT),
        )(*arrays, *self.sems[gi], *after)
        return outs[n:]


def _allreduce_packed(g):
    rows = g.shape[0]

    def body(g_ref, out_ref, buf, send_sems, recv_sems):
        me = _linear(_peer(0))
        buf[me] = g_ref[...]
        copies = []
        for k in range(1, N_DEV):
            copies.append(pltpu.make_async_remote_copy(
                src_ref=g_ref, dst_ref=buf.at[me], send_sem=send_sems.at[k - 1], recv_sem=recv_sems.at[k - 1],
                device_id=_peer(k), device_id_type=MESH))
        for cp in copies:
            cp.start()
        for k in range(1, N_DEV):
            pltpu.make_async_remote_copy(
                src_ref=g_ref, dst_ref=buf.at[_linear(_peer(k))], send_sem=send_sems.at[k - 1],
                recv_sem=recv_sems.at[k - 1], device_id=_peer(k), device_id_type=MESH).wait_recv()
        for cp in copies:
            cp.wait_send()
        total = buf[0]
        for d in range(1, N_DEV):
            total = total + buf[d]
        out_ref[...] = total

    return pl.pallas_call(
        body, name="allreduce_small",
        in_specs=[pl.BlockSpec(memory_space=pltpu.VMEM)], out_specs=pl.BlockSpec(memory_space=pltpu.VMEM),
        out_shape=jax.ShapeDtypeStruct(g.shape, F32),
        scratch_shapes=[pltpu.VMEM((N_DEV, rows, g.shape[1]), F32), pltpu.SemaphoreType.DMA((7,)),
                        pltpu.SemaphoreType.DMA((7,))],
        compiler_params=pltpu.CompilerParams(vmem_limit_bytes=VMEM_LIMIT),
    )(g)


def _rotary_tables(positions):
    rot_dim = HEAD_DIM // 4
    inv_freq = ROPE_THETA ** (-jnp.arange(0, rot_dim, 2, dtype=F32) / rot_dim)
    dim = jnp.arange(LANES) % HEAD_DIM
    ang = positions.astype(F32)[:, None] * inv_freq[dim % ROT_SHIFT][None, :]
    cos, sin = jnp.cos(ang), jnp.sin(ang)
    first, second = dim < ROT_SHIFT, (dim >= ROT_SHIFT) & (dim < rot_dim)
    c = jnp.where(first | second, cos, 1.0)
    sa = jnp.where(second, sin, 0.0)
    sb = jnp.where(first, -sin, 0.0)
    return [c, sa, sb]


def _block_diag(pool_w):
    gc = pool_w.shape[-1]
    out = jnp.zeros((POOL_WIDTH, POOL_WIDTH), pool_w.dtype)
    for grp in range(pool_w.shape[0]):
        out = lax.dynamic_update_slice(out, pool_w[grp], (grp * gc, grp * gc))
    return out


def _diag_blocks(a):
    gc = POOL_WIDTH // len(POOL_WINDOWS)
    return jnp.stack([a[grp * gc:(grp + 1) * gc, grp * gc:(grp + 1) * gc] for grp in range(len(POOL_WINDOWS))])


def _local_step(x, p, positions, loss_target, norm1, pool_w, pool_scale, norm2, norm3, final_norm, weights, send):
    rc, rsa, rsb = _rotary_tables(positions)
    ones_bd = _block_diag(jnp.ones((4, HEAD_DIM, HEAD_DIM), BF16))
    saved = []
    h = x
    for i in range(2):
        tag = f"_l{i}"
        g1, g2, g3 = norm1[i:i + 1], norm2[i:i + 1], norm3[i:i + 1]
        w_bd = _block_diag(pool_w[i]).astype(BF16)
        scale = pool_scale[i:i + 1]
        w_in = weights(i, "in", (h,))
        hn1, u, *qkv = _normproj_fwd(h, g1, w_in, rc, rsa, rsb, "normproj_fwd" + tag)
        qkv = [qkv[3 * grp:3 * grp + 3] for grp in range(3)]
        pool_out, y = _pool_fwd(u, w_bd, scale, "pool_fwd" + tag)
        o, lse = zip(*[_attn_fwd(*qkv[grp], f"attn_fwd{tag}_g{grp}") for grp in range(3)])
        w_out = weights(i, "out", (pool_out, *o))
        h1, a = _outproj_fwd(h, pool_out, o, lse, w_out, "outproj_fwd" + tag)
        w_up, w_down, w_gate, w_ple = weights(i, "rest", (h1,))
        h2, hn2, r = _mlp_fwd(h1, g2, w_up, w_down, "mlp_fwd" + tag)
        h3, hn3, gate, e = _gate_fwd(h2, g3, w_gate, p[i], w_ple, "gate_fwd" + tag)
        saved.append(dict(h0=h, hn1=hn1, qkv=qkv, y=y, o=o, lse=lse, a=a, h1=h1, hn2=hn2, r=r, h2=h2,
                          hn3=hn3, gate=gate, e=e, w_bd=w_bd, scale=scale, g1=g1, g2=g2, g3=g3,
                          w_in=w_in, w_out=w_out, w_up=w_up, w_down=w_down, w_gate=w_gate))
        h = h3
    loss, dh, d_final = _loss_head(h, final_norm.reshape(1, D_MODEL), loss_target, "loss_head")

    grads = [None, None]
    sent = ()
    for i in (1, 0):
        tag = f"_l{i}"
        sv = saved[i]
        dh2, dgl, de, dg3 = _gate_bwd(dh, sv["gate"], sv["e"], sv["h2"], sv["g3"], sv["w_gate"], "gate_bwd" + tag,
                                      after=sent)
        dw_gate = _matmul_tn(sv["hn3"], dgl, "dw_gate" + tag)
        dw_ple = _matmul_tn(p[i], de, "dw_ple" + tag)
        dh1, dup, dg2 = _mlp_bwd(dh2, sv["r"], sv["h1"], sv["g2"], sv["w_up"], sv["w_down"], "mlp_bwd" + tag)
        dw_down = _matmul_tn(sv["r"], dh2, "dw_down" + tag, square_a=True)
        dw_up = _matmul_tn(sv["hn2"], dup, "dw_up" + tag, tn=FF_BLOCK, blocked_out=True)
        dpool, do0, do1, do2, de0, de1, de2 = _outproj_bwd(dh1, sv["w_out"], sv["o"], sv["lse"], ones_bd,
                                                           "outproj_bwd" + tag)
        dw_out = _matmul_tn(sv["a"], dh1, "dw_out" + tag)
        sent = send(i, "main", dict(w_gate=dw_gate, w_ple=dw_ple, w_down=dw_down, w_up=dw_up, w_out=dw_out))
        dqkv = [_attn_bwd(*sv["qkv"][grp], do_g, sv["lse"][grp], de_g, f"attn_bwd{tag}_g{grp}", after=sent)
                for grp, (do_g, de_g) in enumerate(((do0, de0), (do1, de1), (do2, de2)))]
        dq, dk, dv = zip(*dqkv)
        du, dw_bd, dscale = _pool_bwd(dpool, sv["y"], sv["w_bd"], sv["scale"], "pool_bwd" + tag, after=sent)
        dh, dz, dg1 = _normproj_bwd(dh1, du, dq, dk, dv, rc, rsa, rsb, sv["w_in"], sv["h0"], sv["g1"],
                                    "normproj_bwd" + tag)
        dw_in = _matmul_tn(sv["hn1"], dz, "dw_in" + tag, tn=512)
        sent = send(i, "in", dict(w_in=dw_in))
        grads[i] = dict(norm1=dg1, norm2=dg2, norm3=dg3, pool_w=_diag_blocks(dw_bd), pool_scale=dscale)
    return loss, dh, grads, d_final, sent


def _pack_small(norm1, norm2, norm3, final_norm, pool_scale, pool_w):
    scale_row = jnp.concatenate([pool_scale.reshape(1, 2 * POOL_WIDTH), jnp.zeros((1, D_MODEL - 2 * POOL_WIDTH), F32)], axis=1)
    return jnp.concatenate([norm1, norm2, norm3, final_norm.reshape(1, D_MODEL), scale_row,
                            pool_w.reshape(32, D_MODEL)], axis=0)


def _unpack_small(a):
    return dict(norm1=a[0:2], norm2=a[2:4], norm3=a[4:6], final_norm=a[6], pool_scale=a[7, 0:2 * POOL_WIDTH].reshape(2, POOL_WIDTH),
                pool_w=a[8:40].reshape(2, 4, HEAD_DIM, HEAD_DIM))


def _chunks_cols(a, cols):
    return a.reshape(a.shape[0], N_DEV, cols).transpose(1, 0, 2)


def _chunks_rows(a, rows):
    return a.reshape(N_DEV, rows, a.shape[1])


BIG = ("w_in", "w_out", "w_up", "w_down", "w_gate", "w_ple")
SMALL = ("norm1", "norm2", "norm3", "final_norm", "pool_scale", "pool_w")
ORDER = ("norm1", "w_in", "pool_w", "pool_scale", "w_out", "norm2", "w_up", "w_down", "norm3", "w_gate", "w_ple",
         "final_norm")


def kernel(x, p, positions, norm1, w_in, pool_w, pool_scale, w_out, norm2, w_up, w_down, norm3, w_gate, w_ple, final_norm, loss_target, m_norm1, m_w_in, m_pool_w, m_pool_scale, m_w_out, m_norm2, m_w_up, m_w_down, m_norm3, m_w_gate, m_w_ple, m_final_norm, v_norm1, v_w_in, v_pool_w, v_pool_scale, v_w_out, v_norm2, v_w_up, v_w_down, v_norm3, v_w_gate, v_w_ple, v_final_norm):
    w = dict(norm1=norm1, w_in=w_in, pool_w=pool_w, pool_scale=pool_scale, w_out=w_out, norm2=norm2, w_up=w_up,
             w_down=w_down, norm3=norm3, w_gate=w_gate, w_ple=w_ple, final_norm=final_norm)
    m = dict(norm1=m_norm1, w_in=m_w_in, pool_w=m_pool_w, pool_scale=m_pool_scale, w_out=m_w_out, norm2=m_norm2,
             w_up=m_w_up, w_down=m_w_down, norm3=m_norm3, w_gate=m_w_gate, w_ple=m_w_ple, final_norm=m_final_norm)
    v = dict(norm1=v_norm1, w_in=v_w_in, pool_w=v_pool_w, pool_scale=v_pool_scale, w_out=v_w_out, norm2=v_norm2,
             w_up=v_w_up, w_down=v_w_down, norm3=v_norm3, w_gate=v_w_gate, w_ple=v_w_ple, final_norm=v_final_norm)
    seq = x.shape[1]

    bf = {n: [w[n][layer].astype(BF16) for layer in range(2)] for n in BIG}
    rest = ("w_up", "w_down", "w_gate", "w_ple")
    gather = _Exchange("gather_weights", [[bf["w_in"][0]], [bf["w_out"][0]], [bf[n][0] for n in rest],
                                          [bf[n][1] for n in BIG]], scatter=False)
    unpack = dict(w_in=lambda a: a.transpose(1, 0, 2).reshape(D_MODEL, N_IN),
                  w_out=lambda a: a.reshape(D_MODEL, D_MODEL), w_gate=lambda a: a.reshape(D_MODEL, D_MODEL),
                  w_ple=lambda a: a.transpose(1, 0, 2).reshape(PLE_DIM, D_MODEL), w_up=lambda a: a, w_down=lambda a: a)
    layer1 = {}

    def weights(layer, part, after):
        names = dict(out=("w_out",), rest=rest)[part] if part != "in" else ("w_in",)
        if layer == 0:
            got = dict(zip(names, gather.wait(("in", "out", "rest").index(part), after)))
        else:
            if not layer1:
                layer1.update(zip(BIG, gather.wait(3, after)))
            got = layer1
        full = [unpack[n](got[n]) for n in names]
        return full if part == "rest" else full[0]

    me = 4 * lax.axis_index("x") + 2 * lax.axis_index("y") + lax.axis_index("c")
    to_chunks = dict(w_in=lambda a: _chunks_cols(a, N_IN // N_DEV), w_out=lambda a: _chunks_rows(a, D_MODEL // N_DEV),
                     w_up=lambda a: a, w_down=lambda a: _chunks_rows(a, FF_BLOCK),
                     w_gate=lambda a: _chunks_rows(a, D_MODEL // N_DEV), w_ple=lambda a: _chunks_cols(a, D_MODEL // N_DEV))
    own = {n: [None, None] for n in BIG}
    scatters = {}

    def send(layer, part, grads):
        for n, (g32, _) in grads.items():
            own[n][layer] = lax.dynamic_index_in_dim(to_chunks[n](g32), me, axis=0, keepdims=False)
        ex = _Exchange(f"scatter_{part}_l{layer}", [[to_chunks[n](g16) for n, (_, g16) in grads.items()]], scatter=True)
        scatters[layer, part] = (tuple(grads), ex)
        return (ex.token,)

    loss, dx, grads, d_final, sent = _local_step(
        x.reshape(seq, D_MODEL), p.reshape(2, seq, PLE_DIM), positions.reshape(seq), loss_target.reshape(seq, D_MODEL),
        norm1, pool_w, pool_scale, norm2, norm3, final_norm, weights, send)

    small_g = _pack_small(
        *[jnp.concatenate([grads[0][n], grads[1][n]], axis=0) for n in ("norm1", "norm2", "norm3")], d_final.reshape(D_MODEL),
        jnp.concatenate([grads[0]["pool_scale"], grads[1]["pool_scale"]], axis=0),
        jnp.stack([grads[0]["pool_w"], grads[1]["pool_w"]]))
    small_g = _allreduce_packed(small_g)

    g_out, d_out, m_out, v_out = {}, {}, {}, {}
    for part in ("main", "in"):
        recv = {}
        for layer in (1, 0):
            names, ex = scatters[layer, part]
            for n, r in zip(names, ex.wait(0, sent)):
                recv[n, layer] = r
        for n in names:
            g_out[n], d_out[n], m_out[n], v_out[n] = _adamw_sharded(
                w[n], m[n], v[n], jnp.stack(own[n]), recv[n, 0], recv[n, 1], "adamw_" + n)
        sent = tuple(d_out[n] for n in names)
    pack = lambda t: _pack_small(*[t[n] for n in SMALL])
    d_small, m_small, v_small = _adamw_packed(pack(w), small_g, pack(m), pack(v), "adamw_small")
    for dst, a in ((g_out, small_g), (d_out, d_small), (m_out, m_small), (v_out, v_small)):
        dst.update(_unpack_small(a))

    loss = lax.psum(loss[0, 0], ("x", "y", "c"))
    return (loss, dx.reshape(1, seq, D_MODEL), *[g_out[n] for n in ORDER], *[d_out[n] for n in ORDER],
            *[m_out[n] for n in ORDER], *[v_out[n] for n in ORDER])
```

```python
import functools

import jax
import jax.numpy as jnp
from jax import lax
from jax.experimental import pallas as pl
from jax.experimental.pallas import tpu as pltpu

F32 = jnp.float32
BF16 = jnp.bfloat16

D_MODEL = 1024
HEAD_DIM = 64
POOL_WIDTH = 256
POOL_WINDOWS = (2, 4, 8, 16)
POOL_HALO = 16
GROUP_WIDTH = 256
DILATIONS = (1, 4, 16)
ATTN_BLOCK = 128
ROT_SHIFT = 8
ROPE_THETA = 500000.0
D_FF = 4096
FF_BLOCK = 512
N_DEV = 8
N_IN = POOL_WIDTH + 3 * 768
PLE_DIM = 256
EPS = 1e-6
NEG_BIG = -1e30

ADAM_LR = 0.001
ADAM_B1 = 0.9
ADAM_B2 = 0.999
ADAM_EPS = 1e-08
ADAM_WD = 0.01
ADAM_STEP = 10

LANES = 128
VMEM_LIMIT = 56 * 1024 * 1024
MESH = pl.DeviceIdType.MESH


def _params(n_grid):
    return pltpu.CompilerParams(dimension_semantics=("arbitrary",) * n_grid, vmem_limit_bytes=VMEM_LIMIT)


def _dot(a, b):
    return jnp.dot(a, b, preferred_element_type=F32)


def _dot_nt(a, b):
    return lax.dot_general(a, b, (((1,), (1,)), ((), ())), preferred_element_type=F32)


def _dot_tn(a, b):
    return lax.dot_general(a, b, (((0,), (0,)), ((), ())), preferred_element_type=F32)


def _rms(x, g):
    rstd = lax.rsqrt(jnp.mean(x * x, axis=-1, keepdims=True) + EPS)
    n = x * rstd
    return n, rstd, n * g


def _rms_bwd(dy, n, rstd, g):
    dyn = dy * g
    dx = rstd * (dyn - n * jnp.mean(dyn * n, axis=-1, keepdims=True))
    return dx, jnp.sum(dy * n, axis=0, keepdims=True)


def _ordered_after(body, n_in, after):
    if not after:
        return body
    return lambda *refs: body(*refs[:n_in], *refs[n_in + len(after):])


def _row_tile(s, t):
    t = min(s, t)
    assert s % t == 0
    return t


def _rot(z, c, sa, sb):
    return z * c + pltpu.roll(z, ROT_SHIFT, 1) * sa + pltpu.roll(z, LANES - ROT_SHIFT, 1) * sb


def _rot_t(dz, c, sa, sb):
    return dz * c + pltpu.roll(dz * sa, LANES - ROT_SHIFT, 1) + pltpu.roll(dz * sb, ROT_SHIFT, 1)


def _to_residues(value, stage, out_ref, dil):
    if dil == 1:
        out_ref[0] = value.astype(out_ref.dtype)
        return
    rows = value.shape[0] // dil
    for hf in range(GROUP_WIDTH // LANES):
        lanes = slice(hf * LANES, (hf + 1) * LANES)
        stage[hf][...] = value[:, lanes]
        for r in range(dil):
            out_ref[r, :, lanes] = stage[hf][pl.ds(r, rows, stride=dil), :].astype(out_ref.dtype)


def _from_residues(in_ref, stage, dil):
    if dil == 1:
        return in_ref[0]
    rows = in_ref.shape[1]
    for hf in range(GROUP_WIDTH // LANES):
        for r in range(dil):
            stage[hf][pl.ds(r, rows, stride=dil), :] = in_ref[r, :, hf * LANES:(hf + 1) * LANES]
    return jnp.concatenate([stage[0][...], stage[1][...]], axis=1)


def _residue_spec(dil, t):
    return pl.BlockSpec((dil, t // dil, GROUP_WIDTH), lambda i: (0, i, 0))


def _residue_shape(dil, s, dtype):
    return jax.ShapeDtypeStruct((dil, s // dil, GROUP_WIDTH), dtype)


def _stages(t, n):
    return [pltpu.VMEM((t, LANES), F32)] * (n * (GROUP_WIDTH // LANES))


def _pair_stages(refs):
    return [refs[i:i + 2] for i in range(0, len(refs), 2)]


def _normproj_fwd(h, g, w_in, rc, rsa, rsb, name):
    s = h.shape[0]
    t = _row_tile(s, 512)

    def body(h_ref, g_ref, w_ref, c_ref, sa_ref, sb_ref, hn_ref, u_ref, *rest):
        qkv_refs, stages = rest[:9], _pair_stages(rest[9:])
        _, _, hn = _rms(h_ref[...], g_ref[...])
        hb = hn.astype(BF16)
        hn_ref[...] = hb
        c, sa, sb = c_ref[...], sa_ref[...], sb_ref[...]

        def rot(z, scale):
            halves = [_rot(z[:, hf * LANES:(hf + 1) * LANES], c, sa, sb) * scale for hf in range(2)]
            return jnp.concatenate(halves, axis=1)

        u_ref[...] = _dot(hb, w_ref[:, 0:POOL_WIDTH])
        for grp, dil in enumerate(DILATIONS):
            lo = POOL_WIDTH + grp * GROUP_WIDTH
            q_ref, k_ref, v_ref = qkv_refs[3 * grp:3 * grp + 3]
            _to_residues(rot(_dot(hb, w_ref[:, lo:lo + GROUP_WIDTH]), HEAD_DIM ** -0.5), stages[0], q_ref, dil)
            _to_residues(rot(_dot(hb, w_ref[:, lo + 768:lo + 768 + GROUP_WIDTH]), 1.0), stages[1], k_ref, dil)
            _to_residues(_dot(hb, w_ref[:, lo + 1536:lo + 1536 + GROUP_WIDTH]), stages[2], v_ref, dil)

    row = lambda w: pl.BlockSpec((t, w), lambda i: (i, 0))
    return pl.pallas_call(
        body, name=name, grid=(s // t,),
        in_specs=[row(D_MODEL), pl.BlockSpec((1, D_MODEL), lambda i: (0, 0)),
                  pl.BlockSpec((D_MODEL, N_IN), lambda i: (0, 0)), row(LANES), row(LANES), row(LANES)],
        out_specs=[row(D_MODEL), row(POOL_WIDTH)] + [_residue_spec(dil, t) for dil in DILATIONS for _ in range(3)],
        out_shape=[jax.ShapeDtypeStruct((s, D_MODEL), BF16), jax.ShapeDtypeStruct((s, POOL_WIDTH), F32)]
        + [_residue_shape(dil, s, BF16) for dil in DILATIONS for _ in range(3)],
        scratch_shapes=_stages(t, 3),
        compiler_params=_params(1),
    )(h, g, w_in, rc, rsa, rsb)


def _pool_lane_window():
    lane = lax.broadcasted_iota(jnp.int32, (1, POOL_WIDTH), 1)
    return jnp.left_shift(2, lane // (POOL_WIDTH // len(POOL_WINDOWS)))


def _pool_fwd(u, w_bd, scale, name):
    s = u.shape[0]
    t = _row_tile(s, 512)

    def body(u_ref, w_ref, sc_ref, out_ref, y_ref, ext):
        i = pl.program_id(0)

        @pl.when(i == 0)
        def _():
            ext[0:POOL_HALO, :] = jnp.zeros((POOL_HALO, POOL_WIDTH), F32)

        x = u_ref[...]
        ext[POOL_HALO:, :] = x
        win = _pool_lane_window()
        acc = x
        wsum = jnp.zeros_like(x)
        for k in range(1, POOL_HALO):
            acc = acc + ext[POOL_HALO - k:POOL_HALO - k + t, :]
            if k + 1 in POOL_WINDOWS:
                wsum = jnp.where(win == k + 1, acc, wsum)
        pos = i * t + lax.broadcasted_iota(jnp.int32, (t, POOL_WIDTH), 0)
        cnt = jnp.minimum(pos + 1, win).astype(F32)
        y = wsum / cnt - x
        yb = y.astype(BF16)
        y_ref[...] = yb
        out_ref[...] = _dot(yb, w_ref[...]) * sc_ref[...]
        ext[0:POOL_HALO, :] = x[t - POOL_HALO:, :]

    row = pl.BlockSpec((t, POOL_WIDTH), lambda i: (i, 0))
    return pl.pallas_call(
        body, name=name, grid=(s // t,),
        in_specs=[row, pl.BlockSpec((POOL_WIDTH, POOL_WIDTH), lambda i: (0, 0)),
                  pl.BlockSpec((1, POOL_WIDTH), lambda i: (0, 0))],
        out_specs=[row, row],
        out_shape=[jax.ShapeDtypeStruct((s, POOL_WIDTH), F32), jax.ShapeDtypeStruct((s, POOL_WIDTH), BF16)],
        scratch_shapes=[pltpu.VMEM((t + POOL_HALO, POOL_WIDTH), F32)],
        compiler_params=_params(1),
    )(u, w_bd, scale)


def _head_masks():
    lane = lax.broadcasted_iota(jnp.int32, (ATTN_BLOCK, GROUP_WIDTH), 1)
    return [lane // HEAD_DIM == hd for hd in range(GROUP_WIDTH // HEAD_DIM)]


def _stack_heads(a, masks):
    zero = jnp.zeros_like(a)
    return jnp.concatenate([jnp.where(m, a, zero) for m in masks], axis=0)


def _band_mask(first_block):
    rows = ATTN_BLOCK * (GROUP_WIDTH // HEAD_DIM)
    i = lax.broadcasted_iota(jnp.int32, (rows, 2 * ATTN_BLOCK), 0) & (ATTN_BLOCK - 1)
    j = lax.broadcasted_iota(jnp.int32, (rows, 2 * ATTN_BLOCK), 1)
    return (j >= i) & (j <= i + ATTN_BLOCK) & ((j >= ATTN_BLOCK) | jnp.logical_not(first_block))


def _column_per_head(a):
    return jnp.concatenate([a[:, hd * HEAD_DIM:hd * HEAD_DIM + 1] for hd in range(GROUP_WIDTH // HEAD_DIM)], axis=0)


def _blocks_per_step(nb):
    return 2 if nb % 2 == 0 else 1


def _attn_fwd(q, k, v, name):
    dil, length, _ = q.shape
    nb = length // ATTN_BLOCK
    qb = _blocks_per_step(nb)

    def body(q_ref, kp_ref, kc_ref, vp_ref, vc_ref, o_ref, lse_ref):
        j = pl.program_id(1)
        masks = _head_masks()
        for qi in range(qb):
            here = slice(qi * ATTN_BLOCK, (qi + 1) * ATTN_BLOCK)
            before = slice((qi - 1) * ATTN_BLOCK, qi * ATTN_BLOCK)
            kcat = jnp.concatenate([kp_ref[...] if qi == 0 else kc_ref[before], kc_ref[here]], axis=0)
            vcat = jnp.concatenate([vp_ref[...] if qi == 0 else vc_ref[before], vc_ref[here]], axis=0)
            first = (j == 0) if qi == 0 else False
            qs = _stack_heads(q_ref[here], masks)
            sc = jnp.where(_band_mask(first), _dot_nt(qs, kcat), NEG_BIG)
            m = jnp.max(sc, axis=1, keepdims=True)
            e = jnp.exp(sc - m)
            l = jnp.sum(e, axis=1, keepdims=True)
            p = (e / l).astype(BF16)
            lse = m + jnp.log(l)
            o = jnp.zeros((ATTN_BLOCK, GROUP_WIDTH), F32)
            lse_full = jnp.zeros((ATTN_BLOCK, GROUP_WIDTH), F32)
            for hd, msk in enumerate(masks):
                rows = slice(hd * ATTN_BLOCK, (hd + 1) * ATTN_BLOCK)
                o = jnp.where(msk, _dot(p[rows], vcat), o)
                lse_full = jnp.where(msk, lse[rows], lse_full)
            o_ref[here] = o
            lse_ref[here] = lse_full

    cur = pl.BlockSpec((None, qb * ATTN_BLOCK, GROUP_WIDTH), lambda r, j: (r, j, 0))
    prev = pl.BlockSpec((None, ATTN_BLOCK, GROUP_WIDTH), lambda r, j: (r, jnp.maximum(qb * j - 1, 0), 0))
    return pl.pallas_call(
        body, name=name, grid=(dil, nb // qb),
        in_specs=[cur, prev, cur, prev, cur], out_specs=[cur, cur],
        out_shape=[jax.ShapeDtypeStruct(q.shape, F32)] * 2,
        compiler_params=_params(2),
    )(q, k, k, v, v)


def _group_weights(l0, l1, l2):
    m = jnp.maximum(jnp.maximum(l0, l1), l2)
    e0, e1, e2 = jnp.exp(l0 - m), jnp.exp(l1 - m), jnp.exp(l2 - m)
    den = e0 + e1 + e2
    return e0 / den, e1 / den, e2 / den


def _outproj_fwd(h, pool_out, o, lse, w_out, name):
    s = h.shape[0]
    t = _row_tile(s, 512)

    def body(h_ref, po_ref, o0, o1, o2, l0, l1, l2, w_ref, out_ref, a_ref, *stages):
        stages = _pair_stages(stages)
        ov =[_from_residues(r, stages[i], DILATIONS[i]) for i, r in enumerate((o0, o1, o2))]
        lv = [_from_residues(r, stages[3 + i], DILATIONS[i]) for i, r in enumerate((l0, l1, l2))]
        wts = _group_weights(*lv)
        a = jnp.concatenate([po_ref[...]] + [ov[i] * wts[i] for i in range(3)], axis=1).astype(BF16)
        a_ref[...] = a
        out_ref[...] = h_ref[...] + _dot(a, w_ref[...])

    row = lambda w: pl.BlockSpec((t, w), lambda i: (i, 0))
    res = [_residue_spec(dil, t) for dil in DILATIONS]
    return pl.pallas_call(
        body, name=name, grid=(s // t,),
        in_specs=[row(D_MODEL), row(POOL_WIDTH)] + res + res + [pl.BlockSpec((D_MODEL, D_MODEL), lambda i: (0, 0))],
        out_specs=[row(D_MODEL), row(D_MODEL)],
        out_shape=[jax.ShapeDtypeStruct((s, D_MODEL), F32), jax.ShapeDtypeStruct((s, D_MODEL), BF16)],
        scratch_shapes=_stages(t, 6),
        compiler_params=_params(1),
    )(h, pool_out, *o, *lse, w_out)


def _mlp_fwd(h, g, w_up, w_down, name):
    s = h.shape[0]
    t = _row_tile(s, 1024)
    nblk = D_FF // FF_BLOCK

    def body(h_ref, g_ref, wu_ref, wd_ref, out_ref, hn_ref, r_ref, hb_s, acc):
        j = pl.program_id(1)

        @pl.when(j == 0)
        def _():
            _, _, hn = _rms(h_ref[...], g_ref[...])
            hb = hn.astype(BF16)
            hb_s[...] = hb
            hn_ref[...] = hb
            acc[...] = jnp.zeros_like(acc)

        r = jnp.maximum(_dot(hb_s[...], wu_ref[...]), 0.0)
        r_ref[...] = r.astype(BF16)
        acc[...] += _dot((r * r).astype(BF16), wd_ref[...])

        @pl.when(j == nblk - 1)
        def _():
            out_ref[...] = h_ref[...] + acc[...]

    row = pl.BlockSpec((t, D_MODEL), lambda i, j: (i, 0))
    return pl.pallas_call(
        body, name=name, grid=(s // t, nblk),
        in_specs=[row, pl.BlockSpec((1, D_MODEL), lambda i, j: (0, 0)),
                  pl.BlockSpec((None, D_MODEL, FF_BLOCK), lambda i, j: (j, 0, 0)),
                  pl.BlockSpec((None, FF_BLOCK, D_MODEL), lambda i, j: (j, 0, 0))],
        out_specs=[row, row, pl.BlockSpec((t, FF_BLOCK), lambda i, j: (i, j))],
        out_shape=[jax.ShapeDtypeStruct((s, D_MODEL), F32), jax.ShapeDtypeStruct((s, D_MODEL), BF16),
                   jax.ShapeDtypeStruct((s, D_FF), BF16)],
        scratch_shapes=[pltpu.VMEM((t, D_MODEL), BF16), pltpu.VMEM((t, D_MODEL), F32)],
        compiler_params=_params(2),
    )(h, g, w_up, w_down)


def _gate_fwd(h, g, w_gate, p, w_ple, name):
    s = h.shape[0]
    t = _row_tile(s, 512)

    def body(h_ref, g_ref, wg_ref, p_ref, wp_ref, out_ref, hn_ref, gate_ref, pb_ref):
        x = h_ref[...]
        _, _, hn = _rms(x, g_ref[...])
        hb = hn.astype(BF16)
        hn_ref[...] = hb
        gate = 1.0 / (1.0 + jnp.exp(-_dot(hb, wg_ref[...])))
        pb = p_ref[...].astype(BF16)
        pb_ref[...] = pb
        gate_ref[...] = gate.astype(BF16)
        out_ref[...] = x + gate * _dot(pb, wp_ref[...])

    row = lambda w: pl.BlockSpec((t, w), lambda i: (i, 0))
    full = lambda a, b: pl.BlockSpec((a, b), lambda i: (0, 0))
    return pl.pallas_call(
        body, name=name, grid=(s // t,),
        in_specs=[row(D_MODEL), full(1, D_MODEL), full(D_MODEL, D_MODEL), row(PLE_DIM), full(PLE_DIM, D_MODEL)],
        out_specs=[row(D_MODEL), row(D_MODEL), row(D_MODEL), row(PLE_DIM)],
        out_shape=[jax.ShapeDtypeStruct((s, D_MODEL), F32), jax.ShapeDtypeStruct((s, D_MODEL), BF16),
                   jax.ShapeDtypeStruct((s, D_MODEL), BF16), jax.ShapeDtypeStruct((s, PLE_DIM), BF16)],
        compiler_params=_params(1),
    )(h, g, w_gate, p, w_ple)


def _loss_head(h, g, target, name):
    s = h.shape[0]
    t = _row_tile(s, 512)

    def body(h_ref, g_ref, t_ref, loss_ref, dh_ref, dg_ref):
        i = pl.program_id(0)

        @pl.when(i == 0)
        def _():
            loss_ref[...] = jnp.zeros_like(loss_ref)
            dg_ref[...] = jnp.zeros_like(dg_ref)

        gv = g_ref[...]
        n, rstd, y = _rms(h_ref[...], gv)
        err = y - t_ref[...]
        loss_ref[...] += jnp.sum(err * err) * (0.5 / D_MODEL)
        dx, dg = _rms_bwd(err * (1.0 / D_MODEL), n, rstd, gv)
        dh_ref[...] = dx
        dg_ref[...] += dg

    row = pl.BlockSpec((t, D_MODEL), lambda i: (i, 0))
    vec = pl.BlockSpec((1, D_MODEL), lambda i: (0, 0))
    return pl.pallas_call(
        body, name=name, grid=(s // t,),
        in_specs=[row, vec, row],
        out_specs=[pl.BlockSpec((1, LANES), lambda i: (0, 0)), row, vec],
        out_shape=[jax.ShapeDtypeStruct((1, LANES), F32), jax.ShapeDtypeStruct((s, D_MODEL), F32),
                   jax.ShapeDtypeStruct((1, D_MODEL), F32)],
        compiler_params=_params(1),
    )(h, g, target)


def _gate_bwd(dh, gate, pb, w_ple, h, g, w_gate, name, after=()):
    s = h.shape[0]
    t = _row_tile(s, 512)

    def body(dh_ref, gate_ref, pb_ref, wp_ref, h_ref, g_ref, wg_ref, out_ref, dgl_ref, de_ref, dg_ref):
        @pl.when(pl.program_id(0) == 0)
        def _():
            dg_ref[...] = jnp.zeros_like(dg_ref)

        d = dh_ref[...]
        gate = gate_ref[...].astype(F32)
        e = _dot(pb_ref[...], wp_ref[...])
        dgl = (d * e * gate * (1.0 - gate)).astype(BF16)
        dgl_ref[...] = dgl
        de_ref[...] = (d * gate).astype(BF16)
        gv = g_ref[...]
        n, rstd, _ = _rms(h_ref[...], gv)
        dx, dg = _rms_bwd(_dot_nt(dgl, wg_ref[...]), n, rstd, gv)
        out_ref[...] = d + dx
        dg_ref[...] += dg

    row = lambda w: pl.BlockSpec((t, w), lambda i: (i, 0))
    full = lambda a, b: pl.BlockSpec((a, b), lambda i: (0, 0))
    return pl.pallas_call(
        _ordered_after(body, 7, after), name=name, grid=(s // t,),
        in_specs=[row(D_MODEL), row(D_MODEL), row(PLE_DIM), full(PLE_DIM, D_MODEL), row(D_MODEL), full(1, D_MODEL),
                  full(D_MODEL, D_MODEL)] + [pl.BlockSpec(memory_space=pl.ANY)] * len(after),
        out_specs=[row(D_MODEL), row(D_MODEL), row(D_MODEL), full(1, D_MODEL)],
        out_shape=[jax.ShapeDtypeStruct((s, D_MODEL), F32), jax.ShapeDtypeStruct((s, D_MODEL), BF16),
                   jax.ShapeDtypeStruct((s, D_MODEL), BF16), jax.ShapeDtypeStruct((1, D_MODEL), F32)],
        compiler_params=_params(1),
    )(dh, gate, pb, w_ple, h, g, w_gate, *after)


def _mlp_bwd(dh, r, h, g, w_up, w_down, name):
    s = h.shape[0]
    t = _row_tile(s, 1024)
    nblk = D_FF // FF_BLOCK

    def body(dh_ref, r_ref, h_ref, g_ref, wu_ref, wd_ref, out_ref, dup_ref, dg_ref, db_s, acc):
        i, j = pl.program_id(0), pl.program_id(1)

        @pl.when((i == 0) & (j == 0))
        def _():
            dg_ref[...] = jnp.zeros_like(dg_ref)

        @pl.when(j == 0)
        def _():
            db_s[...] = dh_ref[...].astype(BF16)
            acc[...] = jnp.zeros_like(acc)

        dup = (_dot_nt(db_s[...], wd_ref[...]) * (2.0 * r_ref[...].astype(F32))).astype(BF16)
        dup_ref[...] = dup
        acc[...] += _dot_nt(dup, wu_ref[...])

        @pl.when(j == nblk - 1)
        def _():
            gv = g_ref[...]
            n, rstd, _ = _rms(h_ref[...], gv)
            dx, dg = _rms_bwd(acc[...], n, rstd, gv)
            out_ref[...] = dh_ref[...] + dx
            dg_ref[...] += dg

    row = pl.BlockSpec((t, D_MODEL), lambda i, j: (i, 0))
    vec = pl.BlockSpec((1, D_MODEL), lambda i, j: (0, 0))
    blk = pl.BlockSpec((t, FF_BLOCK), lambda i, j: (i, j))
    return pl.pallas_call(
        body, name=name, grid=(s // t, nblk),
        in_specs=[row, blk, row, vec,
                  pl.BlockSpec((None, D_MODEL, FF_BLOCK), lambda i, j: (j, 0, 0)),
                  pl.BlockSpec((None, FF_BLOCK, D_MODEL), lambda i, j: (j, 0, 0))],
        out_specs=[row, blk, vec, row],
        out_shape=[jax.ShapeDtypeStruct((s, D_MODEL), F32), jax.ShapeDtypeStruct((s, D_FF), BF16),
                   jax.ShapeDtypeStruct((1, D_MODEL), F32), jax.ShapeDtypeStruct((s, D_MODEL), BF16)],
        scratch_shapes=[pltpu.VMEM((t, D_MODEL), F32)],
        compiler_params=_params(2),
    )(dh, r, h, g, w_up, w_down)


def _outproj_bwd(dh, w_out, o, lse, ones_bd, name):
    s = dh.shape[0]
    t = _row_tile(s, 512)

    def body(dh_ref, w_ref, o0, o1, o2, l0, l1, l2, bd_ref, dp_ref, do0, do1, do2, de0, de1, de2, dhb_ref, *stages):
        stages = _pair_stages(stages)
        dhb = dh_ref[...].astype(BF16)
        dhb_ref[...] = dhb
        da = _dot_nt(dhb, w_ref[...])
        dp_ref[...] = da[:, 0:POOL_WIDTH]
        ov =[_from_residues(r, stages[i], DILATIONS[i]) for i, r in enumerate((o0, o1, o2))]
        lv = [_from_residues(r, stages[3 + i], DILATIONS[i]) for i, r in enumerate((l0, l1, l2))]
        wts = _group_weights(*lv)
        bd = bd_ref[...]
        cbar = jnp.zeros((t, GROUP_WIDTH), F32)
        for grp, do_ref in enumerate((do0, do1, do2)):
            lo = POOL_WIDTH + grp * GROUP_WIDTH
            dag = da[:, lo:lo + GROUP_WIDTH]
            _to_residues(dag * wts[grp], stages[6 + grp], do_ref, DILATIONS[grp])
            prod = dag * ov[grp]
            hi = prod.astype(BF16)
            low = (prod - hi.astype(F32)).astype(BF16)
            cbar = cbar + wts[grp] * (_dot(hi, bd) + _dot(low, bd))
        for grp, de_ref in enumerate((de0, de1, de2)):
            _to_residues(wts[grp] * cbar, stages[9 + grp], de_ref, DILATIONS[grp])

    row = lambda w: pl.BlockSpec((t, w), lambda i: (i, 0))
    full = lambda a, b: pl.BlockSpec((a, b), lambda i: (0, 0))
    res = [_residue_spec(dil, t) for dil in DILATIONS]
    return pl.pallas_call(
        body, name=name, grid=(s // t,),
        in_specs=[row(D_MODEL), full(D_MODEL, D_MODEL)] + res + res + [full(GROUP_WIDTH, GROUP_WIDTH)],
        out_specs=[row(POOL_WIDTH)] + res + res + [row(D_MODEL)],
        out_shape=[jax.ShapeDtypeStruct((s, POOL_WIDTH), F32)] + [_residue_shape(dil, s, BF16) for dil in DILATIONS]
        + [_residue_shape(dil, s, F32) for dil in DILATIONS] + [jax.ShapeDtypeStruct((s, D_MODEL), BF16)],
        scratch_shapes=_stages(t, 12),
        compiler_params=_params(1),
    )(dh, w_out, *o, *lse, ones_bd)


def _attn_bwd(q, k, v, do, lse, deff, name, after=()):
    dil, length, _ = q.shape
    nb = length // ATTN_BLOCK
    qb = _blocks_per_step(nb)
    nj = nb // qb
    tail = slice((qb - 1) * ATTN_BLOCK, qb * ATTN_BLOCK)

    def body(q_ref, kp_ref, kc_ref, vp_ref, vc_ref, do_ref, lse_ref, de_ref, dq_ref, dk_ref, dv_ref, ck, cv):
        j = pl.program_id(1)

        @pl.when(j < nj)
        def _():
            masks = _head_masks()
            dkc, dvc = [], []
            for qi in range(qb):
                here = slice(qi * ATTN_BLOCK, (qi + 1) * ATTN_BLOCK)
                before = slice((qi - 1) * ATTN_BLOCK, qi * ATTN_BLOCK)
                kcat = jnp.concatenate([kp_ref[...] if qi == 0 else kc_ref[before], kc_ref[here]], axis=0)
                vcat = jnp.concatenate([vp_ref[...] if qi == 0 else vc_ref[before], vc_ref[here]], axis=0)
                first = (j == 0) if qi == 0 else False
                qs = _stack_heads(q_ref[here], masks)
                dos = _stack_heads(do_ref[here], masks)
                sc = jnp.where(_band_mask(first), _dot_nt(qs, kcat), NEG_BIG)
                p = jnp.exp(sc - _column_per_head(lse_ref[here]))
                ds = (p * (_dot_nt(dos, vcat) - _column_per_head(de_ref[here]))).astype(BF16)
                dq = jnp.zeros((ATTN_BLOCK, GROUP_WIDTH), F32)
                for hd, msk in enumerate(masks):
                    dq = jnp.where(msk, _dot(ds[hd * ATTN_BLOCK:(hd + 1) * ATTN_BLOCK], kcat), dq)
                dq_ref[here] = dq
                dkc.append(_dot_tn(ds, qs))
                dvc.append(_dot_tn(p.astype(BF16), dos))

            for out_ref, carry, parts in ((dk_ref, ck, dkc), (dv_ref, cv, dvc)):
                @pl.when(j > 0)
                def _():
                    if qb > 1:
                        out_ref[0:(qb - 1) * ATTN_BLOCK] = carry[0:(qb - 1) * ATTN_BLOCK]
                    out_ref[tail] = carry[tail] + parts[0][0:ATTN_BLOCK]

                for qi in range(qb - 1):
                    carry[qi * ATTN_BLOCK:(qi + 1) * ATTN_BLOCK] = parts[qi][ATTN_BLOCK:] + parts[qi + 1][0:ATTN_BLOCK]
                carry[tail] = parts[qb - 1][ATTN_BLOCK:]

        @pl.when(j == nj)
        def _():
            dk_ref[...] = ck[...]
            dv_ref[...] = cv[...]

    step = lambda j: jnp.minimum(j, nj - 1)
    cur = pl.BlockSpec((None, qb * ATTN_BLOCK, GROUP_WIDTH), lambda r, j: (r, step(j), 0))
    prev = pl.BlockSpec((None, ATTN_BLOCK, GROUP_WIDTH), lambda r, j: (r, jnp.maximum(qb * step(j) - 1, 0), 0))
    late = pl.BlockSpec((None, qb * ATTN_BLOCK, GROUP_WIDTH), lambda r, j: (r, jnp.maximum(j - 1, 0), 0))
    return pl.pallas_call(
        _ordered_after(body, 8, after), name=name, grid=(dil, nj + 1),
        in_specs=[cur, prev, cur, prev, cur, cur, cur, cur] + [pl.BlockSpec(memory_space=pl.ANY)] * len(after),
        out_specs=[cur, late, late],
        out_shape=[jax.ShapeDtypeStruct(q.shape, F32)] * 3,
        scratch_shapes=[pltpu.VMEM((qb * ATTN_BLOCK, GROUP_WIDTH), F32)] * 2,
        compiler_params=_params(2),
    )(q, k, k, v, v, do, lse, deff, *after)


def _pool_bwd(dpool, y, w_bd, scale, name, after=()):
    s = dpool.shape[0]
    t = _row_tile(s, 512)
    nt = s // t

    def body(dp_ref, y_ref, w_ref, sc_ref, du_ref, dw_ref, dsc_ref, ext):
        i = pl.program_id(0)

        @pl.when(i == 0)
        def _():
            ext[t:, :] = jnp.zeros((POOL_HALO, POOL_WIDTH), F32)
            dw_ref[...] = jnp.zeros_like(dw_ref)
            dsc_ref[...] = jnp.zeros_like(dsc_ref)

        dp = dp_ref[...]
        yb = y_ref[...]
        w = w_ref[...]
        dsc_ref[...] += jnp.sum(dp * _dot(yb, w), axis=0, keepdims=True)
        dyo = (dp * sc_ref[...]).astype(BF16)
        dw_ref[...] += _dot_tn(yb, dyo)
        dy = _dot_nt(dyo, w)
        win = _pool_lane_window()
        pos = (nt - 1 - i) * t + lax.broadcasted_iota(jnp.int32, (t, POOL_WIDTH), 0)
        gq = dy / jnp.minimum(pos + 1, win).astype(F32)
        ext[0:t, :] = gq
        acc = gq
        wsum = jnp.zeros_like(gq)
        for k in range(1, POOL_HALO):
            acc = acc + ext[k:k + t, :]
            if k + 1 in POOL_WINDOWS:
                wsum = jnp.where(win == k + 1, acc, wsum)
        du_ref[...] = wsum - dy
        ext[t:, :] = gq[0:POOL_HALO, :]

    rev = pl.BlockSpec((t, POOL_WIDTH), lambda i: (nt - 1 - i, 0))
    full = lambda a, b: pl.BlockSpec((a, b), lambda i: (0, 0))
    return pl.pallas_call(
        _ordered_after(body, 4, after), name=name, grid=(nt,),
        in_specs=[rev, rev, full(POOL_WIDTH, POOL_WIDTH), full(1, POOL_WIDTH)]
        + [pl.BlockSpec(memory_space=pl.ANY)] * len(after),
        out_specs=[rev, full(POOL_WIDTH, POOL_WIDTH), full(1, POOL_WIDTH)],
        out_shape=[jax.ShapeDtypeStruct((s, POOL_WIDTH), F32), jax.ShapeDtypeStruct((POOL_WIDTH, POOL_WIDTH), F32),
                   jax.ShapeDtypeStruct((1, POOL_WIDTH), F32)],
        scratch_shapes=[pltpu.VMEM((t + POOL_HALO, POOL_WIDTH), F32)],
        compiler_params=_params(1),
    )(dpool, y, w_bd, scale, *after)


def _normproj_bwd(dh, du, dq, dk, dv, rc, rsa, rsb, w_in, h, g, name):
    s = h.shape[0]
    t = _row_tile(s, 512)

    def body(dh_ref, du_ref, q0, q1, q2, k0, k1, k2, v0, v1, v2, c_ref, sa_ref, sb_ref, w_ref, h_ref, g_ref,
             out_ref, dz_ref, dg_ref, *stages):
        @pl.when(pl.program_id(0) == 0)
        def _():
            dg_ref[...] = jnp.zeros_like(dg_ref)

        c, sa, sb = c_ref[...], sa_ref[...], sb_ref[...]

        def unrot(a, scale):
            halves = [_rot_t(a[:, hf * LANES:(hf + 1) * LANES] * scale, c, sa, sb) for hf in range(2)]
            return jnp.concatenate(halves, axis=1)

        staged = _pair_stages(stages)
        tok = lambda refs, base: [_from_residues(r, staged[base + i], DILATIONS[i]) for i, r in enumerate(refs)]
        chunks = [du_ref[...]]
        chunks += [unrot(a, HEAD_DIM ** -0.5) for a in tok((q0, q1, q2), 0)]
        chunks += [unrot(a, 1.0) for a in tok((k0, k1, k2), 3)]
        chunks += tok((v0, v1, v2), 6)
        acc = jnp.zeros((t, D_MODEL), F32)
        for ci, ch in enumerate(chunks):
            cols = slice(ci * GROUP_WIDTH, (ci + 1) * GROUP_WIDTH)
            cb = ch.astype(BF16)
            dz_ref[:, cols] = cb
            acc = acc + _dot_nt(cb, w_ref[:, cols])
        gv = g_ref[...]
        n, rstd, _ = _rms(h_ref[...], gv)
        dx, dg = _rms_bwd(acc, n, rstd, gv)
        out_ref[...] = dh_ref[...] + dx
        dg_ref[...] += dg

    row = lambda w: pl.BlockSpec((t, w), lambda i: (i, 0))
    vec = pl.BlockSpec((1, D_MODEL), lambda i: (0, 0))
    res = [_residue_spec(dil, t) for dil in DILATIONS]
    return pl.pallas_call(
        body, name=name, grid=(s // t,),
        in_specs=[row(D_MODEL), row(POOL_WIDTH)] + res * 3 + [row(LANES)] * 3
        + [pl.BlockSpec((D_MODEL, N_IN), lambda i: (0, 0)), row(D_MODEL), vec],
        out_specs=[row(D_MODEL), row(N_IN), vec],
        out_shape=[jax.ShapeDtypeStruct((s, D_MODEL), F32), jax.ShapeDtypeStruct((s, N_IN), BF16),
                   jax.ShapeDtypeStruct((1, D_MODEL), F32)],
        scratch_shapes=_stages(t, 9),
        compiler_params=_params(1),
    )(dh, du, *dq, *dk, *dv, rc, rsa, rsb, w_in, h, g)


def _matmul_tn(a, b, name, *, square_a=False, tn=None, blocked_out=False):
    s, m = a.shape
    n = b.shape[1]
    tk = _row_tile(s, 2048)
    tm = min(m, 1024)
    tn = tn or min(n, 1024)
    assert m % tm == 0 and n % tn == 0
    nk = s // tk
    nsub = tn // FF_BLOCK if blocked_out else 1

    def body(a_ref, b_ref, o_ref, ob_ref, acc):
        k = pl.program_id(2)

        @pl.when(k == 0)
        def _():
            acc[...] = jnp.zeros_like(acc)

        av = a_ref[...]
        if square_a:
            av = av.astype(F32)
            av = av * av
        acc[...] += _dot_tn(av.astype(BF16), b_ref[...].astype(BF16))

        @pl.when(k == nk - 1)
        def _():
            if blocked_out:
                for sub in range(nsub):
                    cols = slice(sub * FF_BLOCK, (sub + 1) * FF_BLOCK)
                    o_ref[sub] = acc[:, cols]
                    ob_ref[sub] = acc[:, cols].astype(BF16)
            else:
                o_ref[...] = acc[...]
                ob_ref[...] = acc[...].astype(BF16)

    if blocked_out:
        shape = (n // FF_BLOCK, m, FF_BLOCK)
        out_spec = pl.BlockSpec((nsub, tm, FF_BLOCK), lambda i, j, k: (j, i, 0))
    else:
        shape = (m, n)
        out_spec = pl.BlockSpec((tm, tn), lambda i, j, k: (i, j))
    return pl.pallas_call(
        body, name=name, grid=(m // tm, n // tn, nk),
        in_specs=[pl.BlockSpec((tk, tm), lambda i, j, k: (k, i)), pl.BlockSpec((tk, tn), lambda i, j, k: (k, j))],
        out_specs=[out_spec, out_spec],
        out_shape=[jax.ShapeDtypeStruct(shape, F32), jax.ShapeDtypeStruct(shape, BF16)],
        scratch_shapes=[pltpu.VMEM((tm, tn), F32)],
        compiler_params=_params(3),
    )(a, b)


def _adamw_math(w, g, m, v):
    m = ADAM_B1 * m + (1.0 - ADAM_B1) * g
    v = ADAM_B2 * v + (1.0 - ADAM_B2) * (g * g)
    m_hat = m / (1.0 - ADAM_B1 ** ADAM_STEP)
    v_hat = v / (1.0 - ADAM_B2 ** ADAM_STEP)
    delta = -ADAM_LR * (m_hat / (jnp.sqrt(v_hat) + ADAM_EPS) + ADAM_WD * w)
    return delta, m, v


def _adamw_sharded(w, m, v, own, recv0, recv1, name):
    _, rows, cols = w.shape
    t = _row_tile(rows, 256)

    def body(w_ref, m_ref, v_ref, own_ref, r0_ref, r1_ref, g_ref, d_ref, nm_ref, nv_ref):
        layer0 = pl.program_id(0) == 0
        g = own_ref[...]
        for k in range(N_DEV - 1):
            g = g + jnp.where(layer0, r0_ref[k], r1_ref[k]).astype(F32)
        g_ref[...] = g
        d_ref[...], nm_ref[...], nv_ref[...] = _adamw_math(w_ref[...], g, m_ref[...], v_ref[...])

    blk = pl.BlockSpec((None, t, cols), lambda l, i: (l, i, 0))
    recv = lambda layer: pl.BlockSpec((N_DEV - 1, t, cols), lambda l, i: (0, jnp.where(l == layer, i, 0), 0))
    return pl.pallas_call(
        body, name=name, grid=(2, rows // t),
        in_specs=[blk, blk, blk, blk, recv(0), recv(1)], out_specs=[blk] * 4,
        out_shape=[jax.ShapeDtypeStruct(w.shape, F32)] * 4,
        compiler_params=_params(2),
    )(w, m, v, own, recv0, recv1)


def _adamw_packed(w, g, m, v, name):
    def body(w_ref, g_ref, m_ref, v_ref, d_ref, nm_ref, nv_ref):
        d_ref[...], nm_ref[...], nv_ref[...] = _adamw_math(w_ref[...], g_ref[...], m_ref[...], v_ref[...])

    return pl.pallas_call(
        body, name=name, out_shape=[jax.ShapeDtypeStruct(w.shape, F32)] * 3,
        compiler_params=pltpu.CompilerParams(vmem_limit_bytes=VMEM_LIMIT),
    )(w, g, m, v)


def _peer(k):
    x, y, c = lax.axis_index("x"), lax.axis_index("y"), lax.axis_index("c")
    return (1 - x if k & 4 else x, 1 - y if k & 2 else y, 1 - c if k & 1 else c)


def _linear(dev):
    return 4 * dev[0] + 2 * dev[1] + dev[2]


HBM_SPEC = pl.BlockSpec(memory_space=pltpu.HBM)
SEM_SPEC = pl.BlockSpec(memory_space=pltpu.SEMAPHORE)
ANY_SPEC = pl.BlockSpec(memory_space=pl.ANY)
EFFECT = pltpu.SideEffectType.DATAFLOW_SIDE_EFFECTING


def _in_hbm(a):
    return pltpu.with_memory_space_constraint(a, pltpu.HBM)


class _Exchange:
    def __init__(self, name, groups, scatter, after=()):
        self.name, self.scatter = name, scatter
        self.sizes = sizes = [len(g) for g in groups]
        srcs = [a for g in groups for a in g]
        n, ng = len(srcs), len(groups)
        lead = (N_DEV - 1,) if scatter else (N_DEV,)
        shapes = [lead + (a.shape[1:] if scatter else a.shape) for a in srcs]
        lands = [lax.empty(sh, a.dtype) for sh, a in zip(shapes, srcs)]
        offsets = [sum(sizes[:gi]) for gi in range(ng)]
        copy = self._copy

        def body(*refs):
            src, land = refs[:n], refs[n:2 * n]
            sems = refs[2 * n + len(after):2 * n + len(after) + 2 * ng]
            token = refs[-1]
            for gi in range(ng):
                for wi in range(sizes[gi]):
                    w = offsets[gi] + wi
                    for k in range(1, N_DEV):
                        copy(src[w], land[w], sems[2 * gi], sems[2 * gi + 1], wi, k).start()
            token[...] = jnp.zeros_like(token)

        sem_shapes = [pltpu.SemaphoreType.DMA((7 * sz,)) for sz in sizes for _ in range(2)]
        outs = pl.pallas_call(
            body, name=name + "_start",
            in_specs=[HBM_SPEC] * (2 * n) + [ANY_SPEC] * len(after),
            out_specs=[SEM_SPEC] * (2 * ng) + [HBM_SPEC] * (2 * n) + [pl.BlockSpec(memory_space=pltpu.VMEM)],
            out_shape=sem_shapes + [pltpu.HBM(a.shape, a.dtype) for a in srcs + lands]
            + [jax.ShapeDtypeStruct((8, LANES), F32)],
            input_output_aliases={i: 2 * ng + i for i in range(2 * n)},
            compiler_params=pltpu.CompilerParams(has_side_effects=EFFECT),
        )(*[_in_hbm(a) for a in srcs + lands], *after)
        self.sems = [outs[2 * gi:2 * gi + 2] for gi in range(ng)]
        thru = outs[2 * ng:2 * ng + 2 * n]
        self.srcs = [thru[offsets[gi]:offsets[gi] + sizes[gi]] for gi in range(ng)]
        self.lands = [thru[n + offsets[gi]:n + offsets[gi] + sizes[gi]] for gi in range(ng)]
        self.token = outs[-1]

    def _copy(self, src, land, send_sems, recv_sems, wi, k):
        to = _peer(k)
        if self.scatter:
            src_ref, dst_ref = src.at[_linear(to)], land.at[k - 1]
        else:
            src_ref, dst_ref = src, land.at[_linear(_peer(0))]
        return pltpu.make_async_remote_copy(
            src_ref=src_ref, dst_ref=dst_ref, send_sem=send_sems.at[7 * wi + k - 1],
            recv_sem=recv_sems.at[7 * wi + k - 1], device_id=to, device_id_type=MESH)

    def wait(self, gi, after):
        n = self.sizes[gi]
        copy = self._copy

        def body(*refs):
            src, land = refs[:n], refs[n:2 * n]
            send_sems, recv_sems = refs[2 * n], refs[2 * n + 1]
            for wi in range(n):
                for k in range(1, N_DEV):
                    cp = copy(src[wi], land[wi], send_sems, recv_sems, wi, k)
                    cp.wait_send()
                    cp.wait_recv()

        arrays = list(self.srcs[gi]) + list(self.lands[gi])
        outs = pl.pallas_call(
            body, name=f"{self.name}_wait{gi}",
            in_specs=[HBM_SPEC] * (2 * n) + [SEM_SPEC, SEM_SPEC] + [ANY_SPEC] * len(after),
            out_specs=[HBM_SPEC] * (2 * n),
            out_shape=[pltpu.HBM(a.shape, a.dtype) for a in arrays],
            input_output_aliases={i: i for i in range(2 * n)},
            compiler_params=pltpu.CompilerParams(has_side_effects=EFFECT),
        )(*arrays, *self.sems[gi], *after)
        return outs[:n], outs[n:]


def _allreduce_packed(g):
    rows = g.shape[0]

    def body(g_ref, out_ref, buf, send_sems, recv_sems):
        me = _linear(_peer(0))
        buf[me] = g_ref[...]
        copies = []
        for k in range(1, N_DEV):
            copies.append(pltpu.make_async_remote_copy(
                src_ref=g_ref, dst_ref=buf.at[me], send_sem=send_sems.at[k - 1], recv_sem=recv_sems.at[k - 1],
                device_id=_peer(k), device_id_type=MESH))
        for cp in copies:
            cp.start()
        for k in range(1, N_DEV):
            pltpu.make_async_remote_copy(
                src_ref=g_ref, dst_ref=buf.at[_linear(_peer(k))], send_sem=send_sems.at[k - 1],
                recv_sem=recv_sems.at[k - 1], device_id=_peer(k), device_id_type=MESH).wait_recv()
        for cp in copies:
            cp.wait_send()
        total = buf[0]
        for d in range(1, N_DEV):
            total = total + buf[d]
        out_ref[...] = total

    return pl.pallas_call(
        body, name="allreduce_small",
        in_specs=[pl.BlockSpec(memory_space=pltpu.VMEM)], out_specs=pl.BlockSpec(memory_space=pltpu.VMEM),
        out_shape=jax.ShapeDtypeStruct(g.shape, F32),
        scratch_shapes=[pltpu.VMEM((N_DEV, rows, g.shape[1]), F32), pltpu.SemaphoreType.DMA((7,)),
                        pltpu.SemaphoreType.DMA((7,))],
        compiler_params=pltpu.CompilerParams(vmem_limit_bytes=VMEM_LIMIT),
    )(g)


def _rotary_tables(positions):
    rot_dim = HEAD_DIM // 4
    inv_freq = ROPE_THETA ** (-jnp.arange(0, rot_dim, 2, dtype=F32) / rot_dim)
    dim = jnp.arange(LANES) % HEAD_DIM
    ang = positions.astype(F32)[:, None] * inv_freq[dim % ROT_SHIFT][None, :]
    cos, sin = jnp.cos(ang), jnp.sin(ang)
    first, second = dim < ROT_SHIFT, (dim >= ROT_SHIFT) & (dim < rot_dim)
    c = jnp.where(first | second, cos, 1.0)
    sa = jnp.where(second, sin, 0.0)
    sb = jnp.where(first, -sin, 0.0)
    return [c, sa, sb]


def _block_diag(pool_w):
    gc = pool_w.shape[-1]
    out = jnp.zeros((POOL_WIDTH, POOL_WIDTH), pool_w.dtype)
    for grp in range(pool_w.shape[0]):
        out = lax.dynamic_update_slice(out, pool_w[grp], (grp * gc, grp * gc))
    return out


def _diag_blocks(a):
    gc = POOL_WIDTH // len(POOL_WINDOWS)
    return jnp.stack([a[grp * gc:(grp + 1) * gc, grp * gc:(grp + 1) * gc] for grp in range(len(POOL_WINDOWS))])


def _local_step(x, p, positions, loss_target, norm1, pool_w, pool_scale, norm2, norm3, final_norm, weights, send):
    rc, rsa, rsb = _rotary_tables(positions)
    ones_bd = _block_diag(jnp.ones((4, HEAD_DIM, HEAD_DIM), BF16))
    saved = []
    h = x
    for i in range(2):
        tag = f"_l{i}"
        g1, g2, g3 = norm1[i:i + 1], norm2[i:i + 1], norm3[i:i + 1]
        w_bd = _block_diag(pool_w[i]).astype(BF16)
        scale = pool_scale[i:i + 1]
        w_in = weights(i, "in", (h,))
        hn1, u, *qkv = _normproj_fwd(h, g1, w_in, rc, rsa, rsb, "normproj_fwd" + tag)
        qkv = [qkv[3 * grp:3 * grp + 3] for grp in range(3)]
        pool_out, y = _pool_fwd(u, w_bd, scale, "pool_fwd" + tag)
        o, lse = zip(*[_attn_fwd(*qkv[grp], f"attn_fwd{tag}_g{grp}") for grp in range(3)])
        w_out = weights(i, "out", (pool_out, *o))
        h1, a = _outproj_fwd(h, pool_out, o, lse, w_out, "outproj_fwd" + tag)
        w_up, w_down, w_gate, w_ple = weights(i, "rest", (h1,))
        h2, hn2, r = _mlp_fwd(h1, g2, w_up, w_down, "mlp_fwd" + tag)
        h3, hn3, gate, pb = _gate_fwd(h2, g3, w_gate, p[i], w_ple, "gate_fwd" + tag)
        saved.append(dict(h0=h, hn1=hn1, qkv=qkv, y=y, o=o, lse=lse, a=a, h1=h1, hn2=hn2, r=r, h2=h2,
                          hn3=hn3, gate=gate, pb=pb, w_bd=w_bd, scale=scale, g1=g1, g2=g2, g3=g3,
                          w_in=w_in, w_out=w_out, w_up=w_up, w_down=w_down, w_gate=w_gate, w_ple=w_ple))
        h = h3
    loss, dh, d_final = _loss_head(h, final_norm.reshape(1, D_MODEL), loss_target, "loss_head")

    grads = [None, None]
    sent = ()
    for i in (1, 0):
        tag = f"_l{i}"
        sv = saved[i]
        dh2, dgl, de, dg3 = _gate_bwd(dh, sv["gate"], sv["pb"], sv["w_ple"], sv["h2"], sv["g3"], sv["w_gate"],
                                      "gate_bwd" + tag, after=sent)
        dw_gate = _matmul_tn(sv["hn3"], dgl, "dw_gate" + tag)
        dw_ple = _matmul_tn(sv["pb"], de, "dw_ple" + tag)
        dh1, dup, dg2, dh2b = _mlp_bwd(dh2, sv["r"], sv["h1"], sv["g2"], sv["w_up"], sv["w_down"], "mlp_bwd" + tag)
        dw_down = _matmul_tn(sv["r"], dh2b, "dw_down" + tag, square_a=True)
        dw_up = _matmul_tn(sv["hn2"], dup, "dw_up" + tag, blocked_out=True)
        dpool, do0, do1, do2, de0, de1, de2, dh1b = _outproj_bwd(dh1, sv["w_out"], sv["o"], sv["lse"], ones_bd,
                                                                 "outproj_bwd" + tag)
        dw_out = _matmul_tn(sv["a"], dh1b, "dw_out" + tag)
        sent = send(i, "main", dict(w_gate=dw_gate, w_ple=dw_ple, w_down=dw_down, w_up=dw_up, w_out=dw_out))
        dqkv = [_attn_bwd(*sv["qkv"][grp], do_g, sv["lse"][grp], de_g, f"attn_bwd{tag}_g{grp}", after=sent)
                for grp, (do_g, de_g) in enumerate(((do0, de0), (do1, de1), (do2, de2)))]
        dq, dk, dv = zip(*dqkv)
        du, dw_bd, dscale = _pool_bwd(dpool, sv["y"], sv["w_bd"], sv["scale"], "pool_bwd" + tag, after=sent)
        dh, dz, dg1 = _normproj_bwd(dh1, du, dq, dk, dv, rc, rsa, rsb, sv["w_in"], sv["h0"], sv["g1"],
                                    "normproj_bwd" + tag)
        dw_in = _matmul_tn(sv["hn1"], dz, "dw_in" + tag, tn=N_IN // 2)
        sent = send(i, "in", dict(w_in=dw_in))
        grads[i] = dict(norm1=dg1, norm2=dg2, norm3=dg3, pool_w=_diag_blocks(dw_bd), pool_scale=dscale)
    return loss, dh, grads, d_final, sent


def _pack_small(norm1, norm2, norm3, final_norm, pool_scale, pool_w):
    scale_row = jnp.concatenate([pool_scale.reshape(1, 2 * POOL_WIDTH), jnp.zeros((1, D_MODEL - 2 * POOL_WIDTH), F32)], axis=1)
    return jnp.concatenate([norm1, norm2, norm3, final_norm.reshape(1, D_MODEL), scale_row,
                            pool_w.reshape(32, D_MODEL)], axis=0)


def _unpack_small(a):
    return dict(norm1=a[0:2], norm2=a[2:4], norm3=a[4:6], final_norm=a[6], pool_scale=a[7, 0:2 * POOL_WIDTH].reshape(2, POOL_WIDTH),
                pool_w=a[8:40].reshape(2, 4, HEAD_DIM, HEAD_DIM))


def _chunks_cols(a, cols):
    return a.reshape(a.shape[0], N_DEV, cols).transpose(1, 0, 2)


def _chunks_rows(a, rows):
    return a.reshape(N_DEV, rows, a.shape[1])


BIG = ("w_in", "w_out", "w_up", "w_down", "w_gate", "w_ple")
SMALL = ("norm1", "norm2", "norm3", "final_norm", "pool_scale", "pool_w")
ORDER = ("norm1", "w_in", "pool_w", "pool_scale", "w_out", "norm2", "w_up", "w_down", "norm3", "w_gate", "w_ple",
         "final_norm")


def kernel(x, p, positions, norm1, w_in, pool_w, pool_scale, w_out, norm2, w_up, w_down, norm3, w_gate, w_ple, final_norm, loss_target, m_norm1, m_w_in, m_pool_w, m_pool_scale, m_w_out, m_norm2, m_w_up, m_w_down, m_norm3, m_w_gate, m_w_ple, m_final_norm, v_norm1, v_w_in, v_pool_w, v_pool_scale, v_w_out, v_norm2, v_w_up, v_w_down, v_norm3, v_w_gate, v_w_ple, v_final_norm):
    w = dict(norm1=norm1, w_in=w_in, pool_w=pool_w, pool_scale=pool_scale, w_out=w_out, norm2=norm2, w_up=w_up,
             w_down=w_down, norm3=norm3, w_gate=w_gate, w_ple=w_ple, final_norm=final_norm)
    m = dict(norm1=m_norm1, w_in=m_w_in, pool_w=m_pool_w, pool_scale=m_pool_scale, w_out=m_w_out, norm2=m_norm2,
             w_up=m_w_up, w_down=m_w_down, norm3=m_norm3, w_gate=m_w_gate, w_ple=m_w_ple, final_norm=m_final_norm)
    v = dict(norm1=v_norm1, w_in=v_w_in, pool_w=v_pool_w, pool_scale=v_pool_scale, w_out=v_w_out, norm2=v_norm2,
             w_up=v_w_up, w_down=v_w_down, norm3=v_norm3, w_gate=v_w_gate, w_ple=v_w_ple, final_norm=v_final_norm)
    seq = x.shape[1]

    bf = {n: [w[n][layer].astype(BF16) for layer in range(2)] for n in BIG}
    rest = ("w_up", "w_down", "w_gate", "w_ple")
    me = 4 * lax.axis_index("x") + 2 * lax.axis_index("y") + lax.axis_index("c")
    gathers = [_Exchange("gather_l0", [[bf["w_in"][0]], [bf["w_out"][0]], [bf[n][0] for n in rest]], scatter=False)]
    unpack = dict(w_in=lambda a: a.transpose(1, 0, 2).reshape(D_MODEL, N_IN),
                  w_out=lambda a: a.reshape(D_MODEL, D_MODEL), w_gate=lambda a: a.reshape(D_MODEL, D_MODEL),
                  w_ple=lambda a: a.transpose(1, 0, 2).reshape(PLE_DIM, D_MODEL), w_up=lambda a: a, w_down=lambda a: a)
    layer1 = {}

    def gathered(exchange, group, names, after):
        shards, lands = exchange.wait(group, after)
        return {n: lax.dynamic_update_slice_in_dim(land, shard[None], me, axis=0)
                for n, shard, land in zip(names, shards, lands)}

    def weights(layer, part, after):
        names = dict(out=("w_out",), rest=rest)[part] if part != "in" else ("w_in",)
        if layer == 0:
            if part == "out":
                gathers.append(_Exchange("gather_l1", [[bf[n][1] for n in BIG]], scatter=False, after=after))
            got = gathered(gathers[0], ("in", "out", "rest").index(part), names, after)
        else:
            if not layer1:
                layer1.update(gathered(gathers[1], 0, BIG, after))
            got = layer1
        full = [unpack[n](got[n]) for n in names]
        return full if part == "rest" else full[0]

    to_chunks = dict(w_in=lambda a: _chunks_cols(a, N_IN // N_DEV), w_out=lambda a: _chunks_rows(a, D_MODEL // N_DEV),
                     w_up=lambda a: a, w_down=lambda a: _chunks_rows(a, FF_BLOCK),
                     w_gate=lambda a: _chunks_rows(a, D_MODEL // N_DEV), w_ple=lambda a: _chunks_cols(a, D_MODEL // N_DEV))
    own = {n: [None, None] for n in BIG}
    scatters = {}

    def send(layer, part, grads):
        for n, (g32, _) in grads.items():
            own[n][layer] = lax.dynamic_index_in_dim(to_chunks[n](g32), me, axis=0, keepdims=False)
        ex = _Exchange(f"scatter_{part}_l{layer}", [[to_chunks[n](g16) for n, (_, g16) in grads.items()]], scatter=True)
        scatters[layer, part] = (tuple(grads), ex)
        return (ex.token,)

    loss, dx, grads, d_final, sent = _local_step(
        x.reshape(seq, D_MODEL), p.reshape(2, seq, PLE_DIM), positions.reshape(seq), loss_target.reshape(seq, D_MODEL),
        norm1, pool_w, pool_scale, norm2, norm3, final_norm, weights, send)

    small_g = _pack_small(
        *[jnp.concatenate([grads[0][n], grads[1][n]], axis=0) for n in ("norm1", "norm2", "norm3")], d_final.reshape(D_MODEL),
        jnp.concatenate([grads[0]["pool_scale"], grads[1]["pool_scale"]], axis=0),
        jnp.stack([grads[0]["pool_w"], grads[1]["pool_w"]]))
    small_g = _allreduce_packed(small_g)

    g_out, d_out, m_out, v_out = {}, {}, {}, {}
    for part in ("main", "in"):
        recv = {}
        for layer in (1, 0):
            names, ex = scatters[layer, part]
            for n, r in zip(names, ex.wait(0, sent)[1]):
                recv[n, layer] = r
        for n in names:
            g_out[n], d_out[n], m_out[n], v_out[n] = _adamw_sharded(
                w[n], m[n], v[n], jnp.stack(own[n]), recv[n, 0], recv[n, 1], "adamw_" + n)
        sent = tuple(d_out[n] for n in names)
    pack = lambda t: _pack_small(*[t[n] for n in SMALL])
    d_small, m_small, v_small = _adamw_packed(pack(w), small_g, pack(m), pack(v), "adamw_small")
    for dst, a in ((g_out, small_g), (d_out, d_small), (m_out, m_small), (v_out, v_small)):
        dst.update(_unpack_small(a))

    loss = lax.psum(loss[0, 0], ("x", "y", "c"))
    return (loss, dx.reshape(1, seq, D_MODEL), *[g_out[n] for n in ORDER], *[d_out[n] for n in ORDER],
            *[m_out[n] for n in ORDER], *[v_out[n] for n in ORDER])
```

```python
import functools

import jax
import jax.numpy as jnp
from jax import lax
from jax.experimental import pallas as pl
from jax.experimental.pallas import tpu as pltpu

F32 = jnp.float32
BF16 = jnp.bfloat16

D_MODEL = 1024
HEAD_DIM = 64
POOL_WIDTH = 256
POOL_WINDOWS = (2, 4, 8, 16)
POOL_HALO = 16
GROUP_WIDTH = 256
DILATIONS = (1, 4, 16)
ATTN_BLOCK = 128
ROT_SHIFT = 8
ROPE_THETA = 500000.0
D_FF = 4096
FF_BLOCK = 512
N_DEV = 8
N_IN = POOL_WIDTH + 3 * 768
PLE_DIM = 256
EPS = 1e-6
NEG_BIG = -1e30

ADAM_LR = 0.001
ADAM_B1 = 0.9
ADAM_B2 = 0.999
ADAM_EPS = 1e-08
ADAM_WD = 0.01
ADAM_STEP = 10

LANES = 128
VMEM_LIMIT = 56 * 1024 * 1024
MESH = pl.DeviceIdType.MESH


def _params(n_grid):
    return pltpu.CompilerParams(dimension_semantics=("arbitrary",) * n_grid, vmem_limit_bytes=VMEM_LIMIT)


def _dot(a, b):
    return jnp.dot(a, b, preferred_element_type=F32)


def _dot_nt(a, b):
    return lax.dot_general(a, b, (((1,), (1,)), ((), ())), preferred_element_type=F32)


def _dot_tn(a, b):
    return lax.dot_general(a, b, (((0,), (0,)), ((), ())), preferred_element_type=F32)


def _rms(x, g):
    rstd = lax.rsqrt(jnp.mean(x * x, axis=-1, keepdims=True) + EPS)
    n = x * rstd
    return n, rstd, n * g


def _rms_bwd(dy, n, rstd, g):
    dyn = dy * g
    dx = rstd * (dyn - n * jnp.mean(dyn * n, axis=-1, keepdims=True))
    return dx, jnp.sum(dy * n, axis=0, keepdims=True)


def _ordered_after(body, n_in, after):
    if not after:
        return body
    return lambda *refs: body(*refs[:n_in], *refs[n_in + len(after):])


def _row_tile(s, t):
    t = min(s, t)
    assert s % t == 0
    return t


def _rot(z, c, sa, sb):
    return z * c + pltpu.roll(z, ROT_SHIFT, 1) * sa + pltpu.roll(z, LANES - ROT_SHIFT, 1) * sb


def _rot_t(dz, c, sa, sb):
    return dz * c + pltpu.roll(dz * sa, LANES - ROT_SHIFT, 1) + pltpu.roll(dz * sb, ROT_SHIFT, 1)


def _to_residues(value, stage, out_ref, dil):
    if dil == 1:
        out_ref[0] = value.astype(out_ref.dtype)
        return
    rows = value.shape[0] // dil
    for hf in range(GROUP_WIDTH // LANES):
        lanes = slice(hf * LANES, (hf + 1) * LANES)
        stage[hf][...] = value[:, lanes]
        for r in range(dil):
            out_ref[r, :, lanes] = stage[hf][pl.ds(r, rows, stride=dil), :].astype(out_ref.dtype)


def _from_residues(in_ref, stage, dil):
    if dil == 1:
        return in_ref[0].astype(F32)
    rows = in_ref.shape[1]
    for hf in range(GROUP_WIDTH // LANES):
        for r in range(dil):
            stage[hf][pl.ds(r, rows, stride=dil), :] = in_ref[r, :, hf * LANES:(hf + 1) * LANES].astype(F32)
    return jnp.concatenate([stage[0][...], stage[1][...]], axis=1)


def _residue_spec(dil, t):
    return pl.BlockSpec((dil, t // dil, GROUP_WIDTH), lambda i: (0, i, 0))


def _residue_shape(dil, s, dtype):
    return jax.ShapeDtypeStruct((dil, s // dil, GROUP_WIDTH), dtype)


def _stages(t, n):
    return [pltpu.VMEM((t, LANES), F32)] * (n * (GROUP_WIDTH // LANES))


def _pair_stages(refs):
    return [refs[i:i + 2] for i in range(0, len(refs), 2)]


def _normproj_fwd(h, g, w_in, rc, rsa, rsb, name):
    s = h.shape[0]
    t = _row_tile(s, 512)

    def body(h_ref, g_ref, w_ref, c_ref, sa_ref, sb_ref, hn_ref, u_ref, *rest):
        qkv_refs, stages = rest[:9], _pair_stages(rest[9:])
        _, _, hn = _rms(h_ref[...], g_ref[...])
        hb = hn.astype(BF16)
        hn_ref[...] = hb
        c, sa, sb = c_ref[...], sa_ref[...], sb_ref[...]

        def rot(z, scale):
            halves = [_rot(z[:, hf * LANES:(hf + 1) * LANES], c, sa, sb) * scale for hf in range(2)]
            return jnp.concatenate(halves, axis=1)

        u_ref[...] = _dot(hb, w_ref[:, 0:POOL_WIDTH])
        for grp, dil in enumerate(DILATIONS):
            lo = POOL_WIDTH + grp * GROUP_WIDTH
            q_ref, k_ref, v_ref = qkv_refs[3 * grp:3 * grp + 3]
            _to_residues(rot(_dot(hb, w_ref[:, lo:lo + GROUP_WIDTH]), HEAD_DIM ** -0.5), stages[0], q_ref, dil)
            _to_residues(rot(_dot(hb, w_ref[:, lo + 768:lo + 768 + GROUP_WIDTH]), 1.0), stages[1], k_ref, dil)
            _to_residues(_dot(hb, w_ref[:, lo + 1536:lo + 1536 + GROUP_WIDTH]), stages[2], v_ref, dil)

    row = lambda w: pl.BlockSpec((t, w), lambda i: (i, 0))
    return pl.pallas_call(
        body, name=name, grid=(s // t,),
        in_specs=[row(D_MODEL), pl.BlockSpec((1, D_MODEL), lambda i: (0, 0)),
                  pl.BlockSpec((D_MODEL, N_IN), lambda i: (0, 0)), row(LANES), row(LANES), row(LANES)],
        out_specs=[row(D_MODEL), row(POOL_WIDTH)] + [_residue_spec(dil, t) for dil in DILATIONS for _ in range(3)],
        out_shape=[jax.ShapeDtypeStruct((s, D_MODEL), BF16), jax.ShapeDtypeStruct((s, POOL_WIDTH), F32)]
        + [_residue_shape(dil, s, BF16) for dil in DILATIONS for _ in range(3)],
        scratch_shapes=_stages(t, 3),
        compiler_params=_params(1),
    )(h, g, w_in, rc, rsa, rsb)


def _pool_lane_window():
    lane = lax.broadcasted_iota(jnp.int32, (1, POOL_WIDTH), 1)
    return jnp.left_shift(2, lane // (POOL_WIDTH // len(POOL_WINDOWS)))


def _pool_fwd(u, w_bd, scale, name):
    s = u.shape[0]
    t = _row_tile(s, 512)

    def body(u_ref, w_ref, sc_ref, out_ref, y_ref, ext):
        i = pl.program_id(0)

        @pl.when(i == 0)
        def _():
            ext[0:POOL_HALO, :] = jnp.zeros((POOL_HALO, POOL_WIDTH), F32)

        x = u_ref[...]
        ext[POOL_HALO:, :] = x
        win = _pool_lane_window()
        acc = x
        wsum = jnp.zeros_like(x)
        for k in range(1, POOL_HALO):
            acc = acc + ext[POOL_HALO - k:POOL_HALO - k + t, :]
            if k + 1 in POOL_WINDOWS:
                wsum = jnp.where(win == k + 1, acc, wsum)
        pos = i * t + lax.broadcasted_iota(jnp.int32, (t, POOL_WIDTH), 0)
        cnt = jnp.minimum(pos + 1, win).astype(F32)
        y = wsum / cnt - x
        yb = y.astype(BF16)
        y_ref[...] = yb
        out_ref[...] = _dot(yb, w_ref[...]) * sc_ref[...]
        ext[0:POOL_HALO, :] = x[t - POOL_HALO:, :]

    row = pl.BlockSpec((t, POOL_WIDTH), lambda i: (i, 0))
    return pl.pallas_call(
        body, name=name, grid=(s // t,),
        in_specs=[row, pl.BlockSpec((POOL_WIDTH, POOL_WIDTH), lambda i: (0, 0)),
                  pl.BlockSpec((1, POOL_WIDTH), lambda i: (0, 0))],
        out_specs=[row, row],
        out_shape=[jax.ShapeDtypeStruct((s, POOL_WIDTH), F32), jax.ShapeDtypeStruct((s, POOL_WIDTH), BF16)],
        scratch_shapes=[pltpu.VMEM((t + POOL_HALO, POOL_WIDTH), F32)],
        compiler_params=_params(1),
    )(u, w_bd, scale)


def _head_masks():
    lane = lax.broadcasted_iota(jnp.int32, (ATTN_BLOCK, GROUP_WIDTH), 1)
    return [lane // HEAD_DIM == hd for hd in range(GROUP_WIDTH // HEAD_DIM)]


def _stack_heads(a, masks):
    zero = jnp.zeros_like(a)
    return jnp.concatenate([jnp.where(m, a, zero) for m in masks], axis=0)


def _band_bias(first_step):
    rows = ATTN_BLOCK * (GROUP_WIDTH // HEAD_DIM)
    i = lax.broadcasted_iota(jnp.int32, (rows, 2 * ATTN_BLOCK), 0) & (ATTN_BLOCK - 1)
    j = lax.broadcasted_iota(jnp.int32, (rows, 2 * ATTN_BLOCK), 1)
    inner = jnp.where((j >= i) & (j <= i + ATTN_BLOCK), 0.0, NEG_BIG)
    return jnp.where((j < ATTN_BLOCK) & first_step, NEG_BIG, inner), inner


def _column_per_head(a):
    return jnp.concatenate([a[:, hd * HEAD_DIM:hd * HEAD_DIM + 1] for hd in range(GROUP_WIDTH // HEAD_DIM)], axis=0)


def _blocks_per_step(nb):
    return 8 if nb % 8 == 0 else 4 if nb % 4 == 0 else 2 if nb % 2 == 0 else 1


def _attn_fwd(q, k, v, name):
    dil, length, _ = q.shape
    nb = length // ATTN_BLOCK
    qb = _blocks_per_step(nb)

    def body(q_ref, kp_ref, kc_ref, vp_ref, vc_ref, o_ref, lse_ref):
        masks = _head_masks()
        bias = _band_bias(pl.program_id(1) == 0)
        for qi in range(qb):
            here = slice(qi * ATTN_BLOCK, (qi + 1) * ATTN_BLOCK)
            before = slice((qi - 1) * ATTN_BLOCK, qi * ATTN_BLOCK)
            kcat = jnp.concatenate([kp_ref[...] if qi == 0 else kc_ref[before], kc_ref[here]], axis=0)
            vcat = jnp.concatenate([vp_ref[...] if qi == 0 else vc_ref[before], vc_ref[here]], axis=0)
            qs = _stack_heads(q_ref[here], masks)
            sc = _dot_nt(qs, kcat) + bias[min(qi, 1)]
            m = jnp.max(sc, axis=1, keepdims=True)
            e = jnp.exp(sc - m)
            l = jnp.sum(e, axis=1, keepdims=True)
            p = (e / l).astype(BF16)
            lse = m + jnp.log(l)
            o = jnp.zeros((ATTN_BLOCK, GROUP_WIDTH), F32)
            lse_full = jnp.zeros((ATTN_BLOCK, GROUP_WIDTH), F32)
            for hd, msk in enumerate(masks):
                rows = slice(hd * ATTN_BLOCK, (hd + 1) * ATTN_BLOCK)
                o = jnp.where(msk, _dot(p[rows], vcat), o)
                lse_full = jnp.where(msk, lse[rows], lse_full)
            o_ref[here] = o.astype(o_ref.dtype)
            lse_ref[here] = lse_full

    cur = pl.BlockSpec((None, qb * ATTN_BLOCK, GROUP_WIDTH), lambda r, j: (r, j, 0))
    prev = pl.BlockSpec((None, ATTN_BLOCK, GROUP_WIDTH), lambda r, j: (r, jnp.maximum(qb * j - 1, 0), 0))
    return pl.pallas_call(
        body, name=name, grid=(dil, nb // qb),
        in_specs=[cur, prev, cur, prev, cur], out_specs=[cur, cur],
        out_shape=[jax.ShapeDtypeStruct(q.shape, BF16), jax.ShapeDtypeStruct(q.shape, F32)],
        compiler_params=_params(2),
    )(q, k, k, v, v)


def _group_weights(l0, l1, l2):
    m = jnp.maximum(jnp.maximum(l0, l1), l2)
    e0, e1, e2 = jnp.exp(l0 - m), jnp.exp(l1 - m), jnp.exp(l2 - m)
    den = e0 + e1 + e2
    return e0 / den, e1 / den, e2 / den


def _outproj_fwd(h, pool_out, o, lse, w_out, name):
    s = h.shape[0]
    t = _row_tile(s, 512)

    def body(h_ref, po_ref, o0, o1, o2, l0, l1, l2, w_ref, out_ref, a_ref, *stages):
        stages = _pair_stages(stages)
        ov =[_from_residues(r, stages[i], DILATIONS[i]) for i, r in enumerate((o0, o1, o2))]
        lv = [_from_residues(r, stages[3 + i], DILATIONS[i]) for i, r in enumerate((l0, l1, l2))]
        wts = _group_weights(*lv)
        a = jnp.concatenate([po_ref[...]] + [ov[i] * wts[i] for i in range(3)], axis=1).astype(BF16)
        a_ref[...] = a
        out_ref[...] = h_ref[...] + _dot(a, w_ref[...])

    row = lambda w: pl.BlockSpec((t, w), lambda i: (i, 0))
    res = [_residue_spec(dil, t) for dil in DILATIONS]
    return pl.pallas_call(
        body, name=name, grid=(s // t,),
        in_specs=[row(D_MODEL), row(POOL_WIDTH)] + res + res + [pl.BlockSpec((D_MODEL, D_MODEL), lambda i: (0, 0))],
        out_specs=[row(D_MODEL), row(D_MODEL)],
        out_shape=[jax.ShapeDtypeStruct((s, D_MODEL), F32), jax.ShapeDtypeStruct((s, D_MODEL), BF16)],
        scratch_shapes=_stages(t, 6),
        compiler_params=_params(1),
    )(h, pool_out, *o, *lse, w_out)


def _mlp_fwd(h, g, w_up, w_down, name):
    s = h.shape[0]
    t = _row_tile(s, 1024)
    nblk = D_FF // FF_BLOCK

    def body(h_ref, g_ref, wu_ref, wd_ref, out_ref, hn_ref, r_ref, hb_s, acc):
        j = pl.program_id(1)

        @pl.when(j == 0)
        def _():
            _, _, hn = _rms(h_ref[...], g_ref[...])
            hb = hn.astype(BF16)
            hb_s[...] = hb
            hn_ref[...] = hb
            acc[...] = jnp.zeros_like(acc)

        r = jnp.maximum(_dot(hb_s[...], wu_ref[...]), 0.0)
        r_ref[...] = r.astype(BF16)
        acc[...] += _dot((r * r).astype(BF16), wd_ref[...])

        @pl.when(j == nblk - 1)
        def _():
            out_ref[...] = h_ref[...] + acc[...]

    row = pl.BlockSpec((t, D_MODEL), lambda i, j: (i, 0))
    return pl.pallas_call(
        body, name=name, grid=(s // t, nblk),
        in_specs=[row, pl.BlockSpec((1, D_MODEL), lambda i, j: (0, 0)),
                  pl.BlockSpec((None, D_MODEL, FF_BLOCK), lambda i, j: (j, 0, 0)),
                  pl.BlockSpec((None, FF_BLOCK, D_MODEL), lambda i, j: (j, 0, 0))],
        out_specs=[row, row, pl.BlockSpec((t, FF_BLOCK), lambda i, j: (i, j))],
        out_shape=[jax.ShapeDtypeStruct((s, D_MODEL), F32), jax.ShapeDtypeStruct((s, D_MODEL), BF16),
                   jax.ShapeDtypeStruct((s, D_FF), BF16)],
        scratch_shapes=[pltpu.VMEM((t, D_MODEL), BF16), pltpu.VMEM((t, D_MODEL), F32)],
        compiler_params=_params(2),
    )(h, g, w_up, w_down)


def _gate_fwd(h, g, w_gate, p, layer, w_ple, name):
    s = h.shape[0]
    t = _row_tile(s, 512)

    def body(h_ref, g_ref, wg_ref, p_ref, wp_ref, out_ref, hn_ref, gate_ref, pb_ref):
        x = h_ref[...]
        _, _, hn = _rms(x, g_ref[...])
        hb = hn.astype(BF16)
        hn_ref[...] = hb
        gate = 1.0 / (1.0 + jnp.exp(-_dot(hb, wg_ref[...])))
        pb = p_ref[...].astype(BF16)
        pb_ref[...] = pb
        gate_ref[...] = gate.astype(BF16)
        out_ref[...] = x + gate * _dot(pb, wp_ref[...])

    row = lambda w: pl.BlockSpec((t, w), lambda i: (i, 0))
    full = lambda a, b: pl.BlockSpec((a, b), lambda i: (0, 0))
    return pl.pallas_call(
        body, name=name, grid=(s // t,),
        in_specs=[row(D_MODEL), full(1, D_MODEL), full(D_MODEL, D_MODEL),
                  pl.BlockSpec((None, t, PLE_DIM), lambda i: (layer, i, 0)), full(PLE_DIM, D_MODEL)],
        out_specs=[row(D_MODEL), row(D_MODEL), row(D_MODEL), row(PLE_DIM)],
        out_shape=[jax.ShapeDtypeStruct((s, D_MODEL), F32), jax.ShapeDtypeStruct((s, D_MODEL), BF16),
                   jax.ShapeDtypeStruct((s, D_MODEL), BF16), jax.ShapeDtypeStruct((s, PLE_DIM), BF16)],
        compiler_params=_params(1),
    )(h, g, w_gate, p, w_ple)


def _loss_head(h, g, target, name):
    s = h.shape[0]
    t = _row_tile(s, 512)

    def body(h_ref, g_ref, t_ref, loss_ref, dh_ref, dg_ref):
        i = pl.program_id(0)

        @pl.when(i == 0)
        def _():
            loss_ref[...] = jnp.zeros_like(loss_ref)
            dg_ref[...] = jnp.zeros_like(dg_ref)

        gv = g_ref[...]
        n, rstd, y = _rms(h_ref[...], gv)
        err = y - t_ref[...]
        loss_ref[...] += jnp.sum(err * err) * (0.5 / D_MODEL)
        dx, dg = _rms_bwd(err * (1.0 / D_MODEL), n, rstd, gv)
        dh_ref[...] = dx
        dg_ref[...] += dg

    row = pl.BlockSpec((t, D_MODEL), lambda i: (i, 0))
    vec = pl.BlockSpec((1, D_MODEL), lambda i: (0, 0))
    return pl.pallas_call(
        body, name=name, grid=(s // t,),
        in_specs=[row, vec, row],
        out_specs=[pl.BlockSpec((1, LANES), lambda i: (0, 0)), row, vec],
        out_shape=[jax.ShapeDtypeStruct((1, LANES), F32), jax.ShapeDtypeStruct((s, D_MODEL), F32),
                   jax.ShapeDtypeStruct((1, D_MODEL), F32)],
        compiler_params=_params(1),
    )(h, g, target)


def _gate_bwd(dh, gate, pb, w_ple, h, g, w_gate, name, after=()):
    s = h.shape[0]
    t = _row_tile(s, 512)

    def body(dh_ref, gate_ref, pb_ref, wp_ref, h_ref, g_ref, wg_ref, out_ref, dgl_ref, de_ref, dg_ref):
        @pl.when(pl.program_id(0) == 0)
        def _():
            dg_ref[...] = jnp.zeros_like(dg_ref)

        d = dh_ref[...]
        gate = gate_ref[...].astype(F32)
        e = _dot(pb_ref[...], wp_ref[...])
        dgl = (d * e * gate * (1.0 - gate)).astype(BF16)
        dgl_ref[...] = dgl
        de_ref[...] = (d * gate).astype(BF16)
        gv = g_ref[...]
        n, rstd, _ = _rms(h_ref[...], gv)
        dx, dg = _rms_bwd(_dot_nt(dgl, wg_ref[...]), n, rstd, gv)
        out_ref[...] = d + dx
        dg_ref[...] += dg

    row = lambda w: pl.BlockSpec((t, w), lambda i: (i, 0))
    full = lambda a, b: pl.BlockSpec((a, b), lambda i: (0, 0))
    return pl.pallas_call(
        _ordered_after(body, 7, after), name=name, grid=(s // t,),
        in_specs=[row(D_MODEL), row(D_MODEL), row(PLE_DIM), full(PLE_DIM, D_MODEL), row(D_MODEL), full(1, D_MODEL),
                  full(D_MODEL, D_MODEL)] + [pl.BlockSpec(memory_space=pl.ANY)] * len(after),
        out_specs=[row(D_MODEL), row(D_MODEL), row(D_MODEL), full(1, D_MODEL)],
        out_shape=[jax.ShapeDtypeStruct((s, D_MODEL), F32), jax.ShapeDtypeStruct((s, D_MODEL), BF16),
                   jax.ShapeDtypeStruct((s, D_MODEL), BF16), jax.ShapeDtypeStruct((1, D_MODEL), F32)],
        compiler_params=_params(1),
    )(dh, gate, pb, w_ple, h, g, w_gate, *after)


def _mlp_bwd(dh, r, h, g, w_up, w_down, name):
    s = h.shape[0]
    t = _row_tile(s, 1024)
    nblk = D_FF // FF_BLOCK

    def body(dh_ref, r_ref, h_ref, g_ref, wu_ref, wd_ref, out_ref, dup_ref, dg_ref, db_s, acc):
        i, j = pl.program_id(0), pl.program_id(1)

        @pl.when((i == 0) & (j == 0))
        def _():
            dg_ref[...] = jnp.zeros_like(dg_ref)

        @pl.when(j == 0)
        def _():
            db_s[...] = dh_ref[...].astype(BF16)
            acc[...] = jnp.zeros_like(acc)

        dup = (_dot_nt(db_s[...], wd_ref[...]) * (2.0 * r_ref[...].astype(F32))).astype(BF16)
        dup_ref[...] = dup
        acc[...] += _dot_nt(dup, wu_ref[...])

        @pl.when(j == nblk - 1)
        def _():
            gv = g_ref[...]
            n, rstd, _ = _rms(h_ref[...], gv)
            dx, dg = _rms_bwd(acc[...], n, rstd, gv)
            out_ref[...] = dh_ref[...] + dx
            dg_ref[...] += dg

    row = pl.BlockSpec((t, D_MODEL), lambda i, j: (i, 0))
    vec = pl.BlockSpec((1, D_MODEL), lambda i, j: (0, 0))
    blk = pl.BlockSpec((t, FF_BLOCK), lambda i, j: (i, j))
    return pl.pallas_call(
        body, name=name, grid=(s // t, nblk),
        in_specs=[row, blk, row, vec,
                  pl.BlockSpec((None, D_MODEL, FF_BLOCK), lambda i, j: (j, 0, 0)),
                  pl.BlockSpec((None, FF_BLOCK, D_MODEL), lambda i, j: (j, 0, 0))],
        out_specs=[row, blk, vec, row],
        out_shape=[jax.ShapeDtypeStruct((s, D_MODEL), F32), jax.ShapeDtypeStruct((s, D_FF), BF16),
                   jax.ShapeDtypeStruct((1, D_MODEL), F32), jax.ShapeDtypeStruct((s, D_MODEL), BF16)],
        scratch_shapes=[pltpu.VMEM((t, D_MODEL), F32)],
        compiler_params=_params(2),
    )(dh, r, h, g, w_up, w_down)


def _outproj_bwd(dh, w_out, o, lse, ones_bd, name):
    s = dh.shape[0]
    t = _row_tile(s, 512)

    def body(dh_ref, w_ref, o0, o1, o2, l0, l1, l2, bd_ref, dp_ref, do0, do1, do2, de0, de1, de2, dhb_ref, *stages):
        stages = _pair_stages(stages)
        dhb = dh_ref[...].astype(BF16)
        dhb_ref[...] = dhb
        da = _dot_nt(dhb, w_ref[...])
        dp_ref[...] = da[:, 0:POOL_WIDTH]
        ov =[_from_residues(r, stages[i], DILATIONS[i]) for i, r in enumerate((o0, o1, o2))]
        lv = [_from_residues(r, stages[3 + i], DILATIONS[i]) for i, r in enumerate((l0, l1, l2))]
        wts = _group_weights(*lv)
        bd = bd_ref[...]
        cbar = jnp.zeros((t, GROUP_WIDTH), F32)
        for grp, do_ref in enumerate((do0, do1, do2)):
            lo = POOL_WIDTH + grp * GROUP_WIDTH
            dag = da[:, lo:lo + GROUP_WIDTH]
            _to_residues(dag * wts[grp], stages[6 + grp], do_ref, DILATIONS[grp])
            prod = dag * ov[grp]
            hi = prod.astype(BF16)
            low = (prod - hi.astype(F32)).astype(BF16)
            cbar = cbar + wts[grp] * (_dot(hi, bd) + _dot(low, bd))
        for grp, de_ref in enumerate((de0, de1, de2)):
            _to_residues(wts[grp] * cbar, stages[9 + grp], de_ref, DILATIONS[grp])

    row = lambda w: pl.BlockSpec((t, w), lambda i: (i, 0))
    full = lambda a, b: pl.BlockSpec((a, b), lambda i: (0, 0))
    res = [_residue_spec(dil, t) for dil in DILATIONS]
    return pl.pallas_call(
        body, name=name, grid=(s // t,),
        in_specs=[row(D_MODEL), full(D_MODEL, D_MODEL)] + res + res + [full(GROUP_WIDTH, GROUP_WIDTH)],
        out_specs=[row(POOL_WIDTH)] + res + res + [row(D_MODEL)],
        out_shape=[jax.ShapeDtypeStruct((s, POOL_WIDTH), F32)] + [_residue_shape(dil, s, BF16) for dil in DILATIONS]
        + [_residue_shape(dil, s, F32) for dil in DILATIONS] + [jax.ShapeDtypeStruct((s, D_MODEL), BF16)],
        scratch_shapes=_stages(t, 12),
        compiler_params=_params(1),
    )(dh, w_out, *o, *lse, ones_bd)


def _attn_bwd(q, k, v, do, lse, deff, name, after=()):
    dil, length, _ = q.shape
    nb = length // ATTN_BLOCK
    qb = _blocks_per_step(nb)
    nj = nb // qb
    tail = slice((qb - 1) * ATTN_BLOCK, qb * ATTN_BLOCK)

    def body(q_ref, kp_ref, kc_ref, vp_ref, vc_ref, do_ref, lse_ref, de_ref, dq_ref, dk_ref, dv_ref, ck, cv):
        j = pl.program_id(1)

        @pl.when(j < nj)
        def _():
            masks = _head_masks()
            bias = _band_bias(j == 0)
            dkc, dvc = [], []
            for qi in range(qb):
                here = slice(qi * ATTN_BLOCK, (qi + 1) * ATTN_BLOCK)
                before = slice((qi - 1) * ATTN_BLOCK, qi * ATTN_BLOCK)
                kcat = jnp.concatenate([kp_ref[...] if qi == 0 else kc_ref[before], kc_ref[here]], axis=0)
                vcat = jnp.concatenate([vp_ref[...] if qi == 0 else vc_ref[before], vc_ref[here]], axis=0)
                qs = _stack_heads(q_ref[here], masks)
                dos = _stack_heads(do_ref[here], masks)
                sc = _dot_nt(qs, kcat) + bias[min(qi, 1)]
                p = jnp.exp(sc - _column_per_head(lse_ref[here]))
                ds = (p * (_dot_nt(dos, vcat) - _column_per_head(de_ref[here]))).astype(BF16)
                dq = jnp.zeros((ATTN_BLOCK, GROUP_WIDTH), F32)
                for hd, msk in enumerate(masks):
                    dq = jnp.where(msk, _dot(ds[hd * ATTN_BLOCK:(hd + 1) * ATTN_BLOCK], kcat), dq)
                dq_ref[here] = dq.astype(dq_ref.dtype)
                dkc.append(_dot_tn(ds, qs))
                dvc.append(_dot_tn(p.astype(BF16), dos))

            for out_ref, carry, parts in ((dk_ref, ck, dkc), (dv_ref, cv, dvc)):
                @pl.when(j > 0)
                def _():
                    if qb > 1:
                        out_ref[0:(qb - 1) * ATTN_BLOCK] = carry[0:(qb - 1) * ATTN_BLOCK].astype(out_ref.dtype)
                    out_ref[tail] = (carry[tail] + parts[0][0:ATTN_BLOCK]).astype(out_ref.dtype)

                for qi in range(qb - 1):
                    carry[qi * ATTN_BLOCK:(qi + 1) * ATTN_BLOCK] = parts[qi][ATTN_BLOCK:] + parts[qi + 1][0:ATTN_BLOCK]
                carry[tail] = parts[qb - 1][ATTN_BLOCK:]

        @pl.when(j == nj)
        def _():
            dk_ref[...] = ck[...].astype(dk_ref.dtype)
            dv_ref[...] = cv[...].astype(dv_ref.dtype)

    step = lambda j: jnp.minimum(j, nj - 1)
    cur = pl.BlockSpec((None, qb * ATTN_BLOCK, GROUP_WIDTH), lambda r, j: (r, step(j), 0))
    prev = pl.BlockSpec((None, ATTN_BLOCK, GROUP_WIDTH), lambda r, j: (r, jnp.maximum(qb * step(j) - 1, 0), 0))
    late = pl.BlockSpec((None, qb * ATTN_BLOCK, GROUP_WIDTH), lambda r, j: (r, jnp.maximum(j - 1, 0), 0))
    return pl.pallas_call(
        _ordered_after(body, 8, after), name=name, grid=(dil, nj + 1),
        in_specs=[cur, prev, cur, prev, cur, cur, cur, cur] + [pl.BlockSpec(memory_space=pl.ANY)] * len(after),
        out_specs=[cur, late, late],
        out_shape=[jax.ShapeDtypeStruct(q.shape, BF16)] * 3,
        scratch_shapes=[pltpu.VMEM((qb * ATTN_BLOCK, GROUP_WIDTH), F32)] * 2,
        compiler_params=_params(2),
    )(q, k, k, v, v, do, lse, deff, *after)


def _pool_bwd(dpool, y, w_bd, scale, name, after=()):
    s = dpool.shape[0]
    t = _row_tile(s, 512)
    nt = s // t

    def body(dp_ref, y_ref, w_ref, sc_ref, du_ref, dw_ref, dsc_ref, ext):
        i = pl.program_id(0)

        @pl.when(i == 0)
        def _():
            ext[t:, :] = jnp.zeros((POOL_HALO, POOL_WIDTH), F32)
            dw_ref[...] = jnp.zeros_like(dw_ref)
            dsc_ref[...] = jnp.zeros_like(dsc_ref)

        dp = dp_ref[...]
        yb = y_ref[...]
        w = w_ref[...]
        dsc_ref[...] += jnp.sum(dp * _dot(yb, w), axis=0, keepdims=True)
        dyo = (dp * sc_ref[...]).astype(BF16)
        dw_ref[...] += _dot_tn(yb, dyo)
        dy = _dot_nt(dyo, w)
        win = _pool_lane_window()
        pos = (nt - 1 - i) * t + lax.broadcasted_iota(jnp.int32, (t, POOL_WIDTH), 0)
        gq = dy / jnp.minimum(pos + 1, win).astype(F32)
        ext[0:t, :] = gq
        acc = gq
        wsum = jnp.zeros_like(gq)
        for k in range(1, POOL_HALO):
            acc = acc + ext[k:k + t, :]
            if k + 1 in POOL_WINDOWS:
                wsum = jnp.where(win == k + 1, acc, wsum)
        du_ref[...] = wsum - dy
        ext[t:, :] = gq[0:POOL_HALO, :]

    rev = pl.BlockSpec((t, POOL_WIDTH), lambda i: (nt - 1 - i, 0))
    full = lambda a, b: pl.BlockSpec((a, b), lambda i: (0, 0))
    return pl.pallas_call(
        _ordered_after(body, 4, after), name=name, grid=(nt,),
        in_specs=[rev, rev, full(POOL_WIDTH, POOL_WIDTH), full(1, POOL_WIDTH)]
        + [pl.BlockSpec(memory_space=pl.ANY)] * len(after),
        out_specs=[rev, full(POOL_WIDTH, POOL_WIDTH), full(1, POOL_WIDTH)],
        out_shape=[jax.ShapeDtypeStruct((s, POOL_WIDTH), F32), jax.ShapeDtypeStruct((POOL_WIDTH, POOL_WIDTH), F32),
                   jax.ShapeDtypeStruct((1, POOL_WIDTH), F32)],
        scratch_shapes=[pltpu.VMEM((t + POOL_HALO, POOL_WIDTH), F32)],
        compiler_params=_params(1),
    )(dpool, y, w_bd, scale, *after)


def _normproj_bwd(dh, du, dq, dk, dv, rc, rsa, rsb, w_in, h, g, name):
    s = h.shape[0]
    t = _row_tile(s, 512)

    def body(dh_ref, du_ref, q0, q1, q2, k0, k1, k2, v0, v1, v2, c_ref, sa_ref, sb_ref, w_ref, h_ref, g_ref,
             out_ref, dz_ref, dg_ref, *stages):
        @pl.when(pl.program_id(0) == 0)
        def _():
            dg_ref[...] = jnp.zeros_like(dg_ref)

        c, sa, sb = c_ref[...], sa_ref[...], sb_ref[...]

        def unrot(a, scale):
            halves = [_rot_t(a[:, hf * LANES:(hf + 1) * LANES] * scale, c, sa, sb) for hf in range(2)]
            return jnp.concatenate(halves, axis=1)

        staged = _pair_stages(stages)
        tok = lambda refs, base: [_from_residues(r, staged[base + i], DILATIONS[i]) for i, r in enumerate(refs)]
        chunks = [du_ref[...]]
        chunks += [unrot(a, HEAD_DIM ** -0.5) for a in tok((q0, q1, q2), 0)]
        chunks += [unrot(a, 1.0) for a in tok((k0, k1, k2), 3)]
        chunks += tok((v0, v1, v2), 6)
        acc = jnp.zeros((t, D_MODEL), F32)
        for ci, ch in enumerate(chunks):
            cols = slice(ci * GROUP_WIDTH, (ci + 1) * GROUP_WIDTH)
            cb = ch.astype(BF16)
            dz_ref[:, cols] = cb
            acc = acc + _dot_nt(cb, w_ref[:, cols])
        gv = g_ref[...]
        n, rstd, _ = _rms(h_ref[...], gv)
        dx, dg = _rms_bwd(acc, n, rstd, gv)
        out_ref[...] = dh_ref[...] + dx
        dg_ref[...] += dg

    row = lambda w: pl.BlockSpec((t, w), lambda i: (i, 0))
    vec = pl.BlockSpec((1, D_MODEL), lambda i: (0, 0))
    res = [_residue_spec(dil, t) for dil in DILATIONS]
    return pl.pallas_call(
        body, name=name, grid=(s // t,),
        in_specs=[row(D_MODEL), row(POOL_WIDTH)] + res * 3 + [row(LANES)] * 3
        + [pl.BlockSpec((D_MODEL, N_IN), lambda i: (0, 0)), row(D_MODEL), vec],
        out_specs=[row(D_MODEL), row(N_IN), vec],
        out_shape=[jax.ShapeDtypeStruct((s, D_MODEL), F32), jax.ShapeDtypeStruct((s, N_IN), BF16),
                   jax.ShapeDtypeStruct((1, D_MODEL), F32)],
        scratch_shapes=_stages(t, 9),
        compiler_params=_params(1),
    )(dh, du, *dq, *dk, *dv, rc, rsa, rsb, w_in, h, g)


def _matmul_tn(a, b, name, *, square_a=False, tn=None, blocked_out=False):
    s, m = a.shape
    n = b.shape[1]
    tk = _row_tile(s, 2048)
    tm = min(m, 1024)
    tn = tn or min(n, 1024)
    assert m % tm == 0 and n % tn == 0
    nk = s // tk
    nsub = tn // FF_BLOCK if blocked_out else 1

    def body(a_ref, b_ref, o_ref, ob_ref, acc):
        k = pl.program_id(2)

        @pl.when(k == 0)
        def _():
            acc[...] = jnp.zeros_like(acc)

        av = a_ref[...]
        if square_a:
            av = av.astype(F32)
            av = av * av
        acc[...] += _dot_tn(av.astype(BF16), b_ref[...].astype(BF16))

        @pl.when(k == nk - 1)
        def _():
            if blocked_out:
                for sub in range(nsub):
                    cols = slice(sub * FF_BLOCK, (sub + 1) * FF_BLOCK)
                    o_ref[sub] = acc[:, cols]
                    ob_ref[sub] = acc[:, cols].astype(BF16)
            else:
                o_ref[...] = acc[...]
                ob_ref[...] = acc[...].astype(BF16)

    if blocked_out:
        shape = (n // FF_BLOCK, m, FF_BLOCK)
        out_spec = pl.BlockSpec((nsub, tm, FF_BLOCK), lambda i, j, k: (j, i, 0))
    else:
        shape = (m, n)
        out_spec = pl.BlockSpec((tm, tn), lambda i, j, k: (i, j))
    return pl.pallas_call(
        body, name=name, grid=(m // tm, n // tn, nk),
        in_specs=[pl.BlockSpec((tk, tm), lambda i, j, k: (k, i)), pl.BlockSpec((tk, tn), lambda i, j, k: (k, j))],
        out_specs=[out_spec, out_spec],
        out_shape=[jax.ShapeDtypeStruct(shape, F32), jax.ShapeDtypeStruct(shape, BF16)],
        scratch_shapes=[pltpu.VMEM((tm, tn), F32)],
        compiler_params=_params(3),
    )(a, b)


def _adamw_math(w, g, m, v):
    m = ADAM_B1 * m + (1.0 - ADAM_B1) * g
    v = ADAM_B2 * v + (1.0 - ADAM_B2) * (g * g)
    m_hat = m / (1.0 - ADAM_B1 ** ADAM_STEP)
    v_hat = v / (1.0 - ADAM_B2 ** ADAM_STEP)
    delta = -ADAM_LR * (m_hat / (jnp.sqrt(v_hat) + ADAM_EPS) + ADAM_WD * w)
    return delta, m, v


def _adamw_sharded(w, m, v, own, recv0, recv1, name):
    _, rows, cols = w.shape
    t = _row_tile(rows, 256)

    def body(w_ref, m_ref, v_ref, own_ref, r0_ref, r1_ref, g_ref, d_ref, nm_ref, nv_ref):
        layer0 = pl.program_id(0) == 0
        g = own_ref[...]
        for k in range(N_DEV - 1):
            g = g + jnp.where(layer0, r0_ref[k], r1_ref[k]).astype(F32)
        g_ref[...] = g
        d_ref[...], nm_ref[...], nv_ref[...] = _adamw_math(w_ref[...], g, m_ref[...], v_ref[...])

    blk = pl.BlockSpec((None, t, cols), lambda l, i: (l, i, 0))
    recv = lambda layer: pl.BlockSpec((N_DEV - 1, t, cols), lambda l, i: (0, jnp.where(l == layer, i, 0), 0))
    return pl.pallas_call(
        body, name=name, grid=(2, rows // t),
        in_specs=[blk, blk, blk, blk, recv(0), recv(1)], out_specs=[blk] * 4,
        out_shape=[jax.ShapeDtypeStruct(w.shape, F32)] * 4,
        compiler_params=_params(2),
    )(w, m, v, own, recv0, recv1)


def _adamw_packed(w, g, m, v, name):
    def body(w_ref, g_ref, m_ref, v_ref, d_ref, nm_ref, nv_ref):
        d_ref[...], nm_ref[...], nv_ref[...] = _adamw_math(w_ref[...], g_ref[...], m_ref[...], v_ref[...])

    return pl.pallas_call(
        body, name=name, out_shape=[jax.ShapeDtypeStruct(w.shape, F32)] * 3,
        compiler_params=pltpu.CompilerParams(vmem_limit_bytes=VMEM_LIMIT),
    )(w, g, m, v)


def _peer(k):
    x, y, c = lax.axis_index("x"), lax.axis_index("y"), lax.axis_index("c")
    return (1 - x if k & 4 else x, 1 - y if k & 2 else y, 1 - c if k & 1 else c)


def _linear(dev):
    return 4 * dev[0] + 2 * dev[1] + dev[2]


HBM_SPEC = pl.BlockSpec(memory_space=pltpu.HBM)
SEM_SPEC = pl.BlockSpec(memory_space=pltpu.SEMAPHORE)
ANY_SPEC = pl.BlockSpec(memory_space=pl.ANY)
EFFECT = pltpu.SideEffectType.DATAFLOW_SIDE_EFFECTING


def _in_hbm(a):
    return pltpu.with_memory_space_constraint(a, pltpu.HBM)


class _Exchange:
    def __init__(self, name, groups, scatter, after=()):
        self.name, self.scatter = name, scatter
        self.sizes = sizes = [len(g) for g in groups]
        srcs = [a for g in groups for a in g]
        n, ng = len(srcs), len(groups)
        lead = (N_DEV - 1,) if scatter else (N_DEV,)
        shapes = [lead + (a.shape[1:] if scatter else a.shape) for a in srcs]
        lands = [lax.empty(sh, a.dtype) for sh, a in zip(shapes, srcs)]
        offsets = [sum(sizes[:gi]) for gi in range(ng)]
        copy = self._copy

        def body(*refs):
            src, land = refs[:n], refs[n:2 * n]
            sems = refs[2 * n + len(after):2 * n + len(after) + 2 * ng]
            token = refs[-1]
            for gi in range(ng):
                for wi in range(sizes[gi]):
                    w = offsets[gi] + wi
                    for k in range(1, N_DEV):
                        copy(src[w], land[w], sems[2 * gi], sems[2 * gi + 1], wi, k).start()
            token[...] = jnp.zeros_like(token)

        sem_shapes = [pltpu.SemaphoreType.DMA((7 * sz,)) for sz in sizes for _ in range(2)]
        outs = pl.pallas_call(
            body, name=name + "_start",
            in_specs=[HBM_SPEC] * (2 * n) + [ANY_SPEC] * len(after),
            out_specs=[SEM_SPEC] * (2 * ng) + [HBM_SPEC] * (2 * n) + [pl.BlockSpec(memory_space=pltpu.VMEM)],
            out_shape=sem_shapes + [pltpu.HBM(a.shape, a.dtype) for a in srcs + lands]
            + [jax.ShapeDtypeStruct((8, LANES), F32)],
            input_output_aliases={i: 2 * ng + i for i in range(2 * n)},
            compiler_params=pltpu.CompilerParams(has_side_effects=EFFECT),
        )(*[_in_hbm(a) for a in srcs + lands], *after)
        self.sems = [outs[2 * gi:2 * gi + 2] for gi in range(ng)]
        thru = outs[2 * ng:2 * ng + 2 * n]
        self.srcs = [thru[offsets[gi]:offsets[gi] + sizes[gi]] for gi in range(ng)]
        self.lands = [thru[n + offsets[gi]:n + offsets[gi] + sizes[gi]] for gi in range(ng)]
        self.token = outs[-1]

    def _copy(self, src, land, send_sems, recv_sems, wi, k):
        to = _peer(k)
        if self.scatter:
            src_ref, dst_ref = src.at[_linear(to)], land.at[k - 1]
        else:
            src_ref, dst_ref = src, land.at[_linear(_peer(0))]
        return pltpu.make_async_remote_copy(
            src_ref=src_ref, dst_ref=dst_ref, send_sem=send_sems.at[7 * wi + k - 1],
            recv_sem=recv_sems.at[7 * wi + k - 1], device_id=to, device_id_type=MESH)

    def wait(self, gi, after):
        n = self.sizes[gi]
        copy = self._copy

        def body(*refs):
            src, land = refs[:n], refs[n:2 * n]
            send_sems, recv_sems = refs[2 * n], refs[2 * n + 1]
            for wi in range(n):
                for k in range(1, N_DEV):
                    cp = copy(src[wi], land[wi], send_sems, recv_sems, wi, k)
                    cp.wait_send()
                    cp.wait_recv()

        arrays = list(self.srcs[gi]) + list(self.lands[gi])
        outs = pl.pallas_call(
            body, name=f"{self.name}_wait{gi}",
            in_specs=[HBM_SPEC] * (2 * n) + [SEM_SPEC, SEM_SPEC] + [ANY_SPEC] * len(after),
            out_specs=[HBM_SPEC] * (2 * n),
            out_shape=[pltpu.HBM(a.shape, a.dtype) for a in arrays],
            input_output_aliases={i: i for i in range(2 * n)},
            compiler_params=pltpu.CompilerParams(has_side_effects=EFFECT),
        )(*arrays, *self.sems[gi], *after)
        return outs[:n], outs[n:]


def _allreduce_packed(g):
    rows = g.shape[0]

    def body(g_ref, out_ref, buf, send_sems, recv_sems):
        me = _linear(_peer(0))
        buf[me] = g_ref[...]
        copies = []
        for k in range(1, N_DEV):
            copies.append(pltpu.make_async_remote_copy(
                src_ref=g_ref, dst_ref=buf.at[me], send_sem=send_sems.at[k - 1], recv_sem=recv_sems.at[k - 1],
                device_id=_peer(k), device_id_type=MESH))
        for cp in copies:
            cp.start()
        for k in range(1, N_DEV):
            pltpu.make_async_remote_copy(
                src_ref=g_ref, dst_ref=buf.at[_linear(_peer(k))], send_sem=send_sems.at[k - 1],
                recv_sem=recv_sems.at[k - 1], device_id=_peer(k), device_id_type=MESH).wait_recv()
        for cp in copies:
            cp.wait_send()
        total = buf[0]
        for d in range(1, N_DEV):
            total = total + buf[d]
        out_ref[...] = total

    return pl.pallas_call(
        body, name="allreduce_small",
        in_specs=[pl.BlockSpec(memory_space=pltpu.VMEM)], out_specs=pl.BlockSpec(memory_space=pltpu.VMEM),
        out_shape=jax.ShapeDtypeStruct(g.shape, F32),
        scratch_shapes=[pltpu.VMEM((N_DEV, rows, g.shape[1]), F32), pltpu.SemaphoreType.DMA((7,)),
                        pltpu.SemaphoreType.DMA((7,))],
        compiler_params=pltpu.CompilerParams(vmem_limit_bytes=VMEM_LIMIT),
    )(g)


def _rotary_tables(positions):
    rot_dim = HEAD_DIM // 4
    inv_freq = ROPE_THETA ** (-jnp.arange(0, rot_dim, 2, dtype=F32) / rot_dim)
    dim = jnp.arange(LANES) % HEAD_DIM
    ang = positions.astype(F32)[:, None] * inv_freq[dim % ROT_SHIFT][None, :]
    cos, sin = jnp.cos(ang), jnp.sin(ang)
    first, second = dim < ROT_SHIFT, (dim >= ROT_SHIFT) & (dim < rot_dim)
    c = jnp.where(first | second, cos, 1.0)
    sa = jnp.where(second, sin, 0.0)
    sb = jnp.where(first, -sin, 0.0)
    return [c, sa, sb]


def _block_diag(pool_w):
    gc = pool_w.shape[-1]
    out = jnp.zeros((POOL_WIDTH, POOL_WIDTH), pool_w.dtype)
    for grp in range(pool_w.shape[0]):
        out = lax.dynamic_update_slice(out, pool_w[grp], (grp * gc, grp * gc))
    return out


def _diag_blocks(a):
    gc = POOL_WIDTH // len(POOL_WINDOWS)
    return jnp.stack([a[grp * gc:(grp + 1) * gc, grp * gc:(grp + 1) * gc] for grp in range(len(POOL_WINDOWS))])


def _local_step(x, p, positions, loss_target, norm1, pool_w, pool_scale, norm2, norm3, final_norm, weights, send):
    rc, rsa, rsb = _rotary_tables(positions)
    ones_bd = _block_diag(jnp.ones((4, HEAD_DIM, HEAD_DIM), BF16))
    saved = []
    h = x
    for i in range(2):
        tag = f"_l{i}"
        g1, g2, g3 = norm1[i:i + 1], norm2[i:i + 1], norm3[i:i + 1]
        w_bd = _block_diag(pool_w[i]).astype(BF16)
        scale = pool_scale[i:i + 1]
        w_in = weights(i, "in", (h, rc, rsa, rsb, w_bd))
        hn1, u, *qkv = _normproj_fwd(h, g1, w_in, rc, rsa, rsb, "normproj_fwd" + tag)
        qkv = [qkv[3 * grp:3 * grp + 3] for grp in range(3)]
        pool_out, y = _pool_fwd(u, w_bd, scale, "pool_fwd" + tag)
        o, lse = zip(*[_attn_fwd(*qkv[grp], f"attn_fwd{tag}_g{grp}") for grp in range(3)])
        w_out = weights(i, "out", (pool_out, *o))
        h1, a = _outproj_fwd(h, pool_out, o, lse, w_out, "outproj_fwd" + tag)
        w_up, w_down, w_gate, w_ple = weights(i, "rest", (h1,))
        h2, hn2, r = _mlp_fwd(h1, g2, w_up, w_down, "mlp_fwd" + tag)
        h3, hn3, gate, pb = _gate_fwd(h2, g3, w_gate, p, i, w_ple, "gate_fwd" + tag)
        saved.append(dict(h0=h, hn1=hn1, qkv=qkv, y=y, o=o, lse=lse, a=a, h1=h1, hn2=hn2, r=r, h2=h2,
                          hn3=hn3, gate=gate, pb=pb, w_bd=w_bd, scale=scale, g1=g1, g2=g2, g3=g3,
                          w_in=w_in, w_out=w_out, w_up=w_up, w_down=w_down, w_gate=w_gate, w_ple=w_ple))
        h = h3
    loss, dh, d_final = _loss_head(h, final_norm.reshape(1, D_MODEL), loss_target, "loss_head")

    grads = [None, None]
    sent = ()
    for i in (1, 0):
        tag = f"_l{i}"
        sv = saved[i]
        dh2, dgl, de, dg3 = _gate_bwd(dh, sv["gate"], sv["pb"], sv["w_ple"], sv["h2"], sv["g3"], sv["w_gate"],
                                      "gate_bwd" + tag, after=sent)
        dw_gate = _matmul_tn(sv["hn3"], dgl, "dw_gate" + tag)
        dw_ple = _matmul_tn(sv["pb"], de, "dw_ple" + tag)
        dh1, dup, dg2, dh2b = _mlp_bwd(dh2, sv["r"], sv["h1"], sv["g2"], sv["w_up"], sv["w_down"], "mlp_bwd" + tag)
        dw_down = _matmul_tn(sv["r"], dh2b, "dw_down" + tag, square_a=True)
        dw_up = _matmul_tn(sv["hn2"], dup, "dw_up" + tag, blocked_out=True)
        dpool, do0, do1, do2, de0, de1, de2, dh1b = _outproj_bwd(dh1, sv["w_out"], sv["o"], sv["lse"], ones_bd,
                                                                 "outproj_bwd" + tag)
        dw_out = _matmul_tn(sv["a"], dh1b, "dw_out" + tag)
        sent = send(i, "main", dict(w_gate=dw_gate, w_ple=dw_ple, w_down=dw_down, w_up=dw_up, w_out=dw_out))
        dqkv = [_attn_bwd(*sv["qkv"][grp], do_g, sv["lse"][grp], de_g, f"attn_bwd{tag}_g{grp}", after=sent)
                for grp, (do_g, de_g) in enumerate(((do0, de0), (do1, de1), (do2, de2)))]
        dq, dk, dv = zip(*dqkv)
        du, dw_bd, dscale = _pool_bwd(dpool, sv["y"], sv["w_bd"], sv["scale"], "pool_bwd" + tag, after=sent)
        dh, dz, dg1 = _normproj_bwd(dh1, du, dq, dk, dv, rc, rsa, rsb, sv["w_in"], sv["h0"], sv["g1"],
                                    "normproj_bwd" + tag)
        dw_in = _matmul_tn(sv["hn1"], dz, "dw_in" + tag, tn=N_IN // 2)
        sent = send(i, "in", dict(w_in=dw_in))
        grads[i] = dict(norm1=dg1, norm2=dg2, norm3=dg3, pool_w=_diag_blocks(dw_bd), pool_scale=dscale)
    return loss, dh, grads, d_final, sent


def _pack_small(norm1, norm2, norm3, final_norm, pool_scale, pool_w):
    scale_row = jnp.concatenate([pool_scale.reshape(1, 2 * POOL_WIDTH), jnp.zeros((1, D_MODEL - 2 * POOL_WIDTH), F32)], axis=1)
    return jnp.concatenate([norm1, norm2, norm3, final_norm.reshape(1, D_MODEL), scale_row,
                            pool_w.reshape(32, D_MODEL)], axis=0)


def _unpack_small(a):
    return dict(norm1=a[0:2], norm2=a[2:4], norm3=a[4:6], final_norm=a[6], pool_scale=a[7, 0:2 * POOL_WIDTH].reshape(2, POOL_WIDTH),
                pool_w=a[8:40].reshape(2, 4, HEAD_DIM, HEAD_DIM))


def _chunks_cols(a, cols):
    return a.reshape(a.shape[0], N_DEV, cols).transpose(1, 0, 2)


def _chunks_rows(a, rows):
    return a.reshape(N_DEV, rows, a.shape[1])


BIG = ("w_in", "w_out", "w_up", "w_down", "w_gate", "w_ple")
SMALL = ("norm1", "norm2", "norm3", "final_norm", "pool_scale", "pool_w")
ORDER = ("norm1", "w_in", "pool_w", "pool_scale", "w_out", "norm2", "w_up", "w_down", "norm3", "w_gate", "w_ple",
         "final_norm")


def kernel(x, p, positions, norm1, w_in, pool_w, pool_scale, w_out, norm2, w_up, w_down, norm3, w_gate, w_ple, final_norm, loss_target, m_norm1, m_w_in, m_pool_w, m_pool_scale, m_w_out, m_norm2, m_w_up, m_w_down, m_norm3, m_w_gate, m_w_ple, m_final_norm, v_norm1, v_w_in, v_pool_w, v_pool_scale, v_w_out, v_norm2, v_w_up, v_w_down, v_norm3, v_w_gate, v_w_ple, v_final_norm):
    w = dict(norm1=norm1, w_in=w_in, pool_w=pool_w, pool_scale=pool_scale, w_out=w_out, norm2=norm2, w_up=w_up,
             w_down=w_down, norm3=norm3, w_gate=w_gate, w_ple=w_ple, final_norm=final_norm)
    m = dict(norm1=m_norm1, w_in=m_w_in, pool_w=m_pool_w, pool_scale=m_pool_scale, w_out=m_w_out, norm2=m_norm2,
             w_up=m_w_up, w_down=m_w_down, norm3=m_norm3, w_gate=m_w_gate, w_ple=m_w_ple, final_norm=m_final_norm)
    v = dict(norm1=v_norm1, w_in=v_w_in, pool_w=v_pool_w, pool_scale=v_pool_scale, w_out=v_w_out, norm2=v_norm2,
             w_up=v_w_up, w_down=v_w_down, norm3=v_norm3, w_gate=v_w_gate, w_ple=v_w_ple, final_norm=v_final_norm)
    seq = x.shape[1]

    bf = {n: [w[n][layer].astype(BF16) for layer in range(2)] for n in BIG}
    rest = ("w_up", "w_down", "w_gate", "w_ple")
    me = 4 * lax.axis_index("x") + 2 * lax.axis_index("y") + lax.axis_index("c")
    gathers = [_Exchange("gather_l0", [[bf["w_in"][0]], [bf["w_out"][0]], [bf[n][0] for n in rest]], scatter=False)]
    unpack = dict(w_in=lambda a: a.transpose(1, 0, 2).reshape(D_MODEL, N_IN),
                  w_out=lambda a: a.reshape(D_MODEL, D_MODEL), w_gate=lambda a: a.reshape(D_MODEL, D_MODEL),
                  w_ple=lambda a: a.transpose(1, 0, 2).reshape(PLE_DIM, D_MODEL), w_up=lambda a: a, w_down=lambda a: a)
    layer1 = {}

    def gathered(exchange, group, names, after):
        shards, lands = exchange.wait(group, after)
        return {n: lax.dynamic_update_slice_in_dim(land, shard[None], me, axis=0)
                for n, shard, land in zip(names, shards, lands)}

    def weights(layer, part, after):
        names = dict(out=("w_out",), rest=rest)[part] if part != "in" else ("w_in",)
        if layer == 0:
            if part == "out":
                gathers.append(_Exchange("gather_l1", [[bf[n][1] for n in BIG]], scatter=False, after=after))
            got = gathered(gathers[0], ("in", "out", "rest").index(part), names, after)
        else:
            if not layer1:
                layer1.update(gathered(gathers[1], 0, BIG, after))
            got = layer1
        full = [unpack[n](got[n]) for n in names]
        return full if part == "rest" else full[0]

    to_chunks = dict(w_in=lambda a: _chunks_cols(a, N_IN // N_DEV), w_out=lambda a: _chunks_rows(a, D_MODEL // N_DEV),
                     w_up=lambda a: a, w_down=lambda a: _chunks_rows(a, FF_BLOCK),
                     w_gate=lambda a: _chunks_rows(a, D_MODEL // N_DEV), w_ple=lambda a: _chunks_cols(a, D_MODEL // N_DEV))
    own = {n: [None, None] for n in BIG}
    scatters = {}

    def send(layer, part, grads):
        for n, (g32, _) in grads.items():
            own[n][layer] = lax.dynamic_index_in_dim(to_chunks[n](g32), me, axis=0, keepdims=False)
        ex = _Exchange(f"scatter_{part}_l{layer}", [[to_chunks[n](g16) for n, (_, g16) in grads.items()]], scatter=True)
        scatters[layer, part] = (tuple(grads), ex)
        return (ex.token,)

    loss, dx, grads, d_final, sent = _local_step(
        x.reshape(seq, D_MODEL), p.reshape(2, seq, PLE_DIM), positions.reshape(seq), loss_target.reshape(seq, D_MODEL),
        norm1, pool_w, pool_scale, norm2, norm3, final_norm, weights, send)

    small_g = _pack_small(
        *[jnp.concatenate([grads[0][n], grads[1][n]], axis=0) for n in ("norm1", "norm2", "norm3")], d_final.reshape(D_MODEL),
        jnp.concatenate([grads[0]["pool_scale"], grads[1]["pool_scale"]], axis=0),
        jnp.stack([grads[0]["pool_w"], grads[1]["pool_w"]]))
    small_g = _allreduce_packed(small_g)

    g_out, d_out, m_out, v_out = {}, {}, {}, {}
    for part in ("main", "in"):
        recv = {}
        for layer in (1, 0):
            names, ex = scatters[layer, part]
            for n, r in zip(names, ex.wait(0, sent)[1]):
                recv[n, layer] = r
        for n in names:
            g_out[n], d_out[n], m_out[n], v_out[n] = _adamw_sharded(
                w[n], m[n], v[n], jnp.stack(own[n]), recv[n, 0], recv[n, 1], "adamw_" + n)
        sent = tuple(d_out[n] for n in names)
    pack = lambda t: _pack_small(*[t[n] for n in SMALL])
    d_small, m_small, v_small = _adamw_packed(pack(w), small_g, pack(m), pack(v), "adamw_small")
    for dst, a in ((g_out, small_g), (d_out, d_small), (m_out, m_small), (v_out, v_small)):
        dst.update(_unpack_small(a))

    loss = lax.psum(loss[0, 0], ("x", "y", "c"))
    return (loss, dx.reshape(1, seq, D_MODEL), *[g_out[n] for n in ORDER], *[d_out[n] for n in ORDER],
            *[m_out[n] for n in ORDER], *[v_out[n] for n in ORDER])
```

```python
import functools

import jax
import jax.numpy as jnp
from jax import lax
from jax.experimental import pallas as pl
from jax.experimental.pallas import tpu as pltpu

F32 = jnp.float32
BF16 = jnp.bfloat16

D_MODEL = 1024
HEAD_DIM = 64
POOL_WIDTH = 256
POOL_WINDOWS = (2, 4, 8, 16)
POOL_HALO = 16
GROUP_WIDTH = 256
DILATIONS = (1, 4, 16)
ATTN_BLOCK = 128
ROT_SHIFT = 8
ROPE_THETA = 500000.0
D_FF = 4096
FF_BLOCK = 512
FF_PER_STEP = 2
FF_PER_STEP_BWD = 1
N_DEV = 8
N_IN = POOL_WIDTH + 3 * 768
PLE_DIM = 256
EPS = 1e-6
NEG_BIG = -1e30

ADAM_LR = 0.001
ADAM_B1 = 0.9
ADAM_B2 = 0.999
ADAM_EPS = 1e-08
ADAM_WD = 0.01
ADAM_STEP = 10

LANES = 128
VMEM_LIMIT = 56 * 1024 * 1024
MESH = pl.DeviceIdType.MESH


def _params(n_grid):
    return pltpu.CompilerParams(dimension_semantics=("arbitrary",) * n_grid, vmem_limit_bytes=VMEM_LIMIT)


def _dot(a, b):
    return jnp.dot(a, b, preferred_element_type=F32)


def _dot_nt(a, b):
    return lax.dot_general(a, b, (((1,), (1,)), ((), ())), preferred_element_type=F32)


def _dot_tn(a, b):
    return lax.dot_general(a, b, (((0,), (0,)), ((), ())), preferred_element_type=F32)


def _rms(x, g):
    rstd = lax.rsqrt(jnp.mean(x * x, axis=-1, keepdims=True) + EPS)
    n = x * rstd
    return n, rstd, n * g


def _rms_bwd(dy, n, rstd, g):
    dyn = dy * g
    dx = rstd * (dyn - n * jnp.mean(dyn * n, axis=-1, keepdims=True))
    return dx, jnp.sum(dy * n, axis=0, keepdims=True)


def _ordered_after(body, n_in, after):
    if not after:
        return body
    return lambda *refs: body(*refs[:n_in], *refs[n_in + len(after):])


def _row_tile(s, t):
    t = min(s, t)
    assert s % t == 0
    return t


def _rot(z, c, sa, sb):
    return z * c + pltpu.roll(z, ROT_SHIFT, 1) * sa + pltpu.roll(z, LANES - ROT_SHIFT, 1) * sb


def _rot_t(dz, c, sa, sb):
    return dz * c + pltpu.roll(dz * sa, LANES - ROT_SHIFT, 1) + pltpu.roll(dz * sb, ROT_SHIFT, 1)


def _to_residues(value, stage, out_ref, dil):
    if dil == 1:
        out_ref[0] = value.astype(out_ref.dtype)
        return
    rows = value.shape[0] // dil
    for hf in range(GROUP_WIDTH // LANES):
        lanes = slice(hf * LANES, (hf + 1) * LANES)
        stage[hf][...] = value[:, lanes]
        for r in range(dil):
            out_ref[r, :, lanes] = stage[hf][pl.ds(r, rows, stride=dil), :].astype(out_ref.dtype)


def _from_residues(in_ref, stage, dil):
    if dil == 1:
        return in_ref[0].astype(F32)
    rows = in_ref.shape[1]
    for hf in range(GROUP_WIDTH // LANES):
        for r in range(dil):
            stage[hf][pl.ds(r, rows, stride=dil), :] = in_ref[r, :, hf * LANES:(hf + 1) * LANES].astype(F32)
    return jnp.concatenate([stage[0][...], stage[1][...]], axis=1)


def _residue_spec(dil, t):
    return pl.BlockSpec((dil, t // dil, GROUP_WIDTH), lambda i: (0, i, 0))


def _residue_shape(dil, s, dtype):
    return jax.ShapeDtypeStruct((dil, s // dil, GROUP_WIDTH), dtype)


def _stages(t, n):
    return [pltpu.VMEM((t, LANES), F32)] * (n * (GROUP_WIDTH // LANES))


def _pair_stages(refs):
    return [refs[i:i + 2] for i in range(0, len(refs), 2)]


def _normproj_fwd(h, g, w_in, rc, rsa, rsb, name):
    s = h.shape[0]
    t = _row_tile(s, 512)

    def body(h_ref, g_ref, w_ref, c_ref, sa_ref, sb_ref, hn_ref, u_ref, *rest):
        qkv_refs, stages = rest[:9], _pair_stages(rest[9:])
        _, _, hn = _rms(h_ref[...], g_ref[...])
        hb = hn.astype(BF16)
        hn_ref[...] = hb
        c, sa, sb = c_ref[...], sa_ref[...], sb_ref[...]

        def rot(z, scale):
            halves = [_rot(z[:, hf * LANES:(hf + 1) * LANES], c, sa, sb) * scale for hf in range(2)]
            return jnp.concatenate(halves, axis=1)

        u_ref[...] = _dot(hb, w_ref[:, 0:POOL_WIDTH])
        for grp, dil in enumerate(DILATIONS):
            lo = POOL_WIDTH + grp * GROUP_WIDTH
            q_ref, k_ref, v_ref = qkv_refs[3 * grp:3 * grp + 3]
            _to_residues(rot(_dot(hb, w_ref[:, lo:lo + GROUP_WIDTH]), HEAD_DIM ** -0.5), stages[0], q_ref, dil)
            _to_residues(rot(_dot(hb, w_ref[:, lo + 768:lo + 768 + GROUP_WIDTH]), 1.0), stages[1], k_ref, dil)
            _to_residues(_dot(hb, w_ref[:, lo + 1536:lo + 1536 + GROUP_WIDTH]), stages[2], v_ref, dil)

    row = lambda w: pl.BlockSpec((t, w), lambda i: (i, 0))
    return pl.pallas_call(
        body, name=name, grid=(s // t,),
        in_specs=[row(D_MODEL), pl.BlockSpec((1, D_MODEL), lambda i: (0, 0)),
                  pl.BlockSpec((D_MODEL, N_IN), lambda i: (0, 0)), row(LANES), row(LANES), row(LANES)],
        out_specs=[row(D_MODEL), row(POOL_WIDTH)] + [_residue_spec(dil, t) for dil in DILATIONS for _ in range(3)],
        out_shape=[jax.ShapeDtypeStruct((s, D_MODEL), BF16), jax.ShapeDtypeStruct((s, POOL_WIDTH), F32)]
        + [_residue_shape(dil, s, BF16) for dil in DILATIONS for _ in range(3)],
        scratch_shapes=_stages(t, 3),
        compiler_params=_params(1),
    )(h, g, w_in, rc, rsa, rsb)


def _pool_lane_window():
    lane = lax.broadcasted_iota(jnp.int32, (1, POOL_WIDTH), 1)
    return jnp.left_shift(2, lane // (POOL_WIDTH // len(POOL_WINDOWS)))


def _pool_fwd(u, w_bd, scale, name):
    s = u.shape[0]
    t = _row_tile(s, 512)

    def body(u_ref, w_ref, sc_ref, out_ref, y_ref, ext):
        i = pl.program_id(0)

        @pl.when(i == 0)
        def _():
            ext[0:POOL_HALO, :] = jnp.zeros((POOL_HALO, POOL_WIDTH), F32)

        x = u_ref[...]
        ext[POOL_HALO:, :] = x
        win = _pool_lane_window()
        acc = x
        wsum = jnp.zeros_like(x)
        for k in range(1, POOL_HALO):
            acc = acc + ext[POOL_HALO - k:POOL_HALO - k + t, :]
            if k + 1 in POOL_WINDOWS:
                wsum = jnp.where(win == k + 1, acc, wsum)
        pos = i * t + lax.broadcasted_iota(jnp.int32, (t, POOL_WIDTH), 0)
        cnt = jnp.minimum(pos + 1, win).astype(F32)
        y = wsum / cnt - x
        yb = y.astype(BF16)
        y_ref[...] = yb
        out_ref[...] = _dot(yb, w_ref[...]) * sc_ref[...]
        ext[0:POOL_HALO, :] = x[t - POOL_HALO:, :]

    row = pl.BlockSpec((t, POOL_WIDTH), lambda i: (i, 0))
    return pl.pallas_call(
        body, name=name, grid=(s // t,),
        in_specs=[row, pl.BlockSpec((POOL_WIDTH, POOL_WIDTH), lambda i: (0, 0)),
                  pl.BlockSpec((1, POOL_WIDTH), lambda i: (0, 0))],
        out_specs=[row, row],
        out_shape=[jax.ShapeDtypeStruct((s, POOL_WIDTH), F32), jax.ShapeDtypeStruct((s, POOL_WIDTH), BF16)],
        scratch_shapes=[pltpu.VMEM((t + POOL_HALO, POOL_WIDTH), F32)],
        compiler_params=_params(1),
    )(u, w_bd, scale)


def _head_masks():
    lane = lax.broadcasted_iota(jnp.int32, (ATTN_BLOCK, GROUP_WIDTH), 1)
    return [lane // HEAD_DIM == hd for hd in range(GROUP_WIDTH // HEAD_DIM)]


def _stack_heads(a, masks):
    zero = jnp.zeros_like(a)
    return jnp.concatenate([jnp.where(m, a, zero) for m in masks], axis=0)


def _band_bias(first_step):
    rows = ATTN_BLOCK * (GROUP_WIDTH // HEAD_DIM)
    i = lax.broadcasted_iota(jnp.int32, (rows, 2 * ATTN_BLOCK), 0) & (ATTN_BLOCK - 1)
    j = lax.broadcasted_iota(jnp.int32, (rows, 2 * ATTN_BLOCK), 1)
    inner = jnp.where((j >= i) & (j <= i + ATTN_BLOCK), 0.0, NEG_BIG)
    return jnp.where((j < ATTN_BLOCK) & first_step, NEG_BIG, inner), inner


def _column_per_head(a):
    return jnp.concatenate([a[:, hd * HEAD_DIM:hd * HEAD_DIM + 1] for hd in range(GROUP_WIDTH // HEAD_DIM)], axis=0)


def _blocks_per_step(nb):
    return 8 if nb % 8 == 0 else 4 if nb % 4 == 0 else 2 if nb % 2 == 0 else 1


def _attn_fwd(q, k, v, name):
    dil, length, _ = q.shape
    nb = length // ATTN_BLOCK
    qb = _blocks_per_step(nb)

    def body(q_ref, kp_ref, kc_ref, vp_ref, vc_ref, o_ref, lse_ref):
        masks = _head_masks()
        bias = _band_bias(pl.program_id(1) == 0)
        for qi in range(qb):
            here = slice(qi * ATTN_BLOCK, (qi + 1) * ATTN_BLOCK)
            before = slice((qi - 1) * ATTN_BLOCK, qi * ATTN_BLOCK)
            kcat = jnp.concatenate([kp_ref[...] if qi == 0 else kc_ref[before], kc_ref[here]], axis=0)
            vcat = jnp.concatenate([vp_ref[...] if qi == 0 else vc_ref[before], vc_ref[here]], axis=0)
            qs = _stack_heads(q_ref[here], masks)
            sc = _dot_nt(qs, kcat) + bias[min(qi, 1)]
            m = jnp.max(sc, axis=1, keepdims=True)
            e = jnp.exp(sc - m)
            l = jnp.sum(e, axis=1, keepdims=True)
            p = (e / l).astype(BF16)
            lse = m + jnp.log(l)
            o = jnp.zeros((ATTN_BLOCK, GROUP_WIDTH), F32)
            lse_full = jnp.zeros((ATTN_BLOCK, GROUP_WIDTH), F32)
            for hd, msk in enumerate(masks):
                rows = slice(hd * ATTN_BLOCK, (hd + 1) * ATTN_BLOCK)
                o = jnp.where(msk, _dot(p[rows], vcat), o)
                lse_full = jnp.where(msk, lse[rows], lse_full)
            o_ref[here] = o.astype(o_ref.dtype)
            lse_ref[here] = lse_full

    cur = pl.BlockSpec((None, qb * ATTN_BLOCK, GROUP_WIDTH), lambda r, j: (r, j, 0))
    prev = pl.BlockSpec((None, ATTN_BLOCK, GROUP_WIDTH), lambda r, j: (r, jnp.maximum(qb * j - 1, 0), 0))
    return pl.pallas_call(
        body, name=name, grid=(dil, nb // qb),
        in_specs=[cur, prev, cur, prev, cur], out_specs=[cur, cur],
        out_shape=[jax.ShapeDtypeStruct(q.shape, BF16), jax.ShapeDtypeStruct(q.shape, F32)],
        compiler_params=_params(2),
    )(q, k, k, v, v)


def _group_weights(l0, l1, l2):
    m = jnp.maximum(jnp.maximum(l0, l1), l2)
    e0, e1, e2 = jnp.exp(l0 - m), jnp.exp(l1 - m), jnp.exp(l2 - m)
    den = e0 + e1 + e2
    return e0 / den, e1 / den, e2 / den


def _outproj_fwd(h, pool_out, o, lse, w_out, name):
    s = h.shape[0]
    t = _row_tile(s, 512)

    def body(h_ref, po_ref, o0, o1, o2, l0, l1, l2, w_ref, out_ref, a_ref, *stages):
        stages = _pair_stages(stages)
        ov =[_from_residues(r, stages[i], DILATIONS[i]) for i, r in enumerate((o0, o1, o2))]
        lv = [_from_residues(r, stages[3 + i], DILATIONS[i]) for i, r in enumerate((l0, l1, l2))]
        wts = _group_weights(*lv)
        a = jnp.concatenate([po_ref[...]] + [ov[i] * wts[i] for i in range(3)], axis=1).astype(BF16)
        a_ref[...] = a
        out_ref[...] = h_ref[...] + _dot(a, w_ref[...])

    row = lambda w: pl.BlockSpec((t, w), lambda i: (i, 0))
    res = [_residue_spec(dil, t) for dil in DILATIONS]
    return pl.pallas_call(
        body, name=name, grid=(s // t,),
        in_specs=[row(D_MODEL), row(POOL_WIDTH)] + res + res + [pl.BlockSpec((D_MODEL, D_MODEL), lambda i: (0, 0))],
        out_specs=[row(D_MODEL), row(D_MODEL)],
        out_shape=[jax.ShapeDtypeStruct((s, D_MODEL), F32), jax.ShapeDtypeStruct((s, D_MODEL), BF16)],
        scratch_shapes=_stages(t, 6),
        compiler_params=_params(1),
    )(h, pool_out, *o, *lse, w_out)


def _mlp_fwd(h, g, w_up, w_down, name):
    s = h.shape[0]
    t = _row_tile(s, 1024)
    nblk = D_FF // (FF_PER_STEP * FF_BLOCK)

    def body(h_ref, g_ref, wu_ref, wd_ref, out_ref, hn_ref, r_ref, hb_s, acc):
        j = pl.program_id(1)

        @pl.when(j == 0)
        def _():
            _, _, hn = _rms(h_ref[...], g_ref[...])
            hb = hn.astype(BF16)
            hb_s[...] = hb
            hn_ref[...] = hb
            acc[...] = jnp.zeros_like(acc)

        hb = hb_s[...]
        acts = []
        for b in range(FF_PER_STEP):
            r = jnp.maximum(_dot(hb, wu_ref[b]), 0.0)
            r_ref[:, b * FF_BLOCK:(b + 1) * FF_BLOCK] = r.astype(BF16)
            acts.append((r * r).astype(BF16))
        acc[...] += _dot(jnp.concatenate(acts, axis=1), wd_ref[...].reshape(FF_PER_STEP * FF_BLOCK, D_MODEL))

        @pl.when(j == nblk - 1)
        def _():
            out_ref[...] = h_ref[...] + acc[...]

    row = pl.BlockSpec((t, D_MODEL), lambda i, j: (i, 0))
    return pl.pallas_call(
        body, name=name, grid=(s // t, nblk),
        in_specs=[row, pl.BlockSpec((1, D_MODEL), lambda i, j: (0, 0)),
                  pl.BlockSpec((FF_PER_STEP, D_MODEL, FF_BLOCK), lambda i, j: (j, 0, 0)),
                  pl.BlockSpec((FF_PER_STEP, FF_BLOCK, D_MODEL), lambda i, j: (j, 0, 0))],
        out_specs=[row, row, pl.BlockSpec((t, FF_PER_STEP * FF_BLOCK), lambda i, j: (i, j))],
        out_shape=[jax.ShapeDtypeStruct((s, D_MODEL), F32), jax.ShapeDtypeStruct((s, D_MODEL), BF16),
                   jax.ShapeDtypeStruct((s, D_FF), BF16)],
        scratch_shapes=[pltpu.VMEM((t, D_MODEL), BF16), pltpu.VMEM((t, D_MODEL), F32)],
        compiler_params=_params(2),
    )(h, g, w_up, w_down)


def _gate_fwd(h, g, w_gate, p, layer, w_ple, name):
    s = h.shape[0]
    t = _row_tile(s, 512)

    def body(h_ref, g_ref, wg_ref, p_ref, wp_ref, out_ref, hn_ref, gate_ref, pb_ref):
        x = h_ref[...]
        _, _, hn = _rms(x, g_ref[...])
        hb = hn.astype(BF16)
        hn_ref[...] = hb
        gate = 1.0 / (1.0 + jnp.exp(-_dot(hb, wg_ref[...])))
        pb = p_ref[...].astype(BF16)
        pb_ref[...] = pb
        gate_ref[...] = gate.astype(BF16)
        out_ref[...] = x + gate * _dot(pb, wp_ref[...])

    row = lambda w: pl.BlockSpec((t, w), lambda i: (i, 0))
    full = lambda a, b: pl.BlockSpec((a, b), lambda i: (0, 0))
    return pl.pallas_call(
        body, name=name, grid=(s // t,),
        in_specs=[row(D_MODEL), full(1, D_MODEL), full(D_MODEL, D_MODEL),
                  pl.BlockSpec((None, t, PLE_DIM), lambda i: (layer, i, 0)), full(PLE_DIM, D_MODEL)],
        out_specs=[row(D_MODEL), row(D_MODEL), row(D_MODEL), row(PLE_DIM)],
        out_shape=[jax.ShapeDtypeStruct((s, D_MODEL), F32), jax.ShapeDtypeStruct((s, D_MODEL), BF16),
                   jax.ShapeDtypeStruct((s, D_MODEL), BF16), jax.ShapeDtypeStruct((s, PLE_DIM), BF16)],
        compiler_params=_params(1),
    )(h, g, w_gate, p, w_ple)


def _loss_head(h, g, target, name):
    s = h.shape[0]
    t = _row_tile(s, 512)

    def body(h_ref, g_ref, t_ref, loss_ref, dh_ref, dg_ref):
        i = pl.program_id(0)

        @pl.when(i == 0)
        def _():
            loss_ref[...] = jnp.zeros_like(loss_ref)
            dg_ref[...] = jnp.zeros_like(dg_ref)

        gv = g_ref[...]
        n, rstd, y = _rms(h_ref[...], gv)
        err = y - t_ref[...]
        loss_ref[...] += jnp.sum(err * err) * (0.5 / D_MODEL)
        dx, dg = _rms_bwd(err * (1.0 / D_MODEL), n, rstd, gv)
        dh_ref[...] = dx
        dg_ref[...] += dg

    row = pl.BlockSpec((t, D_MODEL), lambda i: (i, 0))
    vec = pl.BlockSpec((1, D_MODEL), lambda i: (0, 0))
    return pl.pallas_call(
        body, name=name, grid=(s // t,),
        in_specs=[row, vec, row],
        out_specs=[pl.BlockSpec((1, LANES), lambda i: (0, 0)), row, vec],
        out_shape=[jax.ShapeDtypeStruct((1, LANES), F32), jax.ShapeDtypeStruct((s, D_MODEL), F32),
                   jax.ShapeDtypeStruct((1, D_MODEL), F32)],
        compiler_params=_params(1),
    )(h, g, target)


def _gate_bwd(dh, gate, pb, w_ple, h, g, w_gate, name, after=()):
    s = h.shape[0]
    t = _row_tile(s, 512)

    def body(dh_ref, gate_ref, pb_ref, wp_ref, h_ref, g_ref, wg_ref, out_ref, dgl_ref, de_ref, dg_ref):
        @pl.when(pl.program_id(0) == 0)
        def _():
            dg_ref[...] = jnp.zeros_like(dg_ref)

        d = dh_ref[...]
        gate = gate_ref[...].astype(F32)
        e = _dot(pb_ref[...], wp_ref[...])
        dgl = (d * e * gate * (1.0 - gate)).astype(BF16)
        dgl_ref[...] = dgl
        de_ref[...] = (d * gate).astype(BF16)
        gv = g_ref[...]
        n, rstd, _ = _rms(h_ref[...], gv)
        dx, dg = _rms_bwd(_dot_nt(dgl, wg_ref[...]), n, rstd, gv)
        out_ref[...] = d + dx
        dg_ref[...] += dg

    row = lambda w: pl.BlockSpec((t, w), lambda i: (i, 0))
    full = lambda a, b: pl.BlockSpec((a, b), lambda i: (0, 0))
    return pl.pallas_call(
        _ordered_after(body, 7, after), name=name, grid=(s // t,),
        in_specs=[row(D_MODEL), row(D_MODEL), row(PLE_DIM), full(PLE_DIM, D_MODEL), row(D_MODEL), full(1, D_MODEL),
                  full(D_MODEL, D_MODEL)] + [pl.BlockSpec(memory_space=pl.ANY)] * len(after),
        out_specs=[row(D_MODEL), row(D_MODEL), row(D_MODEL), full(1, D_MODEL)],
        out_shape=[jax.ShapeDtypeStruct((s, D_MODEL), F32), jax.ShapeDtypeStruct((s, D_MODEL), BF16),
                   jax.ShapeDtypeStruct((s, D_MODEL), BF16), jax.ShapeDtypeStruct((1, D_MODEL), F32)],
        compiler_params=_params(1),
    )(dh, gate, pb, w_ple, h, g, w_gate, *after)


def _mlp_bwd(dh, r, h, g, w_up, w_down, name):
    s = h.shape[0]
    t = _row_tile(s, 1024)
    per = FF_PER_STEP_BWD
    nblk = D_FF // (per * FF_BLOCK)

    def body(dh_ref, r_ref, h_ref, g_ref, wu_ref, wd_ref, out_ref, dup_ref, dg_ref, db_s, acc):
        i, j = pl.program_id(0), pl.program_id(1)

        @pl.when((i == 0) & (j == 0))
        def _():
            dg_ref[...] = jnp.zeros_like(dg_ref)

        @pl.when(j == 0)
        def _():
            db_s[...] = dh_ref[...].astype(BF16)
            acc[...] = jnp.zeros_like(acc)

        db = db_s[...]
        back = None
        for b in range(per):
            cols = slice(b * FF_BLOCK, (b + 1) * FF_BLOCK)
            dup = (_dot_nt(db, wd_ref[b]) * (2.0 * r_ref[:, cols].astype(F32))).astype(BF16)
            dup_ref[:, cols] = dup
            part = _dot_nt(dup, wu_ref[b])
            back = part if back is None else back + part
        acc[...] += back

        @pl.when(j == nblk - 1)
        def _():
            gv = g_ref[...]
            n, rstd, _ = _rms(h_ref[...], gv)
            dx, dg = _rms_bwd(acc[...], n, rstd, gv)
            out_ref[...] = dh_ref[...] + dx
            dg_ref[...] += dg

    row = pl.BlockSpec((t, D_MODEL), lambda i, j: (i, 0))
    vec = pl.BlockSpec((1, D_MODEL), lambda i, j: (0, 0))
    blk = pl.BlockSpec((t, per * FF_BLOCK), lambda i, j: (i, j))
    return pl.pallas_call(
        body, name=name, grid=(s // t, nblk),
        in_specs=[row, blk, row, vec,
                  pl.BlockSpec((per, D_MODEL, FF_BLOCK), lambda i, j: (j, 0, 0)),
                  pl.BlockSpec((per, FF_BLOCK, D_MODEL), lambda i, j: (j, 0, 0))],
        out_specs=[row, blk, vec, row],
        out_shape=[jax.ShapeDtypeStruct((s, D_MODEL), F32), jax.ShapeDtypeStruct((s, D_FF), BF16),
                   jax.ShapeDtypeStruct((1, D_MODEL), F32), jax.ShapeDtypeStruct((s, D_MODEL), BF16)],
        scratch_shapes=[pltpu.VMEM((t, D_MODEL), F32)],
        compiler_params=_params(2),
    )(dh, r, h, g, w_up, w_down)


def _outproj_bwd(dh, w_out, o, lse, ones_bd, name):
    s = dh.shape[0]
    t = _row_tile(s, 512)

    def body(dh_ref, w_ref, o0, o1, o2, l0, l1, l2, bd_ref, dp_ref, do0, do1, do2, de0, de1, de2, dhb_ref, *stages):
        stages = _pair_stages(stages)
        dhb = dh_ref[...].astype(BF16)
        dhb_ref[...] = dhb
        da = _dot_nt(dhb, w_ref[...])
        dp_ref[...] = da[:, 0:POOL_WIDTH]
        ov =[_from_residues(r, stages[i], DILATIONS[i]) for i, r in enumerate((o0, o1, o2))]
        lv = [_from_residues(r, stages[3 + i], DILATIONS[i]) for i, r in enumerate((l0, l1, l2))]
        wts = _group_weights(*lv)
        bd = bd_ref[...]
        cbar = jnp.zeros((t, GROUP_WIDTH), F32)
        for grp, do_ref in enumerate((do0, do1, do2)):
            lo = POOL_WIDTH + grp * GROUP_WIDTH
            dag = da[:, lo:lo + GROUP_WIDTH]
            _to_residues(dag * wts[grp], stages[6 + grp], do_ref, DILATIONS[grp])
            prod = dag * ov[grp]
            hi = prod.astype(BF16)
            low = (prod - hi.astype(F32)).astype(BF16)
            cbar = cbar + wts[grp] * (_dot(hi, bd) + _dot(low, bd))
        for grp, de_ref in enumerate((de0, de1, de2)):
            _to_residues(wts[grp] * cbar, stages[9 + grp], de_ref, DILATIONS[grp])

    row = lambda w: pl.BlockSpec((t, w), lambda i: (i, 0))
    full = lambda a, b: pl.BlockSpec((a, b), lambda i: (0, 0))
    res = [_residue_spec(dil, t) for dil in DILATIONS]
    return pl.pallas_call(
        body, name=name, grid=(s // t,),
        in_specs=[row(D_MODEL), full(D_MODEL, D_MODEL)] + res + res + [full(GROUP_WIDTH, GROUP_WIDTH)],
        out_specs=[row(POOL_WIDTH)] + res + res + [row(D_MODEL)],
        out_shape=[jax.ShapeDtypeStruct((s, POOL_WIDTH), F32)] + [_residue_shape(dil, s, BF16) for dil in DILATIONS]
        + [_residue_shape(dil, s, F32) for dil in DILATIONS] + [jax.ShapeDtypeStruct((s, D_MODEL), BF16)],
        scratch_shapes=_stages(t, 12),
        compiler_params=_params(1),
    )(dh, w_out, *o, *lse, ones_bd)


def _attn_bwd(q, k, v, do, lse, deff, name, after=()):
    dil, length, _ = q.shape
    nb = length // ATTN_BLOCK
    qb = _blocks_per_step(nb)
    nj = nb // qb
    tail = slice((qb - 1) * ATTN_BLOCK, qb * ATTN_BLOCK)

    def body(q_ref, kp_ref, kc_ref, vp_ref, vc_ref, do_ref, lse_ref, de_ref, dq_ref, dk_ref, dv_ref, ck, cv):
        j = pl.program_id(1)

        @pl.when(j < nj)
        def _():
            masks = _head_masks()
            bias = _band_bias(j == 0)
            dkc, dvc = [], []
            for qi in range(qb):
                here = slice(qi * ATTN_BLOCK, (qi + 1) * ATTN_BLOCK)
                before = slice((qi - 1) * ATTN_BLOCK, qi * ATTN_BLOCK)
                kcat = jnp.concatenate([kp_ref[...] if qi == 0 else kc_ref[before], kc_ref[here]], axis=0)
                vcat = jnp.concatenate([vp_ref[...] if qi == 0 else vc_ref[before], vc_ref[here]], axis=0)
                qs = _stack_heads(q_ref[here], masks)
                dos = _stack_heads(do_ref[here], masks)
                sc = _dot_nt(qs, kcat) + bias[min(qi, 1)]
                p = jnp.exp(sc - _column_per_head(lse_ref[here]))
                ds = (p * (_dot_nt(dos, vcat) - _column_per_head(de_ref[here]))).astype(BF16)
                dq = jnp.zeros((ATTN_BLOCK, GROUP_WIDTH), F32)
                for hd, msk in enumerate(masks):
                    dq = jnp.where(msk, _dot(ds[hd * ATTN_BLOCK:(hd + 1) * ATTN_BLOCK], kcat), dq)
                dq_ref[here] = dq.astype(dq_ref.dtype)
                dkc.append(_dot_tn(ds, qs))
                dvc.append(_dot_tn(p.astype(BF16), dos))

            for out_ref, carry, parts in ((dk_ref, ck, dkc), (dv_ref, cv, dvc)):
                @pl.when(j > 0)
                def _():
                    if qb > 1:
                        out_ref[0:(qb - 1) * ATTN_BLOCK] = carry[0:(qb - 1) * ATTN_BLOCK].astype(out_ref.dtype)
                    out_ref[tail] = (carry[tail] + parts[0][0:ATTN_BLOCK]).astype(out_ref.dtype)

                for qi in range(qb - 1):
                    carry[qi * ATTN_BLOCK:(qi + 1) * ATTN_BLOCK] = parts[qi][ATTN_BLOCK:] + parts[qi + 1][0:ATTN_BLOCK]
                carry[tail] = parts[qb - 1][ATTN_BLOCK:]

        @pl.when(j == nj)
        def _():
            dk_ref[...] = ck[...].astype(dk_ref.dtype)
            dv_ref[...] = cv[...].astype(dv_ref.dtype)

    step = lambda j: jnp.minimum(j, nj - 1)
    cur = pl.BlockSpec((None, qb * ATTN_BLOCK, GROUP_WIDTH), lambda r, j: (r, step(j), 0))
    prev = pl.BlockSpec((None, ATTN_BLOCK, GROUP_WIDTH), lambda r, j: (r, jnp.maximum(qb * step(j) - 1, 0), 0))
    late = pl.BlockSpec((None, qb * ATTN_BLOCK, GROUP_WIDTH), lambda r, j: (r, jnp.maximum(j - 1, 0), 0))
    return pl.pallas_call(
        _ordered_after(body, 8, after), name=name, grid=(dil, nj + 1),
        in_specs=[cur, prev, cur, prev, cur, cur, cur, cur] + [pl.BlockSpec(memory_space=pl.ANY)] * len(after),
        out_specs=[cur, late, late],
        out_shape=[jax.ShapeDtypeStruct(q.shape, BF16)] * 3,
        scratch_shapes=[pltpu.VMEM((qb * ATTN_BLOCK, GROUP_WIDTH), F32)] * 2,
        compiler_params=_params(2),
    )(q, k, k, v, v, do, lse, deff, *after)


def _pool_bwd(dpool, y, w_bd, scale, name, after=()):
    s = dpool.shape[0]
    t = _row_tile(s, 512)
    nt = s // t

    def body(dp_ref, y_ref, w_ref, sc_ref, du_ref, dw_ref, dsc_ref, ext):
        i = pl.program_id(0)

        @pl.when(i == 0)
        def _():
            ext[t:, :] = jnp.zeros((POOL_HALO, POOL_WIDTH), F32)
            dw_ref[...] = jnp.zeros_like(dw_ref)
            dsc_ref[...] = jnp.zeros_like(dsc_ref)

        dp = dp_ref[...]
        yb = y_ref[...]
        w = w_ref[...]
        dsc_ref[...] += jnp.sum(dp * _dot(yb, w), axis=0, keepdims=True)
        dyo = (dp * sc_ref[...]).astype(BF16)
        dw_ref[...] += _dot_tn(yb, dyo)
        dy = _dot_nt(dyo, w)
        win = _pool_lane_window()
        pos = (nt - 1 - i) * t + lax.broadcasted_iota(jnp.int32, (t, POOL_WIDTH), 0)
        gq = dy / jnp.minimum(pos + 1, win).astype(F32)
        ext[0:t, :] = gq
        acc = gq
        wsum = jnp.zeros_like(gq)
        for k in range(1, POOL_HALO):
            acc = acc + ext[k:k + t, :]
            if k + 1 in POOL_WINDOWS:
                wsum = jnp.where(win == k + 1, acc, wsum)
        du_ref[...] = wsum - dy
        ext[t:, :] = gq[0:POOL_HALO, :]

    rev = pl.BlockSpec((t, POOL_WIDTH), lambda i: (nt - 1 - i, 0))
    full = lambda a, b: pl.BlockSpec((a, b), lambda i: (0, 0))
    return pl.pallas_call(
        _ordered_after(body, 4, after), name=name, grid=(nt,),
        in_specs=[rev, rev, full(POOL_WIDTH, POOL_WIDTH), full(1, POOL_WIDTH)]
        + [pl.BlockSpec(memory_space=pl.ANY)] * len(after),
        out_specs=[rev, full(POOL_WIDTH, POOL_WIDTH), full(1, POOL_WIDTH)],
        out_shape=[jax.ShapeDtypeStruct((s, POOL_WIDTH), F32), jax.ShapeDtypeStruct((POOL_WIDTH, POOL_WIDTH), F32),
                   jax.ShapeDtypeStruct((1, POOL_WIDTH), F32)],
        scratch_shapes=[pltpu.VMEM((t + POOL_HALO, POOL_WIDTH), F32)],
        compiler_params=_params(1),
    )(dpool, y, w_bd, scale, *after)


def _normproj_bwd(dh, du, dq, dk, dv, rc, rsa, rsb, w_in, h, g, name):
    s = h.shape[0]
    t = _row_tile(s, 512)

    def body(dh_ref, du_ref, q0, q1, q2, k0, k1, k2, v0, v1, v2, c_ref, sa_ref, sb_ref, w_ref, h_ref, g_ref,
             out_ref, dz_ref, dg_ref, *stages):
        @pl.when(pl.program_id(0) == 0)
        def _():
            dg_ref[...] = jnp.zeros_like(dg_ref)

        c, sa, sb = c_ref[...], sa_ref[...], sb_ref[...]

        def unrot(a, scale):
            halves = [_rot_t(a[:, hf * LANES:(hf + 1) * LANES] * scale, c, sa, sb) for hf in range(2)]
            return jnp.concatenate(halves, axis=1)

        staged = _pair_stages(stages)
        tok = lambda refs, base: [_from_residues(r, staged[base + i], DILATIONS[i]) for i, r in enumerate(refs)]
        chunks = [du_ref[...]]
        chunks += [unrot(a, HEAD_DIM ** -0.5) for a in tok((q0, q1, q2), 0)]
        chunks += [unrot(a, 1.0) for a in tok((k0, k1, k2), 3)]
        chunks += tok((v0, v1, v2), 6)
        acc = jnp.zeros((t, D_MODEL), F32)
        for ci, ch in enumerate(chunks):
            cols = slice(ci * GROUP_WIDTH, (ci + 1) * GROUP_WIDTH)
            cb = ch.astype(BF16)
            dz_ref[:, cols] = cb
            acc = acc + _dot_nt(cb, w_ref[:, cols])
        gv = g_ref[...]
        n, rstd, _ = _rms(h_ref[...], gv)
        dx, dg = _rms_bwd(acc, n, rstd, gv)
        out_ref[...] = dh_ref[...] + dx
        dg_ref[...] += dg

    row = lambda w: pl.BlockSpec((t, w), lambda i: (i, 0))
    vec = pl.BlockSpec((1, D_MODEL), lambda i: (0, 0))
    res = [_residue_spec(dil, t) for dil in DILATIONS]
    return pl.pallas_call(
        body, name=name, grid=(s // t,),
        in_specs=[row(D_MODEL), row(POOL_WIDTH)] + res * 3 + [row(LANES)] * 3
        + [pl.BlockSpec((D_MODEL, N_IN), lambda i: (0, 0)), row(D_MODEL), vec],
        out_specs=[row(D_MODEL), row(N_IN), vec],
        out_shape=[jax.ShapeDtypeStruct((s, D_MODEL), F32), jax.ShapeDtypeStruct((s, N_IN), BF16),
                   jax.ShapeDtypeStruct((1, D_MODEL), F32)],
        scratch_shapes=_stages(t, 9),
        compiler_params=_params(1),
    )(dh, du, *dq, *dk, *dv, rc, rsa, rsb, w_in, h, g)


def _matmul_tn(a, b, name, *, square_a=False, tn=None, blocked_out=False):
    s, m = a.shape
    n = b.shape[1]
    tk = _row_tile(s, 2048)
    tm = min(m, 1024)
    tn = tn or min(n, 1024)
    assert m % tm == 0 and n % tn == 0
    nk = s // tk
    nsub = tn // FF_BLOCK if blocked_out else 1

    def body(a_ref, b_ref, o_ref, ob_ref, acc):
        k = pl.program_id(2)

        @pl.when(k == 0)
        def _():
            acc[...] = jnp.zeros_like(acc)

        av = a_ref[...]
        if square_a:
            av = av.astype(F32)
            av = av * av
        acc[...] += _dot_tn(av.astype(BF16), b_ref[...].astype(BF16))

        @pl.when(k == nk - 1)
        def _():
            if blocked_out:
                for sub in range(nsub):
                    cols = slice(sub * FF_BLOCK, (sub + 1) * FF_BLOCK)
                    o_ref[sub] = acc[:, cols]
                    ob_ref[sub] = acc[:, cols].astype(BF16)
            else:
                o_ref[...] = acc[...]
                ob_ref[...] = acc[...].astype(BF16)

    if blocked_out:
        shape = (n // FF_BLOCK, m, FF_BLOCK)
        out_spec = pl.BlockSpec((nsub, tm, FF_BLOCK), lambda i, j, k: (j, i, 0))
    else:
        shape = (m, n)
        out_spec = pl.BlockSpec((tm, tn), lambda i, j, k: (i, j))
    return pl.pallas_call(
        body, name=name, grid=(m // tm, n // tn, nk),
        in_specs=[pl.BlockSpec((tk, tm), lambda i, j, k: (k, i)), pl.BlockSpec((tk, tn), lambda i, j, k: (k, j))],
        out_specs=[out_spec, out_spec],
        out_shape=[jax.ShapeDtypeStruct(shape, F32), jax.ShapeDtypeStruct(shape, BF16)],
        scratch_shapes=[pltpu.VMEM((tm, tn), F32)],
        compiler_params=_params(3),
    )(a, b)


def _adamw_math(w, g, m, v):
    m = ADAM_B1 * m + (1.0 - ADAM_B1) * g
    v = ADAM_B2 * v + (1.0 - ADAM_B2) * (g * g)
    m_hat = m / (1.0 - ADAM_B1 ** ADAM_STEP)
    v_hat = v / (1.0 - ADAM_B2 ** ADAM_STEP)
    delta = -ADAM_LR * (m_hat / (jnp.sqrt(v_hat) + ADAM_EPS) + ADAM_WD * w)
    return delta, m, v


def _adamw_sharded(w, m, v, own, recv0, recv1, name):
    _, rows, cols = w.shape
    t = _row_tile(rows, 256)

    def body(w_ref, m_ref, v_ref, own_ref, r0_ref, r1_ref, g_ref, d_ref, nm_ref, nv_ref):
        layer0 = pl.program_id(0) == 0
        g = own_ref[...]
        for k in range(N_DEV - 1):
            g = g + jnp.where(layer0, r0_ref[k], r1_ref[k]).astype(F32)
        g_ref[...] = g
        d_ref[...], nm_ref[...], nv_ref[...] = _adamw_math(w_ref[...], g, m_ref[...], v_ref[...])

    blk = pl.BlockSpec((None, t, cols), lambda l, i: (l, i, 0))
    recv = lambda layer: pl.BlockSpec((N_DEV - 1, t, cols), lambda l, i: (0, jnp.where(l == layer, i, 0), 0))
    return pl.pallas_call(
        body, name=name, grid=(2, rows // t),
        in_specs=[blk, blk, blk, blk, recv(0), recv(1)], out_specs=[blk] * 4,
        out_shape=[jax.ShapeDtypeStruct(w.shape, F32)] * 4,
        compiler_params=_params(2),
    )(w, m, v, own, recv0, recv1)


def _adamw_packed(w, g, m, v, name):
    def body(w_ref, g_ref, m_ref, v_ref, d_ref, nm_ref, nv_ref):
        d_ref[...], nm_ref[...], nv_ref[...] = _adamw_math(w_ref[...], g_ref[...], m_ref[...], v_ref[...])

    return pl.pallas_call(
        body, name=name, out_shape=[jax.ShapeDtypeStruct(w.shape, F32)] * 3,
        compiler_params=pltpu.CompilerParams(vmem_limit_bytes=VMEM_LIMIT),
    )(w, g, m, v)


def _peer(k):
    x, y, c = lax.axis_index("x"), lax.axis_index("y"), lax.axis_index("c")
    return (1 - x if k & 4 else x, 1 - y if k & 2 else y, 1 - c if k & 1 else c)


def _linear(dev):
    return 4 * dev[0] + 2 * dev[1] + dev[2]


HBM_SPEC = pl.BlockSpec(memory_space=pltpu.HBM)
SEM_SPEC = pl.BlockSpec(memory_space=pltpu.SEMAPHORE)
ANY_SPEC = pl.BlockSpec(memory_space=pl.ANY)
EFFECT = pltpu.SideEffectType.DATAFLOW_SIDE_EFFECTING


def _in_hbm(a):
    return pltpu.with_memory_space_constraint(a, pltpu.HBM)


class _Exchange:
    def __init__(self, name, groups, scatter, after=()):
        self.name, self.scatter = name, scatter
        self.sizes = sizes = [len(g) for g in groups]
        srcs = [a for g in groups for a in g]
        n, ng = len(srcs), len(groups)
        lead = (N_DEV - 1,) if scatter else (N_DEV,)
        shapes = [lead + (a.shape[1:] if scatter else a.shape) for a in srcs]
        lands = [lax.empty(sh, a.dtype) for sh, a in zip(shapes, srcs)]
        offsets = [sum(sizes[:gi]) for gi in range(ng)]
        copy = self._copy

        def body(*refs):
            src, land = refs[:n], refs[n:2 * n]
            sems = refs[2 * n + len(after):2 * n + len(after) + 2 * ng]
            token = refs[-1]
            for gi in range(ng):
                for wi in range(sizes[gi]):
                    w = offsets[gi] + wi
                    for k in range(1, N_DEV):
                        copy(src[w], land[w], sems[2 * gi], sems[2 * gi + 1], wi, k).start()
            token[...] = jnp.zeros_like(token)

        sem_shapes = [pltpu.SemaphoreType.DMA((7 * sz,)) for sz in sizes for _ in range(2)]
        outs = pl.pallas_call(
            body, name=name + "_start",
            in_specs=[HBM_SPEC] * (2 * n) + [ANY_SPEC] * len(after),
            out_specs=[SEM_SPEC] * (2 * ng) + [HBM_SPEC] * (2 * n) + [pl.BlockSpec(memory_space=pltpu.VMEM)],
            out_shape=sem_shapes + [pltpu.HBM(a.shape, a.dtype) for a in srcs + lands]
            + [jax.ShapeDtypeStruct((8, LANES), F32)],
            input_output_aliases={i: 2 * ng + i for i in range(2 * n)},
            compiler_params=pltpu.CompilerParams(has_side_effects=EFFECT),
        )(*[_in_hbm(a) for a in srcs + lands], *after)
        self.sems = [outs[2 * gi:2 * gi + 2] for gi in range(ng)]
        thru = outs[2 * ng:2 * ng + 2 * n]
        self.srcs = [thru[offsets[gi]:offsets[gi] + sizes[gi]] for gi in range(ng)]
        self.lands = [thru[n + offsets[gi]:n + offsets[gi] + sizes[gi]] for gi in range(ng)]
        self.token = outs[-1]

    def _copy(self, src, land, send_sems, recv_sems, wi, k):
        to = _peer(k)
        if self.scatter:
            src_ref, dst_ref = src.at[_linear(to)], land.at[k - 1]
        else:
            src_ref, dst_ref = src, land.at[_linear(_peer(0))]
        return pltpu.make_async_remote_copy(
            src_ref=src_ref, dst_ref=dst_ref, send_sem=send_sems.at[7 * wi + k - 1],
            recv_sem=recv_sems.at[7 * wi + k - 1], device_id=to, device_id_type=MESH)

    def wait(self, gi, after):
        n = self.sizes[gi]
        copy = self._copy

        def body(*refs):
            src, land = refs[:n], refs[n:2 * n]
            send_sems, recv_sems = refs[2 * n], refs[2 * n + 1]
            for wi in range(n):
                for k in range(1, N_DEV):
                    cp = copy(src[wi], land[wi], send_sems, recv_sems, wi, k)
                    cp.wait_send()
                    cp.wait_recv()

        arrays = list(self.srcs[gi]) + list(self.lands[gi])
        outs = pl.pallas_call(
            body, name=f"{self.name}_wait{gi}",
            in_specs=[HBM_SPEC] * (2 * n) + [SEM_SPEC, SEM_SPEC] + [ANY_SPEC] * len(after),
            out_specs=[HBM_SPEC] * (2 * n),
            out_shape=[pltpu.HBM(a.shape, a.dtype) for a in arrays],
            input_output_aliases={i: i for i in range(2 * n)},
            compiler_params=pltpu.CompilerParams(has_side_effects=EFFECT),
        )(*arrays, *self.sems[gi], *after)
        return outs[:n], outs[n:]


def _allreduce_packed(g):
    rows = g.shape[0]

    def body(g_ref, out_ref, buf, send_sems, recv_sems):
        me = _linear(_peer(0))
        buf[me] = g_ref[...]
        copies = []
        for k in range(1, N_DEV):
            copies.append(pltpu.make_async_remote_copy(
                src_ref=g_ref, dst_ref=buf.at[me], send_sem=send_sems.at[k - 1], recv_sem=recv_sems.at[k - 1],
                device_id=_peer(k), device_id_type=MESH))
        for cp in copies:
            cp.start()
        for k in range(1, N_DEV):
            pltpu.make_async_remote_copy(
                src_ref=g_ref, dst_ref=buf.at[_linear(_peer(k))], send_sem=send_sems.at[k - 1],
                recv_sem=recv_sems.at[k - 1], device_id=_peer(k), device_id_type=MESH).wait_recv()
        for cp in copies:
            cp.wait_send()
        total = buf[0]
        for d in range(1, N_DEV):
            total = total + buf[d]
        out_ref[...] = total

    return pl.pallas_call(
        body, name="allreduce_small",
        in_specs=[pl.BlockSpec(memory_space=pltpu.VMEM)], out_specs=pl.BlockSpec(memory_space=pltpu.VMEM),
        out_shape=jax.ShapeDtypeStruct(g.shape, F32),
        scratch_shapes=[pltpu.VMEM((N_DEV, rows, g.shape[1]), F32), pltpu.SemaphoreType.DMA((7,)),
                        pltpu.SemaphoreType.DMA((7,))],
        compiler_params=pltpu.CompilerParams(vmem_limit_bytes=VMEM_LIMIT),
    )(g)


def _rotary_tables(positions):
    rot_dim = HEAD_DIM // 4
    inv_freq = ROPE_THETA ** (-jnp.arange(0, rot_dim, 2, dtype=F32) / rot_dim)
    dim = jnp.arange(LANES) % HEAD_DIM
    ang = positions.astype(F32)[:, None] * inv_freq[dim % ROT_SHIFT][None, :]
    cos, sin = jnp.cos(ang), jnp.sin(ang)
    first, second = dim < ROT_SHIFT, (dim >= ROT_SHIFT) & (dim < rot_dim)
    c = jnp.where(first | second, cos, 1.0)
    sa = jnp.where(second, sin, 0.0)
    sb = jnp.where(first, -sin, 0.0)
    return [c, sa, sb]


def _block_diag(pool_w):
    gc = pool_w.shape[-1]
    out = jnp.zeros((POOL_WIDTH, POOL_WIDTH), pool_w.dtype)
    for grp in range(pool_w.shape[0]):
        out = lax.dynamic_update_slice(out, pool_w[grp], (grp * gc, grp * gc))
    return out


def _diag_blocks(a):
    gc = POOL_WIDTH // len(POOL_WINDOWS)
    return jnp.stack([a[grp * gc:(grp + 1) * gc, grp * gc:(grp + 1) * gc] for grp in range(len(POOL_WINDOWS))])


def _local_step(x, p, positions, loss_target, norm1, pool_w, pool_scale, norm2, norm3, final_norm, weights, send):
    rc, rsa, rsb = _rotary_tables(positions)
    ones_bd = _block_diag(jnp.ones((4, HEAD_DIM, HEAD_DIM), BF16))
    saved = []
    h = x
    for i in range(2):
        tag = f"_l{i}"
        g1, g2, g3 = norm1[i:i + 1], norm2[i:i + 1], norm3[i:i + 1]
        w_bd = _block_diag(pool_w[i]).astype(BF16)
        scale = pool_scale[i:i + 1]
        w_in = weights(i, "in", (h, rc, rsa, rsb, w_bd))
        hn1, u, *qkv = _normproj_fwd(h, g1, w_in, rc, rsa, rsb, "normproj_fwd" + tag)
        qkv = [qkv[3 * grp:3 * grp + 3] for grp in range(3)]
        weights(i, "prefetch", (hn1,))
        pool_out, y = _pool_fwd(u, w_bd, scale, "pool_fwd" + tag)
        o, lse = zip(*[_attn_fwd(*qkv[grp], f"attn_fwd{tag}_g{grp}") for grp in range(3)])
        w_out = weights(i, "out", (pool_out, *o))
        h1, a = _outproj_fwd(h, pool_out, o, lse, w_out, "outproj_fwd" + tag)
        w_up, w_down, w_gate, w_ple = weights(i, "rest", (h1,))
        h2, hn2, r = _mlp_fwd(h1, g2, w_up, w_down, "mlp_fwd" + tag)
        h3, hn3, gate, pb = _gate_fwd(h2, g3, w_gate, p, i, w_ple, "gate_fwd" + tag)
        saved.append(dict(h0=h, hn1=hn1, qkv=qkv, y=y, o=o, lse=lse, a=a, h1=h1, hn2=hn2, r=r, h2=h2,
                          hn3=hn3, gate=gate, pb=pb, w_bd=w_bd, scale=scale, g1=g1, g2=g2, g3=g3,
                          w_in=w_in, w_out=w_out, w_up=w_up, w_down=w_down, w_gate=w_gate, w_ple=w_ple))
        h = h3
    loss, dh, d_final = _loss_head(h, final_norm.reshape(1, D_MODEL), loss_target, "loss_head")

    grads = [None, None]
    sent = ()
    for i in (1, 0):
        tag = f"_l{i}"
        sv = saved[i]
        dh2, dgl, de, dg3 = _gate_bwd(dh, sv["gate"], sv["pb"], sv["w_ple"], sv["h2"], sv["g3"], sv["w_gate"],
                                      "gate_bwd" + tag, after=sent)
        dw_gate = _matmul_tn(sv["hn3"], dgl, "dw_gate" + tag)
        dw_ple = _matmul_tn(sv["pb"], de, "dw_ple" + tag)
        dh1, dup, dg2, dh2b = _mlp_bwd(dh2, sv["r"], sv["h1"], sv["g2"], sv["w_up"], sv["w_down"], "mlp_bwd" + tag)
        dw_down = _matmul_tn(sv["r"], dh2b, "dw_down" + tag, square_a=True)
        dw_up = _matmul_tn(sv["hn2"], dup, "dw_up" + tag, blocked_out=True)
        dpool, do0, do1, do2, de0, de1, de2, dh1b = _outproj_bwd(dh1, sv["w_out"], sv["o"], sv["lse"], ones_bd,
                                                                 "outproj_bwd" + tag)
        dw_out = _matmul_tn(sv["a"], dh1b, "dw_out" + tag)
        sent = send(i, "main", dict(w_gate=dw_gate, w_ple=dw_ple, w_down=dw_down, w_up=dw_up, w_out=dw_out))
        dqkv = [_attn_bwd(*sv["qkv"][grp], do_g, sv["lse"][grp], de_g, f"attn_bwd{tag}_g{grp}", after=sent)
                for grp, (do_g, de_g) in enumerate(((do0, de0), (do1, de1), (do2, de2)))]
        dq, dk, dv = zip(*dqkv)
        du, dw_bd, dscale = _pool_bwd(dpool, sv["y"], sv["w_bd"], sv["scale"], "pool_bwd" + tag, after=sent)
        dh, dz, dg1 = _normproj_bwd(dh1, du, dq, dk, dv, rc, rsa, rsb, sv["w_in"], sv["h0"], sv["g1"],
                                    "normproj_bwd" + tag)
        dw_in = _matmul_tn(sv["hn1"], dz, "dw_in" + tag, tn=N_IN // 2)
        sent = send(i, "in", dict(w_in=dw_in))
        grads[i] = dict(norm1=dg1, norm2=dg2, norm3=dg3, pool_w=_diag_blocks(dw_bd), pool_scale=dscale)
    return loss, dh, grads, d_final, sent


def _pack_small(norm1, norm2, norm3, final_norm, pool_scale, pool_w, spare=None):
    spare = jnp.zeros((1, LANES), F32) if spare is None else spare
    scale_row = jnp.concatenate([pool_scale.reshape(1, 2 * POOL_WIDTH), spare,
                                 jnp.zeros((1, D_MODEL - 2 * POOL_WIDTH - LANES), F32)], axis=1)
    return jnp.concatenate([norm1, norm2, norm3, final_norm.reshape(1, D_MODEL), scale_row,
                            pool_w.reshape(32, D_MODEL)], axis=0)


def _unpack_small(a):
    return dict(norm1=a[0:2], norm2=a[2:4], norm3=a[4:6], final_norm=a[6], pool_scale=a[7, 0:2 * POOL_WIDTH].reshape(2, POOL_WIDTH),
                pool_w=a[8:40].reshape(2, 4, HEAD_DIM, HEAD_DIM))


def _chunks_cols(a, cols):
    return a.reshape(a.shape[0], N_DEV, cols).transpose(1, 0, 2)


def _chunks_rows(a, rows):
    return a.reshape(N_DEV, rows, a.shape[1])


BIG = ("w_in", "w_out", "w_up", "w_down", "w_gate", "w_ple")
SMALL = ("norm1", "norm2", "norm3", "final_norm", "pool_scale", "pool_w")
ORDER = ("norm1", "w_in", "pool_w", "pool_scale", "w_out", "norm2", "w_up", "w_down", "norm3", "w_gate", "w_ple",
         "final_norm")


def kernel(x, p, positions, norm1, w_in, pool_w, pool_scale, w_out, norm2, w_up, w_down, norm3, w_gate, w_ple, final_norm, loss_target, m_norm1, m_w_in, m_pool_w, m_pool_scale, m_w_out, m_norm2, m_w_up, m_w_down, m_norm3, m_w_gate, m_w_ple, m_final_norm, v_norm1, v_w_in, v_pool_w, v_pool_scale, v_w_out, v_norm2, v_w_up, v_w_down, v_norm3, v_w_gate, v_w_ple, v_final_norm):
    w = dict(norm1=norm1, w_in=w_in, pool_w=pool_w, pool_scale=pool_scale, w_out=w_out, norm2=norm2, w_up=w_up,
             w_down=w_down, norm3=norm3, w_gate=w_gate, w_ple=w_ple, final_norm=final_norm)
    m = dict(norm1=m_norm1, w_in=m_w_in, pool_w=m_pool_w, pool_scale=m_pool_scale, w_out=m_w_out, norm2=m_norm2,
             w_up=m_w_up, w_down=m_w_down, norm3=m_norm3, w_gate=m_w_gate, w_ple=m_w_ple, final_norm=m_final_norm)
    v = dict(norm1=v_norm1, w_in=v_w_in, pool_w=v_pool_w, pool_scale=v_pool_scale, w_out=v_w_out, norm2=v_norm2,
             w_up=v_w_up, w_down=v_w_down, norm3=v_norm3, w_gate=v_w_gate, w_ple=v_w_ple, final_norm=v_final_norm)
    seq = x.shape[1]

    bf = {n: [w[n][layer].astype(BF16) for layer in range(2)] for n in BIG}
    rest = ("w_up", "w_down", "w_gate", "w_ple")
    me = 4 * lax.axis_index("x") + 2 * lax.axis_index("y") + lax.axis_index("c")
    gathers = [_Exchange("gather_l0", [[bf["w_in"][0]], [bf["w_out"][0]], [bf[n][0] for n in rest]], scatter=False)]
    unpack = dict(w_in=lambda a: a.transpose(1, 0, 2).reshape(D_MODEL, N_IN),
                  w_out=lambda a: a.reshape(D_MODEL, D_MODEL), w_gate=lambda a: a.reshape(D_MODEL, D_MODEL),
                  w_ple=lambda a: a.transpose(1, 0, 2).reshape(PLE_DIM, D_MODEL), w_up=lambda a: a, w_down=lambda a: a)
    parts = dict(zip(("in", "out", "rest"), (("w_in",), ("w_out",), rest)))

    def weights(layer, part, after):
        if part == "prefetch":
            if layer == 0:
                gathers.append(_Exchange("gather_l1", [[bf[n][1] for n in parts[pt]] for pt in parts], scatter=False,
                                         after=after))
            return None
        shards, lands = gathers[layer].wait(tuple(parts).index(part), after)
        full = [unpack[n](lax.dynamic_update_slice_in_dim(land, shard[None], me, axis=0))
                for n, shard, land in zip(parts[part], shards, lands)]
        return full if part == "rest" else full[0]

    to_chunks = dict(w_in=lambda a: _chunks_cols(a, N_IN // N_DEV), w_out=lambda a: _chunks_rows(a, D_MODEL // N_DEV),
                     w_up=lambda a: a, w_down=lambda a: _chunks_rows(a, FF_BLOCK),
                     w_gate=lambda a: _chunks_rows(a, D_MODEL // N_DEV), w_ple=lambda a: _chunks_cols(a, D_MODEL // N_DEV))
    own = {n: [None, None] for n in BIG}
    scatters = {}

    def own_chunk(n, g32):
        if n in ("w_in", "w_ple"):
            cols = g32.shape[1] // N_DEV
            return lax.dynamic_slice(g32, (0, me * cols), (g32.shape[0], cols))
        return lax.dynamic_index_in_dim(to_chunks[n](g32), me, axis=0, keepdims=False)

    def send(layer, part, grads):
        for n, (g32, _) in grads.items():
            own[n][layer] = own_chunk(n, g32)
        ex = _Exchange(f"scatter_{part}_l{layer}", [[to_chunks[n](g16) for n, (_, g16) in grads.items()]], scatter=True)
        scatters[layer, part] = (tuple(grads), ex)
        return (ex.token,)

    loss, dx, grads, d_final, sent = _local_step(
        x.reshape(seq, D_MODEL), p.reshape(2, seq, PLE_DIM), positions.reshape(seq), loss_target.reshape(seq, D_MODEL),
        norm1, pool_w, pool_scale, norm2, norm3, final_norm, weights, send)

    small_g = _pack_small(
        *[jnp.concatenate([grads[0][n], grads[1][n]], axis=0) for n in ("norm1", "norm2", "norm3")], d_final.reshape(D_MODEL),
        jnp.concatenate([grads[0]["pool_scale"], grads[1]["pool_scale"]], axis=0),
        jnp.stack([grads[0]["pool_w"], grads[1]["pool_w"]]), spare=loss)
    small_g = _allreduce_packed(small_g)

    g_out, d_out, m_out, v_out = {}, {}, {}, {}
    for part in ("main", "in"):
        recv = {}
        for layer in (1, 0):
            names, ex = scatters[layer, part]
            for n, r in zip(names, ex.wait(0, sent)[1]):
                recv[n, layer] = r
        for n in names:
            g_out[n], d_out[n], m_out[n], v_out[n] = _adamw_sharded(
                w[n], m[n], v[n], jnp.stack(own[n]), recv[n, 0], recv[n, 1], "adamw_" + n)
        sent = tuple(d_out[n] for n in names)
    pack = lambda t: _pack_small(*[t[n] for n in SMALL])
    d_small, m_small, v_small = _adamw_packed(pack(w), small_g, pack(m), pack(v), "adamw_small")
    for dst, a in ((g_out, small_g), (d_out, d_small), (m_out, m_small), (v_out, v_small)):
        dst.update(_unpack_small(a))

    return (small_g[7, 2 * POOL_WIDTH],dx.reshape(1, seq, D_MODEL), *[g_out[n] for n in ORDER], *[d_out[n] for n in ORDER],
            *[m_out[n] for n in ORDER], *[v_out[n] for n in ORDER])
```

```python
import functools

import jax
import jax.numpy as jnp
from jax import lax
from jax.experimental import pallas as pl
from jax.experimental.pallas import tpu as pltpu

F32 = jnp.float32
BF16 = jnp.bfloat16

D_MODEL = 1024
HEAD_DIM = 64
POOL_WIDTH = 256
POOL_WINDOWS = (2, 4, 8, 16)
POOL_HALO = 16
GROUP_WIDTH = 256
DILATIONS = (1, 4, 16)
ATTN_BLOCK = 128
ROT_SHIFT = 8
ROPE_THETA = 500000.0
D_FF = 4096
FF_BLOCK = 512
FF_PER_STEP = 2
FF_PER_STEP_BWD = 4
N_DEV = 8
N_IN = POOL_WIDTH + 3 * 768
PLE_DIM = 256
EPS = 1e-6
NEG_BIG = -1e30

ADAM_LR = 0.001
ADAM_B1 = 0.9
ADAM_B2 = 0.999
ADAM_EPS = 1e-08
ADAM_WD = 0.01
ADAM_STEP = 10

LANES = 128
VMEM_LIMIT = 56 * 1024 * 1024
MESH = pl.DeviceIdType.MESH


def _params(n_grid):
    return pltpu.CompilerParams(dimension_semantics=("arbitrary",) * n_grid, vmem_limit_bytes=VMEM_LIMIT)


def _dot(a, b):
    return jnp.dot(a, b, preferred_element_type=F32)


def _dot_nt(a, b):
    return lax.dot_general(a, b, (((1,), (1,)), ((), ())), preferred_element_type=F32)


def _dot_tn(a, b):
    return lax.dot_general(a, b, (((0,), (0,)), ((), ())), preferred_element_type=F32)


def _rms(x, g):
    rstd = lax.rsqrt(jnp.mean(x * x, axis=-1, keepdims=True) + EPS)
    n = x * rstd
    return n, rstd, n * g


def _rms_bwd(dy, n, rstd, g):
    dyn = dy * g
    dx = rstd * (dyn - n * jnp.mean(dyn * n, axis=-1, keepdims=True))
    return dx, jnp.sum(dy * n, axis=0, keepdims=True)


def _ordered_after(body, n_in, after):
    if not after:
        return body
    return lambda *refs: body(*refs[:n_in], *refs[n_in + len(after):])


def _row_tile(s, t):
    t = min(s, t)
    assert s % t == 0
    return t


def _rot(z, c, sa, sb):
    return z * c + pltpu.roll(z, ROT_SHIFT, 1) * sa + pltpu.roll(z, LANES - ROT_SHIFT, 1) * sb


def _table_specs(t):
    return [pl.BlockSpec((t, 2 * ROT_SHIFT), lambda i: (i, 0)), pl.BlockSpec((2 * ROT_SHIFT, 3 * LANES), lambda i: (0, 0)),
            pl.BlockSpec((1, 3 * LANES), lambda i: (0, 0))]


def _expand_tables(cs_ref, spread_ref, base_ref):
    tab = jnp.dot(cs_ref[...], spread_ref[...], precision=lax.Precision.HIGHEST, preferred_element_type=F32)
    tab = tab + base_ref[...]
    return tab[:, 0:LANES], tab[:, LANES:2 * LANES], tab[:, 2 * LANES:3 * LANES]


def _rot_t(dz, c, sa, sb):
    return dz * c + pltpu.roll(dz * sa, LANES - ROT_SHIFT, 1) + pltpu.roll(dz * sb, ROT_SHIFT, 1)


def _to_residues(value, stage, out_ref, dil):
    if dil == 1:
        out_ref[0] = value.astype(out_ref.dtype)
        return
    rows = value.shape[0] // dil
    for hf in range(GROUP_WIDTH // LANES):
        lanes = slice(hf * LANES, (hf + 1) * LANES)
        stage[hf][...] = value[:, lanes]
        for r in range(dil):
            out_ref[r, :, lanes] = stage[hf][pl.ds(r, rows, stride=dil), :].astype(out_ref.dtype)


def _from_residues(in_ref, stage, dil):
    if dil == 1:
        return in_ref[0].astype(F32)
    rows = in_ref.shape[1]
    for hf in range(GROUP_WIDTH // LANES):
        for r in range(dil):
            stage[hf][pl.ds(r, rows, stride=dil), :] = in_ref[r, :, hf * LANES:(hf + 1) * LANES].astype(F32)
    return jnp.concatenate([stage[0][...], stage[1][...]], axis=1)


def _residue_spec(dil, t):
    return pl.BlockSpec((dil, t // dil, GROUP_WIDTH), lambda i: (0, i, 0))


def _residue_shape(dil, s, dtype):
    return jax.ShapeDtypeStruct((dil, s // dil, GROUP_WIDTH), dtype)


def _stages(t, n):
    return [pltpu.VMEM((t, LANES), F32)] * (n * (GROUP_WIDTH // LANES))


def _pair_stages(refs):
    return [refs[i:i + 2] for i in range(0, len(refs), 2)]


def _normproj_fwd(h, g, w_in, rc, rsa, rsb, name):
    s = h.shape[0]
    t = _row_tile(s, 512)

    def body(h_ref, g_ref, w_ref, c_ref, sa_ref, sb_ref, hn_ref, u_ref, *rest):
        qkv_refs, stages = rest[:9], _pair_stages(rest[9:])
        _, _, hn = _rms(h_ref[...], g_ref[...])
        hb = hn.astype(BF16)
        hn_ref[...] = hb
        c, sa, sb = _expand_tables(c_ref, sa_ref, sb_ref)

        def rot(z, scale):
            halves = [_rot(z[:, hf * LANES:(hf + 1) * LANES], c, sa, sb) * scale for hf in range(2)]
            return jnp.concatenate(halves, axis=1)

        u_ref[...] = _dot(hb, w_ref[:, 0:POOL_WIDTH])
        for grp, dil in enumerate(DILATIONS):
            lo = POOL_WIDTH + grp * GROUP_WIDTH
            q_ref, k_ref, v_ref = qkv_refs[3 * grp:3 * grp + 3]
            _to_residues(rot(_dot(hb, w_ref[:, lo:lo + GROUP_WIDTH]), HEAD_DIM ** -0.5), stages[0], q_ref, dil)
            _to_residues(rot(_dot(hb, w_ref[:, lo + 768:lo + 768 + GROUP_WIDTH]), 1.0), stages[1], k_ref, dil)
            _to_residues(_dot(hb, w_ref[:, lo + 1536:lo + 1536 + GROUP_WIDTH]), stages[2], v_ref, dil)

    row = lambda w: pl.BlockSpec((t, w), lambda i: (i, 0))
    return pl.pallas_call(
        body, name=name, grid=(s // t,),
        in_specs=[row(D_MODEL), pl.BlockSpec((1, D_MODEL), lambda i: (0, 0)),
                  pl.BlockSpec((D_MODEL, N_IN), lambda i: (0, 0))] + _table_specs(t),
        out_specs=[row(D_MODEL), row(POOL_WIDTH)] + [_residue_spec(dil, t) for dil in DILATIONS for _ in range(3)],
        out_shape=[jax.ShapeDtypeStruct((s, D_MODEL), BF16), jax.ShapeDtypeStruct((s, POOL_WIDTH), F32)]
        + [_residue_shape(dil, s, BF16) for dil in DILATIONS for _ in range(3)],
        scratch_shapes=_stages(t, 3),
        compiler_params=_params(1),
    )(h, g, w_in, rc, rsa, rsb)


def _pool_lane_window():
    lane = lax.broadcasted_iota(jnp.int32, (1, POOL_WIDTH), 1)
    return jnp.left_shift(2, lane // (POOL_WIDTH // len(POOL_WINDOWS)))


def _pool_fwd(u, w_bd, scale, name, after=()):
    s = u.shape[0]
    t = _row_tile(s, 512)

    def body(u_ref, w_ref, sc_ref, out_ref, y_ref, ext):
        i = pl.program_id(0)

        @pl.when(i == 0)
        def _():
            ext[0:POOL_HALO, :] = jnp.zeros((POOL_HALO, POOL_WIDTH), F32)

        x = u_ref[...]
        ext[POOL_HALO:, :] = x
        win = _pool_lane_window()
        acc = x
        wsum = jnp.zeros_like(x)
        for k in range(1, POOL_HALO):
            acc = acc + ext[POOL_HALO - k:POOL_HALO - k + t, :]
            if k + 1 in POOL_WINDOWS:
                wsum = jnp.where(win == k + 1, acc, wsum)
        pos = i * t + lax.broadcasted_iota(jnp.int32, (t, POOL_WIDTH), 0)
        cnt = jnp.minimum(pos + 1, win).astype(F32)
        y = wsum / cnt - x
        yb = y.astype(BF16)
        y_ref[...] = yb
        out_ref[...] = _dot(yb, w_ref[...]) * sc_ref[...]
        ext[0:POOL_HALO, :] = x[t - POOL_HALO:, :]

    row = pl.BlockSpec((t, POOL_WIDTH), lambda i: (i, 0))
    return pl.pallas_call(
        _ordered_after(body, 3, after), name=name, grid=(s // t,),
        in_specs=[row, pl.BlockSpec((POOL_WIDTH, POOL_WIDTH), lambda i: (0, 0)),
                  pl.BlockSpec((1, POOL_WIDTH), lambda i: (0, 0))] + [pl.BlockSpec(memory_space=pl.ANY)] * len(after),
        out_specs=[row, row],
        out_shape=[jax.ShapeDtypeStruct((s, POOL_WIDTH), F32), jax.ShapeDtypeStruct((s, POOL_WIDTH), BF16)],
        scratch_shapes=[pltpu.VMEM((t + POOL_HALO, POOL_WIDTH), F32)],
        compiler_params=_params(1),
    )(u, w_bd, scale, *after)


def _head_masks():
    lane = lax.broadcasted_iota(jnp.int32, (ATTN_BLOCK, GROUP_WIDTH), 1)
    return [lane // HEAD_DIM == hd for hd in range(GROUP_WIDTH // HEAD_DIM)]


def _stack_heads(a, masks):
    zero = jnp.zeros_like(a)
    return jnp.concatenate([jnp.where(m, a, zero) for m in masks], axis=0)


def _band_bias(first_step):
    rows = ATTN_BLOCK * (GROUP_WIDTH // HEAD_DIM)
    i = lax.broadcasted_iota(jnp.int32, (rows, 2 * ATTN_BLOCK), 0) & (ATTN_BLOCK - 1)
    j = lax.broadcasted_iota(jnp.int32, (rows, 2 * ATTN_BLOCK), 1)
    inner = jnp.where((j >= i) & (j <= i + ATTN_BLOCK), 0.0, NEG_BIG)
    return jnp.where((j < ATTN_BLOCK) & first_step, NEG_BIG, inner), inner


def _column_per_head(a):
    return jnp.concatenate([a[:, hd * HEAD_DIM:hd * HEAD_DIM + 1] for hd in range(GROUP_WIDTH // HEAD_DIM)], axis=0)


def _blocks_per_step(nb):
    return 8 if nb % 8 == 0 else 4 if nb % 4 == 0 else 2 if nb % 2 == 0 else 1


def _attn_fwd(q, k, v, name, after=()):
    dil, length, _ = q.shape
    nb = length // ATTN_BLOCK
    qb = _blocks_per_step(nb)

    def body(q_ref, kp_ref, kc_ref, vp_ref, vc_ref, o_ref, lse_ref):
        masks = _head_masks()
        bias = _band_bias(pl.program_id(1) == 0)
        for qi in range(qb):
            here = slice(qi * ATTN_BLOCK, (qi + 1) * ATTN_BLOCK)
            before = slice((qi - 1) * ATTN_BLOCK, qi * ATTN_BLOCK)
            kcat = jnp.concatenate([kp_ref[...] if qi == 0 else kc_ref[before], kc_ref[here]], axis=0)
            vcat = jnp.concatenate([vp_ref[...] if qi == 0 else vc_ref[before], vc_ref[here]], axis=0)
            qs = _stack_heads(q_ref[here], masks)
            sc = _dot_nt(qs, kcat) + bias[min(qi, 1)]
            m = jnp.max(sc, axis=1, keepdims=True)
            e = jnp.exp(sc - m)
            l = jnp.sum(e, axis=1, keepdims=True)
            p = (e / l).astype(BF16)
            lse = m + jnp.log(l)
            o = jnp.zeros((ATTN_BLOCK, GROUP_WIDTH), F32)
            lse_full = jnp.zeros((ATTN_BLOCK, GROUP_WIDTH), F32)
            for hd, msk in enumerate(masks):
                rows = slice(hd * ATTN_BLOCK, (hd + 1) * ATTN_BLOCK)
                o = jnp.where(msk, _dot(p[rows], vcat), o)
                lse_full = jnp.where(msk, lse[rows], lse_full)
            o_ref[here] = o.astype(o_ref.dtype)
            lse_ref[here] = lse_full

    cur = pl.BlockSpec((None, qb * ATTN_BLOCK, GROUP_WIDTH), lambda r, j: (r, j, 0))
    prev = pl.BlockSpec((None, ATTN_BLOCK, GROUP_WIDTH), lambda r, j: (r, jnp.maximum(qb * j - 1, 0), 0))
    return pl.pallas_call(
        _ordered_after(body, 5, after), name=name, grid=(dil, nb // qb),
        in_specs=[cur, prev, cur, prev, cur] + [pl.BlockSpec(memory_space=pl.ANY)] * len(after), out_specs=[cur, cur],
        out_shape=[jax.ShapeDtypeStruct(q.shape, BF16), jax.ShapeDtypeStruct(q.shape, F32)],
        compiler_params=_params(2),
    )(q, k, k, v, v, *after)


def _group_weights(l0, l1, l2):
    m = jnp.maximum(jnp.maximum(l0, l1), l2)
    e0, e1, e2 = jnp.exp(l0 - m), jnp.exp(l1 - m), jnp.exp(l2 - m)
    den = e0 + e1 + e2
    return e0 / den, e1 / den, e2 / den


def _outproj_fwd(h, pool_out, o, lse, w_out, name):
    s = h.shape[0]
    t = _row_tile(s, 512)

    def body(h_ref, po_ref, o0, o1, o2, l0, l1, l2, w_ref, out_ref, a_ref, *stages):
        stages = _pair_stages(stages)
        ov =[_from_residues(r, stages[i], DILATIONS[i]) for i, r in enumerate((o0, o1, o2))]
        lv = [_from_residues(r, stages[3 + i], DILATIONS[i]) for i, r in enumerate((l0, l1, l2))]
        wts = _group_weights(*lv)
        a = jnp.concatenate([po_ref[...]] + [ov[i] * wts[i] for i in range(3)], axis=1).astype(BF16)
        a_ref[...] = a
        out_ref[...] = h_ref[...] + _dot(a, w_ref[...])

    row = lambda w: pl.BlockSpec((t, w), lambda i: (i, 0))
    res = [_residue_spec(dil, t) for dil in DILATIONS]
    return pl.pallas_call(
        body, name=name, grid=(s // t,),
        in_specs=[row(D_MODEL), row(POOL_WIDTH)] + res + res + [pl.BlockSpec((D_MODEL, D_MODEL), lambda i: (0, 0))],
        out_specs=[row(D_MODEL), row(D_MODEL)],
        out_shape=[jax.ShapeDtypeStruct((s, D_MODEL), F32), jax.ShapeDtypeStruct((s, D_MODEL), BF16)],
        scratch_shapes=_stages(t, 6),
        compiler_params=_params(1),
    )(h, pool_out, *o, *lse, w_out)


def _mlp_fwd(h, g, w_up, w_down, name):
    s = h.shape[0]
    t = _row_tile(s, 1024)
    nblk = D_FF // (FF_PER_STEP * FF_BLOCK)

    def body(h_ref, g_ref, wu_ref, wd_ref, out_ref, hn_ref, r_ref, hb_s, acc):
        j = pl.program_id(1)

        @pl.when(j == 0)
        def _():
            _, _, hn = _rms(h_ref[...], g_ref[...])
            hb = hn.astype(BF16)
            hb_s[...] = hb
            hn_ref[...] = hb
            acc[...] = jnp.zeros_like(acc)

        hb = hb_s[...]
        acts = []
        for b in range(FF_PER_STEP):
            r = jnp.maximum(_dot(hb, wu_ref[b]), 0.0)
            r_ref[:, b * FF_BLOCK:(b + 1) * FF_BLOCK] = r.astype(BF16)
            acts.append((r * r).astype(BF16))
        acc[...] += _dot(jnp.concatenate(acts, axis=1), wd_ref[...].reshape(FF_PER_STEP * FF_BLOCK, D_MODEL))

        @pl.when(j == nblk - 1)
        def _():
            out_ref[...] = h_ref[...] + acc[...]

    row = pl.BlockSpec((t, D_MODEL), lambda i, j: (i, 0))
    return pl.pallas_call(
        body, name=name, grid=(s // t, nblk),
        in_specs=[row, pl.BlockSpec((1, D_MODEL), lambda i, j: (0, 0)),
                  pl.BlockSpec((FF_PER_STEP, D_MODEL, FF_BLOCK), lambda i, j: (j, 0, 0)),
                  pl.BlockSpec((FF_PER_STEP, FF_BLOCK, D_MODEL), lambda i, j: (j, 0, 0))],
        out_specs=[row, row, pl.BlockSpec((t, FF_PER_STEP * FF_BLOCK), lambda i, j: (i, j))],
        out_shape=[jax.ShapeDtypeStruct((s, D_MODEL), F32), jax.ShapeDtypeStruct((s, D_MODEL), BF16),
                   jax.ShapeDtypeStruct((s, D_FF), BF16)],
        scratch_shapes=[pltpu.VMEM((t, D_MODEL), BF16), pltpu.VMEM((t, D_MODEL), F32)],
        compiler_params=_params(2),
    )(h, g, w_up, w_down)


def _gate_fwd(h, g, w_gate, p, layer, w_ple, name):
    s = h.shape[0]
    t = _row_tile(s, 512)

    def body(h_ref, g_ref, wg_ref, p_ref, wp_ref, out_ref, hn_ref, gate_ref, pb_ref):
        x = h_ref[...]
        _, _, hn = _rms(x, g_ref[...])
        hb = hn.astype(BF16)
        hn_ref[...] = hb
        gate = 1.0 / (1.0 + jnp.exp(-_dot(hb, wg_ref[...])))
        pb = p_ref[...].astype(BF16)
        pb_ref[...] = pb
        gate_ref[...] = gate.astype(BF16)
        out_ref[...] = x + gate * _dot(pb, wp_ref[...])

    row = lambda w: pl.BlockSpec((t, w), lambda i: (i, 0))
    full = lambda a, b: pl.BlockSpec((a, b), lambda i: (0, 0))
    return pl.pallas_call(
        body, name=name, grid=(s // t,),
        in_specs=[row(D_MODEL), full(1, D_MODEL), full(D_MODEL, D_MODEL),
                  pl.BlockSpec((None, t, PLE_DIM), lambda i: (layer, i, 0)), full(PLE_DIM, D_MODEL)],
        out_specs=[row(D_MODEL), row(D_MODEL), row(D_MODEL), row(PLE_DIM)],
        out_shape=[jax.ShapeDtypeStruct((s, D_MODEL), F32), jax.ShapeDtypeStruct((s, D_MODEL), BF16),
                   jax.ShapeDtypeStruct((s, D_MODEL), BF16), jax.ShapeDtypeStruct((s, PLE_DIM), BF16)],
        compiler_params=_params(1),
    )(h, g, w_gate, p, w_ple)


def _loss_head(h, g, target, name):
    s = h.shape[0]
    t = _row_tile(s, 512)

    def body(h_ref, g_ref, t_ref, loss_ref, dh_ref, dg_ref):
        i = pl.program_id(0)

        @pl.when(i == 0)
        def _():
            loss_ref[...] = jnp.zeros_like(loss_ref)
            dg_ref[...] = jnp.zeros_like(dg_ref)

        gv = g_ref[...]
        n, rstd, y = _rms(h_ref[...], gv)
        err = y - t_ref[...]
        loss_ref[...] += jnp.sum(err * err) * (0.5 / D_MODEL)
        dx, dg = _rms_bwd(err * (1.0 / D_MODEL), n, rstd, gv)
        dh_ref[...] = dx
        dg_ref[...] += dg

    row = pl.BlockSpec((t, D_MODEL), lambda i: (i, 0))
    vec = pl.BlockSpec((1, D_MODEL), lambda i: (0, 0))
    return pl.pallas_call(
        body, name=name, grid=(s // t,),
        in_specs=[row, vec, row],
        out_specs=[pl.BlockSpec((1, LANES), lambda i: (0, 0)), row, vec],
        out_shape=[jax.ShapeDtypeStruct((1, LANES), F32), jax.ShapeDtypeStruct((s, D_MODEL), F32),
                   jax.ShapeDtypeStruct((1, D_MODEL), F32)],
        compiler_params=_params(1),
    )(h, g, target)


def _gate_bwd(dh, gate, pb, w_ple, h, g, w_gate, name, after=()):
    s = h.shape[0]
    t = _row_tile(s, 512)

    def body(dh_ref, gate_ref, pb_ref, wp_ref, h_ref, g_ref, wg_ref, out_ref, dgl_ref, de_ref, dg_ref):
        @pl.when(pl.program_id(0) == 0)
        def _():
            dg_ref[...] = jnp.zeros_like(dg_ref)

        d = dh_ref[...]
        gate = gate_ref[...].astype(F32)
        e = _dot(pb_ref[...], wp_ref[...])
        dgl = (d * e * gate * (1.0 - gate)).astype(BF16)
        dgl_ref[...] = dgl
        de_ref[...] = (d * gate).astype(BF16)
        gv = g_ref[...]
        n, rstd, _ = _rms(h_ref[...], gv)
        dx, dg = _rms_bwd(_dot_nt(dgl, wg_ref[...]), n, rstd, gv)
        out_ref[...] = d + dx
        dg_ref[...] += dg

    row = lambda w: pl.BlockSpec((t, w), lambda i: (i, 0))
    full = lambda a, b: pl.BlockSpec((a, b), lambda i: (0, 0))
    return pl.pallas_call(
        _ordered_after(body, 7, after), name=name, grid=(s // t,),
        in_specs=[row(D_MODEL), row(D_MODEL), row(PLE_DIM), full(PLE_DIM, D_MODEL), row(D_MODEL), full(1, D_MODEL),
                  full(D_MODEL, D_MODEL)] + [pl.BlockSpec(memory_space=pl.ANY)] * len(after),
        out_specs=[row(D_MODEL), row(D_MODEL), row(D_MODEL), full(1, D_MODEL)],
        out_shape=[jax.ShapeDtypeStruct((s, D_MODEL), F32), jax.ShapeDtypeStruct((s, D_MODEL), BF16),
                   jax.ShapeDtypeStruct((s, D_MODEL), BF16), jax.ShapeDtypeStruct((1, D_MODEL), F32)],
        compiler_params=_params(1),
    )(dh, gate, pb, w_ple, h, g, w_gate, *after)


def _mlp_bwd(dh, r, h, g, w_up, w_down, name):
    s = h.shape[0]
    t = _row_tile(s, 512)
    per = FF_PER_STEP_BWD
    nblk = D_FF // (per * FF_BLOCK)

    def body(dh_ref, r_ref, h_ref, g_ref, wu_ref, wd_ref, out_ref, dup_ref, dg_ref, db_s, acc):
        i, j = pl.program_id(0), pl.program_id(1)

        @pl.when((i == 0) & (j == 0))
        def _():
            dg_ref[...] = jnp.zeros_like(dg_ref)

        @pl.when(j == 0)
        def _():
            db_s[...] = dh_ref[...].astype(BF16)
            acc[...] = jnp.zeros_like(acc)

        db = db_s[...]
        back = None
        for b in range(per):
            cols = slice(b * FF_BLOCK, (b + 1) * FF_BLOCK)
            dup = (_dot_nt(db, wd_ref[b]) * (2.0 * r_ref[:, cols].astype(F32))).astype(BF16)
            dup_ref[:, cols] = dup
            part = _dot_nt(dup, wu_ref[b])
            back = part if back is None else back + part
        acc[...] += back

        @pl.when(j == nblk - 1)
        def _():
            gv = g_ref[...]
            n, rstd, _ = _rms(h_ref[...], gv)
            dx, dg = _rms_bwd(acc[...], n, rstd, gv)
            out_ref[...] = dh_ref[...] + dx
            dg_ref[...] += dg

    row = pl.BlockSpec((t, D_MODEL), lambda i, j: (i, 0))
    vec = pl.BlockSpec((1, D_MODEL), lambda i, j: (0, 0))
    blk = pl.BlockSpec((t, per * FF_BLOCK), lambda i, j: (i, j))
    return pl.pallas_call(
        body, name=name, grid=(s // t, nblk),
        in_specs=[row, blk, row, vec,
                  pl.BlockSpec((per, D_MODEL, FF_BLOCK), lambda i, j: (j, 0, 0)),
                  pl.BlockSpec((per, FF_BLOCK, D_MODEL), lambda i, j: (j, 0, 0))],
        out_specs=[row, blk, vec, row],
        out_shape=[jax.ShapeDtypeStruct((s, D_MODEL), F32), jax.ShapeDtypeStruct((s, D_FF), BF16),
                   jax.ShapeDtypeStruct((1, D_MODEL), F32), jax.ShapeDtypeStruct((s, D_MODEL), BF16)],
        scratch_shapes=[pltpu.VMEM((t, D_MODEL), F32)],
        compiler_params=_params(2),
    )(dh, r, h, g, w_up, w_down)


def _outproj_bwd(dh, w_out, o, lse, ones_bd, name):
    s = dh.shape[0]
    t = _row_tile(s, 512)

    def body(dh_ref, w_ref, o0, o1, o2, l0, l1, l2, bd_ref, dp_ref, do0, do1, do2, de0, de1, de2, dhb_ref, *stages):
        stages = _pair_stages(stages)
        dhb = dh_ref[...].astype(BF16)
        dhb_ref[...] = dhb
        da = _dot_nt(dhb, w_ref[...])
        dp_ref[...] = da[:, 0:POOL_WIDTH]
        ov =[_from_residues(r, stages[i], DILATIONS[i]) for i, r in enumerate((o0, o1, o2))]
        lv = [_from_residues(r, stages[3 + i], DILATIONS[i]) for i, r in enumerate((l0, l1, l2))]
        wts = _group_weights(*lv)
        bd = bd_ref[...]
        cbar = jnp.zeros((t, GROUP_WIDTH), F32)
        for grp, do_ref in enumerate((do0, do1, do2)):
            lo = POOL_WIDTH + grp * GROUP_WIDTH
            dag = da[:, lo:lo + GROUP_WIDTH]
            _to_residues(dag * wts[grp], stages[6 + grp], do_ref, DILATIONS[grp])
            prod = dag * ov[grp]
            hi = prod.astype(BF16)
            low = (prod - hi.astype(F32)).astype(BF16)
            cbar = cbar + wts[grp] * (_dot(hi, bd) + _dot(low, bd))
        for grp, de_ref in enumerate((de0, de1, de2)):
            _to_residues(wts[grp] * cbar, stages[9 + grp], de_ref, DILATIONS[grp])

    row = lambda w: pl.BlockSpec((t, w), lambda i: (i, 0))
    full = lambda a, b: pl.BlockSpec((a, b), lambda i: (0, 0))
    res = [_residue_spec(dil, t) for dil in DILATIONS]
    return pl.pallas_call(
        body, name=name, grid=(s // t,),
        in_specs=[row(D_MODEL), full(D_MODEL, D_MODEL)] + res + res + [full(GROUP_WIDTH, GROUP_WIDTH)],
        out_specs=[row(POOL_WIDTH)] + res + res + [row(D_MODEL)],
        out_shape=[jax.ShapeDtypeStruct((s, POOL_WIDTH), F32)] + [_residue_shape(dil, s, BF16) for dil in DILATIONS]
        + [_residue_shape(dil, s, F32) for dil in DILATIONS] + [jax.ShapeDtypeStruct((s, D_MODEL), BF16)],
        scratch_shapes=_stages(t, 12),
        compiler_params=_params(1),
    )(dh, w_out, *o, *lse, ones_bd)


def _attn_bwd(q, k, v, do, lse, deff, name, after=()):
    dil, length, _ = q.shape
    nb = length // ATTN_BLOCK
    qb = _blocks_per_step(nb)
    nj = nb // qb
    tail = slice((qb - 1) * ATTN_BLOCK, qb * ATTN_BLOCK)

    def body(q_ref, kp_ref, kc_ref, vp_ref, vc_ref, do_ref, lse_ref, de_ref, dq_ref, dk_ref, dv_ref, ck, cv):
        j = pl.program_id(1)

        @pl.when(j < nj)
        def _():
            masks = _head_masks()
            bias = _band_bias(j == 0)
            dkc, dvc = [], []
            for qi in range(qb):
                here = slice(qi * ATTN_BLOCK, (qi + 1) * ATTN_BLOCK)
                before = slice((qi - 1) * ATTN_BLOCK, qi * ATTN_BLOCK)
                kcat = jnp.concatenate([kp_ref[...] if qi == 0 else kc_ref[before], kc_ref[here]], axis=0)
                vcat = jnp.concatenate([vp_ref[...] if qi == 0 else vc_ref[before], vc_ref[here]], axis=0)
                qs = _stack_heads(q_ref[here], masks)
                dos = _stack_heads(do_ref[here], masks)
                sc = _dot_nt(qs, kcat) + bias[min(qi, 1)]
                p = jnp.exp(sc - _column_per_head(lse_ref[here]))
                ds = (p * (_dot_nt(dos, vcat) - _column_per_head(de_ref[here]))).astype(BF16)
                dq = jnp.zeros((ATTN_BLOCK, GROUP_WIDTH), F32)
                for hd, msk in enumerate(masks):
                    dq = jnp.where(msk, _dot(ds[hd * ATTN_BLOCK:(hd + 1) * ATTN_BLOCK], kcat), dq)
                dq_ref[here] = dq.astype(dq_ref.dtype)
                dkc.append(_dot_tn(ds, qs))
                dvc.append(_dot_tn(p.astype(BF16), dos))

            for out_ref, carry, parts in ((dk_ref, ck, dkc), (dv_ref, cv, dvc)):
                @pl.when(j > 0)
                def _():
                    if qb > 1:
                        out_ref[0:(qb - 1) * ATTN_BLOCK] = carry[0:(qb - 1) * ATTN_BLOCK].astype(out_ref.dtype)
                    out_ref[tail] = (carry[tail] + parts[0][0:ATTN_BLOCK]).astype(out_ref.dtype)

                for qi in range(qb - 1):
                    carry[qi * ATTN_BLOCK:(qi + 1) * ATTN_BLOCK] = parts[qi][ATTN_BLOCK:] + parts[qi + 1][0:ATTN_BLOCK]
                carry[tail] = parts[qb - 1][ATTN_BLOCK:]

        @pl.when(j == nj)
        def _():
            dk_ref[...] = ck[...].astype(dk_ref.dtype)
            dv_ref[...] = cv[...].astype(dv_ref.dtype)

    step = lambda j: jnp.minimum(j, nj - 1)
    cur = pl.BlockSpec((None, qb * ATTN_BLOCK, GROUP_WIDTH), lambda r, j: (r, step(j), 0))
    prev = pl.BlockSpec((None, ATTN_BLOCK, GROUP_WIDTH), lambda r, j: (r, jnp.maximum(qb * step(j) - 1, 0), 0))
    late = pl.BlockSpec((None, qb * ATTN_BLOCK, GROUP_WIDTH), lambda r, j: (r, jnp.maximum(j - 1, 0), 0))
    return pl.pallas_call(
        _ordered_after(body, 8, after), name=name, grid=(dil, nj + 1),
        in_specs=[cur, prev, cur, prev, cur, cur, cur, cur] + [pl.BlockSpec(memory_space=pl.ANY)] * len(after),
        out_specs=[cur, late, late],
        out_shape=[jax.ShapeDtypeStruct(q.shape, BF16)] * 3,
        scratch_shapes=[pltpu.VMEM((qb * ATTN_BLOCK, GROUP_WIDTH), F32)] * 2,
        compiler_params=_params(2),
    )(q, k, k, v, v, do, lse, deff, *after)


def _pool_bwd(dpool, y, w_bd, scale, name, after=()):
    s = dpool.shape[0]
    t = _row_tile(s, 512)
    nt = s // t

    def body(dp_ref, y_ref, w_ref, sc_ref, du_ref, dw_ref, dsc_ref, ext):
        i = pl.program_id(0)

        @pl.when(i == 0)
        def _():
            ext[t:, :] = jnp.zeros((POOL_HALO, POOL_WIDTH), F32)
            dw_ref[...] = jnp.zeros_like(dw_ref)
            dsc_ref[...] = jnp.zeros_like(dsc_ref)

        dp = dp_ref[...]
        yb = y_ref[...]
        w = w_ref[...]
        dsc_ref[...] += jnp.sum(dp * _dot(yb, w), axis=0, keepdims=True)
        dyo = (dp * sc_ref[...]).astype(BF16)
        dw_ref[...] += _dot_tn(yb, dyo)
        dy = _dot_nt(dyo, w)
        win = _pool_lane_window()
        pos = (nt - 1 - i) * t + lax.broadcasted_iota(jnp.int32, (t, POOL_WIDTH), 0)
        gq = dy / jnp.minimum(pos + 1, win).astype(F32)
        ext[0:t, :] = gq
        acc = gq
        wsum = jnp.zeros_like(gq)
        for k in range(1, POOL_HALO):
            acc = acc + ext[k:k + t, :]
            if k + 1 in POOL_WINDOWS:
                wsum = jnp.where(win == k + 1, acc, wsum)
        du_ref[...] = wsum - dy
        ext[t:, :] = gq[0:POOL_HALO, :]

    rev = pl.BlockSpec((t, POOL_WIDTH), lambda i: (nt - 1 - i, 0))
    full = lambda a, b: pl.BlockSpec((a, b), lambda i: (0, 0))
    return pl.pallas_call(
        _ordered_after(body, 4, after), name=name, grid=(nt,),
        in_specs=[rev, rev, full(POOL_WIDTH, POOL_WIDTH), full(1, POOL_WIDTH)]
        + [pl.BlockSpec(memory_space=pl.ANY)] * len(after),
        out_specs=[rev, full(POOL_WIDTH, POOL_WIDTH), full(1, POOL_WIDTH)],
        out_shape=[jax.ShapeDtypeStruct((s, POOL_WIDTH), F32), jax.ShapeDtypeStruct((POOL_WIDTH, POOL_WIDTH), F32),
                   jax.ShapeDtypeStruct((1, POOL_WIDTH), F32)],
        scratch_shapes=[pltpu.VMEM((t + POOL_HALO, POOL_WIDTH), F32)],
        compiler_params=_params(1),
    )(dpool, y, w_bd, scale, *after)


def _normproj_bwd(dh, du, dq, dk, dv, rc, rsa, rsb, w_in, h, g, name):
    s = h.shape[0]
    t = _row_tile(s, 512)

    def body(dh_ref, du_ref, q0, q1, q2, k0, k1, k2, v0, v1, v2, c_ref, sa_ref, sb_ref, w_ref, h_ref, g_ref,
             out_ref, dz_ref, dg_ref, *stages):
        @pl.when(pl.program_id(0) == 0)
        def _():
            dg_ref[...] = jnp.zeros_like(dg_ref)

        c, sa, sb = _expand_tables(c_ref, sa_ref, sb_ref)

        def unrot(a, scale):
            halves = [_rot_t(a[:, hf * LANES:(hf + 1) * LANES] * scale, c, sa, sb) for hf in range(2)]
            return jnp.concatenate(halves, axis=1)

        staged = _pair_stages(stages)
        tok = lambda refs, base: [_from_residues(r, staged[base + i], DILATIONS[i]) for i, r in enumerate(refs)]
        chunks = [du_ref[...]]
        chunks += [unrot(a, HEAD_DIM ** -0.5) for a in tok((q0, q1, q2), 0)]
        chunks += [unrot(a, 1.0) for a in tok((k0, k1, k2), 3)]
        chunks += tok((v0, v1, v2), 6)
        acc = jnp.zeros((t, D_MODEL), F32)
        for ci, ch in enumerate(chunks):
            cols = slice(ci * GROUP_WIDTH, (ci + 1) * GROUP_WIDTH)
            cb = ch.astype(BF16)
            dz_ref[:, cols] = cb
            acc = acc + _dot_nt(cb, w_ref[:, cols])
        gv = g_ref[...]
        n, rstd, _ = _rms(h_ref[...], gv)
        dx, dg = _rms_bwd(acc, n, rstd, gv)
        out_ref[...] = dh_ref[...] + dx
        dg_ref[...] += dg

    row = lambda w: pl.BlockSpec((t, w), lambda i: (i, 0))
    vec = pl.BlockSpec((1, D_MODEL), lambda i: (0, 0))
    res = [_residue_spec(dil, t) for dil in DILATIONS]
    return pl.pallas_call(
        body, name=name, grid=(s // t,),
        in_specs=[row(D_MODEL), row(POOL_WIDTH)] + res * 3 + _table_specs(t)
        + [pl.BlockSpec((D_MODEL, N_IN), lambda i: (0, 0)), row(D_MODEL), vec],
        out_specs=[row(D_MODEL), row(N_IN), vec],
        out_shape=[jax.ShapeDtypeStruct((s, D_MODEL), F32), jax.ShapeDtypeStruct((s, N_IN), BF16),
                   jax.ShapeDtypeStruct((1, D_MODEL), F32)],
        scratch_shapes=_stages(t, 9),
        compiler_params=_params(1),
    )(dh, du, *dq, *dk, *dv, rc, rsa, rsb, w_in, h, g)


def _matmul_tn(a, b, name, *, square_a=False, tn=None, blocked_out=False):
    s, m = a.shape
    n = b.shape[1]
    tk = _row_tile(s, 2048)
    tm = min(m, 1024)
    tn = tn or min(n, 1024)
    assert m % tm == 0 and n % tn == 0
    nk = s // tk
    nsub = tn // FF_BLOCK if blocked_out else 1

    def body(a_ref, b_ref, o_ref, ob_ref, acc):
        k = pl.program_id(2)

        @pl.when(k == 0)
        def _():
            acc[...] = jnp.zeros_like(acc)

        av = a_ref[...]
        if square_a:
            av = av.astype(F32)
            av = av * av
        acc[...] += _dot_tn(av.astype(BF16), b_ref[...].astype(BF16))

        @pl.when(k == nk - 1)
        def _():
            if blocked_out:
                for sub in range(nsub):
                    cols = slice(sub * FF_BLOCK, (sub + 1) * FF_BLOCK)
                    o_ref[sub] = acc[:, cols]
                    ob_ref[sub] = acc[:, cols].astype(BF16)
            else:
                o_ref[...] = acc[...]
                ob_ref[...] = acc[...].astype(BF16)

    if blocked_out:
        shape = (n // FF_BLOCK, m, FF_BLOCK)
        out_spec = pl.BlockSpec((nsub, tm, FF_BLOCK), lambda i, j, k: (j, i, 0))
    else:
        shape = (m, n)
        out_spec = pl.BlockSpec((tm, tn), lambda i, j, k: (i, j))
    return pl.pallas_call(
        body, name=name, grid=(m // tm, n // tn, nk),
        in_specs=[pl.BlockSpec((tk, tm), lambda i, j, k: (k, i)), pl.BlockSpec((tk, tn), lambda i, j, k: (k, j))],
        out_specs=[out_spec, out_spec],
        out_shape=[jax.ShapeDtypeStruct(shape, F32), jax.ShapeDtypeStruct(shape, BF16)],
        scratch_shapes=[pltpu.VMEM((tm, tn), F32)],
        compiler_params=_params(3),
    )(a, b)


def _adamw_math(w, g, m, v):
    m = ADAM_B1 * m + (1.0 - ADAM_B1) * g
    v = ADAM_B2 * v + (1.0 - ADAM_B2) * (g * g)
    m_hat = m / (1.0 - ADAM_B1 ** ADAM_STEP)
    v_hat = v / (1.0 - ADAM_B2 ** ADAM_STEP)
    delta = -ADAM_LR * (m_hat / (jnp.sqrt(v_hat) + ADAM_EPS) + ADAM_WD * w)
    return delta, m, v


def _adamw_sharded(w, m, v, own, recv0, recv1, name):
    _, rows, cols = w.shape
    t = _row_tile(rows, 256)

    def body(w_ref, m_ref, v_ref, own_ref, r0_ref, r1_ref, g_ref, d_ref, nm_ref, nv_ref):
        layer0 = pl.program_id(0) == 0
        g = own_ref[...]
        for k in range(N_DEV - 1):
            g = g + jnp.where(layer0, r0_ref[k], r1_ref[k]).astype(F32)
        g_ref[...] = g
        d_ref[...], nm_ref[...], nv_ref[...] = _adamw_math(w_ref[...], g, m_ref[...], v_ref[...])

    blk = pl.BlockSpec((None, t, cols), lambda l, i: (l, i, 0))
    recv = lambda layer: pl.BlockSpec((N_DEV - 1, t, cols), lambda l, i: (0, jnp.where(l == layer, i, 0), 0))
    return pl.pallas_call(
        body, name=name, grid=(2, rows // t),
        in_specs=[blk, blk, blk, blk, recv(0), recv(1)], out_specs=[blk] * 4,
        out_shape=[jax.ShapeDtypeStruct(w.shape, F32)] * 4,
        compiler_params=_params(2),
    )(w, m, v, own, recv0, recv1)


def _adamw_packed(w, g, m, v, name):
    def body(w_ref, g_ref, m_ref, v_ref, d_ref, nm_ref, nv_ref):
        d_ref[...], nm_ref[...], nv_ref[...] = _adamw_math(w_ref[...], g_ref[...], m_ref[...], v_ref[...])

    return pl.pallas_call(
        body, name=name, out_shape=[jax.ShapeDtypeStruct(w.shape, F32)] * 3,
        compiler_params=pltpu.CompilerParams(vmem_limit_bytes=VMEM_LIMIT),
    )(w, g, m, v)


def _peer(k):
    x, y, c = lax.axis_index("x"), lax.axis_index("y"), lax.axis_index("c")
    return (1 - x if k & 4 else x, 1 - y if k & 2 else y, 1 - c if k & 1 else c)


def _linear(dev):
    return 4 * dev[0] + 2 * dev[1] + dev[2]


HBM_SPEC = pl.BlockSpec(memory_space=pltpu.HBM)
SEM_SPEC = pl.BlockSpec(memory_space=pltpu.SEMAPHORE)
ANY_SPEC = pl.BlockSpec(memory_space=pl.ANY)
EFFECT = pltpu.SideEffectType.DATAFLOW_SIDE_EFFECTING


def _in_hbm(a):
    return pltpu.with_memory_space_constraint(a, pltpu.HBM)


class _Exchange:
    def __init__(self, name, groups, scatter, after=()):
        self.name, self.scatter = name, scatter
        self.sizes = sizes = [len(g) for g in groups]
        srcs = [a for g in groups for a in g]
        n, ng = len(srcs), len(groups)
        lead = (N_DEV - 1,) if scatter else (N_DEV,)
        shapes = [lead + (a.shape[1:] if scatter else a.shape) for a in srcs]
        lands = [lax.empty(sh, a.dtype) for sh, a in zip(shapes, srcs)]
        offsets = [sum(sizes[:gi]) for gi in range(ng)]
        copy = self._copy

        def body(*refs):
            src, land = refs[:n], refs[n:2 * n]
            sems = refs[2 * n + len(after):2 * n + len(after) + 2 * ng]
            token = refs[-1]
            for gi in range(ng):
                for wi in range(sizes[gi]):
                    w = offsets[gi] + wi
                    for k in range(1, N_DEV):
                        copy(src[w], land[w], sems[2 * gi], sems[2 * gi + 1], wi, k).start()
            token[...] = jnp.zeros_like(token)

        sem_shapes = [pltpu.SemaphoreType.DMA((7 * sz,)) for sz in sizes for _ in range(2)]
        outs = pl.pallas_call(
            body, name=name + "_start",
            in_specs=[HBM_SPEC] * (2 * n) + [ANY_SPEC] * len(after),
            out_specs=[SEM_SPEC] * (2 * ng) + [HBM_SPEC] * (2 * n) + [pl.BlockSpec(memory_space=pltpu.VMEM)],
            out_shape=sem_shapes + [pltpu.HBM(a.shape, a.dtype) for a in srcs + lands]
            + [jax.ShapeDtypeStruct((8, LANES), F32)],
            input_output_aliases={i: 2 * ng + i for i in range(2 * n)},
            compiler_params=pltpu.CompilerParams(has_side_effects=EFFECT),
        )(*[_in_hbm(a) for a in srcs + lands], *after)
        self.sems = [outs[2 * gi:2 * gi + 2] for gi in range(ng)]
        thru = outs[2 * ng:2 * ng + 2 * n]
        self.srcs = [thru[offsets[gi]:offsets[gi] + sizes[gi]] for gi in range(ng)]
        self.lands = [thru[n + offsets[gi]:n + offsets[gi] + sizes[gi]] for gi in range(ng)]
        self.token = outs[-1]

    def _copy(self, src, land, send_sems, recv_sems, wi, k):
        to = _peer(k)
        if self.scatter:
            src_ref, dst_ref = src.at[_linear(to)], land.at[k - 1]
        else:
            src_ref, dst_ref = src, land.at[_linear(_peer(0))]
        return pltpu.make_async_remote_copy(
            src_ref=src_ref, dst_ref=dst_ref, send_sem=send_sems.at[7 * wi + k - 1],
            recv_sem=recv_sems.at[7 * wi + k - 1], device_id=to, device_id_type=MESH)

    def wait(self, gi, after):
        n = self.sizes[gi]
        copy = self._copy

        def body(*refs):
            src, land = refs[:n], refs[n:2 * n]
            send_sems, recv_sems = refs[2 * n], refs[2 * n + 1]
            for wi in range(n):
                for k in range(1, N_DEV):
                    cp = copy(src[wi], land[wi], send_sems, recv_sems, wi, k)
                    cp.wait_send()
                    cp.wait_recv()

        arrays = list(self.srcs[gi]) + list(self.lands[gi])
        outs = pl.pallas_call(
            body, name=f"{self.name}_wait{gi}",
            in_specs=[HBM_SPEC] * (2 * n) + [SEM_SPEC, SEM_SPEC] + [ANY_SPEC] * len(after),
            out_specs=[HBM_SPEC] * (2 * n),
            out_shape=[pltpu.HBM(a.shape, a.dtype) for a in arrays],
            input_output_aliases={i: i for i in range(2 * n)},
            compiler_params=pltpu.CompilerParams(has_side_effects=EFFECT),
        )(*arrays, *self.sems[gi], *after)
        return outs[:n], outs[n:]


def _allreduce_packed(g):
    rows = g.shape[0]

    def body(g_ref, out_ref, buf, send_sems, recv_sems):
        me = _linear(_peer(0))
        buf[me] = g_ref[...]
        copies = []
        for k in range(1, N_DEV):
            copies.append(pltpu.make_async_remote_copy(
                src_ref=g_ref, dst_ref=buf.at[me], send_sem=send_sems.at[k - 1], recv_sem=recv_sems.at[k - 1],
                device_id=_peer(k), device_id_type=MESH))
        for cp in copies:
            cp.start()
        for k in range(1, N_DEV):
            pltpu.make_async_remote_copy(
                src_ref=g_ref, dst_ref=buf.at[_linear(_peer(k))], send_sem=send_sems.at[k - 1],
                recv_sem=recv_sems.at[k - 1], device_id=_peer(k), device_id_type=MESH).wait_recv()
        for cp in copies:
            cp.wait_send()
        total = buf[0]
        for d in range(1, N_DEV):
            total = total + buf[d]
        out_ref[...] = total

    return pl.pallas_call(
        body, name="allreduce_small",
        in_specs=[pl.BlockSpec(memory_space=pltpu.VMEM)], out_specs=pl.BlockSpec(memory_space=pltpu.VMEM),
        out_shape=jax.ShapeDtypeStruct(g.shape, F32),
        scratch_shapes=[pltpu.VMEM((N_DEV, rows, g.shape[1]), F32), pltpu.SemaphoreType.DMA((7,)),
                        pltpu.SemaphoreType.DMA((7,))],
        compiler_params=pltpu.CompilerParams(vmem_limit_bytes=VMEM_LIMIT),
    )(g)


def _rotary_tables(positions):
    rot_dim = HEAD_DIM // 4
    inv_freq = ROPE_THETA ** (-jnp.arange(0, rot_dim, 2, dtype=F32) / rot_dim)
    ang = positions.astype(F32)[:, None] * inv_freq
    cs = jnp.concatenate([jnp.cos(ang), jnp.sin(ang)], axis=1)
    dim = jnp.arange(LANES) % HEAD_DIM
    first, second = dim < ROT_SHIFT, (dim >= ROT_SHIFT) & (dim < rot_dim)
    src = jnp.arange(2 * ROT_SHIFT)[:, None]
    angle = (dim % ROT_SHIFT)[None, :]
    c = jnp.where((first | second)[None, :] & (src == angle), 1.0, 0.0)
    sa = jnp.where(second[None, :] & (src == angle + ROT_SHIFT), 1.0, 0.0)
    sb = jnp.where(first[None, :] & (src == angle + ROT_SHIFT), -1.0, 0.0)
    spread = jnp.concatenate([c, sa, sb], axis=1).astype(F32)
    base = jnp.concatenate([jnp.where(first | second, 0.0, 1.0), jnp.zeros((2 * LANES,))]).astype(F32)[None, :]
    return [cs, spread, base]


def _block_diag(pool_w):
    gc = pool_w.shape[-1]
    out = jnp.zeros((POOL_WIDTH, POOL_WIDTH), pool_w.dtype)
    for grp in range(pool_w.shape[0]):
        out = lax.dynamic_update_slice(out, pool_w[grp], (grp * gc, grp * gc))
    return out


def _diag_blocks(a):
    gc = POOL_WIDTH // len(POOL_WINDOWS)
    return jnp.stack([a[grp * gc:(grp + 1) * gc, grp * gc:(grp + 1) * gc] for grp in range(len(POOL_WINDOWS))])


def _local_step(x, p, positions, loss_target, norm1, pool_w, pool_scale, norm2, norm3, final_norm, weights, send):
    rc, rsa, rsb = _rotary_tables(positions)
    ones_bd = _block_diag(jnp.ones((4, HEAD_DIM, HEAD_DIM), BF16))
    saved = []
    h = x
    for i in range(2):
        tag = f"_l{i}"
        g1, g2, g3 = norm1[i:i + 1], norm2[i:i + 1], norm3[i:i + 1]
        w_bd = _block_diag(pool_w[i]).astype(BF16)
        scale = pool_scale[i:i + 1]
        w_in = weights(i, "in", (h, rc, rsa, rsb, w_bd))
        hn1, u, *qkv = _normproj_fwd(h, g1, w_in, rc, rsa, rsb, "normproj_fwd" + tag)
        qkv = [qkv[3 * grp:3 * grp + 3] for grp in range(3)]
        started = weights(i, "prefetch", (hn1,))
        pool_out, y = _pool_fwd(u, w_bd, scale, "pool_fwd" + tag, after=started)
        o, lse = zip(*[_attn_fwd(*qkv[grp], f"attn_fwd{tag}_g{grp}", after=started) for grp in range(3)])
        w_out = weights(i, "out", (pool_out, *o))
        h1, a = _outproj_fwd(h, pool_out, o, lse, w_out, "outproj_fwd" + tag)
        w_up, w_down, w_gate, w_ple = weights(i, "rest", (h1,))
        h2, hn2, r = _mlp_fwd(h1, g2, w_up, w_down, "mlp_fwd" + tag)
        h3, hn3, gate, pb = _gate_fwd(h2, g3, w_gate, p, i, w_ple, "gate_fwd" + tag)
        saved.append(dict(h0=h, hn1=hn1, qkv=qkv, y=y, o=o, lse=lse, a=a, h1=h1, hn2=hn2, r=r, h2=h2,
                          hn3=hn3, gate=gate, pb=pb, w_bd=w_bd, scale=scale, g1=g1, g2=g2, g3=g3,
                          w_in=w_in, w_out=w_out, w_up=w_up, w_down=w_down, w_gate=w_gate, w_ple=w_ple))
        h = h3
    loss, dh, d_final = _loss_head(h, final_norm.reshape(1, D_MODEL), loss_target, "loss_head")

    grads = [None, None]
    sent = ()
    for i in (1, 0):
        tag = f"_l{i}"
        sv = saved[i]
        dh2, dgl, de, dg3 = _gate_bwd(dh, sv["gate"], sv["pb"], sv["w_ple"], sv["h2"], sv["g3"], sv["w_gate"],
                                      "gate_bwd" + tag, after=sent)
        dw_gate = _matmul_tn(sv["hn3"], dgl, "dw_gate" + tag)
        dw_ple = _matmul_tn(sv["pb"], de, "dw_ple" + tag)
        dh1, dup, dg2, dh2b = _mlp_bwd(dh2, sv["r"], sv["h1"], sv["g2"], sv["w_up"], sv["w_down"], "mlp_bwd" + tag)
        dw_down = _matmul_tn(sv["r"], dh2b, "dw_down" + tag, square_a=True)
        dw_up = _matmul_tn(sv["hn2"], dup, "dw_up" + tag, blocked_out=True)
        dpool, do0, do1, do2, de0, de1, de2, dh1b = _outproj_bwd(dh1, sv["w_out"], sv["o"], sv["lse"], ones_bd,
                                                                 "outproj_bwd" + tag)
        dw_out = _matmul_tn(sv["a"], dh1b, "dw_out" + tag)
        sent = send(i, "main", dict(w_gate=dw_gate, w_ple=dw_ple, w_down=dw_down, w_up=dw_up, w_out=dw_out))
        dqkv = [_attn_bwd(*sv["qkv"][grp], do_g, sv["lse"][grp], de_g, f"attn_bwd{tag}_g{grp}", after=sent)
                for grp, (do_g, de_g) in enumerate(((do0, de0), (do1, de1), (do2, de2)))]
        dq, dk, dv = zip(*dqkv)
        du, dw_bd, dscale = _pool_bwd(dpool, sv["y"], sv["w_bd"], sv["scale"], "pool_bwd" + tag, after=sent)
        dh, dz, dg1 = _normproj_bwd(dh1, du, dq, dk, dv, rc, rsa, rsb, sv["w_in"], sv["h0"], sv["g1"],
                                    "normproj_bwd" + tag)
        dw_in = _matmul_tn(sv["hn1"], dz, "dw_in" + tag, tn=N_IN // 2)
        sent = send(i, "in", dict(w_in=dw_in))
        grads[i] = dict(norm1=dg1, norm2=dg2, norm3=dg3, pool_w=_diag_blocks(dw_bd), pool_scale=dscale)
    return loss, dh, grads, d_final, sent


def _pack_small(norm1, norm2, norm3, final_norm, pool_scale, pool_w, spare=None):
    spare = jnp.zeros((1, LANES), F32) if spare is None else spare
    scale_row = jnp.concatenate([pool_scale.reshape(1, 2 * POOL_WIDTH), spare,
                                 jnp.zeros((1, D_MODEL - 2 * POOL_WIDTH - LANES), F32)], axis=1)
    return jnp.concatenate([norm1, norm2, norm3, final_norm.reshape(1, D_MODEL), scale_row,
                            pool_w.reshape(32, D_MODEL)], axis=0)


def _unpack_small(a):
    return dict(norm1=a[0:2], norm2=a[2:4], norm3=a[4:6], final_norm=a[6], pool_scale=a[7, 0:2 * POOL_WIDTH].reshape(2, POOL_WIDTH),
                pool_w=a[8:40].reshape(2, 4, HEAD_DIM, HEAD_DIM))


def _chunks_cols(a, cols):
    return a.reshape(a.shape[0], N_DEV, cols).transpose(1, 0, 2)


def _chunks_rows(a, rows):
    return a.reshape(N_DEV, rows, a.shape[1])


BIG = ("w_in", "w_out", "w_up", "w_down", "w_gate", "w_ple")
SMALL = ("norm1", "norm2", "norm3", "final_norm", "pool_scale", "pool_w")
ORDER = ("norm1", "w_in", "pool_w", "pool_scale", "w_out", "norm2", "w_up", "w_down", "norm3", "w_gate", "w_ple",
         "final_norm")


def kernel(x, p, positions, norm1, w_in, pool_w, pool_scale, w_out, norm2, w_up, w_down, norm3, w_gate, w_ple, final_norm, loss_target, m_norm1, m_w_in, m_pool_w, m_pool_scale, m_w_out, m_norm2, m_w_up, m_w_down, m_norm3, m_w_gate, m_w_ple, m_final_norm, v_norm1, v_w_in, v_pool_w, v_pool_scale, v_w_out, v_norm2, v_w_up, v_w_down, v_norm3, v_w_gate, v_w_ple, v_final_norm):
    w = dict(norm1=norm1, w_in=w_in, pool_w=pool_w, pool_scale=pool_scale, w_out=w_out, norm2=norm2, w_up=w_up,
             w_down=w_down, norm3=norm3, w_gate=w_gate, w_ple=w_ple, final_norm=final_norm)
    m = dict(norm1=m_norm1, w_in=m_w_in, pool_w=m_pool_w, pool_scale=m_pool_scale, w_out=m_w_out, norm2=m_norm2,
             w_up=m_w_up, w_down=m_w_down, norm3=m_norm3, w_gate=m_w_gate, w_ple=m_w_ple, final_norm=m_final_norm)
    v = dict(norm1=v_norm1, w_in=v_w_in, pool_w=v_pool_w, pool_scale=v_pool_scale, w_out=v_w_out, norm2=v_norm2,
             w_up=v_w_up, w_down=v_w_down, norm3=v_norm3, w_gate=v_w_gate, w_ple=v_w_ple, final_norm=v_final_norm)
    seq = x.shape[1]

    bf = {n: [w[n][layer].astype(BF16) for layer in range(2)] for n in BIG}
    rest = ("w_up", "w_down", "w_gate", "w_ple")
    me = 4 * lax.axis_index("x") + 2 * lax.axis_index("y") + lax.axis_index("c")
    gathers = [_Exchange("gather_l0", [[bf["w_in"][0]], [bf["w_out"][0]], [bf[n][0] for n in rest]], scatter=False)]
    unpack = dict(w_in=lambda a: a.transpose(1, 0, 2).reshape(D_MODEL, N_IN),
                  w_out=lambda a: a.reshape(D_MODEL, D_MODEL), w_gate=lambda a: a.reshape(D_MODEL, D_MODEL),
                  w_ple=lambda a: a.transpose(1, 0, 2).reshape(PLE_DIM, D_MODEL), w_up=lambda a: a, w_down=lambda a: a)
    parts = dict(zip(("in", "out", "rest"), (("w_in",), ("w_out",), rest)))

    def weights(layer, part, after):
        if part == "prefetch":
            if layer != 0:
                return ()
            gathers.append(_Exchange("gather_l1", [[bf[n][1] for n in parts[pt]] for pt in parts], scatter=False,
                                     after=after))
            return (gathers[1].token,)
        shards, lands = gathers[layer].wait(tuple(parts).index(part), after)
        full = [unpack[n](lax.dynamic_update_slice_in_dim(land, shard[None], me, axis=0))
                for n, shard, land in zip(parts[part], shards, lands)]
        return full if part == "rest" else full[0]

    to_chunks = dict(w_in=lambda a: _chunks_cols(a, N_IN // N_DEV), w_out=lambda a: _chunks_rows(a, D_MODEL // N_DEV),
                     w_up=lambda a: a, w_down=lambda a: _chunks_rows(a, FF_BLOCK),
                     w_gate=lambda a: _chunks_rows(a, D_MODEL // N_DEV), w_ple=lambda a: _chunks_cols(a, D_MODEL // N_DEV))
    own = {n: [None, None] for n in BIG}
    scatters = {}

    def own_chunk(n, g32):
        if n in ("w_in", "w_ple"):
            cols = g32.shape[1] // N_DEV
            return lax.dynamic_slice(g32, (0, me * cols), (g32.shape[0], cols))
        return lax.dynamic_index_in_dim(to_chunks[n](g32), me, axis=0, keepdims=False)

    def send(layer, part, grads):
        for n, (g32, _) in grads.items():
            own[n][layer] = own_chunk(n, g32)
        ex = _Exchange(f"scatter_{part}_l{layer}", [[to_chunks[n](g16) for n, (_, g16) in grads.items()]], scatter=True)
        scatters[layer, part] = (tuple(grads), ex)
        return (ex.token,)

    loss, dx, grads, d_final, sent = _local_step(
        x.reshape(seq, D_MODEL), p.reshape(2, seq, PLE_DIM), positions.reshape(seq), loss_target.reshape(seq, D_MODEL),
        norm1, pool_w, pool_scale, norm2, norm3, final_norm, weights, send)

    small_g = _pack_small(
        *[jnp.concatenate([grads[0][n], grads[1][n]], axis=0) for n in ("norm1", "norm2", "norm3")], d_final.reshape(D_MODEL),
        jnp.concatenate([grads[0]["pool_scale"], grads[1]["pool_scale"]], axis=0),
        jnp.stack([grads[0]["pool_w"], grads[1]["pool_w"]]), spare=loss)
    small_g = _allreduce_packed(small_g)

    g_out, d_out, m_out, v_out = {}, {}, {}, {}
    for part in ("main", "in"):
        recv = {}
        for layer in (1, 0):
            names, ex = scatters[layer, part]
            for n, r in zip(names, ex.wait(0, sent)[1]):
                recv[n, layer] = r
        for n in names:
            g_out[n], d_out[n], m_out[n], v_out[n] = _adamw_sharded(
                w[n], m[n], v[n], jnp.stack(own[n]), recv[n, 0], recv[n, 1], "adamw_" + n)
        sent = tuple(d_out[n] for n in names)
    pack = lambda t: _pack_small(*[t[n] for n in SMALL])
    d_small, m_small, v_small = _adamw_packed(pack(w), small_g, pack(m), pack(v), "adamw_small")
    for dst, a in ((g_out, small_g), (d_out, d_small), (m_out, m_small), (v_out, v_small)):
        dst.update(_unpack_small(a))

    return (small_g[7, 2 * POOL_WIDTH],dx.reshape(1, seq, D_MODEL), *[g_out[n] for n in ORDER], *[d_out[n] for n in ORDER],
            *[m_out[n] for n in ORDER], *[v_out[n] for n in ORDER])
```

```python
import functools

import jax
import jax.numpy as jnp
from jax import lax
from jax.experimental import pallas as pl
from jax.experimental.pallas import tpu as pltpu

F32 = jnp.float32
BF16 = jnp.bfloat16

D_MODEL = 1024
HEAD_DIM = 64
POOL_WIDTH = 256
POOL_WINDOWS = (2, 4, 8, 16)
POOL_HALO = 16
GROUP_WIDTH = 256
DILATIONS = (1, 4, 16)
ATTN_BLOCK = 128
ROT_SHIFT = 8
ROPE_THETA = 500000.0
D_FF = 4096
FF_BLOCK = 512
FF_PER_STEP = 2
FF_PER_STEP_BWD = 4
N_DEV = 8
N_IN = POOL_WIDTH + 3 * 768
PLE_DIM = 256
EPS = 1e-6
NEG_BIG = -1e30

ADAM_LR = 0.001
ADAM_B1 = 0.9
ADAM_B2 = 0.999
ADAM_EPS = 1e-08
ADAM_WD = 0.01
ADAM_STEP = 10

LANES = 128
VMEM_LIMIT = 56 * 1024 * 1024
MESH = pl.DeviceIdType.MESH


def _params(n_grid):
    return pltpu.CompilerParams(dimension_semantics=("arbitrary",) * n_grid, vmem_limit_bytes=VMEM_LIMIT)


def _dot(a, b):
    return jnp.dot(a, b, preferred_element_type=F32)


def _dot_nt(a, b):
    return lax.dot_general(a, b, (((1,), (1,)), ((), ())), preferred_element_type=F32)


def _dot_tn(a, b):
    return lax.dot_general(a, b, (((0,), (0,)), ((), ())), preferred_element_type=F32)


def _rms(x, g):
    rstd = lax.rsqrt(jnp.mean(x * x, axis=-1, keepdims=True) + EPS)
    n = x * rstd
    return n, rstd, n * g


def _rms_bwd(dy, n, rstd, g):
    dyn = dy * g
    dx = rstd * (dyn - n * jnp.mean(dyn * n, axis=-1, keepdims=True))
    return dx, jnp.sum(dy * n, axis=0, keepdims=True)


def _ordered_after(body, n_in, after):
    if not after:
        return body
    return lambda *refs: body(*refs[:n_in], *refs[n_in + len(after):])


def _row_tile(s, t):
    t = min(s, t)
    assert s % t == 0
    return t


def _rot(z, c, sa, sb):
    return z * c + pltpu.roll(z, ROT_SHIFT, 1) * sa + pltpu.roll(z, LANES - ROT_SHIFT, 1) * sb


def _table_specs(t):
    return [pl.BlockSpec((t, LANES), functools.partial(lambda i, k: (i, k), k=k)) for k in range(3)]


def _rot_t(dz, c, sa, sb):
    return dz * c + pltpu.roll(dz * sa, LANES - ROT_SHIFT, 1) + pltpu.roll(dz * sb, ROT_SHIFT, 1)


def _to_residues(value, stage, out_ref, dil):
    if dil == 1:
        out_ref[0] = value.astype(out_ref.dtype)
        return
    rows = value.shape[0] // dil
    for hf in range(GROUP_WIDTH // LANES):
        lanes = slice(hf * LANES, (hf + 1) * LANES)
        stage[hf][...] = value[:, lanes]
        for r in range(dil):
            out_ref[r, :, lanes] = stage[hf][pl.ds(r, rows, stride=dil), :].astype(out_ref.dtype)


def _from_residues(in_ref, stage, dil):
    if dil == 1:
        return in_ref[0].astype(F32)
    rows = in_ref.shape[1]
    for hf in range(GROUP_WIDTH // LANES):
        for r in range(dil):
            stage[hf][pl.ds(r, rows, stride=dil), :] = in_ref[r, :, hf * LANES:(hf + 1) * LANES].astype(F32)
    return jnp.concatenate([stage[0][...], stage[1][...]], axis=1)


def _residue_spec(dil, t):
    return pl.BlockSpec((dil, t // dil, GROUP_WIDTH), lambda i: (0, i, 0))


def _residue_shape(dil, s, dtype):
    return jax.ShapeDtypeStruct((dil, s // dil, GROUP_WIDTH), dtype)


def _stages(t, n):
    return [pltpu.VMEM((t, LANES), F32)] * (n * (GROUP_WIDTH // LANES))


def _pair_stages(refs):
    return [refs[i:i + 2] for i in range(0, len(refs), 2)]


def _normproj_fwd(h, g, w_in, rc, rsa, rsb, name):
    s = h.shape[0]
    t = _row_tile(s, 512)

    def body(h_ref, g_ref, w_ref, c_ref, sa_ref, sb_ref, hn_ref, u_ref, *rest):
        qkv_refs, stages = rest[:9], _pair_stages(rest[9:])
        _, _, hn = _rms(h_ref[...], g_ref[...])
        hb = hn.astype(BF16)
        hn_ref[...] = hb
        c, sa, sb = c_ref[...], sa_ref[...], sb_ref[...]

        def rot(z, scale):
            halves = [_rot(z[:, hf * LANES:(hf + 1) * LANES], c, sa, sb) * scale for hf in range(2)]
            return jnp.concatenate(halves, axis=1)

        u_ref[...] = _dot(hb, w_ref[:, 0:POOL_WIDTH])
        for grp, dil in enumerate(DILATIONS):
            lo = POOL_WIDTH + grp * GROUP_WIDTH
            q_ref, k_ref, v_ref = qkv_refs[3 * grp:3 * grp + 3]
            _to_residues(rot(_dot(hb, w_ref[:, lo:lo + GROUP_WIDTH]), HEAD_DIM ** -0.5), stages[0], q_ref, dil)
            _to_residues(rot(_dot(hb, w_ref[:, lo + 768:lo + 768 + GROUP_WIDTH]), 1.0), stages[1], k_ref, dil)
            _to_residues(_dot(hb, w_ref[:, lo + 1536:lo + 1536 + GROUP_WIDTH]), stages[2], v_ref, dil)

    row = lambda w: pl.BlockSpec((t, w), lambda i: (i, 0))
    return pl.pallas_call(
        body, name=name, grid=(s // t,),
        in_specs=[row(D_MODEL), pl.BlockSpec((1, D_MODEL), lambda i: (0, 0)),
                  pl.BlockSpec((D_MODEL, N_IN), lambda i: (0, 0))] + _table_specs(t),
        out_specs=[row(D_MODEL), row(POOL_WIDTH)] + [_residue_spec(dil, t) for dil in DILATIONS for _ in range(3)],
        out_shape=[jax.ShapeDtypeStruct((s, D_MODEL), BF16), jax.ShapeDtypeStruct((s, POOL_WIDTH), F32)]
        + [_residue_shape(dil, s, BF16) for dil in DILATIONS for _ in range(3)],
        scratch_shapes=_stages(t, 3),
        compiler_params=_params(1),
    )(h, g, w_in, rc, rsa, rsb)


def _pool_lane_window():
    lane = lax.broadcasted_iota(jnp.int32, (1, POOL_WIDTH), 1)
    return jnp.left_shift(2, lane // (POOL_WIDTH // len(POOL_WINDOWS)))


def _pool_fwd(u, w_bd, scale, name, after=()):
    s = u.shape[0]
    t = _row_tile(s, 512)

    def body(u_ref, w_ref, sc_ref, out_ref, y_ref, ext):
        i = pl.program_id(0)

        @pl.when(i == 0)
        def _():
            ext[0:POOL_HALO, :] = jnp.zeros((POOL_HALO, POOL_WIDTH), F32)

        x = u_ref[...]
        ext[POOL_HALO:, :] = x
        win = _pool_lane_window()
        acc = x
        wsum = jnp.zeros_like(x)
        for k in range(1, POOL_HALO):
            acc = acc + ext[POOL_HALO - k:POOL_HALO - k + t, :]
            if k + 1 in POOL_WINDOWS:
                wsum = jnp.where(win == k + 1, acc, wsum)
        pos = i * t + lax.broadcasted_iota(jnp.int32, (t, POOL_WIDTH), 0)
        cnt = jnp.minimum(pos + 1, win).astype(F32)
        y = wsum / cnt - x
        yb = y.astype(BF16)
        y_ref[...] = yb
        out_ref[...] = _dot(yb, w_ref[...]) * sc_ref[...]
        ext[0:POOL_HALO, :] = x[t - POOL_HALO:, :]

    row = pl.BlockSpec((t, POOL_WIDTH), lambda i: (i, 0))
    return pl.pallas_call(
        _ordered_after(body, 3, after), name=name, grid=(s // t,),
        in_specs=[row, pl.BlockSpec((POOL_WIDTH, POOL_WIDTH), lambda i: (0, 0)),
                  pl.BlockSpec((1, POOL_WIDTH), lambda i: (0, 0))] + [pl.BlockSpec(memory_space=pl.ANY)] * len(after),
        out_specs=[row, row],
        out_shape=[jax.ShapeDtypeStruct((s, POOL_WIDTH), F32), jax.ShapeDtypeStruct((s, POOL_WIDTH), BF16)],
        scratch_shapes=[pltpu.VMEM((t + POOL_HALO, POOL_WIDTH), F32)],
        compiler_params=_params(1),
    )(u, w_bd, scale, *after)


def _head_masks():
    lane = lax.broadcasted_iota(jnp.int32, (ATTN_BLOCK, GROUP_WIDTH), 1)
    return [lane // HEAD_DIM == hd for hd in range(GROUP_WIDTH // HEAD_DIM)]


def _stack_heads(a, masks):
    zero = jnp.zeros_like(a)
    return jnp.concatenate([jnp.where(m, a, zero) for m in masks], axis=0)


def _band_bias(first_step):
    rows = ATTN_BLOCK * (GROUP_WIDTH // HEAD_DIM)
    i = lax.broadcasted_iota(jnp.int32, (rows, 2 * ATTN_BLOCK), 0) & (ATTN_BLOCK - 1)
    j = lax.broadcasted_iota(jnp.int32, (rows, 2 * ATTN_BLOCK), 1)
    inner = jnp.where((j >= i) & (j <= i + ATTN_BLOCK), 0.0, NEG_BIG)
    return jnp.where((j < ATTN_BLOCK) & first_step, NEG_BIG, inner), inner


def _column_per_head(a):
    return jnp.concatenate([a[:, hd * HEAD_DIM:hd * HEAD_DIM + 1] for hd in range(GROUP_WIDTH // HEAD_DIM)], axis=0)


def _blocks_per_step(nb):
    return 8 if nb % 8 == 0 else 4 if nb % 4 == 0 else 2 if nb % 2 == 0 else 1


def _residues_per_step(dil, nb, qb):
    return 2 if (nb == qb and qb < 8 and dil % 2 == 0) else 1


def _attn_fwd(q, k, v, name, after=()):
    dil, length, _ = q.shape
    nb = length // ATTN_BLOCK
    qb = _blocks_per_step(nb)
    rs = _residues_per_step(dil, nb, qb)

    def body(q_ref, kp_ref, kc_ref, vp_ref, vc_ref, o_ref, lse_ref):
        masks = _head_masks()
        bias = _band_bias(pl.program_id(1) == 0)
        for rr in range(rs):
            for qi in range(qb):
                here = slice(qi * ATTN_BLOCK, (qi + 1) * ATTN_BLOCK)
                before = slice((qi - 1) * ATTN_BLOCK, qi * ATTN_BLOCK)
                kcat = jnp.concatenate([kp_ref[rr] if qi == 0 else kc_ref[rr, before], kc_ref[rr, here]], axis=0)
                vcat = jnp.concatenate([vp_ref[rr] if qi == 0 else vc_ref[rr, before], vc_ref[rr, here]], axis=0)
                qs = _stack_heads(q_ref[rr, here], masks)
                sc = _dot_nt(qs, kcat) + bias[min(qi, 1)]
                m = jnp.max(sc, axis=1, keepdims=True)
                e = jnp.exp(sc - m)
                l = jnp.sum(e, axis=1, keepdims=True)
                p = (e / l).astype(BF16)
                lse = m + jnp.log(l)
                o = jnp.zeros((ATTN_BLOCK, GROUP_WIDTH), F32)
                lse_full = jnp.zeros((ATTN_BLOCK, GROUP_WIDTH), F32)
                for hd, msk in enumerate(masks):
                    rows = slice(hd * ATTN_BLOCK, (hd + 1) * ATTN_BLOCK)
                    o = jnp.where(msk, _dot(p[rows], vcat), o)
                    lse_full = jnp.where(msk, lse[rows], lse_full)
                o_ref[rr, here] = o.astype(o_ref.dtype)
                lse_ref[rr, here] = lse_full

    cur = pl.BlockSpec((rs, qb * ATTN_BLOCK, GROUP_WIDTH), lambda r, j: (r, j, 0))
    prev = pl.BlockSpec((rs, ATTN_BLOCK, GROUP_WIDTH), lambda r, j: (r, jnp.maximum(qb * j - 1, 0), 0))
    return pl.pallas_call(
        _ordered_after(body, 5, after), name=name, grid=(dil // rs, nb // qb),
        in_specs=[cur, prev, cur, prev, cur] + [pl.BlockSpec(memory_space=pl.ANY)] * len(after), out_specs=[cur, cur],
        out_shape=[jax.ShapeDtypeStruct(q.shape, BF16), jax.ShapeDtypeStruct(q.shape, F32)],
        compiler_params=_params(2),
    )(q, k, k, v, v, *after)


def _group_weights(l0, l1, l2):
    m = jnp.maximum(jnp.maximum(l0, l1), l2)
    e0, e1, e2 = jnp.exp(l0 - m), jnp.exp(l1 - m), jnp.exp(l2 - m)
    den = e0 + e1 + e2
    return e0 / den, e1 / den, e2 / den


def _outproj_fwd(h, pool_out, o, lse, w_out, name):
    s = h.shape[0]
    t = _row_tile(s, 512)

    def body(h_ref, po_ref, o0, o1, o2, l0, l1, l2, w_ref, out_ref, a_ref, *stages):
        stages = _pair_stages(stages)
        ov =[_from_residues(r, stages[i], DILATIONS[i]) for i, r in enumerate((o0, o1, o2))]
        lv = [_from_residues(r, stages[3 + i], DILATIONS[i]) for i, r in enumerate((l0, l1, l2))]
        wts = _group_weights(*lv)
        a = jnp.concatenate([po_ref[...]] + [ov[i] * wts[i] for i in range(3)], axis=1).astype(BF16)
        a_ref[...] = a
        out_ref[...] = h_ref[...] + _dot(a, w_ref[...])

    row = lambda w: pl.BlockSpec((t, w), lambda i: (i, 0))
    res = [_residue_spec(dil, t) for dil in DILATIONS]
    return pl.pallas_call(
        body, name=name, grid=(s // t,),
        in_specs=[row(D_MODEL), row(POOL_WIDTH)] + res + res + [pl.BlockSpec((D_MODEL, D_MODEL), lambda i: (0, 0))],
        out_specs=[row(D_MODEL), row(D_MODEL)],
        out_shape=[jax.ShapeDtypeStruct((s, D_MODEL), F32), jax.ShapeDtypeStruct((s, D_MODEL), BF16)],
        scratch_shapes=_stages(t, 6),
        compiler_params=_params(1),
    )(h, pool_out, *o, *lse, w_out)


def _mlp_fwd(h, g, w_up, w_down, name):
    s = h.shape[0]
    t = _row_tile(s, 1024)
    nblk = D_FF // (FF_PER_STEP * FF_BLOCK)

    def body(h_ref, g_ref, wu_ref, wd_ref, out_ref, hn_ref, r_ref, hb_s, acc):
        j = pl.program_id(1)

        @pl.when(j == 0)
        def _():
            _, _, hn = _rms(h_ref[...], g_ref[...])
            hb = hn.astype(BF16)
            hb_s[...] = hb
            hn_ref[...] = hb
            acc[...] = jnp.zeros_like(acc)

        hb = hb_s[...]
        acts = []
        for b in range(FF_PER_STEP):
            r = jnp.maximum(_dot(hb, wu_ref[b]), 0.0)
            r_ref[:, b * FF_BLOCK:(b + 1) * FF_BLOCK] = r.astype(BF16)
            acts.append((r * r).astype(BF16))
        acc[...] += _dot(jnp.concatenate(acts, axis=1), wd_ref[...].reshape(FF_PER_STEP * FF_BLOCK, D_MODEL))

        @pl.when(j == nblk - 1)
        def _():
            out_ref[...] = h_ref[...] + acc[...]

    row = pl.BlockSpec((t, D_MODEL), lambda i, j: (i, 0))
    return pl.pallas_call(
        body, name=name, grid=(s // t, nblk),
        in_specs=[row, pl.BlockSpec((1, D_MODEL), lambda i, j: (0, 0)),
                  pl.BlockSpec((FF_PER_STEP, D_MODEL, FF_BLOCK), lambda i, j: (j, 0, 0)),
                  pl.BlockSpec((FF_PER_STEP, FF_BLOCK, D_MODEL), lambda i, j: (j, 0, 0))],
        out_specs=[row, row, pl.BlockSpec((t, FF_PER_STEP * FF_BLOCK), lambda i, j: (i, j))],
        out_shape=[jax.ShapeDtypeStruct((s, D_MODEL), F32), jax.ShapeDtypeStruct((s, D_MODEL), BF16),
                   jax.ShapeDtypeStruct((s, D_FF), BF16)],
        scratch_shapes=[pltpu.VMEM((t, D_MODEL), BF16), pltpu.VMEM((t, D_MODEL), F32)],
        compiler_params=_params(2),
    )(h, g, w_up, w_down)


def _gate_fwd(h, g, w_gate, p, layer, w_ple, name):
    s = h.shape[0]
    t = _row_tile(s, 512)

    def body(h_ref, g_ref, wg_ref, p_ref, wp_ref, out_ref, hn_ref, gate_ref, pb_ref):
        x = h_ref[...]
        _, _, hn = _rms(x, g_ref[...])
        hb = hn.astype(BF16)
        hn_ref[...] = hb
        gate = 1.0 / (1.0 + jnp.exp(-_dot(hb, wg_ref[...])))
        pb = p_ref[...].astype(BF16)
        pb_ref[...] = pb
        gate_ref[...] = gate.astype(BF16)
        out_ref[...] = x + gate * _dot(pb, wp_ref[...])

    row = lambda w: pl.BlockSpec((t, w), lambda i: (i, 0))
    full = lambda a, b: pl.BlockSpec((a, b), lambda i: (0, 0))
    return pl.pallas_call(
        body, name=name, grid=(s // t,),
        in_specs=[row(D_MODEL), full(1, D_MODEL), full(D_MODEL, D_MODEL),
                  pl.BlockSpec((None, t, PLE_DIM), lambda i: (layer, i, 0)), full(PLE_DIM, D_MODEL)],
        out_specs=[row(D_MODEL), row(D_MODEL), row(D_MODEL), row(PLE_DIM)],
        out_shape=[jax.ShapeDtypeStruct((s, D_MODEL), F32), jax.ShapeDtypeStruct((s, D_MODEL), BF16),
                   jax.ShapeDtypeStruct((s, D_MODEL), BF16), jax.ShapeDtypeStruct((s, PLE_DIM), BF16)],
        compiler_params=_params(1),
    )(h, g, w_gate, p, w_ple)


def _loss_head(h, g, target, name):
    s = h.shape[0]
    t = _row_tile(s, 512)

    def body(h_ref, g_ref, t_ref, loss_ref, dh_ref, dg_ref):
        i = pl.program_id(0)

        @pl.when(i == 0)
        def _():
            loss_ref[...] = jnp.zeros_like(loss_ref)
            dg_ref[...] = jnp.zeros_like(dg_ref)

        gv = g_ref[...]
        n, rstd, y = _rms(h_ref[...], gv)
        err = y - t_ref[...]
        loss_ref[...] += jnp.sum(err * err) * (0.5 / D_MODEL)
        dx, dg = _rms_bwd(err * (1.0 / D_MODEL), n, rstd, gv)
        dh_ref[...] = dx
        dg_ref[...] += dg

    row = pl.BlockSpec((t, D_MODEL), lambda i: (i, 0))
    vec = pl.BlockSpec((1, D_MODEL), lambda i: (0, 0))
    return pl.pallas_call(
        body, name=name, grid=(s // t,),
        in_specs=[row, vec, row],
        out_specs=[pl.BlockSpec((1, LANES), lambda i: (0, 0)), row, vec],
        out_shape=[jax.ShapeDtypeStruct((1, LANES), F32), jax.ShapeDtypeStruct((s, D_MODEL), F32),
                   jax.ShapeDtypeStruct((1, D_MODEL), F32)],
        compiler_params=_params(1),
    )(h, g, target)


def _gate_bwd(dh, gate, pb, w_ple, h, g, w_gate, name, after=()):
    s = h.shape[0]
    t = _row_tile(s, 512)

    def body(dh_ref, gate_ref, pb_ref, wp_ref, h_ref, g_ref, wg_ref, out_ref, dgl_ref, de_ref, dg_ref):
        @pl.when(pl.program_id(0) == 0)
        def _():
            dg_ref[...] = jnp.zeros_like(dg_ref)

        d = dh_ref[...]
        gate = gate_ref[...].astype(F32)
        e = _dot(pb_ref[...], wp_ref[...])
        dgl = (d * e * gate * (1.0 - gate)).astype(BF16)
        dgl_ref[...] = dgl
        de_ref[...] = (d * gate).astype(BF16)
        gv = g_ref[...]
        n, rstd, _ = _rms(h_ref[...], gv)
        dx, dg = _rms_bwd(_dot_nt(dgl, wg_ref[...]), n, rstd, gv)
        out_ref[...] = d + dx
        dg_ref[...] += dg

    row = lambda w: pl.BlockSpec((t, w), lambda i: (i, 0))
    full = lambda a, b: pl.BlockSpec((a, b), lambda i: (0, 0))
    return pl.pallas_call(
        _ordered_after(body, 7, after), name=name, grid=(s // t,),
        in_specs=[row(D_MODEL), row(D_MODEL), row(PLE_DIM), full(PLE_DIM, D_MODEL), row(D_MODEL), full(1, D_MODEL),
                  full(D_MODEL, D_MODEL)] + [pl.BlockSpec(memory_space=pl.ANY)] * len(after),
        out_specs=[row(D_MODEL), row(D_MODEL), row(D_MODEL), full(1, D_MODEL)],
        out_shape=[jax.ShapeDtypeStruct((s, D_MODEL), F32), jax.ShapeDtypeStruct((s, D_MODEL), BF16),
                   jax.ShapeDtypeStruct((s, D_MODEL), BF16), jax.ShapeDtypeStruct((1, D_MODEL), F32)],
        compiler_params=_params(1),
    )(dh, gate, pb, w_ple, h, g, w_gate, *after)


def _mlp_bwd(dh, r, h, g, w_up, w_down, name):
    s = h.shape[0]
    t = _row_tile(s, 512)
    per = FF_PER_STEP_BWD
    nblk = D_FF // (per * FF_BLOCK)

    def body(dh_ref, r_ref, h_ref, g_ref, wu_ref, wd_ref, out_ref, dup_ref, dg_ref, db_s, acc):
        i, j = pl.program_id(0), pl.program_id(1)

        @pl.when((i == 0) & (j == 0))
        def _():
            dg_ref[...] = jnp.zeros_like(dg_ref)

        @pl.when(j == 0)
        def _():
            db_s[...] = dh_ref[...].astype(BF16)
            acc[...] = jnp.zeros_like(acc)

        db = db_s[...]
        back = None
        for b in range(per):
            cols = slice(b * FF_BLOCK, (b + 1) * FF_BLOCK)
            dup = (_dot_nt(db, wd_ref[b]) * (2.0 * r_ref[:, cols].astype(F32))).astype(BF16)
            dup_ref[:, cols] = dup
            part = _dot_nt(dup, wu_ref[b])
            back = part if back is None else back + part
        acc[...] += back

        @pl.when(j == nblk - 1)
        def _():
            gv = g_ref[...]
            n, rstd, _ = _rms(h_ref[...], gv)
            dx, dg = _rms_bwd(acc[...], n, rstd, gv)
            out_ref[...] = dh_ref[...] + dx
            dg_ref[...] += dg

    row = pl.BlockSpec((t, D_MODEL), lambda i, j: (i, 0))
    vec = pl.BlockSpec((1, D_MODEL), lambda i, j: (0, 0))
    blk = pl.BlockSpec((t, per * FF_BLOCK), lambda i, j: (i, j))
    return pl.pallas_call(
        body, name=name, grid=(s // t, nblk),
        in_specs=[row, blk, row, vec,
                  pl.BlockSpec((per, D_MODEL, FF_BLOCK), lambda i, j: (j, 0, 0)),
                  pl.BlockSpec((per, FF_BLOCK, D_MODEL), lambda i, j: (j, 0, 0))],
        out_specs=[row, blk, vec, row],
        out_shape=[jax.ShapeDtypeStruct((s, D_MODEL), F32), jax.ShapeDtypeStruct((s, D_FF), BF16),
                   jax.ShapeDtypeStruct((1, D_MODEL), F32), jax.ShapeDtypeStruct((s, D_MODEL), BF16)],
        scratch_shapes=[pltpu.VMEM((t, D_MODEL), F32)],
        compiler_params=_params(2),
    )(dh, r, h, g, w_up, w_down)


def _outproj_bwd(dh, w_out, o, lse, ones_bd, name):
    s = dh.shape[0]
    t = _row_tile(s, 512)

    def body(dh_ref, w_ref, o0, o1, o2, l0, l1, l2, bd_ref, dp_ref, do0, do1, do2, de0, de1, de2, dhb_ref, *stages):
        stages = _pair_stages(stages)
        dhb = dh_ref[...].astype(BF16)
        dhb_ref[...] = dhb
        da = _dot_nt(dhb, w_ref[...])
        dp_ref[...] = da[:, 0:POOL_WIDTH]
        ov =[_from_residues(r, stages[i], DILATIONS[i]) for i, r in enumerate((o0, o1, o2))]
        lv = [_from_residues(r, stages[3 + i], DILATIONS[i]) for i, r in enumerate((l0, l1, l2))]
        wts = _group_weights(*lv)
        bd = bd_ref[...]
        cbar = jnp.zeros((t, GROUP_WIDTH), F32)
        for grp, do_ref in enumerate((do0, do1, do2)):
            lo = POOL_WIDTH + grp * GROUP_WIDTH
            dag = da[:, lo:lo + GROUP_WIDTH]
            _to_residues(dag * wts[grp], stages[6 + grp], do_ref, DILATIONS[grp])
            prod = dag * ov[grp]
            hi = prod.astype(BF16)
            low = (prod - hi.astype(F32)).astype(BF16)
            cbar = cbar + wts[grp] * (_dot(hi, bd) + _dot(low, bd))
        for grp, de_ref in enumerate((de0, de1, de2)):
            _to_residues(wts[grp] * cbar, stages[9 + grp], de_ref, DILATIONS[grp])

    row = lambda w: pl.BlockSpec((t, w), lambda i: (i, 0))
    full = lambda a, b: pl.BlockSpec((a, b), lambda i: (0, 0))
    res = [_residue_spec(dil, t) for dil in DILATIONS]
    return pl.pallas_call(
        body, name=name, grid=(s // t,),
        in_specs=[row(D_MODEL), full(D_MODEL, D_MODEL)] + res + res + [full(GROUP_WIDTH, GROUP_WIDTH)],
        out_specs=[row(POOL_WIDTH)] + res + res + [row(D_MODEL)],
        out_shape=[jax.ShapeDtypeStruct((s, POOL_WIDTH), F32)] + [_residue_shape(dil, s, BF16) for dil in DILATIONS]
        + [_residue_shape(dil, s, F32) for dil in DILATIONS] + [jax.ShapeDtypeStruct((s, D_MODEL), BF16)],
        scratch_shapes=_stages(t, 12),
        compiler_params=_params(1),
    )(dh, w_out, *o, *lse, ones_bd)


def _attn_bwd(q, k, v, do, lse, deff, name, after=()):
    dil, length, _ = q.shape
    nb = length // ATTN_BLOCK
    qb = _blocks_per_step(nb)
    nj = nb // qb
    rs = _residues_per_step(dil, nb, qb)
    whole = nj == 1
    tail = slice((qb - 1) * ATTN_BLOCK, qb * ATTN_BLOCK)
    block = lambda qi: slice(qi * ATTN_BLOCK, (qi + 1) * ATTN_BLOCK)

    def body(q_ref, kp_ref, kc_ref, vp_ref, vc_ref, do_ref, lse_ref, de_ref, dq_ref, dk_ref, dv_ref, ck, cv):
        j = pl.program_id(1)

        def compute():
            masks = _head_masks()
            bias = _band_bias(j == 0)
            for rr in range(rs):
                dkc, dvc = [], []
                for qi in range(qb):
                    here, before = block(qi), block(qi - 1)
                    kcat = jnp.concatenate([kp_ref[rr] if qi == 0 else kc_ref[rr, before], kc_ref[rr, here]], axis=0)
                    vcat = jnp.concatenate([vp_ref[rr] if qi == 0 else vc_ref[rr, before], vc_ref[rr, here]], axis=0)
                    qs = _stack_heads(q_ref[rr, here], masks)
                    dos = _stack_heads(do_ref[rr, here], masks)
                    sc = _dot_nt(qs, kcat) + bias[min(qi, 1)]
                    p = jnp.exp(sc - _column_per_head(lse_ref[rr, here]))
                    ds = (p * (_dot_nt(dos, vcat) - _column_per_head(de_ref[rr, here]))).astype(BF16)
                    dq = jnp.zeros((ATTN_BLOCK, GROUP_WIDTH), F32)
                    for hd, msk in enumerate(masks):
                        dq = jnp.where(msk, _dot(ds[block(hd)], kcat), dq)
                    dq_ref[rr, here] = dq.astype(dq_ref.dtype)
                    dkc.append(_dot_tn(ds, qs))
                    dvc.append(_dot_tn(p.astype(BF16), dos))

                for out_ref, carry, parts in ((dk_ref, ck, dkc), (dv_ref, cv, dvc)):
                    full = [parts[qi][ATTN_BLOCK:] + parts[qi + 1][0:ATTN_BLOCK] for qi in range(qb - 1)]
                    if whole:
                        for qi, val in enumerate(full + [parts[qb - 1][ATTN_BLOCK:]]):
                            out_ref[rr, block(qi)] = val.astype(out_ref.dtype)
                        continue

                    @pl.when(j > 0)
                    def _():
                        if qb > 1:
                            out_ref[0, 0:(qb - 1) * ATTN_BLOCK] = carry[0:(qb - 1) * ATTN_BLOCK].astype(out_ref.dtype)
                        out_ref[0, tail] = (carry[tail] + parts[0][0:ATTN_BLOCK]).astype(out_ref.dtype)

                    for qi, val in enumerate(full):
                        carry[block(qi)] = val
                    carry[tail] = parts[qb - 1][ATTN_BLOCK:]

        if whole:
            compute()
        else:
            pl.when(j < nj)(compute)

            @pl.when(j == nj)
            def _():
                dk_ref[0] = ck[...].astype(dk_ref.dtype)
                dv_ref[0] = cv[...].astype(dv_ref.dtype)

    step = lambda j: jnp.minimum(j, nj - 1)
    cur = pl.BlockSpec((rs, qb * ATTN_BLOCK, GROUP_WIDTH), lambda r, j: (r, step(j), 0))
    prev = pl.BlockSpec((rs, ATTN_BLOCK, GROUP_WIDTH), lambda r, j: (r, jnp.maximum(qb * step(j) - 1, 0), 0))
    late = pl.BlockSpec((rs, qb * ATTN_BLOCK, GROUP_WIDTH), lambda r, j: (r, jnp.maximum(j - 1, 0), 0))
    return pl.pallas_call(
        _ordered_after(body, 8, after), name=name, grid=(dil // rs, 1 if whole else nj + 1),
        in_specs=[cur, prev, cur, prev, cur, cur, cur, cur] + [pl.BlockSpec(memory_space=pl.ANY)] * len(after),
        out_specs=[cur, cur if whole else late, cur if whole else late],
        out_shape=[jax.ShapeDtypeStruct(q.shape, BF16)] * 3,
        scratch_shapes=[pltpu.VMEM((qb * ATTN_BLOCK, GROUP_WIDTH), F32)] * 2,
        compiler_params=_params(2),
    )(q, k, k, v, v, do, lse, deff, *after)


def _pool_bwd(dpool, y, w_bd, scale, name, after=()):
    s = dpool.shape[0]
    t = _row_tile(s, 512)
    nt = s // t

    def body(dp_ref, y_ref, w_ref, sc_ref, du_ref, dw_ref, dsc_ref, ext):
        i = pl.program_id(0)

        @pl.when(i == 0)
        def _():
            ext[t:, :] = jnp.zeros((POOL_HALO, POOL_WIDTH), F32)
            dw_ref[...] = jnp.zeros_like(dw_ref)
            dsc_ref[...] = jnp.zeros_like(dsc_ref)

        dp = dp_ref[...]
        yb = y_ref[...]
        w = w_ref[...]
        dsc_ref[...] += jnp.sum(dp * _dot(yb, w), axis=0, keepdims=True)
        dyo = (dp * sc_ref[...]).astype(BF16)
        dw_ref[...] += _dot_tn(yb, dyo)
        dy = _dot_nt(dyo, w)
        win = _pool_lane_window()
        pos = (nt - 1 - i) * t + lax.broadcasted_iota(jnp.int32, (t, POOL_WIDTH), 0)
        gq = dy / jnp.minimum(pos + 1, win).astype(F32)
        ext[0:t, :] = gq
        acc = gq
        wsum = jnp.zeros_like(gq)
        for k in range(1, POOL_HALO):
            acc = acc + ext[k:k + t, :]
            if k + 1 in POOL_WINDOWS:
                wsum = jnp.where(win == k + 1, acc, wsum)
        du_ref[...] = wsum - dy
        ext[t:, :] = gq[0:POOL_HALO, :]

    rev = pl.BlockSpec((t, POOL_WIDTH), lambda i: (nt - 1 - i, 0))
    full = lambda a, b: pl.BlockSpec((a, b), lambda i: (0, 0))
    return pl.pallas_call(
        _ordered_after(body, 4, after), name=name, grid=(nt,),
        in_specs=[rev, rev, full(POOL_WIDTH, POOL_WIDTH), full(1, POOL_WIDTH)]
        + [pl.BlockSpec(memory_space=pl.ANY)] * len(after),
        out_specs=[rev, full(POOL_WIDTH, POOL_WIDTH), full(1, POOL_WIDTH)],
        out_shape=[jax.ShapeDtypeStruct((s, POOL_WIDTH), F32), jax.ShapeDtypeStruct((POOL_WIDTH, POOL_WIDTH), F32),
                   jax.ShapeDtypeStruct((1, POOL_WIDTH), F32)],
        scratch_shapes=[pltpu.VMEM((t + POOL_HALO, POOL_WIDTH), F32)],
        compiler_params=_params(1),
    )(dpool, y, w_bd, scale, *after)


def _normproj_bwd(dh, du, dq, dk, dv, rc, rsa, rsb, w_in, h, g, name):
    s = h.shape[0]
    t = _row_tile(s, 512)

    def body(dh_ref, du_ref, q0, q1, q2, k0, k1, k2, v0, v1, v2, c_ref, sa_ref, sb_ref, w_ref, h_ref, g_ref,
             out_ref, dz_ref, dg_ref, *stages):
        @pl.when(pl.program_id(0) == 0)
        def _():
            dg_ref[...] = jnp.zeros_like(dg_ref)

        c, sa, sb = c_ref[...], sa_ref[...], sb_ref[...]

        def unrot(a, scale):
            halves = [_rot_t(a[:, hf * LANES:(hf + 1) * LANES] * scale, c, sa, sb) for hf in range(2)]
            return jnp.concatenate(halves, axis=1)

        staged = _pair_stages(stages)
        tok = lambda refs, base: [_from_residues(r, staged[base + i], DILATIONS[i]) for i, r in enumerate(refs)]
        chunks = [du_ref[...]]
        chunks += [unrot(a, HEAD_DIM ** -0.5) for a in tok((q0, q1, q2), 0)]
        chunks += [unrot(a, 1.0) for a in tok((k0, k1, k2), 3)]
        chunks += tok((v0, v1, v2), 6)
        acc = jnp.zeros((t, D_MODEL), F32)
        for ci, ch in enumerate(chunks):
            cols = slice(ci * GROUP_WIDTH, (ci + 1) * GROUP_WIDTH)
            cb = ch.astype(BF16)
            dz_ref[:, cols] = cb
            acc = acc + _dot_nt(cb, w_ref[:, cols])
        gv = g_ref[...]
        n, rstd, _ = _rms(h_ref[...], gv)
        dx, dg = _rms_bwd(acc, n, rstd, gv)
        out_ref[...] = dh_ref[...] + dx
        dg_ref[...] += dg

    row = lambda w: pl.BlockSpec((t, w), lambda i: (i, 0))
    vec = pl.BlockSpec((1, D_MODEL), lambda i: (0, 0))
    res = [_residue_spec(dil, t) for dil in DILATIONS]
    return pl.pallas_call(
        body, name=name, grid=(s // t,),
        in_specs=[row(D_MODEL), row(POOL_WIDTH)] + res * 3 + _table_specs(t)
        + [pl.BlockSpec((D_MODEL, N_IN), lambda i: (0, 0)), row(D_MODEL), vec],
        out_specs=[row(D_MODEL), row(N_IN), vec],
        out_shape=[jax.ShapeDtypeStruct((s, D_MODEL), F32), jax.ShapeDtypeStruct((s, N_IN), BF16),
                   jax.ShapeDtypeStruct((1, D_MODEL), F32)],
        scratch_shapes=_stages(t, 9),
        compiler_params=_params(1),
    )(dh, du, *dq, *dk, *dv, rc, rsa, rsb, w_in, h, g)


def _matmul_tn(a, b, name, *, square_a=False, tn=None, blocked_out=False):
    s, m = a.shape
    n = b.shape[1]
    tk = _row_tile(s, 2048)
    tm = min(m, 1024)
    tn = tn or min(n, 1024)
    assert m % tm == 0 and n % tn == 0
    nk = s // tk
    nsub = tn // FF_BLOCK if blocked_out else 1

    def body(a_ref, b_ref, o_ref, ob_ref, acc):
        k = pl.program_id(2)

        @pl.when(k == 0)
        def _():
            acc[...] = jnp.zeros_like(acc)

        av = a_ref[...]
        if square_a:
            av = av.astype(F32)
            av = av * av
        acc[...] += _dot_tn(av.astype(BF16), b_ref[...].astype(BF16))

        @pl.when(k == nk - 1)
        def _():
            if blocked_out:
                for sub in range(nsub):
                    cols = slice(sub * FF_BLOCK, (sub + 1) * FF_BLOCK)
                    o_ref[sub] = acc[:, cols]
                    ob_ref[sub] = acc[:, cols].astype(BF16)
            else:
                o_ref[...] = acc[...]
                ob_ref[...] = acc[...].astype(BF16)

    if blocked_out:
        shape = (n // FF_BLOCK, m, FF_BLOCK)
        out_spec = pl.BlockSpec((nsub, tm, FF_BLOCK), lambda i, j, k: (j, i, 0))
    else:
        shape = (m, n)
        out_spec = pl.BlockSpec((tm, tn), lambda i, j, k: (i, j))
    return pl.pallas_call(
        body, name=name, grid=(m // tm, n // tn, nk),
        in_specs=[pl.BlockSpec((tk, tm), lambda i, j, k: (k, i)), pl.BlockSpec((tk, tn), lambda i, j, k: (k, j))],
        out_specs=[out_spec, out_spec],
        out_shape=[jax.ShapeDtypeStruct(shape, F32), jax.ShapeDtypeStruct(shape, BF16)],
        scratch_shapes=[pltpu.VMEM((tm, tn), F32)],
        compiler_params=_params(3),
    )(a, b)


def _adamw_math(w, g, m, v):
    m = ADAM_B1 * m + (1.0 - ADAM_B1) * g
    v = ADAM_B2 * v + (1.0 - ADAM_B2) * (g * g)
    m_hat = m / (1.0 - ADAM_B1 ** ADAM_STEP)
    v_hat = v / (1.0 - ADAM_B2 ** ADAM_STEP)
    delta = -ADAM_LR * (m_hat / (jnp.sqrt(v_hat) + ADAM_EPS) + ADAM_WD * w)
    return delta, m, v


def _adamw_sharded(w, m, v, own, recv0, recv1, name):
    _, rows, cols = w.shape
    t = _row_tile(rows, 256)

    def body(w_ref, m_ref, v_ref, own_ref, r0_ref, r1_ref, g_ref, d_ref, nm_ref, nv_ref):
        layer0 = pl.program_id(0) == 0
        g = own_ref[...]
        for k in range(N_DEV - 1):
            g = g + jnp.where(layer0, r0_ref[k], r1_ref[k]).astype(F32)
        g_ref[...] = g
        d_ref[...], nm_ref[...], nv_ref[...] = _adamw_math(w_ref[...], g, m_ref[...], v_ref[...])

    blk = pl.BlockSpec((None, t, cols), lambda l, i: (l, i, 0))
    recv = lambda layer: pl.BlockSpec((N_DEV - 1, t, cols), lambda l, i: (0, jnp.where(l == layer, i, 0), 0))
    return pl.pallas_call(
        body, name=name, grid=(2, rows // t),
        in_specs=[blk, blk, blk, blk, recv(0), recv(1)], out_specs=[blk] * 4,
        out_shape=[jax.ShapeDtypeStruct(w.shape, F32)] * 4,
        compiler_params=_params(2),
    )(w, m, v, own, recv0, recv1)


def _adamw_packed(w, g, m, v, name):
    def body(w_ref, g_ref, m_ref, v_ref, d_ref, nm_ref, nv_ref):
        d_ref[...], nm_ref[...], nv_ref[...] = _adamw_math(w_ref[...], g_ref[...], m_ref[...], v_ref[...])

    return pl.pallas_call(
        body, name=name, out_shape=[jax.ShapeDtypeStruct(w.shape, F32)] * 3,
        compiler_params=pltpu.CompilerParams(vmem_limit_bytes=VMEM_LIMIT),
    )(w, g, m, v)


def _peer(k):
    x, y, c = lax.axis_index("x"), lax.axis_index("y"), lax.axis_index("c")
    return (1 - x if k & 4 else x, 1 - y if k & 2 else y, 1 - c if k & 1 else c)


def _linear(dev):
    return 4 * dev[0] + 2 * dev[1] + dev[2]


HBM_SPEC = pl.BlockSpec(memory_space=pltpu.HBM)
SEM_SPEC = pl.BlockSpec(memory_space=pltpu.SEMAPHORE)
ANY_SPEC = pl.BlockSpec(memory_space=pl.ANY)
EFFECT = pltpu.SideEffectType.DATAFLOW_SIDE_EFFECTING


def _in_hbm(a):
    return pltpu.with_memory_space_constraint(a, pltpu.HBM)


class _Exchange:
    def __init__(self, name, groups, scatter, after=()):
        self.name, self.scatter = name, scatter
        self.sizes = sizes = [len(g) for g in groups]
        srcs = [a for g in groups for a in g]
        n, ng = len(srcs), len(groups)
        lead = (N_DEV - 1,) if scatter else (N_DEV,)
        shapes = [lead + (a.shape[1:] if scatter else a.shape) for a in srcs]
        lands = [lax.empty(sh, a.dtype) for sh, a in zip(shapes, srcs)]
        offsets = [sum(sizes[:gi]) for gi in range(ng)]
        copy = self._copy

        def body(*refs):
            src, land = refs[:n], refs[n:2 * n]
            sems = refs[2 * n + len(after):2 * n + len(after) + 2 * ng]
            token = refs[-1]
            for gi in range(ng):
                for wi in range(sizes[gi]):
                    w = offsets[gi] + wi
                    for k in range(1, N_DEV):
                        copy(src[w], land[w], sems[2 * gi], sems[2 * gi + 1], wi, k).start()
            token[...] = jnp.zeros_like(token)

        sem_shapes = [pltpu.SemaphoreType.DMA((7 * sz,)) for sz in sizes for _ in range(2)]
        outs = pl.pallas_call(
            body, name=name + "_start",
            in_specs=[HBM_SPEC] * (2 * n) + [ANY_SPEC] * len(after),
            out_specs=[SEM_SPEC] * (2 * ng) + [HBM_SPEC] * (2 * n) + [pl.BlockSpec(memory_space=pltpu.VMEM)],
            out_shape=sem_shapes + [pltpu.HBM(a.shape, a.dtype) for a in srcs + lands]
            + [jax.ShapeDtypeStruct((8, LANES), F32)],
            input_output_aliases={i: 2 * ng + i for i in range(2 * n)},
            compiler_params=pltpu.CompilerParams(has_side_effects=EFFECT),
        )(*[_in_hbm(a) for a in srcs + lands], *after)
        self.sems = [outs[2 * gi:2 * gi + 2] for gi in range(ng)]
        thru = outs[2 * ng:2 * ng + 2 * n]
        self.srcs = [thru[offsets[gi]:offsets[gi] + sizes[gi]] for gi in range(ng)]
        self.lands = [thru[n + offsets[gi]:n + offsets[gi] + sizes[gi]] for gi in range(ng)]
        self.token = outs[-1]

    def _copy(self, src, land, send_sems, recv_sems, wi, k):
        to = _peer(k)
        if self.scatter:
            src_ref, dst_ref = src.at[_linear(to)], land.at[k - 1]
        else:
            src_ref, dst_ref = src, land.at[_linear(_peer(0))]
        return pltpu.make_async_remote_copy(
            src_ref=src_ref, dst_ref=dst_ref, send_sem=send_sems.at[7 * wi + k - 1],
            recv_sem=recv_sems.at[7 * wi + k - 1], device_id=to, device_id_type=MESH)

    def wait(self, gi, after):
        n = self.sizes[gi]
        copy = self._copy

        def body(*refs):
            src, land = refs[:n], refs[n:2 * n]
            send_sems, recv_sems = refs[2 * n], refs[2 * n + 1]
            for wi in range(n):
                for k in range(1, N_DEV):
                    cp = copy(src[wi], land[wi], send_sems, recv_sems, wi, k)
                    cp.wait_send()
                    cp.wait_recv()

        arrays = list(self.srcs[gi]) + list(self.lands[gi])
        outs = pl.pallas_call(
            body, name=f"{self.name}_wait{gi}",
            in_specs=[HBM_SPEC] * (2 * n) + [SEM_SPEC, SEM_SPEC] + [ANY_SPEC] * len(after),
            out_specs=[HBM_SPEC] * (2 * n),
            out_shape=[pltpu.HBM(a.shape, a.dtype) for a in arrays],
            input_output_aliases={i: i for i in range(2 * n)},
            compiler_params=pltpu.CompilerParams(has_side_effects=EFFECT),
        )(*arrays, *self.sems[gi], *after)
        return outs[:n], outs[n:]


def _allreduce_packed(g):
    rows = g.shape[0]

    def body(g_ref, out_ref, buf, send_sems, recv_sems):
        me = _linear(_peer(0))
        buf[me] = g_ref[...]
        copies = []
        for k in range(1, N_DEV):
            copies.append(pltpu.make_async_remote_copy(
                src_ref=g_ref, dst_ref=buf.at[me], send_sem=send_sems.at[k - 1], recv_sem=recv_sems.at[k - 1],
                device_id=_peer(k), device_id_type=MESH))
        for cp in copies:
            cp.start()
        for k in range(1, N_DEV):
            pltpu.make_async_remote_copy(
                src_ref=g_ref, dst_ref=buf.at[_linear(_peer(k))], send_sem=send_sems.at[k - 1],
                recv_sem=recv_sems.at[k - 1], device_id=_peer(k), device_id_type=MESH).wait_recv()
        for cp in copies:
            cp.wait_send()
        total = buf[0]
        for d in range(1, N_DEV):
            total = total + buf[d]
        out_ref[...] = total

    return pl.pallas_call(
        body, name="allreduce_small",
        in_specs=[pl.BlockSpec(memory_space=pltpu.VMEM)], out_specs=pl.BlockSpec(memory_space=pltpu.VMEM),
        out_shape=jax.ShapeDtypeStruct(g.shape, F32),
        scratch_shapes=[pltpu.VMEM((N_DEV, rows, g.shape[1]), F32), pltpu.SemaphoreType.DMA((7,)),
                        pltpu.SemaphoreType.DMA((7,))],
        compiler_params=pltpu.CompilerParams(vmem_limit_bytes=VMEM_LIMIT),
    )(g)


def _rotary_tables(positions):
    rot_dim = HEAD_DIM // 4
    inv_freq = ROPE_THETA ** (-jnp.arange(0, rot_dim, 2, dtype=F32) / rot_dim)
    ang = positions.astype(F32)[:, None] * inv_freq
    cs = jnp.concatenate([jnp.cos(ang), jnp.sin(ang)], axis=1)
    dim = jnp.arange(LANES) % HEAD_DIM
    first, second = dim < ROT_SHIFT, (dim >= ROT_SHIFT) & (dim < rot_dim)
    src = jnp.arange(2 * ROT_SHIFT)[:, None]
    angle = (dim % ROT_SHIFT)[None, :]
    c = jnp.where((first | second)[None, :] & (src == angle), 1.0, 0.0)
    sa = jnp.where(second[None, :] & (src == angle + ROT_SHIFT), 1.0, 0.0)
    sb = jnp.where(first[None, :] & (src == angle + ROT_SHIFT), -1.0, 0.0)
    spread = jnp.concatenate([c, sa, sb], axis=1).astype(F32)
    base = jnp.concatenate([jnp.where(first | second, 0.0, 1.0), jnp.zeros((2 * LANES,))]).astype(F32)[None, :]
    return jnp.dot(cs, spread, precision=lax.Precision.HIGHEST, preferred_element_type=F32) + base


def _block_diag(pool_w):
    gc = pool_w.shape[-1]
    out = jnp.zeros((POOL_WIDTH, POOL_WIDTH), pool_w.dtype)
    for grp in range(pool_w.shape[0]):
        out = lax.dynamic_update_slice(out, pool_w[grp], (grp * gc, grp * gc))
    return out


def _diag_blocks(a):
    gc = POOL_WIDTH // len(POOL_WINDOWS)
    return jnp.stack([a[grp * gc:(grp + 1) * gc, grp * gc:(grp + 1) * gc] for grp in range(len(POOL_WINDOWS))])


def _local_step(x, p, positions, loss_target, norm1, pool_w, pool_scale, norm2, norm3, final_norm, weights, send):
    rc = rsa = rsb = _rotary_tables(positions)
    ones_bd = _block_diag(jnp.ones((4, HEAD_DIM, HEAD_DIM), BF16))
    saved = []
    h = x
    for i in range(2):
        tag = f"_l{i}"
        g1, g2, g3 = norm1[i:i + 1], norm2[i:i + 1], norm3[i:i + 1]
        w_bd = _block_diag(pool_w[i]).astype(BF16)
        scale = pool_scale[i:i + 1]
        w_in = weights(i, "in", (h, rc, w_bd))
        hn1, u, *qkv = _normproj_fwd(h, g1, w_in, rc, rsa, rsb, "normproj_fwd" + tag)
        qkv = [qkv[3 * grp:3 * grp + 3] for grp in range(3)]
        started = weights(i, "prefetch", (hn1,))
        pool_out, y = _pool_fwd(u, w_bd, scale, "pool_fwd" + tag, after=started)
        o, lse = zip(*[_attn_fwd(*qkv[grp], f"attn_fwd{tag}_g{grp}", after=started) for grp in range(3)])
        w_out = weights(i, "out", (pool_out, *o))
        h1, a = _outproj_fwd(h, pool_out, o, lse, w_out, "outproj_fwd" + tag)
        w_up, w_down, w_gate, w_ple = weights(i, "rest", (h1,))
        h2, hn2, r = _mlp_fwd(h1, g2, w_up, w_down, "mlp_fwd" + tag)
        h3, hn3, gate, pb = _gate_fwd(h2, g3, w_gate, p, i, w_ple, "gate_fwd" + tag)
        saved.append(dict(h0=h, hn1=hn1, qkv=qkv, y=y, o=o, lse=lse, a=a, h1=h1, hn2=hn2, r=r, h2=h2,
                          hn3=hn3, gate=gate, pb=pb, w_bd=w_bd, scale=scale, g1=g1, g2=g2, g3=g3,
                          w_in=w_in, w_out=w_out, w_up=w_up, w_down=w_down, w_gate=w_gate, w_ple=w_ple))
        h = h3
    loss, dh, d_final = _loss_head(h, final_norm.reshape(1, D_MODEL), loss_target, "loss_head")

    grads = [None, None]
    sent = ()
    for i in (1, 0):
        tag = f"_l{i}"
        sv = saved[i]
        dh2, dgl, de, dg3 = _gate_bwd(dh, sv["gate"], sv["pb"], sv["w_ple"], sv["h2"], sv["g3"], sv["w_gate"],
                                      "gate_bwd" + tag, after=sent)
        dw_gate = _matmul_tn(sv["hn3"], dgl, "dw_gate" + tag)
        dw_ple = _matmul_tn(sv["pb"], de, "dw_ple" + tag)
        dh1, dup, dg2, dh2b = _mlp_bwd(dh2, sv["r"], sv["h1"], sv["g2"], sv["w_up"], sv["w_down"], "mlp_bwd" + tag)
        dw_down = _matmul_tn(sv["r"], dh2b, "dw_down" + tag, square_a=True)
        dw_up = _matmul_tn(sv["hn2"], dup, "dw_up" + tag, blocked_out=True)
        dpool, do0, do1, do2, de0, de1, de2, dh1b = _outproj_bwd(dh1, sv["w_out"], sv["o"], sv["lse"], ones_bd,
                                                                 "outproj_bwd" + tag)
        dw_out = _matmul_tn(sv["a"], dh1b, "dw_out" + tag)
        sent = send(i, "main", dict(w_gate=dw_gate, w_ple=dw_ple, w_down=dw_down, w_up=dw_up, w_out=dw_out))
        dqkv = [_attn_bwd(*sv["qkv"][grp], do_g, sv["lse"][grp], de_g, f"attn_bwd{tag}_g{grp}", after=sent)
                for grp, (do_g, de_g) in enumerate(((do0, de0), (do1, de1), (do2, de2)))]
        dq, dk, dv = zip(*dqkv)
        du, dw_bd, dscale = _pool_bwd(dpool, sv["y"], sv["w_bd"], sv["scale"], "pool_bwd" + tag, after=sent)
        dh, dz, dg1 = _normproj_bwd(dh1, du, dq, dk, dv, rc, rsa, rsb, sv["w_in"], sv["h0"], sv["g1"],
                                    "normproj_bwd" + tag)
        dw_in = _matmul_tn(sv["hn1"], dz, "dw_in" + tag, tn=N_IN // 2)
        sent = send(i, "in", dict(w_in=dw_in))
        grads[i] = dict(norm1=dg1, norm2=dg2, norm3=dg3, pool_w=_diag_blocks(dw_bd), pool_scale=dscale)
    return loss, dh, grads, d_final, sent


def _pack_small(norm1, norm2, norm3, final_norm, pool_scale, pool_w, spare=None):
    spare = jnp.zeros((1, LANES), F32) if spare is None else spare
    scale_row = jnp.concatenate([pool_scale.reshape(1, 2 * POOL_WIDTH), spare,
                                 jnp.zeros((1, D_MODEL - 2 * POOL_WIDTH - LANES), F32)], axis=1)
    return jnp.concatenate([norm1, norm2, norm3, final_norm.reshape(1, D_MODEL), scale_row,
                            pool_w.reshape(32, D_MODEL)], axis=0)


def _unpack_small(a):
    return dict(norm1=a[0:2], norm2=a[2:4], norm3=a[4:6], final_norm=a[6], pool_scale=a[7, 0:2 * POOL_WIDTH].reshape(2, POOL_WIDTH),
                pool_w=a[8:40].reshape(2, 4, HEAD_DIM, HEAD_DIM))


def _chunks_cols(a, cols):
    return a.reshape(a.shape[0], N_DEV, cols).transpose(1, 0, 2)


def _chunks_rows(a, rows):
    return a.reshape(N_DEV, rows, a.shape[1])


BIG = ("w_in", "w_out", "w_up", "w_down", "w_gate", "w_ple")
SMALL = ("norm1", "norm2", "norm3", "final_norm", "pool_scale", "pool_w")
ORDER = ("norm1", "w_in", "pool_w", "pool_scale", "w_out", "norm2", "w_up", "w_down", "norm3", "w_gate", "w_ple",
         "final_norm")


def kernel(x, p, positions, norm1, w_in, pool_w, pool_scale, w_out, norm2, w_up, w_down, norm3, w_gate, w_ple, final_norm, loss_target, m_norm1, m_w_in, m_pool_w, m_pool_scale, m_w_out, m_norm2, m_w_up, m_w_down, m_norm3, m_w_gate, m_w_ple, m_final_norm, v_norm1, v_w_in, v_pool_w, v_pool_scale, v_w_out, v_norm2, v_w_up, v_w_down, v_norm3, v_w_gate, v_w_ple, v_final_norm):
    w = dict(norm1=norm1, w_in=w_in, pool_w=pool_w, pool_scale=pool_scale, w_out=w_out, norm2=norm2, w_up=w_up,
             w_down=w_down, norm3=norm3, w_gate=w_gate, w_ple=w_ple, final_norm=final_norm)
    m = dict(norm1=m_norm1, w_in=m_w_in, pool_w=m_pool_w, pool_scale=m_pool_scale, w_out=m_w_out, norm2=m_norm2,
             w_up=m_w_up, w_down=m_w_down, norm3=m_norm3, w_gate=m_w_gate, w_ple=m_w_ple, final_norm=m_final_norm)
    v = dict(norm1=v_norm1, w_in=v_w_in, pool_w=v_pool_w, pool_scale=v_pool_scale, w_out=v_w_out, norm2=v_norm2,
             w_up=v_w_up, w_down=v_w_down, norm3=v_norm3, w_gate=v_w_gate, w_ple=v_w_ple, final_norm=v_final_norm)
    seq = x.shape[1]

    bf = {n: [w[n][layer].astype(BF16) for layer in range(2)] for n in BIG}
    rest = ("w_up", "w_down", "w_gate", "w_ple")
    me = 4 * lax.axis_index("x") + 2 * lax.axis_index("y") + lax.axis_index("c")
    gathers = [_Exchange("gather_l0", [[bf["w_in"][0]], [bf["w_out"][0]], [bf[n][0] for n in rest]], scatter=False)]
    unpack = dict(w_in=lambda a: a.transpose(1, 0, 2).reshape(D_MODEL, N_IN),
                  w_out=lambda a: a.reshape(D_MODEL, D_MODEL), w_gate=lambda a: a.reshape(D_MODEL, D_MODEL),
                  w_ple=lambda a: a.transpose(1, 0, 2).reshape(PLE_DIM, D_MODEL), w_up=lambda a: a, w_down=lambda a: a)
    parts = dict(zip(("in", "out", "rest"), (("w_in",), ("w_out",), rest)))

    def weights(layer, part, after):
        if part == "prefetch":
            if layer != 0:
                return ()
            gathers.append(_Exchange("gather_l1", [[bf[n][1] for n in parts[pt]] for pt in parts], scatter=False,
                                     after=after))
            return (gathers[1].token,)
        shards, lands = gathers[layer].wait(tuple(parts).index(part), after)
        full = [unpack[n](lax.dynamic_update_slice_in_dim(land, shard[None], me, axis=0))
                for n, shard, land in zip(parts[part], shards, lands)]
        return full if part == "rest" else full[0]

    to_chunks = dict(w_in=lambda a: _chunks_cols(a, N_IN // N_DEV), w_out=lambda a: _chunks_rows(a, D_MODEL // N_DEV),
                     w_up=lambda a: a, w_down=lambda a: _chunks_rows(a, FF_BLOCK),
                     w_gate=lambda a: _chunks_rows(a, D_MODEL // N_DEV), w_ple=lambda a: _chunks_cols(a, D_MODEL // N_DEV))
    own = {n: [None, None] for n in BIG}
    scatters = {}

    def own_chunk(n, g32):
        if n in ("w_in", "w_ple"):
            cols = g32.shape[1] // N_DEV
            return lax.dynamic_slice(g32, (0, me * cols), (g32.shape[0], cols))
        return lax.dynamic_index_in_dim(to_chunks[n](g32), me, axis=0, keepdims=False)

    def send(layer, part, grads):
        for n, (g32, _) in grads.items():
            own[n][layer] = own_chunk(n, g32)
        ex = _Exchange(f"scatter_{part}_l{layer}", [[to_chunks[n](g16) for n, (_, g16) in grads.items()]], scatter=True)
        scatters[layer, part] = (tuple(grads), ex)
        return (ex.token,)

    loss, dx, grads, d_final, sent = _local_step(
        x.reshape(seq, D_MODEL), p.reshape(2, seq, PLE_DIM), positions.reshape(seq), loss_target.reshape(seq, D_MODEL),
        norm1, pool_w, pool_scale, norm2, norm3, final_norm, weights, send)

    small_g = _pack_small(
        *[jnp.concatenate([grads[0][n], grads[1][n]], axis=0) for n in ("norm1", "norm2", "norm3")], d_final.reshape(D_MODEL),
        jnp.concatenate([grads[0]["pool_scale"], grads[1]["pool_scale"]], axis=0),
        jnp.stack([grads[0]["pool_w"], grads[1]["pool_w"]]), spare=loss)
    small_g = _allreduce_packed(small_g)

    g_out, d_out, m_out, v_out = {}, {}, {}, {}
    for part in ("main", "in"):
        recv = {}
        for layer in (1, 0):
            names, ex = scatters[layer, part]
            for n, r in zip(names, ex.wait(0, sent)[1]):
                recv[n, layer] = r
        for n in names:
            g_out[n], d_out[n], m_out[n], v_out[n] = _adamw_sharded(
                w[n], m[n], v[n], jnp.stack(own[n]), recv[n, 0], recv[n, 1], "adamw_" + n)
        sent = tuple(d_out[n] for n in names)
    pack = lambda t: _pack_small(*[t[n] for n in SMALL])
    d_small, m_small, v_small = _adamw_packed(pack(w), small_g, pack(m), pack(v), "adamw_small")
    for dst, a in ((g_out, small_g), (d_out, d_small), (m_out, m_small), (v_out, v_small)):
        dst.update(_unpack_small(a))

    return (small_g[7, 2 * POOL_WIDTH],dx.reshape(1, seq, D_MODEL), *[g_out[n] for n in ORDER], *[d_out[n] for n in ORDER],
            *[m_out[n] for n in ORDER], *[v_out[n] for n in ORDER])
```

```python
import functools

import jax
import jax.numpy as jnp
from jax import lax
from jax.experimental import pallas as pl
from jax.experimental.pallas import tpu as pltpu

F32 = jnp.float32
BF16 = jnp.bfloat16

D_MODEL = 1024
HEAD_DIM = 64
POOL_WIDTH = 256
POOL_WINDOWS = (2, 4, 8, 16)
POOL_HALO = 16
POOL_PAD = 8
GROUP_WIDTH = 256
DILATIONS = (1, 4, 16)
ATTN_BLOCK = 128
ROT_SHIFT = 8
ROPE_THETA = 500000.0
D_FF = 4096
FF_BLOCK = 512
FF_PER_STEP = 2
FF_PER_STEP_BWD = 4
N_DEV = 8
N_IN = POOL_WIDTH + 3 * 768
PLE_DIM = 256
EPS = 1e-6
NEG_BIG = -1e30

ADAM_LR = 0.001
ADAM_B1 = 0.9
ADAM_B2 = 0.999
ADAM_EPS = 1e-08
ADAM_WD = 0.01
ADAM_STEP = 10

LANES = 128
VMEM_LIMIT = 56 * 1024 * 1024
MESH = pl.DeviceIdType.MESH


def _params(n_grid):
    return pltpu.CompilerParams(dimension_semantics=("arbitrary",) * n_grid, vmem_limit_bytes=VMEM_LIMIT)


def _dot(a, b):
    return jnp.dot(a, b, preferred_element_type=F32)


def _dot_nt(a, b):
    return lax.dot_general(a, b, (((1,), (1,)), ((), ())), preferred_element_type=F32)


def _dot_tn(a, b):
    return lax.dot_general(a, b, (((0,), (0,)), ((), ())), preferred_element_type=F32)


def _rms(x, g):
    rstd = lax.rsqrt(jnp.mean(x * x, axis=-1, keepdims=True) + EPS)
    n = x * rstd
    return n, rstd, n * g


def _rms_bwd(dy, n, rstd, g):
    dyn = dy * g
    dx = rstd * (dyn - n * jnp.mean(dyn * n, axis=-1, keepdims=True))
    return dx, jnp.sum(dy * n, axis=0, keepdims=True)


def _ordered_after(body, n_in, after):
    if not after:
        return body
    return lambda *refs: body(*refs[:n_in], *refs[n_in + len(after):])


def _row_tile(s, t):
    t = min(s, t)
    assert s % t == 0
    return t


def _rot(z, c, sa, sb):
    return z * c + pltpu.roll(z, ROT_SHIFT, 1) * sa + pltpu.roll(z, LANES - ROT_SHIFT, 1) * sb


def _table_specs(t):
    return [pl.BlockSpec((t, LANES), functools.partial(lambda i, k: (i, k), k=k)) for k in range(3)]


def _rot_t(dz, c, sa, sb):
    return dz * c + pltpu.roll(dz * sa, LANES - ROT_SHIFT, 1) + pltpu.roll(dz * sb, ROT_SHIFT, 1)


def _to_residues(value, stage, out_ref, dil):
    if dil == 1:
        out_ref[0] = value.astype(out_ref.dtype)
        return
    rows = value.shape[0] // dil
    for hf in range(GROUP_WIDTH // LANES):
        lanes = slice(hf * LANES, (hf + 1) * LANES)
        stage[hf][...] = value[:, lanes]
        for r in range(dil):
            out_ref[r, :, lanes] = stage[hf][pl.ds(r, rows, stride=dil), :].astype(out_ref.dtype)


def _from_residues(in_ref, stage, dil):
    if dil == 1:
        return in_ref[0].astype(F32)
    rows = in_ref.shape[1]
    for hf in range(GROUP_WIDTH // LANES):
        for r in range(dil):
            stage[hf][pl.ds(r, rows, stride=dil), :] = in_ref[r, :, hf * LANES:(hf + 1) * LANES].astype(F32)
    return jnp.concatenate([stage[0][...], stage[1][...]], axis=1)


def _residue_spec(dil, t):
    return pl.BlockSpec((dil, t // dil, GROUP_WIDTH), lambda i: (0, i, 0))


def _residue_shape(dil, s, dtype):
    return jax.ShapeDtypeStruct((dil, s // dil, GROUP_WIDTH), dtype)


def _stages(t, n):
    return [pltpu.VMEM((t, LANES), F32)] * (n * (GROUP_WIDTH // LANES))


def _pair_stages(refs):
    return [refs[i:i + 2] for i in range(0, len(refs), 2)]


def _normproj_fwd(h, g, w_in, rc, rsa, rsb, name):
    s = h.shape[0]
    t = _row_tile(s, 512)

    def body(h_ref, g_ref, w_ref, c_ref, sa_ref, sb_ref, hn_ref, u_ref, *rest):
        qkv_refs, stages = rest[:9], _pair_stages(rest[9:])
        _, _, hn = _rms(h_ref[...], g_ref[...])
        hb = hn.astype(BF16)
        hn_ref[...] = hb
        c, sa, sb = c_ref[...], sa_ref[...], sb_ref[...]

        def rot(z, scale):
            halves = [_rot(z[:, hf * LANES:(hf + 1) * LANES], c, sa, sb) * scale for hf in range(2)]
            return jnp.concatenate(halves, axis=1)

        u_ref[...] = _dot(hb, w_ref[:, 0:POOL_WIDTH])
        for grp, dil in enumerate(DILATIONS):
            lo = POOL_WIDTH + grp * GROUP_WIDTH
            q_ref, k_ref, v_ref = qkv_refs[3 * grp:3 * grp + 3]
            _to_residues(rot(_dot(hb, w_ref[:, lo:lo + GROUP_WIDTH]), HEAD_DIM ** -0.5), stages[0], q_ref, dil)
            _to_residues(rot(_dot(hb, w_ref[:, lo + 768:lo + 768 + GROUP_WIDTH]), 1.0), stages[1], k_ref, dil)
            _to_residues(_dot(hb, w_ref[:, lo + 1536:lo + 1536 + GROUP_WIDTH]), stages[2], v_ref, dil)

    row = lambda w: pl.BlockSpec((t, w), lambda i: (i, 0))
    return pl.pallas_call(
        body, name=name, grid=(s // t,),
        in_specs=[row(D_MODEL), pl.BlockSpec((1, D_MODEL), lambda i: (0, 0)),
                  pl.BlockSpec((D_MODEL, N_IN), lambda i: (0, 0))] + _table_specs(t),
        out_specs=[row(D_MODEL), row(POOL_WIDTH)] + [_residue_spec(dil, t) for dil in DILATIONS for _ in range(3)],
        out_shape=[jax.ShapeDtypeStruct((s, D_MODEL), BF16), jax.ShapeDtypeStruct((s, POOL_WIDTH), F32)]
        + [_residue_shape(dil, s, BF16) for dil in DILATIONS for _ in range(3)],
        scratch_shapes=_stages(t, 3),
        compiler_params=_params(1),
    )(h, g, w_in, rc, rsa, rsb)


def _pool_lane_window():
    lane = lax.broadcasted_iota(jnp.int32, (1, POOL_WIDTH), 1)
    return jnp.left_shift(2, lane // (POOL_WIDTH // len(POOL_WINDOWS)))


def _window_sums(ext, b2, b4, b8, t, lo, tile, direction):
    rows = t + POOL_HALO
    for src, dst, sh in ((ext, b2, 1), (b2, b4, 2), (b4, b8, 4)):
        dst[lo:lo + rows, :] = src[lo:lo + rows, :] + src[lo + direction * sh:lo + direction * sh + rows, :]
    s16 = b8[tile:tile + t, :] + b8[tile + direction * 8:tile + direction * 8 + t, :]
    win = _pool_lane_window()
    return jnp.where(win == 2, b2[tile:tile + t, :],
                     jnp.where(win == 4, b4[tile:tile + t, :], jnp.where(win == 8, b8[tile:tile + t, :], s16)))


def _pool_fwd(u, w_bd, scale, name, after=()):
    s = u.shape[0]
    t = _row_tile(s, 512)
    first = POOL_PAD + POOL_HALO

    def body(u_ref, w_ref, sc_ref, out_ref, y_ref, ext, b2, b4, b8):
        i = pl.program_id(0)

        @pl.when(i == 0)
        def _():
            for buf in (ext, b2, b4):
                buf[0:POOL_PAD, :] = jnp.zeros((POOL_PAD, POOL_WIDTH), F32)
            ext[POOL_PAD:first, :] = jnp.zeros((POOL_HALO, POOL_WIDTH), F32)

        x = u_ref[...]
        ext[first:, :] = x
        wsum = _window_sums(ext, b2, b4, b8, t, POOL_PAD, first, -1)
        pos = i * t + lax.broadcasted_iota(jnp.int32, (t, POOL_WIDTH), 0)
        cnt = jnp.minimum(pos + 1, _pool_lane_window()).astype(F32)
        y = wsum / cnt - x
        yb = y.astype(BF16)
        y_ref[...] = yb
        out_ref[...] = _dot(yb, w_ref[...]) * sc_ref[...]
        ext[POOL_PAD:first, :] = x[t - POOL_HALO:, :]

    row = pl.BlockSpec((t, POOL_WIDTH), lambda i: (i, 0))
    return pl.pallas_call(
        _ordered_after(body, 3, after), name=name, grid=(s // t,),
        in_specs=[row, pl.BlockSpec((POOL_WIDTH, POOL_WIDTH), lambda i: (0, 0)),
                  pl.BlockSpec((1, POOL_WIDTH), lambda i: (0, 0))] + [pl.BlockSpec(memory_space=pl.ANY)] * len(after),
        out_specs=[row, row],
        out_shape=[jax.ShapeDtypeStruct((s, POOL_WIDTH), F32), jax.ShapeDtypeStruct((s, POOL_WIDTH), BF16)],
        scratch_shapes=[pltpu.VMEM((t + POOL_HALO + POOL_PAD, POOL_WIDTH), F32)] * 4,
        compiler_params=_params(1),
    )(u, w_bd, scale, *after)


def _head_masks():
    lane = lax.broadcasted_iota(jnp.int32, (ATTN_BLOCK, GROUP_WIDTH), 1)
    return [lane // HEAD_DIM == hd for hd in range(GROUP_WIDTH // HEAD_DIM)]


def _stack_heads(a, masks):
    zero = jnp.zeros_like(a)
    return jnp.concatenate([jnp.where(m, a, zero) for m in masks], axis=0)


def _band_bias(first_step):
    rows = ATTN_BLOCK * (GROUP_WIDTH // HEAD_DIM)
    i = lax.broadcasted_iota(jnp.int32, (rows, 2 * ATTN_BLOCK), 0) & (ATTN_BLOCK - 1)
    j = lax.broadcasted_iota(jnp.int32, (rows, 2 * ATTN_BLOCK), 1)
    inner = jnp.where((j >= i) & (j <= i + ATTN_BLOCK), 0.0, NEG_BIG)
    return jnp.where((j < ATTN_BLOCK) & first_step, NEG_BIG, inner), inner


def _column_per_head(a):
    return jnp.concatenate([a[:, hd * HEAD_DIM:hd * HEAD_DIM + 1] for hd in range(GROUP_WIDTH // HEAD_DIM)], axis=0)


def _blocks_per_step(nb):
    return 8 if nb % 8 == 0 else 4 if nb % 4 == 0 else 2 if nb % 2 == 0 else 1


def _residues_per_step(dil, nb, qb):
    return 2 if (nb == qb and qb < 8 and dil % 2 == 0) else 1


def _attn_fwd(q, k, v, name, after=()):
    dil, length, _ = q.shape
    nb = length // ATTN_BLOCK
    qb = _blocks_per_step(nb)
    rs = _residues_per_step(dil, nb, qb)

    def body(q_ref, kp_ref, kc_ref, vp_ref, vc_ref, o_ref, lse_ref):
        masks = _head_masks()
        bias = _band_bias(pl.program_id(1) == 0)
        for rr in range(rs):
            for qi in range(qb):
                here = slice(qi * ATTN_BLOCK, (qi + 1) * ATTN_BLOCK)
                before = slice((qi - 1) * ATTN_BLOCK, qi * ATTN_BLOCK)
                kcat = jnp.concatenate([kp_ref[rr] if qi == 0 else kc_ref[rr, before], kc_ref[rr, here]], axis=0)
                vcat = jnp.concatenate([vp_ref[rr] if qi == 0 else vc_ref[rr, before], vc_ref[rr, here]], axis=0)
                qs = _stack_heads(q_ref[rr, here], masks)
                sc = _dot_nt(qs, kcat) + bias[min(qi, 1)]
                m = jnp.max(sc, axis=1, keepdims=True)
                e = jnp.exp(sc - m)
                l = jnp.sum(e, axis=1, keepdims=True)
                p = (e / l).astype(BF16)
                lse = m + jnp.log(l)
                o = jnp.zeros((ATTN_BLOCK, GROUP_WIDTH), F32)
                lse_full = jnp.zeros((ATTN_BLOCK, GROUP_WIDTH), F32)
                for hd, msk in enumerate(masks):
                    rows = slice(hd * ATTN_BLOCK, (hd + 1) * ATTN_BLOCK)
                    o = jnp.where(msk, _dot(p[rows], vcat), o)
                    lse_full = jnp.where(msk, lse[rows], lse_full)
                o_ref[rr, here] = o.astype(o_ref.dtype)
                lse_ref[rr, here] = lse_full

    cur = pl.BlockSpec((rs, qb * ATTN_BLOCK, GROUP_WIDTH), lambda r, j: (r, j, 0))
    prev = pl.BlockSpec((rs, ATTN_BLOCK, GROUP_WIDTH), lambda r, j: (r, jnp.maximum(qb * j - 1, 0), 0))
    return pl.pallas_call(
        _ordered_after(body, 5, after), name=name, grid=(dil // rs, nb // qb),
        in_specs=[cur, prev, cur, prev, cur] + [pl.BlockSpec(memory_space=pl.ANY)] * len(after), out_specs=[cur, cur],
        out_shape=[jax.ShapeDtypeStruct(q.shape, BF16), jax.ShapeDtypeStruct(q.shape, F32)],
        compiler_params=_params(2),
    )(q, k, k, v, v, *after)


def _group_weights(l0, l1, l2):
    m = jnp.maximum(jnp.maximum(l0, l1), l2)
    e0, e1, e2 = jnp.exp(l0 - m), jnp.exp(l1 - m), jnp.exp(l2 - m)
    den = e0 + e1 + e2
    return e0 / den, e1 / den, e2 / den


def _outproj_fwd(h, pool_out, o, lse, w_out, name):
    s = h.shape[0]
    t = _row_tile(s, 512)

    def body(h_ref, po_ref, o0, o1, o2, l0, l1, l2, w_ref, out_ref, a_ref, *stages):
        stages = _pair_stages(stages)
        ov =[_from_residues(r, stages[i], DILATIONS[i]) for i, r in enumerate((o0, o1, o2))]
        lv = [_from_residues(r, stages[3 + i], DILATIONS[i]) for i, r in enumerate((l0, l1, l2))]
        wts = _group_weights(*lv)
        a = jnp.concatenate([po_ref[...]] + [ov[i] * wts[i] for i in range(3)], axis=1).astype(BF16)
        a_ref[...] = a
        out_ref[...] = h_ref[...] + _dot(a, w_ref[...])

    row = lambda w: pl.BlockSpec((t, w), lambda i: (i, 0))
    res = [_residue_spec(dil, t) for dil in DILATIONS]
    return pl.pallas_call(
        body, name=name, grid=(s // t,),
        in_specs=[row(D_MODEL), row(POOL_WIDTH)] + res + res + [pl.BlockSpec((D_MODEL, D_MODEL), lambda i: (0, 0))],
        out_specs=[row(D_MODEL), row(D_MODEL)],
        out_shape=[jax.ShapeDtypeStruct((s, D_MODEL), F32), jax.ShapeDtypeStruct((s, D_MODEL), BF16)],
        scratch_shapes=_stages(t, 6),
        compiler_params=_params(1),
    )(h, pool_out, *o, *lse, w_out)


def _mlp_fwd(h, g, w_up, w_down, name):
    s = h.shape[0]
    t = _row_tile(s, 1024)
    nblk = D_FF // (FF_PER_STEP * FF_BLOCK)

    def body(h_ref, g_ref, wu_ref, wd_ref, out_ref, hn_ref, r_ref, hb_s, acc):
        j = pl.program_id(1)

        @pl.when(j == 0)
        def _():
            _, _, hn = _rms(h_ref[...], g_ref[...])
            hb = hn.astype(BF16)
            hb_s[...] = hb
            hn_ref[...] = hb
            acc[...] = jnp.zeros_like(acc)

        hb = hb_s[...]
        acts = []
        for b in range(FF_PER_STEP):
            r = jnp.maximum(_dot(hb, wu_ref[b]), 0.0)
            r_ref[:, b * FF_BLOCK:(b + 1) * FF_BLOCK] = r.astype(BF16)
            acts.append((r * r).astype(BF16))
        acc[...] += _dot(jnp.concatenate(acts, axis=1), wd_ref[...].reshape(FF_PER_STEP * FF_BLOCK, D_MODEL))

        @pl.when(j == nblk - 1)
        def _():
            out_ref[...] = h_ref[...] + acc[...]

    row = pl.BlockSpec((t, D_MODEL), lambda i, j: (i, 0))
    return pl.pallas_call(
        body, name=name, grid=(s // t, nblk),
        in_specs=[row, pl.BlockSpec((1, D_MODEL), lambda i, j: (0, 0)),
                  pl.BlockSpec((FF_PER_STEP, D_MODEL, FF_BLOCK), lambda i, j: (j, 0, 0)),
                  pl.BlockSpec((FF_PER_STEP, FF_BLOCK, D_MODEL), lambda i, j: (j, 0, 0))],
        out_specs=[row, row, pl.BlockSpec((t, FF_PER_STEP * FF_BLOCK), lambda i, j: (i, j))],
        out_shape=[jax.ShapeDtypeStruct((s, D_MODEL), F32), jax.ShapeDtypeStruct((s, D_MODEL), BF16),
                   jax.ShapeDtypeStruct((s, D_FF), BF16)],
        scratch_shapes=[pltpu.VMEM((t, D_MODEL), BF16), pltpu.VMEM((t, D_MODEL), F32)],
        compiler_params=_params(2),
    )(h, g, w_up, w_down)


def _gate_fwd(h, g, w_gate, p, layer, w_ple, name):
    s = h.shape[0]
    t = _row_tile(s, 512)

    def body(h_ref, g_ref, wg_ref, p_ref, wp_ref, out_ref, hn_ref, gate_ref, pb_ref):
        x = h_ref[...]
        _, _, hn = _rms(x, g_ref[...])
        hb = hn.astype(BF16)
        hn_ref[...] = hb
        gate = 1.0 / (1.0 + jnp.exp(-_dot(hb, wg_ref[...])))
        pb = p_ref[...].astype(BF16)
        pb_ref[...] = pb
        gate_ref[...] = gate.astype(BF16)
        out_ref[...] = x + gate * _dot(pb, wp_ref[...])

    row = lambda w: pl.BlockSpec((t, w), lambda i: (i, 0))
    full = lambda a, b: pl.BlockSpec((a, b), lambda i: (0, 0))
    return pl.pallas_call(
        body, name=name, grid=(s // t,),
        in_specs=[row(D_MODEL), full(1, D_MODEL), full(D_MODEL, D_MODEL),
                  pl.BlockSpec((None, t, PLE_DIM), lambda i: (layer, i, 0)), full(PLE_DIM, D_MODEL)],
        out_specs=[row(D_MODEL), row(D_MODEL), row(D_MODEL), row(PLE_DIM)],
        out_shape=[jax.ShapeDtypeStruct((s, D_MODEL), F32), jax.ShapeDtypeStruct((s, D_MODEL), BF16),
                   jax.ShapeDtypeStruct((s, D_MODEL), BF16), jax.ShapeDtypeStruct((s, PLE_DIM), BF16)],
        compiler_params=_params(1),
    )(h, g, w_gate, p, w_ple)


def _loss_head(h, g, target, name):
    s = h.shape[0]
    t = _row_tile(s, 512)

    def body(h_ref, g_ref, t_ref, loss_ref, dh_ref, dg_ref):
        i = pl.program_id(0)

        @pl.when(i == 0)
        def _():
            loss_ref[...] = jnp.zeros_like(loss_ref)
            dg_ref[...] = jnp.zeros_like(dg_ref)

        gv = g_ref[...]
        n, rstd, y = _rms(h_ref[...], gv)
        err = y - t_ref[...]
        loss_ref[...] += jnp.sum(err * err) * (0.5 / D_MODEL)
        dx, dg = _rms_bwd(err * (1.0 / D_MODEL), n, rstd, gv)
        dh_ref[...] = dx
        dg_ref[...] += dg

    row = pl.BlockSpec((t, D_MODEL), lambda i: (i, 0))
    vec = pl.BlockSpec((1, D_MODEL), lambda i: (0, 0))
    return pl.pallas_call(
        body, name=name, grid=(s // t,),
        in_specs=[row, vec, row],
        out_specs=[pl.BlockSpec((1, LANES), lambda i: (0, 0)), row, vec],
        out_shape=[jax.ShapeDtypeStruct((1, LANES), F32), jax.ShapeDtypeStruct((s, D_MODEL), F32),
                   jax.ShapeDtypeStruct((1, D_MODEL), F32)],
        compiler_params=_params(1),
    )(h, g, target)


def _gate_bwd(dh, gate, pb, w_ple, h, g, w_gate, name, after=()):
    s = h.shape[0]
    t = _row_tile(s, 512)

    def body(dh_ref, gate_ref, pb_ref, wp_ref, h_ref, g_ref, wg_ref, out_ref, dgl_ref, de_ref, dg_ref):
        @pl.when(pl.program_id(0) == 0)
        def _():
            dg_ref[...] = jnp.zeros_like(dg_ref)

        d = dh_ref[...]
        gate = gate_ref[...].astype(F32)
        e = _dot(pb_ref[...], wp_ref[...])
        dgl = (d * e * gate * (1.0 - gate)).astype(BF16)
        dgl_ref[...] = dgl
        de_ref[...] = (d * gate).astype(BF16)
        gv = g_ref[...]
        n, rstd, _ = _rms(h_ref[...], gv)
        dx, dg = _rms_bwd(_dot_nt(dgl, wg_ref[...]), n, rstd, gv)
        out_ref[...] = d + dx
        dg_ref[...] += dg

    row = lambda w: pl.BlockSpec((t, w), lambda i: (i, 0))
    full = lambda a, b: pl.BlockSpec((a, b), lambda i: (0, 0))
    return pl.pallas_call(
        _ordered_after(body, 7, after), name=name, grid=(s // t,),
        in_specs=[row(D_MODEL), row(D_MODEL), row(PLE_DIM), full(PLE_DIM, D_MODEL), row(D_MODEL), full(1, D_MODEL),
                  full(D_MODEL, D_MODEL)] + [pl.BlockSpec(memory_space=pl.ANY)] * len(after),
        out_specs=[row(D_MODEL), row(D_MODEL), row(D_MODEL), full(1, D_MODEL)],
        out_shape=[jax.ShapeDtypeStruct((s, D_MODEL), F32), jax.ShapeDtypeStruct((s, D_MODEL), BF16),
                   jax.ShapeDtypeStruct((s, D_MODEL), BF16), jax.ShapeDtypeStruct((1, D_MODEL), F32)],
        compiler_params=_params(1),
    )(dh, gate, pb, w_ple, h, g, w_gate, *after)


def _mlp_bwd(dh, r, h, g, w_up, w_down, name):
    s = h.shape[0]
    t = _row_tile(s, 512)
    per = FF_PER_STEP_BWD
    nblk = D_FF // (per * FF_BLOCK)

    def body(dh_ref, r_ref, h_ref, g_ref, wu_ref, wd_ref, out_ref, dup_ref, dg_ref, db_s, acc):
        i, j = pl.program_id(0), pl.program_id(1)

        @pl.when((i == 0) & (j == 0))
        def _():
            dg_ref[...] = jnp.zeros_like(dg_ref)

        @pl.when(j == 0)
        def _():
            db_s[...] = dh_ref[...].astype(BF16)
            acc[...] = jnp.zeros_like(acc)

        db = db_s[...]
        back = None
        for b in range(per):
            cols = slice(b * FF_BLOCK, (b + 1) * FF_BLOCK)
            dup = (_dot_nt(db, wd_ref[b]) * (2.0 * r_ref[:, cols].astype(F32))).astype(BF16)
            dup_ref[:, cols] = dup
            part = _dot_nt(dup, wu_ref[b])
            back = part if back is None else back + part
        acc[...] += back

        @pl.when(j == nblk - 1)
        def _():
            gv = g_ref[...]
            n, rstd, _ = _rms(h_ref[...], gv)
            dx, dg = _rms_bwd(acc[...], n, rstd, gv)
            out_ref[...] = dh_ref[...] + dx
            dg_ref[...] += dg

    row = pl.BlockSpec((t, D_MODEL), lambda i, j: (i, 0))
    vec = pl.BlockSpec((1, D_MODEL), lambda i, j: (0, 0))
    blk = pl.BlockSpec((t, per * FF_BLOCK), lambda i, j: (i, j))
    return pl.pallas_call(
        body, name=name, grid=(s // t, nblk),
        in_specs=[row, blk, row, vec,
                  pl.BlockSpec((per, D_MODEL, FF_BLOCK), lambda i, j: (j, 0, 0)),
                  pl.BlockSpec((per, FF_BLOCK, D_MODEL), lambda i, j: (j, 0, 0))],
        out_specs=[row, blk, vec, row],
        out_shape=[jax.ShapeDtypeStruct((s, D_MODEL), F32), jax.ShapeDtypeStruct((s, D_FF), BF16),
                   jax.ShapeDtypeStruct((1, D_MODEL), F32), jax.ShapeDtypeStruct((s, D_MODEL), BF16)],
        scratch_shapes=[pltpu.VMEM((t, D_MODEL), F32)],
        compiler_params=_params(2),
    )(dh, r, h, g, w_up, w_down)


def _outproj_bwd(dh, w_out, o, lse, ones_bd, name):
    s = dh.shape[0]
    t = _row_tile(s, 512)

    def body(dh_ref, w_ref, o0, o1, o2, l0, l1, l2, bd_ref, dp_ref, do0, do1, do2, de0, de1, de2, dhb_ref, *stages):
        stages = _pair_stages(stages)
        dhb = dh_ref[...].astype(BF16)
        dhb_ref[...] = dhb
        da = _dot_nt(dhb, w_ref[...])
        dp_ref[...] = da[:, 0:POOL_WIDTH]
        ov =[_from_residues(r, stages[i], DILATIONS[i]) for i, r in enumerate((o0, o1, o2))]
        lv = [_from_residues(r, stages[3 + i], DILATIONS[i]) for i, r in enumerate((l0, l1, l2))]
        wts = _group_weights(*lv)
        bd = bd_ref[...]
        cbar = jnp.zeros((t, GROUP_WIDTH), F32)
        for grp, do_ref in enumerate((do0, do1, do2)):
            lo = POOL_WIDTH + grp * GROUP_WIDTH
            dag = da[:, lo:lo + GROUP_WIDTH]
            _to_residues(dag * wts[grp], stages[6 + grp], do_ref, DILATIONS[grp])
            prod = dag * ov[grp]
            hi = prod.astype(BF16)
            low = (prod - hi.astype(F32)).astype(BF16)
            cbar = cbar + wts[grp] * (_dot(hi, bd) + _dot(low, bd))
        for grp, de_ref in enumerate((de0, de1, de2)):
            _to_residues(wts[grp] * cbar, stages[9 + grp], de_ref, DILATIONS[grp])

    row = lambda w: pl.BlockSpec((t, w), lambda i: (i, 0))
    full = lambda a, b: pl.BlockSpec((a, b), lambda i: (0, 0))
    res = [_residue_spec(dil, t) for dil in DILATIONS]
    return pl.pallas_call(
        body, name=name, grid=(s // t,),
        in_specs=[row(D_MODEL), full(D_MODEL, D_MODEL)] + res + res + [full(GROUP_WIDTH, GROUP_WIDTH)],
        out_specs=[row(POOL_WIDTH)] + res + res + [row(D_MODEL)],
        out_shape=[jax.ShapeDtypeStruct((s, POOL_WIDTH), F32)] + [_residue_shape(dil, s, BF16) for dil in DILATIONS]
        + [_residue_shape(dil, s, F32) for dil in DILATIONS] + [jax.ShapeDtypeStruct((s, D_MODEL), BF16)],
        scratch_shapes=_stages(t, 12),
        compiler_params=_params(1),
    )(dh, w_out, *o, *lse, ones_bd)


def _attn_bwd(q, k, v, do, lse, deff, name, after=()):
    dil, length, _ = q.shape
    nb = length // ATTN_BLOCK
    qb = _blocks_per_step(nb)
    nj = nb // qb
    rs = _residues_per_step(dil, nb, qb)
    whole = nj == 1
    tail = slice((qb - 1) * ATTN_BLOCK, qb * ATTN_BLOCK)
    block = lambda qi: slice(qi * ATTN_BLOCK, (qi + 1) * ATTN_BLOCK)

    def body(q_ref, kp_ref, kc_ref, vp_ref, vc_ref, do_ref, lse_ref, de_ref, dq_ref, dk_ref, dv_ref, ck, cv):
        j = pl.program_id(1)

        def compute():
            masks = _head_masks()
            bias = _band_bias(j == 0)
            for rr in range(rs):
                dkc, dvc = [], []
                for qi in range(qb):
                    here, before = block(qi), block(qi - 1)
                    kcat = jnp.concatenate([kp_ref[rr] if qi == 0 else kc_ref[rr, before], kc_ref[rr, here]], axis=0)
                    vcat = jnp.concatenate([vp_ref[rr] if qi == 0 else vc_ref[rr, before], vc_ref[rr, here]], axis=0)
                    qs = _stack_heads(q_ref[rr, here], masks)
                    dos = _stack_heads(do_ref[rr, here], masks)
                    sc = _dot_nt(qs, kcat) + bias[min(qi, 1)]
                    p = jnp.exp(sc - _column_per_head(lse_ref[rr, here]))
                    ds = (p * (_dot_nt(dos, vcat) - _column_per_head(de_ref[rr, here]))).astype(BF16)
                    dq = jnp.zeros((ATTN_BLOCK, GROUP_WIDTH), F32)
                    for hd, msk in enumerate(masks):
                        dq = jnp.where(msk, _dot(ds[block(hd)], kcat), dq)
                    dq_ref[rr, here] = dq.astype(dq_ref.dtype)
                    dkc.append(_dot_tn(ds, qs))
                    dvc.append(_dot_tn(p.astype(BF16), dos))

                for out_ref, carry, parts in ((dk_ref, ck, dkc), (dv_ref, cv, dvc)):
                    full = [parts[qi][ATTN_BLOCK:] + parts[qi + 1][0:ATTN_BLOCK] for qi in range(qb - 1)]
                    if whole:
                        for qi, val in enumerate(full + [parts[qb - 1][ATTN_BLOCK:]]):
                            out_ref[rr, block(qi)] = val.astype(out_ref.dtype)
                        continue

                    @pl.when(j > 0)
                    def _():
                        if qb > 1:
                            out_ref[0, 0:(qb - 1) * ATTN_BLOCK] = carry[0:(qb - 1) * ATTN_BLOCK].astype(out_ref.dtype)
                        out_ref[0, tail] = (carry[tail] + parts[0][0:ATTN_BLOCK]).astype(out_ref.dtype)

                    for qi, val in enumerate(full):
                        carry[block(qi)] = val
                    carry[tail] = parts[qb - 1][ATTN_BLOCK:]

        if whole:
            compute()
        else:
            pl.when(j < nj)(compute)

            @pl.when(j == nj)
            def _():
                dk_ref[0] = ck[...].astype(dk_ref.dtype)
                dv_ref[0] = cv[...].astype(dv_ref.dtype)

    step = lambda j: jnp.minimum(j, nj - 1)
    cur = pl.BlockSpec((rs, qb * ATTN_BLOCK, GROUP_WIDTH), lambda r, j: (r, step(j), 0))
    prev = pl.BlockSpec((rs, ATTN_BLOCK, GROUP_WIDTH), lambda r, j: (r, jnp.maximum(qb * step(j) - 1, 0), 0))
    late = pl.BlockSpec((rs, qb * ATTN_BLOCK, GROUP_WIDTH), lambda r, j: (r, jnp.maximum(j - 1, 0), 0))
    return pl.pallas_call(
        _ordered_after(body, 8, after), name=name, grid=(dil // rs, 1 if whole else nj + 1),
        in_specs=[cur, prev, cur, prev, cur, cur, cur, cur] + [pl.BlockSpec(memory_space=pl.ANY)] * len(after),
        out_specs=[cur, cur if whole else late, cur if whole else late],
        out_shape=[jax.ShapeDtypeStruct(q.shape, BF16)] * 3,
        scratch_shapes=[pltpu.VMEM((qb * ATTN_BLOCK, GROUP_WIDTH), F32)] * 2,
        compiler_params=_params(2),
    )(q, k, k, v, v, do, lse, deff, *after)


def _pool_bwd(dpool, y, w_bd, scale, name, after=()):
    s = dpool.shape[0]
    t = _row_tile(s, 512)
    nt = s // t

    def body(dp_ref, y_ref, w_ref, sc_ref, du_ref, dw_ref, dsc_ref, ext, b2, b4, b8):
        i = pl.program_id(0)

        @pl.when(i == 0)
        def _():
            ext[t:, :] = jnp.zeros((POOL_HALO + POOL_PAD, POOL_WIDTH), F32)
            for buf in (b2, b4):
                buf[t + POOL_HALO:, :] = jnp.zeros((POOL_PAD, POOL_WIDTH), F32)
            dw_ref[...] = jnp.zeros_like(dw_ref)
            dsc_ref[...] = jnp.zeros_like(dsc_ref)

        dp = dp_ref[...]
        yb = y_ref[...]
        w = w_ref[...]
        dsc_ref[...] += jnp.sum(dp * _dot(yb, w), axis=0, keepdims=True)
        dyo = (dp * sc_ref[...]).astype(BF16)
        dw_ref[...] += _dot_tn(yb, dyo)
        dy = _dot_nt(dyo, w)
        win = _pool_lane_window()
        pos = (nt - 1 - i) * t + lax.broadcasted_iota(jnp.int32, (t, POOL_WIDTH), 0)
        gq = dy / jnp.minimum(pos + 1, win).astype(F32)
        ext[0:t, :] = gq
        du_ref[...] = _window_sums(ext, b2, b4, b8, t, 0, 0, 1) - dy
        ext[t:t + POOL_HALO, :] = gq[0:POOL_HALO, :]

    rev = pl.BlockSpec((t, POOL_WIDTH), lambda i: (nt - 1 - i, 0))
    full = lambda a, b: pl.BlockSpec((a, b), lambda i: (0, 0))
    return pl.pallas_call(
        _ordered_after(body, 4, after), name=name, grid=(nt,),
        in_specs=[rev, rev, full(POOL_WIDTH, POOL_WIDTH), full(1, POOL_WIDTH)]
        + [pl.BlockSpec(memory_space=pl.ANY)] * len(after),
        out_specs=[rev, full(POOL_WIDTH, POOL_WIDTH), full(1, POOL_WIDTH)],
        out_shape=[jax.ShapeDtypeStruct((s, POOL_WIDTH), F32), jax.ShapeDtypeStruct((POOL_WIDTH, POOL_WIDTH), F32),
                   jax.ShapeDtypeStruct((1, POOL_WIDTH), F32)],
        scratch_shapes=[pltpu.VMEM((t + POOL_HALO + POOL_PAD, POOL_WIDTH), F32)] * 4,
        compiler_params=_params(1),
    )(dpool, y, w_bd, scale, *after)


def _normproj_bwd(dh, du, dq, dk, dv, rc, rsa, rsb, w_in, h, g, name):
    s = h.shape[0]
    t = _row_tile(s, 512)

    def body(dh_ref, du_ref, q0, q1, q2, k0, k1, k2, v0, v1, v2, c_ref, sa_ref, sb_ref, w_ref, h_ref, g_ref,
             out_ref, dz_ref, dg_ref, *stages):
        @pl.when(pl.program_id(0) == 0)
        def _():
            dg_ref[...] = jnp.zeros_like(dg_ref)

        c, sa, sb = c_ref[...], sa_ref[...], sb_ref[...]

        def unrot(a, scale):
            halves = [_rot_t(a[:, hf * LANES:(hf + 1) * LANES] * scale, c, sa, sb) for hf in range(2)]
            return jnp.concatenate(halves, axis=1)

        staged = _pair_stages(stages)
        tok = lambda refs, base: [_from_residues(r, staged[base + i], DILATIONS[i]) for i, r in enumerate(refs)]
        chunks = [du_ref[...]]
        chunks += [unrot(a, HEAD_DIM ** -0.5) for a in tok((q0, q1, q2), 0)]
        chunks += [unrot(a, 1.0) for a in tok((k0, k1, k2), 3)]
        chunks += tok((v0, v1, v2), 6)
        acc = jnp.zeros((t, D_MODEL), F32)
        for ci, ch in enumerate(chunks):
            cols = slice(ci * GROUP_WIDTH, (ci + 1) * GROUP_WIDTH)
            cb = ch.astype(BF16)
            dz_ref[:, cols] = cb
            acc = acc + _dot_nt(cb, w_ref[:, cols])
        gv = g_ref[...]
        n, rstd, _ = _rms(h_ref[...], gv)
        dx, dg = _rms_bwd(acc, n, rstd, gv)
        out_ref[...] = dh_ref[...] + dx
        dg_ref[...] += dg

    row = lambda w: pl.BlockSpec((t, w), lambda i: (i, 0))
    vec = pl.BlockSpec((1, D_MODEL), lambda i: (0, 0))
    res = [_residue_spec(dil, t) for dil in DILATIONS]
    return pl.pallas_call(
        body, name=name, grid=(s // t,),
        in_specs=[row(D_MODEL), row(POOL_WIDTH)] + res * 3 + _table_specs(t)
        + [pl.BlockSpec((D_MODEL, N_IN), lambda i: (0, 0)), row(D_MODEL), vec],
        out_specs=[row(D_MODEL), row(N_IN), vec],
        out_shape=[jax.ShapeDtypeStruct((s, D_MODEL), F32), jax.ShapeDtypeStruct((s, N_IN), BF16),
                   jax.ShapeDtypeStruct((1, D_MODEL), F32)],
        scratch_shapes=_stages(t, 9),
        compiler_params=_params(1),
    )(dh, du, *dq, *dk, *dv, rc, rsa, rsb, w_in, h, g)


def _matmul_tn(a, b, name, *, square_a=False, tn=None, blocked_out=False, after=()):
    s, m = a.shape
    n = b.shape[1]
    tk = _row_tile(s, 2048)
    tm = min(m, 1024)
    tn = tn or min(n, 1024)
    assert m % tm == 0 and n % tn == 0
    nk = s // tk
    nsub = tn // FF_BLOCK if blocked_out else 1

    def body(a_ref, b_ref, o_ref, ob_ref, acc):
        k = pl.program_id(2)

        @pl.when(k == 0)
        def _():
            acc[...] = jnp.zeros_like(acc)

        av = a_ref[...]
        if square_a:
            av = av.astype(F32)
            av = av * av
        acc[...] += _dot_tn(av.astype(BF16), b_ref[...].astype(BF16))

        @pl.when(k == nk - 1)
        def _():
            if blocked_out:
                for sub in range(nsub):
                    cols = slice(sub * FF_BLOCK, (sub + 1) * FF_BLOCK)
                    o_ref[sub] = acc[:, cols]
                    ob_ref[sub] = acc[:, cols].astype(BF16)
            else:
                o_ref[...] = acc[...]
                ob_ref[...] = acc[...].astype(BF16)

    if blocked_out:
        shape = (n // FF_BLOCK, m, FF_BLOCK)
        out_spec = pl.BlockSpec((nsub, tm, FF_BLOCK), lambda i, j, k: (j, i, 0))
    else:
        shape = (m, n)
        out_spec = pl.BlockSpec((tm, tn), lambda i, j, k: (i, j))
    return pl.pallas_call(
        _ordered_after(body, 2, after), name=name, grid=(m // tm, n // tn, nk),
        in_specs=[pl.BlockSpec((tk, tm), lambda i, j, k: (k, i)), pl.BlockSpec((tk, tn), lambda i, j, k: (k, j))]
        + [pl.BlockSpec(memory_space=pl.ANY)] * len(after),
        out_specs=[out_spec, out_spec],
        out_shape=[jax.ShapeDtypeStruct(shape, F32), jax.ShapeDtypeStruct(shape, BF16)],
        scratch_shapes=[pltpu.VMEM((tm, tn), F32)],
        compiler_params=_params(3),
    )(a, b, *after)


def _adamw_math(w, g, m, v):
    m = ADAM_B1 * m + (1.0 - ADAM_B1) * g
    v = ADAM_B2 * v + (1.0 - ADAM_B2) * (g * g)
    m_hat = m / (1.0 - ADAM_B1 ** ADAM_STEP)
    v_hat = v / (1.0 - ADAM_B2 ** ADAM_STEP)
    delta = -ADAM_LR * (m_hat / (jnp.sqrt(v_hat) + ADAM_EPS) + ADAM_WD * w)
    return delta, m, v


def _adamw_sharded(w, m, v, own, recv0, recv1, name):
    _, rows, cols = w.shape
    t = _row_tile(rows, 256)

    def body(w_ref, m_ref, v_ref, own_ref, r0_ref, r1_ref, g_ref, d_ref, nm_ref, nv_ref):
        layer0 = pl.program_id(0) == 0
        g = own_ref[...]
        for k in range(N_DEV - 1):
            g = g + jnp.where(layer0, r0_ref[k], r1_ref[k]).astype(F32)
        g_ref[...] = g
        d_ref[...], nm_ref[...], nv_ref[...] = _adamw_math(w_ref[...], g, m_ref[...], v_ref[...])

    blk = pl.BlockSpec((None, t, cols), lambda l, i: (l, i, 0))
    recv = lambda layer: pl.BlockSpec((N_DEV - 1, t, cols), lambda l, i: (0, jnp.where(l == layer, i, 0), 0))
    return pl.pallas_call(
        body, name=name, grid=(2, rows // t),
        in_specs=[blk, blk, blk, blk, recv(0), recv(1)], out_specs=[blk] * 4,
        out_shape=[jax.ShapeDtypeStruct(w.shape, F32)] * 4,
        compiler_params=_params(2),
    )(w, m, v, own, recv0, recv1)


def _adamw_packed(w, g8, m, v, name):
    def body(w_ref, g_ref, m_ref, v_ref, go_ref, d_ref, nm_ref, nv_ref):
        g = g_ref[0]
        for dev in range(1, N_DEV):
            g = g + g_ref[dev]
        go_ref[...] = g
        d_ref[...], nm_ref[...], nv_ref[...] = _adamw_math(w_ref[...], g, m_ref[...], v_ref[...])

    return pl.pallas_call(
        body, name=name, out_shape=[jax.ShapeDtypeStruct(w.shape, F32)] * 4,
        compiler_params=pltpu.CompilerParams(vmem_limit_bytes=VMEM_LIMIT),
    )(w, g8, m, v)


def _peer(k):
    x, y, c = lax.axis_index("x"), lax.axis_index("y"), lax.axis_index("c")
    return (1 - x if k & 4 else x, 1 - y if k & 2 else y, 1 - c if k & 1 else c)


def _linear(dev):
    return 4 * dev[0] + 2 * dev[1] + dev[2]


HBM_SPEC = pl.BlockSpec(memory_space=pltpu.HBM)
SEM_SPEC = pl.BlockSpec(memory_space=pltpu.SEMAPHORE)
ANY_SPEC = pl.BlockSpec(memory_space=pl.ANY)
EFFECT = pltpu.SideEffectType.DATAFLOW_SIDE_EFFECTING


def _in_hbm(a):
    return pltpu.with_memory_space_constraint(a, pltpu.HBM)


class _Exchange:
    def __init__(self, name, groups, scatter, after=()):
        self.name, self.scatter = name, scatter
        self.sizes = sizes = [len(g) for g in groups]
        srcs = [a for g in groups for a in g]
        n, ng = len(srcs), len(groups)
        lead = (N_DEV - 1,) if scatter else (N_DEV,)
        shapes = [lead + (a.shape[1:] if scatter else a.shape) for a in srcs]
        lands = [lax.empty(sh, a.dtype) for sh, a in zip(shapes, srcs)]
        offsets = [sum(sizes[:gi]) for gi in range(ng)]
        copy = self._copy

        def body(*refs):
            src, land = refs[:n], refs[n:2 * n]
            sems = refs[2 * n + len(after):2 * n + len(after) + 2 * ng]
            token = refs[-1]
            for gi in range(ng):
                for wi in range(sizes[gi]):
                    w = offsets[gi] + wi
                    for k in range(1, N_DEV):
                        copy(src[w], land[w], sems[2 * gi], sems[2 * gi + 1], wi, k).start()
            token[...] = jnp.zeros_like(token)

        sem_shapes = [pltpu.SemaphoreType.DMA((7 * sz,)) for sz in sizes for _ in range(2)]
        outs = pl.pallas_call(
            body, name=name + "_start",
            in_specs=[HBM_SPEC] * (2 * n) + [ANY_SPEC] * len(after),
            out_specs=[SEM_SPEC] * (2 * ng) + [HBM_SPEC] * (2 * n) + [pl.BlockSpec(memory_space=pltpu.VMEM)],
            out_shape=sem_shapes + [pltpu.HBM(a.shape, a.dtype) for a in srcs + lands]
            + [jax.ShapeDtypeStruct((8, LANES), F32)],
            input_output_aliases={i: 2 * ng + i for i in range(2 * n)},
            compiler_params=pltpu.CompilerParams(has_side_effects=EFFECT),
        )(*[_in_hbm(a) for a in srcs + lands], *after)
        self.sems = [outs[2 * gi:2 * gi + 2] for gi in range(ng)]
        thru = outs[2 * ng:2 * ng + 2 * n]
        self.srcs = [thru[offsets[gi]:offsets[gi] + sizes[gi]] for gi in range(ng)]
        self.lands = [thru[n + offsets[gi]:n + offsets[gi] + sizes[gi]] for gi in range(ng)]
        self.token = outs[-1]

    def _copy(self, src, land, send_sems, recv_sems, wi, k):
        to = _peer(k)
        if self.scatter:
            src_ref, dst_ref = src.at[_linear(to)], land.at[k - 1]
        else:
            src_ref, dst_ref = src, land.at[_linear(_peer(0))]
        return pltpu.make_async_remote_copy(
            src_ref=src_ref, dst_ref=dst_ref, send_sem=send_sems.at[7 * wi + k - 1],
            recv_sem=recv_sems.at[7 * wi + k - 1], device_id=to, device_id_type=MESH)

    def wait(self, gi, after):
        n = self.sizes[gi]
        copy = self._copy

        def body(*refs):
            src, land = refs[:n], refs[n:2 * n]
            send_sems, recv_sems = refs[2 * n], refs[2 * n + 1]
            for wi in range(n):
                for k in range(1, N_DEV):
                    cp = copy(src[wi], land[wi], send_sems, recv_sems, wi, k)
                    cp.wait_send()
                    cp.wait_recv()

        arrays = list(self.srcs[gi]) + list(self.lands[gi])
        outs = pl.pallas_call(
            body, name=f"{self.name}_wait{gi}",
            in_specs=[HBM_SPEC] * (2 * n) + [SEM_SPEC, SEM_SPEC] + [ANY_SPEC] * len(after),
            out_specs=[HBM_SPEC] * (2 * n),
            out_shape=[pltpu.HBM(a.shape, a.dtype) for a in arrays],
            input_output_aliases={i: i for i in range(2 * n)},
            compiler_params=pltpu.CompilerParams(has_side_effects=EFFECT),
        )(*arrays, *self.sems[gi], *after)
        return outs[:n], outs[n:]


def _rotary_tables(positions):
    rot_dim = HEAD_DIM // 4
    inv_freq = ROPE_THETA ** (-jnp.arange(0, rot_dim, 2, dtype=F32) / rot_dim)
    ang = positions.astype(F32)[:, None] * inv_freq
    cs = jnp.concatenate([jnp.cos(ang), jnp.sin(ang)], axis=1)
    dim = jnp.arange(LANES) % HEAD_DIM
    first, second = dim < ROT_SHIFT, (dim >= ROT_SHIFT) & (dim < rot_dim)
    src = jnp.arange(2 * ROT_SHIFT)[:, None]
    angle = (dim % ROT_SHIFT)[None, :]
    c = jnp.where((first | second)[None, :] & (src == angle), 1.0, 0.0)
    sa = jnp.where(second[None, :] & (src == angle + ROT_SHIFT), 1.0, 0.0)
    sb = jnp.where(first[None, :] & (src == angle + ROT_SHIFT), -1.0, 0.0)
    spread = jnp.concatenate([c, sa, sb], axis=1).astype(F32)
    base = jnp.concatenate([jnp.where(first | second, 0.0, 1.0), jnp.zeros((2 * LANES,))]).astype(F32)[None, :]
    return jnp.dot(cs, spread, precision=lax.Precision.HIGHEST, preferred_element_type=F32) + base


def _block_diag(pool_w):
    gc = pool_w.shape[-1]
    out = jnp.zeros((POOL_WIDTH, POOL_WIDTH), pool_w.dtype)
    for grp in range(pool_w.shape[0]):
        out = lax.dynamic_update_slice(out, pool_w[grp], (grp * gc, grp * gc))
    return out


def _diag_blocks(a):
    gc = POOL_WIDTH // len(POOL_WINDOWS)
    return jnp.stack([a[grp * gc:(grp + 1) * gc, grp * gc:(grp + 1) * gc] for grp in range(len(POOL_WINDOWS))])


def _local_step(x, p, positions, loss_target, norm1, pool_w, pool_scale, norm2, norm3, final_norm, weights, send):
    rc = rsa = rsb = _rotary_tables(positions)
    ones_bd = _block_diag(jnp.ones((4, HEAD_DIM, HEAD_DIM), BF16))
    saved = []
    h = x
    for i in range(2):
        tag = f"_l{i}"
        g1, g2, g3 = norm1[i:i + 1], norm2[i:i + 1], norm3[i:i + 1]
        w_bd = _block_diag(pool_w[i]).astype(BF16)
        scale = pool_scale[i:i + 1]
        w_in = weights(i, "in", (h, rc, w_bd))
        hn1, u, *qkv = _normproj_fwd(h, g1, w_in, rc, rsa, rsb, "normproj_fwd" + tag)
        qkv = [qkv[3 * grp:3 * grp + 3] for grp in range(3)]
        started = weights(i, "prefetch", (hn1,))
        pool_out, y = _pool_fwd(u, w_bd, scale, "pool_fwd" + tag, after=started)
        o, lse = zip(*[_attn_fwd(*qkv[grp], f"attn_fwd{tag}_g{grp}", after=started) for grp in range(3)])
        w_out = weights(i, "out", (pool_out, *o))
        h1, a = _outproj_fwd(h, pool_out, o, lse, w_out, "outproj_fwd" + tag)
        w_up, w_down, w_gate, w_ple = weights(i, "rest", (h1,))
        h2, hn2, r = _mlp_fwd(h1, g2, w_up, w_down, "mlp_fwd" + tag)
        h3, hn3, gate, pb = _gate_fwd(h2, g3, w_gate, p, i, w_ple, "gate_fwd" + tag)
        saved.append(dict(h0=h, hn1=hn1, qkv=qkv, y=y, o=o, lse=lse, a=a, h1=h1, hn2=hn2, r=r, h2=h2,
                          hn3=hn3, gate=gate, pb=pb, w_bd=w_bd, scale=scale, g1=g1, g2=g2, g3=g3,
                          w_in=w_in, w_out=w_out, w_up=w_up, w_down=w_down, w_gate=w_gate, w_ple=w_ple))
        h = h3
    loss, dh, d_final = _loss_head(h, final_norm.reshape(1, D_MODEL), loss_target, "loss_head")

    grads = [None, None]
    sent = ()
    for i in (1, 0):
        tag = f"_l{i}"
        sv = saved[i]
        dh2, dgl, de, dg3 = _gate_bwd(dh, sv["gate"], sv["pb"], sv["w_ple"], sv["h2"], sv["g3"], sv["w_gate"],
                                      "gate_bwd" + tag, after=sent)
        dw_gate = _matmul_tn(sv["hn3"], dgl, "dw_gate" + tag)
        dw_ple = _matmul_tn(sv["pb"], de, "dw_ple" + tag)
        dh1, dup, dg2, dh2b = _mlp_bwd(dh2, sv["r"], sv["h1"], sv["g2"], sv["w_up"], sv["w_down"], "mlp_bwd" + tag)
        dw_down = _matmul_tn(sv["r"], dh2b, "dw_down" + tag, square_a=True)
        dw_up = _matmul_tn(sv["hn2"], dup, "dw_up" + tag, blocked_out=True)
        dpool, do0, do1, do2, de0, de1, de2, dh1b = _outproj_bwd(dh1, sv["w_out"], sv["o"], sv["lse"], ones_bd,
                                                                 "outproj_bwd" + tag)
        dw_out = _matmul_tn(sv["a"], dh1b, "dw_out" + tag)
        sent = send(i, "main", dict(w_gate=dw_gate, w_ple=dw_ple, w_down=dw_down, w_up=dw_up, w_out=dw_out))
        dqkv = [_attn_bwd(*sv["qkv"][grp], do_g, sv["lse"][grp], de_g, f"attn_bwd{tag}_g{grp}", after=sent)
                for grp, (do_g, de_g) in enumerate(((do0, de0), (do1, de1), (do2, de2)))]
        dq, dk, dv = zip(*dqkv)
        du, dw_bd, dscale = _pool_bwd(dpool, sv["y"], sv["w_bd"], sv["scale"], "pool_bwd" + tag, after=sent)
        dh, dz, dg1 = _normproj_bwd(dh1, du, dq, dk, dv, rc, rsa, rsb, sv["w_in"], sv["h0"], sv["g1"],
                                    "normproj_bwd" + tag)
        grads[i] = dict(norm1=dg1, norm2=dg2, norm3=dg3, pool_w=_diag_blocks(dw_bd), pool_scale=dscale)
        small_sent = send(0, "small", (grads, d_final, loss)) if i == 0 else ()
        dw_in = _matmul_tn(sv["hn1"], dz, "dw_in" + tag, tn=N_IN // 2, after=small_sent)
        sent = send(i, "in", dict(w_in=dw_in))
    return dh, sent


def _pack_small(norm1, norm2, norm3, final_norm, pool_scale, pool_w, spare=None):
    spare = jnp.zeros((1, LANES), F32) if spare is None else spare
    scale_row = jnp.concatenate([pool_scale.reshape(1, 2 * POOL_WIDTH), spare,
                                 jnp.zeros((1, D_MODEL - 2 * POOL_WIDTH - LANES), F32)], axis=1)
    return jnp.concatenate([norm1, norm2, norm3, final_norm.reshape(1, D_MODEL), scale_row,
                            pool_w.reshape(32, D_MODEL)], axis=0)


def _unpack_small(a):
    return dict(norm1=a[0:2], norm2=a[2:4], norm3=a[4:6], final_norm=a[6], pool_scale=a[7, 0:2 * POOL_WIDTH].reshape(2, POOL_WIDTH),
                pool_w=a[8:40].reshape(2, 4, HEAD_DIM, HEAD_DIM))


def _chunks_cols(a, cols):
    return a.reshape(a.shape[0], N_DEV, cols).transpose(1, 0, 2)


def _chunks_rows(a, rows):
    return a.reshape(N_DEV, rows, a.shape[1])


BIG = ("w_in", "w_out", "w_up", "w_down", "w_gate", "w_ple")
SMALL = ("norm1", "norm2", "norm3", "final_norm", "pool_scale", "pool_w")
ORDER = ("norm1", "w_in", "pool_w", "pool_scale", "w_out", "norm2", "w_up", "w_down", "norm3", "w_gate", "w_ple",
         "final_norm")


def kernel(x, p, positions, norm1, w_in, pool_w, pool_scale, w_out, norm2, w_up, w_down, norm3, w_gate, w_ple, final_norm, loss_target, m_norm1, m_w_in, m_pool_w, m_pool_scale, m_w_out, m_norm2, m_w_up, m_w_down, m_norm3, m_w_gate, m_w_ple, m_final_norm, v_norm1, v_w_in, v_pool_w, v_pool_scale, v_w_out, v_norm2, v_w_up, v_w_down, v_norm3, v_w_gate, v_w_ple, v_final_norm):
    w = dict(norm1=norm1, w_in=w_in, pool_w=pool_w, pool_scale=pool_scale, w_out=w_out, norm2=norm2, w_up=w_up,
             w_down=w_down, norm3=norm3, w_gate=w_gate, w_ple=w_ple, final_norm=final_norm)
    m = dict(norm1=m_norm1, w_in=m_w_in, pool_w=m_pool_w, pool_scale=m_pool_scale, w_out=m_w_out, norm2=m_norm2,
             w_up=m_w_up, w_down=m_w_down, norm3=m_norm3, w_gate=m_w_gate, w_ple=m_w_ple, final_norm=m_final_norm)
    v = dict(norm1=v_norm1, w_in=v_w_in, pool_w=v_pool_w, pool_scale=v_pool_scale, w_out=v_w_out, norm2=v_norm2,
             w_up=v_w_up, w_down=v_w_down, norm3=v_norm3, w_gate=v_w_gate, w_ple=v_w_ple, final_norm=v_final_norm)
    seq = x.shape[1]

    bf = {n: [w[n][layer].astype(BF16) for layer in range(2)] for n in BIG}
    rest = ("w_up", "w_down", "w_gate", "w_ple")
    me = 4 * lax.axis_index("x") + 2 * lax.axis_index("y") + lax.axis_index("c")
    gathers = [_Exchange("gather_l0", [[bf["w_in"][0]], [bf["w_out"][0]], [bf[n][0] for n in rest]], scatter=False)]
    unpack = dict(w_in=lambda a: a.transpose(1, 0, 2).reshape(D_MODEL, N_IN),
                  w_out=lambda a: a.reshape(D_MODEL, D_MODEL), w_gate=lambda a: a.reshape(D_MODEL, D_MODEL),
                  w_ple=lambda a: a.transpose(1, 0, 2).reshape(PLE_DIM, D_MODEL), w_up=lambda a: a, w_down=lambda a: a)
    parts = dict(zip(("in", "out", "rest"), (("w_in",), ("w_out",), rest)))

    def weights(layer, part, after):
        if part == "prefetch":
            if layer != 0:
                return ()
            gathers.append(_Exchange("gather_l1", [[bf[n][1] for n in parts[pt]] for pt in parts], scatter=False,
                                     after=after))
            return (gathers[1].token,)
        shards, lands = gathers[layer].wait(tuple(parts).index(part), after)
        full = [unpack[n](lax.dynamic_update_slice_in_dim(land, shard[None], me, axis=0))
                for n, shard, land in zip(parts[part], shards, lands)]
        return full if part == "rest" else full[0]

    to_chunks = dict(w_in=lambda a: _chunks_cols(a, N_IN // N_DEV), w_out=lambda a: _chunks_rows(a, D_MODEL // N_DEV),
                     w_up=lambda a: a, w_down=lambda a: _chunks_rows(a, FF_BLOCK),
                     w_gate=lambda a: _chunks_rows(a, D_MODEL // N_DEV), w_ple=lambda a: _chunks_cols(a, D_MODEL // N_DEV))
    own = {n: [None, None] for n in BIG}
    scatters = {}

    def own_chunk(n, g32):
        if n in ("w_in", "w_ple"):
            cols = g32.shape[1] // N_DEV
            return lax.dynamic_slice(g32, (0, me * cols), (g32.shape[0], cols))
        return lax.dynamic_index_in_dim(to_chunks[n](g32), me, axis=0, keepdims=False)

    def send(layer, part, grads):
        if part == "small":
            per_layer, d_final, loss = grads
            pack = _pack_small(
                *[jnp.concatenate([per_layer[0][n], per_layer[1][n]], axis=0) for n in ("norm1", "norm2", "norm3")],
                d_final.reshape(D_MODEL),
                jnp.concatenate([per_layer[0]["pool_scale"], per_layer[1]["pool_scale"]], axis=0),
                jnp.stack([per_layer[0]["pool_w"], per_layer[1]["pool_w"]]), spare=loss)
            scatters["small"] = _Exchange("gather_small", [[pack]], scatter=False)
            return (scatters["small"].token,)
        for n, (g32, _) in grads.items():
            own[n][layer] = own_chunk(n, g32)
        ex = _Exchange(f"scatter_{part}_l{layer}", [[to_chunks[n](g16) for n, (_, g16) in grads.items()]], scatter=True)
        scatters[layer, part] = (tuple(grads), ex)
        return (ex.token,)

    dx, sent = _local_step(
        x.reshape(seq, D_MODEL), p.reshape(2, seq, PLE_DIM), positions.reshape(seq), loss_target.reshape(seq, D_MODEL),
        norm1, pool_w, pool_scale, norm2, norm3, final_norm, weights, send)

    g_out, d_out, m_out, v_out = {}, {}, {}, {}
    for part in ("main", "in"):
        recv = {}
        for layer in (1, 0):
            names, ex = scatters[layer, part]
            for n, r in zip(names, ex.wait(0, sent)[1]):
                recv[n, layer] = r
        for n in names:
            g_out[n], d_out[n], m_out[n], v_out[n] = _adamw_sharded(
                w[n], m[n], v[n], jnp.stack(own[n]), recv[n, 0], recv[n, 1], "adamw_" + n)
        sent = tuple(d_out[n] for n in names)
    (mine,), (landed,) = scatters["small"].wait(0, sent)
    small_g8 = lax.dynamic_update_slice_in_dim(landed, mine[None], me, axis=0)
    pack = lambda t: _pack_small(*[t[n] for n in SMALL])
    small_g, d_small, m_small, v_small = _adamw_packed(pack(w), small_g8, pack(m), pack(v), "adamw_small")
    for dst, a in ((g_out, small_g), (d_out, d_small), (m_out, m_small), (v_out, v_small)):
        dst.update(_unpack_small(a))

    return (small_g[7, 2 * POOL_WIDTH],dx.reshape(1, seq, D_MODEL), *[g_out[n] for n in ORDER], *[d_out[n] for n in ORDER],
            *[m_out[n] for n in ORDER], *[v_out[n] for n in ORDER])
```

```python
import functools

import jax
import jax.numpy as jnp
from jax import lax
from jax.experimental import pallas as pl
from jax.experimental.pallas import tpu as pltpu

F32 = jnp.float32
BF16 = jnp.bfloat16

D_MODEL = 1024
HEAD_DIM = 64
POOL_WIDTH = 256
POOL_WINDOWS = (2, 4, 8, 16)
POOL_HALO = 16
POOL_PAD = 8
GROUP_WIDTH = 256
DILATIONS = (1, 4, 16)
ATTN_BLOCK = 128
ROT_SHIFT = 8
ROPE_THETA = 500000.0
D_FF = 4096
FF_BLOCK = 512
FF_PER_STEP = 2
MLP_BWD_TILE = 512
N_DEV = 8
N_IN = POOL_WIDTH + 3 * 768
PLE_DIM = 256
EPS = 1e-6
NEG_BIG = -1e30

ADAM_LR = 0.001
ADAM_B1 = 0.9
ADAM_B2 = 0.999
ADAM_EPS = 1e-08
ADAM_WD = 0.01
ADAM_STEP = 10

LANES = 128
VMEM_LIMIT = 56 * 1024 * 1024
MESH = pl.DeviceIdType.MESH


def _params(n_grid):
    return pltpu.CompilerParams(dimension_semantics=("arbitrary",) * n_grid, vmem_limit_bytes=VMEM_LIMIT)


def _dot(a, b):
    return jnp.dot(a, b, preferred_element_type=F32)


def _dot_nt(a, b):
    return lax.dot_general(a, b, (((1,), (1,)), ((), ())), preferred_element_type=F32)


def _dot_tn(a, b):
    return lax.dot_general(a, b, (((0,), (0,)), ((), ())), preferred_element_type=F32)


def _rms(x, g):
    rstd = lax.rsqrt(jnp.mean(x * x, axis=-1, keepdims=True) + EPS)
    n = x * rstd
    return n, rstd, n * g


def _rms_bwd(dy, n, rstd, g):
    dyn = dy * g
    dx = rstd * (dyn - n * jnp.mean(dyn * n, axis=-1, keepdims=True))
    return dx, jnp.sum(dy * n, axis=0, keepdims=True)


def _ordered_after(body, n_in, after):
    if not after:
        return body
    return lambda *refs: body(*refs[:n_in], *refs[n_in + len(after):])


def _row_tile(s, t):
    t = min(s, t)
    assert s % t == 0
    return t


def _rot(z, c, sa, sb):
    return z * c + pltpu.roll(z, ROT_SHIFT, 1) * sa + pltpu.roll(z, LANES - ROT_SHIFT, 1) * sb


def _table_specs(t):
    return [pl.BlockSpec((t, LANES), functools.partial(lambda i, k: (i, k), k=k)) for k in range(3)]


def _rot_t(dz, c, sa, sb):
    return dz * c + pltpu.roll(dz * sa, LANES - ROT_SHIFT, 1) + pltpu.roll(dz * sb, ROT_SHIFT, 1)


def _to_residues(value, stage, out_ref, dil):
    if dil == 1:
        out_ref[0] = value.astype(out_ref.dtype)
        return
    rows = value.shape[0] // dil
    for hf in range(GROUP_WIDTH // LANES):
        lanes = slice(hf * LANES, (hf + 1) * LANES)
        stage[hf][...] = value[:, lanes]
        for r in range(dil):
            out_ref[r, :, lanes] = stage[hf][pl.ds(r, rows, stride=dil), :].astype(out_ref.dtype)


def _from_residues(in_ref, stage, dil):
    if dil == 1:
        return in_ref[0].astype(F32)
    rows = in_ref.shape[1]
    for hf in range(GROUP_WIDTH // LANES):
        for r in range(dil):
            stage[hf][pl.ds(r, rows, stride=dil), :] = in_ref[r, :, hf * LANES:(hf + 1) * LANES].astype(F32)
    return jnp.concatenate([stage[0][...], stage[1][...]], axis=1)


def _residue_spec(dil, t):
    return pl.BlockSpec((dil, t // dil, GROUP_WIDTH), lambda i: (0, i, 0))


def _residue_shape(dil, s, dtype):
    return jax.ShapeDtypeStruct((dil, s // dil, GROUP_WIDTH), dtype)


def _stages(t, n):
    return [pltpu.VMEM((t, LANES), F32)] * (n * (GROUP_WIDTH // LANES))


def _pair_stages(refs):
    return [refs[i:i + 2] for i in range(0, len(refs), 2)]


def _normproj_fwd(h, g, w_in, rc, rsa, rsb, name):
    s = h.shape[0]
    t = _row_tile(s, 512)

    def body(h_ref, g_ref, w_ref, c_ref, sa_ref, sb_ref, hn_ref, u_ref, *rest):
        qkv_refs, stages = rest[:9], _pair_stages(rest[9:])
        _, _, hn = _rms(h_ref[...], g_ref[...])
        hb = hn.astype(BF16)
        hn_ref[...] = hb
        c, sa, sb = c_ref[...], sa_ref[...], sb_ref[...]

        def rot(z, scale):
            halves = [_rot(z[:, hf * LANES:(hf + 1) * LANES], c, sa, sb) * scale for hf in range(2)]
            return jnp.concatenate(halves, axis=1)

        u_ref[...] = _dot(hb, w_ref[:, 0:POOL_WIDTH])
        for grp, dil in enumerate(DILATIONS):
            lo = POOL_WIDTH + grp * GROUP_WIDTH
            q_ref, k_ref, v_ref = qkv_refs[3 * grp:3 * grp + 3]
            _to_residues(rot(_dot(hb, w_ref[:, lo:lo + GROUP_WIDTH]), HEAD_DIM ** -0.5), stages[0], q_ref, dil)
            _to_residues(rot(_dot(hb, w_ref[:, lo + 768:lo + 768 + GROUP_WIDTH]), 1.0), stages[1], k_ref, dil)
            _to_residues(_dot(hb, w_ref[:, lo + 1536:lo + 1536 + GROUP_WIDTH]), stages[2], v_ref, dil)

    row = lambda w: pl.BlockSpec((t, w), lambda i: (i, 0))
    return pl.pallas_call(
        body, name=name, grid=(s // t,),
        in_specs=[row(D_MODEL), pl.BlockSpec((1, D_MODEL), lambda i: (0, 0)),
                  pl.BlockSpec((D_MODEL, N_IN), lambda i: (0, 0))] + _table_specs(t),
        out_specs=[row(D_MODEL), row(POOL_WIDTH)] + [_residue_spec(dil, t) for dil in DILATIONS for _ in range(3)],
        out_shape=[jax.ShapeDtypeStruct((s, D_MODEL), BF16), jax.ShapeDtypeStruct((s, POOL_WIDTH), F32)]
        + [_residue_shape(dil, s, BF16) for dil in DILATIONS for _ in range(3)],
        scratch_shapes=_stages(t, 3),
        compiler_params=_params(1),
    )(h, g, w_in, rc, rsa, rsb)


def _pool_lane_window():
    lane = lax.broadcasted_iota(jnp.int32, (1, POOL_WIDTH), 1)
    return jnp.left_shift(2, lane // (POOL_WIDTH // len(POOL_WINDOWS)))


def _window_sums(ext, b2, b4, b8, t, lo, tile, direction):
    rows = t + POOL_HALO
    for src, dst, sh in ((ext, b2, 1), (b2, b4, 2), (b4, b8, 4)):
        dst[lo:lo + rows, :] = src[lo:lo + rows, :] + src[lo + direction * sh:lo + direction * sh + rows, :]
    s16 = b8[tile:tile + t, :] + b8[tile + direction * 8:tile + direction * 8 + t, :]
    win = _pool_lane_window()
    return jnp.where(win == 2, b2[tile:tile + t, :],
                     jnp.where(win == 4, b4[tile:tile + t, :], jnp.where(win == 8, b8[tile:tile + t, :], s16)))


def _pool_fwd(u, w_bd, scale, name, after=()):
    s = u.shape[0]
    t = _row_tile(s, 512)
    first = POOL_PAD + POOL_HALO

    def body(u_ref, w_ref, sc_ref, out_ref, y_ref, ext, b2, b4, b8):
        i = pl.program_id(0)

        @pl.when(i == 0)
        def _():
            for buf in (ext, b2, b4):
                buf[0:POOL_PAD, :] = jnp.zeros((POOL_PAD, POOL_WIDTH), F32)
            ext[POOL_PAD:first, :] = jnp.zeros((POOL_HALO, POOL_WIDTH), F32)

        x = u_ref[...]
        ext[first:, :] = x
        wsum = _window_sums(ext, b2, b4, b8, t, POOL_PAD, first, -1)
        pos = i * t + lax.broadcasted_iota(jnp.int32, (t, POOL_WIDTH), 0)
        cnt = jnp.minimum(pos + 1, _pool_lane_window()).astype(F32)
        y = wsum / cnt - x
        yb = y.astype(BF16)
        y_ref[...] = yb
        out_ref[...] = _dot(yb, w_ref[...]) * sc_ref[...]
        ext[POOL_PAD:first, :] = x[t - POOL_HALO:, :]

    row = pl.BlockSpec((t, POOL_WIDTH), lambda i: (i, 0))
    return pl.pallas_call(
        _ordered_after(body, 3, after), name=name, grid=(s // t,),
        in_specs=[row, pl.BlockSpec((POOL_WIDTH, POOL_WIDTH), lambda i: (0, 0)),
                  pl.BlockSpec((1, POOL_WIDTH), lambda i: (0, 0))] + [pl.BlockSpec(memory_space=pl.ANY)] * len(after),
        out_specs=[row, row],
        out_shape=[jax.ShapeDtypeStruct((s, POOL_WIDTH), F32), jax.ShapeDtypeStruct((s, POOL_WIDTH), BF16)],
        scratch_shapes=[pltpu.VMEM((t + POOL_HALO + POOL_PAD, POOL_WIDTH), F32)] * 4,
        compiler_params=_params(1),
    )(u, w_bd, scale, *after)


def _head_masks():
    lane = lax.broadcasted_iota(jnp.int32, (ATTN_BLOCK, GROUP_WIDTH), 1)
    return [lane // HEAD_DIM == hd for hd in range(GROUP_WIDTH // HEAD_DIM)]


def _stack_heads(a, masks):
    zero = jnp.zeros_like(a)
    return jnp.concatenate([jnp.where(m, a, zero) for m in masks], axis=0)


def _band_bias(first_step):
    rows = ATTN_BLOCK * (GROUP_WIDTH // HEAD_DIM)
    i = lax.broadcasted_iota(jnp.int32, (rows, 2 * ATTN_BLOCK), 0) & (ATTN_BLOCK - 1)
    j = lax.broadcasted_iota(jnp.int32, (rows, 2 * ATTN_BLOCK), 1)
    inner = jnp.where((j >= i) & (j <= i + ATTN_BLOCK), 0.0, NEG_BIG)
    return jnp.where((j < ATTN_BLOCK) & first_step, NEG_BIG, inner), inner


def _column_per_head(a):
    return jnp.concatenate([a[:, hd * HEAD_DIM:hd * HEAD_DIM + 1] for hd in range(GROUP_WIDTH // HEAD_DIM)], axis=0)


def _blocks_per_step(nb):
    return 8 if nb % 8 == 0 else 4 if nb % 4 == 0 else 2 if nb % 2 == 0 else 1


def _residues_per_step(dil, nb, qb):
    return 2 if (nb == qb and qb < 8 and dil % 2 == 0) else 1


def _attn_fwd(q, k, v, name, after=()):
    dil, length, _ = q.shape
    nb = length // ATTN_BLOCK
    qb = _blocks_per_step(nb)
    rs = _residues_per_step(dil, nb, qb)

    def body(q_ref, kp_ref, kc_ref, vp_ref, vc_ref, o_ref, lse_ref):
        masks = _head_masks()
        bias = _band_bias(pl.program_id(1) == 0)
        for rr in range(rs):
            for qi in range(qb):
                here = slice(qi * ATTN_BLOCK, (qi + 1) * ATTN_BLOCK)
                before = slice((qi - 1) * ATTN_BLOCK, qi * ATTN_BLOCK)
                kcat = jnp.concatenate([kp_ref[rr] if qi == 0 else kc_ref[rr, before], kc_ref[rr, here]], axis=0)
                vcat = jnp.concatenate([vp_ref[rr] if qi == 0 else vc_ref[rr, before], vc_ref[rr, here]], axis=0)
                qs = _stack_heads(q_ref[rr, here], masks)
                sc = _dot_nt(qs, kcat) + bias[min(qi, 1)]
                m = jnp.max(sc, axis=1, keepdims=True)
                e = jnp.exp(sc - m)
                l = jnp.sum(e, axis=1, keepdims=True)
                p = (e / l).astype(BF16)
                lse = m + jnp.log(l)
                o = jnp.zeros((ATTN_BLOCK, GROUP_WIDTH), F32)
                lse_full = jnp.zeros((ATTN_BLOCK, GROUP_WIDTH), F32)
                for hd, msk in enumerate(masks):
                    rows = slice(hd * ATTN_BLOCK, (hd + 1) * ATTN_BLOCK)
                    o = jnp.where(msk, _dot(p[rows], vcat), o)
                    lse_full = jnp.where(msk, lse[rows], lse_full)
                o_ref[rr, here] = o.astype(o_ref.dtype)
                lse_ref[rr, here] = lse_full

    cur = pl.BlockSpec((rs, qb * ATTN_BLOCK, GROUP_WIDTH), lambda r, j: (r, j, 0))
    prev = pl.BlockSpec((rs, ATTN_BLOCK, GROUP_WIDTH), lambda r, j: (r, jnp.maximum(qb * j - 1, 0), 0))
    return pl.pallas_call(
        _ordered_after(body, 5, after), name=name, grid=(dil // rs, nb // qb),
        in_specs=[cur, prev, cur, prev, cur] + [pl.BlockSpec(memory_space=pl.ANY)] * len(after), out_specs=[cur, cur],
        out_shape=[jax.ShapeDtypeStruct(q.shape, BF16), jax.ShapeDtypeStruct(q.shape, F32)],
        compiler_params=_params(2),
    )(q, k, k, v, v, *after)


def _group_weights(l0, l1, l2):
    m = jnp.maximum(jnp.maximum(l0, l1), l2)
    e0, e1, e2 = jnp.exp(l0 - m), jnp.exp(l1 - m), jnp.exp(l2 - m)
    den = e0 + e1 + e2
    return e0 / den, e1 / den, e2 / den


def _outproj_fwd(h, pool_out, o, lse, w_out, name):
    s = h.shape[0]
    t = _row_tile(s, 512)

    def body(h_ref, po_ref, o0, o1, o2, l0, l1, l2, w_ref, out_ref, a_ref, *stages):
        stages = _pair_stages(stages)
        ov =[_from_residues(r, stages[i], DILATIONS[i]) for i, r in enumerate((o0, o1, o2))]
        lv = [_from_residues(r, stages[3 + i], DILATIONS[i]) for i, r in enumerate((l0, l1, l2))]
        wts = _group_weights(*lv)
        a = jnp.concatenate([po_ref[...]] + [ov[i] * wts[i] for i in range(3)], axis=1).astype(BF16)
        a_ref[...] = a
        out_ref[...] = h_ref[...] + _dot(a, w_ref[...])

    row = lambda w: pl.BlockSpec((t, w), lambda i: (i, 0))
    res = [_residue_spec(dil, t) for dil in DILATIONS]
    return pl.pallas_call(
        body, name=name, grid=(s // t,),
        in_specs=[row(D_MODEL), row(POOL_WIDTH)] + res + res + [pl.BlockSpec((D_MODEL, D_MODEL), lambda i: (0, 0))],
        out_specs=[row(D_MODEL), row(D_MODEL)],
        out_shape=[jax.ShapeDtypeStruct((s, D_MODEL), F32), jax.ShapeDtypeStruct((s, D_MODEL), BF16)],
        scratch_shapes=_stages(t, 6),
        compiler_params=_params(1),
    )(h, pool_out, *o, *lse, w_out)


def _mlp_fwd(h, g, w_up, w_down, name):
    s = h.shape[0]
    t = _row_tile(s, 512)
    nblk = D_FF // FF_BLOCK

    def body(h_ref, g_ref, wu_ref, wd_ref, out_ref, hn_ref, r_ref):
        x = h_ref[...]
        _, _, hn = _rms(x, g_ref[...])
        hb = hn.astype(BF16)
        hn_ref[...] = hb
        acc = None
        for b0 in range(0, nblk, FF_PER_STEP):
            acts = []
            for b in range(b0, b0 + FF_PER_STEP):
                r = jnp.maximum(_dot(hb, wu_ref[b]), 0.0)
                r_ref[:, b * FF_BLOCK:(b + 1) * FF_BLOCK] = r.astype(BF16)
                acts.append((r * r).astype(BF16))
            wd = wd_ref[b0:b0 + FF_PER_STEP].reshape(FF_PER_STEP * FF_BLOCK, D_MODEL)
            part = _dot(jnp.concatenate(acts, axis=1), wd)
            acc = part if acc is None else acc + part
        out_ref[...] = x + acc

    row = lambda w: pl.BlockSpec((t, w), lambda i: (i, 0))
    resident = lambda shape: pl.BlockSpec(shape, lambda i: (0, 0, 0), pipeline_mode=pl.Buffered(1))
    return pl.pallas_call(
        body, name=name, grid=(s // t,),
        in_specs=[row(D_MODEL), pl.BlockSpec((1, D_MODEL), lambda i: (0, 0)),
                  resident((nblk, D_MODEL, FF_BLOCK)), resident((nblk, FF_BLOCK, D_MODEL))],
        out_specs=[row(D_MODEL), row(D_MODEL), row(D_FF)],
        out_shape=[jax.ShapeDtypeStruct((s, D_MODEL), F32), jax.ShapeDtypeStruct((s, D_MODEL), BF16),
                   jax.ShapeDtypeStruct((s, D_FF), BF16)],
        compiler_params=_params(1),
    )(h, g, w_up, w_down)


def _gate_fwd(h, g, w_gate, p, layer, w_ple, name):
    s = h.shape[0]
    t = _row_tile(s, 512)

    def body(h_ref, g_ref, wg_ref, p_ref, wp_ref, out_ref, hn_ref, gate_ref, pb_ref):
        x = h_ref[...]
        _, _, hn = _rms(x, g_ref[...])
        hb = hn.astype(BF16)
        hn_ref[...] = hb
        gate = 1.0 / (1.0 + jnp.exp(-_dot(hb, wg_ref[...])))
        pb = p_ref[...].astype(BF16)
        pb_ref[...] = pb
        gate_ref[...] = gate.astype(BF16)
        out_ref[...] = x + gate * _dot(pb, wp_ref[...])

    row = lambda w: pl.BlockSpec((t, w), lambda i: (i, 0))
    full = lambda a, b: pl.BlockSpec((a, b), lambda i: (0, 0))
    return pl.pallas_call(
        body, name=name, grid=(s // t,),
        in_specs=[row(D_MODEL), full(1, D_MODEL), full(D_MODEL, D_MODEL),
                  pl.BlockSpec((None, t, PLE_DIM), lambda i: (layer, i, 0)), full(PLE_DIM, D_MODEL)],
        out_specs=[row(D_MODEL), row(D_MODEL), row(D_MODEL), row(PLE_DIM)],
        out_shape=[jax.ShapeDtypeStruct((s, D_MODEL), F32), jax.ShapeDtypeStruct((s, D_MODEL), BF16),
                   jax.ShapeDtypeStruct((s, D_MODEL), BF16), jax.ShapeDtypeStruct((s, PLE_DIM), BF16)],
        compiler_params=_params(1),
    )(h, g, w_gate, p, w_ple)


def _loss_head(h, g, target, name):
    s = h.shape[0]
    t = _row_tile(s, 512)

    def body(h_ref, g_ref, t_ref, loss_ref, dh_ref, dg_ref):
        i = pl.program_id(0)

        @pl.when(i == 0)
        def _():
            loss_ref[...] = jnp.zeros_like(loss_ref)
            dg_ref[...] = jnp.zeros_like(dg_ref)

        gv = g_ref[...]
        n, rstd, y = _rms(h_ref[...], gv)
        err = y - t_ref[...]
        loss_ref[...] += jnp.sum(err * err) * (0.5 / D_MODEL)
        dx, dg = _rms_bwd(err * (1.0 / D_MODEL), n, rstd, gv)
        dh_ref[...] = dx
        dg_ref[...] += dg

    row = pl.BlockSpec((t, D_MODEL), lambda i: (i, 0))
    vec = pl.BlockSpec((1, D_MODEL), lambda i: (0, 0))
    return pl.pallas_call(
        body, name=name, grid=(s // t,),
        in_specs=[row, vec, row],
        out_specs=[pl.BlockSpec((1, LANES), lambda i: (0, 0)), row, vec],
        out_shape=[jax.ShapeDtypeStruct((1, LANES), F32), jax.ShapeDtypeStruct((s, D_MODEL), F32),
                   jax.ShapeDtypeStruct((1, D_MODEL), F32)],
        compiler_params=_params(1),
    )(h, g, target)


def _gate_bwd(dh, gate, pb, w_ple, h, g, w_gate, name, after=()):
    s = h.shape[0]
    t = _row_tile(s, 512)

    def body(dh_ref, gate_ref, pb_ref, wp_ref, h_ref, g_ref, wg_ref, out_ref, dgl_ref, de_ref, dg_ref):
        @pl.when(pl.program_id(0) == 0)
        def _():
            dg_ref[...] = jnp.zeros_like(dg_ref)

        d = dh_ref[...]
        gate = gate_ref[...].astype(F32)
        e = _dot(pb_ref[...], wp_ref[...])
        dgl = (d * e * gate * (1.0 - gate)).astype(BF16)
        dgl_ref[...] = dgl
        de_ref[...] = (d * gate).astype(BF16)
        gv = g_ref[...]
        n, rstd, _ = _rms(h_ref[...], gv)
        dx, dg = _rms_bwd(_dot_nt(dgl, wg_ref[...]), n, rstd, gv)
        out_ref[...] = d + dx
        dg_ref[...] += dg

    row = lambda w: pl.BlockSpec((t, w), lambda i: (i, 0))
    full = lambda a, b: pl.BlockSpec((a, b), lambda i: (0, 0))
    return pl.pallas_call(
        _ordered_after(body, 7, after), name=name, grid=(s // t,),
        in_specs=[row(D_MODEL), row(D_MODEL), row(PLE_DIM), full(PLE_DIM, D_MODEL), row(D_MODEL), full(1, D_MODEL),
                  full(D_MODEL, D_MODEL)] + [pl.BlockSpec(memory_space=pl.ANY)] * len(after),
        out_specs=[row(D_MODEL), row(D_MODEL), row(D_MODEL), full(1, D_MODEL)],
        out_shape=[jax.ShapeDtypeStruct((s, D_MODEL), F32), jax.ShapeDtypeStruct((s, D_MODEL), BF16),
                   jax.ShapeDtypeStruct((s, D_MODEL), BF16), jax.ShapeDtypeStruct((1, D_MODEL), F32)],
        compiler_params=_params(1),
    )(dh, gate, pb, w_ple, h, g, w_gate, *after)


def _mlp_bwd(dh, r, h, g, w_up, w_down, name):
    s = h.shape[0]
    t = _row_tile(s, MLP_BWD_TILE)
    nblk = D_FF // FF_BLOCK

    def body(dh_ref, r_ref, h_ref, g_ref, wu_ref, wd_ref, out_ref, dup_ref, dg_ref, dhb_ref):
        @pl.when(pl.program_id(0) == 0)
        def _():
            dg_ref[...] = jnp.zeros_like(dg_ref)

        d = dh_ref[...]
        db = d.astype(BF16)
        dhb_ref[...] = db
        back = None
        for b in range(nblk):
            cols = slice(b * FF_BLOCK, (b + 1) * FF_BLOCK)
            dup = (_dot_nt(db, wd_ref[b]) * (2.0 * r_ref[:, cols].astype(F32))).astype(BF16)
            dup_ref[:, cols] = dup
            part = _dot_nt(dup, wu_ref[b])
            back = part if back is None else back + part
        gv = g_ref[...]
        n, rstd, _ = _rms(h_ref[...], gv)
        dx, dg = _rms_bwd(back, n, rstd, gv)
        out_ref[...] = d + dx
        dg_ref[...] += dg

    row = lambda w: pl.BlockSpec((t, w), lambda i: (i, 0))
    vec = pl.BlockSpec((1, D_MODEL), lambda i: (0, 0))
    resident = lambda shape: pl.BlockSpec(shape, lambda i: (0, 0, 0), pipeline_mode=pl.Buffered(1))
    return pl.pallas_call(
        body, name=name, grid=(s // t,),
        in_specs=[row(D_MODEL), row(D_FF), row(D_MODEL), vec,
                  resident((nblk, D_MODEL, FF_BLOCK)), resident((nblk, FF_BLOCK, D_MODEL))],
        out_specs=[row(D_MODEL), row(D_FF), vec, row(D_MODEL)],
        out_shape=[jax.ShapeDtypeStruct((s, D_MODEL), F32), jax.ShapeDtypeStruct((s, D_FF), BF16),
                   jax.ShapeDtypeStruct((1, D_MODEL), F32), jax.ShapeDtypeStruct((s, D_MODEL), BF16)],
        compiler_params=_params(1),
    )(dh, r, h, g, w_up, w_down)


def _outproj_bwd(dh, w_out, o, lse, ones_bd, name):
    s = dh.shape[0]
    t = _row_tile(s, 512)

    def body(dh_ref, w_ref, o0, o1, o2, l0, l1, l2, bd_ref, dp_ref, do0, do1, do2, de0, de1, de2, dhb_ref, *stages):
        stages = _pair_stages(stages)
        dhb = dh_ref[...].astype(BF16)
        dhb_ref[...] = dhb
        da = _dot_nt(dhb, w_ref[...])
        dp_ref[...] = da[:, 0:POOL_WIDTH]
        ov =[_from_residues(r, stages[i], DILATIONS[i]) for i, r in enumerate((o0, o1, o2))]
        lv = [_from_residues(r, stages[3 + i], DILATIONS[i]) for i, r in enumerate((l0, l1, l2))]
        wts = _group_weights(*lv)
        bd = bd_ref[...]
        cbar = jnp.zeros((t, GROUP_WIDTH), F32)
        for grp, do_ref in enumerate((do0, do1, do2)):
            lo = POOL_WIDTH + grp * GROUP_WIDTH
            dag = da[:, lo:lo + GROUP_WIDTH]
            _to_residues(dag * wts[grp], stages[6 + grp], do_ref, DILATIONS[grp])
            prod = dag * ov[grp]
            hi = prod.astype(BF16)
            low = (prod - hi.astype(F32)).astype(BF16)
            cbar = cbar + wts[grp] * (_dot(hi, bd) + _dot(low, bd))
        for grp, de_ref in enumerate((de0, de1, de2)):
            _to_residues(wts[grp] * cbar, stages[9 + grp], de_ref, DILATIONS[grp])

    row = lambda w: pl.BlockSpec((t, w), lambda i: (i, 0))
    full = lambda a, b: pl.BlockSpec((a, b), lambda i: (0, 0))
    res = [_residue_spec(dil, t) for dil in DILATIONS]
    return pl.pallas_call(
        body, name=name, grid=(s // t,),
        in_specs=[row(D_MODEL), full(D_MODEL, D_MODEL)] + res + res + [full(GROUP_WIDTH, GROUP_WIDTH)],
        out_specs=[row(POOL_WIDTH)] + res + res + [row(D_MODEL)],
        out_shape=[jax.ShapeDtypeStruct((s, POOL_WIDTH), F32)] + [_residue_shape(dil, s, BF16) for dil in DILATIONS]
        + [_residue_shape(dil, s, F32) for dil in DILATIONS] + [jax.ShapeDtypeStruct((s, D_MODEL), BF16)],
        scratch_shapes=_stages(t, 12),
        compiler_params=_params(1),
    )(dh, w_out, *o, *lse, ones_bd)


def _attn_bwd(q, k, v, do, lse, deff, name, after=()):
    dil, length, _ = q.shape
    nb = length // ATTN_BLOCK
    qb = _blocks_per_step(nb)
    nj = nb // qb
    rs = _residues_per_step(dil, nb, qb)
    whole = nj == 1
    tail = slice((qb - 1) * ATTN_BLOCK, qb * ATTN_BLOCK)
    block = lambda qi: slice(qi * ATTN_BLOCK, (qi + 1) * ATTN_BLOCK)

    def body(q_ref, kp_ref, kc_ref, vp_ref, vc_ref, do_ref, lse_ref, de_ref, dq_ref, dk_ref, dv_ref, ck, cv):
        j = pl.program_id(1)

        def compute():
            masks = _head_masks()
            bias = _band_bias(j == 0)
            for rr in range(rs):
                dkc, dvc = [], []
                for qi in range(qb):
                    here, before = block(qi), block(qi - 1)
                    kcat = jnp.concatenate([kp_ref[rr] if qi == 0 else kc_ref[rr, before], kc_ref[rr, here]], axis=0)
                    vcat = jnp.concatenate([vp_ref[rr] if qi == 0 else vc_ref[rr, before], vc_ref[rr, here]], axis=0)
                    qs = _stack_heads(q_ref[rr, here], masks)
                    dos = _stack_heads(do_ref[rr, here], masks)
                    sc = _dot_nt(qs, kcat) + bias[min(qi, 1)]
                    p = jnp.exp(sc - _column_per_head(lse_ref[rr, here]))
                    ds = (p * (_dot_nt(dos, vcat) - _column_per_head(de_ref[rr, here]))).astype(BF16)
                    dq = jnp.zeros((ATTN_BLOCK, GROUP_WIDTH), F32)
                    for hd, msk in enumerate(masks):
                        dq = jnp.where(msk, _dot(ds[block(hd)], kcat), dq)
                    dq_ref[rr, here] = dq.astype(dq_ref.dtype)
                    dkc.append(_dot_tn(ds, qs))
                    dvc.append(_dot_tn(p.astype(BF16), dos))

                for out_ref, carry, parts in ((dk_ref, ck, dkc), (dv_ref, cv, dvc)):
                    full = [parts[qi][ATTN_BLOCK:] + parts[qi + 1][0:ATTN_BLOCK] for qi in range(qb - 1)]
                    if whole:
                        for qi, val in enumerate(full + [parts[qb - 1][ATTN_BLOCK:]]):
                            out_ref[rr, block(qi)] = val.astype(out_ref.dtype)
                        continue

                    @pl.when(j > 0)
                    def _():
                        if qb > 1:
                            out_ref[0, 0:(qb - 1) * ATTN_BLOCK] = carry[0:(qb - 1) * ATTN_BLOCK].astype(out_ref.dtype)
                        out_ref[0, tail] = (carry[tail] + parts[0][0:ATTN_BLOCK]).astype(out_ref.dtype)

                    for qi, val in enumerate(full):
                        carry[block(qi)] = val
                    carry[tail] = parts[qb - 1][ATTN_BLOCK:]

        if whole:
            compute()
        else:
            pl.when(j < nj)(compute)

            @pl.when(j == nj)
            def _():
                dk_ref[0] = ck[...].astype(dk_ref.dtype)
                dv_ref[0] = cv[...].astype(dv_ref.dtype)

    step = lambda j: jnp.minimum(j, nj - 1)
    cur = pl.BlockSpec((rs, qb * ATTN_BLOCK, GROUP_WIDTH), lambda r, j: (r, step(j), 0))
    prev = pl.BlockSpec((rs, ATTN_BLOCK, GROUP_WIDTH), lambda r, j: (r, jnp.maximum(qb * step(j) - 1, 0), 0))
    late = pl.BlockSpec((rs, qb * ATTN_BLOCK, GROUP_WIDTH), lambda r, j: (r, jnp.maximum(j - 1, 0), 0))
    return pl.pallas_call(
        _ordered_after(body, 8, after), name=name, grid=(dil // rs, 1 if whole else nj + 1),
        in_specs=[cur, prev, cur, prev, cur, cur, cur, cur] + [pl.BlockSpec(memory_space=pl.ANY)] * len(after),
        out_specs=[cur, cur if whole else late, cur if whole else late],
        out_shape=[jax.ShapeDtypeStruct(q.shape, BF16)] * 3,
        scratch_shapes=[pltpu.VMEM((qb * ATTN_BLOCK, GROUP_WIDTH), F32)] * 2,
        compiler_params=_params(2),
    )(q, k, k, v, v, do, lse, deff, *after)


def _pool_bwd(dpool, y, w_bd, scale, name, after=()):
    s = dpool.shape[0]
    t = _row_tile(s, 512)
    nt = s // t

    def body(dp_ref, y_ref, w_ref, sc_ref, du_ref, dw_ref, dsc_ref, ext, b2, b4, b8):
        i = pl.program_id(0)

        @pl.when(i == 0)
        def _():
            ext[t:, :] = jnp.zeros((POOL_HALO + POOL_PAD, POOL_WIDTH), F32)
            for buf in (b2, b4):
                buf[t + POOL_HALO:, :] = jnp.zeros((POOL_PAD, POOL_WIDTH), F32)
            dw_ref[...] = jnp.zeros_like(dw_ref)
            dsc_ref[...] = jnp.zeros_like(dsc_ref)

        dp = dp_ref[...]
        yb = y_ref[...]
        w = w_ref[...]
        dsc_ref[...] += jnp.sum(dp * _dot(yb, w), axis=0, keepdims=True)
        dyo = (dp * sc_ref[...]).astype(BF16)
        dw_ref[...] += _dot_tn(yb, dyo)
        dy = _dot_nt(dyo, w)
        win = _pool_lane_window()
        pos = (nt - 1 - i) * t + lax.broadcasted_iota(jnp.int32, (t, POOL_WIDTH), 0)
        gq = dy / jnp.minimum(pos + 1, win).astype(F32)
        ext[0:t, :] = gq
        du_ref[...] = _window_sums(ext, b2, b4, b8, t, 0, 0, 1) - dy
        ext[t:t + POOL_HALO, :] = gq[0:POOL_HALO, :]

    rev = pl.BlockSpec((t, POOL_WIDTH), lambda i: (nt - 1 - i, 0))
    full = lambda a, b: pl.BlockSpec((a, b), lambda i: (0, 0))
    return pl.pallas_call(
        _ordered_after(body, 4, after), name=name, grid=(nt,),
        in_specs=[rev, rev, full(POOL_WIDTH, POOL_WIDTH), full(1, POOL_WIDTH)]
        + [pl.BlockSpec(memory_space=pl.ANY)] * len(after),
        out_specs=[rev, full(POOL_WIDTH, POOL_WIDTH), full(1, POOL_WIDTH)],
        out_shape=[jax.ShapeDtypeStruct((s, POOL_WIDTH), F32), jax.ShapeDtypeStruct((POOL_WIDTH, POOL_WIDTH), F32),
                   jax.ShapeDtypeStruct((1, POOL_WIDTH), F32)],
        scratch_shapes=[pltpu.VMEM((t + POOL_HALO + POOL_PAD, POOL_WIDTH), F32)] * 4,
        compiler_params=_params(1),
    )(dpool, y, w_bd, scale, *after)


def _normproj_bwd(dh, du, dq, dk, dv, rc, rsa, rsb, w_in, h, g, name):
    s = h.shape[0]
    t = _row_tile(s, 512)

    def body(dh_ref, du_ref, q0, q1, q2, k0, k1, k2, v0, v1, v2, c_ref, sa_ref, sb_ref, w_ref, h_ref, g_ref,
             out_ref, dz_ref, dg_ref, *stages):
        @pl.when(pl.program_id(0) == 0)
        def _():
            dg_ref[...] = jnp.zeros_like(dg_ref)

        c, sa, sb = c_ref[...], sa_ref[...], sb_ref[...]

        def unrot(a, scale):
            halves = [_rot_t(a[:, hf * LANES:(hf + 1) * LANES] * scale, c, sa, sb) for hf in range(2)]
            return jnp.concatenate(halves, axis=1)

        staged = _pair_stages(stages)
        tok = lambda refs, base: [_from_residues(r, staged[base + i], DILATIONS[i]) for i, r in enumerate(refs)]
        chunks = [du_ref[...]]
        chunks += [unrot(a, HEAD_DIM ** -0.5) for a in tok((q0, q1, q2), 0)]
        chunks += [unrot(a, 1.0) for a in tok((k0, k1, k2), 3)]
        chunks += tok((v0, v1, v2), 6)
        acc = jnp.zeros((t, D_MODEL), F32)
        for ci, ch in enumerate(chunks):
            cols = slice(ci * GROUP_WIDTH, (ci + 1) * GROUP_WIDTH)
            cb = ch.astype(BF16)
            dz_ref[:, cols] = cb
            acc = acc + _dot_nt(cb, w_ref[:, cols])
        gv = g_ref[...]
        n, rstd, _ = _rms(h_ref[...], gv)
        dx, dg = _rms_bwd(acc, n, rstd, gv)
        out_ref[...] = dh_ref[...] + dx
        dg_ref[...] += dg

    row = lambda w: pl.BlockSpec((t, w), lambda i: (i, 0))
    vec = pl.BlockSpec((1, D_MODEL), lambda i: (0, 0))
    res = [_residue_spec(dil, t) for dil in DILATIONS]
    return pl.pallas_call(
        body, name=name, grid=(s // t,),
        in_specs=[row(D_MODEL), row(POOL_WIDTH)] + res * 3 + _table_specs(t)
        + [pl.BlockSpec((D_MODEL, N_IN), lambda i: (0, 0)), row(D_MODEL), vec],
        out_specs=[row(D_MODEL), row(N_IN), vec],
        out_shape=[jax.ShapeDtypeStruct((s, D_MODEL), F32), jax.ShapeDtypeStruct((s, N_IN), BF16),
                   jax.ShapeDtypeStruct((1, D_MODEL), F32)],
        scratch_shapes=_stages(t, 9),
        compiler_params=_params(1),
    )(dh, du, *dq, *dk, *dv, rc, rsa, rsb, w_in, h, g)


def _matmul_tn(a, b, name, *, square_a=False, tn=None, blocked_out=False, after=()):
    s, m = a.shape
    n = b.shape[1]
    tk = _row_tile(s, 2048)
    tm = min(m, 1024)
    tn = tn or min(n, 1024)
    assert m % tm == 0 and n % tn == 0
    nk = s // tk
    nsub = tn // FF_BLOCK if blocked_out else 1

    def body(a_ref, b_ref, o_ref, ob_ref, acc):
        k = pl.program_id(2)

        @pl.when(k == 0)
        def _():
            acc[...] = jnp.zeros_like(acc)

        av = a_ref[...]
        if square_a:
            av = av.astype(F32)
            av = av * av
        acc[...] += _dot_tn(av.astype(BF16), b_ref[...].astype(BF16))

        @pl.when(k == nk - 1)
        def _():
            if blocked_out:
                for sub in range(nsub):
                    cols = slice(sub * FF_BLOCK, (sub + 1) * FF_BLOCK)
                    o_ref[sub] = acc[:, cols]
                    ob_ref[sub] = acc[:, cols].astype(BF16)
            else:
                o_ref[...] = acc[...]
                ob_ref[...] = acc[...].astype(BF16)

    if blocked_out:
        shape = (n // FF_BLOCK, m, FF_BLOCK)
        out_spec = pl.BlockSpec((nsub, tm, FF_BLOCK), lambda i, j, k: (j, i, 0))
    else:
        shape = (m, n)
        out_spec = pl.BlockSpec((tm, tn), lambda i, j, k: (i, j))
    return pl.pallas_call(
        _ordered_after(body, 2, after), name=name, grid=(m // tm, n // tn, nk),
        in_specs=[pl.BlockSpec((tk, tm), lambda i, j, k: (k, i)), pl.BlockSpec((tk, tn), lambda i, j, k: (k, j))]
        + [pl.BlockSpec(memory_space=pl.ANY)] * len(after),
        out_specs=[out_spec, out_spec],
        out_shape=[jax.ShapeDtypeStruct(shape, F32), jax.ShapeDtypeStruct(shape, BF16)],
        scratch_shapes=[pltpu.VMEM((tm, tn), F32)],
        compiler_params=_params(3),
    )(a, b, *after)


def _adamw_math(w, g, m, v):
    m = ADAM_B1 * m + (1.0 - ADAM_B1) * g
    v = ADAM_B2 * v + (1.0 - ADAM_B2) * (g * g)
    m_hat = m / (1.0 - ADAM_B1 ** ADAM_STEP)
    v_hat = v / (1.0 - ADAM_B2 ** ADAM_STEP)
    delta = -ADAM_LR * (m_hat / (jnp.sqrt(v_hat) + ADAM_EPS) + ADAM_WD * w)
    return delta, m, v


def _adamw_sharded(w, m, v, own, recv0, recv1, name):
    _, rows, cols = w.shape
    t = _row_tile(rows, 256)

    def body(w_ref, m_ref, v_ref, own_ref, r0_ref, r1_ref, g_ref, d_ref, nm_ref, nv_ref):
        layer0 = pl.program_id(0) == 0
        g = own_ref[...]
        for k in range(N_DEV - 1):
            g = g + jnp.where(layer0, r0_ref[k], r1_ref[k]).astype(F32)
        g_ref[...] = g
        d_ref[...], nm_ref[...], nv_ref[...] = _adamw_math(w_ref[...], g, m_ref[...], v_ref[...])

    blk = pl.BlockSpec((None, t, cols), lambda l, i: (l, i, 0))
    recv = lambda layer: pl.BlockSpec((N_DEV - 1, t, cols), lambda l, i: (0, jnp.where(l == layer, i, 0), 0))
    return pl.pallas_call(
        body, name=name, grid=(2, rows // t),
        in_specs=[blk, blk, blk, blk, recv(0), recv(1)], out_specs=[blk] * 4,
        out_shape=[jax.ShapeDtypeStruct(w.shape, F32)] * 4,
        compiler_params=_params(2),
    )(w, m, v, own, recv0, recv1)


def _adamw_packed(w, g8, m, v, name):
    def body(w_ref, g_ref, m_ref, v_ref, go_ref, d_ref, nm_ref, nv_ref):
        g = g_ref[0]
        for dev in range(1, N_DEV):
            g = g + g_ref[dev]
        go_ref[...] = g
        d_ref[...], nm_ref[...], nv_ref[...] = _adamw_math(w_ref[...], g, m_ref[...], v_ref[...])

    return pl.pallas_call(
        body, name=name, out_shape=[jax.ShapeDtypeStruct(w.shape, F32)] * 4,
        compiler_params=pltpu.CompilerParams(vmem_limit_bytes=VMEM_LIMIT),
    )(w, g8, m, v)


def _peer(k):
    x, y, c = lax.axis_index("x"), lax.axis_index("y"), lax.axis_index("c")
    return (1 - x if k & 4 else x, 1 - y if k & 2 else y, 1 - c if k & 1 else c)


def _linear(dev):
    return 4 * dev[0] + 2 * dev[1] + dev[2]


HBM_SPEC = pl.BlockSpec(memory_space=pltpu.HBM)
SEM_SPEC = pl.BlockSpec(memory_space=pltpu.SEMAPHORE)
ANY_SPEC = pl.BlockSpec(memory_space=pl.ANY)
EFFECT = pltpu.SideEffectType.DATAFLOW_SIDE_EFFECTING


def _in_hbm(a):
    return pltpu.with_memory_space_constraint(a, pltpu.HBM)


class _Exchange:
    def __init__(self, name, groups, scatter, after=()):
        self.name, self.scatter = name, scatter
        self.sizes = sizes = [len(g) for g in groups]
        srcs = [a for g in groups for a in g]
        n, ng = len(srcs), len(groups)
        lead = (N_DEV - 1,) if scatter else (N_DEV,)
        shapes = [lead + (a.shape[1:] if scatter else a.shape) for a in srcs]
        lands = [lax.empty(sh, a.dtype) for sh, a in zip(shapes, srcs)]
        offsets = [sum(sizes[:gi]) for gi in range(ng)]
        copy = self._copy

        def body(*refs):
            src, land = refs[:n], refs[n:2 * n]
            sems = refs[2 * n + len(after):2 * n + len(after) + 2 * ng]
            token = refs[-1]
            for gi in range(ng):
                for wi in range(sizes[gi]):
                    w = offsets[gi] + wi
                    for k in range(1, N_DEV):
                        copy(src[w], land[w], sems[2 * gi], sems[2 * gi + 1], wi, k).start()
            token[...] = jnp.zeros_like(token)

        sem_shapes = [pltpu.SemaphoreType.DMA((7 * sz,)) for sz in sizes for _ in range(2)]
        outs = pl.pallas_call(
            body, name=name + "_start",
            in_specs=[HBM_SPEC] * (2 * n) + [ANY_SPEC] * len(after),
            out_specs=[SEM_SPEC] * (2 * ng) + [HBM_SPEC] * (2 * n) + [pl.BlockSpec(memory_space=pltpu.VMEM)],
            out_shape=sem_shapes + [pltpu.HBM(a.shape, a.dtype) for a in srcs + lands]
            + [jax.ShapeDtypeStruct((8, LANES), F32)],
            input_output_aliases={i: 2 * ng + i for i in range(2 * n)},
            compiler_params=pltpu.CompilerParams(has_side_effects=EFFECT),
        )(*[_in_hbm(a) for a in srcs + lands], *after)
        self.sems = [outs[2 * gi:2 * gi + 2] for gi in range(ng)]
        thru = outs[2 * ng:2 * ng + 2 * n]
        self.srcs = [thru[offsets[gi]:offsets[gi] + sizes[gi]] for gi in range(ng)]
        self.lands = [thru[n + offsets[gi]:n + offsets[gi] + sizes[gi]] for gi in range(ng)]
        self.token = outs[-1]

    def _copy(self, src, land, send_sems, recv_sems, wi, k):
        to = _peer(k)
        if self.scatter:
            src_ref, dst_ref = src.at[_linear(to)], land.at[k - 1]
        else:
            src_ref, dst_ref = src, land.at[_linear(_peer(0))]
        return pltpu.make_async_remote_copy(
            src_ref=src_ref, dst_ref=dst_ref, send_sem=send_sems.at[7 * wi + k - 1],
            recv_sem=recv_sems.at[7 * wi + k - 1], device_id=to, device_id_type=MESH)

    def wait(self, gi, after):
        n = self.sizes[gi]
        copy = self._copy

        def body(*refs):
            src, land = refs[:n], refs[n:2 * n]
            send_sems, recv_sems = refs[2 * n], refs[2 * n + 1]
            for wi in range(n):
                for k in range(1, N_DEV):
                    cp = copy(src[wi], land[wi], send_sems, recv_sems, wi, k)
                    cp.wait_send()
                    cp.wait_recv()

        arrays = list(self.srcs[gi]) + list(self.lands[gi])
        outs = pl.pallas_call(
            body, name=f"{self.name}_wait{gi}",
            in_specs=[HBM_SPEC] * (2 * n) + [SEM_SPEC, SEM_SPEC] + [ANY_SPEC] * len(after),
            out_specs=[HBM_SPEC] * (2 * n),
            out_shape=[pltpu.HBM(a.shape, a.dtype) for a in arrays],
            input_output_aliases={i: i for i in range(2 * n)},
            compiler_params=pltpu.CompilerParams(has_side_effects=EFFECT),
        )(*arrays, *self.sems[gi], *after)
        return outs[:n], outs[n:]


def _rotary_tables(positions):
    rot_dim = HEAD_DIM // 4
    inv_freq = ROPE_THETA ** (-jnp.arange(0, rot_dim, 2, dtype=F32) / rot_dim)
    ang = positions.astype(F32)[:, None] * inv_freq
    cs = jnp.concatenate([jnp.cos(ang), jnp.sin(ang)], axis=1)
    dim = jnp.arange(LANES) % HEAD_DIM
    first, second = dim < ROT_SHIFT, (dim >= ROT_SHIFT) & (dim < rot_dim)
    src = jnp.arange(2 * ROT_SHIFT)[:, None]
    angle = (dim % ROT_SHIFT)[None, :]
    c = jnp.where((first | second)[None, :] & (src == angle), 1.0, 0.0)
    sa = jnp.where(second[None, :] & (src == angle + ROT_SHIFT), 1.0, 0.0)
    sb = jnp.where(first[None, :] & (src == angle + ROT_SHIFT), -1.0, 0.0)
    spread = jnp.concatenate([c, sa, sb], axis=1).astype(F32)
    base = jnp.concatenate([jnp.where(first | second, 0.0, 1.0), jnp.zeros((2 * LANES,))]).astype(F32)[None, :]
    return jnp.dot(cs, spread, precision=lax.Precision.HIGHEST, preferred_element_type=F32) + base


def _block_diag(pool_w):
    gc = pool_w.shape[-1]
    out = jnp.zeros((POOL_WIDTH, POOL_WIDTH), pool_w.dtype)
    for grp in range(pool_w.shape[0]):
        out = lax.dynamic_update_slice(out, pool_w[grp], (grp * gc, grp * gc))
    return out


def _diag_blocks(a):
    gc = POOL_WIDTH // len(POOL_WINDOWS)
    return jnp.stack([a[grp * gc:(grp + 1) * gc, grp * gc:(grp + 1) * gc] for grp in range(len(POOL_WINDOWS))])


def _local_step(x, p, positions, loss_target, norm1, pool_w, pool_scale, norm2, norm3, final_norm, weights, send):
    rc = rsa = rsb = _rotary_tables(positions)
    ones_bd = _block_diag(jnp.ones((4, HEAD_DIM, HEAD_DIM), BF16))
    saved = []
    h = x
    for i in range(2):
        tag = f"_l{i}"
        g1, g2, g3 = norm1[i:i + 1], norm2[i:i + 1], norm3[i:i + 1]
        w_bd = _block_diag(pool_w[i]).astype(BF16)
        scale = pool_scale[i:i + 1]
        w_in = weights(i, "in", (h, rc, w_bd))
        hn1, u, *qkv = _normproj_fwd(h, g1, w_in, rc, rsa, rsb, "normproj_fwd" + tag)
        qkv = [qkv[3 * grp:3 * grp + 3] for grp in range(3)]
        started = weights(i, "prefetch", (hn1,))
        pool_out, y = _pool_fwd(u, w_bd, scale, "pool_fwd" + tag, after=started)
        o, lse = zip(*[_attn_fwd(*qkv[grp], f"attn_fwd{tag}_g{grp}", after=started) for grp in range(3)])
        w_out = weights(i, "out", (pool_out, *o))
        h1, a = _outproj_fwd(h, pool_out, o, lse, w_out, "outproj_fwd" + tag)
        w_up, w_down, w_gate, w_ple = weights(i, "rest", (h1,))
        h2, hn2, r = _mlp_fwd(h1, g2, w_up, w_down, "mlp_fwd" + tag)
        h3, hn3, gate, pb = _gate_fwd(h2, g3, w_gate, p, i, w_ple, "gate_fwd" + tag)
        saved.append(dict(h0=h, hn1=hn1, qkv=qkv, y=y, o=o, lse=lse, a=a, h1=h1, hn2=hn2, r=r, h2=h2,
                          hn3=hn3, gate=gate, pb=pb, w_bd=w_bd, scale=scale, g1=g1, g2=g2, g3=g3,
                          w_in=w_in, w_out=w_out, w_up=w_up, w_down=w_down, w_gate=w_gate, w_ple=w_ple))
        h = h3
    loss, dh, d_final = _loss_head(h, final_norm.reshape(1, D_MODEL), loss_target, "loss_head")

    grads = [None, None]
    sent = ()
    for i in (1, 0):
        tag = f"_l{i}"
        sv = saved[i]
        dh2, dgl, de, dg3 = _gate_bwd(dh, sv["gate"], sv["pb"], sv["w_ple"], sv["h2"], sv["g3"], sv["w_gate"],
                                      "gate_bwd" + tag, after=sent)
        dw_gate = _matmul_tn(sv["hn3"], dgl, "dw_gate" + tag)
        dw_ple = _matmul_tn(sv["pb"], de, "dw_ple" + tag)
        dh1, dup, dg2, dh2b = _mlp_bwd(dh2, sv["r"], sv["h1"], sv["g2"], sv["w_up"], sv["w_down"], "mlp_bwd" + tag)
        dw_down = _matmul_tn(sv["r"], dh2b, "dw_down" + tag, square_a=True)
        dw_up = _matmul_tn(sv["hn2"], dup, "dw_up" + tag, blocked_out=True)
        dpool, do0, do1, do2, de0, de1, de2, dh1b = _outproj_bwd(dh1, sv["w_out"], sv["o"], sv["lse"], ones_bd,
                                                                 "outproj_bwd" + tag)
        dw_out = _matmul_tn(sv["a"], dh1b, "dw_out" + tag)
        sent = send(i, "main", dict(w_gate=dw_gate, w_ple=dw_ple, w_down=dw_down, w_up=dw_up, w_out=dw_out))
        dqkv = [_attn_bwd(*sv["qkv"][grp], do_g, sv["lse"][grp], de_g, f"attn_bwd{tag}_g{grp}", after=sent)
                for grp, (do_g, de_g) in enumerate(((do0, de0), (do1, de1), (do2, de2)))]
        dq, dk, dv = zip(*dqkv)
        du, dw_bd, dscale = _pool_bwd(dpool, sv["y"], sv["w_bd"], sv["scale"], "pool_bwd" + tag, after=sent)
        dh, dz, dg1 = _normproj_bwd(dh1, du, dq, dk, dv, rc, rsa, rsb, sv["w_in"], sv["h0"], sv["g1"],
                                    "normproj_bwd" + tag)
        grads[i] = dict(norm1=dg1, norm2=dg2, norm3=dg3, pool_w=_diag_blocks(dw_bd), pool_scale=dscale)
        small_sent = send(0, "small", (grads, d_final, loss)) if i == 0 else ()
        dw_in = _matmul_tn(sv["hn1"], dz, "dw_in" + tag, tn=N_IN // 2, after=small_sent)
        sent = send(i, "in", dict(w_in=dw_in))
    return dh, sent


def _pack_small(norm1, norm2, norm3, final_norm, pool_scale, pool_w, spare=None):
    spare = jnp.zeros((1, LANES), F32) if spare is None else spare
    scale_row = jnp.concatenate([pool_scale.reshape(1, 2 * POOL_WIDTH), spare,
                                 jnp.zeros((1, D_MODEL - 2 * POOL_WIDTH - LANES), F32)], axis=1)
    return jnp.concatenate([norm1, norm2, norm3, final_norm.reshape(1, D_MODEL), scale_row,
                            pool_w.reshape(32, D_MODEL)], axis=0)


def _unpack_small(a):
    return dict(norm1=a[0:2], norm2=a[2:4], norm3=a[4:6], final_norm=a[6], pool_scale=a[7, 0:2 * POOL_WIDTH].reshape(2, POOL_WIDTH),
                pool_w=a[8:40].reshape(2, 4, HEAD_DIM, HEAD_DIM))


def _chunks_cols(a, cols):
    return a.reshape(a.shape[0], N_DEV, cols).transpose(1, 0, 2)


def _chunks_rows(a, rows):
    return a.reshape(N_DEV, rows, a.shape[1])


BIG = ("w_in", "w_out", "w_up", "w_down", "w_gate", "w_ple")
SMALL = ("norm1", "norm2", "norm3", "final_norm", "pool_scale", "pool_w")
ORDER = ("norm1", "w_in", "pool_w", "pool_scale", "w_out", "norm2", "w_up", "w_down", "norm3", "w_gate", "w_ple",
         "final_norm")


def kernel(x, p, positions, norm1, w_in, pool_w, pool_scale, w_out, norm2, w_up, w_down, norm3, w_gate, w_ple, final_norm, loss_target, m_norm1, m_w_in, m_pool_w, m_pool_scale, m_w_out, m_norm2, m_w_up, m_w_down, m_norm3, m_w_gate, m_w_ple, m_final_norm, v_norm1, v_w_in, v_pool_w, v_pool_scale, v_w_out, v_norm2, v_w_up, v_w_down, v_norm3, v_w_gate, v_w_ple, v_final_norm):
    w = dict(norm1=norm1, w_in=w_in, pool_w=pool_w, pool_scale=pool_scale, w_out=w_out, norm2=norm2, w_up=w_up,
             w_down=w_down, norm3=norm3, w_gate=w_gate, w_ple=w_ple, final_norm=final_norm)
    m = dict(norm1=m_norm1, w_in=m_w_in, pool_w=m_pool_w, pool_scale=m_pool_scale, w_out=m_w_out, norm2=m_norm2,
             w_up=m_w_up, w_down=m_w_down, norm3=m_norm3, w_gate=m_w_gate, w_ple=m_w_ple, final_norm=m_final_norm)
    v = dict(norm1=v_norm1, w_in=v_w_in, pool_w=v_pool_w, pool_scale=v_pool_scale, w_out=v_w_out, norm2=v_norm2,
             w_up=v_w_up, w_down=v_w_down, norm3=v_norm3, w_gate=v_w_gate, w_ple=v_w_ple, final_norm=v_final_norm)
    seq = x.shape[1]

    bf = {n: [w[n][layer].astype(BF16) for layer in range(2)] for n in BIG}
    rest = ("w_up", "w_down", "w_gate", "w_ple")
    me = 4 * lax.axis_index("x") + 2 * lax.axis_index("y") + lax.axis_index("c")
    gathers = [_Exchange("gather_l0", [[bf["w_in"][0]], [bf["w_out"][0]], [bf[n][0] for n in rest]], scatter=False)]
    unpack = dict(w_in=lambda a: a.transpose(1, 0, 2).reshape(D_MODEL, N_IN),
                  w_out=lambda a: a.reshape(D_MODEL, D_MODEL), w_gate=lambda a: a.reshape(D_MODEL, D_MODEL),
                  w_ple=lambda a: a.transpose(1, 0, 2).reshape(PLE_DIM, D_MODEL), w_up=lambda a: a, w_down=lambda a: a)
    parts = dict(zip(("in", "out", "rest"), (("w_in",), ("w_out",), rest)))

    def weights(layer, part, after):
        if part == "prefetch":
            if layer != 0:
                return ()
            gathers.append(_Exchange("gather_l1", [[bf[n][1] for n in parts[pt]] for pt in parts], scatter=False,
                                     after=after))
            return (gathers[1].token,)
        shards, lands = gathers[layer].wait(tuple(parts).index(part), after)
        full = [unpack[n](lax.dynamic_update_slice_in_dim(land, shard[None], me, axis=0))
                for n, shard, land in zip(parts[part], shards, lands)]
        return full if part == "rest" else full[0]

    to_chunks = dict(w_in=lambda a: _chunks_cols(a, N_IN // N_DEV), w_out=lambda a: _chunks_rows(a, D_MODEL // N_DEV),
                     w_up=lambda a: a, w_down=lambda a: _chunks_rows(a, FF_BLOCK),
                     w_gate=lambda a: _chunks_rows(a, D_MODEL // N_DEV), w_ple=lambda a: _chunks_cols(a, D_MODEL // N_DEV))
    own = {n: [None, None] for n in BIG}
    scatters = {}

    def own_chunk(n, g32):
        if n in ("w_in", "w_ple"):
            cols = g32.shape[1] // N_DEV
            return lax.dynamic_slice(g32, (0, me * cols), (g32.shape[0], cols))
        return lax.dynamic_index_in_dim(to_chunks[n](g32), me, axis=0, keepdims=False)

    def send(layer, part, grads):
        if part == "small":
            per_layer, d_final, loss = grads
            pack = _pack_small(
                *[jnp.concatenate([per_layer[0][n], per_layer[1][n]], axis=0) for n in ("norm1", "norm2", "norm3")],
                d_final.reshape(D_MODEL),
                jnp.concatenate([per_layer[0]["pool_scale"], per_layer[1]["pool_scale"]], axis=0),
                jnp.stack([per_layer[0]["pool_w"], per_layer[1]["pool_w"]]), spare=loss)
            scatters["small"] = _Exchange("gather_small", [[pack]], scatter=False)
            return (scatters["small"].token,)
        for n, (g32, _) in grads.items():
            own[n][layer] = own_chunk(n, g32)
        ex = _Exchange(f"scatter_{part}_l{layer}", [[to_chunks[n](g16) for n, (_, g16) in grads.items()]], scatter=True)
        scatters[layer, part] = (tuple(grads), ex)
        return (ex.token,)

    dx, sent = _local_step(
        x.reshape(seq, D_MODEL), p.reshape(2, seq, PLE_DIM), positions.reshape(seq), loss_target.reshape(seq, D_MODEL),
        norm1, pool_w, pool_scale, norm2, norm3, final_norm, weights, send)

    g_out, d_out, m_out, v_out = {}, {}, {}, {}
    for part in ("main", "in"):
        recv = {}
        for layer in (1, 0):
            names, ex = scatters[layer, part]
            for n, r in zip(names, ex.wait(0, sent)[1]):
                recv[n, layer] = r
        for n in names:
            g_out[n], d_out[n], m_out[n], v_out[n] = _adamw_sharded(
                w[n], m[n], v[n], jnp.stack(own[n]), recv[n, 0], recv[n, 1], "adamw_" + n)
        sent = tuple(d_out[n] for n in names)
    (mine,), (landed,) = scatters["small"].wait(0, sent)
    small_g8 = lax.dynamic_update_slice_in_dim(landed, mine[None], me, axis=0)
    pack = lambda t: _pack_small(*[t[n] for n in SMALL])
    small_g, d_small, m_small, v_small = _adamw_packed(pack(w), small_g8, pack(m), pack(v), "adamw_small")
    for dst, a in ((g_out, small_g), (d_out, d_small), (m_out, m_small), (v_out, v_small)):
        dst.update(_unpack_small(a))

    return (small_g[7, 2 * POOL_WIDTH],dx.reshape(1, seq, D_MODEL), *[g_out[n] for n in ORDER], *[d_out[n] for n in ORDER],
            *[m_out[n] for n in ORDER], *[v_out[n] for n in ORDER])
```

```python
import functools

import jax
import jax.numpy as jnp
from jax import lax
from jax.experimental import pallas as pl
from jax.experimental.pallas import tpu as pltpu

F32 = jnp.float32
BF16 = jnp.bfloat16

D_MODEL = 1024
HEAD_DIM = 64
POOL_WIDTH = 256
POOL_WINDOWS = (2, 4, 8, 16)
POOL_HALO = 16
POOL_PAD = 8
GROUP_WIDTH = 256
DILATIONS = (1, 4, 16)
ATTN_BLOCK = 128
ROT_SHIFT = 8
ROPE_THETA = 500000.0
D_FF = 4096
FF_BLOCK = 512
FF_PER_STEP = 2
MLP_BWD_TILE = 512
N_DEV = 8
N_IN = POOL_WIDTH + 3 * 768
PLE_DIM = 256
EPS = 1e-6
NEG_BIG = -1e30

ADAM_LR = 0.001
ADAM_B1 = 0.9
ADAM_B2 = 0.999
ADAM_EPS = 1e-08
ADAM_WD = 0.01
ADAM_STEP = 10

LANES = 128
VMEM_LIMIT = 56 * 1024 * 1024
MESH = pl.DeviceIdType.MESH


def _params(n_grid):
    return pltpu.CompilerParams(dimension_semantics=("arbitrary",) * n_grid, vmem_limit_bytes=VMEM_LIMIT)


def _dot(a, b):
    return jnp.dot(a, b, preferred_element_type=F32)


def _dot_nt(a, b):
    return lax.dot_general(a, b, (((1,), (1,)), ((), ())), preferred_element_type=F32)


def _dot_tn(a, b):
    return lax.dot_general(a, b, (((0,), (0,)), ((), ())), preferred_element_type=F32)


def _rms(x, g):
    rstd = lax.rsqrt(jnp.mean(x * x, axis=-1, keepdims=True) + EPS)
    n = x * rstd
    return n, rstd, n * g


def _rms_bwd(dy, n, rstd, g):
    dyn = dy * g
    dx = rstd * (dyn - n * jnp.mean(dyn * n, axis=-1, keepdims=True))
    return dx, jnp.sum(dy * n, axis=0, keepdims=True)


def _ordered_after(body, n_in, after):
    if not after:
        return body
    return lambda *refs: body(*refs[:n_in], *refs[n_in + len(after):])


def _row_tile(s, t):
    t = min(s, t)
    assert s % t == 0
    return t


def _rot(z, c, sa, sb):
    return z * c + pltpu.roll(z, ROT_SHIFT, 1) * sa + pltpu.roll(z, LANES - ROT_SHIFT, 1) * sb


def _table_specs(t):
    return [pl.BlockSpec((t, LANES), functools.partial(lambda i, k: (i, k), k=k)) for k in range(3)]


def _rot_t(dz, c, sa, sb):
    return dz * c + pltpu.roll(dz * sa, LANES - ROT_SHIFT, 1) + pltpu.roll(dz * sb, ROT_SHIFT, 1)


def _to_residues(value, stage, out_ref, dil):
    if dil == 1:
        out_ref[0] = value.astype(out_ref.dtype)
        return
    rows = value.shape[0] // dil
    for hf in range(GROUP_WIDTH // LANES):
        lanes = slice(hf * LANES, (hf + 1) * LANES)
        stage[hf][...] = value[:, lanes]
        for r in range(dil):
            out_ref[r, :, lanes] = stage[hf][pl.ds(r, rows, stride=dil), :].astype(out_ref.dtype)


def _from_residues(in_ref, stage, dil):
    if dil == 1:
        return in_ref[0].astype(F32)
    rows = in_ref.shape[1]
    for hf in range(GROUP_WIDTH // LANES):
        for r in range(dil):
            stage[hf][pl.ds(r, rows, stride=dil), :] = in_ref[r, :, hf * LANES:(hf + 1) * LANES].astype(F32)
    return jnp.concatenate([stage[0][...], stage[1][...]], axis=1)


def _residue_spec(dil, t):
    return pl.BlockSpec((dil, t // dil, GROUP_WIDTH), lambda i: (0, i, 0))


def _residue_shape(dil, s, dtype):
    return jax.ShapeDtypeStruct((dil, s // dil, GROUP_WIDTH), dtype)


def _stages(t, n):
    return [pltpu.VMEM((t, LANES), F32)] * (n * (GROUP_WIDTH // LANES))


def _pair_stages(refs):
    return [refs[i:i + 2] for i in range(0, len(refs), 2)]


def _normproj_fwd(h, g, w_in, rc, rsa, rsb, name):
    s = h.shape[0]
    t = _row_tile(s, 512)

    def body(h_ref, g_ref, w_ref, c_ref, sa_ref, sb_ref, hn_ref, u_ref, *rest):
        qkv_refs, stages = rest[:9], _pair_stages(rest[9:])
        _, _, hn = _rms(h_ref[...], g_ref[...])
        hb = hn.astype(BF16)
        hn_ref[...] = hb
        c, sa, sb = c_ref[...], sa_ref[...], sb_ref[...]

        def rot(z, scale):
            halves = [_rot(z[:, hf * LANES:(hf + 1) * LANES], c, sa, sb) * scale for hf in range(2)]
            return jnp.concatenate(halves, axis=1)

        u_ref[...] = _dot(hb, w_ref[:, 0:POOL_WIDTH])
        for grp, dil in enumerate(DILATIONS):
            lo = POOL_WIDTH + grp * GROUP_WIDTH
            q_ref, k_ref, v_ref = qkv_refs[3 * grp:3 * grp + 3]
            _to_residues(rot(_dot(hb, w_ref[:, lo:lo + GROUP_WIDTH]), HEAD_DIM ** -0.5), stages[0], q_ref, dil)
            _to_residues(rot(_dot(hb, w_ref[:, lo + 768:lo + 768 + GROUP_WIDTH]), 1.0), stages[1], k_ref, dil)
            _to_residues(_dot(hb, w_ref[:, lo + 1536:lo + 1536 + GROUP_WIDTH]), stages[2], v_ref, dil)

    row = lambda w: pl.BlockSpec((t, w), lambda i: (i, 0))
    return pl.pallas_call(
        body, name=name, grid=(s // t,),
        in_specs=[row(D_MODEL), pl.BlockSpec((1, D_MODEL), lambda i: (0, 0)),
                  pl.BlockSpec((D_MODEL, N_IN), lambda i: (0, 0))] + _table_specs(t),
        out_specs=[row(D_MODEL), row(POOL_WIDTH)] + [_residue_spec(dil, t) for dil in DILATIONS for _ in range(3)],
        out_shape=[jax.ShapeDtypeStruct((s, D_MODEL), BF16), jax.ShapeDtypeStruct((s, POOL_WIDTH), F32)]
        + [_residue_shape(dil, s, BF16) for dil in DILATIONS for _ in range(3)],
        scratch_shapes=_stages(t, 3),
        compiler_params=_params(1),
    )(h, g, w_in, rc, rsa, rsb)


def _pool_lane_window():
    lane = lax.broadcasted_iota(jnp.int32, (1, POOL_WIDTH), 1)
    return jnp.left_shift(2, lane // (POOL_WIDTH // len(POOL_WINDOWS)))


def _window_sums(ext, b2, b4, b8, t, lo, tile, direction):
    rows = t + POOL_HALO
    for src, dst, sh in ((ext, b2, 1), (b2, b4, 2), (b4, b8, 4)):
        dst[lo:lo + rows, :] = src[lo:lo + rows, :] + src[lo + direction * sh:lo + direction * sh + rows, :]
    s16 = b8[tile:tile + t, :] + b8[tile + direction * 8:tile + direction * 8 + t, :]
    win = _pool_lane_window()
    return jnp.where(win == 2, b2[tile:tile + t, :],
                     jnp.where(win == 4, b4[tile:tile + t, :], jnp.where(win == 8, b8[tile:tile + t, :], s16)))


def _pool_fwd(u, w_bd, scale, name, after=()):
    s = u.shape[0]
    t = _row_tile(s, 512)
    first = POOL_PAD + POOL_HALO

    def body(u_ref, w_ref, sc_ref, out_ref, y_ref, ext, b2, b4, b8):
        i = pl.program_id(0)

        @pl.when(i == 0)
        def _():
            for buf in (ext, b2, b4):
                buf[0:POOL_PAD, :] = jnp.zeros((POOL_PAD, POOL_WIDTH), F32)
            ext[POOL_PAD:first, :] = jnp.zeros((POOL_HALO, POOL_WIDTH), F32)

        x = u_ref[...]
        ext[first:, :] = x
        wsum = _window_sums(ext, b2, b4, b8, t, POOL_PAD, first, -1)
        pos = i * t + lax.broadcasted_iota(jnp.int32, (t, POOL_WIDTH), 0)
        cnt = jnp.minimum(pos + 1, _pool_lane_window()).astype(F32)
        y = wsum / cnt - x
        yb = y.astype(BF16)
        y_ref[...] = yb
        out_ref[...] = _dot(yb, w_ref[...]) * sc_ref[...]
        ext[POOL_PAD:first, :] = x[t - POOL_HALO:, :]

    row = pl.BlockSpec((t, POOL_WIDTH), lambda i: (i, 0))
    return pl.pallas_call(
        _ordered_after(body, 3, after), name=name, grid=(s // t,),
        in_specs=[row, pl.BlockSpec((POOL_WIDTH, POOL_WIDTH), lambda i: (0, 0)),
                  pl.BlockSpec((1, POOL_WIDTH), lambda i: (0, 0))] + [pl.BlockSpec(memory_space=pl.ANY)] * len(after),
        out_specs=[row, row],
        out_shape=[jax.ShapeDtypeStruct((s, POOL_WIDTH), F32), jax.ShapeDtypeStruct((s, POOL_WIDTH), BF16)],
        scratch_shapes=[pltpu.VMEM((t + POOL_HALO + POOL_PAD, POOL_WIDTH), F32)] * 4,
        compiler_params=_params(1),
    )(u, w_bd, scale, *after)


def _head_masks():
    lane = lax.broadcasted_iota(jnp.int32, (ATTN_BLOCK, GROUP_WIDTH), 1)
    return [lane // HEAD_DIM == hd for hd in range(GROUP_WIDTH // HEAD_DIM)]


def _stack_heads(a, masks):
    zero = jnp.zeros_like(a)
    return jnp.concatenate([jnp.where(m, a, zero) for m in masks], axis=0)


def _band_bias(first_step):
    rows = ATTN_BLOCK * (GROUP_WIDTH // HEAD_DIM)
    i = lax.broadcasted_iota(jnp.int32, (rows, 2 * ATTN_BLOCK), 0) & (ATTN_BLOCK - 1)
    j = lax.broadcasted_iota(jnp.int32, (rows, 2 * ATTN_BLOCK), 1)
    inner = jnp.where((j >= i) & (j <= i + ATTN_BLOCK), 0.0, NEG_BIG)
    return jnp.where((j < ATTN_BLOCK) & first_step, NEG_BIG, inner), inner


def _column_per_head(a):
    return jnp.concatenate([a[:, hd * HEAD_DIM:hd * HEAD_DIM + 1] for hd in range(GROUP_WIDTH // HEAD_DIM)], axis=0)


def _blocks_per_step(nb):
    return 8 if nb % 8 == 0 else 4 if nb % 4 == 0 else 2 if nb % 2 == 0 else 1


def _residues_per_step(dil, nb, qb):
    return 2 if (nb == qb and qb < 8 and dil % 2 == 0) else 1


def _attn_fwd(q, k, v, name, after=()):
    dil, length, _ = q.shape
    nb = length // ATTN_BLOCK
    qb = _blocks_per_step(nb)
    rs = _residues_per_step(dil, nb, qb)

    def body(q_ref, kp_ref, kc_ref, vp_ref, vc_ref, o_ref, lse_ref):
        masks = _head_masks()
        bias = _band_bias(pl.program_id(1) == 0)
        for rr in range(rs):
            for qi in range(qb):
                here = slice(qi * ATTN_BLOCK, (qi + 1) * ATTN_BLOCK)
                before = slice((qi - 1) * ATTN_BLOCK, qi * ATTN_BLOCK)
                kcat = jnp.concatenate([kp_ref[rr] if qi == 0 else kc_ref[rr, before], kc_ref[rr, here]], axis=0)
                vcat = jnp.concatenate([vp_ref[rr] if qi == 0 else vc_ref[rr, before], vc_ref[rr, here]], axis=0)
                qs = _stack_heads(q_ref[rr, here], masks)
                sc = _dot_nt(qs, kcat) + bias[min(qi, 1)]
                m = jnp.max(sc, axis=1, keepdims=True)
                e = jnp.exp(sc - m)
                l = jnp.sum(e, axis=1, keepdims=True)
                p = (e / l).astype(BF16)
                lse = m + jnp.log(l)
                o = jnp.zeros((ATTN_BLOCK, GROUP_WIDTH), F32)
                lse_full = jnp.zeros((ATTN_BLOCK, GROUP_WIDTH), F32)
                for hd, msk in enumerate(masks):
                    rows = slice(hd * ATTN_BLOCK, (hd + 1) * ATTN_BLOCK)
                    o = jnp.where(msk, _dot(p[rows], vcat), o)
                    lse_full = jnp.where(msk, lse[rows], lse_full)
                o_ref[rr, here] = o.astype(o_ref.dtype)
                lse_ref[rr, here] = lse_full

    cur = pl.BlockSpec((rs, qb * ATTN_BLOCK, GROUP_WIDTH), lambda r, j: (r, j, 0))
    prev = pl.BlockSpec((rs, ATTN_BLOCK, GROUP_WIDTH), lambda r, j: (r, jnp.maximum(qb * j - 1, 0), 0))
    return pl.pallas_call(
        _ordered_after(body, 5, after), name=name, grid=(dil // rs, nb // qb),
        in_specs=[cur, prev, cur, prev, cur] + [pl.BlockSpec(memory_space=pl.ANY)] * len(after), out_specs=[cur, cur],
        out_shape=[jax.ShapeDtypeStruct(q.shape, BF16), jax.ShapeDtypeStruct(q.shape, F32)],
        compiler_params=_params(2),
    )(q, k, k, v, v, *after)


def _group_weights(l0, l1, l2):
    m = jnp.maximum(jnp.maximum(l0, l1), l2)
    e0, e1, e2 = jnp.exp(l0 - m), jnp.exp(l1 - m), jnp.exp(l2 - m)
    den = e0 + e1 + e2
    return e0 / den, e1 / den, e2 / den


def _outproj_fwd(h, pool_out, o, lse, w_out, name):
    s = h.shape[0]
    t = _row_tile(s, 512)

    def body(h_ref, po_ref, o0, o1, o2, l0, l1, l2, w_ref, out_ref, a_ref, *stages):
        stages = _pair_stages(stages)
        ov =[_from_residues(r, stages[i], DILATIONS[i]) for i, r in enumerate((o0, o1, o2))]
        lv = [_from_residues(r, stages[3 + i], DILATIONS[i]) for i, r in enumerate((l0, l1, l2))]
        wts = _group_weights(*lv)
        a = jnp.concatenate([po_ref[...]] + [ov[i] * wts[i] for i in range(3)], axis=1).astype(BF16)
        a_ref[...] = a
        out_ref[...] = h_ref[...] + _dot(a, w_ref[...])

    row = lambda w: pl.BlockSpec((t, w), lambda i: (i, 0))
    res = [_residue_spec(dil, t) for dil in DILATIONS]
    return pl.pallas_call(
        body, name=name, grid=(s // t,),
        in_specs=[row(D_MODEL), row(POOL_WIDTH)] + res + res + [pl.BlockSpec((D_MODEL, D_MODEL), lambda i: (0, 0))],
        out_specs=[row(D_MODEL), row(D_MODEL)],
        out_shape=[jax.ShapeDtypeStruct((s, D_MODEL), F32), jax.ShapeDtypeStruct((s, D_MODEL), BF16)],
        scratch_shapes=_stages(t, 6),
        compiler_params=_params(1),
    )(h, pool_out, *o, *lse, w_out)


def _mlp_fwd(h, g, w_up, w_down, name):
    s = h.shape[0]
    t = _row_tile(s, 512)
    nblk = D_FF // FF_BLOCK

    def body(h_ref, g_ref, wu_ref, wd_ref, out_ref, hn_ref, r_ref):
        x = h_ref[...]
        _, _, hn = _rms(x, g_ref[...])
        hb = hn.astype(BF16)
        hn_ref[...] = hb
        acc = None
        for b0 in range(0, nblk, FF_PER_STEP):
            acts = []
            for b in range(b0, b0 + FF_PER_STEP):
                r = jnp.maximum(_dot(hb, wu_ref[b]), 0.0)
                r_ref[:, b * FF_BLOCK:(b + 1) * FF_BLOCK] = r.astype(BF16)
                acts.append((r * r).astype(BF16))
            wd = wd_ref[b0:b0 + FF_PER_STEP].reshape(FF_PER_STEP * FF_BLOCK, D_MODEL)
            part = _dot(jnp.concatenate(acts, axis=1), wd)
            acc = part if acc is None else acc + part
        out_ref[...] = x + acc

    row = lambda w: pl.BlockSpec((t, w), lambda i: (i, 0))
    resident = lambda shape: pl.BlockSpec(shape, lambda i: (0, 0, 0), pipeline_mode=pl.Buffered(1))
    return pl.pallas_call(
        body, name=name, grid=(s // t,),
        in_specs=[row(D_MODEL), pl.BlockSpec((1, D_MODEL), lambda i: (0, 0)),
                  resident((nblk, D_MODEL, FF_BLOCK)), resident((nblk, FF_BLOCK, D_MODEL))],
        out_specs=[row(D_MODEL), row(D_MODEL), row(D_FF)],
        out_shape=[jax.ShapeDtypeStruct((s, D_MODEL), F32), jax.ShapeDtypeStruct((s, D_MODEL), BF16),
                   jax.ShapeDtypeStruct((s, D_FF), BF16)],
        compiler_params=_params(1),
    )(h, g, w_up, w_down)


def _gate_fwd(h, g, w_gate, p, layer, w_ple, name):
    s = h.shape[0]
    t = _row_tile(s, 512)

    def body(h_ref, g_ref, wg_ref, p_ref, wp_ref, out_ref, hn_ref, gate_ref, pb_ref):
        x = h_ref[...]
        _, _, hn = _rms(x, g_ref[...])
        hb = hn.astype(BF16)
        hn_ref[...] = hb
        gate = 1.0 / (1.0 + jnp.exp(-_dot(hb, wg_ref[...])))
        pb = p_ref[...].astype(BF16)
        pb_ref[...] = pb
        gate_ref[...] = gate.astype(BF16)
        out_ref[...] = x + gate * _dot(pb, wp_ref[...])

    row = lambda w: pl.BlockSpec((t, w), lambda i: (i, 0))
    full = lambda a, b: pl.BlockSpec((a, b), lambda i: (0, 0))
    return pl.pallas_call(
        body, name=name, grid=(s // t,),
        in_specs=[row(D_MODEL), full(1, D_MODEL), full(D_MODEL, D_MODEL),
                  pl.BlockSpec((None, t, PLE_DIM), lambda i: (layer, i, 0)), full(PLE_DIM, D_MODEL)],
        out_specs=[row(D_MODEL), row(D_MODEL), row(D_MODEL), row(PLE_DIM)],
        out_shape=[jax.ShapeDtypeStruct((s, D_MODEL), F32), jax.ShapeDtypeStruct((s, D_MODEL), BF16),
                   jax.ShapeDtypeStruct((s, D_MODEL), BF16), jax.ShapeDtypeStruct((s, PLE_DIM), BF16)],
        compiler_params=_params(1),
    )(h, g, w_gate, p, w_ple)


def _loss_head(h, g, target, name):
    s = h.shape[0]
    t = _row_tile(s, 512)

    def body(h_ref, g_ref, t_ref, loss_ref, dh_ref, dg_ref):
        i = pl.program_id(0)

        @pl.when(i == 0)
        def _():
            loss_ref[...] = jnp.zeros_like(loss_ref)
            dg_ref[...] = jnp.zeros_like(dg_ref)

        gv = g_ref[...]
        n, rstd, y = _rms(h_ref[...], gv)
        err = y - t_ref[...]
        loss_ref[...] += jnp.sum(err * err) * (0.5 / D_MODEL)
        dx, dg = _rms_bwd(err * (1.0 / D_MODEL), n, rstd, gv)
        dh_ref[...] = dx
        dg_ref[...] += dg

    row = pl.BlockSpec((t, D_MODEL), lambda i: (i, 0))
    vec = pl.BlockSpec((1, D_MODEL), lambda i: (0, 0))
    return pl.pallas_call(
        body, name=name, grid=(s // t,),
        in_specs=[row, vec, row],
        out_specs=[pl.BlockSpec((1, LANES), lambda i: (0, 0)), row, vec],
        out_shape=[jax.ShapeDtypeStruct((1, LANES), F32), jax.ShapeDtypeStruct((s, D_MODEL), F32),
                   jax.ShapeDtypeStruct((1, D_MODEL), F32)],
        compiler_params=_params(1),
    )(h, g, target)


def _gate_bwd(dh, gate, pb, w_ple, h, g, w_gate, name, after=()):
    s = h.shape[0]
    t = _row_tile(s, 512)

    def body(dh_ref, gate_ref, pb_ref, wp_ref, h_ref, g_ref, wg_ref, out_ref, dgl_ref, de_ref, dg_ref):
        @pl.when(pl.program_id(0) == 0)
        def _():
            dg_ref[...] = jnp.zeros_like(dg_ref)

        d = dh_ref[...]
        gate = gate_ref[...].astype(F32)
        e = _dot(pb_ref[...], wp_ref[...])
        dgl = (d * e * gate * (1.0 - gate)).astype(BF16)
        dgl_ref[...] = dgl
        de_ref[...] = (d * gate).astype(BF16)
        gv = g_ref[...]
        n, rstd, _ = _rms(h_ref[...], gv)
        dx, dg = _rms_bwd(_dot_nt(dgl, wg_ref[...]), n, rstd, gv)
        out_ref[...] = d + dx
        dg_ref[...] += dg

    row = lambda w: pl.BlockSpec((t, w), lambda i: (i, 0))
    full = lambda a, b: pl.BlockSpec((a, b), lambda i: (0, 0))
    return pl.pallas_call(
        _ordered_after(body, 7, after), name=name, grid=(s // t,),
        in_specs=[row(D_MODEL), row(D_MODEL), row(PLE_DIM), full(PLE_DIM, D_MODEL), row(D_MODEL), full(1, D_MODEL),
                  full(D_MODEL, D_MODEL)] + [pl.BlockSpec(memory_space=pl.ANY)] * len(after),
        out_specs=[row(D_MODEL), row(D_MODEL), row(D_MODEL), full(1, D_MODEL)],
        out_shape=[jax.ShapeDtypeStruct((s, D_MODEL), F32), jax.ShapeDtypeStruct((s, D_MODEL), BF16),
                   jax.ShapeDtypeStruct((s, D_MODEL), BF16), jax.ShapeDtypeStruct((1, D_MODEL), F32)],
        compiler_params=_params(1),
    )(dh, gate, pb, w_ple, h, g, w_gate, *after)


def _mlp_bwd(dh, r, h, g, w_up, w_down, name):
    s = h.shape[0]
    t = _row_tile(s, MLP_BWD_TILE)
    nblk = D_FF // FF_BLOCK

    def body(dh_ref, r_ref, h_ref, g_ref, wu_ref, wd_ref, out_ref, dup_ref, dg_ref, dhb_ref):
        @pl.when(pl.program_id(0) == 0)
        def _():
            dg_ref[...] = jnp.zeros_like(dg_ref)

        d = dh_ref[...]
        db = d.astype(BF16)
        dhb_ref[...] = db
        back = None
        for b in range(nblk):
            cols = slice(b * FF_BLOCK, (b + 1) * FF_BLOCK)
            dup = (_dot_nt(db, wd_ref[b]) * (2.0 * r_ref[:, cols].astype(F32))).astype(BF16)
            dup_ref[:, cols] = dup
            part = _dot_nt(dup, wu_ref[b])
            back = part if back is None else back + part
        gv = g_ref[...]
        n, rstd, _ = _rms(h_ref[...], gv)
        dx, dg = _rms_bwd(back, n, rstd, gv)
        out_ref[...] = d + dx
        dg_ref[...] += dg

    row = lambda w: pl.BlockSpec((t, w), lambda i: (i, 0))
    vec = pl.BlockSpec((1, D_MODEL), lambda i: (0, 0))
    resident = lambda shape: pl.BlockSpec(shape, lambda i: (0, 0, 0), pipeline_mode=pl.Buffered(1))
    return pl.pallas_call(
        body, name=name, grid=(s // t,),
        in_specs=[row(D_MODEL), row(D_FF), row(D_MODEL), vec,
                  resident((nblk, D_MODEL, FF_BLOCK)), resident((nblk, FF_BLOCK, D_MODEL))],
        out_specs=[row(D_MODEL), row(D_FF), vec, row(D_MODEL)],
        out_shape=[jax.ShapeDtypeStruct((s, D_MODEL), F32), jax.ShapeDtypeStruct((s, D_FF), BF16),
                   jax.ShapeDtypeStruct((1, D_MODEL), F32), jax.ShapeDtypeStruct((s, D_MODEL), BF16)],
        compiler_params=_params(1),
    )(dh, r, h, g, w_up, w_down)


def _outproj_bwd(dh, w_out, o, lse, ones_bd, name):
    s = dh.shape[0]
    t = _row_tile(s, 512)

    def body(dh_ref, w_ref, o0, o1, o2, l0, l1, l2, bd_ref, dp_ref, do0, do1, do2, de0, de1, de2, dhb_ref, *stages):
        stages = _pair_stages(stages)
        dhb = dh_ref[...].astype(BF16)
        dhb_ref[...] = dhb
        da = _dot_nt(dhb, w_ref[...])
        dp_ref[...] = da[:, 0:POOL_WIDTH]
        ov =[_from_residues(r, stages[i], DILATIONS[i]) for i, r in enumerate((o0, o1, o2))]
        lv = [_from_residues(r, stages[3 + i], DILATIONS[i]) for i, r in enumerate((l0, l1, l2))]
        wts = _group_weights(*lv)
        bd = bd_ref[...]
        cbar = jnp.zeros((t, GROUP_WIDTH), F32)
        for grp, do_ref in enumerate((do0, do1, do2)):
            lo = POOL_WIDTH + grp * GROUP_WIDTH
            dag = da[:, lo:lo + GROUP_WIDTH]
            _to_residues(dag * wts[grp], stages[6 + grp], do_ref, DILATIONS[grp])
            prod = dag * ov[grp]
            hi = prod.astype(BF16)
            low = (prod - hi.astype(F32)).astype(BF16)
            cbar = cbar + wts[grp] * (_dot(hi, bd) + _dot(low, bd))
        for grp, de_ref in enumerate((de0, de1, de2)):
            _to_residues(wts[grp] * cbar, stages[9 + grp], de_ref, DILATIONS[grp])

    row = lambda w: pl.BlockSpec((t, w), lambda i: (i, 0))
    full = lambda a, b: pl.BlockSpec((a, b), lambda i: (0, 0))
    res = [_residue_spec(dil, t) for dil in DILATIONS]
    return pl.pallas_call(
        body, name=name, grid=(s // t,),
        in_specs=[row(D_MODEL), full(D_MODEL, D_MODEL)] + res + res + [full(GROUP_WIDTH, GROUP_WIDTH)],
        out_specs=[row(POOL_WIDTH)] + res + res + [row(D_MODEL)],
        out_shape=[jax.ShapeDtypeStruct((s, POOL_WIDTH), F32)] + [_residue_shape(dil, s, BF16) for dil in DILATIONS]
        + [_residue_shape(dil, s, F32) for dil in DILATIONS] + [jax.ShapeDtypeStruct((s, D_MODEL), BF16)],
        scratch_shapes=_stages(t, 12),
        compiler_params=_params(1),
    )(dh, w_out, *o, *lse, ones_bd)


def _attn_bwd(q, k, v, do, lse, deff, name, after=()):
    dil, length, _ = q.shape
    nb = length // ATTN_BLOCK
    qb = _blocks_per_step(nb)
    nj = nb // qb
    rs = _residues_per_step(dil, nb, qb)
    whole = nj == 1
    tail = slice((qb - 1) * ATTN_BLOCK, qb * ATTN_BLOCK)
    block = lambda qi: slice(qi * ATTN_BLOCK, (qi + 1) * ATTN_BLOCK)

    def body(q_ref, kp_ref, kc_ref, vp_ref, vc_ref, do_ref, lse_ref, de_ref, dq_ref, dk_ref, dv_ref, ck, cv):
        j = pl.program_id(1)

        def compute():
            masks = _head_masks()
            bias = _band_bias(j == 0)
            for rr in range(rs):
                dkc, dvc = [], []
                for qi in range(qb):
                    here, before = block(qi), block(qi - 1)
                    kcat = jnp.concatenate([kp_ref[rr] if qi == 0 else kc_ref[rr, before], kc_ref[rr, here]], axis=0)
                    vcat = jnp.concatenate([vp_ref[rr] if qi == 0 else vc_ref[rr, before], vc_ref[rr, here]], axis=0)
                    qs = _stack_heads(q_ref[rr, here], masks)
                    dos = _stack_heads(do_ref[rr, here], masks)
                    sc = _dot_nt(qs, kcat) + bias[min(qi, 1)]
                    p = jnp.exp(sc - _column_per_head(lse_ref[rr, here]))
                    ds = (p * (_dot_nt(dos, vcat) - _column_per_head(de_ref[rr, here]))).astype(BF16)
                    dq = jnp.zeros((ATTN_BLOCK, GROUP_WIDTH), F32)
                    for hd, msk in enumerate(masks):
                        dq = jnp.where(msk, _dot(ds[block(hd)], kcat), dq)
                    dq_ref[rr, here] = dq.astype(dq_ref.dtype)
                    dkc.append(_dot_tn(ds, qs))
                    dvc.append(_dot_tn(p.astype(BF16), dos))

                for out_ref, carry, parts in ((dk_ref, ck, dkc), (dv_ref, cv, dvc)):
                    full = [parts[qi][ATTN_BLOCK:] + parts[qi + 1][0:ATTN_BLOCK] for qi in range(qb - 1)]
                    if whole:
                        for qi, val in enumerate(full + [parts[qb - 1][ATTN_BLOCK:]]):
                            out_ref[rr, block(qi)] = val.astype(out_ref.dtype)
                        continue

                    @pl.when(j > 0)
                    def _():
                        if qb > 1:
                            out_ref[0, 0:(qb - 1) * ATTN_BLOCK] = carry[0:(qb - 1) * ATTN_BLOCK].astype(out_ref.dtype)
                        out_ref[0, tail] = (carry[tail] + parts[0][0:ATTN_BLOCK]).astype(out_ref.dtype)

                    for qi, val in enumerate(full):
                        carry[block(qi)] = val
                    carry[tail] = parts[qb - 1][ATTN_BLOCK:]

        if whole:
            compute()
        else:
            pl.when(j < nj)(compute)

            @pl.when(j == nj)
            def _():
                dk_ref[0] = ck[...].astype(dk_ref.dtype)
                dv_ref[0] = cv[...].astype(dv_ref.dtype)

    step = lambda j: jnp.minimum(j, nj - 1)
    cur = pl.BlockSpec((rs, qb * ATTN_BLOCK, GROUP_WIDTH), lambda r, j: (r, step(j), 0))
    prev = pl.BlockSpec((rs, ATTN_BLOCK, GROUP_WIDTH), lambda r, j: (r, jnp.maximum(qb * step(j) - 1, 0), 0))
    late = pl.BlockSpec((rs, qb * ATTN_BLOCK, GROUP_WIDTH), lambda r, j: (r, jnp.maximum(j - 1, 0), 0))
    return pl.pallas_call(
        _ordered_after(body, 8, after), name=name, grid=(dil // rs, 1 if whole else nj + 1),
        in_specs=[cur, prev, cur, prev, cur, cur, cur, cur] + [pl.BlockSpec(memory_space=pl.ANY)] * len(after),
        out_specs=[cur, cur if whole else late, cur if whole else late],
        out_shape=[jax.ShapeDtypeStruct(q.shape, BF16)] * 3,
        scratch_shapes=[pltpu.VMEM((qb * ATTN_BLOCK, GROUP_WIDTH), F32)] * 2,
        compiler_params=_params(2),
    )(q, k, k, v, v, do, lse, deff, *after)


def _pool_bwd(dpool, y, w_bd, scale, name, after=()):
    s = dpool.shape[0]
    t = _row_tile(s, 512)
    nt = s // t

    def body(dp_ref, y_ref, w_ref, sc_ref, du_ref, dw_ref, dsc_ref, ext, b2, b4, b8):
        i = pl.program_id(0)

        @pl.when(i == 0)
        def _():
            ext[t:, :] = jnp.zeros((POOL_HALO + POOL_PAD, POOL_WIDTH), F32)
            for buf in (b2, b4):
                buf[t + POOL_HALO:, :] = jnp.zeros((POOL_PAD, POOL_WIDTH), F32)
            dw_ref[...] = jnp.zeros_like(dw_ref)
            dsc_ref[...] = jnp.zeros_like(dsc_ref)

        dp = dp_ref[...]
        yb = y_ref[...]
        w = w_ref[...]
        dsc_ref[...] += jnp.sum(dp * _dot(yb, w), axis=0, keepdims=True)
        dyo = (dp * sc_ref[...]).astype(BF16)
        dw_ref[...] += _dot_tn(yb, dyo)
        dy = _dot_nt(dyo, w)
        win = _pool_lane_window()
        pos = (nt - 1 - i) * t + lax.broadcasted_iota(jnp.int32, (t, POOL_WIDTH), 0)
        gq = dy / jnp.minimum(pos + 1, win).astype(F32)
        ext[0:t, :] = gq
        du_ref[...] = _window_sums(ext, b2, b4, b8, t, 0, 0, 1) - dy
        ext[t:t + POOL_HALO, :] = gq[0:POOL_HALO, :]

    rev = pl.BlockSpec((t, POOL_WIDTH), lambda i: (nt - 1 - i, 0))
    full = lambda a, b: pl.BlockSpec((a, b), lambda i: (0, 0))
    return pl.pallas_call(
        _ordered_after(body, 4, after), name=name, grid=(nt,),
        in_specs=[rev, rev, full(POOL_WIDTH, POOL_WIDTH), full(1, POOL_WIDTH)]
        + [pl.BlockSpec(memory_space=pl.ANY)] * len(after),
        out_specs=[rev, full(POOL_WIDTH, POOL_WIDTH), full(1, POOL_WIDTH)],
        out_shape=[jax.ShapeDtypeStruct((s, POOL_WIDTH), F32), jax.ShapeDtypeStruct((POOL_WIDTH, POOL_WIDTH), F32),
                   jax.ShapeDtypeStruct((1, POOL_WIDTH), F32)],
        scratch_shapes=[pltpu.VMEM((t + POOL_HALO + POOL_PAD, POOL_WIDTH), F32)] * 4,
        compiler_params=_params(1),
    )(dpool, y, w_bd, scale, *after)


def _normproj_bwd(dh, du, dq, dk, dv, rc, rsa, rsb, w_in, h, g, name):
    s = h.shape[0]
    t = _row_tile(s, 512)

    def body(dh_ref, du_ref, q0, q1, q2, k0, k1, k2, v0, v1, v2, c_ref, sa_ref, sb_ref, w_ref, h_ref, g_ref,
             out_ref, dz_ref, dg_ref, *stages):
        @pl.when(pl.program_id(0) == 0)
        def _():
            dg_ref[...] = jnp.zeros_like(dg_ref)

        c, sa, sb = c_ref[...], sa_ref[...], sb_ref[...]

        def unrot(a, scale):
            halves = [_rot_t(a[:, hf * LANES:(hf + 1) * LANES] * scale, c, sa, sb) for hf in range(2)]
            return jnp.concatenate(halves, axis=1)

        staged = _pair_stages(stages)
        tok = lambda refs, base: [_from_residues(r, staged[base + i], DILATIONS[i]) for i, r in enumerate(refs)]
        chunks = [du_ref[...]]
        chunks += [unrot(a, HEAD_DIM ** -0.5) for a in tok((q0, q1, q2), 0)]
        chunks += [unrot(a, 1.0) for a in tok((k0, k1, k2), 3)]
        chunks += tok((v0, v1, v2), 6)
        acc = jnp.zeros((t, D_MODEL), F32)
        for ci, ch in enumerate(chunks):
            cols = slice(ci * GROUP_WIDTH, (ci + 1) * GROUP_WIDTH)
            cb = ch.astype(BF16)
            dz_ref[:, cols] = cb
            acc = acc + _dot_nt(cb, w_ref[:, cols])
        gv = g_ref[...]
        n, rstd, _ = _rms(h_ref[...], gv)
        dx, dg = _rms_bwd(acc, n, rstd, gv)
        out_ref[...] = dh_ref[...] + dx
        dg_ref[...] += dg

    row = lambda w: pl.BlockSpec((t, w), lambda i: (i, 0))
    vec = pl.BlockSpec((1, D_MODEL), lambda i: (0, 0))
    res = [_residue_spec(dil, t) for dil in DILATIONS]
    return pl.pallas_call(
        body, name=name, grid=(s // t,),
        in_specs=[row(D_MODEL), row(POOL_WIDTH)] + res * 3 + _table_specs(t)
        + [pl.BlockSpec((D_MODEL, N_IN), lambda i: (0, 0)), row(D_MODEL), vec],
        out_specs=[row(D_MODEL), row(N_IN), vec],
        out_shape=[jax.ShapeDtypeStruct((s, D_MODEL), F32), jax.ShapeDtypeStruct((s, N_IN), BF16),
                   jax.ShapeDtypeStruct((1, D_MODEL), F32)],
        scratch_shapes=_stages(t, 9),
        compiler_params=_params(1),
    )(dh, du, *dq, *dk, *dv, rc, rsa, rsb, w_in, h, g)


def _matmul_tn(a, b, name, *, square_a=False, tn=None, blocked_out=False, after=()):
    s, m = a.shape
    n = b.shape[1]
    tk = _row_tile(s, 2048)
    tm = min(m, 1024)
    tn = tn or min(n, 1024)
    assert m % tm == 0 and n % tn == 0
    nk = s // tk
    nsub = tn // FF_BLOCK if blocked_out else 1

    def body(a_ref, b_ref, o_ref, ob_ref, acc):
        k = pl.program_id(2)

        def product():
            av = a_ref[...]
            if square_a:
                av = av.astype(F32)
                av = av * av
            return _dot_tn(av.astype(BF16), b_ref[...].astype(BF16))

        def emit(total):
            if blocked_out:
                for sub in range(nsub):
                    cols = slice(sub * FF_BLOCK, (sub + 1) * FF_BLOCK)
                    o_ref[sub] = total[:, cols]
                    ob_ref[sub] = total[:, cols].astype(BF16)
            else:
                o_ref[...] = total
                ob_ref[...] = total.astype(BF16)

        if nk == 1:
            emit(product())
            return

        @pl.when(k == 0)
        def _():
            acc[...] = product()

        @pl.when((k > 0) & (k < nk - 1))
        def _():
            acc[...] += product()

        @pl.when(k == nk - 1)
        def _():
            emit(acc[...] + product())

    if blocked_out:
        shape = (n // FF_BLOCK, m, FF_BLOCK)
        out_spec = pl.BlockSpec((nsub, tm, FF_BLOCK), lambda i, j, k: (j, i, 0))
    else:
        shape = (m, n)
        out_spec = pl.BlockSpec((tm, tn), lambda i, j, k: (i, j))
    return pl.pallas_call(
        _ordered_after(body, 2, after), name=name, grid=(m // tm, n // tn, nk),
        in_specs=[pl.BlockSpec((tk, tm), lambda i, j, k: (k, i)), pl.BlockSpec((tk, tn), lambda i, j, k: (k, j))]
        + [pl.BlockSpec(memory_space=pl.ANY)] * len(after),
        out_specs=[out_spec, out_spec],
        out_shape=[jax.ShapeDtypeStruct(shape, F32), jax.ShapeDtypeStruct(shape, BF16)],
        scratch_shapes=[pltpu.VMEM((tm, tn), F32)],
        compiler_params=_params(3),
    )(a, b, *after)


def _adamw_math(w, g, m, v):
    m = ADAM_B1 * m + (1.0 - ADAM_B1) * g
    v = ADAM_B2 * v + (1.0 - ADAM_B2) * (g * g)
    m_hat = m / (1.0 - ADAM_B1 ** ADAM_STEP)
    v_hat = v / (1.0 - ADAM_B2 ** ADAM_STEP)
    delta = -ADAM_LR * (m_hat / (jnp.sqrt(v_hat) + ADAM_EPS) + ADAM_WD * w)
    return delta, m, v


def _adamw_sharded(w, m, v, own, recv0, recv1, name):
    _, rows, cols = w.shape
    t = _row_tile(rows, 256)

    def body(w_ref, m_ref, v_ref, own_ref, r0_ref, r1_ref, g_ref, d_ref, nm_ref, nv_ref):
        layer0 = pl.program_id(0) == 0
        g = own_ref[...]
        for k in range(N_DEV - 1):
            g = g + jnp.where(layer0, r0_ref[k], r1_ref[k]).astype(F32)
        g_ref[...] = g
        d_ref[...], nm_ref[...], nv_ref[...] = _adamw_math(w_ref[...], g, m_ref[...], v_ref[...])

    blk = pl.BlockSpec((None, t, cols), lambda l, i: (l, i, 0))
    recv = lambda layer: pl.BlockSpec((N_DEV - 1, t, cols), lambda l, i: (0, jnp.where(l == layer, i, 0), 0))
    return pl.pallas_call(
        body, name=name, grid=(2, rows // t),
        in_specs=[blk, blk, blk, blk, recv(0), recv(1)], out_specs=[blk] * 4,
        out_shape=[jax.ShapeDtypeStruct(w.shape, F32)] * 4,
        compiler_params=_params(2),
    )(w, m, v, own, recv0, recv1)


def _adamw_packed(w, g8, m, v, name):
    def body(w_ref, g_ref, m_ref, v_ref, go_ref, d_ref, nm_ref, nv_ref):
        g = g_ref[0]
        for dev in range(1, N_DEV):
            g = g + g_ref[dev]
        go_ref[...] = g
        d_ref[...], nm_ref[...], nv_ref[...] = _adamw_math(w_ref[...], g, m_ref[...], v_ref[...])

    return pl.pallas_call(
        body, name=name, out_shape=[jax.ShapeDtypeStruct(w.shape, F32)] * 4,
        compiler_params=pltpu.CompilerParams(vmem_limit_bytes=VMEM_LIMIT),
    )(w, g8, m, v)


def _peer(k):
    x, y, c = lax.axis_index("x"), lax.axis_index("y"), lax.axis_index("c")
    return (1 - x if k & 4 else x, 1 - y if k & 2 else y, 1 - c if k & 1 else c)


def _linear(dev):
    return 4 * dev[0] + 2 * dev[1] + dev[2]


HBM_SPEC = pl.BlockSpec(memory_space=pltpu.HBM)
SEM_SPEC = pl.BlockSpec(memory_space=pltpu.SEMAPHORE)
ANY_SPEC = pl.BlockSpec(memory_space=pl.ANY)
EFFECT = pltpu.SideEffectType.DATAFLOW_SIDE_EFFECTING


def _in_hbm(a):
    return pltpu.with_memory_space_constraint(a, pltpu.HBM)


class _Exchange:
    def __init__(self, name, groups, scatter, after=()):
        self.name, self.scatter = name, scatter
        self.sizes = sizes = [len(g) for g in groups]
        srcs = [a for g in groups for a in g]
        n, ng = len(srcs), len(groups)
        lead = (N_DEV - 1,) if scatter else (N_DEV,)
        shapes = [lead + (a.shape[1:] if scatter else a.shape) for a in srcs]
        lands = [lax.empty(sh, a.dtype) for sh, a in zip(shapes, srcs)]
        offsets = [sum(sizes[:gi]) for gi in range(ng)]
        copy = self._copy

        def body(*refs):
            src, land = refs[:n], refs[n:2 * n]
            sems = refs[2 * n + len(after):2 * n + len(after) + 2 * ng]
            token = refs[-1]
            for gi in range(ng):
                for wi in range(sizes[gi]):
                    w = offsets[gi] + wi
                    for k in range(1, N_DEV):
                        copy(src[w], land[w], sems[2 * gi], sems[2 * gi + 1], wi, k).start()
            token[...] = jnp.zeros_like(token)

        sem_shapes = [pltpu.SemaphoreType.DMA((7 * sz,)) for sz in sizes for _ in range(2)]
        outs = pl.pallas_call(
            body, name=name + "_start",
            in_specs=[HBM_SPEC] * (2 * n) + [ANY_SPEC] * len(after),
            out_specs=[SEM_SPEC] * (2 * ng) + [HBM_SPEC] * (2 * n) + [pl.BlockSpec(memory_space=pltpu.VMEM)],
            out_shape=sem_shapes + [pltpu.HBM(a.shape, a.dtype) for a in srcs + lands]
            + [jax.ShapeDtypeStruct((8, LANES), F32)],
            input_output_aliases={i: 2 * ng + i for i in range(2 * n)},
            compiler_params=pltpu.CompilerParams(has_side_effects=EFFECT),
        )(*[_in_hbm(a) for a in srcs + lands], *after)
        self.sems = [outs[2 * gi:2 * gi + 2] for gi in range(ng)]
        thru = outs[2 * ng:2 * ng + 2 * n]
        self.srcs = [thru[offsets[gi]:offsets[gi] + sizes[gi]] for gi in range(ng)]
        self.lands = [thru[n + offsets[gi]:n + offsets[gi] + sizes[gi]] for gi in range(ng)]
        self.token = outs[-1]

    def _copy(self, src, land, send_sems, recv_sems, wi, k):
        to = _peer(k)
        if self.scatter:
            src_ref, dst_ref = src.at[_linear(to)], land.at[k - 1]
        else:
            src_ref, dst_ref = src, land.at[_linear(_peer(0))]
        return pltpu.make_async_remote_copy(
            src_ref=src_ref, dst_ref=dst_ref, send_sem=send_sems.at[7 * wi + k - 1],
            recv_sem=recv_sems.at[7 * wi + k - 1], device_id=to, device_id_type=MESH)

    def wait(self, gi, after):
        n = self.sizes[gi]
        copy = self._copy

        def body(*refs):
            src, land = refs[:n], refs[n:2 * n]
            send_sems, recv_sems = refs[2 * n], refs[2 * n + 1]
            for wi in range(n):
                for k in range(1, N_DEV):
                    cp = copy(src[wi], land[wi], send_sems, recv_sems, wi, k)
                    cp.wait_send()
                    cp.wait_recv()

        arrays = list(self.srcs[gi]) + list(self.lands[gi])
        outs = pl.pallas_call(
            body, name=f"{self.name}_wait{gi}",
            in_specs=[HBM_SPEC] * (2 * n) + [SEM_SPEC, SEM_SPEC] + [ANY_SPEC] * len(after),
            out_specs=[HBM_SPEC] * (2 * n),
            out_shape=[pltpu.HBM(a.shape, a.dtype) for a in arrays],
            input_output_aliases={i: i for i in range(2 * n)},
            compiler_params=pltpu.CompilerParams(has_side_effects=EFFECT),
        )(*arrays, *self.sems[gi], *after)
        return outs[:n], outs[n:]


def _rotary_tables(positions):
    rot_dim = HEAD_DIM // 4
    inv_freq = ROPE_THETA ** (-jnp.arange(0, rot_dim, 2, dtype=F32) / rot_dim)
    ang = positions.astype(F32)[:, None] * inv_freq
    cs = jnp.concatenate([jnp.cos(ang), jnp.sin(ang)], axis=1)
    dim = jnp.arange(LANES) % HEAD_DIM
    first, second = dim < ROT_SHIFT, (dim >= ROT_SHIFT) & (dim < rot_dim)
    src = jnp.arange(2 * ROT_SHIFT)[:, None]
    angle = (dim % ROT_SHIFT)[None, :]
    c = jnp.where((first | second)[None, :] & (src == angle), 1.0, 0.0)
    sa = jnp.where(second[None, :] & (src == angle + ROT_SHIFT), 1.0, 0.0)
    sb = jnp.where(first[None, :] & (src == angle + ROT_SHIFT), -1.0, 0.0)
    spread = jnp.concatenate([c, sa, sb], axis=1).astype(F32)
    base = jnp.concatenate([jnp.where(first | second, 0.0, 1.0), jnp.zeros((2 * LANES,))]).astype(F32)[None, :]
    return jnp.dot(cs, spread, precision=lax.Precision.HIGHEST, preferred_element_type=F32) + base


def _block_diag(pool_w):
    gc = pool_w.shape[-1]
    out = jnp.zeros((POOL_WIDTH, POOL_WIDTH), pool_w.dtype)
    for grp in range(pool_w.shape[0]):
        out = lax.dynamic_update_slice(out, pool_w[grp], (grp * gc, grp * gc))
    return out


def _diag_blocks(a):
    gc = POOL_WIDTH // len(POOL_WINDOWS)
    return jnp.stack([a[grp * gc:(grp + 1) * gc, grp * gc:(grp + 1) * gc] for grp in range(len(POOL_WINDOWS))])


def _local_step(x, p, positions, loss_target, norm1, pool_w, pool_scale, norm2, norm3, final_norm, weights, send):
    rc = rsa = rsb = _rotary_tables(positions)
    ones_bd = _block_diag(jnp.ones((4, HEAD_DIM, HEAD_DIM), BF16))
    saved = []
    h = x
    for i in range(2):
        tag = f"_l{i}"
        g1, g2, g3 = norm1[i:i + 1], norm2[i:i + 1], norm3[i:i + 1]
        w_bd = _block_diag(pool_w[i]).astype(BF16)
        scale = pool_scale[i:i + 1]
        w_in = weights(i, "in", (h, rc, w_bd))
        hn1, u, *qkv = _normproj_fwd(h, g1, w_in, rc, rsa, rsb, "normproj_fwd" + tag)
        qkv = [qkv[3 * grp:3 * grp + 3] for grp in range(3)]
        started = weights(i, "prefetch", (hn1,))
        pool_out, y = _pool_fwd(u, w_bd, scale, "pool_fwd" + tag, after=started)
        o, lse = zip(*[_attn_fwd(*qkv[grp], f"attn_fwd{tag}_g{grp}", after=started) for grp in range(3)])
        w_out = weights(i, "out", (pool_out, *o))
        h1, a = _outproj_fwd(h, pool_out, o, lse, w_out, "outproj_fwd" + tag)
        w_up, w_down, w_gate, w_ple = weights(i, "rest", (h1,))
        h2, hn2, r = _mlp_fwd(h1, g2, w_up, w_down, "mlp_fwd" + tag)
        h3, hn3, gate, pb = _gate_fwd(h2, g3, w_gate, p, i, w_ple, "gate_fwd" + tag)
        saved.append(dict(h0=h, hn1=hn1, qkv=qkv, y=y, o=o, lse=lse, a=a, h1=h1, hn2=hn2, r=r, h2=h2,
                          hn3=hn3, gate=gate, pb=pb, w_bd=w_bd, scale=scale, g1=g1, g2=g2, g3=g3,
                          w_in=w_in, w_out=w_out, w_up=w_up, w_down=w_down, w_gate=w_gate, w_ple=w_ple))
        h = h3
    loss, dh, d_final = _loss_head(h, final_norm.reshape(1, D_MODEL), loss_target, "loss_head")

    grads = [None, None]
    sent = ()
    for i in (1, 0):
        tag = f"_l{i}"
        sv = saved[i]
        dh2, dgl, de, dg3 = _gate_bwd(dh, sv["gate"], sv["pb"], sv["w_ple"], sv["h2"], sv["g3"], sv["w_gate"],
                                      "gate_bwd" + tag, after=sent)
        dw_gate = _matmul_tn(sv["hn3"], dgl, "dw_gate" + tag)
        dw_ple = _matmul_tn(sv["pb"], de, "dw_ple" + tag)
        dh1, dup, dg2, dh2b = _mlp_bwd(dh2, sv["r"], sv["h1"], sv["g2"], sv["w_up"], sv["w_down"], "mlp_bwd" + tag)
        dw_down = _matmul_tn(sv["r"], dh2b, "dw_down" + tag, square_a=True)
        dw_up = _matmul_tn(sv["hn2"], dup, "dw_up" + tag, blocked_out=True)
        dpool, do0, do1, do2, de0, de1, de2, dh1b = _outproj_bwd(dh1, sv["w_out"], sv["o"], sv["lse"], ones_bd,
                                                                 "outproj_bwd" + tag)
        dw_out = _matmul_tn(sv["a"], dh1b, "dw_out" + tag)
        sent = send(i, "main", dict(w_gate=dw_gate, w_ple=dw_ple, w_down=dw_down, w_up=dw_up, w_out=dw_out))
        dqkv = [_attn_bwd(*sv["qkv"][grp], do_g, sv["lse"][grp], de_g, f"attn_bwd{tag}_g{grp}", after=sent)
                for grp, (do_g, de_g) in enumerate(((do0, de0), (do1, de1), (do2, de2)))]
        dq, dk, dv = zip(*dqkv)
        du, dw_bd, dscale = _pool_bwd(dpool, sv["y"], sv["w_bd"], sv["scale"], "pool_bwd" + tag, after=sent)
        dh, dz, dg1 = _normproj_bwd(dh1, du, dq, dk, dv, rc, rsa, rsb, sv["w_in"], sv["h0"], sv["g1"],
                                    "normproj_bwd" + tag)
        grads[i] = dict(norm1=dg1, norm2=dg2, norm3=dg3, pool_w=_diag_blocks(dw_bd), pool_scale=dscale)
        small_sent = send(0, "small", (grads, d_final, loss)) if i == 0 else ()
        dw_in = _matmul_tn(sv["hn1"], dz, "dw_in" + tag, tn=N_IN // 2, after=small_sent)
        sent = send(i, "in", dict(w_in=dw_in))
    return dh, sent


def _pack_small(norm1, norm2, norm3, final_norm, pool_scale, pool_w, spare=None):
    spare = jnp.zeros((1, LANES), F32) if spare is None else spare
    scale_row = jnp.concatenate([pool_scale.reshape(1, 2 * POOL_WIDTH), spare,
                                 jnp.zeros((1, D_MODEL - 2 * POOL_WIDTH - LANES), F32)], axis=1)
    return jnp.concatenate([norm1, norm2, norm3, final_norm.reshape(1, D_MODEL), scale_row,
                            pool_w.reshape(32, D_MODEL)], axis=0)


def _unpack_small(a):
    return dict(norm1=a[0:2], norm2=a[2:4], norm3=a[4:6], final_norm=a[6], pool_scale=a[7, 0:2 * POOL_WIDTH].reshape(2, POOL_WIDTH),
                pool_w=a[8:40].reshape(2, 4, HEAD_DIM, HEAD_DIM))


def _chunks_cols(a, cols):
    return a.reshape(a.shape[0], N_DEV, cols).transpose(1, 0, 2)


def _chunks_rows(a, rows):
    return a.reshape(N_DEV, rows, a.shape[1])


BIG = ("w_in", "w_out", "w_up", "w_down", "w_gate", "w_ple")
SMALL = ("norm1", "norm2", "norm3", "final_norm", "pool_scale", "pool_w")
ORDER = ("norm1", "w_in", "pool_w", "pool_scale", "w_out", "norm2", "w_up", "w_down", "norm3", "w_gate", "w_ple",
         "final_norm")


def kernel(x, p, positions, norm1, w_in, pool_w, pool_scale, w_out, norm2, w_up, w_down, norm3, w_gate, w_ple, final_norm, loss_target, m_norm1, m_w_in, m_pool_w, m_pool_scale, m_w_out, m_norm2, m_w_up, m_w_down, m_norm3, m_w_gate, m_w_ple, m_final_norm, v_norm1, v_w_in, v_pool_w, v_pool_scale, v_w_out, v_norm2, v_w_up, v_w_down, v_norm3, v_w_gate, v_w_ple, v_final_norm):
    w = dict(norm1=norm1, w_in=w_in, pool_w=pool_w, pool_scale=pool_scale, w_out=w_out, norm2=norm2, w_up=w_up,
             w_down=w_down, norm3=norm3, w_gate=w_gate, w_ple=w_ple, final_norm=final_norm)
    m = dict(norm1=m_norm1, w_in=m_w_in, pool_w=m_pool_w, pool_scale=m_pool_scale, w_out=m_w_out, norm2=m_norm2,
             w_up=m_w_up, w_down=m_w_down, norm3=m_norm3, w_gate=m_w_gate, w_ple=m_w_ple, final_norm=m_final_norm)
    v = dict(norm1=v_norm1, w_in=v_w_in, pool_w=v_pool_w, pool_scale=v_pool_scale, w_out=v_w_out, norm2=v_norm2,
             w_up=v_w_up, w_down=v_w_down, norm3=v_norm3, w_gate=v_w_gate, w_ple=v_w_ple, final_norm=v_final_norm)
    seq = x.shape[1]

    bf = {n: [w[n][layer].astype(BF16) for layer in range(2)] for n in BIG}
    rest = ("w_up", "w_down", "w_gate", "w_ple")
    me = 4 * lax.axis_index("x") + 2 * lax.axis_index("y") + lax.axis_index("c")
    gathers = [_Exchange("gather_l0", [[bf["w_in"][0]], [bf["w_out"][0]], [bf[n][0] for n in rest]], scatter=False)]
    unpack = dict(w_in=lambda a: a.transpose(1, 0, 2).reshape(D_MODEL, N_IN),
                  w_out=lambda a: a.reshape(D_MODEL, D_MODEL), w_gate=lambda a: a.reshape(D_MODEL, D_MODEL),
                  w_ple=lambda a: a.transpose(1, 0, 2).reshape(PLE_DIM, D_MODEL), w_up=lambda a: a, w_down=lambda a: a)
    parts = dict(zip(("in", "out", "rest"), (("w_in",), ("w_out",), rest)))

    def weights(layer, part, after):
        if part == "prefetch":
            if layer != 0:
                return ()
            gathers.append(_Exchange("gather_l1", [[bf[n][1] for n in parts[pt]] for pt in parts], scatter=False,
                                     after=after))
            return (gathers[1].token,)
        shards, lands = gathers[layer].wait(tuple(parts).index(part), after)
        full = [unpack[n](lax.dynamic_update_slice_in_dim(land, shard[None], me, axis=0))
                for n, shard, land in zip(parts[part], shards, lands)]
        return full if part == "rest" else full[0]

    to_chunks = dict(w_in=lambda a: _chunks_cols(a, N_IN // N_DEV), w_out=lambda a: _chunks_rows(a, D_MODEL // N_DEV),
                     w_up=lambda a: a, w_down=lambda a: _chunks_rows(a, FF_BLOCK),
                     w_gate=lambda a: _chunks_rows(a, D_MODEL // N_DEV), w_ple=lambda a: _chunks_cols(a, D_MODEL // N_DEV))
    own = {n: [None, None] for n in BIG}
    scatters = {}

    def own_chunk(n, g32):
        if n in ("w_in", "w_ple"):
            cols = g32.shape[1] // N_DEV
            return lax.dynamic_slice(g32, (0, me * cols), (g32.shape[0], cols))
        return lax.dynamic_index_in_dim(to_chunks[n](g32), me, axis=0, keepdims=False)

    def send(layer, part, grads):
        if part == "small":
            per_layer, d_final, loss = grads
            pack = _pack_small(
                *[jnp.concatenate([per_layer[0][n], per_layer[1][n]], axis=0) for n in ("norm1", "norm2", "norm3")],
                d_final.reshape(D_MODEL),
                jnp.concatenate([per_layer[0]["pool_scale"], per_layer[1]["pool_scale"]], axis=0),
                jnp.stack([per_layer[0]["pool_w"], per_layer[1]["pool_w"]]), spare=loss)
            scatters["small"] = _Exchange("gather_small", [[pack]], scatter=False)
            return (scatters["small"].token,)
        for n, (g32, _) in grads.items():
            own[n][layer] = own_chunk(n, g32)
        ex = _Exchange(f"scatter_{part}_l{layer}", [[to_chunks[n](g16) for n, (_, g16) in grads.items()]], scatter=True)
        scatters[layer, part] = (tuple(grads), ex)
        return (ex.token,)

    dx, sent = _local_step(
        x.reshape(seq, D_MODEL), p.reshape(2, seq, PLE_DIM), positions.reshape(seq), loss_target.reshape(seq, D_MODEL),
        norm1, pool_w, pool_scale, norm2, norm3, final_norm, weights, send)

    g_out, d_out, m_out, v_out = {}, {}, {}, {}
    for part in ("main", "in"):
        recv = {}
        for layer in (1, 0):
            names, ex = scatters[layer, part]
            for n, r in zip(names, ex.wait(0, sent)[1]):
                recv[n, layer] = r
        for n in names:
            g_out[n], d_out[n], m_out[n], v_out[n] = _adamw_sharded(
                w[n], m[n], v[n], jnp.stack(own[n]), recv[n, 0], recv[n, 1], "adamw_" + n)
        sent = tuple(d_out[n] for n in names)
    (mine,), (landed,) = scatters["small"].wait(0, sent)
    small_g8 = lax.dynamic_update_slice_in_dim(landed, mine[None], me, axis=0)
    pack = lambda t: _pack_small(*[t[n] for n in SMALL])
    small_g, d_small, m_small, v_small = _adamw_packed(pack(w), small_g8, pack(m), pack(v), "adamw_small")
    for dst, a in ((g_out, small_g), (d_out, d_small), (m_out, m_small), (v_out, v_small)):
        dst.update(_unpack_small(a))

    return (small_g[7, 2 * POOL_WIDTH],dx.reshape(1, seq, D_MODEL), *[g_out[n] for n in ORDER], *[d_out[n] for n in ORDER],
            *[m_out[n] for n in ORDER], *[v_out[n] for n in ORDER])
```

```python
import functools

import jax
import jax.numpy as jnp
from jax import lax
from jax.experimental import pallas as pl
from jax.experimental.pallas import tpu as pltpu

F32 = jnp.float32
BF16 = jnp.bfloat16

D_MODEL = 1024
HEAD_DIM = 64
POOL_WIDTH = 256
POOL_WINDOWS = (2, 4, 8, 16)
POOL_HALO = 16
POOL_PAD = 8
GROUP_WIDTH = 256
DILATIONS = (1, 4, 16)
ATTN_BLOCK = 128
ROT_SHIFT = 8
ROPE_THETA = 500000.0
D_FF = 4096
FF_BLOCK = 512
FF_PER_STEP = 2
MLP_BWD_TILE = 512
N_DEV = 8
N_IN = POOL_WIDTH + 3 * 768
PLE_DIM = 256
EPS = 1e-6
NEG_BIG = -1e30

ADAM_LR = 0.001
ADAM_B1 = 0.9
ADAM_B2 = 0.999
ADAM_EPS = 1e-08
ADAM_WD = 0.01
ADAM_STEP = 10

LANES = 128
VMEM_LIMIT = 56 * 1024 * 1024
MESH = pl.DeviceIdType.MESH


def _params(n_grid):
    return pltpu.CompilerParams(dimension_semantics=("arbitrary",) * n_grid, vmem_limit_bytes=VMEM_LIMIT)


def _dot(a, b):
    return jnp.dot(a, b, preferred_element_type=F32)


def _dot_nt(a, b):
    return lax.dot_general(a, b, (((1,), (1,)), ((), ())), preferred_element_type=F32)


def _dot_tn(a, b):
    return lax.dot_general(a, b, (((0,), (0,)), ((), ())), preferred_element_type=F32)


def _rms(x, g):
    rstd = lax.rsqrt(jnp.mean(x * x, axis=-1, keepdims=True) + EPS)
    n = x * rstd
    return n, rstd, n * g


def _rms_bwd(dy, n, rstd, g):
    dyn = dy * g
    dx = rstd * (dyn - n * jnp.mean(dyn * n, axis=-1, keepdims=True))
    return dx, jnp.sum(dy * n, axis=0, keepdims=True)


def _ordered_after(body, n_in, after):
    if not after:
        return body
    return lambda *refs: body(*refs[:n_in], *refs[n_in + len(after):])


def _row_tile(s, t):
    t = min(s, t)
    assert s % t == 0
    return t


def _rot(z, c, sa, sb):
    return z * c + pltpu.roll(z, ROT_SHIFT, 1) * sa + pltpu.roll(z, LANES - ROT_SHIFT, 1) * sb


def _table_specs(t):
    return [pl.BlockSpec((t, LANES), functools.partial(lambda i, k: (i, k), k=k)) for k in range(3)]


def _rot_t(dz, c, sa, sb):
    return dz * c + pltpu.roll(dz * sa, LANES - ROT_SHIFT, 1) + pltpu.roll(dz * sb, ROT_SHIFT, 1)


def _to_residues(value, stage, out_ref, dil):
    if dil == 1:
        out_ref[0] = value.astype(out_ref.dtype)
        return
    rows = value.shape[0] // dil
    for hf in range(GROUP_WIDTH // LANES):
        lanes = slice(hf * LANES, (hf + 1) * LANES)
        stage[hf][...] = value[:, lanes]
        for r in range(dil):
            out_ref[r, :, lanes] = stage[hf][pl.ds(r, rows, stride=dil), :].astype(out_ref.dtype)


def _from_residues(in_ref, stage, dil):
    if dil == 1:
        return in_ref[0].astype(F32)
    rows = in_ref.shape[1]
    for hf in range(GROUP_WIDTH // LANES):
        for r in range(dil):
            stage[hf][pl.ds(r, rows, stride=dil), :] = in_ref[r, :, hf * LANES:(hf + 1) * LANES].astype(F32)
    return jnp.concatenate([stage[0][...], stage[1][...]], axis=1)


def _residue_spec(dil, t):
    return pl.BlockSpec((dil, t // dil, GROUP_WIDTH), lambda i: (0, i, 0))


def _residue_shape(dil, s, dtype):
    return jax.ShapeDtypeStruct((dil, s // dil, GROUP_WIDTH), dtype)


def _stages(t, n):
    return [pltpu.VMEM((t, LANES), F32)] * (n * (GROUP_WIDTH // LANES))


def _pair_stages(refs):
    return [refs[i:i + 2] for i in range(0, len(refs), 2)]


def _normproj_fwd(h, g, w_in, rc, rsa, rsb, name):
    s = h.shape[0]
    t = _row_tile(s, 512)

    def body(h_ref, g_ref, w_ref, c_ref, sa_ref, sb_ref, hn_ref, u_ref, *rest):
        qkv_refs, stages = rest[:9], _pair_stages(rest[9:])
        _, _, hn = _rms(h_ref[...], g_ref[...])
        hb = hn.astype(BF16)
        hn_ref[...] = hb
        c, sa, sb = c_ref[...], sa_ref[...], sb_ref[...]

        def rot(z, scale):
            halves = [_rot(z[:, hf * LANES:(hf + 1) * LANES], c, sa, sb) * scale for hf in range(2)]
            return jnp.concatenate(halves, axis=1)

        u_ref[...] = _dot(hb, w_ref[:, 0:POOL_WIDTH])
        for grp, dil in enumerate(DILATIONS):
            lo = POOL_WIDTH + grp * GROUP_WIDTH
            q_ref, k_ref, v_ref = qkv_refs[3 * grp:3 * grp + 3]
            _to_residues(rot(_dot(hb, w_ref[:, lo:lo + GROUP_WIDTH]), HEAD_DIM ** -0.5), stages[0], q_ref, dil)
            _to_residues(rot(_dot(hb, w_ref[:, lo + 768:lo + 768 + GROUP_WIDTH]), 1.0), stages[1], k_ref, dil)
            _to_residues(_dot(hb, w_ref[:, lo + 1536:lo + 1536 + GROUP_WIDTH]), stages[2], v_ref, dil)

    row = lambda w: pl.BlockSpec((t, w), lambda i: (i, 0))
    return pl.pallas_call(
        body, name=name, grid=(s // t,),
        in_specs=[row(D_MODEL), pl.BlockSpec((1, D_MODEL), lambda i: (0, 0)),
                  pl.BlockSpec((D_MODEL, N_IN), lambda i: (0, 0))] + _table_specs(t),
        out_specs=[row(D_MODEL), row(POOL_WIDTH)] + [_residue_spec(dil, t) for dil in DILATIONS for _ in range(3)],
        out_shape=[jax.ShapeDtypeStruct((s, D_MODEL), BF16), jax.ShapeDtypeStruct((s, POOL_WIDTH), F32)]
        + [_residue_shape(dil, s, BF16) for dil in DILATIONS for _ in range(3)],
        scratch_shapes=_stages(t, 3),
        compiler_params=_params(1),
    )(h, g, w_in, rc, rsa, rsb)


def _pool_lane_window():
    lane = lax.broadcasted_iota(jnp.int32, (1, POOL_WIDTH), 1)
    return jnp.left_shift(2, lane // (POOL_WIDTH // len(POOL_WINDOWS)))


def _window_sums(ext, b2, b4, b8, t, lo, tile, direction):
    rows = t + POOL_HALO
    for src, dst, sh in ((ext, b2, 1), (b2, b4, 2), (b4, b8, 4)):
        dst[lo:lo + rows, :] = src[lo:lo + rows, :] + src[lo + direction * sh:lo + direction * sh + rows, :]
    s16 = b8[tile:tile + t, :] + b8[tile + direction * 8:tile + direction * 8 + t, :]
    win = _pool_lane_window()
    return jnp.where(win == 2, b2[tile:tile + t, :],
                     jnp.where(win == 4, b4[tile:tile + t, :], jnp.where(win == 8, b8[tile:tile + t, :], s16)))


def _pool_fwd(u, w_bd, scale, name, after=()):
    s = u.shape[0]
    t = _row_tile(s, 512)
    first = POOL_PAD + POOL_HALO

    def body(u_ref, w_ref, sc_ref, out_ref, y_ref, ext, b2, b4, b8):
        i = pl.program_id(0)

        @pl.when(i == 0)
        def _():
            for buf in (ext, b2, b4):
                buf[0:POOL_PAD, :] = jnp.zeros((POOL_PAD, POOL_WIDTH), F32)
            ext[POOL_PAD:first, :] = jnp.zeros((POOL_HALO, POOL_WIDTH), F32)

        x = u_ref[...]
        ext[first:, :] = x
        wsum = _window_sums(ext, b2, b4, b8, t, POOL_PAD, first, -1)
        pos = i * t + lax.broadcasted_iota(jnp.int32, (t, POOL_WIDTH), 0)
        cnt = jnp.minimum(pos + 1, _pool_lane_window()).astype(F32)
        y = wsum / cnt - x
        yb = y.astype(BF16)
        y_ref[...] = yb
        out_ref[...] = _dot(yb, w_ref[...]) * sc_ref[...]
        ext[POOL_PAD:first, :] = x[t - POOL_HALO:, :]

    row = pl.BlockSpec((t, POOL_WIDTH), lambda i: (i, 0))
    return pl.pallas_call(
        _ordered_after(body, 3, after), name=name, grid=(s // t,),
        in_specs=[row, pl.BlockSpec((POOL_WIDTH, POOL_WIDTH), lambda i: (0, 0)),
                  pl.BlockSpec((1, POOL_WIDTH), lambda i: (0, 0))] + [pl.BlockSpec(memory_space=pl.ANY)] * len(after),
        out_specs=[row, row],
        out_shape=[jax.ShapeDtypeStruct((s, POOL_WIDTH), F32), jax.ShapeDtypeStruct((s, POOL_WIDTH), BF16)],
        scratch_shapes=[pltpu.VMEM((t + POOL_HALO + POOL_PAD, POOL_WIDTH), F32)] * 4,
        compiler_params=_params(1),
    )(u, w_bd, scale, *after)


def _head_masks():
    lane = lax.broadcasted_iota(jnp.int32, (ATTN_BLOCK, GROUP_WIDTH), 1)
    return [lane // HEAD_DIM == hd for hd in range(GROUP_WIDTH // HEAD_DIM)]


def _stack_heads(a, masks):
    zero = jnp.zeros_like(a)
    return jnp.concatenate([jnp.where(m, a, zero) for m in masks], axis=0)


def _band_bias(first_step):
    rows = ATTN_BLOCK * (GROUP_WIDTH // HEAD_DIM)
    i = lax.broadcasted_iota(jnp.int32, (rows, 2 * ATTN_BLOCK), 0) & (ATTN_BLOCK - 1)
    j = lax.broadcasted_iota(jnp.int32, (rows, 2 * ATTN_BLOCK), 1)
    inner = jnp.where((j >= i) & (j <= i + ATTN_BLOCK), 0.0, NEG_BIG)
    return jnp.where((j < ATTN_BLOCK) & first_step, NEG_BIG, inner), inner


def _column_per_head(a):
    return jnp.concatenate([a[:, hd * HEAD_DIM:hd * HEAD_DIM + 1] for hd in range(GROUP_WIDTH // HEAD_DIM)], axis=0)


def _blocks_per_step(nb):
    return 8 if nb % 8 == 0 else 4 if nb % 4 == 0 else 2 if nb % 2 == 0 else 1


def _residues_per_step(dil, nb, qb):
    return 2 if (nb == qb and qb < 8 and dil % 2 == 0) else 1


def _attn_fwd(q, k, v, name, after=()):
    dil, length, _ = q.shape
    nb = length // ATTN_BLOCK
    qb = _blocks_per_step(nb)
    rs = _residues_per_step(dil, nb, qb)

    def body(q_ref, kp_ref, kc_ref, vp_ref, vc_ref, o_ref, lse_ref):
        masks = _head_masks()
        bias = _band_bias(pl.program_id(1) == 0)
        for rr in range(rs):
            for qi in range(qb):
                here = slice(qi * ATTN_BLOCK, (qi + 1) * ATTN_BLOCK)
                before = slice((qi - 1) * ATTN_BLOCK, qi * ATTN_BLOCK)
                kcat = jnp.concatenate([kp_ref[rr] if qi == 0 else kc_ref[rr, before], kc_ref[rr, here]], axis=0)
                vcat = jnp.concatenate([vp_ref[rr] if qi == 0 else vc_ref[rr, before], vc_ref[rr, here]], axis=0)
                qs = _stack_heads(q_ref[rr, here], masks)
                sc = _dot_nt(qs, kcat) + bias[min(qi, 1)]
                m = jnp.max(sc, axis=1, keepdims=True)
                e = jnp.exp(sc - m)
                l = jnp.sum(e, axis=1, keepdims=True)
                p = (e / l).astype(BF16)
                lse = m + jnp.log(l)
                o = jnp.zeros((ATTN_BLOCK, GROUP_WIDTH), F32)
                lse_full = jnp.zeros((ATTN_BLOCK, GROUP_WIDTH), F32)
                for hd, msk in enumerate(masks):
                    rows = slice(hd * ATTN_BLOCK, (hd + 1) * ATTN_BLOCK)
                    o = jnp.where(msk, _dot(p[rows], vcat), o)
                    lse_full = jnp.where(msk, lse[rows], lse_full)
                o_ref[rr, here] = o.astype(o_ref.dtype)
                lse_ref[rr, here] = lse_full

    cur = pl.BlockSpec((rs, qb * ATTN_BLOCK, GROUP_WIDTH), lambda r, j: (r, j, 0))
    prev = pl.BlockSpec((rs, ATTN_BLOCK, GROUP_WIDTH), lambda r, j: (r, jnp.maximum(qb * j - 1, 0), 0))
    return pl.pallas_call(
        _ordered_after(body, 5, after), name=name, grid=(dil // rs, nb // qb),
        in_specs=[cur, prev, cur, prev, cur] + [pl.BlockSpec(memory_space=pl.ANY)] * len(after), out_specs=[cur, cur],
        out_shape=[jax.ShapeDtypeStruct(q.shape, BF16), jax.ShapeDtypeStruct(q.shape, F32)],
        compiler_params=_params(2),
    )(q, k, k, v, v, *after)


def _group_weights(l0, l1, l2):
    m = jnp.maximum(jnp.maximum(l0, l1), l2)
    e0, e1, e2 = jnp.exp(l0 - m), jnp.exp(l1 - m), jnp.exp(l2 - m)
    den = e0 + e1 + e2
    return e0 / den, e1 / den, e2 / den


def _outproj_fwd(h, pool_out, o, lse, w_out, name):
    s = h.shape[0]
    t = _row_tile(s, 512)

    def body(h_ref, po_ref, o0, o1, o2, l0, l1, l2, w_ref, out_ref, a_ref, *stages):
        stages = _pair_stages(stages)
        ov =[_from_residues(r, stages[i], DILATIONS[i]) for i, r in enumerate((o0, o1, o2))]
        lv = [_from_residues(r, stages[3 + i], DILATIONS[i]) for i, r in enumerate((l0, l1, l2))]
        wts = _group_weights(*lv)
        a = jnp.concatenate([po_ref[...]] + [ov[i] * wts[i] for i in range(3)], axis=1).astype(BF16)
        a_ref[...] = a
        out_ref[...] = h_ref[...] + _dot(a, w_ref[...])

    row = lambda w: pl.BlockSpec((t, w), lambda i: (i, 0))
    res = [_residue_spec(dil, t) for dil in DILATIONS]
    return pl.pallas_call(
        body, name=name, grid=(s // t,),
        in_specs=[row(D_MODEL), row(POOL_WIDTH)] + res + res + [pl.BlockSpec((D_MODEL, D_MODEL), lambda i: (0, 0))],
        out_specs=[row(D_MODEL), row(D_MODEL)],
        out_shape=[jax.ShapeDtypeStruct((s, D_MODEL), F32), jax.ShapeDtypeStruct((s, D_MODEL), BF16)],
        scratch_shapes=_stages(t, 6),
        compiler_params=_params(1),
    )(h, pool_out, *o, *lse, w_out)


def _mlp_fwd(h, g, w_up, w_down, name):
    s = h.shape[0]
    t = _row_tile(s, 512)
    nblk = D_FF // FF_BLOCK

    def body(h_ref, g_ref, wu_ref, wd_ref, out_ref, hn_ref, r_ref):
        x = h_ref[...]
        _, _, hn = _rms(x, g_ref[...])
        hb = hn.astype(BF16)
        hn_ref[...] = hb
        acc = None
        for b0 in range(0, nblk, FF_PER_STEP):
            acts = []
            for b in range(b0, b0 + FF_PER_STEP):
                r = jnp.maximum(_dot(hb, wu_ref[b]), 0.0)
                r_ref[:, b * FF_BLOCK:(b + 1) * FF_BLOCK] = r.astype(BF16)
                acts.append((r * r).astype(BF16))
            wd = wd_ref[b0:b0 + FF_PER_STEP].reshape(FF_PER_STEP * FF_BLOCK, D_MODEL)
            part = _dot(jnp.concatenate(acts, axis=1), wd)
            acc = part if acc is None else acc + part
        out_ref[...] = x + acc

    row = lambda w: pl.BlockSpec((t, w), lambda i: (i, 0))
    resident = lambda shape: pl.BlockSpec(shape, lambda i: (0, 0, 0), pipeline_mode=pl.Buffered(1))
    return pl.pallas_call(
        body, name=name, grid=(s // t,),
        in_specs=[row(D_MODEL), pl.BlockSpec((1, D_MODEL), lambda i: (0, 0)),
                  resident((nblk, D_MODEL, FF_BLOCK)), resident((nblk, FF_BLOCK, D_MODEL))],
        out_specs=[row(D_MODEL), row(D_MODEL), row(D_FF)],
        out_shape=[jax.ShapeDtypeStruct((s, D_MODEL), F32), jax.ShapeDtypeStruct((s, D_MODEL), BF16),
                   jax.ShapeDtypeStruct((s, D_FF), BF16)],
        compiler_params=_params(1),
    )(h, g, w_up, w_down)


def _gate_fwd(h, g, w_gate, p, layer, w_ple, name):
    s = h.shape[0]
    t = _row_tile(s, 512)

    def body(h_ref, g_ref, wg_ref, p_ref, wp_ref, out_ref, hn_ref, gate_ref, pb_ref):
        x = h_ref[...]
        _, _, hn = _rms(x, g_ref[...])
        hb = hn.astype(BF16)
        hn_ref[...] = hb
        gate = 1.0 / (1.0 + jnp.exp(-_dot(hb, wg_ref[...])))
        pb = p_ref[...].astype(BF16)
        pb_ref[...] = pb
        gate_ref[...] = gate.astype(BF16)
        out_ref[...] = x + gate * _dot(pb, wp_ref[...])

    row = lambda w: pl.BlockSpec((t, w), lambda i: (i, 0))
    full = lambda a, b: pl.BlockSpec((a, b), lambda i: (0, 0))
    return pl.pallas_call(
        body, name=name, grid=(s // t,),
        in_specs=[row(D_MODEL), full(1, D_MODEL), full(D_MODEL, D_MODEL),
                  pl.BlockSpec((None, t, PLE_DIM), lambda i: (layer, i, 0)), full(PLE_DIM, D_MODEL)],
        out_specs=[row(D_MODEL), row(D_MODEL), row(D_MODEL), row(PLE_DIM)],
        out_shape=[jax.ShapeDtypeStruct((s, D_MODEL), F32), jax.ShapeDtypeStruct((s, D_MODEL), BF16),
                   jax.ShapeDtypeStruct((s, D_MODEL), BF16), jax.ShapeDtypeStruct((s, PLE_DIM), BF16)],
        compiler_params=_params(1),
    )(h, g, w_gate, p, w_ple)


def _loss_head(h, g, target, name):
    s = h.shape[0]
    t = _row_tile(s, 512)

    def body(h_ref, g_ref, t_ref, loss_ref, dh_ref, dg_ref):
        i = pl.program_id(0)

        @pl.when(i == 0)
        def _():
            loss_ref[...] = jnp.zeros_like(loss_ref)
            dg_ref[...] = jnp.zeros_like(dg_ref)

        gv = g_ref[...]
        n, rstd, y = _rms(h_ref[...], gv)
        err = y - t_ref[...]
        loss_ref[...] += jnp.sum(err * err) * (0.5 / D_MODEL)
        dx, dg = _rms_bwd(err * (1.0 / D_MODEL), n, rstd, gv)
        dh_ref[...] = dx
        dg_ref[...] += dg

    row = pl.BlockSpec((t, D_MODEL), lambda i: (i, 0))
    vec = pl.BlockSpec((1, D_MODEL), lambda i: (0, 0))
    return pl.pallas_call(
        body, name=name, grid=(s // t,),
        in_specs=[row, vec, row],
        out_specs=[pl.BlockSpec((1, LANES), lambda i: (0, 0)), row, vec],
        out_shape=[jax.ShapeDtypeStruct((1, LANES), F32), jax.ShapeDtypeStruct((s, D_MODEL), F32),
                   jax.ShapeDtypeStruct((1, D_MODEL), F32)],
        compiler_params=_params(1),
    )(h, g, target)


def _gate_bwd(dh, gate, pb, w_ple, h, g, w_gate, hn, name, after=()):
    s = h.shape[0]
    t = _row_tile(s, 512)
    last = s // t - 1

    def body(dh_ref, gate_ref, pb_ref, wp_ref, h_ref, g_ref, wg_ref, hn_ref, out_ref, dg_ref, dwg_ref, dwgb_ref,
             dwp_ref, dwpb_ref):
        i = pl.program_id(0)

        @pl.when(i == 0)
        def _():
            dg_ref[...] = jnp.zeros_like(dg_ref)
            dwg_ref[...] = jnp.zeros_like(dwg_ref)
            dwp_ref[...] = jnp.zeros_like(dwp_ref)

        d = dh_ref[...]
        gate = gate_ref[...].astype(F32)
        pb = pb_ref[...]
        e = _dot(pb, wp_ref[...])
        dgl = (d * e * gate * (1.0 - gate)).astype(BF16)
        dwg_ref[...] += _dot_tn(hn_ref[...], dgl)
        dwp_ref[...] += _dot_tn(pb, (d * gate).astype(BF16))
        gv = g_ref[...]
        n, rstd, _ = _rms(h_ref[...], gv)
        dx, dg = _rms_bwd(_dot_nt(dgl, wg_ref[...]), n, rstd, gv)
        out_ref[...] = d + dx
        dg_ref[...] += dg

        @pl.when(i == last)
        def _():
            dwgb_ref[...] = dwg_ref[...].astype(BF16)
            dwpb_ref[...] = dwp_ref[...].astype(BF16)

    row = lambda w: pl.BlockSpec((t, w), lambda i: (i, 0))
    full = lambda a, b: pl.BlockSpec((a, b), lambda i: (0, 0))
    dh2, dg, dwg, dwgb, dwp, dwpb = pl.pallas_call(
        _ordered_after(body, 8, after), name=name, grid=(s // t,),
        in_specs=[row(D_MODEL), row(D_MODEL), row(PLE_DIM), full(PLE_DIM, D_MODEL), row(D_MODEL), full(1, D_MODEL),
                  full(D_MODEL, D_MODEL), row(D_MODEL)] + [pl.BlockSpec(memory_space=pl.ANY)] * len(after),
        out_specs=[row(D_MODEL), full(1, D_MODEL), full(D_MODEL, D_MODEL), full(D_MODEL, D_MODEL),
                   full(PLE_DIM, D_MODEL), full(PLE_DIM, D_MODEL)],
        out_shape=[jax.ShapeDtypeStruct((s, D_MODEL), F32), jax.ShapeDtypeStruct((1, D_MODEL), F32),
                   jax.ShapeDtypeStruct((D_MODEL, D_MODEL), F32), jax.ShapeDtypeStruct((D_MODEL, D_MODEL), BF16),
                   jax.ShapeDtypeStruct((PLE_DIM, D_MODEL), F32), jax.ShapeDtypeStruct((PLE_DIM, D_MODEL), BF16)],
        compiler_params=_params(1),
    )(dh, gate, pb, w_ple, h, g, w_gate, hn, *after)
    return dh2, dg, (dwg, dwgb), (dwp, dwpb)


def _mlp_bwd(dh, r, h, g, w_up, w_down, name):
    s = h.shape[0]
    t = _row_tile(s, MLP_BWD_TILE)
    nblk = D_FF // FF_BLOCK

    def body(dh_ref, r_ref, h_ref, g_ref, wu_ref, wd_ref, out_ref, dup_ref, dg_ref, dhb_ref):
        @pl.when(pl.program_id(0) == 0)
        def _():
            dg_ref[...] = jnp.zeros_like(dg_ref)

        d = dh_ref[...]
        db = d.astype(BF16)
        dhb_ref[...] = db
        back = None
        for b in range(nblk):
            cols = slice(b * FF_BLOCK, (b + 1) * FF_BLOCK)
            dup = (_dot_nt(db, wd_ref[b]) * (2.0 * r_ref[:, cols].astype(F32))).astype(BF16)
            dup_ref[:, cols] = dup
            part = _dot_nt(dup, wu_ref[b])
            back = part if back is None else back + part
        gv = g_ref[...]
        n, rstd, _ = _rms(h_ref[...], gv)
        dx, dg = _rms_bwd(back, n, rstd, gv)
        out_ref[...] = d + dx
        dg_ref[...] += dg

    row = lambda w: pl.BlockSpec((t, w), lambda i: (i, 0))
    vec = pl.BlockSpec((1, D_MODEL), lambda i: (0, 0))
    resident = lambda shape: pl.BlockSpec(shape, lambda i: (0, 0, 0), pipeline_mode=pl.Buffered(1))
    return pl.pallas_call(
        body, name=name, grid=(s // t,),
        in_specs=[row(D_MODEL), row(D_FF), row(D_MODEL), vec,
                  resident((nblk, D_MODEL, FF_BLOCK)), resident((nblk, FF_BLOCK, D_MODEL))],
        out_specs=[row(D_MODEL), row(D_FF), vec, row(D_MODEL)],
        out_shape=[jax.ShapeDtypeStruct((s, D_MODEL), F32), jax.ShapeDtypeStruct((s, D_FF), BF16),
                   jax.ShapeDtypeStruct((1, D_MODEL), F32), jax.ShapeDtypeStruct((s, D_MODEL), BF16)],
        compiler_params=_params(1),
    )(dh, r, h, g, w_up, w_down)


def _outproj_bwd(dh, w_out, o, lse, ones_bd, a, name):
    s = dh.shape[0]
    t = _row_tile(s, 512)
    last = s // t - 1

    def body(dh_ref, w_ref, o0, o1, o2, l0, l1, l2, bd_ref, a_ref, dp_ref, do0, do1, do2, de0, de1, de2, dw_ref,
             dwb_ref, *stages):
        i = pl.program_id(0)

        @pl.when(i == 0)
        def _():
            dw_ref[...] = jnp.zeros_like(dw_ref)

        stages = _pair_stages(stages)
        dhb = dh_ref[...].astype(BF16)
        dw_ref[...] += _dot_tn(a_ref[...], dhb)

        @pl.when(i == last)
        def _():
            dwb_ref[...] = dw_ref[...].astype(BF16)

        da = _dot_nt(dhb, w_ref[...])
        dp_ref[...] = da[:, 0:POOL_WIDTH]
        ov =[_from_residues(r, stages[i], DILATIONS[i]) for i, r in enumerate((o0, o1, o2))]
        lv = [_from_residues(r, stages[3 + i], DILATIONS[i]) for i, r in enumerate((l0, l1, l2))]
        wts = _group_weights(*lv)
        bd = bd_ref[...]
        cbar = jnp.zeros((t, GROUP_WIDTH), F32)
        for grp, do_ref in enumerate((do0, do1, do2)):
            lo = POOL_WIDTH + grp * GROUP_WIDTH
            dag = da[:, lo:lo + GROUP_WIDTH]
            _to_residues(dag * wts[grp], stages[6 + grp], do_ref, DILATIONS[grp])
            prod = dag * ov[grp]
            hi = prod.astype(BF16)
            low = (prod - hi.astype(F32)).astype(BF16)
            cbar = cbar + wts[grp] * (_dot(hi, bd) + _dot(low, bd))
        for grp, de_ref in enumerate((de0, de1, de2)):
            _to_residues(wts[grp] * cbar, stages[9 + grp], de_ref, DILATIONS[grp])

    row = lambda w: pl.BlockSpec((t, w), lambda i: (i, 0))
    full = lambda a, b: pl.BlockSpec((a, b), lambda i: (0, 0))
    res = [_residue_spec(dil, t) for dil in DILATIONS]
    *outs, dw, dwb = pl.pallas_call(
        body, name=name, grid=(s // t,),
        in_specs=[row(D_MODEL), full(D_MODEL, D_MODEL)] + res + res + [full(GROUP_WIDTH, GROUP_WIDTH), row(D_MODEL)],
        out_specs=[row(POOL_WIDTH)] + res + res + [full(D_MODEL, D_MODEL)] * 2,
        out_shape=[jax.ShapeDtypeStruct((s, POOL_WIDTH), F32)] + [_residue_shape(dil, s, BF16) for dil in DILATIONS]
        + [_residue_shape(dil, s, F32) for dil in DILATIONS]
        + [jax.ShapeDtypeStruct((D_MODEL, D_MODEL), F32), jax.ShapeDtypeStruct((D_MODEL, D_MODEL), BF16)],
        scratch_shapes=_stages(t, 12),
        compiler_params=_params(1),
    )(dh, w_out, *o, *lse, ones_bd, a)
    return (*outs, (dw, dwb))


def _attn_bwd(q, k, v, do, lse, deff, name, after=()):
    dil, length, _ = q.shape
    nb = length // ATTN_BLOCK
    qb = _blocks_per_step(nb)
    nj = nb // qb
    rs = _residues_per_step(dil, nb, qb)
    whole = nj == 1
    tail = slice((qb - 1) * ATTN_BLOCK, qb * ATTN_BLOCK)
    block = lambda qi: slice(qi * ATTN_BLOCK, (qi + 1) * ATTN_BLOCK)

    def body(q_ref, kp_ref, kc_ref, vp_ref, vc_ref, do_ref, lse_ref, de_ref, dq_ref, dk_ref, dv_ref, ck, cv):
        j = pl.program_id(1)

        def compute():
            masks = _head_masks()
            bias = _band_bias(j == 0)
            for rr in range(rs):
                dkc, dvc = [], []
                for qi in range(qb):
                    here, before = block(qi), block(qi - 1)
                    kcat = jnp.concatenate([kp_ref[rr] if qi == 0 else kc_ref[rr, before], kc_ref[rr, here]], axis=0)
                    vcat = jnp.concatenate([vp_ref[rr] if qi == 0 else vc_ref[rr, before], vc_ref[rr, here]], axis=0)
                    qs = _stack_heads(q_ref[rr, here], masks)
                    dos = _stack_heads(do_ref[rr, here], masks)
                    sc = _dot_nt(qs, kcat) + bias[min(qi, 1)]
                    p = jnp.exp(sc - _column_per_head(lse_ref[rr, here]))
                    ds = (p * (_dot_nt(dos, vcat) - _column_per_head(de_ref[rr, here]))).astype(BF16)
                    dq = jnp.zeros((ATTN_BLOCK, GROUP_WIDTH), F32)
                    for hd, msk in enumerate(masks):
                        dq = jnp.where(msk, _dot(ds[block(hd)], kcat), dq)
                    dq_ref[rr, here] = dq.astype(dq_ref.dtype)
                    dkc.append(_dot_tn(ds, qs))
                    dvc.append(_dot_tn(p.astype(BF16), dos))

                for out_ref, carry, parts in ((dk_ref, ck, dkc), (dv_ref, cv, dvc)):
                    full = [parts[qi][ATTN_BLOCK:] + parts[qi + 1][0:ATTN_BLOCK] for qi in range(qb - 1)]
                    if whole:
                        for qi, val in enumerate(full + [parts[qb - 1][ATTN_BLOCK:]]):
                            out_ref[rr, block(qi)] = val.astype(out_ref.dtype)
                        continue

                    @pl.when(j > 0)
                    def _():
                        if qb > 1:
                            out_ref[0, 0:(qb - 1) * ATTN_BLOCK] = carry[0:(qb - 1) * ATTN_BLOCK].astype(out_ref.dtype)
                        out_ref[0, tail] = (carry[tail] + parts[0][0:ATTN_BLOCK]).astype(out_ref.dtype)

                    for qi, val in enumerate(full):
                        carry[block(qi)] = val
                    carry[tail] = parts[qb - 1][ATTN_BLOCK:]

        if whole:
            compute()
        else:
            pl.when(j < nj)(compute)

            @pl.when(j == nj)
            def _():
                dk_ref[0] = ck[...].astype(dk_ref.dtype)
                dv_ref[0] = cv[...].astype(dv_ref.dtype)

    step = lambda j: jnp.minimum(j, nj - 1)
    cur = pl.BlockSpec((rs, qb * ATTN_BLOCK, GROUP_WIDTH), lambda r, j: (r, step(j), 0))
    prev = pl.BlockSpec((rs, ATTN_BLOCK, GROUP_WIDTH), lambda r, j: (r, jnp.maximum(qb * step(j) - 1, 0), 0))
    late = pl.BlockSpec((rs, qb * ATTN_BLOCK, GROUP_WIDTH), lambda r, j: (r, jnp.maximum(j - 1, 0), 0))
    return pl.pallas_call(
        _ordered_after(body, 8, after), name=name, grid=(dil // rs, 1 if whole else nj + 1),
        in_specs=[cur, prev, cur, prev, cur, cur, cur, cur] + [pl.BlockSpec(memory_space=pl.ANY)] * len(after),
        out_specs=[cur, cur if whole else late, cur if whole else late],
        out_shape=[jax.ShapeDtypeStruct(q.shape, BF16)] * 3,
        scratch_shapes=[pltpu.VMEM((qb * ATTN_BLOCK, GROUP_WIDTH), F32)] * 2,
        compiler_params=_params(2),
    )(q, k, k, v, v, do, lse, deff, *after)


def _pool_bwd(dpool, y, w_bd, scale, name, after=()):
    s = dpool.shape[0]
    t = _row_tile(s, 512)
    nt = s // t

    def body(dp_ref, y_ref, w_ref, sc_ref, du_ref, dw_ref, dsc_ref, ext, b2, b4, b8):
        i = pl.program_id(0)

        @pl.when(i == 0)
        def _():
            ext[t:, :] = jnp.zeros((POOL_HALO + POOL_PAD, POOL_WIDTH), F32)
            for buf in (b2, b4):
                buf[t + POOL_HALO:, :] = jnp.zeros((POOL_PAD, POOL_WIDTH), F32)
            dw_ref[...] = jnp.zeros_like(dw_ref)
            dsc_ref[...] = jnp.zeros_like(dsc_ref)

        dp = dp_ref[...]
        yb = y_ref[...]
        w = w_ref[...]
        dsc_ref[...] += jnp.sum(dp * _dot(yb, w), axis=0, keepdims=True)
        dyo = (dp * sc_ref[...]).astype(BF16)
        dw_ref[...] += _dot_tn(yb, dyo)
        dy = _dot_nt(dyo, w)
        win = _pool_lane_window()
        pos = (nt - 1 - i) * t + lax.broadcasted_iota(jnp.int32, (t, POOL_WIDTH), 0)
        gq = dy / jnp.minimum(pos + 1, win).astype(F32)
        ext[0:t, :] = gq
        du_ref[...] = _window_sums(ext, b2, b4, b8, t, 0, 0, 1) - dy
        ext[t:t + POOL_HALO, :] = gq[0:POOL_HALO, :]

    rev = pl.BlockSpec((t, POOL_WIDTH), lambda i: (nt - 1 - i, 0))
    full = lambda a, b: pl.BlockSpec((a, b), lambda i: (0, 0))
    return pl.pallas_call(
        _ordered_after(body, 4, after), name=name, grid=(nt,),
        in_specs=[rev, rev, full(POOL_WIDTH, POOL_WIDTH), full(1, POOL_WIDTH)]
        + [pl.BlockSpec(memory_space=pl.ANY)] * len(after),
        out_specs=[rev, full(POOL_WIDTH, POOL_WIDTH), full(1, POOL_WIDTH)],
        out_shape=[jax.ShapeDtypeStruct((s, POOL_WIDTH), F32), jax.ShapeDtypeStruct((POOL_WIDTH, POOL_WIDTH), F32),
                   jax.ShapeDtypeStruct((1, POOL_WIDTH), F32)],
        scratch_shapes=[pltpu.VMEM((t + POOL_HALO + POOL_PAD, POOL_WIDTH), F32)] * 4,
        compiler_params=_params(1),
    )(dpool, y, w_bd, scale, *after)


def _normproj_bwd(dh, du, dq, dk, dv, rc, rsa, rsb, w_in, h, g, name):
    s = h.shape[0]
    t = _row_tile(s, 512)

    def body(dh_ref, du_ref, q0, q1, q2, k0, k1, k2, v0, v1, v2, c_ref, sa_ref, sb_ref, w_ref, h_ref, g_ref,
             out_ref, dz_ref, dg_ref, *stages):
        @pl.when(pl.program_id(0) == 0)
        def _():
            dg_ref[...] = jnp.zeros_like(dg_ref)

        c, sa, sb = c_ref[...], sa_ref[...], sb_ref[...]

        def unrot(a, scale):
            halves = [_rot_t(a[:, hf * LANES:(hf + 1) * LANES] * scale, c, sa, sb) for hf in range(2)]
            return jnp.concatenate(halves, axis=1)

        staged = _pair_stages(stages)
        tok = lambda refs, base: [_from_residues(r, staged[base + i], DILATIONS[i]) for i, r in enumerate(refs)]
        chunks = [du_ref[...]]
        chunks += [unrot(a, HEAD_DIM ** -0.5) for a in tok((q0, q1, q2), 0)]
        chunks += [unrot(a, 1.0) for a in tok((k0, k1, k2), 3)]
        chunks += tok((v0, v1, v2), 6)
        acc = jnp.zeros((t, D_MODEL), F32)
        for ci, ch in enumerate(chunks):
            cols = slice(ci * GROUP_WIDTH, (ci + 1) * GROUP_WIDTH)
            cb = ch.astype(BF16)
            dz_ref[:, cols] = cb
            acc = acc + _dot_nt(cb, w_ref[:, cols])
        gv = g_ref[...]
        n, rstd, _ = _rms(h_ref[...], gv)
        dx, dg = _rms_bwd(acc, n, rstd, gv)
        out_ref[...] = dh_ref[...] + dx
        dg_ref[...] += dg

    row = lambda w: pl.BlockSpec((t, w), lambda i: (i, 0))
    vec = pl.BlockSpec((1, D_MODEL), lambda i: (0, 0))
    res = [_residue_spec(dil, t) for dil in DILATIONS]
    return pl.pallas_call(
        body, name=name, grid=(s // t,),
        in_specs=[row(D_MODEL), row(POOL_WIDTH)] + res * 3 + _table_specs(t)
        + [pl.BlockSpec((D_MODEL, N_IN), lambda i: (0, 0)), row(D_MODEL), vec],
        out_specs=[row(D_MODEL), row(N_IN), vec],
        out_shape=[jax.ShapeDtypeStruct((s, D_MODEL), F32), jax.ShapeDtypeStruct((s, N_IN), BF16),
                   jax.ShapeDtypeStruct((1, D_MODEL), F32)],
        scratch_shapes=_stages(t, 9),
        compiler_params=_params(1),
    )(dh, du, *dq, *dk, *dv, rc, rsa, rsb, w_in, h, g)


def _matmul_tn(a, b, name, *, square_a=False, tn=None, blocked_out=False, after=()):
    s, m = a.shape
    n = b.shape[1]
    tk = _row_tile(s, 2048)
    tm = min(m, 1024)
    tn = tn or min(n, 1024)
    assert m % tm == 0 and n % tn == 0
    nk = s // tk
    nsub = tn // FF_BLOCK if blocked_out else 1

    def body(a_ref, b_ref, o_ref, ob_ref, acc):
        k = pl.program_id(2)

        def product():
            av = a_ref[...]
            if square_a:
                av = av.astype(F32)
                av = av * av
            return _dot_tn(av.astype(BF16), b_ref[...].astype(BF16))

        def emit(total):
            if blocked_out:
                for sub in range(nsub):
                    cols = slice(sub * FF_BLOCK, (sub + 1) * FF_BLOCK)
                    o_ref[sub] = total[:, cols]
                    ob_ref[sub] = total[:, cols].astype(BF16)
            else:
                o_ref[...] = total
                ob_ref[...] = total.astype(BF16)

        if nk == 1:
            emit(product())
            return

        @pl.when(k == 0)
        def _():
            acc[...] = product()

        @pl.when((k > 0) & (k < nk - 1))
        def _():
            acc[...] += product()

        @pl.when(k == nk - 1)
        def _():
            emit(acc[...] + product())

    if blocked_out:
        shape = (n // FF_BLOCK, m, FF_BLOCK)
        out_spec = pl.BlockSpec((nsub, tm, FF_BLOCK), lambda i, j, k: (j, i, 0))
    else:
        shape = (m, n)
        out_spec = pl.BlockSpec((tm, tn), lambda i, j, k: (i, j))
    return pl.pallas_call(
        _ordered_after(body, 2, after), name=name, grid=(m // tm, n // tn, nk),
        in_specs=[pl.BlockSpec((tk, tm), lambda i, j, k: (k, i)), pl.BlockSpec((tk, tn), lambda i, j, k: (k, j))]
        + [pl.BlockSpec(memory_space=pl.ANY)] * len(after),
        out_specs=[out_spec, out_spec],
        out_shape=[jax.ShapeDtypeStruct(shape, F32), jax.ShapeDtypeStruct(shape, BF16)],
        scratch_shapes=[pltpu.VMEM((tm, tn), F32)],
        compiler_params=_params(3),
    )(a, b, *after)


def _adamw_math(w, g, m, v):
    m = ADAM_B1 * m + (1.0 - ADAM_B1) * g
    v = ADAM_B2 * v + (1.0 - ADAM_B2) * (g * g)
    m_hat = m / (1.0 - ADAM_B1 ** ADAM_STEP)
    v_hat = v / (1.0 - ADAM_B2 ** ADAM_STEP)
    delta = -ADAM_LR * (m_hat / (jnp.sqrt(v_hat) + ADAM_EPS) + ADAM_WD * w)
    return delta, m, v


def _adamw_sharded(w, m, v, own, recv0, recv1, name):
    _, rows, cols = w.shape
    t = _row_tile(rows, 256)

    def body(w_ref, m_ref, v_ref, own_ref, r0_ref, r1_ref, g_ref, d_ref, nm_ref, nv_ref):
        layer0 = pl.program_id(0) == 0
        g = own_ref[...]
        for k in range(N_DEV - 1):
            g = g + jnp.where(layer0, r0_ref[k], r1_ref[k]).astype(F32)
        g_ref[...] = g
        d_ref[...], nm_ref[...], nv_ref[...] = _adamw_math(w_ref[...], g, m_ref[...], v_ref[...])

    blk = pl.BlockSpec((None, t, cols), lambda l, i: (l, i, 0))
    recv = lambda layer: pl.BlockSpec((N_DEV - 1, t, cols), lambda l, i: (0, jnp.where(l == layer, i, 0), 0))
    return pl.pallas_call(
        body, name=name, grid=(2, rows // t),
        in_specs=[blk, blk, blk, blk, recv(0), recv(1)], out_specs=[blk] * 4,
        out_shape=[jax.ShapeDtypeStruct(w.shape, F32)] * 4,
        compiler_params=_params(2),
    )(w, m, v, own, recv0, recv1)


def _adamw_packed(w, g8, m, v, name):
    def body(w_ref, g_ref, m_ref, v_ref, go_ref, d_ref, nm_ref, nv_ref):
        g = g_ref[0]
        for dev in range(1, N_DEV):
            g = g + g_ref[dev]
        go_ref[...] = g
        d_ref[...], nm_ref[...], nv_ref[...] = _adamw_math(w_ref[...], g, m_ref[...], v_ref[...])

    return pl.pallas_call(
        body, name=name, out_shape=[jax.ShapeDtypeStruct(w.shape, F32)] * 4,
        compiler_params=pltpu.CompilerParams(vmem_limit_bytes=VMEM_LIMIT),
    )(w, g8, m, v)


def _peer(k):
    x, y, c = lax.axis_index("x"), lax.axis_index("y"), lax.axis_index("c")
    return (1 - x if k & 4 else x, 1 - y if k & 2 else y, 1 - c if k & 1 else c)


def _linear(dev):
    return 4 * dev[0] + 2 * dev[1] + dev[2]


HBM_SPEC = pl.BlockSpec(memory_space=pltpu.HBM)
SEM_SPEC = pl.BlockSpec(memory_space=pltpu.SEMAPHORE)
ANY_SPEC = pl.BlockSpec(memory_space=pl.ANY)
EFFECT = pltpu.SideEffectType.DATAFLOW_SIDE_EFFECTING


def _in_hbm(a):
    return pltpu.with_memory_space_constraint(a, pltpu.HBM)


class _Exchange:
    def __init__(self, name, groups, scatter, after=()):
        self.name, self.scatter = name, scatter
        self.sizes = sizes = [len(g) for g in groups]
        srcs = [a for g in groups for a in g]
        n, ng = len(srcs), len(groups)
        lead = (N_DEV - 1,) if scatter else (N_DEV,)
        shapes = [lead + (a.shape[1:] if scatter else a.shape) for a in srcs]
        lands = [lax.empty(sh, a.dtype) for sh, a in zip(shapes, srcs)]
        offsets = [sum(sizes[:gi]) for gi in range(ng)]
        copy = self._copy

        def body(*refs):
            src, land = refs[:n], refs[n:2 * n]
            sems = refs[2 * n + len(after):2 * n + len(after) + 2 * ng]
            token = refs[-1]
            for gi in range(ng):
                for wi in range(sizes[gi]):
                    w = offsets[gi] + wi
                    for k in range(1, N_DEV):
                        copy(src[w], land[w], sems[2 * gi], sems[2 * gi + 1], wi, k).start()
            token[...] = jnp.zeros_like(token)

        sem_shapes = [pltpu.SemaphoreType.DMA((7 * sz,)) for sz in sizes for _ in range(2)]
        outs = pl.pallas_call(
            body, name=name + "_start",
            in_specs=[HBM_SPEC] * (2 * n) + [ANY_SPEC] * len(after),
            out_specs=[SEM_SPEC] * (2 * ng) + [HBM_SPEC] * (2 * n) + [pl.BlockSpec(memory_space=pltpu.VMEM)],
            out_shape=sem_shapes + [pltpu.HBM(a.shape, a.dtype) for a in srcs + lands]
            + [jax.ShapeDtypeStruct((8, LANES), F32)],
            input_output_aliases={i: 2 * ng + i for i in range(2 * n)},
            compiler_params=pltpu.CompilerParams(has_side_effects=EFFECT),
        )(*[_in_hbm(a) for a in srcs + lands], *after)
        self.sems = [outs[2 * gi:2 * gi + 2] for gi in range(ng)]
        thru = outs[2 * ng:2 * ng + 2 * n]
        self.srcs = [thru[offsets[gi]:offsets[gi] + sizes[gi]] for gi in range(ng)]
        self.lands = [thru[n + offsets[gi]:n + offsets[gi] + sizes[gi]] for gi in range(ng)]
        self.token = outs[-1]

    def _copy(self, src, land, send_sems, recv_sems, wi, k):
        to = _peer(k)
        if self.scatter:
            src_ref, dst_ref = src.at[_linear(to)], land.at[k - 1]
        else:
            src_ref, dst_ref = src, land.at[_linear(_peer(0))]
        return pltpu.make_async_remote_copy(
            src_ref=src_ref, dst_ref=dst_ref, send_sem=send_sems.at[7 * wi + k - 1],
            recv_sem=recv_sems.at[7 * wi + k - 1], device_id=to, device_id_type=MESH)

    def wait(self, gi, after):
        n = self.sizes[gi]
        copy = self._copy

        def body(*refs):
            src, land = refs[:n], refs[n:2 * n]
            send_sems, recv_sems = refs[2 * n], refs[2 * n + 1]
            for wi in range(n):
                for k in range(1, N_DEV):
                    cp = copy(src[wi], land[wi], send_sems, recv_sems, wi, k)
                    cp.wait_send()
                    cp.wait_recv()

        arrays = list(self.srcs[gi]) + list(self.lands[gi])
        outs = pl.pallas_call(
            body, name=f"{self.name}_wait{gi}",
            in_specs=[HBM_SPEC] * (2 * n) + [SEM_SPEC, SEM_SPEC] + [ANY_SPEC] * len(after),
            out_specs=[HBM_SPEC] * (2 * n),
            out_shape=[pltpu.HBM(a.shape, a.dtype) for a in arrays],
            input_output_aliases={i: i for i in range(2 * n)},
            compiler_params=pltpu.CompilerParams(has_side_effects=EFFECT),
        )(*arrays, *self.sems[gi], *after)
        return outs[:n], outs[n:]


def _rotary_tables(positions):
    rot_dim = HEAD_DIM // 4
    inv_freq = ROPE_THETA ** (-jnp.arange(0, rot_dim, 2, dtype=F32) / rot_dim)
    ang = positions.astype(F32)[:, None] * inv_freq
    cs = jnp.concatenate([jnp.cos(ang), jnp.sin(ang)], axis=1)
    dim = jnp.arange(LANES) % HEAD_DIM
    first, second = dim < ROT_SHIFT, (dim >= ROT_SHIFT) & (dim < rot_dim)
    src = jnp.arange(2 * ROT_SHIFT)[:, None]
    angle = (dim % ROT_SHIFT)[None, :]
    c = jnp.where((first | second)[None, :] & (src == angle), 1.0, 0.0)
    sa = jnp.where(second[None, :] & (src == angle + ROT_SHIFT), 1.0, 0.0)
    sb = jnp.where(first[None, :] & (src == angle + ROT_SHIFT), -1.0, 0.0)
    spread = jnp.concatenate([c, sa, sb], axis=1).astype(F32)
    base = jnp.concatenate([jnp.where(first | second, 0.0, 1.0), jnp.zeros((2 * LANES,))]).astype(F32)[None, :]
    return jnp.dot(cs, spread, precision=lax.Precision.HIGHEST, preferred_element_type=F32) + base


def _block_diag(pool_w):
    gc = pool_w.shape[-1]
    out = jnp.zeros((POOL_WIDTH, POOL_WIDTH), pool_w.dtype)
    for grp in range(pool_w.shape[0]):
        out = lax.dynamic_update_slice(out, pool_w[grp], (grp * gc, grp * gc))
    return out


def _diag_blocks(a):
    gc = POOL_WIDTH // len(POOL_WINDOWS)
    return jnp.stack([a[grp * gc:(grp + 1) * gc, grp * gc:(grp + 1) * gc] for grp in range(len(POOL_WINDOWS))])


def _local_step(x, p, positions, loss_target, norm1, pool_w, pool_scale, norm2, norm3, final_norm, weights, send):
    rc = rsa = rsb = _rotary_tables(positions)
    ones_bd = _block_diag(jnp.ones((4, HEAD_DIM, HEAD_DIM), BF16))
    saved = []
    h = x
    for i in range(2):
        tag = f"_l{i}"
        g1, g2, g3 = norm1[i:i + 1], norm2[i:i + 1], norm3[i:i + 1]
        w_bd = _block_diag(pool_w[i]).astype(BF16)
        scale = pool_scale[i:i + 1]
        w_in = weights(i, "in", (h, rc, w_bd))
        hn1, u, *qkv = _normproj_fwd(h, g1, w_in, rc, rsa, rsb, "normproj_fwd" + tag)
        qkv = [qkv[3 * grp:3 * grp + 3] for grp in range(3)]
        started = weights(i, "prefetch", (hn1,))
        pool_out, y = _pool_fwd(u, w_bd, scale, "pool_fwd" + tag, after=started)
        o, lse = zip(*[_attn_fwd(*qkv[grp], f"attn_fwd{tag}_g{grp}", after=started) for grp in range(3)])
        w_out = weights(i, "out", (pool_out, *o))
        h1, a = _outproj_fwd(h, pool_out, o, lse, w_out, "outproj_fwd" + tag)
        w_up, w_down, w_gate, w_ple = weights(i, "rest", (h1,))
        h2, hn2, r = _mlp_fwd(h1, g2, w_up, w_down, "mlp_fwd" + tag)
        h3, hn3, gate, pb = _gate_fwd(h2, g3, w_gate, p, i, w_ple, "gate_fwd" + tag)
        saved.append(dict(h0=h, hn1=hn1, qkv=qkv, y=y, o=o, lse=lse, a=a, h1=h1, hn2=hn2, r=r, h2=h2,
                          hn3=hn3, gate=gate, pb=pb, w_bd=w_bd, scale=scale, g1=g1, g2=g2, g3=g3,
                          w_in=w_in, w_out=w_out, w_up=w_up, w_down=w_down, w_gate=w_gate, w_ple=w_ple))
        h = h3
    loss, dh, d_final = _loss_head(h, final_norm.reshape(1, D_MODEL), loss_target, "loss_head")

    grads = [None, None]
    sent = ()
    for i in (1, 0):
        tag = f"_l{i}"
        sv = saved[i]
        dh2, dg3, dw_gate, dw_ple = _gate_bwd(dh, sv["gate"], sv["pb"], sv["w_ple"], sv["h2"], sv["g3"], sv["w_gate"],
                                              sv["hn3"], "gate_bwd" + tag, after=sent)
        dh1, dup, dg2, dh2b = _mlp_bwd(dh2, sv["r"], sv["h1"], sv["g2"], sv["w_up"], sv["w_down"], "mlp_bwd" + tag)
        dw_down = _matmul_tn(sv["r"], dh2b, "dw_down" + tag, square_a=True)
        dw_up = _matmul_tn(sv["hn2"], dup, "dw_up" + tag, blocked_out=True)
        dpool, do0, do1, do2, de0, de1, de2, dw_out = _outproj_bwd(dh1, sv["w_out"], sv["o"], sv["lse"], ones_bd,
                                                                   sv["a"], "outproj_bwd" + tag)
        sent = send(i, "main", dict(w_gate=dw_gate, w_ple=dw_ple, w_down=dw_down, w_up=dw_up, w_out=dw_out))
        dqkv = [_attn_bwd(*sv["qkv"][grp], do_g, sv["lse"][grp], de_g, f"attn_bwd{tag}_g{grp}", after=sent)
                for grp, (do_g, de_g) in enumerate(((do0, de0), (do1, de1), (do2, de2)))]
        dq, dk, dv = zip(*dqkv)
        du, dw_bd, dscale = _pool_bwd(dpool, sv["y"], sv["w_bd"], sv["scale"], "pool_bwd" + tag, after=sent)
        dh, dz, dg1 = _normproj_bwd(dh1, du, dq, dk, dv, rc, rsa, rsb, sv["w_in"], sv["h0"], sv["g1"],
                                    "normproj_bwd" + tag)
        grads[i] = dict(norm1=dg1, norm2=dg2, norm3=dg3, pool_w=_diag_blocks(dw_bd), pool_scale=dscale)
        small_sent = send(0, "small", (grads, d_final, loss)) if i == 0 else ()
        dw_in = _matmul_tn(sv["hn1"], dz, "dw_in" + tag, tn=N_IN // 2, after=small_sent)
        sent = send(i, "in", dict(w_in=dw_in))
    return dh, sent


def _pack_small(norm1, norm2, norm3, final_norm, pool_scale, pool_w, spare=None):
    spare = jnp.zeros((1, LANES), F32) if spare is None else spare
    scale_row = jnp.concatenate([pool_scale.reshape(1, 2 * POOL_WIDTH), spare,
                                 jnp.zeros((1, D_MODEL - 2 * POOL_WIDTH - LANES), F32)], axis=1)
    return jnp.concatenate([norm1, norm2, norm3, final_norm.reshape(1, D_MODEL), scale_row,
                            pool_w.reshape(32, D_MODEL)], axis=0)


def _unpack_small(a):
    return dict(norm1=a[0:2], norm2=a[2:4], norm3=a[4:6], final_norm=a[6], pool_scale=a[7, 0:2 * POOL_WIDTH].reshape(2, POOL_WIDTH),
                pool_w=a[8:40].reshape(2, 4, HEAD_DIM, HEAD_DIM))


def _chunks_cols(a, cols):
    return a.reshape(a.shape[0], N_DEV, cols).transpose(1, 0, 2)


def _chunks_rows(a, rows):
    return a.reshape(N_DEV, rows, a.shape[1])


BIG = ("w_in", "w_out", "w_up", "w_down", "w_gate", "w_ple")
SMALL = ("norm1", "norm2", "norm3", "final_norm", "pool_scale", "pool_w")
ORDER = ("norm1", "w_in", "pool_w", "pool_scale", "w_out", "norm2", "w_up", "w_down", "norm3", "w_gate", "w_ple",
         "final_norm")


def kernel(x, p, positions, norm1, w_in, pool_w, pool_scale, w_out, norm2, w_up, w_down, norm3, w_gate, w_ple, final_norm, loss_target, m_norm1, m_w_in, m_pool_w, m_pool_scale, m_w_out, m_norm2, m_w_up, m_w_down, m_norm3, m_w_gate, m_w_ple, m_final_norm, v_norm1, v_w_in, v_pool_w, v_pool_scale, v_w_out, v_norm2, v_w_up, v_w_down, v_norm3, v_w_gate, v_w_ple, v_final_norm):
    w = dict(norm1=norm1, w_in=w_in, pool_w=pool_w, pool_scale=pool_scale, w_out=w_out, norm2=norm2, w_up=w_up,
             w_down=w_down, norm3=norm3, w_gate=w_gate, w_ple=w_ple, final_norm=final_norm)
    m = dict(norm1=m_norm1, w_in=m_w_in, pool_w=m_pool_w, pool_scale=m_pool_scale, w_out=m_w_out, norm2=m_norm2,
             w_up=m_w_up, w_down=m_w_down, norm3=m_norm3, w_gate=m_w_gate, w_ple=m_w_ple, final_norm=m_final_norm)
    v = dict(norm1=v_norm1, w_in=v_w_in, pool_w=v_pool_w, pool_scale=v_pool_scale, w_out=v_w_out, norm2=v_norm2,
             w_up=v_w_up, w_down=v_w_down, norm3=v_norm3, w_gate=v_w_gate, w_ple=v_w_ple, final_norm=v_final_norm)
    seq = x.shape[1]

    bf = {n: [w[n][layer].astype(BF16) for layer in range(2)] for n in BIG}
    rest = ("w_up", "w_down", "w_gate", "w_ple")
    me = 4 * lax.axis_index("x") + 2 * lax.axis_index("y") + lax.axis_index("c")
    gathers = [_Exchange("gather_l0", [[bf["w_in"][0]], [bf["w_out"][0]], [bf[n][0] for n in rest]], scatter=False)]
    unpack = dict(w_in=lambda a: a.transpose(1, 0, 2).reshape(D_MODEL, N_IN),
                  w_out=lambda a: a.reshape(D_MODEL, D_MODEL), w_gate=lambda a: a.reshape(D_MODEL, D_MODEL),
                  w_ple=lambda a: a.transpose(1, 0, 2).reshape(PLE_DIM, D_MODEL), w_up=lambda a: a, w_down=lambda a: a)
    parts = dict(zip(("in", "out", "rest"), (("w_in",), ("w_out",), rest)))

    def weights(layer, part, after):
        if part == "prefetch":
            if layer != 0:
                return ()
            gathers.append(_Exchange("gather_l1", [[bf[n][1] for n in parts[pt]] for pt in parts], scatter=False,
                                     after=after))
            return (gathers[1].token,)
        shards, lands = gathers[layer].wait(tuple(parts).index(part), after)
        full = [unpack[n](lax.dynamic_update_slice_in_dim(land, shard[None], me, axis=0))
                for n, shard, land in zip(parts[part], shards, lands)]
        return full if part == "rest" else full[0]

    to_chunks = dict(w_in=lambda a: _chunks_cols(a, N_IN // N_DEV), w_out=lambda a: _chunks_rows(a, D_MODEL // N_DEV),
                     w_up=lambda a: a, w_down=lambda a: _chunks_rows(a, FF_BLOCK),
                     w_gate=lambda a: _chunks_rows(a, D_MODEL // N_DEV), w_ple=lambda a: _chunks_cols(a, D_MODEL // N_DEV))
    own = {n: [None, None] for n in BIG}
    scatters = {}

    def own_chunk(n, g32):
        if n in ("w_in", "w_ple"):
            cols = g32.shape[1] // N_DEV
            return lax.dynamic_slice(g32, (0, me * cols), (g32.shape[0], cols))
        return lax.dynamic_index_in_dim(to_chunks[n](g32), me, axis=0, keepdims=False)

    def send(layer, part, grads):
        if part == "small":
            per_layer, d_final, loss = grads
            pack = _pack_small(
                *[jnp.concatenate([per_layer[0][n], per_layer[1][n]], axis=0) for n in ("norm1", "norm2", "norm3")],
                d_final.reshape(D_MODEL),
                jnp.concatenate([per_layer[0]["pool_scale"], per_layer[1]["pool_scale"]], axis=0),
                jnp.stack([per_layer[0]["pool_w"], per_layer[1]["pool_w"]]), spare=loss)
            scatters["small"] = _Exchange("gather_small", [[pack]], scatter=False)
            return (scatters["small"].token,)
        for n, (g32, _) in grads.items():
            own[n][layer] = own_chunk(n, g32)
        ex = _Exchange(f"scatter_{part}_l{layer}", [[to_chunks[n](g16) for n, (_, g16) in grads.items()]], scatter=True)
        scatters[layer, part] = (tuple(grads), ex)
        return (ex.token,)

    dx, sent = _local_step(
        x.reshape(seq, D_MODEL), p.reshape(2, seq, PLE_DIM), positions.reshape(seq), loss_target.reshape(seq, D_MODEL),
        norm1, pool_w, pool_scale, norm2, norm3, final_norm, weights, send)

    g_out, d_out, m_out, v_out = {}, {}, {}, {}
    for part in ("main", "in"):
        recv = {}
        for layer in (1, 0):
            names, ex = scatters[layer, part]
            for n, r in zip(names, ex.wait(0, sent)[1]):
                recv[n, layer] = r
        for n in names:
            g_out[n], d_out[n], m_out[n], v_out[n] = _adamw_sharded(
                w[n], m[n], v[n], jnp.stack(own[n]), recv[n, 0], recv[n, 1], "adamw_" + n)
        sent = tuple(d_out[n] for n in names)
    (mine,), (landed,) = scatters["small"].wait(0, sent)
    small_g8 = lax.dynamic_update_slice_in_dim(landed, mine[None], me, axis=0)
    pack = lambda t: _pack_small(*[t[n] for n in SMALL])
    small_g, d_small, m_small, v_small = _adamw_packed(pack(w), small_g8, pack(m), pack(v), "adamw_small")
    for dst, a in ((g_out, small_g), (d_out, d_small), (m_out, m_small), (v_out, v_small)):
        dst.update(_unpack_small(a))

    return (small_g[7, 2 * POOL_WIDTH],dx.reshape(1, seq, D_MODEL), *[g_out[n] for n in ORDER], *[d_out[n] for n in ORDER],
            *[m_out[n] for n in ORDER], *[v_out[n] for n in ORDER])
```

```python
import functools

import jax
import jax.numpy as jnp
from jax import lax
from jax.experimental import pallas as pl
from jax.experimental.pallas import tpu as pltpu

F32 = jnp.float32
BF16 = jnp.bfloat16

D_MODEL = 1024
HEAD_DIM = 64
POOL_WIDTH = 256
POOL_WINDOWS = (2, 4, 8, 16)
POOL_HALO = 16
POOL_PAD = 8
GROUP_WIDTH = 256
DILATIONS = (1, 4, 16)
ATTN_BLOCK = 128
ROT_SHIFT = 8
ROPE_THETA = 500000.0
D_FF = 4096
FF_BLOCK = 512
FF_PER_STEP = 2
MLP_BWD_TILE = 512
N_DEV = 8
N_IN = POOL_WIDTH + 3 * 768
PLE_DIM = 256
EPS = 1e-6
NEG_BIG = -1e30

ADAM_LR = 0.001
ADAM_B1 = 0.9
ADAM_B2 = 0.999
ADAM_EPS = 1e-08
ADAM_WD = 0.01
ADAM_STEP = 10

LANES = 128
VMEM_LIMIT = 56 * 1024 * 1024
MESH = pl.DeviceIdType.MESH


def _params(n_grid):
    return pltpu.CompilerParams(dimension_semantics=("arbitrary",) * n_grid, vmem_limit_bytes=VMEM_LIMIT)


def _dot(a, b):
    return jnp.dot(a, b, preferred_element_type=F32)


def _dot_nt(a, b):
    return lax.dot_general(a, b, (((1,), (1,)), ((), ())), preferred_element_type=F32)


def _dot_tn(a, b):
    return lax.dot_general(a, b, (((0,), (0,)), ((), ())), preferred_element_type=F32)


def _rms(x, g):
    rstd = lax.rsqrt(jnp.mean(x * x, axis=-1, keepdims=True) + EPS)
    n = x * rstd
    return n, rstd, n * g


def _rms_bwd(dy, n, rstd, g):
    dyn = dy * g
    dx = rstd * (dyn - n * jnp.mean(dyn * n, axis=-1, keepdims=True))
    return dx, jnp.sum(dy * n, axis=0, keepdims=True)


def _ordered_after(body, n_in, after):
    if not after:
        return body
    return lambda *refs: body(*refs[:n_in], *refs[n_in + len(after):])


def _row_tile(s, t):
    t = min(s, t)
    assert s % t == 0
    return t


def _rot(z, c, sa, sb):
    return z * c + pltpu.roll(z, ROT_SHIFT, 1) * sa + pltpu.roll(z, LANES - ROT_SHIFT, 1) * sb


def _table_specs(t):
    return [pl.BlockSpec((t, LANES), functools.partial(lambda i, k: (i, k), k=k)) for k in range(3)]


def _rot_t(dz, c, sa, sb):
    return dz * c + pltpu.roll(dz * sa, LANES - ROT_SHIFT, 1) + pltpu.roll(dz * sb, ROT_SHIFT, 1)


def _to_residues(value, stage, out_ref, dil):
    if dil == 1:
        out_ref[0] = value.astype(out_ref.dtype)
        return
    rows = value.shape[0] // dil
    for hf in range(GROUP_WIDTH // LANES):
        lanes = slice(hf * LANES, (hf + 1) * LANES)
        stage[hf][...] = value[:, lanes]
        for r in range(dil):
            out_ref[r, :, lanes] = stage[hf][pl.ds(r, rows, stride=dil), :].astype(out_ref.dtype)


def _from_residues(in_ref, stage, dil):
    if dil == 1:
        return in_ref[0].astype(F32)
    rows = in_ref.shape[1]
    for hf in range(GROUP_WIDTH // LANES):
        for r in range(dil):
            stage[hf][pl.ds(r, rows, stride=dil), :] = in_ref[r, :, hf * LANES:(hf + 1) * LANES].astype(F32)
    return jnp.concatenate([stage[0][...], stage[1][...]], axis=1)


def _residue_spec(dil, t):
    return pl.BlockSpec((dil, t // dil, GROUP_WIDTH), lambda i: (0, i, 0))


def _residue_shape(dil, s, dtype):
    return jax.ShapeDtypeStruct((dil, s // dil, GROUP_WIDTH), dtype)


def _stages(t, n):
    return [pltpu.VMEM((t, LANES), F32)] * (n * (GROUP_WIDTH // LANES))


def _pair_stages(refs):
    return [refs[i:i + 2] for i in range(0, len(refs), 2)]


def _normproj_fwd(h, g, w_in, rc, rsa, rsb, name):
    s = h.shape[0]
    t = _row_tile(s, 512)

    def body(h_ref, g_ref, w_ref, c_ref, sa_ref, sb_ref, hn_ref, u_ref, *rest):
        qkv_refs, stages = rest[:9], _pair_stages(rest[9:])
        _, _, hn = _rms(h_ref[...], g_ref[...])
        hb = hn.astype(BF16)
        hn_ref[...] = hb
        c, sa, sb = c_ref[...], sa_ref[...], sb_ref[...]

        def rot(z, scale):
            halves = [_rot(z[:, hf * LANES:(hf + 1) * LANES], c, sa, sb) * scale for hf in range(2)]
            return jnp.concatenate(halves, axis=1)

        u_ref[...] = _dot(hb, w_ref[:, 0:POOL_WIDTH])
        for grp, dil in enumerate(DILATIONS):
            lo = POOL_WIDTH + grp * GROUP_WIDTH
            q_ref, k_ref, v_ref = qkv_refs[3 * grp:3 * grp + 3]
            _to_residues(rot(_dot(hb, w_ref[:, lo:lo + GROUP_WIDTH]), HEAD_DIM ** -0.5), stages[0], q_ref, dil)
            _to_residues(rot(_dot(hb, w_ref[:, lo + 768:lo + 768 + GROUP_WIDTH]), 1.0), stages[1], k_ref, dil)
            _to_residues(_dot(hb, w_ref[:, lo + 1536:lo + 1536 + GROUP_WIDTH]), stages[2], v_ref, dil)

    row = lambda w: pl.BlockSpec((t, w), lambda i: (i, 0))
    return pl.pallas_call(
        body, name=name, grid=(s // t,),
        in_specs=[row(D_MODEL), pl.BlockSpec((1, D_MODEL), lambda i: (0, 0)),
                  pl.BlockSpec((D_MODEL, N_IN), lambda i: (0, 0))] + _table_specs(t),
        out_specs=[row(D_MODEL), row(POOL_WIDTH)] + [_residue_spec(dil, t) for dil in DILATIONS for _ in range(3)],
        out_shape=[jax.ShapeDtypeStruct((s, D_MODEL), BF16), jax.ShapeDtypeStruct((s, POOL_WIDTH), F32)]
        + [_residue_shape(dil, s, BF16) for dil in DILATIONS for _ in range(3)],
        scratch_shapes=_stages(t, 3),
        compiler_params=_params(1),
    )(h, g, w_in, rc, rsa, rsb)


def _pool_lane_window():
    lane = lax.broadcasted_iota(jnp.int32, (1, POOL_WIDTH), 1)
    return jnp.left_shift(2, lane // (POOL_WIDTH // len(POOL_WINDOWS)))


def _window_sums(ext, b2, b4, b8, t, lo, tile, direction):
    rows = t + POOL_HALO
    for src, dst, sh in ((ext, b2, 1), (b2, b4, 2), (b4, b8, 4)):
        dst[lo:lo + rows, :] = src[lo:lo + rows, :] + src[lo + direction * sh:lo + direction * sh + rows, :]
    s16 = b8[tile:tile + t, :] + b8[tile + direction * 8:tile + direction * 8 + t, :]
    win = _pool_lane_window()
    return jnp.where(win == 2, b2[tile:tile + t, :],
                     jnp.where(win == 4, b4[tile:tile + t, :], jnp.where(win == 8, b8[tile:tile + t, :], s16)))


def _pool_fwd(u, w_bd, scale, name, after=()):
    s = u.shape[0]
    t = _row_tile(s, 512)
    first = POOL_PAD + POOL_HALO

    def body(u_ref, w_ref, sc_ref, out_ref, y_ref, ext, b2, b4, b8):
        i = pl.program_id(0)

        @pl.when(i == 0)
        def _():
            for buf in (ext, b2, b4):
                buf[0:POOL_PAD, :] = jnp.zeros((POOL_PAD, POOL_WIDTH), F32)
            ext[POOL_PAD:first, :] = jnp.zeros((POOL_HALO, POOL_WIDTH), F32)

        x = u_ref[...]
        ext[first:, :] = x
        wsum = _window_sums(ext, b2, b4, b8, t, POOL_PAD, first, -1)
        pos = i * t + lax.broadcasted_iota(jnp.int32, (t, POOL_WIDTH), 0)
        cnt = jnp.minimum(pos + 1, _pool_lane_window()).astype(F32)
        y = wsum / cnt - x
        yb = y.astype(BF16)
        y_ref[...] = yb
        out_ref[...] = _dot(yb, w_ref[...]) * sc_ref[...]
        ext[POOL_PAD:first, :] = x[t - POOL_HALO:, :]

    row = pl.BlockSpec((t, POOL_WIDTH), lambda i: (i, 0))
    return pl.pallas_call(
        _ordered_after(body, 3, after), name=name, grid=(s // t,),
        in_specs=[row, pl.BlockSpec((POOL_WIDTH, POOL_WIDTH), lambda i: (0, 0)),
                  pl.BlockSpec((1, POOL_WIDTH), lambda i: (0, 0))] + [pl.BlockSpec(memory_space=pl.ANY)] * len(after),
        out_specs=[row, row],
        out_shape=[jax.ShapeDtypeStruct((s, POOL_WIDTH), F32), jax.ShapeDtypeStruct((s, POOL_WIDTH), BF16)],
        scratch_shapes=[pltpu.VMEM((t + POOL_HALO + POOL_PAD, POOL_WIDTH), F32)] * 4,
        compiler_params=_params(1),
    )(u, w_bd, scale, *after)


def _head_masks():
    lane = lax.broadcasted_iota(jnp.int32, (ATTN_BLOCK, GROUP_WIDTH), 1)
    return [lane // HEAD_DIM == hd for hd in range(GROUP_WIDTH // HEAD_DIM)]


def _stack_heads(a, masks):
    zero = jnp.zeros_like(a)
    return jnp.concatenate([jnp.where(m, a, zero) for m in masks], axis=0)


def _band_bias(first_step):
    rows = ATTN_BLOCK * (GROUP_WIDTH // HEAD_DIM)
    i = lax.broadcasted_iota(jnp.int32, (rows, 2 * ATTN_BLOCK), 0) & (ATTN_BLOCK - 1)
    j = lax.broadcasted_iota(jnp.int32, (rows, 2 * ATTN_BLOCK), 1)
    inner = jnp.where((j >= i) & (j <= i + ATTN_BLOCK), 0.0, NEG_BIG)
    return jnp.where((j < ATTN_BLOCK) & first_step, NEG_BIG, inner), inner


def _column_per_head(a):
    return jnp.concatenate([a[:, hd * HEAD_DIM:hd * HEAD_DIM + 1] for hd in range(GROUP_WIDTH // HEAD_DIM)], axis=0)


def _blocks_per_step(nb):
    return 8 if nb % 8 == 0 else 4 if nb % 4 == 0 else 2 if nb % 2 == 0 else 1


def _residues_per_step(dil, nb, qb):
    return 2 if (nb == qb and qb < 8 and dil % 2 == 0) else 1


def _attn_fwd(q, k, v, name, after=()):
    dil, length, _ = q.shape
    nb = length // ATTN_BLOCK
    qb = _blocks_per_step(nb)
    rs = _residues_per_step(dil, nb, qb)

    def body(q_ref, kp_ref, kc_ref, vp_ref, vc_ref, o_ref, lse_ref):
        masks = _head_masks()
        bias = _band_bias(pl.program_id(1) == 0)
        for rr in range(rs):
            for qi in range(qb):
                here = slice(qi * ATTN_BLOCK, (qi + 1) * ATTN_BLOCK)
                before = slice((qi - 1) * ATTN_BLOCK, qi * ATTN_BLOCK)
                kcat = jnp.concatenate([kp_ref[rr] if qi == 0 else kc_ref[rr, before], kc_ref[rr, here]], axis=0)
                vcat = jnp.concatenate([vp_ref[rr] if qi == 0 else vc_ref[rr, before], vc_ref[rr, here]], axis=0)
                qs = _stack_heads(q_ref[rr, here], masks)
                sc = _dot_nt(qs, kcat) + bias[min(qi, 1)]
                m = jnp.max(sc, axis=1, keepdims=True)
                e = jnp.exp(sc - m)
                l = jnp.sum(e, axis=1, keepdims=True)
                p = (e / l).astype(BF16)
                lse = m + jnp.log(l)
                o = jnp.zeros((ATTN_BLOCK, GROUP_WIDTH), F32)
                lse_full = jnp.zeros((ATTN_BLOCK, GROUP_WIDTH), F32)
                for hd, msk in enumerate(masks):
                    rows = slice(hd * ATTN_BLOCK, (hd + 1) * ATTN_BLOCK)
                    o = jnp.where(msk, _dot(p[rows], vcat), o)
                    lse_full = jnp.where(msk, lse[rows], lse_full)
                o_ref[rr, here] = o.astype(o_ref.dtype)
                lse_ref[rr, here] = lse_full

    cur = pl.BlockSpec((rs, qb * ATTN_BLOCK, GROUP_WIDTH), lambda r, j: (r, j, 0))
    prev = pl.BlockSpec((rs, ATTN_BLOCK, GROUP_WIDTH), lambda r, j: (r, jnp.maximum(qb * j - 1, 0), 0))
    return pl.pallas_call(
        _ordered_after(body, 5, after), name=name, grid=(dil // rs, nb // qb),
        in_specs=[cur, prev, cur, prev, cur] + [pl.BlockSpec(memory_space=pl.ANY)] * len(after), out_specs=[cur, cur],
        out_shape=[jax.ShapeDtypeStruct(q.shape, BF16), jax.ShapeDtypeStruct(q.shape, F32)],
        compiler_params=_params(2),
    )(q, k, k, v, v, *after)


def _group_weights(l0, l1, l2):
    m = jnp.maximum(jnp.maximum(l0, l1), l2)
    e0, e1, e2 = jnp.exp(l0 - m), jnp.exp(l1 - m), jnp.exp(l2 - m)
    den = e0 + e1 + e2
    return e0 / den, e1 / den, e2 / den


def _outproj_fwd(h, pool_out, o, lse, w_out, name):
    s = h.shape[0]
    t = _row_tile(s, 512)

    def body(h_ref, po_ref, o0, o1, o2, l0, l1, l2, w_ref, out_ref, a_ref, *stages):
        stages = _pair_stages(stages)
        ov =[_from_residues(r, stages[i], DILATIONS[i]) for i, r in enumerate((o0, o1, o2))]
        lv = [_from_residues(r, stages[3 + i], DILATIONS[i]) for i, r in enumerate((l0, l1, l2))]
        wts = _group_weights(*lv)
        a = jnp.concatenate([po_ref[...]] + [ov[i] * wts[i] for i in range(3)], axis=1).astype(BF16)
        a_ref[...] = a
        out_ref[...] = h_ref[...] + _dot(a, w_ref[...])

    row = lambda w: pl.BlockSpec((t, w), lambda i: (i, 0))
    res = [_residue_spec(dil, t) for dil in DILATIONS]
    return pl.pallas_call(
        body, name=name, grid=(s // t,),
        in_specs=[row(D_MODEL), row(POOL_WIDTH)] + res + res + [pl.BlockSpec((D_MODEL, D_MODEL), lambda i: (0, 0))],
        out_specs=[row(D_MODEL), row(D_MODEL)],
        out_shape=[jax.ShapeDtypeStruct((s, D_MODEL), F32), jax.ShapeDtypeStruct((s, D_MODEL), BF16)],
        scratch_shapes=_stages(t, 6),
        compiler_params=_params(1),
    )(h, pool_out, *o, *lse, w_out)


def _mlp_fwd(h, g, w_up, w_down, name):
    s = h.shape[0]
    t = _row_tile(s, 512)
    nblk = D_FF // FF_BLOCK

    def body(h_ref, g_ref, wu_ref, wd_ref, out_ref, hn_ref, r_ref):
        x = h_ref[...]
        _, _, hn = _rms(x, g_ref[...])
        hb = hn.astype(BF16)
        hn_ref[...] = hb
        acc = None
        for b0 in range(0, nblk, FF_PER_STEP):
            acts = []
            for b in range(b0, b0 + FF_PER_STEP):
                r = jnp.maximum(_dot(hb, wu_ref[b]), 0.0)
                r_ref[:, b * FF_BLOCK:(b + 1) * FF_BLOCK] = r.astype(BF16)
                acts.append((r * r).astype(BF16))
            wd = wd_ref[b0:b0 + FF_PER_STEP].reshape(FF_PER_STEP * FF_BLOCK, D_MODEL)
            part = _dot(jnp.concatenate(acts, axis=1), wd)
            acc = part if acc is None else acc + part
        out_ref[...] = x + acc

    row = lambda w: pl.BlockSpec((t, w), lambda i: (i, 0))
    resident = lambda shape: pl.BlockSpec(shape, lambda i: (0, 0, 0), pipeline_mode=pl.Buffered(1))
    return pl.pallas_call(
        body, name=name, grid=(s // t,),
        in_specs=[row(D_MODEL), pl.BlockSpec((1, D_MODEL), lambda i: (0, 0)),
                  resident((nblk, D_MODEL, FF_BLOCK)), resident((nblk, FF_BLOCK, D_MODEL))],
        out_specs=[row(D_MODEL), row(D_MODEL), row(D_FF)],
        out_shape=[jax.ShapeDtypeStruct((s, D_MODEL), F32), jax.ShapeDtypeStruct((s, D_MODEL), BF16),
                   jax.ShapeDtypeStruct((s, D_FF), BF16)],
        compiler_params=_params(1),
    )(h, g, w_up, w_down)


def _gate_fwd(h, g, w_gate, p, layer, w_ple, name, head=None):
    s = h.shape[0]
    t = _row_tile(s, 512)

    def body(h_ref, g_ref, wg_ref, p_ref, wp_ref, *refs):
        x = h_ref[...]
        _, _, hn = _rms(x, g_ref[...])
        hb = hn.astype(BF16)
        gate = 1.0 / (1.0 + jnp.exp(-_dot(hb, wg_ref[...])))
        pb = p_ref[...].astype(BF16)
        h3 = x + gate * _dot(pb, wp_ref[...])
        if head is None:
            out_ref, hn_ref, gate_ref, pb_ref = refs
            out_ref[...] = h3
        else:
            gf_ref, t_ref, hn_ref, gate_ref, pb_ref, loss_ref, dh_ref, dgf_ref = refs

            @pl.when(pl.program_id(0) == 0)
            def _():
                loss_ref[...] = jnp.zeros_like(loss_ref)
                dgf_ref[...] = jnp.zeros_like(dgf_ref)

            gf = gf_ref[...]
            n, rstd, y = _rms(h3, gf)
            err = y - t_ref[...]
            loss_ref[...] += jnp.sum(err * err) * (0.5 / D_MODEL)
            dh_ref[...], dgf = _rms_bwd(err * (1.0 / D_MODEL), n, rstd, gf)
            dgf_ref[...] += dgf
        hn_ref[...] = hb
        pb_ref[...] = pb
        gate_ref[...] = gate.astype(BF16)

    row = lambda w: pl.BlockSpec((t, w), lambda i: (i, 0))
    full = lambda a, b: pl.BlockSpec((a, b), lambda i: (0, 0))
    in_specs = [row(D_MODEL), full(1, D_MODEL), full(D_MODEL, D_MODEL),
                pl.BlockSpec((None, t, PLE_DIM), lambda i: (layer, i, 0)), full(PLE_DIM, D_MODEL)]
    saved_specs = [row(D_MODEL), row(D_MODEL), row(PLE_DIM)]
    saved_shapes = [jax.ShapeDtypeStruct((s, D_MODEL), BF16), jax.ShapeDtypeStruct((s, D_MODEL), BF16),
                    jax.ShapeDtypeStruct((s, PLE_DIM), BF16)]
    if head is None:
        return pl.pallas_call(
            body, name=name, grid=(s // t,), in_specs=in_specs, out_specs=[row(D_MODEL)] + saved_specs,
            out_shape=[jax.ShapeDtypeStruct((s, D_MODEL), F32)] + saved_shapes, compiler_params=_params(1),
        )(h, g, w_gate, p, w_ple)
    return pl.pallas_call(
        body, name=name, grid=(s // t,), in_specs=in_specs + [full(1, D_MODEL), row(D_MODEL)],
        out_specs=saved_specs + [pl.BlockSpec((1, LANES), lambda i: (0, 0)), row(D_MODEL), full(1, D_MODEL)],
        out_shape=saved_shapes + [jax.ShapeDtypeStruct((1, LANES), F32), jax.ShapeDtypeStruct((s, D_MODEL), F32),
                                  jax.ShapeDtypeStruct((1, D_MODEL), F32)],
        compiler_params=_params(1),
    )(h, g, w_gate, p, w_ple, *head)


def _gate_bwd(dh, gate, pb, w_ple, h, g, w_gate, hn, name, after=()):
    s = h.shape[0]
    t = _row_tile(s, 512)
    last = s // t - 1

    def body(dh_ref, gate_ref, pb_ref, wp_ref, h_ref, g_ref, wg_ref, hn_ref, out_ref, dg_ref, dwg_ref, dwgb_ref,
             dwp_ref, dwpb_ref):
        i = pl.program_id(0)

        @pl.when(i == 0)
        def _():
            dg_ref[...] = jnp.zeros_like(dg_ref)
            dwg_ref[...] = jnp.zeros_like(dwg_ref)
            dwp_ref[...] = jnp.zeros_like(dwp_ref)

        d = dh_ref[...]
        gate = gate_ref[...].astype(F32)
        pb = pb_ref[...]
        e = _dot(pb, wp_ref[...])
        dgl = (d * e * gate * (1.0 - gate)).astype(BF16)
        dwg_ref[...] += _dot_tn(hn_ref[...], dgl)
        dwp_ref[...] += _dot_tn(pb, (d * gate).astype(BF16))
        gv = g_ref[...]
        n, rstd, _ = _rms(h_ref[...], gv)
        dx, dg = _rms_bwd(_dot_nt(dgl, wg_ref[...]), n, rstd, gv)
        out_ref[...] = d + dx
        dg_ref[...] += dg

        @pl.when(i == last)
        def _():
            dwgb_ref[...] = dwg_ref[...].astype(BF16)
            dwpb_ref[...] = dwp_ref[...].astype(BF16)

    row = lambda w: pl.BlockSpec((t, w), lambda i: (i, 0))
    full = lambda a, b: pl.BlockSpec((a, b), lambda i: (0, 0))
    dh2, dg, dwg, dwgb, dwp, dwpb = pl.pallas_call(
        _ordered_after(body, 8, after), name=name, grid=(s // t,),
        in_specs=[row(D_MODEL), row(D_MODEL), row(PLE_DIM), full(PLE_DIM, D_MODEL), row(D_MODEL), full(1, D_MODEL),
                  full(D_MODEL, D_MODEL), row(D_MODEL)] + [pl.BlockSpec(memory_space=pl.ANY)] * len(after),
        out_specs=[row(D_MODEL), full(1, D_MODEL), full(D_MODEL, D_MODEL), full(D_MODEL, D_MODEL),
                   full(PLE_DIM, D_MODEL), full(PLE_DIM, D_MODEL)],
        out_shape=[jax.ShapeDtypeStruct((s, D_MODEL), F32), jax.ShapeDtypeStruct((1, D_MODEL), F32),
                   jax.ShapeDtypeStruct((D_MODEL, D_MODEL), F32), jax.ShapeDtypeStruct((D_MODEL, D_MODEL), BF16),
                   jax.ShapeDtypeStruct((PLE_DIM, D_MODEL), F32), jax.ShapeDtypeStruct((PLE_DIM, D_MODEL), BF16)],
        compiler_params=_params(1),
    )(dh, gate, pb, w_ple, h, g, w_gate, hn, *after)
    return dh2, dg, (dwg, dwgb), (dwp, dwpb)


def _mlp_bwd(dh, r, h, g, w_up, w_down, name):
    s = h.shape[0]
    t = _row_tile(s, MLP_BWD_TILE)
    nblk = D_FF // FF_BLOCK

    def body(dh_ref, r_ref, h_ref, g_ref, wu_ref, wd_ref, out_ref, dup_ref, dg_ref, dhb_ref):
        @pl.when(pl.program_id(0) == 0)
        def _():
            dg_ref[...] = jnp.zeros_like(dg_ref)

        d = dh_ref[...]
        db = d.astype(BF16)
        dhb_ref[...] = db
        back = None
        for b in range(nblk):
            cols = slice(b * FF_BLOCK, (b + 1) * FF_BLOCK)
            dup = (_dot_nt(db, wd_ref[b]) * (2.0 * r_ref[:, cols].astype(F32))).astype(BF16)
            dup_ref[:, cols] = dup
            part = _dot_nt(dup, wu_ref[b])
            back = part if back is None else back + part
        gv = g_ref[...]
        n, rstd, _ = _rms(h_ref[...], gv)
        dx, dg = _rms_bwd(back, n, rstd, gv)
        out_ref[...] = d + dx
        dg_ref[...] += dg

    row = lambda w: pl.BlockSpec((t, w), lambda i: (i, 0))
    vec = pl.BlockSpec((1, D_MODEL), lambda i: (0, 0))
    resident = lambda shape: pl.BlockSpec(shape, lambda i: (0, 0, 0), pipeline_mode=pl.Buffered(1))
    return pl.pallas_call(
        body, name=name, grid=(s // t,),
        in_specs=[row(D_MODEL), row(D_FF), row(D_MODEL), vec,
                  resident((nblk, D_MODEL, FF_BLOCK)), resident((nblk, FF_BLOCK, D_MODEL))],
        out_specs=[row(D_MODEL), row(D_FF), vec, row(D_MODEL)],
        out_shape=[jax.ShapeDtypeStruct((s, D_MODEL), F32), jax.ShapeDtypeStruct((s, D_FF), BF16),
                   jax.ShapeDtypeStruct((1, D_MODEL), F32), jax.ShapeDtypeStruct((s, D_MODEL), BF16)],
        compiler_params=_params(1),
    )(dh, r, h, g, w_up, w_down)


def _outproj_bwd(dh, w_out, o, lse, ones_bd, a, name):
    s = dh.shape[0]
    t = _row_tile(s, 512)
    last = s // t - 1

    def body(dh_ref, w_ref, o0, o1, o2, l0, l1, l2, bd_ref, a_ref, dp_ref, do0, do1, do2, de0, de1, de2, dw_ref,
             dwb_ref, *stages):
        i = pl.program_id(0)

        @pl.when(i == 0)
        def _():
            dw_ref[...] = jnp.zeros_like(dw_ref)

        stages = _pair_stages(stages)
        dhb = dh_ref[...].astype(BF16)
        dw_ref[...] += _dot_tn(a_ref[...], dhb)

        @pl.when(i == last)
        def _():
            dwb_ref[...] = dw_ref[...].astype(BF16)

        da = _dot_nt(dhb, w_ref[...])
        dp_ref[...] = da[:, 0:POOL_WIDTH]
        ov =[_from_residues(r, stages[i], DILATIONS[i]) for i, r in enumerate((o0, o1, o2))]
        lv = [_from_residues(r, stages[3 + i], DILATIONS[i]) for i, r in enumerate((l0, l1, l2))]
        wts = _group_weights(*lv)
        bd = bd_ref[...]
        cbar = jnp.zeros((t, GROUP_WIDTH), F32)
        for grp, do_ref in enumerate((do0, do1, do2)):
            lo = POOL_WIDTH + grp * GROUP_WIDTH
            dag = da[:, lo:lo + GROUP_WIDTH]
            _to_residues(dag * wts[grp], stages[6 + grp], do_ref, DILATIONS[grp])
            prod = dag * ov[grp]
            hi = prod.astype(BF16)
            low = (prod - hi.astype(F32)).astype(BF16)
            cbar = cbar + wts[grp] * (_dot(hi, bd) + _dot(low, bd))
        for grp, de_ref in enumerate((de0, de1, de2)):
            _to_residues(wts[grp] * cbar, stages[9 + grp], de_ref, DILATIONS[grp])

    row = lambda w: pl.BlockSpec((t, w), lambda i: (i, 0))
    full = lambda a, b: pl.BlockSpec((a, b), lambda i: (0, 0))
    res = [_residue_spec(dil, t) for dil in DILATIONS]
    *outs, dw, dwb = pl.pallas_call(
        body, name=name, grid=(s // t,),
        in_specs=[row(D_MODEL), full(D_MODEL, D_MODEL)] + res + res + [full(GROUP_WIDTH, GROUP_WIDTH), row(D_MODEL)],
        out_specs=[row(POOL_WIDTH)] + res + res + [full(D_MODEL, D_MODEL)] * 2,
        out_shape=[jax.ShapeDtypeStruct((s, POOL_WIDTH), F32)] + [_residue_shape(dil, s, BF16) for dil in DILATIONS]
        + [_residue_shape(dil, s, F32) for dil in DILATIONS]
        + [jax.ShapeDtypeStruct((D_MODEL, D_MODEL), F32), jax.ShapeDtypeStruct((D_MODEL, D_MODEL), BF16)],
        scratch_shapes=_stages(t, 12),
        compiler_params=_params(1),
    )(dh, w_out, *o, *lse, ones_bd, a)
    return (*outs, (dw, dwb))


def _attn_bwd(q, k, v, do, lse, deff, name, after=()):
    dil, length, _ = q.shape
    nb = length // ATTN_BLOCK
    qb = _blocks_per_step(nb)
    nj = nb // qb
    rs = _residues_per_step(dil, nb, qb)
    whole = nj == 1
    tail = slice((qb - 1) * ATTN_BLOCK, qb * ATTN_BLOCK)
    block = lambda qi: slice(qi * ATTN_BLOCK, (qi + 1) * ATTN_BLOCK)

    def body(q_ref, kp_ref, kc_ref, vp_ref, vc_ref, do_ref, lse_ref, de_ref, dq_ref, dk_ref, dv_ref, ck, cv):
        j = pl.program_id(1)

        def compute():
            masks = _head_masks()
            bias = _band_bias(j == 0)
            for rr in range(rs):
                dkc, dvc = [], []
                for qi in range(qb):
                    here, before = block(qi), block(qi - 1)
                    kcat = jnp.concatenate([kp_ref[rr] if qi == 0 else kc_ref[rr, before], kc_ref[rr, here]], axis=0)
                    vcat = jnp.concatenate([vp_ref[rr] if qi == 0 else vc_ref[rr, before], vc_ref[rr, here]], axis=0)
                    qs = _stack_heads(q_ref[rr, here], masks)
                    dos = _stack_heads(do_ref[rr, here], masks)
                    sc = _dot_nt(qs, kcat) + bias[min(qi, 1)]
                    p = jnp.exp(sc - _column_per_head(lse_ref[rr, here]))
                    ds = (p * (_dot_nt(dos, vcat) - _column_per_head(de_ref[rr, here]))).astype(BF16)
                    dq = jnp.zeros((ATTN_BLOCK, GROUP_WIDTH), F32)
                    for hd, msk in enumerate(masks):
                        dq = jnp.where(msk, _dot(ds[block(hd)], kcat), dq)
                    dq_ref[rr, here] = dq.astype(dq_ref.dtype)
                    dkc.append(_dot_tn(ds, qs))
                    dvc.append(_dot_tn(p.astype(BF16), dos))

                for out_ref, carry, parts in ((dk_ref, ck, dkc), (dv_ref, cv, dvc)):
                    full = [parts[qi][ATTN_BLOCK:] + parts[qi + 1][0:ATTN_BLOCK] for qi in range(qb - 1)]
                    if whole:
                        for qi, val in enumerate(full + [parts[qb - 1][ATTN_BLOCK:]]):
                            out_ref[rr, block(qi)] = val.astype(out_ref.dtype)
                        continue

                    @pl.when(j > 0)
                    def _():
                        if qb > 1:
                            out_ref[0, 0:(qb - 1) * ATTN_BLOCK] = carry[0:(qb - 1) * ATTN_BLOCK].astype(out_ref.dtype)
                        out_ref[0, tail] = (carry[tail] + parts[0][0:ATTN_BLOCK]).astype(out_ref.dtype)

                    for qi, val in enumerate(full):
                        carry[block(qi)] = val
                    carry[tail] = parts[qb - 1][ATTN_BLOCK:]

        if whole:
            compute()
        else:
            pl.when(j < nj)(compute)

            @pl.when(j == nj)
            def _():
                dk_ref[0] = ck[...].astype(dk_ref.dtype)
                dv_ref[0] = cv[...].astype(dv_ref.dtype)

    step = lambda j: jnp.minimum(j, nj - 1)
    cur = pl.BlockSpec((rs, qb * ATTN_BLOCK, GROUP_WIDTH), lambda r, j: (r, step(j), 0))
    prev = pl.BlockSpec((rs, ATTN_BLOCK, GROUP_WIDTH), lambda r, j: (r, jnp.maximum(qb * step(j) - 1, 0), 0))
    late = pl.BlockSpec((rs, qb * ATTN_BLOCK, GROUP_WIDTH), lambda r, j: (r, jnp.maximum(j - 1, 0), 0))
    return pl.pallas_call(
        _ordered_after(body, 8, after), name=name, grid=(dil // rs, 1 if whole else nj + 1),
        in_specs=[cur, prev, cur, prev, cur, cur, cur, cur] + [pl.BlockSpec(memory_space=pl.ANY)] * len(after),
        out_specs=[cur, cur if whole else late, cur if whole else late],
        out_shape=[jax.ShapeDtypeStruct(q.shape, BF16)] * 3,
        scratch_shapes=[pltpu.VMEM((qb * ATTN_BLOCK, GROUP_WIDTH), F32)] * 2,
        compiler_params=_params(2),
    )(q, k, k, v, v, do, lse, deff, *after)


def _pool_bwd(dpool, y, w_bd, scale, name, after=()):
    s = dpool.shape[0]
    t = _row_tile(s, 512)
    nt = s // t

    def body(dp_ref, y_ref, w_ref, sc_ref, du_ref, dw_ref, dsc_ref, ext, b2, b4, b8):
        i = pl.program_id(0)

        @pl.when(i == 0)
        def _():
            ext[t:, :] = jnp.zeros((POOL_HALO + POOL_PAD, POOL_WIDTH), F32)
            for buf in (b2, b4):
                buf[t + POOL_HALO:, :] = jnp.zeros((POOL_PAD, POOL_WIDTH), F32)
            dw_ref[...] = jnp.zeros_like(dw_ref)
            dsc_ref[...] = jnp.zeros_like(dsc_ref)

        dp = dp_ref[...]
        yb = y_ref[...]
        w = w_ref[...]
        dsc_ref[...] += jnp.sum(dp * _dot(yb, w), axis=0, keepdims=True)
        dyo = (dp * sc_ref[...]).astype(BF16)
        dw_ref[...] += _dot_tn(yb, dyo)
        dy = _dot_nt(dyo, w)
        win = _pool_lane_window()
        pos = (nt - 1 - i) * t + lax.broadcasted_iota(jnp.int32, (t, POOL_WIDTH), 0)
        gq = dy / jnp.minimum(pos + 1, win).astype(F32)
        ext[0:t, :] = gq
        du_ref[...] = _window_sums(ext, b2, b4, b8, t, 0, 0, 1) - dy
        ext[t:t + POOL_HALO, :] = gq[0:POOL_HALO, :]

    rev = pl.BlockSpec((t, POOL_WIDTH), lambda i: (nt - 1 - i, 0))
    full = lambda a, b: pl.BlockSpec((a, b), lambda i: (0, 0))
    return pl.pallas_call(
        _ordered_after(body, 4, after), name=name, grid=(nt,),
        in_specs=[rev, rev, full(POOL_WIDTH, POOL_WIDTH), full(1, POOL_WIDTH)]
        + [pl.BlockSpec(memory_space=pl.ANY)] * len(after),
        out_specs=[rev, full(POOL_WIDTH, POOL_WIDTH), full(1, POOL_WIDTH)],
        out_shape=[jax.ShapeDtypeStruct((s, POOL_WIDTH), F32), jax.ShapeDtypeStruct((POOL_WIDTH, POOL_WIDTH), F32),
                   jax.ShapeDtypeStruct((1, POOL_WIDTH), F32)],
        scratch_shapes=[pltpu.VMEM((t + POOL_HALO + POOL_PAD, POOL_WIDTH), F32)] * 4,
        compiler_params=_params(1),
    )(dpool, y, w_bd, scale, *after)


def _normproj_bwd(dh, du, dq, dk, dv, rc, rsa, rsb, w_in, h, g, name):
    s = h.shape[0]
    t = _row_tile(s, 512)

    def body(dh_ref, du_ref, q0, q1, q2, k0, k1, k2, v0, v1, v2, c_ref, sa_ref, sb_ref, w_ref, h_ref, g_ref,
             out_ref, dz_ref, dg_ref, *stages):
        @pl.when(pl.program_id(0) == 0)
        def _():
            dg_ref[...] = jnp.zeros_like(dg_ref)

        c, sa, sb = c_ref[...], sa_ref[...], sb_ref[...]

        def unrot(a, scale):
            halves = [_rot_t(a[:, hf * LANES:(hf + 1) * LANES] * scale, c, sa, sb) for hf in range(2)]
            return jnp.concatenate(halves, axis=1)

        staged = _pair_stages(stages)
        tok = lambda refs, base: [_from_residues(r, staged[base + i], DILATIONS[i]) for i, r in enumerate(refs)]
        chunks = [du_ref[...]]
        chunks += [unrot(a, HEAD_DIM ** -0.5) for a in tok((q0, q1, q2), 0)]
        chunks += [unrot(a, 1.0) for a in tok((k0, k1, k2), 3)]
        chunks += tok((v0, v1, v2), 6)
        acc = jnp.zeros((t, D_MODEL), F32)
        for ci, ch in enumerate(chunks):
            cols = slice(ci * GROUP_WIDTH, (ci + 1) * GROUP_WIDTH)
            cb = ch.astype(BF16)
            dz_ref[:, cols] = cb
            acc = acc + _dot_nt(cb, w_ref[:, cols])
        gv = g_ref[...]
        n, rstd, _ = _rms(h_ref[...], gv)
        dx, dg = _rms_bwd(acc, n, rstd, gv)
        out_ref[...] = dh_ref[...] + dx
        dg_ref[...] += dg

    row = lambda w: pl.BlockSpec((t, w), lambda i: (i, 0))
    vec = pl.BlockSpec((1, D_MODEL), lambda i: (0, 0))
    res = [_residue_spec(dil, t) for dil in DILATIONS]
    return pl.pallas_call(
        body, name=name, grid=(s // t,),
        in_specs=[row(D_MODEL), row(POOL_WIDTH)] + res * 3 + _table_specs(t)
        + [pl.BlockSpec((D_MODEL, N_IN), lambda i: (0, 0)), row(D_MODEL), vec],
        out_specs=[row(D_MODEL), row(N_IN), vec],
        out_shape=[jax.ShapeDtypeStruct((s, D_MODEL), F32), jax.ShapeDtypeStruct((s, N_IN), BF16),
                   jax.ShapeDtypeStruct((1, D_MODEL), F32)],
        scratch_shapes=_stages(t, 9),
        compiler_params=_params(1),
    )(dh, du, *dq, *dk, *dv, rc, rsa, rsb, w_in, h, g)


def _matmul_tn(a, b, name, *, square_a=False, tn=None, blocked_out=False, after=()):
    s, m = a.shape
    n = b.shape[1]
    tk = _row_tile(s, 2048)
    tm = min(m, 1024)
    tn = tn or min(n, 1024)
    assert m % tm == 0 and n % tn == 0
    nk = s // tk
    nsub = tn // FF_BLOCK if blocked_out else 1

    def body(a_ref, b_ref, o_ref, ob_ref, acc):
        k = pl.program_id(2)

        def product():
            av = a_ref[...]
            if square_a:
                av = av.astype(F32)
                av = av * av
            return _dot_tn(av.astype(BF16), b_ref[...].astype(BF16))

        def emit(total):
            if blocked_out:
                for sub in range(nsub):
                    cols = slice(sub * FF_BLOCK, (sub + 1) * FF_BLOCK)
                    o_ref[sub] = total[:, cols]
                    ob_ref[sub] = total[:, cols].astype(BF16)
            else:
                o_ref[...] = total
                ob_ref[...] = total.astype(BF16)

        if nk == 1:
            emit(product())
            return

        @pl.when(k == 0)
        def _():
            acc[...] = product()

        @pl.when((k > 0) & (k < nk - 1))
        def _():
            acc[...] += product()

        @pl.when(k == nk - 1)
        def _():
            emit(acc[...] + product())

    if blocked_out:
        shape = (n // FF_BLOCK, m, FF_BLOCK)
        out_spec = pl.BlockSpec((nsub, tm, FF_BLOCK), lambda i, j, k: (j, i, 0))
    else:
        shape = (m, n)
        out_spec = pl.BlockSpec((tm, tn), lambda i, j, k: (i, j))
    return pl.pallas_call(
        _ordered_after(body, 2, after), name=name, grid=(m // tm, n // tn, nk),
        in_specs=[pl.BlockSpec((tk, tm), lambda i, j, k: (k, i)), pl.BlockSpec((tk, tn), lambda i, j, k: (k, j))]
        + [pl.BlockSpec(memory_space=pl.ANY)] * len(after),
        out_specs=[out_spec, out_spec],
        out_shape=[jax.ShapeDtypeStruct(shape, F32), jax.ShapeDtypeStruct(shape, BF16)],
        scratch_shapes=[pltpu.VMEM((tm, tn), F32)],
        compiler_params=_params(3),
    )(a, b, *after)


def _adamw_math(w, g, m, v):
    m = ADAM_B1 * m + (1.0 - ADAM_B1) * g
    v = ADAM_B2 * v + (1.0 - ADAM_B2) * (g * g)
    m_hat = m / (1.0 - ADAM_B1 ** ADAM_STEP)
    v_hat = v / (1.0 - ADAM_B2 ** ADAM_STEP)
    delta = -ADAM_LR * (m_hat / (jnp.sqrt(v_hat) + ADAM_EPS) + ADAM_WD * w)
    return delta, m, v


def _adamw_sharded(w, m, v, own, recv0, recv1, name):
    _, rows, cols = w.shape
    t = _row_tile(rows, 256)

    def body(w_ref, m_ref, v_ref, own_ref, r0_ref, r1_ref, g_ref, d_ref, nm_ref, nv_ref):
        layer0 = pl.program_id(0) == 0
        g = own_ref[...]
        for k in range(N_DEV - 1):
            g = g + jnp.where(layer0, r0_ref[k], r1_ref[k]).astype(F32)
        g_ref[...] = g
        d_ref[...], nm_ref[...], nv_ref[...] = _adamw_math(w_ref[...], g, m_ref[...], v_ref[...])

    blk = pl.BlockSpec((None, t, cols), lambda l, i: (l, i, 0))
    recv = lambda layer: pl.BlockSpec((N_DEV - 1, t, cols), lambda l, i: (0, jnp.where(l == layer, i, 0), 0))
    return pl.pallas_call(
        body, name=name, grid=(2, rows // t),
        in_specs=[blk, blk, blk, blk, recv(0), recv(1)], out_specs=[blk] * 4,
        out_shape=[jax.ShapeDtypeStruct(w.shape, F32)] * 4,
        compiler_params=_params(2),
    )(w, m, v, own, recv0, recv1)


def _adamw_packed(w, g8, m, v, name):
    def body(w_ref, g_ref, m_ref, v_ref, go_ref, d_ref, nm_ref, nv_ref):
        g = g_ref[0]
        for dev in range(1, N_DEV):
            g = g + g_ref[dev]
        go_ref[...] = g
        d_ref[...], nm_ref[...], nv_ref[...] = _adamw_math(w_ref[...], g, m_ref[...], v_ref[...])

    return pl.pallas_call(
        body, name=name, out_shape=[jax.ShapeDtypeStruct(w.shape, F32)] * 4,
        compiler_params=pltpu.CompilerParams(vmem_limit_bytes=VMEM_LIMIT),
    )(w, g8, m, v)


def _peer(k):
    x, y, c = lax.axis_index("x"), lax.axis_index("y"), lax.axis_index("c")
    return (1 - x if k & 4 else x, 1 - y if k & 2 else y, 1 - c if k & 1 else c)


def _linear(dev):
    return 4 * dev[0] + 2 * dev[1] + dev[2]


HBM_SPEC = pl.BlockSpec(memory_space=pltpu.HBM)
SEM_SPEC = pl.BlockSpec(memory_space=pltpu.SEMAPHORE)
ANY_SPEC = pl.BlockSpec(memory_space=pl.ANY)
EFFECT = pltpu.SideEffectType.DATAFLOW_SIDE_EFFECTING


def _in_hbm(a):
    return pltpu.with_memory_space_constraint(a, pltpu.HBM)


class _Exchange:
    def __init__(self, name, groups, scatter, after=()):
        self.name, self.scatter = name, scatter
        self.sizes = sizes = [len(g) for g in groups]
        srcs = [a for g in groups for a in g]
        n, ng = len(srcs), len(groups)
        lead = (N_DEV - 1,) if scatter else (N_DEV,)
        shapes = [lead + (a.shape[1:] if scatter else a.shape) for a in srcs]
        lands = [lax.empty(sh, a.dtype) for sh, a in zip(shapes, srcs)]
        offsets = [sum(sizes[:gi]) for gi in range(ng)]
        copy = self._copy

        def body(*refs):
            src, land = refs[:n], refs[n:2 * n]
            sems = refs[2 * n + len(after):2 * n + len(after) + 2 * ng]
            token = refs[-1]
            for gi in range(ng):
                for wi in range(sizes[gi]):
                    w = offsets[gi] + wi
                    for k in range(1, N_DEV):
                        copy(src[w], land[w], sems[2 * gi], sems[2 * gi + 1], wi, k).start()
            token[...] = jnp.zeros_like(token)

        sem_shapes = [pltpu.SemaphoreType.DMA((7 * sz,)) for sz in sizes for _ in range(2)]
        outs = pl.pallas_call(
            body, name=name + "_start",
            in_specs=[HBM_SPEC] * (2 * n) + [ANY_SPEC] * len(after),
            out_specs=[SEM_SPEC] * (2 * ng) + [HBM_SPEC] * (2 * n) + [pl.BlockSpec(memory_space=pltpu.VMEM)],
            out_shape=sem_shapes + [pltpu.HBM(a.shape, a.dtype) for a in srcs + lands]
            + [jax.ShapeDtypeStruct((8, LANES), F32)],
            input_output_aliases={i: 2 * ng + i for i in range(2 * n)},
            compiler_params=pltpu.CompilerParams(has_side_effects=EFFECT),
        )(*[_in_hbm(a) for a in srcs + lands], *after)
        self.sems = [outs[2 * gi:2 * gi + 2] for gi in range(ng)]
        thru = outs[2 * ng:2 * ng + 2 * n]
        self.srcs = [thru[offsets[gi]:offsets[gi] + sizes[gi]] for gi in range(ng)]
        self.lands = [thru[n + offsets[gi]:n + offsets[gi] + sizes[gi]] for gi in range(ng)]
        self.token = outs[-1]

    def _copy(self, src, land, send_sems, recv_sems, wi, k):
        to = _peer(k)
        if self.scatter:
            src_ref, dst_ref = src.at[_linear(to)], land.at[k - 1]
        else:
            src_ref, dst_ref = src, land.at[_linear(_peer(0))]
        return pltpu.make_async_remote_copy(
            src_ref=src_ref, dst_ref=dst_ref, send_sem=send_sems.at[7 * wi + k - 1],
            recv_sem=recv_sems.at[7 * wi + k - 1], device_id=to, device_id_type=MESH)

    def wait(self, gi, after):
        n = self.sizes[gi]
        copy = self._copy

        def body(*refs):
            src, land = refs[:n], refs[n:2 * n]
            send_sems, recv_sems = refs[2 * n], refs[2 * n + 1]
            for wi in range(n):
                for k in range(1, N_DEV):
                    cp = copy(src[wi], land[wi], send_sems, recv_sems, wi, k)
                    cp.wait_send()
                    cp.wait_recv()

        arrays = list(self.srcs[gi]) + list(self.lands[gi])
        outs = pl.pallas_call(
            body, name=f"{self.name}_wait{gi}",
            in_specs=[HBM_SPEC] * (2 * n) + [SEM_SPEC, SEM_SPEC] + [ANY_SPEC] * len(after),
            out_specs=[HBM_SPEC] * (2 * n),
            out_shape=[pltpu.HBM(a.shape, a.dtype) for a in arrays],
            input_output_aliases={i: i for i in range(2 * n)},
            compiler_params=pltpu.CompilerParams(has_side_effects=EFFECT),
        )(*arrays, *self.sems[gi], *after)
        return outs[:n], outs[n:]


def _rotary_tables(positions):
    rot_dim = HEAD_DIM // 4
    inv_freq = ROPE_THETA ** (-jnp.arange(0, rot_dim, 2, dtype=F32) / rot_dim)
    ang = positions.astype(F32)[:, None] * inv_freq
    cs = jnp.concatenate([jnp.cos(ang), jnp.sin(ang)], axis=1)
    dim = jnp.arange(LANES) % HEAD_DIM
    first, second = dim < ROT_SHIFT, (dim >= ROT_SHIFT) & (dim < rot_dim)
    src = jnp.arange(2 * ROT_SHIFT)[:, None]
    angle = (dim % ROT_SHIFT)[None, :]
    c = jnp.where((first | second)[None, :] & (src == angle), 1.0, 0.0)
    sa = jnp.where(second[None, :] & (src == angle + ROT_SHIFT), 1.0, 0.0)
    sb = jnp.where(first[None, :] & (src == angle + ROT_SHIFT), -1.0, 0.0)
    spread = jnp.concatenate([c, sa, sb], axis=1).astype(F32)
    base = jnp.concatenate([jnp.where(first | second, 0.0, 1.0), jnp.zeros((2 * LANES,))]).astype(F32)[None, :]
    return jnp.dot(cs, spread, precision=lax.Precision.HIGHEST, preferred_element_type=F32) + base


def _block_diag(pool_w):
    gc = pool_w.shape[-1]
    out = jnp.zeros((POOL_WIDTH, POOL_WIDTH), pool_w.dtype)
    for grp in range(pool_w.shape[0]):
        out = lax.dynamic_update_slice(out, pool_w[grp], (grp * gc, grp * gc))
    return out


def _diag_blocks(a):
    gc = POOL_WIDTH // len(POOL_WINDOWS)
    return jnp.stack([a[grp * gc:(grp + 1) * gc, grp * gc:(grp + 1) * gc] for grp in range(len(POOL_WINDOWS))])


def _local_step(x, p, positions, loss_target, norm1, pool_w, pool_scale, norm2, norm3, final_norm, weights, send):
    rc = rsa = rsb = _rotary_tables(positions)
    ones_bd = _block_diag(jnp.ones((4, HEAD_DIM, HEAD_DIM), BF16))
    saved = []
    h = x
    for i in range(2):
        tag = f"_l{i}"
        g1, g2, g3 = norm1[i:i + 1], norm2[i:i + 1], norm3[i:i + 1]
        w_bd = _block_diag(pool_w[i]).astype(BF16)
        scale = pool_scale[i:i + 1]
        w_in = weights(i, "in", (h, rc, w_bd))
        hn1, u, *qkv = _normproj_fwd(h, g1, w_in, rc, rsa, rsb, "normproj_fwd" + tag)
        qkv = [qkv[3 * grp:3 * grp + 3] for grp in range(3)]
        started = weights(i, "prefetch", (hn1,))
        pool_out, y = _pool_fwd(u, w_bd, scale, "pool_fwd" + tag, after=started)
        o, lse = zip(*[_attn_fwd(*qkv[grp], f"attn_fwd{tag}_g{grp}", after=started) for grp in range(3)])
        w_out = weights(i, "out", (pool_out, *o))
        h1, a = _outproj_fwd(h, pool_out, o, lse, w_out, "outproj_fwd" + tag)
        w_up, w_down, w_gate, w_ple = weights(i, "rest", (h1,))
        h2, hn2, r = _mlp_fwd(h1, g2, w_up, w_down, "mlp_fwd" + tag)
        h0 = h
        if i == 0:
            h, hn3, gate, pb = _gate_fwd(h2, g3, w_gate, p, i, w_ple, "gate_fwd" + tag)
        else:
            hn3, gate, pb, loss, dh, d_final = _gate_fwd(h2, g3, w_gate, p, i, w_ple, "gate_fwd" + tag,
                                                         head=(final_norm.reshape(1, D_MODEL), loss_target))
        saved.append(dict(h0=h0, hn1=hn1, qkv=qkv, y=y, o=o, lse=lse, a=a, h1=h1, hn2=hn2, r=r, h2=h2,
                          hn3=hn3, gate=gate, pb=pb, w_bd=w_bd, scale=scale, g1=g1, g2=g2, g3=g3,
                          w_in=w_in, w_out=w_out, w_up=w_up, w_down=w_down, w_gate=w_gate, w_ple=w_ple))

    grads = [None, None]
    sent = ()
    for i in (1, 0):
        tag = f"_l{i}"
        sv = saved[i]
        dh2, dg3, dw_gate, dw_ple = _gate_bwd(dh, sv["gate"], sv["pb"], sv["w_ple"], sv["h2"], sv["g3"], sv["w_gate"],
                                              sv["hn3"], "gate_bwd" + tag, after=sent)
        dh1, dup, dg2, dh2b = _mlp_bwd(dh2, sv["r"], sv["h1"], sv["g2"], sv["w_up"], sv["w_down"], "mlp_bwd" + tag)
        dw_down = _matmul_tn(sv["r"], dh2b, "dw_down" + tag, square_a=True)
        dw_up = _matmul_tn(sv["hn2"], dup, "dw_up" + tag, blocked_out=True)
        dpool, do0, do1, do2, de0, de1, de2, dw_out = _outproj_bwd(dh1, sv["w_out"], sv["o"], sv["lse"], ones_bd,
                                                                   sv["a"], "outproj_bwd" + tag)
        sent = send(i, "main", dict(w_gate=dw_gate, w_ple=dw_ple, w_down=dw_down, w_up=dw_up, w_out=dw_out))
        dqkv = [_attn_bwd(*sv["qkv"][grp], do_g, sv["lse"][grp], de_g, f"attn_bwd{tag}_g{grp}", after=sent)
                for grp, (do_g, de_g) in enumerate(((do0, de0), (do1, de1), (do2, de2)))]
        dq, dk, dv = zip(*dqkv)
        du, dw_bd, dscale = _pool_bwd(dpool, sv["y"], sv["w_bd"], sv["scale"], "pool_bwd" + tag, after=sent)
        dh, dz, dg1 = _normproj_bwd(dh1, du, dq, dk, dv, rc, rsa, rsb, sv["w_in"], sv["h0"], sv["g1"],
                                    "normproj_bwd" + tag)
        grads[i] = dict(norm1=dg1, norm2=dg2, norm3=dg3, pool_w=_diag_blocks(dw_bd), pool_scale=dscale)
        small_sent = send(0, "small", (grads, d_final, loss)) if i == 0 else ()
        dw_in = _matmul_tn(sv["hn1"], dz, "dw_in" + tag, tn=N_IN // 2, after=small_sent)
        sent = send(i, "in", dict(w_in=dw_in))
    return dh, sent


def _pack_small(norm1, norm2, norm3, final_norm, pool_scale, pool_w, spare=None):
    spare = jnp.zeros((1, LANES), F32) if spare is None else spare
    scale_row = jnp.concatenate([pool_scale.reshape(1, 2 * POOL_WIDTH), spare,
                                 jnp.zeros((1, D_MODEL - 2 * POOL_WIDTH - LANES), F32)], axis=1)
    return jnp.concatenate([norm1, norm2, norm3, final_norm.reshape(1, D_MODEL), scale_row,
                            pool_w.reshape(32, D_MODEL)], axis=0)


def _unpack_small(a):
    return dict(norm1=a[0:2], norm2=a[2:4], norm3=a[4:6], final_norm=a[6], pool_scale=a[7, 0:2 * POOL_WIDTH].reshape(2, POOL_WIDTH),
                pool_w=a[8:40].reshape(2, 4, HEAD_DIM, HEAD_DIM))


def _chunks_cols(a, cols):
    return a.reshape(a.shape[0], N_DEV, cols).transpose(1, 0, 2)


def _chunks_rows(a, rows):
    return a.reshape(N_DEV, rows, a.shape[1])


BIG = ("w_in", "w_out", "w_up", "w_down", "w_gate", "w_ple")
SMALL = ("norm1", "norm2", "norm3", "final_norm", "pool_scale", "pool_w")
ORDER = ("norm1", "w_in", "pool_w", "pool_scale", "w_out", "norm2", "w_up", "w_down", "norm3", "w_gate", "w_ple",
         "final_norm")


def kernel(x, p, positions, norm1, w_in, pool_w, pool_scale, w_out, norm2, w_up, w_down, norm3, w_gate, w_ple, final_norm, loss_target, m_norm1, m_w_in, m_pool_w, m_pool_scale, m_w_out, m_norm2, m_w_up, m_w_down, m_norm3, m_w_gate, m_w_ple, m_final_norm, v_norm1, v_w_in, v_pool_w, v_pool_scale, v_w_out, v_norm2, v_w_up, v_w_down, v_norm3, v_w_gate, v_w_ple, v_final_norm):
    w = dict(norm1=norm1, w_in=w_in, pool_w=pool_w, pool_scale=pool_scale, w_out=w_out, norm2=norm2, w_up=w_up,
             w_down=w_down, norm3=norm3, w_gate=w_gate, w_ple=w_ple, final_norm=final_norm)
    m = dict(norm1=m_norm1, w_in=m_w_in, pool_w=m_pool_w, pool_scale=m_pool_scale, w_out=m_w_out, norm2=m_norm2,
             w_up=m_w_up, w_down=m_w_down, norm3=m_norm3, w_gate=m_w_gate, w_ple=m_w_ple, final_norm=m_final_norm)
    v = dict(norm1=v_norm1, w_in=v_w_in, pool_w=v_pool_w, pool_scale=v_pool_scale, w_out=v_w_out, norm2=v_norm2,
             w_up=v_w_up, w_down=v_w_down, norm3=v_norm3, w_gate=v_w_gate, w_ple=v_w_ple, final_norm=v_final_norm)
    seq = x.shape[1]

    bf = {n: [w[n][layer].astype(BF16) for layer in range(2)] for n in BIG}
    rest = ("w_up", "w_down", "w_gate", "w_ple")
    me = 4 * lax.axis_index("x") + 2 * lax.axis_index("y") + lax.axis_index("c")
    gathers = [_Exchange("gather_l0", [[bf["w_in"][0]], [bf["w_out"][0]], [bf[n][0] for n in rest]], scatter=False)]
    unpack = dict(w_in=lambda a: a.transpose(1, 0, 2).reshape(D_MODEL, N_IN),
                  w_out=lambda a: a.reshape(D_MODEL, D_MODEL), w_gate=lambda a: a.reshape(D_MODEL, D_MODEL),
                  w_ple=lambda a: a.transpose(1, 0, 2).reshape(PLE_DIM, D_MODEL), w_up=lambda a: a, w_down=lambda a: a)
    parts = dict(zip(("in", "out", "rest"), (("w_in",), ("w_out",), rest)))

    def weights(layer, part, after):
        if part == "prefetch":
            if layer != 0:
                return ()
            gathers.append(_Exchange("gather_l1", [[bf[n][1] for n in parts[pt]] for pt in parts], scatter=False,
                                     after=after))
            return (gathers[1].token,)
        shards, lands = gathers[layer].wait(tuple(parts).index(part), after)
        full = [unpack[n](lax.dynamic_update_slice_in_dim(land, shard[None], me, axis=0))
                for n, shard, land in zip(parts[part], shards, lands)]
        return full if part == "rest" else full[0]

    to_chunks = dict(w_in=lambda a: _chunks_cols(a, N_IN // N_DEV), w_out=lambda a: _chunks_rows(a, D_MODEL // N_DEV),
                     w_up=lambda a: a, w_down=lambda a: _chunks_rows(a, FF_BLOCK),
                     w_gate=lambda a: _chunks_rows(a, D_MODEL // N_DEV), w_ple=lambda a: _chunks_cols(a, D_MODEL // N_DEV))
    own = {n: [None, None] for n in BIG}
    scatters = {}

    def own_chunk(n, g32):
        if n in ("w_in", "w_ple"):
            cols = g32.shape[1] // N_DEV
            return lax.dynamic_slice(g32, (0, me * cols), (g32.shape[0], cols))
        return lax.dynamic_index_in_dim(to_chunks[n](g32), me, axis=0, keepdims=False)

    def send(layer, part, grads):
        if part == "small":
            per_layer, d_final, loss = grads
            pack = _pack_small(
                *[jnp.concatenate([per_layer[0][n], per_layer[1][n]], axis=0) for n in ("norm1", "norm2", "norm3")],
                d_final.reshape(D_MODEL),
                jnp.concatenate([per_layer[0]["pool_scale"], per_layer[1]["pool_scale"]], axis=0),
                jnp.stack([per_layer[0]["pool_w"], per_layer[1]["pool_w"]]), spare=loss)
            scatters["small"] = _Exchange("gather_small", [[pack]], scatter=False)
            return (scatters["small"].token,)
        for n, (g32, _) in grads.items():
            own[n][layer] = own_chunk(n, g32)
        ex = _Exchange(f"scatter_{part}_l{layer}", [[to_chunks[n](g16) for n, (_, g16) in grads.items()]], scatter=True)
        scatters[layer, part] = (tuple(grads), ex)
        return (ex.token,)

    dx, sent = _local_step(
        x.reshape(seq, D_MODEL), p.reshape(2, seq, PLE_DIM), positions.reshape(seq), loss_target.reshape(seq, D_MODEL),
        norm1, pool_w, pool_scale, norm2, norm3, final_norm, weights, send)

    g_out, d_out, m_out, v_out = {}, {}, {}, {}
    for part in ("main", "in"):
        recv = {}
        for layer in (1, 0):
            names, ex = scatters[layer, part]
            for n, r in zip(names, ex.wait(0, sent)[1]):
                recv[n, layer] = r
        for n in names:
            g_out[n], d_out[n], m_out[n], v_out[n] = _adamw_sharded(
                w[n], m[n], v[n], jnp.stack(own[n]), recv[n, 0], recv[n, 1], "adamw_" + n)
        sent = tuple(d_out[n] for n in names)
    (mine,), (landed,) = scatters["small"].wait(0, sent)
    small_g8 = lax.dynamic_update_slice_in_dim(landed, mine[None], me, axis=0)
    pack = lambda t: _pack_small(*[t[n] for n in SMALL])
    small_g, d_small, m_small, v_small = _adamw_packed(pack(w), small_g8, pack(m), pack(v), "adamw_small")
    for dst, a in ((g_out, small_g), (d_out, d_small), (m_out, m_small), (v_out, v_small)):
        dst.update(_unpack_small(a))

    return (small_g[7, 2 * POOL_WIDTH],dx.reshape(1, seq, D_MODEL), *[g_out[n] for n in ORDER], *[d_out[n] for n in ORDER],
            *[m_out[n] for n in ORDER], *[v_out[n] for n in ORDER])
```

```python
import functools

import jax
import jax.numpy as jnp
from jax import lax
from jax.experimental import pallas as pl
from jax.experimental.pallas import tpu as pltpu

F32 = jnp.float32
BF16 = jnp.bfloat16

D_MODEL = 1024
HEAD_DIM = 64
POOL_WIDTH = 256
POOL_WINDOWS = (2, 4, 8, 16)
POOL_HALO = 16
POOL_PAD = 8
GROUP_WIDTH = 256
DILATIONS = (1, 4, 16)
ATTN_BLOCK = 128
ROT_SHIFT = 8
ROPE_THETA = 500000.0
D_FF = 4096
FF_BLOCK = 512
FF_PER_STEP = 2
MLP_BWD_TILE = 512
N_DEV = 8
N_IN = POOL_WIDTH + 3 * 768
PLE_DIM = 256
EPS = 1e-6
NEG_BIG = -1e30

ADAM_LR = 0.001
ADAM_B1 = 0.9
ADAM_B2 = 0.999
ADAM_EPS = 1e-08
ADAM_WD = 0.01
ADAM_STEP = 10

LANES = 128
VMEM_LIMIT = 56 * 1024 * 1024
MESH = pl.DeviceIdType.MESH


def _params(n_grid):
    return pltpu.CompilerParams(dimension_semantics=("arbitrary",) * n_grid, vmem_limit_bytes=VMEM_LIMIT)


def _dot(a, b):
    return jnp.dot(a, b, preferred_element_type=F32)


def _dot_nt(a, b):
    return lax.dot_general(a, b, (((1,), (1,)), ((), ())), preferred_element_type=F32)


def _dot_tn(a, b):
    return lax.dot_general(a, b, (((0,), (0,)), ((), ())), preferred_element_type=F32)


def _rms(x, g):
    rstd = lax.rsqrt(jnp.mean(x * x, axis=-1, keepdims=True) + EPS)
    n = x * rstd
    return n, rstd, n * g


def _rms_bwd(dy, n, rstd, g):
    dyn = dy * g
    dx = rstd * (dyn - n * jnp.mean(dyn * n, axis=-1, keepdims=True))
    return dx, jnp.sum(dy * n, axis=0, keepdims=True)


def _ordered_after(body, n_in, after):
    if not after:
        return body
    return lambda *refs: body(*refs[:n_in], *refs[n_in + len(after):])


def _row_tile(s, t):
    t = min(s, t)
    assert s % t == 0
    return t


def _rot(z, c, sa, sb):
    return z * c + pltpu.roll(z, ROT_SHIFT, 1) * sa + pltpu.roll(z, LANES - ROT_SHIFT, 1) * sb


def _table_specs(t):
    return [pl.BlockSpec((t, LANES), functools.partial(lambda i, k: (i, k), k=k)) for k in range(3)]


def _rot_t(dz, c, sa, sb):
    return dz * c + pltpu.roll(dz * sa, LANES - ROT_SHIFT, 1) + pltpu.roll(dz * sb, ROT_SHIFT, 1)


def _to_residues(value, stage, out_ref, dil):
    if dil == 1:
        out_ref[0] = value.astype(out_ref.dtype)
        return
    rows = value.shape[0] // dil
    for hf in range(GROUP_WIDTH // LANES):
        lanes = slice(hf * LANES, (hf + 1) * LANES)
        stage[hf][...] = value[:, lanes]
        for r in range(dil):
            out_ref[r, :, lanes] = stage[hf][pl.ds(r, rows, stride=dil), :].astype(out_ref.dtype)


def _from_residues(in_ref, stage, dil):
    if dil == 1:
        return in_ref[0].astype(F32)
    rows = in_ref.shape[1]
    for hf in range(GROUP_WIDTH // LANES):
        for r in range(dil):
            stage[hf][pl.ds(r, rows, stride=dil), :] = in_ref[r, :, hf * LANES:(hf + 1) * LANES].astype(F32)
    return jnp.concatenate([stage[0][...], stage[1][...]], axis=1)


def _residue_spec(dil, t):
    return pl.BlockSpec((dil, t // dil, GROUP_WIDTH), lambda i: (0, i, 0))


def _residue_shape(dil, s, dtype):
    return jax.ShapeDtypeStruct((dil, s // dil, GROUP_WIDTH), dtype)


def _stages(t, n):
    return [pltpu.VMEM((t, LANES), F32)] * (n * (GROUP_WIDTH // LANES))


def _pair_stages(refs):
    return [refs[i:i + 2] for i in range(0, len(refs), 2)]


def _normproj_fwd(h, g, w_in, rc, rsa, rsb, name):
    s = h.shape[0]
    t = _row_tile(s, 512)

    def body(h_ref, g_ref, w_ref, c_ref, sa_ref, sb_ref, hn_ref, u_ref, *rest):
        qkv_refs, stages = rest[:9], _pair_stages(rest[9:])
        _, _, hn = _rms(h_ref[...], g_ref[...])
        hb = hn.astype(BF16)
        hn_ref[...] = hb
        c, sa, sb = c_ref[...], sa_ref[...], sb_ref[...]

        def rot(z, scale):
            halves = [_rot(z[:, hf * LANES:(hf + 1) * LANES], c, sa, sb) * scale for hf in range(2)]
            return jnp.concatenate(halves, axis=1)

        proj = lambda lo: _dot_nt(hb, w_ref[lo:lo + GROUP_WIDTH, :])
        u_ref[...] = proj(0)
        for grp, dil in enumerate(DILATIONS):
            lo = POOL_WIDTH + grp * GROUP_WIDTH
            q_ref, k_ref, v_ref = qkv_refs[3 * grp:3 * grp + 3]
            _to_residues(rot(proj(lo), HEAD_DIM ** -0.5), stages[0], q_ref, dil)
            _to_residues(rot(proj(lo + 768), 1.0), stages[1], k_ref, dil)
            _to_residues(proj(lo + 1536), stages[2], v_ref, dil)

    row = lambda w: pl.BlockSpec((t, w), lambda i: (i, 0))
    return pl.pallas_call(
        body, name=name, grid=(s // t,),
        in_specs=[row(D_MODEL), pl.BlockSpec((1, D_MODEL), lambda i: (0, 0)),
                  pl.BlockSpec((N_IN, D_MODEL), lambda i: (0, 0))] + _table_specs(t),
        out_specs=[row(D_MODEL), row(POOL_WIDTH)] + [_residue_spec(dil, t) for dil in DILATIONS for _ in range(3)],
        out_shape=[jax.ShapeDtypeStruct((s, D_MODEL), BF16), jax.ShapeDtypeStruct((s, POOL_WIDTH), F32)]
        + [_residue_shape(dil, s, BF16) for dil in DILATIONS for _ in range(3)],
        scratch_shapes=_stages(t, 3),
        compiler_params=_params(1),
    )(h, g, w_in, rc, rsa, rsb)


def _pool_lane_window():
    lane = lax.broadcasted_iota(jnp.int32, (1, POOL_WIDTH), 1)
    return jnp.left_shift(2, lane // (POOL_WIDTH // len(POOL_WINDOWS)))


def _window_sums(ext, b2, b4, b8, t, lo, tile, direction):
    rows = t + POOL_HALO
    for src, dst, sh in ((ext, b2, 1), (b2, b4, 2), (b4, b8, 4)):
        dst[lo:lo + rows, :] = src[lo:lo + rows, :] + src[lo + direction * sh:lo + direction * sh + rows, :]
    s16 = b8[tile:tile + t, :] + b8[tile + direction * 8:tile + direction * 8 + t, :]
    win = _pool_lane_window()
    return jnp.where(win == 2, b2[tile:tile + t, :],
                     jnp.where(win == 4, b4[tile:tile + t, :], jnp.where(win == 8, b8[tile:tile + t, :], s16)))


def _pool_fwd(u, w_bd, scale, name, after=()):
    s = u.shape[0]
    t = _row_tile(s, 512)
    first = POOL_PAD + POOL_HALO

    def body(u_ref, w_ref, sc_ref, out_ref, y_ref, ext, b2, b4, b8):
        i = pl.program_id(0)

        @pl.when(i == 0)
        def _():
            for buf in (ext, b2, b4):
                buf[0:POOL_PAD, :] = jnp.zeros((POOL_PAD, POOL_WIDTH), F32)
            ext[POOL_PAD:first, :] = jnp.zeros((POOL_HALO, POOL_WIDTH), F32)

        x = u_ref[...]
        ext[first:, :] = x
        wsum = _window_sums(ext, b2, b4, b8, t, POOL_PAD, first, -1)
        pos = i * t + lax.broadcasted_iota(jnp.int32, (t, POOL_WIDTH), 0)
        cnt = jnp.minimum(pos + 1, _pool_lane_window()).astype(F32)
        y = wsum / cnt - x
        yb = y.astype(BF16)
        y_ref[...] = yb
        out_ref[...] = _dot(yb, w_ref[...]) * sc_ref[...]
        ext[POOL_PAD:first, :] = x[t - POOL_HALO:, :]

    row = pl.BlockSpec((t, POOL_WIDTH), lambda i: (i, 0))
    return pl.pallas_call(
        _ordered_after(body, 3, after), name=name, grid=(s // t,),
        in_specs=[row, pl.BlockSpec((POOL_WIDTH, POOL_WIDTH), lambda i: (0, 0)),
                  pl.BlockSpec((1, POOL_WIDTH), lambda i: (0, 0))] + [pl.BlockSpec(memory_space=pl.ANY)] * len(after),
        out_specs=[row, row],
        out_shape=[jax.ShapeDtypeStruct((s, POOL_WIDTH), F32), jax.ShapeDtypeStruct((s, POOL_WIDTH), BF16)],
        scratch_shapes=[pltpu.VMEM((t + POOL_HALO + POOL_PAD, POOL_WIDTH), F32)] * 4,
        compiler_params=_params(1),
    )(u, w_bd, scale, *after)


def _head_masks():
    lane = lax.broadcasted_iota(jnp.int32, (ATTN_BLOCK, GROUP_WIDTH), 1)
    return [lane // HEAD_DIM == hd for hd in range(GROUP_WIDTH // HEAD_DIM)]


def _stack_heads(a, masks):
    zero = jnp.zeros_like(a)
    return jnp.concatenate([jnp.where(m, a, zero) for m in masks], axis=0)


def _band_bias(first_step):
    rows = ATTN_BLOCK * (GROUP_WIDTH // HEAD_DIM)
    i = lax.broadcasted_iota(jnp.int32, (rows, 2 * ATTN_BLOCK), 0) & (ATTN_BLOCK - 1)
    j = lax.broadcasted_iota(jnp.int32, (rows, 2 * ATTN_BLOCK), 1)
    inner = jnp.where((j >= i) & (j <= i + ATTN_BLOCK), 0.0, NEG_BIG)
    return jnp.where((j < ATTN_BLOCK) & first_step, NEG_BIG, inner), inner


def _column_per_head(a):
    return jnp.concatenate([a[:, hd * HEAD_DIM:hd * HEAD_DIM + 1] for hd in range(GROUP_WIDTH // HEAD_DIM)], axis=0)


def _blocks_per_step(nb):
    return 8 if nb % 8 == 0 else 4 if nb % 4 == 0 else 2 if nb % 2 == 0 else 1


def _residues_per_step(dil, nb, qb):
    return 2 if (nb == qb and qb < 8 and dil % 2 == 0) else 1


def _attn_fwd(q, k, v, name, after=()):
    dil, length, _ = q.shape
    nb = length // ATTN_BLOCK
    qb = _blocks_per_step(nb)
    rs = _residues_per_step(dil, nb, qb)

    def body(q_ref, kp_ref, kc_ref, vp_ref, vc_ref, o_ref, lse_ref):
        masks = _head_masks()
        bias = _band_bias(pl.program_id(1) == 0)
        for rr in range(rs):
            for qi in range(qb):
                here = slice(qi * ATTN_BLOCK, (qi + 1) * ATTN_BLOCK)
                before = slice((qi - 1) * ATTN_BLOCK, qi * ATTN_BLOCK)
                kcat = jnp.concatenate([kp_ref[rr] if qi == 0 else kc_ref[rr, before], kc_ref[rr, here]], axis=0)
                vcat = jnp.concatenate([vp_ref[rr] if qi == 0 else vc_ref[rr, before], vc_ref[rr, here]], axis=0)
                qs = _stack_heads(q_ref[rr, here], masks)
                sc = _dot_nt(qs, kcat) + bias[min(qi, 1)]
                m = jnp.max(sc, axis=1, keepdims=True)
                e = jnp.exp(sc - m)
                l = jnp.sum(e, axis=1, keepdims=True)
                p = (e / l).astype(BF16)
                lse = m + jnp.log(l)
                o = jnp.zeros((ATTN_BLOCK, GROUP_WIDTH), F32)
                lse_full = jnp.zeros((ATTN_BLOCK, GROUP_WIDTH), F32)
                for hd, msk in enumerate(masks):
                    rows = slice(hd * ATTN_BLOCK, (hd + 1) * ATTN_BLOCK)
                    o = jnp.where(msk, _dot(p[rows], vcat), o)
                    lse_full = jnp.where(msk, lse[rows], lse_full)
                o_ref[rr, here] = o.astype(o_ref.dtype)
                lse_ref[rr, here] = lse_full

    cur = pl.BlockSpec((rs, qb * ATTN_BLOCK, GROUP_WIDTH), lambda r, j: (r, j, 0))
    prev = pl.BlockSpec((rs, ATTN_BLOCK, GROUP_WIDTH), lambda r, j: (r, jnp.maximum(qb * j - 1, 0), 0))
    return pl.pallas_call(
        _ordered_after(body, 5, after), name=name, grid=(dil // rs, nb // qb),
        in_specs=[cur, prev, cur, prev, cur] + [pl.BlockSpec(memory_space=pl.ANY)] * len(after), out_specs=[cur, cur],
        out_shape=[jax.ShapeDtypeStruct(q.shape, BF16), jax.ShapeDtypeStruct(q.shape, F32)],
        compiler_params=_params(2),
    )(q, k, k, v, v, *after)


def _group_weights(l0, l1, l2):
    m = jnp.maximum(jnp.maximum(l0, l1), l2)
    e0, e1, e2 = jnp.exp(l0 - m), jnp.exp(l1 - m), jnp.exp(l2 - m)
    den = e0 + e1 + e2
    return e0 / den, e1 / den, e2 / den


def _outproj_fwd(h, pool_out, o, lse, w_out, name):
    s = h.shape[0]
    t = _row_tile(s, 512)

    def body(h_ref, po_ref, o0, o1, o2, l0, l1, l2, w_ref, out_ref, a_ref, *stages):
        stages = _pair_stages(stages)
        ov =[_from_residues(r, stages[i], DILATIONS[i]) for i, r in enumerate((o0, o1, o2))]
        lv = [_from_residues(r, stages[3 + i], DILATIONS[i]) for i, r in enumerate((l0, l1, l2))]
        wts = _group_weights(*lv)
        a = jnp.concatenate([po_ref[...]] + [ov[i] * wts[i] for i in range(3)], axis=1).astype(BF16)
        a_ref[...] = a
        out_ref[...] = h_ref[...] + _dot(a, w_ref[...])

    row = lambda w: pl.BlockSpec((t, w), lambda i: (i, 0))
    res = [_residue_spec(dil, t) for dil in DILATIONS]
    return pl.pallas_call(
        body, name=name, grid=(s // t,),
        in_specs=[row(D_MODEL), row(POOL_WIDTH)] + res + res + [pl.BlockSpec((D_MODEL, D_MODEL), lambda i: (0, 0))],
        out_specs=[row(D_MODEL), row(D_MODEL)],
        out_shape=[jax.ShapeDtypeStruct((s, D_MODEL), F32), jax.ShapeDtypeStruct((s, D_MODEL), BF16)],
        scratch_shapes=_stages(t, 6),
        compiler_params=_params(1),
    )(h, pool_out, *o, *lse, w_out)


def _mlp_fwd(h, g, w_up, w_down, name):
    s = h.shape[0]
    t = _row_tile(s, 512)
    nblk = D_FF // FF_BLOCK

    def body(h_ref, g_ref, wu_ref, wd_ref, out_ref, hn_ref, r_ref):
        x = h_ref[...]
        _, _, hn = _rms(x, g_ref[...])
        hb = hn.astype(BF16)
        hn_ref[...] = hb
        acc = None
        for b0 in range(0, nblk, FF_PER_STEP):
            acts = []
            for b in range(b0, b0 + FF_PER_STEP):
                r = jnp.maximum(_dot(hb, wu_ref[b]), 0.0)
                r_ref[:, b * FF_BLOCK:(b + 1) * FF_BLOCK] = r.astype(BF16)
                acts.append((r * r).astype(BF16))
            wd = wd_ref[b0:b0 + FF_PER_STEP].reshape(FF_PER_STEP * FF_BLOCK, D_MODEL)
            part = _dot(jnp.concatenate(acts, axis=1), wd)
            acc = part if acc is None else acc + part
        out_ref[...] = x + acc

    row = lambda w: pl.BlockSpec((t, w), lambda i: (i, 0))
    resident = lambda shape: pl.BlockSpec(shape, lambda i: (0, 0, 0), pipeline_mode=pl.Buffered(1))
    return pl.pallas_call(
        body, name=name, grid=(s // t,),
        in_specs=[row(D_MODEL), pl.BlockSpec((1, D_MODEL), lambda i: (0, 0)),
                  resident((nblk, D_MODEL, FF_BLOCK)), resident((nblk, FF_BLOCK, D_MODEL))],
        out_specs=[row(D_MODEL), row(D_MODEL), row(D_FF)],
        out_shape=[jax.ShapeDtypeStruct((s, D_MODEL), F32), jax.ShapeDtypeStruct((s, D_MODEL), BF16),
                   jax.ShapeDtypeStruct((s, D_FF), BF16)],
        compiler_params=_params(1),
    )(h, g, w_up, w_down)


def _gate_fwd(h, g, w_gate, p, layer, w_ple, name, head=None):
    s = h.shape[0]
    t = _row_tile(s, 512)

    def body(h_ref, g_ref, wg_ref, p_ref, wp_ref, *refs):
        x = h_ref[...]
        _, _, hn = _rms(x, g_ref[...])
        hb = hn.astype(BF16)
        gate = 1.0 / (1.0 + jnp.exp(-_dot(hb, wg_ref[...])))
        pb = p_ref[...].astype(BF16)
        h3 = x + gate * _dot(pb, wp_ref[...])
        if head is None:
            out_ref, hn_ref, gate_ref, pb_ref = refs
            out_ref[...] = h3
        else:
            gf_ref, t_ref, hn_ref, gate_ref, pb_ref, loss_ref, dh_ref, dgf_ref = refs

            @pl.when(pl.program_id(0) == 0)
            def _():
                loss_ref[...] = jnp.zeros_like(loss_ref)
                dgf_ref[...] = jnp.zeros_like(dgf_ref)

            gf = gf_ref[...]
            n, rstd, y = _rms(h3, gf)
            err = y - t_ref[...]
            loss_ref[...] += jnp.sum(err * err) * (0.5 / D_MODEL)
            dh_ref[...], dgf = _rms_bwd(err * (1.0 / D_MODEL), n, rstd, gf)
            dgf_ref[...] += dgf
        hn_ref[...] = hb
        pb_ref[...] = pb
        gate_ref[...] = gate.astype(BF16)

    row = lambda w: pl.BlockSpec((t, w), lambda i: (i, 0))
    full = lambda a, b: pl.BlockSpec((a, b), lambda i: (0, 0))
    in_specs = [row(D_MODEL), full(1, D_MODEL), full(D_MODEL, D_MODEL),
                pl.BlockSpec((None, t, PLE_DIM), lambda i: (layer, i, 0)), full(PLE_DIM, D_MODEL)]
    saved_specs = [row(D_MODEL), row(D_MODEL), row(PLE_DIM)]
    saved_shapes = [jax.ShapeDtypeStruct((s, D_MODEL), BF16), jax.ShapeDtypeStruct((s, D_MODEL), BF16),
                    jax.ShapeDtypeStruct((s, PLE_DIM), BF16)]
    if head is None:
        return pl.pallas_call(
            body, name=name, grid=(s // t,), in_specs=in_specs, out_specs=[row(D_MODEL)] + saved_specs,
            out_shape=[jax.ShapeDtypeStruct((s, D_MODEL), F32)] + saved_shapes, compiler_params=_params(1),
        )(h, g, w_gate, p, w_ple)
    return pl.pallas_call(
        body, name=name, grid=(s // t,), in_specs=in_specs + [full(1, D_MODEL), row(D_MODEL)],
        out_specs=saved_specs + [pl.BlockSpec((1, LANES), lambda i: (0, 0)), row(D_MODEL), full(1, D_MODEL)],
        out_shape=saved_shapes + [jax.ShapeDtypeStruct((1, LANES), F32), jax.ShapeDtypeStruct((s, D_MODEL), F32),
                                  jax.ShapeDtypeStruct((1, D_MODEL), F32)],
        compiler_params=_params(1),
    )(h, g, w_gate, p, w_ple, *head)


def _gate_bwd(dh, gate, pb, w_ple, h, g, w_gate, hn, name, after=()):
    s = h.shape[0]
    t = _row_tile(s, 512)
    last = s // t - 1

    def body(dh_ref, gate_ref, pb_ref, wp_ref, h_ref, g_ref, wg_ref, hn_ref, out_ref, dg_ref, dwg_ref, dwgb_ref,
             dwp_ref, dwpb_ref):
        i = pl.program_id(0)

        @pl.when(i == 0)
        def _():
            dg_ref[...] = jnp.zeros_like(dg_ref)
            dwg_ref[...] = jnp.zeros_like(dwg_ref)
            dwp_ref[...] = jnp.zeros_like(dwp_ref)

        d = dh_ref[...]
        gate = gate_ref[...].astype(F32)
        pb = pb_ref[...]
        e = _dot(pb, wp_ref[...])
        dgl = (d * e * gate * (1.0 - gate)).astype(BF16)
        dwg_ref[...] += _dot_tn(hn_ref[...], dgl)
        dwp_ref[...] += _dot_tn(pb, (d * gate).astype(BF16))
        gv = g_ref[...]
        n, rstd, _ = _rms(h_ref[...], gv)
        dx, dg = _rms_bwd(_dot_nt(dgl, wg_ref[...]), n, rstd, gv)
        out_ref[...] = d + dx
        dg_ref[...] += dg

        @pl.when(i == last)
        def _():
            dwgb_ref[...] = dwg_ref[...].astype(BF16)
            dwpb_ref[...] = dwp_ref[...].astype(BF16)

    row = lambda w: pl.BlockSpec((t, w), lambda i: (i, 0))
    full = lambda a, b: pl.BlockSpec((a, b), lambda i: (0, 0))
    dh2, dg, dwg, dwgb, dwp, dwpb = pl.pallas_call(
        _ordered_after(body, 8, after), name=name, grid=(s // t,),
        in_specs=[row(D_MODEL), row(D_MODEL), row(PLE_DIM), full(PLE_DIM, D_MODEL), row(D_MODEL), full(1, D_MODEL),
                  full(D_MODEL, D_MODEL), row(D_MODEL)] + [pl.BlockSpec(memory_space=pl.ANY)] * len(after),
        out_specs=[row(D_MODEL), full(1, D_MODEL), full(D_MODEL, D_MODEL), full(D_MODEL, D_MODEL),
                   full(PLE_DIM, D_MODEL), full(PLE_DIM, D_MODEL)],
        out_shape=[jax.ShapeDtypeStruct((s, D_MODEL), F32), jax.ShapeDtypeStruct((1, D_MODEL), F32),
                   jax.ShapeDtypeStruct((D_MODEL, D_MODEL), F32), jax.ShapeDtypeStruct((D_MODEL, D_MODEL), BF16),
                   jax.ShapeDtypeStruct((PLE_DIM, D_MODEL), F32), jax.ShapeDtypeStruct((PLE_DIM, D_MODEL), BF16)],
        compiler_params=_params(1),
    )(dh, gate, pb, w_ple, h, g, w_gate, hn, *after)
    return dh2, dg, (dwg, dwgb), (dwp, dwpb)


def _mlp_bwd(dh, r, h, g, w_up, w_down, name):
    s = h.shape[0]
    t = _row_tile(s, MLP_BWD_TILE)
    nblk = D_FF // FF_BLOCK

    def body(dh_ref, r_ref, h_ref, g_ref, wu_ref, wd_ref, out_ref, dup_ref, dg_ref, dhb_ref):
        @pl.when(pl.program_id(0) == 0)
        def _():
            dg_ref[...] = jnp.zeros_like(dg_ref)

        d = dh_ref[...]
        db = d.astype(BF16)
        dhb_ref[...] = db
        back = None
        for b in range(nblk):
            cols = slice(b * FF_BLOCK, (b + 1) * FF_BLOCK)
            dup = (_dot_nt(db, wd_ref[b]) * (2.0 * r_ref[:, cols].astype(F32))).astype(BF16)
            dup_ref[:, cols] = dup
            part = _dot_nt(dup, wu_ref[b])
            back = part if back is None else back + part
        gv = g_ref[...]
        n, rstd, _ = _rms(h_ref[...], gv)
        dx, dg = _rms_bwd(back, n, rstd, gv)
        out_ref[...] = d + dx
        dg_ref[...] += dg

    row = lambda w: pl.BlockSpec((t, w), lambda i: (i, 0))
    vec = pl.BlockSpec((1, D_MODEL), lambda i: (0, 0))
    resident = lambda shape: pl.BlockSpec(shape, lambda i: (0, 0, 0), pipeline_mode=pl.Buffered(1))
    return pl.pallas_call(
        body, name=name, grid=(s // t,),
        in_specs=[row(D_MODEL), row(D_FF), row(D_MODEL), vec,
                  resident((nblk, D_MODEL, FF_BLOCK)), resident((nblk, FF_BLOCK, D_MODEL))],
        out_specs=[row(D_MODEL), row(D_FF), vec, row(D_MODEL)],
        out_shape=[jax.ShapeDtypeStruct((s, D_MODEL), F32), jax.ShapeDtypeStruct((s, D_FF), BF16),
                   jax.ShapeDtypeStruct((1, D_MODEL), F32), jax.ShapeDtypeStruct((s, D_MODEL), BF16)],
        compiler_params=_params(1),
    )(dh, r, h, g, w_up, w_down)


def _outproj_bwd(dh, w_out, o, lse, ones_bd, a, name):
    s = dh.shape[0]
    t = _row_tile(s, 512)
    last = s // t - 1

    def body(dh_ref, w_ref, o0, o1, o2, l0, l1, l2, bd_ref, a_ref, dp_ref, do0, do1, do2, de0, de1, de2, dw_ref,
             dwb_ref, *stages):
        i = pl.program_id(0)

        @pl.when(i == 0)
        def _():
            dw_ref[...] = jnp.zeros_like(dw_ref)

        stages = _pair_stages(stages)
        dhb = dh_ref[...].astype(BF16)
        dw_ref[...] += _dot_tn(a_ref[...], dhb)

        @pl.when(i == last)
        def _():
            dwb_ref[...] = dw_ref[...].astype(BF16)

        da = _dot_nt(dhb, w_ref[...])
        dp_ref[...] = da[:, 0:POOL_WIDTH]
        ov =[_from_residues(r, stages[i], DILATIONS[i]) for i, r in enumerate((o0, o1, o2))]
        lv = [_from_residues(r, stages[3 + i], DILATIONS[i]) for i, r in enumerate((l0, l1, l2))]
        wts = _group_weights(*lv)
        bd = bd_ref[...]
        cbar = jnp.zeros((t, GROUP_WIDTH), F32)
        for grp, do_ref in enumerate((do0, do1, do2)):
            lo = POOL_WIDTH + grp * GROUP_WIDTH
            dag = da[:, lo:lo + GROUP_WIDTH]
            _to_residues(dag * wts[grp], stages[6 + grp], do_ref, DILATIONS[grp])
            prod = dag * ov[grp]
            hi = prod.astype(BF16)
            low = (prod - hi.astype(F32)).astype(BF16)
            cbar = cbar + wts[grp] * (_dot(hi, bd) + _dot(low, bd))
        for grp, de_ref in enumerate((de0, de1, de2)):
            _to_residues(wts[grp] * cbar, stages[9 + grp], de_ref, DILATIONS[grp])

    row = lambda w: pl.BlockSpec((t, w), lambda i: (i, 0))
    full = lambda a, b: pl.BlockSpec((a, b), lambda i: (0, 0))
    res = [_residue_spec(dil, t) for dil in DILATIONS]
    *outs, dw, dwb = pl.pallas_call(
        body, name=name, grid=(s // t,),
        in_specs=[row(D_MODEL), full(D_MODEL, D_MODEL)] + res + res + [full(GROUP_WIDTH, GROUP_WIDTH), row(D_MODEL)],
        out_specs=[row(POOL_WIDTH)] + res + res + [full(D_MODEL, D_MODEL)] * 2,
        out_shape=[jax.ShapeDtypeStruct((s, POOL_WIDTH), F32)] + [_residue_shape(dil, s, BF16) for dil in DILATIONS]
        + [_residue_shape(dil, s, F32) for dil in DILATIONS]
        + [jax.ShapeDtypeStruct((D_MODEL, D_MODEL), F32), jax.ShapeDtypeStruct((D_MODEL, D_MODEL), BF16)],
        scratch_shapes=_stages(t, 12),
        compiler_params=_params(1),
    )(dh, w_out, *o, *lse, ones_bd, a)
    return (*outs, (dw, dwb))


def _attn_bwd(q, k, v, do, lse, deff, name, after=()):
    dil, length, _ = q.shape
    nb = length // ATTN_BLOCK
    qb = _blocks_per_step(nb)
    nj = nb // qb
    rs = _residues_per_step(dil, nb, qb)
    whole = nj == 1
    tail = slice((qb - 1) * ATTN_BLOCK, qb * ATTN_BLOCK)
    block = lambda qi: slice(qi * ATTN_BLOCK, (qi + 1) * ATTN_BLOCK)

    def body(q_ref, kp_ref, kc_ref, vp_ref, vc_ref, do_ref, lse_ref, de_ref, dq_ref, dk_ref, dv_ref, ck, cv):
        j = pl.program_id(1)

        def compute():
            masks = _head_masks()
            bias = _band_bias(j == 0)
            for rr in range(rs):
                dkc, dvc = [], []
                for qi in range(qb):
                    here, before = block(qi), block(qi - 1)
                    kcat = jnp.concatenate([kp_ref[rr] if qi == 0 else kc_ref[rr, before], kc_ref[rr, here]], axis=0)
                    vcat = jnp.concatenate([vp_ref[rr] if qi == 0 else vc_ref[rr, before], vc_ref[rr, here]], axis=0)
                    qs = _stack_heads(q_ref[rr, here], masks)
                    dos = _stack_heads(do_ref[rr, here], masks)
                    sc = _dot_nt(qs, kcat) + bias[min(qi, 1)]
                    p = jnp.exp(sc - _column_per_head(lse_ref[rr, here]))
                    ds = (p * (_dot_nt(dos, vcat) - _column_per_head(de_ref[rr, here]))).astype(BF16)
                    dq = jnp.zeros((ATTN_BLOCK, GROUP_WIDTH), F32)
                    for hd, msk in enumerate(masks):
                        dq = jnp.where(msk, _dot(ds[block(hd)], kcat), dq)
                    dq_ref[rr, here] = dq.astype(dq_ref.dtype)
                    dkc.append(_dot_tn(ds, qs))
                    dvc.append(_dot_tn(p.astype(BF16), dos))

                for out_ref, carry, parts in ((dk_ref, ck, dkc), (dv_ref, cv, dvc)):
                    full = [parts[qi][ATTN_BLOCK:] + parts[qi + 1][0:ATTN_BLOCK] for qi in range(qb - 1)]
                    if whole:
                        for qi, val in enumerate(full + [parts[qb - 1][ATTN_BLOCK:]]):
                            out_ref[rr, block(qi)] = val.astype(out_ref.dtype)
                        continue

                    @pl.when(j > 0)
                    def _():
                        if qb > 1:
                            out_ref[0, 0:(qb - 1) * ATTN_BLOCK] = carry[0:(qb - 1) * ATTN_BLOCK].astype(out_ref.dtype)
                        out_ref[0, tail] = (carry[tail] + parts[0][0:ATTN_BLOCK]).astype(out_ref.dtype)

                    for qi, val in enumerate(full):
                        carry[block(qi)] = val
                    carry[tail] = parts[qb - 1][ATTN_BLOCK:]

        if whole:
            compute()
        else:
            pl.when(j < nj)(compute)

            @pl.when(j == nj)
            def _():
                dk_ref[0] = ck[...].astype(dk_ref.dtype)
                dv_ref[0] = cv[...].astype(dv_ref.dtype)

    step = lambda j: jnp.minimum(j, nj - 1)
    cur = pl.BlockSpec((rs, qb * ATTN_BLOCK, GROUP_WIDTH), lambda r, j: (r, step(j), 0))
    prev = pl.BlockSpec((rs, ATTN_BLOCK, GROUP_WIDTH), lambda r, j: (r, jnp.maximum(qb * step(j) - 1, 0), 0))
    late = pl.BlockSpec((rs, qb * ATTN_BLOCK, GROUP_WIDTH), lambda r, j: (r, jnp.maximum(j - 1, 0), 0))
    return pl.pallas_call(
        _ordered_after(body, 8, after), name=name, grid=(dil // rs, 1 if whole else nj + 1),
        in_specs=[cur, prev, cur, prev, cur, cur, cur, cur] + [pl.BlockSpec(memory_space=pl.ANY)] * len(after),
        out_specs=[cur, cur if whole else late, cur if whole else late],
        out_shape=[jax.ShapeDtypeStruct(q.shape, BF16)] * 3,
        scratch_shapes=[pltpu.VMEM((qb * ATTN_BLOCK, GROUP_WIDTH), F32)] * 2,
        compiler_params=_params(2),
    )(q, k, k, v, v, do, lse, deff, *after)


def _pool_bwd(dpool, y, w_bd, scale, name, after=()):
    s = dpool.shape[0]
    t = _row_tile(s, 512)
    nt = s // t

    def body(dp_ref, y_ref, w_ref, sc_ref, du_ref, dw_ref, dsc_ref, ext, b2, b4, b8):
        i = pl.program_id(0)

        @pl.when(i == 0)
        def _():
            ext[t:, :] = jnp.zeros((POOL_HALO + POOL_PAD, POOL_WIDTH), F32)
            for buf in (b2, b4):
                buf[t + POOL_HALO:, :] = jnp.zeros((POOL_PAD, POOL_WIDTH), F32)
            dw_ref[...] = jnp.zeros_like(dw_ref)
            dsc_ref[...] = jnp.zeros_like(dsc_ref)

        dp = dp_ref[...]
        yb = y_ref[...]
        w = w_ref[...]
        dsc_ref[...] += jnp.sum(dp * _dot(yb, w), axis=0, keepdims=True)
        dyo = (dp * sc_ref[...]).astype(BF16)
        dw_ref[...] += _dot_tn(yb, dyo)
        dy = _dot_nt(dyo, w)
        win = _pool_lane_window()
        pos = (nt - 1 - i) * t + lax.broadcasted_iota(jnp.int32, (t, POOL_WIDTH), 0)
        gq = dy / jnp.minimum(pos + 1, win).astype(F32)
        ext[0:t, :] = gq
        du_ref[...] = _window_sums(ext, b2, b4, b8, t, 0, 0, 1) - dy
        ext[t:t + POOL_HALO, :] = gq[0:POOL_HALO, :]

    rev = pl.BlockSpec((t, POOL_WIDTH), lambda i: (nt - 1 - i, 0))
    full = lambda a, b: pl.BlockSpec((a, b), lambda i: (0, 0))
    return pl.pallas_call(
        _ordered_after(body, 4, after), name=name, grid=(nt,),
        in_specs=[rev, rev, full(POOL_WIDTH, POOL_WIDTH), full(1, POOL_WIDTH)]
        + [pl.BlockSpec(memory_space=pl.ANY)] * len(after),
        out_specs=[rev, full(POOL_WIDTH, POOL_WIDTH), full(1, POOL_WIDTH)],
        out_shape=[jax.ShapeDtypeStruct((s, POOL_WIDTH), F32), jax.ShapeDtypeStruct((POOL_WIDTH, POOL_WIDTH), F32),
                   jax.ShapeDtypeStruct((1, POOL_WIDTH), F32)],
        scratch_shapes=[pltpu.VMEM((t + POOL_HALO + POOL_PAD, POOL_WIDTH), F32)] * 4,
        compiler_params=_params(1),
    )(dpool, y, w_bd, scale, *after)


def _normproj_bwd(dh, du, dq, dk, dv, rc, rsa, rsb, w_in, h, g, name):
    s = h.shape[0]
    t = _row_tile(s, 512)

    def body(dh_ref, du_ref, q0, q1, q2, k0, k1, k2, v0, v1, v2, c_ref, sa_ref, sb_ref, w_ref, h_ref, g_ref,
             out_ref, dz_ref, dg_ref, *stages):
        @pl.when(pl.program_id(0) == 0)
        def _():
            dg_ref[...] = jnp.zeros_like(dg_ref)

        c, sa, sb = c_ref[...], sa_ref[...], sb_ref[...]

        def unrot(a, scale):
            halves = [_rot_t(a[:, hf * LANES:(hf + 1) * LANES] * scale, c, sa, sb) for hf in range(2)]
            return jnp.concatenate(halves, axis=1)

        staged = _pair_stages(stages)
        tok = lambda refs, base: [_from_residues(r, staged[base + i], DILATIONS[i]) for i, r in enumerate(refs)]
        chunks = [du_ref[...]]
        chunks += [unrot(a, HEAD_DIM ** -0.5) for a in tok((q0, q1, q2), 0)]
        chunks += [unrot(a, 1.0) for a in tok((k0, k1, k2), 3)]
        chunks += tok((v0, v1, v2), 6)
        acc = jnp.zeros((t, D_MODEL), F32)
        for ci, ch in enumerate(chunks):
            cols = slice(ci * GROUP_WIDTH, (ci + 1) * GROUP_WIDTH)
            cb = ch.astype(BF16)
            dz_ref[:, cols] = cb
            acc = acc + _dot(cb, w_ref[cols, :])
        gv = g_ref[...]
        n, rstd, _ = _rms(h_ref[...], gv)
        dx, dg = _rms_bwd(acc, n, rstd, gv)
        out_ref[...] = dh_ref[...] + dx
        dg_ref[...] += dg

    row = lambda w: pl.BlockSpec((t, w), lambda i: (i, 0))
    vec = pl.BlockSpec((1, D_MODEL), lambda i: (0, 0))
    res = [_residue_spec(dil, t) for dil in DILATIONS]
    return pl.pallas_call(
        body, name=name, grid=(s // t,),
        in_specs=[row(D_MODEL), row(POOL_WIDTH)] + res * 3 + _table_specs(t)
        + [pl.BlockSpec((N_IN, D_MODEL), lambda i: (0, 0)), row(D_MODEL), vec],
        out_specs=[row(D_MODEL), row(N_IN), vec],
        out_shape=[jax.ShapeDtypeStruct((s, D_MODEL), F32), jax.ShapeDtypeStruct((s, N_IN), BF16),
                   jax.ShapeDtypeStruct((1, D_MODEL), F32)],
        scratch_shapes=_stages(t, 9),
        compiler_params=_params(1),
    )(dh, du, *dq, *dk, *dv, rc, rsa, rsb, w_in, h, g)


def _matmul_tn(a, b, name, *, square_a=False, tm=None, tn=None, blocked_out=False, after=()):
    s, m = a.shape
    n = b.shape[1]
    tk = _row_tile(s, 2048)
    tm = tm or min(m, 1024)
    tn = tn or min(n, 1024)
    assert m % tm == 0 and n % tn == 0
    nk = s // tk
    nsub = tn // FF_BLOCK if blocked_out else 1

    def body(a_ref, b_ref, o_ref, ob_ref, acc):
        k = pl.program_id(2)

        def product():
            av = a_ref[...]
            if square_a:
                av = av.astype(F32)
                av = av * av
            return _dot_tn(av.astype(BF16), b_ref[...].astype(BF16))

        def emit(total):
            if blocked_out:
                for sub in range(nsub):
                    cols = slice(sub * FF_BLOCK, (sub + 1) * FF_BLOCK)
                    o_ref[sub] = total[:, cols]
                    ob_ref[sub] = total[:, cols].astype(BF16)
            else:
                o_ref[...] = total
                ob_ref[...] = total.astype(BF16)

        if nk == 1:
            emit(product())
            return

        @pl.when(k == 0)
        def _():
            acc[...] = product()

        @pl.when((k > 0) & (k < nk - 1))
        def _():
            acc[...] += product()

        @pl.when(k == nk - 1)
        def _():
            emit(acc[...] + product())

    if blocked_out:
        shape = (n // FF_BLOCK, m, FF_BLOCK)
        out_spec = pl.BlockSpec((nsub, tm, FF_BLOCK), lambda i, j, k: (j, i, 0))
    else:
        shape = (m, n)
        out_spec = pl.BlockSpec((tm, tn), lambda i, j, k: (i, j))
    return pl.pallas_call(
        _ordered_after(body, 2, after), name=name, grid=(m // tm, n // tn, nk),
        in_specs=[pl.BlockSpec((tk, tm), lambda i, j, k: (k, i)), pl.BlockSpec((tk, tn), lambda i, j, k: (k, j))]
        + [pl.BlockSpec(memory_space=pl.ANY)] * len(after),
        out_specs=[out_spec, out_spec],
        out_shape=[jax.ShapeDtypeStruct(shape, F32), jax.ShapeDtypeStruct(shape, BF16)],
        scratch_shapes=[pltpu.VMEM((tm, tn), F32)],
        compiler_params=_params(3),
    )(a, b, *after)


def _adamw_math(w, g, m, v):
    m = ADAM_B1 * m + (1.0 - ADAM_B1) * g
    v = ADAM_B2 * v + (1.0 - ADAM_B2) * (g * g)
    m_hat = m / (1.0 - ADAM_B1 ** ADAM_STEP)
    v_hat = v / (1.0 - ADAM_B2 ** ADAM_STEP)
    delta = -ADAM_LR * (m_hat / (jnp.sqrt(v_hat) + ADAM_EPS) + ADAM_WD * w)
    return delta, m, v


def _sum_chunks_body(own_ref, r0_ref, r1_ref):
    layer0 = pl.program_id(0) == 0
    g = own_ref[...]
    for k in range(N_DEV - 1):
        g = g + jnp.where(layer0, r0_ref[k], r1_ref[k]).astype(F32)
    return g


def _chunk_specs(t, cols):
    blk = pl.BlockSpec((None, t, cols), lambda l, i: (l, i, 0))
    recv = lambda layer: pl.BlockSpec((N_DEV - 1, t, cols), lambda l, i: (0, jnp.where(l == layer, i, 0), 0))
    return blk, recv(0), recv(1)


def _sum_chunks(own, recv0, recv1, name):
    _, rows, cols = own.shape
    t = _row_tile(rows, 320)

    def body(own_ref, r0_ref, r1_ref, g_ref):
        g_ref[...] = _sum_chunks_body(own_ref, r0_ref, r1_ref)

    blk, recv_l0, recv_l1 = _chunk_specs(t, cols)
    return pl.pallas_call(
        body, name=name, grid=(2, rows // t), in_specs=[blk, recv_l0, recv_l1], out_specs=blk,
        out_shape=jax.ShapeDtypeStruct(own.shape, F32), compiler_params=_params(2),
    )(own, recv0, recv1)


def _adamw_sharded(w, m, v, own, recv0, recv1, name):
    _, rows, cols = w.shape
    t = _row_tile(rows, 256)
    summed = recv0 is None

    def body(w_ref, m_ref, v_ref, own_ref, *refs):
        g_ref, d_ref, nm_ref, nv_ref = refs[-4:]
        g = own_ref[...] if summed else _sum_chunks_body(own_ref, *refs[:2])
        g_ref[...] = g
        d_ref[...], nm_ref[...], nv_ref[...] = _adamw_math(w_ref[...], g, m_ref[...], v_ref[...])

    blk, recv_l0, recv_l1 = _chunk_specs(t, cols)
    return pl.pallas_call(
        body, name=name, grid=(2, rows // t),
        in_specs=[blk, blk, blk, blk] + ([] if summed else [recv_l0, recv_l1]), out_specs=[blk] * 4,
        out_shape=[jax.ShapeDtypeStruct(w.shape, F32)] * 4,
        compiler_params=_params(2),
    )(w, m, v, own, *(() if summed else (recv0, recv1)))


def _adamw_packed(w, g8, m, v, name):
    def body(w_ref, g_ref, m_ref, v_ref, go_ref, d_ref, nm_ref, nv_ref):
        g = g_ref[0]
        for dev in range(1, N_DEV):
            g = g + g_ref[dev]
        go_ref[...] = g
        d_ref[...], nm_ref[...], nv_ref[...] = _adamw_math(w_ref[...], g, m_ref[...], v_ref[...])

    return pl.pallas_call(
        body, name=name, out_shape=[jax.ShapeDtypeStruct(w.shape, F32)] * 4,
        compiler_params=pltpu.CompilerParams(vmem_limit_bytes=VMEM_LIMIT),
    )(w, g8, m, v)


def _peer(k):
    x, y, c = lax.axis_index("x"), lax.axis_index("y"), lax.axis_index("c")
    return (1 - x if k & 4 else x, 1 - y if k & 2 else y, 1 - c if k & 1 else c)


def _linear(dev):
    return 4 * dev[0] + 2 * dev[1] + dev[2]


HBM_SPEC = pl.BlockSpec(memory_space=pltpu.HBM)
SEM_SPEC = pl.BlockSpec(memory_space=pltpu.SEMAPHORE)
ANY_SPEC = pl.BlockSpec(memory_space=pl.ANY)
EFFECT = pltpu.SideEffectType.DATAFLOW_SIDE_EFFECTING


def _in_hbm(a):
    return pltpu.with_memory_space_constraint(a, pltpu.HBM)


class _Exchange:
    def __init__(self, name, groups, scatter, after=()):
        self.name, self.scatter = name, scatter
        self.sizes = sizes = [len(g) for g in groups]
        srcs = [a for g in groups for a in g]
        n, ng = len(srcs), len(groups)
        lead = (N_DEV - 1,) if scatter else (N_DEV,)
        shapes = [lead + (a.shape[1:] if scatter else a.shape) for a in srcs]
        lands = [lax.empty(sh, a.dtype) for sh, a in zip(shapes, srcs)]
        offsets = [sum(sizes[:gi]) for gi in range(ng)]
        copy = self._copy

        def body(*refs):
            src, land = refs[:n], refs[n:2 * n]
            sems = refs[2 * n + len(after):2 * n + len(after) + 2 * ng]
            token = refs[-1]
            for gi in range(ng):
                for wi in range(sizes[gi]):
                    w = offsets[gi] + wi
                    for k in range(1, N_DEV):
                        copy(src[w], land[w], sems[2 * gi], sems[2 * gi + 1], wi, k).start()
            token[...] = jnp.zeros_like(token)

        sem_shapes = [pltpu.SemaphoreType.DMA((7 * sz,)) for sz in sizes for _ in range(2)]
        outs = pl.pallas_call(
            body, name=name + "_start",
            in_specs=[HBM_SPEC] * (2 * n) + [ANY_SPEC] * len(after),
            out_specs=[SEM_SPEC] * (2 * ng) + [HBM_SPEC] * (2 * n) + [pl.BlockSpec(memory_space=pltpu.VMEM)],
            out_shape=sem_shapes + [pltpu.HBM(a.shape, a.dtype) for a in srcs + lands]
            + [jax.ShapeDtypeStruct((8, LANES), F32)],
            input_output_aliases={i: 2 * ng + i for i in range(2 * n)},
            compiler_params=pltpu.CompilerParams(has_side_effects=EFFECT),
        )(*[_in_hbm(a) for a in srcs + lands], *after)
        self.sems = [outs[2 * gi:2 * gi + 2] for gi in range(ng)]
        thru = outs[2 * ng:2 * ng + 2 * n]
        self.srcs = [thru[offsets[gi]:offsets[gi] + sizes[gi]] for gi in range(ng)]
        self.lands = [thru[n + offsets[gi]:n + offsets[gi] + sizes[gi]] for gi in range(ng)]
        self.token = outs[-1]

    def _copy(self, src, land, send_sems, recv_sems, wi, k):
        to = _peer(k)
        if self.scatter:
            src_ref, dst_ref = src.at[_linear(to)], land.at[k - 1]
        else:
            src_ref, dst_ref = src, land.at[_linear(_peer(0))]
        return pltpu.make_async_remote_copy(
            src_ref=src_ref, dst_ref=dst_ref, send_sem=send_sems.at[7 * wi + k - 1],
            recv_sem=recv_sems.at[7 * wi + k - 1], device_id=to, device_id_type=MESH)

    def wait(self, gi, after):
        n = self.sizes[gi]
        copy = self._copy

        def body(*refs):
            src, land = refs[:n], refs[n:2 * n]
            send_sems, recv_sems = refs[2 * n], refs[2 * n + 1]
            for wi in range(n):
                for k in range(1, N_DEV):
                    cp = copy(src[wi], land[wi], send_sems, recv_sems, wi, k)
                    cp.wait_send()
                    cp.wait_recv()

        arrays = list(self.srcs[gi]) + list(self.lands[gi])
        outs = pl.pallas_call(
            body, name=f"{self.name}_wait{gi}",
            in_specs=[HBM_SPEC] * (2 * n) + [SEM_SPEC, SEM_SPEC] + [ANY_SPEC] * len(after),
            out_specs=[HBM_SPEC] * (2 * n),
            out_shape=[pltpu.HBM(a.shape, a.dtype) for a in arrays],
            input_output_aliases={i: i for i in range(2 * n)},
            compiler_params=pltpu.CompilerParams(has_side_effects=EFFECT),
        )(*arrays, *self.sems[gi], *after)
        return outs[:n], outs[n:]


def _rotary_tables(positions):
    rot_dim = HEAD_DIM // 4
    inv_freq = ROPE_THETA ** (-jnp.arange(0, rot_dim, 2, dtype=F32) / rot_dim)
    ang = positions.astype(F32)[:, None] * inv_freq
    cs = jnp.concatenate([jnp.cos(ang), jnp.sin(ang)], axis=1)
    dim = jnp.arange(LANES) % HEAD_DIM
    first, second = dim < ROT_SHIFT, (dim >= ROT_SHIFT) & (dim < rot_dim)
    src = jnp.arange(2 * ROT_SHIFT)[:, None]
    angle = (dim % ROT_SHIFT)[None, :]
    c = jnp.where((first | second)[None, :] & (src == angle), 1.0, 0.0)
    sa = jnp.where(second[None, :] & (src == angle + ROT_SHIFT), 1.0, 0.0)
    sb = jnp.where(first[None, :] & (src == angle + ROT_SHIFT), -1.0, 0.0)
    spread = jnp.concatenate([c, sa, sb], axis=1).astype(F32)
    base = jnp.concatenate([jnp.where(first | second, 0.0, 1.0), jnp.zeros((2 * LANES,))]).astype(F32)[None, :]
    return jnp.dot(cs, spread, precision=lax.Precision.HIGHEST, preferred_element_type=F32) + base


def _block_diag(pool_w):
    gc = pool_w.shape[-1]
    out = jnp.zeros((POOL_WIDTH, POOL_WIDTH), pool_w.dtype)
    for grp in range(pool_w.shape[0]):
        out = lax.dynamic_update_slice(out, pool_w[grp], (grp * gc, grp * gc))
    return out


def _diag_blocks(a):
    gc = POOL_WIDTH // len(POOL_WINDOWS)
    return jnp.stack([a[grp * gc:(grp + 1) * gc, grp * gc:(grp + 1) * gc] for grp in range(len(POOL_WINDOWS))])


def _local_step(x, p, positions, loss_target, norm1, pool_w, pool_scale, norm2, norm3, final_norm, weights, send):
    rc = rsa = rsb = _rotary_tables(positions)
    ones_bd = _block_diag(jnp.ones((4, HEAD_DIM, HEAD_DIM), BF16))
    saved = []
    h = x
    for i in range(2):
        tag = f"_l{i}"
        g1, g2, g3 = norm1[i:i + 1], norm2[i:i + 1], norm3[i:i + 1]
        w_bd = _block_diag(pool_w[i]).astype(BF16)
        scale = pool_scale[i:i + 1]
        w_in = weights(i, "in", (h, rc, w_bd))
        hn1, u, *qkv = _normproj_fwd(h, g1, w_in, rc, rsa, rsb, "normproj_fwd" + tag)
        qkv = [qkv[3 * grp:3 * grp + 3] for grp in range(3)]
        started = weights(i, "prefetch", (hn1,))
        pool_out, y = _pool_fwd(u, w_bd, scale, "pool_fwd" + tag, after=started)
        o, lse = zip(*[_attn_fwd(*qkv[grp], f"attn_fwd{tag}_g{grp}", after=started) for grp in range(3)])
        w_out = weights(i, "out", (pool_out, *o))
        h1, a = _outproj_fwd(h, pool_out, o, lse, w_out, "outproj_fwd" + tag)
        w_up, w_down, w_gate, w_ple = weights(i, "rest", (h1,))
        h2, hn2, r = _mlp_fwd(h1, g2, w_up, w_down, "mlp_fwd" + tag)
        h0 = h
        if i == 0:
            h, hn3, gate, pb = _gate_fwd(h2, g3, w_gate, p, i, w_ple, "gate_fwd" + tag)
        else:
            hn3, gate, pb, loss, dh, d_final = _gate_fwd(h2, g3, w_gate, p, i, w_ple, "gate_fwd" + tag,
                                                         head=(final_norm.reshape(1, D_MODEL), loss_target))
        saved.append(dict(h0=h0, hn1=hn1, qkv=qkv, y=y, o=o, lse=lse, a=a, h1=h1, hn2=hn2, r=r, h2=h2,
                          hn3=hn3, gate=gate, pb=pb, w_bd=w_bd, scale=scale, g1=g1, g2=g2, g3=g3,
                          w_in=w_in, w_out=w_out, w_up=w_up, w_down=w_down, w_gate=w_gate, w_ple=w_ple))

    grads = [None, None]
    sent = ()
    for i in (1, 0):
        tag = f"_l{i}"
        sv = saved[i]
        dh2, dg3, dw_gate, dw_ple = _gate_bwd(dh, sv["gate"], sv["pb"], sv["w_ple"], sv["h2"], sv["g3"], sv["w_gate"],
                                              sv["hn3"], "gate_bwd" + tag, after=sent)
        dh1, dup, dg2, dh2b = _mlp_bwd(dh2, sv["r"], sv["h1"], sv["g2"], sv["w_up"], sv["w_down"], "mlp_bwd" + tag)
        dw_down = _matmul_tn(sv["r"], dh2b, "dw_down" + tag, square_a=True)
        dw_up = _matmul_tn(sv["hn2"], dup, "dw_up" + tag, blocked_out=True)
        dpool, do0, do1, do2, de0, de1, de2, dw_out = _outproj_bwd(dh1, sv["w_out"], sv["o"], sv["lse"], ones_bd,
                                                                   sv["a"], "outproj_bwd" + tag)
        sent = send(i, "main", dict(w_gate=dw_gate, w_ple=dw_ple, w_down=dw_down, w_up=dw_up, w_out=dw_out))
        dqkv = [_attn_bwd(*sv["qkv"][grp], do_g, sv["lse"][grp], de_g, f"attn_bwd{tag}_g{grp}", after=sent)
                for grp, (do_g, de_g) in enumerate(((do0, de0), (do1, de1), (do2, de2)))]
        dq, dk, dv = zip(*dqkv)
        du, dw_bd, dscale = _pool_bwd(dpool, sv["y"], sv["w_bd"], sv["scale"], "pool_bwd" + tag, after=sent)
        dh, dz, dg1 = _normproj_bwd(dh1, du, dq, dk, dv, rc, rsa, rsb, sv["w_in"], sv["h0"], sv["g1"],
                                    "normproj_bwd" + tag)
        grads[i] = dict(norm1=dg1, norm2=dg2, norm3=dg3, pool_w=_diag_blocks(dw_bd), pool_scale=dscale)
        small_sent = send(0, "small", (grads, d_final, loss)) if i == 0 else ()
        dw_in = _matmul_tn(dz, sv["hn1"], "dw_in" + tag, tm=N_IN // 2, after=small_sent)
        sent = send(i, "in", dict(w_in=dw_in))
    return dh, sent


def _pack_small(norm1, norm2, norm3, final_norm, pool_scale, pool_w, spare=None):
    spare = jnp.zeros((1, LANES), F32) if spare is None else spare
    scale_row = jnp.concatenate([pool_scale.reshape(1, 2 * POOL_WIDTH), spare,
                                 jnp.zeros((1, D_MODEL - 2 * POOL_WIDTH - LANES), F32)], axis=1)
    return jnp.concatenate([norm1, norm2, norm3, final_norm.reshape(1, D_MODEL), scale_row,
                            pool_w.reshape(32, D_MODEL)], axis=0)


def _unpack_small(a):
    return dict(norm1=a[0:2], norm2=a[2:4], norm3=a[4:6], final_norm=a[6], pool_scale=a[7, 0:2 * POOL_WIDTH].reshape(2, POOL_WIDTH),
                pool_w=a[8:40].reshape(2, 4, HEAD_DIM, HEAD_DIM))


def _chunks_cols(a, cols):
    return a.reshape(a.shape[0], N_DEV, cols).transpose(1, 0, 2)


def _chunks_rows(a, rows):
    return a.reshape(N_DEV, rows, a.shape[1])


BIG = ("w_in", "w_out", "w_up", "w_down", "w_gate", "w_ple")
SMALL = ("norm1", "norm2", "norm3", "final_norm", "pool_scale", "pool_w")
ORDER = ("norm1", "w_in", "pool_w", "pool_scale", "w_out", "norm2", "w_up", "w_down", "norm3", "w_gate", "w_ple",
         "final_norm")


def kernel(x, p, positions, norm1, w_in, pool_w, pool_scale, w_out, norm2, w_up, w_down, norm3, w_gate, w_ple, final_norm, loss_target, m_norm1, m_w_in, m_pool_w, m_pool_scale, m_w_out, m_norm2, m_w_up, m_w_down, m_norm3, m_w_gate, m_w_ple, m_final_norm, v_norm1, v_w_in, v_pool_w, v_pool_scale, v_w_out, v_norm2, v_w_up, v_w_down, v_norm3, v_w_gate, v_w_ple, v_final_norm):
    w = dict(norm1=norm1, w_in=w_in, pool_w=pool_w, pool_scale=pool_scale, w_out=w_out, norm2=norm2, w_up=w_up,
             w_down=w_down, norm3=norm3, w_gate=w_gate, w_ple=w_ple, final_norm=final_norm)
    m = dict(norm1=m_norm1, w_in=m_w_in, pool_w=m_pool_w, pool_scale=m_pool_scale, w_out=m_w_out, norm2=m_norm2,
             w_up=m_w_up, w_down=m_w_down, norm3=m_norm3, w_gate=m_w_gate, w_ple=m_w_ple, final_norm=m_final_norm)
    v = dict(norm1=v_norm1, w_in=v_w_in, pool_w=v_pool_w, pool_scale=v_pool_scale, w_out=v_w_out, norm2=v_norm2,
             w_up=v_w_up, w_down=v_w_down, norm3=v_norm3, w_gate=v_w_gate, w_ple=v_w_ple, final_norm=v_final_norm)
    seq = x.shape[1]

    bf = {n: [w[n][layer].astype(BF16) for layer in range(2)] for n in BIG}
    bf["w_in"] = [a.T for a in bf["w_in"]]
    rest = ("w_up", "w_down", "w_gate", "w_ple")
    me = 4 * lax.axis_index("x") + 2 * lax.axis_index("y") + lax.axis_index("c")
    gathers = [_Exchange("gather_l0", [[bf["w_in"][0]], [bf["w_out"][0]], [bf[n][0] for n in rest]], scatter=False)]
    unpack = dict(w_in=lambda a: a.reshape(N_IN, D_MODEL),
                  w_out=lambda a: a.reshape(D_MODEL, D_MODEL), w_gate=lambda a: a.reshape(D_MODEL, D_MODEL),
                  w_ple=lambda a: a.transpose(1, 0, 2).reshape(PLE_DIM, D_MODEL), w_up=lambda a: a, w_down=lambda a: a)
    parts = dict(zip(("in", "out", "rest"), (("w_in",), ("w_out",), rest)))

    def weights(layer, part, after):
        if part == "prefetch":
            if layer != 0:
                return ()
            gathers.append(_Exchange("gather_l1", [[bf[n][1] for n in parts[pt]] for pt in parts], scatter=False,
                                     after=after))
            return (gathers[1].token,)
        shards, lands = gathers[layer].wait(tuple(parts).index(part), after)
        full = [unpack[n](lax.dynamic_update_slice_in_dim(land, shard[None], me, axis=0))
                for n, shard, land in zip(parts[part], shards, lands)]
        return full if part == "rest" else full[0]

    to_chunks = dict(w_in=lambda a: _chunks_rows(a, N_IN // N_DEV),
                     w_out=lambda a: _chunks_rows(a, D_MODEL // N_DEV),
                     w_up=lambda a: a, w_down=lambda a: _chunks_rows(a, FF_BLOCK),
                     w_gate=lambda a: _chunks_rows(a, D_MODEL // N_DEV), w_ple=lambda a: _chunks_cols(a, D_MODEL // N_DEV))
    own = {n: [None, None] for n in BIG}
    scatters = {}

    def own_chunk(n, g32):
        if n == "w_ple":
            cols = g32.shape[1] // N_DEV
            return lax.dynamic_slice(g32, (0, me * cols), (g32.shape[0], cols))
        return lax.dynamic_index_in_dim(to_chunks[n](g32), me, axis=0, keepdims=False)

    def send(layer, part, grads):
        if part == "small":
            per_layer, d_final, loss = grads
            pack = _pack_small(
                *[jnp.concatenate([per_layer[0][n], per_layer[1][n]], axis=0) for n in ("norm1", "norm2", "norm3")],
                d_final.reshape(D_MODEL),
                jnp.concatenate([per_layer[0]["pool_scale"], per_layer[1]["pool_scale"]], axis=0),
                jnp.stack([per_layer[0]["pool_w"], per_layer[1]["pool_w"]]), spare=loss)
            scatters["small"] = _Exchange("gather_small", [[pack]], scatter=False)
            return (scatters["small"].token,)
        for n, (g32, _) in grads.items():
            own[n][layer] = own_chunk(n, g32)
        ex = _Exchange(f"scatter_{part}_l{layer}", [[to_chunks[n](g16) for n, (_, g16) in grads.items()]], scatter=True)
        scatters[layer, part] = (tuple(grads), ex)
        return (ex.token,)

    dx, sent = _local_step(
        x.reshape(seq, D_MODEL), p.reshape(2, seq, PLE_DIM), positions.reshape(seq), loss_target.reshape(seq, D_MODEL),
        norm1, pool_w, pool_scale, norm2, norm3, final_norm, weights, send)

    g_out, d_out, m_out, v_out = {}, {}, {}, {}
    for part in ("main", "in"):
        recv = {}
        for layer in (1, 0):
            names, ex = scatters[layer, part]
            for n, r in zip(names, ex.wait(0, sent)[1]):
                recv[n, layer] = r
        for n in names:
            grad, peers = jnp.stack(own[n]), (recv[n, 0], recv[n, 1])
            if n == "w_in":
                grad, peers = _sum_chunks(grad, *peers, "sum_w_in").transpose(0, 2, 1), (None, None)
            g_out[n], d_out[n], m_out[n], v_out[n] = _adamw_sharded(w[n], m[n], v[n], grad, *peers, "adamw_" + n)
        sent = tuple(d_out[n] for n in names)
    (mine,), (landed,) = scatters["small"].wait(0, sent)
    small_g8 = lax.dynamic_update_slice_in_dim(landed, mine[None], me, axis=0)
    pack = lambda t: _pack_small(*[t[n] for n in SMALL])
    small_g, d_small, m_small, v_small = _adamw_packed(pack(w), small_g8, pack(m), pack(v), "adamw_small")
    for dst, a in ((g_out, small_g), (d_out, d_small), (m_out, m_small), (v_out, v_small)):
        dst.update(_unpack_small(a))

    return (small_g[7, 2 * POOL_WIDTH],dx.reshape(1, seq, D_MODEL), *[g_out[n] for n in ORDER], *[d_out[n] for n in ORDER],
            *[m_out[n] for n in ORDER], *[v_out[n] for n in ORDER])
```

```python
import functools

import jax
import jax.numpy as jnp
from jax import lax
from jax.experimental import pallas as pl
from jax.experimental.pallas import tpu as pltpu

F32 = jnp.float32
BF16 = jnp.bfloat16

D_MODEL = 1024
HEAD_DIM = 64
POOL_WIDTH = 256
POOL_WINDOWS = (2, 4, 8, 16)
POOL_HALO = 16
POOL_PAD = 8
GROUP_WIDTH = 256
DILATIONS = (1, 4, 16)
ATTN_BLOCK = 128
ROT_SHIFT = 8
ROPE_THETA = 500000.0
D_FF = 4096
FF_BLOCK = 512
FF_PER_STEP = 2
MLP_BWD_TILE = 512
N_DEV = 8
N_IN = POOL_WIDTH + 3 * 768
PLE_DIM = 256
EPS = 1e-6
NEG_BIG = -1e30

ADAM_LR = 0.001
ADAM_B1 = 0.9
ADAM_B2 = 0.999
ADAM_EPS = 1e-08
ADAM_WD = 0.01
ADAM_STEP = 10

LANES = 128
VMEM_LIMIT = 56 * 1024 * 1024
MESH = pl.DeviceIdType.MESH


def _params(n_grid):
    return pltpu.CompilerParams(dimension_semantics=("arbitrary",) * n_grid, vmem_limit_bytes=VMEM_LIMIT)


def _dot(a, b):
    return jnp.dot(a, b, preferred_element_type=F32)


def _dot_nt(a, b):
    return lax.dot_general(a, b, (((1,), (1,)), ((), ())), preferred_element_type=F32)


def _dot_tn(a, b):
    return lax.dot_general(a, b, (((0,), (0,)), ((), ())), preferred_element_type=F32)


def _rms(x, g):
    rstd = lax.rsqrt(jnp.mean(x * x, axis=-1, keepdims=True) + EPS)
    n = x * rstd
    return n, rstd, n * g


def _rms_bwd(dy, n, rstd, g):
    dyn = dy * g
    dx = rstd * (dyn - n * jnp.mean(dyn * n, axis=-1, keepdims=True))
    return dx, jnp.sum(dy * n, axis=0, keepdims=True)


def _ordered_after(body, n_in, after):
    if not after:
        return body
    return lambda *refs: body(*refs[:n_in], *refs[n_in + len(after):])


def _row_tile(s, t):
    t = min(s, t)
    assert s % t == 0
    return t


def _rot(z, c, sa, sb):
    return z * c + pltpu.roll(z, ROT_SHIFT, 1) * sa + pltpu.roll(z, LANES - ROT_SHIFT, 1) * sb


def _table_specs(t):
    return [pl.BlockSpec((t, LANES), functools.partial(lambda i, k: (i, k), k=k)) for k in range(3)]


def _rot_t(dz, c, sa, sb):
    return dz * c + pltpu.roll(dz * sa, LANES - ROT_SHIFT, 1) + pltpu.roll(dz * sb, ROT_SHIFT, 1)


def _to_residues(value, stage, out_ref, dil):
    if dil == 1:
        out_ref[0] = value.astype(out_ref.dtype)
        return
    rows = value.shape[0] // dil
    for hf in range(GROUP_WIDTH // LANES):
        lanes = slice(hf * LANES, (hf + 1) * LANES)
        stage[hf][...] = value[:, lanes]
        for r in range(dil):
            out_ref[r, :, lanes] = stage[hf][pl.ds(r, rows, stride=dil), :].astype(out_ref.dtype)


def _from_residues(in_ref, stage, dil):
    if dil == 1:
        return in_ref[0].astype(F32)
    rows = in_ref.shape[1]
    for hf in range(GROUP_WIDTH // LANES):
        for r in range(dil):
            stage[hf][pl.ds(r, rows, stride=dil), :] = in_ref[r, :, hf * LANES:(hf + 1) * LANES].astype(F32)
    return jnp.concatenate([stage[0][...], stage[1][...]], axis=1)


def _residue_spec(dil, t):
    return pl.BlockSpec((dil, t // dil, GROUP_WIDTH), lambda i: (0, i, 0))


def _residue_shape(dil, s, dtype):
    return jax.ShapeDtypeStruct((dil, s // dil, GROUP_WIDTH), dtype)


def _stages(t, n):
    return [pltpu.VMEM((t, LANES), F32)] * (n * (GROUP_WIDTH // LANES))


def _pair_stages(refs):
    return [refs[i:i + 2] for i in range(0, len(refs), 2)]


def _normproj_fwd(h, g, w_in, rc, rsa, rsb, name):
    s = h.shape[0]
    t = _row_tile(s, 512)

    def body(h_ref, g_ref, w_ref, c_ref, sa_ref, sb_ref, hn_ref, u_ref, *rest):
        qkv_refs, stages = rest[:9], _pair_stages(rest[9:])
        _, _, hn = _rms(h_ref[...], g_ref[...])
        hb = hn.astype(BF16)
        hn_ref[...] = hb
        c, sa, sb = c_ref[...], sa_ref[...], sb_ref[...]

        def rot(z, scale):
            halves = [_rot(z[:, hf * LANES:(hf + 1) * LANES], c, sa, sb) * scale for hf in range(2)]
            return jnp.concatenate(halves, axis=1)

        proj = lambda lo: _dot_nt(hb, w_ref[lo:lo + GROUP_WIDTH, :])
        u_ref[...] = proj(0)
        for grp, dil in enumerate(DILATIONS):
            lo = POOL_WIDTH + grp * GROUP_WIDTH
            q_ref, k_ref, v_ref = qkv_refs[3 * grp:3 * grp + 3]
            _to_residues(rot(proj(lo), HEAD_DIM ** -0.5), stages[0], q_ref, dil)
            _to_residues(rot(proj(lo + 768), 1.0), stages[1], k_ref, dil)
            _to_residues(proj(lo + 1536), stages[2], v_ref, dil)

    row = lambda w: pl.BlockSpec((t, w), lambda i: (i, 0))
    return pl.pallas_call(
        body, name=name, grid=(s // t,),
        in_specs=[row(D_MODEL), pl.BlockSpec((1, D_MODEL), lambda i: (0, 0)),
                  pl.BlockSpec((N_IN, D_MODEL), lambda i: (0, 0))] + _table_specs(t),
        out_specs=[row(D_MODEL), row(POOL_WIDTH)] + [_residue_spec(dil, t) for dil in DILATIONS for _ in range(3)],
        out_shape=[jax.ShapeDtypeStruct((s, D_MODEL), BF16), jax.ShapeDtypeStruct((s, POOL_WIDTH), F32)]
        + [_residue_shape(dil, s, BF16) for dil in DILATIONS for _ in range(3)],
        scratch_shapes=_stages(t, 3),
        compiler_params=_params(1),
    )(h, g, w_in, rc, rsa, rsb)


def _pool_lane_window():
    lane = lax.broadcasted_iota(jnp.int32, (1, POOL_WIDTH), 1)
    return jnp.left_shift(2, lane // (POOL_WIDTH // len(POOL_WINDOWS)))


def _window_sums(ext, b2, b4, b8, t, lo, tile, direction):
    rows = t + POOL_HALO
    for src, dst, sh in ((ext, b2, 1), (b2, b4, 2), (b4, b8, 4)):
        dst[lo:lo + rows, :] = src[lo:lo + rows, :] + src[lo + direction * sh:lo + direction * sh + rows, :]
    s16 = b8[tile:tile + t, :] + b8[tile + direction * 8:tile + direction * 8 + t, :]
    win = _pool_lane_window()
    return jnp.where(win == 2, b2[tile:tile + t, :],
                     jnp.where(win == 4, b4[tile:tile + t, :], jnp.where(win == 8, b8[tile:tile + t, :], s16)))


def _pool_fwd(u, w_bd, scale, name, after=()):
    s = u.shape[0]
    t = _row_tile(s, 512)
    first = POOL_PAD + POOL_HALO

    def body(u_ref, w_ref, sc_ref, out_ref, y_ref, ext, b2, b4, b8):
        i = pl.program_id(0)

        @pl.when(i == 0)
        def _():
            for buf in (ext, b2, b4):
                buf[0:POOL_PAD, :] = jnp.zeros((POOL_PAD, POOL_WIDTH), F32)
            ext[POOL_PAD:first, :] = jnp.zeros((POOL_HALO, POOL_WIDTH), F32)

        x = u_ref[...]
        ext[first:, :] = x
        wsum = _window_sums(ext, b2, b4, b8, t, POOL_PAD, first, -1)
        pos = i * t + lax.broadcasted_iota(jnp.int32, (t, POOL_WIDTH), 0)
        cnt = jnp.minimum(pos + 1, _pool_lane_window()).astype(F32)
        y = wsum / cnt - x
        yb = y.astype(BF16)
        y_ref[...] = yb
        out_ref[...] = _dot(yb, w_ref[...]) * sc_ref[...]
        ext[POOL_PAD:first, :] = x[t - POOL_HALO:, :]

    row = pl.BlockSpec((t, POOL_WIDTH), lambda i: (i, 0))
    return pl.pallas_call(
        _ordered_after(body, 3, after), name=name, grid=(s // t,),
        in_specs=[row, pl.BlockSpec((POOL_WIDTH, POOL_WIDTH), lambda i: (0, 0)),
                  pl.BlockSpec((1, POOL_WIDTH), lambda i: (0, 0))] + [pl.BlockSpec(memory_space=pl.ANY)] * len(after),
        out_specs=[row, row],
        out_shape=[jax.ShapeDtypeStruct((s, POOL_WIDTH), F32), jax.ShapeDtypeStruct((s, POOL_WIDTH), BF16)],
        scratch_shapes=[pltpu.VMEM((t + POOL_HALO + POOL_PAD, POOL_WIDTH), F32)] * 4,
        compiler_params=_params(1),
    )(u, w_bd, scale, *after)


def _head_masks():
    lane = lax.broadcasted_iota(jnp.int32, (ATTN_BLOCK, GROUP_WIDTH), 1)
    return [lane // HEAD_DIM == hd for hd in range(GROUP_WIDTH // HEAD_DIM)]


def _stack_heads(a, masks):
    zero = jnp.zeros_like(a)
    return jnp.concatenate([jnp.where(m, a, zero) for m in masks], axis=0)


def _band_bias(first_step):
    rows = ATTN_BLOCK * (GROUP_WIDTH // HEAD_DIM)
    i = lax.broadcasted_iota(jnp.int32, (rows, 2 * ATTN_BLOCK), 0) & (ATTN_BLOCK - 1)
    j = lax.broadcasted_iota(jnp.int32, (rows, 2 * ATTN_BLOCK), 1)
    inner = jnp.where((j >= i) & (j <= i + ATTN_BLOCK), 0.0, NEG_BIG)
    return jnp.where((j < ATTN_BLOCK) & first_step, NEG_BIG, inner), inner


def _column_per_head(a):
    return jnp.concatenate([a[:, hd * HEAD_DIM:hd * HEAD_DIM + 1] for hd in range(GROUP_WIDTH // HEAD_DIM)], axis=0)


def _blocks_per_step(nb):
    return 8 if nb % 8 == 0 else 4 if nb % 4 == 0 else 2 if nb % 2 == 0 else 1


def _residues_per_step(dil, nb, qb):
    return 2 if (nb == qb and qb < 8 and dil % 2 == 0) else 1


def _attn_fwd(q, k, v, name, after=()):
    dil, length, _ = q.shape
    nb = length // ATTN_BLOCK
    qb = _blocks_per_step(nb)
    rs = _residues_per_step(dil, nb, qb)

    def body(q_ref, kp_ref, kc_ref, vp_ref, vc_ref, o_ref, lse_ref):
        masks = _head_masks()
        bias = _band_bias(pl.program_id(1) == 0)
        for rr in range(rs):
            for qi in range(qb):
                here = slice(qi * ATTN_BLOCK, (qi + 1) * ATTN_BLOCK)
                before = slice((qi - 1) * ATTN_BLOCK, qi * ATTN_BLOCK)
                kcat = jnp.concatenate([kp_ref[rr] if qi == 0 else kc_ref[rr, before], kc_ref[rr, here]], axis=0)
                vcat = jnp.concatenate([vp_ref[rr] if qi == 0 else vc_ref[rr, before], vc_ref[rr, here]], axis=0)
                qs = _stack_heads(q_ref[rr, here], masks)
                sc = _dot_nt(qs, kcat) + bias[min(qi, 1)]
                m = jnp.max(sc, axis=1, keepdims=True)
                e = jnp.exp(sc - m)
                l = jnp.sum(e, axis=1, keepdims=True)
                p = (e / l).astype(BF16)
                lse = m + jnp.log(l)
                o = jnp.zeros((ATTN_BLOCK, GROUP_WIDTH), F32)
                lse_full = jnp.zeros((ATTN_BLOCK, GROUP_WIDTH), F32)
                for hd, msk in enumerate(masks):
                    rows = slice(hd * ATTN_BLOCK, (hd + 1) * ATTN_BLOCK)
                    o = jnp.where(msk, _dot(p[rows], vcat), o)
                    lse_full = jnp.where(msk, lse[rows], lse_full)
                o_ref[rr, here] = o.astype(o_ref.dtype)
                lse_ref[rr, here] = lse_full

    cur = pl.BlockSpec((rs, qb * ATTN_BLOCK, GROUP_WIDTH), lambda r, j: (r, j, 0))
    prev = pl.BlockSpec((rs, ATTN_BLOCK, GROUP_WIDTH), lambda r, j: (r, jnp.maximum(qb * j - 1, 0), 0))
    return pl.pallas_call(
        _ordered_after(body, 5, after), name=name, grid=(dil // rs, nb // qb),
        in_specs=[cur, prev, cur, prev, cur] + [pl.BlockSpec(memory_space=pl.ANY)] * len(after), out_specs=[cur, cur],
        out_shape=[jax.ShapeDtypeStruct(q.shape, BF16), jax.ShapeDtypeStruct(q.shape, F32)],
        compiler_params=_params(2),
    )(q, k, k, v, v, *after)


def _group_weights(l0, l1, l2):
    m = jnp.maximum(jnp.maximum(l0, l1), l2)
    e0, e1, e2 = jnp.exp(l0 - m), jnp.exp(l1 - m), jnp.exp(l2 - m)
    den = e0 + e1 + e2
    return e0 / den, e1 / den, e2 / den


def _outproj_fwd(h, pool_out, o, lse, w_out, name):
    s = h.shape[0]
    t = _row_tile(s, 512)

    def body(h_ref, po_ref, o0, o1, o2, l0, l1, l2, w_ref, out_ref, a_ref, *stages):
        stages = _pair_stages(stages)
        ov =[_from_residues(r, stages[i], DILATIONS[i]) for i, r in enumerate((o0, o1, o2))]
        lv = [_from_residues(r, stages[3 + i], DILATIONS[i]) for i, r in enumerate((l0, l1, l2))]
        wts = _group_weights(*lv)
        a = jnp.concatenate([po_ref[...]] + [ov[i] * wts[i] for i in range(3)], axis=1).astype(BF16)
        a_ref[...] = a
        out_ref[...] = h_ref[...] + _dot(a, w_ref[...])

    row = lambda w: pl.BlockSpec((t, w), lambda i: (i, 0))
    res = [_residue_spec(dil, t) for dil in DILATIONS]
    return pl.pallas_call(
        body, name=name, grid=(s // t,),
        in_specs=[row(D_MODEL), row(POOL_WIDTH)] + res + res + [pl.BlockSpec((D_MODEL, D_MODEL), lambda i: (0, 0))],
        out_specs=[row(D_MODEL), row(D_MODEL)],
        out_shape=[jax.ShapeDtypeStruct((s, D_MODEL), F32), jax.ShapeDtypeStruct((s, D_MODEL), BF16)],
        scratch_shapes=_stages(t, 6),
        compiler_params=_params(1),
    )(h, pool_out, *o, *lse, w_out)


def _mlp_fwd(h, g, w_up, w_down, name):
    s = h.shape[0]
    t = _row_tile(s, 512)
    nblk = D_FF // FF_BLOCK

    def body(h_ref, g_ref, wu_ref, wd_ref, out_ref, hn_ref, r_ref):
        x = h_ref[...]
        _, _, hn = _rms(x, g_ref[...])
        hb = hn.astype(BF16)
        hn_ref[...] = hb
        acc = None
        for b0 in range(0, nblk, FF_PER_STEP):
            acts = []
            for b in range(b0, b0 + FF_PER_STEP):
                r = jnp.maximum(_dot(hb, wu_ref[b]), 0.0)
                r_ref[:, b * FF_BLOCK:(b + 1) * FF_BLOCK] = r.astype(BF16)
                acts.append((r * r).astype(BF16))
            wd = wd_ref[b0:b0 + FF_PER_STEP].reshape(FF_PER_STEP * FF_BLOCK, D_MODEL)
            part = _dot(jnp.concatenate(acts, axis=1), wd)
            acc = part if acc is None else acc + part
        out_ref[...] = x + acc

    row = lambda w: pl.BlockSpec((t, w), lambda i: (i, 0))
    resident = lambda shape: pl.BlockSpec(shape, lambda i: (0, 0, 0), pipeline_mode=pl.Buffered(1))
    return pl.pallas_call(
        body, name=name, grid=(s // t,),
        in_specs=[row(D_MODEL), pl.BlockSpec((1, D_MODEL), lambda i: (0, 0)),
                  resident((nblk, D_MODEL, FF_BLOCK)), resident((nblk, FF_BLOCK, D_MODEL))],
        out_specs=[row(D_MODEL), row(D_MODEL), row(D_FF)],
        out_shape=[jax.ShapeDtypeStruct((s, D_MODEL), F32), jax.ShapeDtypeStruct((s, D_MODEL), BF16),
                   jax.ShapeDtypeStruct((s, D_FF), BF16)],
        compiler_params=_params(1),
    )(h, g, w_up, w_down)


def _gate_fwd(h, g, w_gate, p, layer, w_ple, name, head=None):
    s = h.shape[0]
    t = _row_tile(s, 512)

    def body(h_ref, g_ref, wg_ref, p_ref, wp_ref, *refs):
        x = h_ref[...]
        _, _, hn = _rms(x, g_ref[...])
        hb = hn.astype(BF16)
        gate = 1.0 / (1.0 + jnp.exp(-_dot(hb, wg_ref[...])))
        pb = p_ref[...].astype(BF16)
        h3 = x + gate * _dot(pb, wp_ref[...])
        if head is None:
            out_ref, hn_ref, gate_ref, pb_ref = refs
            out_ref[...] = h3
        else:
            gf_ref, t_ref, hn_ref, gate_ref, pb_ref, loss_ref, dh_ref, dgf_ref = refs

            @pl.when(pl.program_id(0) == 0)
            def _():
                loss_ref[...] = jnp.zeros_like(loss_ref)
                dgf_ref[...] = jnp.zeros_like(dgf_ref)

            gf = gf_ref[...]
            n, rstd, y = _rms(h3, gf)
            err = y - t_ref[...]
            loss_ref[...] += jnp.sum(err * err) * (0.5 / D_MODEL)
            dh_ref[...], dgf = _rms_bwd(err * (1.0 / D_MODEL), n, rstd, gf)
            dgf_ref[...] += dgf
        hn_ref[...] = hb
        pb_ref[...] = pb
        gate_ref[...] = gate.astype(BF16)

    row = lambda w: pl.BlockSpec((t, w), lambda i: (i, 0))
    full = lambda a, b: pl.BlockSpec((a, b), lambda i: (0, 0))
    in_specs = [row(D_MODEL), full(1, D_MODEL), full(D_MODEL, D_MODEL),
                pl.BlockSpec((None, t, PLE_DIM), lambda i: (layer, i, 0)), full(PLE_DIM, D_MODEL)]
    saved_specs = [row(D_MODEL), row(D_MODEL), row(PLE_DIM)]
    saved_shapes = [jax.ShapeDtypeStruct((s, D_MODEL), BF16), jax.ShapeDtypeStruct((s, D_MODEL), BF16),
                    jax.ShapeDtypeStruct((s, PLE_DIM), BF16)]
    if head is None:
        return pl.pallas_call(
            body, name=name, grid=(s // t,), in_specs=in_specs, out_specs=[row(D_MODEL)] + saved_specs,
            out_shape=[jax.ShapeDtypeStruct((s, D_MODEL), F32)] + saved_shapes, compiler_params=_params(1),
        )(h, g, w_gate, p, w_ple)
    return pl.pallas_call(
        body, name=name, grid=(s // t,), in_specs=in_specs + [full(1, D_MODEL), row(D_MODEL)],
        out_specs=saved_specs + [pl.BlockSpec((1, LANES), lambda i: (0, 0)), row(D_MODEL), full(1, D_MODEL)],
        out_shape=saved_shapes + [jax.ShapeDtypeStruct((1, LANES), F32), jax.ShapeDtypeStruct((s, D_MODEL), F32),
                                  jax.ShapeDtypeStruct((1, D_MODEL), F32)],
        compiler_params=_params(1),
    )(h, g, w_gate, p, w_ple, *head)


def _gate_bwd(dh, gate, pb, w_ple, h, g, w_gate, hn, name, after=()):
    s = h.shape[0]
    t = _row_tile(s, 512)
    last = s // t - 1

    def body(dh_ref, gate_ref, pb_ref, wp_ref, h_ref, g_ref, wg_ref, hn_ref, out_ref, dg_ref, dwg_ref, dwgb_ref,
             dwp_ref, dwpb_ref):
        i = pl.program_id(0)

        @pl.when(i == 0)
        def _():
            dg_ref[...] = jnp.zeros_like(dg_ref)
            dwg_ref[...] = jnp.zeros_like(dwg_ref)
            dwp_ref[...] = jnp.zeros_like(dwp_ref)

        d = dh_ref[...]
        gate = gate_ref[...].astype(F32)
        pb = pb_ref[...]
        e = _dot(pb, wp_ref[...])
        dgl = (d * e * gate * (1.0 - gate)).astype(BF16)
        dwg_ref[...] += _dot_tn(hn_ref[...], dgl)
        dwp_ref[...] += _dot_tn(pb, (d * gate).astype(BF16))
        gv = g_ref[...]
        n, rstd, _ = _rms(h_ref[...], gv)
        dx, dg = _rms_bwd(_dot_nt(dgl, wg_ref[...]), n, rstd, gv)
        out_ref[...] = d + dx
        dg_ref[...] += dg

        @pl.when(i == last)
        def _():
            dwgb_ref[...] = dwg_ref[...].astype(BF16)
            dwpb_ref[...] = dwp_ref[...].astype(BF16)

    row = lambda w: pl.BlockSpec((t, w), lambda i: (i, 0))
    full = lambda a, b: pl.BlockSpec((a, b), lambda i: (0, 0))
    dh2, dg, dwg, dwgb, dwp, dwpb = pl.pallas_call(
        _ordered_after(body, 8, after), name=name, grid=(s // t,),
        in_specs=[row(D_MODEL), row(D_MODEL), row(PLE_DIM), full(PLE_DIM, D_MODEL), row(D_MODEL), full(1, D_MODEL),
                  full(D_MODEL, D_MODEL), row(D_MODEL)] + [pl.BlockSpec(memory_space=pl.ANY)] * len(after),
        out_specs=[row(D_MODEL), full(1, D_MODEL), full(D_MODEL, D_MODEL), full(D_MODEL, D_MODEL),
                   full(PLE_DIM, D_MODEL), full(PLE_DIM, D_MODEL)],
        out_shape=[jax.ShapeDtypeStruct((s, D_MODEL), F32), jax.ShapeDtypeStruct((1, D_MODEL), F32),
                   jax.ShapeDtypeStruct((D_MODEL, D_MODEL), F32), jax.ShapeDtypeStruct((D_MODEL, D_MODEL), BF16),
                   jax.ShapeDtypeStruct((PLE_DIM, D_MODEL), F32), jax.ShapeDtypeStruct((PLE_DIM, D_MODEL), BF16)],
        compiler_params=_params(1),
    )(dh, gate, pb, w_ple, h, g, w_gate, hn, *after)
    return dh2, dg, (dwg, dwgb), (dwp, dwpb)


def _mlp_bwd(dh, r, h, g, w_up, w_down, name):
    s = h.shape[0]
    t = _row_tile(s, MLP_BWD_TILE)
    nblk = D_FF // FF_BLOCK

    def body(dh_ref, r_ref, h_ref, g_ref, wu_ref, wd_ref, out_ref, dup_ref, dg_ref, dhb_ref):
        @pl.when(pl.program_id(0) == 0)
        def _():
            dg_ref[...] = jnp.zeros_like(dg_ref)

        d = dh_ref[...]
        db = d.astype(BF16)
        dhb_ref[...] = db
        back = None
        for b in range(nblk):
            cols = slice(b * FF_BLOCK, (b + 1) * FF_BLOCK)
            dup = (_dot_nt(db, wd_ref[b]) * (2.0 * r_ref[:, cols].astype(F32))).astype(BF16)
            dup_ref[:, cols] = dup
            part = _dot_nt(dup, wu_ref[b])
            back = part if back is None else back + part
        gv = g_ref[...]
        n, rstd, _ = _rms(h_ref[...], gv)
        dx, dg = _rms_bwd(back, n, rstd, gv)
        out_ref[...] = d + dx
        dg_ref[...] += dg

    row = lambda w: pl.BlockSpec((t, w), lambda i: (i, 0))
    vec = pl.BlockSpec((1, D_MODEL), lambda i: (0, 0))
    resident = lambda shape: pl.BlockSpec(shape, lambda i: (0, 0, 0), pipeline_mode=pl.Buffered(1))
    return pl.pallas_call(
        body, name=name, grid=(s // t,),
        in_specs=[row(D_MODEL), row(D_FF), row(D_MODEL), vec,
                  resident((nblk, D_MODEL, FF_BLOCK)), resident((nblk, FF_BLOCK, D_MODEL))],
        out_specs=[row(D_MODEL), row(D_FF), vec, row(D_MODEL)],
        out_shape=[jax.ShapeDtypeStruct((s, D_MODEL), F32), jax.ShapeDtypeStruct((s, D_FF), BF16),
                   jax.ShapeDtypeStruct((1, D_MODEL), F32), jax.ShapeDtypeStruct((s, D_MODEL), BF16)],
        compiler_params=_params(1),
    )(dh, r, h, g, w_up, w_down)


def _outproj_bwd(dh, w_out, o, lse, ones_bd, a, name):
    s = dh.shape[0]
    t = _row_tile(s, 512)
    last = s // t - 1

    def body(dh_ref, w_ref, o0, o1, o2, l0, l1, l2, bd_ref, a_ref, dp_ref, do0, do1, do2, de0, de1, de2, dw_ref,
             dwb_ref, *stages):
        i = pl.program_id(0)

        @pl.when(i == 0)
        def _():
            dw_ref[...] = jnp.zeros_like(dw_ref)

        stages = _pair_stages(stages)
        dhb = dh_ref[...].astype(BF16)
        dw_ref[...] += _dot_tn(a_ref[...], dhb)

        @pl.when(i == last)
        def _():
            dwb_ref[...] = dw_ref[...].astype(BF16)

        da = _dot_nt(dhb, w_ref[...])
        dp_ref[...] = da[:, 0:POOL_WIDTH]
        ov =[_from_residues(r, stages[i], DILATIONS[i]) for i, r in enumerate((o0, o1, o2))]
        lv = [_from_residues(r, stages[3 + i], DILATIONS[i]) for i, r in enumerate((l0, l1, l2))]
        wts = _group_weights(*lv)
        bd = bd_ref[...]
        cbar = jnp.zeros((t, GROUP_WIDTH), F32)
        for grp, do_ref in enumerate((do0, do1, do2)):
            lo = POOL_WIDTH + grp * GROUP_WIDTH
            dag = da[:, lo:lo + GROUP_WIDTH]
            _to_residues(dag * wts[grp], stages[6 + grp], do_ref, DILATIONS[grp])
            prod = dag * ov[grp]
            hi = prod.astype(BF16)
            low = (prod - hi.astype(F32)).astype(BF16)
            cbar = cbar + wts[grp] * (_dot(hi, bd) + _dot(low, bd))
        for grp, de_ref in enumerate((de0, de1, de2)):
            _to_residues(wts[grp] * cbar, stages[9 + grp], de_ref, DILATIONS[grp])

    row = lambda w: pl.BlockSpec((t, w), lambda i: (i, 0))
    full = lambda a, b: pl.BlockSpec((a, b), lambda i: (0, 0))
    res = [_residue_spec(dil, t) for dil in DILATIONS]
    *outs, dw, dwb = pl.pallas_call(
        body, name=name, grid=(s // t,),
        in_specs=[row(D_MODEL), full(D_MODEL, D_MODEL)] + res + res + [full(GROUP_WIDTH, GROUP_WIDTH), row(D_MODEL)],
        out_specs=[row(POOL_WIDTH)] + res + res + [full(D_MODEL, D_MODEL)] * 2,
        out_shape=[jax.ShapeDtypeStruct((s, POOL_WIDTH), F32)] + [_residue_shape(dil, s, BF16) for dil in DILATIONS]
        + [_residue_shape(dil, s, F32) for dil in DILATIONS]
        + [jax.ShapeDtypeStruct((D_MODEL, D_MODEL), F32), jax.ShapeDtypeStruct((D_MODEL, D_MODEL), BF16)],
        scratch_shapes=_stages(t, 12),
        compiler_params=_params(1),
    )(dh, w_out, *o, *lse, ones_bd, a)
    return (*outs, (dw, dwb))


def _attn_bwd(q, k, v, do, lse, deff, name, after=()):
    dil, length, _ = q.shape
    nb = length // ATTN_BLOCK
    qb = _blocks_per_step(nb)
    nj = nb // qb
    rs = _residues_per_step(dil, nb, qb)
    whole = nj == 1
    tail = slice((qb - 1) * ATTN_BLOCK, qb * ATTN_BLOCK)
    block = lambda qi: slice(qi * ATTN_BLOCK, (qi + 1) * ATTN_BLOCK)

    def body(q_ref, kp_ref, kc_ref, vp_ref, vc_ref, do_ref, lse_ref, de_ref, dq_ref, dk_ref, dv_ref, ck, cv):
        j = pl.program_id(1)

        def compute():
            masks = _head_masks()
            bias = _band_bias(j == 0)
            for rr in range(rs):
                dkc, dvc = [], []
                for qi in range(qb):
                    here, before = block(qi), block(qi - 1)
                    kcat = jnp.concatenate([kp_ref[rr] if qi == 0 else kc_ref[rr, before], kc_ref[rr, here]], axis=0)
                    vcat = jnp.concatenate([vp_ref[rr] if qi == 0 else vc_ref[rr, before], vc_ref[rr, here]], axis=0)
                    qs = _stack_heads(q_ref[rr, here], masks)
                    dos = _stack_heads(do_ref[rr, here], masks)
                    sc = _dot_nt(qs, kcat) + bias[min(qi, 1)]
                    p = jnp.exp(sc - _column_per_head(lse_ref[rr, here]))
                    ds = (p * (_dot_nt(dos, vcat) - _column_per_head(de_ref[rr, here]))).astype(BF16)
                    dq = jnp.zeros((ATTN_BLOCK, GROUP_WIDTH), F32)
                    for hd, msk in enumerate(masks):
                        dq = jnp.where(msk, _dot(ds[block(hd)], kcat), dq)
                    dq_ref[rr, here] = dq.astype(dq_ref.dtype)
                    dkc.append(_dot_tn(ds, qs))
                    dvc.append(_dot_tn(p.astype(BF16), dos))

                for out_ref, carry, parts in ((dk_ref, ck, dkc), (dv_ref, cv, dvc)):
                    full = [parts[qi][ATTN_BLOCK:] + parts[qi + 1][0:ATTN_BLOCK] for qi in range(qb - 1)]
                    if whole:
                        for qi, val in enumerate(full + [parts[qb - 1][ATTN_BLOCK:]]):
                            out_ref[rr, block(qi)] = val.astype(out_ref.dtype)
                        continue

                    @pl.when(j > 0)
                    def _():
                        if qb > 1:
                            out_ref[0, 0:(qb - 1) * ATTN_BLOCK] = carry[0:(qb - 1) * ATTN_BLOCK].astype(out_ref.dtype)
                        out_ref[0, tail] = (carry[tail] + parts[0][0:ATTN_BLOCK]).astype(out_ref.dtype)

                    for qi, val in enumerate(full):
                        carry[block(qi)] = val
                    carry[tail] = parts[qb - 1][ATTN_BLOCK:]

        if whole:
            compute()
        else:
            pl.when(j < nj)(compute)

            @pl.when(j == nj)
            def _():
                dk_ref[0] = ck[...].astype(dk_ref.dtype)
                dv_ref[0] = cv[...].astype(dv_ref.dtype)

    step = lambda j: jnp.minimum(j, nj - 1)
    cur = pl.BlockSpec((rs, qb * ATTN_BLOCK, GROUP_WIDTH), lambda r, j: (r, step(j), 0))
    prev = pl.BlockSpec((rs, ATTN_BLOCK, GROUP_WIDTH), lambda r, j: (r, jnp.maximum(qb * step(j) - 1, 0), 0))
    late = pl.BlockSpec((rs, qb * ATTN_BLOCK, GROUP_WIDTH), lambda r, j: (r, jnp.maximum(j - 1, 0), 0))
    return pl.pallas_call(
        _ordered_after(body, 8, after), name=name, grid=(dil // rs, 1 if whole else nj + 1),
        in_specs=[cur, prev, cur, prev, cur, cur, cur, cur] + [pl.BlockSpec(memory_space=pl.ANY)] * len(after),
        out_specs=[cur, cur if whole else late, cur if whole else late],
        out_shape=[jax.ShapeDtypeStruct(q.shape, BF16)] * 3,
        scratch_shapes=[pltpu.VMEM((qb * ATTN_BLOCK, GROUP_WIDTH), F32)] * 2,
        compiler_params=_params(2),
    )(q, k, k, v, v, do, lse, deff, *after)


def _pool_bwd(dpool, y, w_bd, scale, name, after=()):
    s = dpool.shape[0]
    t = _row_tile(s, 512)
    nt = s // t

    def body(dp_ref, y_ref, w_ref, sc_ref, du_ref, dw_ref, dsc_ref, ext, b2, b4, b8):
        i = pl.program_id(0)

        @pl.when(i == 0)
        def _():
            ext[t:, :] = jnp.zeros((POOL_HALO + POOL_PAD, POOL_WIDTH), F32)
            for buf in (b2, b4):
                buf[t + POOL_HALO:, :] = jnp.zeros((POOL_PAD, POOL_WIDTH), F32)
            dw_ref[...] = jnp.zeros_like(dw_ref)
            dsc_ref[...] = jnp.zeros_like(dsc_ref)

        dp = dp_ref[...]
        yb = y_ref[...]
        w = w_ref[...]
        dsc_ref[...] += jnp.sum(dp * _dot(yb, w), axis=0, keepdims=True)
        dyo = (dp * sc_ref[...]).astype(BF16)
        dw_ref[...] += _dot_tn(yb, dyo)
        dy = _dot_nt(dyo, w)
        win = _pool_lane_window()
        pos = (nt - 1 - i) * t + lax.broadcasted_iota(jnp.int32, (t, POOL_WIDTH), 0)
        gq = dy / jnp.minimum(pos + 1, win).astype(F32)
        ext[0:t, :] = gq
        du_ref[...] = _window_sums(ext, b2, b4, b8, t, 0, 0, 1) - dy
        ext[t:t + POOL_HALO, :] = gq[0:POOL_HALO, :]

    rev = pl.BlockSpec((t, POOL_WIDTH), lambda i: (nt - 1 - i, 0))
    full = lambda a, b: pl.BlockSpec((a, b), lambda i: (0, 0))
    return pl.pallas_call(
        _ordered_after(body, 4, after), name=name, grid=(nt,),
        in_specs=[rev, rev, full(POOL_WIDTH, POOL_WIDTH), full(1, POOL_WIDTH)]
        + [pl.BlockSpec(memory_space=pl.ANY)] * len(after),
        out_specs=[rev, full(POOL_WIDTH, POOL_WIDTH), full(1, POOL_WIDTH)],
        out_shape=[jax.ShapeDtypeStruct((s, POOL_WIDTH), F32), jax.ShapeDtypeStruct((POOL_WIDTH, POOL_WIDTH), F32),
                   jax.ShapeDtypeStruct((1, POOL_WIDTH), F32)],
        scratch_shapes=[pltpu.VMEM((t + POOL_HALO + POOL_PAD, POOL_WIDTH), F32)] * 4,
        compiler_params=_params(1),
    )(dpool, y, w_bd, scale, *after)


def _normproj_bwd(dh, du, dq, dk, dv, rc, rsa, rsb, w_in, h, g, name):
    s = h.shape[0]
    t = _row_tile(s, 512)

    def body(dh_ref, du_ref, q0, q1, q2, k0, k1, k2, v0, v1, v2, c_ref, sa_ref, sb_ref, w_ref, h_ref, g_ref,
             out_ref, dz_ref, dg_ref, *stages):
        @pl.when(pl.program_id(0) == 0)
        def _():
            dg_ref[...] = jnp.zeros_like(dg_ref)

        c, sa, sb = c_ref[...], sa_ref[...], sb_ref[...]

        def unrot(a, scale):
            halves = [_rot_t(a[:, hf * LANES:(hf + 1) * LANES] * scale, c, sa, sb) for hf in range(2)]
            return jnp.concatenate(halves, axis=1)

        staged = _pair_stages(stages)
        tok = lambda refs, base: [_from_residues(r, staged[base + i], DILATIONS[i]) for i, r in enumerate(refs)]
        chunks = [du_ref[...]]
        chunks += [unrot(a, HEAD_DIM ** -0.5) for a in tok((q0, q1, q2), 0)]
        chunks += [unrot(a, 1.0) for a in tok((k0, k1, k2), 3)]
        chunks += tok((v0, v1, v2), 6)
        acc = jnp.zeros((t, D_MODEL), F32)
        for ci, ch in enumerate(chunks):
            cols = slice(ci * GROUP_WIDTH, (ci + 1) * GROUP_WIDTH)
            cb = ch.astype(BF16)
            dz_ref[:, cols] = cb
            acc = acc + _dot(cb, w_ref[cols, :])
        gv = g_ref[...]
        n, rstd, _ = _rms(h_ref[...], gv)
        dx, dg = _rms_bwd(acc, n, rstd, gv)
        out_ref[...] = dh_ref[...] + dx
        dg_ref[...] += dg

    row = lambda w: pl.BlockSpec((t, w), lambda i: (i, 0))
    vec = pl.BlockSpec((1, D_MODEL), lambda i: (0, 0))
    res = [_residue_spec(dil, t) for dil in DILATIONS]
    return pl.pallas_call(
        body, name=name, grid=(s // t,),
        in_specs=[row(D_MODEL), row(POOL_WIDTH)] + res * 3 + _table_specs(t)
        + [pl.BlockSpec((N_IN, D_MODEL), lambda i: (0, 0)), row(D_MODEL), vec],
        out_specs=[row(D_MODEL), row(N_IN), vec],
        out_shape=[jax.ShapeDtypeStruct((s, D_MODEL), F32), jax.ShapeDtypeStruct((s, N_IN), BF16),
                   jax.ShapeDtypeStruct((1, D_MODEL), F32)],
        scratch_shapes=_stages(t, 9),
        compiler_params=_params(1),
    )(dh, du, *dq, *dk, *dv, rc, rsa, rsb, w_in, h, g)


def _matmul_tn(a, b, name, *, square_a=False, tm=None, tn=None, blocked_out=False, after=()):
    s, m = a.shape
    n = b.shape[1]
    tk = _row_tile(s, 2048)
    tm = tm or min(m, 1024)
    tn = tn or min(n, 1024)
    assert m % tm == 0 and n % tn == 0
    nk = s // tk
    nsub = tn // FF_BLOCK if blocked_out else 1

    def body(a_ref, b_ref, o_ref, ob_ref, acc):
        k = pl.program_id(2)

        def product():
            av = a_ref[...]
            if square_a:
                av = av.astype(F32)
                av = av * av
            return _dot_tn(av.astype(BF16), b_ref[...].astype(BF16))

        def emit(total):
            if blocked_out:
                for sub in range(nsub):
                    cols = slice(sub * FF_BLOCK, (sub + 1) * FF_BLOCK)
                    o_ref[sub] = total[:, cols]
                    ob_ref[sub] = total[:, cols].astype(BF16)
            else:
                o_ref[...] = total
                ob_ref[...] = total.astype(BF16)

        if nk == 1:
            emit(product())
            return

        @pl.when(k == 0)
        def _():
            acc[...] = product()

        @pl.when((k > 0) & (k < nk - 1))
        def _():
            acc[...] += product()

        @pl.when(k == nk - 1)
        def _():
            emit(acc[...] + product())

    if blocked_out:
        shape = (n // FF_BLOCK, m, FF_BLOCK)
        out_spec = pl.BlockSpec((nsub, tm, FF_BLOCK), lambda i, j, k: (j, i, 0))
    else:
        shape = (m, n)
        out_spec = pl.BlockSpec((tm, tn), lambda i, j, k: (i, j))
    return pl.pallas_call(
        _ordered_after(body, 2, after), name=name, grid=(m // tm, n // tn, nk),
        in_specs=[pl.BlockSpec((tk, tm), lambda i, j, k: (k, i)), pl.BlockSpec((tk, tn), lambda i, j, k: (k, j))]
        + [pl.BlockSpec(memory_space=pl.ANY)] * len(after),
        out_specs=[out_spec, out_spec],
        out_shape=[jax.ShapeDtypeStruct(shape, F32), jax.ShapeDtypeStruct(shape, BF16)],
        scratch_shapes=[pltpu.VMEM((tm, tn), F32)],
        compiler_params=_params(3),
    )(a, b, *after)


def _adamw_math(w, g, m, v):
    m = ADAM_B1 * m + (1.0 - ADAM_B1) * g
    v = ADAM_B2 * v + (1.0 - ADAM_B2) * (g * g)
    m_hat = m / (1.0 - ADAM_B1 ** ADAM_STEP)
    v_hat = v / (1.0 - ADAM_B2 ** ADAM_STEP)
    delta = -ADAM_LR * (m_hat / (jnp.sqrt(v_hat) + ADAM_EPS) + ADAM_WD * w)
    return delta, m, v


def _sum_chunks_body(own0_ref, own1_ref, r0_ref, r1_ref):
    layer0 = pl.program_id(0) == 0
    g = jnp.where(layer0, own0_ref[...], own1_ref[...])
    for k in range(N_DEV - 1):
        g = g + jnp.where(layer0, r0_ref[k], r1_ref[k]).astype(F32)
    return g


def _chunk_specs(t, cols):
    rows_of = lambda layer: (lambda l, i: jnp.where(l == layer, i, 0))
    blk = pl.BlockSpec((None, t, cols), lambda l, i, me: (l, i, 0))
    own = [pl.BlockSpec((None, t, cols), functools.partial(lambda l, i, me, pick: (me[0], pick(l, i), 0), pick=rows_of(ly)))
           for ly in range(2)]
    recv = [pl.BlockSpec((N_DEV - 1, t, cols), functools.partial(lambda l, i, me, pick: (0, pick(l, i), 0), pick=rows_of(ly)))
            for ly in range(2)]
    return blk, own + recv


def _sum_chunks(chunks, me, name):
    _, rows, cols = chunks[0].shape
    t = _row_tile(rows, 320)

    def body(me_ref, own0_ref, own1_ref, r0_ref, r1_ref, g_ref):
        g_ref[...] = _sum_chunks_body(own0_ref, own1_ref, r0_ref, r1_ref)

    blk, chunk_specs = _chunk_specs(t, cols)
    return pl.pallas_call(
        body, name=name,
        grid_spec=pltpu.PrefetchScalarGridSpec(num_scalar_prefetch=1, grid=(2, rows // t), in_specs=chunk_specs,
                                               out_specs=blk),
        out_shape=jax.ShapeDtypeStruct((2, rows, cols), F32), compiler_params=_params(2),
    )(me, *chunks)


def _adamw_sharded(w, m, v, grad, me, name):
    _, rows, cols = w.shape
    t = _row_tile(rows, 256)
    summed = not isinstance(grad, tuple)
    grad = (grad,) if summed else grad

    def body(me_ref, w_ref, m_ref, v_ref, *refs):
        g_ref, d_ref, nm_ref, nv_ref = refs[-4:]
        g = refs[0][...] if summed else _sum_chunks_body(*refs[:4])
        g_ref[...] = g
        d_ref[...], nm_ref[...], nv_ref[...] = _adamw_math(w_ref[...], g, m_ref[...], v_ref[...])

    blk, chunk_specs = _chunk_specs(t, cols)
    return pl.pallas_call(
        body, name=name,
        grid_spec=pltpu.PrefetchScalarGridSpec(
            num_scalar_prefetch=1, grid=(2, rows // t),
            in_specs=[blk, blk, blk] + ([blk] if summed else chunk_specs), out_specs=[blk] * 4),
        out_shape=[jax.ShapeDtypeStruct(w.shape, F32)] * 4,
        compiler_params=_params(2),
    )(me, w, m, v, *grad)


def _adamw_packed(w, g8, m, v, name):
    def body(w_ref, g_ref, m_ref, v_ref, go_ref, d_ref, nm_ref, nv_ref):
        g = g_ref[0]
        for dev in range(1, N_DEV):
            g = g + g_ref[dev]
        go_ref[...] = g
        d_ref[...], nm_ref[...], nv_ref[...] = _adamw_math(w_ref[...], g, m_ref[...], v_ref[...])

    return pl.pallas_call(
        body, name=name, out_shape=[jax.ShapeDtypeStruct(w.shape, F32)] * 4,
        compiler_params=pltpu.CompilerParams(vmem_limit_bytes=VMEM_LIMIT),
    )(w, g8, m, v)


def _peer(k):
    x, y, c = lax.axis_index("x"), lax.axis_index("y"), lax.axis_index("c")
    return (1 - x if k & 4 else x, 1 - y if k & 2 else y, 1 - c if k & 1 else c)


def _linear(dev):
    return 4 * dev[0] + 2 * dev[1] + dev[2]


HBM_SPEC = pl.BlockSpec(memory_space=pltpu.HBM)
SEM_SPEC = pl.BlockSpec(memory_space=pltpu.SEMAPHORE)
ANY_SPEC = pl.BlockSpec(memory_space=pl.ANY)
EFFECT = pltpu.SideEffectType.DATAFLOW_SIDE_EFFECTING


def _in_hbm(a):
    return pltpu.with_memory_space_constraint(a, pltpu.HBM)


class _Exchange:
    def __init__(self, name, groups, scatter, after=()):
        self.name, self.scatter = name, scatter
        self.sizes = sizes = [len(g) for g in groups]
        srcs = [a for g in groups for a in g]
        n, ng = len(srcs), len(groups)
        lead = (N_DEV - 1,) if scatter else (N_DEV,)
        shapes = [lead + (a.shape[1:] if scatter else a.shape) for a in srcs]
        lands = [lax.empty(sh, a.dtype) for sh, a in zip(shapes, srcs)]
        offsets = [sum(sizes[:gi]) for gi in range(ng)]
        copy = self._copy

        def body(*refs):
            src, land = refs[:n], refs[n:2 * n]
            sems = refs[2 * n + len(after):2 * n + len(after) + 2 * ng]
            token = refs[-1]
            for gi in range(ng):
                for wi in range(sizes[gi]):
                    w = offsets[gi] + wi
                    for k in range(1, N_DEV):
                        copy(src[w], land[w], sems[2 * gi], sems[2 * gi + 1], wi, k).start()
            token[...] = jnp.zeros_like(token)

        sem_shapes = [pltpu.SemaphoreType.DMA((7 * sz,)) for sz in sizes for _ in range(2)]
        outs = pl.pallas_call(
            body, name=name + "_start",
            in_specs=[HBM_SPEC] * (2 * n) + [ANY_SPEC] * len(after),
            out_specs=[SEM_SPEC] * (2 * ng) + [HBM_SPEC] * (2 * n) + [pl.BlockSpec(memory_space=pltpu.VMEM)],
            out_shape=sem_shapes + [pltpu.HBM(a.shape, a.dtype) for a in srcs + lands]
            + [jax.ShapeDtypeStruct((8, LANES), F32)],
            input_output_aliases={i: 2 * ng + i for i in range(2 * n)},
            compiler_params=pltpu.CompilerParams(has_side_effects=EFFECT),
        )(*[_in_hbm(a) for a in srcs + lands], *after)
        self.sems = [outs[2 * gi:2 * gi + 2] for gi in range(ng)]
        thru = outs[2 * ng:2 * ng + 2 * n]
        self.srcs = [thru[offsets[gi]:offsets[gi] + sizes[gi]] for gi in range(ng)]
        self.lands = [thru[n + offsets[gi]:n + offsets[gi] + sizes[gi]] for gi in range(ng)]
        self.token = outs[-1]

    def _copy(self, src, land, send_sems, recv_sems, wi, k):
        to = _peer(k)
        if self.scatter:
            src_ref, dst_ref = src.at[_linear(to)], land.at[k - 1]
        else:
            src_ref, dst_ref = src, land.at[_linear(_peer(0))]
        return pltpu.make_async_remote_copy(
            src_ref=src_ref, dst_ref=dst_ref, send_sem=send_sems.at[7 * wi + k - 1],
            recv_sem=recv_sems.at[7 * wi + k - 1], device_id=to, device_id_type=MESH)

    def wait(self, gi, after):
        n = self.sizes[gi]
        copy = self._copy

        def body(*refs):
            src, land = refs[:n], refs[n:2 * n]
            send_sems, recv_sems = refs[2 * n], refs[2 * n + 1]
            for wi in range(n):
                for k in range(1, N_DEV):
                    cp = copy(src[wi], land[wi], send_sems, recv_sems, wi, k)
                    cp.wait_send()
                    cp.wait_recv()

        arrays = list(self.srcs[gi]) + list(self.lands[gi])
        outs = pl.pallas_call(
            body, name=f"{self.name}_wait{gi}",
            in_specs=[HBM_SPEC] * (2 * n) + [SEM_SPEC, SEM_SPEC] + [ANY_SPEC] * len(after),
            out_specs=[HBM_SPEC] * (2 * n),
            out_shape=[pltpu.HBM(a.shape, a.dtype) for a in arrays],
            input_output_aliases={i: i for i in range(2 * n)},
            compiler_params=pltpu.CompilerParams(has_side_effects=EFFECT),
        )(*arrays, *self.sems[gi], *after)
        return outs[:n], outs[n:]


def _rotary_tables(positions):
    rot_dim = HEAD_DIM // 4
    inv_freq = ROPE_THETA ** (-jnp.arange(0, rot_dim, 2, dtype=F32) / rot_dim)
    ang = positions.astype(F32)[:, None] * inv_freq
    cs = jnp.concatenate([jnp.cos(ang), jnp.sin(ang)], axis=1)
    dim = jnp.arange(LANES) % HEAD_DIM
    first, second = dim < ROT_SHIFT, (dim >= ROT_SHIFT) & (dim < rot_dim)
    src = jnp.arange(2 * ROT_SHIFT)[:, None]
    angle = (dim % ROT_SHIFT)[None, :]
    c = jnp.where((first | second)[None, :] & (src == angle), 1.0, 0.0)
    sa = jnp.where(second[None, :] & (src == angle + ROT_SHIFT), 1.0, 0.0)
    sb = jnp.where(first[None, :] & (src == angle + ROT_SHIFT), -1.0, 0.0)
    spread = jnp.concatenate([c, sa, sb], axis=1).astype(F32)
    base = jnp.concatenate([jnp.where(first | second, 0.0, 1.0), jnp.zeros((2 * LANES,))]).astype(F32)[None, :]
    return jnp.dot(cs, spread, precision=lax.Precision.HIGHEST, preferred_element_type=F32) + base


def _block_diag(pool_w):
    gc = pool_w.shape[-1]
    out = jnp.zeros((POOL_WIDTH, POOL_WIDTH), pool_w.dtype)
    for grp in range(pool_w.shape[0]):
        out = lax.dynamic_update_slice(out, pool_w[grp], (grp * gc, grp * gc))
    return out


def _diag_blocks(a):
    gc = POOL_WIDTH // len(POOL_WINDOWS)
    return jnp.stack([a[grp * gc:(grp + 1) * gc, grp * gc:(grp + 1) * gc] for grp in range(len(POOL_WINDOWS))])


def _local_step(x, p, positions, loss_target, norm1, pool_w, pool_scale, norm2, norm3, final_norm, weights, send):
    rc = rsa = rsb = _rotary_tables(positions)
    ones_bd = _block_diag(jnp.ones((4, HEAD_DIM, HEAD_DIM), BF16))
    saved = []
    h = x
    for i in range(2):
        tag = f"_l{i}"
        g1, g2, g3 = norm1[i:i + 1], norm2[i:i + 1], norm3[i:i + 1]
        w_bd = _block_diag(pool_w[i]).astype(BF16)
        scale = pool_scale[i:i + 1]
        w_in = weights(i, "in", (h, rc, w_bd))
        hn1, u, *qkv = _normproj_fwd(h, g1, w_in, rc, rsa, rsb, "normproj_fwd" + tag)
        qkv = [qkv[3 * grp:3 * grp + 3] for grp in range(3)]
        started = weights(i, "prefetch", (hn1,))
        pool_out, y = _pool_fwd(u, w_bd, scale, "pool_fwd" + tag, after=started)
        o, lse = zip(*[_attn_fwd(*qkv[grp], f"attn_fwd{tag}_g{grp}", after=started) for grp in range(3)])
        w_out = weights(i, "out", (pool_out, *o))
        h1, a = _outproj_fwd(h, pool_out, o, lse, w_out, "outproj_fwd" + tag)
        w_up, w_down = weights(i, "mlp", (h1,))
        h2, hn2, r = _mlp_fwd(h1, g2, w_up, w_down, "mlp_fwd" + tag)
        w_gate, w_ple = weights(i, "gate", (h2,))
        h0 = h
        if i == 0:
            h, hn3, gate, pb = _gate_fwd(h2, g3, w_gate, p, i, w_ple, "gate_fwd" + tag)
        else:
            hn3, gate, pb, loss, dh, d_final = _gate_fwd(h2, g3, w_gate, p, i, w_ple, "gate_fwd" + tag,
                                                         head=(final_norm.reshape(1, D_MODEL), loss_target))
        saved.append(dict(h0=h0, hn1=hn1, qkv=qkv, y=y, o=o, lse=lse, a=a, h1=h1, hn2=hn2, r=r, h2=h2,
                          hn3=hn3, gate=gate, pb=pb, w_bd=w_bd, scale=scale, g1=g1, g2=g2, g3=g3,
                          w_in=w_in, w_out=w_out, w_up=w_up, w_down=w_down, w_gate=w_gate, w_ple=w_ple))

    grads = [None, None]
    sent = ()
    for i in (1, 0):
        tag = f"_l{i}"
        sv = saved[i]
        dh2, dg3, dw_gate, dw_ple = _gate_bwd(dh, sv["gate"], sv["pb"], sv["w_ple"], sv["h2"], sv["g3"], sv["w_gate"],
                                              sv["hn3"], "gate_bwd" + tag, after=sent)
        dh1, dup, dg2, dh2b = _mlp_bwd(dh2, sv["r"], sv["h1"], sv["g2"], sv["w_up"], sv["w_down"], "mlp_bwd" + tag)
        dw_down = _matmul_tn(sv["r"], dh2b, "dw_down" + tag, square_a=True)
        dw_up = _matmul_tn(sv["hn2"], dup, "dw_up" + tag, blocked_out=True)
        dpool, do0, do1, do2, de0, de1, de2, dw_out = _outproj_bwd(dh1, sv["w_out"], sv["o"], sv["lse"], ones_bd,
                                                                   sv["a"], "outproj_bwd" + tag)
        sent = send(i, "main", dict(w_gate=dw_gate, w_ple=dw_ple, w_down=dw_down, w_up=dw_up, w_out=dw_out))
        dqkv = [_attn_bwd(*sv["qkv"][grp], do_g, sv["lse"][grp], de_g, f"attn_bwd{tag}_g{grp}", after=sent)
                for grp, (do_g, de_g) in enumerate(((do0, de0), (do1, de1), (do2, de2)))]
        dq, dk, dv = zip(*dqkv)
        du, dw_bd, dscale = _pool_bwd(dpool, sv["y"], sv["w_bd"], sv["scale"], "pool_bwd" + tag, after=sent)
        dh, dz, dg1 = _normproj_bwd(dh1, du, dq, dk, dv, rc, rsa, rsb, sv["w_in"], sv["h0"], sv["g1"],
                                    "normproj_bwd" + tag)
        grads[i] = dict(norm1=dg1, norm2=dg2, norm3=dg3, pool_w=_diag_blocks(dw_bd), pool_scale=dscale)
        small_sent = send(0, "small", (grads, d_final, loss)) if i == 0 else ()
        dw_in = _matmul_tn(dz, sv["hn1"], "dw_in" + tag, tm=N_IN // 2, after=small_sent)
        sent = send(i, "in", dict(w_in=dw_in))
    return dh, sent


def _pack_small(norm1, norm2, norm3, final_norm, pool_scale, pool_w, spare=None):
    spare = jnp.zeros((1, LANES), F32) if spare is None else spare
    scale_row = jnp.concatenate([pool_scale.reshape(1, 2 * POOL_WIDTH), spare,
                                 jnp.zeros((1, D_MODEL - 2 * POOL_WIDTH - LANES), F32)], axis=1)
    return jnp.concatenate([norm1, norm2, norm3, final_norm.reshape(1, D_MODEL), scale_row,
                            pool_w.reshape(32, D_MODEL)], axis=0)


def _unpack_small(a):
    return dict(norm1=a[0:2], norm2=a[2:4], norm3=a[4:6], final_norm=a[6], pool_scale=a[7, 0:2 * POOL_WIDTH].reshape(2, POOL_WIDTH),
                pool_w=a[8:40].reshape(2, 4, HEAD_DIM, HEAD_DIM))


def _chunks_cols(a, cols):
    return a.reshape(a.shape[0], N_DEV, cols).transpose(1, 0, 2)


def _chunks_rows(a, rows):
    return a.reshape(N_DEV, rows, a.shape[1])


BIG = ("w_in", "w_out", "w_up", "w_down", "w_gate", "w_ple")
SMALL = ("norm1", "norm2", "norm3", "final_norm", "pool_scale", "pool_w")
ORDER = ("norm1", "w_in", "pool_w", "pool_scale", "w_out", "norm2", "w_up", "w_down", "norm3", "w_gate", "w_ple",
         "final_norm")


def kernel(x, p, positions, norm1, w_in, pool_w, pool_scale, w_out, norm2, w_up, w_down, norm3, w_gate, w_ple, final_norm, loss_target, m_norm1, m_w_in, m_pool_w, m_pool_scale, m_w_out, m_norm2, m_w_up, m_w_down, m_norm3, m_w_gate, m_w_ple, m_final_norm, v_norm1, v_w_in, v_pool_w, v_pool_scale, v_w_out, v_norm2, v_w_up, v_w_down, v_norm3, v_w_gate, v_w_ple, v_final_norm):
    w = dict(norm1=norm1, w_in=w_in, pool_w=pool_w, pool_scale=pool_scale, w_out=w_out, norm2=norm2, w_up=w_up,
             w_down=w_down, norm3=norm3, w_gate=w_gate, w_ple=w_ple, final_norm=final_norm)
    m = dict(norm1=m_norm1, w_in=m_w_in, pool_w=m_pool_w, pool_scale=m_pool_scale, w_out=m_w_out, norm2=m_norm2,
             w_up=m_w_up, w_down=m_w_down, norm3=m_norm3, w_gate=m_w_gate, w_ple=m_w_ple, final_norm=m_final_norm)
    v = dict(norm1=v_norm1, w_in=v_w_in, pool_w=v_pool_w, pool_scale=v_pool_scale, w_out=v_w_out, norm2=v_norm2,
             w_up=v_w_up, w_down=v_w_down, norm3=v_norm3, w_gate=v_w_gate, w_ple=v_w_ple, final_norm=v_final_norm)
    seq = x.shape[1]

    bf = {n: [w[n][layer].astype(BF16) for layer in range(2)] for n in BIG}
    bf["w_in"] = [a.T for a in bf["w_in"]]
    me = 4 * lax.axis_index("x") + 2 * lax.axis_index("y") + lax.axis_index("c")
    parts = dict(zip(("in", "out", "mlp", "gate"), (("w_in",), ("w_out",), ("w_up", "w_down"), ("w_gate", "w_ple"))))
    gathers = [_Exchange("gather_l0", [[bf[n][0] for n in parts[pt]] for pt in parts], scatter=False)]
    unpack = dict(w_in=lambda a: a.reshape(N_IN, D_MODEL),
                  w_out=lambda a: a.reshape(D_MODEL, D_MODEL), w_gate=lambda a: a.reshape(D_MODEL, D_MODEL),
                  w_ple=lambda a: a.transpose(1, 0, 2).reshape(PLE_DIM, D_MODEL), w_up=lambda a: a, w_down=lambda a: a)

    def weights(layer, part, after):
        if part == "prefetch":
            if layer != 0:
                return ()
            gathers.append(_Exchange("gather_l1", [[bf[n][1] for n in parts[pt]] for pt in parts], scatter=False,
                                     after=after))
            return (gathers[1].token,)
        shards, lands = gathers[layer].wait(tuple(parts).index(part), after)
        full = [unpack[n](lax.dynamic_update_slice_in_dim(land, shard[None], me, axis=0))
                for n, shard, land in zip(parts[part], shards, lands)]
        return full if len(full) > 1 else full[0]

    to_chunks = dict(w_in=lambda a: _chunks_rows(a, N_IN // N_DEV),
                     w_out=lambda a: _chunks_rows(a, D_MODEL // N_DEV),
                     w_up=lambda a: a, w_down=lambda a: _chunks_rows(a, FF_BLOCK),
                     w_gate=lambda a: _chunks_rows(a, D_MODEL // N_DEV), w_ple=lambda a: _chunks_cols(a, D_MODEL // N_DEV))
    own = {n: [None, None] for n in BIG}
    scatters = {}

    def send(layer, part, grads):
        if part == "small":
            per_layer, d_final, loss = grads
            pack = _pack_small(
                *[jnp.concatenate([per_layer[0][n], per_layer[1][n]], axis=0) for n in ("norm1", "norm2", "norm3")],
                d_final.reshape(D_MODEL),
                jnp.concatenate([per_layer[0]["pool_scale"], per_layer[1]["pool_scale"]], axis=0),
                jnp.stack([per_layer[0]["pool_w"], per_layer[1]["pool_w"]]), spare=loss)
            scatters["small"] = _Exchange("gather_small", [[pack]], scatter=False)
            return (scatters["small"].token,)
        for n, (g32, _) in grads.items():
            own[n][layer] = to_chunks[n](g32)
        ex = _Exchange(f"scatter_{part}_l{layer}", [[to_chunks[n](g16) for n, (_, g16) in grads.items()]], scatter=True)
        scatters[layer, part] = (tuple(grads), ex)
        return (ex.token,)

    dx, sent = _local_step(
        x.reshape(seq, D_MODEL), p.reshape(2, seq, PLE_DIM), positions.reshape(seq), loss_target.reshape(seq, D_MODEL),
        norm1, pool_w, pool_scale, norm2, norm3, final_norm, weights, send)

    g_out, d_out, m_out, v_out = {}, {}, {}, {}
    my_index = me.reshape(1)
    for part in ("main", "in"):
        recv = {}
        for layer in (1, 0):
            names, ex = scatters[layer, part]
            for n, r in zip(names, ex.wait(0, sent)[1]):
                recv[n, layer] = r
        for n in names:
            grad = (*own[n], recv[n, 0], recv[n, 1])
            if n == "w_in":
                grad = _sum_chunks(grad, my_index, "sum_w_in").transpose(0, 2, 1)
            g_out[n], d_out[n], m_out[n], v_out[n] = _adamw_sharded(w[n], m[n], v[n], grad, my_index, "adamw_" + n)
        sent = tuple(d_out[n] for n in names)
    (mine,), (landed,) = scatters["small"].wait(0, sent)
    small_g8 = lax.dynamic_update_slice_in_dim(landed, mine[None], me, axis=0)
    pack = lambda t: _pack_small(*[t[n] for n in SMALL])
    small_g, d_small, m_small, v_small = _adamw_packed(pack(w), small_g8, pack(m), pack(v), "adamw_small")
    for dst, a in ((g_out, small_g), (d_out, d_small), (m_out, m_small), (v_out, v_small)):
        dst.update(_unpack_small(a))

    return (small_g[7, 2 * POOL_WIDTH],dx.reshape(1, seq, D_MODEL), *[g_out[n] for n in ORDER], *[d_out[n] for n in ORDER],
            *[m_out[n] for n in ORDER], *[v_out[n] for n in ORDER])
```

```python
import functools

import jax
import jax.numpy as jnp
from jax import lax
from jax.experimental import pallas as pl
from jax.experimental.pallas import tpu as pltpu

F32 = jnp.float32
BF16 = jnp.bfloat16

D_MODEL = 1024
HEAD_DIM = 64
POOL_WIDTH = 256
POOL_WINDOWS = (2, 4, 8, 16)
POOL_HALO = 16
POOL_PAD = 8
GROUP_WIDTH = 256
DILATIONS = (1, 4, 16)
ATTN_BLOCK = 128
ROT_SHIFT = 8
ROPE_THETA = 500000.0
D_FF = 4096
FF_BLOCK = 512
FF_PER_STEP = 2
MLP_BWD_TILE = 512
FWD_TILE = 1024
N_DEV = 8
N_IN = POOL_WIDTH + 3 * 768
PLE_DIM = 256
EPS = 1e-6
NEG_BIG = -1e30

ADAM_LR = 0.001
ADAM_B1 = 0.9
ADAM_B2 = 0.999
ADAM_EPS = 1e-08
ADAM_WD = 0.01
ADAM_STEP = 10

LANES = 128
VMEM_LIMIT = 56 * 1024 * 1024
MESH = pl.DeviceIdType.MESH


def _params(n_grid):
    return pltpu.CompilerParams(dimension_semantics=("arbitrary",) * n_grid, vmem_limit_bytes=VMEM_LIMIT)


def _dot(a, b):
    return jnp.dot(a, b, preferred_element_type=F32)


def _dot_nt(a, b):
    return lax.dot_general(a, b, (((1,), (1,)), ((), ())), preferred_element_type=F32)


def _dot_tn(a, b):
    return lax.dot_general(a, b, (((0,), (0,)), ((), ())), preferred_element_type=F32)


def _rms(x, g):
    rstd = lax.rsqrt(jnp.mean(x * x, axis=-1, keepdims=True) + EPS)
    n = x * rstd
    return n, rstd, n * g


def _rms_bwd(dy, n, rstd, g):
    dyn = dy * g
    dx = rstd * (dyn - n * jnp.mean(dyn * n, axis=-1, keepdims=True))
    return dx, jnp.sum(dy * n, axis=0, keepdims=True)


def _ordered_after(body, n_in, after):
    if not after:
        return body
    return lambda *refs: body(*refs[:n_in], *refs[n_in + len(after):])


def _resident(shape):
    return pl.BlockSpec(shape, lambda i: (0,) * len(shape), pipeline_mode=pl.Buffered(1))


def _row_tile(s, t):
    t = min(s, t)
    assert s % t == 0
    return t


def _rot(z, c, sa, sb):
    return z * c + pltpu.roll(z, ROT_SHIFT, 1) * sa + pltpu.roll(z, LANES - ROT_SHIFT, 1) * sb


def _table_specs(t):
    return [pl.BlockSpec((t, LANES), functools.partial(lambda i, k: (i, k), k=k)) for k in range(3)]


def _rot_t(dz, c, sa, sb):
    return dz * c + pltpu.roll(dz * sa, LANES - ROT_SHIFT, 1) + pltpu.roll(dz * sb, ROT_SHIFT, 1)


def _to_residues(value, stage, out_ref, dil):
    if dil == 1:
        out_ref[0] = value.astype(out_ref.dtype)
        return
    rows = value.shape[0] // dil
    for hf in range(GROUP_WIDTH // LANES):
        lanes = slice(hf * LANES, (hf + 1) * LANES)
        stage[hf][...] = value[:, lanes]
        for r in range(dil):
            out_ref[r, :, lanes] = stage[hf][pl.ds(r, rows, stride=dil), :].astype(out_ref.dtype)


def _from_residues(in_ref, stage, dil):
    if dil == 1:
        return in_ref[0].astype(F32)
    rows = in_ref.shape[1]
    for hf in range(GROUP_WIDTH // LANES):
        for r in range(dil):
            stage[hf][pl.ds(r, rows, stride=dil), :] = in_ref[r, :, hf * LANES:(hf + 1) * LANES].astype(F32)
    return jnp.concatenate([stage[0][...], stage[1][...]], axis=1)


def _residue_spec(dil, t):
    return pl.BlockSpec((dil, t // dil, GROUP_WIDTH), lambda i: (0, i, 0))


def _residue_shape(dil, s, dtype):
    return jax.ShapeDtypeStruct((dil, s // dil, GROUP_WIDTH), dtype)


def _stages(t, n):
    return [pltpu.VMEM((t, LANES), F32)] * (n * (GROUP_WIDTH // LANES))


def _pair_stages(refs):
    return [refs[i:i + 2] for i in range(0, len(refs), 2)]


def _normproj_fwd(h, g, w_in, rc, rsa, rsb, name):
    s = h.shape[0]
    t = _row_tile(s, FWD_TILE)

    def body(h_ref, g_ref, w_ref, c_ref, sa_ref, sb_ref, hn_ref, u_ref, *rest):
        qkv_refs, stages = rest[:9], _pair_stages(rest[9:])
        _, _, hn = _rms(h_ref[...], g_ref[...])
        hb = hn.astype(BF16)
        hn_ref[...] = hb
        c, sa, sb = c_ref[...], sa_ref[...], sb_ref[...]

        def rot(z, scale):
            halves = [_rot(z[:, hf * LANES:(hf + 1) * LANES], c, sa, sb) * scale for hf in range(2)]
            return jnp.concatenate(halves, axis=1)

        proj = lambda lo: _dot_nt(hb, w_ref[lo:lo + GROUP_WIDTH, :])
        u_ref[...] = proj(0)
        for grp, dil in enumerate(DILATIONS):
            lo = POOL_WIDTH + grp * GROUP_WIDTH
            q_ref, k_ref, v_ref = qkv_refs[3 * grp:3 * grp + 3]
            _to_residues(rot(proj(lo), HEAD_DIM ** -0.5), stages[0], q_ref, dil)
            _to_residues(rot(proj(lo + 768), 1.0), stages[1], k_ref, dil)
            _to_residues(proj(lo + 1536), stages[2], v_ref, dil)

    row = lambda w: pl.BlockSpec((t, w), lambda i: (i, 0))
    return pl.pallas_call(
        body, name=name, grid=(s // t,),
        in_specs=[row(D_MODEL), pl.BlockSpec((1, D_MODEL), lambda i: (0, 0)),
                  _resident((N_IN, D_MODEL))] + _table_specs(t),
        out_specs=[row(D_MODEL), row(POOL_WIDTH)] + [_residue_spec(dil, t) for dil in DILATIONS for _ in range(3)],
        out_shape=[jax.ShapeDtypeStruct((s, D_MODEL), BF16), jax.ShapeDtypeStruct((s, POOL_WIDTH), F32)]
        + [_residue_shape(dil, s, BF16) for dil in DILATIONS for _ in range(3)],
        scratch_shapes=_stages(t, 3),
        compiler_params=_params(1),
    )(h, g, w_in, rc, rsa, rsb)


def _pool_lane_window():
    lane = lax.broadcasted_iota(jnp.int32, (1, POOL_WIDTH), 1)
    return jnp.left_shift(2, lane // (POOL_WIDTH // len(POOL_WINDOWS)))


def _window_sums(ext, b2, b4, b8, t, lo, tile, direction):
    rows = t + POOL_HALO
    for src, dst, sh in ((ext, b2, 1), (b2, b4, 2), (b4, b8, 4)):
        dst[lo:lo + rows, :] = src[lo:lo + rows, :] + src[lo + direction * sh:lo + direction * sh + rows, :]
    s16 = b8[tile:tile + t, :] + b8[tile + direction * 8:tile + direction * 8 + t, :]
    win = _pool_lane_window()
    return jnp.where(win == 2, b2[tile:tile + t, :],
                     jnp.where(win == 4, b4[tile:tile + t, :], jnp.where(win == 8, b8[tile:tile + t, :], s16)))


def _pool_fwd(u, w_bd, scale, name, after=()):
    s = u.shape[0]
    t = _row_tile(s, 512)
    first = POOL_PAD + POOL_HALO

    def body(u_ref, w_ref, sc_ref, out_ref, y_ref, ext, b2, b4, b8):
        i = pl.program_id(0)

        @pl.when(i == 0)
        def _():
            for buf in (ext, b2, b4):
                buf[0:POOL_PAD, :] = jnp.zeros((POOL_PAD, POOL_WIDTH), F32)
            ext[POOL_PAD:first, :] = jnp.zeros((POOL_HALO, POOL_WIDTH), F32)

        x = u_ref[...]
        ext[first:, :] = x
        wsum = _window_sums(ext, b2, b4, b8, t, POOL_PAD, first, -1)
        pos = i * t + lax.broadcasted_iota(jnp.int32, (t, POOL_WIDTH), 0)
        cnt = jnp.minimum(pos + 1, _pool_lane_window()).astype(F32)
        y = wsum / cnt - x
        yb = y.astype(BF16)
        y_ref[...] = yb
        out_ref[...] = _dot(yb, w_ref[...]) * sc_ref[...]
        ext[POOL_PAD:first, :] = x[t - POOL_HALO:, :]

    row = pl.BlockSpec((t, POOL_WIDTH), lambda i: (i, 0))
    return pl.pallas_call(
        _ordered_after(body, 3, after), name=name, grid=(s // t,),
        in_specs=[row, pl.BlockSpec((POOL_WIDTH, POOL_WIDTH), lambda i: (0, 0)),
                  pl.BlockSpec((1, POOL_WIDTH), lambda i: (0, 0))] + [pl.BlockSpec(memory_space=pl.ANY)] * len(after),
        out_specs=[row, row],
        out_shape=[jax.ShapeDtypeStruct((s, POOL_WIDTH), F32), jax.ShapeDtypeStruct((s, POOL_WIDTH), BF16)],
        scratch_shapes=[pltpu.VMEM((t + POOL_HALO + POOL_PAD, POOL_WIDTH), F32)] * 4,
        compiler_params=_params(1),
    )(u, w_bd, scale, *after)


def _head_masks():
    lane = lax.broadcasted_iota(jnp.int32, (ATTN_BLOCK, GROUP_WIDTH), 1)
    return [lane // HEAD_DIM == hd for hd in range(GROUP_WIDTH // HEAD_DIM)]


def _stack_heads(a, masks):
    zero = jnp.zeros_like(a)
    return jnp.concatenate([jnp.where(m, a, zero) for m in masks], axis=0)


def _band_bias(first_step):
    rows = ATTN_BLOCK * (GROUP_WIDTH // HEAD_DIM)
    i = lax.broadcasted_iota(jnp.int32, (rows, 2 * ATTN_BLOCK), 0) & (ATTN_BLOCK - 1)
    j = lax.broadcasted_iota(jnp.int32, (rows, 2 * ATTN_BLOCK), 1)
    inner = jnp.where((j >= i) & (j <= i + ATTN_BLOCK), 0.0, NEG_BIG)
    return jnp.where((j < ATTN_BLOCK) & first_step, NEG_BIG, inner), inner


def _column_per_head(a):
    return jnp.concatenate([a[:, hd * HEAD_DIM:hd * HEAD_DIM + 1] for hd in range(GROUP_WIDTH // HEAD_DIM)], axis=0)


def _blocks_per_step(nb):
    return 8 if nb % 8 == 0 else 4 if nb % 4 == 0 else 2 if nb % 2 == 0 else 1


def _residues_per_step(dil, nb, qb):
    return 2 if (nb == qb and qb < 8 and dil % 2 == 0) else 1


def _attn_fwd(q, k, v, name, after=()):
    dil, length, _ = q.shape
    nb = length // ATTN_BLOCK
    qb = _blocks_per_step(nb)
    rs = _residues_per_step(dil, nb, qb)

    def body(q_ref, kp_ref, kc_ref, vp_ref, vc_ref, o_ref, lse_ref):
        masks = _head_masks()
        bias = _band_bias(pl.program_id(1) == 0)
        for rr in range(rs):
            for qi in range(qb):
                here = slice(qi * ATTN_BLOCK, (qi + 1) * ATTN_BLOCK)
                before = slice((qi - 1) * ATTN_BLOCK, qi * ATTN_BLOCK)
                kcat = jnp.concatenate([kp_ref[rr] if qi == 0 else kc_ref[rr, before], kc_ref[rr, here]], axis=0)
                vcat = jnp.concatenate([vp_ref[rr] if qi == 0 else vc_ref[rr, before], vc_ref[rr, here]], axis=0)
                qs = _stack_heads(q_ref[rr, here], masks)
                sc = _dot_nt(qs, kcat) + bias[min(qi, 1)]
                m = jnp.max(sc, axis=1, keepdims=True)
                e = jnp.exp(sc - m)
                l = jnp.sum(e, axis=1, keepdims=True)
                p = (e / l).astype(BF16)
                lse = m + jnp.log(l)
                o = jnp.zeros((ATTN_BLOCK, GROUP_WIDTH), F32)
                lse_full = jnp.zeros((ATTN_BLOCK, GROUP_WIDTH), F32)
                for hd, msk in enumerate(masks):
                    rows = slice(hd * ATTN_BLOCK, (hd + 1) * ATTN_BLOCK)
                    o = jnp.where(msk, _dot(p[rows], vcat), o)
                    lse_full = jnp.where(msk, lse[rows], lse_full)
                o_ref[rr, here] = o.astype(o_ref.dtype)
                lse_ref[rr, here] = lse_full

    cur = pl.BlockSpec((rs, qb * ATTN_BLOCK, GROUP_WIDTH), lambda r, j: (r, j, 0))
    prev = pl.BlockSpec((rs, ATTN_BLOCK, GROUP_WIDTH), lambda r, j: (r, jnp.maximum(qb * j - 1, 0), 0))
    return pl.pallas_call(
        _ordered_after(body, 5, after), name=name, grid=(dil // rs, nb // qb),
        in_specs=[cur, prev, cur, prev, cur] + [pl.BlockSpec(memory_space=pl.ANY)] * len(after), out_specs=[cur, cur],
        out_shape=[jax.ShapeDtypeStruct(q.shape, BF16), jax.ShapeDtypeStruct(q.shape, F32)],
        compiler_params=_params(2),
    )(q, k, k, v, v, *after)


def _group_weights(l0, l1, l2):
    m = jnp.maximum(jnp.maximum(l0, l1), l2)
    e0, e1, e2 = jnp.exp(l0 - m), jnp.exp(l1 - m), jnp.exp(l2 - m)
    den = e0 + e1 + e2
    return e0 / den, e1 / den, e2 / den


def _outproj_fwd(h, pool_out, o, lse, w_out, name):
    s = h.shape[0]
    t = _row_tile(s, FWD_TILE)

    def body(h_ref, po_ref, o0, o1, o2, l0, l1, l2, w_ref, out_ref, a_ref, *stages):
        stages = _pair_stages(stages)
        ov =[_from_residues(r, stages[i], DILATIONS[i]) for i, r in enumerate((o0, o1, o2))]
        lv = [_from_residues(r, stages[3 + i], DILATIONS[i]) for i, r in enumerate((l0, l1, l2))]
        wts = _group_weights(*lv)
        a = jnp.concatenate([po_ref[...]] + [ov[i] * wts[i] for i in range(3)], axis=1).astype(BF16)
        a_ref[...] = a
        out_ref[...] = h_ref[...] + _dot(a, w_ref[...])

    row = lambda w: pl.BlockSpec((t, w), lambda i: (i, 0))
    res = [_residue_spec(dil, t) for dil in DILATIONS]
    return pl.pallas_call(
        body, name=name, grid=(s // t,),
        in_specs=[row(D_MODEL), row(POOL_WIDTH)] + res + res + [_resident((D_MODEL, D_MODEL))],
        out_specs=[row(D_MODEL), row(D_MODEL)],
        out_shape=[jax.ShapeDtypeStruct((s, D_MODEL), F32), jax.ShapeDtypeStruct((s, D_MODEL), BF16)],
        scratch_shapes=_stages(t, 6),
        compiler_params=_params(1),
    )(h, pool_out, *o, *lse, w_out)


def _mlp_fwd(h, g, w_up, w_down, name):
    s = h.shape[0]
    t = _row_tile(s, 512)
    nblk = D_FF // FF_BLOCK

    def body(h_ref, g_ref, wu_ref, wd_ref, out_ref, hn_ref, r_ref):
        x = h_ref[...]
        _, _, hn = _rms(x, g_ref[...])
        hb = hn.astype(BF16)
        hn_ref[...] = hb
        acc = None
        for b0 in range(0, nblk, FF_PER_STEP):
            acts = []
            for b in range(b0, b0 + FF_PER_STEP):
                r = jnp.maximum(_dot(hb, wu_ref[b]), 0.0)
                r_ref[:, b * FF_BLOCK:(b + 1) * FF_BLOCK] = r.astype(BF16)
                acts.append((r * r).astype(BF16))
            wd = wd_ref[b0:b0 + FF_PER_STEP].reshape(FF_PER_STEP * FF_BLOCK, D_MODEL)
            part = _dot(jnp.concatenate(acts, axis=1), wd)
            acc = part if acc is None else acc + part
        out_ref[...] = x + acc

    row = lambda w: pl.BlockSpec((t, w), lambda i: (i, 0))
    resident = lambda shape: pl.BlockSpec(shape, lambda i: (0, 0, 0), pipeline_mode=pl.Buffered(1))
    return pl.pallas_call(
        body, name=name, grid=(s // t,),
        in_specs=[row(D_MODEL), pl.BlockSpec((1, D_MODEL), lambda i: (0, 0)),
                  resident((nblk, D_MODEL, FF_BLOCK)), resident((nblk, FF_BLOCK, D_MODEL))],
        out_specs=[row(D_MODEL), row(D_MODEL), row(D_FF)],
        out_shape=[jax.ShapeDtypeStruct((s, D_MODEL), F32), jax.ShapeDtypeStruct((s, D_MODEL), BF16),
                   jax.ShapeDtypeStruct((s, D_FF), BF16)],
        compiler_params=_params(1),
    )(h, g, w_up, w_down)


def _gate_fwd(h, g, w_gate, p, layer, w_ple, name, head=None):
    s = h.shape[0]
    t = _row_tile(s, FWD_TILE)

    def body(h_ref, g_ref, wg_ref, p_ref, wp_ref, *refs):
        x = h_ref[...]
        _, _, hn = _rms(x, g_ref[...])
        hb = hn.astype(BF16)
        gate = 1.0 / (1.0 + jnp.exp(-_dot(hb, wg_ref[...])))
        pb = p_ref[...].astype(BF16)
        h3 = x + gate * _dot(pb, wp_ref[...])
        if head is None:
            out_ref, hn_ref, gate_ref, pb_ref = refs
            out_ref[...] = h3
        else:
            gf_ref, t_ref, hn_ref, gate_ref, pb_ref, loss_ref, dh_ref, dgf_ref = refs

            @pl.when(pl.program_id(0) == 0)
            def _():
                loss_ref[...] = jnp.zeros_like(loss_ref)
                dgf_ref[...] = jnp.zeros_like(dgf_ref)

            gf = gf_ref[...]
            n, rstd, y = _rms(h3, gf)
            err = y - t_ref[...]
            loss_ref[...] += jnp.sum(err * err) * (0.5 / D_MODEL)
            dh_ref[...], dgf = _rms_bwd(err * (1.0 / D_MODEL), n, rstd, gf)
            dgf_ref[...] += dgf
        hn_ref[...] = hb
        pb_ref[...] = pb
        gate_ref[...] = gate.astype(BF16)

    row = lambda w: pl.BlockSpec((t, w), lambda i: (i, 0))
    full = lambda a, b: pl.BlockSpec((a, b), lambda i: (0, 0))
    in_specs = [row(D_MODEL), full(1, D_MODEL), _resident((D_MODEL, D_MODEL)),
                pl.BlockSpec((None, t, PLE_DIM), lambda i: (layer, i, 0)), _resident((PLE_DIM, D_MODEL))]
    saved_specs = [row(D_MODEL), row(D_MODEL), row(PLE_DIM)]
    saved_shapes = [jax.ShapeDtypeStruct((s, D_MODEL), BF16), jax.ShapeDtypeStruct((s, D_MODEL), BF16),
                    jax.ShapeDtypeStruct((s, PLE_DIM), BF16)]
    if head is None:
        return pl.pallas_call(
            body, name=name, grid=(s // t,), in_specs=in_specs, out_specs=[row(D_MODEL)] + saved_specs,
            out_shape=[jax.ShapeDtypeStruct((s, D_MODEL), F32)] + saved_shapes, compiler_params=_params(1),
        )(h, g, w_gate, p, w_ple)
    return pl.pallas_call(
        body, name=name, grid=(s // t,), in_specs=in_specs + [full(1, D_MODEL), row(D_MODEL)],
        out_specs=saved_specs + [pl.BlockSpec((1, LANES), lambda i: (0, 0)), row(D_MODEL), full(1, D_MODEL)],
        out_shape=saved_shapes + [jax.ShapeDtypeStruct((1, LANES), F32), jax.ShapeDtypeStruct((s, D_MODEL), F32),
                                  jax.ShapeDtypeStruct((1, D_MODEL), F32)],
        compiler_params=_params(1),
    )(h, g, w_gate, p, w_ple, *head)


def _gate_bwd(dh, gate, pb, w_ple, h, g, w_gate, hn, name, after=()):
    s = h.shape[0]
    t = _row_tile(s, 512)
    last = s // t - 1

    def body(dh_ref, gate_ref, pb_ref, wp_ref, h_ref, g_ref, wg_ref, hn_ref, out_ref, dg_ref, dwg_ref, dwgb_ref,
             dwp_ref, dwpb_ref):
        i = pl.program_id(0)

        @pl.when(i == 0)
        def _():
            dg_ref[...] = jnp.zeros_like(dg_ref)
            dwg_ref[...] = jnp.zeros_like(dwg_ref)
            dwp_ref[...] = jnp.zeros_like(dwp_ref)

        d = dh_ref[...]
        gate = gate_ref[...].astype(F32)
        pb = pb_ref[...]
        e = _dot(pb, wp_ref[...])
        dgl = (d * e * gate * (1.0 - gate)).astype(BF16)
        dwg_ref[...] += _dot_tn(hn_ref[...], dgl)
        dwp_ref[...] += _dot_tn(pb, (d * gate).astype(BF16))
        gv = g_ref[...]
        n, rstd, _ = _rms(h_ref[...], gv)
        dx, dg = _rms_bwd(_dot_nt(dgl, wg_ref[...]), n, rstd, gv)
        out_ref[...] = d + dx
        dg_ref[...] += dg

        @pl.when(i == last)
        def _():
            dwgb_ref[...] = dwg_ref[...].astype(BF16)
            dwpb_ref[...] = dwp_ref[...].astype(BF16)

    row = lambda w: pl.BlockSpec((t, w), lambda i: (i, 0))
    full = lambda a, b: pl.BlockSpec((a, b), lambda i: (0, 0))
    dh2, dg, dwg, dwgb, dwp, dwpb = pl.pallas_call(
        _ordered_after(body, 8, after), name=name, grid=(s // t,),
        in_specs=[row(D_MODEL), row(D_MODEL), row(PLE_DIM), full(PLE_DIM, D_MODEL), row(D_MODEL), full(1, D_MODEL),
                  full(D_MODEL, D_MODEL), row(D_MODEL)] + [pl.BlockSpec(memory_space=pl.ANY)] * len(after),
        out_specs=[row(D_MODEL), full(1, D_MODEL), full(D_MODEL, D_MODEL), full(D_MODEL, D_MODEL),
                   full(PLE_DIM, D_MODEL), full(PLE_DIM, D_MODEL)],
        out_shape=[jax.ShapeDtypeStruct((s, D_MODEL), F32), jax.ShapeDtypeStruct((1, D_MODEL), F32),
                   jax.ShapeDtypeStruct((D_MODEL, D_MODEL), F32), jax.ShapeDtypeStruct((D_MODEL, D_MODEL), BF16),
                   jax.ShapeDtypeStruct((PLE_DIM, D_MODEL), F32), jax.ShapeDtypeStruct((PLE_DIM, D_MODEL), BF16)],
        compiler_params=_params(1),
    )(dh, gate, pb, w_ple, h, g, w_gate, hn, *after)
    return dh2, dg, (dwg, dwgb), (dwp, dwpb)


def _mlp_bwd(dh, r, h, g, w_up, w_down, name):
    s = h.shape[0]
    t = _row_tile(s, MLP_BWD_TILE)
    nblk = D_FF // FF_BLOCK

    def body(dh_ref, r_ref, h_ref, g_ref, wu_ref, wd_ref, out_ref, dup_ref, dg_ref, dhb_ref):
        @pl.when(pl.program_id(0) == 0)
        def _():
            dg_ref[...] = jnp.zeros_like(dg_ref)

        d = dh_ref[...]
        db = d.astype(BF16)
        dhb_ref[...] = db
        back = None
        for b in range(nblk):
            cols = slice(b * FF_BLOCK, (b + 1) * FF_BLOCK)
            dup = (_dot_nt(db, wd_ref[b]) * (2.0 * r_ref[:, cols].astype(F32))).astype(BF16)
            dup_ref[:, cols] = dup
            part = _dot_nt(dup, wu_ref[b])
            back = part if back is None else back + part
        gv = g_ref[...]
        n, rstd, _ = _rms(h_ref[...], gv)
        dx, dg = _rms_bwd(back, n, rstd, gv)
        out_ref[...] = d + dx
        dg_ref[...] += dg

    row = lambda w: pl.BlockSpec((t, w), lambda i: (i, 0))
    vec = pl.BlockSpec((1, D_MODEL), lambda i: (0, 0))
    resident = lambda shape: pl.BlockSpec(shape, lambda i: (0, 0, 0), pipeline_mode=pl.Buffered(1))
    return pl.pallas_call(
        body, name=name, grid=(s // t,),
        in_specs=[row(D_MODEL), row(D_FF), row(D_MODEL), vec,
                  resident((nblk, D_MODEL, FF_BLOCK)), resident((nblk, FF_BLOCK, D_MODEL))],
        out_specs=[row(D_MODEL), row(D_FF), vec, row(D_MODEL)],
        out_shape=[jax.ShapeDtypeStruct((s, D_MODEL), F32), jax.ShapeDtypeStruct((s, D_FF), BF16),
                   jax.ShapeDtypeStruct((1, D_MODEL), F32), jax.ShapeDtypeStruct((s, D_MODEL), BF16)],
        compiler_params=_params(1),
    )(dh, r, h, g, w_up, w_down)


def _outproj_bwd(dh, w_out, o, lse, ones_bd, a, name):
    s = dh.shape[0]
    t = _row_tile(s, 512)
    last = s // t - 1

    def body(dh_ref, w_ref, o0, o1, o2, l0, l1, l2, bd_ref, a_ref, dp_ref, do0, do1, do2, de0, de1, de2, dw_ref,
             dwb_ref, *stages):
        i = pl.program_id(0)

        @pl.when(i == 0)
        def _():
            dw_ref[...] = jnp.zeros_like(dw_ref)

        stages = _pair_stages(stages)
        dhb = dh_ref[...].astype(BF16)
        dw_ref[...] += _dot_tn(a_ref[...], dhb)

        @pl.when(i == last)
        def _():
            dwb_ref[...] = dw_ref[...].astype(BF16)

        da = _dot_nt(dhb, w_ref[...])
        dp_ref[...] = da[:, 0:POOL_WIDTH]
        ov =[_from_residues(r, stages[i], DILATIONS[i]) for i, r in enumerate((o0, o1, o2))]
        lv = [_from_residues(r, stages[3 + i], DILATIONS[i]) for i, r in enumerate((l0, l1, l2))]
        wts = _group_weights(*lv)
        bd = bd_ref[...]
        cbar = jnp.zeros((t, GROUP_WIDTH), F32)
        for grp, do_ref in enumerate((do0, do1, do2)):
            lo = POOL_WIDTH + grp * GROUP_WIDTH
            dag = da[:, lo:lo + GROUP_WIDTH]
            _to_residues(dag * wts[grp], stages[6 + grp], do_ref, DILATIONS[grp])
            prod = dag * ov[grp]
            hi = prod.astype(BF16)
            low = (prod - hi.astype(F32)).astype(BF16)
            cbar = cbar + wts[grp] * (_dot(hi, bd) + _dot(low, bd))
        for grp, de_ref in enumerate((de0, de1, de2)):
            _to_residues(wts[grp] * cbar, stages[9 + grp], de_ref, DILATIONS[grp])

    row = lambda w: pl.BlockSpec((t, w), lambda i: (i, 0))
    full = lambda a, b: pl.BlockSpec((a, b), lambda i: (0, 0))
    res = [_residue_spec(dil, t) for dil in DILATIONS]
    *outs, dw, dwb = pl.pallas_call(
        body, name=name, grid=(s // t,),
        in_specs=[row(D_MODEL), full(D_MODEL, D_MODEL)] + res + res + [full(GROUP_WIDTH, GROUP_WIDTH), row(D_MODEL)],
        out_specs=[row(POOL_WIDTH)] + res + res + [full(D_MODEL, D_MODEL)] * 2,
        out_shape=[jax.ShapeDtypeStruct((s, POOL_WIDTH), F32)] + [_residue_shape(dil, s, BF16) for dil in DILATIONS]
        + [_residue_shape(dil, s, F32) for dil in DILATIONS]
        + [jax.ShapeDtypeStruct((D_MODEL, D_MODEL), F32), jax.ShapeDtypeStruct((D_MODEL, D_MODEL), BF16)],
        scratch_shapes=_stages(t, 12),
        compiler_params=_params(1),
    )(dh, w_out, *o, *lse, ones_bd, a)
    return (*outs, (dw, dwb))


def _attn_bwd(q, k, v, do, lse, deff, name, after=()):
    dil, length, _ = q.shape
    nb = length // ATTN_BLOCK
    qb = _blocks_per_step(nb)
    nj = nb // qb
    rs = _residues_per_step(dil, nb, qb)
    whole = nj == 1
    tail = slice((qb - 1) * ATTN_BLOCK, qb * ATTN_BLOCK)
    block = lambda qi: slice(qi * ATTN_BLOCK, (qi + 1) * ATTN_BLOCK)

    def body(q_ref, kp_ref, kc_ref, vp_ref, vc_ref, do_ref, lse_ref, de_ref, dq_ref, dk_ref, dv_ref, ck, cv):
        j = pl.program_id(1)

        def compute():
            masks = _head_masks()
            bias = _band_bias(j == 0)
            for rr in range(rs):
                dkc, dvc = [], []
                for qi in range(qb):
                    here, before = block(qi), block(qi - 1)
                    kcat = jnp.concatenate([kp_ref[rr] if qi == 0 else kc_ref[rr, before], kc_ref[rr, here]], axis=0)
                    vcat = jnp.concatenate([vp_ref[rr] if qi == 0 else vc_ref[rr, before], vc_ref[rr, here]], axis=0)
                    qs = _stack_heads(q_ref[rr, here], masks)
                    dos = _stack_heads(do_ref[rr, here], masks)
                    sc = _dot_nt(qs, kcat) + bias[min(qi, 1)]
                    p = jnp.exp(sc - _column_per_head(lse_ref[rr, here]))
                    ds = (p * (_dot_nt(dos, vcat) - _column_per_head(de_ref[rr, here]))).astype(BF16)
                    dq = jnp.zeros((ATTN_BLOCK, GROUP_WIDTH), F32)
                    for hd, msk in enumerate(masks):
                        dq = jnp.where(msk, _dot(ds[block(hd)], kcat), dq)
                    dq_ref[rr, here] = dq.astype(dq_ref.dtype)
                    dkc.append(_dot_tn(ds, qs))
                    dvc.append(_dot_tn(p.astype(BF16), dos))

                for out_ref, carry, parts in ((dk_ref, ck, dkc), (dv_ref, cv, dvc)):
                    full = [parts[qi][ATTN_BLOCK:] + parts[qi + 1][0:ATTN_BLOCK] for qi in range(qb - 1)]
                    if whole:
                        for qi, val in enumerate(full + [parts[qb - 1][ATTN_BLOCK:]]):
                            out_ref[rr, block(qi)] = val.astype(out_ref.dtype)
                        continue

                    @pl.when(j > 0)
                    def _():
                        if qb > 1:
                            out_ref[0, 0:(qb - 1) * ATTN_BLOCK] = carry[0:(qb - 1) * ATTN_BLOCK].astype(out_ref.dtype)
                        out_ref[0, tail] = (carry[tail] + parts[0][0:ATTN_BLOCK]).astype(out_ref.dtype)

                    for qi, val in enumerate(full):
                        carry[block(qi)] = val
                    carry[tail] = parts[qb - 1][ATTN_BLOCK:]

        if whole:
            compute()
        else:
            pl.when(j < nj)(compute)

            @pl.when(j == nj)
            def _():
                dk_ref[0] = ck[...].astype(dk_ref.dtype)
                dv_ref[0] = cv[...].astype(dv_ref.dtype)

    step = lambda j: jnp.minimum(j, nj - 1)
    cur = pl.BlockSpec((rs, qb * ATTN_BLOCK, GROUP_WIDTH), lambda r, j: (r, step(j), 0))
    prev = pl.BlockSpec((rs, ATTN_BLOCK, GROUP_WIDTH), lambda r, j: (r, jnp.maximum(qb * step(j) - 1, 0), 0))
    late = pl.BlockSpec((rs, qb * ATTN_BLOCK, GROUP_WIDTH), lambda r, j: (r, jnp.maximum(j - 1, 0), 0))
    return pl.pallas_call(
        _ordered_after(body, 8, after), name=name, grid=(dil // rs, 1 if whole else nj + 1),
        in_specs=[cur, prev, cur, prev, cur, cur, cur, cur] + [pl.BlockSpec(memory_space=pl.ANY)] * len(after),
        out_specs=[cur, cur if whole else late, cur if whole else late],
        out_shape=[jax.ShapeDtypeStruct(q.shape, BF16)] * 3,
        scratch_shapes=[pltpu.VMEM((qb * ATTN_BLOCK, GROUP_WIDTH), F32)] * 2,
        compiler_params=_params(2),
    )(q, k, k, v, v, do, lse, deff, *after)


def _pool_bwd(dpool, y, w_bd, scale, name, after=()):
    s = dpool.shape[0]
    t = _row_tile(s, 512)
    nt = s // t

    def body(dp_ref, y_ref, w_ref, sc_ref, du_ref, dw_ref, dsc_ref, ext, b2, b4, b8):
        i = pl.program_id(0)

        @pl.when(i == 0)
        def _():
            ext[t:, :] = jnp.zeros((POOL_HALO + POOL_PAD, POOL_WIDTH), F32)
            for buf in (b2, b4):
                buf[t + POOL_HALO:, :] = jnp.zeros((POOL_PAD, POOL_WIDTH), F32)
            dw_ref[...] = jnp.zeros_like(dw_ref)
            dsc_ref[...] = jnp.zeros_like(dsc_ref)

        dp = dp_ref[...]
        yb = y_ref[...]
        w = w_ref[...]
        dsc_ref[...] += jnp.sum(dp * _dot(yb, w), axis=0, keepdims=True)
        dyo = (dp * sc_ref[...]).astype(BF16)
        dw_ref[...] += _dot_tn(yb, dyo)
        dy = _dot_nt(dyo, w)
        win = _pool_lane_window()
        pos = (nt - 1 - i) * t + lax.broadcasted_iota(jnp.int32, (t, POOL_WIDTH), 0)
        gq = dy / jnp.minimum(pos + 1, win).astype(F32)
        ext[0:t, :] = gq
        du_ref[...] = _window_sums(ext, b2, b4, b8, t, 0, 0, 1) - dy
        ext[t:t + POOL_HALO, :] = gq[0:POOL_HALO, :]

    rev = pl.BlockSpec((t, POOL_WIDTH), lambda i: (nt - 1 - i, 0))
    full = lambda a, b: pl.BlockSpec((a, b), lambda i: (0, 0))
    return pl.pallas_call(
        _ordered_after(body, 4, after), name=name, grid=(nt,),
        in_specs=[rev, rev, full(POOL_WIDTH, POOL_WIDTH), full(1, POOL_WIDTH)]
        + [pl.BlockSpec(memory_space=pl.ANY)] * len(after),
        out_specs=[rev, full(POOL_WIDTH, POOL_WIDTH), full(1, POOL_WIDTH)],
        out_shape=[jax.ShapeDtypeStruct((s, POOL_WIDTH), F32), jax.ShapeDtypeStruct((POOL_WIDTH, POOL_WIDTH), F32),
                   jax.ShapeDtypeStruct((1, POOL_WIDTH), F32)],
        scratch_shapes=[pltpu.VMEM((t + POOL_HALO + POOL_PAD, POOL_WIDTH), F32)] * 4,
        compiler_params=_params(1),
    )(dpool, y, w_bd, scale, *after)


def _normproj_bwd(dh, du, dq, dk, dv, rc, rsa, rsb, w_in, h, g, name):
    s = h.shape[0]
    t = _row_tile(s, 512)

    def body(dh_ref, du_ref, q0, q1, q2, k0, k1, k2, v0, v1, v2, c_ref, sa_ref, sb_ref, w_ref, h_ref, g_ref,
             out_ref, dz_ref, dg_ref, *stages):
        @pl.when(pl.program_id(0) == 0)
        def _():
            dg_ref[...] = jnp.zeros_like(dg_ref)

        c, sa, sb = c_ref[...], sa_ref[...], sb_ref[...]

        def unrot(a, scale):
            halves = [_rot_t(a[:, hf * LANES:(hf + 1) * LANES] * scale, c, sa, sb) for hf in range(2)]
            return jnp.concatenate(halves, axis=1)

        staged = _pair_stages(stages)
        tok = lambda refs, base: [_from_residues(r, staged[base + i], DILATIONS[i]) for i, r in enumerate(refs)]
        chunks = [du_ref[...]]
        chunks += [unrot(a, HEAD_DIM ** -0.5) for a in tok((q0, q1, q2), 0)]
        chunks += [unrot(a, 1.0) for a in tok((k0, k1, k2), 3)]
        chunks += tok((v0, v1, v2), 6)
        acc = jnp.zeros((t, D_MODEL), F32)
        for ci, ch in enumerate(chunks):
            cols = slice(ci * GROUP_WIDTH, (ci + 1) * GROUP_WIDTH)
            cb = ch.astype(BF16)
            dz_ref[:, cols] = cb
            acc = acc + _dot(cb, w_ref[cols, :])
        gv = g_ref[...]
        n, rstd, _ = _rms(h_ref[...], gv)
        dx, dg = _rms_bwd(acc, n, rstd, gv)
        out_ref[...] = dh_ref[...] + dx
        dg_ref[...] += dg

    row = lambda w: pl.BlockSpec((t, w), lambda i: (i, 0))
    vec = pl.BlockSpec((1, D_MODEL), lambda i: (0, 0))
    res = [_residue_spec(dil, t) for dil in DILATIONS]
    return pl.pallas_call(
        body, name=name, grid=(s // t,),
        in_specs=[row(D_MODEL), row(POOL_WIDTH)] + res * 3 + _table_specs(t)
        + [pl.BlockSpec((N_IN, D_MODEL), lambda i: (0, 0)), row(D_MODEL), vec],
        out_specs=[row(D_MODEL), row(N_IN), vec],
        out_shape=[jax.ShapeDtypeStruct((s, D_MODEL), F32), jax.ShapeDtypeStruct((s, N_IN), BF16),
                   jax.ShapeDtypeStruct((1, D_MODEL), F32)],
        scratch_shapes=_stages(t, 9),
        compiler_params=_params(1),
    )(dh, du, *dq, *dk, *dv, rc, rsa, rsb, w_in, h, g)


def _matmul_tn(a, b, name, *, square_a=False, tm=None, tn=None, blocked_out=False, after=()):
    s, m = a.shape
    n = b.shape[1]
    tk = _row_tile(s, 2048)
    tm = tm or min(m, 1024)
    tn = tn or min(n, 1024)
    assert m % tm == 0 and n % tn == 0
    nk = s // tk
    nsub = tn // FF_BLOCK if blocked_out else 1

    def body(a_ref, b_ref, o_ref, ob_ref, acc):
        k = pl.program_id(2)

        def product():
            av = a_ref[...]
            if square_a:
                av = av.astype(F32)
                av = av * av
            return _dot_tn(av.astype(BF16), b_ref[...].astype(BF16))

        def emit(total):
            if blocked_out:
                for sub in range(nsub):
                    cols = slice(sub * FF_BLOCK, (sub + 1) * FF_BLOCK)
                    o_ref[sub] = total[:, cols]
                    ob_ref[sub] = total[:, cols].astype(BF16)
            else:
                o_ref[...] = total
                ob_ref[...] = total.astype(BF16)

        if nk == 1:
            emit(product())
            return

        @pl.when(k == 0)
        def _():
            acc[...] = product()

        @pl.when((k > 0) & (k < nk - 1))
        def _():
            acc[...] += product()

        @pl.when(k == nk - 1)
        def _():
            emit(acc[...] + product())

    if blocked_out:
        shape = (n // FF_BLOCK, m, FF_BLOCK)
        out_spec = pl.BlockSpec((nsub, tm, FF_BLOCK), lambda i, j, k: (j, i, 0))
    else:
        shape = (m, n)
        out_spec = pl.BlockSpec((tm, tn), lambda i, j, k: (i, j))
    return pl.pallas_call(
        _ordered_after(body, 2, after), name=name, grid=(m // tm, n // tn, nk),
        in_specs=[pl.BlockSpec((tk, tm), lambda i, j, k: (k, i)), pl.BlockSpec((tk, tn), lambda i, j, k: (k, j))]
        + [pl.BlockSpec(memory_space=pl.ANY)] * len(after),
        out_specs=[out_spec, out_spec],
        out_shape=[jax.ShapeDtypeStruct(shape, F32), jax.ShapeDtypeStruct(shape, BF16)],
        scratch_shapes=[pltpu.VMEM((tm, tn), F32)],
        compiler_params=_params(3),
    )(a, b, *after)


def _adamw_math(w, g, m, v):
    m = ADAM_B1 * m + (1.0 - ADAM_B1) * g
    v = ADAM_B2 * v + (1.0 - ADAM_B2) * (g * g)
    m_hat = m / (1.0 - ADAM_B1 ** ADAM_STEP)
    v_hat = v / (1.0 - ADAM_B2 ** ADAM_STEP)
    delta = -ADAM_LR * (m_hat / (jnp.sqrt(v_hat) + ADAM_EPS) + ADAM_WD * w)
    return delta, m, v


def _sum_chunks_body(own0_ref, own1_ref, r0_ref, r1_ref):
    layer0 = pl.program_id(0) == 0
    g = jnp.where(layer0, own0_ref[...], own1_ref[...])
    for k in range(N_DEV - 1):
        g = g + jnp.where(layer0, r0_ref[k], r1_ref[k]).astype(F32)
    return g


def _chunk_specs(t, cols):
    rows_of = lambda layer: (lambda l, i: jnp.where(l == layer, i, 0))
    blk = pl.BlockSpec((None, t, cols), lambda l, i, me: (l, i, 0))
    own = [pl.BlockSpec((None, t, cols), functools.partial(lambda l, i, me, pick: (me[0], pick(l, i), 0), pick=rows_of(ly)))
           for ly in range(2)]
    recv = [pl.BlockSpec((N_DEV - 1, t, cols), functools.partial(lambda l, i, me, pick: (0, pick(l, i), 0), pick=rows_of(ly)))
            for ly in range(2)]
    return blk, own + recv


def _sum_chunks(chunks, me, name):
    _, rows, cols = chunks[0].shape
    t = _row_tile(rows, 320)

    def body(me_ref, own0_ref, own1_ref, r0_ref, r1_ref, g_ref):
        g_ref[...] = _sum_chunks_body(own0_ref, own1_ref, r0_ref, r1_ref)

    blk, chunk_specs = _chunk_specs(t, cols)
    return pl.pallas_call(
        body, name=name,
        grid_spec=pltpu.PrefetchScalarGridSpec(num_scalar_prefetch=1, grid=(2, rows // t), in_specs=chunk_specs,
                                               out_specs=blk),
        out_shape=jax.ShapeDtypeStruct((2, rows, cols), F32), compiler_params=_params(2),
    )(me, *chunks)


def _adamw_sharded(w, m, v, grad, me, name):
    _, rows, cols = w.shape
    t = _row_tile(rows, 256)
    summed = not isinstance(grad, tuple)
    grad = (grad,) if summed else grad

    def body(me_ref, w_ref, m_ref, v_ref, *refs):
        g_ref, d_ref, nm_ref, nv_ref = refs[-4:]
        g = refs[0][...] if summed else _sum_chunks_body(*refs[:4])
        g_ref[...] = g
        d_ref[...], nm_ref[...], nv_ref[...] = _adamw_math(w_ref[...], g, m_ref[...], v_ref[...])

    blk, chunk_specs = _chunk_specs(t, cols)
    return pl.pallas_call(
        body, name=name,
        grid_spec=pltpu.PrefetchScalarGridSpec(
            num_scalar_prefetch=1, grid=(2, rows // t),
            in_specs=[blk, blk, blk] + ([blk] if summed else chunk_specs), out_specs=[blk] * 4),
        out_shape=[jax.ShapeDtypeStruct(w.shape, F32)] * 4,
        compiler_params=_params(2),
    )(me, w, m, v, *grad)


def _adamw_packed(w, g8, m, v, name):
    def body(w_ref, g_ref, m_ref, v_ref, go_ref, d_ref, nm_ref, nv_ref):
        g = g_ref[0]
        for dev in range(1, N_DEV):
            g = g + g_ref[dev]
        go_ref[...] = g
        d_ref[...], nm_ref[...], nv_ref[...] = _adamw_math(w_ref[...], g, m_ref[...], v_ref[...])

    return pl.pallas_call(
        body, name=name, out_shape=[jax.ShapeDtypeStruct(w.shape, F32)] * 4,
        compiler_params=pltpu.CompilerParams(vmem_limit_bytes=VMEM_LIMIT),
    )(w, g8, m, v)


def _peer(k):
    x, y, c = lax.axis_index("x"), lax.axis_index("y"), lax.axis_index("c")
    return (1 - x if k & 4 else x, 1 - y if k & 2 else y, 1 - c if k & 1 else c)


def _linear(dev):
    return 4 * dev[0] + 2 * dev[1] + dev[2]


HBM_SPEC = pl.BlockSpec(memory_space=pltpu.HBM)
SEM_SPEC = pl.BlockSpec(memory_space=pltpu.SEMAPHORE)
ANY_SPEC = pl.BlockSpec(memory_space=pl.ANY)
EFFECT = pltpu.SideEffectType.DATAFLOW_SIDE_EFFECTING


def _in_hbm(a):
    return pltpu.with_memory_space_constraint(a, pltpu.HBM)


class _Exchange:
    def __init__(self, name, groups, scatter, after=()):
        self.name, self.scatter = name, scatter
        self.sizes = sizes = [len(g) for g in groups]
        srcs = [a for g in groups for a in g]
        n, ng = len(srcs), len(groups)
        lead = (N_DEV - 1,) if scatter else (N_DEV,)
        shapes = [lead + (a.shape[1:] if scatter else a.shape) for a in srcs]
        lands = [lax.empty(sh, a.dtype) for sh, a in zip(shapes, srcs)]
        offsets = [sum(sizes[:gi]) for gi in range(ng)]
        copy = self._copy

        def body(*refs):
            src, land = refs[:n], refs[n:2 * n]
            sems = refs[2 * n + len(after):2 * n + len(after) + 2 * ng]
            token = refs[-1]
            for gi in range(ng):
                for wi in range(sizes[gi]):
                    w = offsets[gi] + wi
                    for k in range(1, N_DEV):
                        copy(src[w], land[w], sems[2 * gi], sems[2 * gi + 1], wi, k).start()
            token[...] = jnp.zeros_like(token)

        sem_shapes = [pltpu.SemaphoreType.DMA((7 * sz,)) for sz in sizes for _ in range(2)]
        outs = pl.pallas_call(
            body, name=name + "_start",
            in_specs=[HBM_SPEC] * (2 * n) + [ANY_SPEC] * len(after),
            out_specs=[SEM_SPEC] * (2 * ng) + [HBM_SPEC] * (2 * n) + [pl.BlockSpec(memory_space=pltpu.VMEM)],
            out_shape=sem_shapes + [pltpu.HBM(a.shape, a.dtype) for a in srcs + lands]
            + [jax.ShapeDtypeStruct((8, LANES), F32)],
            input_output_aliases={i: 2 * ng + i for i in range(2 * n)},
            compiler_params=pltpu.CompilerParams(has_side_effects=EFFECT),
        )(*[_in_hbm(a) for a in srcs + lands], *after)
        self.sems = [outs[2 * gi:2 * gi + 2] for gi in range(ng)]
        thru = outs[2 * ng:2 * ng + 2 * n]
        self.srcs = [thru[offsets[gi]:offsets[gi] + sizes[gi]] for gi in range(ng)]
        self.lands = [thru[n + offsets[gi]:n + offsets[gi] + sizes[gi]] for gi in range(ng)]
        self.token = outs[-1]

    def _copy(self, src, land, send_sems, recv_sems, wi, k):
        to = _peer(k)
        if self.scatter:
            src_ref, dst_ref = src.at[_linear(to)], land.at[k - 1]
        else:
            src_ref, dst_ref = src, land.at[_linear(_peer(0))]
        return pltpu.make_async_remote_copy(
            src_ref=src_ref, dst_ref=dst_ref, send_sem=send_sems.at[7 * wi + k - 1],
            recv_sem=recv_sems.at[7 * wi + k - 1], device_id=to, device_id_type=MESH)

    def wait(self, gi, after):
        n = self.sizes[gi]
        copy = self._copy

        def body(*refs):
            src, land = refs[:n], refs[n:2 * n]
            send_sems, recv_sems = refs[2 * n], refs[2 * n + 1]
            for wi in range(n):
                for k in range(1, N_DEV):
                    cp = copy(src[wi], land[wi], send_sems, recv_sems, wi, k)
                    cp.wait_send()
                    cp.wait_recv()

        arrays = list(self.srcs[gi]) + list(self.lands[gi])
        outs = pl.pallas_call(
            body, name=f"{self.name}_wait{gi}",
            in_specs=[HBM_SPEC] * (2 * n) + [SEM_SPEC, SEM_SPEC] + [ANY_SPEC] * len(after),
            out_specs=[HBM_SPEC] * (2 * n),
            out_shape=[pltpu.HBM(a.shape, a.dtype) for a in arrays],
            input_output_aliases={i: i for i in range(2 * n)},
            compiler_params=pltpu.CompilerParams(has_side_effects=EFFECT),
        )(*arrays, *self.sems[gi], *after)
        return outs[:n], outs[n:]


def _rotary_tables(positions):
    rot_dim = HEAD_DIM // 4
    inv_freq = ROPE_THETA ** (-jnp.arange(0, rot_dim, 2, dtype=F32) / rot_dim)
    ang = positions.astype(F32)[:, None] * inv_freq
    cs = jnp.concatenate([jnp.cos(ang), jnp.sin(ang)], axis=1)
    dim = jnp.arange(LANES) % HEAD_DIM
    first, second = dim < ROT_SHIFT, (dim >= ROT_SHIFT) & (dim < rot_dim)
    src = jnp.arange(2 * ROT_SHIFT)[:, None]
    angle = (dim % ROT_SHIFT)[None, :]
    c = jnp.where((first | second)[None, :] & (src == angle), 1.0, 0.0)
    sa = jnp.where(second[None, :] & (src == angle + ROT_SHIFT), 1.0, 0.0)
    sb = jnp.where(first[None, :] & (src == angle + ROT_SHIFT), -1.0, 0.0)
    spread = jnp.concatenate([c, sa, sb], axis=1).astype(F32)
    base = jnp.concatenate([jnp.where(first | second, 0.0, 1.0), jnp.zeros((2 * LANES,))]).astype(F32)[None, :]
    return jnp.dot(cs, spread, precision=lax.Precision.HIGHEST, preferred_element_type=F32) + base


def _block_diag(pool_w):
    gc = pool_w.shape[-1]
    out = jnp.zeros((POOL_WIDTH, POOL_WIDTH), pool_w.dtype)
    for grp in range(pool_w.shape[0]):
        out = lax.dynamic_update_slice(out, pool_w[grp], (grp * gc, grp * gc))
    return out


def _diag_blocks(a):
    gc = POOL_WIDTH // len(POOL_WINDOWS)
    return jnp.stack([a[grp * gc:(grp + 1) * gc, grp * gc:(grp + 1) * gc] for grp in range(len(POOL_WINDOWS))])


def _local_step(x, p, positions, loss_target, norm1, pool_w, pool_scale, norm2, norm3, final_norm, weights, send):
    rc = rsa = rsb = _rotary_tables(positions)
    ones_bd = _block_diag(jnp.ones((4, HEAD_DIM, HEAD_DIM), BF16))
    saved = []
    h = x
    for i in range(2):
        tag = f"_l{i}"
        g1, g2, g3 = norm1[i:i + 1], norm2[i:i + 1], norm3[i:i + 1]
        w_bd = _block_diag(pool_w[i]).astype(BF16)
        scale = pool_scale[i:i + 1]
        w_in = weights(i, "in", (h, rc, w_bd))
        hn1, u, *qkv = _normproj_fwd(h, g1, w_in, rc, rsa, rsb, "normproj_fwd" + tag)
        qkv = [qkv[3 * grp:3 * grp + 3] for grp in range(3)]
        started = weights(i, "prefetch", (hn1,))
        pool_out, y = _pool_fwd(u, w_bd, scale, "pool_fwd" + tag, after=started)
        o, lse = zip(*[_attn_fwd(*qkv[grp], f"attn_fwd{tag}_g{grp}", after=started) for grp in range(3)])
        w_out = weights(i, "out", (pool_out, *o))
        h1, a = _outproj_fwd(h, pool_out, o, lse, w_out, "outproj_fwd" + tag)
        w_up, w_down = weights(i, "mlp", (h1,))
        h2, hn2, r = _mlp_fwd(h1, g2, w_up, w_down, "mlp_fwd" + tag)
        w_gate, w_ple = weights(i, "gate", (h2,))
        h0 = h
        if i == 0:
            h, hn3, gate, pb = _gate_fwd(h2, g3, w_gate, p, i, w_ple, "gate_fwd" + tag)
        else:
            hn3, gate, pb, loss, dh, d_final = _gate_fwd(h2, g3, w_gate, p, i, w_ple, "gate_fwd" + tag,
                                                         head=(final_norm.reshape(1, D_MODEL), loss_target))
        saved.append(dict(h0=h0, hn1=hn1, qkv=qkv, y=y, o=o, lse=lse, a=a, h1=h1, hn2=hn2, r=r, h2=h2,
                          hn3=hn3, gate=gate, pb=pb, w_bd=w_bd, scale=scale, g1=g1, g2=g2, g3=g3,
                          w_in=w_in, w_out=w_out, w_up=w_up, w_down=w_down, w_gate=w_gate, w_ple=w_ple))

    grads = [None, None]
    sent = ()
    for i in (1, 0):
        tag = f"_l{i}"
        sv = saved[i]
        dh2, dg3, dw_gate, dw_ple = _gate_bwd(dh, sv["gate"], sv["pb"], sv["w_ple"], sv["h2"], sv["g3"], sv["w_gate"],
                                              sv["hn3"], "gate_bwd" + tag, after=sent)
        dh1, dup, dg2, dh2b = _mlp_bwd(dh2, sv["r"], sv["h1"], sv["g2"], sv["w_up"], sv["w_down"], "mlp_bwd" + tag)
        dw_down = _matmul_tn(sv["r"], dh2b, "dw_down" + tag, square_a=True)
        dw_up = _matmul_tn(sv["hn2"], dup, "dw_up" + tag, blocked_out=True)
        dpool, do0, do1, do2, de0, de1, de2, dw_out = _outproj_bwd(dh1, sv["w_out"], sv["o"], sv["lse"], ones_bd,
                                                                   sv["a"], "outproj_bwd" + tag)
        sent = send(i, "main", dict(w_gate=dw_gate, w_ple=dw_ple, w_down=dw_down, w_up=dw_up, w_out=dw_out))
        dqkv = [_attn_bwd(*sv["qkv"][grp], do_g, sv["lse"][grp], de_g, f"attn_bwd{tag}_g{grp}", after=sent)
                for grp, (do_g, de_g) in enumerate(((do0, de0), (do1, de1), (do2, de2)))]
        dq, dk, dv = zip(*dqkv)
        du, dw_bd, dscale = _pool_bwd(dpool, sv["y"], sv["w_bd"], sv["scale"], "pool_bwd" + tag, after=sent)
        dh, dz, dg1 = _normproj_bwd(dh1, du, dq, dk, dv, rc, rsa, rsb, sv["w_in"], sv["h0"], sv["g1"],
                                    "normproj_bwd" + tag)
        grads[i] = dict(norm1=dg1, norm2=dg2, norm3=dg3, pool_w=_diag_blocks(dw_bd), pool_scale=dscale)
        small_sent = send(0, "small", (grads, d_final, loss)) if i == 0 else ()
        dw_in = _matmul_tn(dz, sv["hn1"], "dw_in" + tag, tm=N_IN // 2, after=small_sent)
        sent = send(i, "in", dict(w_in=dw_in))
    return dh, sent


def _pack_small(norm1, norm2, norm3, final_norm, pool_scale, pool_w, spare=None):
    spare = jnp.zeros((1, LANES), F32) if spare is None else spare
    scale_row = jnp.concatenate([pool_scale.reshape(1, 2 * POOL_WIDTH), spare,
                                 jnp.zeros((1, D_MODEL - 2 * POOL_WIDTH - LANES), F32)], axis=1)
    return jnp.concatenate([norm1, norm2, norm3, final_norm.reshape(1, D_MODEL), scale_row,
                            pool_w.reshape(32, D_MODEL)], axis=0)


def _unpack_small(a):
    return dict(norm1=a[0:2], norm2=a[2:4], norm3=a[4:6], final_norm=a[6], pool_scale=a[7, 0:2 * POOL_WIDTH].reshape(2, POOL_WIDTH),
                pool_w=a[8:40].reshape(2, 4, HEAD_DIM, HEAD_DIM))


def _chunks_cols(a, cols):
    return a.reshape(a.shape[0], N_DEV, cols).transpose(1, 0, 2)


def _chunks_rows(a, rows):
    return a.reshape(N_DEV, rows, a.shape[1])


BIG = ("w_in", "w_out", "w_up", "w_down", "w_gate", "w_ple")
SMALL = ("norm1", "norm2", "norm3", "final_norm", "pool_scale", "pool_w")
ORDER = ("norm1", "w_in", "pool_w", "pool_scale", "w_out", "norm2", "w_up", "w_down", "norm3", "w_gate", "w_ple",
         "final_norm")


def kernel(x, p, positions, norm1, w_in, pool_w, pool_scale, w_out, norm2, w_up, w_down, norm3, w_gate, w_ple, final_norm, loss_target, m_norm1, m_w_in, m_pool_w, m_pool_scale, m_w_out, m_norm2, m_w_up, m_w_down, m_norm3, m_w_gate, m_w_ple, m_final_norm, v_norm1, v_w_in, v_pool_w, v_pool_scale, v_w_out, v_norm2, v_w_up, v_w_down, v_norm3, v_w_gate, v_w_ple, v_final_norm):
    w = dict(norm1=norm1, w_in=w_in, pool_w=pool_w, pool_scale=pool_scale, w_out=w_out, norm2=norm2, w_up=w_up,
             w_down=w_down, norm3=norm3, w_gate=w_gate, w_ple=w_ple, final_norm=final_norm)
    m = dict(norm1=m_norm1, w_in=m_w_in, pool_w=m_pool_w, pool_scale=m_pool_scale, w_out=m_w_out, norm2=m_norm2,
             w_up=m_w_up, w_down=m_w_down, norm3=m_norm3, w_gate=m_w_gate, w_ple=m_w_ple, final_norm=m_final_norm)
    v = dict(norm1=v_norm1, w_in=v_w_in, pool_w=v_pool_w, pool_scale=v_pool_scale, w_out=v_w_out, norm2=v_norm2,
             w_up=v_w_up, w_down=v_w_down, norm3=v_norm3, w_gate=v_w_gate, w_ple=v_w_ple, final_norm=v_final_norm)
    seq = x.shape[1]

    bf = {n: [w[n][layer].astype(BF16) for layer in range(2)] for n in BIG}
    bf["w_in"] = [a.T for a in bf["w_in"]]
    me = 4 * lax.axis_index("x") + 2 * lax.axis_index("y") + lax.axis_index("c")
    parts = dict(zip(("in", "out", "mlp", "gate"), (("w_in",), ("w_out",), ("w_up", "w_down"), ("w_gate", "w_ple"))))
    gathers = [_Exchange("gather_l0", [[bf[n][0] for n in parts[pt]] for pt in parts], scatter=False)]
    unpack = dict(w_in=lambda a: a.reshape(N_IN, D_MODEL),
                  w_out=lambda a: a.reshape(D_MODEL, D_MODEL), w_gate=lambda a: a.reshape(D_MODEL, D_MODEL),
                  w_ple=lambda a: a.transpose(1, 0, 2).reshape(PLE_DIM, D_MODEL), w_up=lambda a: a, w_down=lambda a: a)

    def weights(layer, part, after):
        if part == "prefetch":
            if layer != 0:
                return ()
            gathers.append(_Exchange("gather_l1", [[bf[n][1] for n in parts[pt]] for pt in parts], scatter=False,
                                     after=after))
            return (gathers[1].token,)
        shards, lands = gathers[layer].wait(tuple(parts).index(part), after)
        full = [unpack[n](lax.dynamic_update_slice_in_dim(land, shard[None], me, axis=0))
                for n, shard, land in zip(parts[part], shards, lands)]
        return full if len(full) > 1 else full[0]

    to_chunks = dict(w_in=lambda a: _chunks_rows(a, N_IN // N_DEV),
                     w_out=lambda a: _chunks_rows(a, D_MODEL // N_DEV),
                     w_up=lambda a: a, w_down=lambda a: _chunks_rows(a, FF_BLOCK),
                     w_gate=lambda a: _chunks_rows(a, D_MODEL // N_DEV), w_ple=lambda a: _chunks_cols(a, D_MODEL // N_DEV))
    own = {n: [None, None] for n in BIG}
    scatters = {}

    def send(layer, part, grads):
        if part == "small":
            per_layer, d_final, loss = grads
            pack = _pack_small(
                *[jnp.concatenate([per_layer[0][n], per_layer[1][n]], axis=0) for n in ("norm1", "norm2", "norm3")],
                d_final.reshape(D_MODEL),
                jnp.concatenate([per_layer[0]["pool_scale"], per_layer[1]["pool_scale"]], axis=0),
                jnp.stack([per_layer[0]["pool_w"], per_layer[1]["pool_w"]]), spare=loss)
            scatters["small"] = _Exchange("gather_small", [[pack]], scatter=False)
            return (scatters["small"].token,)
        for n, (g32, _) in grads.items():
            own[n][layer] = to_chunks[n](g32)
        ex = _Exchange(f"scatter_{part}_l{layer}", [[to_chunks[n](g16) for n, (_, g16) in grads.items()]], scatter=True)
        scatters[layer, part] = (tuple(grads), ex)
        return (ex.token,)

    dx, sent = _local_step(
        x.reshape(seq, D_MODEL), p.reshape(2, seq, PLE_DIM), positions.reshape(seq), loss_target.reshape(seq, D_MODEL),
        norm1, pool_w, pool_scale, norm2, norm3, final_norm, weights, send)

    g_out, d_out, m_out, v_out = {}, {}, {}, {}
    my_index = me.reshape(1)
    for part in ("main", "in"):
        recv = {}
        for layer in (1, 0):
            names, ex = scatters[layer, part]
            for n, r in zip(names, ex.wait(0, sent)[1]):
                recv[n, layer] = r
        for n in names:
            grad = (*own[n], recv[n, 0], recv[n, 1])
            if n == "w_in":
                grad = _sum_chunks(grad, my_index, "sum_w_in").transpose(0, 2, 1)
            g_out[n], d_out[n], m_out[n], v_out[n] = _adamw_sharded(w[n], m[n], v[n], grad, my_index, "adamw_" + n)
        sent = tuple(d_out[n] for n in names)
    (mine,), (landed,) = scatters["small"].wait(0, sent)
    small_g8 = lax.dynamic_update_slice_in_dim(landed, mine[None], me, axis=0)
    pack = lambda t: _pack_small(*[t[n] for n in SMALL])
    small_g, d_small, m_small, v_small = _adamw_packed(pack(w), small_g8, pack(m), pack(v), "adamw_small")
    for dst, a in ((g_out, small_g), (d_out, d_small), (m_out, m_small), (v_out, v_small)):
        dst.update(_unpack_small(a))

    return (small_g[7, 2 * POOL_WIDTH],dx.reshape(1, seq, D_MODEL), *[g_out[n] for n in ORDER], *[d_out[n] for n in ORDER],
            *[m_out[n] for n in ORDER], *[v_out[n] for n in ORDER])
```

```python
import functools

import jax
import jax.numpy as jnp
from jax import lax
from jax.experimental import pallas as pl
from jax.experimental.pallas import tpu as pltpu

F32 = jnp.float32
BF16 = jnp.bfloat16

D_MODEL = 1024
HEAD_DIM = 64
POOL_WIDTH = 256
POOL_WINDOWS = (2, 4, 8, 16)
POOL_HALO = 16
POOL_PAD = 8
GROUP_WIDTH = 256
DILATIONS = (1, 4, 16)
ATTN_BLOCK = 128
ROT_SHIFT = 8
ROPE_THETA = 500000.0
D_FF = 4096
FF_BLOCK = 512
FF_PER_STEP = 2
MLP_BWD_TILE = 512
FWD_TILE = 1024
N_DEV = 8
N_IN = POOL_WIDTH + 3 * 768
PLE_DIM = 256
EPS = 1e-6
NEG_BIG = -1e30

ADAM_LR = 0.001
ADAM_B1 = 0.9
ADAM_B2 = 0.999
ADAM_EPS = 1e-08
ADAM_WD = 0.01
ADAM_STEP = 10

LANES = 128
VMEM_LIMIT = 56 * 1024 * 1024
MESH = pl.DeviceIdType.MESH


def _params(n_grid):
    return pltpu.CompilerParams(dimension_semantics=("arbitrary",) * n_grid, vmem_limit_bytes=VMEM_LIMIT)


def _dot(a, b):
    return jnp.dot(a, b, preferred_element_type=F32)


def _dot_nt(a, b):
    return lax.dot_general(a, b, (((1,), (1,)), ((), ())), preferred_element_type=F32)


def _dot_tn(a, b):
    return lax.dot_general(a, b, (((0,), (0,)), ((), ())), preferred_element_type=F32)


def _rms(x, g):
    rstd = lax.rsqrt(jnp.mean(x * x, axis=-1, keepdims=True) + EPS)
    n = x * rstd
    return n, rstd, n * g


def _rms_bwd(dy, n, rstd, g):
    dyn = dy * g
    dx = rstd * (dyn - n * jnp.mean(dyn * n, axis=-1, keepdims=True))
    return dx, jnp.sum(dy * n, axis=0, keepdims=True)


def _ordered_after(body, n_in, after):
    if not after:
        return body
    return lambda *refs: body(*refs[:n_in], *refs[n_in + len(after):])


def _resident(shape):
    return pl.BlockSpec(shape, lambda i: (0,) * len(shape), pipeline_mode=pl.Buffered(1))


def _row_tile(s, t):
    t = min(s, t)
    assert s % t == 0
    return t


def _rot(z, c, sa, sb):
    return z * c + pltpu.roll(z, ROT_SHIFT, 1) * sa + pltpu.roll(z, LANES - ROT_SHIFT, 1) * sb


def _table_specs(t):
    return [pl.BlockSpec((t, LANES), functools.partial(lambda i, k: (i, k), k=k)) for k in range(3)]


def _rot_t(dz, c, sa, sb):
    return dz * c + pltpu.roll(dz * sa, LANES - ROT_SHIFT, 1) + pltpu.roll(dz * sb, ROT_SHIFT, 1)


def _to_residues(value, stage, out_ref, dil):
    if dil == 1:
        out_ref[0] = value.astype(out_ref.dtype)
        return
    rows = value.shape[0] // dil
    for hf in range(GROUP_WIDTH // LANES):
        lanes = slice(hf * LANES, (hf + 1) * LANES)
        stage[hf][...] = value[:, lanes]
        for r in range(dil):
            out_ref[r, :, lanes] = stage[hf][pl.ds(r, rows, stride=dil), :].astype(out_ref.dtype)


def _from_residues(in_ref, stage, dil):
    if dil == 1:
        return in_ref[0].astype(F32)
    rows = in_ref.shape[1]
    for hf in range(GROUP_WIDTH // LANES):
        for r in range(dil):
            stage[hf][pl.ds(r, rows, stride=dil), :] = in_ref[r, :, hf * LANES:(hf + 1) * LANES].astype(F32)
    return jnp.concatenate([stage[0][...], stage[1][...]], axis=1)


def _residue_spec(dil, t):
    return pl.BlockSpec((dil, t // dil, GROUP_WIDTH), lambda i: (0, i, 0))


def _residue_shape(dil, s, dtype):
    return jax.ShapeDtypeStruct((dil, s // dil, GROUP_WIDTH), dtype)


def _stages(t, n):
    return [pltpu.VMEM((t, LANES), F32)] * (n * (GROUP_WIDTH // LANES))


def _pair_stages(refs):
    return [refs[i:i + 2] for i in range(0, len(refs), 2)]


def _normproj_fwd(h, g, w_in, rc, rsa, rsb, name):
    s = h.shape[0]
    t = _row_tile(s, FWD_TILE)

    def body(h_ref, g_ref, w_ref, c_ref, sa_ref, sb_ref, hn_ref, u_ref, *rest):
        qkv_refs, stages = rest[:9], _pair_stages(rest[9:])
        _, _, hn = _rms(h_ref[...], g_ref[...])
        hb = hn.astype(BF16)
        hn_ref[...] = hb
        c, sa, sb = c_ref[...], sa_ref[...], sb_ref[...]

        def rot(z, scale):
            halves = [_rot(z[:, hf * LANES:(hf + 1) * LANES], c, sa, sb) * scale for hf in range(2)]
            return jnp.concatenate(halves, axis=1)

        proj = lambda lo: _dot_nt(hb, w_ref[lo:lo + GROUP_WIDTH, :])
        u_ref[...] = proj(0)
        for grp, dil in enumerate(DILATIONS):
            lo = POOL_WIDTH + grp * GROUP_WIDTH
            q_ref, k_ref, v_ref = qkv_refs[3 * grp:3 * grp + 3]
            _to_residues(rot(proj(lo), HEAD_DIM ** -0.5), stages[0], q_ref, dil)
            _to_residues(rot(proj(lo + 768), 1.0), stages[1], k_ref, dil)
            _to_residues(proj(lo + 1536), stages[2], v_ref, dil)

    row = lambda w: pl.BlockSpec((t, w), lambda i: (i, 0))
    return pl.pallas_call(
        body, name=name, grid=(s // t,),
        in_specs=[row(D_MODEL), pl.BlockSpec((1, D_MODEL), lambda i: (0, 0)),
                  _resident((N_IN, D_MODEL))] + _table_specs(t),
        out_specs=[row(D_MODEL), row(POOL_WIDTH)] + [_residue_spec(dil, t) for dil in DILATIONS for _ in range(3)],
        out_shape=[jax.ShapeDtypeStruct((s, D_MODEL), BF16), jax.ShapeDtypeStruct((s, POOL_WIDTH), F32)]
        + [_residue_shape(dil, s, BF16) for dil in DILATIONS for _ in range(3)],
        scratch_shapes=_stages(t, 3),
        compiler_params=_params(1),
    )(h, g, w_in, rc, rsa, rsb)


def _pool_lane_window():
    lane = lax.broadcasted_iota(jnp.int32, (1, POOL_WIDTH), 1)
    return jnp.left_shift(2, lane // (POOL_WIDTH // len(POOL_WINDOWS)))


def _window_sums(ext, b2, b4, b8, t, lo, tile, direction):
    rows = t + POOL_HALO
    for src, dst, sh in ((ext, b2, 1), (b2, b4, 2), (b4, b8, 4)):
        dst[lo:lo + rows, :] = src[lo:lo + rows, :] + src[lo + direction * sh:lo + direction * sh + rows, :]
    s16 = b8[tile:tile + t, :] + b8[tile + direction * 8:tile + direction * 8 + t, :]
    win = _pool_lane_window()
    return jnp.where(win == 2, b2[tile:tile + t, :],
                     jnp.where(win == 4, b4[tile:tile + t, :], jnp.where(win == 8, b8[tile:tile + t, :], s16)))


def _pool_fwd(u, w_bd, scale, name, after=()):
    s = u.shape[0]
    t = _row_tile(s, 512)
    first = POOL_PAD + POOL_HALO

    def body(u_ref, w_ref, sc_ref, out_ref, y_ref, ext, b2, b4, b8):
        i = pl.program_id(0)

        @pl.when(i == 0)
        def _():
            for buf in (ext, b2, b4):
                buf[0:POOL_PAD, :] = jnp.zeros((POOL_PAD, POOL_WIDTH), F32)
            ext[POOL_PAD:first, :] = jnp.zeros((POOL_HALO, POOL_WIDTH), F32)

        x = u_ref[...]
        ext[first:, :] = x
        wsum = _window_sums(ext, b2, b4, b8, t, POOL_PAD, first, -1)
        pos = i * t + lax.broadcasted_iota(jnp.int32, (t, POOL_WIDTH), 0)
        cnt = jnp.minimum(pos + 1, _pool_lane_window()).astype(F32)
        y = wsum / cnt - x
        yb = y.astype(BF16)
        y_ref[...] = yb
        out_ref[...] = _dot(yb, w_ref[...]) * sc_ref[...]
        ext[POOL_PAD:first, :] = x[t - POOL_HALO:, :]

    row = pl.BlockSpec((t, POOL_WIDTH), lambda i: (i, 0))
    return pl.pallas_call(
        _ordered_after(body, 3, after), name=name, grid=(s // t,),
        in_specs=[row, pl.BlockSpec((POOL_WIDTH, POOL_WIDTH), lambda i: (0, 0)),
                  pl.BlockSpec((1, POOL_WIDTH), lambda i: (0, 0))] + [pl.BlockSpec(memory_space=pl.ANY)] * len(after),
        out_specs=[row, row],
        out_shape=[jax.ShapeDtypeStruct((s, POOL_WIDTH), F32), jax.ShapeDtypeStruct((s, POOL_WIDTH), BF16)],
        scratch_shapes=[pltpu.VMEM((t + POOL_HALO + POOL_PAD, POOL_WIDTH), F32)] * 4,
        compiler_params=_params(1),
    )(u, w_bd, scale, *after)


def _head_masks():
    lane = lax.broadcasted_iota(jnp.int32, (ATTN_BLOCK, GROUP_WIDTH), 1)
    return [lane // HEAD_DIM == hd for hd in range(GROUP_WIDTH // HEAD_DIM)]


def _stack_heads(a, masks):
    zero = jnp.zeros_like(a)
    return jnp.concatenate([jnp.where(m, a, zero) for m in masks], axis=0)


def _band_bias(first_step):
    rows = ATTN_BLOCK * (GROUP_WIDTH // HEAD_DIM)
    i = lax.broadcasted_iota(jnp.int32, (rows, 2 * ATTN_BLOCK), 0) & (ATTN_BLOCK - 1)
    j = lax.broadcasted_iota(jnp.int32, (rows, 2 * ATTN_BLOCK), 1)
    inner = jnp.where((j >= i) & (j <= i + ATTN_BLOCK), 0.0, NEG_BIG)
    return jnp.where((j < ATTN_BLOCK) & first_step, NEG_BIG, inner), inner


def _column_per_head(a):
    return jnp.concatenate([a[:, hd * HEAD_DIM:hd * HEAD_DIM + 1] for hd in range(GROUP_WIDTH // HEAD_DIM)], axis=0)


def _blocks_per_step(nb):
    if nb <= 16:
        return nb
    return next(qb for qb in (16, 8, 4, 2, 1) if nb % qb == 0)


def _residues_per_step(dil, nb, qb):
    return 2 if (nb == qb and qb < 8 and dil % 2 == 0) else 1


def _attn_fwd(q, k, v, name, after=()):
    dil, length, _ = q.shape
    nb = length // ATTN_BLOCK
    qb = _blocks_per_step(nb)
    rs = _residues_per_step(dil, nb, qb)

    def body(q_ref, kp_ref, kc_ref, vp_ref, vc_ref, o_ref, lse_ref):
        masks = _head_masks()
        bias = _band_bias(pl.program_id(1) == 0)
        for rr in range(rs):
            for qi in range(qb):
                here = slice(qi * ATTN_BLOCK, (qi + 1) * ATTN_BLOCK)
                before = slice((qi - 1) * ATTN_BLOCK, qi * ATTN_BLOCK)
                kcat = jnp.concatenate([kp_ref[rr] if qi == 0 else kc_ref[rr, before], kc_ref[rr, here]], axis=0)
                vcat = jnp.concatenate([vp_ref[rr] if qi == 0 else vc_ref[rr, before], vc_ref[rr, here]], axis=0)
                qs = _stack_heads(q_ref[rr, here], masks)
                sc = _dot_nt(qs, kcat) + bias[min(qi, 1)]
                m = jnp.max(sc, axis=1, keepdims=True)
                e = jnp.exp(sc - m)
                l = jnp.sum(e, axis=1, keepdims=True)
                p = (e / l).astype(BF16)
                lse = m + jnp.log(l)
                o = jnp.zeros((ATTN_BLOCK, GROUP_WIDTH), F32)
                lse_full = jnp.zeros((ATTN_BLOCK, GROUP_WIDTH), F32)
                for hd, msk in enumerate(masks):
                    rows = slice(hd * ATTN_BLOCK, (hd + 1) * ATTN_BLOCK)
                    o = jnp.where(msk, _dot(p[rows], vcat), o)
                    lse_full = jnp.where(msk, lse[rows], lse_full)
                o_ref[rr, here] = o.astype(o_ref.dtype)
                lse_ref[rr, here] = lse_full

    cur = pl.BlockSpec((rs, qb * ATTN_BLOCK, GROUP_WIDTH), lambda r, j: (r, j, 0))
    prev = pl.BlockSpec((rs, ATTN_BLOCK, GROUP_WIDTH), lambda r, j: (r, jnp.maximum(qb * j - 1, 0), 0))
    return pl.pallas_call(
        _ordered_after(body, 5, after), name=name, grid=(dil // rs, nb // qb),
        in_specs=[cur, prev, cur, prev, cur] + [pl.BlockSpec(memory_space=pl.ANY)] * len(after), out_specs=[cur, cur],
        out_shape=[jax.ShapeDtypeStruct(q.shape, BF16), jax.ShapeDtypeStruct(q.shape, F32)],
        compiler_params=_params(2),
    )(q, k, k, v, v, *after)


def _group_weights(l0, l1, l2):
    m = jnp.maximum(jnp.maximum(l0, l1), l2)
    e0, e1, e2 = jnp.exp(l0 - m), jnp.exp(l1 - m), jnp.exp(l2 - m)
    den = e0 + e1 + e2
    return e0 / den, e1 / den, e2 / den


def _outproj_fwd(h, pool_out, o, lse, w_out, name):
    s = h.shape[0]
    t = _row_tile(s, FWD_TILE)

    def body(h_ref, po_ref, o0, o1, o2, l0, l1, l2, w_ref, out_ref, a_ref, *stages):
        stages = _pair_stages(stages)
        ov =[_from_residues(r, stages[i], DILATIONS[i]) for i, r in enumerate((o0, o1, o2))]
        lv = [_from_residues(r, stages[3 + i], DILATIONS[i]) for i, r in enumerate((l0, l1, l2))]
        wts = _group_weights(*lv)
        a = jnp.concatenate([po_ref[...]] + [ov[i] * wts[i] for i in range(3)], axis=1).astype(BF16)
        a_ref[...] = a
        out_ref[...] = h_ref[...] + _dot(a, w_ref[...])

    row = lambda w: pl.BlockSpec((t, w), lambda i: (i, 0))
    res = [_residue_spec(dil, t) for dil in DILATIONS]
    return pl.pallas_call(
        body, name=name, grid=(s // t,),
        in_specs=[row(D_MODEL), row(POOL_WIDTH)] + res + res + [_resident((D_MODEL, D_MODEL))],
        out_specs=[row(D_MODEL), row(D_MODEL)],
        out_shape=[jax.ShapeDtypeStruct((s, D_MODEL), F32), jax.ShapeDtypeStruct((s, D_MODEL), BF16)],
        scratch_shapes=_stages(t, 6),
        compiler_params=_params(1),
    )(h, pool_out, *o, *lse, w_out)


def _mlp_fwd(h, g, w_up, w_down, name):
    s = h.shape[0]
    t = _row_tile(s, 512)
    nblk = D_FF // FF_BLOCK

    def body(h_ref, g_ref, wu_ref, wd_ref, out_ref, hn_ref, r_ref):
        x = h_ref[...]
        _, _, hn = _rms(x, g_ref[...])
        hb = hn.astype(BF16)
        hn_ref[...] = hb
        acc = None
        for b0 in range(0, nblk, FF_PER_STEP):
            acts = []
            for b in range(b0, b0 + FF_PER_STEP):
                r = jnp.maximum(_dot(hb, wu_ref[b]), 0.0)
                r_ref[:, b * FF_BLOCK:(b + 1) * FF_BLOCK] = r.astype(BF16)
                acts.append((r * r).astype(BF16))
            wd = wd_ref[b0:b0 + FF_PER_STEP].reshape(FF_PER_STEP * FF_BLOCK, D_MODEL)
            part = _dot(jnp.concatenate(acts, axis=1), wd)
            acc = part if acc is None else acc + part
        out_ref[...] = x + acc

    row = lambda w: pl.BlockSpec((t, w), lambda i: (i, 0))
    resident = lambda shape: pl.BlockSpec(shape, lambda i: (0, 0, 0), pipeline_mode=pl.Buffered(1))
    return pl.pallas_call(
        body, name=name, grid=(s // t,),
        in_specs=[row(D_MODEL), pl.BlockSpec((1, D_MODEL), lambda i: (0, 0)),
                  resident((nblk, D_MODEL, FF_BLOCK)), resident((nblk, FF_BLOCK, D_MODEL))],
        out_specs=[row(D_MODEL), row(D_MODEL), row(D_FF)],
        out_shape=[jax.ShapeDtypeStruct((s, D_MODEL), F32), jax.ShapeDtypeStruct((s, D_MODEL), BF16),
                   jax.ShapeDtypeStruct((s, D_FF), BF16)],
        compiler_params=_params(1),
    )(h, g, w_up, w_down)


def _gate_fwd(h, g, w_gate, p, layer, w_ple, name, head=None):
    s = h.shape[0]
    t = _row_tile(s, FWD_TILE)

    def body(h_ref, g_ref, wg_ref, p_ref, wp_ref, *refs):
        x = h_ref[...]
        _, _, hn = _rms(x, g_ref[...])
        hb = hn.astype(BF16)
        gate = 1.0 / (1.0 + jnp.exp(-_dot(hb, wg_ref[...])))
        pb = p_ref[...].astype(BF16)
        h3 = x + gate * _dot(pb, wp_ref[...])
        if head is None:
            out_ref, hn_ref, gate_ref, pb_ref = refs
            out_ref[...] = h3
        else:
            gf_ref, t_ref, hn_ref, gate_ref, pb_ref, loss_ref, dh_ref, dgf_ref = refs

            @pl.when(pl.program_id(0) == 0)
            def _():
                loss_ref[...] = jnp.zeros_like(loss_ref)
                dgf_ref[...] = jnp.zeros_like(dgf_ref)

            gf = gf_ref[...]
            n, rstd, y = _rms(h3, gf)
            err = y - t_ref[...]
            loss_ref[...] += jnp.sum(err * err) * (0.5 / D_MODEL)
            dh_ref[...], dgf = _rms_bwd(err * (1.0 / D_MODEL), n, rstd, gf)
            dgf_ref[...] += dgf
        hn_ref[...] = hb
        pb_ref[...] = pb
        gate_ref[...] = gate.astype(BF16)

    row = lambda w: pl.BlockSpec((t, w), lambda i: (i, 0))
    full = lambda a, b: pl.BlockSpec((a, b), lambda i: (0, 0))
    in_specs = [row(D_MODEL), full(1, D_MODEL), _resident((D_MODEL, D_MODEL)),
                pl.BlockSpec((None, t, PLE_DIM), lambda i: (layer, i, 0)), _resident((PLE_DIM, D_MODEL))]
    saved_specs = [row(D_MODEL), row(D_MODEL), row(PLE_DIM)]
    saved_shapes = [jax.ShapeDtypeStruct((s, D_MODEL), BF16), jax.ShapeDtypeStruct((s, D_MODEL), BF16),
                    jax.ShapeDtypeStruct((s, PLE_DIM), BF16)]
    if head is None:
        return pl.pallas_call(
            body, name=name, grid=(s // t,), in_specs=in_specs, out_specs=[row(D_MODEL)] + saved_specs,
            out_shape=[jax.ShapeDtypeStruct((s, D_MODEL), F32)] + saved_shapes, compiler_params=_params(1),
        )(h, g, w_gate, p, w_ple)
    return pl.pallas_call(
        body, name=name, grid=(s // t,), in_specs=in_specs + [full(1, D_MODEL), row(D_MODEL)],
        out_specs=saved_specs + [pl.BlockSpec((1, LANES), lambda i: (0, 0)), row(D_MODEL), full(1, D_MODEL)],
        out_shape=saved_shapes + [jax.ShapeDtypeStruct((1, LANES), F32), jax.ShapeDtypeStruct((s, D_MODEL), F32),
                                  jax.ShapeDtypeStruct((1, D_MODEL), F32)],
        compiler_params=_params(1),
    )(h, g, w_gate, p, w_ple, *head)


def _gate_bwd(dh, gate, pb, w_ple, h, g, w_gate, hn, name, after=()):
    s = h.shape[0]
    t = _row_tile(s, 512)
    last = s // t - 1

    def body(dh_ref, gate_ref, pb_ref, wp_ref, h_ref, g_ref, wg_ref, hn_ref, out_ref, dg_ref, dwg_ref, dwgb_ref,
             dwp_ref, dwpb_ref):
        i = pl.program_id(0)

        @pl.when(i == 0)
        def _():
            dg_ref[...] = jnp.zeros_like(dg_ref)
            dwg_ref[...] = jnp.zeros_like(dwg_ref)
            dwp_ref[...] = jnp.zeros_like(dwp_ref)

        d = dh_ref[...]
        gate = gate_ref[...].astype(F32)
        pb = pb_ref[...]
        e = _dot(pb, wp_ref[...])
        dgl = (d * e * gate * (1.0 - gate)).astype(BF16)
        dwg_ref[...] += _dot_tn(hn_ref[...], dgl)
        dwp_ref[...] += _dot_tn(pb, (d * gate).astype(BF16))
        gv = g_ref[...]
        n, rstd, _ = _rms(h_ref[...], gv)
        dx, dg = _rms_bwd(_dot_nt(dgl, wg_ref[...]), n, rstd, gv)
        out_ref[...] = d + dx
        dg_ref[...] += dg

        @pl.when(i == last)
        def _():
            dwgb_ref[...] = dwg_ref[...].astype(BF16)
            dwpb_ref[...] = dwp_ref[...].astype(BF16)

    row = lambda w: pl.BlockSpec((t, w), lambda i: (i, 0))
    full = lambda a, b: pl.BlockSpec((a, b), lambda i: (0, 0))
    dh2, dg, dwg, dwgb, dwp, dwpb = pl.pallas_call(
        _ordered_after(body, 8, after), name=name, grid=(s // t,),
        in_specs=[row(D_MODEL), row(D_MODEL), row(PLE_DIM), full(PLE_DIM, D_MODEL), row(D_MODEL), full(1, D_MODEL),
                  full(D_MODEL, D_MODEL), row(D_MODEL)] + [pl.BlockSpec(memory_space=pl.ANY)] * len(after),
        out_specs=[row(D_MODEL), full(1, D_MODEL), full(D_MODEL, D_MODEL), full(D_MODEL, D_MODEL),
                   full(PLE_DIM, D_MODEL), full(PLE_DIM, D_MODEL)],
        out_shape=[jax.ShapeDtypeStruct((s, D_MODEL), F32), jax.ShapeDtypeStruct((1, D_MODEL), F32),
                   jax.ShapeDtypeStruct((D_MODEL, D_MODEL), F32), jax.ShapeDtypeStruct((D_MODEL, D_MODEL), BF16),
                   jax.ShapeDtypeStruct((PLE_DIM, D_MODEL), F32), jax.ShapeDtypeStruct((PLE_DIM, D_MODEL), BF16)],
        compiler_params=_params(1),
    )(dh, gate, pb, w_ple, h, g, w_gate, hn, *after)
    return dh2, dg, (dwg, dwgb), (dwp, dwpb)


def _mlp_bwd(dh, r, h, g, w_up, w_down, name):
    s = h.shape[0]
    t = _row_tile(s, MLP_BWD_TILE)
    nblk = D_FF // FF_BLOCK

    def body(dh_ref, r_ref, h_ref, g_ref, wu_ref, wd_ref, out_ref, dup_ref, dg_ref, dhb_ref):
        @pl.when(pl.program_id(0) == 0)
        def _():
            dg_ref[...] = jnp.zeros_like(dg_ref)

        d = dh_ref[...]
        db = d.astype(BF16)
        dhb_ref[...] = db
        back = None
        for b in range(nblk):
            cols = slice(b * FF_BLOCK, (b + 1) * FF_BLOCK)
            dup = (_dot_nt(db, wd_ref[b]) * (2.0 * r_ref[:, cols].astype(F32))).astype(BF16)
            dup_ref[:, cols] = dup
            part = _dot_nt(dup, wu_ref[b])
            back = part if back is None else back + part
        gv = g_ref[...]
        n, rstd, _ = _rms(h_ref[...], gv)
        dx, dg = _rms_bwd(back, n, rstd, gv)
        out_ref[...] = d + dx
        dg_ref[...] += dg

    row = lambda w: pl.BlockSpec((t, w), lambda i: (i, 0))
    vec = pl.BlockSpec((1, D_MODEL), lambda i: (0, 0))
    resident = lambda shape: pl.BlockSpec(shape, lambda i: (0, 0, 0), pipeline_mode=pl.Buffered(1))
    return pl.pallas_call(
        body, name=name, grid=(s // t,),
        in_specs=[row(D_MODEL), row(D_FF), row(D_MODEL), vec,
                  resident((nblk, D_MODEL, FF_BLOCK)), resident((nblk, FF_BLOCK, D_MODEL))],
        out_specs=[row(D_MODEL), row(D_FF), vec, row(D_MODEL)],
        out_shape=[jax.ShapeDtypeStruct((s, D_MODEL), F32), jax.ShapeDtypeStruct((s, D_FF), BF16),
                   jax.ShapeDtypeStruct((1, D_MODEL), F32), jax.ShapeDtypeStruct((s, D_MODEL), BF16)],
        compiler_params=_params(1),
    )(dh, r, h, g, w_up, w_down)


def _outproj_bwd(dh, w_out, o, lse, ones_bd, a, name):
    s = dh.shape[0]
    t = _row_tile(s, 512)
    last = s // t - 1

    def body(dh_ref, w_ref, o0, o1, o2, l0, l1, l2, bd_ref, a_ref, dp_ref, do0, do1, do2, de0, de1, de2, dw_ref,
             dwb_ref, *stages):
        i = pl.program_id(0)

        @pl.when(i == 0)
        def _():
            dw_ref[...] = jnp.zeros_like(dw_ref)

        stages = _pair_stages(stages)
        dhb = dh_ref[...].astype(BF16)
        dw_ref[...] += _dot_tn(a_ref[...], dhb)

        @pl.when(i == last)
        def _():
            dwb_ref[...] = dw_ref[...].astype(BF16)

        da = _dot_nt(dhb, w_ref[...])
        dp_ref[...] = da[:, 0:POOL_WIDTH]
        ov =[_from_residues(r, stages[i], DILATIONS[i]) for i, r in enumerate((o0, o1, o2))]
        lv = [_from_residues(r, stages[3 + i], DILATIONS[i]) for i, r in enumerate((l0, l1, l2))]
        wts = _group_weights(*lv)
        bd = bd_ref[...]
        cbar = jnp.zeros((t, GROUP_WIDTH), F32)
        for grp, do_ref in enumerate((do0, do1, do2)):
            lo = POOL_WIDTH + grp * GROUP_WIDTH
            dag = da[:, lo:lo + GROUP_WIDTH]
            _to_residues(dag * wts[grp], stages[6 + grp], do_ref, DILATIONS[grp])
            prod = dag * ov[grp]
            hi = prod.astype(BF16)
            low = (prod - hi.astype(F32)).astype(BF16)
            cbar = cbar + wts[grp] * (_dot(hi, bd) + _dot(low, bd))
        for grp, de_ref in enumerate((de0, de1, de2)):
            _to_residues(wts[grp] * cbar, stages[9 + grp], de_ref, DILATIONS[grp])

    row = lambda w: pl.BlockSpec((t, w), lambda i: (i, 0))
    full = lambda a, b: pl.BlockSpec((a, b), lambda i: (0, 0))
    res = [_residue_spec(dil, t) for dil in DILATIONS]
    *outs, dw, dwb = pl.pallas_call(
        body, name=name, grid=(s // t,),
        in_specs=[row(D_MODEL), full(D_MODEL, D_MODEL)] + res + res + [full(GROUP_WIDTH, GROUP_WIDTH), row(D_MODEL)],
        out_specs=[row(POOL_WIDTH)] + res + res + [full(D_MODEL, D_MODEL)] * 2,
        out_shape=[jax.ShapeDtypeStruct((s, POOL_WIDTH), F32)] + [_residue_shape(dil, s, BF16) for dil in DILATIONS]
        + [_residue_shape(dil, s, F32) for dil in DILATIONS]
        + [jax.ShapeDtypeStruct((D_MODEL, D_MODEL), F32), jax.ShapeDtypeStruct((D_MODEL, D_MODEL), BF16)],
        scratch_shapes=_stages(t, 12),
        compiler_params=_params(1),
    )(dh, w_out, *o, *lse, ones_bd, a)
    return (*outs, (dw, dwb))


def _attn_bwd(q, k, v, do, lse, deff, name, after=()):
    dil, length, _ = q.shape
    nb = length // ATTN_BLOCK
    qb = _blocks_per_step(nb)
    nj = nb // qb
    rs = _residues_per_step(dil, nb, qb)
    whole = nj == 1
    tail = slice((qb - 1) * ATTN_BLOCK, qb * ATTN_BLOCK)
    block = lambda qi: slice(qi * ATTN_BLOCK, (qi + 1) * ATTN_BLOCK)

    def body(q_ref, kp_ref, kc_ref, vp_ref, vc_ref, do_ref, lse_ref, de_ref, dq_ref, dk_ref, dv_ref, ck, cv):
        j = pl.program_id(1)

        def compute():
            masks = _head_masks()
            bias = _band_bias(j == 0)
            for rr in range(rs):
                dkc, dvc = [], []
                for qi in range(qb):
                    here, before = block(qi), block(qi - 1)
                    kcat = jnp.concatenate([kp_ref[rr] if qi == 0 else kc_ref[rr, before], kc_ref[rr, here]], axis=0)
                    vcat = jnp.concatenate([vp_ref[rr] if qi == 0 else vc_ref[rr, before], vc_ref[rr, here]], axis=0)
                    qs = _stack_heads(q_ref[rr, here], masks)
                    dos = _stack_heads(do_ref[rr, here], masks)
                    sc = _dot_nt(qs, kcat) + bias[min(qi, 1)]
                    p = jnp.exp(sc - _column_per_head(lse_ref[rr, here]))
                    ds = (p * (_dot_nt(dos, vcat) - _column_per_head(de_ref[rr, here]))).astype(BF16)
                    dq = jnp.zeros((ATTN_BLOCK, GROUP_WIDTH), F32)
                    for hd, msk in enumerate(masks):
                        dq = jnp.where(msk, _dot(ds[block(hd)], kcat), dq)
                    dq_ref[rr, here] = dq.astype(dq_ref.dtype)
                    dkc.append(_dot_tn(ds, qs))
                    dvc.append(_dot_tn(p.astype(BF16), dos))

                for out_ref, carry, parts in ((dk_ref, ck, dkc), (dv_ref, cv, dvc)):
                    full = [parts[qi][ATTN_BLOCK:] + parts[qi + 1][0:ATTN_BLOCK] for qi in range(qb - 1)]
                    if whole:
                        for qi, val in enumerate(full + [parts[qb - 1][ATTN_BLOCK:]]):
                            out_ref[rr, block(qi)] = val.astype(out_ref.dtype)
                        continue

                    @pl.when(j > 0)
                    def _():
                        if qb > 1:
                            out_ref[0, 0:(qb - 1) * ATTN_BLOCK] = carry[0:(qb - 1) * ATTN_BLOCK].astype(out_ref.dtype)
                        out_ref[0, tail] = (carry[tail] + parts[0][0:ATTN_BLOCK]).astype(out_ref.dtype)

                    for qi, val in enumerate(full):
                        carry[block(qi)] = val
                    carry[tail] = parts[qb - 1][ATTN_BLOCK:]

        if whole:
            compute()
        else:
            pl.when(j < nj)(compute)

            @pl.when(j == nj)
            def _():
                dk_ref[0] = ck[...].astype(dk_ref.dtype)
                dv_ref[0] = cv[...].astype(dv_ref.dtype)

    step = lambda j: jnp.minimum(j, nj - 1)
    cur = pl.BlockSpec((rs, qb * ATTN_BLOCK, GROUP_WIDTH), lambda r, j: (r, step(j), 0))
    prev = pl.BlockSpec((rs, ATTN_BLOCK, GROUP_WIDTH), lambda r, j: (r, jnp.maximum(qb * step(j) - 1, 0), 0))
    late = pl.BlockSpec((rs, qb * ATTN_BLOCK, GROUP_WIDTH), lambda r, j: (r, jnp.maximum(j - 1, 0), 0))
    return pl.pallas_call(
        _ordered_after(body, 8, after), name=name, grid=(dil // rs, 1 if whole else nj + 1),
        in_specs=[cur, prev, cur, prev, cur, cur, cur, cur] + [pl.BlockSpec(memory_space=pl.ANY)] * len(after),
        out_specs=[cur, cur if whole else late, cur if whole else late],
        out_shape=[jax.ShapeDtypeStruct(q.shape, BF16)] * 3,
        scratch_shapes=[pltpu.VMEM((qb * ATTN_BLOCK, GROUP_WIDTH), F32)] * 2,
        compiler_params=_params(2),
    )(q, k, k, v, v, do, lse, deff, *after)


def _pool_bwd(dpool, y, w_bd, scale, name, after=()):
    s = dpool.shape[0]
    t = _row_tile(s, 512)
    nt = s // t

    def body(dp_ref, y_ref, w_ref, sc_ref, du_ref, dw_ref, dsc_ref, ext, b2, b4, b8):
        i = pl.program_id(0)

        @pl.when(i == 0)
        def _():
            ext[t:, :] = jnp.zeros((POOL_HALO + POOL_PAD, POOL_WIDTH), F32)
            for buf in (b2, b4):
                buf[t + POOL_HALO:, :] = jnp.zeros((POOL_PAD, POOL_WIDTH), F32)
            dw_ref[...] = jnp.zeros_like(dw_ref)
            dsc_ref[...] = jnp.zeros_like(dsc_ref)

        dp = dp_ref[...]
        yb = y_ref[...]
        w = w_ref[...]
        dsc_ref[...] += jnp.sum(dp * _dot(yb, w), axis=0, keepdims=True)
        dyo = (dp * sc_ref[...]).astype(BF16)
        dw_ref[...] += _dot_tn(yb, dyo)
        dy = _dot_nt(dyo, w)
        win = _pool_lane_window()
        pos = (nt - 1 - i) * t + lax.broadcasted_iota(jnp.int32, (t, POOL_WIDTH), 0)
        gq = dy / jnp.minimum(pos + 1, win).astype(F32)
        ext[0:t, :] = gq
        du_ref[...] = _window_sums(ext, b2, b4, b8, t, 0, 0, 1) - dy
        ext[t:t + POOL_HALO, :] = gq[0:POOL_HALO, :]

    rev = pl.BlockSpec((t, POOL_WIDTH), lambda i: (nt - 1 - i, 0))
    full = lambda a, b: pl.BlockSpec((a, b), lambda i: (0, 0))
    return pl.pallas_call(
        _ordered_after(body, 4, after), name=name, grid=(nt,),
        in_specs=[rev, rev, full(POOL_WIDTH, POOL_WIDTH), full(1, POOL_WIDTH)]
        + [pl.BlockSpec(memory_space=pl.ANY)] * len(after),
        out_specs=[rev, full(POOL_WIDTH, POOL_WIDTH), full(1, POOL_WIDTH)],
        out_shape=[jax.ShapeDtypeStruct((s, POOL_WIDTH), F32), jax.ShapeDtypeStruct((POOL_WIDTH, POOL_WIDTH), F32),
                   jax.ShapeDtypeStruct((1, POOL_WIDTH), F32)],
        scratch_shapes=[pltpu.VMEM((t + POOL_HALO + POOL_PAD, POOL_WIDTH), F32)] * 4,
        compiler_params=_params(1),
    )(dpool, y, w_bd, scale, *after)


def _normproj_bwd(dh, du, dq, dk, dv, rc, rsa, rsb, w_in, h, g, name):
    s = h.shape[0]
    t = _row_tile(s, 512)

    def body(dh_ref, du_ref, q0, q1, q2, k0, k1, k2, v0, v1, v2, c_ref, sa_ref, sb_ref, w_ref, h_ref, g_ref,
             out_ref, dz_ref, dg_ref, *stages):
        @pl.when(pl.program_id(0) == 0)
        def _():
            dg_ref[...] = jnp.zeros_like(dg_ref)

        c, sa, sb = c_ref[...], sa_ref[...], sb_ref[...]

        def unrot(a, scale):
            halves = [_rot_t(a[:, hf * LANES:(hf + 1) * LANES] * scale, c, sa, sb) for hf in range(2)]
            return jnp.concatenate(halves, axis=1)

        staged = _pair_stages(stages)
        tok = lambda refs, base: [_from_residues(r, staged[base + i], DILATIONS[i]) for i, r in enumerate(refs)]
        chunks = [du_ref[...]]
        chunks += [unrot(a, HEAD_DIM ** -0.5) for a in tok((q0, q1, q2), 0)]
        chunks += [unrot(a, 1.0) for a in tok((k0, k1, k2), 3)]
        chunks += tok((v0, v1, v2), 6)
        acc = jnp.zeros((t, D_MODEL), F32)
        for ci, ch in enumerate(chunks):
            cols = slice(ci * GROUP_WIDTH, (ci + 1) * GROUP_WIDTH)
            cb = ch.astype(BF16)
            dz_ref[:, cols] = cb
            acc = acc + _dot(cb, w_ref[cols, :])
        gv = g_ref[...]
        n, rstd, _ = _rms(h_ref[...], gv)
        dx, dg = _rms_bwd(acc, n, rstd, gv)
        out_ref[...] = dh_ref[...] + dx
        dg_ref[...] += dg

    row = lambda w: pl.BlockSpec((t, w), lambda i: (i, 0))
    vec = pl.BlockSpec((1, D_MODEL), lambda i: (0, 0))
    res = [_residue_spec(dil, t) for dil in DILATIONS]
    return pl.pallas_call(
        body, name=name, grid=(s // t,),
        in_specs=[row(D_MODEL), row(POOL_WIDTH)] + res * 3 + _table_specs(t)
        + [pl.BlockSpec((N_IN, D_MODEL), lambda i: (0, 0)), row(D_MODEL), vec],
        out_specs=[row(D_MODEL), row(N_IN), vec],
        out_shape=[jax.ShapeDtypeStruct((s, D_MODEL), F32), jax.ShapeDtypeStruct((s, N_IN), BF16),
                   jax.ShapeDtypeStruct((1, D_MODEL), F32)],
        scratch_shapes=_stages(t, 9),
        compiler_params=_params(1),
    )(dh, du, *dq, *dk, *dv, rc, rsa, rsb, w_in, h, g)


def _matmul_tn(a, b, name, *, square_a=False, tm=None, tn=None, blocked_out=False, after=()):
    s, m = a.shape
    n = b.shape[1]
    tk = _row_tile(s, 2048)
    tm = tm or min(m, 1024)
    tn = tn or min(n, 1024)
    assert m % tm == 0 and n % tn == 0
    nk = s // tk
    nsub = tn // FF_BLOCK if blocked_out else 1

    def body(a_ref, b_ref, o_ref, ob_ref, acc):
        k = pl.program_id(2)

        def product():
            av = a_ref[...]
            if square_a:
                av = av.astype(F32)
                av = av * av
            return _dot_tn(av.astype(BF16), b_ref[...].astype(BF16))

        def emit(total):
            if blocked_out:
                for sub in range(nsub):
                    cols = slice(sub * FF_BLOCK, (sub + 1) * FF_BLOCK)
                    o_ref[sub] = total[:, cols]
                    ob_ref[sub] = total[:, cols].astype(BF16)
            else:
                o_ref[...] = total
                ob_ref[...] = total.astype(BF16)

        if nk == 1:
            emit(product())
            return

        @pl.when(k == 0)
        def _():
            acc[...] = product()

        @pl.when((k > 0) & (k < nk - 1))
        def _():
            acc[...] += product()

        @pl.when(k == nk - 1)
        def _():
            emit(acc[...] + product())

    if blocked_out:
        shape = (n // FF_BLOCK, m, FF_BLOCK)
        out_spec = pl.BlockSpec((nsub, tm, FF_BLOCK), lambda i, j, k: (j, i, 0))
    else:
        shape = (m, n)
        out_spec = pl.BlockSpec((tm, tn), lambda i, j, k: (i, j))
    return pl.pallas_call(
        _ordered_after(body, 2, after), name=name, grid=(m // tm, n // tn, nk),
        in_specs=[pl.BlockSpec((tk, tm), lambda i, j, k: (k, i)), pl.BlockSpec((tk, tn), lambda i, j, k: (k, j))]
        + [pl.BlockSpec(memory_space=pl.ANY)] * len(after),
        out_specs=[out_spec, out_spec],
        out_shape=[jax.ShapeDtypeStruct(shape, F32), jax.ShapeDtypeStruct(shape, BF16)],
        scratch_shapes=[pltpu.VMEM((tm, tn), F32)],
        compiler_params=_params(3),
    )(a, b, *after)


def _adamw_math(w, g, m, v):
    m = ADAM_B1 * m + (1.0 - ADAM_B1) * g
    v = ADAM_B2 * v + (1.0 - ADAM_B2) * (g * g)
    m_hat = m / (1.0 - ADAM_B1 ** ADAM_STEP)
    v_hat = v / (1.0 - ADAM_B2 ** ADAM_STEP)
    delta = -ADAM_LR * (m_hat / (jnp.sqrt(v_hat) + ADAM_EPS) + ADAM_WD * w)
    return delta, m, v


def _sum_chunks_body(own0_ref, own1_ref, r0_ref, r1_ref):
    layer0 = pl.program_id(0) == 0
    g = jnp.where(layer0, own0_ref[...], own1_ref[...])
    for k in range(N_DEV - 1):
        g = g + jnp.where(layer0, r0_ref[k], r1_ref[k]).astype(F32)
    return g


def _chunk_specs(t, cols):
    rows_of = lambda layer: (lambda l, i: jnp.where(l == layer, i, 0))
    blk = pl.BlockSpec((None, t, cols), lambda l, i, me: (l, i, 0))
    own = [pl.BlockSpec((None, t, cols), functools.partial(lambda l, i, me, pick: (me[0], pick(l, i), 0), pick=rows_of(ly)))
           for ly in range(2)]
    recv = [pl.BlockSpec((N_DEV - 1, t, cols), functools.partial(lambda l, i, me, pick: (0, pick(l, i), 0), pick=rows_of(ly)))
            for ly in range(2)]
    return blk, own + recv


def _sum_chunks(chunks, me, name):
    _, rows, cols = chunks[0].shape
    t = _row_tile(rows, 320)

    def body(me_ref, own0_ref, own1_ref, r0_ref, r1_ref, g_ref):
        g_ref[...] = _sum_chunks_body(own0_ref, own1_ref, r0_ref, r1_ref)

    blk, chunk_specs = _chunk_specs(t, cols)
    return pl.pallas_call(
        body, name=name,
        grid_spec=pltpu.PrefetchScalarGridSpec(num_scalar_prefetch=1, grid=(2, rows // t), in_specs=chunk_specs,
                                               out_specs=blk),
        out_shape=jax.ShapeDtypeStruct((2, rows, cols), F32), compiler_params=_params(2),
    )(me, *chunks)


def _adamw_sharded(w, m, v, grad, me, name):
    _, rows, cols = w.shape
    t = _row_tile(rows, 256)
    summed = not isinstance(grad, tuple)
    grad = (grad,) if summed else grad

    def body(me_ref, w_ref, m_ref, v_ref, *refs):
        g_ref, d_ref, nm_ref, nv_ref = refs[-4:]
        g = refs[0][...] if summed else _sum_chunks_body(*refs[:4])
        g_ref[...] = g
        d_ref[...], nm_ref[...], nv_ref[...] = _adamw_math(w_ref[...], g, m_ref[...], v_ref[...])

    blk, chunk_specs = _chunk_specs(t, cols)
    return pl.pallas_call(
        body, name=name,
        grid_spec=pltpu.PrefetchScalarGridSpec(
            num_scalar_prefetch=1, grid=(2, rows // t),
            in_specs=[blk, blk, blk] + ([blk] if summed else chunk_specs), out_specs=[blk] * 4),
        out_shape=[jax.ShapeDtypeStruct(w.shape, F32)] * 4,
        compiler_params=_params(2),
    )(me, w, m, v, *grad)


def _adamw_packed(w, g8, m, v, name):
    def body(w_ref, g_ref, m_ref, v_ref, go_ref, d_ref, nm_ref, nv_ref):
        g = g_ref[0]
        for dev in range(1, N_DEV):
            g = g + g_ref[dev]
        go_ref[...] = g
        d_ref[...], nm_ref[...], nv_ref[...] = _adamw_math(w_ref[...], g, m_ref[...], v_ref[...])

    return pl.pallas_call(
        body, name=name, out_shape=[jax.ShapeDtypeStruct(w.shape, F32)] * 4,
        compiler_params=pltpu.CompilerParams(vmem_limit_bytes=VMEM_LIMIT),
    )(w, g8, m, v)


def _peer(k):
    x, y, c = lax.axis_index("x"), lax.axis_index("y"), lax.axis_index("c")
    return (1 - x if k & 4 else x, 1 - y if k & 2 else y, 1 - c if k & 1 else c)


def _linear(dev):
    return 4 * dev[0] + 2 * dev[1] + dev[2]


HBM_SPEC = pl.BlockSpec(memory_space=pltpu.HBM)
SEM_SPEC = pl.BlockSpec(memory_space=pltpu.SEMAPHORE)
ANY_SPEC = pl.BlockSpec(memory_space=pl.ANY)
EFFECT = pltpu.SideEffectType.DATAFLOW_SIDE_EFFECTING


def _in_hbm(a):
    return pltpu.with_memory_space_constraint(a, pltpu.HBM)


class _Exchange:
    def __init__(self, name, groups, scatter, after=()):
        self.name, self.scatter = name, scatter
        self.sizes = sizes = [len(g) for g in groups]
        srcs = [a for g in groups for a in g]
        n, ng = len(srcs), len(groups)
        lead = (N_DEV - 1,) if scatter else (N_DEV,)
        shapes = [lead + (a.shape[1:] if scatter else a.shape) for a in srcs]
        lands = [lax.empty(sh, a.dtype) for sh, a in zip(shapes, srcs)]
        offsets = [sum(sizes[:gi]) for gi in range(ng)]
        copy = self._copy

        def body(*refs):
            src, land = refs[:n], refs[n:2 * n]
            sems = refs[2 * n + len(after):2 * n + len(after) + 2 * ng]
            token = refs[-1]
            for gi in range(ng):
                for wi in range(sizes[gi]):
                    w = offsets[gi] + wi
                    for k in range(1, N_DEV):
                        copy(src[w], land[w], sems[2 * gi], sems[2 * gi + 1], wi, k).start()
            token[...] = jnp.zeros_like(token)

        sem_shapes = [pltpu.SemaphoreType.DMA((7 * sz,)) for sz in sizes for _ in range(2)]
        outs = pl.pallas_call(
            body, name=name + "_start",
            in_specs=[HBM_SPEC] * (2 * n) + [ANY_SPEC] * len(after),
            out_specs=[SEM_SPEC] * (2 * ng) + [HBM_SPEC] * (2 * n) + [pl.BlockSpec(memory_space=pltpu.VMEM)],
            out_shape=sem_shapes + [pltpu.HBM(a.shape, a.dtype) for a in srcs + lands]
            + [jax.ShapeDtypeStruct((8, LANES), F32)],
            input_output_aliases={i: 2 * ng + i for i in range(2 * n)},
            compiler_params=pltpu.CompilerParams(has_side_effects=EFFECT),
        )(*[_in_hbm(a) for a in srcs + lands], *after)
        self.sems = [outs[2 * gi:2 * gi + 2] for gi in range(ng)]
        thru = outs[2 * ng:2 * ng + 2 * n]
        self.srcs = [thru[offsets[gi]:offsets[gi] + sizes[gi]] for gi in range(ng)]
        self.lands = [thru[n + offsets[gi]:n + offsets[gi] + sizes[gi]] for gi in range(ng)]
        self.token = outs[-1]

    def _copy(self, src, land, send_sems, recv_sems, wi, k):
        to = _peer(k)
        if self.scatter:
            src_ref, dst_ref = src.at[_linear(to)], land.at[k - 1]
        else:
            src_ref, dst_ref = src, land.at[_linear(_peer(0))]
        return pltpu.make_async_remote_copy(
            src_ref=src_ref, dst_ref=dst_ref, send_sem=send_sems.at[7 * wi + k - 1],
            recv_sem=recv_sems.at[7 * wi + k - 1], device_id=to, device_id_type=MESH)

    def wait(self, gi, after):
        n = self.sizes[gi]
        copy = self._copy

        def body(*refs):
            src, land = refs[:n], refs[n:2 * n]
            send_sems, recv_sems = refs[2 * n], refs[2 * n + 1]
            for wi in range(n):
                for k in range(1, N_DEV):
                    cp = copy(src[wi], land[wi], send_sems, recv_sems, wi, k)
                    cp.wait_send()
                    cp.wait_recv()

        arrays = list(self.srcs[gi]) + list(self.lands[gi])
        outs = pl.pallas_call(
            body, name=f"{self.name}_wait{gi}",
            in_specs=[HBM_SPEC] * (2 * n) + [SEM_SPEC, SEM_SPEC] + [ANY_SPEC] * len(after),
            out_specs=[HBM_SPEC] * (2 * n),
            out_shape=[pltpu.HBM(a.shape, a.dtype) for a in arrays],
            input_output_aliases={i: i for i in range(2 * n)},
            compiler_params=pltpu.CompilerParams(has_side_effects=EFFECT),
        )(*arrays, *self.sems[gi], *after)
        return outs[:n], outs[n:]


def _rotary_tables(positions):
    rot_dim = HEAD_DIM // 4
    inv_freq = ROPE_THETA ** (-jnp.arange(0, rot_dim, 2, dtype=F32) / rot_dim)
    ang = positions.astype(F32)[:, None] * inv_freq
    cs = jnp.concatenate([jnp.cos(ang), jnp.sin(ang)], axis=1)
    dim = jnp.arange(LANES) % HEAD_DIM
    first, second = dim < ROT_SHIFT, (dim >= ROT_SHIFT) & (dim < rot_dim)
    src = jnp.arange(2 * ROT_SHIFT)[:, None]
    angle = (dim % ROT_SHIFT)[None, :]
    c = jnp.where((first | second)[None, :] & (src == angle), 1.0, 0.0)
    sa = jnp.where(second[None, :] & (src == angle + ROT_SHIFT), 1.0, 0.0)
    sb = jnp.where(first[None, :] & (src == angle + ROT_SHIFT), -1.0, 0.0)
    spread = jnp.concatenate([c, sa, sb], axis=1).astype(F32)
    base = jnp.concatenate([jnp.where(first | second, 0.0, 1.0), jnp.zeros((2 * LANES,))]).astype(F32)[None, :]
    return jnp.dot(cs, spread, precision=lax.Precision.HIGHEST, preferred_element_type=F32) + base


def _block_diag(pool_w):
    gc = pool_w.shape[-1]
    out = jnp.zeros((POOL_WIDTH, POOL_WIDTH), pool_w.dtype)
    for grp in range(pool_w.shape[0]):
        out = lax.dynamic_update_slice(out, pool_w[grp], (grp * gc, grp * gc))
    return out


def _diag_blocks(a):
    gc = POOL_WIDTH // len(POOL_WINDOWS)
    return jnp.stack([a[grp * gc:(grp + 1) * gc, grp * gc:(grp + 1) * gc] for grp in range(len(POOL_WINDOWS))])


def _local_step(x, p, positions, loss_target, norm1, pool_w, pool_scale, norm2, norm3, final_norm, weights, send):
    rc = rsa = rsb = _rotary_tables(positions)
    ones_bd = _block_diag(jnp.ones((4, HEAD_DIM, HEAD_DIM), BF16))
    saved = []
    h = x
    for i in range(2):
        tag = f"_l{i}"
        g1, g2, g3 = norm1[i:i + 1], norm2[i:i + 1], norm3[i:i + 1]
        w_bd = _block_diag(pool_w[i]).astype(BF16)
        scale = pool_scale[i:i + 1]
        w_in = weights(i, "in", (h, rc, w_bd))
        hn1, u, *qkv = _normproj_fwd(h, g1, w_in, rc, rsa, rsb, "normproj_fwd" + tag)
        qkv = [qkv[3 * grp:3 * grp + 3] for grp in range(3)]
        started = weights(i, "prefetch", (hn1,))
        pool_out, y = _pool_fwd(u, w_bd, scale, "pool_fwd" + tag, after=started)
        o, lse = zip(*[_attn_fwd(*qkv[grp], f"attn_fwd{tag}_g{grp}", after=started) for grp in range(3)])
        w_out = weights(i, "out", (pool_out, *o))
        h1, a = _outproj_fwd(h, pool_out, o, lse, w_out, "outproj_fwd" + tag)
        w_up, w_down = weights(i, "mlp", (h1,))
        h2, hn2, r = _mlp_fwd(h1, g2, w_up, w_down, "mlp_fwd" + tag)
        w_gate, w_ple = weights(i, "gate", (h2,))
        h0 = h
        if i == 0:
            h, hn3, gate, pb = _gate_fwd(h2, g3, w_gate, p, i, w_ple, "gate_fwd" + tag)
        else:
            hn3, gate, pb, loss, dh, d_final = _gate_fwd(h2, g3, w_gate, p, i, w_ple, "gate_fwd" + tag,
                                                         head=(final_norm.reshape(1, D_MODEL), loss_target))
        saved.append(dict(h0=h0, hn1=hn1, qkv=qkv, y=y, o=o, lse=lse, a=a, h1=h1, hn2=hn2, r=r, h2=h2,
                          hn3=hn3, gate=gate, pb=pb, w_bd=w_bd, scale=scale, g1=g1, g2=g2, g3=g3,
                          w_in=w_in, w_out=w_out, w_up=w_up, w_down=w_down, w_gate=w_gate, w_ple=w_ple))

    grads = [None, None]
    sent = ()
    for i in (1, 0):
        tag = f"_l{i}"
        sv = saved[i]
        dh2, dg3, dw_gate, dw_ple = _gate_bwd(dh, sv["gate"], sv["pb"], sv["w_ple"], sv["h2"], sv["g3"], sv["w_gate"],
                                              sv["hn3"], "gate_bwd" + tag, after=sent)
        dh1, dup, dg2, dh2b = _mlp_bwd(dh2, sv["r"], sv["h1"], sv["g2"], sv["w_up"], sv["w_down"], "mlp_bwd" + tag)
        dw_down = _matmul_tn(sv["r"], dh2b, "dw_down" + tag, square_a=True)
        dw_up = _matmul_tn(sv["hn2"], dup, "dw_up" + tag, blocked_out=True)
        dpool, do0, do1, do2, de0, de1, de2, dw_out = _outproj_bwd(dh1, sv["w_out"], sv["o"], sv["lse"], ones_bd,
                                                                   sv["a"], "outproj_bwd" + tag)
        sent = send(i, "main", dict(w_gate=dw_gate, w_ple=dw_ple, w_down=dw_down, w_up=dw_up, w_out=dw_out))
        dqkv = [_attn_bwd(*sv["qkv"][grp], do_g, sv["lse"][grp], de_g, f"attn_bwd{tag}_g{grp}", after=sent)
                for grp, (do_g, de_g) in enumerate(((do0, de0), (do1, de1), (do2, de2)))]
        dq, dk, dv = zip(*dqkv)
        du, dw_bd, dscale = _pool_bwd(dpool, sv["y"], sv["w_bd"], sv["scale"], "pool_bwd" + tag, after=sent)
        dh, dz, dg1 = _normproj_bwd(dh1, du, dq, dk, dv, rc, rsa, rsb, sv["w_in"], sv["h0"], sv["g1"],
                                    "normproj_bwd" + tag)
        grads[i] = dict(norm1=dg1, norm2=dg2, norm3=dg3, pool_w=_diag_blocks(dw_bd), pool_scale=dscale)
        small_sent = send(0, "small", (grads, d_final, loss)) if i == 0 else ()
        dw_in = _matmul_tn(dz, sv["hn1"], "dw_in" + tag, tm=N_IN // 2, after=small_sent)
        sent = send(i, "in", dict(w_in=dw_in))
    return dh, sent


def _pack_small(norm1, norm2, norm3, final_norm, pool_scale, pool_w, spare=None):
    spare = jnp.zeros((1, LANES), F32) if spare is None else spare
    scale_row = jnp.concatenate([pool_scale.reshape(1, 2 * POOL_WIDTH), spare,
                                 jnp.zeros((1, D_MODEL - 2 * POOL_WIDTH - LANES), F32)], axis=1)
    return jnp.concatenate([norm1, norm2, norm3, final_norm.reshape(1, D_MODEL), scale_row,
                            pool_w.reshape(32, D_MODEL)], axis=0)


def _unpack_small(a):
    return dict(norm1=a[0:2], norm2=a[2:4], norm3=a[4:6], final_norm=a[6], pool_scale=a[7, 0:2 * POOL_WIDTH].reshape(2, POOL_WIDTH),
                pool_w=a[8:40].reshape(2, 4, HEAD_DIM, HEAD_DIM))


def _chunks_cols(a, cols):
    return a.reshape(a.shape[0], N_DEV, cols).transpose(1, 0, 2)


def _chunks_rows(a, rows):
    return a.reshape(N_DEV, rows, a.shape[1])


BIG = ("w_in", "w_out", "w_up", "w_down", "w_gate", "w_ple")
SMALL = ("norm1", "norm2", "norm3", "final_norm", "pool_scale", "pool_w")
ORDER = ("norm1", "w_in", "pool_w", "pool_scale", "w_out", "norm2", "w_up", "w_down", "norm3", "w_gate", "w_ple",
         "final_norm")


def kernel(x, p, positions, norm1, w_in, pool_w, pool_scale, w_out, norm2, w_up, w_down, norm3, w_gate, w_ple, final_norm, loss_target, m_norm1, m_w_in, m_pool_w, m_pool_scale, m_w_out, m_norm2, m_w_up, m_w_down, m_norm3, m_w_gate, m_w_ple, m_final_norm, v_norm1, v_w_in, v_pool_w, v_pool_scale, v_w_out, v_norm2, v_w_up, v_w_down, v_norm3, v_w_gate, v_w_ple, v_final_norm):
    w = dict(norm1=norm1, w_in=w_in, pool_w=pool_w, pool_scale=pool_scale, w_out=w_out, norm2=norm2, w_up=w_up,
             w_down=w_down, norm3=norm3, w_gate=w_gate, w_ple=w_ple, final_norm=final_norm)
    m = dict(norm1=m_norm1, w_in=m_w_in, pool_w=m_pool_w, pool_scale=m_pool_scale, w_out=m_w_out, norm2=m_norm2,
             w_up=m_w_up, w_down=m_w_down, norm3=m_norm3, w_gate=m_w_gate, w_ple=m_w_ple, final_norm=m_final_norm)
    v = dict(norm1=v_norm1, w_in=v_w_in, pool_w=v_pool_w, pool_scale=v_pool_scale, w_out=v_w_out, norm2=v_norm2,
             w_up=v_w_up, w_down=v_w_down, norm3=v_norm3, w_gate=v_w_gate, w_ple=v_w_ple, final_norm=v_final_norm)
    seq = x.shape[1]

    bf = {n: [w[n][layer].astype(BF16) for layer in range(2)] for n in BIG}
    bf["w_in"] = [a.T for a in bf["w_in"]]
    me = 4 * lax.axis_index("x") + 2 * lax.axis_index("y") + lax.axis_index("c")
    parts = dict(zip(("in", "out", "mlp", "gate"), (("w_in",), ("w_out",), ("w_up", "w_down"), ("w_gate", "w_ple"))))
    gathers = [_Exchange("gather_l0", [[bf[n][0] for n in parts[pt]] for pt in parts], scatter=False)]
    unpack = dict(w_in=lambda a: a.reshape(N_IN, D_MODEL),
                  w_out=lambda a: a.reshape(D_MODEL, D_MODEL), w_gate=lambda a: a.reshape(D_MODEL, D_MODEL),
                  w_ple=lambda a: a.transpose(1, 0, 2).reshape(PLE_DIM, D_MODEL), w_up=lambda a: a, w_down=lambda a: a)

    def weights(layer, part, after):
        if part == "prefetch":
            if layer != 0:
                return ()
            gathers.append(_Exchange("gather_l1", [[bf[n][1] for n in parts[pt]] for pt in parts], scatter=False,
                                     after=after))
            return (gathers[1].token,)
        shards, lands = gathers[layer].wait(tuple(parts).index(part), after)
        full = [unpack[n](lax.dynamic_update_slice_in_dim(land, shard[None], me, axis=0))
                for n, shard, land in zip(parts[part], shards, lands)]
        return full if len(full) > 1 else full[0]

    to_chunks = dict(w_in=lambda a: _chunks_rows(a, N_IN // N_DEV),
                     w_out=lambda a: _chunks_rows(a, D_MODEL // N_DEV),
                     w_up=lambda a: a, w_down=lambda a: _chunks_rows(a, FF_BLOCK),
                     w_gate=lambda a: _chunks_rows(a, D_MODEL // N_DEV), w_ple=lambda a: _chunks_cols(a, D_MODEL // N_DEV))
    own = {n: [None, None] for n in BIG}
    scatters = {}

    def send(layer, part, grads):
        if part == "small":
            per_layer, d_final, loss = grads
            pack = _pack_small(
                *[jnp.concatenate([per_layer[0][n], per_layer[1][n]], axis=0) for n in ("norm1", "norm2", "norm3")],
                d_final.reshape(D_MODEL),
                jnp.concatenate([per_layer[0]["pool_scale"], per_layer[1]["pool_scale"]], axis=0),
                jnp.stack([per_layer[0]["pool_w"], per_layer[1]["pool_w"]]), spare=loss)
            scatters["small"] = _Exchange("gather_small", [[pack]], scatter=False)
            return (scatters["small"].token,)
        for n, (g32, _) in grads.items():
            own[n][layer] = to_chunks[n](g32)
        ex = _Exchange(f"scatter_{part}_l{layer}", [[to_chunks[n](g16) for n, (_, g16) in grads.items()]], scatter=True)
        scatters[layer, part] = (tuple(grads), ex)
        return (ex.token,)

    dx, sent = _local_step(
        x.reshape(seq, D_MODEL), p.reshape(2, seq, PLE_DIM), positions.reshape(seq), loss_target.reshape(seq, D_MODEL),
        norm1, pool_w, pool_scale, norm2, norm3, final_norm, weights, send)

    g_out, d_out, m_out, v_out = {}, {}, {}, {}
    my_index = me.reshape(1)
    for part in ("main", "in"):
        recv = {}
        for layer in (1, 0):
            names, ex = scatters[layer, part]
            for n, r in zip(names, ex.wait(0, sent)[1]):
                recv[n, layer] = r
        for n in names:
            grad = (*own[n], recv[n, 0], recv[n, 1])
            if n == "w_in":
                grad = _sum_chunks(grad, my_index, "sum_w_in").transpose(0, 2, 1)
            g_out[n], d_out[n], m_out[n], v_out[n] = _adamw_sharded(w[n], m[n], v[n], grad, my_index, "adamw_" + n)
        sent = tuple(d_out[n] for n in names)
    (mine,), (landed,) = scatters["small"].wait(0, sent)
    small_g8 = lax.dynamic_update_slice_in_dim(landed, mine[None], me, axis=0)
    pack = lambda t: _pack_small(*[t[n] for n in SMALL])
    small_g, d_small, m_small, v_small = _adamw_packed(pack(w), small_g8, pack(m), pack(v), "adamw_small")
    for dst, a in ((g_out, small_g), (d_out, d_small), (m_out, m_small), (v_out, v_small)):
        dst.update(_unpack_small(a))

    return (small_g[7, 2 * POOL_WIDTH],dx.reshape(1, seq, D_MODEL), *[g_out[n] for n in ORDER], *[d_out[n] for n in ORDER],
            *[m_out[n] for n in ORDER], *[v_out[n] for n in ORDER])
```

```python
import functools

import jax
import jax.numpy as jnp
from jax import lax
from jax.experimental import pallas as pl
from jax.experimental.pallas import tpu as pltpu

F32 = jnp.float32
BF16 = jnp.bfloat16

D_MODEL = 1024
HEAD_DIM = 64
POOL_WIDTH = 256
POOL_WINDOWS = (2, 4, 8, 16)
POOL_HALO = 16
POOL_PAD = 8
GROUP_WIDTH = 256
DILATIONS = (1, 4, 16)
ATTN_BLOCK = 128
ROT_SHIFT = 8
ROPE_THETA = 500000.0
D_FF = 4096
FF_BLOCK = 512
FF_PER_STEP = 2
MLP_BWD_TILE = 512
FWD_TILE = 1024
N_DEV = 8
N_IN = POOL_WIDTH + 3 * 768
PLE_DIM = 256
EPS = 1e-6
NEG_BIG = -1e30

ADAM_LR = 0.001
ADAM_B1 = 0.9
ADAM_B2 = 0.999
ADAM_EPS = 1e-08
ADAM_WD = 0.01
ADAM_STEP = 10

LANES = 128
VMEM_LIMIT = 56 * 1024 * 1024
MESH = pl.DeviceIdType.MESH


def _params(n_grid):
    return pltpu.CompilerParams(dimension_semantics=("arbitrary",) * n_grid, vmem_limit_bytes=VMEM_LIMIT)


def _dot(a, b):
    return jnp.dot(a, b, preferred_element_type=F32)


def _dot_nt(a, b):
    return lax.dot_general(a, b, (((1,), (1,)), ((), ())), preferred_element_type=F32)


def _dot_tn(a, b):
    return lax.dot_general(a, b, (((0,), (0,)), ((), ())), preferred_element_type=F32)


def _rms(x, g):
    rstd = lax.rsqrt(jnp.mean(x * x, axis=-1, keepdims=True) + EPS)
    n = x * rstd
    return n, rstd, n * g


def _rms_bwd(dy, n, rstd, g):
    dyn = dy * g
    dx = rstd * (dyn - n * jnp.mean(dyn * n, axis=-1, keepdims=True))
    return dx, jnp.sum(dy * n, axis=0, keepdims=True)


def _ordered_after(body, n_in, after):
    if not after:
        return body
    return lambda *refs: body(*refs[:n_in], *refs[n_in + len(after):])


def _resident(shape):
    return pl.BlockSpec(shape, lambda i: (0,) * len(shape), pipeline_mode=pl.Buffered(1))


def _row_tile(s, t):
    t = min(s, t)
    assert s % t == 0
    return t


def _rot(z, c, sa, sb):
    return z * c + pltpu.roll(z, ROT_SHIFT, 1) * sa + pltpu.roll(z, LANES - ROT_SHIFT, 1) * sb


def _table_specs(t):
    return [pl.BlockSpec((t, LANES), functools.partial(lambda i, k: (i, k), k=k)) for k in range(3)]


def _rot_t(dz, c, sa, sb):
    return dz * c + pltpu.roll(dz * sa, LANES - ROT_SHIFT, 1) + pltpu.roll(dz * sb, ROT_SHIFT, 1)


def _to_residues(value, stage, out_ref, dil):
    if dil == 1:
        out_ref[0] = value.astype(out_ref.dtype)
        return
    rows = value.shape[0] // dil
    for hf in range(GROUP_WIDTH // LANES):
        lanes = slice(hf * LANES, (hf + 1) * LANES)
        stage[hf][...] = value[:, lanes]
        for r in range(dil):
            out_ref[r, :, lanes] = stage[hf][pl.ds(r, rows, stride=dil), :].astype(out_ref.dtype)


def _from_residues(in_ref, stage, dil):
    if dil == 1:
        return in_ref[0].astype(F32)
    rows = in_ref.shape[1]
    for hf in range(GROUP_WIDTH // LANES):
        for r in range(dil):
            stage[hf][pl.ds(r, rows, stride=dil), :] = in_ref[r, :, hf * LANES:(hf + 1) * LANES].astype(F32)
    return jnp.concatenate([stage[0][...], stage[1][...]], axis=1)


def _residue_spec(dil, t):
    return pl.BlockSpec((dil, t // dil, GROUP_WIDTH), lambda i: (0, i, 0))


def _residue_shape(dil, s, dtype):
    return jax.ShapeDtypeStruct((dil, s // dil, GROUP_WIDTH), dtype)


def _stages(t, n):
    return [pltpu.VMEM((t, LANES), F32)] * (n * (GROUP_WIDTH // LANES))


def _pair_stages(refs):
    return [refs[i:i + 2] for i in range(0, len(refs), 2)]


def _normproj_fwd(h, g, w_in, rc, rsa, rsb, name):
    s = h.shape[0]
    t = _row_tile(s, FWD_TILE)

    def body(h_ref, g_ref, w_ref, c_ref, sa_ref, sb_ref, hn_ref, u_ref, *rest):
        qkv_refs, stages = rest[:9], _pair_stages(rest[9:])
        _, _, hn = _rms(h_ref[...], g_ref[...])
        hb = hn.astype(BF16)
        hn_ref[...] = hb
        c, sa, sb = c_ref[...], sa_ref[...], sb_ref[...]

        def rot(z, scale):
            halves = [_rot(z[:, hf * LANES:(hf + 1) * LANES], c, sa, sb) * scale for hf in range(2)]
            return jnp.concatenate(halves, axis=1)

        proj = lambda lo: _dot_nt(hb, w_ref[lo:lo + GROUP_WIDTH, :])
        u_ref[...] = proj(0)
        for grp, dil in enumerate(DILATIONS):
            lo = POOL_WIDTH + grp * GROUP_WIDTH
            q_ref, k_ref, v_ref = qkv_refs[3 * grp:3 * grp + 3]
            _to_residues(rot(proj(lo), HEAD_DIM ** -0.5), stages[0], q_ref, dil)
            _to_residues(rot(proj(lo + 768), 1.0), stages[1], k_ref, dil)
            _to_residues(proj(lo + 1536), stages[2], v_ref, dil)

    row = lambda w: pl.BlockSpec((t, w), lambda i: (i, 0))
    return pl.pallas_call(
        body, name=name, grid=(s // t,),
        in_specs=[row(D_MODEL), pl.BlockSpec((1, D_MODEL), lambda i: (0, 0)),
                  _resident((N_IN, D_MODEL))] + _table_specs(t),
        out_specs=[row(D_MODEL), row(POOL_WIDTH)] + [_residue_spec(dil, t) for dil in DILATIONS for _ in range(3)],
        out_shape=[jax.ShapeDtypeStruct((s, D_MODEL), BF16), jax.ShapeDtypeStruct((s, POOL_WIDTH), F32)]
        + [_residue_shape(dil, s, BF16) for dil in DILATIONS for _ in range(3)],
        scratch_shapes=_stages(t, 3),
        compiler_params=_params(1),
    )(h, g, w_in, rc, rsa, rsb)


def _pool_lane_window():
    lane = lax.broadcasted_iota(jnp.int32, (1, POOL_WIDTH), 1)
    return jnp.left_shift(2, lane // (POOL_WIDTH // len(POOL_WINDOWS)))


def _window_sums(ext, b2, b4, b8, t, lo, tile, direction):
    rows = t + POOL_HALO
    for src, dst, sh in ((ext, b2, 1), (b2, b4, 2), (b4, b8, 4)):
        dst[lo:lo + rows, :] = src[lo:lo + rows, :] + src[lo + direction * sh:lo + direction * sh + rows, :]
    s16 = b8[tile:tile + t, :] + b8[tile + direction * 8:tile + direction * 8 + t, :]
    win = _pool_lane_window()
    return jnp.where(win == 2, b2[tile:tile + t, :],
                     jnp.where(win == 4, b4[tile:tile + t, :], jnp.where(win == 8, b8[tile:tile + t, :], s16)))


def _pool_fwd_tile(i, u_ref, w_ref, sc_ref, y_ref, ext, b2, b4, b8):
    t = u_ref.shape[0]
    first = POOL_PAD + POOL_HALO

    @pl.when(i == 0)
    def _():
        for buf in (ext, b2, b4):
            buf[0:POOL_PAD, :] = jnp.zeros((POOL_PAD, POOL_WIDTH), F32)
        ext[POOL_PAD:first, :] = jnp.zeros((POOL_HALO, POOL_WIDTH), F32)

    x = u_ref[...]
    ext[first:, :] = x
    wsum = _window_sums(ext, b2, b4, b8, t, POOL_PAD, first, -1)
    pos = i * t + lax.broadcasted_iota(jnp.int32, (t, POOL_WIDTH), 0)
    cnt = jnp.minimum(pos + 1, _pool_lane_window()).astype(F32)
    yb = (wsum / cnt - x).astype(BF16)
    y_ref[...] = yb
    ext[POOL_PAD:first, :] = x[t - POOL_HALO:, :]
    return _dot(yb, w_ref[...]) * sc_ref[...]


def _head_masks():
    lane = lax.broadcasted_iota(jnp.int32, (ATTN_BLOCK, GROUP_WIDTH), 1)
    return [lane // HEAD_DIM == hd for hd in range(GROUP_WIDTH // HEAD_DIM)]


def _stack_heads(a, masks):
    zero = jnp.zeros_like(a)
    return jnp.concatenate([jnp.where(m, a, zero) for m in masks], axis=0)


def _band_bias(first_step):
    rows = ATTN_BLOCK * (GROUP_WIDTH // HEAD_DIM)
    i = lax.broadcasted_iota(jnp.int32, (rows, 2 * ATTN_BLOCK), 0) & (ATTN_BLOCK - 1)
    j = lax.broadcasted_iota(jnp.int32, (rows, 2 * ATTN_BLOCK), 1)
    inner = jnp.where((j >= i) & (j <= i + ATTN_BLOCK), 0.0, NEG_BIG)
    return jnp.where((j < ATTN_BLOCK) & first_step, NEG_BIG, inner), inner


def _column_per_head(a):
    return jnp.concatenate([a[:, hd * HEAD_DIM:hd * HEAD_DIM + 1] for hd in range(GROUP_WIDTH // HEAD_DIM)], axis=0)


def _blocks_per_step(nb):
    if nb <= 16:
        return nb
    return next(qb for qb in (16, 8, 4, 2, 1) if nb % qb == 0)


def _residues_per_step(dil, nb, qb):
    return 2 if (nb == qb and qb < 8 and dil % 2 == 0) else 1


def _attn_fwd(q, k, v, name, after=()):
    dil, length, _ = q.shape
    nb = length // ATTN_BLOCK
    qb = _blocks_per_step(nb)
    rs = _residues_per_step(dil, nb, qb)

    def body(q_ref, kp_ref, kc_ref, vp_ref, vc_ref, o_ref, lse_ref):
        masks = _head_masks()
        bias = _band_bias(pl.program_id(1) == 0)
        for rr in range(rs):
            for qi in range(qb):
                here = slice(qi * ATTN_BLOCK, (qi + 1) * ATTN_BLOCK)
                before = slice((qi - 1) * ATTN_BLOCK, qi * ATTN_BLOCK)
                kcat = jnp.concatenate([kp_ref[rr] if qi == 0 else kc_ref[rr, before], kc_ref[rr, here]], axis=0)
                vcat = jnp.concatenate([vp_ref[rr] if qi == 0 else vc_ref[rr, before], vc_ref[rr, here]], axis=0)
                qs = _stack_heads(q_ref[rr, here], masks)
                sc = _dot_nt(qs, kcat) + bias[min(qi, 1)]
                m = jnp.max(sc, axis=1, keepdims=True)
                e = jnp.exp(sc - m)
                l = jnp.sum(e, axis=1, keepdims=True)
                p = (e / l).astype(BF16)
                lse = m + jnp.log(l)
                o = jnp.zeros((ATTN_BLOCK, GROUP_WIDTH), F32)
                lse_full = jnp.zeros((ATTN_BLOCK, GROUP_WIDTH), F32)
                for hd, msk in enumerate(masks):
                    rows = slice(hd * ATTN_BLOCK, (hd + 1) * ATTN_BLOCK)
                    o = jnp.where(msk, _dot(p[rows], vcat), o)
                    lse_full = jnp.where(msk, lse[rows], lse_full)
                o_ref[rr, here] = o.astype(o_ref.dtype)
                lse_ref[rr, here] = lse_full

    cur = pl.BlockSpec((rs, qb * ATTN_BLOCK, GROUP_WIDTH), lambda r, j: (r, j, 0))
    prev = pl.BlockSpec((rs, ATTN_BLOCK, GROUP_WIDTH), lambda r, j: (r, jnp.maximum(qb * j - 1, 0), 0))
    return pl.pallas_call(
        _ordered_after(body, 5, after), name=name, grid=(dil // rs, nb // qb),
        in_specs=[cur, prev, cur, prev, cur] + [pl.BlockSpec(memory_space=pl.ANY)] * len(after), out_specs=[cur, cur],
        out_shape=[jax.ShapeDtypeStruct(q.shape, BF16), jax.ShapeDtypeStruct(q.shape, F32)],
        compiler_params=_params(2),
    )(q, k, k, v, v, *after)


def _group_weights(l0, l1, l2):
    m = jnp.maximum(jnp.maximum(l0, l1), l2)
    e0, e1, e2 = jnp.exp(l0 - m), jnp.exp(l1 - m), jnp.exp(l2 - m)
    den = e0 + e1 + e2
    return e0 / den, e1 / den, e2 / den


def _outproj_fwd(h, u, w_bd, scale, o, lse, w_out, name):
    s = h.shape[0]
    t = _row_tile(s, FWD_TILE)

    def body(h_ref, u_ref, wbd_ref, sc_ref, o0, o1, o2, l0, l1, l2, w_ref, out_ref, a_ref, y_ref, ext, b2, b4, b8,
             *stages):
        pool_out = _pool_fwd_tile(pl.program_id(0), u_ref, wbd_ref, sc_ref, y_ref, ext, b2, b4, b8)
        stages = _pair_stages(stages)
        ov = [_from_residues(r, stages[i], DILATIONS[i]) for i, r in enumerate((o0, o1, o2))]
        lv = [_from_residues(r, stages[3 + i], DILATIONS[i]) for i, r in enumerate((l0, l1, l2))]
        wts = _group_weights(*lv)
        a = jnp.concatenate([pool_out] + [ov[i] * wts[i] for i in range(3)], axis=1).astype(BF16)
        a_ref[...] = a
        out_ref[...] = h_ref[...] + _dot(a, w_ref[...])

    row = lambda w: pl.BlockSpec((t, w), lambda i: (i, 0))
    res = [_residue_spec(dil, t) for dil in DILATIONS]
    return pl.pallas_call(
        body, name=name, grid=(s // t,),
        in_specs=[row(D_MODEL), row(POOL_WIDTH), _resident((POOL_WIDTH, POOL_WIDTH)), _resident((1, POOL_WIDTH))]
        + res + res + [_resident((D_MODEL, D_MODEL))],
        out_specs=[row(D_MODEL), row(D_MODEL), row(POOL_WIDTH)],
        out_shape=[jax.ShapeDtypeStruct((s, D_MODEL), F32), jax.ShapeDtypeStruct((s, D_MODEL), BF16),
                   jax.ShapeDtypeStruct((s, POOL_WIDTH), BF16)],
        scratch_shapes=[pltpu.VMEM((t + POOL_HALO + POOL_PAD, POOL_WIDTH), F32)] * 4 + _stages(t, 6),
        compiler_params=_params(1),
    )(h, u, w_bd, scale, *o, *lse, w_out)


def _mlp_fwd(h, g, w_up, w_down, name):
    s = h.shape[0]
    t = _row_tile(s, 512)
    nblk = D_FF // FF_BLOCK

    def body(h_ref, g_ref, wu_ref, wd_ref, out_ref, hn_ref, r_ref):
        x = h_ref[...]
        _, _, hn = _rms(x, g_ref[...])
        hb = hn.astype(BF16)
        hn_ref[...] = hb
        acc = None
        for b0 in range(0, nblk, FF_PER_STEP):
            acts = []
            for b in range(b0, b0 + FF_PER_STEP):
                r = jnp.maximum(_dot(hb, wu_ref[b]), 0.0)
                r_ref[:, b * FF_BLOCK:(b + 1) * FF_BLOCK] = r.astype(BF16)
                acts.append((r * r).astype(BF16))
            wd = wd_ref[b0:b0 + FF_PER_STEP].reshape(FF_PER_STEP * FF_BLOCK, D_MODEL)
            part = _dot(jnp.concatenate(acts, axis=1), wd)
            acc = part if acc is None else acc + part
        out_ref[...] = x + acc

    row = lambda w: pl.BlockSpec((t, w), lambda i: (i, 0))
    resident = lambda shape: pl.BlockSpec(shape, lambda i: (0, 0, 0), pipeline_mode=pl.Buffered(1))
    return pl.pallas_call(
        body, name=name, grid=(s // t,),
        in_specs=[row(D_MODEL), pl.BlockSpec((1, D_MODEL), lambda i: (0, 0)),
                  resident((nblk, D_MODEL, FF_BLOCK)), resident((nblk, FF_BLOCK, D_MODEL))],
        out_specs=[row(D_MODEL), row(D_MODEL), row(D_FF)],
        out_shape=[jax.ShapeDtypeStruct((s, D_MODEL), F32), jax.ShapeDtypeStruct((s, D_MODEL), BF16),
                   jax.ShapeDtypeStruct((s, D_FF), BF16)],
        compiler_params=_params(1),
    )(h, g, w_up, w_down)


def _gate_fwd(h, g, w_gate, p, layer, w_ple, name, head=None):
    s = h.shape[0]
    t = _row_tile(s, FWD_TILE)

    def body(h_ref, g_ref, wg_ref, p_ref, wp_ref, *refs):
        x = h_ref[...]
        _, _, hn = _rms(x, g_ref[...])
        hb = hn.astype(BF16)
        gate = 1.0 / (1.0 + jnp.exp(-_dot(hb, wg_ref[...])))
        pb = p_ref[...].astype(BF16)
        h3 = x + gate * _dot(pb, wp_ref[...])
        if head is None:
            out_ref, hn_ref, gate_ref, pb_ref = refs
            out_ref[...] = h3
        else:
            gf_ref, t_ref, hn_ref, gate_ref, pb_ref, loss_ref, dh_ref, dgf_ref = refs

            @pl.when(pl.program_id(0) == 0)
            def _():
                loss_ref[...] = jnp.zeros_like(loss_ref)
                dgf_ref[...] = jnp.zeros_like(dgf_ref)

            gf = gf_ref[...]
            n, rstd, y = _rms(h3, gf)
            err = y - t_ref[...]
            loss_ref[...] += jnp.sum(err * err) * (0.5 / D_MODEL)
            dh_ref[...], dgf = _rms_bwd(err * (1.0 / D_MODEL), n, rstd, gf)
            dgf_ref[...] += dgf
        hn_ref[...] = hb
        pb_ref[...] = pb
        gate_ref[...] = gate.astype(BF16)

    row = lambda w: pl.BlockSpec((t, w), lambda i: (i, 0))
    full = lambda a, b: pl.BlockSpec((a, b), lambda i: (0, 0))
    in_specs = [row(D_MODEL), full(1, D_MODEL), _resident((D_MODEL, D_MODEL)),
                pl.BlockSpec((None, t, PLE_DIM), lambda i: (layer, i, 0)), _resident((PLE_DIM, D_MODEL))]
    saved_specs = [row(D_MODEL), row(D_MODEL), row(PLE_DIM)]
    saved_shapes = [jax.ShapeDtypeStruct((s, D_MODEL), BF16), jax.ShapeDtypeStruct((s, D_MODEL), BF16),
                    jax.ShapeDtypeStruct((s, PLE_DIM), BF16)]
    if head is None:
        return pl.pallas_call(
            body, name=name, grid=(s // t,), in_specs=in_specs, out_specs=[row(D_MODEL)] + saved_specs,
            out_shape=[jax.ShapeDtypeStruct((s, D_MODEL), F32)] + saved_shapes, compiler_params=_params(1),
        )(h, g, w_gate, p, w_ple)
    return pl.pallas_call(
        body, name=name, grid=(s // t,), in_specs=in_specs + [full(1, D_MODEL), row(D_MODEL)],
        out_specs=saved_specs + [pl.BlockSpec((1, LANES), lambda i: (0, 0)), row(D_MODEL), full(1, D_MODEL)],
        out_shape=saved_shapes + [jax.ShapeDtypeStruct((1, LANES), F32), jax.ShapeDtypeStruct((s, D_MODEL), F32),
                                  jax.ShapeDtypeStruct((1, D_MODEL), F32)],
        compiler_params=_params(1),
    )(h, g, w_gate, p, w_ple, *head)


def _gate_bwd(dh, gate, pb, w_ple, h, g, w_gate, hn, name, after=()):
    s = h.shape[0]
    t = _row_tile(s, 512)
    last = s // t - 1

    def body(dh_ref, gate_ref, pb_ref, wp_ref, h_ref, g_ref, wg_ref, hn_ref, out_ref, dg_ref, dwg_ref, dwgb_ref,
             dwp_ref, dwpb_ref):
        i = pl.program_id(0)

        @pl.when(i == 0)
        def _():
            dg_ref[...] = jnp.zeros_like(dg_ref)
            dwg_ref[...] = jnp.zeros_like(dwg_ref)
            dwp_ref[...] = jnp.zeros_like(dwp_ref)

        d = dh_ref[...]
        gate = gate_ref[...].astype(F32)
        pb = pb_ref[...]
        e = _dot(pb, wp_ref[...])
        dgl = (d * e * gate * (1.0 - gate)).astype(BF16)
        dwg_ref[...] += _dot_tn(hn_ref[...], dgl)
        dwp_ref[...] += _dot_tn(pb, (d * gate).astype(BF16))
        gv = g_ref[...]
        n, rstd, _ = _rms(h_ref[...], gv)
        dx, dg = _rms_bwd(_dot_nt(dgl, wg_ref[...]), n, rstd, gv)
        out_ref[...] = d + dx
        dg_ref[...] += dg

        @pl.when(i == last)
        def _():
            dwgb_ref[...] = dwg_ref[...].astype(BF16)
            dwpb_ref[...] = dwp_ref[...].astype(BF16)

    row = lambda w: pl.BlockSpec((t, w), lambda i: (i, 0))
    full = lambda a, b: pl.BlockSpec((a, b), lambda i: (0, 0))
    dh2, dg, dwg, dwgb, dwp, dwpb = pl.pallas_call(
        _ordered_after(body, 8, after), name=name, grid=(s // t,),
        in_specs=[row(D_MODEL), row(D_MODEL), row(PLE_DIM), full(PLE_DIM, D_MODEL), row(D_MODEL), full(1, D_MODEL),
                  full(D_MODEL, D_MODEL), row(D_MODEL)] + [pl.BlockSpec(memory_space=pl.ANY)] * len(after),
        out_specs=[row(D_MODEL), full(1, D_MODEL), full(D_MODEL, D_MODEL), full(D_MODEL, D_MODEL),
                   full(PLE_DIM, D_MODEL), full(PLE_DIM, D_MODEL)],
        out_shape=[jax.ShapeDtypeStruct((s, D_MODEL), F32), jax.ShapeDtypeStruct((1, D_MODEL), F32),
                   jax.ShapeDtypeStruct((D_MODEL, D_MODEL), F32), jax.ShapeDtypeStruct((D_MODEL, D_MODEL), BF16),
                   jax.ShapeDtypeStruct((PLE_DIM, D_MODEL), F32), jax.ShapeDtypeStruct((PLE_DIM, D_MODEL), BF16)],
        compiler_params=_params(1),
    )(dh, gate, pb, w_ple, h, g, w_gate, hn, *after)
    return dh2, dg, (dwg, dwgb), (dwp, dwpb)


def _mlp_bwd(dh, r, h, g, w_up, w_down, name):
    s = h.shape[0]
    t = _row_tile(s, MLP_BWD_TILE)
    nblk = D_FF // FF_BLOCK

    def body(dh_ref, r_ref, h_ref, g_ref, wu_ref, wd_ref, out_ref, dup_ref, dg_ref, dhb_ref):
        @pl.when(pl.program_id(0) == 0)
        def _():
            dg_ref[...] = jnp.zeros_like(dg_ref)

        d = dh_ref[...]
        db = d.astype(BF16)
        dhb_ref[...] = db
        back = None
        for b in range(nblk):
            cols = slice(b * FF_BLOCK, (b + 1) * FF_BLOCK)
            dup = (_dot_nt(db, wd_ref[b]) * (2.0 * r_ref[:, cols].astype(F32))).astype(BF16)
            dup_ref[:, cols] = dup
            part = _dot_nt(dup, wu_ref[b])
            back = part if back is None else back + part
        gv = g_ref[...]
        n, rstd, _ = _rms(h_ref[...], gv)
        dx, dg = _rms_bwd(back, n, rstd, gv)
        out_ref[...] = d + dx
        dg_ref[...] += dg

    row = lambda w: pl.BlockSpec((t, w), lambda i: (i, 0))
    vec = pl.BlockSpec((1, D_MODEL), lambda i: (0, 0))
    resident = lambda shape: pl.BlockSpec(shape, lambda i: (0, 0, 0), pipeline_mode=pl.Buffered(1))
    return pl.pallas_call(
        body, name=name, grid=(s // t,),
        in_specs=[row(D_MODEL), row(D_FF), row(D_MODEL), vec,
                  resident((nblk, D_MODEL, FF_BLOCK)), resident((nblk, FF_BLOCK, D_MODEL))],
        out_specs=[row(D_MODEL), row(D_FF), vec, row(D_MODEL)],
        out_shape=[jax.ShapeDtypeStruct((s, D_MODEL), F32), jax.ShapeDtypeStruct((s, D_FF), BF16),
                   jax.ShapeDtypeStruct((1, D_MODEL), F32), jax.ShapeDtypeStruct((s, D_MODEL), BF16)],
        compiler_params=_params(1),
    )(dh, r, h, g, w_up, w_down)


def _outproj_bwd(dh, w_out, o, lse, ones_bd, a, name):
    s = dh.shape[0]
    t = _row_tile(s, 512)
    last = s // t - 1

    def body(dh_ref, w_ref, o0, o1, o2, l0, l1, l2, bd_ref, a_ref, dp_ref, do0, do1, do2, de0, de1, de2, dw_ref,
             dwb_ref, *stages):
        i = pl.program_id(0)

        @pl.when(i == 0)
        def _():
            dw_ref[...] = jnp.zeros_like(dw_ref)

        stages = _pair_stages(stages)
        dhb = dh_ref[...].astype(BF16)
        dw_ref[...] += _dot_tn(a_ref[...], dhb)

        @pl.when(i == last)
        def _():
            dwb_ref[...] = dw_ref[...].astype(BF16)

        da = _dot_nt(dhb, w_ref[...])
        dp_ref[...] = da[:, 0:POOL_WIDTH]
        ov =[_from_residues(r, stages[i], DILATIONS[i]) for i, r in enumerate((o0, o1, o2))]
        lv = [_from_residues(r, stages[3 + i], DILATIONS[i]) for i, r in enumerate((l0, l1, l2))]
        wts = _group_weights(*lv)
        bd = bd_ref[...]
        cbar = jnp.zeros((t, GROUP_WIDTH), F32)
        for grp, do_ref in enumerate((do0, do1, do2)):
            lo = POOL_WIDTH + grp * GROUP_WIDTH
            dag = da[:, lo:lo + GROUP_WIDTH]
            _to_residues(dag * wts[grp], stages[6 + grp], do_ref, DILATIONS[grp])
            prod = dag * ov[grp]
            hi = prod.astype(BF16)
            low = (prod - hi.astype(F32)).astype(BF16)
            cbar = cbar + wts[grp] * (_dot(hi, bd) + _dot(low, bd))
        for grp, de_ref in enumerate((de0, de1, de2)):
            _to_residues(wts[grp] * cbar, stages[9 + grp], de_ref, DILATIONS[grp])

    row = lambda w: pl.BlockSpec((t, w), lambda i: (i, 0))
    full = lambda a, b: pl.BlockSpec((a, b), lambda i: (0, 0))
    res = [_residue_spec(dil, t) for dil in DILATIONS]
    *outs, dw, dwb = pl.pallas_call(
        body, name=name, grid=(s // t,),
        in_specs=[row(D_MODEL), full(D_MODEL, D_MODEL)] + res + res + [full(GROUP_WIDTH, GROUP_WIDTH), row(D_MODEL)],
        out_specs=[row(POOL_WIDTH)] + res + res + [full(D_MODEL, D_MODEL)] * 2,
        out_shape=[jax.ShapeDtypeStruct((s, POOL_WIDTH), F32)] + [_residue_shape(dil, s, BF16) for dil in DILATIONS]
        + [_residue_shape(dil, s, F32) for dil in DILATIONS]
        + [jax.ShapeDtypeStruct((D_MODEL, D_MODEL), F32), jax.ShapeDtypeStruct((D_MODEL, D_MODEL), BF16)],
        scratch_shapes=_stages(t, 12),
        compiler_params=_params(1),
    )(dh, w_out, *o, *lse, ones_bd, a)
    return (*outs, (dw, dwb))


def _attn_bwd(q, k, v, do, lse, deff, name, after=()):
    dil, length, _ = q.shape
    nb = length // ATTN_BLOCK
    qb = _blocks_per_step(nb)
    nj = nb // qb
    rs = _residues_per_step(dil, nb, qb)
    whole = nj == 1
    tail = slice((qb - 1) * ATTN_BLOCK, qb * ATTN_BLOCK)
    block = lambda qi: slice(qi * ATTN_BLOCK, (qi + 1) * ATTN_BLOCK)

    def body(q_ref, kp_ref, kc_ref, vp_ref, vc_ref, do_ref, lse_ref, de_ref, dq_ref, dk_ref, dv_ref, ck, cv):
        j = pl.program_id(1)

        def compute():
            masks = _head_masks()
            bias = _band_bias(j == 0)
            for rr in range(rs):
                dkc, dvc = [], []
                for qi in range(qb):
                    here, before = block(qi), block(qi - 1)
                    kcat = jnp.concatenate([kp_ref[rr] if qi == 0 else kc_ref[rr, before], kc_ref[rr, here]], axis=0)
                    vcat = jnp.concatenate([vp_ref[rr] if qi == 0 else vc_ref[rr, before], vc_ref[rr, here]], axis=0)
                    qs = _stack_heads(q_ref[rr, here], masks)
                    dos = _stack_heads(do_ref[rr, here], masks)
                    sc = _dot_nt(qs, kcat) + bias[min(qi, 1)]
                    p = jnp.exp(sc - _column_per_head(lse_ref[rr, here]))
                    ds = (p * (_dot_nt(dos, vcat) - _column_per_head(de_ref[rr, here]))).astype(BF16)
                    dq = jnp.zeros((ATTN_BLOCK, GROUP_WIDTH), F32)
                    for hd, msk in enumerate(masks):
                        dq = jnp.where(msk, _dot(ds[block(hd)], kcat), dq)
                    dq_ref[rr, here] = dq.astype(dq_ref.dtype)
                    dkc.append(_dot_tn(ds, qs))
                    dvc.append(_dot_tn(p.astype(BF16), dos))

                for out_ref, carry, parts in ((dk_ref, ck, dkc), (dv_ref, cv, dvc)):
                    full = [parts[qi][ATTN_BLOCK:] + parts[qi + 1][0:ATTN_BLOCK] for qi in range(qb - 1)]
                    if whole:
                        for qi, val in enumerate(full + [parts[qb - 1][ATTN_BLOCK:]]):
                            out_ref[rr, block(qi)] = val.astype(out_ref.dtype)
                        continue

                    @pl.when(j > 0)
                    def _():
                        if qb > 1:
                            out_ref[0, 0:(qb - 1) * ATTN_BLOCK] = carry[0:(qb - 1) * ATTN_BLOCK].astype(out_ref.dtype)
                        out_ref[0, tail] = (carry[tail] + parts[0][0:ATTN_BLOCK]).astype(out_ref.dtype)

                    for qi, val in enumerate(full):
                        carry[block(qi)] = val
                    carry[tail] = parts[qb - 1][ATTN_BLOCK:]

        if whole:
            compute()
        else:
            pl.when(j < nj)(compute)

            @pl.when(j == nj)
            def _():
                dk_ref[0] = ck[...].astype(dk_ref.dtype)
                dv_ref[0] = cv[...].astype(dv_ref.dtype)

    step = lambda j: jnp.minimum(j, nj - 1)
    cur = pl.BlockSpec((rs, qb * ATTN_BLOCK, GROUP_WIDTH), lambda r, j: (r, step(j), 0))
    prev = pl.BlockSpec((rs, ATTN_BLOCK, GROUP_WIDTH), lambda r, j: (r, jnp.maximum(qb * step(j) - 1, 0), 0))
    late = pl.BlockSpec((rs, qb * ATTN_BLOCK, GROUP_WIDTH), lambda r, j: (r, jnp.maximum(j - 1, 0), 0))
    return pl.pallas_call(
        _ordered_after(body, 8, after), name=name, grid=(dil // rs, 1 if whole else nj + 1),
        in_specs=[cur, prev, cur, prev, cur, cur, cur, cur] + [pl.BlockSpec(memory_space=pl.ANY)] * len(after),
        out_specs=[cur, cur if whole else late, cur if whole else late],
        out_shape=[jax.ShapeDtypeStruct(q.shape, BF16)] * 3,
        scratch_shapes=[pltpu.VMEM((qb * ATTN_BLOCK, GROUP_WIDTH), F32)] * 2,
        compiler_params=_params(2),
    )(q, k, k, v, v, do, lse, deff, *after)


def _pool_bwd(dpool, y, w_bd, scale, name, after=()):
    s = dpool.shape[0]
    t = _row_tile(s, 512)
    nt = s // t

    def body(dp_ref, y_ref, w_ref, sc_ref, du_ref, dw_ref, dsc_ref, ext, b2, b4, b8):
        i = pl.program_id(0)

        @pl.when(i == 0)
        def _():
            ext[t:, :] = jnp.zeros((POOL_HALO + POOL_PAD, POOL_WIDTH), F32)
            for buf in (b2, b4):
                buf[t + POOL_HALO:, :] = jnp.zeros((POOL_PAD, POOL_WIDTH), F32)
            dw_ref[...] = jnp.zeros_like(dw_ref)
            dsc_ref[...] = jnp.zeros_like(dsc_ref)

        dp = dp_ref[...]
        yb = y_ref[...]
        w = w_ref[...]
        dsc_ref[...] += jnp.sum(dp * _dot(yb, w), axis=0, keepdims=True)
        dyo = (dp * sc_ref[...]).astype(BF16)
        dw_ref[...] += _dot_tn(yb, dyo)
        dy = _dot_nt(dyo, w)
        win = _pool_lane_window()
        pos = (nt - 1 - i) * t + lax.broadcasted_iota(jnp.int32, (t, POOL_WIDTH), 0)
        gq = dy / jnp.minimum(pos + 1, win).astype(F32)
        ext[0:t, :] = gq
        du_ref[...] = _window_sums(ext, b2, b4, b8, t, 0, 0, 1) - dy
        ext[t:t + POOL_HALO, :] = gq[0:POOL_HALO, :]

    rev = pl.BlockSpec((t, POOL_WIDTH), lambda i: (nt - 1 - i, 0))
    full = lambda a, b: pl.BlockSpec((a, b), lambda i: (0, 0))
    return pl.pallas_call(
        _ordered_after(body, 4, after), name=name, grid=(nt,),
        in_specs=[rev, rev, full(POOL_WIDTH, POOL_WIDTH), full(1, POOL_WIDTH)]
        + [pl.BlockSpec(memory_space=pl.ANY)] * len(after),
        out_specs=[rev, full(POOL_WIDTH, POOL_WIDTH), full(1, POOL_WIDTH)],
        out_shape=[jax.ShapeDtypeStruct((s, POOL_WIDTH), F32), jax.ShapeDtypeStruct((POOL_WIDTH, POOL_WIDTH), F32),
                   jax.ShapeDtypeStruct((1, POOL_WIDTH), F32)],
        scratch_shapes=[pltpu.VMEM((t + POOL_HALO + POOL_PAD, POOL_WIDTH), F32)] * 4,
        compiler_params=_params(1),
    )(dpool, y, w_bd, scale, *after)


def _normproj_bwd(dh, du, dq, dk, dv, rc, rsa, rsb, w_in, h, g, name):
    s = h.shape[0]
    t = _row_tile(s, 512)

    def body(dh_ref, du_ref, q0, q1, q2, k0, k1, k2, v0, v1, v2, c_ref, sa_ref, sb_ref, w_ref, h_ref, g_ref,
             out_ref, dz_ref, dg_ref, *stages):
        @pl.when(pl.program_id(0) == 0)
        def _():
            dg_ref[...] = jnp.zeros_like(dg_ref)

        c, sa, sb = c_ref[...], sa_ref[...], sb_ref[...]

        def unrot(a, scale):
            halves = [_rot_t(a[:, hf * LANES:(hf + 1) * LANES] * scale, c, sa, sb) for hf in range(2)]
            return jnp.concatenate(halves, axis=1)

        staged = _pair_stages(stages)
        tok = lambda refs, base: [_from_residues(r, staged[base + i], DILATIONS[i]) for i, r in enumerate(refs)]
        chunks = [du_ref[...]]
        chunks += [unrot(a, HEAD_DIM ** -0.5) for a in tok((q0, q1, q2), 0)]
        chunks += [unrot(a, 1.0) for a in tok((k0, k1, k2), 3)]
        chunks += tok((v0, v1, v2), 6)
        acc = jnp.zeros((t, D_MODEL), F32)
        for ci, ch in enumerate(chunks):
            cols = slice(ci * GROUP_WIDTH, (ci + 1) * GROUP_WIDTH)
            cb = ch.astype(BF16)
            dz_ref[:, cols] = cb
            acc = acc + _dot(cb, w_ref[cols, :])
        gv = g_ref[...]
        n, rstd, _ = _rms(h_ref[...], gv)
        dx, dg = _rms_bwd(acc, n, rstd, gv)
        out_ref[...] = dh_ref[...] + dx
        dg_ref[...] += dg

    row = lambda w: pl.BlockSpec((t, w), lambda i: (i, 0))
    vec = pl.BlockSpec((1, D_MODEL), lambda i: (0, 0))
    res = [_residue_spec(dil, t) for dil in DILATIONS]
    return pl.pallas_call(
        body, name=name, grid=(s // t,),
        in_specs=[row(D_MODEL), row(POOL_WIDTH)] + res * 3 + _table_specs(t)
        + [pl.BlockSpec((N_IN, D_MODEL), lambda i: (0, 0)), row(D_MODEL), vec],
        out_specs=[row(D_MODEL), row(N_IN), vec],
        out_shape=[jax.ShapeDtypeStruct((s, D_MODEL), F32), jax.ShapeDtypeStruct((s, N_IN), BF16),
                   jax.ShapeDtypeStruct((1, D_MODEL), F32)],
        scratch_shapes=_stages(t, 9),
        compiler_params=_params(1),
    )(dh, du, *dq, *dk, *dv, rc, rsa, rsb, w_in, h, g)


def _matmul_tn(a, b, name, *, square_a=False, tm=None, tn=None, blocked_out=False, after=()):
    s, m = a.shape
    n = b.shape[1]
    tk = _row_tile(s, 2048)
    tm = tm or min(m, 1024)
    tn = tn or min(n, 1024)
    assert m % tm == 0 and n % tn == 0
    nk = s // tk
    nsub = tn // FF_BLOCK if blocked_out else 1

    def body(a_ref, b_ref, o_ref, ob_ref, acc):
        k = pl.program_id(2)

        def product():
            av = a_ref[...]
            if square_a:
                av = av.astype(F32)
                av = av * av
            return _dot_tn(av.astype(BF16), b_ref[...].astype(BF16))

        def emit(total):
            if blocked_out:
                for sub in range(nsub):
                    cols = slice(sub * FF_BLOCK, (sub + 1) * FF_BLOCK)
                    o_ref[sub] = total[:, cols]
                    ob_ref[sub] = total[:, cols].astype(BF16)
            else:
                o_ref[...] = total
                ob_ref[...] = total.astype(BF16)

        if nk == 1:
            emit(product())
            return

        @pl.when(k == 0)
        def _():
            acc[...] = product()

        @pl.when((k > 0) & (k < nk - 1))
        def _():
            acc[...] += product()

        @pl.when(k == nk - 1)
        def _():
            emit(acc[...] + product())

    if blocked_out:
        shape = (n // FF_BLOCK, m, FF_BLOCK)
        out_spec = pl.BlockSpec((nsub, tm, FF_BLOCK), lambda i, j, k: (j, i, 0))
    else:
        shape = (m, n)
        out_spec = pl.BlockSpec((tm, tn), lambda i, j, k: (i, j))
    return pl.pallas_call(
        _ordered_after(body, 2, after), name=name, grid=(m // tm, n // tn, nk),
        in_specs=[pl.BlockSpec((tk, tm), lambda i, j, k: (k, i)), pl.BlockSpec((tk, tn), lambda i, j, k: (k, j))]
        + [pl.BlockSpec(memory_space=pl.ANY)] * len(after),
        out_specs=[out_spec, out_spec],
        out_shape=[jax.ShapeDtypeStruct(shape, F32), jax.ShapeDtypeStruct(shape, BF16)],
        scratch_shapes=[pltpu.VMEM((tm, tn), F32)],
        compiler_params=_params(3),
    )(a, b, *after)


def _adamw_math(w, g, m, v):
    m = ADAM_B1 * m + (1.0 - ADAM_B1) * g
    v = ADAM_B2 * v + (1.0 - ADAM_B2) * (g * g)
    m_hat = m / (1.0 - ADAM_B1 ** ADAM_STEP)
    v_hat = v / (1.0 - ADAM_B2 ** ADAM_STEP)
    delta = -ADAM_LR * (m_hat / (jnp.sqrt(v_hat) + ADAM_EPS) + ADAM_WD * w)
    return delta, m, v


def _sum_chunks_body(own0_ref, own1_ref, r0_ref, r1_ref):
    layer0 = pl.program_id(0) == 0
    g = jnp.where(layer0, own0_ref[...], own1_ref[...])
    for k in range(N_DEV - 1):
        g = g + jnp.where(layer0, r0_ref[k], r1_ref[k]).astype(F32)
    return g


def _chunk_specs(t, cols):
    rows_of = lambda layer: (lambda l, i: jnp.where(l == layer, i, 0))
    blk = pl.BlockSpec((None, t, cols), lambda l, i, me: (l, i, 0))
    own = [pl.BlockSpec((None, t, cols), functools.partial(lambda l, i, me, pick: (me[0], pick(l, i), 0), pick=rows_of(ly)))
           for ly in range(2)]
    recv = [pl.BlockSpec((N_DEV - 1, t, cols), functools.partial(lambda l, i, me, pick: (0, pick(l, i), 0), pick=rows_of(ly)))
            for ly in range(2)]
    return blk, own + recv


def _sum_chunks(chunks, me, name):
    _, rows, cols = chunks[0].shape
    t = _row_tile(rows, 320)

    def body(me_ref, own0_ref, own1_ref, r0_ref, r1_ref, g_ref):
        g_ref[...] = _sum_chunks_body(own0_ref, own1_ref, r0_ref, r1_ref)

    blk, chunk_specs = _chunk_specs(t, cols)
    return pl.pallas_call(
        body, name=name,
        grid_spec=pltpu.PrefetchScalarGridSpec(num_scalar_prefetch=1, grid=(2, rows // t), in_specs=chunk_specs,
                                               out_specs=blk),
        out_shape=jax.ShapeDtypeStruct((2, rows, cols), F32), compiler_params=_params(2),
    )(me, *chunks)


def _adamw_sharded(w, m, v, grad, me, name):
    _, rows, cols = w.shape
    t = _row_tile(rows, 256)
    summed = not isinstance(grad, tuple)
    grad = (grad,) if summed else grad

    def body(me_ref, w_ref, m_ref, v_ref, *refs):
        g_ref, d_ref, nm_ref, nv_ref = refs[-4:]
        g = refs[0][...] if summed else _sum_chunks_body(*refs[:4])
        g_ref[...] = g
        d_ref[...], nm_ref[...], nv_ref[...] = _adamw_math(w_ref[...], g, m_ref[...], v_ref[...])

    blk, chunk_specs = _chunk_specs(t, cols)
    return pl.pallas_call(
        body, name=name,
        grid_spec=pltpu.PrefetchScalarGridSpec(
            num_scalar_prefetch=1, grid=(2, rows // t),
            in_specs=[blk, blk, blk] + ([blk] if summed else chunk_specs), out_specs=[blk] * 4),
        out_shape=[jax.ShapeDtypeStruct(w.shape, F32)] * 4,
        compiler_params=_params(2),
    )(me, w, m, v, *grad)


def _adamw_packed(w, g8, m, v, name):
    def body(w_ref, g_ref, m_ref, v_ref, go_ref, d_ref, nm_ref, nv_ref):
        g = g_ref[0]
        for dev in range(1, N_DEV):
            g = g + g_ref[dev]
        go_ref[...] = g
        d_ref[...], nm_ref[...], nv_ref[...] = _adamw_math(w_ref[...], g, m_ref[...], v_ref[...])

    return pl.pallas_call(
        body, name=name, out_shape=[jax.ShapeDtypeStruct(w.shape, F32)] * 4,
        compiler_params=pltpu.CompilerParams(vmem_limit_bytes=VMEM_LIMIT),
    )(w, g8, m, v)


def _peer(k):
    x, y, c = lax.axis_index("x"), lax.axis_index("y"), lax.axis_index("c")
    return (1 - x if k & 4 else x, 1 - y if k & 2 else y, 1 - c if k & 1 else c)


def _linear(dev):
    return 4 * dev[0] + 2 * dev[1] + dev[2]


HBM_SPEC = pl.BlockSpec(memory_space=pltpu.HBM)
SEM_SPEC = pl.BlockSpec(memory_space=pltpu.SEMAPHORE)
ANY_SPEC = pl.BlockSpec(memory_space=pl.ANY)
EFFECT = pltpu.SideEffectType.DATAFLOW_SIDE_EFFECTING


def _in_hbm(a):
    return pltpu.with_memory_space_constraint(a, pltpu.HBM)


class _Exchange:
    def __init__(self, name, groups, scatter, after=()):
        self.name, self.scatter = name, scatter
        self.sizes = sizes = [len(g) for g in groups]
        srcs = [a for g in groups for a in g]
        n, ng = len(srcs), len(groups)
        lead = (N_DEV - 1,) if scatter else (N_DEV,)
        shapes = [lead + (a.shape[1:] if scatter else a.shape) for a in srcs]
        lands = [lax.empty(sh, a.dtype) for sh, a in zip(shapes, srcs)]
        offsets = [sum(sizes[:gi]) for gi in range(ng)]
        copy = self._copy

        def body(*refs):
            src, land = refs[:n], refs[n:2 * n]
            sems = refs[2 * n + len(after):2 * n + len(after) + 2 * ng]
            token = refs[-1]
            for gi in range(ng):
                for wi in range(sizes[gi]):
                    w = offsets[gi] + wi
                    for k in range(1, N_DEV):
                        copy(src[w], land[w], sems[2 * gi], sems[2 * gi + 1], wi, k).start()
            token[...] = jnp.zeros_like(token)

        sem_shapes = [pltpu.SemaphoreType.DMA((7 * sz,)) for sz in sizes for _ in range(2)]
        outs = pl.pallas_call(
            body, name=name + "_start",
            in_specs=[HBM_SPEC] * (2 * n) + [ANY_SPEC] * len(after),
            out_specs=[SEM_SPEC] * (2 * ng) + [HBM_SPEC] * (2 * n) + [pl.BlockSpec(memory_space=pltpu.VMEM)],
            out_shape=sem_shapes + [pltpu.HBM(a.shape, a.dtype) for a in srcs + lands]
            + [jax.ShapeDtypeStruct((8, LANES), F32)],
            input_output_aliases={i: 2 * ng + i for i in range(2 * n)},
            compiler_params=pltpu.CompilerParams(has_side_effects=EFFECT),
        )(*[_in_hbm(a) for a in srcs + lands], *after)
        self.sems = [outs[2 * gi:2 * gi + 2] for gi in range(ng)]
        thru = outs[2 * ng:2 * ng + 2 * n]
        self.srcs = [thru[offsets[gi]:offsets[gi] + sizes[gi]] for gi in range(ng)]
        self.lands = [thru[n + offsets[gi]:n + offsets[gi] + sizes[gi]] for gi in range(ng)]
        self.token = outs[-1]

    def _copy(self, src, land, send_sems, recv_sems, wi, k):
        to = _peer(k)
        if self.scatter:
            src_ref, dst_ref = src.at[_linear(to)], land.at[k - 1]
        else:
            src_ref, dst_ref = src, land.at[_linear(_peer(0))]
        return pltpu.make_async_remote_copy(
            src_ref=src_ref, dst_ref=dst_ref, send_sem=send_sems.at[7 * wi + k - 1],
            recv_sem=recv_sems.at[7 * wi + k - 1], device_id=to, device_id_type=MESH)

    def wait(self, gi, after):
        n = self.sizes[gi]
        copy = self._copy

        def body(*refs):
            src, land = refs[:n], refs[n:2 * n]
            send_sems, recv_sems = refs[2 * n], refs[2 * n + 1]
            for wi in range(n):
                for k in range(1, N_DEV):
                    cp = copy(src[wi], land[wi], send_sems, recv_sems, wi, k)
                    cp.wait_send()
                    cp.wait_recv()

        arrays = list(self.srcs[gi]) + list(self.lands[gi])
        outs = pl.pallas_call(
            body, name=f"{self.name}_wait{gi}",
            in_specs=[HBM_SPEC] * (2 * n) + [SEM_SPEC, SEM_SPEC] + [ANY_SPEC] * len(after),
            out_specs=[HBM_SPEC] * (2 * n),
            out_shape=[pltpu.HBM(a.shape, a.dtype) for a in arrays],
            input_output_aliases={i: i for i in range(2 * n)},
            compiler_params=pltpu.CompilerParams(has_side_effects=EFFECT),
        )(*arrays, *self.sems[gi], *after)
        return outs[:n], outs[n:]


def _rotary_tables(positions):
    rot_dim = HEAD_DIM // 4
    inv_freq = ROPE_THETA ** (-jnp.arange(0, rot_dim, 2, dtype=F32) / rot_dim)
    ang = positions.astype(F32)[:, None] * inv_freq
    cs = jnp.concatenate([jnp.cos(ang), jnp.sin(ang)], axis=1)
    dim = jnp.arange(LANES) % HEAD_DIM
    first, second = dim < ROT_SHIFT, (dim >= ROT_SHIFT) & (dim < rot_dim)
    src = jnp.arange(2 * ROT_SHIFT)[:, None]
    angle = (dim % ROT_SHIFT)[None, :]
    c = jnp.where((first | second)[None, :] & (src == angle), 1.0, 0.0)
    sa = jnp.where(second[None, :] & (src == angle + ROT_SHIFT), 1.0, 0.0)
    sb = jnp.where(first[None, :] & (src == angle + ROT_SHIFT), -1.0, 0.0)
    spread = jnp.concatenate([c, sa, sb], axis=1).astype(F32)
    base = jnp.concatenate([jnp.where(first | second, 0.0, 1.0), jnp.zeros((2 * LANES,))]).astype(F32)[None, :]
    return jnp.dot(cs, spread, precision=lax.Precision.HIGHEST, preferred_element_type=F32) + base


def _block_diag(pool_w):
    gc = pool_w.shape[-1]
    out = jnp.zeros((POOL_WIDTH, POOL_WIDTH), pool_w.dtype)
    for grp in range(pool_w.shape[0]):
        out = lax.dynamic_update_slice(out, pool_w[grp], (grp * gc, grp * gc))
    return out


def _diag_blocks(a):
    gc = POOL_WIDTH // len(POOL_WINDOWS)
    return jnp.stack([a[grp * gc:(grp + 1) * gc, grp * gc:(grp + 1) * gc] for grp in range(len(POOL_WINDOWS))])


def _local_step(x, p, positions, loss_target, norm1, pool_w, pool_scale, norm2, norm3, final_norm, weights, send):
    rc = rsa = rsb = _rotary_tables(positions)
    ones_bd = _block_diag(jnp.ones((4, HEAD_DIM, HEAD_DIM), BF16))
    saved = []
    h = x
    for i in range(2):
        tag = f"_l{i}"
        g1, g2, g3 = norm1[i:i + 1], norm2[i:i + 1], norm3[i:i + 1]
        w_bd = _block_diag(pool_w[i]).astype(BF16)
        scale = pool_scale[i:i + 1]
        w_in = weights(i, "in", (h, rc, w_bd))
        hn1, u, *qkv = _normproj_fwd(h, g1, w_in, rc, rsa, rsb, "normproj_fwd" + tag)
        qkv = [qkv[3 * grp:3 * grp + 3] for grp in range(3)]
        started = weights(i, "prefetch", (hn1,))
        o, lse = zip(*[_attn_fwd(*qkv[grp], f"attn_fwd{tag}_g{grp}", after=started) for grp in range(3)])
        w_out = weights(i, "out", o)
        h1, a, y = _outproj_fwd(h, u, w_bd, scale, o, lse, w_out, "outproj_fwd" + tag)
        w_up, w_down = weights(i, "mlp", (h1,))
        h2, hn2, r = _mlp_fwd(h1, g2, w_up, w_down, "mlp_fwd" + tag)
        w_gate, w_ple = weights(i, "gate", (h2,))
        h0 = h
        if i == 0:
            h, hn3, gate, pb = _gate_fwd(h2, g3, w_gate, p, i, w_ple, "gate_fwd" + tag)
        else:
            hn3, gate, pb, loss, dh, d_final = _gate_fwd(h2, g3, w_gate, p, i, w_ple, "gate_fwd" + tag,
                                                         head=(final_norm.reshape(1, D_MODEL), loss_target))
        saved.append(dict(h0=h0, hn1=hn1, qkv=qkv, y=y, o=o, lse=lse, a=a, h1=h1, hn2=hn2, r=r, h2=h2,
                          hn3=hn3, gate=gate, pb=pb, w_bd=w_bd, scale=scale, g1=g1, g2=g2, g3=g3,
                          w_in=w_in, w_out=w_out, w_up=w_up, w_down=w_down, w_gate=w_gate, w_ple=w_ple))

    grads = [None, None]
    sent = ()
    for i in (1, 0):
        tag = f"_l{i}"
        sv = saved[i]
        dh2, dg3, dw_gate, dw_ple = _gate_bwd(dh, sv["gate"], sv["pb"], sv["w_ple"], sv["h2"], sv["g3"], sv["w_gate"],
                                              sv["hn3"], "gate_bwd" + tag, after=sent)
        dh1, dup, dg2, dh2b = _mlp_bwd(dh2, sv["r"], sv["h1"], sv["g2"], sv["w_up"], sv["w_down"], "mlp_bwd" + tag)
        dw_down = _matmul_tn(sv["r"], dh2b, "dw_down" + tag, square_a=True)
        dw_up = _matmul_tn(sv["hn2"], dup, "dw_up" + tag, blocked_out=True)
        dpool, do0, do1, do2, de0, de1, de2, dw_out = _outproj_bwd(dh1, sv["w_out"], sv["o"], sv["lse"], ones_bd,
                                                                   sv["a"], "outproj_bwd" + tag)
        sent = send(i, "main", dict(w_gate=dw_gate, w_ple=dw_ple, w_down=dw_down, w_up=dw_up, w_out=dw_out))
        dqkv = [_attn_bwd(*sv["qkv"][grp], do_g, sv["lse"][grp], de_g, f"attn_bwd{tag}_g{grp}", after=sent)
                for grp, (do_g, de_g) in enumerate(((do0, de0), (do1, de1), (do2, de2)))]
        dq, dk, dv = zip(*dqkv)
        du, dw_bd, dscale = _pool_bwd(dpool, sv["y"], sv["w_bd"], sv["scale"], "pool_bwd" + tag, after=sent)
        dh, dz, dg1 = _normproj_bwd(dh1, du, dq, dk, dv, rc, rsa, rsb, sv["w_in"], sv["h0"], sv["g1"],
                                    "normproj_bwd" + tag)
        grads[i] = dict(norm1=dg1, norm2=dg2, norm3=dg3, pool_w=_diag_blocks(dw_bd), pool_scale=dscale)
        small_sent = send(0, "small", (grads, d_final, loss)) if i == 0 else ()
        dw_in = _matmul_tn(dz, sv["hn1"], "dw_in" + tag, tm=N_IN // 2, after=small_sent)
        sent = send(i, "in", dict(w_in=dw_in))
    return dh, sent


def _pack_small(norm1, norm2, norm3, final_norm, pool_scale, pool_w, spare=None):
    spare = jnp.zeros((1, LANES), F32) if spare is None else spare
    scale_row = jnp.concatenate([pool_scale.reshape(1, 2 * POOL_WIDTH), spare,
                                 jnp.zeros((1, D_MODEL - 2 * POOL_WIDTH - LANES), F32)], axis=1)
    return jnp.concatenate([norm1, norm2, norm3, final_norm.reshape(1, D_MODEL), scale_row,
                            pool_w.reshape(32, D_MODEL)], axis=0)


def _unpack_small(a):
    return dict(norm1=a[0:2], norm2=a[2:4], norm3=a[4:6], final_norm=a[6], pool_scale=a[7, 0:2 * POOL_WIDTH].reshape(2, POOL_WIDTH),
                pool_w=a[8:40].reshape(2, 4, HEAD_DIM, HEAD_DIM))


def _chunks_cols(a, cols):
    return a.reshape(a.shape[0], N_DEV, cols).transpose(1, 0, 2)


def _chunks_rows(a, rows):
    return a.reshape(N_DEV, rows, a.shape[1])


BIG = ("w_in", "w_out", "w_up", "w_down", "w_gate", "w_ple")
SMALL = ("norm1", "norm2", "norm3", "final_norm", "pool_scale", "pool_w")
ORDER = ("norm1", "w_in", "pool_w", "pool_scale", "w_out", "norm2", "w_up", "w_down", "norm3", "w_gate", "w_ple",
         "final_norm")


def kernel(x, p, positions, norm1, w_in, pool_w, pool_scale, w_out, norm2, w_up, w_down, norm3, w_gate, w_ple, final_norm, loss_target, m_norm1, m_w_in, m_pool_w, m_pool_scale, m_w_out, m_norm2, m_w_up, m_w_down, m_norm3, m_w_gate, m_w_ple, m_final_norm, v_norm1, v_w_in, v_pool_w, v_pool_scale, v_w_out, v_norm2, v_w_up, v_w_down, v_norm3, v_w_gate, v_w_ple, v_final_norm):
    w = dict(norm1=norm1, w_in=w_in, pool_w=pool_w, pool_scale=pool_scale, w_out=w_out, norm2=norm2, w_up=w_up,
             w_down=w_down, norm3=norm3, w_gate=w_gate, w_ple=w_ple, final_norm=final_norm)
    m = dict(norm1=m_norm1, w_in=m_w_in, pool_w=m_pool_w, pool_scale=m_pool_scale, w_out=m_w_out, norm2=m_norm2,
             w_up=m_w_up, w_down=m_w_down, norm3=m_norm3, w_gate=m_w_gate, w_ple=m_w_ple, final_norm=m_final_norm)
    v = dict(norm1=v_norm1, w_in=v_w_in, pool_w=v_pool_w, pool_scale=v_pool_scale, w_out=v_w_out, norm2=v_norm2,
             w_up=v_w_up, w_down=v_w_down, norm3=v_norm3, w_gate=v_w_gate, w_ple=v_w_ple, final_norm=v_final_norm)
    seq = x.shape[1]

    bf = {n: [w[n][layer].astype(BF16) for layer in range(2)] for n in BIG}
    bf["w_in"] = [a.T for a in bf["w_in"]]
    me = 4 * lax.axis_index("x") + 2 * lax.axis_index("y") + lax.axis_index("c")
    parts = dict(zip(("in", "out", "mlp", "gate"), (("w_in",), ("w_out",), ("w_up", "w_down"), ("w_gate", "w_ple"))))
    gathers = [_Exchange("gather_l0", [[bf[n][0] for n in parts[pt]] for pt in parts], scatter=False)]
    unpack = dict(w_in=lambda a: a.reshape(N_IN, D_MODEL),
                  w_out=lambda a: a.reshape(D_MODEL, D_MODEL), w_gate=lambda a: a.reshape(D_MODEL, D_MODEL),
                  w_ple=lambda a: a.transpose(1, 0, 2).reshape(PLE_DIM, D_MODEL), w_up=lambda a: a, w_down=lambda a: a)

    def weights(layer, part, after):
        if part == "prefetch":
            if layer != 0:
                return ()
            gathers.append(_Exchange("gather_l1", [[bf[n][1] for n in parts[pt]] for pt in parts], scatter=False,
                                     after=after))
            return (gathers[1].token,)
        shards, lands = gathers[layer].wait(tuple(parts).index(part), after)
        full = [unpack[n](lax.dynamic_update_slice_in_dim(land, shard[None], me, axis=0))
                for n, shard, land in zip(parts[part], shards, lands)]
        return full if len(full) > 1 else full[0]

    to_chunks = dict(w_in=lambda a: _chunks_rows(a, N_IN // N_DEV),
                     w_out=lambda a: _chunks_rows(a, D_MODEL // N_DEV),
                     w_up=lambda a: a, w_down=lambda a: _chunks_rows(a, FF_BLOCK),
                     w_gate=lambda a: _chunks_rows(a, D_MODEL // N_DEV), w_ple=lambda a: _chunks_cols(a, D_MODEL // N_DEV))
    own = {n: [None, None] for n in BIG}
    scatters = {}

    def send(layer, part, grads):
        if part == "small":
            per_layer, d_final, loss = grads
            pack = _pack_small(
                *[jnp.concatenate([per_layer[0][n], per_layer[1][n]], axis=0) for n in ("norm1", "norm2", "norm3")],
                d_final.reshape(D_MODEL),
                jnp.concatenate([per_layer[0]["pool_scale"], per_layer[1]["pool_scale"]], axis=0),
                jnp.stack([per_layer[0]["pool_w"], per_layer[1]["pool_w"]]), spare=loss)
            scatters["small"] = _Exchange("gather_small", [[pack]], scatter=False)
            return (scatters["small"].token,)
        for n, (g32, _) in grads.items():
            own[n][layer] = to_chunks[n](g32)
        ex = _Exchange(f"scatter_{part}_l{layer}", [[to_chunks[n](g16) for n, (_, g16) in grads.items()]], scatter=True)
        scatters[layer, part] = (tuple(grads), ex)
        return (ex.token,)

    dx, sent = _local_step(
        x.reshape(seq, D_MODEL), p.reshape(2, seq, PLE_DIM), positions.reshape(seq), loss_target.reshape(seq, D_MODEL),
        norm1, pool_w, pool_scale, norm2, norm3, final_norm, weights, send)

    g_out, d_out, m_out, v_out = {}, {}, {}, {}
    my_index = me.reshape(1)
    for part in ("main", "in"):
        recv = {}
        for layer in (1, 0):
            names, ex = scatters[layer, part]
            for n, r in zip(names, ex.wait(0, sent)[1]):
                recv[n, layer] = r
        for n in names:
            grad = (*own[n], recv[n, 0], recv[n, 1])
            if n == "w_in":
                grad = _sum_chunks(grad, my_index, "sum_w_in").transpose(0, 2, 1)
            g_out[n], d_out[n], m_out[n], v_out[n] = _adamw_sharded(w[n], m[n], v[n], grad, my_index, "adamw_" + n)
        sent = tuple(d_out[n] for n in names)
    (mine,), (landed,) = scatters["small"].wait(0, sent)
    small_g8 = lax.dynamic_update_slice_in_dim(landed, mine[None], me, axis=0)
    pack = lambda t: _pack_small(*[t[n] for n in SMALL])
    small_g, d_small, m_small, v_small = _adamw_packed(pack(w), small_g8, pack(m), pack(v), "adamw_small")
    for dst, a in ((g_out, small_g), (d_out, d_small), (m_out, m_small), (v_out, v_small)):
        dst.update(_unpack_small(a))

    return (small_g[7, 2 * POOL_WIDTH],dx.reshape(1, seq, D_MODEL), *[g_out[n] for n in ORDER], *[d_out[n] for n in ORDER],
            *[m_out[n] for n in ORDER], *[v_out[n] for n in ORDER])
```

```python
import functools

import jax
import jax.numpy as jnp
from jax import lax
from jax.experimental import pallas as pl
from jax.experimental.pallas import tpu as pltpu

F32 = jnp.float32
BF16 = jnp.bfloat16

D_MODEL = 1024
HEAD_DIM = 64
POOL_WIDTH = 256
POOL_WINDOWS = (2, 4, 8, 16)
POOL_HALO = 16
POOL_PAD = 8
GROUP_WIDTH = 256
DILATIONS = (1, 4, 16)
ATTN_BLOCK = 128
ROT_SHIFT = 8
ROPE_THETA = 500000.0
D_FF = 4096
FF_BLOCK = 512
FF_PER_STEP = 2
MLP_BWD_TILE = 512
FWD_TILE = 1024
N_DEV = 8
N_IN = POOL_WIDTH + 3 * 768
PLE_DIM = 256
EPS = 1e-6
NEG_BIG = -1e30

ADAM_LR = 0.001
ADAM_B1 = 0.9
ADAM_B2 = 0.999
ADAM_EPS = 1e-08
ADAM_WD = 0.01
ADAM_STEP = 10

LANES = 128
VMEM_LIMIT = 56 * 1024 * 1024
MESH = pl.DeviceIdType.MESH


def _params(n_grid):
    return pltpu.CompilerParams(dimension_semantics=("arbitrary",) * n_grid, vmem_limit_bytes=VMEM_LIMIT)


def _dot(a, b):
    return jnp.dot(a, b, preferred_element_type=F32)


def _dot_nt(a, b):
    return lax.dot_general(a, b, (((1,), (1,)), ((), ())), preferred_element_type=F32)


def _dot_tn(a, b):
    return lax.dot_general(a, b, (((0,), (0,)), ((), ())), preferred_element_type=F32)


def _rms(x, g):
    rstd = lax.rsqrt(jnp.mean(x * x, axis=-1, keepdims=True) + EPS)
    n = x * rstd
    return n, rstd, n * g


def _rms_bwd(dy, n, rstd, g):
    dyn = dy * g
    dx = rstd * (dyn - n * jnp.mean(dyn * n, axis=-1, keepdims=True))
    return dx, jnp.sum(dy * n, axis=0, keepdims=True)


def _ordered_after(body, n_in, after):
    if not after:
        return body
    return lambda *refs: body(*refs[:n_in], *refs[n_in + len(after):])


def _resident(shape):
    return pl.BlockSpec(shape, lambda i: (0,) * len(shape), pipeline_mode=pl.Buffered(1))


def _row_tile(s, t):
    t = min(s, t)
    assert s % t == 0
    return t


def _rot(z, c, sa, sb):
    return z * c + pltpu.roll(z, ROT_SHIFT, 1) * sa + pltpu.roll(z, LANES - ROT_SHIFT, 1) * sb


def _table_specs(t):
    return [pl.BlockSpec((t, LANES), functools.partial(lambda i, k: (i, k), k=k)) for k in range(3)]


def _rot_t(dz, c, sa, sb):
    return dz * c + pltpu.roll(dz * sa, LANES - ROT_SHIFT, 1) + pltpu.roll(dz * sb, ROT_SHIFT, 1)


def _to_residues(value, stage, out_ref, dil):
    if dil == 1:
        out_ref[0] = value.astype(out_ref.dtype)
        return
    rows = value.shape[0] // dil
    for hf in range(GROUP_WIDTH // LANES):
        lanes = slice(hf * LANES, (hf + 1) * LANES)
        stage[hf][...] = value[:, lanes]
        for r in range(dil):
            out_ref[r, :, lanes] = stage[hf][pl.ds(r, rows, stride=dil), :].astype(out_ref.dtype)


def _from_residues(in_ref, stage, dil):
    if dil == 1:
        return in_ref[0].astype(F32)
    rows = in_ref.shape[1]
    for hf in range(GROUP_WIDTH // LANES):
        for r in range(dil):
            stage[hf][pl.ds(r, rows, stride=dil), :] = in_ref[r, :, hf * LANES:(hf + 1) * LANES].astype(F32)
    return jnp.concatenate([stage[0][...], stage[1][...]], axis=1)


def _residue_spec(dil, t):
    return pl.BlockSpec((dil, t // dil, GROUP_WIDTH), lambda i: (0, i, 0))


def _residue_shape(dil, s, dtype):
    return jax.ShapeDtypeStruct((dil, s // dil, GROUP_WIDTH), dtype)


def _stages(t, n):
    return [pltpu.VMEM((t, LANES), F32)] * (n * (GROUP_WIDTH // LANES))


def _pair_stages(refs):
    return [refs[i:i + 2] for i in range(0, len(refs), 2)]


def _normproj_fwd(h, g, w_in, rc, rsa, rsb, name):
    s = h.shape[0]
    t = _row_tile(s, FWD_TILE)

    def body(h_ref, g_ref, w_ref, c_ref, sa_ref, sb_ref, hn_ref, u_ref, *rest):
        qkv_refs, stages = rest[:9], _pair_stages(rest[9:])
        _, _, hn = _rms(h_ref[...], g_ref[...])
        hb = hn.astype(BF16)
        hn_ref[...] = hb
        c, sa, sb = c_ref[...], sa_ref[...], sb_ref[...]

        def rot(z, scale):
            halves = [_rot(z[:, hf * LANES:(hf + 1) * LANES], c, sa, sb) * scale for hf in range(2)]
            return jnp.concatenate(halves, axis=1)

        proj = lambda lo: _dot_nt(hb, w_ref[lo:lo + GROUP_WIDTH, :])
        u_ref[...] = proj(0)
        for grp, dil in enumerate(DILATIONS):
            lo = POOL_WIDTH + grp * GROUP_WIDTH
            q_ref, k_ref, v_ref = qkv_refs[3 * grp:3 * grp + 3]
            _to_residues(rot(proj(lo), HEAD_DIM ** -0.5), stages[0], q_ref, dil)
            _to_residues(rot(proj(lo + 768), 1.0), stages[1], k_ref, dil)
            _to_residues(proj(lo + 1536), stages[2], v_ref, dil)

    row = lambda w: pl.BlockSpec((t, w), lambda i: (i, 0))
    return pl.pallas_call(
        body, name=name, grid=(s // t,),
        in_specs=[row(D_MODEL), pl.BlockSpec((1, D_MODEL), lambda i: (0, 0)),
                  _resident((N_IN, D_MODEL))] + _table_specs(t),
        out_specs=[row(D_MODEL), row(POOL_WIDTH)] + [_residue_spec(dil, t) for dil in DILATIONS for _ in range(3)],
        out_shape=[jax.ShapeDtypeStruct((s, D_MODEL), BF16), jax.ShapeDtypeStruct((s, POOL_WIDTH), F32)]
        + [_residue_shape(dil, s, BF16) for dil in DILATIONS for _ in range(3)],
        scratch_shapes=_stages(t, 3),
        compiler_params=_params(1),
    )(h, g, w_in, rc, rsa, rsb)


def _pool_lane_window():
    lane = lax.broadcasted_iota(jnp.int32, (1, POOL_WIDTH), 1)
    return jnp.left_shift(2, lane // (POOL_WIDTH // len(POOL_WINDOWS)))


def _window_sums(ext, b2, b4, b8, t, lo, tile, direction):
    rows = t + POOL_HALO
    for src, dst, sh in ((ext, b2, 1), (b2, b4, 2), (b4, b8, 4)):
        dst[lo:lo + rows, :] = src[lo:lo + rows, :] + src[lo + direction * sh:lo + direction * sh + rows, :]
    s16 = b8[tile:tile + t, :] + b8[tile + direction * 8:tile + direction * 8 + t, :]
    win = _pool_lane_window()
    return jnp.where(win == 2, b2[tile:tile + t, :],
                     jnp.where(win == 4, b4[tile:tile + t, :], jnp.where(win == 8, b8[tile:tile + t, :], s16)))


def _pool_fwd_tile(i, u_ref, w_ref, sc_ref, y_ref, ext, b2, b4, b8):
    t = u_ref.shape[0]
    first = POOL_PAD + POOL_HALO

    @pl.when(i == 0)
    def _():
        for buf in (ext, b2, b4):
            buf[0:POOL_PAD, :] = jnp.zeros((POOL_PAD, POOL_WIDTH), F32)
        ext[POOL_PAD:first, :] = jnp.zeros((POOL_HALO, POOL_WIDTH), F32)

    x = u_ref[...]
    ext[first:, :] = x
    wsum = _window_sums(ext, b2, b4, b8, t, POOL_PAD, first, -1)
    pos = i * t + lax.broadcasted_iota(jnp.int32, (t, POOL_WIDTH), 0)
    cnt = jnp.minimum(pos + 1, _pool_lane_window()).astype(F32)
    yb = (wsum / cnt - x).astype(BF16)
    y_ref[...] = yb
    ext[POOL_PAD:first, :] = x[t - POOL_HALO:, :]
    return _dot(yb, w_ref[...]) * sc_ref[...]


def _head_masks():
    lane = lax.broadcasted_iota(jnp.int32, (ATTN_BLOCK, GROUP_WIDTH), 1)
    return [lane // HEAD_DIM == hd for hd in range(GROUP_WIDTH // HEAD_DIM)]


def _stack_heads(a, masks):
    zero = jnp.zeros_like(a)
    return jnp.concatenate([jnp.where(m, a, zero) for m in masks], axis=0)


def _band_bias(first_step):
    rows = ATTN_BLOCK * (GROUP_WIDTH // HEAD_DIM)
    i = lax.broadcasted_iota(jnp.int32, (rows, 2 * ATTN_BLOCK), 0) & (ATTN_BLOCK - 1)
    j = lax.broadcasted_iota(jnp.int32, (rows, 2 * ATTN_BLOCK), 1)
    inner = jnp.where((j >= i) & (j <= i + ATTN_BLOCK), 0.0, NEG_BIG)
    return jnp.where((j < ATTN_BLOCK) & first_step, NEG_BIG, inner), inner


def _column_per_head(a):
    return jnp.concatenate([a[:, hd * HEAD_DIM:hd * HEAD_DIM + 1] for hd in range(GROUP_WIDTH // HEAD_DIM)], axis=0)


def _blocks_per_step(nb):
    if nb <= 16:
        return nb
    return next(qb for qb in (16, 8, 4, 2, 1) if nb % qb == 0)


def _residues_per_step(dil, nb, qb):
    return 2 if (nb == qb and qb < 8 and dil % 2 == 0) else 1


def _attn_fwd(q, k, v, name, after=()):
    dil, length, _ = q.shape
    nb = length // ATTN_BLOCK
    qb = _blocks_per_step(nb)
    rs = _residues_per_step(dil, nb, qb)

    def body(q_ref, kp_ref, kc_ref, vp_ref, vc_ref, o_ref, lse_ref):
        masks = _head_masks()
        bias = _band_bias(pl.program_id(1) == 0)
        for rr in range(rs):
            for qi in range(qb):
                here = slice(qi * ATTN_BLOCK, (qi + 1) * ATTN_BLOCK)
                before = slice((qi - 1) * ATTN_BLOCK, qi * ATTN_BLOCK)
                kcat = jnp.concatenate([kp_ref[rr] if qi == 0 else kc_ref[rr, before], kc_ref[rr, here]], axis=0)
                vcat = jnp.concatenate([vp_ref[rr] if qi == 0 else vc_ref[rr, before], vc_ref[rr, here]], axis=0)
                qs = _stack_heads(q_ref[rr, here], masks)
                sc = _dot_nt(qs, kcat) + bias[min(qi, 1)]
                m = jnp.max(sc, axis=1, keepdims=True)
                e = jnp.exp(sc - m)
                l = jnp.sum(e, axis=1, keepdims=True)
                p = (e / l).astype(BF16)
                lse = m + jnp.log(l)
                o = jnp.zeros((ATTN_BLOCK, GROUP_WIDTH), F32)
                lse_full = jnp.zeros((ATTN_BLOCK, GROUP_WIDTH), F32)
                for hd, msk in enumerate(masks):
                    rows = slice(hd * ATTN_BLOCK, (hd + 1) * ATTN_BLOCK)
                    o = jnp.where(msk, _dot(p[rows], vcat), o)
                    lse_full = jnp.where(msk, lse[rows], lse_full)
                o_ref[rr, here] = o.astype(o_ref.dtype)
                lse_ref[rr, here] = lse_full

    cur = pl.BlockSpec((rs, qb * ATTN_BLOCK, GROUP_WIDTH), lambda r, j: (r, j, 0))
    prev = pl.BlockSpec((rs, ATTN_BLOCK, GROUP_WIDTH), lambda r, j: (r, jnp.maximum(qb * j - 1, 0), 0))
    return pl.pallas_call(
        _ordered_after(body, 5, after), name=name, grid=(dil // rs, nb // qb),
        in_specs=[cur, prev, cur, prev, cur] + [pl.BlockSpec(memory_space=pl.ANY)] * len(after), out_specs=[cur, cur],
        out_shape=[jax.ShapeDtypeStruct(q.shape, BF16), jax.ShapeDtypeStruct(q.shape, F32)],
        compiler_params=_params(2),
    )(q, k, k, v, v, *after)


def _group_weights(l0, l1, l2):
    m = jnp.maximum(jnp.maximum(l0, l1), l2)
    e0, e1, e2 = jnp.exp(l0 - m), jnp.exp(l1 - m), jnp.exp(l2 - m)
    den = e0 + e1 + e2
    return e0 / den, e1 / den, e2 / den


def _outproj_fwd(h, u, w_bd, scale, o, lse, w_out, name):
    s = h.shape[0]
    t = _row_tile(s, FWD_TILE)

    def body(h_ref, u_ref, wbd_ref, sc_ref, o0, o1, o2, l0, l1, l2, w_ref, out_ref, a_ref, y_ref, ext, b2, b4, b8,
             *stages):
        pool_out = _pool_fwd_tile(pl.program_id(0), u_ref, wbd_ref, sc_ref, y_ref, ext, b2, b4, b8)
        stages = _pair_stages(stages)
        ov = [_from_residues(r, stages[i], DILATIONS[i]) for i, r in enumerate((o0, o1, o2))]
        lv = [_from_residues(r, stages[3 + i], DILATIONS[i]) for i, r in enumerate((l0, l1, l2))]
        wts = _group_weights(*lv)
        a = jnp.concatenate([pool_out] + [ov[i] * wts[i] for i in range(3)], axis=1).astype(BF16)
        a_ref[...] = a
        out_ref[...] = h_ref[...] + _dot(a, w_ref[...])

    row = lambda w: pl.BlockSpec((t, w), lambda i: (i, 0))
    res = [_residue_spec(dil, t) for dil in DILATIONS]
    return pl.pallas_call(
        body, name=name, grid=(s // t,),
        in_specs=[row(D_MODEL), row(POOL_WIDTH), _resident((POOL_WIDTH, POOL_WIDTH)), _resident((1, POOL_WIDTH))]
        + res + res + [_resident((D_MODEL, D_MODEL))],
        out_specs=[row(D_MODEL), row(D_MODEL), row(POOL_WIDTH)],
        out_shape=[jax.ShapeDtypeStruct((s, D_MODEL), F32), jax.ShapeDtypeStruct((s, D_MODEL), BF16),
                   jax.ShapeDtypeStruct((s, POOL_WIDTH), BF16)],
        scratch_shapes=[pltpu.VMEM((t + POOL_HALO + POOL_PAD, POOL_WIDTH), F32)] * 4 + _stages(t, 6),
        compiler_params=_params(1),
    )(h, u, w_bd, scale, *o, *lse, w_out)


def _mlp_fwd(h, g, w_up, w_down, name):
    s = h.shape[0]
    t = _row_tile(s, 512)
    nblk = D_FF // FF_BLOCK

    def body(h_ref, g_ref, wu_ref, wd_ref, out_ref, hn_ref, r_ref):
        x = h_ref[...]
        _, _, hn = _rms(x, g_ref[...])
        hb = hn.astype(BF16)
        hn_ref[...] = hb
        acc = None
        for b0 in range(0, nblk, FF_PER_STEP):
            acts = []
            for b in range(b0, b0 + FF_PER_STEP):
                r = jnp.maximum(_dot(hb, wu_ref[b]), 0.0)
                r_ref[:, b * FF_BLOCK:(b + 1) * FF_BLOCK] = r.astype(BF16)
                acts.append((r * r).astype(BF16))
            wd = wd_ref[b0:b0 + FF_PER_STEP].reshape(FF_PER_STEP * FF_BLOCK, D_MODEL)
            part = _dot(jnp.concatenate(acts, axis=1), wd)
            acc = part if acc is None else acc + part
        out_ref[...] = x + acc

    row = lambda w: pl.BlockSpec((t, w), lambda i: (i, 0))
    resident = lambda shape: pl.BlockSpec(shape, lambda i: (0, 0, 0), pipeline_mode=pl.Buffered(1))
    return pl.pallas_call(
        body, name=name, grid=(s // t,),
        in_specs=[row(D_MODEL), pl.BlockSpec((1, D_MODEL), lambda i: (0, 0)),
                  resident((nblk, D_MODEL, FF_BLOCK)), resident((nblk, FF_BLOCK, D_MODEL))],
        out_specs=[row(D_MODEL), row(D_MODEL), row(D_FF)],
        out_shape=[jax.ShapeDtypeStruct((s, D_MODEL), F32), jax.ShapeDtypeStruct((s, D_MODEL), BF16),
                   jax.ShapeDtypeStruct((s, D_FF), BF16)],
        compiler_params=_params(1),
    )(h, g, w_up, w_down)


def _gate_fwd(h, g, w_gate, p, layer, w_ple, name, head=None):
    s = h.shape[0]
    t = _row_tile(s, FWD_TILE)

    def body(h_ref, g_ref, wg_ref, p_ref, wp_ref, *refs):
        x = h_ref[...]
        _, _, hn = _rms(x, g_ref[...])
        hb = hn.astype(BF16)
        gate = 1.0 / (1.0 + jnp.exp(-_dot(hb, wg_ref[...])))
        pb = p_ref[...].astype(BF16)
        h3 = x + gate * _dot(pb, wp_ref[...])
        if head is None:
            out_ref, hn_ref, gate_ref, pb_ref = refs
            out_ref[...] = h3
        else:
            gf_ref, t_ref, hn_ref, gate_ref, pb_ref, loss_ref, dh_ref, dgf_ref = refs

            @pl.when(pl.program_id(0) == 0)
            def _():
                loss_ref[...] = jnp.zeros_like(loss_ref)
                dgf_ref[...] = jnp.zeros_like(dgf_ref)

            gf = gf_ref[...]
            n, rstd, y = _rms(h3, gf)
            err = y - t_ref[...]
            loss_ref[...] += jnp.sum(err * err) * (0.5 / D_MODEL)
            dh_ref[...], dgf = _rms_bwd(err * (1.0 / D_MODEL), n, rstd, gf)
            dgf_ref[...] += dgf
        hn_ref[...] = hb
        pb_ref[...] = pb
        gate_ref[...] = gate.astype(BF16)

    row = lambda w: pl.BlockSpec((t, w), lambda i: (i, 0))
    full = lambda a, b: pl.BlockSpec((a, b), lambda i: (0, 0))
    in_specs = [row(D_MODEL), full(1, D_MODEL), _resident((D_MODEL, D_MODEL)),
                pl.BlockSpec((None, t, PLE_DIM), lambda i: (layer, i, 0)), _resident((PLE_DIM, D_MODEL))]
    saved_specs = [row(D_MODEL), row(D_MODEL), row(PLE_DIM)]
    saved_shapes = [jax.ShapeDtypeStruct((s, D_MODEL), BF16), jax.ShapeDtypeStruct((s, D_MODEL), BF16),
                    jax.ShapeDtypeStruct((s, PLE_DIM), BF16)]
    if head is None:
        return pl.pallas_call(
            body, name=name, grid=(s // t,), in_specs=in_specs, out_specs=[row(D_MODEL)] + saved_specs,
            out_shape=[jax.ShapeDtypeStruct((s, D_MODEL), F32)] + saved_shapes, compiler_params=_params(1),
        )(h, g, w_gate, p, w_ple)
    return pl.pallas_call(
        body, name=name, grid=(s // t,), in_specs=in_specs + [full(1, D_MODEL), row(D_MODEL)],
        out_specs=saved_specs + [pl.BlockSpec((1, LANES), lambda i: (0, 0)), row(D_MODEL), full(1, D_MODEL)],
        out_shape=saved_shapes + [jax.ShapeDtypeStruct((1, LANES), F32), jax.ShapeDtypeStruct((s, D_MODEL), F32),
                                  jax.ShapeDtypeStruct((1, D_MODEL), F32)],
        compiler_params=_params(1),
    )(h, g, w_gate, p, w_ple, *head)


def _gate_bwd(dh, gate, pb, w_ple, h, g, w_gate, hn, name, after=()):
    s = h.shape[0]
    t = _row_tile(s, FWD_TILE)
    last = s // t - 1

    def body(dh_ref, gate_ref, pb_ref, wp_ref, h_ref, g_ref, wg_ref, hn_ref, out_ref, dg_ref, dwg_ref, dwgb_ref,
             dwp_ref, dwpb_ref):
        i = pl.program_id(0)

        @pl.when(i == 0)
        def _():
            dg_ref[...] = jnp.zeros_like(dg_ref)
            dwg_ref[...] = jnp.zeros_like(dwg_ref)
            dwp_ref[...] = jnp.zeros_like(dwp_ref)

        d = dh_ref[...]
        gate = gate_ref[...].astype(F32)
        pb = pb_ref[...]
        e = _dot(pb, wp_ref[...])
        dgl = (d * e * gate * (1.0 - gate)).astype(BF16)
        dwg_ref[...] += _dot_tn(hn_ref[...], dgl)
        dwp_ref[...] += _dot_tn(pb, (d * gate).astype(BF16))
        gv = g_ref[...]
        n, rstd, _ = _rms(h_ref[...], gv)
        dx, dg = _rms_bwd(_dot_nt(dgl, wg_ref[...]), n, rstd, gv)
        out_ref[...] = d + dx
        dg_ref[...] += dg

        @pl.when(i == last)
        def _():
            dwgb_ref[...] = dwg_ref[...].astype(BF16)
            dwpb_ref[...] = dwp_ref[...].astype(BF16)

    row = lambda w: pl.BlockSpec((t, w), lambda i: (i, 0))
    full = lambda a, b: pl.BlockSpec((a, b), lambda i: (0, 0))
    dh2, dg, dwg, dwgb, dwp, dwpb = pl.pallas_call(
        _ordered_after(body, 8, after), name=name, grid=(s // t,),
        in_specs=[row(D_MODEL), row(D_MODEL), row(PLE_DIM), _resident((PLE_DIM, D_MODEL)), row(D_MODEL),
                  full(1, D_MODEL), _resident((D_MODEL, D_MODEL)), row(D_MODEL)]
        + [pl.BlockSpec(memory_space=pl.ANY)] * len(after),
        out_specs=[row(D_MODEL), full(1, D_MODEL), full(D_MODEL, D_MODEL), full(D_MODEL, D_MODEL),
                   full(PLE_DIM, D_MODEL), full(PLE_DIM, D_MODEL)],
        out_shape=[jax.ShapeDtypeStruct((s, D_MODEL), F32), jax.ShapeDtypeStruct((1, D_MODEL), F32),
                   jax.ShapeDtypeStruct((D_MODEL, D_MODEL), F32), jax.ShapeDtypeStruct((D_MODEL, D_MODEL), BF16),
                   jax.ShapeDtypeStruct((PLE_DIM, D_MODEL), F32), jax.ShapeDtypeStruct((PLE_DIM, D_MODEL), BF16)],
        compiler_params=_params(1),
    )(dh, gate, pb, w_ple, h, g, w_gate, hn, *after)
    return dh2, dg, (dwg, dwgb), (dwp, dwpb)


def _mlp_bwd(dh, r, h, g, w_up, w_down, name):
    s = h.shape[0]
    t = _row_tile(s, MLP_BWD_TILE)
    nblk = D_FF // FF_BLOCK

    def body(dh_ref, r_ref, h_ref, g_ref, wu_ref, wd_ref, out_ref, dup_ref, dg_ref, dhb_ref):
        @pl.when(pl.program_id(0) == 0)
        def _():
            dg_ref[...] = jnp.zeros_like(dg_ref)

        d = dh_ref[...]
        db = d.astype(BF16)
        dhb_ref[...] = db
        back = None
        for b in range(nblk):
            cols = slice(b * FF_BLOCK, (b + 1) * FF_BLOCK)
            dup = (_dot_nt(db, wd_ref[b]) * (2.0 * r_ref[:, cols].astype(F32))).astype(BF16)
            dup_ref[:, cols] = dup
            part = _dot_nt(dup, wu_ref[b])
            back = part if back is None else back + part
        gv = g_ref[...]
        n, rstd, _ = _rms(h_ref[...], gv)
        dx, dg = _rms_bwd(back, n, rstd, gv)
        out_ref[...] = d + dx
        dg_ref[...] += dg

    row = lambda w: pl.BlockSpec((t, w), lambda i: (i, 0))
    vec = pl.BlockSpec((1, D_MODEL), lambda i: (0, 0))
    resident = lambda shape: pl.BlockSpec(shape, lambda i: (0, 0, 0), pipeline_mode=pl.Buffered(1))
    return pl.pallas_call(
        body, name=name, grid=(s // t,),
        in_specs=[row(D_MODEL), row(D_FF), row(D_MODEL), vec,
                  resident((nblk, D_MODEL, FF_BLOCK)), resident((nblk, FF_BLOCK, D_MODEL))],
        out_specs=[row(D_MODEL), row(D_FF), vec, row(D_MODEL)],
        out_shape=[jax.ShapeDtypeStruct((s, D_MODEL), F32), jax.ShapeDtypeStruct((s, D_FF), BF16),
                   jax.ShapeDtypeStruct((1, D_MODEL), F32), jax.ShapeDtypeStruct((s, D_MODEL), BF16)],
        compiler_params=_params(1),
    )(dh, r, h, g, w_up, w_down)


def _outproj_bwd(dh, w_out, o, lse, ones_bd, a, name):
    s = dh.shape[0]
    t = _row_tile(s, 512)
    last = s // t - 1

    def body(dh_ref, w_ref, o0, o1, o2, l0, l1, l2, bd_ref, a_ref, dp_ref, do0, do1, do2, de0, de1, de2, dw_ref,
             dwb_ref, *stages):
        i = pl.program_id(0)

        @pl.when(i == 0)
        def _():
            dw_ref[...] = jnp.zeros_like(dw_ref)

        stages = _pair_stages(stages)
        dhb = dh_ref[...].astype(BF16)
        dw_ref[...] += _dot_tn(a_ref[...], dhb)

        @pl.when(i == last)
        def _():
            dwb_ref[...] = dw_ref[...].astype(BF16)

        da = _dot_nt(dhb, w_ref[...])
        dp_ref[...] = da[:, 0:POOL_WIDTH]
        ov =[_from_residues(r, stages[i], DILATIONS[i]) for i, r in enumerate((o0, o1, o2))]
        lv = [_from_residues(r, stages[3 + i], DILATIONS[i]) for i, r in enumerate((l0, l1, l2))]
        wts = _group_weights(*lv)
        bd = bd_ref[...]
        cbar = jnp.zeros((t, GROUP_WIDTH), F32)
        for grp, do_ref in enumerate((do0, do1, do2)):
            lo = POOL_WIDTH + grp * GROUP_WIDTH
            dag = da[:, lo:lo + GROUP_WIDTH]
            _to_residues(dag * wts[grp], stages[6 + grp], do_ref, DILATIONS[grp])
            prod = dag * ov[grp]
            hi = prod.astype(BF16)
            low = (prod - hi.astype(F32)).astype(BF16)
            cbar = cbar + wts[grp] * (_dot(hi, bd) + _dot(low, bd))
        for grp, de_ref in enumerate((de0, de1, de2)):
            _to_residues(wts[grp] * cbar, stages[9 + grp], de_ref, DILATIONS[grp])

    row = lambda w: pl.BlockSpec((t, w), lambda i: (i, 0))
    full = lambda a, b: pl.BlockSpec((a, b), lambda i: (0, 0))
    res = [_residue_spec(dil, t) for dil in DILATIONS]
    *outs, dw, dwb = pl.pallas_call(
        body, name=name, grid=(s // t,),
        in_specs=[row(D_MODEL), full(D_MODEL, D_MODEL)] + res + res + [full(GROUP_WIDTH, GROUP_WIDTH), row(D_MODEL)],
        out_specs=[row(POOL_WIDTH)] + res + res + [full(D_MODEL, D_MODEL)] * 2,
        out_shape=[jax.ShapeDtypeStruct((s, POOL_WIDTH), F32)] + [_residue_shape(dil, s, BF16) for dil in DILATIONS]
        + [_residue_shape(dil, s, F32) for dil in DILATIONS]
        + [jax.ShapeDtypeStruct((D_MODEL, D_MODEL), F32), jax.ShapeDtypeStruct((D_MODEL, D_MODEL), BF16)],
        scratch_shapes=_stages(t, 12),
        compiler_params=_params(1),
    )(dh, w_out, *o, *lse, ones_bd, a)
    return (*outs, (dw, dwb))


def _attn_bwd(q, k, v, do, lse, deff, name, after=()):
    dil, length, _ = q.shape
    nb = length // ATTN_BLOCK
    qb = _blocks_per_step(nb)
    nj = nb // qb
    rs = _residues_per_step(dil, nb, qb)
    whole = nj == 1
    tail = slice((qb - 1) * ATTN_BLOCK, qb * ATTN_BLOCK)
    block = lambda qi: slice(qi * ATTN_BLOCK, (qi + 1) * ATTN_BLOCK)

    def body(q_ref, kp_ref, kc_ref, vp_ref, vc_ref, do_ref, lse_ref, de_ref, dq_ref, dk_ref, dv_ref, ck, cv):
        j = pl.program_id(1)

        def compute():
            masks = _head_masks()
            bias = _band_bias(j == 0)
            for rr in range(rs):
                dkc, dvc = [], []
                for qi in range(qb):
                    here, before = block(qi), block(qi - 1)
                    kcat = jnp.concatenate([kp_ref[rr] if qi == 0 else kc_ref[rr, before], kc_ref[rr, here]], axis=0)
                    vcat = jnp.concatenate([vp_ref[rr] if qi == 0 else vc_ref[rr, before], vc_ref[rr, here]], axis=0)
                    qs = _stack_heads(q_ref[rr, here], masks)
                    dos = _stack_heads(do_ref[rr, here], masks)
                    sc = _dot_nt(qs, kcat) + bias[min(qi, 1)]
                    p = jnp.exp(sc - _column_per_head(lse_ref[rr, here]))
                    ds = (p * (_dot_nt(dos, vcat) - _column_per_head(de_ref[rr, here]))).astype(BF16)
                    dq = jnp.zeros((ATTN_BLOCK, GROUP_WIDTH), F32)
                    for hd, msk in enumerate(masks):
                        dq = jnp.where(msk, _dot(ds[block(hd)], kcat), dq)
                    dq_ref[rr, here] = dq.astype(dq_ref.dtype)
                    dkc.append(_dot_tn(ds, qs))
                    dvc.append(_dot_tn(p.astype(BF16), dos))

                for out_ref, carry, parts in ((dk_ref, ck, dkc), (dv_ref, cv, dvc)):
                    full = [parts[qi][ATTN_BLOCK:] + parts[qi + 1][0:ATTN_BLOCK] for qi in range(qb - 1)]
                    if whole:
                        for qi, val in enumerate(full + [parts[qb - 1][ATTN_BLOCK:]]):
                            out_ref[rr, block(qi)] = val.astype(out_ref.dtype)
                        continue

                    @pl.when(j > 0)
                    def _():
                        if qb > 1:
                            out_ref[0, 0:(qb - 1) * ATTN_BLOCK] = carry[0:(qb - 1) * ATTN_BLOCK].astype(out_ref.dtype)
                        out_ref[0, tail] = (carry[tail] + parts[0][0:ATTN_BLOCK]).astype(out_ref.dtype)

                    for qi, val in enumerate(full):
                        carry[block(qi)] = val
                    carry[tail] = parts[qb - 1][ATTN_BLOCK:]

        if whole:
            compute()
        else:
            pl.when(j < nj)(compute)

            @pl.when(j == nj)
            def _():
                dk_ref[0] = ck[...].astype(dk_ref.dtype)
                dv_ref[0] = cv[...].astype(dv_ref.dtype)

    step = lambda j: jnp.minimum(j, nj - 1)
    cur = pl.BlockSpec((rs, qb * ATTN_BLOCK, GROUP_WIDTH), lambda r, j: (r, step(j), 0))
    prev = pl.BlockSpec((rs, ATTN_BLOCK, GROUP_WIDTH), lambda r, j: (r, jnp.maximum(qb * step(j) - 1, 0), 0))
    late = pl.BlockSpec((rs, qb * ATTN_BLOCK, GROUP_WIDTH), lambda r, j: (r, jnp.maximum(j - 1, 0), 0))
    return pl.pallas_call(
        _ordered_after(body, 8, after), name=name, grid=(dil // rs, 1 if whole else nj + 1),
        in_specs=[cur, prev, cur, prev, cur, cur, cur, cur] + [pl.BlockSpec(memory_space=pl.ANY)] * len(after),
        out_specs=[cur, cur if whole else late, cur if whole else late],
        out_shape=[jax.ShapeDtypeStruct(q.shape, BF16)] * 3,
        scratch_shapes=[pltpu.VMEM((qb * ATTN_BLOCK, GROUP_WIDTH), F32)] * 2,
        compiler_params=_params(2),
    )(q, k, k, v, v, do, lse, deff, *after)


def _pool_bwd(dpool, y, w_bd, scale, name, after=()):
    s = dpool.shape[0]
    t = _row_tile(s, 512)
    nt = s // t

    def body(dp_ref, y_ref, w_ref, sc_ref, du_ref, dw_ref, dsc_ref, ext, b2, b4, b8):
        i = pl.program_id(0)

        @pl.when(i == 0)
        def _():
            ext[t:, :] = jnp.zeros((POOL_HALO + POOL_PAD, POOL_WIDTH), F32)
            for buf in (b2, b4):
                buf[t + POOL_HALO:, :] = jnp.zeros((POOL_PAD, POOL_WIDTH), F32)
            dw_ref[...] = jnp.zeros_like(dw_ref)
            dsc_ref[...] = jnp.zeros_like(dsc_ref)

        dp = dp_ref[...]
        yb = y_ref[...]
        w = w_ref[...]
        dsc_ref[...] += jnp.sum(dp * _dot(yb, w), axis=0, keepdims=True)
        dyo = (dp * sc_ref[...]).astype(BF16)
        dw_ref[...] += _dot_tn(yb, dyo)
        dy = _dot_nt(dyo, w)
        win = _pool_lane_window()
        pos = (nt - 1 - i) * t + lax.broadcasted_iota(jnp.int32, (t, POOL_WIDTH), 0)
        gq = dy / jnp.minimum(pos + 1, win).astype(F32)
        ext[0:t, :] = gq
        du_ref[...] = _window_sums(ext, b2, b4, b8, t, 0, 0, 1) - dy
        ext[t:t + POOL_HALO, :] = gq[0:POOL_HALO, :]

    rev = pl.BlockSpec((t, POOL_WIDTH), lambda i: (nt - 1 - i, 0))
    full = lambda a, b: pl.BlockSpec((a, b), lambda i: (0, 0))
    return pl.pallas_call(
        _ordered_after(body, 4, after), name=name, grid=(nt,),
        in_specs=[rev, rev, full(POOL_WIDTH, POOL_WIDTH), full(1, POOL_WIDTH)]
        + [pl.BlockSpec(memory_space=pl.ANY)] * len(after),
        out_specs=[rev, full(POOL_WIDTH, POOL_WIDTH), full(1, POOL_WIDTH)],
        out_shape=[jax.ShapeDtypeStruct((s, POOL_WIDTH), F32), jax.ShapeDtypeStruct((POOL_WIDTH, POOL_WIDTH), F32),
                   jax.ShapeDtypeStruct((1, POOL_WIDTH), F32)],
        scratch_shapes=[pltpu.VMEM((t + POOL_HALO + POOL_PAD, POOL_WIDTH), F32)] * 4,
        compiler_params=_params(1),
    )(dpool, y, w_bd, scale, *after)


def _normproj_bwd(dh, du, dq, dk, dv, rc, rsa, rsb, w_in, h, g, name):
    s = h.shape[0]
    t = _row_tile(s, 512)

    def body(dh_ref, du_ref, q0, q1, q2, k0, k1, k2, v0, v1, v2, c_ref, sa_ref, sb_ref, w_ref, h_ref, g_ref,
             out_ref, dz_ref, dg_ref, *stages):
        @pl.when(pl.program_id(0) == 0)
        def _():
            dg_ref[...] = jnp.zeros_like(dg_ref)

        c, sa, sb = c_ref[...], sa_ref[...], sb_ref[...]

        def unrot(a, scale):
            halves = [_rot_t(a[:, hf * LANES:(hf + 1) * LANES] * scale, c, sa, sb) for hf in range(2)]
            return jnp.concatenate(halves, axis=1)

        staged = _pair_stages(stages)
        tok = lambda refs, base: [_from_residues(r, staged[base + i], DILATIONS[i]) for i, r in enumerate(refs)]
        chunks = [du_ref[...]]
        chunks += [unrot(a, HEAD_DIM ** -0.5) for a in tok((q0, q1, q2), 0)]
        chunks += [unrot(a, 1.0) for a in tok((k0, k1, k2), 3)]
        chunks += tok((v0, v1, v2), 6)
        acc = jnp.zeros((t, D_MODEL), F32)
        for ci, ch in enumerate(chunks):
            cols = slice(ci * GROUP_WIDTH, (ci + 1) * GROUP_WIDTH)
            cb = ch.astype(BF16)
            dz_ref[:, cols] = cb
            acc = acc + _dot(cb, w_ref[cols, :])
        gv = g_ref[...]
        n, rstd, _ = _rms(h_ref[...], gv)
        dx, dg = _rms_bwd(acc, n, rstd, gv)
        out_ref[...] = dh_ref[...] + dx
        dg_ref[...] += dg

    row = lambda w: pl.BlockSpec((t, w), lambda i: (i, 0))
    vec = pl.BlockSpec((1, D_MODEL), lambda i: (0, 0))
    res = [_residue_spec(dil, t) for dil in DILATIONS]
    return pl.pallas_call(
        body, name=name, grid=(s // t,),
        in_specs=[row(D_MODEL), row(POOL_WIDTH)] + res * 3 + _table_specs(t)
        + [pl.BlockSpec((N_IN, D_MODEL), lambda i: (0, 0)), row(D_MODEL), vec],
        out_specs=[row(D_MODEL), row(N_IN), vec],
        out_shape=[jax.ShapeDtypeStruct((s, D_MODEL), F32), jax.ShapeDtypeStruct((s, N_IN), BF16),
                   jax.ShapeDtypeStruct((1, D_MODEL), F32)],
        scratch_shapes=_stages(t, 9),
        compiler_params=_params(1),
    )(dh, du, *dq, *dk, *dv, rc, rsa, rsb, w_in, h, g)


def _matmul_tn(a, b, name, *, square_a=False, tm=None, tn=None, blocked_out=False, after=()):
    s, m = a.shape
    n = b.shape[1]
    tk = _row_tile(s, 2048)
    tm = tm or min(m, 1024)
    tn = tn or min(n, 1024)
    assert m % tm == 0 and n % tn == 0
    nk = s // tk
    nsub = tn // FF_BLOCK if blocked_out else 1

    def body(a_ref, b_ref, o_ref, ob_ref, acc):
        k = pl.program_id(2)

        def product():
            av = a_ref[...]
            if square_a:
                av = av.astype(F32)
                av = av * av
            return _dot_tn(av.astype(BF16), b_ref[...].astype(BF16))

        def emit(total):
            if blocked_out:
                for sub in range(nsub):
                    cols = slice(sub * FF_BLOCK, (sub + 1) * FF_BLOCK)
                    o_ref[sub] = total[:, cols]
                    ob_ref[sub] = total[:, cols].astype(BF16)
            else:
                o_ref[...] = total
                ob_ref[...] = total.astype(BF16)

        if nk == 1:
            emit(product())
            return

        @pl.when(k == 0)
        def _():
            acc[...] = product()

        @pl.when((k > 0) & (k < nk - 1))
        def _():
            acc[...] += product()

        @pl.when(k == nk - 1)
        def _():
            emit(acc[...] + product())

    if blocked_out:
        shape = (n // FF_BLOCK, m, FF_BLOCK)
        out_spec = pl.BlockSpec((nsub, tm, FF_BLOCK), lambda i, j, k: (j, i, 0))
    else:
        shape = (m, n)
        out_spec = pl.BlockSpec((tm, tn), lambda i, j, k: (i, j))
    return pl.pallas_call(
        _ordered_after(body, 2, after), name=name, grid=(m // tm, n // tn, nk),
        in_specs=[pl.BlockSpec((tk, tm), lambda i, j, k: (k, i)), pl.BlockSpec((tk, tn), lambda i, j, k: (k, j))]
        + [pl.BlockSpec(memory_space=pl.ANY)] * len(after),
        out_specs=[out_spec, out_spec],
        out_shape=[jax.ShapeDtypeStruct(shape, F32), jax.ShapeDtypeStruct(shape, BF16)],
        scratch_shapes=[pltpu.VMEM((tm, tn), F32)],
        compiler_params=_params(3),
    )(a, b, *after)


def _adamw_math(w, g, m, v):
    m = ADAM_B1 * m + (1.0 - ADAM_B1) * g
    v = ADAM_B2 * v + (1.0 - ADAM_B2) * (g * g)
    m_hat = m / (1.0 - ADAM_B1 ** ADAM_STEP)
    v_hat = v / (1.0 - ADAM_B2 ** ADAM_STEP)
    delta = -ADAM_LR * (m_hat / (jnp.sqrt(v_hat) + ADAM_EPS) + ADAM_WD * w)
    return delta, m, v


def _sum_chunks_body(own0_ref, own1_ref, r0_ref, r1_ref):
    layer0 = pl.program_id(0) == 0
    g = jnp.where(layer0, own0_ref[...], own1_ref[...])
    for k in range(N_DEV - 1):
        g = g + jnp.where(layer0, r0_ref[k], r1_ref[k]).astype(F32)
    return g


def _chunk_specs(t, cols):
    rows_of = lambda layer: (lambda l, i: jnp.where(l == layer, i, 0))
    blk = pl.BlockSpec((None, t, cols), lambda l, i, me: (l, i, 0))
    own = [pl.BlockSpec((None, t, cols), functools.partial(lambda l, i, me, pick: (me[0], pick(l, i), 0), pick=rows_of(ly)))
           for ly in range(2)]
    recv = [pl.BlockSpec((N_DEV - 1, t, cols), functools.partial(lambda l, i, me, pick: (0, pick(l, i), 0), pick=rows_of(ly)))
            for ly in range(2)]
    return blk, own + recv


def _sum_chunks(chunks, me, name):
    _, rows, cols = chunks[0].shape
    t = _row_tile(rows, 320)

    def body(me_ref, own0_ref, own1_ref, r0_ref, r1_ref, g_ref):
        g_ref[...] = _sum_chunks_body(own0_ref, own1_ref, r0_ref, r1_ref)

    blk, chunk_specs = _chunk_specs(t, cols)
    return pl.pallas_call(
        body, name=name,
        grid_spec=pltpu.PrefetchScalarGridSpec(num_scalar_prefetch=1, grid=(2, rows // t), in_specs=chunk_specs,
                                               out_specs=blk),
        out_shape=jax.ShapeDtypeStruct((2, rows, cols), F32), compiler_params=_params(2),
    )(me, *chunks)


def _adamw_sharded(w, m, v, grad, me, name):
    _, rows, cols = w.shape
    t = _row_tile(rows, 256)
    summed = not isinstance(grad, tuple)
    grad = (grad,) if summed else grad

    def body(me_ref, w_ref, m_ref, v_ref, *refs):
        g_ref, d_ref, nm_ref, nv_ref = refs[-4:]
        g = refs[0][...] if summed else _sum_chunks_body(*refs[:4])
        g_ref[...] = g
        d_ref[...], nm_ref[...], nv_ref[...] = _adamw_math(w_ref[...], g, m_ref[...], v_ref[...])

    blk, chunk_specs = _chunk_specs(t, cols)
    return pl.pallas_call(
        body, name=name,
        grid_spec=pltpu.PrefetchScalarGridSpec(
            num_scalar_prefetch=1, grid=(2, rows // t),
            in_specs=[blk, blk, blk] + ([blk] if summed else chunk_specs), out_specs=[blk] * 4),
        out_shape=[jax.ShapeDtypeStruct(w.shape, F32)] * 4,
        compiler_params=_params(2),
    )(me, w, m, v, *grad)


def _adamw_packed(w, g8, m, v, name):
    def body(w_ref, g_ref, m_ref, v_ref, go_ref, d_ref, nm_ref, nv_ref):
        g = g_ref[0]
        for dev in range(1, N_DEV):
            g = g + g_ref[dev]
        go_ref[...] = g
        d_ref[...], nm_ref[...], nv_ref[...] = _adamw_math(w_ref[...], g, m_ref[...], v_ref[...])

    return pl.pallas_call(
        body, name=name, out_shape=[jax.ShapeDtypeStruct(w.shape, F32)] * 4,
        compiler_params=pltpu.CompilerParams(vmem_limit_bytes=VMEM_LIMIT),
    )(w, g8, m, v)


def _peer(k):
    x, y, c = lax.axis_index("x"), lax.axis_index("y"), lax.axis_index("c")
    return (1 - x if k & 4 else x, 1 - y if k & 2 else y, 1 - c if k & 1 else c)


def _linear(dev):
    return 4 * dev[0] + 2 * dev[1] + dev[2]


HBM_SPEC = pl.BlockSpec(memory_space=pltpu.HBM)
SEM_SPEC = pl.BlockSpec(memory_space=pltpu.SEMAPHORE)
ANY_SPEC = pl.BlockSpec(memory_space=pl.ANY)
EFFECT = pltpu.SideEffectType.DATAFLOW_SIDE_EFFECTING


def _in_hbm(a):
    return pltpu.with_memory_space_constraint(a, pltpu.HBM)


class _Exchange:
    def __init__(self, name, groups, scatter, after=()):
        self.name, self.scatter = name, scatter
        self.sizes = sizes = [len(g) for g in groups]
        srcs = [a for g in groups for a in g]
        n, ng = len(srcs), len(groups)
        lead = (N_DEV - 1,) if scatter else (N_DEV,)
        shapes = [lead + (a.shape[1:] if scatter else a.shape) for a in srcs]
        lands = [lax.empty(sh, a.dtype) for sh, a in zip(shapes, srcs)]
        offsets = [sum(sizes[:gi]) for gi in range(ng)]
        copy = self._copy

        def body(*refs):
            src, land = refs[:n], refs[n:2 * n]
            sems = refs[2 * n + len(after):2 * n + len(after) + 2 * ng]
            token = refs[-1]
            for gi in range(ng):
                for wi in range(sizes[gi]):
                    w = offsets[gi] + wi
                    for k in range(1, N_DEV):
                        copy(src[w], land[w], sems[2 * gi], sems[2 * gi + 1], wi, k).start()
            token[...] = jnp.zeros_like(token)

        sem_shapes = [pltpu.SemaphoreType.DMA(((N_DEV - 1) * sz,)) for sz in sizes for _ in range(2)]
        outs = pl.pallas_call(
            body, name=name + "_start",
            in_specs=[HBM_SPEC] * (2 * n) + [ANY_SPEC] * len(after),
            out_specs=[SEM_SPEC] * (2 * ng) + [HBM_SPEC] * (2 * n) + [pl.BlockSpec(memory_space=pltpu.VMEM)],
            out_shape=sem_shapes + [pltpu.HBM(a.shape, a.dtype) for a in srcs + lands]
            + [jax.ShapeDtypeStruct((8, LANES), F32)],
            input_output_aliases={i: 2 * ng + i for i in range(2 * n)},
            compiler_params=pltpu.CompilerParams(has_side_effects=EFFECT),
        )(*[_in_hbm(a) for a in srcs + lands], *after)
        self.sems = [outs[2 * gi:2 * gi + 2] for gi in range(ng)]
        thru = outs[2 * ng:2 * ng + 2 * n]
        self.srcs = [thru[offsets[gi]:offsets[gi] + sizes[gi]] for gi in range(ng)]
        self.lands = [thru[n + offsets[gi]:n + offsets[gi] + sizes[gi]] for gi in range(ng)]
        self.token = outs[-1]

    def _copy(self, src, land, send_sems, recv_sems, wi, k):
        to = _peer(k)
        if self.scatter:
            src_ref, dst_ref = src.at[_linear(to)], land.at[k - 1]
        else:
            src_ref, dst_ref = src, land.at[_linear(_peer(0))]
        return pltpu.make_async_remote_copy(
            src_ref=src_ref, dst_ref=dst_ref, send_sem=send_sems.at[(N_DEV - 1) * wi + k - 1],
            recv_sem=recv_sems.at[(N_DEV - 1) * wi + k - 1], device_id=to, device_id_type=MESH)

    def wait(self, gi, after):
        n = self.sizes[gi]
        copy = self._copy

        def body(*refs):
            src, land = refs[:n], refs[n:2 * n]
            send_sems, recv_sems = refs[2 * n], refs[2 * n + 1]
            for wi in range(n):
                for k in range(1, N_DEV):
                    cp = copy(src[wi], land[wi], send_sems, recv_sems, wi, k)
                    cp.wait_send()
                    cp.wait_recv()

        arrays = list(self.srcs[gi]) + list(self.lands[gi])
        outs = pl.pallas_call(
            body, name=f"{self.name}_wait{gi}",
            in_specs=[HBM_SPEC] * (2 * n) + [SEM_SPEC, SEM_SPEC] + [ANY_SPEC] * len(after),
            out_specs=[HBM_SPEC] * (2 * n),
            out_shape=[pltpu.HBM(a.shape, a.dtype) for a in arrays],
            input_output_aliases={i: i for i in range(2 * n)},
            compiler_params=pltpu.CompilerParams(has_side_effects=EFFECT),
        )(*arrays, *self.sems[gi], *after)
        return outs[:n], outs[n:]


def _rotary_tables(positions):
    rot_dim = HEAD_DIM // 4
    inv_freq = ROPE_THETA ** (-jnp.arange(0, rot_dim, 2, dtype=F32) / rot_dim)
    ang = positions.astype(F32)[:, None] * inv_freq
    cs = jnp.concatenate([jnp.cos(ang), jnp.sin(ang)], axis=1)
    dim = jnp.arange(LANES) % HEAD_DIM
    first, second = dim < ROT_SHIFT, (dim >= ROT_SHIFT) & (dim < rot_dim)
    src = jnp.arange(2 * ROT_SHIFT)[:, None]
    angle = (dim % ROT_SHIFT)[None, :]
    c = jnp.where((first | second)[None, :] & (src == angle), 1.0, 0.0)
    sa = jnp.where(second[None, :] & (src == angle + ROT_SHIFT), 1.0, 0.0)
    sb = jnp.where(first[None, :] & (src == angle + ROT_SHIFT), -1.0, 0.0)
    spread = jnp.concatenate([c, sa, sb], axis=1).astype(F32)
    base = jnp.concatenate([jnp.where(first | second, 0.0, 1.0), jnp.zeros((2 * LANES,))]).astype(F32)[None, :]
    return jnp.dot(cs, spread, precision=lax.Precision.HIGHEST, preferred_element_type=F32) + base


def _block_diag(pool_w):
    gc = pool_w.shape[-1]
    out = jnp.zeros((POOL_WIDTH, POOL_WIDTH), pool_w.dtype)
    for grp in range(pool_w.shape[0]):
        out = lax.dynamic_update_slice(out, pool_w[grp], (grp * gc, grp * gc))
    return out


def _diag_blocks(a):
    gc = POOL_WIDTH // len(POOL_WINDOWS)
    return jnp.stack([a[grp * gc:(grp + 1) * gc, grp * gc:(grp + 1) * gc] for grp in range(len(POOL_WINDOWS))])


def _local_step(x, p, positions, loss_target, norm1, pool_w, pool_scale, norm2, norm3, final_norm, weights, send):
    rc = rsa = rsb = _rotary_tables(positions)
    ones_bd = _block_diag(jnp.ones((4, HEAD_DIM, HEAD_DIM), BF16))
    saved = []
    h = x
    for i in range(2):
        tag = f"_l{i}"
        g1, g2, g3 = norm1[i:i + 1], norm2[i:i + 1], norm3[i:i + 1]
        w_bd = _block_diag(pool_w[i]).astype(BF16)
        scale = pool_scale[i:i + 1]
        w_in = weights(i, "in", (h, rc, w_bd))
        hn1, u, *qkv = _normproj_fwd(h, g1, w_in, rc, rsa, rsb, "normproj_fwd" + tag)
        qkv = [qkv[3 * grp:3 * grp + 3] for grp in range(3)]
        started = weights(i, "prefetch", (hn1,))
        o, lse = zip(*[_attn_fwd(*qkv[grp], f"attn_fwd{tag}_g{grp}", after=started) for grp in range(3)])
        w_out = weights(i, "out", o)
        h1, a, y = _outproj_fwd(h, u, w_bd, scale, o, lse, w_out, "outproj_fwd" + tag)
        w_up, w_down = weights(i, "mlp", (h1,))
        h2, hn2, r = _mlp_fwd(h1, g2, w_up, w_down, "mlp_fwd" + tag)
        w_gate, w_ple = weights(i, "gate", (h2,))
        h0 = h
        if i == 0:
            h, hn3, gate, pb = _gate_fwd(h2, g3, w_gate, p, i, w_ple, "gate_fwd" + tag)
        else:
            hn3, gate, pb, loss, dh, d_final = _gate_fwd(h2, g3, w_gate, p, i, w_ple, "gate_fwd" + tag,
                                                         head=(final_norm.reshape(1, D_MODEL), loss_target))
        saved.append(dict(h0=h0, hn1=hn1, qkv=qkv, y=y, o=o, lse=lse, a=a, h1=h1, hn2=hn2, r=r, h2=h2,
                          hn3=hn3, gate=gate, pb=pb, w_bd=w_bd, scale=scale, g1=g1, g2=g2, g3=g3,
                          w_in=w_in, w_out=w_out, w_up=w_up, w_down=w_down, w_gate=w_gate, w_ple=w_ple))

    grads = [None, None]
    sent = ()
    for i in (1, 0):
        tag = f"_l{i}"
        sv = saved[i]
        dh2, dg3, dw_gate, dw_ple = _gate_bwd(dh, sv["gate"], sv["pb"], sv["w_ple"], sv["h2"], sv["g3"], sv["w_gate"],
                                              sv["hn3"], "gate_bwd" + tag, after=sent)
        dh1, dup, dg2, dh2b = _mlp_bwd(dh2, sv["r"], sv["h1"], sv["g2"], sv["w_up"], sv["w_down"], "mlp_bwd" + tag)
        dw_down = _matmul_tn(sv["r"], dh2b, "dw_down" + tag, square_a=True)
        dw_up = _matmul_tn(sv["hn2"], dup, "dw_up" + tag, blocked_out=True)
        dpool, do0, do1, do2, de0, de1, de2, dw_out = _outproj_bwd(dh1, sv["w_out"], sv["o"], sv["lse"], ones_bd,
                                                                   sv["a"], "outproj_bwd" + tag)
        sent = send(i, "main", dict(w_gate=dw_gate, w_ple=dw_ple, w_down=dw_down, w_up=dw_up, w_out=dw_out))
        dqkv = [_attn_bwd(*sv["qkv"][grp], do_g, sv["lse"][grp], de_g, f"attn_bwd{tag}_g{grp}", after=sent)
                for grp, (do_g, de_g) in enumerate(((do0, de0), (do1, de1), (do2, de2)))]
        dq, dk, dv = zip(*dqkv)
        du, dw_bd, dscale = _pool_bwd(dpool, sv["y"], sv["w_bd"], sv["scale"], "pool_bwd" + tag, after=sent)
        dh, dz, dg1 = _normproj_bwd(dh1, du, dq, dk, dv, rc, rsa, rsb, sv["w_in"], sv["h0"], sv["g1"],
                                    "normproj_bwd" + tag)
        grads[i] = dict(norm1=dg1, norm2=dg2, norm3=dg3, pool_w=_diag_blocks(dw_bd), pool_scale=dscale)
        small_sent = send(0, "small", (grads, d_final, loss)) if i == 0 else ()
        dw_in = _matmul_tn(dz, sv["hn1"], "dw_in" + tag, tm=N_IN // 2, after=small_sent)
        sent = send(i, "in", dict(w_in=dw_in))
    return dh, sent


def _pack_small(norm1, norm2, norm3, final_norm, pool_scale, pool_w, spare=None):
    spare = jnp.zeros((1, LANES), F32) if spare is None else spare
    scale_row = jnp.concatenate([pool_scale.reshape(1, 2 * POOL_WIDTH), spare,
                                 jnp.zeros((1, D_MODEL - 2 * POOL_WIDTH - LANES), F32)], axis=1)
    return jnp.concatenate([norm1, norm2, norm3, final_norm.reshape(1, D_MODEL), scale_row,
                            pool_w.reshape(32, D_MODEL)], axis=0)


def _unpack_small(a):
    return dict(norm1=a[0:2], norm2=a[2:4], norm3=a[4:6], final_norm=a[6], pool_scale=a[7, 0:2 * POOL_WIDTH].reshape(2, POOL_WIDTH),
                pool_w=a[8:40].reshape(2, 4, HEAD_DIM, HEAD_DIM))


def _chunks_cols(a, cols):
    return a.reshape(a.shape[0], N_DEV, cols).transpose(1, 0, 2)


def _chunks_rows(a, rows):
    return a.reshape(N_DEV, rows, a.shape[1])


BIG = ("w_in", "w_out", "w_up", "w_down", "w_gate", "w_ple")
SMALL = ("norm1", "norm2", "norm3", "final_norm", "pool_scale", "pool_w")
ORDER = ("norm1", "w_in", "pool_w", "pool_scale", "w_out", "norm2", "w_up", "w_down", "norm3", "w_gate", "w_ple",
         "final_norm")


def kernel(x, p, positions, norm1, w_in, pool_w, pool_scale, w_out, norm2, w_up, w_down, norm3, w_gate, w_ple, final_norm, loss_target, m_norm1, m_w_in, m_pool_w, m_pool_scale, m_w_out, m_norm2, m_w_up, m_w_down, m_norm3, m_w_gate, m_w_ple, m_final_norm, v_norm1, v_w_in, v_pool_w, v_pool_scale, v_w_out, v_norm2, v_w_up, v_w_down, v_norm3, v_w_gate, v_w_ple, v_final_norm):
    w = dict(norm1=norm1, w_in=w_in, pool_w=pool_w, pool_scale=pool_scale, w_out=w_out, norm2=norm2, w_up=w_up,
             w_down=w_down, norm3=norm3, w_gate=w_gate, w_ple=w_ple, final_norm=final_norm)
    m = dict(norm1=m_norm1, w_in=m_w_in, pool_w=m_pool_w, pool_scale=m_pool_scale, w_out=m_w_out, norm2=m_norm2,
             w_up=m_w_up, w_down=m_w_down, norm3=m_norm3, w_gate=m_w_gate, w_ple=m_w_ple, final_norm=m_final_norm)
    v = dict(norm1=v_norm1, w_in=v_w_in, pool_w=v_pool_w, pool_scale=v_pool_scale, w_out=v_w_out, norm2=v_norm2,
             w_up=v_w_up, w_down=v_w_down, norm3=v_norm3, w_gate=v_w_gate, w_ple=v_w_ple, final_norm=v_final_norm)
    seq = x.shape[1]

    bf = {n: [w[n][layer].astype(BF16) for layer in range(2)] for n in BIG}
    bf["w_in"] = [a.T for a in bf["w_in"]]
    me = 4 * lax.axis_index("x") + 2 * lax.axis_index("y") + lax.axis_index("c")
    parts = dict(zip(("in", "out", "mlp", "gate"), (("w_in",), ("w_out",), ("w_up", "w_down"), ("w_gate", "w_ple"))))
    first = _Exchange("gather_first", [[bf["w_in"][0]]], scatter=False)
    later = [pt for pt in parts if pt != "in"]
    gathers = [_Exchange("gather_l0", [[bf[n][0] for n in parts[pt]] for pt in later], scatter=False,
                         after=(first.token,))]
    unpack = dict(w_in=lambda a: a.reshape(N_IN, D_MODEL),
                  w_out=lambda a: a.reshape(D_MODEL, D_MODEL), w_gate=lambda a: a.reshape(D_MODEL, D_MODEL),
                  w_ple=lambda a: a.transpose(1, 0, 2).reshape(PLE_DIM, D_MODEL), w_up=lambda a: a, w_down=lambda a: a)

    def weights(layer, part, after):
        if part == "prefetch":
            if layer != 0:
                return ()
            gathers.append(_Exchange("gather_l1", [[bf[n][1] for n in parts[pt]] for pt in parts], scatter=False,
                                     after=after))
            return (gathers[1].token,)
        if layer == 0:
            shards, lands = first.wait(0, after) if part == "in" else gathers[0].wait(later.index(part), after)
        else:
            shards, lands = gathers[1].wait(tuple(parts).index(part), after)
        full = [unpack[n](lax.dynamic_update_slice_in_dim(land, shard[None], me, axis=0))
                for n, shard, land in zip(parts[part], shards, lands)]
        return full if len(full) > 1 else full[0]

    to_chunks = dict(w_in=lambda a: _chunks_rows(a, N_IN // N_DEV),
                     w_out=lambda a: _chunks_rows(a, D_MODEL // N_DEV),
                     w_up=lambda a: a, w_down=lambda a: _chunks_rows(a, FF_BLOCK),
                     w_gate=lambda a: _chunks_rows(a, D_MODEL // N_DEV), w_ple=lambda a: _chunks_cols(a, D_MODEL // N_DEV))
    own = {n: [None, None] for n in BIG}
    scatters = {}

    def send(layer, part, grads):
        if part == "small":
            per_layer, d_final, loss = grads
            pack = _pack_small(
                *[jnp.concatenate([per_layer[0][n], per_layer[1][n]], axis=0) for n in ("norm1", "norm2", "norm3")],
                d_final.reshape(D_MODEL),
                jnp.concatenate([per_layer[0]["pool_scale"], per_layer[1]["pool_scale"]], axis=0),
                jnp.stack([per_layer[0]["pool_w"], per_layer[1]["pool_w"]]), spare=loss)
            scatters["small"] = _Exchange("gather_small", [[pack]], scatter=False)
            return (scatters["small"].token,)
        for n, (g32, _) in grads.items():
            own[n][layer] = to_chunks[n](g32)
        ex = _Exchange(f"scatter_{part}_l{layer}", [[to_chunks[n](g16) for n, (_, g16) in grads.items()]], scatter=True)
        scatters[layer, part] = (tuple(grads), ex)
        return (ex.token,)

    dx, sent = _local_step(
        x.reshape(seq, D_MODEL), p.reshape(2, seq, PLE_DIM), positions.reshape(seq), loss_target.reshape(seq, D_MODEL),
        norm1, pool_w, pool_scale, norm2, norm3, final_norm, weights, send)

    g_out, d_out, m_out, v_out = {}, {}, {}, {}
    my_index = me.reshape(1)
    for part in ("main", "in"):
        recv = {}
        for layer in (1, 0):
            names, ex = scatters[layer, part]
            for n, r in zip(names, ex.wait(0, sent)[1]):
                recv[n, layer] = r
        for n in names:
            grad = (*own[n], recv[n, 0], recv[n, 1])
            if n == "w_in":
                grad = _sum_chunks(grad, my_index, "sum_w_in").transpose(0, 2, 1)
            g_out[n], d_out[n], m_out[n], v_out[n] = _adamw_sharded(w[n], m[n], v[n], grad, my_index, "adamw_" + n)
        sent = tuple(d_out[n] for n in names)
    (mine,), (landed,) = scatters["small"].wait(0, sent)
    small_g8 = lax.dynamic_update_slice_in_dim(landed, mine[None], me, axis=0)
    pack = lambda t: _pack_small(*[t[n] for n in SMALL])
    small_g, d_small, m_small, v_small = _adamw_packed(pack(w), small_g8, pack(m), pack(v), "adamw_small")
    for dst, a in ((g_out, small_g), (d_out, d_small), (m_out, m_small), (v_out, v_small)):
        dst.update(_unpack_small(a))

    return (small_g[7, 2 * POOL_WIDTH],dx.reshape(1, seq, D_MODEL), *[g_out[n] for n in ORDER], *[d_out[n] for n in ORDER],
            *[m_out[n] for n in ORDER], *[v_out[n] for n in ORDER])
```

```python
import functools

import jax
import jax.numpy as jnp
from jax import lax
from jax.experimental import pallas as pl
from jax.experimental.pallas import tpu as pltpu

F32 = jnp.float32
BF16 = jnp.bfloat16

D_MODEL = 1024
HEAD_DIM = 64
POOL_WIDTH = 256
POOL_WINDOWS = (2, 4, 8, 16)
POOL_HALO = 16
POOL_PAD = 8
GROUP_WIDTH = 256
DILATIONS = (1, 4, 16)
ATTN_BLOCK = 128
ROT_SHIFT = 8
ROPE_THETA = 500000.0
D_FF = 4096
FF_BLOCK = 512
FF_PER_STEP = 2
MLP_BWD_TILE = 512
FWD_TILE = 1024
N_DEV = 8
N_IN = POOL_WIDTH + 3 * 768
PLE_DIM = 256
EPS = 1e-6
NEG_BIG = -1e30

ADAM_LR = 0.001
ADAM_B1 = 0.9
ADAM_B2 = 0.999
ADAM_EPS = 1e-08
ADAM_WD = 0.01
ADAM_STEP = 10

LANES = 128
VMEM_LIMIT = 56 * 1024 * 1024
MESH = pl.DeviceIdType.MESH


def _params(n_grid):
    return pltpu.CompilerParams(dimension_semantics=("arbitrary",) * n_grid, vmem_limit_bytes=VMEM_LIMIT)


def _dot(a, b):
    return jnp.dot(a, b, preferred_element_type=F32)


def _dot_nt(a, b):
    return lax.dot_general(a, b, (((1,), (1,)), ((), ())), preferred_element_type=F32)


def _dot_tn(a, b):
    return lax.dot_general(a, b, (((0,), (0,)), ((), ())), preferred_element_type=F32)


def _rms(x, g):
    rstd = lax.rsqrt(jnp.mean(x * x, axis=-1, keepdims=True) + EPS)
    n = x * rstd
    return n, rstd, n * g


def _rms_bwd(dy, n, rstd, g):
    dyn = dy * g
    dx = rstd * (dyn - n * jnp.mean(dyn * n, axis=-1, keepdims=True))
    return dx, jnp.sum(dy * n, axis=0, keepdims=True)


def _ordered_after(body, n_in, after):
    if not after:
        return body
    return lambda *refs: body(*refs[:n_in], *refs[n_in + len(after):])


def _resident(shape):
    return pl.BlockSpec(shape, lambda i: (0,) * len(shape), pipeline_mode=pl.Buffered(1))


def _row_tile(s, t):
    t = min(s, t)
    assert s % t == 0
    return t


def _rot(z, c, sa, sb):
    return z * c + pltpu.roll(z, ROT_SHIFT, 1) * sa + pltpu.roll(z, LANES - ROT_SHIFT, 1) * sb


def _table_specs(t):
    return [pl.BlockSpec((t, LANES), functools.partial(lambda i, k: (i, k), k=k)) for k in range(3)]


def _rot_t(dz, c, sa, sb):
    return dz * c + pltpu.roll(dz * sa, LANES - ROT_SHIFT, 1) + pltpu.roll(dz * sb, ROT_SHIFT, 1)


def _to_residues(value, stage, out_ref, dil):
    if dil == 1:
        out_ref[0] = value.astype(out_ref.dtype)
        return
    rows = value.shape[0] // dil
    for hf in range(GROUP_WIDTH // LANES):
        lanes = slice(hf * LANES, (hf + 1) * LANES)
        stage[hf][...] = value[:, lanes]
        for r in range(dil):
            out_ref[r, :, lanes] = stage[hf][pl.ds(r, rows, stride=dil), :].astype(out_ref.dtype)


def _from_residues(in_ref, stage, dil):
    if dil == 1:
        return in_ref[0].astype(F32)
    rows = in_ref.shape[1]
    for hf in range(GROUP_WIDTH // LANES):
        for r in range(dil):
            stage[hf][pl.ds(r, rows, stride=dil), :] = in_ref[r, :, hf * LANES:(hf + 1) * LANES].astype(F32)
    return jnp.concatenate([stage[0][...], stage[1][...]], axis=1)


def _residue_spec(dil, t):
    return pl.BlockSpec((dil, t // dil, GROUP_WIDTH), lambda i: (0, i, 0))


def _residue_shape(dil, s, dtype):
    return jax.ShapeDtypeStruct((dil, s // dil, GROUP_WIDTH), dtype)


def _stages(t, n):
    return [pltpu.VMEM((t, LANES), F32)] * (n * (GROUP_WIDTH // LANES))


def _pair_stages(refs):
    return [refs[i:i + 2] for i in range(0, len(refs), 2)]


def _normproj_fwd(h, g, w_in, rc, rsa, rsb, name):
    s = h.shape[0]
    t = _row_tile(s, FWD_TILE)

    def body(h_ref, g_ref, w_ref, c_ref, sa_ref, sb_ref, hn_ref, u_ref, *rest):
        qkv_refs, stages = rest[:9], _pair_stages(rest[9:])
        _, _, hn = _rms(h_ref[...], g_ref[...])
        hb = hn.astype(BF16)
        hn_ref[...] = hb
        c, sa, sb = c_ref[...], sa_ref[...], sb_ref[...]

        def rot(z, scale):
            halves = [_rot(z[:, hf * LANES:(hf + 1) * LANES], c, sa, sb) * scale for hf in range(2)]
            return jnp.concatenate(halves, axis=1)

        proj = lambda lo: _dot_nt(hb, w_ref[lo:lo + GROUP_WIDTH, :])
        u_ref[...] = proj(0)
        for grp, dil in enumerate(DILATIONS):
            lo = POOL_WIDTH + grp * GROUP_WIDTH
            q_ref, k_ref, v_ref = qkv_refs[3 * grp:3 * grp + 3]
            _to_residues(rot(proj(lo), HEAD_DIM ** -0.5), stages[0], q_ref, dil)
            _to_residues(rot(proj(lo + 768), 1.0), stages[1], k_ref, dil)
            _to_residues(proj(lo + 1536), stages[2], v_ref, dil)

    row = lambda w: pl.BlockSpec((t, w), lambda i: (i, 0))
    return pl.pallas_call(
        body, name=name, grid=(s // t,),
        in_specs=[row(D_MODEL), pl.BlockSpec((1, D_MODEL), lambda i: (0, 0)),
                  _resident((N_IN, D_MODEL))] + _table_specs(t),
        out_specs=[row(D_MODEL), row(POOL_WIDTH)] + [_residue_spec(dil, t) for dil in DILATIONS for _ in range(3)],
        out_shape=[jax.ShapeDtypeStruct((s, D_MODEL), BF16), jax.ShapeDtypeStruct((s, POOL_WIDTH), F32)]
        + [_residue_shape(dil, s, BF16) for dil in DILATIONS for _ in range(3)],
        scratch_shapes=_stages(t, 3),
        compiler_params=_params(1),
    )(h, g, w_in, rc, rsa, rsb)


def _pool_lane_window():
    lane = lax.broadcasted_iota(jnp.int32, (1, POOL_WIDTH), 1)
    return jnp.left_shift(2, lane // (POOL_WIDTH // len(POOL_WINDOWS)))


def _window_sums(ext, b2, b4, b8, t, lo, tile, direction):
    rows = t + POOL_HALO
    for src, dst, sh in ((ext, b2, 1), (b2, b4, 2), (b4, b8, 4)):
        dst[lo:lo + rows, :] = src[lo:lo + rows, :] + src[lo + direction * sh:lo + direction * sh + rows, :]
    s16 = b8[tile:tile + t, :] + b8[tile + direction * 8:tile + direction * 8 + t, :]
    win = _pool_lane_window()
    return jnp.where(win == 2, b2[tile:tile + t, :],
                     jnp.where(win == 4, b4[tile:tile + t, :], jnp.where(win == 8, b8[tile:tile + t, :], s16)))


def _pool_fwd_tile(i, u_ref, w_ref, sc_ref, y_ref, ext, b2, b4, b8):
    t = u_ref.shape[0]
    first = POOL_PAD + POOL_HALO

    @pl.when(i == 0)
    def _():
        for buf in (ext, b2, b4):
            buf[0:POOL_PAD, :] = jnp.zeros((POOL_PAD, POOL_WIDTH), F32)
        ext[POOL_PAD:first, :] = jnp.zeros((POOL_HALO, POOL_WIDTH), F32)

    x = u_ref[...]
    ext[first:, :] = x
    wsum = _window_sums(ext, b2, b4, b8, t, POOL_PAD, first, -1)
    pos = i * t + lax.broadcasted_iota(jnp.int32, (t, POOL_WIDTH), 0)
    cnt = jnp.minimum(pos + 1, _pool_lane_window()).astype(F32)
    yb = (wsum / cnt - x).astype(BF16)
    y_ref[...] = yb
    ext[POOL_PAD:first, :] = x[t - POOL_HALO:, :]
    return _dot(yb, w_ref[...]) * sc_ref[...]


def _head_masks():
    lane = lax.broadcasted_iota(jnp.int32, (ATTN_BLOCK, GROUP_WIDTH), 1)
    return [lane // HEAD_DIM == hd for hd in range(GROUP_WIDTH // HEAD_DIM)]


def _stack_heads(a, masks):
    zero = jnp.zeros_like(a)
    return jnp.concatenate([jnp.where(m, a, zero) for m in masks], axis=0)


def _band_bias(first_step):
    rows = ATTN_BLOCK * (GROUP_WIDTH // HEAD_DIM)
    i = lax.broadcasted_iota(jnp.int32, (rows, 2 * ATTN_BLOCK), 0) & (ATTN_BLOCK - 1)
    j = lax.broadcasted_iota(jnp.int32, (rows, 2 * ATTN_BLOCK), 1)
    inner = jnp.where((j >= i) & (j <= i + ATTN_BLOCK), 0.0, NEG_BIG)
    return jnp.where((j < ATTN_BLOCK) & first_step, NEG_BIG, inner), inner


def _column_per_head(a):
    return jnp.concatenate([a[:, hd * HEAD_DIM:hd * HEAD_DIM + 1] for hd in range(GROUP_WIDTH // HEAD_DIM)], axis=0)


def _blocks_per_step(nb):
    if nb <= 16:
        return nb
    return next(qb for qb in (16, 8, 4, 2, 1) if nb % qb == 0)


def _residues_per_step(dil, nb, qb):
    return 2 if (nb == qb and qb < 8 and dil % 2 == 0) else 1


def _attn_fwd(q, k, v, name, after=()):
    dil, length, _ = q.shape
    nb = length // ATTN_BLOCK
    qb = _blocks_per_step(nb)
    rs = _residues_per_step(dil, nb, qb)

    def body(q_ref, kp_ref, kc_ref, vp_ref, vc_ref, o_ref, lse_ref):
        masks = _head_masks()
        bias = _band_bias(pl.program_id(1) == 0)
        for rr in range(rs):
            for qi in range(qb):
                here = slice(qi * ATTN_BLOCK, (qi + 1) * ATTN_BLOCK)
                before = slice((qi - 1) * ATTN_BLOCK, qi * ATTN_BLOCK)
                kcat = jnp.concatenate([kp_ref[rr] if qi == 0 else kc_ref[rr, before], kc_ref[rr, here]], axis=0)
                vcat = jnp.concatenate([vp_ref[rr] if qi == 0 else vc_ref[rr, before], vc_ref[rr, here]], axis=0)
                qs = _stack_heads(q_ref[rr, here], masks)
                sc = _dot_nt(qs, kcat) + bias[min(qi, 1)]
                m = jnp.max(sc, axis=1, keepdims=True)
                e = jnp.exp(sc - m)
                l = jnp.sum(e, axis=1, keepdims=True)
                p = (e / l).astype(BF16)
                lse = m + jnp.log(l)
                o = jnp.zeros((ATTN_BLOCK, GROUP_WIDTH), F32)
                lse_full = jnp.zeros((ATTN_BLOCK, GROUP_WIDTH), F32)
                for hd, msk in enumerate(masks):
                    rows = slice(hd * ATTN_BLOCK, (hd + 1) * ATTN_BLOCK)
                    o = jnp.where(msk, _dot(p[rows], vcat), o)
                    lse_full = jnp.where(msk, lse[rows], lse_full)
                o_ref[rr, here] = o.astype(o_ref.dtype)
                lse_ref[rr, here] = lse_full

    cur = pl.BlockSpec((rs, qb * ATTN_BLOCK, GROUP_WIDTH), lambda r, j: (r, j, 0))
    prev = pl.BlockSpec((rs, ATTN_BLOCK, GROUP_WIDTH), lambda r, j: (r, jnp.maximum(qb * j - 1, 0), 0))
    return pl.pallas_call(
        _ordered_after(body, 5, after), name=name, grid=(dil // rs, nb // qb),
        in_specs=[cur, prev, cur, prev, cur] + [pl.BlockSpec(memory_space=pl.ANY)] * len(after), out_specs=[cur, cur],
        out_shape=[jax.ShapeDtypeStruct(q.shape, BF16), jax.ShapeDtypeStruct(q.shape, F32)],
        compiler_params=_params(2),
    )(q, k, k, v, v, *after)


def _group_weights(l0, l1, l2):
    m = jnp.maximum(jnp.maximum(l0, l1), l2)
    e0, e1, e2 = jnp.exp(l0 - m), jnp.exp(l1 - m), jnp.exp(l2 - m)
    den = e0 + e1 + e2
    return e0 / den, e1 / den, e2 / den


def _outproj_fwd(h, u, w_bd, scale, o, lse, w_out, name):
    s = h.shape[0]
    t = _row_tile(s, FWD_TILE)

    def body(h_ref, u_ref, wbd_ref, sc_ref, o0, o1, o2, l0, l1, l2, w_ref, out_ref, a_ref, y_ref, ext, b2, b4, b8,
             *stages):
        pool_out = _pool_fwd_tile(pl.program_id(0), u_ref, wbd_ref, sc_ref, y_ref, ext, b2, b4, b8)
        stages = _pair_stages(stages)
        ov = [_from_residues(r, stages[i], DILATIONS[i]) for i, r in enumerate((o0, o1, o2))]
        lv = [_from_residues(r, stages[3 + i], DILATIONS[i]) for i, r in enumerate((l0, l1, l2))]
        wts = _group_weights(*lv)
        a = jnp.concatenate([pool_out] + [ov[i] * wts[i] for i in range(3)], axis=1).astype(BF16)
        a_ref[...] = a
        out_ref[...] = h_ref[...] + _dot(a, w_ref[...])

    row = lambda w: pl.BlockSpec((t, w), lambda i: (i, 0))
    res = [_residue_spec(dil, t) for dil in DILATIONS]
    return pl.pallas_call(
        body, name=name, grid=(s // t,),
        in_specs=[row(D_MODEL), row(POOL_WIDTH), _resident((POOL_WIDTH, POOL_WIDTH)), _resident((1, POOL_WIDTH))]
        + res + res + [_resident((D_MODEL, D_MODEL))],
        out_specs=[row(D_MODEL), row(D_MODEL), row(POOL_WIDTH)],
        out_shape=[jax.ShapeDtypeStruct((s, D_MODEL), F32), jax.ShapeDtypeStruct((s, D_MODEL), BF16),
                   jax.ShapeDtypeStruct((s, POOL_WIDTH), BF16)],
        scratch_shapes=[pltpu.VMEM((t + POOL_HALO + POOL_PAD, POOL_WIDTH), F32)] * 4 + _stages(t, 6),
        compiler_params=_params(1),
    )(h, u, w_bd, scale, *o, *lse, w_out)


def _mlp_fwd(h, g, w_up, w_down, name):
    s = h.shape[0]
    t = _row_tile(s, 512)
    nblk = D_FF // FF_BLOCK

    def body(h_ref, g_ref, wu_ref, wd_ref, out_ref, hn_ref, r_ref):
        x = h_ref[...]
        _, _, hn = _rms(x, g_ref[...])
        hb = hn.astype(BF16)
        hn_ref[...] = hb
        acc = None
        for b0 in range(0, nblk, FF_PER_STEP):
            acts = []
            for b in range(b0, b0 + FF_PER_STEP):
                r = jnp.maximum(_dot(hb, wu_ref[b]), 0.0)
                r_ref[:, b * FF_BLOCK:(b + 1) * FF_BLOCK] = r.astype(BF16)
                acts.append((r * r).astype(BF16))
            wd = wd_ref[b0:b0 + FF_PER_STEP].reshape(FF_PER_STEP * FF_BLOCK, D_MODEL)
            part = _dot(jnp.concatenate(acts, axis=1), wd)
            acc = part if acc is None else acc + part
        out_ref[...] = x + acc

    row = lambda w: pl.BlockSpec((t, w), lambda i: (i, 0))
    resident = lambda shape: pl.BlockSpec(shape, lambda i: (0, 0, 0), pipeline_mode=pl.Buffered(1))
    return pl.pallas_call(
        body, name=name, grid=(s // t,),
        in_specs=[row(D_MODEL), pl.BlockSpec((1, D_MODEL), lambda i: (0, 0)),
                  resident((nblk, D_MODEL, FF_BLOCK)), resident((nblk, FF_BLOCK, D_MODEL))],
        out_specs=[row(D_MODEL), row(D_MODEL), row(D_FF)],
        out_shape=[jax.ShapeDtypeStruct((s, D_MODEL), F32), jax.ShapeDtypeStruct((s, D_MODEL), BF16),
                   jax.ShapeDtypeStruct((s, D_FF), BF16)],
        compiler_params=_params(1),
    )(h, g, w_up, w_down)


def _gate_fwd(h, g, w_gate, p, layer, w_ple, name, head=None):
    s = h.shape[0]
    t = _row_tile(s, FWD_TILE)

    def body(h_ref, g_ref, wg_ref, p_ref, wp_ref, *refs):
        x = h_ref[...]
        _, _, hn = _rms(x, g_ref[...])
        hb = hn.astype(BF16)
        gate = 1.0 / (1.0 + jnp.exp(-_dot(hb, wg_ref[...])))
        pb = p_ref[...].astype(BF16)
        h3 = x + gate * _dot(pb, wp_ref[...])
        if head is None:
            out_ref, hn_ref, gate_ref, pb_ref = refs
            out_ref[...] = h3
        else:
            gf_ref, t_ref, hn_ref, gate_ref, pb_ref, loss_ref, dh_ref, dgf_ref = refs

            @pl.when(pl.program_id(0) == 0)
            def _():
                loss_ref[...] = jnp.zeros_like(loss_ref)
                dgf_ref[...] = jnp.zeros_like(dgf_ref)

            gf = gf_ref[...]
            n, rstd, y = _rms(h3, gf)
            err = y - t_ref[...]
            loss_ref[...] += jnp.sum(err * err) * (0.5 / D_MODEL)
            dh_ref[...], dgf = _rms_bwd(err * (1.0 / D_MODEL), n, rstd, gf)
            dgf_ref[...] += dgf
        hn_ref[...] = hb
        pb_ref[...] = pb
        gate_ref[...] = gate.astype(BF16)

    row = lambda w: pl.BlockSpec((t, w), lambda i: (i, 0))
    full = lambda a, b: pl.BlockSpec((a, b), lambda i: (0, 0))
    in_specs = [row(D_MODEL), full(1, D_MODEL), _resident((D_MODEL, D_MODEL)),
                pl.BlockSpec((None, t, PLE_DIM), lambda i: (layer, i, 0)), _resident((PLE_DIM, D_MODEL))]
    saved_specs = [row(D_MODEL), row(D_MODEL), row(PLE_DIM)]
    saved_shapes = [jax.ShapeDtypeStruct((s, D_MODEL), BF16), jax.ShapeDtypeStruct((s, D_MODEL), BF16),
                    jax.ShapeDtypeStruct((s, PLE_DIM), BF16)]
    if head is None:
        return pl.pallas_call(
            body, name=name, grid=(s // t,), in_specs=in_specs, out_specs=[row(D_MODEL)] + saved_specs,
            out_shape=[jax.ShapeDtypeStruct((s, D_MODEL), F32)] + saved_shapes, compiler_params=_params(1),
        )(h, g, w_gate, p, w_ple)
    return pl.pallas_call(
        body, name=name, grid=(s // t,), in_specs=in_specs + [full(1, D_MODEL), row(D_MODEL)],
        out_specs=saved_specs + [pl.BlockSpec((1, LANES), lambda i: (0, 0)), row(D_MODEL), full(1, D_MODEL)],
        out_shape=saved_shapes + [jax.ShapeDtypeStruct((1, LANES), F32), jax.ShapeDtypeStruct((s, D_MODEL), F32),
                                  jax.ShapeDtypeStruct((1, D_MODEL), F32)],
        compiler_params=_params(1),
    )(h, g, w_gate, p, w_ple, *head)


def _gate_bwd(dh, gate, pb, w_ple, h, g, w_gate, hn, name, after=()):
    s = h.shape[0]
    t = _row_tile(s, FWD_TILE)
    last = s // t - 1

    def body(dh_ref, gate_ref, pb_ref, wp_ref, h_ref, g_ref, wg_ref, hn_ref, out_ref, dg_ref, dwg_ref, dwgb_ref,
             dwp_ref, dwpb_ref):
        i = pl.program_id(0)

        @pl.when(i == 0)
        def _():
            dg_ref[...] = jnp.zeros_like(dg_ref)
            dwg_ref[...] = jnp.zeros_like(dwg_ref)
            dwp_ref[...] = jnp.zeros_like(dwp_ref)

        d = dh_ref[...]
        gate = gate_ref[...].astype(F32)
        pb = pb_ref[...]
        e = _dot(pb, wp_ref[...])
        dgl = (d * e * gate * (1.0 - gate)).astype(BF16)
        dwg_ref[...] += _dot_tn(hn_ref[...], dgl)
        dwp_ref[...] += _dot_tn(pb, (d * gate).astype(BF16))
        gv = g_ref[...]
        n, rstd, _ = _rms(h_ref[...], gv)
        dx, dg = _rms_bwd(_dot_nt(dgl, wg_ref[...]), n, rstd, gv)
        out_ref[...] = d + dx
        dg_ref[...] += dg

        @pl.when(i == last)
        def _():
            dwgb_ref[...] = dwg_ref[...].astype(BF16)
            dwpb_ref[...] = dwp_ref[...].astype(BF16)

    row = lambda w: pl.BlockSpec((t, w), lambda i: (i, 0))
    full = lambda a, b: pl.BlockSpec((a, b), lambda i: (0, 0))
    dh2, dg, dwg, dwgb, dwp, dwpb = pl.pallas_call(
        _ordered_after(body, 8, after), name=name, grid=(s // t,),
        in_specs=[row(D_MODEL), row(D_MODEL), row(PLE_DIM), _resident((PLE_DIM, D_MODEL)), row(D_MODEL),
                  full(1, D_MODEL), _resident((D_MODEL, D_MODEL)), row(D_MODEL)]
        + [pl.BlockSpec(memory_space=pl.ANY)] * len(after),
        out_specs=[row(D_MODEL), full(1, D_MODEL), full(D_MODEL, D_MODEL), full(D_MODEL, D_MODEL),
                   full(PLE_DIM, D_MODEL), full(PLE_DIM, D_MODEL)],
        out_shape=[jax.ShapeDtypeStruct((s, D_MODEL), F32), jax.ShapeDtypeStruct((1, D_MODEL), F32),
                   jax.ShapeDtypeStruct((D_MODEL, D_MODEL), F32), jax.ShapeDtypeStruct((D_MODEL, D_MODEL), BF16),
                   jax.ShapeDtypeStruct((PLE_DIM, D_MODEL), F32), jax.ShapeDtypeStruct((PLE_DIM, D_MODEL), BF16)],
        compiler_params=_params(1),
    )(dh, gate, pb, w_ple, h, g, w_gate, hn, *after)
    return dh2, dg, (dwg, dwgb), (dwp, dwpb)


def _mlp_bwd(dh, r, h, g, w_up, w_down, name):
    s = h.shape[0]
    t = _row_tile(s, MLP_BWD_TILE)
    nblk = D_FF // FF_BLOCK

    def body(dh_ref, r_ref, h_ref, g_ref, wu_ref, wd_ref, out_ref, dup_ref, dg_ref, dhb_ref):
        @pl.when(pl.program_id(0) == 0)
        def _():
            dg_ref[...] = jnp.zeros_like(dg_ref)

        d = dh_ref[...]
        db = d.astype(BF16)
        dhb_ref[...] = db
        back = None
        for b in range(nblk):
            cols = slice(b * FF_BLOCK, (b + 1) * FF_BLOCK)
            dup = (_dot_nt(db, wd_ref[b]) * (2.0 * r_ref[:, cols].astype(F32))).astype(BF16)
            dup_ref[:, cols] = dup
            part = _dot_nt(dup, wu_ref[b])
            back = part if back is None else back + part
        gv = g_ref[...]
        n, rstd, _ = _rms(h_ref[...], gv)
        dx, dg = _rms_bwd(back, n, rstd, gv)
        out_ref[...] = d + dx
        dg_ref[...] += dg

    row = lambda w: pl.BlockSpec((t, w), lambda i: (i, 0))
    vec = pl.BlockSpec((1, D_MODEL), lambda i: (0, 0))
    resident = lambda shape: pl.BlockSpec(shape, lambda i: (0, 0, 0), pipeline_mode=pl.Buffered(1))
    return pl.pallas_call(
        body, name=name, grid=(s // t,),
        in_specs=[row(D_MODEL), row(D_FF), row(D_MODEL), vec,
                  resident((nblk, D_MODEL, FF_BLOCK)), resident((nblk, FF_BLOCK, D_MODEL))],
        out_specs=[row(D_MODEL), row(D_FF), vec, row(D_MODEL)],
        out_shape=[jax.ShapeDtypeStruct((s, D_MODEL), F32), jax.ShapeDtypeStruct((s, D_FF), BF16),
                   jax.ShapeDtypeStruct((1, D_MODEL), F32), jax.ShapeDtypeStruct((s, D_MODEL), BF16)],
        compiler_params=_params(1),
    )(dh, r, h, g, w_up, w_down)


def _outproj_bwd(dh, w_out, o, lse, ones_bd, a, name):
    s = dh.shape[0]
    t = _row_tile(s, 512)
    last = s // t - 1

    def body(dh_ref, w_ref, o0, o1, o2, l0, l1, l2, bd_ref, a_ref, dp_ref, do0, do1, do2, de0, de1, de2, dw_ref,
             dwb_ref, *stages):
        i = pl.program_id(0)

        @pl.when(i == 0)
        def _():
            dw_ref[...] = jnp.zeros_like(dw_ref)

        stages = _pair_stages(stages)
        dhb = dh_ref[...].astype(BF16)
        dw_ref[...] += _dot_tn(a_ref[...], dhb)

        @pl.when(i == last)
        def _():
            dwb_ref[...] = dw_ref[...].astype(BF16)

        da = _dot_nt(dhb, w_ref[...])
        dp_ref[...] = da[:, 0:POOL_WIDTH]
        ov =[_from_residues(r, stages[i], DILATIONS[i]) for i, r in enumerate((o0, o1, o2))]
        lv = [_from_residues(r, stages[3 + i], DILATIONS[i]) for i, r in enumerate((l0, l1, l2))]
        wts = _group_weights(*lv)
        bd = bd_ref[...]
        cbar = jnp.zeros((t, GROUP_WIDTH), F32)
        for grp, do_ref in enumerate((do0, do1, do2)):
            lo = POOL_WIDTH + grp * GROUP_WIDTH
            dag = da[:, lo:lo + GROUP_WIDTH]
            _to_residues(dag * wts[grp], stages[6 + grp], do_ref, DILATIONS[grp])
            prod = dag * ov[grp]
            hi = prod.astype(BF16)
            low = (prod - hi.astype(F32)).astype(BF16)
            cbar = cbar + wts[grp] * (_dot(hi, bd) + _dot(low, bd))
        for grp, de_ref in enumerate((de0, de1, de2)):
            _to_residues(wts[grp] * cbar, stages[9 + grp], de_ref, DILATIONS[grp])

    row = lambda w: pl.BlockSpec((t, w), lambda i: (i, 0))
    full = lambda a, b: pl.BlockSpec((a, b), lambda i: (0, 0))
    res = [_residue_spec(dil, t) for dil in DILATIONS]
    *outs, dw, dwb = pl.pallas_call(
        body, name=name, grid=(s // t,),
        in_specs=[row(D_MODEL), full(D_MODEL, D_MODEL)] + res + res + [full(GROUP_WIDTH, GROUP_WIDTH), row(D_MODEL)],
        out_specs=[row(POOL_WIDTH)] + res + res + [full(D_MODEL, D_MODEL)] * 2,
        out_shape=[jax.ShapeDtypeStruct((s, POOL_WIDTH), F32)] + [_residue_shape(dil, s, BF16) for dil in DILATIONS]
        + [_residue_shape(dil, s, F32) for dil in DILATIONS]
        + [jax.ShapeDtypeStruct((D_MODEL, D_MODEL), F32), jax.ShapeDtypeStruct((D_MODEL, D_MODEL), BF16)],
        scratch_shapes=_stages(t, 12),
        compiler_params=_params(1),
    )(dh, w_out, *o, *lse, ones_bd, a)
    return (*outs, (dw, dwb))


def _attn_bwd(q, k, v, do, lse, deff, name, after=()):
    dil, length, _ = q.shape
    nb = length // ATTN_BLOCK
    qb = _blocks_per_step(nb)
    nj = nb // qb
    rs = _residues_per_step(dil, nb, qb)
    whole = nj == 1
    tail = slice((qb - 1) * ATTN_BLOCK, qb * ATTN_BLOCK)
    block = lambda qi: slice(qi * ATTN_BLOCK, (qi + 1) * ATTN_BLOCK)

    def body(q_ref, kp_ref, kc_ref, vp_ref, vc_ref, do_ref, lse_ref, de_ref, dq_ref, dk_ref, dv_ref, ck, cv):
        j = pl.program_id(1)

        def compute():
            masks = _head_masks()
            bias = _band_bias(j == 0)
            for rr in range(rs):
                dkc, dvc = [], []
                for qi in range(qb):
                    here, before = block(qi), block(qi - 1)
                    kcat = jnp.concatenate([kp_ref[rr] if qi == 0 else kc_ref[rr, before], kc_ref[rr, here]], axis=0)
                    vcat = jnp.concatenate([vp_ref[rr] if qi == 0 else vc_ref[rr, before], vc_ref[rr, here]], axis=0)
                    qs = _stack_heads(q_ref[rr, here], masks)
                    dos = _stack_heads(do_ref[rr, here], masks)
                    sc = _dot_nt(qs, kcat) + bias[min(qi, 1)]
                    p = jnp.exp(sc - _column_per_head(lse_ref[rr, here]))
                    ds = (p * (_dot_nt(dos, vcat) - _column_per_head(de_ref[rr, here]))).astype(BF16)
                    dq = jnp.zeros((ATTN_BLOCK, GROUP_WIDTH), F32)
                    for hd, msk in enumerate(masks):
                        dq = jnp.where(msk, _dot(ds[block(hd)], kcat), dq)
                    dq_ref[rr, here] = dq.astype(dq_ref.dtype)
                    dkc.append(_dot_tn(ds, qs))
                    dvc.append(_dot_tn(p.astype(BF16), dos))

                for out_ref, carry, parts in ((dk_ref, ck, dkc), (dv_ref, cv, dvc)):
                    full = [parts[qi][ATTN_BLOCK:] + parts[qi + 1][0:ATTN_BLOCK] for qi in range(qb - 1)]
                    if whole:
                        for qi, val in enumerate(full + [parts[qb - 1][ATTN_BLOCK:]]):
                            out_ref[rr, block(qi)] = val.astype(out_ref.dtype)
                        continue

                    @pl.when(j > 0)
                    def _():
                        if qb > 1:
                            out_ref[0, 0:(qb - 1) * ATTN_BLOCK] = carry[0:(qb - 1) * ATTN_BLOCK].astype(out_ref.dtype)
                        out_ref[0, tail] = (carry[tail] + parts[0][0:ATTN_BLOCK]).astype(out_ref.dtype)

                    for qi, val in enumerate(full):
                        carry[block(qi)] = val
                    carry[tail] = parts[qb - 1][ATTN_BLOCK:]

        if whole:
            compute()
        else:
            pl.when(j < nj)(compute)

            @pl.when(j == nj)
            def _():
                dk_ref[0] = ck[...].astype(dk_ref.dtype)
                dv_ref[0] = cv[...].astype(dv_ref.dtype)

    step = lambda j: jnp.minimum(j, nj - 1)
    cur = pl.BlockSpec((rs, qb * ATTN_BLOCK, GROUP_WIDTH), lambda r, j: (r, step(j), 0))
    prev = pl.BlockSpec((rs, ATTN_BLOCK, GROUP_WIDTH), lambda r, j: (r, jnp.maximum(qb * step(j) - 1, 0), 0))
    late = pl.BlockSpec((rs, qb * ATTN_BLOCK, GROUP_WIDTH), lambda r, j: (r, jnp.maximum(j - 1, 0), 0))
    return pl.pallas_call(
        _ordered_after(body, 8, after), name=name, grid=(dil // rs, 1 if whole else nj + 1),
        in_specs=[cur, prev, cur, prev, cur, cur, cur, cur] + [pl.BlockSpec(memory_space=pl.ANY)] * len(after),
        out_specs=[cur, cur if whole else late, cur if whole else late],
        out_shape=[jax.ShapeDtypeStruct(q.shape, BF16)] * 3,
        scratch_shapes=[pltpu.VMEM((qb * ATTN_BLOCK, GROUP_WIDTH), F32)] * 2,
        compiler_params=_params(2),
    )(q, k, k, v, v, do, lse, deff, *after)


def _pool_bwd(dpool, y, w_bd, scale, name, after=()):
    s = dpool.shape[0]
    t = _row_tile(s, 512)
    nt = s // t

    def body(dp_ref, y_ref, w_ref, sc_ref, du_ref, dw_ref, dsc_ref, ext, b2, b4, b8):
        i = pl.program_id(0)

        @pl.when(i == 0)
        def _():
            ext[t:, :] = jnp.zeros((POOL_HALO + POOL_PAD, POOL_WIDTH), F32)
            for buf in (b2, b4):
                buf[t + POOL_HALO:, :] = jnp.zeros((POOL_PAD, POOL_WIDTH), F32)
            dw_ref[...] = jnp.zeros_like(dw_ref)
            dsc_ref[...] = jnp.zeros_like(dsc_ref)

        dp = dp_ref[...]
        yb = y_ref[...]
        w = w_ref[...]
        dsc_ref[...] += jnp.sum(dp * _dot(yb, w), axis=0, keepdims=True)
        dyo = (dp * sc_ref[...]).astype(BF16)
        dw_ref[...] += _dot_tn(yb, dyo)
        dy = _dot_nt(dyo, w)
        win = _pool_lane_window()
        pos = (nt - 1 - i) * t + lax.broadcasted_iota(jnp.int32, (t, POOL_WIDTH), 0)
        gq = dy / jnp.minimum(pos + 1, win).astype(F32)
        ext[0:t, :] = gq
        du_ref[...] = _window_sums(ext, b2, b4, b8, t, 0, 0, 1) - dy
        ext[t:t + POOL_HALO, :] = gq[0:POOL_HALO, :]

    rev = pl.BlockSpec((t, POOL_WIDTH), lambda i: (nt - 1 - i, 0))
    full = lambda a, b: pl.BlockSpec((a, b), lambda i: (0, 0))
    return pl.pallas_call(
        _ordered_after(body, 4, after), name=name, grid=(nt,),
        in_specs=[rev, rev, full(POOL_WIDTH, POOL_WIDTH), full(1, POOL_WIDTH)]
        + [pl.BlockSpec(memory_space=pl.ANY)] * len(after),
        out_specs=[rev, full(POOL_WIDTH, POOL_WIDTH), full(1, POOL_WIDTH)],
        out_shape=[jax.ShapeDtypeStruct((s, POOL_WIDTH), F32), jax.ShapeDtypeStruct((POOL_WIDTH, POOL_WIDTH), F32),
                   jax.ShapeDtypeStruct((1, POOL_WIDTH), F32)],
        scratch_shapes=[pltpu.VMEM((t + POOL_HALO + POOL_PAD, POOL_WIDTH), F32)] * 4,
        compiler_params=_params(1),
    )(dpool, y, w_bd, scale, *after)


def _normproj_bwd(dh, du, dq, dk, dv, rc, rsa, rsb, w_in, h, g, name):
    s = h.shape[0]
    t = _row_tile(s, 512)

    def body(dh_ref, du_ref, q0, q1, q2, k0, k1, k2, v0, v1, v2, c_ref, sa_ref, sb_ref, w_ref, h_ref, g_ref,
             out_ref, dz_ref, dg_ref, *stages):
        @pl.when(pl.program_id(0) == 0)
        def _():
            dg_ref[...] = jnp.zeros_like(dg_ref)

        c, sa, sb = c_ref[...], sa_ref[...], sb_ref[...]

        def unrot(a, scale):
            halves = [_rot_t(a[:, hf * LANES:(hf + 1) * LANES] * scale, c, sa, sb) for hf in range(2)]
            return jnp.concatenate(halves, axis=1)

        staged = _pair_stages(stages)
        tok = lambda refs, base: [_from_residues(r, staged[base + i], DILATIONS[i]) for i, r in enumerate(refs)]
        chunks = [du_ref[...]]
        chunks += [unrot(a, HEAD_DIM ** -0.5) for a in tok((q0, q1, q2), 0)]
        chunks += [unrot(a, 1.0) for a in tok((k0, k1, k2), 3)]
        chunks += tok((v0, v1, v2), 6)
        acc = jnp.zeros((t, D_MODEL), F32)
        for ci, ch in enumerate(chunks):
            cols = slice(ci * GROUP_WIDTH, (ci + 1) * GROUP_WIDTH)
            cb = ch.astype(BF16)
            dz_ref[:, cols] = cb
            acc = acc + _dot(cb, w_ref[cols, :])
        gv = g_ref[...]
        n, rstd, _ = _rms(h_ref[...], gv)
        dx, dg = _rms_bwd(acc, n, rstd, gv)
        out_ref[...] = dh_ref[...] + dx
        dg_ref[...] += dg

    row = lambda w: pl.BlockSpec((t, w), lambda i: (i, 0))
    vec = pl.BlockSpec((1, D_MODEL), lambda i: (0, 0))
    res = [_residue_spec(dil, t) for dil in DILATIONS]
    return pl.pallas_call(
        body, name=name, grid=(s // t,),
        in_specs=[row(D_MODEL), row(POOL_WIDTH)] + res * 3 + _table_specs(t)
        + [pl.BlockSpec((N_IN, D_MODEL), lambda i: (0, 0)), row(D_MODEL), vec],
        out_specs=[row(D_MODEL), row(N_IN), vec],
        out_shape=[jax.ShapeDtypeStruct((s, D_MODEL), F32), jax.ShapeDtypeStruct((s, N_IN), BF16),
                   jax.ShapeDtypeStruct((1, D_MODEL), F32)],
        scratch_shapes=_stages(t, 9),
        compiler_params=_params(1),
    )(dh, du, *dq, *dk, *dv, rc, rsa, rsb, w_in, h, g)


def _matmul_tn(a, b, name, *, square_a=False, tm=None, tn=None, blocked_out=False, after=()):
    s, m = a.shape
    n = b.shape[1]
    tk = _row_tile(s, 2048)
    tm = tm or min(m, 1024)
    tn = tn or min(n, 1024)
    assert m % tm == 0 and n % tn == 0
    nk = s // tk
    nsub = tn // FF_BLOCK if blocked_out else 1

    def body(a_ref, b_ref, o_ref, ob_ref, acc):
        k = pl.program_id(2)

        def product():
            av = a_ref[...]
            if square_a:
                av = av.astype(F32)
                av = av * av
            return _dot_tn(av.astype(BF16), b_ref[...].astype(BF16))

        def emit(total):
            if blocked_out:
                for sub in range(nsub):
                    cols = slice(sub * FF_BLOCK, (sub + 1) * FF_BLOCK)
                    o_ref[sub] = total[:, cols]
                    ob_ref[sub] = total[:, cols].astype(BF16)
            else:
                o_ref[...] = total
                ob_ref[...] = total.astype(BF16)

        if nk == 1:
            emit(product())
            return

        @pl.when(k == 0)
        def _():
            acc[...] = product()

        @pl.when((k > 0) & (k < nk - 1))
        def _():
            acc[...] += product()

        @pl.when(k == nk - 1)
        def _():
            emit(acc[...] + product())

    if blocked_out:
        shape = (n // FF_BLOCK, m, FF_BLOCK)
        out_spec = pl.BlockSpec((nsub, tm, FF_BLOCK), lambda i, j, k: (j, i, 0))
    else:
        shape = (m, n)
        out_spec = pl.BlockSpec((tm, tn), lambda i, j, k: (i, j))
    return pl.pallas_call(
        _ordered_after(body, 2, after), name=name, grid=(m // tm, n // tn, nk),
        in_specs=[pl.BlockSpec((tk, tm), lambda i, j, k: (k, i)), pl.BlockSpec((tk, tn), lambda i, j, k: (k, j))]
        + [pl.BlockSpec(memory_space=pl.ANY)] * len(after),
        out_specs=[out_spec, out_spec],
        out_shape=[jax.ShapeDtypeStruct(shape, F32), jax.ShapeDtypeStruct(shape, BF16)],
        scratch_shapes=[pltpu.VMEM((tm, tn), F32)],
        compiler_params=_params(3),
    )(a, b, *after)


def _adamw_math(w, g, m, v):
    m = ADAM_B1 * m + (1.0 - ADAM_B1) * g
    v = ADAM_B2 * v + (1.0 - ADAM_B2) * (g * g)
    m_hat = m / (1.0 - ADAM_B1 ** ADAM_STEP)
    v_hat = v / (1.0 - ADAM_B2 ** ADAM_STEP)
    delta = -ADAM_LR * (m_hat / (jnp.sqrt(v_hat) + ADAM_EPS) + ADAM_WD * w)
    return delta, m, v


def _sum_chunks_body(own0_ref, own1_ref, r0_ref, r1_ref):
    layer0 = pl.program_id(0) == 0
    g = jnp.where(layer0, own0_ref[...], own1_ref[...])
    for k in range(N_DEV - 1):
        g = g + jnp.where(layer0, r0_ref[k], r1_ref[k]).astype(F32)
    return g


def _chunk_specs(t, cols):
    rows_of = lambda layer: (lambda l, i: jnp.where(l == layer, i, 0))
    blk = pl.BlockSpec((None, t, cols), lambda l, i, me: (l, i, 0))
    own = [pl.BlockSpec((None, t, cols), functools.partial(lambda l, i, me, pick: (me[0], pick(l, i), 0), pick=rows_of(ly)))
           for ly in range(2)]
    recv = [pl.BlockSpec((N_DEV - 1, t, cols), functools.partial(lambda l, i, me, pick: (0, pick(l, i), 0), pick=rows_of(ly)))
            for ly in range(2)]
    return blk, own + recv


def _sum_chunks(chunks, me, name):
    _, rows, cols = chunks[0].shape
    t = _row_tile(rows, 320)

    def body(me_ref, own0_ref, own1_ref, r0_ref, r1_ref, g_ref):
        g_ref[...] = _sum_chunks_body(own0_ref, own1_ref, r0_ref, r1_ref)

    blk, chunk_specs = _chunk_specs(t, cols)
    return pl.pallas_call(
        body, name=name,
        grid_spec=pltpu.PrefetchScalarGridSpec(num_scalar_prefetch=1, grid=(2, rows // t), in_specs=chunk_specs,
                                               out_specs=blk),
        out_shape=jax.ShapeDtypeStruct((2, rows, cols), F32), compiler_params=_params(2),
    )(me, *chunks)


def _adamw_sharded(w, m, v, grad, me, name):
    _, rows, cols = w.shape
    t = _row_tile(rows, 256)
    summed = not isinstance(grad, tuple)
    grad = (grad,) if summed else grad

    def body(me_ref, w_ref, m_ref, v_ref, *refs):
        g_ref, d_ref, nm_ref, nv_ref = refs[-4:]
        g = refs[0][...] if summed else _sum_chunks_body(*refs[:4])
        g_ref[...] = g
        d_ref[...], nm_ref[...], nv_ref[...] = _adamw_math(w_ref[...], g, m_ref[...], v_ref[...])

    blk, chunk_specs = _chunk_specs(t, cols)
    return pl.pallas_call(
        body, name=name,
        grid_spec=pltpu.PrefetchScalarGridSpec(
            num_scalar_prefetch=1, grid=(2, rows // t),
            in_specs=[blk, blk, blk] + ([blk] if summed else chunk_specs), out_specs=[blk] * 4),
        out_shape=[jax.ShapeDtypeStruct(w.shape, F32)] * 4,
        compiler_params=_params(2),
    )(me, w, m, v, *grad)


def _adamw_packed(w, g8, m, v, name):
    def body(w_ref, g_ref, m_ref, v_ref, go_ref, d_ref, nm_ref, nv_ref):
        g = g_ref[0]
        for dev in range(1, N_DEV):
            g = g + g_ref[dev]
        go_ref[...] = g
        d_ref[...], nm_ref[...], nv_ref[...] = _adamw_math(w_ref[...], g, m_ref[...], v_ref[...])

    return pl.pallas_call(
        body, name=name, out_shape=[jax.ShapeDtypeStruct(w.shape, F32)] * 4,
        compiler_params=pltpu.CompilerParams(vmem_limit_bytes=VMEM_LIMIT),
    )(w, g8, m, v)


def _peer(k):
    x, y, c = lax.axis_index("x"), lax.axis_index("y"), lax.axis_index("c")
    return (1 - x if k & 4 else x, 1 - y if k & 2 else y, 1 - c if k & 1 else c)


def _linear(dev):
    return 4 * dev[0] + 2 * dev[1] + dev[2]


HBM_SPEC = pl.BlockSpec(memory_space=pltpu.HBM)
SEM_SPEC = pl.BlockSpec(memory_space=pltpu.SEMAPHORE)
ANY_SPEC = pl.BlockSpec(memory_space=pl.ANY)
EFFECT = pltpu.SideEffectType.DATAFLOW_SIDE_EFFECTING


def _in_hbm(a):
    return pltpu.with_memory_space_constraint(a, pltpu.HBM)


class _Exchange:
    def __init__(self, name, groups, scatter, after=()):
        self.name, self.scatter = name, scatter
        self.sizes = sizes = [len(g) for g in groups]
        srcs = [a for g in groups for a in g]
        n, ng = len(srcs), len(groups)
        lead = (N_DEV - 1,) if scatter else (N_DEV,)
        shapes = [lead + (a.shape[1:] if scatter else a.shape) for a in srcs]
        lands = [lax.empty(sh, a.dtype) for sh, a in zip(shapes, srcs)]
        offsets = [sum(sizes[:gi]) for gi in range(ng)]
        copy = self._copy

        def body(*refs):
            src, land = refs[:n], refs[n:2 * n]
            sems = refs[2 * n + len(after):2 * n + len(after) + 2 * ng]
            token = refs[-1]
            for gi in range(ng):
                for wi in range(sizes[gi]):
                    w = offsets[gi] + wi
                    for k in range(1, N_DEV):
                        copy(src[w], land[w], sems[2 * gi], sems[2 * gi + 1], wi, k).start()
            token[...] = jnp.zeros_like(token)

        sem_shapes = [pltpu.SemaphoreType.DMA(((N_DEV - 1) * sz,)) for sz in sizes for _ in range(2)]
        outs = pl.pallas_call(
            body, name=name + "_start",
            in_specs=[HBM_SPEC] * (2 * n) + [ANY_SPEC] * len(after),
            out_specs=[SEM_SPEC] * (2 * ng) + [HBM_SPEC] * (2 * n) + [pl.BlockSpec(memory_space=pltpu.VMEM)],
            out_shape=sem_shapes + [pltpu.HBM(a.shape, a.dtype) for a in srcs + lands]
            + [jax.ShapeDtypeStruct((8, LANES), F32)],
            input_output_aliases={i: 2 * ng + i for i in range(2 * n)},
            compiler_params=pltpu.CompilerParams(has_side_effects=EFFECT),
        )(*[_in_hbm(a) for a in srcs + lands], *after)
        self.sems = [outs[2 * gi:2 * gi + 2] for gi in range(ng)]
        thru = outs[2 * ng:2 * ng + 2 * n]
        self.srcs = [thru[offsets[gi]:offsets[gi] + sizes[gi]] for gi in range(ng)]
        self.lands = [thru[n + offsets[gi]:n + offsets[gi] + sizes[gi]] for gi in range(ng)]
        self.token = outs[-1]

    def _copy(self, src, land, send_sems, recv_sems, wi, k):
        to = _peer(k)
        if self.scatter:
            src_ref, dst_ref = src.at[_linear(to)], land.at[k - 1]
        else:
            src_ref, dst_ref = src, land.at[_linear(_peer(0))]
        return pltpu.make_async_remote_copy(
            src_ref=src_ref, dst_ref=dst_ref, send_sem=send_sems.at[(N_DEV - 1) * wi + k - 1],
            recv_sem=recv_sems.at[(N_DEV - 1) * wi + k - 1], device_id=to, device_id_type=MESH)

    def wait(self, gi, after):
        n = self.sizes[gi]
        copy = self._copy

        def body(*refs):
            src, land = refs[:n], refs[n:2 * n]
            send_sems, recv_sems = refs[2 * n], refs[2 * n + 1]
            for wi in range(n):
                for k in range(1, N_DEV):
                    cp = copy(src[wi], land[wi], send_sems, recv_sems, wi, k)
                    cp.wait_send()
                    cp.wait_recv()

        arrays = list(self.srcs[gi]) + list(self.lands[gi])
        outs = pl.pallas_call(
            body, name=f"{self.name}_wait{gi}",
            in_specs=[HBM_SPEC] * (2 * n) + [SEM_SPEC, SEM_SPEC] + [ANY_SPEC] * len(after),
            out_specs=[HBM_SPEC] * (2 * n),
            out_shape=[pltpu.HBM(a.shape, a.dtype) for a in arrays],
            input_output_aliases={i: i for i in range(2 * n)},
            compiler_params=pltpu.CompilerParams(has_side_effects=EFFECT),
        )(*arrays, *self.sems[gi], *after)
        return outs[:n], outs[n:]


def _rotary_tables(positions):
    rot_dim = HEAD_DIM // 4
    inv_freq = ROPE_THETA ** (-jnp.arange(0, rot_dim, 2, dtype=F32) / rot_dim)
    ang = positions.astype(F32)[:, None] * inv_freq
    cs = jnp.concatenate([jnp.cos(ang), jnp.sin(ang)], axis=1)
    dim = jnp.arange(LANES) % HEAD_DIM
    first, second = dim < ROT_SHIFT, (dim >= ROT_SHIFT) & (dim < rot_dim)
    src = jnp.arange(2 * ROT_SHIFT)[:, None]
    angle = (dim % ROT_SHIFT)[None, :]
    c = jnp.where((first | second)[None, :] & (src == angle), 1.0, 0.0)
    sa = jnp.where(second[None, :] & (src == angle + ROT_SHIFT), 1.0, 0.0)
    sb = jnp.where(first[None, :] & (src == angle + ROT_SHIFT), -1.0, 0.0)
    spread = jnp.concatenate([c, sa, sb], axis=1).astype(F32)
    base = jnp.concatenate([jnp.where(first | second, 0.0, 1.0), jnp.zeros((2 * LANES,))]).astype(F32)[None, :]
    return jnp.dot(cs, spread, precision=lax.Precision.HIGHEST, preferred_element_type=F32) + base


def _block_diag(pool_w):
    gc = pool_w.shape[-1]
    out = jnp.zeros((POOL_WIDTH, POOL_WIDTH), pool_w.dtype)
    for grp in range(pool_w.shape[0]):
        out = lax.dynamic_update_slice(out, pool_w[grp], (grp * gc, grp * gc))
    return out


def _diag_blocks(a):
    gc = POOL_WIDTH // len(POOL_WINDOWS)
    return jnp.stack([a[grp * gc:(grp + 1) * gc, grp * gc:(grp + 1) * gc] for grp in range(len(POOL_WINDOWS))])


def _local_step(x, p, positions, loss_target, norm1, pool_w, pool_scale, norm2, norm3, final_norm, weights, send):
    rc = rsa = rsb = _rotary_tables(positions)
    ones_bd = _block_diag(jnp.ones((4, HEAD_DIM, HEAD_DIM), BF16))
    saved = []
    h = x
    for i in range(2):
        tag = f"_l{i}"
        g1, g2, g3 = norm1[i:i + 1], norm2[i:i + 1], norm3[i:i + 1]
        w_bd = _block_diag(pool_w[i]).astype(BF16)
        scale = pool_scale[i:i + 1]
        w_in = weights(i, "in", (h, rc, w_bd))
        hn1, u, *qkv = _normproj_fwd(h, g1, w_in, rc, rsa, rsb, "normproj_fwd" + tag)
        qkv = [qkv[3 * grp:3 * grp + 3] for grp in range(3)]
        started = weights(i, "prefetch", (hn1,))
        o, lse = zip(*[_attn_fwd(*qkv[grp], f"attn_fwd{tag}_g{grp}", after=started) for grp in range(3)])
        w_out = weights(i, "out", o)
        h1, a, y = _outproj_fwd(h, u, w_bd, scale, o, lse, w_out, "outproj_fwd" + tag)
        w_up, w_down = weights(i, "mlp", (h1,))
        h2, hn2, r = _mlp_fwd(h1, g2, w_up, w_down, "mlp_fwd" + tag)
        w_gate, w_ple = weights(i, "gate", (h2,))
        h0 = h
        if i == 0:
            h, hn3, gate, pb = _gate_fwd(h2, g3, w_gate, p, i, w_ple, "gate_fwd" + tag)
        else:
            hn3, gate, pb, loss, dh, d_final = _gate_fwd(h2, g3, w_gate, p, i, w_ple, "gate_fwd" + tag,
                                                         head=(final_norm.reshape(1, D_MODEL), loss_target))
        saved.append(dict(h0=h0, hn1=hn1, qkv=qkv, y=y, o=o, lse=lse, a=a, h1=h1, hn2=hn2, r=r, h2=h2,
                          hn3=hn3, gate=gate, pb=pb, w_bd=w_bd, scale=scale, g1=g1, g2=g2, g3=g3,
                          w_in=w_in, w_out=w_out, w_up=w_up, w_down=w_down, w_gate=w_gate, w_ple=w_ple))

    grads = [None, None]
    sent = ()
    for i in (1, 0):
        tag = f"_l{i}"
        sv = saved[i]
        dh2, dg3, dw_gate, dw_ple = _gate_bwd(dh, sv["gate"], sv["pb"], sv["w_ple"], sv["h2"], sv["g3"], sv["w_gate"],
                                              sv["hn3"], "gate_bwd" + tag, after=sent)
        dh1, dup, dg2, dh2b = _mlp_bwd(dh2, sv["r"], sv["h1"], sv["g2"], sv["w_up"], sv["w_down"], "mlp_bwd" + tag)
        dw_down = _matmul_tn(sv["r"], dh2b, "dw_down" + tag, square_a=True)
        dw_up = _matmul_tn(sv["hn2"], dup, "dw_up" + tag, blocked_out=True)
        dpool, do0, do1, do2, de0, de1, de2, dw_out = _outproj_bwd(dh1, sv["w_out"], sv["o"], sv["lse"], ones_bd,
                                                                   sv["a"], "outproj_bwd" + tag)
        sent = send(i, "main", dict(w_gate=dw_gate, w_ple=dw_ple, w_down=dw_down, w_up=dw_up, w_out=dw_out))
        dqkv = [_attn_bwd(*sv["qkv"][grp], do_g, sv["lse"][grp], de_g, f"attn_bwd{tag}_g{grp}", after=sent)
                for grp, (do_g, de_g) in enumerate(((do0, de0), (do1, de1), (do2, de2)))]
        dq, dk, dv = zip(*dqkv)
        du, dw_bd, dscale = _pool_bwd(dpool, sv["y"], sv["w_bd"], sv["scale"], "pool_bwd" + tag, after=sent)
        dh, dz, dg1 = _normproj_bwd(dh1, du, dq, dk, dv, rc, rsa, rsb, sv["w_in"], sv["h0"], sv["g1"],
                                    "normproj_bwd" + tag)
        grads[i] = dict(norm1=dg1, norm2=dg2, norm3=dg3, pool_w=_diag_blocks(dw_bd), pool_scale=dscale)
        small_sent = send(0, "small", (grads, d_final, loss)) if i == 0 else ()
        dw_in = _matmul_tn(dz, sv["hn1"], "dw_in" + tag, tm=N_IN // 2, after=small_sent)
        sent = send(i, "in", dict(w_in=dw_in))
    return dh, sent


def _pack_small(norm1, norm2, norm3, final_norm, pool_scale, pool_w, spare=None):
    spare = jnp.zeros((1, LANES), F32) if spare is None else spare
    scale_row = jnp.concatenate([pool_scale.reshape(1, 2 * POOL_WIDTH), spare,
                                 jnp.zeros((1, D_MODEL - 2 * POOL_WIDTH - LANES), F32)], axis=1)
    return jnp.concatenate([norm1, norm2, norm3, final_norm.reshape(1, D_MODEL), scale_row,
                            pool_w.reshape(32, D_MODEL)], axis=0)


def _unpack_small(a):
    return dict(norm1=a[0:2], norm2=a[2:4], norm3=a[4:6], final_norm=a[6], pool_scale=a[7, 0:2 * POOL_WIDTH].reshape(2, POOL_WIDTH),
                pool_w=a[8:40].reshape(2, 4, HEAD_DIM, HEAD_DIM))


def _chunks_cols(a, cols):
    return a.reshape(a.shape[0], N_DEV, cols).transpose(1, 0, 2)


def _chunks_rows(a, rows):
    return a.reshape(N_DEV, rows, a.shape[1])


BIG = ("w_in", "w_out", "w_up", "w_down", "w_gate", "w_ple")
SMALL = ("norm1", "norm2", "norm3", "final_norm", "pool_scale", "pool_w")
ORDER = ("norm1", "w_in", "pool_w", "pool_scale", "w_out", "norm2", "w_up", "w_down", "norm3", "w_gate", "w_ple",
         "final_norm")


def kernel(x, p, positions, norm1, w_in, pool_w, pool_scale, w_out, norm2, w_up, w_down, norm3, w_gate, w_ple, final_norm, loss_target, m_norm1, m_w_in, m_pool_w, m_pool_scale, m_w_out, m_norm2, m_w_up, m_w_down, m_norm3, m_w_gate, m_w_ple, m_final_norm, v_norm1, v_w_in, v_pool_w, v_pool_scale, v_w_out, v_norm2, v_w_up, v_w_down, v_norm3, v_w_gate, v_w_ple, v_final_norm):
    w = dict(norm1=norm1, w_in=w_in, pool_w=pool_w, pool_scale=pool_scale, w_out=w_out, norm2=norm2, w_up=w_up,
             w_down=w_down, norm3=norm3, w_gate=w_gate, w_ple=w_ple, final_norm=final_norm)
    m = dict(norm1=m_norm1, w_in=m_w_in, pool_w=m_pool_w, pool_scale=m_pool_scale, w_out=m_w_out, norm2=m_norm2,
             w_up=m_w_up, w_down=m_w_down, norm3=m_norm3, w_gate=m_w_gate, w_ple=m_w_ple, final_norm=m_final_norm)
    v = dict(norm1=v_norm1, w_in=v_w_in, pool_w=v_pool_w, pool_scale=v_pool_scale, w_out=v_w_out, norm2=v_norm2,
             w_up=v_w_up, w_down=v_w_down, norm3=v_norm3, w_gate=v_w_gate, w_ple=v_w_ple, final_norm=v_final_norm)
    seq = x.shape[1]

    bf = {n: [w[n][layer].astype(BF16) for layer in range(2)] for n in BIG}
    bf["w_in"] = [a.T for a in bf["w_in"]]
    me = 4 * lax.axis_index("x") + 2 * lax.axis_index("y") + lax.axis_index("c")
    parts = dict(zip(("in", "out", "mlp", "gate"), (("w_in",), ("w_out",), ("w_up", "w_down"), ("w_gate", "w_ple"))))
    first = _Exchange("gather_first", [[bf["w_in"][0]]], scatter=False)
    later = [pt for pt in parts if pt != "in"]
    gathers = [_Exchange("gather_l0", [[bf[n][0] for n in parts[pt]] for pt in later], scatter=False,
                         after=(first.token,))]
    unpack = dict(w_in=lambda a: a.reshape(N_IN, D_MODEL),
                  w_out=lambda a: a.reshape(D_MODEL, D_MODEL), w_gate=lambda a: a.reshape(D_MODEL, D_MODEL),
                  w_ple=lambda a: a.transpose(1, 0, 2).reshape(PLE_DIM, D_MODEL), w_up=lambda a: a, w_down=lambda a: a)

    def weights(layer, part, after):
        if part == "prefetch":
            if layer != 0:
                return ()
            gathers.append(_Exchange("gather_l1", [[bf[n][1] for n in parts[pt]] for pt in parts], scatter=False,
                                     after=after))
            return (gathers[1].token,)
        if layer == 0 and part == "in":
            shards, lands = first.wait(0, (*after, gathers[0].token))
        elif layer == 0:
            shards, lands = gathers[0].wait(later.index(part), after)
        else:
            shards, lands = gathers[1].wait(tuple(parts).index(part), after)
        full = [unpack[n](lax.dynamic_update_slice_in_dim(land, shard[None], me, axis=0))
                for n, shard, land in zip(parts[part], shards, lands)]
        return full if len(full) > 1 else full[0]

    to_chunks = dict(w_in=lambda a: _chunks_rows(a, N_IN // N_DEV),
                     w_out=lambda a: _chunks_rows(a, D_MODEL // N_DEV),
                     w_up=lambda a: a, w_down=lambda a: _chunks_rows(a, FF_BLOCK),
                     w_gate=lambda a: _chunks_rows(a, D_MODEL // N_DEV), w_ple=lambda a: _chunks_cols(a, D_MODEL // N_DEV))
    own = {n: [None, None] for n in BIG}
    scatters = {}

    def send(layer, part, grads):
        if part == "small":
            per_layer, d_final, loss = grads
            pack = _pack_small(
                *[jnp.concatenate([per_layer[0][n], per_layer[1][n]], axis=0) for n in ("norm1", "norm2", "norm3")],
                d_final.reshape(D_MODEL),
                jnp.concatenate([per_layer[0]["pool_scale"], per_layer[1]["pool_scale"]], axis=0),
                jnp.stack([per_layer[0]["pool_w"], per_layer[1]["pool_w"]]), spare=loss)
            scatters["small"] = _Exchange("gather_small", [[pack]], scatter=False)
            return (scatters["small"].token,)
        for n, (g32, _) in grads.items():
            own[n][layer] = to_chunks[n](g32)
        ex = _Exchange(f"scatter_{part}_l{layer}", [[to_chunks[n](g16) for n, (_, g16) in grads.items()]], scatter=True)
        scatters[layer, part] = (tuple(grads), ex)
        return (ex.token,)

    dx, sent = _local_step(
        x.reshape(seq, D_MODEL), p.reshape(2, seq, PLE_DIM), positions.reshape(seq), loss_target.reshape(seq, D_MODEL),
        norm1, pool_w, pool_scale, norm2, norm3, final_norm, weights, send)

    g_out, d_out, m_out, v_out = {}, {}, {}, {}
    my_index = me.reshape(1)
    for part in ("main", "in"):
        recv = {}
        for layer in (1, 0):
            names, ex = scatters[layer, part]
            for n, r in zip(names, ex.wait(0, sent)[1]):
                recv[n, layer] = r
        for n in names:
            grad = (*own[n], recv[n, 0], recv[n, 1])
            if n == "w_in":
                grad = _sum_chunks(grad, my_index, "sum_w_in").transpose(0, 2, 1)
            g_out[n], d_out[n], m_out[n], v_out[n] = _adamw_sharded(w[n], m[n], v[n], grad, my_index, "adamw_" + n)
        sent = tuple(d_out[n] for n in names)
    (mine,), (landed,) = scatters["small"].wait(0, sent)
    small_g8 = lax.dynamic_update_slice_in_dim(landed, mine[None], me, axis=0)
    pack = lambda t: _pack_small(*[t[n] for n in SMALL])
    small_g, d_small, m_small, v_small = _adamw_packed(pack(w), small_g8, pack(m), pack(v), "adamw_small")
    for dst, a in ((g_out, small_g), (d_out, d_small), (m_out, m_small), (v_out, v_small)):
        dst.update(_unpack_small(a))

    return (small_g[7, 2 * POOL_WIDTH],dx.reshape(1, seq, D_MODEL), *[g_out[n] for n in ORDER], *[d_out[n] for n in ORDER],
            *[m_out[n] for n in ORDER], *[v_out[n] for n in ORDER])
```

```python
import functools

import jax
import jax.numpy as jnp
from jax import lax
from jax.experimental import pallas as pl
from jax.experimental.pallas import tpu as pltpu

F32 = jnp.float32
BF16 = jnp.bfloat16

D_MODEL = 1024
HEAD_DIM = 64
POOL_WIDTH = 256
POOL_WINDOWS = (2, 4, 8, 16)
POOL_HALO = 16
POOL_PAD = 8
GROUP_WIDTH = 256
DILATIONS = (1, 4, 16)
ATTN_BLOCK = 128
ROT_SHIFT = 8
ROPE_THETA = 500000.0
D_FF = 4096
FF_BLOCK = 512
FF_PER_STEP = 2
MLP_BWD_TILE = 512
FWD_TILE = 1024
N_DEV = 8
N_IN = POOL_WIDTH + 3 * 768
PLE_DIM = 256
EPS = 1e-6
NEG_BIG = -1e30

ADAM_LR = 0.001
ADAM_B1 = 0.9
ADAM_B2 = 0.999
ADAM_EPS = 1e-08
ADAM_WD = 0.01
ADAM_STEP = 10

LANES = 128
VMEM_LIMIT = 56 * 1024 * 1024
MESH = pl.DeviceIdType.MESH


def _params(n_grid):
    return pltpu.CompilerParams(dimension_semantics=("arbitrary",) * n_grid, vmem_limit_bytes=VMEM_LIMIT)


def _dot(a, b):
    return jnp.dot(a, b, preferred_element_type=F32)


def _dot_nt(a, b):
    return lax.dot_general(a, b, (((1,), (1,)), ((), ())), preferred_element_type=F32)


def _dot_tn(a, b):
    return lax.dot_general(a, b, (((0,), (0,)), ((), ())), preferred_element_type=F32)


def _rms(x, g):
    rstd = lax.rsqrt(jnp.mean(x * x, axis=-1, keepdims=True) + EPS)
    n = x * rstd
    return n, rstd, n * g


def _rms_bwd(dy, n, rstd, g):
    dyn = dy * g
    dx = rstd * (dyn - n * jnp.mean(dyn * n, axis=-1, keepdims=True))
    return dx, jnp.sum(dy * n, axis=0, keepdims=True)


def _ordered_after(body, n_in, after):
    if not after:
        return body
    return lambda *refs: body(*refs[:n_in], *refs[n_in + len(after):])


def _resident(shape):
    return pl.BlockSpec(shape, lambda i: (0,) * len(shape), pipeline_mode=pl.Buffered(1))


def _row_tile(s, t):
    t = min(s, t)
    assert s % t == 0
    return t


def _rot(z, c, sa, sb):
    return z * c + pltpu.roll(z, ROT_SHIFT, 1) * sa + pltpu.roll(z, LANES - ROT_SHIFT, 1) * sb


def _table_specs(t):
    return [pl.BlockSpec((t, LANES), functools.partial(lambda i, k: (i, k), k=k)) for k in range(3)]


def _rot_t(dz, c, sa, sb):
    return dz * c + pltpu.roll(dz * sa, LANES - ROT_SHIFT, 1) + pltpu.roll(dz * sb, ROT_SHIFT, 1)


def _to_residues(value, stage, out_ref, dil):
    if dil == 1:
        out_ref[0] = value.astype(out_ref.dtype)
        return
    rows = value.shape[0] // dil
    for hf in range(GROUP_WIDTH // LANES):
        lanes = slice(hf * LANES, (hf + 1) * LANES)
        stage[hf][...] = value[:, lanes]
        for r in range(dil):
            out_ref[r, :, lanes] = stage[hf][pl.ds(r, rows, stride=dil), :].astype(out_ref.dtype)


def _from_residues(in_ref, stage, dil):
    if dil == 1:
        return in_ref[0].astype(F32)
    rows = in_ref.shape[1]
    for hf in range(GROUP_WIDTH // LANES):
        for r in range(dil):
            stage[hf][pl.ds(r, rows, stride=dil), :] = in_ref[r, :, hf * LANES:(hf + 1) * LANES].astype(F32)
    return jnp.concatenate([stage[0][...], stage[1][...]], axis=1)


def _residue_spec(dil, t):
    return pl.BlockSpec((dil, t // dil, GROUP_WIDTH), lambda i: (0, i, 0))


def _residue_shape(dil, s, dtype):
    return jax.ShapeDtypeStruct((dil, s // dil, GROUP_WIDTH), dtype)


def _stages(t, n):
    return [pltpu.VMEM((t, LANES), F32)] * (n * (GROUP_WIDTH // LANES))


def _pair_stages(refs):
    return [refs[i:i + 2] for i in range(0, len(refs), 2)]


def _normproj_tile(x, g_ref, w_ref, c_ref, sa_ref, sb_ref, hn_ref, u_ref, *rest):
    qkv_refs, stages = rest[:9], _pair_stages(rest[9:])
    _, _, hn = _rms(x, g_ref[...])
    hb = hn.astype(BF16)
    hn_ref[...] = hb
    c, sa, sb = c_ref[...], sa_ref[...], sb_ref[...]

    def rot(z, scale):
        halves = [_rot(z[:, hf * LANES:(hf + 1) * LANES], c, sa, sb) * scale for hf in range(2)]
        return jnp.concatenate(halves, axis=1)

    proj = lambda lo: _dot_nt(hb, w_ref[lo:lo + GROUP_WIDTH, :])
    u_ref[...] = proj(0)
    for grp, dil in enumerate(DILATIONS):
        lo = POOL_WIDTH + grp * GROUP_WIDTH
        q_ref, k_ref, v_ref = qkv_refs[3 * grp:3 * grp + 3]
        _to_residues(rot(proj(lo), HEAD_DIM ** -0.5), stages[0], q_ref, dil)
        _to_residues(rot(proj(lo + 768), 1.0), stages[1], k_ref, dil)
        _to_residues(proj(lo + 1536), stages[2], v_ref, dil)


def _normproj_operands(s, t):
    row = lambda w: pl.BlockSpec((t, w), lambda i: (i, 0))
    in_specs = [pl.BlockSpec((1, D_MODEL), lambda i: (0, 0)), _resident((N_IN, D_MODEL))] + _table_specs(t)
    out_specs = [row(D_MODEL), row(POOL_WIDTH)] + [_residue_spec(dil, t) for dil in DILATIONS for _ in range(3)]
    out_shape = [jax.ShapeDtypeStruct((s, D_MODEL), BF16), jax.ShapeDtypeStruct((s, POOL_WIDTH), F32)]
    out_shape += [_residue_shape(dil, s, BF16) for dil in DILATIONS for _ in range(3)]
    return in_specs, out_specs, out_shape, _stages(t, 3)


def _normproj_fwd(h, g, w_in, rc, rsa, rsb, name):
    s = h.shape[0]
    t = _row_tile(s, FWD_TILE)

    def body(h_ref, *refs):
        _normproj_tile(h_ref[...], *refs)

    in_specs, out_specs, out_shape, scratch = _normproj_operands(s, t)
    return pl.pallas_call(
        body, name=name, grid=(s // t,), in_specs=[pl.BlockSpec((t, D_MODEL), lambda i: (i, 0))] + in_specs,
        out_specs=out_specs, out_shape=out_shape, scratch_shapes=scratch, compiler_params=_params(1),
    )(h, g, w_in, rc, rsa, rsb)


def _pool_lane_window():
    lane = lax.broadcasted_iota(jnp.int32, (1, POOL_WIDTH), 1)
    return jnp.left_shift(2, lane // (POOL_WIDTH // len(POOL_WINDOWS)))


def _window_sums(ext, b2, b4, b8, t, lo, tile, direction):
    rows = t + POOL_HALO
    for src, dst, sh in ((ext, b2, 1), (b2, b4, 2), (b4, b8, 4)):
        dst[lo:lo + rows, :] = src[lo:lo + rows, :] + src[lo + direction * sh:lo + direction * sh + rows, :]
    s16 = b8[tile:tile + t, :] + b8[tile + direction * 8:tile + direction * 8 + t, :]
    win = _pool_lane_window()
    return jnp.where(win == 2, b2[tile:tile + t, :],
                     jnp.where(win == 4, b4[tile:tile + t, :], jnp.where(win == 8, b8[tile:tile + t, :], s16)))


def _pool_fwd_tile(i, u_ref, w_ref, sc_ref, y_ref, ext, b2, b4, b8):
    t = u_ref.shape[0]
    first = POOL_PAD + POOL_HALO

    @pl.when(i == 0)
    def _():
        for buf in (ext, b2, b4):
            buf[0:POOL_PAD, :] = jnp.zeros((POOL_PAD, POOL_WIDTH), F32)
        ext[POOL_PAD:first, :] = jnp.zeros((POOL_HALO, POOL_WIDTH), F32)

    x = u_ref[...]
    ext[first:, :] = x
    wsum = _window_sums(ext, b2, b4, b8, t, POOL_PAD, first, -1)
    pos = i * t + lax.broadcasted_iota(jnp.int32, (t, POOL_WIDTH), 0)
    cnt = jnp.minimum(pos + 1, _pool_lane_window()).astype(F32)
    yb = (wsum / cnt - x).astype(BF16)
    y_ref[...] = yb
    ext[POOL_PAD:first, :] = x[t - POOL_HALO:, :]
    return _dot(yb, w_ref[...]) * sc_ref[...]


def _head_masks():
    lane = lax.broadcasted_iota(jnp.int32, (ATTN_BLOCK, GROUP_WIDTH), 1)
    return [lane // HEAD_DIM == hd for hd in range(GROUP_WIDTH // HEAD_DIM)]


def _stack_heads(a, masks):
    zero = jnp.zeros_like(a)
    return jnp.concatenate([jnp.where(m, a, zero) for m in masks], axis=0)


def _band_bias(first_step):
    rows = ATTN_BLOCK * (GROUP_WIDTH // HEAD_DIM)
    i = lax.broadcasted_iota(jnp.int32, (rows, 2 * ATTN_BLOCK), 0) & (ATTN_BLOCK - 1)
    j = lax.broadcasted_iota(jnp.int32, (rows, 2 * ATTN_BLOCK), 1)
    inner = jnp.where((j >= i) & (j <= i + ATTN_BLOCK), 0.0, NEG_BIG)
    return jnp.where((j < ATTN_BLOCK) & first_step, NEG_BIG, inner), inner


def _column_per_head(a):
    return jnp.concatenate([a[:, hd * HEAD_DIM:hd * HEAD_DIM + 1] for hd in range(GROUP_WIDTH // HEAD_DIM)], axis=0)


def _blocks_per_step(nb):
    if nb <= 16:
        return nb
    return next(qb for qb in (16, 8, 4, 2, 1) if nb % qb == 0)


def _residues_per_step(dil, nb, qb):
    return 2 if (nb == qb and qb < 8 and dil % 2 == 0) else 1


def _attn_fwd(q, k, v, name, after=()):
    dil, length, _ = q.shape
    nb = length // ATTN_BLOCK
    qb = _blocks_per_step(nb)
    rs = _residues_per_step(dil, nb, qb)

    def body(q_ref, kp_ref, kc_ref, vp_ref, vc_ref, o_ref, lse_ref):
        masks = _head_masks()
        bias = _band_bias(pl.program_id(1) == 0)
        for rr in range(rs):
            for qi in range(qb):
                here = slice(qi * ATTN_BLOCK, (qi + 1) * ATTN_BLOCK)
                before = slice((qi - 1) * ATTN_BLOCK, qi * ATTN_BLOCK)
                kcat = jnp.concatenate([kp_ref[rr] if qi == 0 else kc_ref[rr, before], kc_ref[rr, here]], axis=0)
                vcat = jnp.concatenate([vp_ref[rr] if qi == 0 else vc_ref[rr, before], vc_ref[rr, here]], axis=0)
                qs = _stack_heads(q_ref[rr, here], masks)
                sc = _dot_nt(qs, kcat) + bias[min(qi, 1)]
                m = jnp.max(sc, axis=1, keepdims=True)
                e = jnp.exp(sc - m)
                l = jnp.sum(e, axis=1, keepdims=True)
                p = (e / l).astype(BF16)
                lse = m + jnp.log(l)
                o = jnp.zeros((ATTN_BLOCK, GROUP_WIDTH), F32)
                lse_full = jnp.zeros((ATTN_BLOCK, GROUP_WIDTH), F32)
                for hd, msk in enumerate(masks):
                    rows = slice(hd * ATTN_BLOCK, (hd + 1) * ATTN_BLOCK)
                    o = jnp.where(msk, _dot(p[rows], vcat), o)
                    lse_full = jnp.where(msk, lse[rows], lse_full)
                o_ref[rr, here] = o.astype(o_ref.dtype)
                lse_ref[rr, here] = lse_full

    cur = pl.BlockSpec((rs, qb * ATTN_BLOCK, GROUP_WIDTH), lambda r, j: (r, j, 0))
    prev = pl.BlockSpec((rs, ATTN_BLOCK, GROUP_WIDTH), lambda r, j: (r, jnp.maximum(qb * j - 1, 0), 0))
    return pl.pallas_call(
        _ordered_after(body, 5, after), name=name, grid=(dil // rs, nb // qb),
        in_specs=[cur, prev, cur, prev, cur] + [pl.BlockSpec(memory_space=pl.ANY)] * len(after), out_specs=[cur, cur],
        out_shape=[jax.ShapeDtypeStruct(q.shape, BF16), jax.ShapeDtypeStruct(q.shape, F32)],
        compiler_params=_params(2),
    )(q, k, k, v, v, *after)


def _group_weights(l0, l1, l2):
    m = jnp.maximum(jnp.maximum(l0, l1), l2)
    e0, e1, e2 = jnp.exp(l0 - m), jnp.exp(l1 - m), jnp.exp(l2 - m)
    den = e0 + e1 + e2
    return e0 / den, e1 / den, e2 / den


def _outproj_fwd(h, u, w_bd, scale, o, lse, w_out, name):
    s = h.shape[0]
    t = _row_tile(s, FWD_TILE)

    def body(h_ref, u_ref, wbd_ref, sc_ref, o0, o1, o2, l0, l1, l2, w_ref, out_ref, a_ref, y_ref, ext, b2, b4, b8,
             *stages):
        pool_out = _pool_fwd_tile(pl.program_id(0), u_ref, wbd_ref, sc_ref, y_ref, ext, b2, b4, b8)
        stages = _pair_stages(stages)
        ov = [_from_residues(r, stages[i], DILATIONS[i]) for i, r in enumerate((o0, o1, o2))]
        lv = [_from_residues(r, stages[3 + i], DILATIONS[i]) for i, r in enumerate((l0, l1, l2))]
        wts = _group_weights(*lv)
        a = jnp.concatenate([pool_out] + [ov[i] * wts[i] for i in range(3)], axis=1).astype(BF16)
        a_ref[...] = a
        out_ref[...] = h_ref[...] + _dot(a, w_ref[...])

    row = lambda w: pl.BlockSpec((t, w), lambda i: (i, 0))
    res = [_residue_spec(dil, t) for dil in DILATIONS]
    return pl.pallas_call(
        body, name=name, grid=(s // t,),
        in_specs=[row(D_MODEL), row(POOL_WIDTH), _resident((POOL_WIDTH, POOL_WIDTH)), _resident((1, POOL_WIDTH))]
        + res + res + [_resident((D_MODEL, D_MODEL))],
        out_specs=[row(D_MODEL), row(D_MODEL), row(POOL_WIDTH)],
        out_shape=[jax.ShapeDtypeStruct((s, D_MODEL), F32), jax.ShapeDtypeStruct((s, D_MODEL), BF16),
                   jax.ShapeDtypeStruct((s, POOL_WIDTH), BF16)],
        scratch_shapes=[pltpu.VMEM((t + POOL_HALO + POOL_PAD, POOL_WIDTH), F32)] * 4 + _stages(t, 6),
        compiler_params=_params(1),
    )(h, u, w_bd, scale, *o, *lse, w_out)


def _mlp_fwd(h, g, w_up, w_down, name):
    s = h.shape[0]
    t = _row_tile(s, 512)
    nblk = D_FF // FF_BLOCK

    def body(h_ref, g_ref, wu_ref, wd_ref, out_ref, hn_ref, r_ref):
        x = h_ref[...]
        _, _, hn = _rms(x, g_ref[...])
        hb = hn.astype(BF16)
        hn_ref[...] = hb
        acc = None
        for b0 in range(0, nblk, FF_PER_STEP):
            acts = []
            for b in range(b0, b0 + FF_PER_STEP):
                r = jnp.maximum(_dot(hb, wu_ref[b]), 0.0)
                r_ref[:, b * FF_BLOCK:(b + 1) * FF_BLOCK] = r.astype(BF16)
                acts.append((r * r).astype(BF16))
            wd = wd_ref[b0:b0 + FF_PER_STEP].reshape(FF_PER_STEP * FF_BLOCK, D_MODEL)
            part = _dot(jnp.concatenate(acts, axis=1), wd)
            acc = part if acc is None else acc + part
        out_ref[...] = x + acc

    row = lambda w: pl.BlockSpec((t, w), lambda i: (i, 0))
    resident = lambda shape: pl.BlockSpec(shape, lambda i: (0, 0, 0), pipeline_mode=pl.Buffered(1))
    return pl.pallas_call(
        body, name=name, grid=(s // t,),
        in_specs=[row(D_MODEL), pl.BlockSpec((1, D_MODEL), lambda i: (0, 0)),
                  resident((nblk, D_MODEL, FF_BLOCK)), resident((nblk, FF_BLOCK, D_MODEL))],
        out_specs=[row(D_MODEL), row(D_MODEL), row(D_FF)],
        out_shape=[jax.ShapeDtypeStruct((s, D_MODEL), F32), jax.ShapeDtypeStruct((s, D_MODEL), BF16),
                   jax.ShapeDtypeStruct((s, D_FF), BF16)],
        compiler_params=_params(1),
    )(h, g, w_up, w_down)


def _gate_fwd(h, g, w_gate, p, layer, w_ple, name, head=None, follow=None):
    s = h.shape[0]
    t = _row_tile(s, FWD_TILE if follow is None else 512)

    def body(h_ref, g_ref, wg_ref, p_ref, wp_ref, *refs):
        x = h_ref[...]
        _, _, hn = _rms(x, g_ref[...])
        hb = hn.astype(BF16)
        gate = 1.0 / (1.0 + jnp.exp(-_dot(hb, wg_ref[...])))
        pb = p_ref[...].astype(BF16)
        h3 = x + gate * _dot(pb, wp_ref[...])
        if follow is not None:
            out_ref, hn_ref, gate_ref, pb_ref = refs[5:9]
            out_ref[...] = h3
            _normproj_tile(h3, *refs[:5], *refs[9:])
        elif head is None:
            out_ref, hn_ref, gate_ref, pb_ref = refs
            out_ref[...] = h3
        else:
            gf_ref, t_ref, hn_ref, gate_ref, pb_ref, loss_ref, dh_ref, dgf_ref = refs

            @pl.when(pl.program_id(0) == 0)
            def _():
                loss_ref[...] = jnp.zeros_like(loss_ref)
                dgf_ref[...] = jnp.zeros_like(dgf_ref)

            gf = gf_ref[...]
            n, rstd, y = _rms(h3, gf)
            err = y - t_ref[...]
            loss_ref[...] += jnp.sum(err * err) * (0.5 / D_MODEL)
            dh_ref[...], dgf = _rms_bwd(err * (1.0 / D_MODEL), n, rstd, gf)
            dgf_ref[...] += dgf
        hn_ref[...] = hb
        pb_ref[...] = pb
        gate_ref[...] = gate.astype(BF16)

    row = lambda w: pl.BlockSpec((t, w), lambda i: (i, 0))
    full = lambda a, b: pl.BlockSpec((a, b), lambda i: (0, 0))
    in_specs = [row(D_MODEL), full(1, D_MODEL), _resident((D_MODEL, D_MODEL)),
                pl.BlockSpec((None, t, PLE_DIM), lambda i: (layer, i, 0)), _resident((PLE_DIM, D_MODEL))]
    saved_specs = [row(D_MODEL), row(D_MODEL), row(PLE_DIM)]
    saved_shapes = [jax.ShapeDtypeStruct((s, D_MODEL), BF16), jax.ShapeDtypeStruct((s, D_MODEL), BF16),
                    jax.ShapeDtypeStruct((s, PLE_DIM), BF16)]
    if follow is not None:
        next_in, next_out, next_shape, scratch = _normproj_operands(s, t)
        return pl.pallas_call(
            body, name=name, grid=(s // t,), in_specs=in_specs + next_in,
            out_specs=[row(D_MODEL)] + saved_specs + next_out,
            out_shape=[jax.ShapeDtypeStruct((s, D_MODEL), F32)] + saved_shapes + next_shape, scratch_shapes=scratch,
            compiler_params=_params(1),
        )(h, g, w_gate, p, w_ple, *follow)
    if head is None:
        return pl.pallas_call(
            body, name=name, grid=(s // t,), in_specs=in_specs, out_specs=[row(D_MODEL)] + saved_specs,
            out_shape=[jax.ShapeDtypeStruct((s, D_MODEL), F32)] + saved_shapes, compiler_params=_params(1),
        )(h, g, w_gate, p, w_ple)
    return pl.pallas_call(
        body, name=name, grid=(s // t,), in_specs=in_specs + [full(1, D_MODEL), row(D_MODEL)],
        out_specs=saved_specs + [pl.BlockSpec((1, LANES), lambda i: (0, 0)), row(D_MODEL), full(1, D_MODEL)],
        out_shape=saved_shapes + [jax.ShapeDtypeStruct((1, LANES), F32), jax.ShapeDtypeStruct((s, D_MODEL), F32),
                                  jax.ShapeDtypeStruct((1, D_MODEL), F32)],
        compiler_params=_params(1),
    )(h, g, w_gate, p, w_ple, *head)


def _gate_bwd(dh, gate, pb, w_ple, h, g, w_gate, hn, name, after=()):
    s = h.shape[0]
    t = _row_tile(s, FWD_TILE)
    last = s // t - 1

    def body(dh_ref, gate_ref, pb_ref, wp_ref, h_ref, g_ref, wg_ref, hn_ref, out_ref, dg_ref, dwg_ref, dwgb_ref,
             dwp_ref, dwpb_ref):
        i = pl.program_id(0)

        @pl.when(i == 0)
        def _():
            dg_ref[...] = jnp.zeros_like(dg_ref)
            dwg_ref[...] = jnp.zeros_like(dwg_ref)
            dwp_ref[...] = jnp.zeros_like(dwp_ref)

        d = dh_ref[...]
        gate = gate_ref[...].astype(F32)
        pb = pb_ref[...]
        e = _dot(pb, wp_ref[...])
        dgl = (d * e * gate * (1.0 - gate)).astype(BF16)
        dwg_ref[...] += _dot_tn(hn_ref[...], dgl)
        dwp_ref[...] += _dot_tn(pb, (d * gate).astype(BF16))
        gv = g_ref[...]
        n, rstd, _ = _rms(h_ref[...], gv)
        dx, dg = _rms_bwd(_dot_nt(dgl, wg_ref[...]), n, rstd, gv)
        out_ref[...] = d + dx
        dg_ref[...] += dg

        @pl.when(i == last)
        def _():
            dwgb_ref[...] = dwg_ref[...].astype(BF16)
            dwpb_ref[...] = dwp_ref[...].astype(BF16)

    row = lambda w: pl.BlockSpec((t, w), lambda i: (i, 0))
    full = lambda a, b: pl.BlockSpec((a, b), lambda i: (0, 0))
    dh2, dg, dwg, dwgb, dwp, dwpb = pl.pallas_call(
        _ordered_after(body, 8, after), name=name, grid=(s // t,),
        in_specs=[row(D_MODEL), row(D_MODEL), row(PLE_DIM), _resident((PLE_DIM, D_MODEL)), row(D_MODEL),
                  full(1, D_MODEL), _resident((D_MODEL, D_MODEL)), row(D_MODEL)]
        + [pl.BlockSpec(memory_space=pl.ANY)] * len(after),
        out_specs=[row(D_MODEL), full(1, D_MODEL), full(D_MODEL, D_MODEL), full(D_MODEL, D_MODEL),
                   full(PLE_DIM, D_MODEL), full(PLE_DIM, D_MODEL)],
        out_shape=[jax.ShapeDtypeStruct((s, D_MODEL), F32), jax.ShapeDtypeStruct((1, D_MODEL), F32),
                   jax.ShapeDtypeStruct((D_MODEL, D_MODEL), F32), jax.ShapeDtypeStruct((D_MODEL, D_MODEL), BF16),
                   jax.ShapeDtypeStruct((PLE_DIM, D_MODEL), F32), jax.ShapeDtypeStruct((PLE_DIM, D_MODEL), BF16)],
        compiler_params=_params(1),
    )(dh, gate, pb, w_ple, h, g, w_gate, hn, *after)
    return dh2, dg, (dwg, dwgb), (dwp, dwpb)


def _mlp_bwd(dh, r, h, g, w_up, w_down, name):
    s = h.shape[0]
    t = _row_tile(s, MLP_BWD_TILE)
    nblk = D_FF // FF_BLOCK

    def body(dh_ref, r_ref, h_ref, g_ref, wu_ref, wd_ref, out_ref, dup_ref, dg_ref, dhb_ref):
        @pl.when(pl.program_id(0) == 0)
        def _():
            dg_ref[...] = jnp.zeros_like(dg_ref)

        d = dh_ref[...]
        db = d.astype(BF16)
        dhb_ref[...] = db
        back = None
        for b in range(nblk):
            cols = slice(b * FF_BLOCK, (b + 1) * FF_BLOCK)
            dup = (_dot_nt(db, wd_ref[b]) * (2.0 * r_ref[:, cols].astype(F32))).astype(BF16)
            dup_ref[:, cols] = dup
            part = _dot_nt(dup, wu_ref[b])
            back = part if back is None else back + part
        gv = g_ref[...]
        n, rstd, _ = _rms(h_ref[...], gv)
        dx, dg = _rms_bwd(back, n, rstd, gv)
        out_ref[...] = d + dx
        dg_ref[...] += dg

    row = lambda w: pl.BlockSpec((t, w), lambda i: (i, 0))
    vec = pl.BlockSpec((1, D_MODEL), lambda i: (0, 0))
    resident = lambda shape: pl.BlockSpec(shape, lambda i: (0, 0, 0), pipeline_mode=pl.Buffered(1))
    return pl.pallas_call(
        body, name=name, grid=(s // t,),
        in_specs=[row(D_MODEL), row(D_FF), row(D_MODEL), vec,
                  resident((nblk, D_MODEL, FF_BLOCK)), resident((nblk, FF_BLOCK, D_MODEL))],
        out_specs=[row(D_MODEL), row(D_FF), vec, row(D_MODEL)],
        out_shape=[jax.ShapeDtypeStruct((s, D_MODEL), F32), jax.ShapeDtypeStruct((s, D_FF), BF16),
                   jax.ShapeDtypeStruct((1, D_MODEL), F32), jax.ShapeDtypeStruct((s, D_MODEL), BF16)],
        compiler_params=_params(1),
    )(dh, r, h, g, w_up, w_down)


def _outproj_bwd(dh, w_out, o, lse, ones_bd, a, name):
    s = dh.shape[0]
    t = _row_tile(s, 512)
    last = s // t - 1

    def body(dh_ref, w_ref, o0, o1, o2, l0, l1, l2, bd_ref, a_ref, dp_ref, do0, do1, do2, de0, de1, de2, dw_ref,
             dwb_ref, *stages):
        i = pl.program_id(0)

        @pl.when(i == 0)
        def _():
            dw_ref[...] = jnp.zeros_like(dw_ref)

        stages = _pair_stages(stages)
        dhb = dh_ref[...].astype(BF16)
        dw_ref[...] += _dot_tn(a_ref[...], dhb)

        @pl.when(i == last)
        def _():
            dwb_ref[...] = dw_ref[...].astype(BF16)

        da = _dot_nt(dhb, w_ref[...])
        dp_ref[...] = da[:, 0:POOL_WIDTH]
        ov =[_from_residues(r, stages[i], DILATIONS[i]) for i, r in enumerate((o0, o1, o2))]
        lv = [_from_residues(r, stages[3 + i], DILATIONS[i]) for i, r in enumerate((l0, l1, l2))]
        wts = _group_weights(*lv)
        bd = bd_ref[...]
        cbar = jnp.zeros((t, GROUP_WIDTH), F32)
        for grp, do_ref in enumerate((do0, do1, do2)):
            lo = POOL_WIDTH + grp * GROUP_WIDTH
            dag = da[:, lo:lo + GROUP_WIDTH]
            _to_residues(dag * wts[grp], stages[6 + grp], do_ref, DILATIONS[grp])
            prod = dag * ov[grp]
            hi = prod.astype(BF16)
            low = (prod - hi.astype(F32)).astype(BF16)
            cbar = cbar + wts[grp] * (_dot(hi, bd) + _dot(low, bd))
        for grp, de_ref in enumerate((de0, de1, de2)):
            _to_residues(wts[grp] * cbar, stages[9 + grp], de_ref, DILATIONS[grp])

    row = lambda w: pl.BlockSpec((t, w), lambda i: (i, 0))
    full = lambda a, b: pl.BlockSpec((a, b), lambda i: (0, 0))
    res = [_residue_spec(dil, t) for dil in DILATIONS]
    *outs, dw, dwb = pl.pallas_call(
        body, name=name, grid=(s // t,),
        in_specs=[row(D_MODEL), full(D_MODEL, D_MODEL)] + res + res + [full(GROUP_WIDTH, GROUP_WIDTH), row(D_MODEL)],
        out_specs=[row(POOL_WIDTH)] + res + res + [full(D_MODEL, D_MODEL)] * 2,
        out_shape=[jax.ShapeDtypeStruct((s, POOL_WIDTH), F32)] + [_residue_shape(dil, s, BF16) for dil in DILATIONS]
        + [_residue_shape(dil, s, F32) for dil in DILATIONS]
        + [jax.ShapeDtypeStruct((D_MODEL, D_MODEL), F32), jax.ShapeDtypeStruct((D_MODEL, D_MODEL), BF16)],
        scratch_shapes=_stages(t, 12),
        compiler_params=_params(1),
    )(dh, w_out, *o, *lse, ones_bd, a)
    return (*outs, (dw, dwb))


def _attn_bwd(q, k, v, do, lse, deff, name, after=()):
    dil, length, _ = q.shape
    nb = length // ATTN_BLOCK
    qb = _blocks_per_step(nb)
    nj = nb // qb
    rs = _residues_per_step(dil, nb, qb)
    whole = nj == 1
    tail = slice((qb - 1) * ATTN_BLOCK, qb * ATTN_BLOCK)
    block = lambda qi: slice(qi * ATTN_BLOCK, (qi + 1) * ATTN_BLOCK)

    def body(q_ref, kp_ref, kc_ref, vp_ref, vc_ref, do_ref, lse_ref, de_ref, dq_ref, dk_ref, dv_ref, ck, cv):
        j = pl.program_id(1)

        def compute():
            masks = _head_masks()
            bias = _band_bias(j == 0)
            for rr in range(rs):
                dkc, dvc = [], []
                for qi in range(qb):
                    here, before = block(qi), block(qi - 1)
                    kcat = jnp.concatenate([kp_ref[rr] if qi == 0 else kc_ref[rr, before], kc_ref[rr, here]], axis=0)
                    vcat = jnp.concatenate([vp_ref[rr] if qi == 0 else vc_ref[rr, before], vc_ref[rr, here]], axis=0)
                    qs = _stack_heads(q_ref[rr, here], masks)
                    dos = _stack_heads(do_ref[rr, here], masks)
                    sc = _dot_nt(qs, kcat) + bias[min(qi, 1)]
                    p = jnp.exp(sc - _column_per_head(lse_ref[rr, here]))
                    ds = (p * (_dot_nt(dos, vcat) - _column_per_head(de_ref[rr, here]))).astype(BF16)
                    dq = jnp.zeros((ATTN_BLOCK, GROUP_WIDTH), F32)
                    for hd, msk in enumerate(masks):
                        dq = jnp.where(msk, _dot(ds[block(hd)], kcat), dq)
                    dq_ref[rr, here] = dq.astype(dq_ref.dtype)
                    dkc.append(_dot_tn(ds, qs))
                    dvc.append(_dot_tn(p.astype(BF16), dos))

                for out_ref, carry, parts in ((dk_ref, ck, dkc), (dv_ref, cv, dvc)):
                    full = [parts[qi][ATTN_BLOCK:] + parts[qi + 1][0:ATTN_BLOCK] for qi in range(qb - 1)]
                    if whole:
                        for qi, val in enumerate(full + [parts[qb - 1][ATTN_BLOCK:]]):
                            out_ref[rr, block(qi)] = val.astype(out_ref.dtype)
                        continue

                    @pl.when(j > 0)
                    def _():
                        if qb > 1:
                            out_ref[0, 0:(qb - 1) * ATTN_BLOCK] = carry[0:(qb - 1) * ATTN_BLOCK].astype(out_ref.dtype)
                        out_ref[0, tail] = (carry[tail] + parts[0][0:ATTN_BLOCK]).astype(out_ref.dtype)

                    for qi, val in enumerate(full):
                        carry[block(qi)] = val
                    carry[tail] = parts[qb - 1][ATTN_BLOCK:]

        if whole:
            compute()
        else:
            pl.when(j < nj)(compute)

            @pl.when(j == nj)
            def _():
                dk_ref[0] = ck[...].astype(dk_ref.dtype)
                dv_ref[0] = cv[...].astype(dv_ref.dtype)

    step = lambda j: jnp.minimum(j, nj - 1)
    cur = pl.BlockSpec((rs, qb * ATTN_BLOCK, GROUP_WIDTH), lambda r, j: (r, step(j), 0))
    prev = pl.BlockSpec((rs, ATTN_BLOCK, GROUP_WIDTH), lambda r, j: (r, jnp.maximum(qb * step(j) - 1, 0), 0))
    late = pl.BlockSpec((rs, qb * ATTN_BLOCK, GROUP_WIDTH), lambda r, j: (r, jnp.maximum(j - 1, 0), 0))
    return pl.pallas_call(
        _ordered_after(body, 8, after), name=name, grid=(dil // rs, 1 if whole else nj + 1),
        in_specs=[cur, prev, cur, prev, cur, cur, cur, cur] + [pl.BlockSpec(memory_space=pl.ANY)] * len(after),
        out_specs=[cur, cur if whole else late, cur if whole else late],
        out_shape=[jax.ShapeDtypeStruct(q.shape, BF16)] * 3,
        scratch_shapes=[pltpu.VMEM((qb * ATTN_BLOCK, GROUP_WIDTH), F32)] * 2,
        compiler_params=_params(2),
    )(q, k, k, v, v, do, lse, deff, *after)


def _pool_bwd(dpool, y, w_bd, scale, name, after=()):
    s = dpool.shape[0]
    t = _row_tile(s, 512)
    nt = s // t

    def body(dp_ref, y_ref, w_ref, sc_ref, du_ref, dw_ref, dsc_ref, ext, b2, b4, b8):
        i = pl.program_id(0)

        @pl.when(i == 0)
        def _():
            ext[t:, :] = jnp.zeros((POOL_HALO + POOL_PAD, POOL_WIDTH), F32)
            for buf in (b2, b4):
                buf[t + POOL_HALO:, :] = jnp.zeros((POOL_PAD, POOL_WIDTH), F32)
            dw_ref[...] = jnp.zeros_like(dw_ref)
            dsc_ref[...] = jnp.zeros_like(dsc_ref)

        dp = dp_ref[...]
        yb = y_ref[...]
        w = w_ref[...]
        dsc_ref[...] += jnp.sum(dp * _dot(yb, w), axis=0, keepdims=True)
        dyo = (dp * sc_ref[...]).astype(BF16)
        dw_ref[...] += _dot_tn(yb, dyo)
        dy = _dot_nt(dyo, w)
        win = _pool_lane_window()
        pos = (nt - 1 - i) * t + lax.broadcasted_iota(jnp.int32, (t, POOL_WIDTH), 0)
        gq = dy / jnp.minimum(pos + 1, win).astype(F32)
        ext[0:t, :] = gq
        du_ref[...] = _window_sums(ext, b2, b4, b8, t, 0, 0, 1) - dy
        ext[t:t + POOL_HALO, :] = gq[0:POOL_HALO, :]

    rev = pl.BlockSpec((t, POOL_WIDTH), lambda i: (nt - 1 - i, 0))
    full = lambda a, b: pl.BlockSpec((a, b), lambda i: (0, 0))
    return pl.pallas_call(
        _ordered_after(body, 4, after), name=name, grid=(nt,),
        in_specs=[rev, rev, full(POOL_WIDTH, POOL_WIDTH), full(1, POOL_WIDTH)]
        + [pl.BlockSpec(memory_space=pl.ANY)] * len(after),
        out_specs=[rev, full(POOL_WIDTH, POOL_WIDTH), full(1, POOL_WIDTH)],
        out_shape=[jax.ShapeDtypeStruct((s, POOL_WIDTH), F32), jax.ShapeDtypeStruct((POOL_WIDTH, POOL_WIDTH), F32),
                   jax.ShapeDtypeStruct((1, POOL_WIDTH), F32)],
        scratch_shapes=[pltpu.VMEM((t + POOL_HALO + POOL_PAD, POOL_WIDTH), F32)] * 4,
        compiler_params=_params(1),
    )(dpool, y, w_bd, scale, *after)


def _normproj_bwd(dh, du, dq, dk, dv, rc, rsa, rsb, w_in, h, g, name):
    s = h.shape[0]
    t = _row_tile(s, 512)

    def body(dh_ref, du_ref, q0, q1, q2, k0, k1, k2, v0, v1, v2, c_ref, sa_ref, sb_ref, w_ref, h_ref, g_ref,
             out_ref, dz_ref, dg_ref, *stages):
        @pl.when(pl.program_id(0) == 0)
        def _():
            dg_ref[...] = jnp.zeros_like(dg_ref)

        c, sa, sb = c_ref[...], sa_ref[...], sb_ref[...]

        def unrot(a, scale):
            halves = [_rot_t(a[:, hf * LANES:(hf + 1) * LANES] * scale, c, sa, sb) for hf in range(2)]
            return jnp.concatenate(halves, axis=1)

        staged = _pair_stages(stages)
        tok = lambda refs, base: [_from_residues(r, staged[base + i], DILATIONS[i]) for i, r in enumerate(refs)]
        chunks = [du_ref[...]]
        chunks += [unrot(a, HEAD_DIM ** -0.5) for a in tok((q0, q1, q2), 0)]
        chunks += [unrot(a, 1.0) for a in tok((k0, k1, k2), 3)]
        chunks += tok((v0, v1, v2), 6)
        acc = jnp.zeros((t, D_MODEL), F32)
        for ci, ch in enumerate(chunks):
            cols = slice(ci * GROUP_WIDTH, (ci + 1) * GROUP_WIDTH)
            cb = ch.astype(BF16)
            dz_ref[:, cols] = cb
            acc = acc + _dot(cb, w_ref[cols, :])
        gv = g_ref[...]
        n, rstd, _ = _rms(h_ref[...], gv)
        dx, dg = _rms_bwd(acc, n, rstd, gv)
        out_ref[...] = dh_ref[...] + dx
        dg_ref[...] += dg

    row = lambda w: pl.BlockSpec((t, w), lambda i: (i, 0))
    vec = pl.BlockSpec((1, D_MODEL), lambda i: (0, 0))
    res = [_residue_spec(dil, t) for dil in DILATIONS]
    return pl.pallas_call(
        body, name=name, grid=(s // t,),
        in_specs=[row(D_MODEL), row(POOL_WIDTH)] + res * 3 + _table_specs(t)
        + [pl.BlockSpec((N_IN, D_MODEL), lambda i: (0, 0)), row(D_MODEL), vec],
        out_specs=[row(D_MODEL), row(N_IN), vec],
        out_shape=[jax.ShapeDtypeStruct((s, D_MODEL), F32), jax.ShapeDtypeStruct((s, N_IN), BF16),
                   jax.ShapeDtypeStruct((1, D_MODEL), F32)],
        scratch_shapes=_stages(t, 9),
        compiler_params=_params(1),
    )(dh, du, *dq, *dk, *dv, rc, rsa, rsb, w_in, h, g)


def _matmul_tn(a, b, name, *, square_a=False, tm=None, tn=None, blocked_out=False, after=()):
    s, m = a.shape
    n = b.shape[1]
    tk = _row_tile(s, 2048)
    tm = tm or min(m, 1024)
    tn = tn or min(n, 1024)
    assert m % tm == 0 and n % tn == 0
    nk = s // tk
    nsub = tn // FF_BLOCK if blocked_out else 1

    def body(a_ref, b_ref, o_ref, ob_ref, acc):
        k = pl.program_id(2)

        def product():
            av = a_ref[...]
            if square_a:
                av = av.astype(F32)
                av = av * av
            return _dot_tn(av.astype(BF16), b_ref[...].astype(BF16))

        def emit(total):
            if blocked_out:
                for sub in range(nsub):
                    cols = slice(sub * FF_BLOCK, (sub + 1) * FF_BLOCK)
                    o_ref[sub] = total[:, cols]
                    ob_ref[sub] = total[:, cols].astype(BF16)
            else:
                o_ref[...] = total
                ob_ref[...] = total.astype(BF16)

        if nk == 1:
            emit(product())
            return

        @pl.when(k == 0)
        def _():
            acc[...] = product()

        @pl.when((k > 0) & (k < nk - 1))
        def _():
            acc[...] += product()

        @pl.when(k == nk - 1)
        def _():
            emit(acc[...] + product())

    if blocked_out:
        shape = (n // FF_BLOCK, m, FF_BLOCK)
        out_spec = pl.BlockSpec((nsub, tm, FF_BLOCK), lambda i, j, k: (j, i, 0))
    else:
        shape = (m, n)
        out_spec = pl.BlockSpec((tm, tn), lambda i, j, k: (i, j))
    return pl.pallas_call(
        _ordered_after(body, 2, after), name=name, grid=(m // tm, n // tn, nk),
        in_specs=[pl.BlockSpec((tk, tm), lambda i, j, k: (k, i)), pl.BlockSpec((tk, tn), lambda i, j, k: (k, j))]
        + [pl.BlockSpec(memory_space=pl.ANY)] * len(after),
        out_specs=[out_spec, out_spec],
        out_shape=[jax.ShapeDtypeStruct(shape, F32), jax.ShapeDtypeStruct(shape, BF16)],
        scratch_shapes=[pltpu.VMEM((tm, tn), F32)],
        compiler_params=_params(3),
    )(a, b, *after)


def _adamw_math(w, g, m, v):
    m = ADAM_B1 * m + (1.0 - ADAM_B1) * g
    v = ADAM_B2 * v + (1.0 - ADAM_B2) * (g * g)
    m_hat = m / (1.0 - ADAM_B1 ** ADAM_STEP)
    v_hat = v / (1.0 - ADAM_B2 ** ADAM_STEP)
    delta = -ADAM_LR * (m_hat / (jnp.sqrt(v_hat) + ADAM_EPS) + ADAM_WD * w)
    return delta, m, v


def _sum_chunks_body(own0_ref, own1_ref, r0_ref, r1_ref):
    layer0 = pl.program_id(0) == 0
    g = jnp.where(layer0, own0_ref[...], own1_ref[...])
    for k in range(N_DEV - 1):
        g = g + jnp.where(layer0, r0_ref[k], r1_ref[k]).astype(F32)
    return g


def _chunk_specs(t, cols):
    rows_of = lambda layer: (lambda l, i: jnp.where(l == layer, i, 0))
    blk = pl.BlockSpec((None, t, cols), lambda l, i, me: (l, i, 0))
    own = [pl.BlockSpec((None, t, cols), functools.partial(lambda l, i, me, pick: (me[0], pick(l, i), 0), pick=rows_of(ly)))
           for ly in range(2)]
    recv = [pl.BlockSpec((N_DEV - 1, t, cols), functools.partial(lambda l, i, me, pick: (0, pick(l, i), 0), pick=rows_of(ly)))
            for ly in range(2)]
    return blk, own + recv


def _sum_chunks(chunks, me, name):
    _, rows, cols = chunks[0].shape
    t = _row_tile(rows, 320)

    def body(me_ref, own0_ref, own1_ref, r0_ref, r1_ref, g_ref):
        g_ref[...] = _sum_chunks_body(own0_ref, own1_ref, r0_ref, r1_ref)

    blk, chunk_specs = _chunk_specs(t, cols)
    return pl.pallas_call(
        body, name=name,
        grid_spec=pltpu.PrefetchScalarGridSpec(num_scalar_prefetch=1, grid=(2, rows // t), in_specs=chunk_specs,
                                               out_specs=blk),
        out_shape=jax.ShapeDtypeStruct((2, rows, cols), F32), compiler_params=_params(2),
    )(me, *chunks)


def _adamw_sharded(w, m, v, grad, me, name):
    _, rows, cols = w.shape
    t = _row_tile(rows, 256)
    summed = not isinstance(grad, tuple)
    grad = (grad,) if summed else grad

    def body(me_ref, w_ref, m_ref, v_ref, *refs):
        g_ref, d_ref, nm_ref, nv_ref = refs[-4:]
        g = refs[0][...] if summed else _sum_chunks_body(*refs[:4])
        g_ref[...] = g
        d_ref[...], nm_ref[...], nv_ref[...] = _adamw_math(w_ref[...], g, m_ref[...], v_ref[...])

    blk, chunk_specs = _chunk_specs(t, cols)
    return pl.pallas_call(
        body, name=name,
        grid_spec=pltpu.PrefetchScalarGridSpec(
            num_scalar_prefetch=1, grid=(2, rows // t),
            in_specs=[blk, blk, blk] + ([blk] if summed else chunk_specs), out_specs=[blk] * 4),
        out_shape=[jax.ShapeDtypeStruct(w.shape, F32)] * 4,
        compiler_params=_params(2),
    )(me, w, m, v, *grad)


def _adamw_packed(w, g8, m, v, name):
    def body(w_ref, g_ref, m_ref, v_ref, go_ref, d_ref, nm_ref, nv_ref):
        g = g_ref[0]
        for dev in range(1, N_DEV):
            g = g + g_ref[dev]
        go_ref[...] = g
        d_ref[...], nm_ref[...], nv_ref[...] = _adamw_math(w_ref[...], g, m_ref[...], v_ref[...])

    return pl.pallas_call(
        body, name=name, out_shape=[jax.ShapeDtypeStruct(w.shape, F32)] * 4,
        compiler_params=pltpu.CompilerParams(vmem_limit_bytes=VMEM_LIMIT),
    )(w, g8, m, v)


def _peer(k):
    x, y, c = lax.axis_index("x"), lax.axis_index("y"), lax.axis_index("c")
    return (1 - x if k & 4 else x, 1 - y if k & 2 else y, 1 - c if k & 1 else c)


def _linear(dev):
    return 4 * dev[0] + 2 * dev[1] + dev[2]


HBM_SPEC = pl.BlockSpec(memory_space=pltpu.HBM)
SEM_SPEC = pl.BlockSpec(memory_space=pltpu.SEMAPHORE)
ANY_SPEC = pl.BlockSpec(memory_space=pl.ANY)
EFFECT = pltpu.SideEffectType.DATAFLOW_SIDE_EFFECTING


def _in_hbm(a):
    return pltpu.with_memory_space_constraint(a, pltpu.HBM)


class _Exchange:
    def __init__(self, name, groups, scatter, after=()):
        self.name, self.scatter = name, scatter
        self.sizes = sizes = [len(g) for g in groups]
        srcs = [a for g in groups for a in g]
        n, ng = len(srcs), len(groups)
        lead = (N_DEV - 1,) if scatter else (N_DEV,)
        shapes = [lead + (a.shape[1:] if scatter else a.shape) for a in srcs]
        lands = [lax.empty(sh, a.dtype) for sh, a in zip(shapes, srcs)]
        offsets = [sum(sizes[:gi]) for gi in range(ng)]
        copy = self._copy

        def body(*refs):
            src, land = refs[:n], refs[n:2 * n]
            sems = refs[2 * n + len(after):2 * n + len(after) + 2 * ng]
            token = refs[-1]
            for gi in range(ng):
                for wi in range(sizes[gi]):
                    w = offsets[gi] + wi
                    for k in range(1, N_DEV):
                        copy(src[w], land[w], sems[2 * gi], sems[2 * gi + 1], wi, k).start()
            token[...] = jnp.zeros_like(token)

        sem_shapes = [pltpu.SemaphoreType.DMA(((N_DEV - 1) * sz,)) for sz in sizes for _ in range(2)]
        outs = pl.pallas_call(
            body, name=name + "_start",
            in_specs=[HBM_SPEC] * (2 * n) + [ANY_SPEC] * len(after),
            out_specs=[SEM_SPEC] * (2 * ng) + [HBM_SPEC] * (2 * n) + [pl.BlockSpec(memory_space=pltpu.VMEM)],
            out_shape=sem_shapes + [pltpu.HBM(a.shape, a.dtype) for a in srcs + lands]
            + [jax.ShapeDtypeStruct((8, LANES), F32)],
            input_output_aliases={i: 2 * ng + i for i in range(2 * n)},
            compiler_params=pltpu.CompilerParams(has_side_effects=EFFECT),
        )(*[_in_hbm(a) for a in srcs + lands], *after)
        self.sems = [outs[2 * gi:2 * gi + 2] for gi in range(ng)]
        thru = outs[2 * ng:2 * ng + 2 * n]
        self.srcs = [thru[offsets[gi]:offsets[gi] + sizes[gi]] for gi in range(ng)]
        self.lands = [thru[n + offsets[gi]:n + offsets[gi] + sizes[gi]] for gi in range(ng)]
        self.token = outs[-1]

    def _copy(self, src, land, send_sems, recv_sems, wi, k):
        to = _peer(k)
        if self.scatter:
            src_ref, dst_ref = src.at[_linear(to)], land.at[k - 1]
        else:
            src_ref, dst_ref = src, land.at[_linear(_peer(0))]
        return pltpu.make_async_remote_copy(
            src_ref=src_ref, dst_ref=dst_ref, send_sem=send_sems.at[(N_DEV - 1) * wi + k - 1],
            recv_sem=recv_sems.at[(N_DEV - 1) * wi + k - 1], device_id=to, device_id_type=MESH)

    def wait(self, gi, after):
        n = self.sizes[gi]
        copy = self._copy

        def body(*refs):
            src, land = refs[:n], refs[n:2 * n]
            send_sems, recv_sems = refs[2 * n], refs[2 * n + 1]
            for wi in range(n):
                for k in range(1, N_DEV):
                    cp = copy(src[wi], land[wi], send_sems, recv_sems, wi, k)
                    cp.wait_send()
                    cp.wait_recv()

        arrays = list(self.srcs[gi]) + list(self.lands[gi])
        outs = pl.pallas_call(
            body, name=f"{self.name}_wait{gi}",
            in_specs=[HBM_SPEC] * (2 * n) + [SEM_SPEC, SEM_SPEC] + [ANY_SPEC] * len(after),
            out_specs=[HBM_SPEC] * (2 * n),
            out_shape=[pltpu.HBM(a.shape, a.dtype) for a in arrays],
            input_output_aliases={i: i for i in range(2 * n)},
            compiler_params=pltpu.CompilerParams(has_side_effects=EFFECT),
        )(*arrays, *self.sems[gi], *after)
        return outs[:n], outs[n:]


def _rotary_tables(positions):
    rot_dim = HEAD_DIM // 4
    inv_freq = ROPE_THETA ** (-jnp.arange(0, rot_dim, 2, dtype=F32) / rot_dim)
    ang = positions.astype(F32)[:, None] * inv_freq
    cs = jnp.concatenate([jnp.cos(ang), jnp.sin(ang)], axis=1)
    dim = jnp.arange(LANES) % HEAD_DIM
    first, second = dim < ROT_SHIFT, (dim >= ROT_SHIFT) & (dim < rot_dim)
    src = jnp.arange(2 * ROT_SHIFT)[:, None]
    angle = (dim % ROT_SHIFT)[None, :]
    c = jnp.where((first | second)[None, :] & (src == angle), 1.0, 0.0)
    sa = jnp.where(second[None, :] & (src == angle + ROT_SHIFT), 1.0, 0.0)
    sb = jnp.where(first[None, :] & (src == angle + ROT_SHIFT), -1.0, 0.0)
    spread = jnp.concatenate([c, sa, sb], axis=1).astype(F32)
    base = jnp.concatenate([jnp.where(first | second, 0.0, 1.0), jnp.zeros((2 * LANES,))]).astype(F32)[None, :]
    return jnp.dot(cs, spread, precision=lax.Precision.HIGHEST, preferred_element_type=F32) + base


def _block_diag(pool_w):
    gc = pool_w.shape[-1]
    out = jnp.zeros((POOL_WIDTH, POOL_WIDTH), pool_w.dtype)
    for grp in range(pool_w.shape[0]):
        out = lax.dynamic_update_slice(out, pool_w[grp], (grp * gc, grp * gc))
    return out


def _diag_blocks(a):
    gc = POOL_WIDTH // len(POOL_WINDOWS)
    return jnp.stack([a[grp * gc:(grp + 1) * gc, grp * gc:(grp + 1) * gc] for grp in range(len(POOL_WINDOWS))])


def _local_step(x, p, positions, loss_target, norm1, pool_w, pool_scale, norm2, norm3, final_norm, weights, send):
    rc = rsa = rsb = _rotary_tables(positions)
    ones_bd = _block_diag(jnp.ones((4, HEAD_DIM, HEAD_DIM), BF16))
    saved = []
    h = x
    for i in range(2):
        tag = f"_l{i}"
        g1, g2, g3 = norm1[i:i + 1], norm2[i:i + 1], norm3[i:i + 1]
        w_bd = _block_diag(pool_w[i]).astype(BF16)
        scale = pool_scale[i:i + 1]
        if i == 0:
            w_in = weights(i, "in", (h, rc, w_bd))
            hn1, u, *qkv = _normproj_fwd(h, g1, w_in, rc, rsa, rsb, "normproj_fwd" + tag)
        else:
            w_in, (hn1, u, *qkv) = ahead
        qkv = [qkv[3 * grp:3 * grp + 3] for grp in range(3)]
        started = weights(i, "prefetch", (hn1,))
        o, lse = zip(*[_attn_fwd(*qkv[grp], f"attn_fwd{tag}_g{grp}", after=started) for grp in range(3)])
        w_out = weights(i, "out", o)
        h1, a, y = _outproj_fwd(h, u, w_bd, scale, o, lse, w_out, "outproj_fwd" + tag)
        w_up, w_down = weights(i, "mlp", (h1,))
        h2, hn2, r = _mlp_fwd(h1, g2, w_up, w_down, "mlp_fwd" + tag)
        w_gate, w_ple = weights(i, "gate", (h2,))
        h0 = h
        if i == 0:
            w_in_next = weights(1, "in", (h2,))
            h, hn3, gate, pb, *ahead = _gate_fwd(h2, g3, w_gate, p, i, w_ple, "gate_normproj_fwd",
                                                 follow=(norm1[1:2], w_in_next, rc, rsa, rsb))
            ahead = (w_in_next, ahead)
        else:
            hn3, gate, pb, loss, dh, d_final = _gate_fwd(h2, g3, w_gate, p, i, w_ple, "gate_fwd" + tag,
                                                         head=(final_norm.reshape(1, D_MODEL), loss_target))
        saved.append(dict(h0=h0, hn1=hn1, qkv=qkv, y=y, o=o, lse=lse, a=a, h1=h1, hn2=hn2, r=r, h2=h2,
                          hn3=hn3, gate=gate, pb=pb, w_bd=w_bd, scale=scale, g1=g1, g2=g2, g3=g3,
                          w_in=w_in, w_out=w_out, w_up=w_up, w_down=w_down, w_gate=w_gate, w_ple=w_ple))

    grads = [None, None]
    sent = ()
    for i in (1, 0):
        tag = f"_l{i}"
        sv = saved[i]
        dh2, dg3, dw_gate, dw_ple = _gate_bwd(dh, sv["gate"], sv["pb"], sv["w_ple"], sv["h2"], sv["g3"], sv["w_gate"],
                                              sv["hn3"], "gate_bwd" + tag, after=sent)
        dh1, dup, dg2, dh2b = _mlp_bwd(dh2, sv["r"], sv["h1"], sv["g2"], sv["w_up"], sv["w_down"], "mlp_bwd" + tag)
        dw_down = _matmul_tn(sv["r"], dh2b, "dw_down" + tag, square_a=True)
        dw_up = _matmul_tn(sv["hn2"], dup, "dw_up" + tag, blocked_out=True)
        dpool, do0, do1, do2, de0, de1, de2, dw_out = _outproj_bwd(dh1, sv["w_out"], sv["o"], sv["lse"], ones_bd,
                                                                   sv["a"], "outproj_bwd" + tag)
        sent = send(i, "main", dict(w_gate=dw_gate, w_ple=dw_ple, w_down=dw_down, w_up=dw_up, w_out=dw_out))
        dqkv = [_attn_bwd(*sv["qkv"][grp], do_g, sv["lse"][grp], de_g, f"attn_bwd{tag}_g{grp}", after=sent)
                for grp, (do_g, de_g) in enumerate(((do0, de0), (do1, de1), (do2, de2)))]
        dq, dk, dv = zip(*dqkv)
        du, dw_bd, dscale = _pool_bwd(dpool, sv["y"], sv["w_bd"], sv["scale"], "pool_bwd" + tag, after=sent)
        dh, dz, dg1 = _normproj_bwd(dh1, du, dq, dk, dv, rc, rsa, rsb, sv["w_in"], sv["h0"], sv["g1"],
                                    "normproj_bwd" + tag)
        grads[i] = dict(norm1=dg1, norm2=dg2, norm3=dg3, pool_w=_diag_blocks(dw_bd), pool_scale=dscale)
        small_sent = send(0, "small", (grads, d_final, loss)) if i == 0 else ()
        dw_in = _matmul_tn(dz, sv["hn1"], "dw_in" + tag, tm=N_IN // 2, after=small_sent)
        sent = send(i, "in", dict(w_in=dw_in))
    return dh, sent


def _pack_small(norm1, norm2, norm3, final_norm, pool_scale, pool_w, spare=None):
    spare = jnp.zeros((1, LANES), F32) if spare is None else spare
    scale_row = jnp.concatenate([pool_scale.reshape(1, 2 * POOL_WIDTH), spare,
                                 jnp.zeros((1, D_MODEL - 2 * POOL_WIDTH - LANES), F32)], axis=1)
    return jnp.concatenate([norm1, norm2, norm3, final_norm.reshape(1, D_MODEL), scale_row,
                            pool_w.reshape(32, D_MODEL)], axis=0)


def _unpack_small(a):
    return dict(norm1=a[0:2], norm2=a[2:4], norm3=a[4:6], final_norm=a[6], pool_scale=a[7, 0:2 * POOL_WIDTH].reshape(2, POOL_WIDTH),
                pool_w=a[8:40].reshape(2, 4, HEAD_DIM, HEAD_DIM))


def _chunks_cols(a, cols):
    return a.reshape(a.shape[0], N_DEV, cols).transpose(1, 0, 2)


def _chunks_rows(a, rows):
    return a.reshape(N_DEV, rows, a.shape[1])


BIG = ("w_in", "w_out", "w_up", "w_down", "w_gate", "w_ple")
SMALL = ("norm1", "norm2", "norm3", "final_norm", "pool_scale", "pool_w")
ORDER = ("norm1", "w_in", "pool_w", "pool_scale", "w_out", "norm2", "w_up", "w_down", "norm3", "w_gate", "w_ple",
         "final_norm")


def kernel(x, p, positions, norm1, w_in, pool_w, pool_scale, w_out, norm2, w_up, w_down, norm3, w_gate, w_ple, final_norm, loss_target, m_norm1, m_w_in, m_pool_w, m_pool_scale, m_w_out, m_norm2, m_w_up, m_w_down, m_norm3, m_w_gate, m_w_ple, m_final_norm, v_norm1, v_w_in, v_pool_w, v_pool_scale, v_w_out, v_norm2, v_w_up, v_w_down, v_norm3, v_w_gate, v_w_ple, v_final_norm):
    w = dict(norm1=norm1, w_in=w_in, pool_w=pool_w, pool_scale=pool_scale, w_out=w_out, norm2=norm2, w_up=w_up,
             w_down=w_down, norm3=norm3, w_gate=w_gate, w_ple=w_ple, final_norm=final_norm)
    m = dict(norm1=m_norm1, w_in=m_w_in, pool_w=m_pool_w, pool_scale=m_pool_scale, w_out=m_w_out, norm2=m_norm2,
             w_up=m_w_up, w_down=m_w_down, norm3=m_norm3, w_gate=m_w_gate, w_ple=m_w_ple, final_norm=m_final_norm)
    v = dict(norm1=v_norm1, w_in=v_w_in, pool_w=v_pool_w, pool_scale=v_pool_scale, w_out=v_w_out, norm2=v_norm2,
             w_up=v_w_up, w_down=v_w_down, norm3=v_norm3, w_gate=v_w_gate, w_ple=v_w_ple, final_norm=v_final_norm)
    seq = x.shape[1]

    bf = {n: [w[n][layer].astype(BF16) for layer in range(2)] for n in BIG}
    bf["w_in"] = [a.T for a in bf["w_in"]]
    me = 4 * lax.axis_index("x") + 2 * lax.axis_index("y") + lax.axis_index("c")
    parts = dict(zip(("in", "out", "mlp", "gate"), (("w_in",), ("w_out",), ("w_up", "w_down"), ("w_gate", "w_ple"))))
    first = _Exchange("gather_first", [[bf["w_in"][0]]], scatter=False)
    later = [pt for pt in parts if pt != "in"]
    gathers = [_Exchange("gather_l0", [[bf[n][0] for n in parts[pt]] for pt in later], scatter=False,
                         after=(first.token,))]
    unpack = dict(w_in=lambda a: a.reshape(N_IN, D_MODEL),
                  w_out=lambda a: a.reshape(D_MODEL, D_MODEL), w_gate=lambda a: a.reshape(D_MODEL, D_MODEL),
                  w_ple=lambda a: a.transpose(1, 0, 2).reshape(PLE_DIM, D_MODEL), w_up=lambda a: a, w_down=lambda a: a)

    def weights(layer, part, after):
        if part == "prefetch":
            if layer != 0:
                return ()
            gathers.append(_Exchange("gather_l1", [[bf[n][1] for n in parts[pt]] for pt in parts], scatter=False,
                                     after=after))
            return (gathers[1].token,)
        if layer == 0 and part == "in":
            shards, lands = first.wait(0, (*after, gathers[0].token))
        elif layer == 0:
            shards, lands = gathers[0].wait(later.index(part), after)
        else:
            shards, lands = gathers[1].wait(tuple(parts).index(part), after)
        full = [unpack[n](lax.dynamic_update_slice_in_dim(land, shard[None], me, axis=0))
                for n, shard, land in zip(parts[part], shards, lands)]
        return full if len(full) > 1 else full[0]

    to_chunks = dict(w_in=lambda a: _chunks_rows(a, N_IN // N_DEV),
                     w_out=lambda a: _chunks_rows(a, D_MODEL // N_DEV),
                     w_up=lambda a: a, w_down=lambda a: _chunks_rows(a, FF_BLOCK),
                     w_gate=lambda a: _chunks_rows(a, D_MODEL // N_DEV), w_ple=lambda a: _chunks_cols(a, D_MODEL // N_DEV))
    own = {n: [None, None] for n in BIG}
    scatters = {}

    def send(layer, part, grads):
        if part == "small":
            per_layer, d_final, loss = grads
            pack = _pack_small(
                *[jnp.concatenate([per_layer[0][n], per_layer[1][n]], axis=0) for n in ("norm1", "norm2", "norm3")],
                d_final.reshape(D_MODEL),
                jnp.concatenate([per_layer[0]["pool_scale"], per_layer[1]["pool_scale"]], axis=0),
                jnp.stack([per_layer[0]["pool_w"], per_layer[1]["pool_w"]]), spare=loss)
            scatters["small"] = _Exchange("gather_small", [[pack]], scatter=False)
            return (scatters["small"].token,)
        for n, (g32, _) in grads.items():
            own[n][layer] = to_chunks[n](g32)
        ex = _Exchange(f"scatter_{part}_l{layer}", [[to_chunks[n](g16) for n, (_, g16) in grads.items()]], scatter=True)
        scatters[layer, part] = (tuple(grads), ex)
        return (ex.token,)

    dx, sent = _local_step(
        x.reshape(seq, D_MODEL), p.reshape(2, seq, PLE_DIM), positions.reshape(seq), loss_target.reshape(seq, D_MODEL),
        norm1, pool_w, pool_scale, norm2, norm3, final_norm, weights, send)

    g_out, d_out, m_out, v_out = {}, {}, {}, {}
    my_index = me.reshape(1)
    for part in ("main", "in"):
        recv = {}
        for layer in (1, 0):
            names, ex = scatters[layer, part]
            for n, r in zip(names, ex.wait(0, sent)[1]):
                recv[n, layer] = r
        for n in names:
            grad = (*own[n], recv[n, 0], recv[n, 1])
            if n == "w_in":
                grad = _sum_chunks(grad, my_index, "sum_w_in").transpose(0, 2, 1)
            g_out[n], d_out[n], m_out[n], v_out[n] = _adamw_sharded(w[n], m[n], v[n], grad, my_index, "adamw_" + n)
        sent = tuple(d_out[n] for n in names)
    (mine,), (landed,) = scatters["small"].wait(0, sent)
    small_g8 = lax.dynamic_update_slice_in_dim(landed, mine[None], me, axis=0)
    pack = lambda t: _pack_small(*[t[n] for n in SMALL])
    small_g, d_small, m_small, v_small = _adamw_packed(pack(w), small_g8, pack(m), pack(v), "adamw_small")
    for dst, a in ((g_out, small_g), (d_out, d_small), (m_out, m_small), (v_out, v_small)):
        dst.update(_unpack_small(a))

    return (small_g[7, 2 * POOL_WIDTH],dx.reshape(1, seq, D_MODEL), *[g_out[n] for n in ORDER], *[d_out[n] for n in ORDER],
            *[m_out[n] for n in ORDER], *[v_out[n] for n in ORDER])
```

```python
import functools

import jax
import jax.numpy as jnp
from jax import lax
from jax.experimental import pallas as pl
from jax.experimental.pallas import tpu as pltpu

F32 = jnp.float32
BF16 = jnp.bfloat16

D_MODEL = 1024
HEAD_DIM = 64
POOL_WIDTH = 256
POOL_WINDOWS = (2, 4, 8, 16)
POOL_HALO = 16
POOL_PAD = 8
GROUP_WIDTH = 256
DILATIONS = (1, 4, 16)
ATTN_BLOCK = 128
ROT_SHIFT = 8
ROPE_THETA = 500000.0
D_FF = 4096
FF_BLOCK = 512
FF_PER_STEP = 2
MLP_BWD_TILE = 512
FWD_TILE = 1024
N_DEV = 8
N_IN = POOL_WIDTH + 3 * 768
PLE_DIM = 256
EPS = 1e-6
NEG_BIG = -1e30

ADAM_LR = 0.001
ADAM_B1 = 0.9
ADAM_B2 = 0.999
ADAM_EPS = 1e-08
ADAM_WD = 0.01
ADAM_STEP = 10

LANES = 128
VMEM_LIMIT = 56 * 1024 * 1024
MESH = pl.DeviceIdType.MESH


def _params(n_grid):
    return pltpu.CompilerParams(dimension_semantics=("arbitrary",) * n_grid, vmem_limit_bytes=VMEM_LIMIT)


def _dot(a, b):
    return jnp.dot(a, b, preferred_element_type=F32)


def _dot_nt(a, b):
    return lax.dot_general(a, b, (((1,), (1,)), ((), ())), preferred_element_type=F32)


def _dot_tn(a, b):
    return lax.dot_general(a, b, (((0,), (0,)), ((), ())), preferred_element_type=F32)


def _rms(x, g):
    rstd = lax.rsqrt(jnp.mean(x * x, axis=-1, keepdims=True) + EPS)
    n = x * rstd
    return n, rstd, n * g


def _rms_bwd(dy, n, rstd, g):
    dyn = dy * g
    dx = rstd * (dyn - n * jnp.mean(dyn * n, axis=-1, keepdims=True))
    return dx, jnp.sum(dy * n, axis=0, keepdims=True)


def _ordered_after(body, n_in, after):
    if not after:
        return body
    return lambda *refs: body(*refs[:n_in], *refs[n_in + len(after):])


def _resident(shape):
    return pl.BlockSpec(shape, lambda i: (0,) * len(shape), pipeline_mode=pl.Buffered(1))


def _row_tile(s, t):
    t = min(s, t)
    assert s % t == 0
    return t


def _rot(z, c, sa, sb):
    return z * c + pltpu.roll(z, ROT_SHIFT, 1) * sa + pltpu.roll(z, LANES - ROT_SHIFT, 1) * sb


def _table_specs(t):
    return [pl.BlockSpec((t, LANES), functools.partial(lambda i, k: (i, k), k=k)) for k in range(3)]


def _rot_t(dz, c, sa, sb):
    return dz * c + pltpu.roll(dz * sa, LANES - ROT_SHIFT, 1) + pltpu.roll(dz * sb, ROT_SHIFT, 1)


def _to_residues(value, stage, out_ref, dil):
    if dil == 1:
        out_ref[0] = value.astype(out_ref.dtype)
        return
    rows = value.shape[0] // dil
    for hf in range(GROUP_WIDTH // LANES):
        lanes = slice(hf * LANES, (hf + 1) * LANES)
        stage[hf][...] = value[:, lanes]
        for r in range(dil):
            out_ref[r, :, lanes] = stage[hf][pl.ds(r, rows, stride=dil), :].astype(out_ref.dtype)


def _from_residues(in_ref, stage, dil):
    if dil == 1:
        return in_ref[0].astype(F32)
    rows = in_ref.shape[1]
    for hf in range(GROUP_WIDTH // LANES):
        for r in range(dil):
            stage[hf][pl.ds(r, rows, stride=dil), :] = in_ref[r, :, hf * LANES:(hf + 1) * LANES].astype(F32)
    return jnp.concatenate([stage[0][...], stage[1][...]], axis=1)


def _residue_spec(dil, t):
    return pl.BlockSpec((dil, t // dil, GROUP_WIDTH), lambda i: (0, i, 0))


def _residue_shape(dil, s, dtype):
    return jax.ShapeDtypeStruct((dil, s // dil, GROUP_WIDTH), dtype)


def _stages(t, n):
    return [pltpu.VMEM((t, LANES), F32)] * (n * (GROUP_WIDTH // LANES))


def _pair_stages(refs):
    return [refs[i:i + 2] for i in range(0, len(refs), 2)]


def _normproj_tile(x, g_ref, w_ref, c_ref, sa_ref, sb_ref, hn_ref, u_ref, *rest):
    qkv_refs, stages = rest[:9], _pair_stages(rest[9:])
    _, _, hn = _rms(x, g_ref[...])
    hb = hn.astype(BF16)
    hn_ref[...] = hb
    c, sa, sb = c_ref[...], sa_ref[...], sb_ref[...]

    def rot(z, scale):
        halves = [_rot(z[:, hf * LANES:(hf + 1) * LANES], c, sa, sb) * scale for hf in range(2)]
        return jnp.concatenate(halves, axis=1)

    proj = lambda lo: _dot_nt(hb, w_ref[lo:lo + GROUP_WIDTH, :])
    u_ref[...] = proj(0)
    for grp, dil in enumerate(DILATIONS):
        lo = POOL_WIDTH + grp * GROUP_WIDTH
        q_ref, k_ref, v_ref = qkv_refs[3 * grp:3 * grp + 3]
        _to_residues(rot(proj(lo), HEAD_DIM ** -0.5), stages[0], q_ref, dil)
        _to_residues(rot(proj(lo + 768), 1.0), stages[1], k_ref, dil)
        _to_residues(proj(lo + 1536), stages[2], v_ref, dil)


def _normproj_operands(s, t):
    row = lambda w: pl.BlockSpec((t, w), lambda i: (i, 0))
    in_specs = [pl.BlockSpec((1, D_MODEL), lambda i: (0, 0)), _resident((N_IN, D_MODEL))] + _table_specs(t)
    out_specs = [row(D_MODEL), row(POOL_WIDTH)] + [_residue_spec(dil, t) for dil in DILATIONS for _ in range(3)]
    out_shape = [jax.ShapeDtypeStruct((s, D_MODEL), BF16), jax.ShapeDtypeStruct((s, POOL_WIDTH), F32)]
    out_shape += [_residue_shape(dil, s, BF16) for dil in DILATIONS for _ in range(3)]
    return in_specs, out_specs, out_shape, _stages(t, 3)


def _normproj_fwd(h, g, w_in, rc, rsa, rsb, name):
    s = h.shape[0]
    t = _row_tile(s, FWD_TILE)

    def body(h_ref, *refs):
        _normproj_tile(h_ref[...], *refs)

    in_specs, out_specs, out_shape, scratch = _normproj_operands(s, t)
    return pl.pallas_call(
        body, name=name, grid=(s // t,), in_specs=[pl.BlockSpec((t, D_MODEL), lambda i: (i, 0))] + in_specs,
        out_specs=out_specs, out_shape=out_shape, scratch_shapes=scratch, compiler_params=_params(1),
    )(h, g, w_in, rc, rsa, rsb)


def _pool_lane_window():
    lane = lax.broadcasted_iota(jnp.int32, (1, POOL_WIDTH), 1)
    return jnp.left_shift(2, lane // (POOL_WIDTH // len(POOL_WINDOWS)))


def _window_sums(ext, b2, b4, b8, t, lo, tile, direction):
    rows = t + POOL_HALO
    for src, dst, sh in ((ext, b2, 1), (b2, b4, 2), (b4, b8, 4)):
        dst[lo:lo + rows, :] = src[lo:lo + rows, :] + src[lo + direction * sh:lo + direction * sh + rows, :]
    s16 = b8[tile:tile + t, :] + b8[tile + direction * 8:tile + direction * 8 + t, :]
    win = _pool_lane_window()
    return jnp.where(win == 2, b2[tile:tile + t, :],
                     jnp.where(win == 4, b4[tile:tile + t, :], jnp.where(win == 8, b8[tile:tile + t, :], s16)))


def _pool_fwd_tile(i, u_ref, w_ref, sc_ref, y_ref, ext, b2, b4, b8):
    t = u_ref.shape[0]
    first = POOL_PAD + POOL_HALO

    @pl.when(i == 0)
    def _():
        for buf in (ext, b2, b4):
            buf[0:POOL_PAD, :] = jnp.zeros((POOL_PAD, POOL_WIDTH), F32)
        ext[POOL_PAD:first, :] = jnp.zeros((POOL_HALO, POOL_WIDTH), F32)

    x = u_ref[...]
    ext[first:, :] = x
    wsum = _window_sums(ext, b2, b4, b8, t, POOL_PAD, first, -1)
    pos = i * t + lax.broadcasted_iota(jnp.int32, (t, POOL_WIDTH), 0)
    cnt = jnp.minimum(pos + 1, _pool_lane_window()).astype(F32)
    yb = (wsum / cnt - x).astype(BF16)
    y_ref[...] = yb
    ext[POOL_PAD:first, :] = x[t - POOL_HALO:, :]
    return _dot(yb, w_ref[...]) * sc_ref[...]


def _head_masks():
    lane = lax.broadcasted_iota(jnp.int32, (ATTN_BLOCK, GROUP_WIDTH), 1)
    return [lane // HEAD_DIM == hd for hd in range(GROUP_WIDTH // HEAD_DIM)]


def _stack_heads(a, masks):
    zero = jnp.zeros_like(a)
    return jnp.concatenate([jnp.where(m, a, zero) for m in masks], axis=0)


def _band_bias(first_step):
    rows = ATTN_BLOCK * (GROUP_WIDTH // HEAD_DIM)
    i = lax.broadcasted_iota(jnp.int32, (rows, 2 * ATTN_BLOCK), 0) & (ATTN_BLOCK - 1)
    j = lax.broadcasted_iota(jnp.int32, (rows, 2 * ATTN_BLOCK), 1)
    inner = jnp.where((j >= i) & (j <= i + ATTN_BLOCK), 0.0, NEG_BIG)
    return jnp.where((j < ATTN_BLOCK) & first_step, NEG_BIG, inner), inner


def _column_per_head(a):
    return jnp.concatenate([a[:, hd * HEAD_DIM:hd * HEAD_DIM + 1] for hd in range(GROUP_WIDTH // HEAD_DIM)], axis=0)


def _blocks_per_step(nb):
    if nb <= 16:
        return nb
    return next(qb for qb in (16, 8, 4, 2, 1) if nb % qb == 0)


def _residues_per_step(dil, nb, qb):
    return 2 if (nb == qb and qb < 8 and dil % 2 == 0) else 1


def _attn_fwd(q, k, v, name, after=()):
    dil, length, _ = q.shape
    nb = length // ATTN_BLOCK
    qb = _blocks_per_step(nb)
    rs = _residues_per_step(dil, nb, qb)

    def body(q_ref, kp_ref, kc_ref, vp_ref, vc_ref, o_ref, lse_ref):
        masks = _head_masks()
        bias = _band_bias(pl.program_id(1) == 0)
        for rr in range(rs):
            for qi in range(qb):
                here = slice(qi * ATTN_BLOCK, (qi + 1) * ATTN_BLOCK)
                before = slice((qi - 1) * ATTN_BLOCK, qi * ATTN_BLOCK)
                kcat = jnp.concatenate([kp_ref[rr] if qi == 0 else kc_ref[rr, before], kc_ref[rr, here]], axis=0)
                vcat = jnp.concatenate([vp_ref[rr] if qi == 0 else vc_ref[rr, before], vc_ref[rr, here]], axis=0)
                qs = _stack_heads(q_ref[rr, here], masks)
                sc = _dot_nt(qs, kcat) + bias[min(qi, 1)]
                m = jnp.max(sc, axis=1, keepdims=True)
                e = jnp.exp(sc - m)
                l = jnp.sum(e, axis=1, keepdims=True)
                p = (e / l).astype(BF16)
                lse = m + jnp.log(l)
                o = jnp.zeros((ATTN_BLOCK, GROUP_WIDTH), F32)
                lse_full = jnp.zeros((ATTN_BLOCK, GROUP_WIDTH), F32)
                for hd, msk in enumerate(masks):
                    rows = slice(hd * ATTN_BLOCK, (hd + 1) * ATTN_BLOCK)
                    o = jnp.where(msk, _dot(p[rows], vcat), o)
                    lse_full = jnp.where(msk, lse[rows], lse_full)
                o_ref[rr, here] = o.astype(o_ref.dtype)
                lse_ref[rr, here] = lse_full

    cur = pl.BlockSpec((rs, qb * ATTN_BLOCK, GROUP_WIDTH), lambda r, j: (r, j, 0))
    prev = pl.BlockSpec((rs, ATTN_BLOCK, GROUP_WIDTH), lambda r, j: (r, jnp.maximum(qb * j - 1, 0), 0))
    return pl.pallas_call(
        _ordered_after(body, 5, after), name=name, grid=(dil // rs, nb // qb),
        in_specs=[cur, prev, cur, prev, cur] + [pl.BlockSpec(memory_space=pl.ANY)] * len(after), out_specs=[cur, cur],
        out_shape=[jax.ShapeDtypeStruct(q.shape, BF16), jax.ShapeDtypeStruct(q.shape, F32)],
        compiler_params=_params(2),
    )(q, k, k, v, v, *after)


def _group_weights(l0, l1, l2):
    m = jnp.maximum(jnp.maximum(l0, l1), l2)
    e0, e1, e2 = jnp.exp(l0 - m), jnp.exp(l1 - m), jnp.exp(l2 - m)
    den = e0 + e1 + e2
    return e0 / den, e1 / den, e2 / den


def _outproj_fwd(h, u, w_bd, scale, o, lse, w_out, name):
    s = h.shape[0]
    t = _row_tile(s, FWD_TILE)

    def body(h_ref, u_ref, wbd_ref, sc_ref, o0, o1, o2, l0, l1, l2, w_ref, out_ref, a_ref, y_ref, ext, b2, b4, b8,
             *stages):
        pool_out = _pool_fwd_tile(pl.program_id(0), u_ref, wbd_ref, sc_ref, y_ref, ext, b2, b4, b8)
        stages = _pair_stages(stages)
        ov = [_from_residues(r, stages[i], DILATIONS[i]) for i, r in enumerate((o0, o1, o2))]
        lv = [_from_residues(r, stages[3 + i], DILATIONS[i]) for i, r in enumerate((l0, l1, l2))]
        wts = _group_weights(*lv)
        a = jnp.concatenate([pool_out] + [ov[i] * wts[i] for i in range(3)], axis=1).astype(BF16)
        a_ref[...] = a
        out_ref[...] = h_ref[...] + _dot(a, w_ref[...])

    row = lambda w: pl.BlockSpec((t, w), lambda i: (i, 0))
    res = [_residue_spec(dil, t) for dil in DILATIONS]
    return pl.pallas_call(
        body, name=name, grid=(s // t,),
        in_specs=[row(D_MODEL), row(POOL_WIDTH), _resident((POOL_WIDTH, POOL_WIDTH)), _resident((1, POOL_WIDTH))]
        + res + res + [_resident((D_MODEL, D_MODEL))],
        out_specs=[row(D_MODEL), row(D_MODEL), row(POOL_WIDTH)],
        out_shape=[jax.ShapeDtypeStruct((s, D_MODEL), F32), jax.ShapeDtypeStruct((s, D_MODEL), BF16),
                   jax.ShapeDtypeStruct((s, POOL_WIDTH), BF16)],
        scratch_shapes=[pltpu.VMEM((t + POOL_HALO + POOL_PAD, POOL_WIDTH), F32)] * 4 + _stages(t, 6),
        compiler_params=_params(1),
    )(h, u, w_bd, scale, *o, *lse, w_out)


def _mlp_fwd(h, g, w_up, w_down, name):
    s = h.shape[0]
    t = _row_tile(s, 512)
    nblk = D_FF // FF_BLOCK

    def body(h_ref, g_ref, wu_ref, wd_ref, out_ref, hn_ref, r_ref):
        x = h_ref[...]
        _, _, hn = _rms(x, g_ref[...])
        hb = hn.astype(BF16)
        hn_ref[...] = hb
        acc = None
        for b0 in range(0, nblk, FF_PER_STEP):
            acts = []
            for b in range(b0, b0 + FF_PER_STEP):
                r = jnp.maximum(_dot(hb, wu_ref[b]), 0.0)
                r_ref[:, b * FF_BLOCK:(b + 1) * FF_BLOCK] = r.astype(BF16)
                acts.append((r * r).astype(BF16))
            wd = wd_ref[b0:b0 + FF_PER_STEP].reshape(FF_PER_STEP * FF_BLOCK, D_MODEL)
            part = _dot(jnp.concatenate(acts, axis=1), wd)
            acc = part if acc is None else acc + part
        out_ref[...] = x + acc

    row = lambda w: pl.BlockSpec((t, w), lambda i: (i, 0))
    resident = lambda shape: pl.BlockSpec(shape, lambda i: (0, 0, 0), pipeline_mode=pl.Buffered(1))
    return pl.pallas_call(
        body, name=name, grid=(s // t,),
        in_specs=[row(D_MODEL), pl.BlockSpec((1, D_MODEL), lambda i: (0, 0)),
                  resident((nblk, D_MODEL, FF_BLOCK)), resident((nblk, FF_BLOCK, D_MODEL))],
        out_specs=[row(D_MODEL), row(D_MODEL), row(D_FF)],
        out_shape=[jax.ShapeDtypeStruct((s, D_MODEL), F32), jax.ShapeDtypeStruct((s, D_MODEL), BF16),
                   jax.ShapeDtypeStruct((s, D_FF), BF16)],
        compiler_params=_params(1),
    )(h, g, w_up, w_down)


def _gate_fwd(h, g, w_gate, p, layer, w_ple, name, head=None, follow=None):
    s = h.shape[0]
    t = _row_tile(s, FWD_TILE if follow is None else 512)

    def body(h_ref, g_ref, wg_ref, p_ref, wp_ref, *refs):
        x = h_ref[...]
        _, _, hn = _rms(x, g_ref[...])
        hb = hn.astype(BF16)
        gate = 1.0 / (1.0 + jnp.exp(-_dot(hb, wg_ref[...])))
        pb = p_ref[...].astype(BF16)
        h3 = x + gate * _dot(pb, wp_ref[...])
        if follow is not None:
            out_ref, hn_ref, gate_ref, pb_ref = refs[5:9]
            out_ref[...] = h3
            _normproj_tile(h3, *refs[:5], *refs[9:])
        elif head is None:
            out_ref, hn_ref, gate_ref, pb_ref = refs
            out_ref[...] = h3
        else:
            gf_ref, t_ref, hn_ref, gate_ref, pb_ref, loss_ref, dh_ref, dgf_ref = refs

            @pl.when(pl.program_id(0) == 0)
            def _():
                loss_ref[...] = jnp.zeros_like(loss_ref)
                dgf_ref[...] = jnp.zeros_like(dgf_ref)

            gf = gf_ref[...]
            n, rstd, y = _rms(h3, gf)
            err = y - t_ref[...]
            loss_ref[...] += jnp.sum(err * err) * (0.5 / D_MODEL)
            dh_ref[...], dgf = _rms_bwd(err * (1.0 / D_MODEL), n, rstd, gf)
            dgf_ref[...] += dgf
        hn_ref[...] = hb
        pb_ref[...] = pb
        gate_ref[...] = gate.astype(BF16)

    row = lambda w: pl.BlockSpec((t, w), lambda i: (i, 0))
    full = lambda a, b: pl.BlockSpec((a, b), lambda i: (0, 0))
    in_specs = [row(D_MODEL), full(1, D_MODEL), _resident((D_MODEL, D_MODEL)),
                pl.BlockSpec((None, t, PLE_DIM), lambda i: (layer, i, 0)), _resident((PLE_DIM, D_MODEL))]
    saved_specs = [row(D_MODEL), row(D_MODEL), row(PLE_DIM)]
    saved_shapes = [jax.ShapeDtypeStruct((s, D_MODEL), BF16), jax.ShapeDtypeStruct((s, D_MODEL), BF16),
                    jax.ShapeDtypeStruct((s, PLE_DIM), BF16)]
    if follow is not None:
        next_in, next_out, next_shape, scratch = _normproj_operands(s, t)
        return pl.pallas_call(
            body, name=name, grid=(s // t,), in_specs=in_specs + next_in,
            out_specs=[row(D_MODEL)] + saved_specs + next_out,
            out_shape=[jax.ShapeDtypeStruct((s, D_MODEL), F32)] + saved_shapes + next_shape, scratch_shapes=scratch,
            compiler_params=_params(1),
        )(h, g, w_gate, p, w_ple, *follow)
    if head is None:
        return pl.pallas_call(
            body, name=name, grid=(s // t,), in_specs=in_specs, out_specs=[row(D_MODEL)] + saved_specs,
            out_shape=[jax.ShapeDtypeStruct((s, D_MODEL), F32)] + saved_shapes, compiler_params=_params(1),
        )(h, g, w_gate, p, w_ple)
    return pl.pallas_call(
        body, name=name, grid=(s // t,), in_specs=in_specs + [full(1, D_MODEL), row(D_MODEL)],
        out_specs=saved_specs + [pl.BlockSpec((1, LANES), lambda i: (0, 0)), row(D_MODEL), full(1, D_MODEL)],
        out_shape=saved_shapes + [jax.ShapeDtypeStruct((1, LANES), F32), jax.ShapeDtypeStruct((s, D_MODEL), F32),
                                  jax.ShapeDtypeStruct((1, D_MODEL), F32)],
        compiler_params=_params(1),
    )(h, g, w_gate, p, w_ple, *head)


def _gate_bwd_tile(d, last, gate_ref, pb_ref, wp_ref, h_ref, g_ref, wg_ref, hn_ref, out_ref, dg_ref, dwg_ref,
                   dwgb_ref, dwp_ref, dwpb_ref):
    i = pl.program_id(0)

    @pl.when(i == 0)
    def _():
        dg_ref[...] = jnp.zeros_like(dg_ref)
        dwg_ref[...] = jnp.zeros_like(dwg_ref)
        dwp_ref[...] = jnp.zeros_like(dwp_ref)

    gate = gate_ref[...].astype(F32)
    pb = pb_ref[...]
    e = _dot(pb, wp_ref[...])
    dgl = (d * e * gate * (1.0 - gate)).astype(BF16)
    dwg_ref[...] += _dot_tn(hn_ref[...], dgl)
    dwp_ref[...] += _dot_tn(pb, (d * gate).astype(BF16))
    gv = g_ref[...]
    n, rstd, _ = _rms(h_ref[...], gv)
    dx, dg = _rms_bwd(_dot_nt(dgl, wg_ref[...]), n, rstd, gv)
    out_ref[...] = d + dx
    dg_ref[...] += dg

    @pl.when(i == last)
    def _():
        dwgb_ref[...] = dwg_ref[...].astype(BF16)
        dwpb_ref[...] = dwp_ref[...].astype(BF16)


def _gate_bwd_operands(s, t):
    row = lambda w: pl.BlockSpec((t, w), lambda i: (i, 0))
    full = lambda a, b: pl.BlockSpec((a, b), lambda i: (0, 0))
    in_specs = [row(D_MODEL), row(PLE_DIM), _resident((PLE_DIM, D_MODEL)), row(D_MODEL), full(1, D_MODEL),
                _resident((D_MODEL, D_MODEL)), row(D_MODEL)]
    out_specs = [row(D_MODEL), full(1, D_MODEL), full(D_MODEL, D_MODEL), full(D_MODEL, D_MODEL),
                 full(PLE_DIM, D_MODEL), full(PLE_DIM, D_MODEL)]
    out_shape = [jax.ShapeDtypeStruct((s, D_MODEL), F32), jax.ShapeDtypeStruct((1, D_MODEL), F32),
                 jax.ShapeDtypeStruct((D_MODEL, D_MODEL), F32), jax.ShapeDtypeStruct((D_MODEL, D_MODEL), BF16),
                 jax.ShapeDtypeStruct((PLE_DIM, D_MODEL), F32), jax.ShapeDtypeStruct((PLE_DIM, D_MODEL), BF16)]
    return in_specs, out_specs, out_shape


def _gate_bwd(dh, gate, pb, w_ple, h, g, w_gate, hn, name, after=()):
    s = h.shape[0]
    t = _row_tile(s, FWD_TILE)

    def body(dh_ref, *refs):
        _gate_bwd_tile(dh_ref[...], s // t - 1, *refs)

    in_specs, out_specs, out_shape = _gate_bwd_operands(s, t)
    dh2, dg, dwg, dwgb, dwp, dwpb = pl.pallas_call(
        _ordered_after(body, 8, after), name=name, grid=(s // t,),
        in_specs=[pl.BlockSpec((t, D_MODEL), lambda i: (i, 0))] + in_specs
        + [pl.BlockSpec(memory_space=pl.ANY)] * len(after),
        out_specs=out_specs, out_shape=out_shape, compiler_params=_params(1),
    )(dh, gate, pb, w_ple, h, g, w_gate, hn, *after)
    return dh2, dg, (dwg, dwgb), (dwp, dwpb)


def _mlp_bwd(dh, r, h, g, w_up, w_down, name, after=()):
    s = h.shape[0]
    t = _row_tile(s, MLP_BWD_TILE)
    nblk = D_FF // FF_BLOCK

    def body(dh_ref, r_ref, h_ref, g_ref, wu_ref, wd_ref, out_ref, dup_ref, dg_ref, dhb_ref):
        @pl.when(pl.program_id(0) == 0)
        def _():
            dg_ref[...] = jnp.zeros_like(dg_ref)

        d = dh_ref[...]
        db = d.astype(BF16)
        dhb_ref[...] = db
        back = None
        for b in range(nblk):
            cols = slice(b * FF_BLOCK, (b + 1) * FF_BLOCK)
            dup = (_dot_nt(db, wd_ref[b]) * (2.0 * r_ref[:, cols].astype(F32))).astype(BF16)
            dup_ref[:, cols] = dup
            part = _dot_nt(dup, wu_ref[b])
            back = part if back is None else back + part
        gv = g_ref[...]
        n, rstd, _ = _rms(h_ref[...], gv)
        dx, dg = _rms_bwd(back, n, rstd, gv)
        out_ref[...] = d + dx
        dg_ref[...] += dg

    row = lambda w: pl.BlockSpec((t, w), lambda i: (i, 0))
    vec = pl.BlockSpec((1, D_MODEL), lambda i: (0, 0))
    resident = lambda shape: pl.BlockSpec(shape, lambda i: (0, 0, 0), pipeline_mode=pl.Buffered(1))
    return pl.pallas_call(
        _ordered_after(body, 6, after), name=name, grid=(s // t,),
        in_specs=[row(D_MODEL), row(D_FF), row(D_MODEL), vec,
                  resident((nblk, D_MODEL, FF_BLOCK)), resident((nblk, FF_BLOCK, D_MODEL))]
        + [pl.BlockSpec(memory_space=pl.ANY)] * len(after),
        out_specs=[row(D_MODEL), row(D_FF), vec, row(D_MODEL)],
        out_shape=[jax.ShapeDtypeStruct((s, D_MODEL), F32), jax.ShapeDtypeStruct((s, D_FF), BF16),
                   jax.ShapeDtypeStruct((1, D_MODEL), F32), jax.ShapeDtypeStruct((s, D_MODEL), BF16)],
        compiler_params=_params(1),
    )(dh, r, h, g, w_up, w_down, *after)


def _outproj_bwd(dh, w_out, o, lse, ones_bd, a, name):
    s = dh.shape[0]
    t = _row_tile(s, 512)
    last = s // t - 1

    def body(dh_ref, w_ref, o0, o1, o2, l0, l1, l2, bd_ref, a_ref, dp_ref, do0, do1, do2, de0, de1, de2, dw_ref,
             dwb_ref, *stages):
        i = pl.program_id(0)

        @pl.when(i == 0)
        def _():
            dw_ref[...] = jnp.zeros_like(dw_ref)

        stages = _pair_stages(stages)
        dhb = dh_ref[...].astype(BF16)
        dw_ref[...] += _dot_tn(a_ref[...], dhb)

        @pl.when(i == last)
        def _():
            dwb_ref[...] = dw_ref[...].astype(BF16)

        da = _dot_nt(dhb, w_ref[...])
        dp_ref[...] = da[:, 0:POOL_WIDTH]
        ov =[_from_residues(r, stages[i], DILATIONS[i]) for i, r in enumerate((o0, o1, o2))]
        lv = [_from_residues(r, stages[3 + i], DILATIONS[i]) for i, r in enumerate((l0, l1, l2))]
        wts = _group_weights(*lv)
        bd = bd_ref[...]
        cbar = jnp.zeros((t, GROUP_WIDTH), F32)
        for grp, do_ref in enumerate((do0, do1, do2)):
            lo = POOL_WIDTH + grp * GROUP_WIDTH
            dag = da[:, lo:lo + GROUP_WIDTH]
            _to_residues(dag * wts[grp], stages[6 + grp], do_ref, DILATIONS[grp])
            prod = dag * ov[grp]
            hi = prod.astype(BF16)
            low = (prod - hi.astype(F32)).astype(BF16)
            cbar = cbar + wts[grp] * (_dot(hi, bd) + _dot(low, bd))
        for grp, de_ref in enumerate((de0, de1, de2)):
            _to_residues(wts[grp] * cbar, stages[9 + grp], de_ref, DILATIONS[grp])

    row = lambda w: pl.BlockSpec((t, w), lambda i: (i, 0))
    full = lambda a, b: pl.BlockSpec((a, b), lambda i: (0, 0))
    res = [_residue_spec(dil, t) for dil in DILATIONS]
    *outs, dw, dwb = pl.pallas_call(
        body, name=name, grid=(s // t,),
        in_specs=[row(D_MODEL), full(D_MODEL, D_MODEL)] + res + res + [full(GROUP_WIDTH, GROUP_WIDTH), row(D_MODEL)],
        out_specs=[row(POOL_WIDTH)] + res + res + [full(D_MODEL, D_MODEL)] * 2,
        out_shape=[jax.ShapeDtypeStruct((s, POOL_WIDTH), F32)] + [_residue_shape(dil, s, BF16) for dil in DILATIONS]
        + [_residue_shape(dil, s, F32) for dil in DILATIONS]
        + [jax.ShapeDtypeStruct((D_MODEL, D_MODEL), F32), jax.ShapeDtypeStruct((D_MODEL, D_MODEL), BF16)],
        scratch_shapes=_stages(t, 12),
        compiler_params=_params(1),
    )(dh, w_out, *o, *lse, ones_bd, a)
    return (*outs, (dw, dwb))


def _attn_bwd(q, k, v, do, lse, deff, name, after=()):
    dil, length, _ = q.shape
    nb = length // ATTN_BLOCK
    qb = _blocks_per_step(nb)
    nj = nb // qb
    rs = _residues_per_step(dil, nb, qb)
    whole = nj == 1
    tail = slice((qb - 1) * ATTN_BLOCK, qb * ATTN_BLOCK)
    block = lambda qi: slice(qi * ATTN_BLOCK, (qi + 1) * ATTN_BLOCK)

    def body(q_ref, kp_ref, kc_ref, vp_ref, vc_ref, do_ref, lse_ref, de_ref, dq_ref, dk_ref, dv_ref, ck, cv):
        j = pl.program_id(1)

        def compute():
            masks = _head_masks()
            bias = _band_bias(j == 0)
            for rr in range(rs):
                dkc, dvc = [], []
                for qi in range(qb):
                    here, before = block(qi), block(qi - 1)
                    kcat = jnp.concatenate([kp_ref[rr] if qi == 0 else kc_ref[rr, before], kc_ref[rr, here]], axis=0)
                    vcat = jnp.concatenate([vp_ref[rr] if qi == 0 else vc_ref[rr, before], vc_ref[rr, here]], axis=0)
                    qs = _stack_heads(q_ref[rr, here], masks)
                    dos = _stack_heads(do_ref[rr, here], masks)
                    sc = _dot_nt(qs, kcat) + bias[min(qi, 1)]
                    p = jnp.exp(sc - _column_per_head(lse_ref[rr, here]))
                    ds = (p * (_dot_nt(dos, vcat) - _column_per_head(de_ref[rr, here]))).astype(BF16)
                    dq = jnp.zeros((ATTN_BLOCK, GROUP_WIDTH), F32)
                    for hd, msk in enumerate(masks):
                        dq = jnp.where(msk, _dot(ds[block(hd)], kcat), dq)
                    dq_ref[rr, here] = dq.astype(dq_ref.dtype)
                    dkc.append(_dot_tn(ds, qs))
                    dvc.append(_dot_tn(p.astype(BF16), dos))

                for out_ref, carry, parts in ((dk_ref, ck, dkc), (dv_ref, cv, dvc)):
                    full = [parts[qi][ATTN_BLOCK:] + parts[qi + 1][0:ATTN_BLOCK] for qi in range(qb - 1)]
                    if whole:
                        for qi, val in enumerate(full + [parts[qb - 1][ATTN_BLOCK:]]):
                            out_ref[rr, block(qi)] = val.astype(out_ref.dtype)
                        continue

                    @pl.when(j > 0)
                    def _():
                        if qb > 1:
                            out_ref[0, 0:(qb - 1) * ATTN_BLOCK] = carry[0:(qb - 1) * ATTN_BLOCK].astype(out_ref.dtype)
                        out_ref[0, tail] = (carry[tail] + parts[0][0:ATTN_BLOCK]).astype(out_ref.dtype)

                    for qi, val in enumerate(full):
                        carry[block(qi)] = val
                    carry[tail] = parts[qb - 1][ATTN_BLOCK:]

        if whole:
            compute()
        else:
            pl.when(j < nj)(compute)

            @pl.when(j == nj)
            def _():
                dk_ref[0] = ck[...].astype(dk_ref.dtype)
                dv_ref[0] = cv[...].astype(dv_ref.dtype)

    step = lambda j: jnp.minimum(j, nj - 1)
    cur = pl.BlockSpec((rs, qb * ATTN_BLOCK, GROUP_WIDTH), lambda r, j: (r, step(j), 0))
    prev = pl.BlockSpec((rs, ATTN_BLOCK, GROUP_WIDTH), lambda r, j: (r, jnp.maximum(qb * step(j) - 1, 0), 0))
    late = pl.BlockSpec((rs, qb * ATTN_BLOCK, GROUP_WIDTH), lambda r, j: (r, jnp.maximum(j - 1, 0), 0))
    return pl.pallas_call(
        _ordered_after(body, 8, after), name=name, grid=(dil // rs, 1 if whole else nj + 1),
        in_specs=[cur, prev, cur, prev, cur, cur, cur, cur] + [pl.BlockSpec(memory_space=pl.ANY)] * len(after),
        out_specs=[cur, cur if whole else late, cur if whole else late],
        out_shape=[jax.ShapeDtypeStruct(q.shape, BF16)] * 3,
        scratch_shapes=[pltpu.VMEM((qb * ATTN_BLOCK, GROUP_WIDTH), F32)] * 2,
        compiler_params=_params(2),
    )(q, k, k, v, v, do, lse, deff, *after)


def _pool_bwd(dpool, y, w_bd, scale, name, after=()):
    s = dpool.shape[0]
    t = _row_tile(s, 512)
    nt = s // t

    def body(dp_ref, y_ref, w_ref, sc_ref, du_ref, dw_ref, dsc_ref, ext, b2, b4, b8):
        i = pl.program_id(0)

        @pl.when(i == 0)
        def _():
            ext[t:, :] = jnp.zeros((POOL_HALO + POOL_PAD, POOL_WIDTH), F32)
            for buf in (b2, b4):
                buf[t + POOL_HALO:, :] = jnp.zeros((POOL_PAD, POOL_WIDTH), F32)
            dw_ref[...] = jnp.zeros_like(dw_ref)
            dsc_ref[...] = jnp.zeros_like(dsc_ref)

        dp = dp_ref[...]
        yb = y_ref[...]
        w = w_ref[...]
        dsc_ref[...] += jnp.sum(dp * _dot(yb, w), axis=0, keepdims=True)
        dyo = (dp * sc_ref[...]).astype(BF16)
        dw_ref[...] += _dot_tn(yb, dyo)
        dy = _dot_nt(dyo, w)
        win = _pool_lane_window()
        pos = (nt - 1 - i) * t + lax.broadcasted_iota(jnp.int32, (t, POOL_WIDTH), 0)
        gq = dy / jnp.minimum(pos + 1, win).astype(F32)
        ext[0:t, :] = gq
        du_ref[...] = _window_sums(ext, b2, b4, b8, t, 0, 0, 1) - dy
        ext[t:t + POOL_HALO, :] = gq[0:POOL_HALO, :]

    rev = pl.BlockSpec((t, POOL_WIDTH), lambda i: (nt - 1 - i, 0))
    full = lambda a, b: pl.BlockSpec((a, b), lambda i: (0, 0))
    return pl.pallas_call(
        _ordered_after(body, 4, after), name=name, grid=(nt,),
        in_specs=[rev, rev, full(POOL_WIDTH, POOL_WIDTH), full(1, POOL_WIDTH)]
        + [pl.BlockSpec(memory_space=pl.ANY)] * len(after),
        out_specs=[rev, full(POOL_WIDTH, POOL_WIDTH), full(1, POOL_WIDTH)],
        out_shape=[jax.ShapeDtypeStruct((s, POOL_WIDTH), F32), jax.ShapeDtypeStruct((POOL_WIDTH, POOL_WIDTH), F32),
                   jax.ShapeDtypeStruct((1, POOL_WIDTH), F32)],
        scratch_shapes=[pltpu.VMEM((t + POOL_HALO + POOL_PAD, POOL_WIDTH), F32)] * 4,
        compiler_params=_params(1),
    )(dpool, y, w_bd, scale, *after)


def _normproj_bwd(dh, du, dq, dk, dv, rc, rsa, rsb, w_in, h, g, name, follow=None):
    s = h.shape[0]
    follow = () if follow is None else tuple(follow)
    t = _row_tile(s, 256 if follow else 512)
    n_follow = len(follow)

    def body(dh_ref, du_ref, q0, q1, q2, k0, k1, k2, v0, v1, v2, c_ref, sa_ref, sb_ref, w_ref, h_ref, g_ref, *refs):
        follow_in, refs = refs[:n_follow], refs[n_follow:]
        if not follow:
            out_ref, dz_ref, dg_ref, *stages = refs
        else:
            follow_out, (dz_ref, dg_ref, *stages) = refs[:6], refs[6:]

        @pl.when(pl.program_id(0) == 0)
        def _():
            dg_ref[...] = jnp.zeros_like(dg_ref)

        c, sa, sb = c_ref[...], sa_ref[...], sb_ref[...]

        def unrot(a, scale):
            halves = [_rot_t(a[:, hf * LANES:(hf + 1) * LANES] * scale, c, sa, sb) for hf in range(2)]
            return jnp.concatenate(halves, axis=1)

        staged = _pair_stages(stages)
        tok = lambda refs, base: [_from_residues(r, staged[base + i], DILATIONS[i]) for i, r in enumerate(refs)]
        chunks = [du_ref[...]]
        chunks += [unrot(a, HEAD_DIM ** -0.5) for a in tok((q0, q1, q2), 0)]
        chunks += [unrot(a, 1.0) for a in tok((k0, k1, k2), 3)]
        chunks += tok((v0, v1, v2), 6)
        acc = jnp.zeros((t, D_MODEL), F32)
        for ci, ch in enumerate(chunks):
            cols = slice(ci * GROUP_WIDTH, (ci + 1) * GROUP_WIDTH)
            cb = ch.astype(BF16)
            dz_ref[:, cols] = cb
            acc = acc + _dot(cb, w_ref[cols, :])
        gv = g_ref[...]
        n, rstd, _ = _rms(h_ref[...], gv)
        dx, dg = _rms_bwd(acc, n, rstd, gv)
        dg_ref[...] += dg
        if not follow:
            out_ref[...] = dh_ref[...] + dx
        else:
            _gate_bwd_tile(dh_ref[...] + dx, s // t - 1, *follow_in, *follow_out)

    row = lambda w: pl.BlockSpec((t, w), lambda i: (i, 0))
    vec = pl.BlockSpec((1, D_MODEL), lambda i: (0, 0))
    res = [_residue_spec(dil, t) for dil in DILATIONS]
    in_specs = [row(D_MODEL), row(POOL_WIDTH)] + res * 3 + _table_specs(t) + [_resident((N_IN, D_MODEL)), row(D_MODEL), vec]
    out_specs = [row(N_IN), vec]
    out_shape = [jax.ShapeDtypeStruct((s, N_IN), BF16), jax.ShapeDtypeStruct((1, D_MODEL), F32)]
    if not follow:
        first_specs, first_shape = [row(D_MODEL)], [jax.ShapeDtypeStruct((s, D_MODEL), F32)]
    else:
        follow_specs, first_specs, first_shape = _gate_bwd_operands(s, t)
        in_specs += follow_specs
    return pl.pallas_call(
        body, name=name, grid=(s // t,), in_specs=in_specs, out_specs=first_specs + out_specs,
        out_shape=first_shape + out_shape, scratch_shapes=_stages(t, 9), compiler_params=_params(1),
    )(dh, du, *dq, *dk, *dv, rc, rsa, rsb, w_in, h, g, *follow)


def _matmul_tn(a, b, name, *, square_a=False, tm=None, tn=None, blocked_out=False, after=()):
    s, m = a.shape
    n = b.shape[1]
    tk = _row_tile(s, 2048)
    tm = tm or min(m, 1024)
    tn = tn or min(n, 1024)
    assert m % tm == 0 and n % tn == 0
    nk = s // tk
    nsub = tn // FF_BLOCK if blocked_out else 1

    def body(a_ref, b_ref, o_ref, ob_ref, acc):
        k = pl.program_id(2)

        def product():
            av = a_ref[...]
            if square_a:
                av = av.astype(F32)
                av = av * av
            return _dot_tn(av.astype(BF16), b_ref[...].astype(BF16))

        def emit(total):
            if blocked_out:
                for sub in range(nsub):
                    cols = slice(sub * FF_BLOCK, (sub + 1) * FF_BLOCK)
                    o_ref[sub] = total[:, cols]
                    ob_ref[sub] = total[:, cols].astype(BF16)
            else:
                o_ref[...] = total
                ob_ref[...] = total.astype(BF16)

        if nk == 1:
            emit(product())
            return

        @pl.when(k == 0)
        def _():
            acc[...] = product()

        @pl.when((k > 0) & (k < nk - 1))
        def _():
            acc[...] += product()

        @pl.when(k == nk - 1)
        def _():
            emit(acc[...] + product())

    if blocked_out:
        shape = (n // FF_BLOCK, m, FF_BLOCK)
        out_spec = pl.BlockSpec((nsub, tm, FF_BLOCK), lambda i, j, k: (j, i, 0))
    else:
        shape = (m, n)
        out_spec = pl.BlockSpec((tm, tn), lambda i, j, k: (i, j))
    return pl.pallas_call(
        _ordered_after(body, 2, after), name=name, grid=(m // tm, n // tn, nk),
        in_specs=[pl.BlockSpec((tk, tm), lambda i, j, k: (k, i)), pl.BlockSpec((tk, tn), lambda i, j, k: (k, j))]
        + [pl.BlockSpec(memory_space=pl.ANY)] * len(after),
        out_specs=[out_spec, out_spec],
        out_shape=[jax.ShapeDtypeStruct(shape, F32), jax.ShapeDtypeStruct(shape, BF16)],
        scratch_shapes=[pltpu.VMEM((tm, tn), F32)],
        compiler_params=_params(3),
    )(a, b, *after)


def _adamw_math(w, g, m, v):
    m = ADAM_B1 * m + (1.0 - ADAM_B1) * g
    v = ADAM_B2 * v + (1.0 - ADAM_B2) * (g * g)
    m_hat = m / (1.0 - ADAM_B1 ** ADAM_STEP)
    v_hat = v / (1.0 - ADAM_B2 ** ADAM_STEP)
    delta = -ADAM_LR * (m_hat / (jnp.sqrt(v_hat) + ADAM_EPS) + ADAM_WD * w)
    return delta, m, v


def _sum_chunks_body(own0_ref, own1_ref, r0_ref, r1_ref):
    layer0 = pl.program_id(0) == 0
    g = jnp.where(layer0, own0_ref[...], own1_ref[...])
    for k in range(N_DEV - 1):
        g = g + jnp.where(layer0, r0_ref[k], r1_ref[k]).astype(F32)
    return g


def _chunk_specs(t, cols):
    rows_of = lambda layer: (lambda l, i: jnp.where(l == layer, i, 0))
    blk = pl.BlockSpec((None, t, cols), lambda l, i, me: (l, i, 0))
    own = [pl.BlockSpec((None, t, cols), functools.partial(lambda l, i, me, pick: (me[0], pick(l, i), 0), pick=rows_of(ly)))
           for ly in range(2)]
    recv = [pl.BlockSpec((N_DEV - 1, t, cols), functools.partial(lambda l, i, me, pick: (0, pick(l, i), 0), pick=rows_of(ly)))
            for ly in range(2)]
    return blk, own + recv


def _sum_chunks(chunks, me, name):
    _, rows, cols = chunks[0].shape
    t = _row_tile(rows, 320)

    def body(me_ref, own0_ref, own1_ref, r0_ref, r1_ref, g_ref):
        g_ref[...] = _sum_chunks_body(own0_ref, own1_ref, r0_ref, r1_ref)

    blk, chunk_specs = _chunk_specs(t, cols)
    return pl.pallas_call(
        body, name=name,
        grid_spec=pltpu.PrefetchScalarGridSpec(num_scalar_prefetch=1, grid=(2, rows // t), in_specs=chunk_specs,
                                               out_specs=blk),
        out_shape=jax.ShapeDtypeStruct((2, rows, cols), F32), compiler_params=_params(2),
    )(me, *chunks)


def _adamw_sharded(w, m, v, grad, me, name):
    _, rows, cols = w.shape
    t = _row_tile(rows, 256)
    summed = not isinstance(grad, tuple)
    grad = (grad,) if summed else grad

    def body(me_ref, w_ref, m_ref, v_ref, *refs):
        g_ref, d_ref, nm_ref, nv_ref = refs[-4:]
        g = refs[0][...] if summed else _sum_chunks_body(*refs[:4])
        g_ref[...] = g
        d_ref[...], nm_ref[...], nv_ref[...] = _adamw_math(w_ref[...], g, m_ref[...], v_ref[...])

    blk, chunk_specs = _chunk_specs(t, cols)
    return pl.pallas_call(
        body, name=name,
        grid_spec=pltpu.PrefetchScalarGridSpec(
            num_scalar_prefetch=1, grid=(2, rows // t),
            in_specs=[blk, blk, blk] + ([blk] if summed else chunk_specs), out_specs=[blk] * 4),
        out_shape=[jax.ShapeDtypeStruct(w.shape, F32)] * 4,
        compiler_params=_params(2),
    )(me, w, m, v, *grad)


def _adamw_packed(w, g8, m, v, name):
    def body(w_ref, g_ref, m_ref, v_ref, go_ref, d_ref, nm_ref, nv_ref):
        g = g_ref[0]
        for dev in range(1, N_DEV):
            g = g + g_ref[dev]
        go_ref[...] = g
        d_ref[...], nm_ref[...], nv_ref[...] = _adamw_math(w_ref[...], g, m_ref[...], v_ref[...])

    return pl.pallas_call(
        body, name=name, out_shape=[jax.ShapeDtypeStruct(w.shape, F32)] * 4,
        compiler_params=pltpu.CompilerParams(vmem_limit_bytes=VMEM_LIMIT),
    )(w, g8, m, v)


def _peer(k):
    x, y, c = lax.axis_index("x"), lax.axis_index("y"), lax.axis_index("c")
    return (1 - x if k & 4 else x, 1 - y if k & 2 else y, 1 - c if k & 1 else c)


def _linear(dev):
    return 4 * dev[0] + 2 * dev[1] + dev[2]


HBM_SPEC = pl.BlockSpec(memory_space=pltpu.HBM)
SEM_SPEC = pl.BlockSpec(memory_space=pltpu.SEMAPHORE)
ANY_SPEC = pl.BlockSpec(memory_space=pl.ANY)
EFFECT = pltpu.SideEffectType.DATAFLOW_SIDE_EFFECTING


def _in_hbm(a):
    return pltpu.with_memory_space_constraint(a, pltpu.HBM)


class _Exchange:
    def __init__(self, name, groups, scatter, after=()):
        self.name, self.scatter = name, scatter
        self.sizes = sizes = [len(g) for g in groups]
        srcs = [a for g in groups for a in g]
        n, ng = len(srcs), len(groups)
        lead = (N_DEV - 1,) if scatter else (N_DEV,)
        shapes = [lead + (a.shape[1:] if scatter else a.shape) for a in srcs]
        lands = [lax.empty(sh, a.dtype) for sh, a in zip(shapes, srcs)]
        offsets = [sum(sizes[:gi]) for gi in range(ng)]
        copy = self._copy

        def body(*refs):
            src, land = refs[:n], refs[n:2 * n]
            sems = refs[2 * n + len(after):2 * n + len(after) + 2 * ng]
            token = refs[-1]
            for gi in range(ng):
                for wi in range(sizes[gi]):
                    w = offsets[gi] + wi
                    for k in range(1, N_DEV):
                        copy(src[w], land[w], sems[2 * gi], sems[2 * gi + 1], wi, k).start()
            token[...] = jnp.zeros_like(token)

        sem_shapes = [pltpu.SemaphoreType.DMA(((N_DEV - 1) * sz,)) for sz in sizes for _ in range(2)]
        outs = pl.pallas_call(
            body, name=name + "_start",
            in_specs=[HBM_SPEC] * (2 * n) + [ANY_SPEC] * len(after),
            out_specs=[SEM_SPEC] * (2 * ng) + [HBM_SPEC] * (2 * n) + [pl.BlockSpec(memory_space=pltpu.VMEM)],
            out_shape=sem_shapes + [pltpu.HBM(a.shape, a.dtype) for a in srcs + lands]
            + [jax.ShapeDtypeStruct((8, LANES), F32)],
            input_output_aliases={i: 2 * ng + i for i in range(2 * n)},
            compiler_params=pltpu.CompilerParams(has_side_effects=EFFECT),
        )(*[_in_hbm(a) for a in srcs + lands], *after)
        self.sems = [outs[2 * gi:2 * gi + 2] for gi in range(ng)]
        thru = outs[2 * ng:2 * ng + 2 * n]
        self.srcs = [thru[offsets[gi]:offsets[gi] + sizes[gi]] for gi in range(ng)]
        self.lands = [thru[n + offsets[gi]:n + offsets[gi] + sizes[gi]] for gi in range(ng)]
        self.token = outs[-1]

    def _copy(self, src, land, send_sems, recv_sems, wi, k):
        to = _peer(k)
        if self.scatter:
            src_ref, dst_ref = src.at[_linear(to)], land.at[k - 1]
        else:
            src_ref, dst_ref = src, land.at[_linear(_peer(0))]
        return pltpu.make_async_remote_copy(
            src_ref=src_ref, dst_ref=dst_ref, send_sem=send_sems.at[(N_DEV - 1) * wi + k - 1],
            recv_sem=recv_sems.at[(N_DEV - 1) * wi + k - 1], device_id=to, device_id_type=MESH)

    def wait(self, gi, after):
        n = self.sizes[gi]
        copy = self._copy

        def body(*refs):
            src, land = refs[:n], refs[n:2 * n]
            send_sems, recv_sems = refs[2 * n], refs[2 * n + 1]
            for wi in range(n):
                for k in range(1, N_DEV):
                    cp = copy(src[wi], land[wi], send_sems, recv_sems, wi, k)
                    cp.wait_send()
                    cp.wait_recv()

        arrays = list(self.srcs[gi]) + list(self.lands[gi])
        outs = pl.pallas_call(
            body, name=f"{self.name}_wait{gi}",
            in_specs=[HBM_SPEC] * (2 * n) + [SEM_SPEC, SEM_SPEC] + [ANY_SPEC] * len(after),
            out_specs=[HBM_SPEC] * (2 * n),
            out_shape=[pltpu.HBM(a.shape, a.dtype) for a in arrays],
            input_output_aliases={i: i for i in range(2 * n)},
            compiler_params=pltpu.CompilerParams(has_side_effects=EFFECT),
        )(*arrays, *self.sems[gi], *after)
        return outs[:n], outs[n:]


def _rotary_tables(positions):
    rot_dim = HEAD_DIM // 4
    inv_freq = ROPE_THETA ** (-jnp.arange(0, rot_dim, 2, dtype=F32) / rot_dim)
    ang = positions.astype(F32)[:, None] * inv_freq
    cs = jnp.concatenate([jnp.cos(ang), jnp.sin(ang)], axis=1)
    dim = jnp.arange(LANES) % HEAD_DIM
    first, second = dim < ROT_SHIFT, (dim >= ROT_SHIFT) & (dim < rot_dim)
    src = jnp.arange(2 * ROT_SHIFT)[:, None]
    angle = (dim % ROT_SHIFT)[None, :]
    c = jnp.where((first | second)[None, :] & (src == angle), 1.0, 0.0)
    sa = jnp.where(second[None, :] & (src == angle + ROT_SHIFT), 1.0, 0.0)
    sb = jnp.where(first[None, :] & (src == angle + ROT_SHIFT), -1.0, 0.0)
    spread = jnp.concatenate([c, sa, sb], axis=1).astype(F32)
    base = jnp.concatenate([jnp.where(first | second, 0.0, 1.0), jnp.zeros((2 * LANES,))]).astype(F32)[None, :]
    return jnp.dot(cs, spread, precision=lax.Precision.HIGHEST, preferred_element_type=F32) + base


def _block_diag(pool_w):
    gc = pool_w.shape[-1]
    out = jnp.zeros((POOL_WIDTH, POOL_WIDTH), pool_w.dtype)
    for grp in range(pool_w.shape[0]):
        out = lax.dynamic_update_slice(out, pool_w[grp], (grp * gc, grp * gc))
    return out


def _diag_blocks(a):
    gc = POOL_WIDTH // len(POOL_WINDOWS)
    return jnp.stack([a[grp * gc:(grp + 1) * gc, grp * gc:(grp + 1) * gc] for grp in range(len(POOL_WINDOWS))])


def _local_step(x, p, positions, loss_target, norm1, pool_w, pool_scale, norm2, norm3, final_norm, weights, send):
    rc = rsa = rsb = _rotary_tables(positions)
    ones_bd = _block_diag(jnp.ones((4, HEAD_DIM, HEAD_DIM), BF16))
    saved = []
    h = x
    for i in range(2):
        tag = f"_l{i}"
        g1, g2, g3 = norm1[i:i + 1], norm2[i:i + 1], norm3[i:i + 1]
        w_bd = _block_diag(pool_w[i]).astype(BF16)
        scale = pool_scale[i:i + 1]
        if i == 0:
            w_in = weights(i, "in", (h, rc, w_bd))
            hn1, u, *qkv = _normproj_fwd(h, g1, w_in, rc, rsa, rsb, "normproj_fwd" + tag)
        else:
            w_in, (hn1, u, *qkv) = ahead
        qkv = [qkv[3 * grp:3 * grp + 3] for grp in range(3)]
        started = weights(i, "prefetch", (hn1,))
        o, lse = zip(*[_attn_fwd(*qkv[grp], f"attn_fwd{tag}_g{grp}", after=started) for grp in range(3)])
        w_out = weights(i, "out", o)
        h1, a, y = _outproj_fwd(h, u, w_bd, scale, o, lse, w_out, "outproj_fwd" + tag)
        w_up, w_down = weights(i, "mlp", (h1,))
        h2, hn2, r = _mlp_fwd(h1, g2, w_up, w_down, "mlp_fwd" + tag)
        w_gate, w_ple = weights(i, "gate", (h2,))
        h0 = h
        if i == 0:
            w_in_next = weights(1, "in", (h2,))
            h, hn3, gate, pb, *ahead = _gate_fwd(h2, g3, w_gate, p, i, w_ple, "gate_normproj_fwd",
                                                 follow=(norm1[1:2], w_in_next, rc, rsa, rsb))
            ahead = (w_in_next, ahead)
        else:
            hn3, gate, pb, loss, dh, d_final = _gate_fwd(h2, g3, w_gate, p, i, w_ple, "gate_fwd" + tag,
                                                         head=(final_norm.reshape(1, D_MODEL), loss_target))
        saved.append(dict(h0=h0, hn1=hn1, qkv=qkv, y=y, o=o, lse=lse, a=a, h1=h1, hn2=hn2, r=r, h2=h2,
                          hn3=hn3, gate=gate, pb=pb, w_bd=w_bd, scale=scale, g1=g1, g2=g2, g3=g3,
                          w_in=w_in, w_out=w_out, w_up=w_up, w_down=w_down, w_gate=w_gate, w_ple=w_ple))

    grads = [None, None]
    sent = ()
    for i in (1, 0):
        tag = f"_l{i}"
        sv = saved[i]
        gate_operands = lambda v: (v["gate"], v["pb"], v["w_ple"], v["h2"], v["g3"], v["w_gate"], v["hn3"])
        if i == 1:
            dh2, dg3, dw_gate, dw_ple = _gate_bwd(dh, *gate_operands(sv), "gate_bwd" + tag)
        else:
            dh2, dg3, dw_gate, dw_ple = below
        dh1, dup, dg2, dh2b = _mlp_bwd(dh2, sv["r"], sv["h1"], sv["g2"], sv["w_up"], sv["w_down"], "mlp_bwd" + tag,
                                       after=sent)
        dw_down = _matmul_tn(sv["r"], dh2b, "dw_down" + tag, square_a=True)
        dw_up = _matmul_tn(sv["hn2"], dup, "dw_up" + tag, blocked_out=True)
        dpool, do0, do1, do2, de0, de1, de2, dw_out = _outproj_bwd(dh1, sv["w_out"], sv["o"], sv["lse"], ones_bd,
                                                                   sv["a"], "outproj_bwd" + tag)
        sent = send(i, "main", dict(w_gate=dw_gate, w_ple=dw_ple, w_down=dw_down, w_up=dw_up, w_out=dw_out))
        dqkv = [_attn_bwd(*sv["qkv"][grp], do_g, sv["lse"][grp], de_g, f"attn_bwd{tag}_g{grp}", after=sent)
                for grp, (do_g, de_g) in enumerate(((do0, de0), (do1, de1), (do2, de2)))]
        dq, dk, dv = zip(*dqkv)
        du, dw_bd, dscale = _pool_bwd(dpool, sv["y"], sv["w_bd"], sv["scale"], "pool_bwd" + tag, after=sent)
        if i == 1:
            dh2_b, dg3_b, dwg, dwgb, dwp, dwpb, dz, dg1 = _normproj_bwd(
                dh1, du, dq, dk, dv, rc, rsa, rsb, sv["w_in"], sv["h0"], sv["g1"], "normproj_gate_bwd",
                follow=gate_operands(saved[0]))
            below = (dh2_b, dg3_b, (dwg, dwgb), (dwp, dwpb))
        else:
            dh, dz, dg1 = _normproj_bwd(dh1, du, dq, dk, dv, rc, rsa, rsb, sv["w_in"], sv["h0"], sv["g1"],
                                        "normproj_bwd" + tag)
        grads[i] = dict(norm1=dg1, norm2=dg2, norm3=dg3, pool_w=_diag_blocks(dw_bd), pool_scale=dscale)
        small_sent = send(0, "small", (grads, d_final, loss)) if i == 0 else ()
        dw_in = _matmul_tn(dz, sv["hn1"], "dw_in" + tag, tm=N_IN // 2, after=small_sent)
        sent = send(i, "in", dict(w_in=dw_in))
    return dh, sent


def _pack_small(norm1, norm2, norm3, final_norm, pool_scale, pool_w, spare=None):
    spare = jnp.zeros((1, LANES), F32) if spare is None else spare
    scale_row = jnp.concatenate([pool_scale.reshape(1, 2 * POOL_WIDTH), spare,
                                 jnp.zeros((1, D_MODEL - 2 * POOL_WIDTH - LANES), F32)], axis=1)
    return jnp.concatenate([norm1, norm2, norm3, final_norm.reshape(1, D_MODEL), scale_row,
                            pool_w.reshape(32, D_MODEL)], axis=0)


def _unpack_small(a):
    return dict(norm1=a[0:2], norm2=a[2:4], norm3=a[4:6], final_norm=a[6], pool_scale=a[7, 0:2 * POOL_WIDTH].reshape(2, POOL_WIDTH),
                pool_w=a[8:40].reshape(2, 4, HEAD_DIM, HEAD_DIM))


def _chunks_cols(a, cols):
    return a.reshape(a.shape[0], N_DEV, cols).transpose(1, 0, 2)


def _chunks_rows(a, rows):
    return a.reshape(N_DEV, rows, a.shape[1])


BIG = ("w_in", "w_out", "w_up", "w_down", "w_gate", "w_ple")
SMALL = ("norm1", "norm2", "norm3", "final_norm", "pool_scale", "pool_w")
ORDER = ("norm1", "w_in", "pool_w", "pool_scale", "w_out", "norm2", "w_up", "w_down", "norm3", "w_gate", "w_ple",
         "final_norm")


def kernel(x, p, positions, norm1, w_in, pool_w, pool_scale, w_out, norm2, w_up, w_down, norm3, w_gate, w_ple, final_norm, loss_target, m_norm1, m_w_in, m_pool_w, m_pool_scale, m_w_out, m_norm2, m_w_up, m_w_down, m_norm3, m_w_gate, m_w_ple, m_final_norm, v_norm1, v_w_in, v_pool_w, v_pool_scale, v_w_out, v_norm2, v_w_up, v_w_down, v_norm3, v_w_gate, v_w_ple, v_final_norm):
    w = dict(norm1=norm1, w_in=w_in, pool_w=pool_w, pool_scale=pool_scale, w_out=w_out, norm2=norm2, w_up=w_up,
             w_down=w_down, norm3=norm3, w_gate=w_gate, w_ple=w_ple, final_norm=final_norm)
    m = dict(norm1=m_norm1, w_in=m_w_in, pool_w=m_pool_w, pool_scale=m_pool_scale, w_out=m_w_out, norm2=m_norm2,
             w_up=m_w_up, w_down=m_w_down, norm3=m_norm3, w_gate=m_w_gate, w_ple=m_w_ple, final_norm=m_final_norm)
    v = dict(norm1=v_norm1, w_in=v_w_in, pool_w=v_pool_w, pool_scale=v_pool_scale, w_out=v_w_out, norm2=v_norm2,
             w_up=v_w_up, w_down=v_w_down, norm3=v_norm3, w_gate=v_w_gate, w_ple=v_w_ple, final_norm=v_final_norm)
    seq = x.shape[1]

    bf = {n: [w[n][layer].astype(BF16) for layer in range(2)] for n in BIG}
    bf["w_in"] = [a.T for a in bf["w_in"]]
    me = 4 * lax.axis_index("x") + 2 * lax.axis_index("y") + lax.axis_index("c")
    parts = dict(zip(("in", "out", "mlp", "gate"), (("w_in",), ("w_out",), ("w_up", "w_down"), ("w_gate", "w_ple"))))
    first = _Exchange("gather_first", [[bf["w_in"][0]]], scatter=False)
    later = [pt for pt in parts if pt != "in"]
    gathers = [_Exchange("gather_l0", [[bf[n][0] for n in parts[pt]] for pt in later], scatter=False,
                         after=(first.token,))]
    unpack = dict(w_in=lambda a: a.reshape(N_IN, D_MODEL),
                  w_out=lambda a: a.reshape(D_MODEL, D_MODEL), w_gate=lambda a: a.reshape(D_MODEL, D_MODEL),
                  w_ple=lambda a: a.transpose(1, 0, 2).reshape(PLE_DIM, D_MODEL), w_up=lambda a: a, w_down=lambda a: a)

    def weights(layer, part, after):
        if part == "prefetch":
            if layer != 0:
                return ()
            gathers.append(_Exchange("gather_l1", [[bf[n][1] for n in parts[pt]] for pt in parts], scatter=False,
                                     after=after))
            return (gathers[1].token,)
        if layer == 0 and part == "in":
            shards, lands = first.wait(0, (*after, gathers[0].token))
        elif layer == 0:
            shards, lands = gathers[0].wait(later.index(part), after)
        else:
            shards, lands = gathers[1].wait(tuple(parts).index(part), after)
        full = [unpack[n](lax.dynamic_update_slice_in_dim(land, shard[None], me, axis=0))
                for n, shard, land in zip(parts[part], shards, lands)]
        return full if len(full) > 1 else full[0]

    to_chunks = dict(w_in=lambda a: _chunks_rows(a, N_IN // N_DEV),
                     w_out=lambda a: _chunks_rows(a, D_MODEL // N_DEV),
                     w_up=lambda a: a, w_down=lambda a: _chunks_rows(a, FF_BLOCK),
                     w_gate=lambda a: _chunks_rows(a, D_MODEL // N_DEV), w_ple=lambda a: _chunks_cols(a, D_MODEL // N_DEV))
    own = {n: [None, None] for n in BIG}
    scatters = {}

    def send(layer, part, grads):
        if part == "small":
            per_layer, d_final, loss = grads
            pack = _pack_small(
                *[jnp.concatenate([per_layer[0][n], per_layer[1][n]], axis=0) for n in ("norm1", "norm2", "norm3")],
                d_final.reshape(D_MODEL),
                jnp.concatenate([per_layer[0]["pool_scale"], per_layer[1]["pool_scale"]], axis=0),
                jnp.stack([per_layer[0]["pool_w"], per_layer[1]["pool_w"]]), spare=loss)
            scatters["small"] = _Exchange("gather_small", [[pack]], scatter=False)
            return (scatters["small"].token,)
        for n, (g32, _) in grads.items():
            own[n][layer] = to_chunks[n](g32)
        ex = _Exchange(f"scatter_{part}_l{layer}", [[to_chunks[n](g16) for n, (_, g16) in grads.items()]], scatter=True)
        scatters[layer, part] = (tuple(grads), ex)
        return (ex.token,)

    dx, sent = _local_step(
        x.reshape(seq, D_MODEL), p.reshape(2, seq, PLE_DIM), positions.reshape(seq), loss_target.reshape(seq, D_MODEL),
        norm1, pool_w, pool_scale, norm2, norm3, final_norm, weights, send)

    g_out, d_out, m_out, v_out = {}, {}, {}, {}
    my_index = me.reshape(1)
    for part in ("main", "in"):
        recv = {}
        for layer in (1, 0):
            names, ex = scatters[layer, part]
            for n, r in zip(names, ex.wait(0, sent)[1]):
                recv[n, layer] = r
        for n in names:
            grad = (*own[n], recv[n, 0], recv[n, 1])
            if n == "w_in":
                grad = _sum_chunks(grad, my_index, "sum_w_in").transpose(0, 2, 1)
            g_out[n], d_out[n], m_out[n], v_out[n] = _adamw_sharded(w[n], m[n], v[n], grad, my_index, "adamw_" + n)
        sent = tuple(d_out[n] for n in names)
    (mine,), (landed,) = scatters["small"].wait(0, sent)
    small_g8 = lax.dynamic_update_slice_in_dim(landed, mine[None], me, axis=0)
    pack = lambda t: _pack_small(*[t[n] for n in SMALL])
    small_g, d_small, m_small, v_small = _adamw_packed(pack(w), small_g8, pack(m), pack(v), "adamw_small")
    for dst, a in ((g_out, small_g), (d_out, d_small), (m_out, m_small), (v_out, v_small)):
        dst.update(_unpack_small(a))

    return (small_g[7, 2 * POOL_WIDTH],dx.reshape(1, seq, D_MODEL), *[g_out[n] for n in ORDER], *[d_out[n] for n in ORDER],
            *[m_out[n] for n in ORDER], *[v_out[n] for n in ORDER])
```

```python
import functools

import jax
import jax.numpy as jnp
from jax import lax
from jax.experimental import pallas as pl
from jax.experimental.pallas import tpu as pltpu

F32 = jnp.float32
BF16 = jnp.bfloat16

D_MODEL = 1024
HEAD_DIM = 64
POOL_WIDTH = 256
POOL_WINDOWS = (2, 4, 8, 16)
POOL_HALO = 16
POOL_PAD = 8
GROUP_WIDTH = 256
DILATIONS = (1, 4, 16)
ATTN_BLOCK = 128
ROT_SHIFT = 8
ROPE_THETA = 500000.0
D_FF = 4096
FF_BLOCK = 512
FF_PER_STEP = 2
MLP_BWD_TILE = 512
FWD_TILE = 1024
N_DEV = 8
N_IN = POOL_WIDTH + 3 * 768
PLE_DIM = 256
EPS = 1e-6
NEG_BIG = -1e30

ADAM_LR = 0.001
ADAM_B1 = 0.9
ADAM_B2 = 0.999
ADAM_EPS = 1e-08
ADAM_WD = 0.01
ADAM_STEP = 10

LANES = 128
VMEM_LIMIT = 56 * 1024 * 1024
MESH = pl.DeviceIdType.MESH


def _params(n_grid):
    return pltpu.CompilerParams(dimension_semantics=("arbitrary",) * n_grid, vmem_limit_bytes=VMEM_LIMIT)


def _dot(a, b):
    return jnp.dot(a, b, preferred_element_type=F32)


def _dot_nt(a, b):
    return lax.dot_general(a, b, (((1,), (1,)), ((), ())), preferred_element_type=F32)


def _dot_tn(a, b):
    return lax.dot_general(a, b, (((0,), (0,)), ((), ())), preferred_element_type=F32)


def _rms(x, g):
    rstd = lax.rsqrt(jnp.mean(x * x, axis=-1, keepdims=True) + EPS)
    n = x * rstd
    return n, rstd, n * g


def _rms_bwd(dy, n, rstd, g):
    dyn = dy * g
    dx = rstd * (dyn - n * jnp.mean(dyn * n, axis=-1, keepdims=True))
    return dx, jnp.sum(dy * n, axis=0, keepdims=True)


def _ordered_after(body, n_in, after):
    if not after:
        return body
    return lambda *refs: body(*refs[:n_in], *refs[n_in + len(after):])


def _resident(shape):
    return pl.BlockSpec(shape, lambda i: (0,) * len(shape), pipeline_mode=pl.Buffered(1))


def _row_tile(s, t):
    t = min(s, t)
    assert s % t == 0
    return t


def _rot(z, c, sa, sb):
    return z * c + pltpu.roll(z, ROT_SHIFT, 1) * sa + pltpu.roll(z, LANES - ROT_SHIFT, 1) * sb


def _table_specs(t):
    return [pl.BlockSpec((t, LANES), functools.partial(lambda i, k: (i, k), k=k)) for k in range(3)]


def _rot_t(dz, c, sa, sb):
    return dz * c + pltpu.roll(dz * sa, LANES - ROT_SHIFT, 1) + pltpu.roll(dz * sb, ROT_SHIFT, 1)


def _to_residues(value, stage, out_ref, dil):
    if dil == 1:
        out_ref[0] = value.astype(out_ref.dtype)
        return
    rows = value.shape[0] // dil
    for hf in range(GROUP_WIDTH // LANES):
        lanes = slice(hf * LANES, (hf + 1) * LANES)
        stage[hf][...] = value[:, lanes]
        for r in range(dil):
            out_ref[r, :, lanes] = stage[hf][pl.ds(r, rows, stride=dil), :].astype(out_ref.dtype)


def _from_residues(in_ref, stage, dil):
    if dil == 1:
        return in_ref[0].astype(F32)
    rows = in_ref.shape[1]
    for hf in range(GROUP_WIDTH // LANES):
        for r in range(dil):
            stage[hf][pl.ds(r, rows, stride=dil), :] = in_ref[r, :, hf * LANES:(hf + 1) * LANES].astype(F32)
    return jnp.concatenate([stage[0][...], stage[1][...]], axis=1)


def _residue_spec(dil, t):
    return pl.BlockSpec((dil, t // dil, GROUP_WIDTH), lambda i: (0, i, 0))


def _residue_shape(dil, s, dtype):
    return jax.ShapeDtypeStruct((dil, s // dil, GROUP_WIDTH), dtype)


def _stages(t, n):
    return [pltpu.VMEM((t, LANES), F32)] * (n * (GROUP_WIDTH // LANES))


def _pair_stages(refs):
    return [refs[i:i + 2] for i in range(0, len(refs), 2)]


def _normproj_tile(x, g_ref, w_ref, c_ref, sa_ref, sb_ref, hn_ref, u_ref, *rest):
    qkv_refs, stages = rest[:9], _pair_stages(rest[9:])
    _, _, hn = _rms(x, g_ref[...])
    hb = hn.astype(BF16)
    hn_ref[...] = hb
    c, sa, sb = c_ref[...], sa_ref[...], sb_ref[...]

    def rot(z, scale):
        halves = [_rot(z[:, hf * LANES:(hf + 1) * LANES], c, sa, sb) * scale for hf in range(2)]
        return jnp.concatenate(halves, axis=1)

    proj = lambda lo: _dot_nt(hb, w_ref[lo:lo + GROUP_WIDTH, :])
    u_ref[...] = proj(0)
    for grp, dil in enumerate(DILATIONS):
        lo = POOL_WIDTH + grp * GROUP_WIDTH
        q_ref, k_ref, v_ref = qkv_refs[3 * grp:3 * grp + 3]
        _to_residues(rot(proj(lo), HEAD_DIM ** -0.5), stages[0], q_ref, dil)
        _to_residues(rot(proj(lo + 768), 1.0), stages[1], k_ref, dil)
        _to_residues(proj(lo + 1536), stages[2], v_ref, dil)


def _normproj_operands(s, t):
    row = lambda w: pl.BlockSpec((t, w), lambda i: (i, 0))
    in_specs = [pl.BlockSpec((1, D_MODEL), lambda i: (0, 0)), _resident((N_IN, D_MODEL))] + _table_specs(t)
    out_specs = [row(D_MODEL), row(POOL_WIDTH)] + [_residue_spec(dil, t) for dil in DILATIONS for _ in range(3)]
    out_shape = [jax.ShapeDtypeStruct((s, D_MODEL), BF16), jax.ShapeDtypeStruct((s, POOL_WIDTH), F32)]
    out_shape += [_residue_shape(dil, s, BF16) for dil in DILATIONS for _ in range(3)]
    return in_specs, out_specs, out_shape, _stages(t, 3)


def _normproj_fwd(h, g, w_in, rc, rsa, rsb, name):
    s = h.shape[0]
    t = _row_tile(s, FWD_TILE)

    def body(h_ref, *refs):
        _normproj_tile(h_ref[...], *refs)

    in_specs, out_specs, out_shape, scratch = _normproj_operands(s, t)
    return pl.pallas_call(
        body, name=name, grid=(s // t,), in_specs=[pl.BlockSpec((t, D_MODEL), lambda i: (i, 0))] + in_specs,
        out_specs=out_specs, out_shape=out_shape, scratch_shapes=scratch, compiler_params=_params(1),
    )(h, g, w_in, rc, rsa, rsb)


def _pool_lane_window():
    lane = lax.broadcasted_iota(jnp.int32, (1, POOL_WIDTH), 1)
    return jnp.left_shift(2, lane // (POOL_WIDTH // len(POOL_WINDOWS)))


def _window_sums(ext, b2, b4, b8, t, lo, tile, direction):
    rows = t + POOL_HALO
    for src, dst, sh in ((ext, b2, 1), (b2, b4, 2), (b4, b8, 4)):
        dst[lo:lo + rows, :] = src[lo:lo + rows, :] + src[lo + direction * sh:lo + direction * sh + rows, :]
    s16 = b8[tile:tile + t, :] + b8[tile + direction * 8:tile + direction * 8 + t, :]
    win = _pool_lane_window()
    return jnp.where(win == 2, b2[tile:tile + t, :],
                     jnp.where(win == 4, b4[tile:tile + t, :], jnp.where(win == 8, b8[tile:tile + t, :], s16)))


def _pool_fwd_tile(i, u_ref, w_ref, sc_ref, y_ref, ext, b2, b4, b8):
    t = u_ref.shape[0]
    first = POOL_PAD + POOL_HALO

    @pl.when(i == 0)
    def _():
        for buf in (ext, b2, b4):
            buf[0:POOL_PAD, :] = jnp.zeros((POOL_PAD, POOL_WIDTH), F32)
        ext[POOL_PAD:first, :] = jnp.zeros((POOL_HALO, POOL_WIDTH), F32)

    x = u_ref[...]
    ext[first:, :] = x
    wsum = _window_sums(ext, b2, b4, b8, t, POOL_PAD, first, -1)
    pos = i * t + lax.broadcasted_iota(jnp.int32, (t, POOL_WIDTH), 0)
    cnt = jnp.minimum(pos + 1, _pool_lane_window()).astype(F32)
    yb = (wsum / cnt - x).astype(BF16)
    y_ref[...] = yb
    ext[POOL_PAD:first, :] = x[t - POOL_HALO:, :]
    return _dot(yb, w_ref[...]) * sc_ref[...]


def _head_masks():
    lane = lax.broadcasted_iota(jnp.int32, (ATTN_BLOCK, GROUP_WIDTH), 1)
    return [lane // HEAD_DIM == hd for hd in range(GROUP_WIDTH // HEAD_DIM)]


def _stack_heads(a, masks):
    zero = jnp.zeros_like(a)
    return jnp.concatenate([jnp.where(m, a, zero) for m in masks], axis=0)


def _band_bias(first_step):
    rows = ATTN_BLOCK * (GROUP_WIDTH // HEAD_DIM)
    i = lax.broadcasted_iota(jnp.int32, (rows, 2 * ATTN_BLOCK), 0) & (ATTN_BLOCK - 1)
    j = lax.broadcasted_iota(jnp.int32, (rows, 2 * ATTN_BLOCK), 1)
    inner = jnp.where((j >= i) & (j <= i + ATTN_BLOCK), 0.0, NEG_BIG)
    return jnp.where((j < ATTN_BLOCK) & first_step, NEG_BIG, inner), inner


def _column_per_head(a):
    return jnp.concatenate([a[:, hd * HEAD_DIM:hd * HEAD_DIM + 1] for hd in range(GROUP_WIDTH // HEAD_DIM)], axis=0)


def _blocks_per_step(nb):
    if nb <= 16:
        return nb
    return next(qb for qb in (16, 8, 4, 2, 1) if nb % qb == 0)


def _residues_per_step(dil, nb, qb):
    return 2 if (nb == qb and qb < 8 and dil % 2 == 0) else 1


def _attn_fwd(q, k, v, name, after=()):
    dil, length, _ = q.shape
    nb = length // ATTN_BLOCK
    qb = _blocks_per_step(nb)
    rs = _residues_per_step(dil, nb, qb)

    def body(q_ref, kp_ref, kc_ref, vp_ref, vc_ref, o_ref, lse_ref):
        masks = _head_masks()
        bias = _band_bias(pl.program_id(1) == 0)
        for rr in range(rs):
            for qi in range(qb):
                here = slice(qi * ATTN_BLOCK, (qi + 1) * ATTN_BLOCK)
                before = slice((qi - 1) * ATTN_BLOCK, qi * ATTN_BLOCK)
                kcat = jnp.concatenate([kp_ref[rr] if qi == 0 else kc_ref[rr, before], kc_ref[rr, here]], axis=0)
                vcat = jnp.concatenate([vp_ref[rr] if qi == 0 else vc_ref[rr, before], vc_ref[rr, here]], axis=0)
                qs = _stack_heads(q_ref[rr, here], masks)
                sc = _dot_nt(qs, kcat) + bias[min(qi, 1)]
                m = jnp.max(sc, axis=1, keepdims=True)
                e = jnp.exp(sc - m)
                l = jnp.sum(e, axis=1, keepdims=True)
                p = (e / l).astype(BF16)
                lse = m + jnp.log(l)
                o = jnp.zeros((ATTN_BLOCK, GROUP_WIDTH), F32)
                lse_full = jnp.zeros((ATTN_BLOCK, GROUP_WIDTH), F32)
                for hd, msk in enumerate(masks):
                    rows = slice(hd * ATTN_BLOCK, (hd + 1) * ATTN_BLOCK)
                    o = jnp.where(msk, _dot(p[rows], vcat), o)
                    lse_full = jnp.where(msk, lse[rows], lse_full)
                o_ref[rr, here] = o.astype(o_ref.dtype)
                lse_ref[rr, here] = lse_full

    cur = pl.BlockSpec((rs, qb * ATTN_BLOCK, GROUP_WIDTH), lambda r, j: (r, j, 0))
    prev = pl.BlockSpec((rs, ATTN_BLOCK, GROUP_WIDTH), lambda r, j: (r, jnp.maximum(qb * j - 1, 0), 0))
    return pl.pallas_call(
        _ordered_after(body, 5, after), name=name, grid=(dil // rs, nb // qb),
        in_specs=[cur, prev, cur, prev, cur] + [pl.BlockSpec(memory_space=pl.ANY)] * len(after), out_specs=[cur, cur],
        out_shape=[jax.ShapeDtypeStruct(q.shape, BF16), jax.ShapeDtypeStruct(q.shape, F32)],
        compiler_params=_params(2),
    )(q, k, k, v, v, *after)


def _group_weights(l0, l1, l2):
    m = jnp.maximum(jnp.maximum(l0, l1), l2)
    e0, e1, e2 = jnp.exp(l0 - m), jnp.exp(l1 - m), jnp.exp(l2 - m)
    den = e0 + e1 + e2
    return e0 / den, e1 / den, e2 / den


def _outproj_fwd(h, u, w_bd, scale, o, lse, w_out, name):
    s = h.shape[0]
    t = _row_tile(s, FWD_TILE)

    def body(h_ref, u_ref, wbd_ref, sc_ref, o0, o1, o2, l0, l1, l2, w_ref, out_ref, a_ref, y_ref, ext, b2, b4, b8,
             *stages):
        pool_out = _pool_fwd_tile(pl.program_id(0), u_ref, wbd_ref, sc_ref, y_ref, ext, b2, b4, b8)
        stages = _pair_stages(stages)
        ov = [_from_residues(r, stages[i], DILATIONS[i]) for i, r in enumerate((o0, o1, o2))]
        lv = [_from_residues(r, stages[3 + i], DILATIONS[i]) for i, r in enumerate((l0, l1, l2))]
        wts = _group_weights(*lv)
        a = jnp.concatenate([pool_out] + [ov[i] * wts[i] for i in range(3)], axis=1).astype(BF16)
        a_ref[...] = a
        out_ref[...] = h_ref[...] + _dot(a, w_ref[...])

    row = lambda w: pl.BlockSpec((t, w), lambda i: (i, 0))
    res = [_residue_spec(dil, t) for dil in DILATIONS]
    return pl.pallas_call(
        body, name=name, grid=(s // t,),
        in_specs=[row(D_MODEL), row(POOL_WIDTH), _resident((POOL_WIDTH, POOL_WIDTH)), _resident((1, POOL_WIDTH))]
        + res + res + [_resident((D_MODEL, D_MODEL))],
        out_specs=[row(D_MODEL), row(D_MODEL), row(POOL_WIDTH)],
        out_shape=[jax.ShapeDtypeStruct((s, D_MODEL), F32), jax.ShapeDtypeStruct((s, D_MODEL), BF16),
                   jax.ShapeDtypeStruct((s, POOL_WIDTH), BF16)],
        scratch_shapes=[pltpu.VMEM((t + POOL_HALO + POOL_PAD, POOL_WIDTH), F32)] * 4 + _stages(t, 6),
        compiler_params=_params(1),
    )(h, u, w_bd, scale, *o, *lse, w_out)


def _mlp_fwd(h, g, w_up, w_down, name):
    s = h.shape[0]
    t = _row_tile(s, 512)
    nblk = D_FF // FF_BLOCK

    def body(h_ref, g_ref, wu_ref, wd_ref, out_ref, hn_ref, r_ref):
        x = h_ref[...]
        _, _, hn = _rms(x, g_ref[...])
        hb = hn.astype(BF16)
        hn_ref[...] = hb
        acc = None
        for b0 in range(0, nblk, FF_PER_STEP):
            acts = []
            for b in range(b0, b0 + FF_PER_STEP):
                r = jnp.maximum(_dot(hb, wu_ref[b]), 0.0)
                r_ref[:, b * FF_BLOCK:(b + 1) * FF_BLOCK] = r.astype(BF16)
                acts.append((r * r).astype(BF16))
            wd = wd_ref[b0:b0 + FF_PER_STEP].reshape(FF_PER_STEP * FF_BLOCK, D_MODEL)
            part = _dot(jnp.concatenate(acts, axis=1), wd)
            acc = part if acc is None else acc + part
        out_ref[...] = x + acc

    row = lambda w: pl.BlockSpec((t, w), lambda i: (i, 0))
    resident = lambda shape: pl.BlockSpec(shape, lambda i: (0, 0, 0), pipeline_mode=pl.Buffered(1))
    return pl.pallas_call(
        body, name=name, grid=(s // t,),
        in_specs=[row(D_MODEL), pl.BlockSpec((1, D_MODEL), lambda i: (0, 0)),
                  resident((nblk, D_MODEL, FF_BLOCK)), resident((nblk, FF_BLOCK, D_MODEL))],
        out_specs=[row(D_MODEL), row(D_MODEL), row(D_FF)],
        out_shape=[jax.ShapeDtypeStruct((s, D_MODEL), F32), jax.ShapeDtypeStruct((s, D_MODEL), BF16),
                   jax.ShapeDtypeStruct((s, D_FF), BF16)],
        compiler_params=_params(1),
    )(h, g, w_up, w_down)


def _gate_fwd(h, g, w_gate, p, layer, w_ple, name, head=None, follow=None):
    assert (head is None) != (follow is None)
    s = h.shape[0]
    t = _row_tile(s, 512)
    last = s // t - 1

    def body(h_ref, g_ref, wg_ref, p_ref, wp_ref, *refs):
        x = h_ref[...]
        gv = g_ref[...]
        n, rstd, hn = _rms(x, gv)
        hb = hn.astype(BF16)
        gate = 1.0 / (1.0 + jnp.exp(-_dot(hb, wg_ref[...])))
        pb = p_ref[...].astype(BF16)
        e = _dot(pb, wp_ref[...])
        h3 = x + gate * e
        if follow is not None:
            out_ref, hn_ref, gate_ref, pb_ref = refs[5:9]
            out_ref[...] = h3
            hn_ref[...] = hb
            pb_ref[...] = pb
            gate_ref[...] = gate.astype(BF16)
            _normproj_tile(h3, *refs[:5], *refs[9:])
            return
        gf_ref, t_ref, loss_ref, dgf_ref, out_ref, dg_ref, dwg_ref, dwgb_ref, dwp_ref, dwpb_ref = refs
        i = pl.program_id(0)

        @pl.when(i == 0)
        def _():
            for ref in (loss_ref, dgf_ref, dg_ref, dwg_ref, dwp_ref):
                ref[...] = jnp.zeros_like(ref)

        gf = gf_ref[...]
        n3, rstd3, y = _rms(h3, gf)
        err = y - t_ref[...]
        loss_ref[...] += jnp.sum(err * err) * (0.5 / D_MODEL)
        d, dgf = _rms_bwd(err * (1.0 / D_MODEL), n3, rstd3, gf)
        dgf_ref[...] += dgf
        dgl = (d * e * gate * (1.0 - gate)).astype(BF16)
        dwg_ref[...] += _dot_tn(hb, dgl)
        dwp_ref[...] += _dot_tn(pb, (d * gate).astype(BF16))
        dx, dg = _rms_bwd(_dot_nt(dgl, wg_ref[...]), n, rstd, gv)
        out_ref[...] = d + dx
        dg_ref[...] += dg

        @pl.when(i == last)
        def _():
            dwgb_ref[...] = dwg_ref[...].astype(BF16)
            dwpb_ref[...] = dwp_ref[...].astype(BF16)

    row = lambda w: pl.BlockSpec((t, w), lambda i: (i, 0))
    full = lambda a, b: pl.BlockSpec((a, b), lambda i: (0, 0))
    in_specs = [row(D_MODEL), full(1, D_MODEL), _resident((D_MODEL, D_MODEL)),
                pl.BlockSpec((None, t, PLE_DIM), lambda i: (layer, i, 0)), _resident((PLE_DIM, D_MODEL))]
    if follow is not None:
        next_in, next_out, next_shape, scratch = _normproj_operands(s, t)
        return pl.pallas_call(
            body, name=name, grid=(s // t,), in_specs=in_specs + next_in,
            out_specs=[row(D_MODEL), row(D_MODEL), row(D_MODEL), row(PLE_DIM)] + next_out,
            out_shape=[jax.ShapeDtypeStruct((s, D_MODEL), F32), jax.ShapeDtypeStruct((s, D_MODEL), BF16),
                       jax.ShapeDtypeStruct((s, D_MODEL), BF16), jax.ShapeDtypeStruct((s, PLE_DIM), BF16)] + next_shape,
            scratch_shapes=scratch, compiler_params=_params(1),
        )(h, g, w_gate, p, w_ple, *follow)
    loss, dgf, dh2, dg, dwg, dwgb, dwp, dwpb = pl.pallas_call(
        body, name=name, grid=(s // t,), in_specs=in_specs + [full(1, D_MODEL), row(D_MODEL)],
        out_specs=[pl.BlockSpec((1, LANES), lambda i: (0, 0)), full(1, D_MODEL), row(D_MODEL), full(1, D_MODEL),
                   full(D_MODEL, D_MODEL), full(D_MODEL, D_MODEL), full(PLE_DIM, D_MODEL), full(PLE_DIM, D_MODEL)],
        out_shape=[jax.ShapeDtypeStruct((1, LANES), F32), jax.ShapeDtypeStruct((1, D_MODEL), F32),
                   jax.ShapeDtypeStruct((s, D_MODEL), F32), jax.ShapeDtypeStruct((1, D_MODEL), F32),
                   jax.ShapeDtypeStruct((D_MODEL, D_MODEL), F32), jax.ShapeDtypeStruct((D_MODEL, D_MODEL), BF16),
                   jax.ShapeDtypeStruct((PLE_DIM, D_MODEL), F32), jax.ShapeDtypeStruct((PLE_DIM, D_MODEL), BF16)],
        compiler_params=_params(1),
    )(h, g, w_gate, p, w_ple, *head)
    return loss, dgf, dh2, dg, (dwg, dwgb), (dwp, dwpb)


def _gate_bwd(dh, gate, pb, w_ple, h, g, w_gate, hn, name, after=()):
    s = h.shape[0]
    t = _row_tile(s, FWD_TILE)
    last = s // t - 1

    def body(dh_ref, gate_ref, pb_ref, wp_ref, h_ref, g_ref, wg_ref, hn_ref, out_ref, dg_ref, dwg_ref, dwgb_ref,
             dwp_ref, dwpb_ref):
        i = pl.program_id(0)

        @pl.when(i == 0)
        def _():
            dg_ref[...] = jnp.zeros_like(dg_ref)
            dwg_ref[...] = jnp.zeros_like(dwg_ref)
            dwp_ref[...] = jnp.zeros_like(dwp_ref)

        d = dh_ref[...]
        gate = gate_ref[...].astype(F32)
        pb = pb_ref[...]
        e = _dot(pb, wp_ref[...])
        dgl = (d * e * gate * (1.0 - gate)).astype(BF16)
        dwg_ref[...] += _dot_tn(hn_ref[...], dgl)
        dwp_ref[...] += _dot_tn(pb, (d * gate).astype(BF16))
        gv = g_ref[...]
        n, rstd, _ = _rms(h_ref[...], gv)
        dx, dg = _rms_bwd(_dot_nt(dgl, wg_ref[...]), n, rstd, gv)
        out_ref[...] = d + dx
        dg_ref[...] += dg

        @pl.when(i == last)
        def _():
            dwgb_ref[...] = dwg_ref[...].astype(BF16)
            dwpb_ref[...] = dwp_ref[...].astype(BF16)

    row = lambda w: pl.BlockSpec((t, w), lambda i: (i, 0))
    full = lambda a, b: pl.BlockSpec((a, b), lambda i: (0, 0))
    dh2, dg, dwg, dwgb, dwp, dwpb = pl.pallas_call(
        _ordered_after(body, 8, after), name=name, grid=(s // t,),
        in_specs=[row(D_MODEL), row(D_MODEL), row(PLE_DIM), _resident((PLE_DIM, D_MODEL)), row(D_MODEL),
                  full(1, D_MODEL), _resident((D_MODEL, D_MODEL)), row(D_MODEL)]
        + [pl.BlockSpec(memory_space=pl.ANY)] * len(after),
        out_specs=[row(D_MODEL), full(1, D_MODEL), full(D_MODEL, D_MODEL), full(D_MODEL, D_MODEL),
                   full(PLE_DIM, D_MODEL), full(PLE_DIM, D_MODEL)],
        out_shape=[jax.ShapeDtypeStruct((s, D_MODEL), F32), jax.ShapeDtypeStruct((1, D_MODEL), F32),
                   jax.ShapeDtypeStruct((D_MODEL, D_MODEL), F32), jax.ShapeDtypeStruct((D_MODEL, D_MODEL), BF16),
                   jax.ShapeDtypeStruct((PLE_DIM, D_MODEL), F32), jax.ShapeDtypeStruct((PLE_DIM, D_MODEL), BF16)],
        compiler_params=_params(1),
    )(dh, gate, pb, w_ple, h, g, w_gate, hn, *after)
    return dh2, dg, (dwg, dwgb), (dwp, dwpb)


def _mlp_bwd(dh, r, h, g, w_up, w_down, name):
    s = h.shape[0]
    t = _row_tile(s, MLP_BWD_TILE)
    nblk = D_FF // FF_BLOCK

    def body(dh_ref, r_ref, h_ref, g_ref, wu_ref, wd_ref, out_ref, dup_ref, dg_ref, dhb_ref):
        @pl.when(pl.program_id(0) == 0)
        def _():
            dg_ref[...] = jnp.zeros_like(dg_ref)

        d = dh_ref[...]
        db = d.astype(BF16)
        dhb_ref[...] = db
        back = None
        for b in range(nblk):
            cols = slice(b * FF_BLOCK, (b + 1) * FF_BLOCK)
            dup = (_dot_nt(db, wd_ref[b]) * (2.0 * r_ref[:, cols].astype(F32))).astype(BF16)
            dup_ref[:, cols] = dup
            part = _dot_nt(dup, wu_ref[b])
            back = part if back is None else back + part
        gv = g_ref[...]
        n, rstd, _ = _rms(h_ref[...], gv)
        dx, dg = _rms_bwd(back, n, rstd, gv)
        out_ref[...] = d + dx
        dg_ref[...] += dg

    row = lambda w: pl.BlockSpec((t, w), lambda i: (i, 0))
    vec = pl.BlockSpec((1, D_MODEL), lambda i: (0, 0))
    resident = lambda shape: pl.BlockSpec(shape, lambda i: (0, 0, 0), pipeline_mode=pl.Buffered(1))
    return pl.pallas_call(
        body, name=name, grid=(s // t,),
        in_specs=[row(D_MODEL), row(D_FF), row(D_MODEL), vec,
                  resident((nblk, D_MODEL, FF_BLOCK)), resident((nblk, FF_BLOCK, D_MODEL))],
        out_specs=[row(D_MODEL), row(D_FF), vec, row(D_MODEL)],
        out_shape=[jax.ShapeDtypeStruct((s, D_MODEL), F32), jax.ShapeDtypeStruct((s, D_FF), BF16),
                   jax.ShapeDtypeStruct((1, D_MODEL), F32), jax.ShapeDtypeStruct((s, D_MODEL), BF16)],
        compiler_params=_params(1),
    )(dh, r, h, g, w_up, w_down)


def _outproj_bwd(dh, w_out, o, lse, ones_bd, a, name):
    s = dh.shape[0]
    t = _row_tile(s, 512)
    last = s // t - 1

    def body(dh_ref, w_ref, o0, o1, o2, l0, l1, l2, bd_ref, a_ref, dp_ref, do0, do1, do2, de0, de1, de2, dw_ref,
             dwb_ref, *stages):
        i = pl.program_id(0)

        @pl.when(i == 0)
        def _():
            dw_ref[...] = jnp.zeros_like(dw_ref)

        stages = _pair_stages(stages)
        dhb = dh_ref[...].astype(BF16)
        dw_ref[...] += _dot_tn(a_ref[...], dhb)

        @pl.when(i == last)
        def _():
            dwb_ref[...] = dw_ref[...].astype(BF16)

        da = _dot_nt(dhb, w_ref[...])
        dp_ref[...] = da[:, 0:POOL_WIDTH]
        ov =[_from_residues(r, stages[i], DILATIONS[i]) for i, r in enumerate((o0, o1, o2))]
        lv = [_from_residues(r, stages[3 + i], DILATIONS[i]) for i, r in enumerate((l0, l1, l2))]
        wts = _group_weights(*lv)
        bd = bd_ref[...]
        cbar = jnp.zeros((t, GROUP_WIDTH), F32)
        for grp, do_ref in enumerate((do0, do1, do2)):
            lo = POOL_WIDTH + grp * GROUP_WIDTH
            dag = da[:, lo:lo + GROUP_WIDTH]
            _to_residues(dag * wts[grp], stages[6 + grp], do_ref, DILATIONS[grp])
            prod = dag * ov[grp]
            hi = prod.astype(BF16)
            low = (prod - hi.astype(F32)).astype(BF16)
            cbar = cbar + wts[grp] * (_dot(hi, bd) + _dot(low, bd))
        for grp, de_ref in enumerate((de0, de1, de2)):
            _to_residues(wts[grp] * cbar, stages[9 + grp], de_ref, DILATIONS[grp])

    row = lambda w: pl.BlockSpec((t, w), lambda i: (i, 0))
    full = lambda a, b: pl.BlockSpec((a, b), lambda i: (0, 0))
    res = [_residue_spec(dil, t) for dil in DILATIONS]
    *outs, dw, dwb = pl.pallas_call(
        body, name=name, grid=(s // t,),
        in_specs=[row(D_MODEL), full(D_MODEL, D_MODEL)] + res + res + [full(GROUP_WIDTH, GROUP_WIDTH), row(D_MODEL)],
        out_specs=[row(POOL_WIDTH)] + res + res + [full(D_MODEL, D_MODEL)] * 2,
        out_shape=[jax.ShapeDtypeStruct((s, POOL_WIDTH), F32)] + [_residue_shape(dil, s, BF16) for dil in DILATIONS]
        + [_residue_shape(dil, s, F32) for dil in DILATIONS]
        + [jax.ShapeDtypeStruct((D_MODEL, D_MODEL), F32), jax.ShapeDtypeStruct((D_MODEL, D_MODEL), BF16)],
        scratch_shapes=_stages(t, 12),
        compiler_params=_params(1),
    )(dh, w_out, *o, *lse, ones_bd, a)
    return (*outs, (dw, dwb))


def _attn_bwd(q, k, v, do, lse, deff, name, after=()):
    dil, length, _ = q.shape
    nb = length // ATTN_BLOCK
    qb = _blocks_per_step(nb)
    nj = nb // qb
    rs = _residues_per_step(dil, nb, qb)
    whole = nj == 1
    tail = slice((qb - 1) * ATTN_BLOCK, qb * ATTN_BLOCK)
    block = lambda qi: slice(qi * ATTN_BLOCK, (qi + 1) * ATTN_BLOCK)

    def body(q_ref, kp_ref, kc_ref, vp_ref, vc_ref, do_ref, lse_ref, de_ref, dq_ref, dk_ref, dv_ref, ck, cv):
        j = pl.program_id(1)

        def compute():
            masks = _head_masks()
            bias = _band_bias(j == 0)
            for rr in range(rs):
                dkc, dvc = [], []
                for qi in range(qb):
                    here, before = block(qi), block(qi - 1)
                    kcat = jnp.concatenate([kp_ref[rr] if qi == 0 else kc_ref[rr, before], kc_ref[rr, here]], axis=0)
                    vcat = jnp.concatenate([vp_ref[rr] if qi == 0 else vc_ref[rr, before], vc_ref[rr, here]], axis=0)
                    qs = _stack_heads(q_ref[rr, here], masks)
                    dos = _stack_heads(do_ref[rr, here], masks)
                    sc = _dot_nt(qs, kcat) + bias[min(qi, 1)]
                    p = jnp.exp(sc - _column_per_head(lse_ref[rr, here]))
                    ds = (p * (_dot_nt(dos, vcat) - _column_per_head(de_ref[rr, here]))).astype(BF16)
                    dq = jnp.zeros((ATTN_BLOCK, GROUP_WIDTH), F32)
                    for hd, msk in enumerate(masks):
                        dq = jnp.where(msk, _dot(ds[block(hd)], kcat), dq)
                    dq_ref[rr, here] = dq.astype(dq_ref.dtype)
                    dkc.append(_dot_tn(ds, qs))
                    dvc.append(_dot_tn(p.astype(BF16), dos))

                for out_ref, carry, parts in ((dk_ref, ck, dkc), (dv_ref, cv, dvc)):
                    full = [parts[qi][ATTN_BLOCK:] + parts[qi + 1][0:ATTN_BLOCK] for qi in range(qb - 1)]
                    if whole:
                        for qi, val in enumerate(full + [parts[qb - 1][ATTN_BLOCK:]]):
                            out_ref[rr, block(qi)] = val.astype(out_ref.dtype)
                        continue

                    @pl.when(j > 0)
                    def _():
                        if qb > 1:
                            out_ref[0, 0:(qb - 1) * ATTN_BLOCK] = carry[0:(qb - 1) * ATTN_BLOCK].astype(out_ref.dtype)
                        out_ref[0, tail] = (carry[tail] + parts[0][0:ATTN_BLOCK]).astype(out_ref.dtype)

                    for qi, val in enumerate(full):
                        carry[block(qi)] = val
                    carry[tail] = parts[qb - 1][ATTN_BLOCK:]

        if whole:
            compute()
        else:
            pl.when(j < nj)(compute)

            @pl.when(j == nj)
            def _():
                dk_ref[0] = ck[...].astype(dk_ref.dtype)
                dv_ref[0] = cv[...].astype(dv_ref.dtype)

    step = lambda j: jnp.minimum(j, nj - 1)
    cur = pl.BlockSpec((rs, qb * ATTN_BLOCK, GROUP_WIDTH), lambda r, j: (r, step(j), 0))
    prev = pl.BlockSpec((rs, ATTN_BLOCK, GROUP_WIDTH), lambda r, j: (r, jnp.maximum(qb * step(j) - 1, 0), 0))
    late = pl.BlockSpec((rs, qb * ATTN_BLOCK, GROUP_WIDTH), lambda r, j: (r, jnp.maximum(j - 1, 0), 0))
    return pl.pallas_call(
        _ordered_after(body, 8, after), name=name, grid=(dil // rs, 1 if whole else nj + 1),
        in_specs=[cur, prev, cur, prev, cur, cur, cur, cur] + [pl.BlockSpec(memory_space=pl.ANY)] * len(after),
        out_specs=[cur, cur if whole else late, cur if whole else late],
        out_shape=[jax.ShapeDtypeStruct(q.shape, BF16)] * 3,
        scratch_shapes=[pltpu.VMEM((qb * ATTN_BLOCK, GROUP_WIDTH), F32)] * 2,
        compiler_params=_params(2),
    )(q, k, k, v, v, do, lse, deff, *after)


def _pool_bwd(dpool, y, w_bd, scale, name, after=()):
    s = dpool.shape[0]
    t = _row_tile(s, 512)
    nt = s // t

    def body(dp_ref, y_ref, w_ref, sc_ref, du_ref, dw_ref, dsc_ref, ext, b2, b4, b8):
        i = pl.program_id(0)

        @pl.when(i == 0)
        def _():
            ext[t:, :] = jnp.zeros((POOL_HALO + POOL_PAD, POOL_WIDTH), F32)
            for buf in (b2, b4):
                buf[t + POOL_HALO:, :] = jnp.zeros((POOL_PAD, POOL_WIDTH), F32)
            dw_ref[...] = jnp.zeros_like(dw_ref)
            dsc_ref[...] = jnp.zeros_like(dsc_ref)

        dp = dp_ref[...]
        yb = y_ref[...]
        w = w_ref[...]
        dsc_ref[...] += jnp.sum(dp * _dot(yb, w), axis=0, keepdims=True)
        dyo = (dp * sc_ref[...]).astype(BF16)
        dw_ref[...] += _dot_tn(yb, dyo)
        dy = _dot_nt(dyo, w)
        win = _pool_lane_window()
        pos = (nt - 1 - i) * t + lax.broadcasted_iota(jnp.int32, (t, POOL_WIDTH), 0)
        gq = dy / jnp.minimum(pos + 1, win).astype(F32)
        ext[0:t, :] = gq
        du_ref[...] = _window_sums(ext, b2, b4, b8, t, 0, 0, 1) - dy
        ext[t:t + POOL_HALO, :] = gq[0:POOL_HALO, :]

    rev = pl.BlockSpec((t, POOL_WIDTH), lambda i: (nt - 1 - i, 0))
    full = lambda a, b: pl.BlockSpec((a, b), lambda i: (0, 0))
    return pl.pallas_call(
        _ordered_after(body, 4, after), name=name, grid=(nt,),
        in_specs=[rev, rev, full(POOL_WIDTH, POOL_WIDTH), full(1, POOL_WIDTH)]
        + [pl.BlockSpec(memory_space=pl.ANY)] * len(after),
        out_specs=[rev, full(POOL_WIDTH, POOL_WIDTH), full(1, POOL_WIDTH)],
        out_shape=[jax.ShapeDtypeStruct((s, POOL_WIDTH), F32), jax.ShapeDtypeStruct((POOL_WIDTH, POOL_WIDTH), F32),
                   jax.ShapeDtypeStruct((1, POOL_WIDTH), F32)],
        scratch_shapes=[pltpu.VMEM((t + POOL_HALO + POOL_PAD, POOL_WIDTH), F32)] * 4,
        compiler_params=_params(1),
    )(dpool, y, w_bd, scale, *after)


def _normproj_bwd(dh, du, dq, dk, dv, rc, rsa, rsb, w_in, h, g, name):
    s = h.shape[0]
    t = _row_tile(s, 512)

    def body(dh_ref, du_ref, q0, q1, q2, k0, k1, k2, v0, v1, v2, c_ref, sa_ref, sb_ref, w_ref, h_ref, g_ref,
             out_ref, dz_ref, dg_ref, *stages):
        @pl.when(pl.program_id(0) == 0)
        def _():
            dg_ref[...] = jnp.zeros_like(dg_ref)

        c, sa, sb = c_ref[...], sa_ref[...], sb_ref[...]

        def unrot(a, scale):
            halves = [_rot_t(a[:, hf * LANES:(hf + 1) * LANES] * scale, c, sa, sb) for hf in range(2)]
            return jnp.concatenate(halves, axis=1)

        staged = _pair_stages(stages)
        tok = lambda refs, base: [_from_residues(r, staged[base + i], DILATIONS[i]) for i, r in enumerate(refs)]
        chunks = [du_ref[...]]
        chunks += [unrot(a, HEAD_DIM ** -0.5) for a in tok((q0, q1, q2), 0)]
        chunks += [unrot(a, 1.0) for a in tok((k0, k1, k2), 3)]
        chunks += tok((v0, v1, v2), 6)
        acc = jnp.zeros((t, D_MODEL), F32)
        for ci, ch in enumerate(chunks):
            cols = slice(ci * GROUP_WIDTH, (ci + 1) * GROUP_WIDTH)
            cb = ch.astype(BF16)
            dz_ref[:, cols] = cb
            acc = acc + _dot(cb, w_ref[cols, :])
        gv = g_ref[...]
        n, rstd, _ = _rms(h_ref[...], gv)
        dx, dg = _rms_bwd(acc, n, rstd, gv)
        out_ref[...] = dh_ref[...] + dx
        dg_ref[...] += dg

    row = lambda w: pl.BlockSpec((t, w), lambda i: (i, 0))
    vec = pl.BlockSpec((1, D_MODEL), lambda i: (0, 0))
    res = [_residue_spec(dil, t) for dil in DILATIONS]
    return pl.pallas_call(
        body, name=name, grid=(s // t,),
        in_specs=[row(D_MODEL), row(POOL_WIDTH)] + res * 3 + _table_specs(t)
        + [pl.BlockSpec((N_IN, D_MODEL), lambda i: (0, 0)), row(D_MODEL), vec],
        out_specs=[row(D_MODEL), row(N_IN), vec],
        out_shape=[jax.ShapeDtypeStruct((s, D_MODEL), F32), jax.ShapeDtypeStruct((s, N_IN), BF16),
                   jax.ShapeDtypeStruct((1, D_MODEL), F32)],
        scratch_shapes=_stages(t, 9),
        compiler_params=_params(1),
    )(dh, du, *dq, *dk, *dv, rc, rsa, rsb, w_in, h, g)


def _matmul_tn(a, b, name, *, square_a=False, tm=None, tn=None, blocked_out=False, after=()):
    s, m = a.shape
    n = b.shape[1]
    tk = _row_tile(s, 2048)
    tm = tm or min(m, 1024)
    tn = tn or min(n, 1024)
    assert m % tm == 0 and n % tn == 0
    nk = s // tk
    nsub = tn // FF_BLOCK if blocked_out else 1

    def body(a_ref, b_ref, o_ref, ob_ref, acc):
        k = pl.program_id(2)

        def product():
            av = a_ref[...]
            if square_a:
                av = av.astype(F32)
                av = av * av
            return _dot_tn(av.astype(BF16), b_ref[...].astype(BF16))

        def emit(total):
            if blocked_out:
                for sub in range(nsub):
                    cols = slice(sub * FF_BLOCK, (sub + 1) * FF_BLOCK)
                    o_ref[sub] = total[:, cols]
                    ob_ref[sub] = total[:, cols].astype(BF16)
            else:
                o_ref[...] = total
                ob_ref[...] = total.astype(BF16)

        if nk == 1:
            emit(product())
            return

        @pl.when(k == 0)
        def _():
            acc[...] = product()

        @pl.when((k > 0) & (k < nk - 1))
        def _():
            acc[...] += product()

        @pl.when(k == nk - 1)
        def _():
            emit(acc[...] + product())

    if blocked_out:
        shape = (n // FF_BLOCK, m, FF_BLOCK)
        out_spec = pl.BlockSpec((nsub, tm, FF_BLOCK), lambda i, j, k: (j, i, 0))
    else:
        shape = (m, n)
        out_spec = pl.BlockSpec((tm, tn), lambda i, j, k: (i, j))
    return pl.pallas_call(
        _ordered_after(body, 2, after), name=name, grid=(m // tm, n // tn, nk),
        in_specs=[pl.BlockSpec((tk, tm), lambda i, j, k: (k, i)), pl.BlockSpec((tk, tn), lambda i, j, k: (k, j))]
        + [pl.BlockSpec(memory_space=pl.ANY)] * len(after),
        out_specs=[out_spec, out_spec],
        out_shape=[jax.ShapeDtypeStruct(shape, F32), jax.ShapeDtypeStruct(shape, BF16)],
        scratch_shapes=[pltpu.VMEM((tm, tn), F32)],
        compiler_params=_params(3),
    )(a, b, *after)


def _adamw_math(w, g, m, v):
    m = ADAM_B1 * m + (1.0 - ADAM_B1) * g
    v = ADAM_B2 * v + (1.0 - ADAM_B2) * (g * g)
    m_hat = m / (1.0 - ADAM_B1 ** ADAM_STEP)
    v_hat = v / (1.0 - ADAM_B2 ** ADAM_STEP)
    delta = -ADAM_LR * (m_hat / (jnp.sqrt(v_hat) + ADAM_EPS) + ADAM_WD * w)
    return delta, m, v


def _sum_chunks_body(own0_ref, own1_ref, r0_ref, r1_ref):
    layer0 = pl.program_id(0) == 0
    g = jnp.where(layer0, own0_ref[...], own1_ref[...])
    for k in range(N_DEV - 1):
        g = g + jnp.where(layer0, r0_ref[k], r1_ref[k]).astype(F32)
    return g


def _chunk_specs(t, cols):
    rows_of = lambda layer: (lambda l, i: jnp.where(l == layer, i, 0))
    blk = pl.BlockSpec((None, t, cols), lambda l, i, me: (l, i, 0))
    own = [pl.BlockSpec((None, t, cols), functools.partial(lambda l, i, me, pick: (me[0], pick(l, i), 0), pick=rows_of(ly)))
           for ly in range(2)]
    recv = [pl.BlockSpec((N_DEV - 1, t, cols), functools.partial(lambda l, i, me, pick: (0, pick(l, i), 0), pick=rows_of(ly)))
            for ly in range(2)]
    return blk, own + recv


def _sum_chunks(chunks, me, name):
    _, rows, cols = chunks[0].shape
    t = _row_tile(rows, 320)

    def body(me_ref, own0_ref, own1_ref, r0_ref, r1_ref, g_ref):
        g_ref[...] = _sum_chunks_body(own0_ref, own1_ref, r0_ref, r1_ref)

    blk, chunk_specs = _chunk_specs(t, cols)
    return pl.pallas_call(
        body, name=name,
        grid_spec=pltpu.PrefetchScalarGridSpec(num_scalar_prefetch=1, grid=(2, rows // t), in_specs=chunk_specs,
                                               out_specs=blk),
        out_shape=jax.ShapeDtypeStruct((2, rows, cols), F32), compiler_params=_params(2),
    )(me, *chunks)


def _adamw_sharded(w, m, v, grad, me, name):
    _, rows, cols = w.shape
    t = _row_tile(rows, 256)
    summed = not isinstance(grad, tuple)
    grad = (grad,) if summed else grad

    def body(me_ref, w_ref, m_ref, v_ref, *refs):
        g_ref, d_ref, nm_ref, nv_ref = refs[-4:]
        g = refs[0][...] if summed else _sum_chunks_body(*refs[:4])
        g_ref[...] = g
        d_ref[...], nm_ref[...], nv_ref[...] = _adamw_math(w_ref[...], g, m_ref[...], v_ref[...])

    blk, chunk_specs = _chunk_specs(t, cols)
    return pl.pallas_call(
        body, name=name,
        grid_spec=pltpu.PrefetchScalarGridSpec(
            num_scalar_prefetch=1, grid=(2, rows // t),
            in_specs=[blk, blk, blk] + ([blk] if summed else chunk_specs), out_specs=[blk] * 4),
        out_shape=[jax.ShapeDtypeStruct(w.shape, F32)] * 4,
        compiler_params=_params(2),
    )(me, w, m, v, *grad)


def _adamw_packed(w, g8, m, v, name):
    def body(w_ref, g_ref, m_ref, v_ref, go_ref, d_ref, nm_ref, nv_ref):
        g = g_ref[0]
        for dev in range(1, N_DEV):
            g = g + g_ref[dev]
        go_ref[...] = g
        d_ref[...], nm_ref[...], nv_ref[...] = _adamw_math(w_ref[...], g, m_ref[...], v_ref[...])

    return pl.pallas_call(
        body, name=name, out_shape=[jax.ShapeDtypeStruct(w.shape, F32)] * 4,
        compiler_params=pltpu.CompilerParams(vmem_limit_bytes=VMEM_LIMIT),
    )(w, g8, m, v)


def _peer(k):
    x, y, c = lax.axis_index("x"), lax.axis_index("y"), lax.axis_index("c")
    return (1 - x if k & 4 else x, 1 - y if k & 2 else y, 1 - c if k & 1 else c)


def _linear(dev):
    return 4 * dev[0] + 2 * dev[1] + dev[2]


HBM_SPEC = pl.BlockSpec(memory_space=pltpu.HBM)
SEM_SPEC = pl.BlockSpec(memory_space=pltpu.SEMAPHORE)
ANY_SPEC = pl.BlockSpec(memory_space=pl.ANY)
EFFECT = pltpu.SideEffectType.DATAFLOW_SIDE_EFFECTING


def _in_hbm(a):
    return pltpu.with_memory_space_constraint(a, pltpu.HBM)


class _Exchange:
    def __init__(self, name, groups, scatter, after=()):
        self.name, self.scatter = name, scatter
        self.sizes = sizes = [len(g) for g in groups]
        srcs = [a for g in groups for a in g]
        n, ng = len(srcs), len(groups)
        lead = (N_DEV - 1,) if scatter else (N_DEV,)
        shapes = [lead + (a.shape[1:] if scatter else a.shape) for a in srcs]
        lands = [lax.empty(sh, a.dtype) for sh, a in zip(shapes, srcs)]
        offsets = [sum(sizes[:gi]) for gi in range(ng)]
        copy = self._copy

        def body(*refs):
            src, land = refs[:n], refs[n:2 * n]
            sems = refs[2 * n + len(after):2 * n + len(after) + 2 * ng]
            token = refs[-1]
            for gi in range(ng):
                for wi in range(sizes[gi]):
                    w = offsets[gi] + wi
                    for k in range(1, N_DEV):
                        copy(src[w], land[w], sems[2 * gi], sems[2 * gi + 1], wi, k).start()
            token[...] = jnp.zeros_like(token)

        sem_shapes = [pltpu.SemaphoreType.DMA(((N_DEV - 1) * sz,)) for sz in sizes for _ in range(2)]
        outs = pl.pallas_call(
            body, name=name + "_start",
            in_specs=[HBM_SPEC] * (2 * n) + [ANY_SPEC] * len(after),
            out_specs=[SEM_SPEC] * (2 * ng) + [HBM_SPEC] * (2 * n) + [pl.BlockSpec(memory_space=pltpu.VMEM)],
            out_shape=sem_shapes + [pltpu.HBM(a.shape, a.dtype) for a in srcs + lands]
            + [jax.ShapeDtypeStruct((8, LANES), F32)],
            input_output_aliases={i: 2 * ng + i for i in range(2 * n)},
            compiler_params=pltpu.CompilerParams(has_side_effects=EFFECT),
        )(*[_in_hbm(a) for a in srcs + lands], *after)
        self.sems = [outs[2 * gi:2 * gi + 2] for gi in range(ng)]
        thru = outs[2 * ng:2 * ng + 2 * n]
        self.srcs = [thru[offsets[gi]:offsets[gi] + sizes[gi]] for gi in range(ng)]
        self.lands = [thru[n + offsets[gi]:n + offsets[gi] + sizes[gi]] for gi in range(ng)]
        self.token = outs[-1]

    def _copy(self, src, land, send_sems, recv_sems, wi, k):
        to = _peer(k)
        if self.scatter:
            src_ref, dst_ref = src.at[_linear(to)], land.at[k - 1]
        else:
            src_ref, dst_ref = src, land.at[_linear(_peer(0))]
        return pltpu.make_async_remote_copy(
            src_ref=src_ref, dst_ref=dst_ref, send_sem=send_sems.at[(N_DEV - 1) * wi + k - 1],
            recv_sem=recv_sems.at[(N_DEV - 1) * wi + k - 1], device_id=to, device_id_type=MESH)

    def wait(self, gi, after):
        n = self.sizes[gi]
        copy = self._copy

        def body(*refs):
            src, land = refs[:n], refs[n:2 * n]
            send_sems, recv_sems = refs[2 * n], refs[2 * n + 1]
            for wi in range(n):
                for k in range(1, N_DEV):
                    cp = copy(src[wi], land[wi], send_sems, recv_sems, wi, k)
                    cp.wait_send()
                    cp.wait_recv()

        arrays = list(self.srcs[gi]) + list(self.lands[gi])
        outs = pl.pallas_call(
            body, name=f"{self.name}_wait{gi}",
            in_specs=[HBM_SPEC] * (2 * n) + [SEM_SPEC, SEM_SPEC] + [ANY_SPEC] * len(after),
            out_specs=[HBM_SPEC] * (2 * n),
            out_shape=[pltpu.HBM(a.shape, a.dtype) for a in arrays],
            input_output_aliases={i: i for i in range(2 * n)},
            compiler_params=pltpu.CompilerParams(has_side_effects=EFFECT),
        )(*arrays, *self.sems[gi], *after)
        return outs[:n], outs[n:]


def _rotary_tables(positions):
    rot_dim = HEAD_DIM // 4
    inv_freq = ROPE_THETA ** (-jnp.arange(0, rot_dim, 2, dtype=F32) / rot_dim)
    ang = positions.astype(F32)[:, None] * inv_freq
    cs = jnp.concatenate([jnp.cos(ang), jnp.sin(ang)], axis=1)
    dim = jnp.arange(LANES) % HEAD_DIM
    first, second = dim < ROT_SHIFT, (dim >= ROT_SHIFT) & (dim < rot_dim)
    src = jnp.arange(2 * ROT_SHIFT)[:, None]
    angle = (dim % ROT_SHIFT)[None, :]
    c = jnp.where((first | second)[None, :] & (src == angle), 1.0, 0.0)
    sa = jnp.where(second[None, :] & (src == angle + ROT_SHIFT), 1.0, 0.0)
    sb = jnp.where(first[None, :] & (src == angle + ROT_SHIFT), -1.0, 0.0)
    spread = jnp.concatenate([c, sa, sb], axis=1).astype(F32)
    base = jnp.concatenate([jnp.where(first | second, 0.0, 1.0), jnp.zeros((2 * LANES,))]).astype(F32)[None, :]
    return jnp.dot(cs, spread, precision=lax.Precision.HIGHEST, preferred_element_type=F32) + base


def _block_diag(pool_w):
    gc = pool_w.shape[-1]
    out = jnp.zeros((POOL_WIDTH, POOL_WIDTH), pool_w.dtype)
    for grp in range(pool_w.shape[0]):
        out = lax.dynamic_update_slice(out, pool_w[grp], (grp * gc, grp * gc))
    return out


def _diag_blocks(a):
    gc = POOL_WIDTH // len(POOL_WINDOWS)
    return jnp.stack([a[grp * gc:(grp + 1) * gc, grp * gc:(grp + 1) * gc] for grp in range(len(POOL_WINDOWS))])


def _local_step(x, p, positions, loss_target, norm1, pool_w, pool_scale, norm2, norm3, final_norm, weights, send):
    rc = rsa = rsb = _rotary_tables(positions)
    ones_bd = _block_diag(jnp.ones((4, HEAD_DIM, HEAD_DIM), BF16))
    saved = []
    h = x
    for i in range(2):
        tag = f"_l{i}"
        g1, g2, g3 = norm1[i:i + 1], norm2[i:i + 1], norm3[i:i + 1]
        w_bd = _block_diag(pool_w[i]).astype(BF16)
        scale = pool_scale[i:i + 1]
        if i == 0:
            w_in = weights(i, "in", (h, rc, w_bd))
            hn1, u, *qkv = _normproj_fwd(h, g1, w_in, rc, rsa, rsb, "normproj_fwd" + tag)
        else:
            w_in, (hn1, u, *qkv) = ahead
        qkv = [qkv[3 * grp:3 * grp + 3] for grp in range(3)]
        started = weights(i, "prefetch", (hn1,))
        o, lse = zip(*[_attn_fwd(*qkv[grp], f"attn_fwd{tag}_g{grp}", after=started) for grp in range(3)])
        w_out = weights(i, "out", o)
        h1, a, y = _outproj_fwd(h, u, w_bd, scale, o, lse, w_out, "outproj_fwd" + tag)
        w_up, w_down = weights(i, "mlp", (h1,))
        h2, hn2, r = _mlp_fwd(h1, g2, w_up, w_down, "mlp_fwd" + tag)
        w_gate, w_ple = weights(i, "gate", (h2,))
        h0 = h
        if i == 0:
            w_in_next = weights(1, "in", (h2,))
            h, hn3, gate, pb, *ahead = _gate_fwd(h2, g3, w_gate, p, i, w_ple, "gate_normproj_fwd",
                                                 follow=(norm1[1:2], w_in_next, rc, rsa, rsb))
            ahead = (w_in_next, ahead)
        else:
            hn3 = gate = pb = None
            loss, d_final, *top = _gate_fwd(h2, g3, w_gate, p, i, w_ple, "gate_loss_gate_bwd",
                                            head=(final_norm.reshape(1, D_MODEL), loss_target))
        saved.append(dict(h0=h0, hn1=hn1, qkv=qkv, y=y, o=o, lse=lse, a=a, h1=h1, hn2=hn2, r=r, h2=h2,
                          hn3=hn3, gate=gate, pb=pb, w_bd=w_bd, scale=scale, g1=g1, g2=g2, g3=g3,
                          w_in=w_in, w_out=w_out, w_up=w_up, w_down=w_down, w_gate=w_gate, w_ple=w_ple))

    grads = [None, None]
    sent = ()
    for i in (1, 0):
        tag = f"_l{i}"
        sv = saved[i]
        if i == 1:
            dh2, dg3, dw_gate, dw_ple = top
        else:
            dh2, dg3, dw_gate, dw_ple = _gate_bwd(dh, sv["gate"], sv["pb"], sv["w_ple"], sv["h2"], sv["g3"],
                                                  sv["w_gate"], sv["hn3"], "gate_bwd" + tag, after=sent)
        dh1, dup, dg2, dh2b = _mlp_bwd(dh2, sv["r"], sv["h1"], sv["g2"], sv["w_up"], sv["w_down"], "mlp_bwd" + tag)
        dw_down = _matmul_tn(sv["r"], dh2b, "dw_down" + tag, square_a=True)
        dw_up = _matmul_tn(sv["hn2"], dup, "dw_up" + tag, blocked_out=True)
        dpool, do0, do1, do2, de0, de1, de2, dw_out = _outproj_bwd(dh1, sv["w_out"], sv["o"], sv["lse"], ones_bd,
                                                                   sv["a"], "outproj_bwd" + tag)
        sent = send(i, "main", dict(w_gate=dw_gate, w_ple=dw_ple, w_down=dw_down, w_up=dw_up, w_out=dw_out))
        dqkv = [_attn_bwd(*sv["qkv"][grp], do_g, sv["lse"][grp], de_g, f"attn_bwd{tag}_g{grp}", after=sent)
                for grp, (do_g, de_g) in enumerate(((do0, de0), (do1, de1), (do2, de2)))]
        dq, dk, dv = zip(*dqkv)
        du, dw_bd, dscale = _pool_bwd(dpool, sv["y"], sv["w_bd"], sv["scale"], "pool_bwd" + tag, after=sent)
        dh, dz, dg1 = _normproj_bwd(dh1, du, dq, dk, dv, rc, rsa, rsb, sv["w_in"], sv["h0"], sv["g1"],
                                    "normproj_bwd" + tag)
        grads[i] = dict(norm1=dg1, norm2=dg2, norm3=dg3, pool_w=_diag_blocks(dw_bd), pool_scale=dscale)
        small_sent = send(0, "small", (grads, d_final, loss)) if i == 0 else ()
        dw_in = _matmul_tn(dz, sv["hn1"], "dw_in" + tag, tm=N_IN // 2, after=small_sent)
        sent = send(i, "in", dict(w_in=dw_in))
    return dh, sent


def _pack_small(norm1, norm2, norm3, final_norm, pool_scale, pool_w, spare=None):
    spare = jnp.zeros((1, LANES), F32) if spare is None else spare
    scale_row = jnp.concatenate([pool_scale.reshape(1, 2 * POOL_WIDTH), spare,
                                 jnp.zeros((1, D_MODEL - 2 * POOL_WIDTH - LANES), F32)], axis=1)
    return jnp.concatenate([norm1, norm2, norm3, final_norm.reshape(1, D_MODEL), scale_row,
                            pool_w.reshape(32, D_MODEL)], axis=0)


def _unpack_small(a):
    return dict(norm1=a[0:2], norm2=a[2:4], norm3=a[4:6], final_norm=a[6], pool_scale=a[7, 0:2 * POOL_WIDTH].reshape(2, POOL_WIDTH),
                pool_w=a[8:40].reshape(2, 4, HEAD_DIM, HEAD_DIM))


def _chunks_cols(a, cols):
    return a.reshape(a.shape[0], N_DEV, cols).transpose(1, 0, 2)


def _chunks_rows(a, rows):
    return a.reshape(N_DEV, rows, a.shape[1])


BIG = ("w_in", "w_out", "w_up", "w_down", "w_gate", "w_ple")
SMALL = ("norm1", "norm2", "norm3", "final_norm", "pool_scale", "pool_w")
ORDER = ("norm1", "w_in", "pool_w", "pool_scale", "w_out", "norm2", "w_up", "w_down", "norm3", "w_gate", "w_ple",
         "final_norm")


def kernel(x, p, positions, norm1, w_in, pool_w, pool_scale, w_out, norm2, w_up, w_down, norm3, w_gate, w_ple, final_norm, loss_target, m_norm1, m_w_in, m_pool_w, m_pool_scale, m_w_out, m_norm2, m_w_up, m_w_down, m_norm3, m_w_gate, m_w_ple, m_final_norm, v_norm1, v_w_in, v_pool_w, v_pool_scale, v_w_out, v_norm2, v_w_up, v_w_down, v_norm3, v_w_gate, v_w_ple, v_final_norm):
    w = dict(norm1=norm1, w_in=w_in, pool_w=pool_w, pool_scale=pool_scale, w_out=w_out, norm2=norm2, w_up=w_up,
             w_down=w_down, norm3=norm3, w_gate=w_gate, w_ple=w_ple, final_norm=final_norm)
    m = dict(norm1=m_norm1, w_in=m_w_in, pool_w=m_pool_w, pool_scale=m_pool_scale, w_out=m_w_out, norm2=m_norm2,
             w_up=m_w_up, w_down=m_w_down, norm3=m_norm3, w_gate=m_w_gate, w_ple=m_w_ple, final_norm=m_final_norm)
    v = dict(norm1=v_norm1, w_in=v_w_in, pool_w=v_pool_w, pool_scale=v_pool_scale, w_out=v_w_out, norm2=v_norm2,
             w_up=v_w_up, w_down=v_w_down, norm3=v_norm3, w_gate=v_w_gate, w_ple=v_w_ple, final_norm=v_final_norm)
    seq = x.shape[1]

    bf = {n: [w[n][layer].astype(BF16) for layer in range(2)] for n in BIG}
    bf["w_in"] = [a.T for a in bf["w_in"]]
    me = 4 * lax.axis_index("x") + 2 * lax.axis_index("y") + lax.axis_index("c")
    parts = dict(zip(("in", "out", "mlp", "gate"), (("w_in",), ("w_out",), ("w_up", "w_down"), ("w_gate", "w_ple"))))
    first = _Exchange("gather_first", [[bf["w_in"][0]]], scatter=False)
    later = [pt for pt in parts if pt != "in"]
    gathers = [_Exchange("gather_l0", [[bf[n][0] for n in parts[pt]] for pt in later], scatter=False,
                         after=(first.token,))]
    unpack = dict(w_in=lambda a: a.reshape(N_IN, D_MODEL),
                  w_out=lambda a: a.reshape(D_MODEL, D_MODEL), w_gate=lambda a: a.reshape(D_MODEL, D_MODEL),
                  w_ple=lambda a: a.transpose(1, 0, 2).reshape(PLE_DIM, D_MODEL), w_up=lambda a: a, w_down=lambda a: a)

    def weights(layer, part, after):
        if part == "prefetch":
            if layer != 0:
                return ()
            gathers.append(_Exchange("gather_l1", [[bf[n][1] for n in parts[pt]] for pt in parts], scatter=False,
                                     after=after))
            return (gathers[1].token,)
        if layer == 0 and part == "in":
            shards, lands = first.wait(0, (*after, gathers[0].token))
        elif layer == 0:
            shards, lands = gathers[0].wait(later.index(part), after)
        else:
            shards, lands = gathers[1].wait(tuple(parts).index(part), after)
        full = [unpack[n](lax.dynamic_update_slice_in_dim(land, shard[None], me, axis=0))
                for n, shard, land in zip(parts[part], shards, lands)]
        return full if len(full) > 1 else full[0]

    to_chunks = dict(w_in=lambda a: _chunks_rows(a, N_IN // N_DEV),
                     w_out=lambda a: _chunks_rows(a, D_MODEL // N_DEV),
                     w_up=lambda a: a, w_down=lambda a: _chunks_rows(a, FF_BLOCK),
                     w_gate=lambda a: _chunks_rows(a, D_MODEL // N_DEV), w_ple=lambda a: _chunks_cols(a, D_MODEL // N_DEV))
    own = {n: [None, None] for n in BIG}
    scatters = {}

    def send(layer, part, grads):
        if part == "small":
            per_layer, d_final, loss = grads
            pack = _pack_small(
                *[jnp.concatenate([per_layer[0][n], per_layer[1][n]], axis=0) for n in ("norm1", "norm2", "norm3")],
                d_final.reshape(D_MODEL),
                jnp.concatenate([per_layer[0]["pool_scale"], per_layer[1]["pool_scale"]], axis=0),
                jnp.stack([per_layer[0]["pool_w"], per_layer[1]["pool_w"]]), spare=loss)
            scatters["small"] = _Exchange("gather_small", [[pack]], scatter=False)
            return (scatters["small"].token,)
        for n, (g32, _) in grads.items():
            own[n][layer] = to_chunks[n](g32)
        ex = _Exchange(f"scatter_{part}_l{layer}", [[to_chunks[n](g16) for n, (_, g16) in grads.items()]], scatter=True)
        scatters[layer, part] = (tuple(grads), ex)
        return (ex.token,)

    dx, sent = _local_step(
        x.reshape(seq, D_MODEL), p.reshape(2, seq, PLE_DIM), positions.reshape(seq), loss_target.reshape(seq, D_MODEL),
        norm1, pool_w, pool_scale, norm2, norm3, final_norm, weights, send)

    g_out, d_out, m_out, v_out = {}, {}, {}, {}
    my_index = me.reshape(1)
    for part in ("main", "in"):
        recv = {}
        for layer in (1, 0):
            names, ex = scatters[layer, part]
            for n, r in zip(names, ex.wait(0, sent)[1]):
                recv[n, layer] = r
        for n in names:
            grad = (*own[n], recv[n, 0], recv[n, 1])
            if n == "w_in":
                grad = _sum_chunks(grad, my_index, "sum_w_in").transpose(0, 2, 1)
            g_out[n], d_out[n], m_out[n], v_out[n] = _adamw_sharded(w[n], m[n], v[n], grad, my_index, "adamw_" + n)
        sent = tuple(d_out[n] for n in names)
    (mine,), (landed,) = scatters["small"].wait(0, sent)
    small_g8 = lax.dynamic_update_slice_in_dim(landed, mine[None], me, axis=0)
    pack = lambda t: _pack_small(*[t[n] for n in SMALL])
    small_g, d_small, m_small, v_small = _adamw_packed(pack(w), small_g8, pack(m), pack(v), "adamw_small")
    for dst, a in ((g_out, small_g), (d_out, d_small), (m_out, m_small), (v_out, v_small)):
        dst.update(_unpack_small(a))

    return (small_g[7, 2 * POOL_WIDTH],dx.reshape(1, seq, D_MODEL), *[g_out[n] for n in ORDER], *[d_out[n] for n in ORDER],
            *[m_out[n] for n in ORDER], *[v_out[n] for n in ORDER])
```

```python
import functools

import jax
import jax.numpy as jnp
from jax import lax
from jax.experimental import pallas as pl
from jax.experimental.pallas import tpu as pltpu

F32 = jnp.float32
BF16 = jnp.bfloat16

D_MODEL = 1024
HEAD_DIM = 64
POOL_WIDTH = 256
POOL_WINDOWS = (2, 4, 8, 16)
POOL_HALO = 16
POOL_PAD = 8
GROUP_WIDTH = 256
DILATIONS = (1, 4, 16)
ATTN_BLOCK = 128
ROT_SHIFT = 8
ROPE_THETA = 500000.0
D_FF = 4096
FF_BLOCK = 512
FF_PER_STEP = 2
MLP_BWD_TILE = 512
FWD_TILE = 1024
N_DEV = 8
N_IN = POOL_WIDTH + 3 * 768
PLE_DIM = 256
EPS = 1e-6
NEG_BIG = -1e30

ADAM_LR = 0.001
ADAM_B1 = 0.9
ADAM_B2 = 0.999
ADAM_EPS = 1e-08
ADAM_WD = 0.01
ADAM_STEP = 10

LANES = 128
VMEM_LIMIT = 56 * 1024 * 1024
MESH = pl.DeviceIdType.MESH


def _params(n_grid):
    return pltpu.CompilerParams(dimension_semantics=("arbitrary",) * n_grid, vmem_limit_bytes=VMEM_LIMIT)


def _dot(a, b):
    return jnp.dot(a, b, preferred_element_type=F32)


def _dot_nt(a, b):
    return lax.dot_general(a, b, (((1,), (1,)), ((), ())), preferred_element_type=F32)


def _dot_tn(a, b):
    return lax.dot_general(a, b, (((0,), (0,)), ((), ())), preferred_element_type=F32)


def _rms(x, g):
    rstd = lax.rsqrt(jnp.mean(x * x, axis=-1, keepdims=True) + EPS)
    n = x * rstd
    return n, rstd, n * g


def _rms_bwd(dy, n, rstd, g):
    dyn = dy * g
    dx = rstd * (dyn - n * jnp.mean(dyn * n, axis=-1, keepdims=True))
    return dx, jnp.sum(dy * n, axis=0, keepdims=True)


def _ordered_after(body, n_in, after):
    if not after:
        return body
    return lambda *refs: body(*refs[:n_in], *refs[n_in + len(after):])


def _resident(shape):
    return pl.BlockSpec(shape, lambda i: (0,) * len(shape), pipeline_mode=pl.Buffered(1))


def _row_tile(s, t):
    t = min(s, t)
    assert s % t == 0
    return t


def _rot(z, c, sa, sb):
    return z * c + pltpu.roll(z, ROT_SHIFT, 1) * sa + pltpu.roll(z, LANES - ROT_SHIFT, 1) * sb


def _table_specs(t):
    return [pl.BlockSpec((t, LANES), functools.partial(lambda i, k: (i, k), k=k)) for k in range(3)]


def _rot_t(dz, c, sa, sb):
    return dz * c + pltpu.roll(dz * sa, LANES - ROT_SHIFT, 1) + pltpu.roll(dz * sb, ROT_SHIFT, 1)


def _to_residues(value, stage, out_ref, dil):
    if dil == 1:
        out_ref[0] = value.astype(out_ref.dtype)
        return
    rows = value.shape[0] // dil
    for hf in range(GROUP_WIDTH // LANES):
        lanes = slice(hf * LANES, (hf + 1) * LANES)
        stage[hf][...] = value[:, lanes]
        for r in range(dil):
            out_ref[r, :, lanes] = stage[hf][pl.ds(r, rows, stride=dil), :].astype(out_ref.dtype)


def _from_residues(in_ref, stage, dil):
    if dil == 1:
        return in_ref[0].astype(F32)
    rows = in_ref.shape[1]
    for hf in range(GROUP_WIDTH // LANES):
        for r in range(dil):
            stage[hf][pl.ds(r, rows, stride=dil), :] = in_ref[r, :, hf * LANES:(hf + 1) * LANES].astype(F32)
    return jnp.concatenate([stage[0][...], stage[1][...]], axis=1)


def _residue_spec(dil, t):
    return pl.BlockSpec((dil, t // dil, GROUP_WIDTH), lambda i: (0, i, 0))


def _residue_shape(dil, s, dtype):
    return jax.ShapeDtypeStruct((dil, s // dil, GROUP_WIDTH), dtype)


def _stages(t, n):
    return [pltpu.VMEM((t, LANES), F32)] * (n * (GROUP_WIDTH // LANES))


def _pair_stages(refs):
    return [refs[i:i + 2] for i in range(0, len(refs), 2)]


def _normproj_tile(x, g_ref, w_ref, c_ref, sa_ref, sb_ref, hn_ref, u_ref, *rest):
    qkv_refs, stages = rest[:9], _pair_stages(rest[9:])
    _, _, hn = _rms(x, g_ref[...])
    hb = hn.astype(BF16)
    hn_ref[...] = hb
    c, sa, sb = c_ref[...], sa_ref[...], sb_ref[...]

    def rot(z, scale):
        halves = [_rot(z[:, hf * LANES:(hf + 1) * LANES], c, sa, sb) * scale for hf in range(2)]
        return jnp.concatenate(halves, axis=1)

    proj = lambda lo: _dot_nt(hb, w_ref[lo:lo + GROUP_WIDTH, :])
    u_ref[...] = proj(0)
    for grp, dil in enumerate(DILATIONS):
        lo = POOL_WIDTH + grp * GROUP_WIDTH
        q_ref, k_ref, v_ref = qkv_refs[3 * grp:3 * grp + 3]
        _to_residues(rot(proj(lo), HEAD_DIM ** -0.5), stages[0], q_ref, dil)
        _to_residues(rot(proj(lo + 768), 1.0), stages[1], k_ref, dil)
        _to_residues(proj(lo + 1536), stages[2], v_ref, dil)


def _normproj_operands(s, t):
    row = lambda w: pl.BlockSpec((t, w), lambda i: (i, 0))
    in_specs = [pl.BlockSpec((1, D_MODEL), lambda i: (0, 0)), _resident((N_IN, D_MODEL))] + _table_specs(t)
    out_specs = [row(D_MODEL), row(POOL_WIDTH)] + [_residue_spec(dil, t) for dil in DILATIONS for _ in range(3)]
    out_shape = [jax.ShapeDtypeStruct((s, D_MODEL), BF16), jax.ShapeDtypeStruct((s, POOL_WIDTH), F32)]
    out_shape += [_residue_shape(dil, s, BF16) for dil in DILATIONS for _ in range(3)]
    return in_specs, out_specs, out_shape, _stages(t, 3)


def _normproj_fwd(h, g, w_in, rc, rsa, rsb, name):
    s = h.shape[0]
    t = _row_tile(s, FWD_TILE)

    def body(h_ref, *refs):
        _normproj_tile(h_ref[...], *refs)

    in_specs, out_specs, out_shape, scratch = _normproj_operands(s, t)
    return pl.pallas_call(
        body, name=name, grid=(s // t,), in_specs=[pl.BlockSpec((t, D_MODEL), lambda i: (i, 0))] + in_specs,
        out_specs=out_specs, out_shape=out_shape, scratch_shapes=scratch, compiler_params=_params(1),
    )(h, g, w_in, rc, rsa, rsb)


def _pool_lane_window():
    lane = lax.broadcasted_iota(jnp.int32, (1, POOL_WIDTH), 1)
    return jnp.left_shift(2, lane // (POOL_WIDTH // len(POOL_WINDOWS)))


def _window_sums(ext, b2, b4, b8, t, lo, tile, direction):
    rows = t + POOL_HALO
    for src, dst, sh in ((ext, b2, 1), (b2, b4, 2), (b4, b8, 4)):
        dst[lo:lo + rows, :] = src[lo:lo + rows, :] + src[lo + direction * sh:lo + direction * sh + rows, :]
    s16 = b8[tile:tile + t, :] + b8[tile + direction * 8:tile + direction * 8 + t, :]
    win = _pool_lane_window()
    return jnp.where(win == 2, b2[tile:tile + t, :],
                     jnp.where(win == 4, b4[tile:tile + t, :], jnp.where(win == 8, b8[tile:tile + t, :], s16)))


def _pool_fwd_tile(i, u_ref, w_ref, sc_ref, y_ref, ext, b2, b4, b8):
    t = u_ref.shape[0]
    first = POOL_PAD + POOL_HALO

    @pl.when(i == 0)
    def _():
        for buf in (ext, b2, b4):
            buf[0:POOL_PAD, :] = jnp.zeros((POOL_PAD, POOL_WIDTH), F32)
        ext[POOL_PAD:first, :] = jnp.zeros((POOL_HALO, POOL_WIDTH), F32)

    x = u_ref[...]
    ext[first:, :] = x
    wsum = _window_sums(ext, b2, b4, b8, t, POOL_PAD, first, -1)
    pos = i * t + lax.broadcasted_iota(jnp.int32, (t, POOL_WIDTH), 0)
    cnt = jnp.minimum(pos + 1, _pool_lane_window()).astype(F32)
    yb = (wsum / cnt - x).astype(BF16)
    y_ref[...] = yb
    ext[POOL_PAD:first, :] = x[t - POOL_HALO:, :]
    return _dot(yb, w_ref[...]) * sc_ref[...]


def _head_masks():
    lane = lax.broadcasted_iota(jnp.int32, (ATTN_BLOCK, GROUP_WIDTH), 1)
    return [lane // HEAD_DIM == hd for hd in range(GROUP_WIDTH // HEAD_DIM)]


def _stack_heads(a, masks):
    zero = jnp.zeros_like(a)
    return jnp.concatenate([jnp.where(m, a, zero) for m in masks], axis=0)


def _band_bias(first_step):
    rows = ATTN_BLOCK * (GROUP_WIDTH // HEAD_DIM)
    i = lax.broadcasted_iota(jnp.int32, (rows, 2 * ATTN_BLOCK), 0) & (ATTN_BLOCK - 1)
    j = lax.broadcasted_iota(jnp.int32, (rows, 2 * ATTN_BLOCK), 1)
    inner = jnp.where((j >= i) & (j <= i + ATTN_BLOCK), 0.0, NEG_BIG)
    return jnp.where((j < ATTN_BLOCK) & first_step, NEG_BIG, inner), inner


def _column_per_head(a):
    return jnp.concatenate([a[:, hd * HEAD_DIM:hd * HEAD_DIM + 1] for hd in range(GROUP_WIDTH // HEAD_DIM)], axis=0)


def _blocks_per_step(nb):
    if nb <= 16:
        return nb
    return next(qb for qb in (16, 8, 4, 2, 1) if nb % qb == 0)


def _residues_per_step(dil, nb, qb):
    if nb != qb:
        return 1
    return next(rs for rs in (8, 4, 2, 1) if rs * qb <= 16 and dil % rs == 0)


def _attn_fwd(q, k, v, name, after=()):
    dil, length, _ = q.shape
    nb = length // ATTN_BLOCK
    qb = _blocks_per_step(nb)
    rs = _residues_per_step(dil, nb, qb)

    def body(q_ref, kp_ref, kc_ref, vp_ref, vc_ref, o_ref, lse_ref):
        masks = _head_masks()
        bias = _band_bias(pl.program_id(1) == 0)
        for rr in range(rs):
            for qi in range(qb):
                here = slice(qi * ATTN_BLOCK, (qi + 1) * ATTN_BLOCK)
                before = slice((qi - 1) * ATTN_BLOCK, qi * ATTN_BLOCK)
                kcat = jnp.concatenate([kp_ref[rr] if qi == 0 else kc_ref[rr, before], kc_ref[rr, here]], axis=0)
                vcat = jnp.concatenate([vp_ref[rr] if qi == 0 else vc_ref[rr, before], vc_ref[rr, here]], axis=0)
                qs = _stack_heads(q_ref[rr, here], masks)
                sc = _dot_nt(qs, kcat) + bias[min(qi, 1)]
                m = jnp.max(sc, axis=1, keepdims=True)
                e = jnp.exp(sc - m)
                l = jnp.sum(e, axis=1, keepdims=True)
                p = (e / l).astype(BF16)
                lse = m + jnp.log(l)
                o = jnp.zeros((ATTN_BLOCK, GROUP_WIDTH), F32)
                lse_full = jnp.zeros((ATTN_BLOCK, GROUP_WIDTH), F32)
                for hd, msk in enumerate(masks):
                    rows = slice(hd * ATTN_BLOCK, (hd + 1) * ATTN_BLOCK)
                    o = jnp.where(msk, _dot(p[rows], vcat), o)
                    lse_full = jnp.where(msk, lse[rows], lse_full)
                o_ref[rr, here] = o.astype(o_ref.dtype)
                lse_ref[rr, here] = lse_full

    cur = pl.BlockSpec((rs, qb * ATTN_BLOCK, GROUP_WIDTH), lambda r, j: (r, j, 0))
    prev = pl.BlockSpec((rs, ATTN_BLOCK, GROUP_WIDTH), lambda r, j: (r, jnp.maximum(qb * j - 1, 0), 0))
    return pl.pallas_call(
        _ordered_after(body, 5, after), name=name, grid=(dil // rs, nb // qb),
        in_specs=[cur, prev, cur, prev, cur] + [pl.BlockSpec(memory_space=pl.ANY)] * len(after), out_specs=[cur, cur],
        out_shape=[jax.ShapeDtypeStruct(q.shape, BF16), jax.ShapeDtypeStruct(q.shape, F32)],
        compiler_params=_params(2),
    )(q, k, k, v, v, *after)


def _group_weights(l0, l1, l2):
    m = jnp.maximum(jnp.maximum(l0, l1), l2)
    e0, e1, e2 = jnp.exp(l0 - m), jnp.exp(l1 - m), jnp.exp(l2 - m)
    den = e0 + e1 + e2
    return e0 / den, e1 / den, e2 / den


def _outproj_fwd(h, u, w_bd, scale, o, lse, w_out, name):
    s = h.shape[0]
    t = _row_tile(s, FWD_TILE)

    def body(h_ref, u_ref, wbd_ref, sc_ref, o0, o1, o2, l0, l1, l2, w_ref, out_ref, a_ref, y_ref, ext, b2, b4, b8,
             *stages):
        pool_out = _pool_fwd_tile(pl.program_id(0), u_ref, wbd_ref, sc_ref, y_ref, ext, b2, b4, b8)
        stages = _pair_stages(stages)
        ov = [_from_residues(r, stages[i], DILATIONS[i]) for i, r in enumerate((o0, o1, o2))]
        lv = [_from_residues(r, stages[3 + i], DILATIONS[i]) for i, r in enumerate((l0, l1, l2))]
        wts = _group_weights(*lv)
        a = jnp.concatenate([pool_out] + [ov[i] * wts[i] for i in range(3)], axis=1).astype(BF16)
        a_ref[...] = a
        out_ref[...] = h_ref[...] + _dot(a, w_ref[...])

    row = lambda w: pl.BlockSpec((t, w), lambda i: (i, 0))
    res = [_residue_spec(dil, t) for dil in DILATIONS]
    return pl.pallas_call(
        body, name=name, grid=(s // t,),
        in_specs=[row(D_MODEL), row(POOL_WIDTH), _resident((POOL_WIDTH, POOL_WIDTH)), _resident((1, POOL_WIDTH))]
        + res + res + [_resident((D_MODEL, D_MODEL))],
        out_specs=[row(D_MODEL), row(D_MODEL), row(POOL_WIDTH)],
        out_shape=[jax.ShapeDtypeStruct((s, D_MODEL), F32), jax.ShapeDtypeStruct((s, D_MODEL), BF16),
                   jax.ShapeDtypeStruct((s, POOL_WIDTH), BF16)],
        scratch_shapes=[pltpu.VMEM((t + POOL_HALO + POOL_PAD, POOL_WIDTH), F32)] * 4 + _stages(t, 6),
        compiler_params=_params(1),
    )(h, u, w_bd, scale, *o, *lse, w_out)


def _mlp_fwd(h, g, w_up, w_down, name):
    s = h.shape[0]
    t = _row_tile(s, 512)
    nblk = D_FF // FF_BLOCK

    def body(h_ref, g_ref, wu_ref, wd_ref, out_ref, hn_ref, r_ref):
        x = h_ref[...]
        _, _, hn = _rms(x, g_ref[...])
        hb = hn.astype(BF16)
        hn_ref[...] = hb
        acc = None
        for b0 in range(0, nblk, FF_PER_STEP):
            acts = []
            for b in range(b0, b0 + FF_PER_STEP):
                r = jnp.maximum(_dot(hb, wu_ref[b]), 0.0)
                r_ref[:, b * FF_BLOCK:(b + 1) * FF_BLOCK] = r.astype(BF16)
                acts.append((r * r).astype(BF16))
            wd = wd_ref[b0:b0 + FF_PER_STEP].reshape(FF_PER_STEP * FF_BLOCK, D_MODEL)
            part = _dot(jnp.concatenate(acts, axis=1), wd)
            acc = part if acc is None else acc + part
        out_ref[...] = x + acc

    row = lambda w: pl.BlockSpec((t, w), lambda i: (i, 0))
    resident = lambda shape: pl.BlockSpec(shape, lambda i: (0, 0, 0), pipeline_mode=pl.Buffered(1))
    return pl.pallas_call(
        body, name=name, grid=(s // t,),
        in_specs=[row(D_MODEL), pl.BlockSpec((1, D_MODEL), lambda i: (0, 0)),
                  resident((nblk, D_MODEL, FF_BLOCK)), resident((nblk, FF_BLOCK, D_MODEL))],
        out_specs=[row(D_MODEL), row(D_MODEL), row(D_FF)],
        out_shape=[jax.ShapeDtypeStruct((s, D_MODEL), F32), jax.ShapeDtypeStruct((s, D_MODEL), BF16),
                   jax.ShapeDtypeStruct((s, D_FF), BF16)],
        compiler_params=_params(1),
    )(h, g, w_up, w_down)


def _gate_fwd(h, g, w_gate, p, layer, w_ple, name, head=None, follow=None):
    assert (head is None) != (follow is None)
    s = h.shape[0]
    t = _row_tile(s, 512)
    last = s // t - 1

    def body(h_ref, g_ref, wg_ref, p_ref, wp_ref, *refs):
        x = h_ref[...]
        gv = g_ref[...]
        n, rstd, hn = _rms(x, gv)
        hb = hn.astype(BF16)
        gate = 1.0 / (1.0 + jnp.exp(-_dot(hb, wg_ref[...])))
        pb = p_ref[...].astype(BF16)
        e = _dot(pb, wp_ref[...])
        h3 = x + gate * e
        if follow is not None:
            out_ref, hn_ref, gate_ref, pb_ref = refs[5:9]
            out_ref[...] = h3
            hn_ref[...] = hb
            pb_ref[...] = pb
            gate_ref[...] = gate.astype(BF16)
            _normproj_tile(h3, *refs[:5], *refs[9:])
            return
        gf_ref, t_ref, loss_ref, dgf_ref, out_ref, dg_ref, dwg_ref, dwgb_ref, dwp_ref, dwpb_ref = refs
        i = pl.program_id(0)

        @pl.when(i == 0)
        def _():
            for ref in (loss_ref, dgf_ref, dg_ref, dwg_ref, dwp_ref):
                ref[...] = jnp.zeros_like(ref)

        gf = gf_ref[...]
        n3, rstd3, y = _rms(h3, gf)
        err = y - t_ref[...]
        loss_ref[...] += jnp.sum(err * err) * (0.5 / D_MODEL)
        d, dgf = _rms_bwd(err * (1.0 / D_MODEL), n3, rstd3, gf)
        dgf_ref[...] += dgf
        dgl = (d * e * gate * (1.0 - gate)).astype(BF16)
        dwg_ref[...] += _dot_tn(hb, dgl)
        dwp_ref[...] += _dot_tn(pb, (d * gate).astype(BF16))
        dx, dg = _rms_bwd(_dot_nt(dgl, wg_ref[...]), n, rstd, gv)
        out_ref[...] = d + dx
        dg_ref[...] += dg

        @pl.when(i == last)
        def _():
            dwgb_ref[...] = dwg_ref[...].astype(BF16)
            dwpb_ref[...] = dwp_ref[...].astype(BF16)

    row = lambda w: pl.BlockSpec((t, w), lambda i: (i, 0))
    full = lambda a, b: pl.BlockSpec((a, b), lambda i: (0, 0))
    in_specs = [row(D_MODEL), full(1, D_MODEL), _resident((D_MODEL, D_MODEL)),
                pl.BlockSpec((None, t, PLE_DIM), lambda i: (layer, i, 0)), _resident((PLE_DIM, D_MODEL))]
    if follow is not None:
        next_in, next_out, next_shape, scratch = _normproj_operands(s, t)
        return pl.pallas_call(
            body, name=name, grid=(s // t,), in_specs=in_specs + next_in,
            out_specs=[row(D_MODEL), row(D_MODEL), row(D_MODEL), row(PLE_DIM)] + next_out,
            out_shape=[jax.ShapeDtypeStruct((s, D_MODEL), F32), jax.ShapeDtypeStruct((s, D_MODEL), BF16),
                       jax.ShapeDtypeStruct((s, D_MODEL), BF16), jax.ShapeDtypeStruct((s, PLE_DIM), BF16)] + next_shape,
            scratch_shapes=scratch, compiler_params=_params(1),
        )(h, g, w_gate, p, w_ple, *follow)
    loss, dgf, dh2, dg, dwg, dwgb, dwp, dwpb = pl.pallas_call(
        body, name=name, grid=(s // t,), in_specs=in_specs + [full(1, D_MODEL), row(D_MODEL)],
        out_specs=[pl.BlockSpec((1, LANES), lambda i: (0, 0)), full(1, D_MODEL), row(D_MODEL), full(1, D_MODEL),
                   full(D_MODEL, D_MODEL), full(D_MODEL, D_MODEL), full(PLE_DIM, D_MODEL), full(PLE_DIM, D_MODEL)],
        out_shape=[jax.ShapeDtypeStruct((1, LANES), F32), jax.ShapeDtypeStruct((1, D_MODEL), F32),
                   jax.ShapeDtypeStruct((s, D_MODEL), F32), jax.ShapeDtypeStruct((1, D_MODEL), F32),
                   jax.ShapeDtypeStruct((D_MODEL, D_MODEL), F32), jax.ShapeDtypeStruct((D_MODEL, D_MODEL), BF16),
                   jax.ShapeDtypeStruct((PLE_DIM, D_MODEL), F32), jax.ShapeDtypeStruct((PLE_DIM, D_MODEL), BF16)],
        compiler_params=_params(1),
    )(h, g, w_gate, p, w_ple, *head)
    return loss, dgf, dh2, dg, (dwg, dwgb), (dwp, dwpb)


def _gate_bwd(dh, gate, pb, w_ple, h, g, w_gate, hn, name, after=()):
    s = h.shape[0]
    t = _row_tile(s, FWD_TILE)
    last = s // t - 1

    def body(dh_ref, gate_ref, pb_ref, wp_ref, h_ref, g_ref, wg_ref, hn_ref, out_ref, dg_ref, dwg_ref, dwgb_ref,
             dwp_ref, dwpb_ref):
        i = pl.program_id(0)

        @pl.when(i == 0)
        def _():
            dg_ref[...] = jnp.zeros_like(dg_ref)
            dwg_ref[...] = jnp.zeros_like(dwg_ref)
            dwp_ref[...] = jnp.zeros_like(dwp_ref)

        d = dh_ref[...]
        gate = gate_ref[...].astype(F32)
        pb = pb_ref[...]
        e = _dot(pb, wp_ref[...])
        dgl = (d * e * gate * (1.0 - gate)).astype(BF16)
        dwg_ref[...] += _dot_tn(hn_ref[...], dgl)
        dwp_ref[...] += _dot_tn(pb, (d * gate).astype(BF16))
        gv = g_ref[...]
        n, rstd, _ = _rms(h_ref[...], gv)
        dx, dg = _rms_bwd(_dot_nt(dgl, wg_ref[...]), n, rstd, gv)
        out_ref[...] = d + dx
        dg_ref[...] += dg

        @pl.when(i == last)
        def _():
            dwgb_ref[...] = dwg_ref[...].astype(BF16)
            dwpb_ref[...] = dwp_ref[...].astype(BF16)

    row = lambda w: pl.BlockSpec((t, w), lambda i: (i, 0))
    full = lambda a, b: pl.BlockSpec((a, b), lambda i: (0, 0))
    dh2, dg, dwg, dwgb, dwp, dwpb = pl.pallas_call(
        _ordered_after(body, 8, after), name=name, grid=(s // t,),
        in_specs=[row(D_MODEL), row(D_MODEL), row(PLE_DIM), _resident((PLE_DIM, D_MODEL)), row(D_MODEL),
                  full(1, D_MODEL), _resident((D_MODEL, D_MODEL)), row(D_MODEL)]
        + [pl.BlockSpec(memory_space=pl.ANY)] * len(after),
        out_specs=[row(D_MODEL), full(1, D_MODEL), full(D_MODEL, D_MODEL), full(D_MODEL, D_MODEL),
                   full(PLE_DIM, D_MODEL), full(PLE_DIM, D_MODEL)],
        out_shape=[jax.ShapeDtypeStruct((s, D_MODEL), F32), jax.ShapeDtypeStruct((1, D_MODEL), F32),
                   jax.ShapeDtypeStruct((D_MODEL, D_MODEL), F32), jax.ShapeDtypeStruct((D_MODEL, D_MODEL), BF16),
                   jax.ShapeDtypeStruct((PLE_DIM, D_MODEL), F32), jax.ShapeDtypeStruct((PLE_DIM, D_MODEL), BF16)],
        compiler_params=_params(1),
    )(dh, gate, pb, w_ple, h, g, w_gate, hn, *after)
    return dh2, dg, (dwg, dwgb), (dwp, dwpb)


def _mlp_bwd(dh, r, h, g, w_up, w_down, name):
    s = h.shape[0]
    t = _row_tile(s, MLP_BWD_TILE)
    nblk = D_FF // FF_BLOCK

    def body(dh_ref, r_ref, h_ref, g_ref, wu_ref, wd_ref, out_ref, dup_ref, dg_ref, dhb_ref):
        @pl.when(pl.program_id(0) == 0)
        def _():
            dg_ref[...] = jnp.zeros_like(dg_ref)

        d = dh_ref[...]
        db = d.astype(BF16)
        dhb_ref[...] = db
        back = None
        for b in range(nblk):
            cols = slice(b * FF_BLOCK, (b + 1) * FF_BLOCK)
            dup = (_dot_nt(db, wd_ref[b]) * (2.0 * r_ref[:, cols].astype(F32))).astype(BF16)
            dup_ref[:, cols] = dup
            part = _dot_nt(dup, wu_ref[b])
            back = part if back is None else back + part
        gv = g_ref[...]
        n, rstd, _ = _rms(h_ref[...], gv)
        dx, dg = _rms_bwd(back, n, rstd, gv)
        out_ref[...] = d + dx
        dg_ref[...] += dg

    row = lambda w: pl.BlockSpec((t, w), lambda i: (i, 0))
    vec = pl.BlockSpec((1, D_MODEL), lambda i: (0, 0))
    resident = lambda shape: pl.BlockSpec(shape, lambda i: (0, 0, 0), pipeline_mode=pl.Buffered(1))
    return pl.pallas_call(
        body, name=name, grid=(s // t,),
        in_specs=[row(D_MODEL), row(D_FF), row(D_MODEL), vec,
                  resident((nblk, D_MODEL, FF_BLOCK)), resident((nblk, FF_BLOCK, D_MODEL))],
        out_specs=[row(D_MODEL), row(D_FF), vec, row(D_MODEL)],
        out_shape=[jax.ShapeDtypeStruct((s, D_MODEL), F32), jax.ShapeDtypeStruct((s, D_FF), BF16),
                   jax.ShapeDtypeStruct((1, D_MODEL), F32), jax.ShapeDtypeStruct((s, D_MODEL), BF16)],
        compiler_params=_params(1),
    )(dh, r, h, g, w_up, w_down)


def _outproj_bwd(dh, w_out, o, lse, ones_bd, a, name):
    s = dh.shape[0]
    t = _row_tile(s, 512)
    last = s // t - 1

    def body(dh_ref, w_ref, o0, o1, o2, l0, l1, l2, bd_ref, a_ref, dp_ref, do0, do1, do2, de0, de1, de2, dw_ref,
             dwb_ref, *stages):
        i = pl.program_id(0)

        @pl.when(i == 0)
        def _():
            dw_ref[...] = jnp.zeros_like(dw_ref)

        stages = _pair_stages(stages)
        dhb = dh_ref[...].astype(BF16)
        dw_ref[...] += _dot_tn(a_ref[...], dhb)

        @pl.when(i == last)
        def _():
            dwb_ref[...] = dw_ref[...].astype(BF16)

        da = _dot_nt(dhb, w_ref[...])
        dp_ref[...] = da[:, 0:POOL_WIDTH]
        ov =[_from_residues(r, stages[i], DILATIONS[i]) for i, r in enumerate((o0, o1, o2))]
        lv = [_from_residues(r, stages[3 + i], DILATIONS[i]) for i, r in enumerate((l0, l1, l2))]
        wts = _group_weights(*lv)
        bd = bd_ref[...]
        cbar = jnp.zeros((t, GROUP_WIDTH), F32)
        for grp, do_ref in enumerate((do0, do1, do2)):
            lo = POOL_WIDTH + grp * GROUP_WIDTH
            dag = da[:, lo:lo + GROUP_WIDTH]
            _to_residues(dag * wts[grp], stages[6 + grp], do_ref, DILATIONS[grp])
            prod = dag * ov[grp]
            hi = prod.astype(BF16)
            low = (prod - hi.astype(F32)).astype(BF16)
            cbar = cbar + wts[grp] * (_dot(hi, bd) + _dot(low, bd))
        for grp, de_ref in enumerate((de0, de1, de2)):
            _to_residues(wts[grp] * cbar, stages[9 + grp], de_ref, DILATIONS[grp])

    row = lambda w: pl.BlockSpec((t, w), lambda i: (i, 0))
    full = lambda a, b: pl.BlockSpec((a, b), lambda i: (0, 0))
    res = [_residue_spec(dil, t) for dil in DILATIONS]
    *outs, dw, dwb = pl.pallas_call(
        body, name=name, grid=(s // t,),
        in_specs=[row(D_MODEL), full(D_MODEL, D_MODEL)] + res + res + [full(GROUP_WIDTH, GROUP_WIDTH), row(D_MODEL)],
        out_specs=[row(POOL_WIDTH)] + res + res + [full(D_MODEL, D_MODEL)] * 2,
        out_shape=[jax.ShapeDtypeStruct((s, POOL_WIDTH), F32)] + [_residue_shape(dil, s, BF16) for dil in DILATIONS]
        + [_residue_shape(dil, s, F32) for dil in DILATIONS]
        + [jax.ShapeDtypeStruct((D_MODEL, D_MODEL), F32), jax.ShapeDtypeStruct((D_MODEL, D_MODEL), BF16)],
        scratch_shapes=_stages(t, 12),
        compiler_params=_params(1),
    )(dh, w_out, *o, *lse, ones_bd, a)
    return (*outs, (dw, dwb))


def _attn_bwd(q, k, v, do, lse, deff, name, after=()):
    dil, length, _ = q.shape
    nb = length // ATTN_BLOCK
    qb = _blocks_per_step(nb)
    nj = nb // qb
    rs = _residues_per_step(dil, nb, qb)
    whole = nj == 1
    tail = slice((qb - 1) * ATTN_BLOCK, qb * ATTN_BLOCK)
    block = lambda qi: slice(qi * ATTN_BLOCK, (qi + 1) * ATTN_BLOCK)

    def body(q_ref, kp_ref, kc_ref, vp_ref, vc_ref, do_ref, lse_ref, de_ref, dq_ref, dk_ref, dv_ref, ck, cv):
        j = pl.program_id(1)

        def compute():
            masks = _head_masks()
            bias = _band_bias(j == 0)
            for rr in range(rs):
                dkc, dvc = [], []
                for qi in range(qb):
                    here, before = block(qi), block(qi - 1)
                    kcat = jnp.concatenate([kp_ref[rr] if qi == 0 else kc_ref[rr, before], kc_ref[rr, here]], axis=0)
                    vcat = jnp.concatenate([vp_ref[rr] if qi == 0 else vc_ref[rr, before], vc_ref[rr, here]], axis=0)
                    qs = _stack_heads(q_ref[rr, here], masks)
                    dos = _stack_heads(do_ref[rr, here], masks)
                    sc = _dot_nt(qs, kcat) + bias[min(qi, 1)]
                    p = jnp.exp(sc - _column_per_head(lse_ref[rr, here]))
                    ds = (p * (_dot_nt(dos, vcat) - _column_per_head(de_ref[rr, here]))).astype(BF16)
                    dq = jnp.zeros((ATTN_BLOCK, GROUP_WIDTH), F32)
                    for hd, msk in enumerate(masks):
                        dq = jnp.where(msk, _dot(ds[block(hd)], kcat), dq)
                    dq_ref[rr, here] = dq.astype(dq_ref.dtype)
                    dkc.append(_dot_tn(ds, qs))
                    dvc.append(_dot_tn(p.astype(BF16), dos))

                for out_ref, carry, parts in ((dk_ref, ck, dkc), (dv_ref, cv, dvc)):
                    full = [parts[qi][ATTN_BLOCK:] + parts[qi + 1][0:ATTN_BLOCK] for qi in range(qb - 1)]
                    if whole:
                        for qi, val in enumerate(full + [parts[qb - 1][ATTN_BLOCK:]]):
                            out_ref[rr, block(qi)] = val.astype(out_ref.dtype)
                        continue

                    @pl.when(j > 0)
                    def _():
                        if qb > 1:
                            out_ref[0, 0:(qb - 1) * ATTN_BLOCK] = carry[0:(qb - 1) * ATTN_BLOCK].astype(out_ref.dtype)
                        out_ref[0, tail] = (carry[tail] + parts[0][0:ATTN_BLOCK]).astype(out_ref.dtype)

                    for qi, val in enumerate(full):
                        carry[block(qi)] = val
                    carry[tail] = parts[qb - 1][ATTN_BLOCK:]

        if whole:
            compute()
        else:
            pl.when(j < nj)(compute)

            @pl.when(j == nj)
            def _():
                dk_ref[0] = ck[...].astype(dk_ref.dtype)
                dv_ref[0] = cv[...].astype(dv_ref.dtype)

    step = lambda j: jnp.minimum(j, nj - 1)
    cur = pl.BlockSpec((rs, qb * ATTN_BLOCK, GROUP_WIDTH), lambda r, j: (r, step(j), 0))
    prev = pl.BlockSpec((rs, ATTN_BLOCK, GROUP_WIDTH), lambda r, j: (r, jnp.maximum(qb * step(j) - 1, 0), 0))
    late = pl.BlockSpec((rs, qb * ATTN_BLOCK, GROUP_WIDTH), lambda r, j: (r, jnp.maximum(j - 1, 0), 0))
    return pl.pallas_call(
        _ordered_after(body, 8, after), name=name, grid=(dil // rs, 1 if whole else nj + 1),
        in_specs=[cur, prev, cur, prev, cur, cur, cur, cur] + [pl.BlockSpec(memory_space=pl.ANY)] * len(after),
        out_specs=[cur, cur if whole else late, cur if whole else late],
        out_shape=[jax.ShapeDtypeStruct(q.shape, BF16)] * 3,
        scratch_shapes=[pltpu.VMEM((qb * ATTN_BLOCK, GROUP_WIDTH), F32)] * 2,
        compiler_params=_params(2),
    )(q, k, k, v, v, do, lse, deff, *after)


def _pool_bwd(dpool, y, w_bd, scale, name, after=()):
    s = dpool.shape[0]
    t = _row_tile(s, 512)
    nt = s // t

    def body(dp_ref, y_ref, w_ref, sc_ref, du_ref, dw_ref, dsc_ref, ext, b2, b4, b8):
        i = pl.program_id(0)

        @pl.when(i == 0)
        def _():
            ext[t:, :] = jnp.zeros((POOL_HALO + POOL_PAD, POOL_WIDTH), F32)
            for buf in (b2, b4):
                buf[t + POOL_HALO:, :] = jnp.zeros((POOL_PAD, POOL_WIDTH), F32)
            dw_ref[...] = jnp.zeros_like(dw_ref)
            dsc_ref[...] = jnp.zeros_like(dsc_ref)

        dp = dp_ref[...]
        yb = y_ref[...]
        w = w_ref[...]
        dsc_ref[...] += jnp.sum(dp * _dot(yb, w), axis=0, keepdims=True)
        dyo = (dp * sc_ref[...]).astype(BF16)
        dw_ref[...] += _dot_tn(yb, dyo)
        dy = _dot_nt(dyo, w)
        win = _pool_lane_window()
        pos = (nt - 1 - i) * t + lax.broadcasted_iota(jnp.int32, (t, POOL_WIDTH), 0)
        gq = dy / jnp.minimum(pos + 1, win).astype(F32)
        ext[0:t, :] = gq
        du_ref[...] = _window_sums(ext, b2, b4, b8, t, 0, 0, 1) - dy
        ext[t:t + POOL_HALO, :] = gq[0:POOL_HALO, :]

    rev = pl.BlockSpec((t, POOL_WIDTH), lambda i: (nt - 1 - i, 0))
    full = lambda a, b: pl.BlockSpec((a, b), lambda i: (0, 0))
    return pl.pallas_call(
        _ordered_after(body, 4, after), name=name, grid=(nt,),
        in_specs=[rev, rev, full(POOL_WIDTH, POOL_WIDTH), full(1, POOL_WIDTH)]
        + [pl.BlockSpec(memory_space=pl.ANY)] * len(after),
        out_specs=[rev, full(POOL_WIDTH, POOL_WIDTH), full(1, POOL_WIDTH)],
        out_shape=[jax.ShapeDtypeStruct((s, POOL_WIDTH), F32), jax.ShapeDtypeStruct((POOL_WIDTH, POOL_WIDTH), F32),
                   jax.ShapeDtypeStruct((1, POOL_WIDTH), F32)],
        scratch_shapes=[pltpu.VMEM((t + POOL_HALO + POOL_PAD, POOL_WIDTH), F32)] * 4,
        compiler_params=_params(1),
    )(dpool, y, w_bd, scale, *after)


def _normproj_bwd(dh, du, dq, dk, dv, rc, rsa, rsb, w_in, h, g, name):
    s = h.shape[0]
    t = _row_tile(s, 512)

    def body(dh_ref, du_ref, q0, q1, q2, k0, k1, k2, v0, v1, v2, c_ref, sa_ref, sb_ref, w_ref, h_ref, g_ref,
             out_ref, dz_ref, dg_ref, *stages):
        @pl.when(pl.program_id(0) == 0)
        def _():
            dg_ref[...] = jnp.zeros_like(dg_ref)

        c, sa, sb = c_ref[...], sa_ref[...], sb_ref[...]

        def unrot(a, scale):
            halves = [_rot_t(a[:, hf * LANES:(hf + 1) * LANES] * scale, c, sa, sb) for hf in range(2)]
            return jnp.concatenate(halves, axis=1)

        staged = _pair_stages(stages)
        tok = lambda refs, base: [_from_residues(r, staged[base + i], DILATIONS[i]) for i, r in enumerate(refs)]
        chunks = [du_ref[...]]
        chunks += [unrot(a, HEAD_DIM ** -0.5) for a in tok((q0, q1, q2), 0)]
        chunks += [unrot(a, 1.0) for a in tok((k0, k1, k2), 3)]
        chunks += tok((v0, v1, v2), 6)
        acc = jnp.zeros((t, D_MODEL), F32)
        for ci, ch in enumerate(chunks):
            cols = slice(ci * GROUP_WIDTH, (ci + 1) * GROUP_WIDTH)
            cb = ch.astype(BF16)
            dz_ref[:, cols] = cb
            acc = acc + _dot(cb, w_ref[cols, :])
        gv = g_ref[...]
        n, rstd, _ = _rms(h_ref[...], gv)
        dx, dg = _rms_bwd(acc, n, rstd, gv)
        out_ref[...] = dh_ref[...] + dx
        dg_ref[...] += dg

    row = lambda w: pl.BlockSpec((t, w), lambda i: (i, 0))
    vec = pl.BlockSpec((1, D_MODEL), lambda i: (0, 0))
    res = [_residue_spec(dil, t) for dil in DILATIONS]
    return pl.pallas_call(
        body, name=name, grid=(s // t,),
        in_specs=[row(D_MODEL), row(POOL_WIDTH)] + res * 3 + _table_specs(t)
        + [pl.BlockSpec((N_IN, D_MODEL), lambda i: (0, 0)), row(D_MODEL), vec],
        out_specs=[row(D_MODEL), row(N_IN), vec],
        out_shape=[jax.ShapeDtypeStruct((s, D_MODEL), F32), jax.ShapeDtypeStruct((s, N_IN), BF16),
                   jax.ShapeDtypeStruct((1, D_MODEL), F32)],
        scratch_shapes=_stages(t, 9),
        compiler_params=_params(1),
    )(dh, du, *dq, *dk, *dv, rc, rsa, rsb, w_in, h, g)


def _matmul_tn(a, b, name, *, square_a=False, tm=None, tn=None, blocked_out=False, after=()):
    s, m = a.shape
    n = b.shape[1]
    tk = _row_tile(s, 2048)
    tm = tm or min(m, 1024)
    tn = tn or min(n, 1024)
    assert m % tm == 0 and n % tn == 0
    nk = s // tk
    nsub = tn // FF_BLOCK if blocked_out else 1

    def body(a_ref, b_ref, o_ref, ob_ref, acc):
        k = pl.program_id(2)

        def product():
            av = a_ref[...]
            if square_a:
                av = av.astype(F32)
                av = av * av
            return _dot_tn(av.astype(BF16), b_ref[...].astype(BF16))

        def emit(total):
            if blocked_out:
                for sub in range(nsub):
                    cols = slice(sub * FF_BLOCK, (sub + 1) * FF_BLOCK)
                    o_ref[sub] = total[:, cols]
                    ob_ref[sub] = total[:, cols].astype(BF16)
            else:
                o_ref[...] = total
                ob_ref[...] = total.astype(BF16)

        if nk == 1:
            emit(product())
            return

        @pl.when(k == 0)
        def _():
            acc[...] = product()

        @pl.when((k > 0) & (k < nk - 1))
        def _():
            acc[...] += product()

        @pl.when(k == nk - 1)
        def _():
            emit(acc[...] + product())

    if blocked_out:
        shape = (n // FF_BLOCK, m, FF_BLOCK)
        out_spec = pl.BlockSpec((nsub, tm, FF_BLOCK), lambda i, j, k: (j, i, 0))
    else:
        shape = (m, n)
        out_spec = pl.BlockSpec((tm, tn), lambda i, j, k: (i, j))
    return pl.pallas_call(
        _ordered_after(body, 2, after), name=name, grid=(m // tm, n // tn, nk),
        in_specs=[pl.BlockSpec((tk, tm), lambda i, j, k: (k, i)), pl.BlockSpec((tk, tn), lambda i, j, k: (k, j))]
        + [pl.BlockSpec(memory_space=pl.ANY)] * len(after),
        out_specs=[out_spec, out_spec],
        out_shape=[jax.ShapeDtypeStruct(shape, F32), jax.ShapeDtypeStruct(shape, BF16)],
        scratch_shapes=[pltpu.VMEM((tm, tn), F32)],
        compiler_params=_params(3),
    )(a, b, *after)


def _adamw_math(w, g, m, v):
    m = ADAM_B1 * m + (1.0 - ADAM_B1) * g
    v = ADAM_B2 * v + (1.0 - ADAM_B2) * (g * g)
    m_hat = m / (1.0 - ADAM_B1 ** ADAM_STEP)
    v_hat = v / (1.0 - ADAM_B2 ** ADAM_STEP)
    delta = -ADAM_LR * (m_hat / (jnp.sqrt(v_hat) + ADAM_EPS) + ADAM_WD * w)
    return delta, m, v


def _sum_chunks_body(own0_ref, own1_ref, r0_ref, r1_ref):
    layer0 = pl.program_id(0) == 0
    g = jnp.where(layer0, own0_ref[...], own1_ref[...])
    for k in range(N_DEV - 1):
        g = g + jnp.where(layer0, r0_ref[k], r1_ref[k]).astype(F32)
    return g


def _chunk_specs(t, cols):
    rows_of = lambda layer: (lambda l, i: jnp.where(l == layer, i, 0))
    blk = pl.BlockSpec((None, t, cols), lambda l, i, me: (l, i, 0))
    own = [pl.BlockSpec((None, t, cols), functools.partial(lambda l, i, me, pick: (me[0], pick(l, i), 0), pick=rows_of(ly)))
           for ly in range(2)]
    recv = [pl.BlockSpec((N_DEV - 1, t, cols), functools.partial(lambda l, i, me, pick: (0, pick(l, i), 0), pick=rows_of(ly)))
            for ly in range(2)]
    return blk, own + recv


def _sum_chunks(chunks, me, name):
    _, rows, cols = chunks[0].shape
    t = _row_tile(rows, 320)

    def body(me_ref, own0_ref, own1_ref, r0_ref, r1_ref, g_ref):
        g_ref[...] = _sum_chunks_body(own0_ref, own1_ref, r0_ref, r1_ref)

    blk, chunk_specs = _chunk_specs(t, cols)
    return pl.pallas_call(
        body, name=name,
        grid_spec=pltpu.PrefetchScalarGridSpec(num_scalar_prefetch=1, grid=(2, rows // t), in_specs=chunk_specs,
                                               out_specs=blk),
        out_shape=jax.ShapeDtypeStruct((2, rows, cols), F32), compiler_params=_params(2),
    )(me, *chunks)


def _adamw_sharded(w, m, v, grad, me, name):
    _, rows, cols = w.shape
    t = _row_tile(rows, 256)
    summed = not isinstance(grad, tuple)
    grad = (grad,) if summed else grad

    def body(me_ref, w_ref, m_ref, v_ref, *refs):
        g_ref, d_ref, nm_ref, nv_ref = refs[-4:]
        g = refs[0][...] if summed else _sum_chunks_body(*refs[:4])
        g_ref[...] = g
        d_ref[...], nm_ref[...], nv_ref[...] = _adamw_math(w_ref[...], g, m_ref[...], v_ref[...])

    blk, chunk_specs = _chunk_specs(t, cols)
    return pl.pallas_call(
        body, name=name,
        grid_spec=pltpu.PrefetchScalarGridSpec(
            num_scalar_prefetch=1, grid=(2, rows // t),
            in_specs=[blk, blk, blk] + ([blk] if summed else chunk_specs), out_specs=[blk] * 4),
        out_shape=[jax.ShapeDtypeStruct(w.shape, F32)] * 4,
        compiler_params=_params(2),
    )(me, w, m, v, *grad)


def _adamw_packed(w, g8, m, v, name):
    def body(w_ref, g_ref, m_ref, v_ref, go_ref, d_ref, nm_ref, nv_ref):
        g = g_ref[0]
        for dev in range(1, N_DEV):
            g = g + g_ref[dev]
        go_ref[...] = g
        d_ref[...], nm_ref[...], nv_ref[...] = _adamw_math(w_ref[...], g, m_ref[...], v_ref[...])

    return pl.pallas_call(
        body, name=name, out_shape=[jax.ShapeDtypeStruct(w.shape, F32)] * 4,
        compiler_params=pltpu.CompilerParams(vmem_limit_bytes=VMEM_LIMIT),
    )(w, g8, m, v)


def _peer(k):
    x, y, c = lax.axis_index("x"), lax.axis_index("y"), lax.axis_index("c")
    return (1 - x if k & 4 else x, 1 - y if k & 2 else y, 1 - c if k & 1 else c)


def _linear(dev):
    return 4 * dev[0] + 2 * dev[1] + dev[2]


HBM_SPEC = pl.BlockSpec(memory_space=pltpu.HBM)
SEM_SPEC = pl.BlockSpec(memory_space=pltpu.SEMAPHORE)
ANY_SPEC = pl.BlockSpec(memory_space=pl.ANY)
EFFECT = pltpu.SideEffectType.DATAFLOW_SIDE_EFFECTING


def _in_hbm(a):
    return pltpu.with_memory_space_constraint(a, pltpu.HBM)


class _Exchange:
    def __init__(self, name, groups, scatter, after=()):
        self.name, self.scatter = name, scatter
        self.sizes = sizes = [len(g) for g in groups]
        srcs = [a for g in groups for a in g]
        n, ng = len(srcs), len(groups)
        lead = (N_DEV - 1,) if scatter else (N_DEV,)
        shapes = [lead + (a.shape[1:] if scatter else a.shape) for a in srcs]
        lands = [lax.empty(sh, a.dtype) for sh, a in zip(shapes, srcs)]
        offsets = [sum(sizes[:gi]) for gi in range(ng)]
        copy = self._copy

        def body(*refs):
            src, land = refs[:n], refs[n:2 * n]
            sems = refs[2 * n + len(after):2 * n + len(after) + 2 * ng]
            token = refs[-1]
            for gi in range(ng):
                for wi in range(sizes[gi]):
                    w = offsets[gi] + wi
                    for k in range(1, N_DEV):
                        copy(src[w], land[w], sems[2 * gi], sems[2 * gi + 1], wi, k).start()
            token[...] = jnp.zeros_like(token)

        sem_shapes = [pltpu.SemaphoreType.DMA(((N_DEV - 1) * sz,)) for sz in sizes for _ in range(2)]
        outs = pl.pallas_call(
            body, name=name + "_start",
            in_specs=[HBM_SPEC] * (2 * n) + [ANY_SPEC] * len(after),
            out_specs=[SEM_SPEC] * (2 * ng) + [HBM_SPEC] * (2 * n) + [pl.BlockSpec(memory_space=pltpu.VMEM)],
            out_shape=sem_shapes + [pltpu.HBM(a.shape, a.dtype) for a in srcs + lands]
            + [jax.ShapeDtypeStruct((8, LANES), F32)],
            input_output_aliases={i: 2 * ng + i for i in range(2 * n)},
            compiler_params=pltpu.CompilerParams(has_side_effects=EFFECT),
        )(*[_in_hbm(a) for a in srcs + lands], *after)
        self.sems = [outs[2 * gi:2 * gi + 2] for gi in range(ng)]
        thru = outs[2 * ng:2 * ng + 2 * n]
        self.srcs = [thru[offsets[gi]:offsets[gi] + sizes[gi]] for gi in range(ng)]
        self.lands = [thru[n + offsets[gi]:n + offsets[gi] + sizes[gi]] for gi in range(ng)]
        self.token = outs[-1]

    def _copy(self, src, land, send_sems, recv_sems, wi, k):
        to = _peer(k)
        if self.scatter:
            src_ref, dst_ref = src.at[_linear(to)], land.at[k - 1]
        else:
            src_ref, dst_ref = src, land.at[_linear(_peer(0))]
        return pltpu.make_async_remote_copy(
            src_ref=src_ref, dst_ref=dst_ref, send_sem=send_sems.at[(N_DEV - 1) * wi + k - 1],
            recv_sem=recv_sems.at[(N_DEV - 1) * wi + k - 1], device_id=to, device_id_type=MESH)

    def wait(self, gi, after):
        n = self.sizes[gi]
        copy = self._copy

        def body(*refs):
            src, land = refs[:n], refs[n:2 * n]
            send_sems, recv_sems = refs[2 * n], refs[2 * n + 1]
            for wi in range(n):
                for k in range(1, N_DEV):
                    cp = copy(src[wi], land[wi], send_sems, recv_sems, wi, k)
                    cp.wait_send()
                    cp.wait_recv()

        arrays = list(self.srcs[gi]) + list(self.lands[gi])
        outs = pl.pallas_call(
            body, name=f"{self.name}_wait{gi}",
            in_specs=[HBM_SPEC] * (2 * n) + [SEM_SPEC, SEM_SPEC] + [ANY_SPEC] * len(after),
            out_specs=[HBM_SPEC] * (2 * n),
            out_shape=[pltpu.HBM(a.shape, a.dtype) for a in arrays],
            input_output_aliases={i: i for i in range(2 * n)},
            compiler_params=pltpu.CompilerParams(has_side_effects=EFFECT),
        )(*arrays, *self.sems[gi], *after)
        return outs[:n], outs[n:]


def _rotary_tables(positions):
    rot_dim = HEAD_DIM // 4
    inv_freq = ROPE_THETA ** (-jnp.arange(0, rot_dim, 2, dtype=F32) / rot_dim)
    ang = positions.astype(F32)[:, None] * inv_freq
    cs = jnp.concatenate([jnp.cos(ang), jnp.sin(ang)], axis=1)
    dim = jnp.arange(LANES) % HEAD_DIM
    first, second = dim < ROT_SHIFT, (dim >= ROT_SHIFT) & (dim < rot_dim)
    src = jnp.arange(2 * ROT_SHIFT)[:, None]
    angle = (dim % ROT_SHIFT)[None, :]
    c = jnp.where((first | second)[None, :] & (src == angle), 1.0, 0.0)
    sa = jnp.where(second[None, :] & (src == angle + ROT_SHIFT), 1.0, 0.0)
    sb = jnp.where(first[None, :] & (src == angle + ROT_SHIFT), -1.0, 0.0)
    spread = jnp.concatenate([c, sa, sb], axis=1).astype(F32)
    base = jnp.concatenate([jnp.where(first | second, 0.0, 1.0), jnp.zeros((2 * LANES,))]).astype(F32)[None, :]
    return jnp.dot(cs, spread, precision=lax.Precision.HIGHEST, preferred_element_type=F32) + base


def _block_diag(pool_w):
    gc = pool_w.shape[-1]
    out = jnp.zeros((POOL_WIDTH, POOL_WIDTH), pool_w.dtype)
    for grp in range(pool_w.shape[0]):
        out = lax.dynamic_update_slice(out, pool_w[grp], (grp * gc, grp * gc))
    return out


def _diag_blocks(a):
    gc = POOL_WIDTH // len(POOL_WINDOWS)
    return jnp.stack([a[grp * gc:(grp + 1) * gc, grp * gc:(grp + 1) * gc] for grp in range(len(POOL_WINDOWS))])


def _local_step(x, p, positions, loss_target, norm1, pool_w, pool_scale, norm2, norm3, final_norm, weights, send):
    rc = rsa = rsb = _rotary_tables(positions)
    ones_bd = _block_diag(jnp.ones((4, HEAD_DIM, HEAD_DIM), BF16))
    saved = []
    h = x
    for i in range(2):
        tag = f"_l{i}"
        g1, g2, g3 = norm1[i:i + 1], norm2[i:i + 1], norm3[i:i + 1]
        w_bd = _block_diag(pool_w[i]).astype(BF16)
        scale = pool_scale[i:i + 1]
        if i == 0:
            w_in = weights(i, "in", (h, rc, w_bd))
            hn1, u, *qkv = _normproj_fwd(h, g1, w_in, rc, rsa, rsb, "normproj_fwd" + tag)
        else:
            w_in, (hn1, u, *qkv) = ahead
        qkv = [qkv[3 * grp:3 * grp + 3] for grp in range(3)]
        started = weights(i, "prefetch", (hn1,))
        o, lse = zip(*[_attn_fwd(*qkv[grp], f"attn_fwd{tag}_g{grp}", after=started) for grp in range(3)])
        w_out = weights(i, "out", o)
        h1, a, y = _outproj_fwd(h, u, w_bd, scale, o, lse, w_out, "outproj_fwd" + tag)
        w_up, w_down = weights(i, "mlp", (h1,))
        h2, hn2, r = _mlp_fwd(h1, g2, w_up, w_down, "mlp_fwd" + tag)
        w_gate, w_ple = weights(i, "gate", (h2,))
        h0 = h
        if i == 0:
            w_in_next = weights(1, "in", (h2,))
            h, hn3, gate, pb, *ahead = _gate_fwd(h2, g3, w_gate, p, i, w_ple, "gate_normproj_fwd",
                                                 follow=(norm1[1:2], w_in_next, rc, rsa, rsb))
            ahead = (w_in_next, ahead)
        else:
            hn3 = gate = pb = None
            loss, d_final, *top = _gate_fwd(h2, g3, w_gate, p, i, w_ple, "gate_loss_gate_bwd",
                                            head=(final_norm.reshape(1, D_MODEL), loss_target))
        saved.append(dict(h0=h0, hn1=hn1, qkv=qkv, y=y, o=o, lse=lse, a=a, h1=h1, hn2=hn2, r=r, h2=h2,
                          hn3=hn3, gate=gate, pb=pb, w_bd=w_bd, scale=scale, g1=g1, g2=g2, g3=g3,
                          w_in=w_in, w_out=w_out, w_up=w_up, w_down=w_down, w_gate=w_gate, w_ple=w_ple))

    grads = [None, None]
    sent = ()
    for i in (1, 0):
        tag = f"_l{i}"
        sv = saved[i]
        if i == 1:
            dh2, dg3, dw_gate, dw_ple = top
        else:
            dh2, dg3, dw_gate, dw_ple = _gate_bwd(dh, sv["gate"], sv["pb"], sv["w_ple"], sv["h2"], sv["g3"],
                                                  sv["w_gate"], sv["hn3"], "gate_bwd" + tag, after=sent)
        dh1, dup, dg2, dh2b = _mlp_bwd(dh2, sv["r"], sv["h1"], sv["g2"], sv["w_up"], sv["w_down"], "mlp_bwd" + tag)
        dw_down = _matmul_tn(sv["r"], dh2b, "dw_down" + tag, square_a=True)
        dw_up = _matmul_tn(sv["hn2"], dup, "dw_up" + tag, blocked_out=True)
        dpool, do0, do1, do2, de0, de1, de2, dw_out = _outproj_bwd(dh1, sv["w_out"], sv["o"], sv["lse"], ones_bd,
                                                                   sv["a"], "outproj_bwd" + tag)
        sent = send(i, "main", dict(w_gate=dw_gate, w_ple=dw_ple, w_down=dw_down, w_up=dw_up, w_out=dw_out))
        dqkv = [_attn_bwd(*sv["qkv"][grp], do_g, sv["lse"][grp], de_g, f"attn_bwd{tag}_g{grp}", after=sent)
                for grp, (do_g, de_g) in enumerate(((do0, de0), (do1, de1), (do2, de2)))]
        dq, dk, dv = zip(*dqkv)
        du, dw_bd, dscale = _pool_bwd(dpool, sv["y"], sv["w_bd"], sv["scale"], "pool_bwd" + tag, after=sent)
        dh, dz, dg1 = _normproj_bwd(dh1, du, dq, dk, dv, rc, rsa, rsb, sv["w_in"], sv["h0"], sv["g1"],
                                    "normproj_bwd" + tag)
        grads[i] = dict(norm1=dg1, norm2=dg2, norm3=dg3, pool_w=_diag_blocks(dw_bd), pool_scale=dscale)
        small_sent = send(0, "small", (grads, d_final, loss)) if i == 0 else ()
        dw_in = _matmul_tn(dz, sv["hn1"], "dw_in" + tag, tm=N_IN // 2, after=small_sent)
        sent = send(i, "in", dict(w_in=dw_in))
    return dh, sent


def _pack_small(norm1, norm2, norm3, final_norm, pool_scale, pool_w, spare=None):
    spare = jnp.zeros((1, LANES), F32) if spare is None else spare
    scale_row = jnp.concatenate([pool_scale.reshape(1, 2 * POOL_WIDTH), spare,
                                 jnp.zeros((1, D_MODEL - 2 * POOL_WIDTH - LANES), F32)], axis=1)
    return jnp.concatenate([norm1, norm2, norm3, final_norm.reshape(1, D_MODEL), scale_row,
                            pool_w.reshape(32, D_MODEL)], axis=0)


def _unpack_small(a):
    return dict(norm1=a[0:2], norm2=a[2:4], norm3=a[4:6], final_norm=a[6], pool_scale=a[7, 0:2 * POOL_WIDTH].reshape(2, POOL_WIDTH),
                pool_w=a[8:40].reshape(2, 4, HEAD_DIM, HEAD_DIM))


def _chunks_cols(a, cols):
    return a.reshape(a.shape[0], N_DEV, cols).transpose(1, 0, 2)


def _chunks_rows(a, rows):
    return a.reshape(N_DEV, rows, a.shape[1])


BIG = ("w_in", "w_out", "w_up", "w_down", "w_gate", "w_ple")
SMALL = ("norm1", "norm2", "norm3", "final_norm", "pool_scale", "pool_w")
ORDER = ("norm1", "w_in", "pool_w", "pool_scale", "w_out", "norm2", "w_up", "w_down", "norm3", "w_gate", "w_ple",
         "final_norm")


def kernel(x, p, positions, norm1, w_in, pool_w, pool_scale, w_out, norm2, w_up, w_down, norm3, w_gate, w_ple, final_norm, loss_target, m_norm1, m_w_in, m_pool_w, m_pool_scale, m_w_out, m_norm2, m_w_up, m_w_down, m_norm3, m_w_gate, m_w_ple, m_final_norm, v_norm1, v_w_in, v_pool_w, v_pool_scale, v_w_out, v_norm2, v_w_up, v_w_down, v_norm3, v_w_gate, v_w_ple, v_final_norm):
    w = dict(norm1=norm1, w_in=w_in, pool_w=pool_w, pool_scale=pool_scale, w_out=w_out, norm2=norm2, w_up=w_up,
             w_down=w_down, norm3=norm3, w_gate=w_gate, w_ple=w_ple, final_norm=final_norm)
    m = dict(norm1=m_norm1, w_in=m_w_in, pool_w=m_pool_w, pool_scale=m_pool_scale, w_out=m_w_out, norm2=m_norm2,
             w_up=m_w_up, w_down=m_w_down, norm3=m_norm3, w_gate=m_w_gate, w_ple=m_w_ple, final_norm=m_final_norm)
    v = dict(norm1=v_norm1, w_in=v_w_in, pool_w=v_pool_w, pool_scale=v_pool_scale, w_out=v_w_out, norm2=v_norm2,
             w_up=v_w_up, w_down=v_w_down, norm3=v_norm3, w_gate=v_w_gate, w_ple=v_w_ple, final_norm=v_final_norm)
    seq = x.shape[1]

    bf = {n: [w[n][layer].astype(BF16) for layer in range(2)] for n in BIG}
    bf["w_in"] = [a.T for a in bf["w_in"]]
    me = 4 * lax.axis_index("x") + 2 * lax.axis_index("y") + lax.axis_index("c")
    parts = dict(zip(("in", "out", "mlp", "gate"), (("w_in",), ("w_out",), ("w_up", "w_down"), ("w_gate", "w_ple"))))
    first = _Exchange("gather_first", [[bf["w_in"][0]]], scatter=False)
    later = [pt for pt in parts if pt != "in"]
    gathers = [_Exchange("gather_l0", [[bf[n][0] for n in parts[pt]] for pt in later], scatter=False,
                         after=(first.token,))]
    unpack = dict(w_in=lambda a: a.reshape(N_IN, D_MODEL),
                  w_out=lambda a: a.reshape(D_MODEL, D_MODEL), w_gate=lambda a: a.reshape(D_MODEL, D_MODEL),
                  w_ple=lambda a: a.transpose(1, 0, 2).reshape(PLE_DIM, D_MODEL), w_up=lambda a: a, w_down=lambda a: a)

    def weights(layer, part, after):
        if part == "prefetch":
            if layer != 0:
                return ()
            gathers.append(_Exchange("gather_l1", [[bf[n][1] for n in parts[pt]] for pt in parts], scatter=False,
                                     after=after))
            return (gathers[1].token,)
        if layer == 0 and part == "in":
            shards, lands = first.wait(0, (*after, gathers[0].token))
        elif layer == 0:
            shards, lands = gathers[0].wait(later.index(part), after)
        else:
            shards, lands = gathers[1].wait(tuple(parts).index(part), after)
        full = [unpack[n](lax.dynamic_update_slice_in_dim(land, shard[None], me, axis=0))
                for n, shard, land in zip(parts[part], shards, lands)]
        return full if len(full) > 1 else full[0]

    to_chunks = dict(w_in=lambda a: _chunks_rows(a, N_IN // N_DEV),
                     w_out=lambda a: _chunks_rows(a, D_MODEL // N_DEV),
                     w_up=lambda a: a, w_down=lambda a: _chunks_rows(a, FF_BLOCK),
                     w_gate=lambda a: _chunks_rows(a, D_MODEL // N_DEV), w_ple=lambda a: _chunks_cols(a, D_MODEL // N_DEV))
    own = {n: [None, None] for n in BIG}
    scatters = {}

    def send(layer, part, grads):
        if part == "small":
            per_layer, d_final, loss = grads
            pack = _pack_small(
                *[jnp.concatenate([per_layer[0][n], per_layer[1][n]], axis=0) for n in ("norm1", "norm2", "norm3")],
                d_final.reshape(D_MODEL),
                jnp.concatenate([per_layer[0]["pool_scale"], per_layer[1]["pool_scale"]], axis=0),
                jnp.stack([per_layer[0]["pool_w"], per_layer[1]["pool_w"]]), spare=loss)
            scatters["small"] = _Exchange("gather_small", [[pack]], scatter=False)
            return (scatters["small"].token,)
        for n, (g32, _) in grads.items():
            own[n][layer] = to_chunks[n](g32)
        ex = _Exchange(f"scatter_{part}_l{layer}", [[to_chunks[n](g16) for n, (_, g16) in grads.items()]], scatter=True)
        scatters[layer, part] = (tuple(grads), ex)
        return (ex.token,)

    dx, sent = _local_step(
        x.reshape(seq, D_MODEL), p.reshape(2, seq, PLE_DIM), positions.reshape(seq), loss_target.reshape(seq, D_MODEL),
        norm1, pool_w, pool_scale, norm2, norm3, final_norm, weights, send)

    g_out, d_out, m_out, v_out = {}, {}, {}, {}
    my_index = me.reshape(1)
    for part in ("main", "in"):
        recv = {}
        for layer in (1, 0):
            names, ex = scatters[layer, part]
            for n, r in zip(names, ex.wait(0, sent)[1]):
                recv[n, layer] = r
        for n in names:
            grad = (*own[n], recv[n, 0], recv[n, 1])
            if n == "w_in":
                grad = _sum_chunks(grad, my_index, "sum_w_in").transpose(0, 2, 1)
            g_out[n], d_out[n], m_out[n], v_out[n] = _adamw_sharded(w[n], m[n], v[n], grad, my_index, "adamw_" + n)
        sent = tuple(d_out[n] for n in names)
    (mine,), (landed,) = scatters["small"].wait(0, sent)
    small_g8 = lax.dynamic_update_slice_in_dim(landed, mine[None], me, axis=0)
    pack = lambda t: _pack_small(*[t[n] for n in SMALL])
    small_g, d_small, m_small, v_small = _adamw_packed(pack(w), small_g8, pack(m), pack(v), "adamw_small")
    for dst, a in ((g_out, small_g), (d_out, d_small), (m_out, m_small), (v_out, v_small)):
        dst.update(_unpack_small(a))

    return (small_g[7, 2 * POOL_WIDTH],dx.reshape(1, seq, D_MODEL), *[g_out[n] for n in ORDER], *[d_out[n] for n in ORDER],
            *[m_out[n] for n in ORDER], *[v_out[n] for n in ORDER])
```

```python
import functools

import jax
import jax.numpy as jnp
from jax import lax
from jax.experimental import pallas as pl
from jax.experimental.pallas import tpu as pltpu

F32 = jnp.float32
BF16 = jnp.bfloat16

D_MODEL = 1024
HEAD_DIM = 64
POOL_WIDTH = 256
POOL_WINDOWS = (2, 4, 8, 16)
POOL_HALO = 16
POOL_PAD = 8
GROUP_WIDTH = 256
DILATIONS = (1, 4, 16)
ATTN_BLOCK = 128
ROT_SHIFT = 8
ROPE_THETA = 500000.0
D_FF = 4096
FF_BLOCK = 512
FF_PER_STEP = 2
MLP_BWD_TILE = 512
FWD_TILE = 1024
N_DEV = 8
N_IN = POOL_WIDTH + 3 * 768
PLE_DIM = 256
EPS = 1e-6
NEG_BIG = -1e30

ADAM_LR = 0.001
ADAM_B1 = 0.9
ADAM_B2 = 0.999
ADAM_EPS = 1e-08
ADAM_WD = 0.01
ADAM_STEP = 10

LANES = 128
VMEM_LIMIT = 56 * 1024 * 1024
MESH = pl.DeviceIdType.MESH


def _params(n_grid):
    return pltpu.CompilerParams(dimension_semantics=("arbitrary",) * n_grid, vmem_limit_bytes=VMEM_LIMIT)


def _dot(a, b):
    return jnp.dot(a, b, preferred_element_type=F32)


def _dot_nt(a, b):
    return lax.dot_general(a, b, (((1,), (1,)), ((), ())), preferred_element_type=F32)


def _dot_tn(a, b):
    return lax.dot_general(a, b, (((0,), (0,)), ((), ())), preferred_element_type=F32)


def _rms(x, g):
    rstd = lax.rsqrt(jnp.mean(x * x, axis=-1, keepdims=True) + EPS)
    n = x * rstd
    return n, rstd, n * g


def _rms_bwd(dy, n, rstd, g):
    dyn = dy * g
    dx = rstd * (dyn - n * jnp.mean(dyn * n, axis=-1, keepdims=True))
    return dx, jnp.sum(dy * n, axis=0, keepdims=True)


def _ordered_after(body, n_in, after):
    if not after:
        return body
    return lambda *refs: body(*refs[:n_in], *refs[n_in + len(after):])


def _resident(shape):
    return pl.BlockSpec(shape, lambda i: (0,) * len(shape), pipeline_mode=pl.Buffered(1))


def _row_tile(s, t):
    t = min(s, t)
    assert s % t == 0
    return t


def _rot(z, c, sa, sb):
    return z * c + pltpu.roll(z, ROT_SHIFT, 1) * sa + pltpu.roll(z, LANES - ROT_SHIFT, 1) * sb


def _table_specs(t):
    return [pl.BlockSpec((t, LANES), functools.partial(lambda i, k: (i, k), k=k)) for k in range(3)]


def _rot_t(dz, c, sa, sb):
    return dz * c + pltpu.roll(dz * sa, LANES - ROT_SHIFT, 1) + pltpu.roll(dz * sb, ROT_SHIFT, 1)


def _to_residues(value, stage, out_ref, dil):
    if dil == 1:
        out_ref[0] = value.astype(out_ref.dtype)
        return
    rows = value.shape[0] // dil
    for hf in range(GROUP_WIDTH // LANES):
        lanes = slice(hf * LANES, (hf + 1) * LANES)
        stage[hf][...] = value[:, lanes]
        for r in range(dil):
            out_ref[r, :, lanes] = stage[hf][pl.ds(r, rows, stride=dil), :].astype(out_ref.dtype)


def _from_residues(in_ref, stage, dil):
    if dil == 1:
        return in_ref[0].astype(F32)
    rows = in_ref.shape[1]
    for hf in range(GROUP_WIDTH // LANES):
        for r in range(dil):
            stage[hf][pl.ds(r, rows, stride=dil), :] = in_ref[r, :, hf * LANES:(hf + 1) * LANES].astype(F32)
    return jnp.concatenate([stage[0][...], stage[1][...]], axis=1)


def _residue_spec(dil, t):
    return pl.BlockSpec((dil, t // dil, GROUP_WIDTH), lambda i: (0, i, 0))


def _residue_shape(dil, s, dtype):
    return jax.ShapeDtypeStruct((dil, s // dil, GROUP_WIDTH), dtype)


def _stages(t, n):
    return [pltpu.VMEM((t, LANES), F32)] * (n * (GROUP_WIDTH // LANES))


def _pair_stages(refs):
    return [refs[i:i + 2] for i in range(0, len(refs), 2)]


def _normproj_tile(x, g_ref, w_ref, c_ref, sa_ref, sb_ref, hn_ref, u_ref, *rest):
    qkv_refs, stages = rest[:9], _pair_stages(rest[9:])
    _, _, hn = _rms(x, g_ref[...])
    hb = hn.astype(BF16)
    hn_ref[...] = hb
    c, sa, sb = c_ref[...], sa_ref[...], sb_ref[...]

    def rot(z, scale):
        halves = [_rot(z[:, hf * LANES:(hf + 1) * LANES], c, sa, sb) * scale for hf in range(2)]
        return jnp.concatenate(halves, axis=1)

    proj = lambda lo: _dot_nt(hb, w_ref[lo:lo + GROUP_WIDTH, :])
    u_ref[...] = proj(0)
    for grp, dil in enumerate(DILATIONS):
        lo = POOL_WIDTH + grp * GROUP_WIDTH
        q_ref, k_ref, v_ref = qkv_refs[3 * grp:3 * grp + 3]
        _to_residues(rot(proj(lo), HEAD_DIM ** -0.5), stages[0], q_ref, dil)
        _to_residues(rot(proj(lo + 768), 1.0), stages[1], k_ref, dil)
        _to_residues(proj(lo + 1536), stages[2], v_ref, dil)


def _normproj_operands(s, t):
    row = lambda w: pl.BlockSpec((t, w), lambda i: (i, 0))
    in_specs = [pl.BlockSpec((1, D_MODEL), lambda i: (0, 0)), _resident((N_IN, D_MODEL))] + _table_specs(t)
    out_specs = [row(D_MODEL), row(POOL_WIDTH)] + [_residue_spec(dil, t) for dil in DILATIONS for _ in range(3)]
    out_shape = [jax.ShapeDtypeStruct((s, D_MODEL), BF16), jax.ShapeDtypeStruct((s, POOL_WIDTH), F32)]
    out_shape += [_residue_shape(dil, s, BF16) for dil in DILATIONS for _ in range(3)]
    return in_specs, out_specs, out_shape, _stages(t, 3)


def _normproj_fwd(h, g, w_in, rc, rsa, rsb, name):
    s = h.shape[0]
    t = _row_tile(s, FWD_TILE)

    def body(h_ref, *refs):
        _normproj_tile(h_ref[...], *refs)

    in_specs, out_specs, out_shape, scratch = _normproj_operands(s, t)
    return pl.pallas_call(
        body, name=name, grid=(s // t,), in_specs=[pl.BlockSpec((t, D_MODEL), lambda i: (i, 0))] + in_specs,
        out_specs=out_specs, out_shape=out_shape, scratch_shapes=scratch, compiler_params=_params(1),
    )(h, g, w_in, rc, rsa, rsb)


def _pool_lane_window():
    lane = lax.broadcasted_iota(jnp.int32, (1, POOL_WIDTH), 1)
    return jnp.left_shift(2, lane // (POOL_WIDTH // len(POOL_WINDOWS)))


def _window_sums(ext, b2, b4, b8, t, lo, tile, direction):
    rows = t + POOL_HALO
    for src, dst, sh in ((ext, b2, 1), (b2, b4, 2), (b4, b8, 4)):
        dst[lo:lo + rows, :] = src[lo:lo + rows, :] + src[lo + direction * sh:lo + direction * sh + rows, :]
    s16 = b8[tile:tile + t, :] + b8[tile + direction * 8:tile + direction * 8 + t, :]
    win = _pool_lane_window()
    return jnp.where(win == 2, b2[tile:tile + t, :],
                     jnp.where(win == 4, b4[tile:tile + t, :], jnp.where(win == 8, b8[tile:tile + t, :], s16)))


def _pool_fwd_tile(i, u_ref, w_ref, sc_ref, y_ref, ext, b2, b4, b8):
    t = u_ref.shape[0]
    first = POOL_PAD + POOL_HALO

    @pl.when(i == 0)
    def _():
        for buf in (ext, b2, b4):
            buf[0:POOL_PAD, :] = jnp.zeros((POOL_PAD, POOL_WIDTH), F32)
        ext[POOL_PAD:first, :] = jnp.zeros((POOL_HALO, POOL_WIDTH), F32)

    x = u_ref[...]
    ext[first:, :] = x
    wsum = _window_sums(ext, b2, b4, b8, t, POOL_PAD, first, -1)
    pos = i * t + lax.broadcasted_iota(jnp.int32, (t, POOL_WIDTH), 0)
    cnt = jnp.minimum(pos + 1, _pool_lane_window()).astype(F32)
    yb = (wsum / cnt - x).astype(BF16)
    y_ref[...] = yb
    ext[POOL_PAD:first, :] = x[t - POOL_HALO:, :]
    return _dot(yb, w_ref[...]) * sc_ref[...]


def _head_masks():
    lane = lax.broadcasted_iota(jnp.int32, (ATTN_BLOCK, GROUP_WIDTH), 1)
    return [lane // HEAD_DIM == hd for hd in range(GROUP_WIDTH // HEAD_DIM)]


def _stack_heads(a, masks):
    zero = jnp.zeros_like(a)
    return jnp.concatenate([jnp.where(m, a, zero) for m in masks], axis=0)


def _band_bias(first_step):
    rows = ATTN_BLOCK * (GROUP_WIDTH // HEAD_DIM)
    i = lax.broadcasted_iota(jnp.int32, (rows, 2 * ATTN_BLOCK), 0) & (ATTN_BLOCK - 1)
    j = lax.broadcasted_iota(jnp.int32, (rows, 2 * ATTN_BLOCK), 1)
    inner = jnp.where((j >= i) & (j <= i + ATTN_BLOCK), 0.0, NEG_BIG)
    return jnp.where((j < ATTN_BLOCK) & first_step, NEG_BIG, inner), inner


def _column_per_head(a):
    return jnp.concatenate([a[:, hd * HEAD_DIM:hd * HEAD_DIM + 1] for hd in range(GROUP_WIDTH // HEAD_DIM)], axis=0)


def _blocks_per_step(nb):
    if nb <= 16:
        return nb
    return next(qb for qb in (16, 8, 4, 2, 1) if nb % qb == 0)


def _residues_per_step(dil, nb, qb):
    return 2 if (nb == qb and qb < 8 and dil % 2 == 0) else 1


def _attn_fwd(q, k, v, name, after=()):
    dil, length, _ = q.shape
    nb = length // ATTN_BLOCK
    qb = _blocks_per_step(nb)
    rs = _residues_per_step(dil, nb, qb)

    def body(q_ref, kp_ref, kc_ref, vp_ref, vc_ref, o_ref, lse_ref):
        masks = _head_masks()
        bias = _band_bias(pl.program_id(1) == 0)
        for rr in range(rs):
            for qi in range(qb):
                here = slice(qi * ATTN_BLOCK, (qi + 1) * ATTN_BLOCK)
                before = slice((qi - 1) * ATTN_BLOCK, qi * ATTN_BLOCK)
                kcat = jnp.concatenate([kp_ref[rr] if qi == 0 else kc_ref[rr, before], kc_ref[rr, here]], axis=0)
                vcat = jnp.concatenate([vp_ref[rr] if qi == 0 else vc_ref[rr, before], vc_ref[rr, here]], axis=0)
                qs = _stack_heads(q_ref[rr, here], masks)
                sc = _dot_nt(qs, kcat) + bias[min(qi, 1)]
                m = jnp.max(sc, axis=1, keepdims=True)
                e = jnp.exp(sc - m)
                l = jnp.sum(e, axis=1, keepdims=True)
                p = (e / l).astype(BF16)
                lse = m + jnp.log(l)
                o = jnp.zeros((ATTN_BLOCK, GROUP_WIDTH), F32)
                lse_full = jnp.zeros((ATTN_BLOCK, GROUP_WIDTH), F32)
                for hd, msk in enumerate(masks):
                    rows = slice(hd * ATTN_BLOCK, (hd + 1) * ATTN_BLOCK)
                    o = jnp.where(msk, _dot(p[rows], vcat), o)
                    lse_full = jnp.where(msk, lse[rows], lse_full)
                o_ref[rr, here] = o.astype(o_ref.dtype)
                lse_ref[rr, here] = lse_full

    cur = pl.BlockSpec((rs, qb * ATTN_BLOCK, GROUP_WIDTH), lambda r, j: (r, j, 0))
    prev = pl.BlockSpec((rs, ATTN_BLOCK, GROUP_WIDTH), lambda r, j: (r, jnp.maximum(qb * j - 1, 0), 0))
    return pl.pallas_call(
        _ordered_after(body, 5, after), name=name, grid=(dil // rs, nb // qb),
        in_specs=[cur, prev, cur, prev, cur] + [pl.BlockSpec(memory_space=pl.ANY)] * len(after), out_specs=[cur, cur],
        out_shape=[jax.ShapeDtypeStruct(q.shape, BF16), jax.ShapeDtypeStruct(q.shape, F32)],
        compiler_params=_params(2),
    )(q, k, k, v, v, *after)


def _group_weights(l0, l1, l2):
    m = jnp.maximum(jnp.maximum(l0, l1), l2)
    e0, e1, e2 = jnp.exp(l0 - m), jnp.exp(l1 - m), jnp.exp(l2 - m)
    den = e0 + e1 + e2
    return e0 / den, e1 / den, e2 / den


def _outproj_fwd(h, u, w_bd, scale, o, lse, w_out, name):
    s = h.shape[0]
    t = _row_tile(s, FWD_TILE)

    def body(h_ref, u_ref, wbd_ref, sc_ref, o0, o1, o2, l0, l1, l2, w_ref, out_ref, a_ref, y_ref, ext, b2, b4, b8,
             *stages):
        pool_out = _pool_fwd_tile(pl.program_id(0), u_ref, wbd_ref, sc_ref, y_ref, ext, b2, b4, b8)
        stages = _pair_stages(stages)
        ov = [_from_residues(r, stages[i], DILATIONS[i]) for i, r in enumerate((o0, o1, o2))]
        lv = [_from_residues(r, stages[3 + i], DILATIONS[i]) for i, r in enumerate((l0, l1, l2))]
        wts = _group_weights(*lv)
        a = jnp.concatenate([pool_out] + [ov[i] * wts[i] for i in range(3)], axis=1).astype(BF16)
        a_ref[...] = a
        out_ref[...] = h_ref[...] + _dot(a, w_ref[...])

    row = lambda w: pl.BlockSpec((t, w), lambda i: (i, 0))
    res = [_residue_spec(dil, t) for dil in DILATIONS]
    return pl.pallas_call(
        body, name=name, grid=(s // t,),
        in_specs=[row(D_MODEL), row(POOL_WIDTH), _resident((POOL_WIDTH, POOL_WIDTH)), _resident((1, POOL_WIDTH))]
        + res + res + [_resident((D_MODEL, D_MODEL))],
        out_specs=[row(D_MODEL), row(D_MODEL), row(POOL_WIDTH)],
        out_shape=[jax.ShapeDtypeStruct((s, D_MODEL), F32), jax.ShapeDtypeStruct((s, D_MODEL), BF16),
                   jax.ShapeDtypeStruct((s, POOL_WIDTH), BF16)],
        scratch_shapes=[pltpu.VMEM((t + POOL_HALO + POOL_PAD, POOL_WIDTH), F32)] * 4 + _stages(t, 6),
        compiler_params=_params(1),
    )(h, u, w_bd, scale, *o, *lse, w_out)


def _mlp_fwd(h, g, w_up, w_down, name):
    s = h.shape[0]
    t = _row_tile(s, 512)
    nblk = D_FF // FF_BLOCK

    def body(h_ref, g_ref, wu_ref, wd_ref, out_ref, hn_ref, r_ref):
        x = h_ref[...]
        _, _, hn = _rms(x, g_ref[...])
        hb = hn.astype(BF16)
        hn_ref[...] = hb
        acc = None
        for b0 in range(0, nblk, FF_PER_STEP):
            acts = []
            for b in range(b0, b0 + FF_PER_STEP):
                r = jnp.maximum(_dot(hb, wu_ref[b]), 0.0)
                r_ref[:, b * FF_BLOCK:(b + 1) * FF_BLOCK] = r.astype(BF16)
                acts.append((r * r).astype(BF16))
            wd = wd_ref[b0:b0 + FF_PER_STEP].reshape(FF_PER_STEP * FF_BLOCK, D_MODEL)
            part = _dot(jnp.concatenate(acts, axis=1), wd)
            acc = part if acc is None else acc + part
        out_ref[...] = x + acc

    row = lambda w: pl.BlockSpec((t, w), lambda i: (i, 0))
    resident = lambda shape: pl.BlockSpec(shape, lambda i: (0, 0, 0), pipeline_mode=pl.Buffered(1))
    return pl.pallas_call(
        body, name=name, grid=(s // t,),
        in_specs=[row(D_MODEL), pl.BlockSpec((1, D_MODEL), lambda i: (0, 0)),
                  resident((nblk, D_MODEL, FF_BLOCK)), resident((nblk, FF_BLOCK, D_MODEL))],
        out_specs=[row(D_MODEL), row(D_MODEL), row(D_FF)],
        out_shape=[jax.ShapeDtypeStruct((s, D_MODEL), F32), jax.ShapeDtypeStruct((s, D_MODEL), BF16),
                   jax.ShapeDtypeStruct((s, D_FF), BF16)],
        compiler_params=_params(1),
    )(h, g, w_up, w_down)


def _gate_fwd(h, g, w_gate, p, layer, w_ple, name, head=None, follow=None):
    assert (head is None) != (follow is None)
    s = h.shape[0]
    t = _row_tile(s, 512)
    last = s // t - 1

    def body(h_ref, g_ref, wg_ref, p_ref, wp_ref, *refs):
        x = h_ref[...]
        gv = g_ref[...]
        n, rstd, hn = _rms(x, gv)
        hb = hn.astype(BF16)
        gate = 1.0 / (1.0 + jnp.exp(-_dot(hb, wg_ref[...])))
        pb = p_ref[...].astype(BF16)
        e = _dot(pb, wp_ref[...])
        h3 = x + gate * e
        if follow is not None:
            out_ref, hn_ref, gate_ref, pb_ref = refs[5:9]
            out_ref[...] = h3
            hn_ref[...] = hb
            pb_ref[...] = pb
            gate_ref[...] = gate.astype(BF16)
            _normproj_tile(h3, *refs[:5], *refs[9:])
            return
        gf_ref, t_ref, loss_ref, dgf_ref, out_ref, dg_ref, dwg_ref, dwgb_ref, dwp_ref, dwpb_ref = refs
        i = pl.program_id(0)

        @pl.when(i == 0)
        def _():
            for ref in (loss_ref, dgf_ref, dg_ref, dwg_ref, dwp_ref):
                ref[...] = jnp.zeros_like(ref)

        gf = gf_ref[...]
        n3, rstd3, y = _rms(h3, gf)
        err = y - t_ref[...]
        loss_ref[...] += jnp.sum(err * err) * (0.5 / D_MODEL)
        d, dgf = _rms_bwd(err * (1.0 / D_MODEL), n3, rstd3, gf)
        dgf_ref[...] += dgf
        dgl = (d * e * gate * (1.0 - gate)).astype(BF16)
        dwg_ref[...] += _dot_tn(hb, dgl)
        dwp_ref[...] += _dot_tn(pb, (d * gate).astype(BF16))
        dx, dg = _rms_bwd(_dot_nt(dgl, wg_ref[...]), n, rstd, gv)
        out_ref[...] = d + dx
        dg_ref[...] += dg

        @pl.when(i == last)
        def _():
            dwgb_ref[...] = dwg_ref[...].astype(BF16)
            dwpb_ref[...] = dwp_ref[...].astype(BF16)

    row = lambda w: pl.BlockSpec((t, w), lambda i: (i, 0))
    full = lambda a, b: pl.BlockSpec((a, b), lambda i: (0, 0))
    in_specs = [row(D_MODEL), full(1, D_MODEL), _resident((D_MODEL, D_MODEL)),
                pl.BlockSpec((None, t, PLE_DIM), lambda i: (layer, i, 0)), _resident((PLE_DIM, D_MODEL))]
    if follow is not None:
        next_in, next_out, next_shape, scratch = _normproj_operands(s, t)
        return pl.pallas_call(
            body, name=name, grid=(s // t,), in_specs=in_specs + next_in,
            out_specs=[row(D_MODEL), row(D_MODEL), row(D_MODEL), row(PLE_DIM)] + next_out,
            out_shape=[jax.ShapeDtypeStruct((s, D_MODEL), F32), jax.ShapeDtypeStruct((s, D_MODEL), BF16),
                       jax.ShapeDtypeStruct((s, D_MODEL), BF16), jax.ShapeDtypeStruct((s, PLE_DIM), BF16)] + next_shape,
            scratch_shapes=scratch, compiler_params=_params(1),
        )(h, g, w_gate, p, w_ple, *follow)
    loss, dgf, dh2, dg, dwg, dwgb, dwp, dwpb = pl.pallas_call(
        body, name=name, grid=(s // t,), in_specs=in_specs + [full(1, D_MODEL), row(D_MODEL)],
        out_specs=[pl.BlockSpec((1, LANES), lambda i: (0, 0)), full(1, D_MODEL), row(D_MODEL), full(1, D_MODEL),
                   full(D_MODEL, D_MODEL), full(D_MODEL, D_MODEL), full(PLE_DIM, D_MODEL), full(PLE_DIM, D_MODEL)],
        out_shape=[jax.ShapeDtypeStruct((1, LANES), F32), jax.ShapeDtypeStruct((1, D_MODEL), F32),
                   jax.ShapeDtypeStruct((s, D_MODEL), F32), jax.ShapeDtypeStruct((1, D_MODEL), F32),
                   jax.ShapeDtypeStruct((D_MODEL, D_MODEL), F32), jax.ShapeDtypeStruct((D_MODEL, D_MODEL), BF16),
                   jax.ShapeDtypeStruct((PLE_DIM, D_MODEL), F32), jax.ShapeDtypeStruct((PLE_DIM, D_MODEL), BF16)],
        compiler_params=_params(1),
    )(h, g, w_gate, p, w_ple, *head)
    return loss, dgf, dh2, dg, (dwg, dwgb), (dwp, dwpb)


def _gate_bwd(dh, gate, pb, w_ple, h, g, w_gate, hn, name, after=()):
    s = h.shape[0]
    t = _row_tile(s, FWD_TILE)
    last = s // t - 1

    def body(dh_ref, gate_ref, pb_ref, wp_ref, h_ref, g_ref, wg_ref, hn_ref, out_ref, dg_ref, dwg_ref, dwgb_ref,
             dwp_ref, dwpb_ref):
        i = pl.program_id(0)

        @pl.when(i == 0)
        def _():
            dg_ref[...] = jnp.zeros_like(dg_ref)
            dwg_ref[...] = jnp.zeros_like(dwg_ref)
            dwp_ref[...] = jnp.zeros_like(dwp_ref)

        d = dh_ref[...]
        gate = gate_ref[...].astype(F32)
        pb = pb_ref[...]
        e = _dot(pb, wp_ref[...])
        dgl = (d * e * gate * (1.0 - gate)).astype(BF16)
        dwg_ref[...] += _dot_tn(hn_ref[...], dgl)
        dwp_ref[...] += _dot_tn(pb, (d * gate).astype(BF16))
        gv = g_ref[...]
        n, rstd, _ = _rms(h_ref[...], gv)
        dx, dg = _rms_bwd(_dot_nt(dgl, wg_ref[...]), n, rstd, gv)
        out_ref[...] = d + dx
        dg_ref[...] += dg

        @pl.when(i == last)
        def _():
            dwgb_ref[...] = dwg_ref[...].astype(BF16)
            dwpb_ref[...] = dwp_ref[...].astype(BF16)

    row = lambda w: pl.BlockSpec((t, w), lambda i: (i, 0))
    full = lambda a, b: pl.BlockSpec((a, b), lambda i: (0, 0))
    dh2, dg, dwg, dwgb, dwp, dwpb = pl.pallas_call(
        _ordered_after(body, 8, after), name=name, grid=(s // t,),
        in_specs=[row(D_MODEL), row(D_MODEL), row(PLE_DIM), _resident((PLE_DIM, D_MODEL)), row(D_MODEL),
                  full(1, D_MODEL), _resident((D_MODEL, D_MODEL)), row(D_MODEL)]
        + [pl.BlockSpec(memory_space=pl.ANY)] * len(after),
        out_specs=[row(D_MODEL), full(1, D_MODEL), full(D_MODEL, D_MODEL), full(D_MODEL, D_MODEL),
                   full(PLE_DIM, D_MODEL), full(PLE_DIM, D_MODEL)],
        out_shape=[jax.ShapeDtypeStruct((s, D_MODEL), F32), jax.ShapeDtypeStruct((1, D_MODEL), F32),
                   jax.ShapeDtypeStruct((D_MODEL, D_MODEL), F32), jax.ShapeDtypeStruct((D_MODEL, D_MODEL), BF16),
                   jax.ShapeDtypeStruct((PLE_DIM, D_MODEL), F32), jax.ShapeDtypeStruct((PLE_DIM, D_MODEL), BF16)],
        compiler_params=_params(1),
    )(dh, gate, pb, w_ple, h, g, w_gate, hn, *after)
    return dh2, dg, (dwg, dwgb), (dwp, dwpb)


def _mlp_bwd(dh, r, h, g, w_up, w_down, name):
    s = h.shape[0]
    t = _row_tile(s, MLP_BWD_TILE)
    nblk = D_FF // FF_BLOCK

    def body(dh_ref, r_ref, h_ref, g_ref, wu_ref, wd_ref, out_ref, dup_ref, dg_ref, dhb_ref):
        @pl.when(pl.program_id(0) == 0)
        def _():
            dg_ref[...] = jnp.zeros_like(dg_ref)

        d = dh_ref[...]
        db = d.astype(BF16)
        dhb_ref[...] = db
        back = None
        for b in range(nblk):
            cols = slice(b * FF_BLOCK, (b + 1) * FF_BLOCK)
            dup = (_dot_nt(db, wd_ref[b]) * (2.0 * r_ref[:, cols].astype(F32))).astype(BF16)
            dup_ref[:, cols] = dup
            part = _dot_nt(dup, wu_ref[b])
            back = part if back is None else back + part
        gv = g_ref[...]
        n, rstd, _ = _rms(h_ref[...], gv)
        dx, dg = _rms_bwd(back, n, rstd, gv)
        out_ref[...] = d + dx
        dg_ref[...] += dg

    row = lambda w: pl.BlockSpec((t, w), lambda i: (i, 0))
    vec = pl.BlockSpec((1, D_MODEL), lambda i: (0, 0))
    resident = lambda shape: pl.BlockSpec(shape, lambda i: (0, 0, 0), pipeline_mode=pl.Buffered(1))
    return pl.pallas_call(
        body, name=name, grid=(s // t,),
        in_specs=[row(D_MODEL), row(D_FF), row(D_MODEL), vec,
                  resident((nblk, D_MODEL, FF_BLOCK)), resident((nblk, FF_BLOCK, D_MODEL))],
        out_specs=[row(D_MODEL), row(D_FF), vec, row(D_MODEL)],
        out_shape=[jax.ShapeDtypeStruct((s, D_MODEL), F32), jax.ShapeDtypeStruct((s, D_FF), BF16),
                   jax.ShapeDtypeStruct((1, D_MODEL), F32), jax.ShapeDtypeStruct((s, D_MODEL), BF16)],
        compiler_params=_params(1),
    )(dh, r, h, g, w_up, w_down)


def _outproj_bwd(dh, w_out, o, lse, ones_bd, a, name):
    s = dh.shape[0]
    t = _row_tile(s, 512)
    last = s // t - 1

    def body(dh_ref, w_ref, o0, o1, o2, l0, l1, l2, bd_ref, a_ref, dp_ref, do0, do1, do2, de0, de1, de2, dw_ref,
             dwb_ref, *stages):
        i = pl.program_id(0)

        @pl.when(i == 0)
        def _():
            dw_ref[...] = jnp.zeros_like(dw_ref)

        stages = _pair_stages(stages)
        dhb = dh_ref[...].astype(BF16)
        dw_ref[...] += _dot_tn(a_ref[...], dhb)

        @pl.when(i == last)
        def _():
            dwb_ref[...] = dw_ref[...].astype(BF16)

        da = _dot_nt(dhb, w_ref[...])
        dp_ref[...] = da[:, 0:POOL_WIDTH]
        ov =[_from_residues(r, stages[i], DILATIONS[i]) for i, r in enumerate((o0, o1, o2))]
        lv = [_from_residues(r, stages[3 + i], DILATIONS[i]) for i, r in enumerate((l0, l1, l2))]
        wts = _group_weights(*lv)
        bd = bd_ref[...]
        cbar = jnp.zeros((t, GROUP_WIDTH), F32)
        for grp, do_ref in enumerate((do0, do1, do2)):
            lo = POOL_WIDTH + grp * GROUP_WIDTH
            dag = da[:, lo:lo + GROUP_WIDTH]
            _to_residues(dag * wts[grp], stages[6 + grp], do_ref, DILATIONS[grp])
            prod = dag * ov[grp]
            hi = prod.astype(BF16)
            low = (prod - hi.astype(F32)).astype(BF16)
            cbar = cbar + wts[grp] * (_dot(hi, bd) + _dot(low, bd))
        for grp, de_ref in enumerate((de0, de1, de2)):
            _to_residues(wts[grp] * cbar, stages[9 + grp], de_ref, DILATIONS[grp])

    row = lambda w: pl.BlockSpec((t, w), lambda i: (i, 0))
    full = lambda a, b: pl.BlockSpec((a, b), lambda i: (0, 0))
    res = [_residue_spec(dil, t) for dil in DILATIONS]
    *outs, dw, dwb = pl.pallas_call(
        body, name=name, grid=(s // t,),
        in_specs=[row(D_MODEL), full(D_MODEL, D_MODEL)] + res + res + [full(GROUP_WIDTH, GROUP_WIDTH), row(D_MODEL)],
        out_specs=[row(POOL_WIDTH)] + res + res + [full(D_MODEL, D_MODEL)] * 2,
        out_shape=[jax.ShapeDtypeStruct((s, POOL_WIDTH), F32)] + [_residue_shape(dil, s, BF16) for dil in DILATIONS]
        + [_residue_shape(dil, s, F32) for dil in DILATIONS]
        + [jax.ShapeDtypeStruct((D_MODEL, D_MODEL), F32), jax.ShapeDtypeStruct((D_MODEL, D_MODEL), BF16)],
        scratch_shapes=_stages(t, 12),
        compiler_params=_params(1),
    )(dh, w_out, *o, *lse, ones_bd, a)
    return (*outs, (dw, dwb))


def _attn_bwd(q, k, v, do, lse, deff, name, after=()):
    dil, length, _ = q.shape
    nb = length // ATTN_BLOCK
    qb = _blocks_per_step(nb)
    nj = nb // qb
    rs = _residues_per_step(dil, nb, qb)
    whole = nj == 1
    tail = slice((qb - 1) * ATTN_BLOCK, qb * ATTN_BLOCK)
    block = lambda qi: slice(qi * ATTN_BLOCK, (qi + 1) * ATTN_BLOCK)

    def body(q_ref, kp_ref, kc_ref, vp_ref, vc_ref, do_ref, lse_ref, de_ref, dq_ref, dk_ref, dv_ref, ck, cv):
        j = pl.program_id(1)

        def compute():
            masks = _head_masks()
            bias = _band_bias(j == 0)
            for rr in range(rs):
                dkc, dvc = [], []
                for qi in range(qb):
                    here, before = block(qi), block(qi - 1)
                    kcat = jnp.concatenate([kp_ref[rr] if qi == 0 else kc_ref[rr, before], kc_ref[rr, here]], axis=0)
                    vcat = jnp.concatenate([vp_ref[rr] if qi == 0 else vc_ref[rr, before], vc_ref[rr, here]], axis=0)
                    qs = _stack_heads(q_ref[rr, here], masks)
                    dos = _stack_heads(do_ref[rr, here], masks)
                    sc = _dot_nt(qs, kcat) + bias[min(qi, 1)]
                    p = jnp.exp(sc - _column_per_head(lse_ref[rr, here]))
                    ds = (p * (_dot_nt(dos, vcat) - _column_per_head(de_ref[rr, here]))).astype(BF16)
                    dq = jnp.zeros((ATTN_BLOCK, GROUP_WIDTH), F32)
                    for hd, msk in enumerate(masks):
                        dq = jnp.where(msk, _dot(ds[block(hd)], kcat), dq)
                    dq_ref[rr, here] = dq.astype(dq_ref.dtype)
                    dkc.append(_dot_tn(ds, qs))
                    dvc.append(_dot_tn(p.astype(BF16), dos))

                for out_ref, carry, parts in ((dk_ref, ck, dkc), (dv_ref, cv, dvc)):
                    full = [parts[qi][ATTN_BLOCK:] + parts[qi + 1][0:ATTN_BLOCK] for qi in range(qb - 1)]
                    if whole:
                        for qi, val in enumerate(full + [parts[qb - 1][ATTN_BLOCK:]]):
                            out_ref[rr, block(qi)] = val.astype(out_ref.dtype)
                        continue

                    @pl.when(j > 0)
                    def _():
                        if qb > 1:
                            out_ref[0, 0:(qb - 1) * ATTN_BLOCK] = carry[0:(qb - 1) * ATTN_BLOCK].astype(out_ref.dtype)
                        out_ref[0, tail] = (carry[tail] + parts[0][0:ATTN_BLOCK]).astype(out_ref.dtype)

                    for qi, val in enumerate(full):
                        carry[block(qi)] = val
                    carry[tail] = parts[qb - 1][ATTN_BLOCK:]

        if whole:
            compute()
        else:
            pl.when(j < nj)(compute)

            @pl.when(j == nj)
            def _():
                dk_ref[0] = ck[...].astype(dk_ref.dtype)
                dv_ref[0] = cv[...].astype(dv_ref.dtype)

    step = lambda j: jnp.minimum(j, nj - 1)
    cur = pl.BlockSpec((rs, qb * ATTN_BLOCK, GROUP_WIDTH), lambda r, j: (r, step(j), 0))
    prev = pl.BlockSpec((rs, ATTN_BLOCK, GROUP_WIDTH), lambda r, j: (r, jnp.maximum(qb * step(j) - 1, 0), 0))
    late = pl.BlockSpec((rs, qb * ATTN_BLOCK, GROUP_WIDTH), lambda r, j: (r, jnp.maximum(j - 1, 0), 0))
    return pl.pallas_call(
        _ordered_after(body, 8, after), name=name, grid=(dil // rs, 1 if whole else nj + 1),
        in_specs=[cur, prev, cur, prev, cur, cur, cur, cur] + [pl.BlockSpec(memory_space=pl.ANY)] * len(after),
        out_specs=[cur, cur if whole else late, cur if whole else late],
        out_shape=[jax.ShapeDtypeStruct(q.shape, BF16)] * 3,
        scratch_shapes=[pltpu.VMEM((qb * ATTN_BLOCK, GROUP_WIDTH), F32)] * 2,
        compiler_params=_params(2),
    )(q, k, k, v, v, do, lse, deff, *after)


def _pool_bwd_tile(i, nt, dp_ref, y_ref, w_ref, sc_ref, dw_ref, dsc_ref, ext, b2, b4, b8):
    t = dp_ref.shape[0]

    @pl.when(i == 0)
    def _():
        ext[t:, :] = jnp.zeros((POOL_HALO + POOL_PAD, POOL_WIDTH), F32)
        for buf in (b2, b4):
            buf[t + POOL_HALO:, :] = jnp.zeros((POOL_PAD, POOL_WIDTH), F32)
        dw_ref[...] = jnp.zeros_like(dw_ref)
        dsc_ref[...] = jnp.zeros_like(dsc_ref)

    dp = dp_ref[...]
    yb = y_ref[...]
    w = w_ref[...]
    dsc_ref[...] += jnp.sum(dp * _dot(yb, w), axis=0, keepdims=True)
    dyo = (dp * sc_ref[...]).astype(BF16)
    dw_ref[...] += _dot_tn(yb, dyo)
    dy = _dot_nt(dyo, w)
    pos = (nt - 1 - i) * t + lax.broadcasted_iota(jnp.int32, (t, POOL_WIDTH), 0)
    gq = dy / jnp.minimum(pos + 1, _pool_lane_window()).astype(F32)
    ext[0:t, :] = gq
    du = _window_sums(ext, b2, b4, b8, t, 0, 0, 1) - dy
    ext[t:t + POOL_HALO, :] = gq[0:POOL_HALO, :]
    return du


def _normproj_bwd(dh, dpool, y, w_bd, scale, dq, dk, dv, rc, rsa, rsb, w_in, h, g, name):
    s = h.shape[0]
    t = _row_tile(s, 512)
    nt = s // t

    def body(dh_ref, dp_ref, y_ref, wbd_ref, sc_ref, q0, q1, q2, k0, k1, k2, v0, v1, v2, c_ref, sa_ref, sb_ref, w_ref,
             h_ref, g_ref, out_ref, dz_ref, dg_ref, dwbd_ref, dsc_ref, ext, b2, b4, b8, *stages):
        step = pl.program_id(0)

        @pl.when(step == 0)
        def _():
            dg_ref[...] = jnp.zeros_like(dg_ref)

        du = _pool_bwd_tile(step, nt, dp_ref, y_ref, wbd_ref, sc_ref, dwbd_ref, dsc_ref, ext, b2, b4, b8)
        c, sa, sb = c_ref[...], sa_ref[...], sb_ref[...]

        def unrot(a, scale):
            halves = [_rot_t(a[:, hf * LANES:(hf + 1) * LANES] * scale, c, sa, sb) for hf in range(2)]
            return jnp.concatenate(halves, axis=1)

        staged = _pair_stages(stages)
        tok = lambda refs, base: [_from_residues(r, staged[base + i], DILATIONS[i]) for i, r in enumerate(refs)]
        chunks = [du]
        chunks += [unrot(a, HEAD_DIM ** -0.5) for a in tok((q0, q1, q2), 0)]
        chunks += [unrot(a, 1.0) for a in tok((k0, k1, k2), 3)]
        chunks += tok((v0, v1, v2), 6)
        acc = jnp.zeros((t, D_MODEL), F32)
        for ci, ch in enumerate(chunks):
            cols = slice(ci * GROUP_WIDTH, (ci + 1) * GROUP_WIDTH)
            cb = ch.astype(BF16)
            dz_ref[:, cols] = cb
            acc = acc + _dot(cb, w_ref[cols, :])
        gv = g_ref[...]
        n, rstd, _ = _rms(h_ref[...], gv)
        dx, dg = _rms_bwd(acc, n, rstd, gv)
        out_ref[...] = dh_ref[...] + dx
        dg_ref[...] += dg

    back = lambda i: nt - 1 - i
    row = lambda w: pl.BlockSpec((t, w), lambda i: (back(i), 0))
    full = lambda a, b: pl.BlockSpec((a, b), lambda i: (0, 0))
    res = [pl.BlockSpec((dil, t // dil, GROUP_WIDTH), lambda i: (0, back(i), 0)) for dil in DILATIONS]
    tables = [pl.BlockSpec((t, LANES), functools.partial(lambda i, k: (back(i), k), k=k)) for k in range(3)]
    return pl.pallas_call(
        body, name=name, grid=(nt,),
        in_specs=[row(D_MODEL), row(POOL_WIDTH), row(POOL_WIDTH), full(POOL_WIDTH, POOL_WIDTH), full(1, POOL_WIDTH)]
        + res * 3 + tables + [full(N_IN, D_MODEL), row(D_MODEL), full(1, D_MODEL)],
        out_specs=[row(D_MODEL), row(N_IN), full(1, D_MODEL), full(POOL_WIDTH, POOL_WIDTH), full(1, POOL_WIDTH)],
        out_shape=[jax.ShapeDtypeStruct((s, D_MODEL), F32), jax.ShapeDtypeStruct((s, N_IN), BF16),
                   jax.ShapeDtypeStruct((1, D_MODEL), F32), jax.ShapeDtypeStruct((POOL_WIDTH, POOL_WIDTH), F32),
                   jax.ShapeDtypeStruct((1, POOL_WIDTH), F32)],
        scratch_shapes=[pltpu.VMEM((t + POOL_HALO + POOL_PAD, POOL_WIDTH), F32)] * 4 + _stages(t, 9),
        compiler_params=_params(1),
    )(dh, dpool, y, w_bd, scale, *dq, *dk, *dv, rc, rsa, rsb, w_in, h, g)


def _matmul_tn(a, b, name, *, square_a=False, tm=None, tn=None, blocked_out=False, after=()):
    s, m = a.shape
    n = b.shape[1]
    tk = _row_tile(s, 2048)
    tm = tm or min(m, 1024)
    tn = tn or min(n, 1024)
    assert m % tm == 0 and n % tn == 0
    nk = s // tk
    nsub = tn // FF_BLOCK if blocked_out else 1

    def body(a_ref, b_ref, o_ref, ob_ref, acc):
        k = pl.program_id(2)

        def product():
            av = a_ref[...]
            if square_a:
                av = av.astype(F32)
                av = av * av
            return _dot_tn(av.astype(BF16), b_ref[...].astype(BF16))

        def emit(total):
            if blocked_out:
                for sub in range(nsub):
                    cols = slice(sub * FF_BLOCK, (sub + 1) * FF_BLOCK)
                    o_ref[sub] = total[:, cols]
                    ob_ref[sub] = total[:, cols].astype(BF16)
            else:
                o_ref[...] = total
                ob_ref[...] = total.astype(BF16)

        if nk == 1:
            emit(product())
            return

        @pl.when(k == 0)
        def _():
            acc[...] = product()

        @pl.when((k > 0) & (k < nk - 1))
        def _():
            acc[...] += product()

        @pl.when(k == nk - 1)
        def _():
            emit(acc[...] + product())

    if blocked_out:
        shape = (n // FF_BLOCK, m, FF_BLOCK)
        out_spec = pl.BlockSpec((nsub, tm, FF_BLOCK), lambda i, j, k: (j, i, 0))
    else:
        shape = (m, n)
        out_spec = pl.BlockSpec((tm, tn), lambda i, j, k: (i, j))
    return pl.pallas_call(
        _ordered_after(body, 2, after), name=name, grid=(m // tm, n // tn, nk),
        in_specs=[pl.BlockSpec((tk, tm), lambda i, j, k: (k, i)), pl.BlockSpec((tk, tn), lambda i, j, k: (k, j))]
        + [pl.BlockSpec(memory_space=pl.ANY)] * len(after),
        out_specs=[out_spec, out_spec],
        out_shape=[jax.ShapeDtypeStruct(shape, F32), jax.ShapeDtypeStruct(shape, BF16)],
        scratch_shapes=[pltpu.VMEM((tm, tn), F32)],
        compiler_params=_params(3),
    )(a, b, *after)


def _adamw_math(w, g, m, v):
    m = ADAM_B1 * m + (1.0 - ADAM_B1) * g
    v = ADAM_B2 * v + (1.0 - ADAM_B2) * (g * g)
    m_hat = m / (1.0 - ADAM_B1 ** ADAM_STEP)
    v_hat = v / (1.0 - ADAM_B2 ** ADAM_STEP)
    delta = -ADAM_LR * (m_hat / (jnp.sqrt(v_hat) + ADAM_EPS) + ADAM_WD * w)
    return delta, m, v


def _sum_chunks_body(own0_ref, own1_ref, r0_ref, r1_ref):
    layer0 = pl.program_id(0) == 0
    g = jnp.where(layer0, own0_ref[...], own1_ref[...])
    for k in range(N_DEV - 1):
        g = g + jnp.where(layer0, r0_ref[k], r1_ref[k]).astype(F32)
    return g


def _chunk_specs(t, cols):
    rows_of = lambda layer: (lambda l, i: jnp.where(l == layer, i, 0))
    blk = pl.BlockSpec((None, t, cols), lambda l, i, me: (l, i, 0))
    own = [pl.BlockSpec((None, t, cols), functools.partial(lambda l, i, me, pick: (me[0], pick(l, i), 0), pick=rows_of(ly)))
           for ly in range(2)]
    recv = [pl.BlockSpec((N_DEV - 1, t, cols), functools.partial(lambda l, i, me, pick: (0, pick(l, i), 0), pick=rows_of(ly)))
            for ly in range(2)]
    return blk, own + recv


def _sum_chunks(chunks, me, name):
    _, rows, cols = chunks[0].shape
    t = _row_tile(rows, 320)

    def body(me_ref, own0_ref, own1_ref, r0_ref, r1_ref, g_ref):
        g_ref[...] = _sum_chunks_body(own0_ref, own1_ref, r0_ref, r1_ref)

    blk, chunk_specs = _chunk_specs(t, cols)
    return pl.pallas_call(
        body, name=name,
        grid_spec=pltpu.PrefetchScalarGridSpec(num_scalar_prefetch=1, grid=(2, rows // t), in_specs=chunk_specs,
                                               out_specs=blk),
        out_shape=jax.ShapeDtypeStruct((2, rows, cols), F32), compiler_params=_params(2),
    )(me, *chunks)


def _adamw_sharded(w, m, v, grad, me, name):
    _, rows, cols = w.shape
    t = _row_tile(rows, 256)
    summed = not isinstance(grad, tuple)
    grad = (grad,) if summed else grad

    def body(me_ref, w_ref, m_ref, v_ref, *refs):
        g_ref, d_ref, nm_ref, nv_ref = refs[-4:]
        g = refs[0][...] if summed else _sum_chunks_body(*refs[:4])
        g_ref[...] = g
        d_ref[...], nm_ref[...], nv_ref[...] = _adamw_math(w_ref[...], g, m_ref[...], v_ref[...])

    blk, chunk_specs = _chunk_specs(t, cols)
    return pl.pallas_call(
        body, name=name,
        grid_spec=pltpu.PrefetchScalarGridSpec(
            num_scalar_prefetch=1, grid=(2, rows // t),
            in_specs=[blk, blk, blk] + ([blk] if summed else chunk_specs), out_specs=[blk] * 4),
        out_shape=[jax.ShapeDtypeStruct(w.shape, F32)] * 4,
        compiler_params=_params(2),
    )(me, w, m, v, *grad)


def _adamw_packed(w, g8, m, v, name):
    def body(w_ref, g_ref, m_ref, v_ref, go_ref, d_ref, nm_ref, nv_ref):
        g = g_ref[0]
        for dev in range(1, N_DEV):
            g = g + g_ref[dev]
        go_ref[...] = g
        d_ref[...], nm_ref[...], nv_ref[...] = _adamw_math(w_ref[...], g, m_ref[...], v_ref[...])

    return pl.pallas_call(
        body, name=name, out_shape=[jax.ShapeDtypeStruct(w.shape, F32)] * 4,
        compiler_params=pltpu.CompilerParams(vmem_limit_bytes=VMEM_LIMIT),
    )(w, g8, m, v)


def _peer(k):
    x, y, c = lax.axis_index("x"), lax.axis_index("y"), lax.axis_index("c")
    return (1 - x if k & 4 else x, 1 - y if k & 2 else y, 1 - c if k & 1 else c)


def _linear(dev):
    return 4 * dev[0] + 2 * dev[1] + dev[2]


HBM_SPEC = pl.BlockSpec(memory_space=pltpu.HBM)
SEM_SPEC = pl.BlockSpec(memory_space=pltpu.SEMAPHORE)
ANY_SPEC = pl.BlockSpec(memory_space=pl.ANY)
EFFECT = pltpu.SideEffectType.DATAFLOW_SIDE_EFFECTING


def _in_hbm(a):
    return pltpu.with_memory_space_constraint(a, pltpu.HBM)


class _Exchange:
    def __init__(self, name, groups, scatter, after=()):
        self.name, self.scatter = name, scatter
        self.sizes = sizes = [len(g) for g in groups]
        srcs = [a for g in groups for a in g]
        n, ng = len(srcs), len(groups)
        lead = (N_DEV - 1,) if scatter else (N_DEV,)
        shapes = [lead + (a.shape[1:] if scatter else a.shape) for a in srcs]
        lands = [lax.empty(sh, a.dtype) for sh, a in zip(shapes, srcs)]
        offsets = [sum(sizes[:gi]) for gi in range(ng)]
        copy = self._copy

        def body(*refs):
            src, land = refs[:n], refs[n:2 * n]
            sems = refs[2 * n + len(after):2 * n + len(after) + 2 * ng]
            token = refs[-1]
            for gi in range(ng):
                for wi in range(sizes[gi]):
                    w = offsets[gi] + wi
                    for k in range(1, N_DEV):
                        copy(src[w], land[w], sems[2 * gi], sems[2 * gi + 1], wi, k).start()
            token[...] = jnp.zeros_like(token)

        sem_shapes = [pltpu.SemaphoreType.DMA(((N_DEV - 1) * sz,)) for sz in sizes for _ in range(2)]
        outs = pl.pallas_call(
            body, name=name + "_start",
            in_specs=[HBM_SPEC] * (2 * n) + [ANY_SPEC] * len(after),
            out_specs=[SEM_SPEC] * (2 * ng) + [HBM_SPEC] * (2 * n) + [pl.BlockSpec(memory_space=pltpu.VMEM)],
            out_shape=sem_shapes + [pltpu.HBM(a.shape, a.dtype) for a in srcs + lands]
            + [jax.ShapeDtypeStruct((8, LANES), F32)],
            input_output_aliases={i: 2 * ng + i for i in range(2 * n)},
            compiler_params=pltpu.CompilerParams(has_side_effects=EFFECT),
        )(*[_in_hbm(a) for a in srcs + lands], *after)
        self.sems = [outs[2 * gi:2 * gi + 2] for gi in range(ng)]
        thru = outs[2 * ng:2 * ng + 2 * n]
        self.srcs = [thru[offsets[gi]:offsets[gi] + sizes[gi]] for gi in range(ng)]
        self.lands = [thru[n + offsets[gi]:n + offsets[gi] + sizes[gi]] for gi in range(ng)]
        self.token = outs[-1]

    def _copy(self, src, land, send_sems, recv_sems, wi, k):
        to = _peer(k)
        if self.scatter:
            src_ref, dst_ref = src.at[_linear(to)], land.at[k - 1]
        else:
            src_ref, dst_ref = src, land.at[_linear(_peer(0))]
        return pltpu.make_async_remote_copy(
            src_ref=src_ref, dst_ref=dst_ref, send_sem=send_sems.at[(N_DEV - 1) * wi + k - 1],
            recv_sem=recv_sems.at[(N_DEV - 1) * wi + k - 1], device_id=to, device_id_type=MESH)

    def wait(self, gi, after):
        n = self.sizes[gi]
        copy = self._copy

        def body(*refs):
            src, land = refs[:n], refs[n:2 * n]
            send_sems, recv_sems = refs[2 * n], refs[2 * n + 1]
            for wi in range(n):
                for k in range(1, N_DEV):
                    cp = copy(src[wi], land[wi], send_sems, recv_sems, wi, k)
                    cp.wait_send()
                    cp.wait_recv()

        arrays = list(self.srcs[gi]) + list(self.lands[gi])
        outs = pl.pallas_call(
            body, name=f"{self.name}_wait{gi}",
            in_specs=[HBM_SPEC] * (2 * n) + [SEM_SPEC, SEM_SPEC] + [ANY_SPEC] * len(after),
            out_specs=[HBM_SPEC] * (2 * n),
            out_shape=[pltpu.HBM(a.shape, a.dtype) for a in arrays],
            input_output_aliases={i: i for i in range(2 * n)},
            compiler_params=pltpu.CompilerParams(has_side_effects=EFFECT),
        )(*arrays, *self.sems[gi], *after)
        return outs[:n], outs[n:]


def _rotary_tables(positions):
    rot_dim = HEAD_DIM // 4
    inv_freq = ROPE_THETA ** (-jnp.arange(0, rot_dim, 2, dtype=F32) / rot_dim)
    ang = positions.astype(F32)[:, None] * inv_freq
    cs = jnp.concatenate([jnp.cos(ang), jnp.sin(ang)], axis=1)
    dim = jnp.arange(LANES) % HEAD_DIM
    first, second = dim < ROT_SHIFT, (dim >= ROT_SHIFT) & (dim < rot_dim)
    src = jnp.arange(2 * ROT_SHIFT)[:, None]
    angle = (dim % ROT_SHIFT)[None, :]
    c = jnp.where((first | second)[None, :] & (src == angle), 1.0, 0.0)
    sa = jnp.where(second[None, :] & (src == angle + ROT_SHIFT), 1.0, 0.0)
    sb = jnp.where(first[None, :] & (src == angle + ROT_SHIFT), -1.0, 0.0)
    spread = jnp.concatenate([c, sa, sb], axis=1).astype(F32)
    base = jnp.concatenate([jnp.where(first | second, 0.0, 1.0), jnp.zeros((2 * LANES,))]).astype(F32)[None, :]
    return jnp.dot(cs, spread, precision=lax.Precision.HIGHEST, preferred_element_type=F32) + base


def _block_diag(pool_w):
    gc = pool_w.shape[-1]
    out = jnp.zeros((POOL_WIDTH, POOL_WIDTH), pool_w.dtype)
    for grp in range(pool_w.shape[0]):
        out = lax.dynamic_update_slice(out, pool_w[grp], (grp * gc, grp * gc))
    return out


def _diag_blocks(a):
    gc = POOL_WIDTH // len(POOL_WINDOWS)
    return jnp.stack([a[grp * gc:(grp + 1) * gc, grp * gc:(grp + 1) * gc] for grp in range(len(POOL_WINDOWS))])


def _local_step(x, p, positions, loss_target, norm1, pool_w, pool_scale, norm2, norm3, final_norm, weights, send):
    rc = rsa = rsb = _rotary_tables(positions)
    ones_bd = _block_diag(jnp.ones((4, HEAD_DIM, HEAD_DIM), BF16))
    saved = []
    h = x
    for i in range(2):
        tag = f"_l{i}"
        g1, g2, g3 = norm1[i:i + 1], norm2[i:i + 1], norm3[i:i + 1]
        w_bd = _block_diag(pool_w[i]).astype(BF16)
        scale = pool_scale[i:i + 1]
        if i == 0:
            w_in = weights(i, "in", (h, rc, w_bd))
            hn1, u, *qkv = _normproj_fwd(h, g1, w_in, rc, rsa, rsb, "normproj_fwd" + tag)
        else:
            w_in, (hn1, u, *qkv) = ahead
        qkv = [qkv[3 * grp:3 * grp + 3] for grp in range(3)]
        started = weights(i, "prefetch", (hn1,))
        o, lse = zip(*[_attn_fwd(*qkv[grp], f"attn_fwd{tag}_g{grp}", after=started) for grp in range(3)])
        w_out = weights(i, "out", o)
        h1, a, y = _outproj_fwd(h, u, w_bd, scale, o, lse, w_out, "outproj_fwd" + tag)
        w_up, w_down = weights(i, "mlp", (h1,))
        h2, hn2, r = _mlp_fwd(h1, g2, w_up, w_down, "mlp_fwd" + tag)
        w_gate, w_ple = weights(i, "gate", (h2,))
        h0 = h
        if i == 0:
            w_in_next = weights(1, "in", (h2,))
            h, hn3, gate, pb, *ahead = _gate_fwd(h2, g3, w_gate, p, i, w_ple, "gate_normproj_fwd",
                                                 follow=(norm1[1:2], w_in_next, rc, rsa, rsb))
            ahead = (w_in_next, ahead)
        else:
            hn3 = gate = pb = None
            loss, d_final, *top = _gate_fwd(h2, g3, w_gate, p, i, w_ple, "gate_loss_gate_bwd",
                                            head=(final_norm.reshape(1, D_MODEL), loss_target))
        saved.append(dict(h0=h0, hn1=hn1, qkv=qkv, y=y, o=o, lse=lse, a=a, h1=h1, hn2=hn2, r=r, h2=h2,
                          hn3=hn3, gate=gate, pb=pb, w_bd=w_bd, scale=scale, g1=g1, g2=g2, g3=g3,
                          w_in=w_in, w_out=w_out, w_up=w_up, w_down=w_down, w_gate=w_gate, w_ple=w_ple))

    grads = [None, None]
    sent = ()
    for i in (1, 0):
        tag = f"_l{i}"
        sv = saved[i]
        if i == 1:
            dh2, dg3, dw_gate, dw_ple = top
        else:
            dh2, dg3, dw_gate, dw_ple = _gate_bwd(dh, sv["gate"], sv["pb"], sv["w_ple"], sv["h2"], sv["g3"],
                                                  sv["w_gate"], sv["hn3"], "gate_bwd" + tag, after=sent)
        dh1, dup, dg2, dh2b = _mlp_bwd(dh2, sv["r"], sv["h1"], sv["g2"], sv["w_up"], sv["w_down"], "mlp_bwd" + tag)
        dw_down = _matmul_tn(sv["r"], dh2b, "dw_down" + tag, square_a=True)
        dw_up = _matmul_tn(sv["hn2"], dup, "dw_up" + tag, blocked_out=True)
        dpool, do0, do1, do2, de0, de1, de2, dw_out = _outproj_bwd(dh1, sv["w_out"], sv["o"], sv["lse"], ones_bd,
                                                                   sv["a"], "outproj_bwd" + tag)
        sent = send(i, "main", dict(w_gate=dw_gate, w_ple=dw_ple, w_down=dw_down, w_up=dw_up, w_out=dw_out))
        dqkv = [_attn_bwd(*sv["qkv"][grp], do_g, sv["lse"][grp], de_g, f"attn_bwd{tag}_g{grp}", after=sent)
                for grp, (do_g, de_g) in enumerate(((do0, de0), (do1, de1), (do2, de2)))]
        dq, dk, dv = zip(*dqkv)
        dh, dz, dg1, dw_bd, dscale = _normproj_bwd(dh1, dpool, sv["y"], sv["w_bd"], sv["scale"], dq, dk, dv, rc, rsa, rsb,
                                                   sv["w_in"], sv["h0"], sv["g1"], "normproj_bwd" + tag)
        grads[i] = dict(norm1=dg1, norm2=dg2, norm3=dg3, pool_w=_diag_blocks(dw_bd), pool_scale=dscale)
        small_sent = send(0, "small", (grads, d_final, loss)) if i == 0 else ()
        dw_in = _matmul_tn(dz, sv["hn1"], "dw_in" + tag, tm=N_IN // 2, after=small_sent)
        sent = send(i, "in", dict(w_in=dw_in))
    return dh, sent


def _pack_small(norm1, norm2, norm3, final_norm, pool_scale, pool_w, spare=None):
    spare = jnp.zeros((1, LANES), F32) if spare is None else spare
    scale_row = jnp.concatenate([pool_scale.reshape(1, 2 * POOL_WIDTH), spare,
                                 jnp.zeros((1, D_MODEL - 2 * POOL_WIDTH - LANES), F32)], axis=1)
    return jnp.concatenate([norm1, norm2, norm3, final_norm.reshape(1, D_MODEL), scale_row,
                            pool_w.reshape(32, D_MODEL)], axis=0)


def _unpack_small(a):
    return dict(norm1=a[0:2], norm2=a[2:4], norm3=a[4:6], final_norm=a[6], pool_scale=a[7, 0:2 * POOL_WIDTH].reshape(2, POOL_WIDTH),
                pool_w=a[8:40].reshape(2, 4, HEAD_DIM, HEAD_DIM))


def _chunks_cols(a, cols):
    return a.reshape(a.shape[0], N_DEV, cols).transpose(1, 0, 2)


def _chunks_rows(a, rows):
    return a.reshape(N_DEV, rows, a.shape[1])


BIG = ("w_in", "w_out", "w_up", "w_down", "w_gate", "w_ple")
SMALL = ("norm1", "norm2", "norm3", "final_norm", "pool_scale", "pool_w")
ORDER = ("norm1", "w_in", "pool_w", "pool_scale", "w_out", "norm2", "w_up", "w_down", "norm3", "w_gate", "w_ple",
         "final_norm")


def kernel(x, p, positions, norm1, w_in, pool_w, pool_scale, w_out, norm2, w_up, w_down, norm3, w_gate, w_ple, final_norm, loss_target, m_norm1, m_w_in, m_pool_w, m_pool_scale, m_w_out, m_norm2, m_w_up, m_w_down, m_norm3, m_w_gate, m_w_ple, m_final_norm, v_norm1, v_w_in, v_pool_w, v_pool_scale, v_w_out, v_norm2, v_w_up, v_w_down, v_norm3, v_w_gate, v_w_ple, v_final_norm):
    w = dict(norm1=norm1, w_in=w_in, pool_w=pool_w, pool_scale=pool_scale, w_out=w_out, norm2=norm2, w_up=w_up,
             w_down=w_down, norm3=norm3, w_gate=w_gate, w_ple=w_ple, final_norm=final_norm)
    m = dict(norm1=m_norm1, w_in=m_w_in, pool_w=m_pool_w, pool_scale=m_pool_scale, w_out=m_w_out, norm2=m_norm2,
             w_up=m_w_up, w_down=m_w_down, norm3=m_norm3, w_gate=m_w_gate, w_ple=m_w_ple, final_norm=m_final_norm)
    v = dict(norm1=v_norm1, w_in=v_w_in, pool_w=v_pool_w, pool_scale=v_pool_scale, w_out=v_w_out, norm2=v_norm2,
             w_up=v_w_up, w_down=v_w_down, norm3=v_norm3, w_gate=v_w_gate, w_ple=v_w_ple, final_norm=v_final_norm)
    seq = x.shape[1]

    bf = {n: [w[n][layer].astype(BF16) for layer in range(2)] for n in BIG}
    bf["w_in"] = [a.T for a in bf["w_in"]]
    me = 4 * lax.axis_index("x") + 2 * lax.axis_index("y") + lax.axis_index("c")
    parts = dict(zip(("in", "out", "mlp", "gate"), (("w_in",), ("w_out",), ("w_up", "w_down"), ("w_gate", "w_ple"))))
    first = _Exchange("gather_first", [[bf["w_in"][0]]], scatter=False)
    later = [pt for pt in parts if pt != "in"]
    gathers = [_Exchange("gather_l0", [[bf[n][0] for n in parts[pt]] for pt in later], scatter=False,
                         after=(first.token,))]
    unpack = dict(w_in=lambda a: a.reshape(N_IN, D_MODEL),
                  w_out=lambda a: a.reshape(D_MODEL, D_MODEL), w_gate=lambda a: a.reshape(D_MODEL, D_MODEL),
                  w_ple=lambda a: a.transpose(1, 0, 2).reshape(PLE_DIM, D_MODEL), w_up=lambda a: a, w_down=lambda a: a)

    def weights(layer, part, after):
        if part == "prefetch":
            if layer != 0:
                return ()
            gathers.append(_Exchange("gather_l1", [[bf[n][1] for n in parts[pt]] for pt in parts], scatter=False,
                                     after=after))
            return (gathers[1].token,)
        if layer == 0 and part == "in":
            shards, lands = first.wait(0, (*after, gathers[0].token))
        elif layer == 0:
            shards, lands = gathers[0].wait(later.index(part), after)
        else:
            shards, lands = gathers[1].wait(tuple(parts).index(part), after)
        full = [unpack[n](lax.dynamic_update_slice_in_dim(land, shard[None], me, axis=0))
                for n, shard, land in zip(parts[part], shards, lands)]
        return full if len(full) > 1 else full[0]

    to_chunks = dict(w_in=lambda a: _chunks_rows(a, N_IN // N_DEV),
                     w_out=lambda a: _chunks_rows(a, D_MODEL // N_DEV),
                     w_up=lambda a: a, w_down=lambda a: _chunks_rows(a, FF_BLOCK),
                     w_gate=lambda a: _chunks_rows(a, D_MODEL // N_DEV), w_ple=lambda a: _chunks_cols(a, D_MODEL // N_DEV))
    own = {n: [None, None] for n in BIG}
    scatters = {}

    def send(layer, part, grads):
        if part == "small":
            per_layer, d_final, loss = grads
            pack = _pack_small(
                *[jnp.concatenate([per_layer[0][n], per_layer[1][n]], axis=0) for n in ("norm1", "norm2", "norm3")],
                d_final.reshape(D_MODEL),
                jnp.concatenate([per_layer[0]["pool_scale"], per_layer[1]["pool_scale"]], axis=0),
                jnp.stack([per_layer[0]["pool_w"], per_layer[1]["pool_w"]]), spare=loss)
            scatters["small"] = _Exchange("gather_small", [[pack]], scatter=False)
            return (scatters["small"].token,)
        for n, (g32, _) in grads.items():
            own[n][layer] = to_chunks[n](g32)
        ex = _Exchange(f"scatter_{part}_l{layer}", [[to_chunks[n](g16) for n, (_, g16) in grads.items()]], scatter=True)
        scatters[layer, part] = (tuple(grads), ex)
        return (ex.token,)

    dx, sent = _local_step(
        x.reshape(seq, D_MODEL), p.reshape(2, seq, PLE_DIM), positions.reshape(seq), loss_target.reshape(seq, D_MODEL),
        norm1, pool_w, pool_scale, norm2, norm3, final_norm, weights, send)

    g_out, d_out, m_out, v_out = {}, {}, {}, {}
    my_index = me.reshape(1)
    for part in ("main", "in"):
        recv = {}
        for layer in (1, 0):
            names, ex = scatters[layer, part]
            for n, r in zip(names, ex.wait(0, sent)[1]):
                recv[n, layer] = r
        for n in names:
            grad = (*own[n], recv[n, 0], recv[n, 1])
            if n == "w_in":
                grad = _sum_chunks(grad, my_index, "sum_w_in").transpose(0, 2, 1)
            g_out[n], d_out[n], m_out[n], v_out[n] = _adamw_sharded(w[n], m[n], v[n], grad, my_index, "adamw_" + n)
        sent = tuple(d_out[n] for n in names)
    (mine,), (landed,) = scatters["small"].wait(0, sent)
    small_g8 = lax.dynamic_update_slice_in_dim(landed, mine[None], me, axis=0)
    pack = lambda t: _pack_small(*[t[n] for n in SMALL])
    small_g, d_small, m_small, v_small = _adamw_packed(pack(w), small_g8, pack(m), pack(v), "adamw_small")
    for dst, a in ((g_out, small_g), (d_out, d_small), (m_out, m_small), (v_out, v_small)):
        dst.update(_unpack_small(a))

    return (small_g[7, 2 * POOL_WIDTH],dx.reshape(1, seq, D_MODEL), *[g_out[n] for n in ORDER], *[d_out[n] for n in ORDER],
            *[m_out[n] for n in ORDER], *[v_out[n] for n in ORDER])
```

```python
import functools

import jax
import jax.numpy as jnp
from jax import lax
from jax.experimental import pallas as pl
from jax.experimental.pallas import tpu as pltpu

F32 = jnp.float32
BF16 = jnp.bfloat16

D_MODEL = 1024
HEAD_DIM = 64
POOL_WIDTH = 256
POOL_WINDOWS = (2, 4, 8, 16)
POOL_HALO = 16
POOL_PAD = 8
GROUP_WIDTH = 256
DILATIONS = (1, 4, 16)
ATTN_BLOCK = 128
ROT_SHIFT = 8
ROPE_THETA = 500000.0
D_FF = 4096
FF_BLOCK = 512
FF_PER_STEP = 2
MLP_BWD_TILE = 512
FWD_TILE = 1024
N_DEV = 8
N_IN = POOL_WIDTH + 3 * 768
PLE_DIM = 256
EPS = 1e-6
NEG_BIG = -1e30

ADAM_LR = 0.001
ADAM_B1 = 0.9
ADAM_B2 = 0.999
ADAM_EPS = 1e-08
ADAM_WD = 0.01
ADAM_STEP = 10

LANES = 128
VMEM_LIMIT = 56 * 1024 * 1024
MESH = pl.DeviceIdType.MESH


def _params(n_grid):
    return pltpu.CompilerParams(dimension_semantics=("arbitrary",) * n_grid, vmem_limit_bytes=VMEM_LIMIT)


def _dot(a, b):
    return jnp.dot(a, b, preferred_element_type=F32)


def _dot_nt(a, b):
    return lax.dot_general(a, b, (((1,), (1,)), ((), ())), preferred_element_type=F32)


def _dot_tn(a, b):
    return lax.dot_general(a, b, (((0,), (0,)), ((), ())), preferred_element_type=F32)


def _rms(x, g):
    rstd = lax.rsqrt(jnp.mean(x * x, axis=-1, keepdims=True) + EPS)
    n = x * rstd
    return n, rstd, n * g


def _rms_bwd(dy, n, rstd, g):
    dyn = dy * g
    dx = rstd * (dyn - n * jnp.mean(dyn * n, axis=-1, keepdims=True))
    return dx, jnp.sum(dy * n, axis=0, keepdims=True)


def _ordered_after(body, n_in, after):
    if not after:
        return body
    return lambda *refs: body(*refs[:n_in], *refs[n_in + len(after):])


def _resident(shape):
    return pl.BlockSpec(shape, lambda i: (0,) * len(shape), pipeline_mode=pl.Buffered(1))


def _row_tile(s, t):
    t = min(s, t)
    assert s % t == 0
    return t


def _rot(z, c, sa, sb):
    return z * c + pltpu.roll(z, ROT_SHIFT, 1) * sa + pltpu.roll(z, LANES - ROT_SHIFT, 1) * sb


def _table_specs(t):
    return [pl.BlockSpec((t, LANES), functools.partial(lambda i, k: (i, k), k=k)) for k in range(3)]


def _rot_t(dz, c, sa, sb):
    return dz * c + pltpu.roll(dz * sa, LANES - ROT_SHIFT, 1) + pltpu.roll(dz * sb, ROT_SHIFT, 1)


def _to_residues(value, stage, out_ref, dil):
    if dil == 1:
        out_ref[0] = value.astype(out_ref.dtype)
        return
    rows = value.shape[0] // dil
    for hf in range(GROUP_WIDTH // LANES):
        lanes = slice(hf * LANES, (hf + 1) * LANES)
        stage[hf][...] = value[:, lanes]
        for r in range(dil):
            out_ref[r, :, lanes] = stage[hf][pl.ds(r, rows, stride=dil), :].astype(out_ref.dtype)


def _from_residues(in_ref, stage, dil):
    if dil == 1:
        return in_ref[0].astype(F32)
    rows = in_ref.shape[1]
    for hf in range(GROUP_WIDTH // LANES):
        for r in range(dil):
            stage[hf][pl.ds(r, rows, stride=dil), :] = in_ref[r, :, hf * LANES:(hf + 1) * LANES].astype(F32)
    return jnp.concatenate([stage[0][...], stage[1][...]], axis=1)


def _residue_spec(dil, t):
    return pl.BlockSpec((dil, t // dil, GROUP_WIDTH), lambda i: (0, i, 0))


def _residue_shape(dil, s, dtype):
    return jax.ShapeDtypeStruct((dil, s // dil, GROUP_WIDTH), dtype)


def _stages(t, n):
    return [pltpu.VMEM((t, LANES), F32)] * (n * (GROUP_WIDTH // LANES))


def _pair_stages(refs):
    return [refs[i:i + 2] for i in range(0, len(refs), 2)]


def _normproj_tile(x, g_ref, w_ref, c_ref, sa_ref, sb_ref, hn_ref, u_ref, *rest):
    qkv_refs, stages = rest[:9], _pair_stages(rest[9:])
    _, _, hn = _rms(x, g_ref[...])
    hb = hn.astype(BF16)
    hn_ref[...] = hb
    c, sa, sb = c_ref[...], sa_ref[...], sb_ref[...]

    def rot(z, scale):
        halves = [_rot(z[:, hf * LANES:(hf + 1) * LANES], c, sa, sb) * scale for hf in range(2)]
        return jnp.concatenate(halves, axis=1)

    proj = lambda lo: _dot_nt(hb, w_ref[lo:lo + GROUP_WIDTH, :])
    u_ref[...] = proj(0)
    for grp, dil in enumerate(DILATIONS):
        lo = POOL_WIDTH + grp * GROUP_WIDTH
        q_ref, k_ref, v_ref = qkv_refs[3 * grp:3 * grp + 3]
        _to_residues(rot(proj(lo), HEAD_DIM ** -0.5), stages[0], q_ref, dil)
        _to_residues(rot(proj(lo + 768), 1.0), stages[1], k_ref, dil)
        _to_residues(proj(lo + 1536), stages[2], v_ref, dil)


def _normproj_operands(s, t):
    row = lambda w: pl.BlockSpec((t, w), lambda i: (i, 0))
    in_specs = [pl.BlockSpec((1, D_MODEL), lambda i: (0, 0)), _resident((N_IN, D_MODEL))] + _table_specs(t)
    out_specs = [row(D_MODEL), row(POOL_WIDTH)] + [_residue_spec(dil, t) for dil in DILATIONS for _ in range(3)]
    out_shape = [jax.ShapeDtypeStruct((s, D_MODEL), BF16), jax.ShapeDtypeStruct((s, POOL_WIDTH), F32)]
    out_shape += [_residue_shape(dil, s, BF16) for dil in DILATIONS for _ in range(3)]
    return in_specs, out_specs, out_shape, _stages(t, 3)


def _normproj_fwd(h, g, w_in, rc, rsa, rsb, name):
    s = h.shape[0]
    t = _row_tile(s, FWD_TILE)

    def body(h_ref, *refs):
        _normproj_tile(h_ref[...], *refs)

    in_specs, out_specs, out_shape, scratch = _normproj_operands(s, t)
    return pl.pallas_call(
        body, name=name, grid=(s // t,), in_specs=[pl.BlockSpec((t, D_MODEL), lambda i: (i, 0))] + in_specs,
        out_specs=out_specs, out_shape=out_shape, scratch_shapes=scratch, compiler_params=_params(1),
    )(h, g, w_in, rc, rsa, rsb)


def _pool_lane_window():
    lane = lax.broadcasted_iota(jnp.int32, (1, POOL_WIDTH), 1)
    return jnp.left_shift(2, lane // (POOL_WIDTH // len(POOL_WINDOWS)))


def _window_sums(ext, b2, b4, b8, t, lo, tile, direction):
    rows = t + POOL_HALO
    for src, dst, sh in ((ext, b2, 1), (b2, b4, 2), (b4, b8, 4)):
        dst[lo:lo + rows, :] = src[lo:lo + rows, :] + src[lo + direction * sh:lo + direction * sh + rows, :]
    s16 = b8[tile:tile + t, :] + b8[tile + direction * 8:tile + direction * 8 + t, :]
    win = _pool_lane_window()
    return jnp.where(win == 2, b2[tile:tile + t, :],
                     jnp.where(win == 4, b4[tile:tile + t, :], jnp.where(win == 8, b8[tile:tile + t, :], s16)))


def _pool_fwd_tile(i, u_ref, w_ref, sc_ref, y_ref, ext, b2, b4, b8):
    t = u_ref.shape[0]
    first = POOL_PAD + POOL_HALO

    @pl.when(i == 0)
    def _():
        for buf in (ext, b2, b4):
            buf[0:POOL_PAD, :] = jnp.zeros((POOL_PAD, POOL_WIDTH), F32)
        ext[POOL_PAD:first, :] = jnp.zeros((POOL_HALO, POOL_WIDTH), F32)

    x = u_ref[...]
    ext[first:, :] = x
    wsum = _window_sums(ext, b2, b4, b8, t, POOL_PAD, first, -1)
    pos = i * t + lax.broadcasted_iota(jnp.int32, (t, POOL_WIDTH), 0)
    cnt = jnp.minimum(pos + 1, _pool_lane_window()).astype(F32)
    yb = (wsum / cnt - x).astype(BF16)
    y_ref[...] = yb
    ext[POOL_PAD:first, :] = x[t - POOL_HALO:, :]
    return _dot(yb, w_ref[...]) * sc_ref[...]


def _head_masks():
    lane = lax.broadcasted_iota(jnp.int32, (ATTN_BLOCK, GROUP_WIDTH), 1)
    return [lane // HEAD_DIM == hd for hd in range(GROUP_WIDTH // HEAD_DIM)]


def _stack_heads(a, masks):
    zero = jnp.zeros_like(a)
    return jnp.concatenate([jnp.where(m, a, zero) for m in masks], axis=0)


def _band_bias(first_step):
    rows = ATTN_BLOCK * (GROUP_WIDTH // HEAD_DIM)
    i = lax.broadcasted_iota(jnp.int32, (rows, 2 * ATTN_BLOCK), 0) & (ATTN_BLOCK - 1)
    j = lax.broadcasted_iota(jnp.int32, (rows, 2 * ATTN_BLOCK), 1)
    inner = jnp.where((j >= i) & (j <= i + ATTN_BLOCK), 0.0, NEG_BIG)
    return jnp.where((j < ATTN_BLOCK) & first_step, NEG_BIG, inner), inner


def _column_per_head(a):
    return jnp.concatenate([a[:, hd * HEAD_DIM:hd * HEAD_DIM + 1] for hd in range(GROUP_WIDTH // HEAD_DIM)], axis=0)


def _blocks_per_step(nb):
    if nb <= 16:
        return nb
    return next(qb for qb in (16, 8, 4, 2, 1) if nb % qb == 0)


def _residues_per_step(dil, nb, qb):
    return 2 if (nb == qb and qb < 8 and dil % 2 == 0) else 1


def _attn_fwd(q, k, v, name, after=()):
    dil, length, _ = q.shape
    nb = length // ATTN_BLOCK
    qb = _blocks_per_step(nb)
    rs = _residues_per_step(dil, nb, qb)

    def body(q_ref, kp_ref, kc_ref, vp_ref, vc_ref, o_ref, lse_ref):
        masks = _head_masks()
        bias = _band_bias(pl.program_id(1) == 0)
        for rr in range(rs):
            for qi in range(qb):
                here = slice(qi * ATTN_BLOCK, (qi + 1) * ATTN_BLOCK)
                before = slice((qi - 1) * ATTN_BLOCK, qi * ATTN_BLOCK)
                kcat = jnp.concatenate([kp_ref[rr] if qi == 0 else kc_ref[rr, before], kc_ref[rr, here]], axis=0)
                vcat = jnp.concatenate([vp_ref[rr] if qi == 0 else vc_ref[rr, before], vc_ref[rr, here]], axis=0)
                qs = _stack_heads(q_ref[rr, here], masks)
                sc = _dot_nt(qs, kcat) + bias[min(qi, 1)]
                m = jnp.max(sc, axis=1, keepdims=True)
                e = jnp.exp(sc - m)
                l = jnp.sum(e, axis=1, keepdims=True)
                p = (e / l).astype(BF16)
                lse = m + jnp.log(l)
                o = jnp.zeros((ATTN_BLOCK, GROUP_WIDTH), F32)
                lse_full = jnp.zeros((ATTN_BLOCK, GROUP_WIDTH), F32)
                for hd, msk in enumerate(masks):
                    rows = slice(hd * ATTN_BLOCK, (hd + 1) * ATTN_BLOCK)
                    o = jnp.where(msk, _dot(p[rows], vcat), o)
                    lse_full = jnp.where(msk, lse[rows], lse_full)
                o_ref[rr, here] = o.astype(o_ref.dtype)
                lse_ref[rr, here] = lse_full

    cur = pl.BlockSpec((rs, qb * ATTN_BLOCK, GROUP_WIDTH), lambda r, j: (r, j, 0))
    prev = pl.BlockSpec((rs, ATTN_BLOCK, GROUP_WIDTH), lambda r, j: (r, jnp.maximum(qb * j - 1, 0), 0))
    return pl.pallas_call(
        _ordered_after(body, 5, after), name=name, grid=(dil // rs, nb // qb),
        in_specs=[cur, prev, cur, prev, cur] + [pl.BlockSpec(memory_space=pl.ANY)] * len(after), out_specs=[cur, cur],
        out_shape=[jax.ShapeDtypeStruct(q.shape, BF16), jax.ShapeDtypeStruct(q.shape, F32)],
        compiler_params=_params(2),
    )(q, k, k, v, v, *after)


def _group_weights(l0, l1, l2):
    m = jnp.maximum(jnp.maximum(l0, l1), l2)
    e0, e1, e2 = jnp.exp(l0 - m), jnp.exp(l1 - m), jnp.exp(l2 - m)
    den = e0 + e1 + e2
    return e0 / den, e1 / den, e2 / den


def _outproj_fwd(h, u, w_bd, scale, o, lse, w_out, name):
    s = h.shape[0]
    t = _row_tile(s, FWD_TILE)

    def body(h_ref, u_ref, wbd_ref, sc_ref, o0, o1, o2, l0, l1, l2, w_ref, out_ref, a_ref, y_ref, ext, b2, b4, b8,
             *stages):
        pool_out = _pool_fwd_tile(pl.program_id(0), u_ref, wbd_ref, sc_ref, y_ref, ext, b2, b4, b8)
        stages = _pair_stages(stages)
        ov = [_from_residues(r, stages[i], DILATIONS[i]) for i, r in enumerate((o0, o1, o2))]
        lv = [_from_residues(r, stages[3 + i], DILATIONS[i]) for i, r in enumerate((l0, l1, l2))]
        wts = _group_weights(*lv)
        a = jnp.concatenate([pool_out] + [ov[i] * wts[i] for i in range(3)], axis=1).astype(BF16)
        a_ref[...] = a
        out_ref[...] = h_ref[...] + _dot(a, w_ref[...])

    row = lambda w: pl.BlockSpec((t, w), lambda i: (i, 0))
    res = [_residue_spec(dil, t) for dil in DILATIONS]
    return pl.pallas_call(
        body, name=name, grid=(s // t,),
        in_specs=[row(D_MODEL), row(POOL_WIDTH), _resident((POOL_WIDTH, POOL_WIDTH)), _resident((1, POOL_WIDTH))]
        + res + res + [_resident((D_MODEL, D_MODEL))],
        out_specs=[row(D_MODEL), row(D_MODEL), row(POOL_WIDTH)],
        out_shape=[jax.ShapeDtypeStruct((s, D_MODEL), F32), jax.ShapeDtypeStruct((s, D_MODEL), BF16),
                   jax.ShapeDtypeStruct((s, POOL_WIDTH), BF16)],
        scratch_shapes=[pltpu.VMEM((t + POOL_HALO + POOL_PAD, POOL_WIDTH), F32)] * 4 + _stages(t, 6),
        compiler_params=_params(1),
    )(h, u, w_bd, scale, *o, *lse, w_out)


def _mlp_fwd(h, g, w_up, w_down, name):
    s = h.shape[0]
    t = _row_tile(s, 512)
    nblk = D_FF // FF_BLOCK

    def body(h_ref, g_ref, wu_ref, wd_ref, out_ref, hn_ref, r_ref):
        x = h_ref[...]
        _, _, hn = _rms(x, g_ref[...])
        hb = hn.astype(BF16)
        hn_ref[...] = hb
        acc = None
        for b0 in range(0, nblk, FF_PER_STEP):
            acts = []
            for b in range(b0, b0 + FF_PER_STEP):
                r = jnp.maximum(_dot(hb, wu_ref[b]), 0.0)
                r_ref[:, b * FF_BLOCK:(b + 1) * FF_BLOCK] = r.astype(BF16)
                acts.append((r * r).astype(BF16))
            wd = wd_ref[b0:b0 + FF_PER_STEP].reshape(FF_PER_STEP * FF_BLOCK, D_MODEL)
            part = _dot(jnp.concatenate(acts, axis=1), wd)
            acc = part if acc is None else acc + part
        out_ref[...] = x + acc

    row = lambda w: pl.BlockSpec((t, w), lambda i: (i, 0))
    resident = lambda shape: pl.BlockSpec(shape, lambda i: (0, 0, 0), pipeline_mode=pl.Buffered(1))
    return pl.pallas_call(
        body, name=name, grid=(s // t,),
        in_specs=[row(D_MODEL), pl.BlockSpec((1, D_MODEL), lambda i: (0, 0)),
                  resident((nblk, D_MODEL, FF_BLOCK)), resident((nblk, FF_BLOCK, D_MODEL))],
        out_specs=[row(D_MODEL), row(D_MODEL), row(D_FF)],
        out_shape=[jax.ShapeDtypeStruct((s, D_MODEL), F32), jax.ShapeDtypeStruct((s, D_MODEL), BF16),
                   jax.ShapeDtypeStruct((s, D_FF), BF16)],
        compiler_params=_params(1),
    )(h, g, w_up, w_down)


def _gate_fwd(h, g, w_gate, p, layer, w_ple, name, head=None, follow=None):
    assert (head is None) != (follow is None)
    s = h.shape[0]
    t = _row_tile(s, 512)
    last = s // t - 1

    def body(h_ref, g_ref, wg_ref, p_ref, wp_ref, *refs):
        x = h_ref[...]
        gv = g_ref[...]
        n, rstd, hn = _rms(x, gv)
        hb = hn.astype(BF16)
        gate = 1.0 / (1.0 + jnp.exp(-_dot(hb, wg_ref[...])))
        pb = p_ref[...].astype(BF16)
        e = _dot(pb, wp_ref[...])
        h3 = x + gate * e
        if follow is not None:
            out_ref, hn_ref, gate_ref, pb_ref = refs[5:9]
            out_ref[...] = h3
            hn_ref[...] = hb
            pb_ref[...] = pb
            gate_ref[...] = gate.astype(BF16)
            _normproj_tile(h3, *refs[:5], *refs[9:])
            return
        gf_ref, t_ref, loss_ref, dgf_ref, out_ref, dg_ref, dwg_ref, dwgb_ref, dwp_ref, dwpb_ref = refs
        i = pl.program_id(0)

        @pl.when(i == 0)
        def _():
            for ref in (loss_ref, dgf_ref, dg_ref, dwg_ref, dwp_ref):
                ref[...] = jnp.zeros_like(ref)

        gf = gf_ref[...]
        n3, rstd3, y = _rms(h3, gf)
        err = y - t_ref[...]
        loss_ref[...] += jnp.sum(err * err) * (0.5 / D_MODEL)
        d, dgf = _rms_bwd(err * (1.0 / D_MODEL), n3, rstd3, gf)
        dgf_ref[...] += dgf
        dgl = (d * e * gate * (1.0 - gate)).astype(BF16)
        dwg_ref[...] += _dot_tn(hb, dgl)
        dwp_ref[...] += _dot_tn(pb, (d * gate).astype(BF16))
        dx, dg = _rms_bwd(_dot_nt(dgl, wg_ref[...]), n, rstd, gv)
        out_ref[...] = d + dx
        dg_ref[...] += dg

        @pl.when(i == last)
        def _():
            dwgb_ref[...] = dwg_ref[...].astype(BF16)
            dwpb_ref[...] = dwp_ref[...].astype(BF16)

    row = lambda w: pl.BlockSpec((t, w), lambda i: (i, 0))
    full = lambda a, b: pl.BlockSpec((a, b), lambda i: (0, 0))
    in_specs = [row(D_MODEL), full(1, D_MODEL), _resident((D_MODEL, D_MODEL)),
                pl.BlockSpec((None, t, PLE_DIM), lambda i: (layer, i, 0)), _resident((PLE_DIM, D_MODEL))]
    if follow is not None:
        next_in, next_out, next_shape, scratch = _normproj_operands(s, t)
        return pl.pallas_call(
            body, name=name, grid=(s // t,), in_specs=in_specs + next_in,
            out_specs=[row(D_MODEL), row(D_MODEL), row(D_MODEL), row(PLE_DIM)] + next_out,
            out_shape=[jax.ShapeDtypeStruct((s, D_MODEL), F32), jax.ShapeDtypeStruct((s, D_MODEL), BF16),
                       jax.ShapeDtypeStruct((s, D_MODEL), BF16), jax.ShapeDtypeStruct((s, PLE_DIM), BF16)] + next_shape,
            scratch_shapes=scratch, compiler_params=_params(1),
        )(h, g, w_gate, p, w_ple, *follow)
    loss, dgf, dh2, dg, dwg, dwgb, dwp, dwpb = pl.pallas_call(
        body, name=name, grid=(s // t,), in_specs=in_specs + [full(1, D_MODEL), row(D_MODEL)],
        out_specs=[pl.BlockSpec((1, LANES), lambda i: (0, 0)), full(1, D_MODEL), row(D_MODEL), full(1, D_MODEL),
                   full(D_MODEL, D_MODEL), full(D_MODEL, D_MODEL), full(PLE_DIM, D_MODEL), full(PLE_DIM, D_MODEL)],
        out_shape=[jax.ShapeDtypeStruct((1, LANES), F32), jax.ShapeDtypeStruct((1, D_MODEL), F32),
                   jax.ShapeDtypeStruct((s, D_MODEL), F32), jax.ShapeDtypeStruct((1, D_MODEL), F32),
                   jax.ShapeDtypeStruct((D_MODEL, D_MODEL), F32), jax.ShapeDtypeStruct((D_MODEL, D_MODEL), BF16),
                   jax.ShapeDtypeStruct((PLE_DIM, D_MODEL), F32), jax.ShapeDtypeStruct((PLE_DIM, D_MODEL), BF16)],
        compiler_params=_params(1),
    )(h, g, w_gate, p, w_ple, *head)
    return loss, dgf, dh2, dg, (dwg, dwgb), (dwp, dwpb)


def _gate_bwd(dh, gate, pb, w_ple, h, g, w_gate, hn, name, after=()):
    s = h.shape[0]
    t = _row_tile(s, FWD_TILE)
    last = s // t - 1

    def body(dh_ref, gate_ref, pb_ref, wp_ref, h_ref, g_ref, wg_ref, hn_ref, out_ref, dg_ref, dwg_ref, dwgb_ref,
             dwp_ref, dwpb_ref):
        i = pl.program_id(0)

        @pl.when(i == 0)
        def _():
            dg_ref[...] = jnp.zeros_like(dg_ref)
            dwg_ref[...] = jnp.zeros_like(dwg_ref)
            dwp_ref[...] = jnp.zeros_like(dwp_ref)

        d = dh_ref[...]
        gate = gate_ref[...].astype(F32)
        pb = pb_ref[...]
        e = _dot(pb, wp_ref[...])
        dgl = (d * e * gate * (1.0 - gate)).astype(BF16)
        dwg_ref[...] += _dot_tn(hn_ref[...], dgl)
        dwp_ref[...] += _dot_tn(pb, (d * gate).astype(BF16))
        gv = g_ref[...]
        n, rstd, _ = _rms(h_ref[...], gv)
        dx, dg = _rms_bwd(_dot_nt(dgl, wg_ref[...]), n, rstd, gv)
        out_ref[...] = d + dx
        dg_ref[...] += dg

        @pl.when(i == last)
        def _():
            dwgb_ref[...] = dwg_ref[...].astype(BF16)
            dwpb_ref[...] = dwp_ref[...].astype(BF16)

    row = lambda w: pl.BlockSpec((t, w), lambda i: (i, 0))
    full = lambda a, b: pl.BlockSpec((a, b), lambda i: (0, 0))
    dh2, dg, dwg, dwgb, dwp, dwpb = pl.pallas_call(
        _ordered_after(body, 8, after), name=name, grid=(s // t,),
        in_specs=[row(D_MODEL), row(D_MODEL), row(PLE_DIM), _resident((PLE_DIM, D_MODEL)), row(D_MODEL),
                  full(1, D_MODEL), _resident((D_MODEL, D_MODEL)), row(D_MODEL)]
        + [pl.BlockSpec(memory_space=pl.ANY)] * len(after),
        out_specs=[row(D_MODEL), full(1, D_MODEL), full(D_MODEL, D_MODEL), full(D_MODEL, D_MODEL),
                   full(PLE_DIM, D_MODEL), full(PLE_DIM, D_MODEL)],
        out_shape=[jax.ShapeDtypeStruct((s, D_MODEL), F32), jax.ShapeDtypeStruct((1, D_MODEL), F32),
                   jax.ShapeDtypeStruct((D_MODEL, D_MODEL), F32), jax.ShapeDtypeStruct((D_MODEL, D_MODEL), BF16),
                   jax.ShapeDtypeStruct((PLE_DIM, D_MODEL), F32), jax.ShapeDtypeStruct((PLE_DIM, D_MODEL), BF16)],
        compiler_params=_params(1),
    )(dh, gate, pb, w_ple, h, g, w_gate, hn, *after)
    return dh2, dg, (dwg, dwgb), (dwp, dwpb)


def _mlp_bwd(dh, r, h, g, w_up, w_down, name):
    s = h.shape[0]
    t = _row_tile(s, MLP_BWD_TILE)
    nblk = D_FF // FF_BLOCK

    def body(dh_ref, r_ref, h_ref, g_ref, wu_ref, wd_ref, out_ref, dup_ref, dg_ref, dhb_ref):
        @pl.when(pl.program_id(0) == 0)
        def _():
            dg_ref[...] = jnp.zeros_like(dg_ref)

        d = dh_ref[...]
        db = d.astype(BF16)
        dhb_ref[...] = db
        back = None
        for b in range(nblk):
            cols = slice(b * FF_BLOCK, (b + 1) * FF_BLOCK)
            dup = (_dot_nt(db, wd_ref[b]) * (2.0 * r_ref[:, cols].astype(F32))).astype(BF16)
            dup_ref[:, cols] = dup
            part = _dot_nt(dup, wu_ref[b])
            back = part if back is None else back + part
        gv = g_ref[...]
        n, rstd, _ = _rms(h_ref[...], gv)
        dx, dg = _rms_bwd(back, n, rstd, gv)
        out_ref[...] = d + dx
        dg_ref[...] += dg

    row = lambda w: pl.BlockSpec((t, w), lambda i: (i, 0))
    vec = pl.BlockSpec((1, D_MODEL), lambda i: (0, 0))
    resident = lambda shape: pl.BlockSpec(shape, lambda i: (0, 0, 0), pipeline_mode=pl.Buffered(1))
    return pl.pallas_call(
        body, name=name, grid=(s // t,),
        in_specs=[row(D_MODEL), row(D_FF), row(D_MODEL), vec,
                  resident((nblk, D_MODEL, FF_BLOCK)), resident((nblk, FF_BLOCK, D_MODEL))],
        out_specs=[row(D_MODEL), row(D_FF), vec, row(D_MODEL)],
        out_shape=[jax.ShapeDtypeStruct((s, D_MODEL), F32), jax.ShapeDtypeStruct((s, D_FF), BF16),
                   jax.ShapeDtypeStruct((1, D_MODEL), F32), jax.ShapeDtypeStruct((s, D_MODEL), BF16)],
        compiler_params=_params(1),
    )(dh, r, h, g, w_up, w_down)


def _outproj_bwd(dh, w_out, o, lse, ones_bd, a, name):
    s = dh.shape[0]
    t = _row_tile(s, 512)
    last = s // t - 1

    def body(dh_ref, w_ref, o0, o1, o2, l0, l1, l2, bd_ref, a_ref, dp_ref, do0, do1, do2, de0, de1, de2, dw_ref,
             dwb_ref, *stages):
        i = pl.program_id(0)

        @pl.when(i == 0)
        def _():
            dw_ref[...] = jnp.zeros_like(dw_ref)

        stages = _pair_stages(stages)
        dhb = dh_ref[...].astype(BF16)
        dw_ref[...] += _dot_tn(a_ref[...], dhb)

        @pl.when(i == last)
        def _():
            dwb_ref[...] = dw_ref[...].astype(BF16)

        da = _dot_nt(dhb, w_ref[...])
        dp_ref[...] = da[:, 0:POOL_WIDTH]
        ov =[_from_residues(r, stages[i], DILATIONS[i]) for i, r in enumerate((o0, o1, o2))]
        lv = [_from_residues(r, stages[3 + i], DILATIONS[i]) for i, r in enumerate((l0, l1, l2))]
        wts = _group_weights(*lv)
        bd = bd_ref[...]
        cbar = jnp.zeros((t, GROUP_WIDTH), F32)
        for grp, do_ref in enumerate((do0, do1, do2)):
            lo = POOL_WIDTH + grp * GROUP_WIDTH
            dag = da[:, lo:lo + GROUP_WIDTH]
            _to_residues(dag * wts[grp], stages[6 + grp], do_ref, DILATIONS[grp])
            prod = dag * ov[grp]
            hi = prod.astype(BF16)
            low = (prod - hi.astype(F32)).astype(BF16)
            cbar = cbar + wts[grp] * (_dot(hi, bd) + _dot(low, bd))
        for grp, de_ref in enumerate((de0, de1, de2)):
            _to_residues(wts[grp] * cbar, stages[9 + grp], de_ref, DILATIONS[grp])

    row = lambda w: pl.BlockSpec((t, w), lambda i: (i, 0))
    full = lambda a, b: pl.BlockSpec((a, b), lambda i: (0, 0))
    res = [_residue_spec(dil, t) for dil in DILATIONS]
    *outs, dw, dwb = pl.pallas_call(
        body, name=name, grid=(s // t,),
        in_specs=[row(D_MODEL), full(D_MODEL, D_MODEL)] + res + res + [full(GROUP_WIDTH, GROUP_WIDTH), row(D_MODEL)],
        out_specs=[row(POOL_WIDTH)] + res + res + [full(D_MODEL, D_MODEL)] * 2,
        out_shape=[jax.ShapeDtypeStruct((s, POOL_WIDTH), F32)] + [_residue_shape(dil, s, BF16) for dil in DILATIONS]
        + [_residue_shape(dil, s, F32) for dil in DILATIONS]
        + [jax.ShapeDtypeStruct((D_MODEL, D_MODEL), F32), jax.ShapeDtypeStruct((D_MODEL, D_MODEL), BF16)],
        scratch_shapes=_stages(t, 12),
        compiler_params=_params(1),
    )(dh, w_out, *o, *lse, ones_bd, a)
    return (*outs, (dw, dwb))


def _attn_bwd(q, k, v, do, lse, deff, name, after=()):
    dil, length, _ = q.shape
    nb = length // ATTN_BLOCK
    qb = _blocks_per_step(nb)
    nj = nb // qb
    rs = _residues_per_step(dil, nb, qb)
    whole = nj == 1
    tail = slice((qb - 1) * ATTN_BLOCK, qb * ATTN_BLOCK)
    block = lambda qi: slice(qi * ATTN_BLOCK, (qi + 1) * ATTN_BLOCK)

    def body(q_ref, kp_ref, kc_ref, vp_ref, vc_ref, do_ref, lse_ref, de_ref, dq_ref, dk_ref, dv_ref, ck, cv):
        j = pl.program_id(1)

        def compute():
            masks = _head_masks()
            bias = _band_bias(j == 0)
            for rr in range(rs):
                dkc, dvc = [], []
                for qi in range(qb):
                    here, before = block(qi), block(qi - 1)
                    kcat = jnp.concatenate([kp_ref[rr] if qi == 0 else kc_ref[rr, before], kc_ref[rr, here]], axis=0)
                    vcat = jnp.concatenate([vp_ref[rr] if qi == 0 else vc_ref[rr, before], vc_ref[rr, here]], axis=0)
                    qs = _stack_heads(q_ref[rr, here], masks)
                    dos = _stack_heads(do_ref[rr, here], masks)
                    sc = _dot_nt(qs, kcat) + bias[min(qi, 1)]
                    p = jnp.exp(sc - _column_per_head(lse_ref[rr, here]))
                    ds = (p * (_dot_nt(dos, vcat) - _column_per_head(de_ref[rr, here]))).astype(BF16)
                    dq = jnp.zeros((ATTN_BLOCK, GROUP_WIDTH), F32)
                    for hd, msk in enumerate(masks):
                        dq = jnp.where(msk, _dot(ds[block(hd)], kcat), dq)
                    dq_ref[rr, here] = dq.astype(dq_ref.dtype)
                    dkc.append(_dot_tn(ds, qs))
                    dvc.append(_dot_tn(p.astype(BF16), dos))

                for out_ref, carry, parts in ((dk_ref, ck, dkc), (dv_ref, cv, dvc)):
                    full = [parts[qi][ATTN_BLOCK:] + parts[qi + 1][0:ATTN_BLOCK] for qi in range(qb - 1)]
                    if whole:
                        for qi, val in enumerate(full + [parts[qb - 1][ATTN_BLOCK:]]):
                            out_ref[rr, block(qi)] = val.astype(out_ref.dtype)
                        continue

                    @pl.when(j > 0)
                    def _():
                        if qb > 1:
                            out_ref[0, 0:(qb - 1) * ATTN_BLOCK] = carry[0:(qb - 1) * ATTN_BLOCK].astype(out_ref.dtype)
                        out_ref[0, tail] = (carry[tail] + parts[0][0:ATTN_BLOCK]).astype(out_ref.dtype)

                    for qi, val in enumerate(full):
                        carry[block(qi)] = val
                    carry[tail] = parts[qb - 1][ATTN_BLOCK:]

        if whole:
            compute()
        else:
            pl.when(j < nj)(compute)

            @pl.when(j == nj)
            def _():
                dk_ref[0] = ck[...].astype(dk_ref.dtype)
                dv_ref[0] = cv[...].astype(dv_ref.dtype)

    step = lambda j: jnp.minimum(j, nj - 1)
    cur = pl.BlockSpec((rs, qb * ATTN_BLOCK, GROUP_WIDTH), lambda r, j: (r, step(j), 0))
    prev = pl.BlockSpec((rs, ATTN_BLOCK, GROUP_WIDTH), lambda r, j: (r, jnp.maximum(qb * step(j) - 1, 0), 0))
    late = pl.BlockSpec((rs, qb * ATTN_BLOCK, GROUP_WIDTH), lambda r, j: (r, jnp.maximum(j - 1, 0), 0))
    return pl.pallas_call(
        _ordered_after(body, 8, after), name=name, grid=(dil // rs, 1 if whole else nj + 1),
        in_specs=[cur, prev, cur, prev, cur, cur, cur, cur] + [pl.BlockSpec(memory_space=pl.ANY)] * len(after),
        out_specs=[cur, cur if whole else late, cur if whole else late],
        out_shape=[jax.ShapeDtypeStruct(q.shape, BF16)] * 3,
        scratch_shapes=[pltpu.VMEM((qb * ATTN_BLOCK, GROUP_WIDTH), F32)] * 2,
        compiler_params=_params(2),
    )(q, k, k, v, v, do, lse, deff, *after)


def _pool_bwd_tile(i, nt, dp_ref, y_ref, w_ref, sc_ref, dw_ref, dsc_ref, ext, b2, b4, b8):
    t = dp_ref.shape[0]

    @pl.when(i == 0)
    def _():
        ext[t:, :] = jnp.zeros((POOL_HALO + POOL_PAD, POOL_WIDTH), F32)
        for buf in (b2, b4):
            buf[t + POOL_HALO:, :] = jnp.zeros((POOL_PAD, POOL_WIDTH), F32)
        dw_ref[...] = jnp.zeros_like(dw_ref)
        dsc_ref[...] = jnp.zeros_like(dsc_ref)

    dp = dp_ref[...]
    yb = y_ref[...]
    w = w_ref[...]
    dsc_ref[...] += jnp.sum(dp * _dot(yb, w), axis=0, keepdims=True)
    dyo = (dp * sc_ref[...]).astype(BF16)
    dw_ref[...] += _dot_tn(yb, dyo)
    dy = _dot_nt(dyo, w)
    pos = (nt - 1 - i) * t + lax.broadcasted_iota(jnp.int32, (t, POOL_WIDTH), 0)
    gq = dy / jnp.minimum(pos + 1, _pool_lane_window()).astype(F32)
    ext[0:t, :] = gq
    du = _window_sums(ext, b2, b4, b8, t, 0, 0, 1) - dy
    ext[t:t + POOL_HALO, :] = gq[0:POOL_HALO, :]
    return du


def _normproj_bwd(dh, dpool, y, w_bd, scale, dq, dk, dv, rc, rsa, rsb, w_in, h, g, name):
    s = h.shape[0]
    t = _row_tile(s, 512)
    nt = s // t

    def body(dh_ref, dp_ref, y_ref, wbd_ref, sc_ref, q0, q1, q2, k0, k1, k2, v0, v1, v2, c_ref, sa_ref, sb_ref, w_ref,
             h_ref, g_ref, out_ref, dz_ref, dg_ref, dwbd_ref, dsc_ref, ext, b2, b4, b8, *stages):
        step = pl.program_id(0)

        @pl.when(step == 0)
        def _():
            dg_ref[...] = jnp.zeros_like(dg_ref)

        du = _pool_bwd_tile(step, nt, dp_ref, y_ref, wbd_ref, sc_ref, dwbd_ref, dsc_ref, ext, b2, b4, b8)
        c, sa, sb = c_ref[...], sa_ref[...], sb_ref[...]

        def unrot(a, scale):
            halves = [_rot_t(a[:, hf * LANES:(hf + 1) * LANES] * scale, c, sa, sb) for hf in range(2)]
            return jnp.concatenate(halves, axis=1)

        staged = _pair_stages(stages)
        tok = lambda refs, base: [_from_residues(r, staged[base + i], DILATIONS[i]) for i, r in enumerate(refs)]
        chunks = [du]
        chunks += [unrot(a, HEAD_DIM ** -0.5) for a in tok((q0, q1, q2), 0)]
        chunks += [unrot(a, 1.0) for a in tok((k0, k1, k2), 3)]
        chunks += tok((v0, v1, v2), 6)
        acc = jnp.zeros((t, D_MODEL), F32)
        for ci, ch in enumerate(chunks):
            cols = slice(ci * GROUP_WIDTH, (ci + 1) * GROUP_WIDTH)
            cb = ch.astype(BF16)
            dz_ref[:, cols] = cb
            acc = acc + _dot(cb, w_ref[cols, :])
        gv = g_ref[...]
        n, rstd, _ = _rms(h_ref[...], gv)
        dx, dg = _rms_bwd(acc, n, rstd, gv)
        out_ref[...] = dh_ref[...] + dx
        dg_ref[...] += dg

    back = lambda i: nt - 1 - i
    row = lambda w: pl.BlockSpec((t, w), lambda i: (back(i), 0))
    full = lambda a, b: pl.BlockSpec((a, b), lambda i: (0, 0))
    res = [pl.BlockSpec((dil, t // dil, GROUP_WIDTH), lambda i: (0, back(i), 0)) for dil in DILATIONS]
    tables = [pl.BlockSpec((t, LANES), functools.partial(lambda i, k: (back(i), k), k=k)) for k in range(3)]
    return pl.pallas_call(
        body, name=name, grid=(nt,),
        in_specs=[row(D_MODEL), row(POOL_WIDTH), row(POOL_WIDTH), full(POOL_WIDTH, POOL_WIDTH), full(1, POOL_WIDTH)]
        + res * 3 + tables + [full(N_IN, D_MODEL), row(D_MODEL), full(1, D_MODEL)],
        out_specs=[row(D_MODEL), row(N_IN), full(1, D_MODEL), full(POOL_WIDTH, POOL_WIDTH), full(1, POOL_WIDTH)],
        out_shape=[jax.ShapeDtypeStruct((s, D_MODEL), F32), jax.ShapeDtypeStruct((s, N_IN), BF16),
                   jax.ShapeDtypeStruct((1, D_MODEL), F32), jax.ShapeDtypeStruct((POOL_WIDTH, POOL_WIDTH), F32),
                   jax.ShapeDtypeStruct((1, POOL_WIDTH), F32)],
        scratch_shapes=[pltpu.VMEM((t + POOL_HALO + POOL_PAD, POOL_WIDTH), F32)] * 4 + _stages(t, 9),
        compiler_params=_params(1),
    )(dh, dpool, y, w_bd, scale, *dq, *dk, *dv, rc, rsa, rsb, w_in, h, g)


def _matmul_tn(a, b, name, *, square_a=False, tm=None, tn=None, blocked_out=False, after=()):
    s, m = a.shape
    n = b.shape[1]
    tk = _row_tile(s, 2048)
    tm = tm or min(m, 1024)
    tn = tn or min(n, 1024)
    assert m % tm == 0 and n % tn == 0
    nk = s // tk
    nsub = tn // FF_BLOCK if blocked_out else 1

    def body(a_ref, b_ref, o_ref, ob_ref, acc):
        k = pl.program_id(2)

        def product():
            av = a_ref[...]
            if square_a:
                av = av.astype(F32)
                av = av * av
            return _dot_tn(av.astype(BF16), b_ref[...].astype(BF16))

        def emit(total):
            if blocked_out:
                for sub in range(nsub):
                    cols = slice(sub * FF_BLOCK, (sub + 1) * FF_BLOCK)
                    o_ref[sub] = total[:, cols]
                    ob_ref[sub] = total[:, cols].astype(BF16)
            else:
                o_ref[...] = total
                ob_ref[...] = total.astype(BF16)

        if nk == 1:
            emit(product())
            return

        @pl.when(k == 0)
        def _():
            acc[...] = product()

        @pl.when((k > 0) & (k < nk - 1))
        def _():
            acc[...] += product()

        @pl.when(k == nk - 1)
        def _():
            emit(acc[...] + product())

    if blocked_out:
        shape = (n // FF_BLOCK, m, FF_BLOCK)
        out_spec = pl.BlockSpec((nsub, tm, FF_BLOCK), lambda i, j, k: (j, i, 0))
    else:
        shape = (m, n)
        out_spec = pl.BlockSpec((tm, tn), lambda i, j, k: (i, j))
    return pl.pallas_call(
        _ordered_after(body, 2, after), name=name, grid=(m // tm, n // tn, nk),
        in_specs=[pl.BlockSpec((tk, tm), lambda i, j, k: (k, i)), pl.BlockSpec((tk, tn), lambda i, j, k: (k, j))]
        + [pl.BlockSpec(memory_space=pl.ANY)] * len(after),
        out_specs=[out_spec, out_spec],
        out_shape=[jax.ShapeDtypeStruct(shape, F32), jax.ShapeDtypeStruct(shape, BF16)],
        scratch_shapes=[pltpu.VMEM((tm, tn), F32)],
        compiler_params=_params(3),
    )(a, b, *after)


def _adamw_math(w, g, m, v):
    m = ADAM_B1 * m + (1.0 - ADAM_B1) * g
    v = ADAM_B2 * v + (1.0 - ADAM_B2) * (g * g)
    m_hat = m / (1.0 - ADAM_B1 ** ADAM_STEP)
    v_hat = v / (1.0 - ADAM_B2 ** ADAM_STEP)
    delta = -ADAM_LR * (m_hat / (jnp.sqrt(v_hat) + ADAM_EPS) + ADAM_WD * w)
    return delta, m, v


def _sum_chunks_body(own0_ref, own1_ref, r0_ref, r1_ref):
    layer0 = pl.program_id(0) == 0
    g = jnp.where(layer0, own0_ref[...], own1_ref[...])
    for k in range(N_DEV - 1):
        g = g + jnp.where(layer0, r0_ref[k], r1_ref[k]).astype(F32)
    return g


def _chunk_specs(t, cols):
    rows_of = lambda layer: (lambda l, i: jnp.where(l == layer, i, 0))
    blk = pl.BlockSpec((None, t, cols), lambda l, i, me: (l, i, 0))
    own = [pl.BlockSpec((None, t, cols), functools.partial(lambda l, i, me, pick: (me[0], pick(l, i), 0), pick=rows_of(ly)))
           for ly in range(2)]
    recv = [pl.BlockSpec((N_DEV - 1, t, cols), functools.partial(lambda l, i, me, pick: (0, pick(l, i), 0), pick=rows_of(ly)))
            for ly in range(2)]
    return blk, own + recv


def _sum_chunks(chunks, me, name):
    _, rows, cols = chunks[0].shape
    t = _row_tile(rows, 320)

    def body(me_ref, own0_ref, own1_ref, r0_ref, r1_ref, g_ref):
        g_ref[...] = _sum_chunks_body(own0_ref, own1_ref, r0_ref, r1_ref)

    blk, chunk_specs = _chunk_specs(t, cols)
    return pl.pallas_call(
        body, name=name,
        grid_spec=pltpu.PrefetchScalarGridSpec(num_scalar_prefetch=1, grid=(2, rows // t), in_specs=chunk_specs,
                                               out_specs=blk),
        out_shape=jax.ShapeDtypeStruct((2, rows, cols), F32), compiler_params=_params(2),
    )(me, *chunks)


def _adamw_sharded(w, m, v, grad, me, name):
    _, rows, cols = w.shape
    t = _row_tile(rows, 256)
    summed = not isinstance(grad, tuple)
    grad = (grad,) if summed else grad

    def body(me_ref, w_ref, m_ref, v_ref, *refs):
        g_ref, d_ref, nm_ref, nv_ref = refs[-4:]
        g = refs[0][...] if summed else _sum_chunks_body(*refs[:4])
        g_ref[...] = g
        d_ref[...], nm_ref[...], nv_ref[...] = _adamw_math(w_ref[...], g, m_ref[...], v_ref[...])

    blk, chunk_specs = _chunk_specs(t, cols)
    return pl.pallas_call(
        body, name=name,
        grid_spec=pltpu.PrefetchScalarGridSpec(
            num_scalar_prefetch=1, grid=(2, rows // t),
            in_specs=[blk, blk, blk] + ([blk] if summed else chunk_specs), out_specs=[blk] * 4),
        out_shape=[jax.ShapeDtypeStruct(w.shape, F32)] * 4,
        compiler_params=_params(2),
    )(me, w, m, v, *grad)


def _adamw_packed(w, g8, m, v, name):
    def body(w_ref, g_ref, m_ref, v_ref, go_ref, d_ref, nm_ref, nv_ref):
        g = g_ref[0]
        for dev in range(1, N_DEV):
            g = g + g_ref[dev]
        go_ref[...] = g
        d_ref[...], nm_ref[...], nv_ref[...] = _adamw_math(w_ref[...], g, m_ref[...], v_ref[...])

    return pl.pallas_call(
        body, name=name, out_shape=[jax.ShapeDtypeStruct(w.shape, F32)] * 4,
        compiler_params=pltpu.CompilerParams(vmem_limit_bytes=VMEM_LIMIT),
    )(w, g8, m, v)


def _peer(k):
    x, y, c = lax.axis_index("x"), lax.axis_index("y"), lax.axis_index("c")
    return (1 - x if k & 4 else x, 1 - y if k & 2 else y, 1 - c if k & 1 else c)


def _linear(dev):
    return 4 * dev[0] + 2 * dev[1] + dev[2]


HBM_SPEC = pl.BlockSpec(memory_space=pltpu.HBM)
SEM_SPEC = pl.BlockSpec(memory_space=pltpu.SEMAPHORE)
ANY_SPEC = pl.BlockSpec(memory_space=pl.ANY)
EFFECT = pltpu.SideEffectType.DATAFLOW_SIDE_EFFECTING


def _in_hbm(a):
    return pltpu.with_memory_space_constraint(a, pltpu.HBM)


class _Exchange:
    def __init__(self, name, groups, scatter, after=()):
        self.name, self.scatter = name, scatter
        self.sizes = sizes = [len(g) for g in groups]
        srcs = [a for g in groups for a in g]
        n, ng = len(srcs), len(groups)
        lead = (N_DEV - 1,) if scatter else (N_DEV,)
        shapes = [lead + (a.shape[1:] if scatter else a.shape) for a in srcs]
        lands = [lax.empty(sh, a.dtype) for sh, a in zip(shapes, srcs)]
        offsets = [sum(sizes[:gi]) for gi in range(ng)]
        copy = self._copy

        def body(*refs):
            src, land = refs[:n], refs[n:2 * n]
            sems = refs[2 * n + len(after):2 * n + len(after) + 2 * ng]
            token = refs[-1]
            for gi in range(ng):
                for wi in range(sizes[gi]):
                    w = offsets[gi] + wi
                    for k in range(1, N_DEV):
                        copy(src[w], land[w], sems[2 * gi], sems[2 * gi + 1], wi, k).start()
            token[...] = jnp.zeros_like(token)

        sem_shapes = [pltpu.SemaphoreType.DMA(((N_DEV - 1) * sz,)) for sz in sizes for _ in range(2)]
        outs = pl.pallas_call(
            body, name=name + "_start",
            in_specs=[HBM_SPEC] * (2 * n) + [ANY_SPEC] * len(after),
            out_specs=[SEM_SPEC] * (2 * ng) + [HBM_SPEC] * (2 * n) + [pl.BlockSpec(memory_space=pltpu.VMEM)],
            out_shape=sem_shapes + [pltpu.HBM(a.shape, a.dtype) for a in srcs + lands]
            + [jax.ShapeDtypeStruct((8, LANES), F32)],
            input_output_aliases={i: 2 * ng + i for i in range(2 * n)},
            compiler_params=pltpu.CompilerParams(has_side_effects=EFFECT),
        )(*[_in_hbm(a) for a in srcs + lands], *after)
        self.sems = [outs[2 * gi:2 * gi + 2] for gi in range(ng)]
        thru = outs[2 * ng:2 * ng + 2 * n]
        self.srcs = [thru[offsets[gi]:offsets[gi] + sizes[gi]] for gi in range(ng)]
        self.lands = [thru[n + offsets[gi]:n + offsets[gi] + sizes[gi]] for gi in range(ng)]
        self.token = outs[-1]

    def _copy(self, src, land, send_sems, recv_sems, wi, k):
        to = _peer(k)
        if self.scatter:
            src_ref, dst_ref = src.at[_linear(to)], land.at[k - 1]
        else:
            src_ref, dst_ref = src, land.at[_linear(_peer(0))]
        return pltpu.make_async_remote_copy(
            src_ref=src_ref, dst_ref=dst_ref, send_sem=send_sems.at[(N_DEV - 1) * wi + k - 1],
            recv_sem=recv_sems.at[(N_DEV - 1) * wi + k - 1], device_id=to, device_id_type=MESH)

    def wait(self, gi, after):
        n = self.sizes[gi]
        copy = self._copy
        scatter = self.scatter

        def body(*refs):
            src, land = refs[:n], refs[n:2 * n]
            send_sems, recv_sems = refs[2 * n], refs[2 * n + 1]
            for wi in range(n):
                for k in range(1, N_DEV):
                    cp = copy(src[wi], land[wi], send_sems, recv_sems, wi, k)
                    cp.wait_send()
                    cp.wait_recv()
            if not scatter:
                local_sems = refs[-1]
                me = _linear(_peer(0))
                own = [pltpu.make_async_copy(src[wi], land[wi].at[me], local_sems.at[wi]) for wi in range(n)]
                for cp in own:
                    cp.start()
                for cp in own:
                    cp.wait()

        arrays = list(self.srcs[gi]) + list(self.lands[gi])
        outs = pl.pallas_call(
            body, name=f"{self.name}_wait{gi}",
            in_specs=[HBM_SPEC] * (2 * n) + [SEM_SPEC, SEM_SPEC] + [ANY_SPEC] * len(after),
            out_specs=[HBM_SPEC] * (2 * n),
            out_shape=[pltpu.HBM(a.shape, a.dtype) for a in arrays],
            input_output_aliases={i: i for i in range(2 * n)},
            scratch_shapes=[] if scatter else [pltpu.SemaphoreType.DMA((n,))],
            compiler_params=pltpu.CompilerParams(has_side_effects=EFFECT),
        )(*arrays, *self.sems[gi], *after)
        return outs[:n], outs[n:]


def _rotary_tables(positions):
    rot_dim = HEAD_DIM // 4
    inv_freq = ROPE_THETA ** (-jnp.arange(0, rot_dim, 2, dtype=F32) / rot_dim)
    ang = positions.astype(F32)[:, None] * inv_freq
    cs = jnp.concatenate([jnp.cos(ang), jnp.sin(ang)], axis=1)
    dim = jnp.arange(LANES) % HEAD_DIM
    first, second = dim < ROT_SHIFT, (dim >= ROT_SHIFT) & (dim < rot_dim)
    src = jnp.arange(2 * ROT_SHIFT)[:, None]
    angle = (dim % ROT_SHIFT)[None, :]
    c = jnp.where((first | second)[None, :] & (src == angle), 1.0, 0.0)
    sa = jnp.where(second[None, :] & (src == angle + ROT_SHIFT), 1.0, 0.0)
    sb = jnp.where(first[None, :] & (src == angle + ROT_SHIFT), -1.0, 0.0)
    spread = jnp.concatenate([c, sa, sb], axis=1).astype(F32)
    base = jnp.concatenate([jnp.where(first | second, 0.0, 1.0), jnp.zeros((2 * LANES,))]).astype(F32)[None, :]
    return jnp.dot(cs, spread, precision=lax.Precision.HIGHEST, preferred_element_type=F32) + base


def _block_diag(pool_w):
    gc = pool_w.shape[-1]
    out = jnp.zeros((POOL_WIDTH, POOL_WIDTH), pool_w.dtype)
    for grp in range(pool_w.shape[0]):
        out = lax.dynamic_update_slice(out, pool_w[grp], (grp * gc, grp * gc))
    return out


def _diag_blocks(a):
    gc = POOL_WIDTH // len(POOL_WINDOWS)
    return jnp.stack([a[grp * gc:(grp + 1) * gc, grp * gc:(grp + 1) * gc] for grp in range(len(POOL_WINDOWS))])


def _local_step(x, p, positions, loss_target, norm1, pool_w, pool_scale, norm2, norm3, final_norm, weights, send):
    rc = rsa = rsb = _rotary_tables(positions)
    ones_bd = _block_diag(jnp.ones((4, HEAD_DIM, HEAD_DIM), BF16))
    saved = []
    h = x
    for i in range(2):
        tag = f"_l{i}"
        g1, g2, g3 = norm1[i:i + 1], norm2[i:i + 1], norm3[i:i + 1]
        w_bd = _block_diag(pool_w[i]).astype(BF16)
        scale = pool_scale[i:i + 1]
        if i == 0:
            w_in = weights(i, "in", (h, rc, w_bd))
            hn1, u, *qkv = _normproj_fwd(h, g1, w_in, rc, rsa, rsb, "normproj_fwd" + tag)
        else:
            w_in, (hn1, u, *qkv) = ahead
        qkv = [qkv[3 * grp:3 * grp + 3] for grp in range(3)]
        started = weights(i, "prefetch", (hn1,))
        o, lse = zip(*[_attn_fwd(*qkv[grp], f"attn_fwd{tag}_g{grp}", after=started) for grp in range(3)])
        w_out = weights(i, "out", o)
        h1, a, y = _outproj_fwd(h, u, w_bd, scale, o, lse, w_out, "outproj_fwd" + tag)
        w_up, w_down = weights(i, "mlp", (h1,))
        h2, hn2, r = _mlp_fwd(h1, g2, w_up, w_down, "mlp_fwd" + tag)
        w_gate, w_ple = weights(i, "gate", (h2,))
        h0 = h
        if i == 0:
            w_in_next = weights(1, "in", (h2,))
            h, hn3, gate, pb, *ahead = _gate_fwd(h2, g3, w_gate, p, i, w_ple, "gate_normproj_fwd",
                                                 follow=(norm1[1:2], w_in_next, rc, rsa, rsb))
            ahead = (w_in_next, ahead)
        else:
            hn3 = gate = pb = None
            loss, d_final, *top = _gate_fwd(h2, g3, w_gate, p, i, w_ple, "gate_loss_gate_bwd",
                                            head=(final_norm.reshape(1, D_MODEL), loss_target))
        saved.append(dict(h0=h0, hn1=hn1, qkv=qkv, y=y, o=o, lse=lse, a=a, h1=h1, hn2=hn2, r=r, h2=h2,
                          hn3=hn3, gate=gate, pb=pb, w_bd=w_bd, scale=scale, g1=g1, g2=g2, g3=g3,
                          w_in=w_in, w_out=w_out, w_up=w_up, w_down=w_down, w_gate=w_gate, w_ple=w_ple))

    grads = [None, None]
    sent = ()
    for i in (1, 0):
        tag = f"_l{i}"
        sv = saved[i]
        if i == 1:
            dh2, dg3, dw_gate, dw_ple = top
        else:
            dh2, dg3, dw_gate, dw_ple = _gate_bwd(dh, sv["gate"], sv["pb"], sv["w_ple"], sv["h2"], sv["g3"],
                                                  sv["w_gate"], sv["hn3"], "gate_bwd" + tag, after=sent)
        dh1, dup, dg2, dh2b = _mlp_bwd(dh2, sv["r"], sv["h1"], sv["g2"], sv["w_up"], sv["w_down"], "mlp_bwd" + tag)
        dw_down = _matmul_tn(sv["r"], dh2b, "dw_down" + tag, square_a=True)
        dw_up = _matmul_tn(sv["hn2"], dup, "dw_up" + tag, blocked_out=True)
        dpool, do0, do1, do2, de0, de1, de2, dw_out = _outproj_bwd(dh1, sv["w_out"], sv["o"], sv["lse"], ones_bd,
                                                                   sv["a"], "outproj_bwd" + tag)
        sent = send(i, "main", dict(w_gate=dw_gate, w_ple=dw_ple, w_down=dw_down, w_up=dw_up, w_out=dw_out))
        dqkv = [_attn_bwd(*sv["qkv"][grp], do_g, sv["lse"][grp], de_g, f"attn_bwd{tag}_g{grp}", after=sent)
                for grp, (do_g, de_g) in enumerate(((do0, de0), (do1, de1), (do2, de2)))]
        dq, dk, dv = zip(*dqkv)
        dh, dz, dg1, dw_bd, dscale = _normproj_bwd(dh1, dpool, sv["y"], sv["w_bd"], sv["scale"], dq, dk, dv, rc, rsa, rsb,
                                                   sv["w_in"], sv["h0"], sv["g1"], "normproj_bwd" + tag)
        grads[i] = dict(norm1=dg1, norm2=dg2, norm3=dg3, pool_w=_diag_blocks(dw_bd), pool_scale=dscale)
        small_sent = send(0, "small", (grads, d_final, loss)) if i == 0 else ()
        dw_in = _matmul_tn(dz, sv["hn1"], "dw_in" + tag, tm=N_IN // 2, after=small_sent)
        sent = send(i, "in", dict(w_in=dw_in))
    return dh, sent


def _pack_small(norm1, norm2, norm3, final_norm, pool_scale, pool_w, spare=None):
    spare = jnp.zeros((1, LANES), F32) if spare is None else spare
    scale_row = jnp.concatenate([pool_scale.reshape(1, 2 * POOL_WIDTH), spare,
                                 jnp.zeros((1, D_MODEL - 2 * POOL_WIDTH - LANES), F32)], axis=1)
    return jnp.concatenate([norm1, norm2, norm3, final_norm.reshape(1, D_MODEL), scale_row,
                            pool_w.reshape(32, D_MODEL)], axis=0)


def _unpack_small(a):
    return dict(norm1=a[0:2], norm2=a[2:4], norm3=a[4:6], final_norm=a[6], pool_scale=a[7, 0:2 * POOL_WIDTH].reshape(2, POOL_WIDTH),
                pool_w=a[8:40].reshape(2, 4, HEAD_DIM, HEAD_DIM))


def _chunks_cols(a, cols):
    return a.reshape(a.shape[0], N_DEV, cols).transpose(1, 0, 2)


def _chunks_rows(a, rows):
    return a.reshape(N_DEV, rows, a.shape[1])


BIG = ("w_in", "w_out", "w_up", "w_down", "w_gate", "w_ple")
SMALL = ("norm1", "norm2", "norm3", "final_norm", "pool_scale", "pool_w")
ORDER = ("norm1", "w_in", "pool_w", "pool_scale", "w_out", "norm2", "w_up", "w_down", "norm3", "w_gate", "w_ple",
         "final_norm")


def kernel(x, p, positions, norm1, w_in, pool_w, pool_scale, w_out, norm2, w_up, w_down, norm3, w_gate, w_ple, final_norm, loss_target, m_norm1, m_w_in, m_pool_w, m_pool_scale, m_w_out, m_norm2, m_w_up, m_w_down, m_norm3, m_w_gate, m_w_ple, m_final_norm, v_norm1, v_w_in, v_pool_w, v_pool_scale, v_w_out, v_norm2, v_w_up, v_w_down, v_norm3, v_w_gate, v_w_ple, v_final_norm):
    w = dict(norm1=norm1, w_in=w_in, pool_w=pool_w, pool_scale=pool_scale, w_out=w_out, norm2=norm2, w_up=w_up,
             w_down=w_down, norm3=norm3, w_gate=w_gate, w_ple=w_ple, final_norm=final_norm)
    m = dict(norm1=m_norm1, w_in=m_w_in, pool_w=m_pool_w, pool_scale=m_pool_scale, w_out=m_w_out, norm2=m_norm2,
             w_up=m_w_up, w_down=m_w_down, norm3=m_norm3, w_gate=m_w_gate, w_ple=m_w_ple, final_norm=m_final_norm)
    v = dict(norm1=v_norm1, w_in=v_w_in, pool_w=v_pool_w, pool_scale=v_pool_scale, w_out=v_w_out, norm2=v_norm2,
             w_up=v_w_up, w_down=v_w_down, norm3=v_norm3, w_gate=v_w_gate, w_ple=v_w_ple, final_norm=v_final_norm)
    seq = x.shape[1]

    bf = {n: [w[n][layer].astype(BF16) for layer in range(2)] for n in BIG}
    bf["w_in"] = [a.T for a in bf["w_in"]]
    me = 4 * lax.axis_index("x") + 2 * lax.axis_index("y") + lax.axis_index("c")
    parts = dict(zip(("in", "out", "mlp", "gate"), (("w_in",), ("w_out",), ("w_up", "w_down"), ("w_gate", "w_ple"))))
    first = _Exchange("gather_first", [[bf["w_in"][0]]], scatter=False)
    later = [pt for pt in parts if pt != "in"]
    gathers = [_Exchange("gather_l0", [[bf[n][0] for n in parts[pt]] for pt in later], scatter=False,
                         after=(first.token,))]
    unpack = dict(w_in=lambda a: a.reshape(N_IN, D_MODEL),
                  w_out=lambda a: a.reshape(D_MODEL, D_MODEL), w_gate=lambda a: a.reshape(D_MODEL, D_MODEL),
                  w_ple=lambda a: a.transpose(1, 0, 2).reshape(PLE_DIM, D_MODEL), w_up=lambda a: a, w_down=lambda a: a)

    def weights(layer, part, after):
        if part == "prefetch":
            if layer != 0:
                return ()
            gathers.append(_Exchange("gather_l1", [[bf[n][1] for n in parts[pt]] for pt in parts], scatter=False,
                                     after=after))
            return (gathers[1].token,)
        if layer == 0 and part == "in":
            shards, lands = first.wait(0, (*after, gathers[0].token))
        elif layer == 0:
            shards, lands = gathers[0].wait(later.index(part), after)
        else:
            shards, lands = gathers[1].wait(tuple(parts).index(part), after)
        full = [unpack[n](land) for n, land in zip(parts[part], lands)]
        return full if len(full) > 1 else full[0]

    to_chunks = dict(w_in=lambda a: _chunks_rows(a, N_IN // N_DEV),
                     w_out=lambda a: _chunks_rows(a, D_MODEL // N_DEV),
                     w_up=lambda a: a, w_down=lambda a: _chunks_rows(a, FF_BLOCK),
                     w_gate=lambda a: _chunks_rows(a, D_MODEL // N_DEV), w_ple=lambda a: _chunks_cols(a, D_MODEL // N_DEV))
    own = {n: [None, None] for n in BIG}
    scatters = {}

    def send(layer, part, grads):
        if part == "small":
            per_layer, d_final, loss = grads
            pack = _pack_small(
                *[jnp.concatenate([per_layer[0][n], per_layer[1][n]], axis=0) for n in ("norm1", "norm2", "norm3")],
                d_final.reshape(D_MODEL),
                jnp.concatenate([per_layer[0]["pool_scale"], per_layer[1]["pool_scale"]], axis=0),
                jnp.stack([per_layer[0]["pool_w"], per_layer[1]["pool_w"]]), spare=loss)
            scatters["small"] = _Exchange("gather_small", [[pack]], scatter=False)
            return (scatters["small"].token,)
        for n, (g32, _) in grads.items():
            own[n][layer] = to_chunks[n](g32)
        ex = _Exchange(f"scatter_{part}_l{layer}", [[to_chunks[n](g16) for n, (_, g16) in grads.items()]], scatter=True)
        scatters[layer, part] = (tuple(grads), ex)
        return (ex.token,)

    dx, sent = _local_step(
        x.reshape(seq, D_MODEL), p.reshape(2, seq, PLE_DIM), positions.reshape(seq), loss_target.reshape(seq, D_MODEL),
        norm1, pool_w, pool_scale, norm2, norm3, final_norm, weights, send)

    g_out, d_out, m_out, v_out = {}, {}, {}, {}
    my_index = me.reshape(1)
    for part in ("main", "in"):
        recv = {}
        for layer in (1, 0):
            names, ex = scatters[layer, part]
            for n, r in zip(names, ex.wait(0, sent)[1]):
                recv[n, layer] = r
        for n in names:
            grad = (*own[n], recv[n, 0], recv[n, 1])
            if n == "w_in":
                grad = _sum_chunks(grad, my_index, "sum_w_in").transpose(0, 2, 1)
            g_out[n], d_out[n], m_out[n], v_out[n] = _adamw_sharded(w[n], m[n], v[n], grad, my_index, "adamw_" + n)
        sent = tuple(d_out[n] for n in names)
    _, (small_g8,) = scatters["small"].wait(0, sent)
    pack = lambda t: _pack_small(*[t[n] for n in SMALL])
    small_g, d_small, m_small, v_small = _adamw_packed(pack(w), small_g8, pack(m), pack(v), "adamw_small")
    for dst, a in ((g_out, small_g), (d_out, d_small), (m_out, m_small), (v_out, v_small)):
        dst.update(_unpack_small(a))

    return (small_g[7, 2 * POOL_WIDTH],dx.reshape(1, seq, D_MODEL), *[g_out[n] for n in ORDER], *[d_out[n] for n in ORDER],
            *[m_out[n] for n in ORDER], *[v_out[n] for n in ORDER])
```

```python
import functools

import jax
import jax.numpy as jnp
from jax import lax
from jax.experimental import pallas as pl
from jax.experimental.pallas import tpu as pltpu

F32 = jnp.float32
BF16 = jnp.bfloat16

D_MODEL = 1024
HEAD_DIM = 64
POOL_WIDTH = 256
POOL_WINDOWS = (2, 4, 8, 16)
POOL_HALO = 16
POOL_PAD = 8
GROUP_WIDTH = 256
DILATIONS = (1, 4, 16)
ATTN_BLOCK = 128
ROT_SHIFT = 8
ROPE_THETA = 500000.0
D_FF = 4096
FF_BLOCK = 512
FF_PER_STEP = 2
MLP_BWD_TILE = 512
FWD_TILE = 1024
N_DEV = 8
N_IN = POOL_WIDTH + 3 * 768
PLE_DIM = 256
EPS = 1e-6
NEG_BIG = -1e30

ADAM_LR = 0.001
ADAM_B1 = 0.9
ADAM_B2 = 0.999
ADAM_EPS = 1e-08
ADAM_WD = 0.01
ADAM_STEP = 10

LANES = 128
SUBLANES = 8
VMEM_LIMIT = 56 * 1024 * 1024
MESH = pl.DeviceIdType.MESH


def _params(n_grid):
    return pltpu.CompilerParams(dimension_semantics=("arbitrary",) * n_grid, vmem_limit_bytes=VMEM_LIMIT)


def _dot(a, b):
    return jnp.dot(a, b, preferred_element_type=F32)


def _dot_nt(a, b):
    return lax.dot_general(a, b, (((1,), (1,)), ((), ())), preferred_element_type=F32)


def _dot_tn(a, b):
    return lax.dot_general(a, b, (((0,), (0,)), ((), ())), preferred_element_type=F32)


def _rms(x, g):
    rstd = lax.rsqrt(jnp.mean(x * x, axis=-1, keepdims=True) + EPS)
    n = x * rstd
    return n, rstd, n * g


def _rms_bwd(dy, n, rstd, g):
    dyn = dy * g
    dx = rstd * (dyn - n * jnp.mean(dyn * n, axis=-1, keepdims=True))
    return dx, jnp.sum(dy * n, axis=0, keepdims=True)


def _ordered_after(body, n_in, after):
    if not after:
        return body
    return lambda *refs: body(*refs[:n_in], *refs[n_in + len(after):])


def _resident(shape):
    return pl.BlockSpec(shape, lambda i: (0,) * len(shape), pipeline_mode=pl.Buffered(1))


def _row_tile(s, t):
    t = min(s, t)
    assert s % t == 0
    return t


def _rot(z, c, sa, sb):
    return z * c + pltpu.roll(z, ROT_SHIFT, 1) * sa + pltpu.roll(z, LANES - ROT_SHIFT, 1) * sb


def _table_specs(t):
    return [pl.BlockSpec((t, LANES), functools.partial(lambda i, k: (i, k), k=k)) for k in range(3)]


def _rot_t(dz, c, sa, sb):
    return dz * c + pltpu.roll(dz * sa, LANES - ROT_SHIFT, 1) + pltpu.roll(dz * sb, ROT_SHIFT, 1)


def _to_residues(value, stage, out_ref, dil):
    if dil == 1:
        out_ref[0] = value.astype(out_ref.dtype)
        return
    rows = value.shape[0] // dil
    for hf in range(GROUP_WIDTH // LANES):
        lanes = slice(hf * LANES, (hf + 1) * LANES)
        stage[hf][...] = value[:, lanes]
        for r in range(dil):
            out_ref[r, :, lanes] = stage[hf][pl.ds(r, rows, stride=dil), :].astype(out_ref.dtype)


def _from_residues(in_ref, stage, dil):
    if dil == 1:
        return in_ref[0].astype(F32)
    rows = in_ref.shape[1]
    for hf in range(GROUP_WIDTH // LANES):
        for r in range(dil):
            stage[hf][pl.ds(r, rows, stride=dil), :] = in_ref[r, :, hf * LANES:(hf + 1) * LANES].astype(F32)
    return jnp.concatenate([stage[0][...], stage[1][...]], axis=1)


def _residue_spec(dil, t):
    return pl.BlockSpec((dil, t // dil, GROUP_WIDTH), lambda i: (0, i, 0))


def _residue_shape(dil, s, dtype):
    return jax.ShapeDtypeStruct((dil, s // dil, GROUP_WIDTH), dtype)


def _stages(t, n):
    return [pltpu.VMEM((t, LANES), F32)] * (n * (GROUP_WIDTH // LANES))


def _pair_stages(refs):
    return [refs[i:i + 2] for i in range(0, len(refs), 2)]


def _normproj_tile(x, g_ref, w_ref, c_ref, sa_ref, sb_ref, hn_ref, u_ref, *rest):
    qkv_refs, stages = rest[:9], _pair_stages(rest[9:])
    _, _, hn = _rms(x, g_ref[...])
    hb = hn.astype(BF16)
    hn_ref[...] = hb
    c, sa, sb = c_ref[...], sa_ref[...], sb_ref[...]

    def rot(z, scale):
        halves = [_rot(z[:, hf * LANES:(hf + 1) * LANES], c, sa, sb) * scale for hf in range(2)]
        return jnp.concatenate(halves, axis=1)

    proj = lambda lo: _dot_nt(hb, w_ref[lo:lo + GROUP_WIDTH, :])
    u_ref[...] = proj(0)
    for grp, dil in enumerate(DILATIONS):
        lo = POOL_WIDTH + grp * GROUP_WIDTH
        q_ref, k_ref, v_ref = qkv_refs[3 * grp:3 * grp + 3]
        _to_residues(rot(proj(lo), HEAD_DIM ** -0.5), stages[0], q_ref, dil)
        _to_residues(rot(proj(lo + 768), 1.0), stages[1], k_ref, dil)
        _to_residues(proj(lo + 1536), stages[2], v_ref, dil)


def _normproj_operands(s, t):
    row = lambda w: pl.BlockSpec((t, w), lambda i: (i, 0))
    in_specs = [pl.BlockSpec((1, D_MODEL), lambda i: (0, 0)), _resident((N_IN, D_MODEL))] + _table_specs(t)
    out_specs = [row(D_MODEL), row(POOL_WIDTH)] + [_residue_spec(dil, t) for dil in DILATIONS for _ in range(3)]
    out_shape = [jax.ShapeDtypeStruct((s, D_MODEL), BF16), jax.ShapeDtypeStruct((s, POOL_WIDTH), F32)]
    out_shape += [_residue_shape(dil, s, BF16) for dil in DILATIONS for _ in range(3)]
    return in_specs, out_specs, out_shape, _stages(t, 3)


def _normproj_fwd(h, g, w_in, rc, rsa, rsb, name):
    s = h.shape[0]
    t = _row_tile(s, FWD_TILE)

    def body(h_ref, *refs):
        _normproj_tile(h_ref[...], *refs)

    in_specs, out_specs, out_shape, scratch = _normproj_operands(s, t)
    return pl.pallas_call(
        body, name=name, grid=(s // t,), in_specs=[pl.BlockSpec((t, D_MODEL), lambda i: (i, 0))] + in_specs,
        out_specs=out_specs, out_shape=out_shape, scratch_shapes=scratch, compiler_params=_params(1),
    )(h, g, w_in, rc, rsa, rsb)


def _pool_lane_window():
    lane = lax.broadcasted_iota(jnp.int32, (1, POOL_WIDTH), 1)
    return jnp.left_shift(2, lane // (POOL_WIDTH // len(POOL_WINDOWS)))


def _window_sums(ext, b2, b4, b8, t, lo, tile, direction):
    rows = t + POOL_HALO
    for src, dst, sh in ((ext, b2, 1), (b2, b4, 2), (b4, b8, 4)):
        dst[lo:lo + rows, :] = src[lo:lo + rows, :] + src[lo + direction * sh:lo + direction * sh + rows, :]
    s16 = b8[tile:tile + t, :] + b8[tile + direction * 8:tile + direction * 8 + t, :]
    win = _pool_lane_window()
    return jnp.where(win == 2, b2[tile:tile + t, :],
                     jnp.where(win == 4, b4[tile:tile + t, :], jnp.where(win == 8, b8[tile:tile + t, :], s16)))


def _pool_fwd_tile(i, u_ref, w_ref, sc_ref, y_ref, ext, b2, b4, b8):
    t = u_ref.shape[0]
    first = POOL_PAD + POOL_HALO

    @pl.when(i == 0)
    def _():
        for buf in (ext, b2, b4):
            buf[0:POOL_PAD, :] = jnp.zeros((POOL_PAD, POOL_WIDTH), F32)
        ext[POOL_PAD:first, :] = jnp.zeros((POOL_HALO, POOL_WIDTH), F32)

    x = u_ref[...]
    ext[first:, :] = x
    wsum = _window_sums(ext, b2, b4, b8, t, POOL_PAD, first, -1)
    pos = i * t + lax.broadcasted_iota(jnp.int32, (t, POOL_WIDTH), 0)
    cnt = jnp.minimum(pos + 1, _pool_lane_window()).astype(F32)
    yb = (wsum / cnt - x).astype(BF16)
    y_ref[...] = yb
    ext[POOL_PAD:first, :] = x[t - POOL_HALO:, :]
    return _dot(yb, w_ref[...]) * sc_ref[...]


def _head_masks():
    lane = lax.broadcasted_iota(jnp.int32, (ATTN_BLOCK, GROUP_WIDTH), 1)
    return [lane // HEAD_DIM == hd for hd in range(GROUP_WIDTH // HEAD_DIM)]


def _stack_heads(a, masks):
    zero = jnp.zeros_like(a)
    return jnp.concatenate([jnp.where(m, a, zero) for m in masks], axis=0)


def _band_bias(first_step):
    rows = ATTN_BLOCK * (GROUP_WIDTH // HEAD_DIM)
    i = lax.broadcasted_iota(jnp.int32, (rows, 2 * ATTN_BLOCK), 0) & (ATTN_BLOCK - 1)
    j = lax.broadcasted_iota(jnp.int32, (rows, 2 * ATTN_BLOCK), 1)
    inner = jnp.where((j >= i) & (j <= i + ATTN_BLOCK), 0.0, NEG_BIG)
    return jnp.where((j < ATTN_BLOCK) & first_step, NEG_BIG, inner), inner


def _column_per_head(a):
    return jnp.concatenate([a[:, hd * HEAD_DIM:hd * HEAD_DIM + 1] for hd in range(GROUP_WIDTH // HEAD_DIM)], axis=0)


def _blocks_per_step(nb):
    if nb <= 16:
        return nb
    return next(qb for qb in (16, 8, 4, 2, 1) if nb % qb == 0)


def _residues_per_step(dil, nb, qb):
    return 2 if (nb == qb and qb < 8 and dil % 2 == 0) else 1


def _attn_fwd(q, k, v, name, after=()):
    dil, length, _ = q.shape
    nb = length // ATTN_BLOCK
    qb = _blocks_per_step(nb)
    rs = _residues_per_step(dil, nb, qb)

    def body(q_ref, kp_ref, kc_ref, vp_ref, vc_ref, o_ref, lse_ref):
        masks = _head_masks()
        bias = _band_bias(pl.program_id(1) == 0)
        for rr in range(rs):
            for qi in range(qb):
                here = slice(qi * ATTN_BLOCK, (qi + 1) * ATTN_BLOCK)
                before = slice((qi - 1) * ATTN_BLOCK, qi * ATTN_BLOCK)
                kcat = jnp.concatenate([kp_ref[rr] if qi == 0 else kc_ref[rr, before], kc_ref[rr, here]], axis=0)
                vcat = jnp.concatenate([vp_ref[rr] if qi == 0 else vc_ref[rr, before], vc_ref[rr, here]], axis=0)
                qs = _stack_heads(q_ref[rr, here], masks)
                sc = _dot_nt(qs, kcat) + bias[min(qi, 1)]
                m = jnp.max(sc, axis=1, keepdims=True)
                e = jnp.exp(sc - m)
                l = jnp.sum(e, axis=1, keepdims=True)
                p = (e / l).astype(BF16)
                lse = m + jnp.log(l)
                o = jnp.zeros((ATTN_BLOCK, GROUP_WIDTH), F32)
                lse_full = jnp.zeros((ATTN_BLOCK, GROUP_WIDTH), F32)
                for hd, msk in enumerate(masks):
                    rows = slice(hd * ATTN_BLOCK, (hd + 1) * ATTN_BLOCK)
                    o = jnp.where(msk, _dot(p[rows], vcat), o)
                    lse_full = jnp.where(msk, lse[rows], lse_full)
                o_ref[rr, here] = o.astype(o_ref.dtype)
                lse_ref[rr, here] = lse_full

    cur = pl.BlockSpec((rs, qb * ATTN_BLOCK, GROUP_WIDTH), lambda r, j: (r, j, 0))
    prev = pl.BlockSpec((rs, ATTN_BLOCK, GROUP_WIDTH), lambda r, j: (r, jnp.maximum(qb * j - 1, 0), 0))
    return pl.pallas_call(
        _ordered_after(body, 5, after), name=name, grid=(dil // rs, nb // qb),
        in_specs=[cur, prev, cur, prev, cur] + [pl.BlockSpec(memory_space=pl.ANY)] * len(after), out_specs=[cur, cur],
        out_shape=[jax.ShapeDtypeStruct(q.shape, BF16), jax.ShapeDtypeStruct(q.shape, F32)],
        compiler_params=_params(2),
    )(q, k, k, v, v, *after)


def _group_weights(l0, l1, l2):
    m = jnp.maximum(jnp.maximum(l0, l1), l2)
    e0, e1, e2 = jnp.exp(l0 - m), jnp.exp(l1 - m), jnp.exp(l2 - m)
    den = e0 + e1 + e2
    return e0 / den, e1 / den, e2 / den


def _outproj_fwd(h, u, w_bd, scale, o, lse, w_out, name):
    s = h.shape[0]
    t = _row_tile(s, FWD_TILE)

    def body(h_ref, u_ref, wbd_ref, sc_ref, o0, o1, o2, l0, l1, l2, w_ref, out_ref, a_ref, y_ref, ext, b2, b4, b8,
             *stages):
        pool_out = _pool_fwd_tile(pl.program_id(0), u_ref, wbd_ref, sc_ref, y_ref, ext, b2, b4, b8)
        stages = _pair_stages(stages)
        ov = [_from_residues(r, stages[i], DILATIONS[i]) for i, r in enumerate((o0, o1, o2))]
        lv = [_from_residues(r, stages[3 + i], DILATIONS[i]) for i, r in enumerate((l0, l1, l2))]
        wts = _group_weights(*lv)
        a = jnp.concatenate([pool_out] + [ov[i] * wts[i] for i in range(3)], axis=1).astype(BF16)
        a_ref[...] = a
        out_ref[...] = h_ref[...] + _dot(a, w_ref[...])

    row = lambda w: pl.BlockSpec((t, w), lambda i: (i, 0))
    res = [_residue_spec(dil, t) for dil in DILATIONS]
    return pl.pallas_call(
        body, name=name, grid=(s // t,),
        in_specs=[row(D_MODEL), row(POOL_WIDTH), _resident((POOL_WIDTH, POOL_WIDTH)), _resident((1, POOL_WIDTH))]
        + res + res + [_resident((D_MODEL, D_MODEL))],
        out_specs=[row(D_MODEL), row(D_MODEL), row(POOL_WIDTH)],
        out_shape=[jax.ShapeDtypeStruct((s, D_MODEL), F32), jax.ShapeDtypeStruct((s, D_MODEL), BF16),
                   jax.ShapeDtypeStruct((s, POOL_WIDTH), BF16)],
        scratch_shapes=[pltpu.VMEM((t + POOL_HALO + POOL_PAD, POOL_WIDTH), F32)] * 4 + _stages(t, 6),
        compiler_params=_params(1),
    )(h, u, w_bd, scale, *o, *lse, w_out)


def _mlp_fwd(h, g, w_up, w_down, name):
    s = h.shape[0]
    t = _row_tile(s, 512)
    nblk = D_FF // FF_BLOCK

    def body(h_ref, g_ref, wu_ref, wd_ref, out_ref, hn_ref, r_ref):
        x = h_ref[...]
        _, _, hn = _rms(x, g_ref[...])
        hb = hn.astype(BF16)
        hn_ref[...] = hb
        acc = None
        for b0 in range(0, nblk, FF_PER_STEP):
            acts = []
            for b in range(b0, b0 + FF_PER_STEP):
                r = jnp.maximum(_dot(hb, wu_ref[b]), 0.0)
                r_ref[:, b * FF_BLOCK:(b + 1) * FF_BLOCK] = r.astype(BF16)
                acts.append((r * r).astype(BF16))
            wd = wd_ref[b0:b0 + FF_PER_STEP].reshape(FF_PER_STEP * FF_BLOCK, D_MODEL)
            part = _dot(jnp.concatenate(acts, axis=1), wd)
            acc = part if acc is None else acc + part
        out_ref[...] = x + acc

    row = lambda w: pl.BlockSpec((t, w), lambda i: (i, 0))
    resident = lambda shape: pl.BlockSpec(shape, lambda i: (0, 0, 0), pipeline_mode=pl.Buffered(1))
    return pl.pallas_call(
        body, name=name, grid=(s // t,),
        in_specs=[row(D_MODEL), pl.BlockSpec((1, D_MODEL), lambda i: (0, 0)),
                  resident((nblk, D_MODEL, FF_BLOCK)), resident((nblk, FF_BLOCK, D_MODEL))],
        out_specs=[row(D_MODEL), row(D_MODEL), row(D_FF)],
        out_shape=[jax.ShapeDtypeStruct((s, D_MODEL), F32), jax.ShapeDtypeStruct((s, D_MODEL), BF16),
                   jax.ShapeDtypeStruct((s, D_FF), BF16)],
        compiler_params=_params(1),
    )(h, g, w_up, w_down)


def _gate_fwd(h, g, w_gate, p, layer, w_ple, name, head=None, follow=None):
    assert (head is None) != (follow is None)
    s = h.shape[0]
    t = _row_tile(s, 512)
    last = s // t - 1

    def body(h_ref, g_ref, wg_ref, p_ref, wp_ref, *refs):
        x = h_ref[...]
        gv = g_ref[...]
        n, rstd, hn = _rms(x, gv)
        hb = hn.astype(BF16)
        gate = 1.0 / (1.0 + jnp.exp(-_dot(hb, wg_ref[...])))
        pb = p_ref[...].astype(BF16)
        e = _dot(pb, wp_ref[...])
        h3 = x + gate * e
        if follow is not None:
            out_ref, hn_ref, gate_ref, pb_ref = refs[5:9]
            out_ref[...] = h3
            hn_ref[...] = hb
            pb_ref[...] = pb
            gate_ref[...] = gate.astype(BF16)
            _normproj_tile(h3, *refs[:5], *refs[9:])
            return
        gf_ref, t_ref, loss_ref, dgf_ref, out_ref, dg_ref, dwg_ref, dwgb_ref, dwp_ref, dwpb_ref = refs
        i = pl.program_id(0)

        @pl.when(i == 0)
        def _():
            for ref in (loss_ref, dgf_ref, dg_ref, dwg_ref, dwp_ref):
                ref[...] = jnp.zeros_like(ref)

        gf = gf_ref[...]
        n3, rstd3, y = _rms(h3, gf)
        err = y - t_ref[...]
        loss_ref[...] += jnp.sum(err * err) * (0.5 / D_MODEL)
        d, dgf = _rms_bwd(err * (1.0 / D_MODEL), n3, rstd3, gf)
        dgf_ref[...] += dgf
        dgl = (d * e * gate * (1.0 - gate)).astype(BF16)
        dwg_ref[...] += _dot_tn(hb, dgl)
        dwp_ref[...] += _dot_tn(pb, (d * gate).astype(BF16))
        dx, dg = _rms_bwd(_dot_nt(dgl, wg_ref[...]), n, rstd, gv)
        out_ref[...] = d + dx
        dg_ref[...] += dg

        @pl.when(i == last)
        def _():
            dwgb_ref[...] = dwg_ref[...].astype(BF16)
            dwpb_ref[...] = dwp_ref[...].astype(BF16)

    row = lambda w: pl.BlockSpec((t, w), lambda i: (i, 0))
    full = lambda a, b: pl.BlockSpec((a, b), lambda i: (0, 0))
    in_specs = [row(D_MODEL), full(1, D_MODEL), _resident((D_MODEL, D_MODEL)),
                pl.BlockSpec((None, t, PLE_DIM), lambda i: (layer, i, 0)), _resident((PLE_DIM, D_MODEL))]
    if follow is not None:
        next_in, next_out, next_shape, scratch = _normproj_operands(s, t)
        return pl.pallas_call(
            body, name=name, grid=(s // t,), in_specs=in_specs + next_in,
            out_specs=[row(D_MODEL), row(D_MODEL), row(D_MODEL), row(PLE_DIM)] + next_out,
            out_shape=[jax.ShapeDtypeStruct((s, D_MODEL), F32), jax.ShapeDtypeStruct((s, D_MODEL), BF16),
                       jax.ShapeDtypeStruct((s, D_MODEL), BF16), jax.ShapeDtypeStruct((s, PLE_DIM), BF16)] + next_shape,
            scratch_shapes=scratch, compiler_params=_params(1),
        )(h, g, w_gate, p, w_ple, *follow)
    loss, dgf, dh2, dg, dwg, dwgb, dwp, dwpb = pl.pallas_call(
        body, name=name, grid=(s // t,), in_specs=in_specs + [full(1, D_MODEL), row(D_MODEL)],
        out_specs=[pl.BlockSpec((1, LANES), lambda i: (0, 0)), full(1, D_MODEL), row(D_MODEL), full(1, D_MODEL),
                   full(D_MODEL, D_MODEL), full(D_MODEL, D_MODEL), full(PLE_DIM, D_MODEL), full(PLE_DIM, D_MODEL)],
        out_shape=[jax.ShapeDtypeStruct((1, LANES), F32), jax.ShapeDtypeStruct((1, D_MODEL), F32),
                   jax.ShapeDtypeStruct((s, D_MODEL), F32), jax.ShapeDtypeStruct((1, D_MODEL), F32),
                   jax.ShapeDtypeStruct((D_MODEL, D_MODEL), F32), jax.ShapeDtypeStruct((D_MODEL, D_MODEL), BF16),
                   jax.ShapeDtypeStruct((PLE_DIM, D_MODEL), F32), jax.ShapeDtypeStruct((PLE_DIM, D_MODEL), BF16)],
        compiler_params=_params(1),
    )(h, g, w_gate, p, w_ple, *head)
    return loss, dgf, dh2, dg, (dwg, dwgb), (dwp, dwpb)


def _gate_bwd(dh, gate, pb, w_ple, h, g, w_gate, hn, name, after=()):
    s = h.shape[0]
    t = _row_tile(s, FWD_TILE)
    last = s // t - 1

    def body(dh_ref, gate_ref, pb_ref, wp_ref, h_ref, g_ref, wg_ref, hn_ref, out_ref, dg_ref, dwg_ref, dwgb_ref,
             dwp_ref, dwpb_ref):
        i = pl.program_id(0)

        @pl.when(i == 0)
        def _():
            dg_ref[...] = jnp.zeros_like(dg_ref)
            dwg_ref[...] = jnp.zeros_like(dwg_ref)
            dwp_ref[...] = jnp.zeros_like(dwp_ref)

        d = dh_ref[...]
        gate = gate_ref[...].astype(F32)
        pb = pb_ref[...]
        e = _dot(pb, wp_ref[...])
        dgl = (d * e * gate * (1.0 - gate)).astype(BF16)
        dwg_ref[...] += _dot_tn(hn_ref[...], dgl)
        dwp_ref[...] += _dot_tn(pb, (d * gate).astype(BF16))
        gv = g_ref[...]
        n, rstd, _ = _rms(h_ref[...], gv)
        dx, dg = _rms_bwd(_dot_nt(dgl, wg_ref[...]), n, rstd, gv)
        out_ref[...] = d + dx
        dg_ref[...] += dg

        @pl.when(i == last)
        def _():
            dwgb_ref[...] = dwg_ref[...].astype(BF16)
            dwpb_ref[...] = dwp_ref[...].astype(BF16)

    row = lambda w: pl.BlockSpec((t, w), lambda i: (i, 0))
    full = lambda a, b: pl.BlockSpec((a, b), lambda i: (0, 0))
    dh2, dg, dwg, dwgb, dwp, dwpb = pl.pallas_call(
        _ordered_after(body, 8, after), name=name, grid=(s // t,),
        in_specs=[row(D_MODEL), row(D_MODEL), row(PLE_DIM), _resident((PLE_DIM, D_MODEL)), row(D_MODEL),
                  full(1, D_MODEL), _resident((D_MODEL, D_MODEL)), row(D_MODEL)]
        + [pl.BlockSpec(memory_space=pl.ANY)] * len(after),
        out_specs=[row(D_MODEL), full(1, D_MODEL), full(D_MODEL, D_MODEL), full(D_MODEL, D_MODEL),
                   full(PLE_DIM, D_MODEL), full(PLE_DIM, D_MODEL)],
        out_shape=[jax.ShapeDtypeStruct((s, D_MODEL), F32), jax.ShapeDtypeStruct((1, D_MODEL), F32),
                   jax.ShapeDtypeStruct((D_MODEL, D_MODEL), F32), jax.ShapeDtypeStruct((D_MODEL, D_MODEL), BF16),
                   jax.ShapeDtypeStruct((PLE_DIM, D_MODEL), F32), jax.ShapeDtypeStruct((PLE_DIM, D_MODEL), BF16)],
        compiler_params=_params(1),
    )(dh, gate, pb, w_ple, h, g, w_gate, hn, *after)
    return dh2, dg, (dwg, dwgb), (dwp, dwpb)


def _mlp_bwd(dh, r, h, g, w_up, w_down, name):
    s = h.shape[0]
    t = _row_tile(s, MLP_BWD_TILE)
    nblk = D_FF // FF_BLOCK

    def body(dh_ref, r_ref, h_ref, g_ref, wu_ref, wd_ref, out_ref, dup_ref, dg_ref, dhb_ref):
        @pl.when(pl.program_id(0) == 0)
        def _():
            dg_ref[...] = jnp.zeros_like(dg_ref)

        d = dh_ref[...]
        db = d.astype(BF16)
        dhb_ref[...] = db
        back = None
        for b in range(nblk):
            cols = slice(b * FF_BLOCK, (b + 1) * FF_BLOCK)
            dup = (_dot_nt(db, wd_ref[b]) * (2.0 * r_ref[:, cols].astype(F32))).astype(BF16)
            dup_ref[:, cols] = dup
            part = _dot_nt(dup, wu_ref[b])
            back = part if back is None else back + part
        gv = g_ref[...]
        n, rstd, _ = _rms(h_ref[...], gv)
        dx, dg = _rms_bwd(back, n, rstd, gv)
        out_ref[...] = d + dx
        dg_ref[...] += dg

    row = lambda w: pl.BlockSpec((t, w), lambda i: (i, 0))
    vec = pl.BlockSpec((1, D_MODEL), lambda i: (0, 0))
    resident = lambda shape: pl.BlockSpec(shape, lambda i: (0, 0, 0), pipeline_mode=pl.Buffered(1))
    return pl.pallas_call(
        body, name=name, grid=(s // t,),
        in_specs=[row(D_MODEL), row(D_FF), row(D_MODEL), vec,
                  resident((nblk, D_MODEL, FF_BLOCK)), resident((nblk, FF_BLOCK, D_MODEL))],
        out_specs=[row(D_MODEL), row(D_FF), vec, row(D_MODEL)],
        out_shape=[jax.ShapeDtypeStruct((s, D_MODEL), F32), jax.ShapeDtypeStruct((s, D_FF), BF16),
                   jax.ShapeDtypeStruct((1, D_MODEL), F32), jax.ShapeDtypeStruct((s, D_MODEL), BF16)],
        compiler_params=_params(1),
    )(dh, r, h, g, w_up, w_down)


def _outproj_bwd(dh, w_out, o, lse, ones_bd, a, name):
    s = dh.shape[0]
    t = _row_tile(s, 512)
    last = s // t - 1

    def body(dh_ref, w_ref, o0, o1, o2, l0, l1, l2, bd_ref, a_ref, dp_ref, do0, do1, do2, de0, de1, de2, dw_ref,
             dwb_ref, *stages):
        i = pl.program_id(0)

        @pl.when(i == 0)
        def _():
            dw_ref[...] = jnp.zeros_like(dw_ref)

        stages = _pair_stages(stages)
        dhb = dh_ref[...].astype(BF16)
        dw_ref[...] += _dot_tn(a_ref[...], dhb)

        @pl.when(i == last)
        def _():
            dwb_ref[...] = dw_ref[...].astype(BF16)

        da = _dot_nt(dhb, w_ref[...])
        dp_ref[...] = da[:, 0:POOL_WIDTH]
        ov =[_from_residues(r, stages[i], DILATIONS[i]) for i, r in enumerate((o0, o1, o2))]
        lv = [_from_residues(r, stages[3 + i], DILATIONS[i]) for i, r in enumerate((l0, l1, l2))]
        wts = _group_weights(*lv)
        bd = bd_ref[...]
        cbar = jnp.zeros((t, GROUP_WIDTH), F32)
        for grp, do_ref in enumerate((do0, do1, do2)):
            lo = POOL_WIDTH + grp * GROUP_WIDTH
            dag = da[:, lo:lo + GROUP_WIDTH]
            _to_residues(dag * wts[grp], stages[6 + grp], do_ref, DILATIONS[grp])
            prod = dag * ov[grp]
            hi = prod.astype(BF16)
            low = (prod - hi.astype(F32)).astype(BF16)
            cbar = cbar + wts[grp] * (_dot(hi, bd) + _dot(low, bd))
        for grp, de_ref in enumerate((de0, de1, de2)):
            _to_residues(wts[grp] * cbar, stages[9 + grp], de_ref, DILATIONS[grp])

    row = lambda w: pl.BlockSpec((t, w), lambda i: (i, 0))
    full = lambda a, b: pl.BlockSpec((a, b), lambda i: (0, 0))
    res = [_residue_spec(dil, t) for dil in DILATIONS]
    *outs, dw, dwb = pl.pallas_call(
        body, name=name, grid=(s // t,),
        in_specs=[row(D_MODEL), full(D_MODEL, D_MODEL)] + res + res + [full(GROUP_WIDTH, GROUP_WIDTH), row(D_MODEL)],
        out_specs=[row(POOL_WIDTH)] + res + res + [full(D_MODEL, D_MODEL)] * 2,
        out_shape=[jax.ShapeDtypeStruct((s, POOL_WIDTH), F32)] + [_residue_shape(dil, s, BF16) for dil in DILATIONS]
        + [_residue_shape(dil, s, F32) for dil in DILATIONS]
        + [jax.ShapeDtypeStruct((D_MODEL, D_MODEL), F32), jax.ShapeDtypeStruct((D_MODEL, D_MODEL), BF16)],
        scratch_shapes=_stages(t, 12),
        compiler_params=_params(1),
    )(dh, w_out, *o, *lse, ones_bd, a)
    return (*outs, (dw, dwb))


def _attn_bwd(q, k, v, do, lse, deff, name, after=()):
    dil, length, _ = q.shape
    nb = length // ATTN_BLOCK
    qb = _blocks_per_step(nb)
    nj = nb // qb
    rs = _residues_per_step(dil, nb, qb)
    whole = nj == 1
    tail = slice((qb - 1) * ATTN_BLOCK, qb * ATTN_BLOCK)
    block = lambda qi: slice(qi * ATTN_BLOCK, (qi + 1) * ATTN_BLOCK)

    def body(q_ref, kp_ref, kc_ref, vp_ref, vc_ref, do_ref, lse_ref, de_ref, dq_ref, dk_ref, dv_ref, ck, cv):
        j = pl.program_id(1)

        def compute():
            masks = _head_masks()
            bias = _band_bias(j == 0)
            for rr in range(rs):
                dkc, dvc = [], []
                for qi in range(qb):
                    here, before = block(qi), block(qi - 1)
                    kcat = jnp.concatenate([kp_ref[rr] if qi == 0 else kc_ref[rr, before], kc_ref[rr, here]], axis=0)
                    vcat = jnp.concatenate([vp_ref[rr] if qi == 0 else vc_ref[rr, before], vc_ref[rr, here]], axis=0)
                    qs = _stack_heads(q_ref[rr, here], masks)
                    dos = _stack_heads(do_ref[rr, here], masks)
                    sc = _dot_nt(qs, kcat) + bias[min(qi, 1)]
                    p = jnp.exp(sc - _column_per_head(lse_ref[rr, here]))
                    ds = (p * (_dot_nt(dos, vcat) - _column_per_head(de_ref[rr, here]))).astype(BF16)
                    dq = jnp.zeros((ATTN_BLOCK, GROUP_WIDTH), F32)
                    for hd, msk in enumerate(masks):
                        dq = jnp.where(msk, _dot(ds[block(hd)], kcat), dq)
                    dq_ref[rr, here] = dq.astype(dq_ref.dtype)
                    dkc.append(_dot_tn(ds, qs))
                    dvc.append(_dot_tn(p.astype(BF16), dos))

                for out_ref, carry, parts in ((dk_ref, ck, dkc), (dv_ref, cv, dvc)):
                    full = [parts[qi][ATTN_BLOCK:] + parts[qi + 1][0:ATTN_BLOCK] for qi in range(qb - 1)]
                    if whole:
                        for qi, val in enumerate(full + [parts[qb - 1][ATTN_BLOCK:]]):
                            out_ref[rr, block(qi)] = val.astype(out_ref.dtype)
                        continue

                    @pl.when(j > 0)
                    def _():
                        if qb > 1:
                            out_ref[0, 0:(qb - 1) * ATTN_BLOCK] = carry[0:(qb - 1) * ATTN_BLOCK].astype(out_ref.dtype)
                        out_ref[0, tail] = (carry[tail] + parts[0][0:ATTN_BLOCK]).astype(out_ref.dtype)

                    for qi, val in enumerate(full):
                        carry[block(qi)] = val
                    carry[tail] = parts[qb - 1][ATTN_BLOCK:]

        if whole:
            compute()
        else:
            pl.when(j < nj)(compute)

            @pl.when(j == nj)
            def _():
                dk_ref[0] = ck[...].astype(dk_ref.dtype)
                dv_ref[0] = cv[...].astype(dv_ref.dtype)

    step = lambda j: jnp.minimum(j, nj - 1)
    cur = pl.BlockSpec((rs, qb * ATTN_BLOCK, GROUP_WIDTH), lambda r, j: (r, step(j), 0))
    prev = pl.BlockSpec((rs, ATTN_BLOCK, GROUP_WIDTH), lambda r, j: (r, jnp.maximum(qb * step(j) - 1, 0), 0))
    late = pl.BlockSpec((rs, qb * ATTN_BLOCK, GROUP_WIDTH), lambda r, j: (r, jnp.maximum(j - 1, 0), 0))
    return pl.pallas_call(
        _ordered_after(body, 8, after), name=name, grid=(dil // rs, 1 if whole else nj + 1),
        in_specs=[cur, prev, cur, prev, cur, cur, cur, cur] + [pl.BlockSpec(memory_space=pl.ANY)] * len(after),
        out_specs=[cur, cur if whole else late, cur if whole else late],
        out_shape=[jax.ShapeDtypeStruct(q.shape, BF16)] * 3,
        scratch_shapes=[pltpu.VMEM((qb * ATTN_BLOCK, GROUP_WIDTH), F32)] * 2,
        compiler_params=_params(2),
    )(q, k, k, v, v, do, lse, deff, *after)


def _pool_bwd_tile(i, nt, dp_ref, y_ref, w_ref, sc_ref, dw_ref, dsc_ref, ext, b2, b4, b8):
    t = dp_ref.shape[0]

    @pl.when(i == 0)
    def _():
        ext[t:, :] = jnp.zeros((POOL_HALO + POOL_PAD, POOL_WIDTH), F32)
        for buf in (b2, b4):
            buf[t + POOL_HALO:, :] = jnp.zeros((POOL_PAD, POOL_WIDTH), F32)
        dw_ref[...] = jnp.zeros_like(dw_ref)
        dsc_ref[...] = jnp.zeros_like(dsc_ref)

    dp = dp_ref[...]
    yb = y_ref[...]
    w = w_ref[...]
    dsc_ref[...] += jnp.sum(dp * _dot(yb, w), axis=0, keepdims=True)
    dyo = (dp * sc_ref[...]).astype(BF16)
    dw_ref[...] += _dot_tn(yb, dyo)
    dy = _dot_nt(dyo, w)
    pos = (nt - 1 - i) * t + lax.broadcasted_iota(jnp.int32, (t, POOL_WIDTH), 0)
    gq = dy / jnp.minimum(pos + 1, _pool_lane_window()).astype(F32)
    ext[0:t, :] = gq
    du = _window_sums(ext, b2, b4, b8, t, 0, 0, 1) - dy
    ext[t:t + POOL_HALO, :] = gq[0:POOL_HALO, :]
    return du


def _normproj_bwd(dh, dpool, y, w_bd, scale, dq, dk, dv, rc, rsa, rsb, w_in, h, g, name):
    s = h.shape[0]
    t = _row_tile(s, 512)
    nt = s // t

    def body(dh_ref, dp_ref, y_ref, wbd_ref, sc_ref, q0, q1, q2, k0, k1, k2, v0, v1, v2, c_ref, sa_ref, sb_ref, w_ref,
             h_ref, g_ref, out_ref, dz_ref, dg_ref, dwbd_ref, dsc_ref, ext, b2, b4, b8, *stages):
        step = pl.program_id(0)

        @pl.when(step == 0)
        def _():
            dg_ref[...] = jnp.zeros_like(dg_ref)

        du = _pool_bwd_tile(step, nt, dp_ref, y_ref, wbd_ref, sc_ref, dwbd_ref, dsc_ref, ext, b2, b4, b8)
        c, sa, sb = c_ref[...], sa_ref[...], sb_ref[...]

        def unrot(a, scale):
            halves = [_rot_t(a[:, hf * LANES:(hf + 1) * LANES] * scale, c, sa, sb) for hf in range(2)]
            return jnp.concatenate(halves, axis=1)

        staged = _pair_stages(stages)
        tok = lambda refs, base: [_from_residues(r, staged[base + i], DILATIONS[i]) for i, r in enumerate(refs)]
        chunks = [du]
        chunks += [unrot(a, HEAD_DIM ** -0.5) for a in tok((q0, q1, q2), 0)]
        chunks += [unrot(a, 1.0) for a in tok((k0, k1, k2), 3)]
        chunks += tok((v0, v1, v2), 6)
        acc = jnp.zeros((t, D_MODEL), F32)
        for ci, ch in enumerate(chunks):
            cols = slice(ci * GROUP_WIDTH, (ci + 1) * GROUP_WIDTH)
            cb = ch.astype(BF16)
            dz_ref[:, cols] = cb
            acc = acc + _dot(cb, w_ref[cols, :])
        gv = g_ref[...]
        n, rstd, _ = _rms(h_ref[...], gv)
        dx, dg = _rms_bwd(acc, n, rstd, gv)
        out_ref[...] = dh_ref[...] + dx
        dg_ref[...] += dg

    back = lambda i: nt - 1 - i
    row = lambda w: pl.BlockSpec((t, w), lambda i: (back(i), 0))
    full = lambda a, b: pl.BlockSpec((a, b), lambda i: (0, 0))
    res = [pl.BlockSpec((dil, t // dil, GROUP_WIDTH), lambda i: (0, back(i), 0)) for dil in DILATIONS]
    tables = [pl.BlockSpec((t, LANES), functools.partial(lambda i, k: (back(i), k), k=k)) for k in range(3)]
    return pl.pallas_call(
        body, name=name, grid=(nt,),
        in_specs=[row(D_MODEL), row(POOL_WIDTH), row(POOL_WIDTH), full(POOL_WIDTH, POOL_WIDTH), full(1, POOL_WIDTH)]
        + res * 3 + tables + [full(N_IN, D_MODEL), row(D_MODEL), full(1, D_MODEL)],
        out_specs=[row(D_MODEL), row(N_IN), full(1, D_MODEL), full(POOL_WIDTH, POOL_WIDTH), full(1, POOL_WIDTH)],
        out_shape=[jax.ShapeDtypeStruct((s, D_MODEL), F32), jax.ShapeDtypeStruct((s, N_IN), BF16),
                   jax.ShapeDtypeStruct((1, D_MODEL), F32), jax.ShapeDtypeStruct((POOL_WIDTH, POOL_WIDTH), F32),
                   jax.ShapeDtypeStruct((1, POOL_WIDTH), F32)],
        scratch_shapes=[pltpu.VMEM((t + POOL_HALO + POOL_PAD, POOL_WIDTH), F32)] * 4 + _stages(t, 9),
        compiler_params=_params(1),
    )(dh, dpool, y, w_bd, scale, *dq, *dk, *dv, rc, rsa, rsb, w_in, h, g)


def _matmul_tn(a, b, name, *, square_a=False, tm=None, tn=None, blocked_out=False, after=()):
    s, m = a.shape
    n = b.shape[1]
    tk = _row_tile(s, 2048)
    tm = tm or min(m, 1024)
    tn = tn or min(n, 1024)
    assert m % tm == 0 and n % tn == 0
    nk = s // tk
    nsub = tn // FF_BLOCK if blocked_out else 1

    def body(a_ref, b_ref, o_ref, ob_ref, acc):
        k = pl.program_id(2)

        def product():
            av = a_ref[...]
            if square_a:
                av = av.astype(F32)
                av = av * av
            return _dot_tn(av.astype(BF16), b_ref[...].astype(BF16))

        def emit(total):
            if blocked_out:
                for sub in range(nsub):
                    cols = slice(sub * FF_BLOCK, (sub + 1) * FF_BLOCK)
                    o_ref[sub] = total[:, cols]
                    ob_ref[sub] = total[:, cols].astype(BF16)
            else:
                o_ref[...] = total
                ob_ref[...] = total.astype(BF16)

        if nk == 1:
            emit(product())
            return

        @pl.when(k == 0)
        def _():
            acc[...] = product()

        @pl.when((k > 0) & (k < nk - 1))
        def _():
            acc[...] += product()

        @pl.when(k == nk - 1)
        def _():
            emit(acc[...] + product())

    if blocked_out:
        shape = (n // FF_BLOCK, m, FF_BLOCK)
        out_spec = pl.BlockSpec((nsub, tm, FF_BLOCK), lambda i, j, k: (j, i, 0))
    else:
        shape = (m, n)
        out_spec = pl.BlockSpec((tm, tn), lambda i, j, k: (i, j))
    return pl.pallas_call(
        _ordered_after(body, 2, after), name=name, grid=(m // tm, n // tn, nk),
        in_specs=[pl.BlockSpec((tk, tm), lambda i, j, k: (k, i)), pl.BlockSpec((tk, tn), lambda i, j, k: (k, j))]
        + [pl.BlockSpec(memory_space=pl.ANY)] * len(after),
        out_specs=[out_spec, out_spec],
        out_shape=[jax.ShapeDtypeStruct(shape, F32), jax.ShapeDtypeStruct(shape, BF16)],
        scratch_shapes=[pltpu.VMEM((tm, tn), F32)],
        compiler_params=_params(3),
    )(a, b, *after)


def _adamw_math(w, g, m, v):
    m = ADAM_B1 * m + (1.0 - ADAM_B1) * g
    v = ADAM_B2 * v + (1.0 - ADAM_B2) * (g * g)
    m_hat = m / (1.0 - ADAM_B1 ** ADAM_STEP)
    v_hat = v / (1.0 - ADAM_B2 ** ADAM_STEP)
    delta = -ADAM_LR * (m_hat / (jnp.sqrt(v_hat) + ADAM_EPS) + ADAM_WD * w)
    return delta, m, v


def _sum_chunks_body(own0_ref, own1_ref, r0_ref, r1_ref):
    layer0 = pl.program_id(0) == 0
    g = jnp.where(layer0, own0_ref[...], own1_ref[...])
    for k in range(N_DEV - 1):
        g = g + jnp.where(layer0, r0_ref[k], r1_ref[k]).astype(F32)
    return g


def _chunk_specs(t, cols):
    rows_of = lambda layer: (lambda l, i: jnp.where(l == layer, i, 0))
    blk = pl.BlockSpec((None, t, cols), lambda l, i, me: (l, i, 0))
    own = [pl.BlockSpec((None, t, cols), functools.partial(lambda l, i, me, pick: (me[0], pick(l, i), 0), pick=rows_of(ly)))
           for ly in range(2)]
    recv = [pl.BlockSpec((N_DEV - 1, t, cols), functools.partial(lambda l, i, me, pick: (0, pick(l, i), 0), pick=rows_of(ly)))
            for ly in range(2)]
    return blk, own + recv


def _adamw_sharded(w, m, v, chunks, me, name):
    _, rows, cols = w.shape
    t = max(d for d in range(SUBLANES, min(rows, 256) + 1, SUBLANES) if rows % d == 0)

    def body(me_ref, w_ref, m_ref, v_ref, own0_ref, own1_ref, r0_ref, r1_ref, g_ref, d_ref, nm_ref, nv_ref):
        g = _sum_chunks_body(own0_ref, own1_ref, r0_ref, r1_ref)
        g_ref[...] = g
        d_ref[...], nm_ref[...], nv_ref[...] = _adamw_math(w_ref[...], g, m_ref[...], v_ref[...])

    blk, chunk_specs = _chunk_specs(t, cols)
    return pl.pallas_call(
        body, name=name,
        grid_spec=pltpu.PrefetchScalarGridSpec(
            num_scalar_prefetch=1, grid=(2, rows // t), in_specs=[blk, blk, blk] + chunk_specs, out_specs=[blk] * 4),
        out_shape=[jax.ShapeDtypeStruct(w.shape, F32)] * 4,
        compiler_params=_params(2),
    )(me, w, m, v, *chunks)


def _adamw_packed(w, g8, m, v, name):
    def body(w_ref, g_ref, m_ref, v_ref, go_ref, d_ref, nm_ref, nv_ref):
        g = g_ref[0]
        for dev in range(1, N_DEV):
            g = g + g_ref[dev]
        go_ref[...] = g
        d_ref[...], nm_ref[...], nv_ref[...] = _adamw_math(w_ref[...], g, m_ref[...], v_ref[...])

    return pl.pallas_call(
        body, name=name, out_shape=[jax.ShapeDtypeStruct(w.shape, F32)] * 4,
        compiler_params=pltpu.CompilerParams(vmem_limit_bytes=VMEM_LIMIT),
    )(w, g8, m, v)


def _peer(k):
    x, y, c = lax.axis_index("x"), lax.axis_index("y"), lax.axis_index("c")
    return (1 - x if k & 4 else x, 1 - y if k & 2 else y, 1 - c if k & 1 else c)


def _linear(dev):
    return 4 * dev[0] + 2 * dev[1] + dev[2]


HBM_SPEC = pl.BlockSpec(memory_space=pltpu.HBM)
SEM_SPEC = pl.BlockSpec(memory_space=pltpu.SEMAPHORE)
ANY_SPEC = pl.BlockSpec(memory_space=pl.ANY)
EFFECT = pltpu.SideEffectType.DATAFLOW_SIDE_EFFECTING


def _in_hbm(a):
    return pltpu.with_memory_space_constraint(a, pltpu.HBM)


class _Exchange:
    def __init__(self, name, groups, scatter, after=()):
        self.name, self.scatter = name, scatter
        self.sizes = sizes = [len(g) for g in groups]
        srcs = [a for g in groups for a in g]
        n, ng = len(srcs), len(groups)
        lead = (N_DEV - 1,) if scatter else (N_DEV,)
        shapes = [lead + (a.shape[1:] if scatter else a.shape) for a in srcs]
        lands = [lax.empty(sh, a.dtype) for sh, a in zip(shapes, srcs)]
        offsets = [sum(sizes[:gi]) for gi in range(ng)]
        copy = self._copy

        def body(*refs):
            src, land = refs[:n], refs[n:2 * n]
            sems = refs[2 * n + len(after):2 * n + len(after) + 2 * ng]
            token = refs[-1]
            for gi in range(ng):
                for wi in range(sizes[gi]):
                    w = offsets[gi] + wi
                    for k in range(1, N_DEV):
                        copy(src[w], land[w], sems[2 * gi], sems[2 * gi + 1], wi, k).start()
            token[...] = jnp.zeros_like(token)

        sem_shapes = [pltpu.SemaphoreType.DMA(((N_DEV - 1) * sz,)) for sz in sizes for _ in range(2)]
        outs = pl.pallas_call(
            body, name=name + "_start",
            in_specs=[HBM_SPEC] * (2 * n) + [ANY_SPEC] * len(after),
            out_specs=[SEM_SPEC] * (2 * ng) + [HBM_SPEC] * (2 * n) + [pl.BlockSpec(memory_space=pltpu.VMEM)],
            out_shape=sem_shapes + [pltpu.HBM(a.shape, a.dtype) for a in srcs + lands]
            + [jax.ShapeDtypeStruct((8, LANES), F32)],
            input_output_aliases={i: 2 * ng + i for i in range(2 * n)},
            compiler_params=pltpu.CompilerParams(has_side_effects=EFFECT),
        )(*[_in_hbm(a) for a in srcs + lands], *after)
        self.sems = [outs[2 * gi:2 * gi + 2] for gi in range(ng)]
        thru = outs[2 * ng:2 * ng + 2 * n]
        self.srcs = [thru[offsets[gi]:offsets[gi] + sizes[gi]] for gi in range(ng)]
        self.lands = [thru[n + offsets[gi]:n + offsets[gi] + sizes[gi]] for gi in range(ng)]
        self.token = outs[-1]

    def _copy(self, src, land, send_sems, recv_sems, wi, k):
        to = _peer(k)
        if self.scatter:
            src_ref, dst_ref = src.at[_linear(to)], land.at[k - 1]
        else:
            src_ref, dst_ref = src, land.at[_linear(_peer(0))]
        return pltpu.make_async_remote_copy(
            src_ref=src_ref, dst_ref=dst_ref, send_sem=send_sems.at[(N_DEV - 1) * wi + k - 1],
            recv_sem=recv_sems.at[(N_DEV - 1) * wi + k - 1], device_id=to, device_id_type=MESH)

    def wait(self, gi, after):
        n = self.sizes[gi]
        copy = self._copy

        def body(*refs):
            src, land = refs[:n], refs[n:2 * n]
            send_sems, recv_sems = refs[2 * n], refs[2 * n + 1]
            for wi in range(n):
                for k in range(1, N_DEV):
                    cp = copy(src[wi], land[wi], send_sems, recv_sems, wi, k)
                    cp.wait_send()
                    cp.wait_recv()

        arrays = list(self.srcs[gi]) + list(self.lands[gi])
        outs = pl.pallas_call(
            body, name=f"{self.name}_wait{gi}",
            in_specs=[HBM_SPEC] * (2 * n) + [SEM_SPEC, SEM_SPEC] + [ANY_SPEC] * len(after),
            out_specs=[HBM_SPEC] * (2 * n),
            out_shape=[pltpu.HBM(a.shape, a.dtype) for a in arrays],
            input_output_aliases={i: i for i in range(2 * n)},
            compiler_params=pltpu.CompilerParams(has_side_effects=EFFECT),
        )(*arrays, *self.sems[gi], *after)
        return outs[:n], outs[n:]


def _rotary_tables(positions):
    rot_dim = HEAD_DIM // 4
    inv_freq = ROPE_THETA ** (-jnp.arange(0, rot_dim, 2, dtype=F32) / rot_dim)
    ang = positions.astype(F32)[:, None] * inv_freq
    cs = jnp.concatenate([jnp.cos(ang), jnp.sin(ang)], axis=1)
    dim = jnp.arange(LANES) % HEAD_DIM
    first, second = dim < ROT_SHIFT, (dim >= ROT_SHIFT) & (dim < rot_dim)
    src = jnp.arange(2 * ROT_SHIFT)[:, None]
    angle = (dim % ROT_SHIFT)[None, :]
    c = jnp.where((first | second)[None, :] & (src == angle), 1.0, 0.0)
    sa = jnp.where(second[None, :] & (src == angle + ROT_SHIFT), 1.0, 0.0)
    sb = jnp.where(first[None, :] & (src == angle + ROT_SHIFT), -1.0, 0.0)
    spread = jnp.concatenate([c, sa, sb], axis=1).astype(F32)
    base = jnp.concatenate([jnp.where(first | second, 0.0, 1.0), jnp.zeros((2 * LANES,))]).astype(F32)[None, :]
    return jnp.dot(cs, spread, precision=lax.Precision.HIGHEST, preferred_element_type=F32) + base


def _block_diag(pool_w):
    gc = pool_w.shape[-1]
    out = jnp.zeros((POOL_WIDTH, POOL_WIDTH), pool_w.dtype)
    for grp in range(pool_w.shape[0]):
        out = lax.dynamic_update_slice(out, pool_w[grp], (grp * gc, grp * gc))
    return out


def _diag_blocks(a):
    gc = POOL_WIDTH // len(POOL_WINDOWS)
    return jnp.stack([a[grp * gc:(grp + 1) * gc, grp * gc:(grp + 1) * gc] for grp in range(len(POOL_WINDOWS))])


def _local_step(x, p, positions, loss_target, norm1, pool_w, pool_scale, norm2, norm3, final_norm, weights, send):
    rc = rsa = rsb = _rotary_tables(positions)
    ones_bd = _block_diag(jnp.ones((4, HEAD_DIM, HEAD_DIM), BF16))
    saved = []
    h = x
    for i in range(2):
        tag = f"_l{i}"
        g1, g2, g3 = norm1[i:i + 1], norm2[i:i + 1], norm3[i:i + 1]
        w_bd = _block_diag(pool_w[i]).astype(BF16)
        scale = pool_scale[i:i + 1]
        if i == 0:
            w_in = weights(i, "in", (h, rc, w_bd))
            hn1, u, *qkv = _normproj_fwd(h, g1, w_in, rc, rsa, rsb, "normproj_fwd" + tag)
        else:
            w_in, (hn1, u, *qkv) = ahead
        qkv = [qkv[3 * grp:3 * grp + 3] for grp in range(3)]
        started = weights(i, "prefetch", (hn1,))
        o, lse = zip(*[_attn_fwd(*qkv[grp], f"attn_fwd{tag}_g{grp}", after=started) for grp in range(3)])
        w_out = weights(i, "out", o)
        h1, a, y = _outproj_fwd(h, u, w_bd, scale, o, lse, w_out, "outproj_fwd" + tag)
        w_up, w_down = weights(i, "mlp", (h1,))
        h2, hn2, r = _mlp_fwd(h1, g2, w_up, w_down, "mlp_fwd" + tag)
        w_gate, w_ple = weights(i, "gate", (h2,))
        h0 = h
        if i == 0:
            w_in_next = weights(1, "in", (h2,))
            h, hn3, gate, pb, *ahead = _gate_fwd(h2, g3, w_gate, p, i, w_ple, "gate_normproj_fwd",
                                                 follow=(norm1[1:2], w_in_next, rc, rsa, rsb))
            ahead = (w_in_next, ahead)
        else:
            hn3 = gate = pb = None
            loss, d_final, *top = _gate_fwd(h2, g3, w_gate, p, i, w_ple, "gate_loss_gate_bwd",
                                            head=(final_norm.reshape(1, D_MODEL), loss_target))
        saved.append(dict(h0=h0, hn1=hn1, qkv=qkv, y=y, o=o, lse=lse, a=a, h1=h1, hn2=hn2, r=r, h2=h2,
                          hn3=hn3, gate=gate, pb=pb, w_bd=w_bd, scale=scale, g1=g1, g2=g2, g3=g3,
                          w_in=w_in, w_out=w_out, w_up=w_up, w_down=w_down, w_gate=w_gate, w_ple=w_ple))

    grads = [None, None]
    sent = ()
    for i in (1, 0):
        tag = f"_l{i}"
        sv = saved[i]
        if i == 1:
            dh2, dg3, dw_gate, dw_ple = top
        else:
            dh2, dg3, dw_gate, dw_ple = _gate_bwd(dh, sv["gate"], sv["pb"], sv["w_ple"], sv["h2"], sv["g3"],
                                                  sv["w_gate"], sv["hn3"], "gate_bwd" + tag, after=sent)
        dh1, dup, dg2, dh2b = _mlp_bwd(dh2, sv["r"], sv["h1"], sv["g2"], sv["w_up"], sv["w_down"], "mlp_bwd" + tag)
        dw_down = _matmul_tn(sv["r"], dh2b, "dw_down" + tag, square_a=True)
        dw_up = _matmul_tn(sv["hn2"], dup, "dw_up" + tag, blocked_out=True)
        dpool, do0, do1, do2, de0, de1, de2, dw_out = _outproj_bwd(dh1, sv["w_out"], sv["o"], sv["lse"], ones_bd,
                                                                   sv["a"], "outproj_bwd" + tag)
        sent = send(i, "main", dict(w_gate=dw_gate, w_ple=dw_ple, w_down=dw_down, w_up=dw_up, w_out=dw_out))
        dqkv = [_attn_bwd(*sv["qkv"][grp], do_g, sv["lse"][grp], de_g, f"attn_bwd{tag}_g{grp}", after=sent)
                for grp, (do_g, de_g) in enumerate(((do0, de0), (do1, de1), (do2, de2)))]
        dq, dk, dv = zip(*dqkv)
        dh, dz, dg1, dw_bd, dscale = _normproj_bwd(dh1, dpool, sv["y"], sv["w_bd"], sv["scale"], dq, dk, dv, rc, rsa, rsb,
                                                   sv["w_in"], sv["h0"], sv["g1"], "normproj_bwd" + tag)
        grads[i] = dict(norm1=dg1, norm2=dg2, norm3=dg3, pool_w=_diag_blocks(dw_bd), pool_scale=dscale)
        small_sent = send(0, "small", (grads, d_final, loss)) if i == 0 else ()
        dw_in = _matmul_tn(dz, sv["hn1"], "dw_in" + tag, tm=N_IN // 2, after=small_sent)
        sent = send(i, "in", dict(w_in=dw_in))
    return dh, sent


def _pack_small(norm1, norm2, norm3, final_norm, pool_scale, pool_w, spare=None):
    spare = jnp.zeros((1, LANES), F32) if spare is None else spare
    scale_row = jnp.concatenate([pool_scale.reshape(1, 2 * POOL_WIDTH), spare,
                                 jnp.zeros((1, D_MODEL - 2 * POOL_WIDTH - LANES), F32)], axis=1)
    return jnp.concatenate([norm1, norm2, norm3, final_norm.reshape(1, D_MODEL), scale_row,
                            pool_w.reshape(32, D_MODEL)], axis=0)


def _unpack_small(a):
    return dict(norm1=a[0:2], norm2=a[2:4], norm3=a[4:6], final_norm=a[6], pool_scale=a[7, 0:2 * POOL_WIDTH].reshape(2, POOL_WIDTH),
                pool_w=a[8:40].reshape(2, 4, HEAD_DIM, HEAD_DIM))


def _chunks_cols(a, cols):
    return a.reshape(a.shape[0], N_DEV, cols).transpose(1, 0, 2)


def _chunks_rows(a, rows):
    return a.reshape(N_DEV, rows, a.shape[1])


BIG = ("w_in", "w_out", "w_up", "w_down", "w_gate", "w_ple")
SMALL = ("norm1", "norm2", "norm3", "final_norm", "pool_scale", "pool_w")
ORDER = ("norm1", "w_in", "pool_w", "pool_scale", "w_out", "norm2", "w_up", "w_down", "norm3", "w_gate", "w_ple",
         "final_norm")


def kernel(x, p, positions, norm1, w_in, pool_w, pool_scale, w_out, norm2, w_up, w_down, norm3, w_gate, w_ple, final_norm, loss_target, m_norm1, m_w_in, m_pool_w, m_pool_scale, m_w_out, m_norm2, m_w_up, m_w_down, m_norm3, m_w_gate, m_w_ple, m_final_norm, v_norm1, v_w_in, v_pool_w, v_pool_scale, v_w_out, v_norm2, v_w_up, v_w_down, v_norm3, v_w_gate, v_w_ple, v_final_norm):
    w = dict(norm1=norm1, w_in=w_in, pool_w=pool_w, pool_scale=pool_scale, w_out=w_out, norm2=norm2, w_up=w_up,
             w_down=w_down, norm3=norm3, w_gate=w_gate, w_ple=w_ple, final_norm=final_norm)
    m = dict(norm1=m_norm1, w_in=m_w_in, pool_w=m_pool_w, pool_scale=m_pool_scale, w_out=m_w_out, norm2=m_norm2,
             w_up=m_w_up, w_down=m_w_down, norm3=m_norm3, w_gate=m_w_gate, w_ple=m_w_ple, final_norm=m_final_norm)
    v = dict(norm1=v_norm1, w_in=v_w_in, pool_w=v_pool_w, pool_scale=v_pool_scale, w_out=v_w_out, norm2=v_norm2,
             w_up=v_w_up, w_down=v_w_down, norm3=v_norm3, w_gate=v_w_gate, w_ple=v_w_ple, final_norm=v_final_norm)
    seq = x.shape[1]

    bf = {n: [w[n][layer].astype(BF16) for layer in range(2)] for n in BIG}
    bf["w_in"] = [a.T for a in bf["w_in"]]
    me = 4 * lax.axis_index("x") + 2 * lax.axis_index("y") + lax.axis_index("c")
    parts = dict(zip(("in", "out", "mlp", "gate"), (("w_in",), ("w_out",), ("w_up", "w_down"), ("w_gate", "w_ple"))))
    first = _Exchange("gather_first", [[bf["w_in"][0]]], scatter=False)
    later = [pt for pt in parts if pt != "in"]
    gathers = [_Exchange("gather_l0", [[bf[n][0] for n in parts[pt]] for pt in later], scatter=False,
                         after=(first.token,))]
    unpack = dict(w_in=lambda a: a.reshape(N_IN, D_MODEL),
                  w_out=lambda a: a.reshape(D_MODEL, D_MODEL), w_gate=lambda a: a.reshape(D_MODEL, D_MODEL),
                  w_ple=lambda a: a.transpose(1, 0, 2).reshape(PLE_DIM, D_MODEL), w_up=lambda a: a, w_down=lambda a: a)

    def weights(layer, part, after):
        if part == "prefetch":
            if layer != 0:
                return ()
            gathers.append(_Exchange("gather_l1", [[bf[n][1] for n in parts[pt]] for pt in parts], scatter=False,
                                     after=after))
            return (gathers[1].token,)
        if layer == 0 and part == "in":
            shards, lands = first.wait(0, (*after, gathers[0].token))
        elif layer == 0:
            shards, lands = gathers[0].wait(later.index(part), after)
        else:
            shards, lands = gathers[1].wait(tuple(parts).index(part), after)
        full = [unpack[n](lax.dynamic_update_slice_in_dim(land, shard[None], me, axis=0))
                for n, shard, land in zip(parts[part], shards, lands)]
        return full if len(full) > 1 else full[0]

    to_chunks = dict(w_in=lambda a: _chunks_rows(a, N_IN // N_DEV),
                     w_out=lambda a: _chunks_rows(a, D_MODEL // N_DEV),
                     w_up=lambda a: a, w_down=lambda a: _chunks_rows(a, FF_BLOCK),
                     w_gate=lambda a: _chunks_rows(a, D_MODEL // N_DEV), w_ple=lambda a: _chunks_cols(a, D_MODEL // N_DEV))
    own = {n: [None, None] for n in BIG}
    scatters = {}

    def send(layer, part, grads):
        if part == "small":
            per_layer, d_final, loss = grads
            pack = _pack_small(
                *[jnp.concatenate([per_layer[0][n], per_layer[1][n]], axis=0) for n in ("norm1", "norm2", "norm3")],
                d_final.reshape(D_MODEL),
                jnp.concatenate([per_layer[0]["pool_scale"], per_layer[1]["pool_scale"]], axis=0),
                jnp.stack([per_layer[0]["pool_w"], per_layer[1]["pool_w"]]), spare=loss)
            scatters["small"] = _Exchange("gather_small", [[pack]], scatter=False)
            return (scatters["small"].token,)
        for n, (g32, _) in grads.items():
            own[n][layer] = to_chunks[n](g32)
        ex = _Exchange(f"scatter_{part}_l{layer}", [[to_chunks[n](g16) for n, (_, g16) in grads.items()]], scatter=True)
        scatters[layer, part] = (tuple(grads), ex)
        return (ex.token,)

    dx, sent = _local_step(
        x.reshape(seq, D_MODEL), p.reshape(2, seq, PLE_DIM), positions.reshape(seq), loss_target.reshape(seq, D_MODEL),
        norm1, pool_w, pool_scale, norm2, norm3, final_norm, weights, send)

    g_out, d_out, m_out, v_out = {}, {}, {}, {}
    my_index = me.reshape(1)
    for part in ("main", "in"):
        recv = {}
        for layer in (1, 0):
            names, ex = scatters[layer, part]
            for n, r in zip(names, ex.wait(0, sent)[1]):
                recv[n, layer] = r
        for n in names:
            grad = (*own[n], recv[n, 0], recv[n, 1])
            turn = (lambda a: a.transpose(0, 2, 1)) if n == "w_in" else (lambda a: a)
            g_out[n], d_out[n], m_out[n], v_out[n] = map(
                turn, _adamw_sharded(turn(w[n]), turn(m[n]), turn(v[n]), grad, my_index, "adamw_" + n))
        sent = tuple(d_out[n] for n in names)
    (mine,), (landed,) = scatters["small"].wait(0, sent)
    small_g8 = lax.dynamic_update_slice_in_dim(landed, mine[None], me, axis=0)
    pack = lambda t: _pack_small(*[t[n] for n in SMALL])
    small_g, d_small, m_small, v_small = _adamw_packed(pack(w), small_g8, pack(m), pack(v), "adamw_small")
    for dst, a in ((g_out, small_g), (d_out, d_small), (m_out, m_small), (v_out, v_small)):
        dst.update(_unpack_small(a))

    return (small_g[7, 2 * POOL_WIDTH],dx.reshape(1, seq, D_MODEL), *[g_out[n] for n in ORDER], *[d_out[n] for n in ORDER],
            *[m_out[n] for n in ORDER], *[v_out[n] for n in ORDER])
```

```python
import functools

import jax
import jax.numpy as jnp
from jax import lax
from jax.experimental import pallas as pl
from jax.experimental.pallas import tpu as pltpu

F32 = jnp.float32
BF16 = jnp.bfloat16

D_MODEL = 1024
HEAD_DIM = 64
POOL_WIDTH = 256
POOL_WINDOWS = (2, 4, 8, 16)
POOL_HALO = 16
POOL_PAD = 8
GROUP_WIDTH = 256
DILATIONS = (1, 4, 16)
ATTN_BLOCK = 128
ROT_SHIFT = 8
ROPE_THETA = 500000.0
D_FF = 4096
FF_BLOCK = 512
FF_PER_STEP = 2
MLP_BWD_TILE = 512
FWD_TILE = 1024
N_DEV = 8
N_IN = POOL_WIDTH + 3 * 768
PLE_DIM = 256
EPS = 1e-6
NEG_BIG = -1e30

ADAM_LR = 0.001
ADAM_B1 = 0.9
ADAM_B2 = 0.999
ADAM_EPS = 1e-08
ADAM_WD = 0.01
ADAM_STEP = 10

LANES = 128
SUBLANES = 8
VMEM_LIMIT = 56 * 1024 * 1024
MESH = pl.DeviceIdType.MESH


def _params(n_grid):
    return pltpu.CompilerParams(dimension_semantics=("arbitrary",) * n_grid, vmem_limit_bytes=VMEM_LIMIT)


def _dot(a, b):
    return jnp.dot(a, b, preferred_element_type=F32)


def _dot_nt(a, b):
    return lax.dot_general(a, b, (((1,), (1,)), ((), ())), preferred_element_type=F32)


def _dot_tn(a, b):
    return lax.dot_general(a, b, (((0,), (0,)), ((), ())), preferred_element_type=F32)


def _rms(x, g):
    rstd = lax.rsqrt(jnp.mean(x * x, axis=-1, keepdims=True) + EPS)
    n = x * rstd
    return n, rstd, n * g


def _rms_bwd(dy, n, rstd, g):
    dyn = dy * g
    dx = rstd * (dyn - n * jnp.mean(dyn * n, axis=-1, keepdims=True))
    return dx, jnp.sum(dy * n, axis=0, keepdims=True)


def _ordered_after(body, n_in, after):
    if not after:
        return body
    return lambda *refs: body(*refs[:n_in], *refs[n_in + len(after):])


def _resident(shape):
    return pl.BlockSpec(shape, lambda i: (0,) * len(shape), pipeline_mode=pl.Buffered(1))


def _row_tile(s, t):
    t = min(s, t)
    assert s % t == 0
    return t


def _rot(z, c, sa, sb):
    return z * c + pltpu.roll(z, ROT_SHIFT, 1) * sa + pltpu.roll(z, LANES - ROT_SHIFT, 1) * sb


def _table_specs(t):
    return [pl.BlockSpec((t, LANES), functools.partial(lambda i, k: (i, k), k=k)) for k in range(3)]


def _rot_t(dz, c, sa, sb):
    return dz * c + pltpu.roll(dz * sa, LANES - ROT_SHIFT, 1) + pltpu.roll(dz * sb, ROT_SHIFT, 1)


def _to_residues(value, stage, out_ref, dil):
    if dil == 1:
        out_ref[0] = value.astype(out_ref.dtype)
        return
    rows = value.shape[0] // dil
    for hf in range(GROUP_WIDTH // LANES):
        lanes = slice(hf * LANES, (hf + 1) * LANES)
        stage[hf][...] = value[:, lanes]
        for r in range(dil):
            out_ref[r, :, lanes] = stage[hf][pl.ds(r, rows, stride=dil), :].astype(out_ref.dtype)


def _from_residues(in_ref, stage, dil):
    if dil == 1:
        return in_ref[0].astype(F32)
    rows = in_ref.shape[1]
    for hf in range(GROUP_WIDTH // LANES):
        for r in range(dil):
            stage[hf][pl.ds(r, rows, stride=dil), :] = in_ref[r, :, hf * LANES:(hf + 1) * LANES].astype(F32)
    return jnp.concatenate([stage[0][...], stage[1][...]], axis=1)


def _residue_spec(dil, t):
    return pl.BlockSpec((dil, t // dil, GROUP_WIDTH), lambda i: (0, i, 0))


def _residue_shape(dil, s, dtype):
    return jax.ShapeDtypeStruct((dil, s // dil, GROUP_WIDTH), dtype)


def _stages(t, n):
    return [pltpu.VMEM((t, LANES), F32)] * (n * (GROUP_WIDTH // LANES))


def _pair_stages(refs):
    return [refs[i:i + 2] for i in range(0, len(refs), 2)]


def _normproj_tile(x, g_ref, w_ref, c_ref, sa_ref, sb_ref, hn_ref, u_ref, *rest):
    qkv_refs, stages = rest[:9], _pair_stages(rest[9:])
    _, _, hn = _rms(x, g_ref[...])
    hb = hn.astype(BF16)
    hn_ref[...] = hb
    c, sa, sb = c_ref[...], sa_ref[...], sb_ref[...]

    def rot(z, scale):
        halves = [_rot(z[:, hf * LANES:(hf + 1) * LANES], c, sa, sb) * scale for hf in range(2)]
        return jnp.concatenate(halves, axis=1)

    proj = lambda lo: _dot_nt(hb, w_ref[lo:lo + GROUP_WIDTH, :])
    u_ref[...] = proj(0)
    for grp, dil in enumerate(DILATIONS):
        lo = POOL_WIDTH + grp * GROUP_WIDTH
        q_ref, k_ref, v_ref = qkv_refs[3 * grp:3 * grp + 3]
        _to_residues(rot(proj(lo), HEAD_DIM ** -0.5), stages[0], q_ref, dil)
        _to_residues(rot(proj(lo + 768), 1.0), stages[1], k_ref, dil)
        _to_residues(proj(lo + 1536), stages[2], v_ref, dil)


def _normproj_operands(s, t):
    row = lambda w: pl.BlockSpec((t, w), lambda i: (i, 0))
    in_specs = [pl.BlockSpec((1, D_MODEL), lambda i: (0, 0)), _resident((N_IN, D_MODEL))] + _table_specs(t)
    out_specs = [row(D_MODEL), row(POOL_WIDTH)] + [_residue_spec(dil, t) for dil in DILATIONS for _ in range(3)]
    out_shape = [jax.ShapeDtypeStruct((s, D_MODEL), BF16), jax.ShapeDtypeStruct((s, POOL_WIDTH), F32)]
    out_shape += [_residue_shape(dil, s, BF16) for dil in DILATIONS for _ in range(3)]
    return in_specs, out_specs, out_shape, _stages(t, 3)


def _normproj_fwd(h, g, w_in, rc, rsa, rsb, name):
    s = h.shape[0]
    t = _row_tile(s, FWD_TILE)

    def body(h_ref, *refs):
        _normproj_tile(h_ref[...], *refs)

    in_specs, out_specs, out_shape, scratch = _normproj_operands(s, t)
    return pl.pallas_call(
        body, name=name, grid=(s // t,), in_specs=[pl.BlockSpec((t, D_MODEL), lambda i: (i, 0))] + in_specs,
        out_specs=out_specs, out_shape=out_shape, scratch_shapes=scratch, compiler_params=_params(1),
    )(h, g, w_in, rc, rsa, rsb)


def _pool_lane_window():
    lane = lax.broadcasted_iota(jnp.int32, (1, POOL_WIDTH), 1)
    return jnp.left_shift(2, lane // (POOL_WIDTH // len(POOL_WINDOWS)))


def _window_sums(ext, b2, b4, b8, t, lo, tile, direction):
    rows = t + POOL_HALO
    for src, dst, sh in ((ext, b2, 1), (b2, b4, 2), (b4, b8, 4)):
        dst[lo:lo + rows, :] = src[lo:lo + rows, :] + src[lo + direction * sh:lo + direction * sh + rows, :]
    s16 = b8[tile:tile + t, :] + b8[tile + direction * 8:tile + direction * 8 + t, :]
    win = _pool_lane_window()
    return jnp.where(win == 2, b2[tile:tile + t, :],
                     jnp.where(win == 4, b4[tile:tile + t, :], jnp.where(win == 8, b8[tile:tile + t, :], s16)))


def _pool_fwd_tile(i, u_ref, w_ref, sc_ref, y_ref, ext, b2, b4, b8):
    t = u_ref.shape[0]
    first = POOL_PAD + POOL_HALO

    @pl.when(i == 0)
    def _():
        for buf in (ext, b2, b4):
            buf[0:POOL_PAD, :] = jnp.zeros((POOL_PAD, POOL_WIDTH), F32)
        ext[POOL_PAD:first, :] = jnp.zeros((POOL_HALO, POOL_WIDTH), F32)

    x = u_ref[...]
    ext[first:, :] = x
    wsum = _window_sums(ext, b2, b4, b8, t, POOL_PAD, first, -1)
    pos = i * t + lax.broadcasted_iota(jnp.int32, (t, POOL_WIDTH), 0)
    cnt = jnp.minimum(pos + 1, _pool_lane_window()).astype(F32)
    yb = (wsum / cnt - x).astype(BF16)
    y_ref[...] = yb
    ext[POOL_PAD:first, :] = x[t - POOL_HALO:, :]
    return _dot(yb, w_ref[...]) * sc_ref[...]


def _head_masks():
    lane = lax.broadcasted_iota(jnp.int32, (ATTN_BLOCK, GROUP_WIDTH), 1)
    return [lane // HEAD_DIM == hd for hd in range(GROUP_WIDTH // HEAD_DIM)]


def _stack_heads(a, masks):
    zero = jnp.zeros_like(a)
    return jnp.concatenate([jnp.where(m, a, zero) for m in masks], axis=0)


def _band_bias(first_step):
    rows = ATTN_BLOCK * (GROUP_WIDTH // HEAD_DIM)
    i = lax.broadcasted_iota(jnp.int32, (rows, 2 * ATTN_BLOCK), 0) & (ATTN_BLOCK - 1)
    j = lax.broadcasted_iota(jnp.int32, (rows, 2 * ATTN_BLOCK), 1)
    inner = jnp.where((j >= i) & (j <= i + ATTN_BLOCK), 0.0, NEG_BIG)
    return jnp.where((j < ATTN_BLOCK) & first_step, NEG_BIG, inner), inner


def _column_per_head(a):
    return jnp.concatenate([a[:, hd * HEAD_DIM:hd * HEAD_DIM + 1] for hd in range(GROUP_WIDTH // HEAD_DIM)], axis=0)


def _blocks_per_step(nb):
    if nb <= 16:
        return nb
    return next(qb for qb in (16, 8, 4, 2, 1) if nb % qb == 0)


def _residues_per_step(dil, nb, qb):
    return 2 if (nb == qb and qb < 8 and dil % 2 == 0) else 1


def _attn_fwd(q, k, v, name, after=()):
    dil, length, _ = q.shape
    nb = length // ATTN_BLOCK
    qb = _blocks_per_step(nb)
    rs = _residues_per_step(dil, nb, qb)

    def body(q_ref, kp_ref, kc_ref, vp_ref, vc_ref, o_ref, lse_ref):
        masks = _head_masks()
        bias = _band_bias(pl.program_id(1) == 0)
        for rr in range(rs):
            for qi in range(qb):
                here = slice(qi * ATTN_BLOCK, (qi + 1) * ATTN_BLOCK)
                before = slice((qi - 1) * ATTN_BLOCK, qi * ATTN_BLOCK)
                kcat = jnp.concatenate([kp_ref[rr] if qi == 0 else kc_ref[rr, before], kc_ref[rr, here]], axis=0)
                vcat = jnp.concatenate([vp_ref[rr] if qi == 0 else vc_ref[rr, before], vc_ref[rr, here]], axis=0)
                qs = _stack_heads(q_ref[rr, here], masks)
                sc = _dot_nt(qs, kcat) + bias[min(qi, 1)]
                m = jnp.max(sc, axis=1, keepdims=True)
                e = jnp.exp(sc - m)
                l = jnp.sum(e, axis=1, keepdims=True)
                p = (e / l).astype(BF16)
                lse = m + jnp.log(l)
                o = jnp.zeros((ATTN_BLOCK, GROUP_WIDTH), F32)
                lse_full = jnp.zeros((ATTN_BLOCK, GROUP_WIDTH), F32)
                for hd, msk in enumerate(masks):
                    rows = slice(hd * ATTN_BLOCK, (hd + 1) * ATTN_BLOCK)
                    o = jnp.where(msk, _dot(p[rows], vcat), o)
                    lse_full = jnp.where(msk, lse[rows], lse_full)
                o_ref[rr, here] = o.astype(o_ref.dtype)
                lse_ref[rr, here] = lse_full

    cur = pl.BlockSpec((rs, qb * ATTN_BLOCK, GROUP_WIDTH), lambda r, j: (r, j, 0))
    prev = pl.BlockSpec((rs, ATTN_BLOCK, GROUP_WIDTH), lambda r, j: (r, jnp.maximum(qb * j - 1, 0), 0))
    return pl.pallas_call(
        _ordered_after(body, 5, after), name=name, grid=(dil // rs, nb // qb),
        in_specs=[cur, prev, cur, prev, cur] + [pl.BlockSpec(memory_space=pl.ANY)] * len(after), out_specs=[cur, cur],
        out_shape=[jax.ShapeDtypeStruct(q.shape, BF16), jax.ShapeDtypeStruct(q.shape, F32)],
        compiler_params=_params(2),
    )(q, k, k, v, v, *after)


def _group_weights(l0, l1, l2):
    m = jnp.maximum(jnp.maximum(l0, l1), l2)
    e0, e1, e2 = jnp.exp(l0 - m), jnp.exp(l1 - m), jnp.exp(l2 - m)
    den = e0 + e1 + e2
    return e0 / den, e1 / den, e2 / den


def _outproj_fwd(h, u, w_bd, scale, o, lse, w_out, name):
    s = h.shape[0]
    t = _row_tile(s, FWD_TILE)

    def body(h_ref, u_ref, wbd_ref, sc_ref, o0, o1, o2, l0, l1, l2, w_ref, out_ref, a_ref, y_ref, ext, b2, b4, b8,
             *stages):
        pool_out = _pool_fwd_tile(pl.program_id(0), u_ref, wbd_ref, sc_ref, y_ref, ext, b2, b4, b8)
        stages = _pair_stages(stages)
        ov = [_from_residues(r, stages[i], DILATIONS[i]) for i, r in enumerate((o0, o1, o2))]
        lv = [_from_residues(r, stages[3 + i], DILATIONS[i]) for i, r in enumerate((l0, l1, l2))]
        wts = _group_weights(*lv)
        a = jnp.concatenate([pool_out] + [ov[i] * wts[i] for i in range(3)], axis=1).astype(BF16)
        a_ref[...] = a
        out_ref[...] = h_ref[...] + _dot(a, w_ref[...])

    row = lambda w: pl.BlockSpec((t, w), lambda i: (i, 0))
    res = [_residue_spec(dil, t) for dil in DILATIONS]
    return pl.pallas_call(
        body, name=name, grid=(s // t,),
        in_specs=[row(D_MODEL), row(POOL_WIDTH), _resident((POOL_WIDTH, POOL_WIDTH)), _resident((1, POOL_WIDTH))]
        + res + res + [_resident((D_MODEL, D_MODEL))],
        out_specs=[row(D_MODEL), row(D_MODEL), row(POOL_WIDTH)],
        out_shape=[jax.ShapeDtypeStruct((s, D_MODEL), F32), jax.ShapeDtypeStruct((s, D_MODEL), BF16),
                   jax.ShapeDtypeStruct((s, POOL_WIDTH), BF16)],
        scratch_shapes=[pltpu.VMEM((t + POOL_HALO + POOL_PAD, POOL_WIDTH), F32)] * 4 + _stages(t, 6),
        compiler_params=_params(1),
    )(h, u, w_bd, scale, *o, *lse, w_out)


def _mlp_fwd(h, g, w_up, w_down, name):
    s = h.shape[0]
    t = _row_tile(s, 512)
    nblk = D_FF // FF_BLOCK

    def body(h_ref, g_ref, wu_ref, wd_ref, out_ref, hn_ref, r_ref):
        x = h_ref[...]
        _, _, hn = _rms(x, g_ref[...])
        hb = hn.astype(BF16)
        hn_ref[...] = hb
        acc = None
        for b0 in range(0, nblk, FF_PER_STEP):
            acts = []
            for b in range(b0, b0 + FF_PER_STEP):
                r = jnp.maximum(_dot(hb, wu_ref[b]), 0.0)
                r_ref[:, b * FF_BLOCK:(b + 1) * FF_BLOCK] = r.astype(BF16)
                acts.append((r * r).astype(BF16))
            wd = wd_ref[b0:b0 + FF_PER_STEP].reshape(FF_PER_STEP * FF_BLOCK, D_MODEL)
            part = _dot(jnp.concatenate(acts, axis=1), wd)
            acc = part if acc is None else acc + part
        out_ref[...] = x + acc

    row = lambda w: pl.BlockSpec((t, w), lambda i: (i, 0))
    resident = lambda shape: pl.BlockSpec(shape, lambda i: (0, 0, 0), pipeline_mode=pl.Buffered(1))
    return pl.pallas_call(
        body, name=name, grid=(s // t,),
        in_specs=[row(D_MODEL), pl.BlockSpec((1, D_MODEL), lambda i: (0, 0)),
                  resident((nblk, D_MODEL, FF_BLOCK)), resident((nblk, FF_BLOCK, D_MODEL))],
        out_specs=[row(D_MODEL), row(D_MODEL), row(D_FF)],
        out_shape=[jax.ShapeDtypeStruct((s, D_MODEL), F32), jax.ShapeDtypeStruct((s, D_MODEL), BF16),
                   jax.ShapeDtypeStruct((s, D_FF), BF16)],
        compiler_params=_params(1),
    )(h, g, w_up, w_down)


def _gate_fwd(h, g, w_gate, p, layer, w_ple, name, head=None, follow=None):
    assert (head is None) != (follow is None)
    s = h.shape[0]
    t = _row_tile(s, 512)
    last = s // t - 1

    def body(h_ref, g_ref, wg_ref, p_ref, wp_ref, *refs):
        x = h_ref[...]
        gv = g_ref[...]
        n, rstd, hn = _rms(x, gv)
        hb = hn.astype(BF16)
        gate = 1.0 / (1.0 + jnp.exp(-_dot(hb, wg_ref[...])))
        pb = p_ref[...].astype(BF16)
        e = _dot(pb, wp_ref[...])
        h3 = x + gate * e
        if follow is not None:
            out_ref, hn_ref, gate_ref, pb_ref = refs[5:9]
            out_ref[...] = h3
            hn_ref[...] = hb
            pb_ref[...] = pb
            gate_ref[...] = gate.astype(BF16)
            _normproj_tile(h3, *refs[:5], *refs[9:])
            return
        gf_ref, t_ref, loss_ref, dgf_ref, out_ref, dg_ref, dwg_ref, dwgb_ref, dwp_ref, dwpb_ref = refs
        i = pl.program_id(0)

        @pl.when(i == 0)
        def _():
            for ref in (loss_ref, dgf_ref, dg_ref, dwg_ref, dwp_ref):
                ref[...] = jnp.zeros_like(ref)

        gf = gf_ref[...]
        n3, rstd3, y = _rms(h3, gf)
        err = y - t_ref[...]
        loss_ref[...] += jnp.sum(err * err) * (0.5 / D_MODEL)
        d, dgf = _rms_bwd(err * (1.0 / D_MODEL), n3, rstd3, gf)
        dgf_ref[...] += dgf
        dgl = (d * e * gate * (1.0 - gate)).astype(BF16)
        dwg_ref[...] += _dot_tn(hb, dgl)
        dwp_ref[...] += _dot_tn(pb, (d * gate).astype(BF16))
        dx, dg = _rms_bwd(_dot_nt(dgl, wg_ref[...]), n, rstd, gv)
        out_ref[...] = d + dx
        dg_ref[...] += dg

        @pl.when(i == last)
        def _():
            dwgb_ref[...] = dwg_ref[...].astype(BF16)
            dwpb_ref[...] = dwp_ref[...].astype(BF16)

    row = lambda w: pl.BlockSpec((t, w), lambda i: (i, 0))
    full = lambda a, b: pl.BlockSpec((a, b), lambda i: (0, 0))
    in_specs = [row(D_MODEL), full(1, D_MODEL), _resident((D_MODEL, D_MODEL)),
                pl.BlockSpec((None, t, PLE_DIM), lambda i: (layer, i, 0)), _resident((PLE_DIM, D_MODEL))]
    if follow is not None:
        next_in, next_out, next_shape, scratch = _normproj_operands(s, t)
        return pl.pallas_call(
            body, name=name, grid=(s // t,), in_specs=in_specs + next_in,
            out_specs=[row(D_MODEL), row(D_MODEL), row(D_MODEL), row(PLE_DIM)] + next_out,
            out_shape=[jax.ShapeDtypeStruct((s, D_MODEL), F32), jax.ShapeDtypeStruct((s, D_MODEL), BF16),
                       jax.ShapeDtypeStruct((s, D_MODEL), BF16), jax.ShapeDtypeStruct((s, PLE_DIM), BF16)] + next_shape,
            scratch_shapes=scratch, compiler_params=_params(1),
        )(h, g, w_gate, p, w_ple, *follow)
    loss, dgf, dh2, dg, dwg, dwgb, dwp, dwpb = pl.pallas_call(
        body, name=name, grid=(s // t,), in_specs=in_specs + [full(1, D_MODEL), row(D_MODEL)],
        out_specs=[pl.BlockSpec((1, LANES), lambda i: (0, 0)), full(1, D_MODEL), row(D_MODEL), full(1, D_MODEL),
                   full(D_MODEL, D_MODEL), full(D_MODEL, D_MODEL), full(PLE_DIM, D_MODEL), full(PLE_DIM, D_MODEL)],
        out_shape=[jax.ShapeDtypeStruct((1, LANES), F32), jax.ShapeDtypeStruct((1, D_MODEL), F32),
                   jax.ShapeDtypeStruct((s, D_MODEL), F32), jax.ShapeDtypeStruct((1, D_MODEL), F32),
                   jax.ShapeDtypeStruct((D_MODEL, D_MODEL), F32), jax.ShapeDtypeStruct((D_MODEL, D_MODEL), BF16),
                   jax.ShapeDtypeStruct((PLE_DIM, D_MODEL), F32), jax.ShapeDtypeStruct((PLE_DIM, D_MODEL), BF16)],
        compiler_params=_params(1),
    )(h, g, w_gate, p, w_ple, *head)
    return loss, dgf, dh2, dg, (dwg, dwgb), (dwp, dwpb)


def _gate_bwd(dh, gate, pb, w_ple, h, g, w_gate, hn, name, after=()):
    s = h.shape[0]
    t = _row_tile(s, FWD_TILE)
    last = s // t - 1

    def body(dh_ref, gate_ref, pb_ref, wp_ref, h_ref, g_ref, wg_ref, hn_ref, out_ref, dg_ref, dwg_ref, dwgb_ref,
             dwp_ref, dwpb_ref):
        i = pl.program_id(0)

        @pl.when(i == 0)
        def _():
            dg_ref[...] = jnp.zeros_like(dg_ref)
            dwg_ref[...] = jnp.zeros_like(dwg_ref)
            dwp_ref[...] = jnp.zeros_like(dwp_ref)

        d = dh_ref[...]
        gate = gate_ref[...].astype(F32)
        pb = pb_ref[...]
        e = _dot(pb, wp_ref[...])
        dgl = (d * e * gate * (1.0 - gate)).astype(BF16)
        dwg_ref[...] += _dot_tn(hn_ref[...], dgl)
        dwp_ref[...] += _dot_tn(pb, (d * gate).astype(BF16))
        gv = g_ref[...]
        n, rstd, _ = _rms(h_ref[...], gv)
        dx, dg = _rms_bwd(_dot_nt(dgl, wg_ref[...]), n, rstd, gv)
        out_ref[...] = d + dx
        dg_ref[...] += dg

        @pl.when(i == last)
        def _():
            dwgb_ref[...] = dwg_ref[...].astype(BF16)
            dwpb_ref[...] = dwp_ref[...].astype(BF16)

    row = lambda w: pl.BlockSpec((t, w), lambda i: (i, 0))
    full = lambda a, b: pl.BlockSpec((a, b), lambda i: (0, 0))
    dh2, dg, dwg, dwgb, dwp, dwpb = pl.pallas_call(
        _ordered_after(body, 8, after), name=name, grid=(s // t,),
        in_specs=[row(D_MODEL), row(D_MODEL), row(PLE_DIM), _resident((PLE_DIM, D_MODEL)), row(D_MODEL),
                  full(1, D_MODEL), _resident((D_MODEL, D_MODEL)), row(D_MODEL)]
        + [pl.BlockSpec(memory_space=pl.ANY)] * len(after),
        out_specs=[row(D_MODEL), full(1, D_MODEL), full(D_MODEL, D_MODEL), full(D_MODEL, D_MODEL),
                   full(PLE_DIM, D_MODEL), full(PLE_DIM, D_MODEL)],
        out_shape=[jax.ShapeDtypeStruct((s, D_MODEL), F32), jax.ShapeDtypeStruct((1, D_MODEL), F32),
                   jax.ShapeDtypeStruct((D_MODEL, D_MODEL), F32), jax.ShapeDtypeStruct((D_MODEL, D_MODEL), BF16),
                   jax.ShapeDtypeStruct((PLE_DIM, D_MODEL), F32), jax.ShapeDtypeStruct((PLE_DIM, D_MODEL), BF16)],
        compiler_params=_params(1),
    )(dh, gate, pb, w_ple, h, g, w_gate, hn, *after)
    return dh2, dg, (dwg, dwgb), (dwp, dwpb)


def _mlp_bwd(dh, r, h, g, w_up, w_down, name):
    s = h.shape[0]
    t = _row_tile(s, MLP_BWD_TILE)
    nblk = D_FF // FF_BLOCK

    def body(dh_ref, r_ref, h_ref, g_ref, wu_ref, wd_ref, out_ref, dup_ref, dg_ref, dhb_ref):
        @pl.when(pl.program_id(0) == 0)
        def _():
            dg_ref[...] = jnp.zeros_like(dg_ref)

        d = dh_ref[...]
        db = d.astype(BF16)
        dhb_ref[...] = db
        back = None
        for b in range(nblk):
            cols = slice(b * FF_BLOCK, (b + 1) * FF_BLOCK)
            dup = (_dot_nt(db, wd_ref[b]) * (2.0 * r_ref[:, cols].astype(F32))).astype(BF16)
            dup_ref[:, cols] = dup
            part = _dot_nt(dup, wu_ref[b])
            back = part if back is None else back + part
        gv = g_ref[...]
        n, rstd, _ = _rms(h_ref[...], gv)
        dx, dg = _rms_bwd(back, n, rstd, gv)
        out_ref[...] = d + dx
        dg_ref[...] += dg

    row = lambda w: pl.BlockSpec((t, w), lambda i: (i, 0))
    vec = pl.BlockSpec((1, D_MODEL), lambda i: (0, 0))
    resident = lambda shape: pl.BlockSpec(shape, lambda i: (0, 0, 0), pipeline_mode=pl.Buffered(1))
    return pl.pallas_call(
        body, name=name, grid=(s // t,),
        in_specs=[row(D_MODEL), row(D_FF), row(D_MODEL), vec,
                  resident((nblk, D_MODEL, FF_BLOCK)), resident((nblk, FF_BLOCK, D_MODEL))],
        out_specs=[row(D_MODEL), row(D_FF), vec, row(D_MODEL)],
        out_shape=[jax.ShapeDtypeStruct((s, D_MODEL), F32), jax.ShapeDtypeStruct((s, D_FF), BF16),
                   jax.ShapeDtypeStruct((1, D_MODEL), F32), jax.ShapeDtypeStruct((s, D_MODEL), BF16)],
        compiler_params=_params(1),
    )(dh, r, h, g, w_up, w_down)


def _outproj_bwd(dh, w_out, o, lse, ones_bd, a, name):
    s = dh.shape[0]
    t = _row_tile(s, 512)
    last = s // t - 1

    def body(dh_ref, w_ref, o0, o1, o2, l0, l1, l2, bd_ref, a_ref, dp_ref, do0, do1, do2, de0, de1, de2, dw_ref,
             dwb_ref, *stages):
        i = pl.program_id(0)

        @pl.when(i == 0)
        def _():
            dw_ref[...] = jnp.zeros_like(dw_ref)

        stages = _pair_stages(stages)
        dhb = dh_ref[...].astype(BF16)
        dw_ref[...] += _dot_tn(a_ref[...], dhb)

        @pl.when(i == last)
        def _():
            dwb_ref[...] = dw_ref[...].astype(BF16)

        da = _dot_nt(dhb, w_ref[...])
        dp_ref[...] = da[:, 0:POOL_WIDTH]
        ov =[_from_residues(r, stages[i], DILATIONS[i]) for i, r in enumerate((o0, o1, o2))]
        lv = [_from_residues(r, stages[3 + i], DILATIONS[i]) for i, r in enumerate((l0, l1, l2))]
        wts = _group_weights(*lv)
        bd = bd_ref[...]
        cbar = jnp.zeros((t, GROUP_WIDTH), F32)
        for grp, do_ref in enumerate((do0, do1, do2)):
            lo = POOL_WIDTH + grp * GROUP_WIDTH
            dag = da[:, lo:lo + GROUP_WIDTH]
            _to_residues(dag * wts[grp], stages[6 + grp], do_ref, DILATIONS[grp])
            prod = dag * ov[grp]
            hi = prod.astype(BF16)
            low = (prod - hi.astype(F32)).astype(BF16)
            cbar = cbar + wts[grp] * (_dot(hi, bd) + _dot(low, bd))
        for grp, de_ref in enumerate((de0, de1, de2)):
            _to_residues(wts[grp] * cbar, stages[9 + grp], de_ref, DILATIONS[grp])

    row = lambda w: pl.BlockSpec((t, w), lambda i: (i, 0))
    full = lambda a, b: pl.BlockSpec((a, b), lambda i: (0, 0))
    res = [_residue_spec(dil, t) for dil in DILATIONS]
    *outs, dw, dwb = pl.pallas_call(
        body, name=name, grid=(s // t,),
        in_specs=[row(D_MODEL), full(D_MODEL, D_MODEL)] + res + res + [full(GROUP_WIDTH, GROUP_WIDTH), row(D_MODEL)],
        out_specs=[row(POOL_WIDTH)] + res + res + [full(D_MODEL, D_MODEL)] * 2,
        out_shape=[jax.ShapeDtypeStruct((s, POOL_WIDTH), F32)] + [_residue_shape(dil, s, BF16) for dil in DILATIONS]
        + [_residue_shape(dil, s, F32) for dil in DILATIONS]
        + [jax.ShapeDtypeStruct((D_MODEL, D_MODEL), F32), jax.ShapeDtypeStruct((D_MODEL, D_MODEL), BF16)],
        scratch_shapes=_stages(t, 12),
        compiler_params=_params(1),
    )(dh, w_out, *o, *lse, ones_bd, a)
    return (*outs, (dw, dwb))


def _attn_bwd(q, k, v, do, lse, deff, name, after=()):
    dil, length, _ = q.shape
    nb = length // ATTN_BLOCK
    qb = _blocks_per_step(nb)
    nj = nb // qb
    rs = _residues_per_step(dil, nb, qb)
    whole = nj == 1
    tail = slice((qb - 1) * ATTN_BLOCK, qb * ATTN_BLOCK)
    block = lambda qi: slice(qi * ATTN_BLOCK, (qi + 1) * ATTN_BLOCK)

    def body(q_ref, kp_ref, kc_ref, vp_ref, vc_ref, do_ref, lse_ref, de_ref, dq_ref, dk_ref, dv_ref, ck, cv):
        j = pl.program_id(1)

        def compute():
            masks = _head_masks()
            bias = _band_bias(j == 0)
            for rr in range(rs):
                dkc, dvc = [], []
                for qi in range(qb):
                    here, before = block(qi), block(qi - 1)
                    kcat = jnp.concatenate([kp_ref[rr] if qi == 0 else kc_ref[rr, before], kc_ref[rr, here]], axis=0)
                    vcat = jnp.concatenate([vp_ref[rr] if qi == 0 else vc_ref[rr, before], vc_ref[rr, here]], axis=0)
                    qs = _stack_heads(q_ref[rr, here], masks)
                    dos = _stack_heads(do_ref[rr, here], masks)
                    sc = _dot_nt(qs, kcat) + bias[min(qi, 1)]
                    p = jnp.exp(sc - _column_per_head(lse_ref[rr, here]))
                    ds = (p * (_dot_nt(dos, vcat) - _column_per_head(de_ref[rr, here]))).astype(BF16)
                    dq = jnp.zeros((ATTN_BLOCK, GROUP_WIDTH), F32)
                    for hd, msk in enumerate(masks):
                        dq = jnp.where(msk, _dot(ds[block(hd)], kcat), dq)
                    dq_ref[rr, here] = dq.astype(dq_ref.dtype)
                    dkc.append(_dot_tn(ds, qs))
                    dvc.append(_dot_tn(p.astype(BF16), dos))

                for out_ref, carry, parts in ((dk_ref, ck, dkc), (dv_ref, cv, dvc)):
                    full = [parts[qi][ATTN_BLOCK:] + parts[qi + 1][0:ATTN_BLOCK] for qi in range(qb - 1)]
                    if whole:
                        for qi, val in enumerate(full + [parts[qb - 1][ATTN_BLOCK:]]):
                            out_ref[rr, block(qi)] = val.astype(out_ref.dtype)
                        continue

                    @pl.when(j > 0)
                    def _():
                        if qb > 1:
                            out_ref[0, 0:(qb - 1) * ATTN_BLOCK] = carry[0:(qb - 1) * ATTN_BLOCK].astype(out_ref.dtype)
                        out_ref[0, tail] = (carry[tail] + parts[0][0:ATTN_BLOCK]).astype(out_ref.dtype)

                    for qi, val in enumerate(full):
                        carry[block(qi)] = val
                    carry[tail] = parts[qb - 1][ATTN_BLOCK:]

        if whole:
            compute()
        else:
            pl.when(j < nj)(compute)

            @pl.when(j == nj)
            def _():
                dk_ref[0] = ck[...].astype(dk_ref.dtype)
                dv_ref[0] = cv[...].astype(dv_ref.dtype)

    step = lambda j: jnp.minimum(j, nj - 1)
    cur = pl.BlockSpec((rs, qb * ATTN_BLOCK, GROUP_WIDTH), lambda r, j: (r, step(j), 0))
    prev = pl.BlockSpec((rs, ATTN_BLOCK, GROUP_WIDTH), lambda r, j: (r, jnp.maximum(qb * step(j) - 1, 0), 0))
    late = pl.BlockSpec((rs, qb * ATTN_BLOCK, GROUP_WIDTH), lambda r, j: (r, jnp.maximum(j - 1, 0), 0))
    return pl.pallas_call(
        _ordered_after(body, 8, after), name=name, grid=(dil // rs, 1 if whole else nj + 1),
        in_specs=[cur, prev, cur, prev, cur, cur, cur, cur] + [pl.BlockSpec(memory_space=pl.ANY)] * len(after),
        out_specs=[cur, cur if whole else late, cur if whole else late],
        out_shape=[jax.ShapeDtypeStruct(q.shape, BF16)] * 3,
        scratch_shapes=[pltpu.VMEM((qb * ATTN_BLOCK, GROUP_WIDTH), F32)] * 2,
        compiler_params=_params(2),
    )(q, k, k, v, v, do, lse, deff, *after)


def _pool_bwd_tile(i, nt, dp_ref, y_ref, w_ref, sc_ref, dw_ref, dsc_ref, ext, b2, b4, b8):
    t = dp_ref.shape[0]

    @pl.when(i == 0)
    def _():
        ext[t:, :] = jnp.zeros((POOL_HALO + POOL_PAD, POOL_WIDTH), F32)
        for buf in (b2, b4):
            buf[t + POOL_HALO:, :] = jnp.zeros((POOL_PAD, POOL_WIDTH), F32)
        dw_ref[...] = jnp.zeros_like(dw_ref)
        dsc_ref[...] = jnp.zeros_like(dsc_ref)

    dp = dp_ref[...]
    yb = y_ref[...]
    w = w_ref[...]
    dsc_ref[...] += jnp.sum(dp * _dot(yb, w), axis=0, keepdims=True)
    dyo = (dp * sc_ref[...]).astype(BF16)
    dw_ref[...] += _dot_tn(yb, dyo)
    dy = _dot_nt(dyo, w)
    pos = (nt - 1 - i) * t + lax.broadcasted_iota(jnp.int32, (t, POOL_WIDTH), 0)
    gq = dy / jnp.minimum(pos + 1, _pool_lane_window()).astype(F32)
    ext[0:t, :] = gq
    du = _window_sums(ext, b2, b4, b8, t, 0, 0, 1) - dy
    ext[t:t + POOL_HALO, :] = gq[0:POOL_HALO, :]
    return du


def _normproj_bwd(dh, dpool, y, w_bd, scale, dq, dk, dv, rc, rsa, rsb, w_in, h, g, name):
    s = h.shape[0]
    t = _row_tile(s, 512)
    nt = s // t

    def body(dh_ref, dp_ref, y_ref, wbd_ref, sc_ref, q0, q1, q2, k0, k1, k2, v0, v1, v2, c_ref, sa_ref, sb_ref, w_ref,
             h_ref, g_ref, out_ref, dz_ref, dg_ref, dwbd_ref, dsc_ref, ext, b2, b4, b8, *stages):
        step = pl.program_id(0)

        @pl.when(step == 0)
        def _():
            dg_ref[...] = jnp.zeros_like(dg_ref)

        du = _pool_bwd_tile(step, nt, dp_ref, y_ref, wbd_ref, sc_ref, dwbd_ref, dsc_ref, ext, b2, b4, b8)
        c, sa, sb = c_ref[...], sa_ref[...], sb_ref[...]

        def unrot(a, scale):
            halves = [_rot_t(a[:, hf * LANES:(hf + 1) * LANES] * scale, c, sa, sb) for hf in range(2)]
            return jnp.concatenate(halves, axis=1)

        staged = _pair_stages(stages)
        tok = lambda refs, base: [_from_residues(r, staged[base + i], DILATIONS[i]) for i, r in enumerate(refs)]
        chunks = [du]
        chunks += [unrot(a, HEAD_DIM ** -0.5) for a in tok((q0, q1, q2), 0)]
        chunks += [unrot(a, 1.0) for a in tok((k0, k1, k2), 3)]
        chunks += tok((v0, v1, v2), 6)
        acc = jnp.zeros((t, D_MODEL), F32)
        for ci, ch in enumerate(chunks):
            cols = slice(ci * GROUP_WIDTH, (ci + 1) * GROUP_WIDTH)
            cb = ch.astype(BF16)
            dz_ref[:, cols] = cb
            acc = acc + _dot(cb, w_ref[cols, :])
        gv = g_ref[...]
        n, rstd, _ = _rms(h_ref[...], gv)
        dx, dg = _rms_bwd(acc, n, rstd, gv)
        out_ref[...] = dh_ref[...] + dx
        dg_ref[...] += dg

    back = lambda i: nt - 1 - i
    row = lambda w: pl.BlockSpec((t, w), lambda i: (back(i), 0))
    full = lambda a, b: pl.BlockSpec((a, b), lambda i: (0, 0))
    res = [pl.BlockSpec((dil, t // dil, GROUP_WIDTH), lambda i: (0, back(i), 0)) for dil in DILATIONS]
    tables = [pl.BlockSpec((t, LANES), functools.partial(lambda i, k: (back(i), k), k=k)) for k in range(3)]
    return pl.pallas_call(
        body, name=name, grid=(nt,),
        in_specs=[row(D_MODEL), row(POOL_WIDTH), row(POOL_WIDTH), full(POOL_WIDTH, POOL_WIDTH), full(1, POOL_WIDTH)]
        + res * 3 + tables + [full(N_IN, D_MODEL), row(D_MODEL), full(1, D_MODEL)],
        out_specs=[row(D_MODEL), row(N_IN), full(1, D_MODEL), full(POOL_WIDTH, POOL_WIDTH), full(1, POOL_WIDTH)],
        out_shape=[jax.ShapeDtypeStruct((s, D_MODEL), F32), jax.ShapeDtypeStruct((s, N_IN), BF16),
                   jax.ShapeDtypeStruct((1, D_MODEL), F32), jax.ShapeDtypeStruct((POOL_WIDTH, POOL_WIDTH), F32),
                   jax.ShapeDtypeStruct((1, POOL_WIDTH), F32)],
        scratch_shapes=[pltpu.VMEM((t + POOL_HALO + POOL_PAD, POOL_WIDTH), F32)] * 4 + _stages(t, 9),
        compiler_params=_params(1),
    )(dh, dpool, y, w_bd, scale, *dq, *dk, *dv, rc, rsa, rsb, w_in, h, g)


def _matmul_tn(a, b, name, *, square_a=False, tm=None, tn=None, blocked_out=False, after=()):
    s, m = a.shape
    n = b.shape[1]
    tk = _row_tile(s, 2048)
    tm = tm or min(m, 1024)
    tn = tn or min(n, 1024)
    assert m % tm == 0 and n % tn == 0
    nk = s // tk
    nsub = tn // FF_BLOCK if blocked_out else 1

    def body(a_ref, b_ref, o_ref, ob_ref, acc):
        k = pl.program_id(2)

        def product():
            av = a_ref[...]
            if square_a:
                av = av.astype(F32)
                av = av * av
            return _dot_tn(av.astype(BF16), b_ref[...].astype(BF16))

        def emit(total):
            if blocked_out:
                for sub in range(nsub):
                    cols = slice(sub * FF_BLOCK, (sub + 1) * FF_BLOCK)
                    o_ref[sub] = total[:, cols]
                    ob_ref[sub] = total[:, cols].astype(BF16)
            else:
                o_ref[...] = total
                ob_ref[...] = total.astype(BF16)

        if nk == 1:
            emit(product())
            return

        @pl.when(k == 0)
        def _():
            acc[...] = product()

        @pl.when((k > 0) & (k < nk - 1))
        def _():
            acc[...] += product()

        @pl.when(k == nk - 1)
        def _():
            emit(acc[...] + product())

    if blocked_out:
        shape = (n // FF_BLOCK, m, FF_BLOCK)
        out_spec = pl.BlockSpec((nsub, tm, FF_BLOCK), lambda i, j, k: (j, i, 0))
    else:
        shape = (m, n)
        out_spec = pl.BlockSpec((tm, tn), lambda i, j, k: (i, j))
    return pl.pallas_call(
        _ordered_after(body, 2, after), name=name, grid=(m // tm, n // tn, nk),
        in_specs=[pl.BlockSpec((tk, tm), lambda i, j, k: (k, i)), pl.BlockSpec((tk, tn), lambda i, j, k: (k, j))]
        + [pl.BlockSpec(memory_space=pl.ANY)] * len(after),
        out_specs=[out_spec, out_spec],
        out_shape=[jax.ShapeDtypeStruct(shape, F32), jax.ShapeDtypeStruct(shape, BF16)],
        scratch_shapes=[pltpu.VMEM((tm, tn), F32)],
        compiler_params=_params(3),
    )(a, b, *after)


def _adamw_math(w, g, m, v):
    m = ADAM_B1 * m + (1.0 - ADAM_B1) * g
    v = ADAM_B2 * v + (1.0 - ADAM_B2) * (g * g)
    m_hat = m / (1.0 - ADAM_B1 ** ADAM_STEP)
    v_hat = v / (1.0 - ADAM_B2 ** ADAM_STEP)
    delta = -ADAM_LR * (m_hat / (jnp.sqrt(v_hat) + ADAM_EPS) + ADAM_WD * w)
    return delta, m, v


def _sum_chunks_body(own0_ref, own1_ref, r0_ref, r1_ref):
    layer0 = pl.program_id(0) == 0
    g = jnp.where(layer0, own0_ref[...], own1_ref[...])
    for k in range(N_DEV - 1):
        g = g + jnp.where(layer0, r0_ref[k], r1_ref[k]).astype(F32)
    return g


def _chunk_specs(t, cols):
    rows_of = lambda layer: (lambda l, i: jnp.where(l == layer, i, 0))
    blk = pl.BlockSpec((None, t, cols), lambda l, i, me: (l, i, 0))
    own = [pl.BlockSpec((None, t, cols), functools.partial(lambda l, i, me, pick: (me[0], pick(l, i), 0), pick=rows_of(ly)))
           for ly in range(2)]
    recv = [pl.BlockSpec((N_DEV - 1, t, cols), functools.partial(lambda l, i, me, pick: (0, pick(l, i), 0), pick=rows_of(ly)))
            for ly in range(2)]
    return blk, own + recv


def _adamw_sharded(w, m, v, chunks, me, name):
    _, rows, cols = w.shape
    t = max(d for d in range(SUBLANES, min(rows, 256) + 1, SUBLANES) if rows % d == 0)

    def body(me_ref, w_ref, m_ref, v_ref, own0_ref, own1_ref, r0_ref, r1_ref, g_ref, d_ref, nm_ref, nv_ref):
        g = _sum_chunks_body(own0_ref, own1_ref, r0_ref, r1_ref)
        g_ref[...] = g
        d_ref[...], nm_ref[...], nv_ref[...] = _adamw_math(w_ref[...], g, m_ref[...], v_ref[...])

    blk, chunk_specs = _chunk_specs(t, cols)
    return pl.pallas_call(
        body, name=name,
        grid_spec=pltpu.PrefetchScalarGridSpec(
            num_scalar_prefetch=1, grid=(2, rows // t), in_specs=[blk, blk, blk] + chunk_specs, out_specs=[blk] * 4),
        out_shape=[jax.ShapeDtypeStruct(w.shape, F32)] * 4,
        compiler_params=_params(2),
    )(me, w, m, v, *chunks)


def _adamw_small(w, g8, m, v, name):
    rows = dict(norm1=(0, 2), norm2=(2, 4), norm3=(4, 6), final_norm=(6, 7), pool_w=(8, 40))
    shaped = lambda t: [t[n].reshape(rows[n][1] - rows[n][0], D_MODEL) if n in rows else t[n] for n in SMALL]
    k = len(SMALL)

    def body(g8_ref, *refs):
        w_refs, m_refs, v_refs = refs[:k], refs[k:2 * k], refs[2 * k:3 * k]
        outs = [refs[(3 + i) * k:(4 + i) * k] for i in range(4)]
        spare_ref = refs[-1]
        g = g8_ref[0]
        for dev in range(1, N_DEV):
            g = g + g8_ref[dev]
        spare_ref[...] = g[7:8, 2 * POOL_WIDTH:2 * POOL_WIDTH + LANES]
        for i, n in enumerate(SMALL):
            if n in rows:
                pieces = [(slice(None), g[rows[n][0]:rows[n][1]])]
            else:
                pieces = [(slice(ly, ly + 1), g[7:8, ly * POOL_WIDTH:(ly + 1) * POOL_WIDTH]) for ly in range(2)]
            for at, gp in pieces:
                new = _adamw_math(w_refs[i][at], gp, m_refs[i][at], v_refs[i][at])
                for out, val in zip(outs, (gp, *new)):
                    out[i][at] = val

    ins = shaped(w) + shaped(m) + shaped(v)
    res = pl.pallas_call(
        body, name=name,
        out_shape=[jax.ShapeDtypeStruct(a.shape, F32) for a in shaped(w)] * 4 + [jax.ShapeDtypeStruct((1, LANES), F32)],
        compiler_params=pltpu.CompilerParams(vmem_limit_bytes=VMEM_LIMIT),
    )(g8, *ins)
    dicts = [{n: a.reshape(w[n].shape) for n, a in zip(SMALL, res[i * k:(i + 1) * k])} for i in range(4)]
    return (*dicts, res[-1])


def _peer(k):
    x, y, c = lax.axis_index("x"), lax.axis_index("y"), lax.axis_index("c")
    return (1 - x if k & 4 else x, 1 - y if k & 2 else y, 1 - c if k & 1 else c)


def _linear(dev):
    return 4 * dev[0] + 2 * dev[1] + dev[2]


HBM_SPEC = pl.BlockSpec(memory_space=pltpu.HBM)
SEM_SPEC = pl.BlockSpec(memory_space=pltpu.SEMAPHORE)
ANY_SPEC = pl.BlockSpec(memory_space=pl.ANY)
EFFECT = pltpu.SideEffectType.DATAFLOW_SIDE_EFFECTING


def _in_hbm(a):
    return pltpu.with_memory_space_constraint(a, pltpu.HBM)


class _Exchange:
    def __init__(self, name, groups, scatter, after=()):
        self.name, self.scatter = name, scatter
        self.sizes = sizes = [len(g) for g in groups]
        srcs = [a for g in groups for a in g]
        n, ng = len(srcs), len(groups)
        lead = (N_DEV - 1,) if scatter else (N_DEV,)
        shapes = [lead + (a.shape[1:] if scatter else a.shape) for a in srcs]
        lands = [lax.empty(sh, a.dtype) for sh, a in zip(shapes, srcs)]
        offsets = [sum(sizes[:gi]) for gi in range(ng)]
        copy = self._copy

        def body(*refs):
            src, land = refs[:n], refs[n:2 * n]
            sems = refs[2 * n + len(after):2 * n + len(after) + 2 * ng]
            token = refs[-1]
            for gi in range(ng):
                for wi in range(sizes[gi]):
                    w = offsets[gi] + wi
                    for k in range(1, N_DEV):
                        copy(src[w], land[w], sems[2 * gi], sems[2 * gi + 1], wi, k).start()
            token[...] = jnp.zeros_like(token)

        sem_shapes = [pltpu.SemaphoreType.DMA(((N_DEV - 1) * sz,)) for sz in sizes for _ in range(2)]
        outs = pl.pallas_call(
            body, name=name + "_start",
            in_specs=[HBM_SPEC] * (2 * n) + [ANY_SPEC] * len(after),
            out_specs=[SEM_SPEC] * (2 * ng) + [HBM_SPEC] * (2 * n) + [pl.BlockSpec(memory_space=pltpu.VMEM)],
            out_shape=sem_shapes + [pltpu.HBM(a.shape, a.dtype) for a in srcs + lands]
            + [jax.ShapeDtypeStruct((8, LANES), F32)],
            input_output_aliases={i: 2 * ng + i for i in range(2 * n)},
            compiler_params=pltpu.CompilerParams(has_side_effects=EFFECT),
        )(*[_in_hbm(a) for a in srcs + lands], *after)
        self.sems = [outs[2 * gi:2 * gi + 2] for gi in range(ng)]
        thru = outs[2 * ng:2 * ng + 2 * n]
        self.srcs = [thru[offsets[gi]:offsets[gi] + sizes[gi]] for gi in range(ng)]
        self.lands = [thru[n + offsets[gi]:n + offsets[gi] + sizes[gi]] for gi in range(ng)]
        self.token = outs[-1]

    def _copy(self, src, land, send_sems, recv_sems, wi, k):
        to = _peer(k)
        if self.scatter:
            src_ref, dst_ref = src.at[_linear(to)], land.at[k - 1]
        else:
            src_ref, dst_ref = src, land.at[_linear(_peer(0))]
        return pltpu.make_async_remote_copy(
            src_ref=src_ref, dst_ref=dst_ref, send_sem=send_sems.at[(N_DEV - 1) * wi + k - 1],
            recv_sem=recv_sems.at[(N_DEV - 1) * wi + k - 1], device_id=to, device_id_type=MESH)

    def wait(self, gi, after):
        n = self.sizes[gi]
        copy = self._copy

        def body(*refs):
            src, land = refs[:n], refs[n:2 * n]
            send_sems, recv_sems = refs[2 * n], refs[2 * n + 1]
            for wi in range(n):
                for k in range(1, N_DEV):
                    cp = copy(src[wi], land[wi], send_sems, recv_sems, wi, k)
                    cp.wait_send()
                    cp.wait_recv()

        arrays = list(self.srcs[gi]) + list(self.lands[gi])
        outs = pl.pallas_call(
            body, name=f"{self.name}_wait{gi}",
            in_specs=[HBM_SPEC] * (2 * n) + [SEM_SPEC, SEM_SPEC] + [ANY_SPEC] * len(after),
            out_specs=[HBM_SPEC] * (2 * n),
            out_shape=[pltpu.HBM(a.shape, a.dtype) for a in arrays],
            input_output_aliases={i: i for i in range(2 * n)},
            compiler_params=pltpu.CompilerParams(has_side_effects=EFFECT),
        )(*arrays, *self.sems[gi], *after)
        return outs[:n], outs[n:]


def _rotary_tables(positions):
    rot_dim = HEAD_DIM // 4
    inv_freq = ROPE_THETA ** (-jnp.arange(0, rot_dim, 2, dtype=F32) / rot_dim)
    ang = positions.astype(F32)[:, None] * inv_freq
    cs = jnp.concatenate([jnp.cos(ang), jnp.sin(ang)], axis=1)
    dim = jnp.arange(LANES) % HEAD_DIM
    first, second = dim < ROT_SHIFT, (dim >= ROT_SHIFT) & (dim < rot_dim)
    src = jnp.arange(2 * ROT_SHIFT)[:, None]
    angle = (dim % ROT_SHIFT)[None, :]
    c = jnp.where((first | second)[None, :] & (src == angle), 1.0, 0.0)
    sa = jnp.where(second[None, :] & (src == angle + ROT_SHIFT), 1.0, 0.0)
    sb = jnp.where(first[None, :] & (src == angle + ROT_SHIFT), -1.0, 0.0)
    spread = jnp.concatenate([c, sa, sb], axis=1).astype(F32)
    base = jnp.concatenate([jnp.where(first | second, 0.0, 1.0), jnp.zeros((2 * LANES,))]).astype(F32)[None, :]
    return jnp.dot(cs, spread, precision=lax.Precision.HIGHEST, preferred_element_type=F32) + base


def _block_diag(pool_w):
    gc = pool_w.shape[-1]
    out = jnp.zeros((POOL_WIDTH, POOL_WIDTH), pool_w.dtype)
    for grp in range(pool_w.shape[0]):
        out = lax.dynamic_update_slice(out, pool_w[grp], (grp * gc, grp * gc))
    return out


def _diag_blocks(a):
    gc = POOL_WIDTH // len(POOL_WINDOWS)
    return jnp.stack([a[grp * gc:(grp + 1) * gc, grp * gc:(grp + 1) * gc] for grp in range(len(POOL_WINDOWS))])


def _local_step(x, p, positions, loss_target, norm1, pool_w, pool_scale, norm2, norm3, final_norm, weights, send):
    rc = rsa = rsb = _rotary_tables(positions)
    ones_bd = _block_diag(jnp.ones((4, HEAD_DIM, HEAD_DIM), BF16))
    saved = []
    h = x
    for i in range(2):
        tag = f"_l{i}"
        g1, g2, g3 = norm1[i:i + 1], norm2[i:i + 1], norm3[i:i + 1]
        w_bd = _block_diag(pool_w[i]).astype(BF16)
        scale = pool_scale[i:i + 1]
        if i == 0:
            w_in = weights(i, "in", (h, rc, w_bd))
            hn1, u, *qkv = _normproj_fwd(h, g1, w_in, rc, rsa, rsb, "normproj_fwd" + tag)
        else:
            w_in, (hn1, u, *qkv) = ahead
        qkv = [qkv[3 * grp:3 * grp + 3] for grp in range(3)]
        started = weights(i, "prefetch", (hn1,))
        o, lse = zip(*[_attn_fwd(*qkv[grp], f"attn_fwd{tag}_g{grp}", after=started) for grp in range(3)])
        w_out = weights(i, "out", o)
        h1, a, y = _outproj_fwd(h, u, w_bd, scale, o, lse, w_out, "outproj_fwd" + tag)
        w_up, w_down = weights(i, "mlp", (h1,))
        h2, hn2, r = _mlp_fwd(h1, g2, w_up, w_down, "mlp_fwd" + tag)
        w_gate, w_ple = weights(i, "gate", (h2,))
        h0 = h
        if i == 0:
            w_in_next = weights(1, "in", (h2,))
            h, hn3, gate, pb, *ahead = _gate_fwd(h2, g3, w_gate, p, i, w_ple, "gate_normproj_fwd",
                                                 follow=(norm1[1:2], w_in_next, rc, rsa, rsb))
            ahead = (w_in_next, ahead)
        else:
            hn3 = gate = pb = None
            loss, d_final, *top = _gate_fwd(h2, g3, w_gate, p, i, w_ple, "gate_loss_gate_bwd",
                                            head=(final_norm.reshape(1, D_MODEL), loss_target))
        saved.append(dict(h0=h0, hn1=hn1, qkv=qkv, y=y, o=o, lse=lse, a=a, h1=h1, hn2=hn2, r=r, h2=h2,
                          hn3=hn3, gate=gate, pb=pb, w_bd=w_bd, scale=scale, g1=g1, g2=g2, g3=g3,
                          w_in=w_in, w_out=w_out, w_up=w_up, w_down=w_down, w_gate=w_gate, w_ple=w_ple))

    grads = [None, None]
    sent = ()
    for i in (1, 0):
        tag = f"_l{i}"
        sv = saved[i]
        if i == 1:
            dh2, dg3, dw_gate, dw_ple = top
        else:
            dh2, dg3, dw_gate, dw_ple = _gate_bwd(dh, sv["gate"], sv["pb"], sv["w_ple"], sv["h2"], sv["g3"],
                                                  sv["w_gate"], sv["hn3"], "gate_bwd" + tag, after=sent)
        dh1, dup, dg2, dh2b = _mlp_bwd(dh2, sv["r"], sv["h1"], sv["g2"], sv["w_up"], sv["w_down"], "mlp_bwd" + tag)
        dw_down = _matmul_tn(sv["r"], dh2b, "dw_down" + tag, square_a=True)
        dw_up = _matmul_tn(sv["hn2"], dup, "dw_up" + tag, blocked_out=True)
        dpool, do0, do1, do2, de0, de1, de2, dw_out = _outproj_bwd(dh1, sv["w_out"], sv["o"], sv["lse"], ones_bd,
                                                                   sv["a"], "outproj_bwd" + tag)
        sent = send(i, "main", dict(w_gate=dw_gate, w_ple=dw_ple, w_down=dw_down, w_up=dw_up, w_out=dw_out))
        dqkv = [_attn_bwd(*sv["qkv"][grp], do_g, sv["lse"][grp], de_g, f"attn_bwd{tag}_g{grp}", after=sent)
                for grp, (do_g, de_g) in enumerate(((do0, de0), (do1, de1), (do2, de2)))]
        dq, dk, dv = zip(*dqkv)
        dh, dz, dg1, dw_bd, dscale = _normproj_bwd(dh1, dpool, sv["y"], sv["w_bd"], sv["scale"], dq, dk, dv, rc, rsa, rsb,
                                                   sv["w_in"], sv["h0"], sv["g1"], "normproj_bwd" + tag)
        grads[i] = dict(norm1=dg1, norm2=dg2, norm3=dg3, pool_w=_diag_blocks(dw_bd), pool_scale=dscale)
        small_sent = send(0, "small", (grads, d_final, loss)) if i == 0 else ()
        dw_in = _matmul_tn(dz, sv["hn1"], "dw_in" + tag, tm=N_IN // 2, after=small_sent)
        sent = send(i, "in", dict(w_in=dw_in))
    return dh, sent


def _pack_small(norm1, norm2, norm3, final_norm, pool_scale, pool_w, spare=None):
    spare = jnp.zeros((1, LANES), F32) if spare is None else spare
    scale_row = jnp.concatenate([pool_scale.reshape(1, 2 * POOL_WIDTH), spare,
                                 jnp.zeros((1, D_MODEL - 2 * POOL_WIDTH - LANES), F32)], axis=1)
    return jnp.concatenate([norm1, norm2, norm3, final_norm.reshape(1, D_MODEL), scale_row,
                            pool_w.reshape(32, D_MODEL)], axis=0)


def _chunks_cols(a, cols):
    return a.reshape(a.shape[0], N_DEV, cols).transpose(1, 0, 2)


def _chunks_rows(a, rows):
    return a.reshape(N_DEV, rows, a.shape[1])


BIG = ("w_in", "w_out", "w_up", "w_down", "w_gate", "w_ple")
SMALL = ("norm1", "norm2", "norm3", "final_norm", "pool_scale", "pool_w")
ORDER = ("norm1", "w_in", "pool_w", "pool_scale", "w_out", "norm2", "w_up", "w_down", "norm3", "w_gate", "w_ple",
         "final_norm")


def kernel(x, p, positions, norm1, w_in, pool_w, pool_scale, w_out, norm2, w_up, w_down, norm3, w_gate, w_ple, final_norm, loss_target, m_norm1, m_w_in, m_pool_w, m_pool_scale, m_w_out, m_norm2, m_w_up, m_w_down, m_norm3, m_w_gate, m_w_ple, m_final_norm, v_norm1, v_w_in, v_pool_w, v_pool_scale, v_w_out, v_norm2, v_w_up, v_w_down, v_norm3, v_w_gate, v_w_ple, v_final_norm):
    w = dict(norm1=norm1, w_in=w_in, pool_w=pool_w, pool_scale=pool_scale, w_out=w_out, norm2=norm2, w_up=w_up,
             w_down=w_down, norm3=norm3, w_gate=w_gate, w_ple=w_ple, final_norm=final_norm)
    m = dict(norm1=m_norm1, w_in=m_w_in, pool_w=m_pool_w, pool_scale=m_pool_scale, w_out=m_w_out, norm2=m_norm2,
             w_up=m_w_up, w_down=m_w_down, norm3=m_norm3, w_gate=m_w_gate, w_ple=m_w_ple, final_norm=m_final_norm)
    v = dict(norm1=v_norm1, w_in=v_w_in, pool_w=v_pool_w, pool_scale=v_pool_scale, w_out=v_w_out, norm2=v_norm2,
             w_up=v_w_up, w_down=v_w_down, norm3=v_norm3, w_gate=v_w_gate, w_ple=v_w_ple, final_norm=v_final_norm)
    seq = x.shape[1]

    bf = {n: [w[n][layer].astype(BF16) for layer in range(2)] for n in BIG}
    bf["w_in"] = [a.T for a in bf["w_in"]]
    me = 4 * lax.axis_index("x") + 2 * lax.axis_index("y") + lax.axis_index("c")
    parts = dict(zip(("in", "out", "mlp", "gate"), (("w_in",), ("w_out",), ("w_up", "w_down"), ("w_gate", "w_ple"))))
    first = _Exchange("gather_first", [[bf["w_in"][0]]], scatter=False)
    later = [pt for pt in parts if pt != "in"]
    gathers = [_Exchange("gather_l0", [[bf[n][0] for n in parts[pt]] for pt in later], scatter=False,
                         after=(first.token,))]
    unpack = dict(w_in=lambda a: a.reshape(N_IN, D_MODEL),
                  w_out=lambda a: a.reshape(D_MODEL, D_MODEL), w_gate=lambda a: a.reshape(D_MODEL, D_MODEL),
                  w_ple=lambda a: a.transpose(1, 0, 2).reshape(PLE_DIM, D_MODEL), w_up=lambda a: a, w_down=lambda a: a)

    def weights(layer, part, after):
        if part == "prefetch":
            if layer != 0:
                return ()
            gathers.append(_Exchange("gather_l1", [[bf[n][1] for n in parts[pt]] for pt in parts], scatter=False,
                                     after=after))
            return (gathers[1].token,)
        if layer == 0 and part == "in":
            shards, lands = first.wait(0, (*after, gathers[0].token))
        elif layer == 0:
            shards, lands = gathers[0].wait(later.index(part), after)
        else:
            shards, lands = gathers[1].wait(tuple(parts).index(part), after)
        full = [unpack[n](lax.dynamic_update_slice_in_dim(land, shard[None], me, axis=0))
                for n, shard, land in zip(parts[part], shards, lands)]
        return full if len(full) > 1 else full[0]

    to_chunks = dict(w_in=lambda a: _chunks_rows(a, N_IN // N_DEV),
                     w_out=lambda a: _chunks_rows(a, D_MODEL // N_DEV),
                     w_up=lambda a: a, w_down=lambda a: _chunks_rows(a, FF_BLOCK),
                     w_gate=lambda a: _chunks_rows(a, D_MODEL // N_DEV), w_ple=lambda a: _chunks_cols(a, D_MODEL // N_DEV))
    own = {n: [None, None] for n in BIG}
    scatters = {}

    def send(layer, part, grads):
        if part == "small":
            per_layer, d_final, loss = grads
            pack = _pack_small(
                *[jnp.concatenate([per_layer[0][n], per_layer[1][n]], axis=0) for n in ("norm1", "norm2", "norm3")],
                d_final.reshape(D_MODEL),
                jnp.concatenate([per_layer[0]["pool_scale"], per_layer[1]["pool_scale"]], axis=0),
                jnp.stack([per_layer[0]["pool_w"], per_layer[1]["pool_w"]]), spare=loss)
            scatters["small"] = _Exchange("gather_small", [[pack]], scatter=False)
            return (scatters["small"].token,)
        for n, (g32, _) in grads.items():
            own[n][layer] = to_chunks[n](g32)
        ex = _Exchange(f"scatter_{part}_l{layer}", [[to_chunks[n](g16) for n, (_, g16) in grads.items()]], scatter=True)
        scatters[layer, part] = (tuple(grads), ex)
        return (ex.token,)

    dx, sent = _local_step(
        x.reshape(seq, D_MODEL), p.reshape(2, seq, PLE_DIM), positions.reshape(seq), loss_target.reshape(seq, D_MODEL),
        norm1, pool_w, pool_scale, norm2, norm3, final_norm, weights, send)

    g_out, d_out, m_out, v_out = {}, {}, {}, {}
    my_index = me.reshape(1)
    for part in ("main", "in"):
        recv = {}
        for layer in (1, 0):
            names, ex = scatters[layer, part]
            for n, r in zip(names, ex.wait(0, sent)[1]):
                recv[n, layer] = r
        for n in names:
            grad = (*own[n], recv[n, 0], recv[n, 1])
            turn = (lambda a: a.transpose(0, 2, 1)) if n == "w_in" else (lambda a: a)
            g_out[n], d_out[n], m_out[n], v_out[n] = map(
                turn, _adamw_sharded(turn(w[n]), turn(m[n]), turn(v[n]), grad, my_index, "adamw_" + n))
        sent = tuple(d_out[n] for n in names)
    (mine,), (landed,) = scatters["small"].wait(0, sent)
    small_g8 = lax.dynamic_update_slice_in_dim(landed, mine[None], me, axis=0)
    *small, spare = _adamw_small(w, small_g8, m, v, "adamw_small")
    for dst, a in zip((g_out, d_out, m_out, v_out), small):
        dst.update(a)

    return (spare[0, 0], dx.reshape(1, seq, D_MODEL), *[g_out[n] for n in ORDER], *[d_out[n] for n in ORDER],
            *[m_out[n] for n in ORDER], *[v_out[n] for n in ORDER])
```

```python
import functools

import jax
import jax.numpy as jnp
from jax import lax
from jax.experimental import pallas as pl
from jax.experimental.pallas import tpu as pltpu

F32 = jnp.float32
BF16 = jnp.bfloat16

D_MODEL = 1024
HEAD_DIM = 64
POOL_WIDTH = 256
POOL_WINDOWS = (2, 4, 8, 16)
POOL_HALO = 16
POOL_PAD = 8
GROUP_WIDTH = 256
DILATIONS = (1, 4, 16)
ATTN_BLOCK = 128
ROT_SHIFT = 8
ROPE_THETA = 500000.0
D_FF = 4096
FF_BLOCK = 512
FF_PER_STEP = 2
MLP_BWD_TILE = 512
FWD_TILE = 1024
N_DEV = 8
N_IN = POOL_WIDTH + 3 * 768
PLE_DIM = 256
EPS = 1e-6
NEG_BIG = -1e30

ADAM_LR = 0.001
ADAM_B1 = 0.9
ADAM_B2 = 0.999
ADAM_EPS = 1e-08
ADAM_WD = 0.01
ADAM_STEP = 10

LANES = 128
SUBLANES = 8
VMEM_LIMIT = 56 * 1024 * 1024
MESH = pl.DeviceIdType.MESH


def _params(n_grid):
    return pltpu.CompilerParams(dimension_semantics=("arbitrary",) * n_grid, vmem_limit_bytes=VMEM_LIMIT)


def _dot(a, b):
    return jnp.dot(a, b, preferred_element_type=F32)


def _dot_nt(a, b):
    return lax.dot_general(a, b, (((1,), (1,)), ((), ())), preferred_element_type=F32)


def _dot_tn(a, b):
    return lax.dot_general(a, b, (((0,), (0,)), ((), ())), preferred_element_type=F32)


def _rms(x, g):
    rstd = lax.rsqrt(jnp.mean(x * x, axis=-1, keepdims=True) + EPS)
    n = x * rstd
    return n, rstd, n * g


def _rms_bwd(dy, n, rstd, g):
    dyn = dy * g
    dx = rstd * (dyn - n * jnp.mean(dyn * n, axis=-1, keepdims=True))
    return dx, jnp.sum(dy * n, axis=0, keepdims=True)


def _ordered_after(body, n_in, after):
    if not after:
        return body
    return lambda *refs: body(*refs[:n_in], *refs[n_in + len(after):])


def _resident(shape):
    return pl.BlockSpec(shape, lambda i: (0,) * len(shape), pipeline_mode=pl.Buffered(1))


def _row_tile(s, t):
    t = min(s, t)
    assert s % t == 0
    return t


def _rot(z, c, sa, sb):
    return z * c + pltpu.roll(z, ROT_SHIFT, 1) * sa + pltpu.roll(z, LANES - ROT_SHIFT, 1) * sb


def _table_specs(t):
    return [pl.BlockSpec((t, LANES), functools.partial(lambda i, k: (i, k), k=k)) for k in range(3)]


def _rot_t(dz, c, sa, sb):
    return dz * c + pltpu.roll(dz * sa, LANES - ROT_SHIFT, 1) + pltpu.roll(dz * sb, ROT_SHIFT, 1)


def _to_residues(value, stage, out_ref, dil):
    if dil == 1:
        out_ref[0] = value.astype(out_ref.dtype)
        return
    rows = value.shape[0] // dil
    for hf in range(GROUP_WIDTH // LANES):
        lanes = slice(hf * LANES, (hf + 1) * LANES)
        stage[hf][...] = value[:, lanes]
        for r in range(dil):
            out_ref[r, :, lanes] = stage[hf][pl.ds(r, rows, stride=dil), :].astype(out_ref.dtype)


def _from_residues(in_ref, stage, dil):
    if dil == 1:
        return in_ref[0].astype(F32)
    rows = in_ref.shape[1]
    for hf in range(GROUP_WIDTH // LANES):
        for r in range(dil):
            stage[hf][pl.ds(r, rows, stride=dil), :] = in_ref[r, :, hf * LANES:(hf + 1) * LANES].astype(F32)
    return jnp.concatenate([stage[0][...], stage[1][...]], axis=1)


def _residue_spec(dil, t):
    return pl.BlockSpec((dil, t // dil, GROUP_WIDTH), lambda i: (0, i, 0))


def _residue_shape(dil, s, dtype):
    return jax.ShapeDtypeStruct((dil, s // dil, GROUP_WIDTH), dtype)


def _stages(t, n):
    return [pltpu.VMEM((t, LANES), F32)] * (n * (GROUP_WIDTH // LANES))


def _pair_stages(refs):
    return [refs[i:i + 2] for i in range(0, len(refs), 2)]


def _normproj_tile(x, g_ref, w_ref, c_ref, sa_ref, sb_ref, hn_ref, u_ref, *rest):
    qkv_refs, stages = rest[:9], _pair_stages(rest[9:])
    _, _, hn = _rms(x, g_ref[...])
    hb = hn.astype(BF16)
    hn_ref[...] = hb
    c, sa, sb = c_ref[...], sa_ref[...], sb_ref[...]

    def rot(z, scale):
        halves = [_rot(z[:, hf * LANES:(hf + 1) * LANES], c, sa, sb) * scale for hf in range(2)]
        return jnp.concatenate(halves, axis=1)

    proj = lambda lo: _dot_nt(hb, w_ref[lo:lo + GROUP_WIDTH, :])
    u_ref[...] = proj(0)
    for grp, dil in enumerate(DILATIONS):
        lo = POOL_WIDTH + grp * GROUP_WIDTH
        q_ref, k_ref, v_ref = qkv_refs[3 * grp:3 * grp + 3]
        _to_residues(rot(proj(lo), HEAD_DIM ** -0.5), stages[0], q_ref, dil)
        _to_residues(rot(proj(lo + 768), 1.0), stages[1], k_ref, dil)
        _to_residues(proj(lo + 1536), stages[2], v_ref, dil)


def _normproj_operands(s, t):
    row = lambda w: pl.BlockSpec((t, w), lambda i: (i, 0))
    in_specs = [pl.BlockSpec((1, D_MODEL), lambda i: (0, 0)), _resident((N_IN, D_MODEL))] + _table_specs(t)
    out_specs = [row(D_MODEL), row(POOL_WIDTH)] + [_residue_spec(dil, t) for dil in DILATIONS for _ in range(3)]
    out_shape = [jax.ShapeDtypeStruct((s, D_MODEL), BF16), jax.ShapeDtypeStruct((s, POOL_WIDTH), F32)]
    out_shape += [_residue_shape(dil, s, BF16) for dil in DILATIONS for _ in range(3)]
    return in_specs, out_specs, out_shape, _stages(t, 3)


def _normproj_fwd(h, g, w_in, rc, rsa, rsb, name):
    s = h.shape[0]
    t = _row_tile(s, FWD_TILE)

    def body(h_ref, *refs):
        _normproj_tile(h_ref[...], *refs)

    in_specs, out_specs, out_shape, scratch = _normproj_operands(s, t)
    return pl.pallas_call(
        body, name=name, grid=(s // t,), in_specs=[pl.BlockSpec((t, D_MODEL), lambda i: (i, 0))] + in_specs,
        out_specs=out_specs, out_shape=out_shape, scratch_shapes=scratch, compiler_params=_params(1),
    )(h, g, w_in, rc, rsa, rsb)


def _pool_lane_window():
    lane = lax.broadcasted_iota(jnp.int32, (1, POOL_WIDTH), 1)
    return jnp.left_shift(2, lane // (POOL_WIDTH // len(POOL_WINDOWS)))


def _window_sums(ext, b2, b4, b8, t, lo, tile, direction):
    rows = t + POOL_HALO
    for src, dst, sh in ((ext, b2, 1), (b2, b4, 2), (b4, b8, 4)):
        dst[lo:lo + rows, :] = src[lo:lo + rows, :] + src[lo + direction * sh:lo + direction * sh + rows, :]
    s16 = b8[tile:tile + t, :] + b8[tile + direction * 8:tile + direction * 8 + t, :]
    win = _pool_lane_window()
    return jnp.where(win == 2, b2[tile:tile + t, :],
                     jnp.where(win == 4, b4[tile:tile + t, :], jnp.where(win == 8, b8[tile:tile + t, :], s16)))


def _pool_fwd_tile(i, u_ref, w_ref, sc_ref, y_ref, ext, b2, b4, b8):
    t = u_ref.shape[0]
    first = POOL_PAD + POOL_HALO

    @pl.when(i == 0)
    def _():
        for buf in (ext, b2, b4):
            buf[0:POOL_PAD, :] = jnp.zeros((POOL_PAD, POOL_WIDTH), F32)
        ext[POOL_PAD:first, :] = jnp.zeros((POOL_HALO, POOL_WIDTH), F32)

    x = u_ref[...]
    ext[first:, :] = x
    wsum = _window_sums(ext, b2, b4, b8, t, POOL_PAD, first, -1)
    pos = i * t + lax.broadcasted_iota(jnp.int32, (t, POOL_WIDTH), 0)
    cnt = jnp.minimum(pos + 1, _pool_lane_window()).astype(F32)
    yb = (wsum / cnt - x).astype(BF16)
    y_ref[...] = yb
    ext[POOL_PAD:first, :] = x[t - POOL_HALO:, :]
    return _dot(yb, w_ref[...]) * sc_ref[...]


def _head_masks():
    lane = lax.broadcasted_iota(jnp.int32, (ATTN_BLOCK, GROUP_WIDTH), 1)
    return [lane // HEAD_DIM == hd for hd in range(GROUP_WIDTH // HEAD_DIM)]


def _stack_heads(a, masks):
    zero = jnp.zeros_like(a)
    return jnp.concatenate([jnp.where(m, a, zero) for m in masks], axis=0)


def _band_bias(first_step):
    rows = ATTN_BLOCK * (GROUP_WIDTH // HEAD_DIM)
    i = lax.broadcasted_iota(jnp.int32, (rows, 2 * ATTN_BLOCK), 0) & (ATTN_BLOCK - 1)
    j = lax.broadcasted_iota(jnp.int32, (rows, 2 * ATTN_BLOCK), 1)
    inner = jnp.where((j >= i) & (j <= i + ATTN_BLOCK), 0.0, NEG_BIG)
    return jnp.where((j < ATTN_BLOCK) & first_step, NEG_BIG, inner), inner


def _column_per_head(a):
    return jnp.concatenate([a[:, hd * HEAD_DIM:hd * HEAD_DIM + 1] for hd in range(GROUP_WIDTH // HEAD_DIM)], axis=0)


def _blocks_per_step(nb):
    if nb <= 16:
        return nb
    return next(qb for qb in (16, 8, 4, 2, 1) if nb % qb == 0)


def _residues_per_step(dil, nb, qb):
    return 2 if (nb == qb and qb < 8 and dil % 2 == 0) else 1


def _attn_fwd(q, k, v, name, after=()):
    dil, length, _ = q.shape
    nb = length // ATTN_BLOCK
    qb = _blocks_per_step(nb)
    rs = _residues_per_step(dil, nb, qb)

    def body(q_ref, kp_ref, kc_ref, vp_ref, vc_ref, o_ref, lse_ref):
        masks = _head_masks()
        bias = _band_bias(pl.program_id(1) == 0)
        for rr in range(rs):
            for qi in range(qb):
                here = slice(qi * ATTN_BLOCK, (qi + 1) * ATTN_BLOCK)
                before = slice((qi - 1) * ATTN_BLOCK, qi * ATTN_BLOCK)
                kcat = jnp.concatenate([kp_ref[rr] if qi == 0 else kc_ref[rr, before], kc_ref[rr, here]], axis=0)
                vcat = jnp.concatenate([vp_ref[rr] if qi == 0 else vc_ref[rr, before], vc_ref[rr, here]], axis=0)
                qs = _stack_heads(q_ref[rr, here], masks)
                sc = _dot_nt(qs, kcat) + bias[min(qi, 1)]
                m = jnp.max(sc, axis=1, keepdims=True)
                e = jnp.exp(sc - m)
                l = jnp.sum(e, axis=1, keepdims=True)
                p = (e / l).astype(BF16)
                lse = m + jnp.log(l)
                o = jnp.zeros((ATTN_BLOCK, GROUP_WIDTH), F32)
                lse_full = jnp.zeros((ATTN_BLOCK, GROUP_WIDTH), F32)
                for hd, msk in enumerate(masks):
                    rows = slice(hd * ATTN_BLOCK, (hd + 1) * ATTN_BLOCK)
                    o = jnp.where(msk, _dot(p[rows], vcat), o)
                    lse_full = jnp.where(msk, lse[rows], lse_full)
                o_ref[rr, here] = o.astype(o_ref.dtype)
                lse_ref[rr, here] = lse_full

    cur = pl.BlockSpec((rs, qb * ATTN_BLOCK, GROUP_WIDTH), lambda r, j: (r, j, 0))
    prev = pl.BlockSpec((rs, ATTN_BLOCK, GROUP_WIDTH), lambda r, j: (r, jnp.maximum(qb * j - 1, 0), 0))
    return pl.pallas_call(
        _ordered_after(body, 5, after), name=name, grid=(dil // rs, nb // qb),
        in_specs=[cur, prev, cur, prev, cur] + [pl.BlockSpec(memory_space=pl.ANY)] * len(after), out_specs=[cur, cur],
        out_shape=[jax.ShapeDtypeStruct(q.shape, BF16), jax.ShapeDtypeStruct(q.shape, F32)],
        compiler_params=_params(2),
    )(q, k, k, v, v, *after)


def _group_weights(l0, l1, l2):
    m = jnp.maximum(jnp.maximum(l0, l1), l2)
    e0, e1, e2 = jnp.exp(l0 - m), jnp.exp(l1 - m), jnp.exp(l2 - m)
    den = e0 + e1 + e2
    return e0 / den, e1 / den, e2 / den


def _outproj_fwd(h, u, w_bd, scale, o, lse, w_out, name):
    s = h.shape[0]
    t = _row_tile(s, FWD_TILE)

    def body(h_ref, u_ref, wbd_ref, sc_ref, o0, o1, o2, l0, l1, l2, w_ref, out_ref, a_ref, y_ref, ext, b2, b4, b8,
             *stages):
        pool_out = _pool_fwd_tile(pl.program_id(0), u_ref, wbd_ref, sc_ref, y_ref, ext, b2, b4, b8)
        stages = _pair_stages(stages)
        ov = [_from_residues(r, stages[i], DILATIONS[i]) for i, r in enumerate((o0, o1, o2))]
        lv = [_from_residues(r, stages[3 + i], DILATIONS[i]) for i, r in enumerate((l0, l1, l2))]
        wts = _group_weights(*lv)
        a = jnp.concatenate([pool_out] + [ov[i] * wts[i] for i in range(3)], axis=1).astype(BF16)
        a_ref[...] = a
        out_ref[...] = h_ref[...] + _dot(a, w_ref[...])

    row = lambda w: pl.BlockSpec((t, w), lambda i: (i, 0))
    res = [_residue_spec(dil, t) for dil in DILATIONS]
    return pl.pallas_call(
        body, name=name, grid=(s // t,),
        in_specs=[row(D_MODEL), row(POOL_WIDTH), _resident((POOL_WIDTH, POOL_WIDTH)), _resident((1, POOL_WIDTH))]
        + res + res + [_resident((D_MODEL, D_MODEL))],
        out_specs=[row(D_MODEL), row(D_MODEL), row(POOL_WIDTH)],
        out_shape=[jax.ShapeDtypeStruct((s, D_MODEL), F32), jax.ShapeDtypeStruct((s, D_MODEL), BF16),
                   jax.ShapeDtypeStruct((s, POOL_WIDTH), BF16)],
        scratch_shapes=[pltpu.VMEM((t + POOL_HALO + POOL_PAD, POOL_WIDTH), F32)] * 4 + _stages(t, 6),
        compiler_params=_params(1),
    )(h, u, w_bd, scale, *o, *lse, w_out)


def _mlp_fwd(h, g, w_up, w_down, name):
    s = h.shape[0]
    t = _row_tile(s, 512)
    nblk = D_FF // FF_BLOCK

    def body(h_ref, g_ref, wu_ref, wd_ref, out_ref, hn_ref, r_ref):
        x = h_ref[...]
        _, _, hn = _rms(x, g_ref[...])
        hb = hn.astype(BF16)
        hn_ref[...] = hb
        acc = None
        for b0 in range(0, nblk, FF_PER_STEP):
            acts = []
            for b in range(b0, b0 + FF_PER_STEP):
                r = jnp.maximum(_dot(hb, wu_ref[b]), 0.0)
                r_ref[:, b * FF_BLOCK:(b + 1) * FF_BLOCK] = r.astype(BF16)
                acts.append((r * r).astype(BF16))
            wd = wd_ref[b0:b0 + FF_PER_STEP].reshape(FF_PER_STEP * FF_BLOCK, D_MODEL)
            part = _dot(jnp.concatenate(acts, axis=1), wd)
            acc = part if acc is None else acc + part
        out_ref[...] = x + acc

    row = lambda w: pl.BlockSpec((t, w), lambda i: (i, 0))
    resident = lambda shape: pl.BlockSpec(shape, lambda i: (0, 0, 0), pipeline_mode=pl.Buffered(1))
    return pl.pallas_call(
        body, name=name, grid=(s // t,),
        in_specs=[row(D_MODEL), pl.BlockSpec((1, D_MODEL), lambda i: (0, 0)),
                  resident((nblk, D_MODEL, FF_BLOCK)), resident((nblk, FF_BLOCK, D_MODEL))],
        out_specs=[row(D_MODEL), row(D_MODEL), row(D_FF)],
        out_shape=[jax.ShapeDtypeStruct((s, D_MODEL), F32), jax.ShapeDtypeStruct((s, D_MODEL), BF16),
                   jax.ShapeDtypeStruct((s, D_FF), BF16)],
        compiler_params=_params(1),
    )(h, g, w_up, w_down)


def _gate_fwd(h, g, w_gate, p, layer, w_ple, name, head=None, follow=None):
    assert (head is None) != (follow is None)
    s = h.shape[0]
    t = _row_tile(s, 512)
    last = s // t - 1

    def body(h_ref, g_ref, wg_ref, p_ref, wp_ref, *refs):
        x = h_ref[...]
        gv = g_ref[...]
        n, rstd, hn = _rms(x, gv)
        hb = hn.astype(BF16)
        gate = 1.0 / (1.0 + jnp.exp(-_dot(hb, wg_ref[...])))
        pb = p_ref[...].astype(BF16)
        e = _dot(pb, wp_ref[...])
        h3 = x + gate * e
        if follow is not None:
            out_ref, hn_ref, gate_ref, pb_ref = refs[5:9]
            out_ref[...] = h3
            hn_ref[...] = hb
            pb_ref[...] = pb
            gate_ref[...] = gate.astype(BF16)
            _normproj_tile(h3, *refs[:5], *refs[9:])
            return
        gf_ref, t_ref, loss_ref, dgf_ref, out_ref, dg_ref, dwg_ref, dwgb_ref, dwp_ref, dwpb_ref = refs
        i = pl.program_id(0)

        @pl.when(i == 0)
        def _():
            for ref in (loss_ref, dgf_ref, dg_ref, dwg_ref, dwp_ref):
                ref[...] = jnp.zeros_like(ref)

        gf = gf_ref[...]
        n3, rstd3, y = _rms(h3, gf)
        err = y - t_ref[...]
        loss_ref[...] += jnp.sum(err * err) * (0.5 / D_MODEL)
        d, dgf = _rms_bwd(err * (1.0 / D_MODEL), n3, rstd3, gf)
        dgf_ref[...] += dgf
        dgl = (d * e * gate * (1.0 - gate)).astype(BF16)
        dwg_ref[...] += _dot_tn(hb, dgl)
        dwp_ref[...] += _dot_tn(pb, (d * gate).astype(BF16))
        dx, dg = _rms_bwd(_dot_nt(dgl, wg_ref[...]), n, rstd, gv)
        out_ref[...] = d + dx
        dg_ref[...] += dg

        @pl.when(i == last)
        def _():
            dwgb_ref[...] = dwg_ref[...].astype(BF16)
            dwpb_ref[...] = dwp_ref[...].astype(BF16)

    row = lambda w: pl.BlockSpec((t, w), lambda i: (i, 0))
    full = lambda a, b: pl.BlockSpec((a, b), lambda i: (0, 0))
    in_specs = [row(D_MODEL), full(1, D_MODEL), _resident((D_MODEL, D_MODEL)),
                pl.BlockSpec((None, t, PLE_DIM), lambda i: (layer, i, 0)), _resident((PLE_DIM, D_MODEL))]
    if follow is not None:
        next_in, next_out, next_shape, scratch = _normproj_operands(s, t)
        return pl.pallas_call(
            body, name=name, grid=(s // t,), in_specs=in_specs + next_in,
            out_specs=[row(D_MODEL), row(D_MODEL), row(D_MODEL), row(PLE_DIM)] + next_out,
            out_shape=[jax.ShapeDtypeStruct((s, D_MODEL), F32), jax.ShapeDtypeStruct((s, D_MODEL), BF16),
                       jax.ShapeDtypeStruct((s, D_MODEL), BF16), jax.ShapeDtypeStruct((s, PLE_DIM), BF16)] + next_shape,
            scratch_shapes=scratch, compiler_params=_params(1),
        )(h, g, w_gate, p, w_ple, *follow)
    loss, dgf, dh2, dg, dwg, dwgb, dwp, dwpb = pl.pallas_call(
        body, name=name, grid=(s // t,), in_specs=in_specs + [full(1, D_MODEL), row(D_MODEL)],
        out_specs=[pl.BlockSpec((1, LANES), lambda i: (0, 0)), full(1, D_MODEL), row(D_MODEL), full(1, D_MODEL),
                   full(D_MODEL, D_MODEL), full(D_MODEL, D_MODEL), full(PLE_DIM, D_MODEL), full(PLE_DIM, D_MODEL)],
        out_shape=[jax.ShapeDtypeStruct((1, LANES), F32), jax.ShapeDtypeStruct((1, D_MODEL), F32),
                   jax.ShapeDtypeStruct((s, D_MODEL), F32), jax.ShapeDtypeStruct((1, D_MODEL), F32),
                   jax.ShapeDtypeStruct((D_MODEL, D_MODEL), F32), jax.ShapeDtypeStruct((D_MODEL, D_MODEL), BF16),
                   jax.ShapeDtypeStruct((PLE_DIM, D_MODEL), F32), jax.ShapeDtypeStruct((PLE_DIM, D_MODEL), BF16)],
        compiler_params=_params(1),
    )(h, g, w_gate, p, w_ple, *head)
    return loss, dgf, dh2, dg, (dwg, dwgb), (dwp, dwpb)


def _gate_bwd(dh, gate, pb, w_ple, h, g, w_gate, hn, name, after=()):
    s = h.shape[0]
    t = _row_tile(s, FWD_TILE)
    last = s // t - 1

    def body(dh_ref, gate_ref, pb_ref, wp_ref, h_ref, g_ref, wg_ref, hn_ref, out_ref, dg_ref, dwg_ref, dwgb_ref,
             dwp_ref, dwpb_ref):
        i = pl.program_id(0)

        @pl.when(i == 0)
        def _():
            dg_ref[...] = jnp.zeros_like(dg_ref)
            dwg_ref[...] = jnp.zeros_like(dwg_ref)
            dwp_ref[...] = jnp.zeros_like(dwp_ref)

        d = dh_ref[...]
        gate = gate_ref[...].astype(F32)
        pb = pb_ref[...]
        e = _dot(pb, wp_ref[...])
        dgl = (d * e * gate * (1.0 - gate)).astype(BF16)
        dwg_ref[...] += _dot_tn(hn_ref[...], dgl)
        dwp_ref[...] += _dot_tn(pb, (d * gate).astype(BF16))
        gv = g_ref[...]
        n, rstd, _ = _rms(h_ref[...], gv)
        dx, dg = _rms_bwd(_dot_nt(dgl, wg_ref[...]), n, rstd, gv)
        out_ref[...] = d + dx
        dg_ref[...] += dg

        @pl.when(i == last)
        def _():
            dwgb_ref[...] = dwg_ref[...].astype(BF16)
            dwpb_ref[...] = dwp_ref[...].astype(BF16)

    row = lambda w: pl.BlockSpec((t, w), lambda i: (i, 0))
    full = lambda a, b: pl.BlockSpec((a, b), lambda i: (0, 0))
    dh2, dg, dwg, dwgb, dwp, dwpb = pl.pallas_call(
        _ordered_after(body, 8, after), name=name, grid=(s // t,),
        in_specs=[row(D_MODEL), row(D_MODEL), row(PLE_DIM), _resident((PLE_DIM, D_MODEL)), row(D_MODEL),
                  full(1, D_MODEL), _resident((D_MODEL, D_MODEL)), row(D_MODEL)]
        + [pl.BlockSpec(memory_space=pl.ANY)] * len(after),
        out_specs=[row(D_MODEL), full(1, D_MODEL), full(D_MODEL, D_MODEL), full(D_MODEL, D_MODEL),
                   full(PLE_DIM, D_MODEL), full(PLE_DIM, D_MODEL)],
        out_shape=[jax.ShapeDtypeStruct((s, D_MODEL), F32), jax.ShapeDtypeStruct((1, D_MODEL), F32),
                   jax.ShapeDtypeStruct((D_MODEL, D_MODEL), F32), jax.ShapeDtypeStruct((D_MODEL, D_MODEL), BF16),
                   jax.ShapeDtypeStruct((PLE_DIM, D_MODEL), F32), jax.ShapeDtypeStruct((PLE_DIM, D_MODEL), BF16)],
        compiler_params=_params(1),
    )(dh, gate, pb, w_ple, h, g, w_gate, hn, *after)
    return dh2, dg, (dwg, dwgb), (dwp, dwpb)


def _mlp_bwd(dh, r, h, g, w_up, w_down, name):
    s = h.shape[0]
    t = _row_tile(s, MLP_BWD_TILE)
    nblk = D_FF // FF_BLOCK

    def body(dh_ref, r_ref, h_ref, g_ref, wu_ref, wd_ref, out_ref, dup_ref, dg_ref, dhb_ref):
        @pl.when(pl.program_id(0) == 0)
        def _():
            dg_ref[...] = jnp.zeros_like(dg_ref)

        d = dh_ref[...]
        db = d.astype(BF16)
        dhb_ref[...] = db
        back = None
        for b in range(nblk):
            cols = slice(b * FF_BLOCK, (b + 1) * FF_BLOCK)
            dup = (_dot_nt(db, wd_ref[b]) * (2.0 * r_ref[:, cols].astype(F32))).astype(BF16)
            dup_ref[:, cols] = dup
            part = _dot_nt(dup, wu_ref[b])
            back = part if back is None else back + part
        gv = g_ref[...]
        n, rstd, _ = _rms(h_ref[...], gv)
        dx, dg = _rms_bwd(back, n, rstd, gv)
        out_ref[...] = d + dx
        dg_ref[...] += dg

    row = lambda w: pl.BlockSpec((t, w), lambda i: (i, 0))
    vec = pl.BlockSpec((1, D_MODEL), lambda i: (0, 0))
    resident = lambda shape: pl.BlockSpec(shape, lambda i: (0, 0, 0), pipeline_mode=pl.Buffered(1))
    return pl.pallas_call(
        body, name=name, grid=(s // t,),
        in_specs=[row(D_MODEL), row(D_FF), row(D_MODEL), vec,
                  resident((nblk, D_MODEL, FF_BLOCK)), resident((nblk, FF_BLOCK, D_MODEL))],
        out_specs=[row(D_MODEL), row(D_FF), vec, row(D_MODEL)],
        out_shape=[jax.ShapeDtypeStruct((s, D_MODEL), F32), jax.ShapeDtypeStruct((s, D_FF), BF16),
                   jax.ShapeDtypeStruct((1, D_MODEL), F32), jax.ShapeDtypeStruct((s, D_MODEL), BF16)],
        compiler_params=_params(1),
    )(dh, r, h, g, w_up, w_down)


def _outproj_bwd(dh, w_out, o, lse, ones_bd, a, name):
    s = dh.shape[0]
    t = _row_tile(s, 512)
    last = s // t - 1

    def body(dh_ref, w_ref, o0, o1, o2, l0, l1, l2, bd_ref, a_ref, dp_ref, do0, do1, do2, de0, de1, de2, dw_ref,
             dwb_ref, *stages):
        i = pl.program_id(0)

        @pl.when(i == 0)
        def _():
            dw_ref[...] = jnp.zeros_like(dw_ref)

        stages = _pair_stages(stages)
        dhb = dh_ref[...].astype(BF16)
        dw_ref[...] += _dot_tn(a_ref[...], dhb)

        @pl.when(i == last)
        def _():
            dwb_ref[...] = dw_ref[...].astype(BF16)

        da = _dot_nt(dhb, w_ref[...])
        dp_ref[...] = da[:, 0:POOL_WIDTH]
        ov =[_from_residues(r, stages[i], DILATIONS[i]) for i, r in enumerate((o0, o1, o2))]
        lv = [_from_residues(r, stages[3 + i], DILATIONS[i]) for i, r in enumerate((l0, l1, l2))]
        wts = _group_weights(*lv)
        bd = bd_ref[...]
        cbar = jnp.zeros((t, GROUP_WIDTH), F32)
        for grp, do_ref in enumerate((do0, do1, do2)):
            lo = POOL_WIDTH + grp * GROUP_WIDTH
            dag = da[:, lo:lo + GROUP_WIDTH]
            _to_residues(dag * wts[grp], stages[6 + grp], do_ref, DILATIONS[grp])
            prod = dag * ov[grp]
            hi = prod.astype(BF16)
            low = (prod - hi.astype(F32)).astype(BF16)
            cbar = cbar + wts[grp] * (_dot(hi, bd) + _dot(low, bd))
        for grp, de_ref in enumerate((de0, de1, de2)):
            _to_residues(wts[grp] * cbar, stages[9 + grp], de_ref, DILATIONS[grp])

    row = lambda w: pl.BlockSpec((t, w), lambda i: (i, 0))
    full = lambda a, b: pl.BlockSpec((a, b), lambda i: (0, 0))
    res = [_residue_spec(dil, t) for dil in DILATIONS]
    *outs, dw, dwb = pl.pallas_call(
        body, name=name, grid=(s // t,),
        in_specs=[row(D_MODEL), full(D_MODEL, D_MODEL)] + res + res + [full(GROUP_WIDTH, GROUP_WIDTH), row(D_MODEL)],
        out_specs=[row(POOL_WIDTH)] + res + res + [full(D_MODEL, D_MODEL)] * 2,
        out_shape=[jax.ShapeDtypeStruct((s, POOL_WIDTH), F32)] + [_residue_shape(dil, s, BF16) for dil in DILATIONS]
        + [_residue_shape(dil, s, F32) for dil in DILATIONS]
        + [jax.ShapeDtypeStruct((D_MODEL, D_MODEL), F32), jax.ShapeDtypeStruct((D_MODEL, D_MODEL), BF16)],
        scratch_shapes=_stages(t, 12),
        compiler_params=_params(1),
    )(dh, w_out, *o, *lse, ones_bd, a)
    return (*outs, (dw, dwb))


def _attn_bwd(q, k, v, do, lse, deff, name, after=()):
    dil, length, _ = q.shape
    nb = length // ATTN_BLOCK
    qb = _blocks_per_step(nb)
    nj = nb // qb
    rs = _residues_per_step(dil, nb, qb)
    whole = nj == 1
    tail = slice((qb - 1) * ATTN_BLOCK, qb * ATTN_BLOCK)
    block = lambda qi: slice(qi * ATTN_BLOCK, (qi + 1) * ATTN_BLOCK)

    def body(q_ref, kp_ref, kc_ref, vp_ref, vc_ref, do_ref, lse_ref, de_ref, dq_ref, dk_ref, dv_ref, ck, cv):
        j = pl.program_id(1)

        def compute():
            masks = _head_masks()
            bias = _band_bias(j == 0)
            for rr in range(rs):
                dkc, dvc = [], []
                for qi in range(qb):
                    here, before = block(qi), block(qi - 1)
                    kcat = jnp.concatenate([kp_ref[rr] if qi == 0 else kc_ref[rr, before], kc_ref[rr, here]], axis=0)
                    vcat = jnp.concatenate([vp_ref[rr] if qi == 0 else vc_ref[rr, before], vc_ref[rr, here]], axis=0)
                    qs = _stack_heads(q_ref[rr, here], masks)
                    dos = _stack_heads(do_ref[rr, here], masks)
                    sc = _dot_nt(qs, kcat) + bias[min(qi, 1)]
                    p = jnp.exp(sc - _column_per_head(lse_ref[rr, here]))
                    ds = (p * (_dot_nt(dos, vcat) - _column_per_head(de_ref[rr, here]))).astype(BF16)
                    dq = jnp.zeros((ATTN_BLOCK, GROUP_WIDTH), F32)
                    for hd, msk in enumerate(masks):
                        dq = jnp.where(msk, _dot(ds[block(hd)], kcat), dq)
                    dq_ref[rr, here] = dq.astype(dq_ref.dtype)
                    dkc.append(_dot_tn(ds, qs))
                    dvc.append(_dot_tn(p.astype(BF16), dos))

                for out_ref, carry, parts in ((dk_ref, ck, dkc), (dv_ref, cv, dvc)):
                    full = [parts[qi][ATTN_BLOCK:] + parts[qi + 1][0:ATTN_BLOCK] for qi in range(qb - 1)]
                    if whole:
                        for qi, val in enumerate(full + [parts[qb - 1][ATTN_BLOCK:]]):
                            out_ref[rr, block(qi)] = val.astype(out_ref.dtype)
                        continue

                    @pl.when(j > 0)
                    def _():
                        if qb > 1:
                            out_ref[0, 0:(qb - 1) * ATTN_BLOCK] = carry[0:(qb - 1) * ATTN_BLOCK].astype(out_ref.dtype)
                        out_ref[0, tail] = (carry[tail] + parts[0][0:ATTN_BLOCK]).astype(out_ref.dtype)

                    for qi, val in enumerate(full):
                        carry[block(qi)] = val
                    carry[tail] = parts[qb - 1][ATTN_BLOCK:]

        if whole:
            compute()
        else:
            pl.when(j < nj)(compute)

            @pl.when(j == nj)
            def _():
                dk_ref[0] = ck[...].astype(dk_ref.dtype)
                dv_ref[0] = cv[...].astype(dv_ref.dtype)

    step = lambda j: jnp.minimum(j, nj - 1)
    cur = pl.BlockSpec((rs, qb * ATTN_BLOCK, GROUP_WIDTH), lambda r, j: (r, step(j), 0))
    prev = pl.BlockSpec((rs, ATTN_BLOCK, GROUP_WIDTH), lambda r, j: (r, jnp.maximum(qb * step(j) - 1, 0), 0))
    late = pl.BlockSpec((rs, qb * ATTN_BLOCK, GROUP_WIDTH), lambda r, j: (r, jnp.maximum(j - 1, 0), 0))
    return pl.pallas_call(
        _ordered_after(body, 8, after), name=name, grid=(dil // rs, 1 if whole else nj + 1),
        in_specs=[cur, prev, cur, prev, cur, cur, cur, cur] + [pl.BlockSpec(memory_space=pl.ANY)] * len(after),
        out_specs=[cur, cur if whole else late, cur if whole else late],
        out_shape=[jax.ShapeDtypeStruct(q.shape, BF16)] * 3,
        scratch_shapes=[pltpu.VMEM((qb * ATTN_BLOCK, GROUP_WIDTH), F32)] * 2,
        compiler_params=_params(2),
    )(q, k, k, v, v, do, lse, deff, *after)


def _pool_bwd_tile(i, nt, dp_ref, y_ref, w_ref, sc_ref, dw_ref, dsc_ref, ext, b2, b4, b8):
    t = dp_ref.shape[0]

    @pl.when(i == 0)
    def _():
        ext[t:, :] = jnp.zeros((POOL_HALO + POOL_PAD, POOL_WIDTH), F32)
        for buf in (b2, b4):
            buf[t + POOL_HALO:, :] = jnp.zeros((POOL_PAD, POOL_WIDTH), F32)
        dw_ref[...] = jnp.zeros_like(dw_ref)
        dsc_ref[...] = jnp.zeros_like(dsc_ref)

    dp = dp_ref[...]
    yb = y_ref[...]
    w = w_ref[...]
    dsc_ref[...] += jnp.sum(dp * _dot(yb, w), axis=0, keepdims=True)
    dyo = (dp * sc_ref[...]).astype(BF16)
    dw_ref[...] += _dot_tn(yb, dyo)
    dy = _dot_nt(dyo, w)
    pos = (nt - 1 - i) * t + lax.broadcasted_iota(jnp.int32, (t, POOL_WIDTH), 0)
    gq = dy / jnp.minimum(pos + 1, _pool_lane_window()).astype(F32)
    ext[0:t, :] = gq
    du = _window_sums(ext, b2, b4, b8, t, 0, 0, 1) - dy
    ext[t:t + POOL_HALO, :] = gq[0:POOL_HALO, :]
    return du


def _normproj_bwd(dh, dpool, y, w_bd, scale, dq, dk, dv, rc, rsa, rsb, w_in, h, g, name):
    s = h.shape[0]
    t = _row_tile(s, 512)
    nt = s // t

    def body(dh_ref, dp_ref, y_ref, wbd_ref, sc_ref, q0, q1, q2, k0, k1, k2, v0, v1, v2, c_ref, sa_ref, sb_ref, w_ref,
             h_ref, g_ref, out_ref, dz_ref, dg_ref, dwbd_ref, dsc_ref, ext, b2, b4, b8, *stages):
        step = pl.program_id(0)

        @pl.when(step == 0)
        def _():
            dg_ref[...] = jnp.zeros_like(dg_ref)

        du = _pool_bwd_tile(step, nt, dp_ref, y_ref, wbd_ref, sc_ref, dwbd_ref, dsc_ref, ext, b2, b4, b8)
        c, sa, sb = c_ref[...], sa_ref[...], sb_ref[...]

        def unrot(a, scale):
            halves = [_rot_t(a[:, hf * LANES:(hf + 1) * LANES] * scale, c, sa, sb) for hf in range(2)]
            return jnp.concatenate(halves, axis=1)

        staged = _pair_stages(stages)
        tok = lambda refs, base: [_from_residues(r, staged[base + i], DILATIONS[i]) for i, r in enumerate(refs)]
        chunks = [du]
        chunks += [unrot(a, HEAD_DIM ** -0.5) for a in tok((q0, q1, q2), 0)]
        chunks += [unrot(a, 1.0) for a in tok((k0, k1, k2), 3)]
        chunks += tok((v0, v1, v2), 6)
        acc = jnp.zeros((t, D_MODEL), F32)
        for ci, ch in enumerate(chunks):
            cols = slice(ci * GROUP_WIDTH, (ci + 1) * GROUP_WIDTH)
            cb = ch.astype(BF16)
            dz_ref[:, cols] = cb
            acc = acc + _dot(cb, w_ref[cols, :])
        gv = g_ref[...]
        n, rstd, _ = _rms(h_ref[...], gv)
        dx, dg = _rms_bwd(acc, n, rstd, gv)
        out_ref[...] = dh_ref[...] + dx
        dg_ref[...] += dg

    back = lambda i: nt - 1 - i
    row = lambda w: pl.BlockSpec((t, w), lambda i: (back(i), 0))
    full = lambda a, b: pl.BlockSpec((a, b), lambda i: (0, 0))
    res = [pl.BlockSpec((dil, t // dil, GROUP_WIDTH), lambda i: (0, back(i), 0)) for dil in DILATIONS]
    tables = [pl.BlockSpec((t, LANES), functools.partial(lambda i, k: (back(i), k), k=k)) for k in range(3)]
    return pl.pallas_call(
        body, name=name, grid=(nt,),
        in_specs=[row(D_MODEL), row(POOL_WIDTH), row(POOL_WIDTH), full(POOL_WIDTH, POOL_WIDTH), full(1, POOL_WIDTH)]
        + res * 3 + tables + [full(N_IN, D_MODEL), row(D_MODEL), full(1, D_MODEL)],
        out_specs=[row(D_MODEL), row(N_IN), full(1, D_MODEL), full(POOL_WIDTH, POOL_WIDTH), full(1, POOL_WIDTH)],
        out_shape=[jax.ShapeDtypeStruct((s, D_MODEL), F32), jax.ShapeDtypeStruct((s, N_IN), BF16),
                   jax.ShapeDtypeStruct((1, D_MODEL), F32), jax.ShapeDtypeStruct((POOL_WIDTH, POOL_WIDTH), F32),
                   jax.ShapeDtypeStruct((1, POOL_WIDTH), F32)],
        scratch_shapes=[pltpu.VMEM((t + POOL_HALO + POOL_PAD, POOL_WIDTH), F32)] * 4 + _stages(t, 9),
        compiler_params=_params(1),
    )(dh, dpool, y, w_bd, scale, *dq, *dk, *dv, rc, rsa, rsb, w_in, h, g)


def _matmul_tn(a, b, name, *, square_a=False, tm=None, tn=None, blocked_out=False, after=()):
    s, m = a.shape
    n = b.shape[1]
    tk = _row_tile(s, 2048)
    tm = tm or min(m, 1024)
    tn = tn or min(n, 1024)
    assert m % tm == 0 and n % tn == 0
    nk = s // tk
    nsub = tn // FF_BLOCK if blocked_out else 1

    def body(a_ref, b_ref, o_ref, ob_ref, acc):
        k = pl.program_id(2)

        def product():
            av = a_ref[...]
            if square_a:
                av = av.astype(F32)
                av = av * av
            return _dot_tn(av.astype(BF16), b_ref[...].astype(BF16))

        def emit(total):
            if blocked_out:
                for sub in range(nsub):
                    cols = slice(sub * FF_BLOCK, (sub + 1) * FF_BLOCK)
                    o_ref[sub] = total[:, cols]
                    ob_ref[sub] = total[:, cols].astype(BF16)
            else:
                o_ref[...] = total
                ob_ref[...] = total.astype(BF16)

        if nk == 1:
            emit(product())
            return

        @pl.when(k == 0)
        def _():
            acc[...] = product()

        @pl.when((k > 0) & (k < nk - 1))
        def _():
            acc[...] += product()

        @pl.when(k == nk - 1)
        def _():
            emit(acc[...] + product())

    if blocked_out:
        shape = (n // FF_BLOCK, m, FF_BLOCK)
        out_spec = pl.BlockSpec((nsub, tm, FF_BLOCK), lambda i, j, k: (j, i, 0))
    else:
        shape = (m, n)
        out_spec = pl.BlockSpec((tm, tn), lambda i, j, k: (i, j))
    return pl.pallas_call(
        _ordered_after(body, 2, after), name=name, grid=(m // tm, n // tn, nk),
        in_specs=[pl.BlockSpec((tk, tm), lambda i, j, k: (k, i)), pl.BlockSpec((tk, tn), lambda i, j, k: (k, j))]
        + [pl.BlockSpec(memory_space=pl.ANY)] * len(after),
        out_specs=[out_spec, out_spec],
        out_shape=[jax.ShapeDtypeStruct(shape, F32), jax.ShapeDtypeStruct(shape, BF16)],
        scratch_shapes=[pltpu.VMEM((tm, tn), F32)],
        compiler_params=_params(3),
    )(a, b, *after)


def _adamw_math(w, g, m, v):
    m = ADAM_B1 * m + (1.0 - ADAM_B1) * g
    v = ADAM_B2 * v + (1.0 - ADAM_B2) * (g * g)
    m_hat = m / (1.0 - ADAM_B1 ** ADAM_STEP)
    v_hat = v / (1.0 - ADAM_B2 ** ADAM_STEP)
    delta = -ADAM_LR * (m_hat / (jnp.sqrt(v_hat) + ADAM_EPS) + ADAM_WD * w)
    return delta, m, v


def _sum_chunks_body(own0_ref, own1_ref, r0_ref, r1_ref):
    layer0 = pl.program_id(0) == 0
    g = jnp.where(layer0, own0_ref[...], own1_ref[...])
    for k in range(N_DEV - 1):
        g = g + jnp.where(layer0, r0_ref[k], r1_ref[k]).astype(F32)
    return g


def _chunk_specs(t, cols):
    rows_of = lambda layer: (lambda l, i: jnp.where(l == layer, i, 0))
    blk = pl.BlockSpec((None, t, cols), lambda l, i, me: (l, i, 0))
    own = [pl.BlockSpec((None, t, cols), functools.partial(lambda l, i, me, pick: (me[0], pick(l, i), 0), pick=rows_of(ly)))
           for ly in range(2)]
    recv = [pl.BlockSpec((N_DEV - 1, t, cols), functools.partial(lambda l, i, me, pick: (0, pick(l, i), 0), pick=rows_of(ly)))
            for ly in range(2)]
    return blk, own + recv


def _adamw_sharded(w, m, v, chunks, me, name):
    _, rows, cols = w.shape
    t = max(d for d in range(SUBLANES, min(rows, 256) + 1, SUBLANES) if rows % d == 0)

    def body(me_ref, w_ref, m_ref, v_ref, own0_ref, own1_ref, r0_ref, r1_ref, g_ref, d_ref, nm_ref, nv_ref):
        g = _sum_chunks_body(own0_ref, own1_ref, r0_ref, r1_ref)
        g_ref[...] = g
        d_ref[...], nm_ref[...], nv_ref[...] = _adamw_math(w_ref[...], g, m_ref[...], v_ref[...])

    blk, chunk_specs = _chunk_specs(t, cols)
    return pl.pallas_call(
        body, name=name,
        grid_spec=pltpu.PrefetchScalarGridSpec(
            num_scalar_prefetch=1, grid=(2, rows // t), in_specs=[blk, blk, blk] + chunk_specs, out_specs=[blk] * 4),
        out_shape=[jax.ShapeDtypeStruct(w.shape, F32)] * 4,
        compiler_params=_params(2),
    )(me, w, m, v, *chunks)


def _adamw_small(w, g8, m, v, name):
    rows = dict(norm1=(0, 2), norm2=(2, 4), norm3=(4, 6), final_norm=(6, 7), pool_w=(8, 40))
    shaped = lambda t: [t[n].reshape(rows[n][1] - rows[n][0], D_MODEL) if n in rows else t[n] for n in SMALL]
    k = len(SMALL)

    def body(g8_ref, *refs):
        w_refs, m_refs, v_refs = refs[:k], refs[k:2 * k], refs[2 * k:3 * k]
        outs = [refs[(3 + i) * k:(4 + i) * k] for i in range(4)]
        spare_ref = refs[-1]
        g = g8_ref[0]
        for dev in range(1, N_DEV):
            g = g + g8_ref[dev]
        spare_ref[...] = g[7:8, 2 * POOL_WIDTH:2 * POOL_WIDTH + LANES]
        for i, n in enumerate(SMALL):
            if n in rows:
                pieces = [(slice(None), g[rows[n][0]:rows[n][1]])]
            else:
                pieces = [(slice(ly, ly + 1), g[7:8, ly * POOL_WIDTH:(ly + 1) * POOL_WIDTH]) for ly in range(2)]
            for at, gp in pieces:
                new = _adamw_math(w_refs[i][at], gp, m_refs[i][at], v_refs[i][at])
                for out, val in zip(outs, (gp, *new)):
                    out[i][at] = val

    ins = shaped(w) + shaped(m) + shaped(v)
    res = pl.pallas_call(
        body, name=name,
        out_shape=[jax.ShapeDtypeStruct(a.shape, F32) for a in shaped(w)] * 4 + [jax.ShapeDtypeStruct((1, LANES), F32)],
        compiler_params=pltpu.CompilerParams(vmem_limit_bytes=VMEM_LIMIT),
    )(g8, *ins)
    dicts = [{n: a.reshape(w[n].shape) for n, a in zip(SMALL, res[i * k:(i + 1) * k])} for i in range(4)]
    return (*dicts, res[-1])


def _peer(k):
    x, y, c = lax.axis_index("x"), lax.axis_index("y"), lax.axis_index("c")
    return (1 - x if k & 4 else x, 1 - y if k & 2 else y, 1 - c if k & 1 else c)


def _linear(dev):
    return 4 * dev[0] + 2 * dev[1] + dev[2]


HBM_SPEC = pl.BlockSpec(memory_space=pltpu.HBM)
SEM_SPEC = pl.BlockSpec(memory_space=pltpu.SEMAPHORE)
ANY_SPEC = pl.BlockSpec(memory_space=pl.ANY)
EFFECT = pltpu.SideEffectType.DATAFLOW_SIDE_EFFECTING


def _in_hbm(a):
    return pltpu.with_memory_space_constraint(a, pltpu.HBM)


class _Exchange:
    def __init__(self, name, groups, scatter, after=()):
        self.name, self.scatter = name, scatter
        self.sizes = sizes = [len(g) for g in groups]
        srcs = [a for g in groups for a in g]
        n, ng = len(srcs), len(groups)
        lead = (N_DEV - 1,) if scatter else (N_DEV,)
        shapes = [lead + (a.shape[1:] if scatter else a.shape) for a in srcs]
        lands = [lax.empty(sh, a.dtype) for sh, a in zip(shapes, srcs)]
        offsets = [sum(sizes[:gi]) for gi in range(ng)]
        copy = self._copy

        def body(*refs):
            src, land = refs[:n], refs[n:2 * n]
            sems = refs[2 * n + len(after):2 * n + len(after) + 2 * ng]
            token = refs[-1]
            for gi in range(ng):
                for wi in range(sizes[gi]):
                    w = offsets[gi] + wi
                    for k in range(1, N_DEV):
                        copy(src[w], land[w], sems[2 * gi], sems[2 * gi + 1], wi, k).start()
            token[...] = jnp.zeros_like(token)

        sem_shapes = [pltpu.SemaphoreType.DMA(((N_DEV - 1) * sz,)) for sz in sizes for _ in range(2)]
        outs = pl.pallas_call(
            body, name=name + "_start",
            in_specs=[HBM_SPEC] * (2 * n) + [ANY_SPEC] * len(after),
            out_specs=[SEM_SPEC] * (2 * ng) + [HBM_SPEC] * (2 * n) + [pl.BlockSpec(memory_space=pltpu.VMEM)],
            out_shape=sem_shapes + [pltpu.HBM(a.shape, a.dtype) for a in srcs + lands]
            + [jax.ShapeDtypeStruct((8, LANES), F32)],
            input_output_aliases={i: 2 * ng + i for i in range(2 * n)},
            compiler_params=pltpu.CompilerParams(has_side_effects=EFFECT),
        )(*[_in_hbm(a) for a in srcs + lands], *after)
        self.sems = [outs[2 * gi:2 * gi + 2] for gi in range(ng)]
        thru = outs[2 * ng:2 * ng + 2 * n]
        self.srcs = [thru[offsets[gi]:offsets[gi] + sizes[gi]] for gi in range(ng)]
        self.lands = [thru[n + offsets[gi]:n + offsets[gi] + sizes[gi]] for gi in range(ng)]
        self.token = outs[-1]

    def _copy(self, src, land, send_sems, recv_sems, wi, k):
        to = _peer(k)
        if self.scatter:
            src_ref, dst_ref = src.at[_linear(to)], land.at[k - 1]
        else:
            src_ref, dst_ref = src, land.at[_linear(_peer(0))]
        return pltpu.make_async_remote_copy(
            src_ref=src_ref, dst_ref=dst_ref, send_sem=send_sems.at[(N_DEV - 1) * wi + k - 1],
            recv_sem=recv_sems.at[(N_DEV - 1) * wi + k - 1], device_id=to, device_id_type=MESH)

    def wait(self, gi, after):
        n = self.sizes[gi]
        copy = self._copy

        def body(*refs):
            src, land = refs[:n], refs[n:2 * n]
            send_sems, recv_sems = refs[2 * n], refs[2 * n + 1]
            for wi in range(n):
                for k in range(1, N_DEV):
                    cp = copy(src[wi], land[wi], send_sems, recv_sems, wi, k)
                    cp.wait_send()
                    cp.wait_recv()

        arrays = list(self.srcs[gi]) + list(self.lands[gi])
        outs = pl.pallas_call(
            body, name=f"{self.name}_wait{gi}",
            in_specs=[HBM_SPEC] * (2 * n) + [SEM_SPEC, SEM_SPEC] + [ANY_SPEC] * len(after),
            out_specs=[HBM_SPEC] * (2 * n),
            out_shape=[pltpu.HBM(a.shape, a.dtype) for a in arrays],
            input_output_aliases={i: i for i in range(2 * n)},
            compiler_params=pltpu.CompilerParams(has_side_effects=EFFECT),
        )(*arrays, *self.sems[gi], *after)
        return outs[:n], outs[n:]


def _rotary_tables(positions):
    rot_dim = HEAD_DIM // 4
    inv_freq = ROPE_THETA ** (-jnp.arange(0, rot_dim, 2, dtype=F32) / rot_dim)
    ang = positions.astype(F32)[:, None] * inv_freq
    cs = jnp.concatenate([jnp.cos(ang), jnp.sin(ang)], axis=1)
    dim = jnp.arange(LANES) % HEAD_DIM
    first, second = dim < ROT_SHIFT, (dim >= ROT_SHIFT) & (dim < rot_dim)
    src = jnp.arange(2 * ROT_SHIFT)[:, None]
    angle = (dim % ROT_SHIFT)[None, :]
    c = jnp.where((first | second)[None, :] & (src == angle), 1.0, 0.0)
    sa = jnp.where(second[None, :] & (src == angle + ROT_SHIFT), 1.0, 0.0)
    sb = jnp.where(first[None, :] & (src == angle + ROT_SHIFT), -1.0, 0.0)
    spread = jnp.concatenate([c, sa, sb], axis=1).astype(F32)
    base = jnp.concatenate([jnp.where(first | second, 0.0, 1.0), jnp.zeros((2 * LANES,))]).astype(F32)[None, :]
    return jnp.dot(cs, spread, precision=lax.Precision.HIGHEST, preferred_element_type=F32) + base


def _block_diag(pool_w):
    gc = pool_w.shape[-1]
    out = jnp.zeros((POOL_WIDTH, POOL_WIDTH), pool_w.dtype)
    for grp in range(pool_w.shape[0]):
        out = lax.dynamic_update_slice(out, pool_w[grp], (grp * gc, grp * gc))
    return out


def _diag_blocks(a):
    gc = POOL_WIDTH // len(POOL_WINDOWS)
    return jnp.stack([a[grp * gc:(grp + 1) * gc, grp * gc:(grp + 1) * gc] for grp in range(len(POOL_WINDOWS))])


def _local_step(x, p, positions, loss_target, norm1, pool_w, pool_scale, norm2, norm3, final_norm, weights, send):
    rc = rsa = rsb = _rotary_tables(positions)
    ones_bd = _block_diag(jnp.ones((4, HEAD_DIM, HEAD_DIM), BF16))
    saved = []
    h = x
    for i in range(2):
        tag = f"_l{i}"
        g1, g2, g3 = norm1[i:i + 1], norm2[i:i + 1], norm3[i:i + 1]
        w_bd = _block_diag(pool_w[i]).astype(BF16)
        scale = pool_scale[i:i + 1]
        if i == 0:
            w_in = weights(i, "in", (h, rc, w_bd))
            hn1, u, *qkv = _normproj_fwd(h, g1, w_in, rc, rsa, rsb, "normproj_fwd" + tag)
        else:
            w_in, (hn1, u, *qkv) = ahead
        qkv = [qkv[3 * grp:3 * grp + 3] for grp in range(3)]
        started = weights(i, "prefetch", (hn1,))
        o, lse = zip(*[_attn_fwd(*qkv[grp], f"attn_fwd{tag}_g{grp}", after=started) for grp in range(3)])
        w_out = weights(i, "out", o)
        h1, a, y = _outproj_fwd(h, u, w_bd, scale, o, lse, w_out, "outproj_fwd" + tag)
        w_up, w_down = weights(i, "mlp", (h1,))
        h2, hn2, r = _mlp_fwd(h1, g2, w_up, w_down, "mlp_fwd" + tag)
        w_gate, w_ple = weights(i, "gate", (h2,))
        h0 = h
        if i == 0:
            w_in_next = weights(1, "in", (h2,))
            h, hn3, gate, pb, *ahead = _gate_fwd(h2, g3, w_gate, p, i, w_ple, "gate_normproj_fwd",
                                                 follow=(norm1[1:2], w_in_next, rc, rsa, rsb))
            ahead = (w_in_next, ahead)
        else:
            hn3 = gate = pb = None
            loss, d_final, *top = _gate_fwd(h2, g3, w_gate, p, i, w_ple, "gate_loss_gate_bwd",
                                            head=(final_norm.reshape(1, D_MODEL), loss_target))
        saved.append(dict(h0=h0, hn1=hn1, qkv=qkv, y=y, o=o, lse=lse, a=a, h1=h1, hn2=hn2, r=r, h2=h2,
                          hn3=hn3, gate=gate, pb=pb, w_bd=w_bd, scale=scale, g1=g1, g2=g2, g3=g3,
                          w_in=w_in, w_out=w_out, w_up=w_up, w_down=w_down, w_gate=w_gate, w_ple=w_ple))

    grads = [None, None]
    sent = ()
    for i in (1, 0):
        tag = f"_l{i}"
        sv = saved[i]
        if i == 1:
            dh2, dg3, dw_gate, dw_ple = top
        else:
            dh2, dg3, dw_gate, dw_ple = _gate_bwd(dh, sv["gate"], sv["pb"], sv["w_ple"], sv["h2"], sv["g3"],
                                                  sv["w_gate"], sv["hn3"], "gate_bwd" + tag, after=sent)
        dh1, dup, dg2, dh2b = _mlp_bwd(dh2, sv["r"], sv["h1"], sv["g2"], sv["w_up"], sv["w_down"], "mlp_bwd" + tag)
        dw_down = _matmul_tn(sv["r"], dh2b, "dw_down" + tag, square_a=True)
        dw_up = _matmul_tn(sv["hn2"], dup, "dw_up" + tag, blocked_out=True)
        dpool, do0, do1, do2, de0, de1, de2, dw_out = _outproj_bwd(dh1, sv["w_out"], sv["o"], sv["lse"], ones_bd,
                                                                   sv["a"], "outproj_bwd" + tag)
        sent = send(i, "main", dict(w_gate=dw_gate, w_ple=dw_ple, w_down=dw_down, w_up=dw_up, w_out=dw_out))
        dqkv = [_attn_bwd(*sv["qkv"][grp], do_g, sv["lse"][grp], de_g, f"attn_bwd{tag}_g{grp}", after=sent)
                for grp, (do_g, de_g) in enumerate(((do0, de0), (do1, de1), (do2, de2)))]
        dq, dk, dv = zip(*dqkv)
        dh, dz, dg1, dw_bd, dscale = _normproj_bwd(dh1, dpool, sv["y"], sv["w_bd"], sv["scale"], dq, dk, dv, rc, rsa, rsb,
                                                   sv["w_in"], sv["h0"], sv["g1"], "normproj_bwd" + tag)
        grads[i] = dict(norm1=dg1, norm2=dg2, norm3=dg3, pool_w=_diag_blocks(dw_bd), pool_scale=dscale)
        small_sent = send(0, "small", (grads, d_final, loss)) if i == 0 else ()
        dw_in = _matmul_tn(dz, sv["hn1"], "dw_in" + tag, tm=N_IN // 2, after=small_sent)
        sent = send(i, "in", dict(w_in=dw_in))
    return dh, sent


def _pack_small(norm1, norm2, norm3, final_norm, pool_scale, pool_w, spare=None):
    spare = jnp.zeros((1, LANES), F32) if spare is None else spare
    scale_row = jnp.concatenate([pool_scale.reshape(1, 2 * POOL_WIDTH), spare,
                                 jnp.zeros((1, D_MODEL - 2 * POOL_WIDTH - LANES), F32)], axis=1)
    return jnp.concatenate([norm1, norm2, norm3, final_norm.reshape(1, D_MODEL), scale_row,
                            pool_w.reshape(32, D_MODEL)], axis=0)


def _chunks_cols(a, cols):
    return a.reshape(a.shape[0], N_DEV, cols).transpose(1, 0, 2)


def _chunks_rows(a, rows):
    return a.reshape(N_DEV, rows, a.shape[1])


BIG = ("w_in", "w_out", "w_up", "w_down", "w_gate", "w_ple")
SMALL = ("norm1", "norm2", "norm3", "final_norm", "pool_scale", "pool_w")
ORDER = ("norm1", "w_in", "pool_w", "pool_scale", "w_out", "norm2", "w_up", "w_down", "norm3", "w_gate", "w_ple",
         "final_norm")


def kernel(x, p, positions, norm1, w_in, pool_w, pool_scale, w_out, norm2, w_up, w_down, norm3, w_gate, w_ple, final_norm, loss_target, m_norm1, m_w_in, m_pool_w, m_pool_scale, m_w_out, m_norm2, m_w_up, m_w_down, m_norm3, m_w_gate, m_w_ple, m_final_norm, v_norm1, v_w_in, v_pool_w, v_pool_scale, v_w_out, v_norm2, v_w_up, v_w_down, v_norm3, v_w_gate, v_w_ple, v_final_norm):
    w = dict(norm1=norm1, w_in=w_in, pool_w=pool_w, pool_scale=pool_scale, w_out=w_out, norm2=norm2, w_up=w_up,
             w_down=w_down, norm3=norm3, w_gate=w_gate, w_ple=w_ple, final_norm=final_norm)
    m = dict(norm1=m_norm1, w_in=m_w_in, pool_w=m_pool_w, pool_scale=m_pool_scale, w_out=m_w_out, norm2=m_norm2,
             w_up=m_w_up, w_down=m_w_down, norm3=m_norm3, w_gate=m_w_gate, w_ple=m_w_ple, final_norm=m_final_norm)
    v = dict(norm1=v_norm1, w_in=v_w_in, pool_w=v_pool_w, pool_scale=v_pool_scale, w_out=v_w_out, norm2=v_norm2,
             w_up=v_w_up, w_down=v_w_down, norm3=v_norm3, w_gate=v_w_gate, w_ple=v_w_ple, final_norm=v_final_norm)
    seq = x.shape[1]

    bf = {n: [w[n][layer].astype(BF16) for layer in range(2)] for n in BIG}
    bf["w_in"] = [a.T for a in bf["w_in"]]
    me = 4 * lax.axis_index("x") + 2 * lax.axis_index("y") + lax.axis_index("c")
    parts = dict(zip(("in", "out", "mlp", "gate"), (("w_in",), ("w_out",), ("w_up", "w_down"), ("w_gate", "w_ple"))))
    first = _Exchange("gather_first", [[bf["w_in"][0]]], scatter=False)
    later = [pt for pt in parts if pt != "in"]
    gathers = [_Exchange("gather_l0", [[bf[n][0] for n in parts[pt]] for pt in later], scatter=False,
                         after=(first.token,))]
    unpack = dict(w_in=lambda a: a.reshape(N_IN, D_MODEL),
                  w_out=lambda a: a.reshape(D_MODEL, D_MODEL), w_gate=lambda a: a.reshape(D_MODEL, D_MODEL),
                  w_ple=lambda a: a.transpose(1, 0, 2).reshape(PLE_DIM, D_MODEL), w_up=lambda a: a, w_down=lambda a: a)

    def weights(layer, part, after):
        if part == "prefetch":
            if layer != 0:
                return ()
            gathers.append(_Exchange("gather_l1", [[bf[n][1] for n in parts[pt]] for pt in parts], scatter=False,
                                     after=after))
            return (gathers[1].token,)
        if layer == 0 and part == "in":
            shards, lands = first.wait(0, (*after, gathers[0].token))
        elif layer == 0:
            shards, lands = gathers[0].wait(later.index(part), after)
        else:
            shards, lands = gathers[1].wait(tuple(parts).index(part), after)
        full = [unpack[n](lax.dynamic_update_slice_in_dim(land, shard[None], me, axis=0))
                for n, shard, land in zip(parts[part], shards, lands)]
        return full if len(full) > 1 else full[0]

    to_chunks = dict(w_in=lambda a: _chunks_rows(a, N_IN // N_DEV),
                     w_out=lambda a: _chunks_rows(a, D_MODEL // N_DEV),
                     w_up=lambda a: a, w_down=lambda a: _chunks_rows(a, FF_BLOCK),
                     w_gate=lambda a: _chunks_rows(a, D_MODEL // N_DEV), w_ple=lambda a: _chunks_cols(a, D_MODEL // N_DEV))
    own = {n: [None, None] for n in BIG}
    scatters = {}

    def send(layer, part, grads):
        if part == "small":
            per_layer, d_final, loss = grads
            pack = _pack_small(
                *[jnp.concatenate([per_layer[0][n], per_layer[1][n]], axis=0) for n in ("norm1", "norm2", "norm3")],
                d_final.reshape(D_MODEL),
                jnp.concatenate([per_layer[0]["pool_scale"], per_layer[1]["pool_scale"]], axis=0),
                jnp.stack([per_layer[0]["pool_w"], per_layer[1]["pool_w"]]), spare=loss)
            scatters["small"] = _Exchange("gather_small", [[pack]], scatter=False)
            return (scatters["small"].token,)
        for n, (g32, _) in grads.items():
            own[n][layer] = to_chunks[n](g32)
        ex = _Exchange(f"scatter_{part}_l{layer}", [[to_chunks[n](g16) for n, (_, g16) in grads.items()]], scatter=True)
        scatters[layer, part] = (tuple(grads), ex)
        return (ex.token,)

    _, pool_w_late = lax.optimization_barrier((first.token, pool_w))
    dx, sent = _local_step(
        x.reshape(seq, D_MODEL), p.reshape(2, seq, PLE_DIM), positions.reshape(seq), loss_target.reshape(seq, D_MODEL),
        norm1, pool_w_late, pool_scale, norm2, norm3, final_norm, weights, send)

    g_out, d_out, m_out, v_out = {}, {}, {}, {}
    my_index = me.reshape(1)
    for part in ("main", "in"):
        recv = {}
        for layer in (1, 0):
            names, ex = scatters[layer, part]
            for n, r in zip(names, ex.wait(0, sent)[1]):
                recv[n, layer] = r
        sent = ()
        for n in names:
            grad = (*own[n], recv[n, 0], recv[n, 1])
            turn = (lambda a: a.transpose(0, 2, 1)) if n == "w_in" else (lambda a: a)
            updated = _adamw_sharded(turn(w[n]), turn(m[n]), turn(v[n]), grad, my_index, "adamw_" + n)
            g_out[n], d_out[n], m_out[n], v_out[n] = map(turn, updated)
            sent += (updated[1],)
    (mine,), (landed,) = scatters["small"].wait(0, sent)
    small_g8 = lax.dynamic_update_slice_in_dim(landed, mine[None], me, axis=0)
    *small, spare = _adamw_small(w, small_g8, m, v, "adamw_small")
    for dst, a in zip((g_out, d_out, m_out, v_out), small):
        dst.update(a)

    return (spare[0, 0], dx.reshape(1, seq, D_MODEL), *[g_out[n] for n in ORDER], *[d_out[n] for n in ORDER],
            *[m_out[n] for n in ORDER], *[v_out[n] for n in ORDER])
```

```python
import functools

import jax
import jax.numpy as jnp
from jax import lax
from jax.experimental import pallas as pl
from jax.experimental.pallas import tpu as pltpu

F32 = jnp.float32
BF16 = jnp.bfloat16

D_MODEL = 1024
HEAD_DIM = 64
POOL_WIDTH = 256
POOL_WINDOWS = (2, 4, 8, 16)
POOL_HALO = 16
POOL_PAD = 8
GROUP_WIDTH = 256
DILATIONS = (1, 4, 16)
ATTN_BLOCK = 128
ROT_SHIFT = 8
ROPE_THETA = 500000.0
D_FF = 4096
FF_BLOCK = 512
FF_PER_STEP = 2
MLP_BWD_TILE = 512
FWD_TILE = 1024
N_DEV = 8
N_IN = POOL_WIDTH + 3 * 768
PLE_DIM = 256
EPS = 1e-6
NEG_BIG = -1e30

ADAM_LR = 0.001
ADAM_B1 = 0.9
ADAM_B2 = 0.999
ADAM_EPS = 1e-08
ADAM_WD = 0.01
ADAM_STEP = 10

LANES = 128
SUBLANES = 8
VMEM_LIMIT = 56 * 1024 * 1024
MESH = pl.DeviceIdType.MESH


def _params(n_grid):
    return pltpu.CompilerParams(dimension_semantics=("arbitrary",) * n_grid, vmem_limit_bytes=VMEM_LIMIT)


def _dot(a, b):
    return jnp.dot(a, b, preferred_element_type=F32)


def _dot_nt(a, b):
    return lax.dot_general(a, b, (((1,), (1,)), ((), ())), preferred_element_type=F32)


def _dot_tn(a, b):
    return lax.dot_general(a, b, (((0,), (0,)), ((), ())), preferred_element_type=F32)


def _rms(x, g):
    rstd = lax.rsqrt(jnp.mean(x * x, axis=-1, keepdims=True) + EPS)
    n = x * rstd
    return n, rstd, n * g


def _rms_bwd(dy, n, rstd, g):
    dyn = dy * g
    dx = rstd * (dyn - n * jnp.mean(dyn * n, axis=-1, keepdims=True))
    return dx, jnp.sum(dy * n, axis=0, keepdims=True)


def _ordered_after(body, n_in, after):
    if not after:
        return body
    return lambda *refs: body(*refs[:n_in], *refs[n_in + len(after):])


def _resident(shape):
    return pl.BlockSpec(shape, lambda i: (0,) * len(shape), pipeline_mode=pl.Buffered(1))


def _row_tile(s, t):
    t = min(s, t)
    assert s % t == 0
    return t


def _rot(z, c, sa, sb):
    return z * c + pltpu.roll(z, ROT_SHIFT, 1) * sa + pltpu.roll(z, LANES - ROT_SHIFT, 1) * sb


def _table_specs(t):
    return [pl.BlockSpec((t, LANES), functools.partial(lambda i, k: (i, k), k=k)) for k in range(3)]


def _rot_t(dz, c, sa, sb):
    return dz * c + pltpu.roll(dz * sa, LANES - ROT_SHIFT, 1) + pltpu.roll(dz * sb, ROT_SHIFT, 1)


def _to_residues(value, stage, out_ref, dil):
    if dil == 1:
        out_ref[0] = value.astype(out_ref.dtype)
        return
    rows = value.shape[0] // dil
    for hf in range(GROUP_WIDTH // LANES):
        lanes = slice(hf * LANES, (hf + 1) * LANES)
        stage[hf][...] = value[:, lanes]
        for r in range(dil):
            out_ref[r, :, lanes] = stage[hf][pl.ds(r, rows, stride=dil), :].astype(out_ref.dtype)


def _from_residues(in_ref, stage, dil):
    if dil == 1:
        return in_ref[0].astype(F32)
    rows = in_ref.shape[1]
    for hf in range(GROUP_WIDTH // LANES):
        for r in range(dil):
            stage[hf][pl.ds(r, rows, stride=dil), :] = in_ref[r, :, hf * LANES:(hf + 1) * LANES].astype(F32)
    return jnp.concatenate([stage[0][...], stage[1][...]], axis=1)


def _residue_spec(dil, t):
    return pl.BlockSpec((dil, t // dil, GROUP_WIDTH), lambda i: (0, i, 0))


def _residue_shape(dil, s, dtype):
    return jax.ShapeDtypeStruct((dil, s // dil, GROUP_WIDTH), dtype)


def _stages(t, n):
    return [pltpu.VMEM((t, LANES), F32)] * (n * (GROUP_WIDTH // LANES))


def _pair_stages(refs):
    return [refs[i:i + 2] for i in range(0, len(refs), 2)]


def _normproj_tile(x, g_ref, w_ref, c_ref, sa_ref, sb_ref, hn_ref, u_ref, *rest):
    qkv_refs, stages = rest[:9], _pair_stages(rest[9:])
    _, _, hn = _rms(x, g_ref[...])
    hb = hn.astype(BF16)
    hn_ref[...] = hb
    c, sa, sb = c_ref[...], sa_ref[...], sb_ref[...]

    def rot(z, scale):
        halves = [_rot(z[:, hf * LANES:(hf + 1) * LANES], c, sa, sb) * scale for hf in range(2)]
        return jnp.concatenate(halves, axis=1)

    proj = lambda lo: _dot_nt(hb, w_ref[lo:lo + GROUP_WIDTH, :])
    u_ref[...] = proj(0)
    for grp, dil in enumerate(DILATIONS):
        lo = POOL_WIDTH + grp * GROUP_WIDTH
        q_ref, k_ref, v_ref = qkv_refs[3 * grp:3 * grp + 3]
        _to_residues(rot(proj(lo), HEAD_DIM ** -0.5), stages[0], q_ref, dil)
        _to_residues(rot(proj(lo + 768), 1.0), stages[1], k_ref, dil)
        _to_residues(proj(lo + 1536), stages[2], v_ref, dil)


def _normproj_operands(s, t):
    row = lambda w: pl.BlockSpec((t, w), lambda i: (i, 0))
    in_specs = [pl.BlockSpec((1, D_MODEL), lambda i: (0, 0)), _resident((N_IN, D_MODEL))] + _table_specs(t)
    out_specs = [row(D_MODEL), row(POOL_WIDTH)] + [_residue_spec(dil, t) for dil in DILATIONS for _ in range(3)]
    out_shape = [jax.ShapeDtypeStruct((s, D_MODEL), BF16), jax.ShapeDtypeStruct((s, POOL_WIDTH), F32)]
    out_shape += [_residue_shape(dil, s, BF16) for dil in DILATIONS for _ in range(3)]
    return in_specs, out_specs, out_shape, _stages(t, 3)


def _normproj_fwd(h, g, w_in, rc, rsa, rsb, name):
    s = h.shape[0]
    t = _row_tile(s, FWD_TILE)

    def body(h_ref, *refs):
        _normproj_tile(h_ref[...], *refs)

    in_specs, out_specs, out_shape, scratch = _normproj_operands(s, t)
    return pl.pallas_call(
        body, name=name, grid=(s // t,), in_specs=[pl.BlockSpec((t, D_MODEL), lambda i: (i, 0))] + in_specs,
        out_specs=out_specs, out_shape=out_shape, scratch_shapes=scratch, compiler_params=_params(1),
    )(h, g, w_in, rc, rsa, rsb)


def _pool_lane_window():
    lane = lax.broadcasted_iota(jnp.int32, (1, POOL_WIDTH), 1)
    return jnp.left_shift(2, lane // (POOL_WIDTH // len(POOL_WINDOWS)))


def _window_sums(ext, b2, b4, b8, t, lo, tile, direction):
    rows = t + POOL_HALO
    for src, dst, sh in ((ext, b2, 1), (b2, b4, 2), (b4, b8, 4)):
        dst[lo:lo + rows, :] = src[lo:lo + rows, :] + src[lo + direction * sh:lo + direction * sh + rows, :]
    s16 = b8[tile:tile + t, :] + b8[tile + direction * 8:tile + direction * 8 + t, :]
    win = _pool_lane_window()
    return jnp.where(win == 2, b2[tile:tile + t, :],
                     jnp.where(win == 4, b4[tile:tile + t, :], jnp.where(win == 8, b8[tile:tile + t, :], s16)))


def _pool_fwd_tile(i, u_ref, w_ref, sc_ref, y_ref, ext, b2, b4, b8):
    t = u_ref.shape[0]
    first = POOL_PAD + POOL_HALO

    @pl.when(i == 0)
    def _():
        for buf in (ext, b2, b4):
            buf[0:POOL_PAD, :] = jnp.zeros((POOL_PAD, POOL_WIDTH), F32)
        ext[POOL_PAD:first, :] = jnp.zeros((POOL_HALO, POOL_WIDTH), F32)

    x = u_ref[...]
    ext[first:, :] = x
    wsum = _window_sums(ext, b2, b4, b8, t, POOL_PAD, first, -1)
    pos = i * t + lax.broadcasted_iota(jnp.int32, (t, POOL_WIDTH), 0)
    cnt = jnp.minimum(pos + 1, _pool_lane_window()).astype(F32)
    yb = (wsum / cnt - x).astype(BF16)
    y_ref[...] = yb
    ext[POOL_PAD:first, :] = x[t - POOL_HALO:, :]
    return _dot(yb, w_ref[...]) * sc_ref[...]


def _head_masks():
    lane = lax.broadcasted_iota(jnp.int32, (ATTN_BLOCK, GROUP_WIDTH), 1)
    return [lane // HEAD_DIM == hd for hd in range(GROUP_WIDTH // HEAD_DIM)]


def _stack_heads(a, masks):
    zero = jnp.zeros_like(a)
    return jnp.concatenate([jnp.where(m, a, zero) for m in masks], axis=0)


def _band_bias(first_step):
    rows = ATTN_BLOCK * (GROUP_WIDTH // HEAD_DIM)
    i = lax.broadcasted_iota(jnp.int32, (rows, 2 * ATTN_BLOCK), 0) & (ATTN_BLOCK - 1)
    j = lax.broadcasted_iota(jnp.int32, (rows, 2 * ATTN_BLOCK), 1)
    inner = jnp.where((j >= i) & (j <= i + ATTN_BLOCK), 0.0, NEG_BIG)
    return jnp.where((j < ATTN_BLOCK) & first_step, NEG_BIG, inner), inner


def _column_per_head(a):
    return jnp.concatenate([a[:, hd * HEAD_DIM:hd * HEAD_DIM + 1] for hd in range(GROUP_WIDTH // HEAD_DIM)], axis=0)


def _blocks_per_step(nb):
    if nb <= 16:
        return nb
    return next(qb for qb in (16, 8, 4, 2, 1) if nb % qb == 0)


def _residues_per_step(dil, nb, qb):
    return 2 if (nb == qb and qb < 8 and dil % 2 == 0) else 1


def _attn_fwd(q, k, v, name, after=()):
    dil, length, _ = q.shape
    nb = length // ATTN_BLOCK
    qb = _blocks_per_step(nb)
    rs = _residues_per_step(dil, nb, qb)

    def body(q_ref, kp_ref, kc_ref, vp_ref, vc_ref, o_ref, lse_ref):
        masks = _head_masks()
        bias = _band_bias(pl.program_id(1) == 0)
        for rr in range(rs):
            for qi in range(qb):
                here = slice(qi * ATTN_BLOCK, (qi + 1) * ATTN_BLOCK)
                before = slice((qi - 1) * ATTN_BLOCK, qi * ATTN_BLOCK)
                kcat = jnp.concatenate([kp_ref[rr] if qi == 0 else kc_ref[rr, before], kc_ref[rr, here]], axis=0)
                vcat = jnp.concatenate([vp_ref[rr] if qi == 0 else vc_ref[rr, before], vc_ref[rr, here]], axis=0)
                qs = _stack_heads(q_ref[rr, here], masks)
                sc = _dot_nt(qs, kcat) + bias[min(qi, 1)]
                m = jnp.max(sc, axis=1, keepdims=True)
                e = jnp.exp(sc - m)
                l = jnp.sum(e, axis=1, keepdims=True)
                p = (e / l).astype(BF16)
                lse = m + jnp.log(l)
                o = jnp.zeros((ATTN_BLOCK, GROUP_WIDTH), F32)
                lse_full = jnp.zeros((ATTN_BLOCK, GROUP_WIDTH), F32)
                for hd, msk in enumerate(masks):
                    rows = slice(hd * ATTN_BLOCK, (hd + 1) * ATTN_BLOCK)
                    o = jnp.where(msk, _dot(p[rows], vcat), o)
                    lse_full = jnp.where(msk, lse[rows], lse_full)
                o_ref[rr, here] = o.astype(o_ref.dtype)
                lse_ref[rr, here] = lse_full

    cur = pl.BlockSpec((rs, qb * ATTN_BLOCK, GROUP_WIDTH), lambda r, j: (r, j, 0))
    prev = pl.BlockSpec((rs, ATTN_BLOCK, GROUP_WIDTH), lambda r, j: (r, jnp.maximum(qb * j - 1, 0), 0))
    return pl.pallas_call(
        _ordered_after(body, 5, after), name=name, grid=(dil // rs, nb // qb),
        in_specs=[cur, prev, cur, prev, cur] + [pl.BlockSpec(memory_space=pl.ANY)] * len(after), out_specs=[cur, cur],
        out_shape=[jax.ShapeDtypeStruct(q.shape, BF16), jax.ShapeDtypeStruct(q.shape, F32)],
        compiler_params=_params(2),
    )(q, k, k, v, v, *after)


def _group_weights(l0, l1, l2):
    m = jnp.maximum(jnp.maximum(l0, l1), l2)
    e0, e1, e2 = jnp.exp(l0 - m), jnp.exp(l1 - m), jnp.exp(l2 - m)
    den = e0 + e1 + e2
    return e0 / den, e1 / den, e2 / den


def _outproj_fwd(h, u, w_bd, scale, o, lse, w_out, name):
    s = h.shape[0]
    t = _row_tile(s, FWD_TILE)

    def body(h_ref, u_ref, wbd_ref, sc_ref, o0, o1, o2, l0, l1, l2, w_ref, out_ref, a_ref, y_ref, ext, b2, b4, b8,
             *stages):
        pool_out = _pool_fwd_tile(pl.program_id(0), u_ref, wbd_ref, sc_ref, y_ref, ext, b2, b4, b8)
        stages = _pair_stages(stages)
        ov = [_from_residues(r, stages[i], DILATIONS[i]) for i, r in enumerate((o0, o1, o2))]
        lv = [_from_residues(r, stages[3 + i], DILATIONS[i]) for i, r in enumerate((l0, l1, l2))]
        wts = _group_weights(*lv)
        a = jnp.concatenate([pool_out] + [ov[i] * wts[i] for i in range(3)], axis=1).astype(BF16)
        a_ref[...] = a
        out_ref[...] = h_ref[...] + _dot(a, w_ref[...])

    row = lambda w: pl.BlockSpec((t, w), lambda i: (i, 0))
    res = [_residue_spec(dil, t) for dil in DILATIONS]
    return pl.pallas_call(
        body, name=name, grid=(s // t,),
        in_specs=[row(D_MODEL), row(POOL_WIDTH), _resident((POOL_WIDTH, POOL_WIDTH)), _resident((1, POOL_WIDTH))]
        + res + res + [_resident((D_MODEL, D_MODEL))],
        out_specs=[row(D_MODEL), row(D_MODEL), row(POOL_WIDTH)],
        out_shape=[jax.ShapeDtypeStruct((s, D_MODEL), F32), jax.ShapeDtypeStruct((s, D_MODEL), BF16),
                   jax.ShapeDtypeStruct((s, POOL_WIDTH), BF16)],
        scratch_shapes=[pltpu.VMEM((t + POOL_HALO + POOL_PAD, POOL_WIDTH), F32)] * 4 + _stages(t, 6),
        compiler_params=_params(1),
    )(h, u, w_bd, scale, *o, *lse, w_out)


def _mlp_fwd(h, g, w_up, w_down, name):
    s = h.shape[0]
    t = _row_tile(s, 512)
    nblk = D_FF // FF_BLOCK

    def body(h_ref, g_ref, wu_ref, wd_ref, out_ref, hn_ref, r_ref):
        x = h_ref[...]
        _, _, hn = _rms(x, g_ref[...])
        hb = hn.astype(BF16)
        hn_ref[...] = hb
        acc = None
        for b0 in range(0, nblk, FF_PER_STEP):
            acts = []
            for b in range(b0, b0 + FF_PER_STEP):
                r = jnp.maximum(_dot(hb, wu_ref[b]), 0.0)
                r_ref[:, b * FF_BLOCK:(b + 1) * FF_BLOCK] = r.astype(BF16)
                acts.append((r * r).astype(BF16))
            wd = wd_ref[b0:b0 + FF_PER_STEP].reshape(FF_PER_STEP * FF_BLOCK, D_MODEL)
            part = _dot(jnp.concatenate(acts, axis=1), wd)
            acc = part if acc is None else acc + part
        out_ref[...] = x + acc

    row = lambda w: pl.BlockSpec((t, w), lambda i: (i, 0))
    resident = lambda shape: pl.BlockSpec(shape, lambda i: (0, 0, 0), pipeline_mode=pl.Buffered(1))
    return pl.pallas_call(
        body, name=name, grid=(s // t,),
        in_specs=[row(D_MODEL), pl.BlockSpec((1, D_MODEL), lambda i: (0, 0)),
                  resident((nblk, D_MODEL, FF_BLOCK)), resident((nblk, FF_BLOCK, D_MODEL))],
        out_specs=[row(D_MODEL), row(D_MODEL), row(D_FF)],
        out_shape=[jax.ShapeDtypeStruct((s, D_MODEL), F32), jax.ShapeDtypeStruct((s, D_MODEL), BF16),
                   jax.ShapeDtypeStruct((s, D_FF), BF16)],
        compiler_params=_params(1),
    )(h, g, w_up, w_down)


def _gate_fwd(h, g, w_gate, p, layer, w_ple, name, head=None, follow=None):
    assert (head is None) != (follow is None)
    s = h.shape[0]
    t = _row_tile(s, 512)
    last = s // t - 1

    def body(h_ref, g_ref, wg_ref, p_ref, wp_ref, *refs):
        x = h_ref[...]
        gv = g_ref[...]
        n, rstd, hn = _rms(x, gv)
        hb = hn.astype(BF16)
        gate = 1.0 / (1.0 + jnp.exp(-_dot(hb, wg_ref[...])))
        pb = p_ref[...].astype(BF16)
        e = _dot(pb, wp_ref[...])
        h3 = x + gate * e
        if follow is not None:
            out_ref, hn_ref, gate_ref, pb_ref = refs[5:9]
            out_ref[...] = h3
            hn_ref[...] = hb
            pb_ref[...] = pb
            gate_ref[...] = gate.astype(BF16)
            _normproj_tile(h3, *refs[:5], *refs[9:])
            return
        gf_ref, t_ref, loss_ref, dgf_ref, out_ref, dg_ref, dwg_ref, dwgb_ref, dwp_ref, dwpb_ref = refs
        i = pl.program_id(0)

        @pl.when(i == 0)
        def _():
            for ref in (loss_ref, dgf_ref, dg_ref, dwg_ref, dwp_ref):
                ref[...] = jnp.zeros_like(ref)

        gf = gf_ref[...]
        n3, rstd3, y = _rms(h3, gf)
        err = y - t_ref[...]
        loss_ref[...] += jnp.sum(err * err) * (0.5 / D_MODEL)
        d, dgf = _rms_bwd(err * (1.0 / D_MODEL), n3, rstd3, gf)
        dgf_ref[...] += dgf
        dgl = (d * e * gate * (1.0 - gate)).astype(BF16)
        dwg_ref[...] += _dot_tn(hb, dgl)
        dwp_ref[...] += _dot_tn(pb, (d * gate).astype(BF16))
        dx, dg = _rms_bwd(_dot_nt(dgl, wg_ref[...]), n, rstd, gv)
        out_ref[...] = d + dx
        dg_ref[...] += dg

        @pl.when(i == last)
        def _():
            dwgb_ref[...] = dwg_ref[...].astype(BF16)
            dwpb_ref[...] = dwp_ref[...].astype(BF16)

    row = lambda w: pl.BlockSpec((t, w), lambda i: (i, 0))
    full = lambda a, b: pl.BlockSpec((a, b), lambda i: (0, 0))
    in_specs = [row(D_MODEL), full(1, D_MODEL), _resident((D_MODEL, D_MODEL)),
                pl.BlockSpec((None, t, PLE_DIM), lambda i: (layer, i, 0)), _resident((PLE_DIM, D_MODEL))]
    if follow is not None:
        next_in, next_out, next_shape, scratch = _normproj_operands(s, t)
        return pl.pallas_call(
            body, name=name, grid=(s // t,), in_specs=in_specs + next_in,
            out_specs=[row(D_MODEL), row(D_MODEL), row(D_MODEL), row(PLE_DIM)] + next_out,
            out_shape=[jax.ShapeDtypeStruct((s, D_MODEL), F32), jax.ShapeDtypeStruct((s, D_MODEL), BF16),
                       jax.ShapeDtypeStruct((s, D_MODEL), BF16), jax.ShapeDtypeStruct((s, PLE_DIM), BF16)] + next_shape,
            scratch_shapes=scratch, compiler_params=_params(1),
        )(h, g, w_gate, p, w_ple, *follow)
    loss, dgf, dh2, dg, dwg, dwgb, dwp, dwpb = pl.pallas_call(
        body, name=name, grid=(s // t,), in_specs=in_specs + [full(1, D_MODEL), row(D_MODEL)],
        out_specs=[pl.BlockSpec((1, LANES), lambda i: (0, 0)), full(1, D_MODEL), row(D_MODEL), full(1, D_MODEL),
                   full(D_MODEL, D_MODEL), full(D_MODEL, D_MODEL), full(PLE_DIM, D_MODEL), full(PLE_DIM, D_MODEL)],
        out_shape=[jax.ShapeDtypeStruct((1, LANES), F32), jax.ShapeDtypeStruct((1, D_MODEL), F32),
                   jax.ShapeDtypeStruct((s, D_MODEL), F32), jax.ShapeDtypeStruct((1, D_MODEL), F32),
                   jax.ShapeDtypeStruct((D_MODEL, D_MODEL), F32), jax.ShapeDtypeStruct((D_MODEL, D_MODEL), BF16),
                   jax.ShapeDtypeStruct((PLE_DIM, D_MODEL), F32), jax.ShapeDtypeStruct((PLE_DIM, D_MODEL), BF16)],
        compiler_params=_params(1),
    )(h, g, w_gate, p, w_ple, *head)
    return loss, dgf, dh2, dg, (dwg, dwgb), (dwp, dwpb)


def _gate_bwd(dh, gate, pb, w_ple, h, g, w_gate, hn, name, after=()):
    s = h.shape[0]
    t = _row_tile(s, FWD_TILE)
    last = s // t - 1

    def body(dh_ref, gate_ref, pb_ref, wp_ref, h_ref, g_ref, wg_ref, hn_ref, out_ref, dg_ref, dwg_ref, dwgb_ref,
             dwp_ref, dwpb_ref):
        i = pl.program_id(0)

        @pl.when(i == 0)
        def _():
            dg_ref[...] = jnp.zeros_like(dg_ref)
            dwg_ref[...] = jnp.zeros_like(dwg_ref)
            dwp_ref[...] = jnp.zeros_like(dwp_ref)

        d = dh_ref[...]
        gate = gate_ref[...].astype(F32)
        pb = pb_ref[...]
        e = _dot(pb, wp_ref[...])
        dgl = (d * e * gate * (1.0 - gate)).astype(BF16)
        dwg_ref[...] += _dot_tn(hn_ref[...], dgl)
        dwp_ref[...] += _dot_tn(pb, (d * gate).astype(BF16))
        gv = g_ref[...]
        n, rstd, _ = _rms(h_ref[...], gv)
        dx, dg = _rms_bwd(_dot_nt(dgl, wg_ref[...]), n, rstd, gv)
        out_ref[...] = d + dx
        dg_ref[...] += dg

        @pl.when(i == last)
        def _():
            dwgb_ref[...] = dwg_ref[...].astype(BF16)
            dwpb_ref[...] = dwp_ref[...].astype(BF16)

    row = lambda w: pl.BlockSpec((t, w), lambda i: (i, 0))
    full = lambda a, b: pl.BlockSpec((a, b), lambda i: (0, 0))
    dh2, dg, dwg, dwgb, dwp, dwpb = pl.pallas_call(
        _ordered_after(body, 8, after), name=name, grid=(s // t,),
        in_specs=[row(D_MODEL), row(D_MODEL), row(PLE_DIM), _resident((PLE_DIM, D_MODEL)), row(D_MODEL),
                  full(1, D_MODEL), _resident((D_MODEL, D_MODEL)), row(D_MODEL)]
        + [pl.BlockSpec(memory_space=pl.ANY)] * len(after),
        out_specs=[row(D_MODEL), full(1, D_MODEL), full(D_MODEL, D_MODEL), full(D_MODEL, D_MODEL),
                   full(PLE_DIM, D_MODEL), full(PLE_DIM, D_MODEL)],
        out_shape=[jax.ShapeDtypeStruct((s, D_MODEL), F32), jax.ShapeDtypeStruct((1, D_MODEL), F32),
                   jax.ShapeDtypeStruct((D_MODEL, D_MODEL), F32), jax.ShapeDtypeStruct((D_MODEL, D_MODEL), BF16),
                   jax.ShapeDtypeStruct((PLE_DIM, D_MODEL), F32), jax.ShapeDtypeStruct((PLE_DIM, D_MODEL), BF16)],
        compiler_params=_params(1),
    )(dh, gate, pb, w_ple, h, g, w_gate, hn, *after)
    return dh2, dg, (dwg, dwgb), (dwp, dwpb)


def _mlp_bwd(dh, r, h, g, w_up, w_down, name):
    s = h.shape[0]
    t = _row_tile(s, MLP_BWD_TILE)
    nblk = D_FF // FF_BLOCK

    def body(dh_ref, r_ref, h_ref, g_ref, wu_ref, wd_ref, out_ref, dup_ref, dg_ref, dhb_ref):
        @pl.when(pl.program_id(0) == 0)
        def _():
            dg_ref[...] = jnp.zeros_like(dg_ref)

        d = dh_ref[...]
        db = d.astype(BF16)
        dhb_ref[...] = db
        back = None
        for b in range(nblk):
            cols = slice(b * FF_BLOCK, (b + 1) * FF_BLOCK)
            dup = (_dot_nt(db, wd_ref[b]) * (2.0 * r_ref[:, cols].astype(F32))).astype(BF16)
            dup_ref[:, cols] = dup
            part = _dot_nt(dup, wu_ref[b])
            back = part if back is None else back + part
        gv = g_ref[...]
        n, rstd, _ = _rms(h_ref[...], gv)
        dx, dg = _rms_bwd(back, n, rstd, gv)
        out_ref[...] = d + dx
        dg_ref[...] += dg

    row = lambda w: pl.BlockSpec((t, w), lambda i: (i, 0))
    vec = pl.BlockSpec((1, D_MODEL), lambda i: (0, 0))
    resident = lambda shape: pl.BlockSpec(shape, lambda i: (0, 0, 0), pipeline_mode=pl.Buffered(1))
    return pl.pallas_call(
        body, name=name, grid=(s // t,),
        in_specs=[row(D_MODEL), row(D_FF), row(D_MODEL), vec,
                  resident((nblk, D_MODEL, FF_BLOCK)), resident((nblk, FF_BLOCK, D_MODEL))],
        out_specs=[row(D_MODEL), row(D_FF), vec, row(D_MODEL)],
        out_shape=[jax.ShapeDtypeStruct((s, D_MODEL), F32), jax.ShapeDtypeStruct((s, D_FF), BF16),
                   jax.ShapeDtypeStruct((1, D_MODEL), F32), jax.ShapeDtypeStruct((s, D_MODEL), BF16)],
        compiler_params=_params(1),
    )(dh, r, h, g, w_up, w_down)


def _outproj_bwd(dh, w_out, o, lse, ones_bd, a, name):
    s = dh.shape[0]
    t = _row_tile(s, 512)
    last = s // t - 1

    def body(dh_ref, w_ref, o0, o1, o2, l0, l1, l2, bd_ref, a_ref, dp_ref, do0, do1, do2, de0, de1, de2, dw_ref,
             dwb_ref, *stages):
        i = pl.program_id(0)

        @pl.when(i == 0)
        def _():
            dw_ref[...] = jnp.zeros_like(dw_ref)

        stages = _pair_stages(stages)
        dhb = dh_ref[...].astype(BF16)
        dw_ref[...] += _dot_tn(a_ref[...], dhb)

        @pl.when(i == last)
        def _():
            dwb_ref[...] = dw_ref[...].astype(BF16)

        da = _dot_nt(dhb, w_ref[...])
        dp_ref[...] = da[:, 0:POOL_WIDTH]
        ov =[_from_residues(r, stages[i], DILATIONS[i]) for i, r in enumerate((o0, o1, o2))]
        lv = [_from_residues(r, stages[3 + i], DILATIONS[i]) for i, r in enumerate((l0, l1, l2))]
        wts = _group_weights(*lv)
        bd = bd_ref[...]
        cbar = jnp.zeros((t, GROUP_WIDTH), F32)
        for grp, do_ref in enumerate((do0, do1, do2)):
            lo = POOL_WIDTH + grp * GROUP_WIDTH
            dag = da[:, lo:lo + GROUP_WIDTH]
            _to_residues(dag * wts[grp], stages[6 + grp], do_ref, DILATIONS[grp])
            prod = dag * ov[grp]
            hi = prod.astype(BF16)
            low = (prod - hi.astype(F32)).astype(BF16)
            cbar = cbar + wts[grp] * (_dot(hi, bd) + _dot(low, bd))
        for grp, de_ref in enumerate((de0, de1, de2)):
            _to_residues(wts[grp] * cbar, stages[9 + grp], de_ref, DILATIONS[grp])

    row = lambda w: pl.BlockSpec((t, w), lambda i: (i, 0))
    full = lambda a, b: pl.BlockSpec((a, b), lambda i: (0, 0))
    res = [_residue_spec(dil, t) for dil in DILATIONS]
    *outs, dw, dwb = pl.pallas_call(
        body, name=name, grid=(s // t,),
        in_specs=[row(D_MODEL), full(D_MODEL, D_MODEL)] + res + res + [full(GROUP_WIDTH, GROUP_WIDTH), row(D_MODEL)],
        out_specs=[row(POOL_WIDTH)] + res + res + [full(D_MODEL, D_MODEL)] * 2,
        out_shape=[jax.ShapeDtypeStruct((s, POOL_WIDTH), F32)] + [_residue_shape(dil, s, BF16) for dil in DILATIONS]
        + [_residue_shape(dil, s, F32) for dil in DILATIONS]
        + [jax.ShapeDtypeStruct((D_MODEL, D_MODEL), F32), jax.ShapeDtypeStruct((D_MODEL, D_MODEL), BF16)],
        scratch_shapes=_stages(t, 12),
        compiler_params=_params(1),
    )(dh, w_out, *o, *lse, ones_bd, a)
    return (*outs, (dw, dwb))


def _attn_bwd(q, k, v, do, lse, deff, name, after=()):
    dil, length, _ = q.shape
    nb = length // ATTN_BLOCK
    qb = _blocks_per_step(nb)
    nj = nb // qb
    rs = _residues_per_step(dil, nb, qb)
    whole = nj == 1
    tail = slice((qb - 1) * ATTN_BLOCK, qb * ATTN_BLOCK)
    block = lambda qi: slice(qi * ATTN_BLOCK, (qi + 1) * ATTN_BLOCK)

    def body(q_ref, kp_ref, kc_ref, vp_ref, vc_ref, do_ref, lse_ref, de_ref, dq_ref, dk_ref, dv_ref, ck, cv):
        j = pl.program_id(1)

        def compute():
            masks = _head_masks()
            bias = _band_bias(j == 0)
            for rr in range(rs):
                dkc, dvc = [], []
                for qi in range(qb):
                    here, before = block(qi), block(qi - 1)
                    kcat = jnp.concatenate([kp_ref[rr] if qi == 0 else kc_ref[rr, before], kc_ref[rr, here]], axis=0)
                    vcat = jnp.concatenate([vp_ref[rr] if qi == 0 else vc_ref[rr, before], vc_ref[rr, here]], axis=0)
                    qs = _stack_heads(q_ref[rr, here], masks)
                    dos = _stack_heads(do_ref[rr, here], masks)
                    sc = _dot_nt(qs, kcat) + bias[min(qi, 1)]
                    p = jnp.exp(sc - _column_per_head(lse_ref[rr, here]))
                    ds = (p * (_dot_nt(dos, vcat) - _column_per_head(de_ref[rr, here]))).astype(BF16)
                    dq = jnp.zeros((ATTN_BLOCK, GROUP_WIDTH), F32)
                    for hd, msk in enumerate(masks):
                        dq = jnp.where(msk, _dot(ds[block(hd)], kcat), dq)
                    dq_ref[rr, here] = dq.astype(dq_ref.dtype)
                    dkc.append(_dot_tn(ds, qs))
                    dvc.append(_dot_tn(p.astype(BF16), dos))

                for out_ref, carry, parts in ((dk_ref, ck, dkc), (dv_ref, cv, dvc)):
                    full = [parts[qi][ATTN_BLOCK:] + parts[qi + 1][0:ATTN_BLOCK] for qi in range(qb - 1)]
                    if whole:
                        for qi, val in enumerate(full + [parts[qb - 1][ATTN_BLOCK:]]):
                            out_ref[rr, block(qi)] = val.astype(out_ref.dtype)
                        continue

                    @pl.when(j > 0)
                    def _():
                        if qb > 1:
                            out_ref[0, 0:(qb - 1) * ATTN_BLOCK] = carry[0:(qb - 1) * ATTN_BLOCK].astype(out_ref.dtype)
                        out_ref[0, tail] = (carry[tail] + parts[0][0:ATTN_BLOCK]).astype(out_ref.dtype)

                    for qi, val in enumerate(full):
                        carry[block(qi)] = val
                    carry[tail] = parts[qb - 1][ATTN_BLOCK:]

        if whole:
            compute()
        else:
            pl.when(j < nj)(compute)

            @pl.when(j == nj)
            def _():
                dk_ref[0] = ck[...].astype(dk_ref.dtype)
                dv_ref[0] = cv[...].astype(dv_ref.dtype)

    step = lambda j: jnp.minimum(j, nj - 1)
    cur = pl.BlockSpec((rs, qb * ATTN_BLOCK, GROUP_WIDTH), lambda r, j: (r, step(j), 0))
    prev = pl.BlockSpec((rs, ATTN_BLOCK, GROUP_WIDTH), lambda r, j: (r, jnp.maximum(qb * step(j) - 1, 0), 0))
    late = pl.BlockSpec((rs, qb * ATTN_BLOCK, GROUP_WIDTH), lambda r, j: (r, jnp.maximum(j - 1, 0), 0))
    return pl.pallas_call(
        _ordered_after(body, 8, after), name=name, grid=(dil // rs, 1 if whole else nj + 1),
        in_specs=[cur, prev, cur, prev, cur, cur, cur, cur] + [pl.BlockSpec(memory_space=pl.ANY)] * len(after),
        out_specs=[cur, cur if whole else late, cur if whole else late],
        out_shape=[jax.ShapeDtypeStruct(q.shape, BF16)] * 3,
        scratch_shapes=[pltpu.VMEM((qb * ATTN_BLOCK, GROUP_WIDTH), F32)] * 2,
        compiler_params=_params(2),
    )(q, k, k, v, v, do, lse, deff, *after)


def _pool_bwd_tile(i, nt, dp_ref, y_ref, w_ref, sc_ref, dw_ref, dsc_ref, ext, b2, b4, b8):
    t = dp_ref.shape[0]

    @pl.when(i == 0)
    def _():
        ext[t:, :] = jnp.zeros((POOL_HALO + POOL_PAD, POOL_WIDTH), F32)
        for buf in (b2, b4):
            buf[t + POOL_HALO:, :] = jnp.zeros((POOL_PAD, POOL_WIDTH), F32)
        dw_ref[...] = jnp.zeros_like(dw_ref)
        dsc_ref[...] = jnp.zeros_like(dsc_ref)

    dp = dp_ref[...]
    yb = y_ref[...]
    w = w_ref[...]
    dsc_ref[...] += jnp.sum(dp * _dot(yb, w), axis=0, keepdims=True)
    dyo = (dp * sc_ref[...]).astype(BF16)
    dw_ref[...] += _dot_tn(yb, dyo)
    dy = _dot_nt(dyo, w)
    pos = (nt - 1 - i) * t + lax.broadcasted_iota(jnp.int32, (t, POOL_WIDTH), 0)
    gq = dy / jnp.minimum(pos + 1, _pool_lane_window()).astype(F32)
    ext[0:t, :] = gq
    du = _window_sums(ext, b2, b4, b8, t, 0, 0, 1) - dy
    ext[t:t + POOL_HALO, :] = gq[0:POOL_HALO, :]
    return du


def _normproj_bwd(dh, dpool, y, w_bd, scale, dq, dk, dv, rc, rsa, rsb, w_in, h, g, name):
    s = h.shape[0]
    t = _row_tile(s, 512)
    nt = s // t

    def body(dh_ref, dp_ref, y_ref, wbd_ref, sc_ref, q0, q1, q2, k0, k1, k2, v0, v1, v2, c_ref, sa_ref, sb_ref, w_ref,
             h_ref, g_ref, out_ref, dz_ref, dg_ref, dwbd_ref, dsc_ref, ext, b2, b4, b8, *stages):
        step = pl.program_id(0)

        @pl.when(step == 0)
        def _():
            dg_ref[...] = jnp.zeros_like(dg_ref)

        du = _pool_bwd_tile(step, nt, dp_ref, y_ref, wbd_ref, sc_ref, dwbd_ref, dsc_ref, ext, b2, b4, b8)
        c, sa, sb = c_ref[...], sa_ref[...], sb_ref[...]

        def unrot(a, scale):
            halves = [_rot_t(a[:, hf * LANES:(hf + 1) * LANES] * scale, c, sa, sb) for hf in range(2)]
            return jnp.concatenate(halves, axis=1)

        staged = _pair_stages(stages)
        tok = lambda refs, base: [_from_residues(r, staged[base + i], DILATIONS[i]) for i, r in enumerate(refs)]
        chunks = [du]
        chunks += [unrot(a, HEAD_DIM ** -0.5) for a in tok((q0, q1, q2), 0)]
        chunks += [unrot(a, 1.0) for a in tok((k0, k1, k2), 3)]
        chunks += tok((v0, v1, v2), 6)
        acc = jnp.zeros((t, D_MODEL), F32)
        for ci, ch in enumerate(chunks):
            cols = slice(ci * GROUP_WIDTH, (ci + 1) * GROUP_WIDTH)
            cb = ch.astype(BF16)
            dz_ref[:, cols] = cb
            acc = acc + _dot(cb, w_ref[cols, :])
        gv = g_ref[...]
        n, rstd, _ = _rms(h_ref[...], gv)
        dx, dg = _rms_bwd(acc, n, rstd, gv)
        out_ref[...] = dh_ref[...] + dx
        dg_ref[...] += dg

    back = lambda i: nt - 1 - i
    row = lambda w: pl.BlockSpec((t, w), lambda i: (back(i), 0))
    full = lambda a, b: pl.BlockSpec((a, b), lambda i: (0, 0))
    res = [pl.BlockSpec((dil, t // dil, GROUP_WIDTH), lambda i: (0, back(i), 0)) for dil in DILATIONS]
    tables = [pl.BlockSpec((t, LANES), functools.partial(lambda i, k: (back(i), k), k=k)) for k in range(3)]
    return pl.pallas_call(
        body, name=name, grid=(nt,),
        in_specs=[row(D_MODEL), row(POOL_WIDTH), row(POOL_WIDTH), full(POOL_WIDTH, POOL_WIDTH), full(1, POOL_WIDTH)]
        + res * 3 + tables + [full(N_IN, D_MODEL), row(D_MODEL), full(1, D_MODEL)],
        out_specs=[row(D_MODEL), row(N_IN), full(1, D_MODEL), full(POOL_WIDTH, POOL_WIDTH), full(1, POOL_WIDTH)],
        out_shape=[jax.ShapeDtypeStruct((s, D_MODEL), F32), jax.ShapeDtypeStruct((s, N_IN), BF16),
                   jax.ShapeDtypeStruct((1, D_MODEL), F32), jax.ShapeDtypeStruct((POOL_WIDTH, POOL_WIDTH), F32),
                   jax.ShapeDtypeStruct((1, POOL_WIDTH), F32)],
        scratch_shapes=[pltpu.VMEM((t + POOL_HALO + POOL_PAD, POOL_WIDTH), F32)] * 4 + _stages(t, 9),
        compiler_params=_params(1),
    )(dh, dpool, y, w_bd, scale, *dq, *dk, *dv, rc, rsa, rsb, w_in, h, g)


def _matmul_tn(a, b, name, *, square_a=False, tm=None, tn=None, blocked_out=False, after=()):
    s, m = a.shape
    n = b.shape[1]
    tk = _row_tile(s, 2048)
    tm = tm or min(m, 1024)
    tn = tn or min(n, 1024)
    assert m % tm == 0 and n % tn == 0
    nk = s // tk
    nsub = tn // FF_BLOCK if blocked_out else 1

    def body(a_ref, b_ref, o_ref, ob_ref, acc):
        k = pl.program_id(2)

        def product():
            av = a_ref[...]
            if square_a:
                av = av.astype(F32)
                av = av * av
            return _dot_tn(av.astype(BF16), b_ref[...].astype(BF16))

        def emit(total):
            if blocked_out:
                for sub in range(nsub):
                    cols = slice(sub * FF_BLOCK, (sub + 1) * FF_BLOCK)
                    o_ref[sub] = total[:, cols]
                    ob_ref[sub] = total[:, cols].astype(BF16)
            else:
                o_ref[...] = total
                ob_ref[...] = total.astype(BF16)

        if nk == 1:
            emit(product())
            return

        @pl.when(k == 0)
        def _():
            acc[...] = product()

        @pl.when((k > 0) & (k < nk - 1))
        def _():
            acc[...] += product()

        @pl.when(k == nk - 1)
        def _():
            emit(acc[...] + product())

    if blocked_out:
        shape = (n // FF_BLOCK, m, FF_BLOCK)
        out_spec = pl.BlockSpec((nsub, tm, FF_BLOCK), lambda i, j, k: (j, i, 0))
    else:
        shape = (m, n)
        out_spec = pl.BlockSpec((tm, tn), lambda i, j, k: (i, j))
    return pl.pallas_call(
        _ordered_after(body, 2, after), name=name, grid=(m // tm, n // tn, nk),
        in_specs=[pl.BlockSpec((tk, tm), lambda i, j, k: (k, i)), pl.BlockSpec((tk, tn), lambda i, j, k: (k, j))]
        + [pl.BlockSpec(memory_space=pl.ANY)] * len(after),
        out_specs=[out_spec, out_spec],
        out_shape=[jax.ShapeDtypeStruct(shape, F32), jax.ShapeDtypeStruct(shape, BF16)],
        scratch_shapes=[pltpu.VMEM((tm, tn), F32)],
        compiler_params=_params(3),
    )(a, b, *after)


def _adamw_math(w, g, m, v):
    m = ADAM_B1 * m + (1.0 - ADAM_B1) * g
    v = ADAM_B2 * v + (1.0 - ADAM_B2) * (g * g)
    m_hat = m / (1.0 - ADAM_B1 ** ADAM_STEP)
    v_hat = v / (1.0 - ADAM_B2 ** ADAM_STEP)
    delta = -ADAM_LR * (m_hat / (jnp.sqrt(v_hat) + ADAM_EPS) + ADAM_WD * w)
    return delta, m, v


def _sum_chunks_body(own0_ref, own1_ref, r0_ref, r1_ref):
    layer0 = pl.program_id(0) == 0
    g = jnp.where(layer0, own0_ref[...], own1_ref[...])
    for k in range(N_DEV - 1):
        g = g + jnp.where(layer0, r0_ref[k], r1_ref[k]).astype(F32)
    return g


def _chunk_specs(t, cols):
    rows_of = lambda layer: (lambda l, i: jnp.where(l == layer, i, 0))
    blk = pl.BlockSpec((None, t, cols), lambda l, i, me: (l, i, 0))
    own = [pl.BlockSpec((None, t, cols), functools.partial(lambda l, i, me, pick: (me[0], pick(l, i), 0), pick=rows_of(ly)))
           for ly in range(2)]
    recv = [pl.BlockSpec((N_DEV - 1, t, cols), functools.partial(lambda l, i, me, pick: (0, pick(l, i), 0), pick=rows_of(ly)))
            for ly in range(2)]
    return blk, own + recv


def _adamw_sharded(w, m, v, chunks, me, name):
    _, rows, cols = w.shape
    t = max(d for d in range(SUBLANES, min(rows, 256) + 1, SUBLANES) if rows % d == 0)

    def body(me_ref, w_ref, m_ref, v_ref, own0_ref, own1_ref, r0_ref, r1_ref, g_ref, d_ref, nm_ref, nv_ref):
        g = _sum_chunks_body(own0_ref, own1_ref, r0_ref, r1_ref)
        g_ref[...] = g
        d_ref[...], nm_ref[...], nv_ref[...] = _adamw_math(w_ref[...], g, m_ref[...], v_ref[...])

    blk, chunk_specs = _chunk_specs(t, cols)
    return pl.pallas_call(
        body, name=name,
        grid_spec=pltpu.PrefetchScalarGridSpec(
            num_scalar_prefetch=1, grid=(2, rows // t), in_specs=[blk, blk, blk] + chunk_specs, out_specs=[blk] * 4),
        out_shape=[jax.ShapeDtypeStruct(w.shape, F32)] * 4,
        compiler_params=_params(2),
    )(me, w, m, v, *chunks)


def _adamw_small(w, g8, m, v, name):
    rows = dict(norm1=(0, 2), norm2=(2, 4), norm3=(4, 6), final_norm=(6, 7), pool_w=(8, 40))
    shaped = lambda t: [t[n].reshape(rows[n][1] - rows[n][0], D_MODEL) if n in rows else t[n] for n in SMALL]
    k = len(SMALL)

    def body(g8_ref, *refs):
        w_refs, m_refs, v_refs = refs[:k], refs[k:2 * k], refs[2 * k:3 * k]
        outs = [refs[(3 + i) * k:(4 + i) * k] for i in range(4)]
        spare_ref = refs[-1]
        g = g8_ref[0]
        for dev in range(1, N_DEV):
            g = g + g8_ref[dev]
        spare_ref[...] = g[7:8, 2 * POOL_WIDTH:2 * POOL_WIDTH + LANES]
        for i, n in enumerate(SMALL):
            if n in rows:
                pieces = [(slice(None), g[rows[n][0]:rows[n][1]])]
            else:
                pieces = [(slice(ly, ly + 1), g[7:8, ly * POOL_WIDTH:(ly + 1) * POOL_WIDTH]) for ly in range(2)]
            for at, gp in pieces:
                new = _adamw_math(w_refs[i][at], gp, m_refs[i][at], v_refs[i][at])
                for out, val in zip(outs, (gp, *new)):
                    out[i][at] = val

    ins = shaped(w) + shaped(m) + shaped(v)
    res = pl.pallas_call(
        body, name=name,
        out_shape=[jax.ShapeDtypeStruct(a.shape, F32) for a in shaped(w)] * 4 + [jax.ShapeDtypeStruct((1, LANES), F32)],
        compiler_params=pltpu.CompilerParams(vmem_limit_bytes=VMEM_LIMIT),
    )(g8, *ins)
    dicts = [{n: a.reshape(w[n].shape) for n, a in zip(SMALL, res[i * k:(i + 1) * k])} for i in range(4)]
    return (*dicts, res[-1])


def _peer(k):
    x, y, c = lax.axis_index("x"), lax.axis_index("y"), lax.axis_index("c")
    return (1 - x if k & 4 else x, 1 - y if k & 2 else y, 1 - c if k & 1 else c)


def _linear(dev):
    return 4 * dev[0] + 2 * dev[1] + dev[2]


HBM_SPEC = pl.BlockSpec(memory_space=pltpu.HBM)
SEM_SPEC = pl.BlockSpec(memory_space=pltpu.SEMAPHORE)
ANY_SPEC = pl.BlockSpec(memory_space=pl.ANY)
EFFECT = pltpu.SideEffectType.DATAFLOW_SIDE_EFFECTING


def _in_hbm(a):
    return pltpu.with_memory_space_constraint(a, pltpu.HBM)


class _Exchange:
    def __init__(self, name, groups, scatter, after=()):
        self.name, self.scatter = name, scatter
        self.sizes = sizes = [len(g) for g in groups]
        srcs = [a for g in groups for a in g]
        n, ng = len(srcs), len(groups)
        lead = (N_DEV - 1,) if scatter else (N_DEV,)
        shapes = [lead + (a.shape[1:] if scatter else a.shape) for a in srcs]
        lands = [lax.empty(sh, a.dtype) for sh, a in zip(shapes, srcs)]
        if not scatter:
            lands = [lax.dynamic_update_slice_in_dim(land, a[None], _linear(_peer(0)), axis=0)
                     for land, a in zip(lands, srcs)]
        offsets = [sum(sizes[:gi]) for gi in range(ng)]
        copy = self._copy

        def body(*refs):
            src, land = refs[:n], refs[n:2 * n]
            sems = refs[2 * n + len(after):2 * n + len(after) + 2 * ng]
            token = refs[-1]
            for gi in range(ng):
                for wi in range(sizes[gi]):
                    w = offsets[gi] + wi
                    for k in range(1, N_DEV):
                        copy(src[w], land[w], sems[2 * gi], sems[2 * gi + 1], wi, k).start()
            token[...] = jnp.zeros_like(token)

        sem_shapes = [pltpu.SemaphoreType.DMA(((N_DEV - 1) * sz,)) for sz in sizes for _ in range(2)]
        outs = pl.pallas_call(
            body, name=name + "_start",
            in_specs=[HBM_SPEC] * (2 * n) + [ANY_SPEC] * len(after),
            out_specs=[SEM_SPEC] * (2 * ng) + [HBM_SPEC] * (2 * n) + [pl.BlockSpec(memory_space=pltpu.VMEM)],
            out_shape=sem_shapes + [pltpu.HBM(a.shape, a.dtype) for a in srcs + lands]
            + [jax.ShapeDtypeStruct((8, LANES), F32)],
            input_output_aliases={i: 2 * ng + i for i in range(2 * n)},
            compiler_params=pltpu.CompilerParams(has_side_effects=EFFECT),
        )(*[_in_hbm(a) for a in srcs + lands], *after)
        self.sems = [outs[2 * gi:2 * gi + 2] for gi in range(ng)]
        thru = outs[2 * ng:2 * ng + 2 * n]
        self.srcs = [thru[offsets[gi]:offsets[gi] + sizes[gi]] for gi in range(ng)]
        self.lands = [thru[n + offsets[gi]:n + offsets[gi] + sizes[gi]] for gi in range(ng)]
        self.token = outs[-1]

    def _copy(self, src, land, send_sems, recv_sems, wi, k):
        to = _peer(k)
        if self.scatter:
            src_ref, dst_ref = src.at[_linear(to)], land.at[k - 1]
        else:
            src_ref, dst_ref = src, land.at[_linear(_peer(0))]
        return pltpu.make_async_remote_copy(
            src_ref=src_ref, dst_ref=dst_ref, send_sem=send_sems.at[(N_DEV - 1) * wi + k - 1],
            recv_sem=recv_sems.at[(N_DEV - 1) * wi + k - 1], device_id=to, device_id_type=MESH)

    def wait(self, gi, after):
        n = self.sizes[gi]
        copy = self._copy

        def body(*refs):
            src, land = refs[:n], refs[n:2 * n]
            send_sems, recv_sems = refs[2 * n], refs[2 * n + 1]
            for wi in range(n):
                for k in range(1, N_DEV):
                    cp = copy(src[wi], land[wi], send_sems, recv_sems, wi, k)
                    cp.wait_send()
                    cp.wait_recv()

        arrays = list(self.srcs[gi]) + list(self.lands[gi])
        outs = pl.pallas_call(
            body, name=f"{self.name}_wait{gi}",
            in_specs=[HBM_SPEC] * (2 * n) + [SEM_SPEC, SEM_SPEC] + [ANY_SPEC] * len(after),
            out_specs=[HBM_SPEC] * (2 * n),
            out_shape=[pltpu.HBM(a.shape, a.dtype) for a in arrays],
            input_output_aliases={i: i for i in range(2 * n)},
            compiler_params=pltpu.CompilerParams(has_side_effects=EFFECT),
        )(*arrays, *self.sems[gi], *after)
        return outs[:n], outs[n:]


def _rotary_tables(positions):
    rot_dim = HEAD_DIM // 4
    inv_freq = ROPE_THETA ** (-jnp.arange(0, rot_dim, 2, dtype=F32) / rot_dim)
    ang = positions.astype(F32)[:, None] * inv_freq
    cs = jnp.concatenate([jnp.cos(ang), jnp.sin(ang)], axis=1)
    dim = jnp.arange(LANES) % HEAD_DIM
    first, second = dim < ROT_SHIFT, (dim >= ROT_SHIFT) & (dim < rot_dim)
    src = jnp.arange(2 * ROT_SHIFT)[:, None]
    angle = (dim % ROT_SHIFT)[None, :]
    c = jnp.where((first | second)[None, :] & (src == angle), 1.0, 0.0)
    sa = jnp.where(second[None, :] & (src == angle + ROT_SHIFT), 1.0, 0.0)
    sb = jnp.where(first[None, :] & (src == angle + ROT_SHIFT), -1.0, 0.0)
    spread = jnp.concatenate([c, sa, sb], axis=1).astype(F32)
    base = jnp.concatenate([jnp.where(first | second, 0.0, 1.0), jnp.zeros((2 * LANES,))]).astype(F32)[None, :]
    return jnp.dot(cs, spread, precision=lax.Precision.HIGHEST, preferred_element_type=F32) + base


def _block_diag(pool_w):
    gc = pool_w.shape[-1]
    out = jnp.zeros((POOL_WIDTH, POOL_WIDTH), pool_w.dtype)
    for grp in range(pool_w.shape[0]):
        out = lax.dynamic_update_slice(out, pool_w[grp], (grp * gc, grp * gc))
    return out


def _diag_blocks(a):
    gc = POOL_WIDTH // len(POOL_WINDOWS)
    return jnp.stack([a[grp * gc:(grp + 1) * gc, grp * gc:(grp + 1) * gc] for grp in range(len(POOL_WINDOWS))])


def _local_step(x, p, positions, loss_target, norm1, pool_w, pool_scale, norm2, norm3, final_norm, weights, send):
    rc = rsa = rsb = _rotary_tables(positions)
    ones_bd = _block_diag(jnp.ones((4, HEAD_DIM, HEAD_DIM), BF16))
    saved = []
    h = x
    for i in range(2):
        tag = f"_l{i}"
        g1, g2, g3 = norm1[i:i + 1], norm2[i:i + 1], norm3[i:i + 1]
        w_bd = _block_diag(pool_w[i]).astype(BF16)
        scale = pool_scale[i:i + 1]
        if i == 0:
            w_in = weights(i, "in", (h, rc, w_bd))
            hn1, u, *qkv = _normproj_fwd(h, g1, w_in, rc, rsa, rsb, "normproj_fwd" + tag)
        else:
            w_in, (hn1, u, *qkv) = ahead
        qkv = [qkv[3 * grp:3 * grp + 3] for grp in range(3)]
        started = weights(i, "prefetch", (hn1,))
        o, lse = zip(*[_attn_fwd(*qkv[grp], f"attn_fwd{tag}_g{grp}", after=started) for grp in range(3)])
        w_out = weights(i, "out", o)
        h1, a, y = _outproj_fwd(h, u, w_bd, scale, o, lse, w_out, "outproj_fwd" + tag)
        w_up, w_down = weights(i, "mlp", (h1,))
        h2, hn2, r = _mlp_fwd(h1, g2, w_up, w_down, "mlp_fwd" + tag)
        w_gate, w_ple = weights(i, "gate", (h2,))
        h0 = h
        if i == 0:
            w_in_next = weights(1, "in", (h2,))
            h, hn3, gate, pb, *ahead = _gate_fwd(h2, g3, w_gate, p, i, w_ple, "gate_normproj_fwd",
                                                 follow=(norm1[1:2], w_in_next, rc, rsa, rsb))
            ahead = (w_in_next, ahead)
        else:
            hn3 = gate = pb = None
            loss, d_final, *top = _gate_fwd(h2, g3, w_gate, p, i, w_ple, "gate_loss_gate_bwd",
                                            head=(final_norm.reshape(1, D_MODEL), loss_target))
        saved.append(dict(h0=h0, hn1=hn1, qkv=qkv, y=y, o=o, lse=lse, a=a, h1=h1, hn2=hn2, r=r, h2=h2,
                          hn3=hn3, gate=gate, pb=pb, w_bd=w_bd, scale=scale, g1=g1, g2=g2, g3=g3,
                          w_in=w_in, w_out=w_out, w_up=w_up, w_down=w_down, w_gate=w_gate, w_ple=w_ple))

    grads = [None, None]
    sent = ()
    for i in (1, 0):
        tag = f"_l{i}"
        sv = saved[i]
        if i == 1:
            dh2, dg3, dw_gate, dw_ple = top
        else:
            dh2, dg3, dw_gate, dw_ple = _gate_bwd(dh, sv["gate"], sv["pb"], sv["w_ple"], sv["h2"], sv["g3"],
                                                  sv["w_gate"], sv["hn3"], "gate_bwd" + tag, after=sent)
        dh1, dup, dg2, dh2b = _mlp_bwd(dh2, sv["r"], sv["h1"], sv["g2"], sv["w_up"], sv["w_down"], "mlp_bwd" + tag)
        dw_down = _matmul_tn(sv["r"], dh2b, "dw_down" + tag, square_a=True)
        dw_up = _matmul_tn(sv["hn2"], dup, "dw_up" + tag, blocked_out=True)
        dpool, do0, do1, do2, de0, de1, de2, dw_out = _outproj_bwd(dh1, sv["w_out"], sv["o"], sv["lse"], ones_bd,
                                                                   sv["a"], "outproj_bwd" + tag)
        sent = send(i, "main", dict(w_gate=dw_gate, w_ple=dw_ple, w_down=dw_down, w_up=dw_up, w_out=dw_out))
        dqkv = [_attn_bwd(*sv["qkv"][grp], do_g, sv["lse"][grp], de_g, f"attn_bwd{tag}_g{grp}", after=sent)
                for grp, (do_g, de_g) in enumerate(((do0, de0), (do1, de1), (do2, de2)))]
        dq, dk, dv = zip(*dqkv)
        dh, dz, dg1, dw_bd, dscale = _normproj_bwd(dh1, dpool, sv["y"], sv["w_bd"], sv["scale"], dq, dk, dv, rc, rsa, rsb,
                                                   sv["w_in"], sv["h0"], sv["g1"], "normproj_bwd" + tag)
        grads[i] = dict(norm1=dg1, norm2=dg2, norm3=dg3, pool_w=_diag_blocks(dw_bd), pool_scale=dscale)
        small_sent = send(0, "small", (grads, d_final, loss)) if i == 0 else ()
        dw_in = _matmul_tn(dz, sv["hn1"], "dw_in" + tag, tm=N_IN // 2, after=small_sent)
        sent = send(i, "in", dict(w_in=dw_in))
    return dh, sent


def _pack_small(norm1, norm2, norm3, final_norm, pool_scale, pool_w, spare=None):
    spare = jnp.zeros((1, LANES), F32) if spare is None else spare
    scale_row = jnp.concatenate([pool_scale.reshape(1, 2 * POOL_WIDTH), spare,
                                 jnp.zeros((1, D_MODEL - 2 * POOL_WIDTH - LANES), F32)], axis=1)
    return jnp.concatenate([norm1, norm2, norm3, final_norm.reshape(1, D_MODEL), scale_row,
                            pool_w.reshape(32, D_MODEL)], axis=0)


def _chunks_cols(a, cols):
    return a.reshape(a.shape[0], N_DEV, cols).transpose(1, 0, 2)


def _chunks_rows(a, rows):
    return a.reshape(N_DEV, rows, a.shape[1])


BIG = ("w_in", "w_out", "w_up", "w_down", "w_gate", "w_ple")
SMALL = ("norm1", "norm2", "norm3", "final_norm", "pool_scale", "pool_w")
ORDER = ("norm1", "w_in", "pool_w", "pool_scale", "w_out", "norm2", "w_up", "w_down", "norm3", "w_gate", "w_ple",
         "final_norm")


def kernel(x, p, positions, norm1, w_in, pool_w, pool_scale, w_out, norm2, w_up, w_down, norm3, w_gate, w_ple, final_norm, loss_target, m_norm1, m_w_in, m_pool_w, m_pool_scale, m_w_out, m_norm2, m_w_up, m_w_down, m_norm3, m_w_gate, m_w_ple, m_final_norm, v_norm1, v_w_in, v_pool_w, v_pool_scale, v_w_out, v_norm2, v_w_up, v_w_down, v_norm3, v_w_gate, v_w_ple, v_final_norm):
    w = dict(norm1=norm1, w_in=w_in, pool_w=pool_w, pool_scale=pool_scale, w_out=w_out, norm2=norm2, w_up=w_up,
             w_down=w_down, norm3=norm3, w_gate=w_gate, w_ple=w_ple, final_norm=final_norm)
    m = dict(norm1=m_norm1, w_in=m_w_in, pool_w=m_pool_w, pool_scale=m_pool_scale, w_out=m_w_out, norm2=m_norm2,
             w_up=m_w_up, w_down=m_w_down, norm3=m_norm3, w_gate=m_w_gate, w_ple=m_w_ple, final_norm=m_final_norm)
    v = dict(norm1=v_norm1, w_in=v_w_in, pool_w=v_pool_w, pool_scale=v_pool_scale, w_out=v_w_out, norm2=v_norm2,
             w_up=v_w_up, w_down=v_w_down, norm3=v_norm3, w_gate=v_w_gate, w_ple=v_w_ple, final_norm=v_final_norm)
    seq = x.shape[1]

    bf = {n: [w[n][layer].astype(BF16) for layer in range(2)] for n in BIG}
    bf["w_in"] = [a.T for a in bf["w_in"]]
    me = 4 * lax.axis_index("x") + 2 * lax.axis_index("y") + lax.axis_index("c")
    parts = dict(zip(("in", "out", "mlp", "gate"), (("w_in",), ("w_out",), ("w_up", "w_down"), ("w_gate", "w_ple"))))
    first = _Exchange("gather_first", [[bf["w_in"][0]]], scatter=False)
    later = [pt for pt in parts if pt != "in"]
    gathers = [_Exchange("gather_l0", [[bf[n][0] for n in parts[pt]] for pt in later], scatter=False,
                         after=(first.token,))]
    unpack = dict(w_in=lambda a: a.reshape(N_IN, D_MODEL),
                  w_out=lambda a: a.reshape(D_MODEL, D_MODEL), w_gate=lambda a: a.reshape(D_MODEL, D_MODEL),
                  w_ple=lambda a: a.transpose(1, 0, 2).reshape(PLE_DIM, D_MODEL), w_up=lambda a: a, w_down=lambda a: a)

    def weights(layer, part, after):
        if part == "prefetch":
            if layer != 0:
                return ()
            gathers.append(_Exchange("gather_l1", [[bf[n][1] for n in parts[pt]] for pt in parts], scatter=False,
                                     after=after))
            return (gathers[1].token,)
        if layer == 0 and part == "in":
            _, lands = first.wait(0, (*after, gathers[0].token))
        elif layer == 0:
            _, lands = gathers[0].wait(later.index(part), after)
        else:
            _, lands = gathers[1].wait(tuple(parts).index(part), after)
        full = [unpack[n](land) for n, land in zip(parts[part], lands)]
        return full if len(full) > 1 else full[0]

    to_chunks = dict(w_in=lambda a: _chunks_rows(a, N_IN // N_DEV),
                     w_out=lambda a: _chunks_rows(a, D_MODEL // N_DEV),
                     w_up=lambda a: a, w_down=lambda a: _chunks_rows(a, FF_BLOCK),
                     w_gate=lambda a: _chunks_rows(a, D_MODEL // N_DEV), w_ple=lambda a: _chunks_cols(a, D_MODEL // N_DEV))
    own = {n: [None, None] for n in BIG}
    scatters = {}

    def send(layer, part, grads):
        if part == "small":
            per_layer, d_final, loss = grads
            pack = _pack_small(
                *[jnp.concatenate([per_layer[0][n], per_layer[1][n]], axis=0) for n in ("norm1", "norm2", "norm3")],
                d_final.reshape(D_MODEL),
                jnp.concatenate([per_layer[0]["pool_scale"], per_layer[1]["pool_scale"]], axis=0),
                jnp.stack([per_layer[0]["pool_w"], per_layer[1]["pool_w"]]), spare=loss)
            scatters["small"] = _Exchange("gather_small", [[pack]], scatter=False)
            return (scatters["small"].token,)
        for n, (g32, _) in grads.items():
            own[n][layer] = to_chunks[n](g32)
        ex = _Exchange(f"scatter_{part}_l{layer}", [[to_chunks[n](g16) for n, (_, g16) in grads.items()]], scatter=True)
        scatters[layer, part] = (tuple(grads), ex)
        return (ex.token,)

    _, pool_w_late = lax.optimization_barrier((first.token, pool_w))
    dx, sent = _local_step(
        x.reshape(seq, D_MODEL), p.reshape(2, seq, PLE_DIM), positions.reshape(seq), loss_target.reshape(seq, D_MODEL),
        norm1, pool_w_late, pool_scale, norm2, norm3, final_norm, weights, send)

    g_out, d_out, m_out, v_out = {}, {}, {}, {}
    my_index = me.reshape(1)
    for part in ("main", "in"):
        recv = {}
        for layer in (1, 0):
            names, ex = scatters[layer, part]
            for n, r in zip(names, ex.wait(0, sent)[1]):
                recv[n, layer] = r
        sent = ()
        for n in names:
            grad = (*own[n], recv[n, 0], recv[n, 1])
            turn = (lambda a: a.transpose(0, 2, 1)) if n == "w_in" else (lambda a: a)
            updated = _adamw_sharded(turn(w[n]), turn(m[n]), turn(v[n]), grad, my_index, "adamw_" + n)
            g_out[n], d_out[n], m_out[n], v_out[n] = map(turn, updated)
            sent += (updated[1],)
    _, (small_g8,) = scatters["small"].wait(0, sent)
    *small, spare = _adamw_small(w, small_g8, m, v, "adamw_small")
    for dst, a in zip((g_out, d_out, m_out, v_out), small):
        dst.update(a)

    return (spare[0, 0], dx.reshape(1, seq, D_MODEL), *[g_out[n] for n in ORDER], *[d_out[n] for n in ORDER],
            *[m_out[n] for n in ORDER], *[v_out[n] for n in ORDER])
```

```python
import functools

import jax
import jax.numpy as jnp
from jax import lax
from jax.experimental import pallas as pl
from jax.experimental.pallas import tpu as pltpu

F32 = jnp.float32
BF16 = jnp.bfloat16

D_MODEL = 1024
HEAD_DIM = 64
POOL_WIDTH = 256
POOL_WINDOWS = (2, 4, 8, 16)
POOL_HALO = 16
POOL_PAD = 8
GROUP_WIDTH = 256
DILATIONS = (1, 4, 16)
ATTN_BLOCK = 128
ROT_SHIFT = 8
ROPE_THETA = 500000.0
D_FF = 4096
FF_BLOCK = 512
FF_PER_STEP = 2
MLP_BWD_TILE = 512
FWD_TILE = 1024
N_DEV = 8
N_IN = POOL_WIDTH + 3 * 768
PLE_DIM = 256
EPS = 1e-6
NEG_BIG = -1e30

ADAM_LR = 0.001
ADAM_B1 = 0.9
ADAM_B2 = 0.999
ADAM_EPS = 1e-08
ADAM_WD = 0.01
ADAM_STEP = 10

LANES = 128
SUBLANES = 8
VMEM_LIMIT = 56 * 1024 * 1024
MESH = pl.DeviceIdType.MESH


def _params(n_grid):
    return pltpu.CompilerParams(dimension_semantics=("arbitrary",) * n_grid, vmem_limit_bytes=VMEM_LIMIT)


def _dot(a, b):
    return jnp.dot(a, b, preferred_element_type=F32)


def _dot_nt(a, b):
    return lax.dot_general(a, b, (((1,), (1,)), ((), ())), preferred_element_type=F32)


def _dot_tn(a, b):
    return lax.dot_general(a, b, (((0,), (0,)), ((), ())), preferred_element_type=F32)


def _rms(x, g):
    rstd = lax.rsqrt(jnp.mean(x * x, axis=-1, keepdims=True) + EPS)
    n = x * rstd
    return n, rstd, n * g


def _rms_bwd(dy, n, rstd, g):
    dyn = dy * g
    dx = rstd * (dyn - n * jnp.mean(dyn * n, axis=-1, keepdims=True))
    return dx, jnp.sum(dy * n, axis=0, keepdims=True)


def _ordered_after(body, n_in, after):
    if not after:
        return body
    return lambda *refs: body(*refs[:n_in], *refs[n_in + len(after):])


def _resident(shape):
    return pl.BlockSpec(shape, lambda i: (0,) * len(shape), pipeline_mode=pl.Buffered(1))


def _row_tile(s, t):
    t = min(s, t)
    assert s % t == 0
    return t


def _rot(z, c, sa, sb):
    return z * c + pltpu.roll(z, ROT_SHIFT, 1) * sa + pltpu.roll(z, LANES - ROT_SHIFT, 1) * sb


def _table_specs(t):
    return [pl.BlockSpec((t, LANES), functools.partial(lambda i, k: (i, k), k=k)) for k in range(3)]


def _rot_t(dz, c, sa, sb):
    return dz * c + pltpu.roll(dz * sa, LANES - ROT_SHIFT, 1) + pltpu.roll(dz * sb, ROT_SHIFT, 1)


def _to_residues(value, stage, out_ref, dil):
    if dil == 1:
        out_ref[0] = value.astype(out_ref.dtype)
        return
    rows = value.shape[0] // dil
    for hf in range(GROUP_WIDTH // LANES):
        lanes = slice(hf * LANES, (hf + 1) * LANES)
        stage[hf][...] = value[:, lanes]
        for r in range(dil):
            out_ref[r, :, lanes] = stage[hf][pl.ds(r, rows, stride=dil), :].astype(out_ref.dtype)


def _from_residues(in_ref, stage, dil):
    if dil == 1:
        return in_ref[0].astype(F32)
    rows = in_ref.shape[1]
    for hf in range(GROUP_WIDTH // LANES):
        for r in range(dil):
            stage[hf][pl.ds(r, rows, stride=dil), :] = in_ref[r, :, hf * LANES:(hf + 1) * LANES].astype(F32)
    return jnp.concatenate([stage[0][...], stage[1][...]], axis=1)


def _residue_spec(dil, t):
    return pl.BlockSpec((dil, t // dil, GROUP_WIDTH), lambda i: (0, i, 0))


def _residue_shape(dil, s, dtype):
    return jax.ShapeDtypeStruct((dil, s // dil, GROUP_WIDTH), dtype)


def _stages(t, n):
    return [pltpu.VMEM((t, LANES), F32)] * (n * (GROUP_WIDTH // LANES))


def _pair_stages(refs):
    return [refs[i:i + 2] for i in range(0, len(refs), 2)]


def _normproj_tile(x, g_ref, w_ref, c_ref, sa_ref, sb_ref, hn_ref, u_ref, *rest):
    qkv_refs, stages = rest[:9], _pair_stages(rest[9:])
    _, _, hn = _rms(x, g_ref[...])
    hb = hn.astype(BF16)
    hn_ref[...] = hb
    c, sa, sb = c_ref[...], sa_ref[...], sb_ref[...]

    def rot(z, scale):
        halves = [_rot(z[:, hf * LANES:(hf + 1) * LANES], c, sa, sb) * scale for hf in range(2)]
        return jnp.concatenate(halves, axis=1)

    proj = lambda lo: _dot_nt(hb, w_ref[lo:lo + GROUP_WIDTH, :])
    u_ref[...] = proj(0)
    for grp, dil in enumerate(DILATIONS):
        lo = POOL_WIDTH + grp * GROUP_WIDTH
        q_ref, k_ref, v_ref = qkv_refs[3 * grp:3 * grp + 3]
        _to_residues(rot(proj(lo), HEAD_DIM ** -0.5), stages[0], q_ref, dil)
        _to_residues(rot(proj(lo + 768), 1.0), stages[1], k_ref, dil)
        _to_residues(proj(lo + 1536), stages[2], v_ref, dil)


def _normproj_operands(s, t):
    row = lambda w: pl.BlockSpec((t, w), lambda i: (i, 0))
    in_specs = [pl.BlockSpec((1, D_MODEL), lambda i: (0, 0)), _resident((N_IN, D_MODEL))] + _table_specs(t)
    out_specs = [row(D_MODEL), row(POOL_WIDTH)] + [_residue_spec(dil, t) for dil in DILATIONS for _ in range(3)]
    out_shape = [jax.ShapeDtypeStruct((s, D_MODEL), BF16), jax.ShapeDtypeStruct((s, POOL_WIDTH), F32)]
    out_shape += [_residue_shape(dil, s, BF16) for dil in DILATIONS for _ in range(3)]
    return in_specs, out_specs, out_shape, _stages(t, 3)


def _normproj_fwd(h, g, w_in, rc, rsa, rsb, name):
    s = h.shape[0]
    t = _row_tile(s, FWD_TILE)

    def body(h_ref, *refs):
        _normproj_tile(h_ref[...], *refs)

    in_specs, out_specs, out_shape, scratch = _normproj_operands(s, t)
    return pl.pallas_call(
        body, name=name, grid=(s // t,), in_specs=[pl.BlockSpec((t, D_MODEL), lambda i: (i, 0))] + in_specs,
        out_specs=out_specs, out_shape=out_shape, scratch_shapes=scratch, compiler_params=_params(1),
    )(h, g, w_in, rc, rsa, rsb)


def _pool_lane_window():
    lane = lax.broadcasted_iota(jnp.int32, (1, POOL_WIDTH), 1)
    return jnp.left_shift(2, lane // (POOL_WIDTH // len(POOL_WINDOWS)))


def _window_sums(ext, b2, b4, b8, t, lo, tile, direction):
    rows = t + POOL_HALO
    for src, dst, sh in ((ext, b2, 1), (b2, b4, 2), (b4, b8, 4)):
        dst[lo:lo + rows, :] = src[lo:lo + rows, :] + src[lo + direction * sh:lo + direction * sh + rows, :]
    s16 = b8[tile:tile + t, :] + b8[tile + direction * 8:tile + direction * 8 + t, :]
    win = _pool_lane_window()
    return jnp.where(win == 2, b2[tile:tile + t, :],
                     jnp.where(win == 4, b4[tile:tile + t, :], jnp.where(win == 8, b8[tile:tile + t, :], s16)))


def _pool_fwd_tile(i, u_ref, w_ref, sc_ref, y_ref, ext, b2, b4, b8):
    t = u_ref.shape[0]
    first = POOL_PAD + POOL_HALO

    @pl.when(i == 0)
    def _():
        for buf in (ext, b2, b4):
            buf[0:POOL_PAD, :] = jnp.zeros((POOL_PAD, POOL_WIDTH), F32)
        ext[POOL_PAD:first, :] = jnp.zeros((POOL_HALO, POOL_WIDTH), F32)

    x = u_ref[...]
    ext[first:, :] = x
    wsum = _window_sums(ext, b2, b4, b8, t, POOL_PAD, first, -1)
    pos = i * t + lax.broadcasted_iota(jnp.int32, (t, POOL_WIDTH), 0)
    cnt = jnp.minimum(pos + 1, _pool_lane_window()).astype(F32)
    yb = (wsum / cnt - x).astype(BF16)
    y_ref[...] = yb
    ext[POOL_PAD:first, :] = x[t - POOL_HALO:, :]
    return _dot(yb, w_ref[...]) * sc_ref[...]


def _head_masks():
    lane = lax.broadcasted_iota(jnp.int32, (ATTN_BLOCK, GROUP_WIDTH), 1)
    return [lane // HEAD_DIM == hd for hd in range(GROUP_WIDTH // HEAD_DIM)]


def _stack_heads(a, masks):
    zero = jnp.zeros_like(a)
    return jnp.concatenate([jnp.where(m, a, zero) for m in masks], axis=0)


def _band_bias(first_step):
    rows = ATTN_BLOCK * (GROUP_WIDTH // HEAD_DIM)
    i = lax.broadcasted_iota(jnp.int32, (rows, 2 * ATTN_BLOCK), 0) & (ATTN_BLOCK - 1)
    j = lax.broadcasted_iota(jnp.int32, (rows, 2 * ATTN_BLOCK), 1)
    inner = jnp.where((j >= i) & (j <= i + ATTN_BLOCK), 0.0, NEG_BIG)
    return jnp.where((j < ATTN_BLOCK) & first_step, NEG_BIG, inner), inner


def _column_per_head(a):
    return jnp.concatenate([a[:, hd * HEAD_DIM:hd * HEAD_DIM + 1] for hd in range(GROUP_WIDTH // HEAD_DIM)], axis=0)


def _blocks_per_step(nb):
    if nb <= 16:
        return nb
    return next(qb for qb in (16, 8, 4, 2, 1) if nb % qb == 0)


def _residues_per_step(dil, nb, qb):
    return 2 if (nb == qb and qb < 8 and dil % 2 == 0) else 1


def _attn_fwd(q, k, v, name, after=()):
    dil, length, _ = q.shape
    nb = length // ATTN_BLOCK
    qb = _blocks_per_step(nb)
    rs = _residues_per_step(dil, nb, qb)

    def body(q_ref, kp_ref, kc_ref, vp_ref, vc_ref, o_ref, lse_ref):
        masks = _head_masks()
        bias = _band_bias(pl.program_id(1) == 0)
        for rr in range(rs):
            for qi in range(qb):
                here = slice(qi * ATTN_BLOCK, (qi + 1) * ATTN_BLOCK)
                before = slice((qi - 1) * ATTN_BLOCK, qi * ATTN_BLOCK)
                kcat = jnp.concatenate([kp_ref[rr] if qi == 0 else kc_ref[rr, before], kc_ref[rr, here]], axis=0)
                vcat = jnp.concatenate([vp_ref[rr] if qi == 0 else vc_ref[rr, before], vc_ref[rr, here]], axis=0)
                qs = _stack_heads(q_ref[rr, here], masks)
                sc = _dot_nt(qs, kcat) + bias[min(qi, 1)]
                m = jnp.max(sc, axis=1, keepdims=True)
                e = jnp.exp(sc - m)
                l = jnp.sum(e, axis=1, keepdims=True)
                p = (e / l).astype(BF16)
                lse = m + jnp.log(l)
                o = jnp.zeros((ATTN_BLOCK, GROUP_WIDTH), F32)
                lse_full = jnp.zeros((ATTN_BLOCK, GROUP_WIDTH), F32)
                for hd, msk in enumerate(masks):
                    rows = slice(hd * ATTN_BLOCK, (hd + 1) * ATTN_BLOCK)
                    o = jnp.where(msk, _dot(p[rows], vcat), o)
                    lse_full = jnp.where(msk, lse[rows], lse_full)
                o_ref[rr, here] = o.astype(o_ref.dtype)
                lse_ref[rr, here] = lse_full

    cur = pl.BlockSpec((rs, qb * ATTN_BLOCK, GROUP_WIDTH), lambda r, j: (r, j, 0))
    prev = pl.BlockSpec((rs, ATTN_BLOCK, GROUP_WIDTH), lambda r, j: (r, jnp.maximum(qb * j - 1, 0), 0))
    return pl.pallas_call(
        _ordered_after(body, 5, after), name=name, grid=(dil // rs, nb // qb),
        in_specs=[cur, prev, cur, prev, cur] + [pl.BlockSpec(memory_space=pl.ANY)] * len(after), out_specs=[cur, cur],
        out_shape=[jax.ShapeDtypeStruct(q.shape, BF16), jax.ShapeDtypeStruct(q.shape, F32)],
        compiler_params=_params(2),
    )(q, k, k, v, v, *after)


def _group_weights(l0, l1, l2):
    m = jnp.maximum(jnp.maximum(l0, l1), l2)
    e0, e1, e2 = jnp.exp(l0 - m), jnp.exp(l1 - m), jnp.exp(l2 - m)
    den = e0 + e1 + e2
    return e0 / den, e1 / den, e2 / den


def _outproj_fwd(h, u, w_bd, scale, o, lse, w_out, name):
    s = h.shape[0]
    t = _row_tile(s, FWD_TILE)

    def body(h_ref, u_ref, wbd_ref, sc_ref, o0, o1, o2, l0, l1, l2, w_ref, out_ref, a_ref, y_ref, ext, b2, b4, b8,
             *stages):
        pool_out = _pool_fwd_tile(pl.program_id(0), u_ref, wbd_ref, sc_ref, y_ref, ext, b2, b4, b8)
        stages = _pair_stages(stages)
        ov = [_from_residues(r, stages[i], DILATIONS[i]) for i, r in enumerate((o0, o1, o2))]
        lv = [_from_residues(r, stages[3 + i], DILATIONS[i]) for i, r in enumerate((l0, l1, l2))]
        wts = _group_weights(*lv)
        a = jnp.concatenate([pool_out] + [ov[i] * wts[i] for i in range(3)], axis=1).astype(BF16)
        a_ref[...] = a
        out_ref[...] = h_ref[...] + _dot(a, w_ref[...])

    row = lambda w: pl.BlockSpec((t, w), lambda i: (i, 0))
    res = [_residue_spec(dil, t) for dil in DILATIONS]
    return pl.pallas_call(
        body, name=name, grid=(s // t,),
        in_specs=[row(D_MODEL), row(POOL_WIDTH), _resident((POOL_WIDTH, POOL_WIDTH)), _resident((1, POOL_WIDTH))]
        + res + res + [_resident((D_MODEL, D_MODEL))],
        out_specs=[row(D_MODEL), row(D_MODEL), row(POOL_WIDTH)],
        out_shape=[jax.ShapeDtypeStruct((s, D_MODEL), F32), jax.ShapeDtypeStruct((s, D_MODEL), BF16),
                   jax.ShapeDtypeStruct((s, POOL_WIDTH), BF16)],
        scratch_shapes=[pltpu.VMEM((t + POOL_HALO + POOL_PAD, POOL_WIDTH), F32)] * 4 + _stages(t, 6),
        compiler_params=_params(1),
    )(h, u, w_bd, scale, *o, *lse, w_out)


def _mlp_fwd(h, g, w_up, w_down, name):
    s = h.shape[0]
    t = _row_tile(s, 512)
    nblk = D_FF // FF_BLOCK

    def body(h_ref, g_ref, wu_ref, wd_ref, out_ref, hn_ref, r_ref):
        x = h_ref[...]
        _, _, hn = _rms(x, g_ref[...])
        hb = hn.astype(BF16)
        hn_ref[...] = hb
        acc = None
        for b0 in range(0, nblk, FF_PER_STEP):
            acts = []
            for b in range(b0, b0 + FF_PER_STEP):
                r = jnp.maximum(_dot(hb, wu_ref[b]), 0.0)
                r_ref[:, b * FF_BLOCK:(b + 1) * FF_BLOCK] = r.astype(BF16)
                acts.append((r * r).astype(BF16))
            wd = wd_ref[b0:b0 + FF_PER_STEP].reshape(FF_PER_STEP * FF_BLOCK, D_MODEL)
            part = _dot(jnp.concatenate(acts, axis=1), wd)
            acc = part if acc is None else acc + part
        out_ref[...] = x + acc

    row = lambda w: pl.BlockSpec((t, w), lambda i: (i, 0))
    resident = lambda shape: pl.BlockSpec(shape, lambda i: (0, 0, 0), pipeline_mode=pl.Buffered(1))
    return pl.pallas_call(
        body, name=name, grid=(s // t,),
        in_specs=[row(D_MODEL), pl.BlockSpec((1, D_MODEL), lambda i: (0, 0)),
                  resident((nblk, D_MODEL, FF_BLOCK)), resident((nblk, FF_BLOCK, D_MODEL))],
        out_specs=[row(D_MODEL), row(D_MODEL), row(D_FF)],
        out_shape=[jax.ShapeDtypeStruct((s, D_MODEL), F32), jax.ShapeDtypeStruct((s, D_MODEL), BF16),
                   jax.ShapeDtypeStruct((s, D_FF), BF16)],
        compiler_params=_params(1),
    )(h, g, w_up, w_down)


def _gate_fwd(h, g, w_gate, p, layer, w_ple, name, head=None, follow=None):
    assert (head is None) != (follow is None)
    s = h.shape[0]
    t = _row_tile(s, 512)
    last = s // t - 1

    def body(h_ref, g_ref, wg_ref, p_ref, wp_ref, *refs):
        x = h_ref[...]
        gv = g_ref[...]
        n, rstd, hn = _rms(x, gv)
        hb = hn.astype(BF16)
        gate = 1.0 / (1.0 + jnp.exp(-_dot(hb, wg_ref[...])))
        pb = p_ref[...].astype(BF16)
        e = _dot(pb, wp_ref[...])
        h3 = x + gate * e
        if follow is not None:
            out_ref, hn_ref, gate_ref, pb_ref = refs[5:9]
            out_ref[...] = h3
            hn_ref[...] = hb
            pb_ref[...] = pb
            gate_ref[...] = gate.astype(BF16)
            _normproj_tile(h3, *refs[:5], *refs[9:])
            return
        gf_ref, t_ref, loss_ref, dgf_ref, out_ref, dg_ref, dwg_ref, dwgb_ref, dwp_ref, dwpb_ref = refs
        i = pl.program_id(0)

        @pl.when(i == 0)
        def _():
            for ref in (loss_ref, dgf_ref, dg_ref, dwg_ref, dwp_ref):
                ref[...] = jnp.zeros_like(ref)

        gf = gf_ref[...]
        n3, rstd3, y = _rms(h3, gf)
        err = y - t_ref[...]
        loss_ref[...] += jnp.sum(err * err) * (0.5 / D_MODEL)
        d, dgf = _rms_bwd(err * (1.0 / D_MODEL), n3, rstd3, gf)
        dgf_ref[...] += dgf
        dgl = (d * e * gate * (1.0 - gate)).astype(BF16)
        dwg_ref[...] += _dot_tn(hb, dgl)
        dwp_ref[...] += _dot_tn(pb, (d * gate).astype(BF16))
        dx, dg = _rms_bwd(_dot_nt(dgl, wg_ref[...]), n, rstd, gv)
        out_ref[...] = d + dx
        dg_ref[...] += dg

        @pl.when(i == last)
        def _():
            dwgb_ref[...] = dwg_ref[...].astype(BF16)
            dwpb_ref[...] = dwp_ref[...].astype(BF16)

    row = lambda w: pl.BlockSpec((t, w), lambda i: (i, 0))
    full = lambda a, b: pl.BlockSpec((a, b), lambda i: (0, 0))
    in_specs = [row(D_MODEL), full(1, D_MODEL), _resident((D_MODEL, D_MODEL)),
                pl.BlockSpec((None, t, PLE_DIM), lambda i: (layer, i, 0)), _resident((PLE_DIM, D_MODEL))]
    if follow is not None:
        next_in, next_out, next_shape, scratch = _normproj_operands(s, t)
        return pl.pallas_call(
            body, name=name, grid=(s // t,), in_specs=in_specs + next_in,
            out_specs=[row(D_MODEL), row(D_MODEL), row(D_MODEL), row(PLE_DIM)] + next_out,
            out_shape=[jax.ShapeDtypeStruct((s, D_MODEL), F32), jax.ShapeDtypeStruct((s, D_MODEL), BF16),
                       jax.ShapeDtypeStruct((s, D_MODEL), BF16), jax.ShapeDtypeStruct((s, PLE_DIM), BF16)] + next_shape,
            scratch_shapes=scratch, compiler_params=_params(1),
        )(h, g, w_gate, p, w_ple, *follow)
    loss, dgf, dh2, dg, dwg, dwgb, dwp, dwpb = pl.pallas_call(
        body, name=name, grid=(s // t,), in_specs=in_specs + [full(1, D_MODEL), row(D_MODEL)],
        out_specs=[pl.BlockSpec((1, LANES), lambda i: (0, 0)), full(1, D_MODEL), row(D_MODEL), full(1, D_MODEL),
                   full(D_MODEL, D_MODEL), full(D_MODEL, D_MODEL), full(PLE_DIM, D_MODEL), full(PLE_DIM, D_MODEL)],
        out_shape=[jax.ShapeDtypeStruct((1, LANES), F32), jax.ShapeDtypeStruct((1, D_MODEL), F32),
                   jax.ShapeDtypeStruct((s, D_MODEL), F32), jax.ShapeDtypeStruct((1, D_MODEL), F32),
                   jax.ShapeDtypeStruct((D_MODEL, D_MODEL), F32), jax.ShapeDtypeStruct((D_MODEL, D_MODEL), BF16),
                   jax.ShapeDtypeStruct((PLE_DIM, D_MODEL), F32), jax.ShapeDtypeStruct((PLE_DIM, D_MODEL), BF16)],
        compiler_params=_params(1),
    )(h, g, w_gate, p, w_ple, *head)
    return loss, dgf, dh2, dg, (dwg, dwgb), (dwp, dwpb)


def _gate_bwd(dh, gate, pb, w_ple, h, g, w_gate, hn, name, after=()):
    s = h.shape[0]
    t = _row_tile(s, FWD_TILE)
    last = s // t - 1

    def body(dh_ref, gate_ref, pb_ref, wp_ref, h_ref, g_ref, wg_ref, hn_ref, out_ref, dg_ref, dwg_ref, dwgb_ref,
             dwp_ref, dwpb_ref):
        i = pl.program_id(0)

        @pl.when(i == 0)
        def _():
            dg_ref[...] = jnp.zeros_like(dg_ref)
            dwg_ref[...] = jnp.zeros_like(dwg_ref)
            dwp_ref[...] = jnp.zeros_like(dwp_ref)

        d = dh_ref[...]
        gate = gate_ref[...].astype(F32)
        pb = pb_ref[...]
        e = _dot(pb, wp_ref[...])
        dgl = (d * e * gate * (1.0 - gate)).astype(BF16)
        dwg_ref[...] += _dot_tn(hn_ref[...], dgl)
        dwp_ref[...] += _dot_tn(pb, (d * gate).astype(BF16))
        gv = g_ref[...]
        n, rstd, _ = _rms(h_ref[...], gv)
        dx, dg = _rms_bwd(_dot_nt(dgl, wg_ref[...]), n, rstd, gv)
        out_ref[...] = d + dx
        dg_ref[...] += dg

        @pl.when(i == last)
        def _():
            dwgb_ref[...] = dwg_ref[...].astype(BF16)
            dwpb_ref[...] = dwp_ref[...].astype(BF16)

    row = lambda w: pl.BlockSpec((t, w), lambda i: (i, 0))
    full = lambda a, b: pl.BlockSpec((a, b), lambda i: (0, 0))
    dh2, dg, dwg, dwgb, dwp, dwpb = pl.pallas_call(
        _ordered_after(body, 8, after), name=name, grid=(s // t,),
        in_specs=[row(D_MODEL), row(D_MODEL), row(PLE_DIM), _resident((PLE_DIM, D_MODEL)), row(D_MODEL),
                  full(1, D_MODEL), _resident((D_MODEL, D_MODEL)), row(D_MODEL)]
        + [pl.BlockSpec(memory_space=pl.ANY)] * len(after),
        out_specs=[row(D_MODEL), full(1, D_MODEL), full(D_MODEL, D_MODEL), full(D_MODEL, D_MODEL),
                   full(PLE_DIM, D_MODEL), full(PLE_DIM, D_MODEL)],
        out_shape=[jax.ShapeDtypeStruct((s, D_MODEL), F32), jax.ShapeDtypeStruct((1, D_MODEL), F32),
                   jax.ShapeDtypeStruct((D_MODEL, D_MODEL), F32), jax.ShapeDtypeStruct((D_MODEL, D_MODEL), BF16),
                   jax.ShapeDtypeStruct((PLE_DIM, D_MODEL), F32), jax.ShapeDtypeStruct((PLE_DIM, D_MODEL), BF16)],
        compiler_params=_params(1),
    )(dh, gate, pb, w_ple, h, g, w_gate, hn, *after)
    return dh2, dg, (dwg, dwgb), (dwp, dwpb)


def _mlp_bwd(dh, r, h, g, w_up, w_down, name):
    s = h.shape[0]
    t = _row_tile(s, MLP_BWD_TILE)
    nblk = D_FF // FF_BLOCK

    def body(dh_ref, r_ref, h_ref, g_ref, wu_ref, wd_ref, out_ref, dup_ref, dg_ref, dhb_ref):
        @pl.when(pl.program_id(0) == 0)
        def _():
            dg_ref[...] = jnp.zeros_like(dg_ref)

        d = dh_ref[...]
        db = d.astype(BF16)
        dhb_ref[...] = db
        back = None
        for b in range(nblk):
            cols = slice(b * FF_BLOCK, (b + 1) * FF_BLOCK)
            dup = (_dot_nt(db, wd_ref[b]) * (2.0 * r_ref[:, cols].astype(F32))).astype(BF16)
            dup_ref[:, cols] = dup
            part = _dot_nt(dup, wu_ref[b])
            back = part if back is None else back + part
        gv = g_ref[...]
        n, rstd, _ = _rms(h_ref[...], gv)
        dx, dg = _rms_bwd(back, n, rstd, gv)
        out_ref[...] = d + dx
        dg_ref[...] += dg

    row = lambda w: pl.BlockSpec((t, w), lambda i: (i, 0))
    vec = pl.BlockSpec((1, D_MODEL), lambda i: (0, 0))
    resident = lambda shape: pl.BlockSpec(shape, lambda i: (0, 0, 0), pipeline_mode=pl.Buffered(1))
    return pl.pallas_call(
        body, name=name, grid=(s // t,),
        in_specs=[row(D_MODEL), row(D_FF), row(D_MODEL), vec,
                  resident((nblk, D_MODEL, FF_BLOCK)), resident((nblk, FF_BLOCK, D_MODEL))],
        out_specs=[row(D_MODEL), row(D_FF), vec, row(D_MODEL)],
        out_shape=[jax.ShapeDtypeStruct((s, D_MODEL), F32), jax.ShapeDtypeStruct((s, D_FF), BF16),
                   jax.ShapeDtypeStruct((1, D_MODEL), F32), jax.ShapeDtypeStruct((s, D_MODEL), BF16)],
        compiler_params=_params(1),
    )(dh, r, h, g, w_up, w_down)


def _outproj_bwd(dh, w_out, o, lse, ones_bd, a, name):
    s = dh.shape[0]
    t = _row_tile(s, 512)
    last = s // t - 1

    def body(dh_ref, w_ref, o0, o1, o2, l0, l1, l2, bd_ref, a_ref, dp_ref, do0, do1, do2, de0, de1, de2, dw_ref,
             dwb_ref, *stages):
        i = pl.program_id(0)

        @pl.when(i == 0)
        def _():
            dw_ref[...] = jnp.zeros_like(dw_ref)

        stages = _pair_stages(stages)
        dhb = dh_ref[...].astype(BF16)
        dw_ref[...] += _dot_tn(a_ref[...], dhb)

        @pl.when(i == last)
        def _():
            dwb_ref[...] = dw_ref[...].astype(BF16)

        da = _dot_nt(dhb, w_ref[...])
        dp_ref[...] = da[:, 0:POOL_WIDTH]
        ov =[_from_residues(r, stages[i], DILATIONS[i]) for i, r in enumerate((o0, o1, o2))]
        lv = [_from_residues(r, stages[3 + i], DILATIONS[i]) for i, r in enumerate((l0, l1, l2))]
        wts = _group_weights(*lv)
        bd = bd_ref[...]
        cbar = jnp.zeros((t, GROUP_WIDTH), F32)
        for grp, do_ref in enumerate((do0, do1, do2)):
            lo = POOL_WIDTH + grp * GROUP_WIDTH
            dag = da[:, lo:lo + GROUP_WIDTH]
            _to_residues(dag * wts[grp], stages[6 + grp], do_ref, DILATIONS[grp])
            prod = dag * ov[grp]
            hi = prod.astype(BF16)
            low = (prod - hi.astype(F32)).astype(BF16)
            cbar = cbar + wts[grp] * (_dot(hi, bd) + _dot(low, bd))
        for grp, de_ref in enumerate((de0, de1, de2)):
            _to_residues(wts[grp] * cbar, stages[9 + grp], de_ref, DILATIONS[grp])

    row = lambda w: pl.BlockSpec((t, w), lambda i: (i, 0))
    full = lambda a, b: pl.BlockSpec((a, b), lambda i: (0, 0))
    res = [_residue_spec(dil, t) for dil in DILATIONS]
    *outs, dw, dwb = pl.pallas_call(
        body, name=name, grid=(s // t,),
        in_specs=[row(D_MODEL), full(D_MODEL, D_MODEL)] + res + res + [full(GROUP_WIDTH, GROUP_WIDTH), row(D_MODEL)],
        out_specs=[row(POOL_WIDTH)] + res + res + [full(D_MODEL, D_MODEL)] * 2,
        out_shape=[jax.ShapeDtypeStruct((s, POOL_WIDTH), F32)] + [_residue_shape(dil, s, BF16) for dil in DILATIONS]
        + [_residue_shape(dil, s, F32) for dil in DILATIONS]
        + [jax.ShapeDtypeStruct((D_MODEL, D_MODEL), F32), jax.ShapeDtypeStruct((D_MODEL, D_MODEL), BF16)],
        scratch_shapes=_stages(t, 12),
        compiler_params=_params(1),
    )(dh, w_out, *o, *lse, ones_bd, a)
    return (*outs, (dw, dwb))


def _attn_bwd(q, k, v, do, lse, deff, name, after=()):
    dil, length, _ = q.shape
    nb = length // ATTN_BLOCK
    qb = _blocks_per_step(nb)
    nj = nb // qb
    rs = _residues_per_step(dil, nb, qb)
    whole = nj == 1
    tail = slice((qb - 1) * ATTN_BLOCK, qb * ATTN_BLOCK)
    block = lambda qi: slice(qi * ATTN_BLOCK, (qi + 1) * ATTN_BLOCK)

    def body(q_ref, kp_ref, kc_ref, vp_ref, vc_ref, do_ref, lse_ref, de_ref, dq_ref, dk_ref, dv_ref, ck, cv):
        j = pl.program_id(1)

        def compute():
            masks = _head_masks()
            bias = _band_bias(j == 0)
            for rr in range(rs):
                dkc, dvc = [], []
                for qi in range(qb):
                    here, before = block(qi), block(qi - 1)
                    kcat = jnp.concatenate([kp_ref[rr] if qi == 0 else kc_ref[rr, before], kc_ref[rr, here]], axis=0)
                    vcat = jnp.concatenate([vp_ref[rr] if qi == 0 else vc_ref[rr, before], vc_ref[rr, here]], axis=0)
                    qs = _stack_heads(q_ref[rr, here], masks)
                    dos = _stack_heads(do_ref[rr, here], masks)
                    sc = _dot_nt(qs, kcat) + bias[min(qi, 1)]
                    p = jnp.exp(sc - _column_per_head(lse_ref[rr, here]))
                    ds = (p * (_dot_nt(dos, vcat) - _column_per_head(de_ref[rr, here]))).astype(BF16)
                    dq = jnp.zeros((ATTN_BLOCK, GROUP_WIDTH), F32)
                    for hd, msk in enumerate(masks):
                        dq = jnp.where(msk, _dot(ds[block(hd)], kcat), dq)
                    dq_ref[rr, here] = dq.astype(dq_ref.dtype)
                    dkc.append(_dot_tn(ds, qs))
                    dvc.append(_dot_tn(p.astype(BF16), dos))

                for out_ref, carry, parts in ((dk_ref, ck, dkc), (dv_ref, cv, dvc)):
                    full = [parts[qi][ATTN_BLOCK:] + parts[qi + 1][0:ATTN_BLOCK] for qi in range(qb - 1)]
                    if whole:
                        for qi, val in enumerate(full + [parts[qb - 1][ATTN_BLOCK:]]):
                            out_ref[rr, block(qi)] = val.astype(out_ref.dtype)
                        continue

                    @pl.when(j > 0)
                    def _():
                        if qb > 1:
                            out_ref[0, 0:(qb - 1) * ATTN_BLOCK] = carry[0:(qb - 1) * ATTN_BLOCK].astype(out_ref.dtype)
                        out_ref[0, tail] = (carry[tail] + parts[0][0:ATTN_BLOCK]).astype(out_ref.dtype)

                    for qi, val in enumerate(full):
                        carry[block(qi)] = val
                    carry[tail] = parts[qb - 1][ATTN_BLOCK:]

        if whole:
            compute()
        else:
            pl.when(j < nj)(compute)

            @pl.when(j == nj)
            def _():
                dk_ref[0] = ck[...].astype(dk_ref.dtype)
                dv_ref[0] = cv[...].astype(dv_ref.dtype)

    step = lambda j: jnp.minimum(j, nj - 1)
    cur = pl.BlockSpec((rs, qb * ATTN_BLOCK, GROUP_WIDTH), lambda r, j: (r, step(j), 0))
    prev = pl.BlockSpec((rs, ATTN_BLOCK, GROUP_WIDTH), lambda r, j: (r, jnp.maximum(qb * step(j) - 1, 0), 0))
    late = pl.BlockSpec((rs, qb * ATTN_BLOCK, GROUP_WIDTH), lambda r, j: (r, jnp.maximum(j - 1, 0), 0))
    return pl.pallas_call(
        _ordered_after(body, 8, after), name=name, grid=(dil // rs, 1 if whole else nj + 1),
        in_specs=[cur, prev, cur, prev, cur, cur, cur, cur] + [pl.BlockSpec(memory_space=pl.ANY)] * len(after),
        out_specs=[cur, cur if whole else late, cur if whole else late],
        out_shape=[jax.ShapeDtypeStruct(q.shape, BF16)] * 3,
        scratch_shapes=[pltpu.VMEM((qb * ATTN_BLOCK, GROUP_WIDTH), F32)] * 2,
        compiler_params=_params(2),
    )(q, k, k, v, v, do, lse, deff, *after)


def _pool_bwd_tile(i, nt, dp_ref, y_ref, w_ref, sc_ref, dw_ref, dsc_ref, ext, b2, b4, b8):
    t = dp_ref.shape[0]

    @pl.when(i == 0)
    def _():
        ext[t:, :] = jnp.zeros((POOL_HALO + POOL_PAD, POOL_WIDTH), F32)
        for buf in (b2, b4):
            buf[t + POOL_HALO:, :] = jnp.zeros((POOL_PAD, POOL_WIDTH), F32)
        dw_ref[...] = jnp.zeros_like(dw_ref)
        dsc_ref[...] = jnp.zeros_like(dsc_ref)

    dp = dp_ref[...]
    yb = y_ref[...]
    w = w_ref[...]
    dsc_ref[...] += jnp.sum(dp * _dot(yb, w), axis=0, keepdims=True)
    dyo = (dp * sc_ref[...]).astype(BF16)
    dw_ref[...] += _dot_tn(yb, dyo)
    dy = _dot_nt(dyo, w)
    pos = (nt - 1 - i) * t + lax.broadcasted_iota(jnp.int32, (t, POOL_WIDTH), 0)
    gq = dy / jnp.minimum(pos + 1, _pool_lane_window()).astype(F32)
    ext[0:t, :] = gq
    du = _window_sums(ext, b2, b4, b8, t, 0, 0, 1) - dy
    ext[t:t + POOL_HALO, :] = gq[0:POOL_HALO, :]
    return du


def _normproj_bwd(dh, dpool, y, w_bd, scale, dq, dk, dv, rc, rsa, rsb, w_in, h, g, name):
    s = h.shape[0]
    t = _row_tile(s, 512)
    nt = s // t

    def body(dh_ref, dp_ref, y_ref, wbd_ref, sc_ref, q0, q1, q2, k0, k1, k2, v0, v1, v2, c_ref, sa_ref, sb_ref, w_ref,
             h_ref, g_ref, out_ref, dz_ref, dg_ref, dwbd_ref, dsc_ref, ext, b2, b4, b8, *stages):
        step = pl.program_id(0)

        @pl.when(step == 0)
        def _():
            dg_ref[...] = jnp.zeros_like(dg_ref)

        du = _pool_bwd_tile(step, nt, dp_ref, y_ref, wbd_ref, sc_ref, dwbd_ref, dsc_ref, ext, b2, b4, b8)
        c, sa, sb = c_ref[...], sa_ref[...], sb_ref[...]

        def unrot(a, scale):
            halves = [_rot_t(a[:, hf * LANES:(hf + 1) * LANES] * scale, c, sa, sb) for hf in range(2)]
            return jnp.concatenate(halves, axis=1)

        staged = _pair_stages(stages)
        tok = lambda refs, base: [_from_residues(r, staged[base + i], DILATIONS[i]) for i, r in enumerate(refs)]
        chunks = [du]
        chunks += [unrot(a, HEAD_DIM ** -0.5) for a in tok((q0, q1, q2), 0)]
        chunks += [unrot(a, 1.0) for a in tok((k0, k1, k2), 3)]
        chunks += tok((v0, v1, v2), 6)
        acc = jnp.zeros((t, D_MODEL), F32)
        for ci, ch in enumerate(chunks):
            cols = slice(ci * GROUP_WIDTH, (ci + 1) * GROUP_WIDTH)
            cb = ch.astype(BF16)
            dz_ref[:, cols] = cb
            acc = acc + _dot(cb, w_ref[cols, :])
        gv = g_ref[...]
        n, rstd, _ = _rms(h_ref[...], gv)
        dx, dg = _rms_bwd(acc, n, rstd, gv)
        out_ref[...] = dh_ref[...] + dx
        dg_ref[...] += dg

    back = lambda i: nt - 1 - i
    row = lambda w: pl.BlockSpec((t, w), lambda i: (back(i), 0))
    full = lambda a, b: pl.BlockSpec((a, b), lambda i: (0, 0))
    res = [pl.BlockSpec((dil, t // dil, GROUP_WIDTH), lambda i: (0, back(i), 0)) for dil in DILATIONS]
    tables = [pl.BlockSpec((t, LANES), functools.partial(lambda i, k: (back(i), k), k=k)) for k in range(3)]
    return pl.pallas_call(
        body, name=name, grid=(nt,),
        in_specs=[row(D_MODEL), row(POOL_WIDTH), row(POOL_WIDTH), full(POOL_WIDTH, POOL_WIDTH), full(1, POOL_WIDTH)]
        + res * 3 + tables + [full(N_IN, D_MODEL), row(D_MODEL), full(1, D_MODEL)],
        out_specs=[row(D_MODEL), row(N_IN), full(1, D_MODEL), full(POOL_WIDTH, POOL_WIDTH), full(1, POOL_WIDTH)],
        out_shape=[jax.ShapeDtypeStruct((s, D_MODEL), F32), jax.ShapeDtypeStruct((s, N_IN), BF16),
                   jax.ShapeDtypeStruct((1, D_MODEL), F32), jax.ShapeDtypeStruct((POOL_WIDTH, POOL_WIDTH), F32),
                   jax.ShapeDtypeStruct((1, POOL_WIDTH), F32)],
        scratch_shapes=[pltpu.VMEM((t + POOL_HALO + POOL_PAD, POOL_WIDTH), F32)] * 4 + _stages(t, 9),
        compiler_params=_params(1),
    )(dh, dpool, y, w_bd, scale, *dq, *dk, *dv, rc, rsa, rsb, w_in, h, g)


def _matmul_tn(a, b, name, *, square_a=False, tm=None, tn=None, blocked_out=False, after=()):
    s, m = a.shape
    n = b.shape[1]
    tk = _row_tile(s, 2048)
    tm = tm or min(m, 1024)
    tn = tn or min(n, 1024)
    assert m % tm == 0 and n % tn == 0
    nk = s // tk
    nsub = tn // FF_BLOCK if blocked_out else 1

    def body(a_ref, b_ref, o_ref, ob_ref, acc):
        k = pl.program_id(2)

        def product():
            av = a_ref[...]
            if square_a:
                av = av.astype(F32)
                av = av * av
            return _dot_tn(av.astype(BF16), b_ref[...].astype(BF16))

        def emit(total):
            if blocked_out:
                for sub in range(nsub):
                    cols = slice(sub * FF_BLOCK, (sub + 1) * FF_BLOCK)
                    o_ref[sub] = total[:, cols]
                    ob_ref[sub] = total[:, cols].astype(BF16)
            else:
                o_ref[...] = total
                ob_ref[...] = total.astype(BF16)

        if nk == 1:
            emit(product())
            return

        @pl.when(k == 0)
        def _():
            acc[...] = product()

        @pl.when((k > 0) & (k < nk - 1))
        def _():
            acc[...] += product()

        @pl.when(k == nk - 1)
        def _():
            emit(acc[...] + product())

    if blocked_out:
        shape = (n // FF_BLOCK, m, FF_BLOCK)
        out_spec = pl.BlockSpec((nsub, tm, FF_BLOCK), lambda i, j, k: (j, i, 0))
    else:
        shape = (m, n)
        out_spec = pl.BlockSpec((tm, tn), lambda i, j, k: (i, j))
    return pl.pallas_call(
        _ordered_after(body, 2, after), name=name, grid=(m // tm, n // tn, nk),
        in_specs=[pl.BlockSpec((tk, tm), lambda i, j, k: (k, i)), pl.BlockSpec((tk, tn), lambda i, j, k: (k, j))]
        + [pl.BlockSpec(memory_space=pl.ANY)] * len(after),
        out_specs=[out_spec, out_spec],
        out_shape=[jax.ShapeDtypeStruct(shape, F32), jax.ShapeDtypeStruct(shape, BF16)],
        scratch_shapes=[pltpu.VMEM((tm, tn), F32)],
        compiler_params=_params(3),
    )(a, b, *after)


def _adamw_math(w, g, m, v):
    m = ADAM_B1 * m + (1.0 - ADAM_B1) * g
    v = ADAM_B2 * v + (1.0 - ADAM_B2) * (g * g)
    m_hat = m / (1.0 - ADAM_B1 ** ADAM_STEP)
    v_hat = v / (1.0 - ADAM_B2 ** ADAM_STEP)
    delta = -ADAM_LR * (m_hat / (jnp.sqrt(v_hat) + ADAM_EPS) + ADAM_WD * w)
    return delta, m, v


def _sum_chunks_body(own0_ref, own1_ref, r0_ref, r1_ref):
    layer0 = pl.program_id(0) == 0
    g = jnp.where(layer0, own0_ref[...], own1_ref[...])
    for k in range(N_DEV - 1):
        g = g + jnp.where(layer0, r0_ref[k], r1_ref[k]).astype(F32)
    return g


def _chunk_specs(t, cols):
    rows_of = lambda layer: (lambda l, i: jnp.where(l == layer, i, 0))
    blk = pl.BlockSpec((None, t, cols), lambda l, i, me: (l, i, 0))
    own = [pl.BlockSpec((None, t, cols), functools.partial(lambda l, i, me, pick: (me[0], pick(l, i), 0), pick=rows_of(ly)))
           for ly in range(2)]
    recv = [pl.BlockSpec((N_DEV - 1, t, cols), functools.partial(lambda l, i, me, pick: (0, pick(l, i), 0), pick=rows_of(ly)))
            for ly in range(2)]
    return blk, own + recv


def _adamw_sharded(w, m, v, chunks, me, name):
    _, rows, cols = w.shape
    t = max(d for d in range(SUBLANES, min(rows, 256) + 1, SUBLANES) if rows % d == 0)

    def body(me_ref, w_ref, m_ref, v_ref, own0_ref, own1_ref, r0_ref, r1_ref, g_ref, d_ref, nm_ref, nv_ref):
        g = _sum_chunks_body(own0_ref, own1_ref, r0_ref, r1_ref)
        g_ref[...] = g
        d_ref[...], nm_ref[...], nv_ref[...] = _adamw_math(w_ref[...], g, m_ref[...], v_ref[...])

    blk, chunk_specs = _chunk_specs(t, cols)
    return pl.pallas_call(
        body, name=name,
        grid_spec=pltpu.PrefetchScalarGridSpec(
            num_scalar_prefetch=1, grid=(2, rows // t), in_specs=[blk, blk, blk] + chunk_specs, out_specs=[blk] * 4),
        out_shape=[jax.ShapeDtypeStruct(w.shape, F32)] * 4,
        compiler_params=_params(2),
    )(me, w, m, v, *chunks)


def _adamw_small(w, g8, m, v, name):
    rows = dict(norm1=(0, 2), norm2=(2, 4), norm3=(4, 6), final_norm=(6, 7), pool_w=(8, 40))
    shaped = lambda t: [t[n].reshape(rows[n][1] - rows[n][0], D_MODEL) if n in rows else t[n] for n in SMALL]
    k = len(SMALL)

    def body(g8_ref, *refs):
        w_refs, m_refs, v_refs = refs[:k], refs[k:2 * k], refs[2 * k:3 * k]
        outs = [refs[(3 + i) * k:(4 + i) * k] for i in range(4)]
        spare_ref = refs[-1]
        g = g8_ref[0]
        for dev in range(1, N_DEV):
            g = g + g8_ref[dev]
        spare_ref[...] = g[7:8, 2 * POOL_WIDTH:2 * POOL_WIDTH + LANES]
        for i, n in enumerate(SMALL):
            if n in rows:
                pieces = [(slice(None), g[rows[n][0]:rows[n][1]])]
            else:
                pieces = [(slice(ly, ly + 1), g[7:8, ly * POOL_WIDTH:(ly + 1) * POOL_WIDTH]) for ly in range(2)]
            for at, gp in pieces:
                new = _adamw_math(w_refs[i][at], gp, m_refs[i][at], v_refs[i][at])
                for out, val in zip(outs, (gp, *new)):
                    out[i][at] = val

    ins = shaped(w) + shaped(m) + shaped(v)
    res = pl.pallas_call(
        body, name=name,
        out_shape=[jax.ShapeDtypeStruct(a.shape, F32) for a in shaped(w)] * 4 + [jax.ShapeDtypeStruct((1, LANES), F32)],
        compiler_params=pltpu.CompilerParams(vmem_limit_bytes=VMEM_LIMIT),
    )(g8, *ins)
    dicts = [{n: a.reshape(w[n].shape) for n, a in zip(SMALL, res[i * k:(i + 1) * k])} for i in range(4)]
    return (*dicts, res[-1])


def _peer(k):
    x, y, c = lax.axis_index("x"), lax.axis_index("y"), lax.axis_index("c")
    return (1 - x if k & 4 else x, 1 - y if k & 2 else y, 1 - c if k & 1 else c)


def _linear(dev):
    return 4 * dev[0] + 2 * dev[1] + dev[2]


HBM_SPEC = pl.BlockSpec(memory_space=pltpu.HBM)
SEM_SPEC = pl.BlockSpec(memory_space=pltpu.SEMAPHORE)
ANY_SPEC = pl.BlockSpec(memory_space=pl.ANY)
EFFECT = pltpu.SideEffectType.DATAFLOW_SIDE_EFFECTING


def _in_hbm(a):
    return pltpu.with_memory_space_constraint(a, pltpu.HBM)


class _Exchange:
    def __init__(self, name, groups, scatter, after=()):
        self.name, self.scatter = name, scatter
        self.sizes = sizes = [len(g) for g in groups]
        srcs = [a for g in groups for a in g]
        n, ng = len(srcs), len(groups)
        lead = (N_DEV - 1,) if scatter else (N_DEV,)
        shapes = [lead + (a.shape[1:] if scatter else a.shape) for a in srcs]
        lands = [lax.empty(sh, a.dtype) for sh, a in zip(shapes, srcs)]
        if not scatter:
            lands = [lax.dynamic_update_slice_in_dim(land, a[None], _linear(_peer(0)), axis=0)
                     for land, a in zip(lands, srcs)]
        offsets = [sum(sizes[:gi]) for gi in range(ng)]
        copy = self._copy

        def body(*refs):
            src, land = refs[:n], refs[n:2 * n]
            sems = refs[2 * n + len(after):2 * n + len(after) + 2 * ng]
            token = refs[-1]
            for gi in range(ng):
                for wi in range(sizes[gi]):
                    w = offsets[gi] + wi
                    for k in range(1, N_DEV):
                        copy(src[w], land[w], sems[2 * gi], sems[2 * gi + 1], wi, k).start()
            token[...] = jnp.zeros_like(token)

        sem_shapes = [pltpu.SemaphoreType.DMA(((N_DEV - 1) * sz,)) for sz in sizes for _ in range(2)]
        outs = pl.pallas_call(
            body, name=name + "_start",
            in_specs=[HBM_SPEC] * (2 * n) + [ANY_SPEC] * len(after),
            out_specs=[SEM_SPEC] * (2 * ng) + [HBM_SPEC] * (2 * n) + [pl.BlockSpec(memory_space=pltpu.VMEM)],
            out_shape=sem_shapes + [pltpu.HBM(a.shape, a.dtype) for a in srcs + lands]
            + [jax.ShapeDtypeStruct((8, LANES), F32)],
            input_output_aliases={i: 2 * ng + i for i in range(2 * n)},
            compiler_params=pltpu.CompilerParams(has_side_effects=EFFECT),
        )(*[_in_hbm(a) for a in srcs + lands], *after)
        self.sems = [outs[2 * gi:2 * gi + 2] for gi in range(ng)]
        thru = outs[2 * ng:2 * ng + 2 * n]
        self.srcs = [thru[offsets[gi]:offsets[gi] + sizes[gi]] for gi in range(ng)]
        self.lands = [thru[n + offsets[gi]:n + offsets[gi] + sizes[gi]] for gi in range(ng)]
        self.token = outs[-1]

    def _copy(self, src, land, send_sems, recv_sems, wi, k):
        to = _peer(k)
        if self.scatter:
            src_ref, dst_ref = src.at[_linear(to)], land.at[k - 1]
        else:
            src_ref, dst_ref = src, land.at[_linear(_peer(0))]
        return pltpu.make_async_remote_copy(
            src_ref=src_ref, dst_ref=dst_ref, send_sem=send_sems.at[(N_DEV - 1) * wi + k - 1],
            recv_sem=recv_sems.at[(N_DEV - 1) * wi + k - 1], device_id=to, device_id_type=MESH)

    def wait(self, gi, after):
        n = self.sizes[gi]
        copy = self._copy

        def body(*refs):
            src, land = refs[:n], refs[n:2 * n]
            send_sems, recv_sems = refs[2 * n], refs[2 * n + 1]
            for wi in range(n):
                for k in range(1, N_DEV):
                    cp = copy(src[wi], land[wi], send_sems, recv_sems, wi, k)
                    cp.wait_send()
                    cp.wait_recv()

        arrays = list(self.srcs[gi]) + list(self.lands[gi])
        outs = pl.pallas_call(
            body, name=f"{self.name}_wait{gi}",
            in_specs=[HBM_SPEC] * (2 * n) + [SEM_SPEC, SEM_SPEC] + [ANY_SPEC] * len(after),
            out_specs=[HBM_SPEC] * (2 * n),
            out_shape=[pltpu.HBM(a.shape, a.dtype) for a in arrays],
            input_output_aliases={i: i for i in range(2 * n)},
            compiler_params=pltpu.CompilerParams(has_side_effects=EFFECT),
        )(*arrays, *self.sems[gi], *after)
        return outs[:n], outs[n:]


def _rotary_tables(positions):
    rot_dim = HEAD_DIM // 4
    inv_freq = ROPE_THETA ** (-jnp.arange(0, rot_dim, 2, dtype=F32) / rot_dim)
    ang = positions.astype(F32)[:, None] * inv_freq
    cs = jnp.concatenate([jnp.cos(ang), jnp.sin(ang)], axis=1)
    dim = jnp.arange(LANES) % HEAD_DIM
    first, second = dim < ROT_SHIFT, (dim >= ROT_SHIFT) & (dim < rot_dim)
    src = jnp.arange(2 * ROT_SHIFT)[:, None]
    angle = (dim % ROT_SHIFT)[None, :]
    c = jnp.where((first | second)[None, :] & (src == angle), 1.0, 0.0)
    sa = jnp.where(second[None, :] & (src == angle + ROT_SHIFT), 1.0, 0.0)
    sb = jnp.where(first[None, :] & (src == angle + ROT_SHIFT), -1.0, 0.0)
    spread = jnp.concatenate([c, sa, sb], axis=1).astype(F32)
    base = jnp.concatenate([jnp.where(first | second, 0.0, 1.0), jnp.zeros((2 * LANES,))]).astype(F32)[None, :]
    return jnp.dot(cs, spread, precision=lax.Precision.HIGHEST, preferred_element_type=F32) + base


def _block_diag(pool_w):
    gc = pool_w.shape[-1]
    out = jnp.zeros((POOL_WIDTH, POOL_WIDTH), pool_w.dtype)
    for grp in range(pool_w.shape[0]):
        out = lax.dynamic_update_slice(out, pool_w[grp], (grp * gc, grp * gc))
    return out


def _diag_blocks(a):
    gc = POOL_WIDTH // len(POOL_WINDOWS)
    return jnp.stack([a[grp * gc:(grp + 1) * gc, grp * gc:(grp + 1) * gc] for grp in range(len(POOL_WINDOWS))])


def _local_step(x, p, positions, loss_target, norm1, pool_w, pool_scale, norm2, norm3, final_norm, weights, send):
    rc = rsa = rsb = _rotary_tables(positions)
    ones_bd = _block_diag(jnp.ones((4, HEAD_DIM, HEAD_DIM), BF16))
    w_bds = [_block_diag(pool_w[i]).astype(BF16) for i in range(2)]
    saved = []
    h = x
    for i in range(2):
        tag = f"_l{i}"
        g1, g2, g3 = norm1[i:i + 1], norm2[i:i + 1], norm3[i:i + 1]
        w_bd = w_bds[i]
        scale = pool_scale[i:i + 1]
        if i == 0:
            w_in = weights(i, "in", (h, rc, *w_bds, ones_bd))
            hn1, u, *qkv = _normproj_fwd(h, g1, w_in, rc, rsa, rsb, "normproj_fwd" + tag)
        else:
            w_in, (hn1, u, *qkv) = ahead
        qkv = [qkv[3 * grp:3 * grp + 3] for grp in range(3)]
        started = weights(i, "prefetch", (hn1,))
        o, lse = zip(*[_attn_fwd(*qkv[grp], f"attn_fwd{tag}_g{grp}", after=started) for grp in range(3)])
        w_out = weights(i, "out", o)
        h1, a, y = _outproj_fwd(h, u, w_bd, scale, o, lse, w_out, "outproj_fwd" + tag)
        w_up, w_down = weights(i, "mlp", (h1,))
        h2, hn2, r = _mlp_fwd(h1, g2, w_up, w_down, "mlp_fwd" + tag)
        w_gate, w_ple = weights(i, "gate", (h2,))
        h0 = h
        if i == 0:
            w_in_next = weights(1, "in", (h2,))
            h, hn3, gate, pb, *ahead = _gate_fwd(h2, g3, w_gate, p, i, w_ple, "gate_normproj_fwd",
                                                 follow=(norm1[1:2], w_in_next, rc, rsa, rsb))
            ahead = (w_in_next, ahead)
        else:
            hn3 = gate = pb = None
            loss, d_final, *top = _gate_fwd(h2, g3, w_gate, p, i, w_ple, "gate_loss_gate_bwd",
                                            head=(final_norm.reshape(1, D_MODEL), loss_target))
        saved.append(dict(h0=h0, hn1=hn1, qkv=qkv, y=y, o=o, lse=lse, a=a, h1=h1, hn2=hn2, r=r, h2=h2,
                          hn3=hn3, gate=gate, pb=pb, w_bd=w_bd, scale=scale, g1=g1, g2=g2, g3=g3,
                          w_in=w_in, w_out=w_out, w_up=w_up, w_down=w_down, w_gate=w_gate, w_ple=w_ple))

    grads = [None, None]
    sent = ()
    for i in (1, 0):
        tag = f"_l{i}"
        sv = saved[i]
        if i == 1:
            dh2, dg3, dw_gate, dw_ple = top
        else:
            dh2, dg3, dw_gate, dw_ple = _gate_bwd(dh, sv["gate"], sv["pb"], sv["w_ple"], sv["h2"], sv["g3"],
                                                  sv["w_gate"], sv["hn3"], "gate_bwd" + tag, after=sent)
        dh1, dup, dg2, dh2b = _mlp_bwd(dh2, sv["r"], sv["h1"], sv["g2"], sv["w_up"], sv["w_down"], "mlp_bwd" + tag)
        dw_down = _matmul_tn(sv["r"], dh2b, "dw_down" + tag, square_a=True)
        dw_up = _matmul_tn(sv["hn2"], dup, "dw_up" + tag, blocked_out=True)
        dpool, do0, do1, do2, de0, de1, de2, dw_out = _outproj_bwd(dh1, sv["w_out"], sv["o"], sv["lse"], ones_bd,
                                                                   sv["a"], "outproj_bwd" + tag)
        sent = send(i, "main", dict(w_gate=dw_gate, w_ple=dw_ple, w_down=dw_down, w_up=dw_up, w_out=dw_out))
        dqkv = [_attn_bwd(*sv["qkv"][grp], do_g, sv["lse"][grp], de_g, f"attn_bwd{tag}_g{grp}", after=sent)
                for grp, (do_g, de_g) in enumerate(((do0, de0), (do1, de1), (do2, de2)))]
        dq, dk, dv = zip(*dqkv)
        dh, dz, dg1, dw_bd, dscale = _normproj_bwd(dh1, dpool, sv["y"], sv["w_bd"], sv["scale"], dq, dk, dv, rc, rsa, rsb,
                                                   sv["w_in"], sv["h0"], sv["g1"], "normproj_bwd" + tag)
        grads[i] = dict(norm1=dg1, norm2=dg2, norm3=dg3, pool_w=_diag_blocks(dw_bd), pool_scale=dscale)
        small_sent = send(0, "small", (grads, d_final, loss)) if i == 0 else ()
        dw_in = _matmul_tn(dz, sv["hn1"], "dw_in" + tag, tm=N_IN // 2, after=small_sent)
        sent = send(i, "in", dict(w_in=dw_in))
    return dh, sent


def _pack_small(norm1, norm2, norm3, final_norm, pool_scale, pool_w, spare=None):
    spare = jnp.zeros((1, LANES), F32) if spare is None else spare
    scale_row = jnp.concatenate([pool_scale.reshape(1, 2 * POOL_WIDTH), spare,
                                 jnp.zeros((1, D_MODEL - 2 * POOL_WIDTH - LANES), F32)], axis=1)
    return jnp.concatenate([norm1, norm2, norm3, final_norm.reshape(1, D_MODEL), scale_row,
                            pool_w.reshape(32, D_MODEL)], axis=0)


def _chunks_cols(a, cols):
    return a.reshape(a.shape[0], N_DEV, cols).transpose(1, 0, 2)


def _chunks_rows(a, rows):
    return a.reshape(N_DEV, rows, a.shape[1])


BIG = ("w_in", "w_out", "w_up", "w_down", "w_gate", "w_ple")
SMALL = ("norm1", "norm2", "norm3", "final_norm", "pool_scale", "pool_w")
ORDER = ("norm1", "w_in", "pool_w", "pool_scale", "w_out", "norm2", "w_up", "w_down", "norm3", "w_gate", "w_ple",
         "final_norm")


def kernel(x, p, positions, norm1, w_in, pool_w, pool_scale, w_out, norm2, w_up, w_down, norm3, w_gate, w_ple, final_norm, loss_target, m_norm1, m_w_in, m_pool_w, m_pool_scale, m_w_out, m_norm2, m_w_up, m_w_down, m_norm3, m_w_gate, m_w_ple, m_final_norm, v_norm1, v_w_in, v_pool_w, v_pool_scale, v_w_out, v_norm2, v_w_up, v_w_down, v_norm3, v_w_gate, v_w_ple, v_final_norm):
    w = dict(norm1=norm1, w_in=w_in, pool_w=pool_w, pool_scale=pool_scale, w_out=w_out, norm2=norm2, w_up=w_up,
             w_down=w_down, norm3=norm3, w_gate=w_gate, w_ple=w_ple, final_norm=final_norm)
    m = dict(norm1=m_norm1, w_in=m_w_in, pool_w=m_pool_w, pool_scale=m_pool_scale, w_out=m_w_out, norm2=m_norm2,
             w_up=m_w_up, w_down=m_w_down, norm3=m_norm3, w_gate=m_w_gate, w_ple=m_w_ple, final_norm=m_final_norm)
    v = dict(norm1=v_norm1, w_in=v_w_in, pool_w=v_pool_w, pool_scale=v_pool_scale, w_out=v_w_out, norm2=v_norm2,
             w_up=v_w_up, w_down=v_w_down, norm3=v_norm3, w_gate=v_w_gate, w_ple=v_w_ple, final_norm=v_final_norm)
    seq = x.shape[1]

    bf = {n: [w[n][layer].astype(BF16) for layer in range(2)] for n in BIG}
    bf["w_in"] = [a.T for a in bf["w_in"]]
    me = 4 * lax.axis_index("x") + 2 * lax.axis_index("y") + lax.axis_index("c")
    parts = dict(zip(("in", "out", "mlp", "gate"), (("w_in",), ("w_out",), ("w_up", "w_down"), ("w_gate", "w_ple"))))
    first = _Exchange("gather_first", [[bf["w_in"][0]]], scatter=False)
    later = [pt for pt in parts if pt != "in"]
    gathers = [_Exchange("gather_l0", [[bf[n][0] for n in parts[pt]] for pt in later], scatter=False,
                         after=(first.token,))]
    unpack = dict(w_in=lambda a: a.reshape(N_IN, D_MODEL),
                  w_out=lambda a: a.reshape(D_MODEL, D_MODEL), w_gate=lambda a: a.reshape(D_MODEL, D_MODEL),
                  w_ple=lambda a: a.transpose(1, 0, 2).reshape(PLE_DIM, D_MODEL), w_up=lambda a: a, w_down=lambda a: a)

    def weights(layer, part, after):
        if part == "prefetch":
            if layer != 0:
                return ()
            gathers.append(_Exchange("gather_l1", [[bf[n][1] for n in parts[pt]] for pt in parts], scatter=False,
                                     after=after))
            return (gathers[1].token,)
        if layer == 0 and part == "in":
            _, lands = first.wait(0, (*after, gathers[0].token))
        elif layer == 0:
            _, lands = gathers[0].wait(later.index(part), after)
        else:
            _, lands = gathers[1].wait(tuple(parts).index(part), after)
        full = [unpack[n](land) for n, land in zip(parts[part], lands)]
        return full if len(full) > 1 else full[0]

    to_chunks = dict(w_in=lambda a: _chunks_rows(a, N_IN // N_DEV),
                     w_out=lambda a: _chunks_rows(a, D_MODEL // N_DEV),
                     w_up=lambda a: a, w_down=lambda a: _chunks_rows(a, FF_BLOCK),
                     w_gate=lambda a: _chunks_rows(a, D_MODEL // N_DEV), w_ple=lambda a: _chunks_cols(a, D_MODEL // N_DEV))
    own = {n: [None, None] for n in BIG}
    scatters = {}

    def send(layer, part, grads):
        if part == "small":
            per_layer, d_final, loss = grads
            pack = _pack_small(
                *[jnp.concatenate([per_layer[0][n], per_layer[1][n]], axis=0) for n in ("norm1", "norm2", "norm3")],
                d_final.reshape(D_MODEL),
                jnp.concatenate([per_layer[0]["pool_scale"], per_layer[1]["pool_scale"]], axis=0),
                jnp.stack([per_layer[0]["pool_w"], per_layer[1]["pool_w"]]), spare=loss)
            scatters["small"] = _Exchange("gather_small", [[pack]], scatter=False)
            return (scatters["small"].token,)
        for n, (g32, _) in grads.items():
            own[n][layer] = to_chunks[n](g32)
        ex = _Exchange(f"scatter_{part}_l{layer}", [[to_chunks[n](g16) for n, (_, g16) in grads.items()]], scatter=True)
        scatters[layer, part] = (tuple(grads), ex)
        return (ex.token,)

    _, pool_w_late = lax.optimization_barrier((first.token, pool_w))
    dx, sent = _local_step(
        x.reshape(seq, D_MODEL), p.reshape(2, seq, PLE_DIM), positions.reshape(seq), loss_target.reshape(seq, D_MODEL),
        norm1, pool_w_late, pool_scale, norm2, norm3, final_norm, weights, send)

    g_out, d_out, m_out, v_out = {}, {}, {}, {}
    my_index = me.reshape(1)
    for part in ("main", "in"):
        recv = {}
        for layer in (1, 0):
            names, ex = scatters[layer, part]
            for n, r in zip(names, ex.wait(0, sent)[1]):
                recv[n, layer] = r
        sent = ()
        for n in names:
            grad = (*own[n], recv[n, 0], recv[n, 1])
            turn = (lambda a: a.transpose(0, 2, 1)) if n == "w_in" else (lambda a: a)
            updated = _adamw_sharded(turn(w[n]), turn(m[n]), turn(v[n]), grad, my_index, "adamw_" + n)
            g_out[n], d_out[n], m_out[n], v_out[n] = map(turn, updated)
            sent += (updated[1],)
    _, (small_g8,) = scatters["small"].wait(0, sent)
    *small, spare = _adamw_small(w, small_g8, m, v, "adamw_small")
    for dst, a in zip((g_out, d_out, m_out, v_out), small):
        dst.update(a)

    return (spare[0, 0], dx.reshape(1, seq, D_MODEL), *[g_out[n] for n in ORDER], *[d_out[n] for n in ORDER],
            *[m_out[n] for n in ORDER], *[v_out[n] for n in ORDER])
```

```python
import functools

import jax
import jax.numpy as jnp
from jax import lax
from jax.experimental import pallas as pl
from jax.experimental.pallas import tpu as pltpu

F32 = jnp.float32
BF16 = jnp.bfloat16

D_MODEL = 1024
HEAD_DIM = 64
POOL_WIDTH = 256
POOL_WINDOWS = (2, 4, 8, 16)
POOL_HALO = 16
POOL_PAD = 8
GROUP_WIDTH = 256
DILATIONS = (1, 4, 16)
ATTN_BLOCK = 128
ROT_SHIFT = 8
ROPE_THETA = 500000.0
D_FF = 4096
FF_BLOCK = 512
FF_PER_STEP = 2
MLP_BWD_TILE = 512
FWD_TILE = 1024
N_DEV = 8
N_IN = POOL_WIDTH + 3 * 768
PLE_DIM = 256
EPS = 1e-6
NEG_BIG = -1e30

ADAM_LR = 0.001
ADAM_B1 = 0.9
ADAM_B2 = 0.999
ADAM_EPS = 1e-08
ADAM_WD = 0.01
ADAM_STEP = 10

LANES = 128
SUBLANES = 8
VMEM_LIMIT = 56 * 1024 * 1024
MESH = pl.DeviceIdType.MESH


def _params(n_grid):
    return pltpu.CompilerParams(dimension_semantics=("arbitrary",) * n_grid, vmem_limit_bytes=VMEM_LIMIT)


def _dot(a, b):
    return jnp.dot(a, b, preferred_element_type=F32)


def _dot_nt(a, b):
    return lax.dot_general(a, b, (((1,), (1,)), ((), ())), preferred_element_type=F32)


def _dot_tn(a, b):
    return lax.dot_general(a, b, (((0,), (0,)), ((), ())), preferred_element_type=F32)


def _rms(x, g):
    rstd = lax.rsqrt(jnp.mean(x * x, axis=-1, keepdims=True) + EPS)
    n = x * rstd
    return n, rstd, n * g


def _rms_bwd(dy, n, rstd, g):
    dyn = dy * g
    dx = rstd * (dyn - n * jnp.mean(dyn * n, axis=-1, keepdims=True))
    return dx, jnp.sum(dy * n, axis=0, keepdims=True)


def _ordered_after(body, n_in, after):
    if not after:
        return body
    return lambda *refs: body(*refs[:n_in], *refs[n_in + len(after):])


def _resident(shape):
    return pl.BlockSpec(shape, lambda i: (0,) * len(shape), pipeline_mode=pl.Buffered(1))


def _row_tile(s, t):
    t = min(s, t)
    assert s % t == 0
    return t


def _rot(z, c, sa, sb):
    return z * c + pltpu.roll(z, ROT_SHIFT, 1) * sa + pltpu.roll(z, LANES - ROT_SHIFT, 1) * sb


def _table_specs(t):
    return [pl.BlockSpec((t, LANES), functools.partial(lambda i, k: (i, k), k=k)) for k in range(3)]


def _rot_t(dz, c, sa, sb):
    return dz * c + pltpu.roll(dz * sa, LANES - ROT_SHIFT, 1) + pltpu.roll(dz * sb, ROT_SHIFT, 1)


def _to_residues(value, stage, out_ref, dil):
    if dil == 1:
        out_ref[0] = value.astype(out_ref.dtype)
        return
    rows = value.shape[0] // dil
    for hf in range(GROUP_WIDTH // LANES):
        lanes = slice(hf * LANES, (hf + 1) * LANES)
        stage[hf][...] = value[:, lanes]
        for r in range(dil):
            out_ref[r, :, lanes] = stage[hf][pl.ds(r, rows, stride=dil), :].astype(out_ref.dtype)


def _from_residues(in_ref, stage, dil):
    if dil == 1:
        return in_ref[0].astype(F32)
    rows = in_ref.shape[1]
    for hf in range(GROUP_WIDTH // LANES):
        for r in range(dil):
            stage[hf][pl.ds(r, rows, stride=dil), :] = in_ref[r, :, hf * LANES:(hf + 1) * LANES].astype(F32)
    return jnp.concatenate([stage[0][...], stage[1][...]], axis=1)


def _residue_spec(dil, t):
    return pl.BlockSpec((dil, t // dil, GROUP_WIDTH), lambda i: (0, i, 0))


def _residue_shape(dil, s, dtype):
    return jax.ShapeDtypeStruct((dil, s // dil, GROUP_WIDTH), dtype)


def _stages(t, n):
    return [pltpu.VMEM((t, LANES), F32)] * (n * (GROUP_WIDTH // LANES))


def _pair_stages(refs):
    return [refs[i:i + 2] for i in range(0, len(refs), 2)]


def _normproj_tile(x, g_ref, w_ref, c_ref, sa_ref, sb_ref, hn_ref, u_ref, *rest):
    qkv_refs, stages = rest[:9], _pair_stages(rest[9:])
    _, _, hn = _rms(x, g_ref[...])
    hb = hn.astype(BF16)
    hn_ref[...] = hb
    c, sa, sb = c_ref[...], sa_ref[...], sb_ref[...]

    def rot(z, scale):
        halves = [_rot(z[:, hf * LANES:(hf + 1) * LANES], c, sa, sb) * scale for hf in range(2)]
        return jnp.concatenate(halves, axis=1)

    proj = lambda lo: _dot_nt(hb, w_ref[lo:lo + GROUP_WIDTH, :])
    u_ref[...] = proj(0)
    for grp, dil in enumerate(DILATIONS):
        lo = POOL_WIDTH + grp * GROUP_WIDTH
        q_ref, k_ref, v_ref = qkv_refs[3 * grp:3 * grp + 3]
        _to_residues(rot(proj(lo), HEAD_DIM ** -0.5), stages[0], q_ref, dil)
        _to_residues(rot(proj(lo + 768), 1.0), stages[1], k_ref, dil)
        _to_residues(proj(lo + 1536), stages[2], v_ref, dil)


def _normproj_operands(s, t):
    row = lambda w: pl.BlockSpec((t, w), lambda i: (i, 0))
    in_specs = [pl.BlockSpec((1, D_MODEL), lambda i: (0, 0)), _resident((N_IN, D_MODEL))] + _table_specs(t)
    out_specs = [row(D_MODEL), row(POOL_WIDTH)] + [_residue_spec(dil, t) for dil in DILATIONS for _ in range(3)]
    out_shape = [jax.ShapeDtypeStruct((s, D_MODEL), BF16), jax.ShapeDtypeStruct((s, POOL_WIDTH), F32)]
    out_shape += [_residue_shape(dil, s, BF16) for dil in DILATIONS for _ in range(3)]
    return in_specs, out_specs, out_shape, _stages(t, 3)


def _normproj_fwd(h, g, w_in, rc, rsa, rsb, name):
    s = h.shape[0]
    t = _row_tile(s, FWD_TILE)

    def body(h_ref, *refs):
        _normproj_tile(h_ref[...], *refs)

    in_specs, out_specs, out_shape, scratch = _normproj_operands(s, t)
    return pl.pallas_call(
        body, name=name, grid=(s // t,), in_specs=[pl.BlockSpec((t, D_MODEL), lambda i: (i, 0))] + in_specs,
        out_specs=out_specs, out_shape=out_shape, scratch_shapes=scratch, compiler_params=_params(1),
    )(h, g, w_in, rc, rsa, rsb)


def _pool_lane_window():
    lane = lax.broadcasted_iota(jnp.int32, (1, POOL_WIDTH), 1)
    return jnp.left_shift(2, lane // (POOL_WIDTH // len(POOL_WINDOWS)))


def _window_sums(ext, b2, b4, b8, t, lo, tile, direction):
    rows = t + POOL_HALO
    for src, dst, sh in ((ext, b2, 1), (b2, b4, 2), (b4, b8, 4)):
        dst[lo:lo + rows, :] = src[lo:lo + rows, :] + src[lo + direction * sh:lo + direction * sh + rows, :]
    s16 = b8[tile:tile + t, :] + b8[tile + direction * 8:tile + direction * 8 + t, :]
    win = _pool_lane_window()
    return jnp.where(win == 2, b2[tile:tile + t, :],
                     jnp.where(win == 4, b4[tile:tile + t, :], jnp.where(win == 8, b8[tile:tile + t, :], s16)))


def _pool_fwd_tile(i, u_ref, w_ref, sc_ref, y_ref, ext, b2, b4, b8):
    t = u_ref.shape[0]
    first = POOL_PAD + POOL_HALO

    @pl.when(i == 0)
    def _():
        for buf in (ext, b2, b4):
            buf[0:POOL_PAD, :] = jnp.zeros((POOL_PAD, POOL_WIDTH), F32)
        ext[POOL_PAD:first, :] = jnp.zeros((POOL_HALO, POOL_WIDTH), F32)

    x = u_ref[...]
    ext[first:, :] = x
    wsum = _window_sums(ext, b2, b4, b8, t, POOL_PAD, first, -1)
    pos = i * t + lax.broadcasted_iota(jnp.int32, (t, POOL_WIDTH), 0)
    cnt = jnp.minimum(pos + 1, _pool_lane_window()).astype(F32)
    yb = (wsum / cnt - x).astype(BF16)
    y_ref[...] = yb
    ext[POOL_PAD:first, :] = x[t - POOL_HALO:, :]
    return _dot(yb, w_ref[...]) * sc_ref[...]


def _head_masks():
    lane = lax.broadcasted_iota(jnp.int32, (ATTN_BLOCK, GROUP_WIDTH), 1)
    return [lane // HEAD_DIM == hd for hd in range(GROUP_WIDTH // HEAD_DIM)]


def _stack_heads(a, masks):
    zero = jnp.zeros_like(a)
    return jnp.concatenate([jnp.where(m, a, zero) for m in masks], axis=0)


def _band_bias(first_step):
    rows = ATTN_BLOCK * (GROUP_WIDTH // HEAD_DIM)
    i = lax.broadcasted_iota(jnp.int32, (rows, 2 * ATTN_BLOCK), 0) & (ATTN_BLOCK - 1)
    j = lax.broadcasted_iota(jnp.int32, (rows, 2 * ATTN_BLOCK), 1)
    inner = jnp.where((j >= i) & (j <= i + ATTN_BLOCK), 0.0, NEG_BIG)
    return jnp.where((j < ATTN_BLOCK) & first_step, NEG_BIG, inner), inner


def _column_per_head(a):
    return jnp.concatenate([a[:, hd * HEAD_DIM:hd * HEAD_DIM + 1] for hd in range(GROUP_WIDTH // HEAD_DIM)], axis=0)


def _blocks_per_step(nb):
    if nb <= 16:
        return nb
    return next(qb for qb in (16, 8, 4, 2, 1) if nb % qb == 0)


def _residues_per_step(dil, nb, qb):
    return 2 if (nb == qb and qb < 8 and dil % 2 == 0) else 1


def _attn_fwd(q, k, v, name, after=()):
    dil, length, _ = q.shape
    nb = length // ATTN_BLOCK
    qb = _blocks_per_step(nb)
    rs = _residues_per_step(dil, nb, qb)

    def body(q_ref, kp_ref, kc_ref, vp_ref, vc_ref, o_ref, lse_ref):
        masks = _head_masks()
        bias = _band_bias(pl.program_id(1) == 0)
        for rr in range(rs):
            for qi in range(qb):
                here = slice(qi * ATTN_BLOCK, (qi + 1) * ATTN_BLOCK)
                before = slice((qi - 1) * ATTN_BLOCK, qi * ATTN_BLOCK)
                kcat = jnp.concatenate([kp_ref[rr] if qi == 0 else kc_ref[rr, before], kc_ref[rr, here]], axis=0)
                vcat = jnp.concatenate([vp_ref[rr] if qi == 0 else vc_ref[rr, before], vc_ref[rr, here]], axis=0)
                qs = _stack_heads(q_ref[rr, here], masks)
                sc = _dot_nt(qs, kcat) + bias[min(qi, 1)]
                m = jnp.max(sc, axis=1, keepdims=True)
                e = jnp.exp(sc - m)
                l = jnp.sum(e, axis=1, keepdims=True)
                p = (e / l).astype(BF16)
                lse = m + jnp.log(l)
                o = jnp.zeros((ATTN_BLOCK, GROUP_WIDTH), F32)
                lse_full = jnp.zeros((ATTN_BLOCK, GROUP_WIDTH), F32)
                for hd, msk in enumerate(masks):
                    rows = slice(hd * ATTN_BLOCK, (hd + 1) * ATTN_BLOCK)
                    o = jnp.where(msk, _dot(p[rows], vcat), o)
                    lse_full = jnp.where(msk, lse[rows], lse_full)
                o_ref[rr, here] = o.astype(o_ref.dtype)
                lse_ref[rr, here] = lse_full

    cur = pl.BlockSpec((rs, qb * ATTN_BLOCK, GROUP_WIDTH), lambda r, j: (r, j, 0))
    prev = pl.BlockSpec((rs, ATTN_BLOCK, GROUP_WIDTH), lambda r, j: (r, jnp.maximum(qb * j - 1, 0), 0))
    return pl.pallas_call(
        _ordered_after(body, 5, after), name=name, grid=(dil // rs, nb // qb),
        in_specs=[cur, prev, cur, prev, cur] + [pl.BlockSpec(memory_space=pl.ANY)] * len(after), out_specs=[cur, cur],
        out_shape=[jax.ShapeDtypeStruct(q.shape, BF16), jax.ShapeDtypeStruct(q.shape, F32)],
        compiler_params=_params(2),
    )(q, k, k, v, v, *after)


def _group_weights(l0, l1, l2):
    m = jnp.maximum(jnp.maximum(l0, l1), l2)
    e0, e1, e2 = jnp.exp(l0 - m), jnp.exp(l1 - m), jnp.exp(l2 - m)
    den = e0 + e1 + e2
    return e0 / den, e1 / den, e2 / den


def _outproj_fwd(h, u, w_bd, scale, o, lse, w_out, name):
    s = h.shape[0]
    t = _row_tile(s, FWD_TILE)

    def body(h_ref, u_ref, wbd_ref, sc_ref, o0, o1, o2, l0, l1, l2, w_ref, out_ref, a_ref, y_ref, ext, b2, b4, b8,
             *stages):
        pool_out = _pool_fwd_tile(pl.program_id(0), u_ref, wbd_ref, sc_ref, y_ref, ext, b2, b4, b8)
        stages = _pair_stages(stages)
        ov = [_from_residues(r, stages[i], DILATIONS[i]) for i, r in enumerate((o0, o1, o2))]
        lv = [_from_residues(r, stages[3 + i], DILATIONS[i]) for i, r in enumerate((l0, l1, l2))]
        wts = _group_weights(*lv)
        a = jnp.concatenate([pool_out] + [ov[i] * wts[i] for i in range(3)], axis=1).astype(BF16)
        a_ref[...] = a
        out_ref[...] = h_ref[...] + _dot(a, w_ref[...])

    row = lambda w: pl.BlockSpec((t, w), lambda i: (i, 0))
    res = [_residue_spec(dil, t) for dil in DILATIONS]
    return pl.pallas_call(
        body, name=name, grid=(s // t,),
        in_specs=[row(D_MODEL), row(POOL_WIDTH), _resident((POOL_WIDTH, POOL_WIDTH)), _resident((1, POOL_WIDTH))]
        + res + res + [_resident((D_MODEL, D_MODEL))],
        out_specs=[row(D_MODEL), row(D_MODEL), row(POOL_WIDTH)],
        out_shape=[jax.ShapeDtypeStruct((s, D_MODEL), F32), jax.ShapeDtypeStruct((s, D_MODEL), BF16),
                   jax.ShapeDtypeStruct((s, POOL_WIDTH), BF16)],
        scratch_shapes=[pltpu.VMEM((t + POOL_HALO + POOL_PAD, POOL_WIDTH), F32)] * 4 + _stages(t, 6),
        compiler_params=_params(1),
    )(h, u, w_bd, scale, *o, *lse, w_out)


def _mlp_fwd(h, g, w_up, w_down, name):
    s = h.shape[0]
    t = _row_tile(s, 512)
    nblk = D_FF // FF_BLOCK

    def body(h_ref, g_ref, wu_ref, wd_ref, out_ref, hn_ref, r_ref):
        x = h_ref[...]
        _, _, hn = _rms(x, g_ref[...])
        hb = hn.astype(BF16)
        hn_ref[...] = hb
        acc = None
        for b0 in range(0, nblk, FF_PER_STEP):
            acts = []
            for b in range(b0, b0 + FF_PER_STEP):
                r = jnp.maximum(_dot(hb, wu_ref[b]), 0.0)
                r_ref[:, b * FF_BLOCK:(b + 1) * FF_BLOCK] = r.astype(BF16)
                acts.append((r * r).astype(BF16))
            wd = wd_ref[b0:b0 + FF_PER_STEP].reshape(FF_PER_STEP * FF_BLOCK, D_MODEL)
            part = _dot(jnp.concatenate(acts, axis=1), wd)
            acc = part if acc is None else acc + part
        out_ref[...] = x + acc

    row = lambda w: pl.BlockSpec((t, w), lambda i: (i, 0))
    resident = lambda shape: pl.BlockSpec(shape, lambda i: (0, 0, 0), pipeline_mode=pl.Buffered(1))
    return pl.pallas_call(
        body, name=name, grid=(s // t,),
        in_specs=[row(D_MODEL), pl.BlockSpec((1, D_MODEL), lambda i: (0, 0)),
                  resident((nblk, D_MODEL, FF_BLOCK)), resident((nblk, FF_BLOCK, D_MODEL))],
        out_specs=[row(D_MODEL), row(D_MODEL), row(D_FF)],
        out_shape=[jax.ShapeDtypeStruct((s, D_MODEL), F32), jax.ShapeDtypeStruct((s, D_MODEL), BF16),
                   jax.ShapeDtypeStruct((s, D_FF), BF16)],
        compiler_params=_params(1),
    )(h, g, w_up, w_down)


def _gate_fwd(h, g, w_gate, p, layer, w_ple, name, head=None, follow=None):
    assert (head is None) != (follow is None)
    s = h.shape[0]
    t = _row_tile(s, 512)
    last = s // t - 1

    def body(h_ref, g_ref, wg_ref, p_ref, wp_ref, *refs):
        x = h_ref[...]
        gv = g_ref[...]
        n, rstd, hn = _rms(x, gv)
        hb = hn.astype(BF16)
        gate = 1.0 / (1.0 + jnp.exp(-_dot(hb, wg_ref[...])))
        pb = p_ref[...].astype(BF16)
        e = _dot(pb, wp_ref[...])
        h3 = x + gate * e
        if follow is not None:
            out_ref, hn_ref, gate_ref, pb_ref = refs[5:9]
            out_ref[...] = h3
            hn_ref[...] = hb
            pb_ref[...] = pb
            gate_ref[...] = gate.astype(BF16)
            _normproj_tile(h3, *refs[:5], *refs[9:])
            return
        gf_ref, t_ref, loss_ref, dgf_ref, out_ref, dg_ref, dwg_ref, dwgb_ref, dwp_ref, dwpb_ref = refs
        i = pl.program_id(0)

        @pl.when(i == 0)
        def _():
            for ref in (loss_ref, dgf_ref, dg_ref, dwg_ref, dwp_ref):
                ref[...] = jnp.zeros_like(ref)

        gf = gf_ref[...]
        n3, rstd3, y = _rms(h3, gf)
        err = y - t_ref[...]
        loss_ref[...] += jnp.sum(err * err) * (0.5 / D_MODEL)
        d, dgf = _rms_bwd(err * (1.0 / D_MODEL), n3, rstd3, gf)
        dgf_ref[...] += dgf
        dgl = (d * e * gate * (1.0 - gate)).astype(BF16)
        dwg_ref[...] += _dot_tn(hb, dgl)
        dwp_ref[...] += _dot_tn(pb, (d * gate).astype(BF16))
        dx, dg = _rms_bwd(_dot_nt(dgl, wg_ref[...]), n, rstd, gv)
        out_ref[...] = d + dx
        dg_ref[...] += dg

        @pl.when(i == last)
        def _():
            dwgb_ref[...] = dwg_ref[...].astype(BF16)
            dwpb_ref[...] = dwp_ref[...].astype(BF16)

    row = lambda w: pl.BlockSpec((t, w), lambda i: (i, 0))
    full = lambda a, b: pl.BlockSpec((a, b), lambda i: (0, 0))
    in_specs = [row(D_MODEL), full(1, D_MODEL), _resident((D_MODEL, D_MODEL)),
                pl.BlockSpec((None, t, PLE_DIM), lambda i: (layer, i, 0)), _resident((PLE_DIM, D_MODEL))]
    if follow is not None:
        next_in, next_out, next_shape, scratch = _normproj_operands(s, t)
        return pl.pallas_call(
            body, name=name, grid=(s // t,), in_specs=in_specs + next_in,
            out_specs=[row(D_MODEL), row(D_MODEL), row(D_MODEL), row(PLE_DIM)] + next_out,
            out_shape=[jax.ShapeDtypeStruct((s, D_MODEL), F32), jax.ShapeDtypeStruct((s, D_MODEL), BF16),
                       jax.ShapeDtypeStruct((s, D_MODEL), BF16), jax.ShapeDtypeStruct((s, PLE_DIM), BF16)] + next_shape,
            scratch_shapes=scratch, compiler_params=_params(1),
        )(h, g, w_gate, p, w_ple, *follow)
    loss, dgf, dh2, dg, dwg, dwgb, dwp, dwpb = pl.pallas_call(
        body, name=name, grid=(s // t,), in_specs=in_specs + [full(1, D_MODEL), row(D_MODEL)],
        out_specs=[pl.BlockSpec((1, LANES), lambda i: (0, 0)), full(1, D_MODEL), row(D_MODEL), full(1, D_MODEL),
                   full(D_MODEL, D_MODEL), full(D_MODEL, D_MODEL), full(PLE_DIM, D_MODEL), full(PLE_DIM, D_MODEL)],
        out_shape=[jax.ShapeDtypeStruct((1, LANES), F32), jax.ShapeDtypeStruct((1, D_MODEL), F32),
                   jax.ShapeDtypeStruct((s, D_MODEL), F32), jax.ShapeDtypeStruct((1, D_MODEL), F32),
                   jax.ShapeDtypeStruct((D_MODEL, D_MODEL), F32), jax.ShapeDtypeStruct((D_MODEL, D_MODEL), BF16),
                   jax.ShapeDtypeStruct((PLE_DIM, D_MODEL), F32), jax.ShapeDtypeStruct((PLE_DIM, D_MODEL), BF16)],
        compiler_params=_params(1),
    )(h, g, w_gate, p, w_ple, *head)
    return loss, dgf, dh2, dg, (dwg, dwgb), (dwp, dwpb)


def _gate_bwd(dh, gate, pb, w_ple, h, g, w_gate, hn, name, after=()):
    s = h.shape[0]
    t = _row_tile(s, FWD_TILE)
    last = s // t - 1

    def body(dh_ref, gate_ref, pb_ref, wp_ref, h_ref, g_ref, wg_ref, hn_ref, out_ref, dg_ref, dwg_ref, dwgb_ref,
             dwp_ref, dwpb_ref):
        i = pl.program_id(0)

        @pl.when(i == 0)
        def _():
            dg_ref[...] = jnp.zeros_like(dg_ref)
            dwg_ref[...] = jnp.zeros_like(dwg_ref)
            dwp_ref[...] = jnp.zeros_like(dwp_ref)

        d = dh_ref[...]
        gate = gate_ref[...].astype(F32)
        pb = pb_ref[...]
        e = _dot(pb, wp_ref[...])
        dgl = (d * e * gate * (1.0 - gate)).astype(BF16)
        dwg_ref[...] += _dot_tn(hn_ref[...], dgl)
        dwp_ref[...] += _dot_tn(pb, (d * gate).astype(BF16))
        gv = g_ref[...]
        n, rstd, _ = _rms(h_ref[...], gv)
        dx, dg = _rms_bwd(_dot_nt(dgl, wg_ref[...]), n, rstd, gv)
        out_ref[...] = d + dx
        dg_ref[...] += dg

        @pl.when(i == last)
        def _():
            dwgb_ref[...] = dwg_ref[...].astype(BF16)
            dwpb_ref[...] = dwp_ref[...].astype(BF16)

    row = lambda w: pl.BlockSpec((t, w), lambda i: (i, 0))
    full = lambda a, b: pl.BlockSpec((a, b), lambda i: (0, 0))
    dh2, dg, dwg, dwgb, dwp, dwpb = pl.pallas_call(
        _ordered_after(body, 8, after), name=name, grid=(s // t,),
        in_specs=[row(D_MODEL), row(D_MODEL), row(PLE_DIM), _resident((PLE_DIM, D_MODEL)), row(D_MODEL),
                  full(1, D_MODEL), _resident((D_MODEL, D_MODEL)), row(D_MODEL)]
        + [pl.BlockSpec(memory_space=pl.ANY)] * len(after),
        out_specs=[row(D_MODEL), full(1, D_MODEL), full(D_MODEL, D_MODEL), full(D_MODEL, D_MODEL),
                   full(PLE_DIM, D_MODEL), full(PLE_DIM, D_MODEL)],
        out_shape=[jax.ShapeDtypeStruct((s, D_MODEL), F32), jax.ShapeDtypeStruct((1, D_MODEL), F32),
                   jax.ShapeDtypeStruct((D_MODEL, D_MODEL), F32), jax.ShapeDtypeStruct((D_MODEL, D_MODEL), BF16),
                   jax.ShapeDtypeStruct((PLE_DIM, D_MODEL), F32), jax.ShapeDtypeStruct((PLE_DIM, D_MODEL), BF16)],
        compiler_params=_params(1),
    )(dh, gate, pb, w_ple, h, g, w_gate, hn, *after)
    return dh2, dg, (dwg, dwgb), (dwp, dwpb)


def _mlp_bwd(dh, r, h, g, w_up, w_down, name):
    s = h.shape[0]
    t = _row_tile(s, MLP_BWD_TILE)
    nblk = D_FF // FF_BLOCK

    def body(dh_ref, r_ref, h_ref, g_ref, wu_ref, wd_ref, out_ref, dup_ref, dg_ref, dhb_ref):
        @pl.when(pl.program_id(0) == 0)
        def _():
            dg_ref[...] = jnp.zeros_like(dg_ref)

        d = dh_ref[...]
        db = d.astype(BF16)
        dhb_ref[...] = db
        back = None
        for b in range(nblk):
            cols = slice(b * FF_BLOCK, (b + 1) * FF_BLOCK)
            dup = (_dot_nt(db, wd_ref[b]) * (2.0 * r_ref[:, cols].astype(F32))).astype(BF16)
            dup_ref[:, cols] = dup
            part = _dot_nt(dup, wu_ref[b])
            back = part if back is None else back + part
        gv = g_ref[...]
        n, rstd, _ = _rms(h_ref[...], gv)
        dx, dg = _rms_bwd(back, n, rstd, gv)
        out_ref[...] = d + dx
        dg_ref[...] += dg

    row = lambda w: pl.BlockSpec((t, w), lambda i: (i, 0))
    vec = pl.BlockSpec((1, D_MODEL), lambda i: (0, 0))
    resident = lambda shape: pl.BlockSpec(shape, lambda i: (0, 0, 0), pipeline_mode=pl.Buffered(1))
    return pl.pallas_call(
        body, name=name, grid=(s // t,),
        in_specs=[row(D_MODEL), row(D_FF), row(D_MODEL), vec,
                  resident((nblk, D_MODEL, FF_BLOCK)), resident((nblk, FF_BLOCK, D_MODEL))],
        out_specs=[row(D_MODEL), row(D_FF), vec, row(D_MODEL)],
        out_shape=[jax.ShapeDtypeStruct((s, D_MODEL), F32), jax.ShapeDtypeStruct((s, D_FF), BF16),
                   jax.ShapeDtypeStruct((1, D_MODEL), F32), jax.ShapeDtypeStruct((s, D_MODEL), BF16)],
        compiler_params=_params(1),
    )(dh, r, h, g, w_up, w_down)


def _outproj_bwd(dh, w_out, o, lse, ones_bd, a, name):
    s = dh.shape[0]
    t = _row_tile(s, 512)
    last = s // t - 1

    def body(dh_ref, w_ref, o0, o1, o2, l0, l1, l2, bd_ref, a_ref, dp_ref, do0, do1, do2, de0, de1, de2, dw_ref,
             dwb_ref, *stages):
        i = pl.program_id(0)

        @pl.when(i == 0)
        def _():
            dw_ref[...] = jnp.zeros_like(dw_ref)

        stages = _pair_stages(stages)
        dhb = dh_ref[...].astype(BF16)
        dw_ref[...] += _dot_tn(a_ref[...], dhb)

        @pl.when(i == last)
        def _():
            dwb_ref[...] = dw_ref[...].astype(BF16)

        da = _dot_nt(dhb, w_ref[...])
        dp_ref[...] = da[:, 0:POOL_WIDTH]
        ov =[_from_residues(r, stages[i], DILATIONS[i]) for i, r in enumerate((o0, o1, o2))]
        lv = [_from_residues(r, stages[3 + i], DILATIONS[i]) for i, r in enumerate((l0, l1, l2))]
        wts = _group_weights(*lv)
        bd = bd_ref[...]
        cbar = jnp.zeros((t, GROUP_WIDTH), F32)
        for grp, do_ref in enumerate((do0, do1, do2)):
            lo = POOL_WIDTH + grp * GROUP_WIDTH
            dag = da[:, lo:lo + GROUP_WIDTH]
            _to_residues(dag * wts[grp], stages[6 + grp], do_ref, DILATIONS[grp])
            prod = dag * ov[grp]
            hi = prod.astype(BF16)
            low = (prod - hi.astype(F32)).astype(BF16)
            cbar = cbar + wts[grp] * (_dot(hi, bd) + _dot(low, bd))
        for grp, de_ref in enumerate((de0, de1, de2)):
            _to_residues(wts[grp] * cbar, stages[9 + grp], de_ref, DILATIONS[grp])

    row = lambda w: pl.BlockSpec((t, w), lambda i: (i, 0))
    full = lambda a, b: pl.BlockSpec((a, b), lambda i: (0, 0))
    res = [_residue_spec(dil, t) for dil in DILATIONS]
    *outs, dw, dwb = pl.pallas_call(
        body, name=name, grid=(s // t,),
        in_specs=[row(D_MODEL), full(D_MODEL, D_MODEL)] + res + res + [full(GROUP_WIDTH, GROUP_WIDTH), row(D_MODEL)],
        out_specs=[row(POOL_WIDTH)] + res + res + [full(D_MODEL, D_MODEL)] * 2,
        out_shape=[jax.ShapeDtypeStruct((s, POOL_WIDTH), F32)] + [_residue_shape(dil, s, BF16) for dil in DILATIONS]
        + [_residue_shape(dil, s, F32) for dil in DILATIONS]
        + [jax.ShapeDtypeStruct((D_MODEL, D_MODEL), F32), jax.ShapeDtypeStruct((D_MODEL, D_MODEL), BF16)],
        scratch_shapes=_stages(t, 12),
        compiler_params=_params(1),
    )(dh, w_out, *o, *lse, ones_bd, a)
    return (*outs, (dw, dwb))


def _attn_bwd(q, k, v, do, lse, deff, name, after=()):
    dil, length, _ = q.shape
    nb = length // ATTN_BLOCK
    qb = _blocks_per_step(nb)
    nj = nb // qb
    rs = _residues_per_step(dil, nb, qb)
    whole = nj == 1
    tail = slice((qb - 1) * ATTN_BLOCK, qb * ATTN_BLOCK)
    block = lambda qi: slice(qi * ATTN_BLOCK, (qi + 1) * ATTN_BLOCK)

    def body(q_ref, kp_ref, kc_ref, vp_ref, vc_ref, do_ref, lse_ref, de_ref, dq_ref, dk_ref, dv_ref, ck, cv):
        j = pl.program_id(1)

        def compute():
            masks = _head_masks()
            bias = _band_bias(j == 0)
            for rr in range(rs):
                dkc, dvc = [], []
                for qi in range(qb):
                    here, before = block(qi), block(qi - 1)
                    kcat = jnp.concatenate([kp_ref[rr] if qi == 0 else kc_ref[rr, before], kc_ref[rr, here]], axis=0)
                    vcat = jnp.concatenate([vp_ref[rr] if qi == 0 else vc_ref[rr, before], vc_ref[rr, here]], axis=0)
                    qs = _stack_heads(q_ref[rr, here], masks)
                    dos = _stack_heads(do_ref[rr, here], masks)
                    sc = _dot_nt(qs, kcat) + bias[min(qi, 1)]
                    p = jnp.exp(sc - _column_per_head(lse_ref[rr, here]))
                    ds = (p * (_dot_nt(dos, vcat) - _column_per_head(de_ref[rr, here]))).astype(BF16)
                    dq = jnp.zeros((ATTN_BLOCK, GROUP_WIDTH), F32)
                    for hd, msk in enumerate(masks):
                        dq = jnp.where(msk, _dot(ds[block(hd)], kcat), dq)
                    dq_ref[rr, here] = dq.astype(dq_ref.dtype)
                    dkc.append(_dot_tn(ds, qs))
                    dvc.append(_dot_tn(p.astype(BF16), dos))

                for out_ref, carry, parts in ((dk_ref, ck, dkc), (dv_ref, cv, dvc)):
                    full = [parts[qi][ATTN_BLOCK:] + parts[qi + 1][0:ATTN_BLOCK] for qi in range(qb - 1)]
                    if whole:
                        for qi, val in enumerate(full + [parts[qb - 1][ATTN_BLOCK:]]):
                            out_ref[rr, block(qi)] = val.astype(out_ref.dtype)
                        continue

                    @pl.when(j > 0)
                    def _():
                        if qb > 1:
                            out_ref[0, 0:(qb - 1) * ATTN_BLOCK] = carry[0:(qb - 1) * ATTN_BLOCK].astype(out_ref.dtype)
                        out_ref[0, tail] = (carry[tail] + parts[0][0:ATTN_BLOCK]).astype(out_ref.dtype)

                    for qi, val in enumerate(full):
                        carry[block(qi)] = val
                    carry[tail] = parts[qb - 1][ATTN_BLOCK:]

        if whole:
            compute()
        else:
            pl.when(j < nj)(compute)

            @pl.when(j == nj)
            def _():
                dk_ref[0] = ck[...].astype(dk_ref.dtype)
                dv_ref[0] = cv[...].astype(dv_ref.dtype)

    step = lambda j: jnp.minimum(j, nj - 1)
    cur = pl.BlockSpec((rs, qb * ATTN_BLOCK, GROUP_WIDTH), lambda r, j: (r, step(j), 0))
    prev = pl.BlockSpec((rs, ATTN_BLOCK, GROUP_WIDTH), lambda r, j: (r, jnp.maximum(qb * step(j) - 1, 0), 0))
    late = pl.BlockSpec((rs, qb * ATTN_BLOCK, GROUP_WIDTH), lambda r, j: (r, jnp.maximum(j - 1, 0), 0))
    return pl.pallas_call(
        _ordered_after(body, 8, after), name=name, grid=(dil // rs, 1 if whole else nj + 1),
        in_specs=[cur, prev, cur, prev, cur, cur, cur, cur] + [pl.BlockSpec(memory_space=pl.ANY)] * len(after),
        out_specs=[cur, cur if whole else late, cur if whole else late],
        out_shape=[jax.ShapeDtypeStruct(q.shape, BF16)] * 3,
        scratch_shapes=[pltpu.VMEM((qb * ATTN_BLOCK, GROUP_WIDTH), F32)] * 2,
        compiler_params=_params(2),
    )(q, k, k, v, v, do, lse, deff, *after)


def _pool_bwd_tile(i, nt, dp_ref, y_ref, w_ref, sc_ref, dw_ref, dsc_ref, ext, b2, b4, b8):
    t = dp_ref.shape[0]

    @pl.when(i == 0)
    def _():
        ext[t:, :] = jnp.zeros((POOL_HALO + POOL_PAD, POOL_WIDTH), F32)
        for buf in (b2, b4):
            buf[t + POOL_HALO:, :] = jnp.zeros((POOL_PAD, POOL_WIDTH), F32)
        dw_ref[...] = jnp.zeros_like(dw_ref)
        dsc_ref[...] = jnp.zeros_like(dsc_ref)

    dp = dp_ref[...]
    yb = y_ref[...]
    w = w_ref[...]
    dsc_ref[...] += jnp.sum(dp * _dot(yb, w), axis=0, keepdims=True)
    dyo = (dp * sc_ref[...]).astype(BF16)
    dw_ref[...] += _dot_tn(yb, dyo)
    dy = _dot_nt(dyo, w)
    pos = (nt - 1 - i) * t + lax.broadcasted_iota(jnp.int32, (t, POOL_WIDTH), 0)
    gq = dy / jnp.minimum(pos + 1, _pool_lane_window()).astype(F32)
    ext[0:t, :] = gq
    du = _window_sums(ext, b2, b4, b8, t, 0, 0, 1) - dy
    ext[t:t + POOL_HALO, :] = gq[0:POOL_HALO, :]
    return du


def _normproj_bwd(dh, dpool, y, w_bd, scale, dq, dk, dv, rc, rsa, rsb, w_in, h, g, name):
    s = h.shape[0]
    t = _row_tile(s, 512)
    nt = s // t

    def body(dh_ref, dp_ref, y_ref, wbd_ref, sc_ref, q0, q1, q2, k0, k1, k2, v0, v1, v2, c_ref, sa_ref, sb_ref, w_ref,
             h_ref, g_ref, out_ref, dz_ref, dg_ref, dwbd_ref, dsc_ref, ext, b2, b4, b8, *stages):
        step = pl.program_id(0)

        @pl.when(step == 0)
        def _():
            dg_ref[...] = jnp.zeros_like(dg_ref)

        du = _pool_bwd_tile(step, nt, dp_ref, y_ref, wbd_ref, sc_ref, dwbd_ref, dsc_ref, ext, b2, b4, b8)
        c, sa, sb = c_ref[...], sa_ref[...], sb_ref[...]

        def unrot(a, scale):
            halves = [_rot_t(a[:, hf * LANES:(hf + 1) * LANES] * scale, c, sa, sb) for hf in range(2)]
            return jnp.concatenate(halves, axis=1)

        staged = _pair_stages(stages)
        tok = lambda refs, base: [_from_residues(r, staged[base + i], DILATIONS[i]) for i, r in enumerate(refs)]
        chunks = [du]
        chunks += [unrot(a, HEAD_DIM ** -0.5) for a in tok((q0, q1, q2), 0)]
        chunks += [unrot(a, 1.0) for a in tok((k0, k1, k2), 3)]
        chunks += tok((v0, v1, v2), 6)
        acc = jnp.zeros((t, D_MODEL), F32)
        for ci, ch in enumerate(chunks):
            cols = slice(ci * GROUP_WIDTH, (ci + 1) * GROUP_WIDTH)
            cb = ch.astype(BF16)
            dz_ref[:, cols] = cb
            acc = acc + _dot(cb, w_ref[cols, :])
        gv = g_ref[...]
        n, rstd, _ = _rms(h_ref[...], gv)
        dx, dg = _rms_bwd(acc, n, rstd, gv)
        out_ref[...] = dh_ref[...] + dx
        dg_ref[...] += dg

    back = lambda i: nt - 1 - i
    row = lambda w: pl.BlockSpec((t, w), lambda i: (back(i), 0))
    full = lambda a, b: pl.BlockSpec((a, b), lambda i: (0, 0))
    res = [pl.BlockSpec((dil, t // dil, GROUP_WIDTH), lambda i: (0, back(i), 0)) for dil in DILATIONS]
    tables = [pl.BlockSpec((t, LANES), functools.partial(lambda i, k: (back(i), k), k=k)) for k in range(3)]
    return pl.pallas_call(
        body, name=name, grid=(nt,),
        in_specs=[row(D_MODEL), row(POOL_WIDTH), row(POOL_WIDTH), full(POOL_WIDTH, POOL_WIDTH), full(1, POOL_WIDTH)]
        + res * 3 + tables + [full(N_IN, D_MODEL), row(D_MODEL), full(1, D_MODEL)],
        out_specs=[row(D_MODEL), row(N_IN), full(1, D_MODEL), full(POOL_WIDTH, POOL_WIDTH), full(1, POOL_WIDTH)],
        out_shape=[jax.ShapeDtypeStruct((s, D_MODEL), F32), jax.ShapeDtypeStruct((s, N_IN), BF16),
                   jax.ShapeDtypeStruct((1, D_MODEL), F32), jax.ShapeDtypeStruct((POOL_WIDTH, POOL_WIDTH), F32),
                   jax.ShapeDtypeStruct((1, POOL_WIDTH), F32)],
        scratch_shapes=[pltpu.VMEM((t + POOL_HALO + POOL_PAD, POOL_WIDTH), F32)] * 4 + _stages(t, 9),
        compiler_params=_params(1),
    )(dh, dpool, y, w_bd, scale, *dq, *dk, *dv, rc, rsa, rsb, w_in, h, g)


def _matmul_tn(a, b, name, *, square_a=False, tm=None, tn=None, blocked_out=False, after=()):
    s, m = a.shape
    n = b.shape[1]
    tk = _row_tile(s, 2048)
    tm = tm or min(m, 1024)
    tn = tn or min(n, 1024)
    assert m % tm == 0 and n % tn == 0
    nk = s // tk
    nsub = tn // FF_BLOCK if blocked_out else 1

    def body(a_ref, b_ref, o_ref, ob_ref, acc):
        k = pl.program_id(2)

        def product():
            av = a_ref[...]
            if square_a:
                av = av.astype(F32)
                av = av * av
            return _dot_tn(av.astype(BF16), b_ref[...].astype(BF16))

        def emit(total):
            if blocked_out:
                for sub in range(nsub):
                    cols = slice(sub * FF_BLOCK, (sub + 1) * FF_BLOCK)
                    o_ref[sub] = total[:, cols]
                    ob_ref[sub] = total[:, cols].astype(BF16)
            else:
                o_ref[...] = total
                ob_ref[...] = total.astype(BF16)

        if nk == 1:
            emit(product())
            return

        @pl.when(k == 0)
        def _():
            acc[...] = product()

        @pl.when((k > 0) & (k < nk - 1))
        def _():
            acc[...] += product()

        @pl.when(k == nk - 1)
        def _():
            emit(acc[...] + product())

    if blocked_out:
        shape = (n // FF_BLOCK, m, FF_BLOCK)
        out_spec = pl.BlockSpec((nsub, tm, FF_BLOCK), lambda i, j, k: (j, i, 0))
    else:
        shape = (m, n)
        out_spec = pl.BlockSpec((tm, tn), lambda i, j, k: (i, j))
    return pl.pallas_call(
        _ordered_after(body, 2, after), name=name, grid=(m // tm, n // tn, nk),
        in_specs=[pl.BlockSpec((tk, tm), lambda i, j, k: (k, i)), pl.BlockSpec((tk, tn), lambda i, j, k: (k, j))]
        + [pl.BlockSpec(memory_space=pl.ANY)] * len(after),
        out_specs=[out_spec, out_spec],
        out_shape=[jax.ShapeDtypeStruct(shape, F32), jax.ShapeDtypeStruct(shape, BF16)],
        scratch_shapes=[pltpu.VMEM((tm, tn), F32)],
        compiler_params=_params(3),
    )(a, b, *after)


def _adamw_math(w, g, m, v):
    m = ADAM_B1 * m + (1.0 - ADAM_B1) * g
    v = ADAM_B2 * v + (1.0 - ADAM_B2) * (g * g)
    m_hat = m / (1.0 - ADAM_B1 ** ADAM_STEP)
    v_hat = v / (1.0 - ADAM_B2 ** ADAM_STEP)
    delta = -ADAM_LR * (m_hat / (jnp.sqrt(v_hat) + ADAM_EPS) + ADAM_WD * w)
    return delta, m, v


def _sum_chunks_body(own0_ref, own1_ref, r0_ref, r1_ref):
    layer0 = pl.program_id(0) == 0
    g = jnp.where(layer0, own0_ref[...], own1_ref[...])
    for k in range(N_DEV - 1):
        g = g + jnp.where(layer0, r0_ref[k], r1_ref[k]).astype(F32)
    return g


def _chunk_specs(t, cols):
    rows_of = lambda layer: (lambda l, i: jnp.where(l == layer, i, 0))
    blk = pl.BlockSpec((None, t, cols), lambda l, i, me: (l, i, 0))
    own = [pl.BlockSpec((None, t, cols), functools.partial(lambda l, i, me, pick: (me[0], pick(l, i), 0), pick=rows_of(ly)))
           for ly in range(2)]
    recv = [pl.BlockSpec((N_DEV - 1, t, cols), functools.partial(lambda l, i, me, pick: (0, pick(l, i), 0), pick=rows_of(ly)))
            for ly in range(2)]
    return blk, own + recv


def _adamw_sharded(w, m, v, chunks, me, name):
    _, rows, cols = w.shape
    t = max(d for d in range(SUBLANES, min(rows, 256) + 1, SUBLANES) if rows % d == 0)

    def body(me_ref, w_ref, m_ref, v_ref, own0_ref, own1_ref, r0_ref, r1_ref, g_ref, d_ref, nm_ref, nv_ref):
        g = _sum_chunks_body(own0_ref, own1_ref, r0_ref, r1_ref)
        g_ref[...] = g
        d_ref[...], nm_ref[...], nv_ref[...] = _adamw_math(w_ref[...], g, m_ref[...], v_ref[...])

    blk, chunk_specs = _chunk_specs(t, cols)
    return pl.pallas_call(
        body, name=name,
        grid_spec=pltpu.PrefetchScalarGridSpec(
            num_scalar_prefetch=1, grid=(2, rows // t), in_specs=[blk, blk, blk] + chunk_specs, out_specs=[blk] * 4),
        out_shape=[jax.ShapeDtypeStruct(w.shape, F32)] * 4,
        compiler_params=_params(2),
    )(me, w, m, v, *chunks)


def _adamw_small(w, g8, m, v, name):
    rows = dict(norm1=(0, 2), norm2=(2, 4), norm3=(4, 6), final_norm=(6, 7), pool_w=(8, 40))
    shaped = lambda t: [t[n].reshape(rows[n][1] - rows[n][0], D_MODEL) if n in rows else t[n] for n in SMALL]
    k = len(SMALL)

    def body(g8_ref, *refs):
        w_refs, m_refs, v_refs = refs[:k], refs[k:2 * k], refs[2 * k:3 * k]
        outs = [refs[(3 + i) * k:(4 + i) * k] for i in range(4)]
        spare_ref = refs[-1]
        g = g8_ref[0]
        for dev in range(1, N_DEV):
            g = g + g8_ref[dev]
        spare_ref[...] = g[7:8, 2 * POOL_WIDTH:2 * POOL_WIDTH + LANES]
        for i, n in enumerate(SMALL):
            if n in rows:
                pieces = [(slice(None), g[rows[n][0]:rows[n][1]])]
            else:
                pieces = [(slice(ly, ly + 1), g[7:8, ly * POOL_WIDTH:(ly + 1) * POOL_WIDTH]) for ly in range(2)]
            for at, gp in pieces:
                new = _adamw_math(w_refs[i][at], gp, m_refs[i][at], v_refs[i][at])
                for out, val in zip(outs, (gp, *new)):
                    out[i][at] = val

    ins = shaped(w) + shaped(m) + shaped(v)
    res = pl.pallas_call(
        body, name=name,
        out_shape=[jax.ShapeDtypeStruct(a.shape, F32) for a in shaped(w)] * 4 + [jax.ShapeDtypeStruct((1, LANES), F32)],
        compiler_params=pltpu.CompilerParams(vmem_limit_bytes=VMEM_LIMIT),
    )(g8, *ins)
    dicts = [{n: a.reshape(w[n].shape) for n, a in zip(SMALL, res[i * k:(i + 1) * k])} for i in range(4)]
    return (*dicts, res[-1])


def _peer(k):
    x, y, c = lax.axis_index("x"), lax.axis_index("y"), lax.axis_index("c")
    return (1 - x if k & 4 else x, 1 - y if k & 2 else y, 1 - c if k & 1 else c)


def _linear(dev):
    return 4 * dev[0] + 2 * dev[1] + dev[2]


HBM_SPEC = pl.BlockSpec(memory_space=pltpu.HBM)
SEM_SPEC = pl.BlockSpec(memory_space=pltpu.SEMAPHORE)
ANY_SPEC = pl.BlockSpec(memory_space=pl.ANY)
EFFECT = pltpu.SideEffectType.DATAFLOW_SIDE_EFFECTING


def _in_hbm(a):
    return pltpu.with_memory_space_constraint(a, pltpu.HBM)


class _Exchange:
    def __init__(self, name, groups, scatter, after=()):
        self.name, self.scatter = name, scatter
        self.sizes = sizes = [len(g) for g in groups]
        srcs = [a for g in groups for a in g]
        n, ng = len(srcs), len(groups)
        lead = (N_DEV - 1,) if scatter else (N_DEV,)
        shapes = [lead + (a.shape[1:] if scatter else a.shape) for a in srcs]
        lands = [lax.empty(sh, a.dtype) for sh, a in zip(shapes, srcs)]
        if not scatter:
            lands = [lax.dynamic_update_slice_in_dim(land, a[None], _linear(_peer(0)), axis=0)
                     for land, a in zip(lands, srcs)]
        offsets = [sum(sizes[:gi]) for gi in range(ng)]
        copy = self._copy

        def body(*refs):
            src, land = refs[:n], refs[n:2 * n]
            sems = refs[2 * n + len(after):2 * n + len(after) + 2 * ng]
            token = refs[-1]
            for gi in range(ng):
                for wi in range(sizes[gi]):
                    w = offsets[gi] + wi
                    for k in range(1, N_DEV):
                        copy(src[w], land[w], sems[2 * gi], sems[2 * gi + 1], wi, k).start()
            token[...] = jnp.zeros_like(token)

        sem_shapes = [pltpu.SemaphoreType.DMA(((N_DEV - 1) * sz,)) for sz in sizes for _ in range(2)]
        outs = pl.pallas_call(
            body, name=name + "_start",
            in_specs=[HBM_SPEC] * (2 * n) + [ANY_SPEC] * len(after),
            out_specs=[SEM_SPEC] * (2 * ng) + [HBM_SPEC] * (2 * n) + [pl.BlockSpec(memory_space=pltpu.VMEM)],
            out_shape=sem_shapes + [pltpu.HBM(a.shape, a.dtype) for a in srcs + lands]
            + [jax.ShapeDtypeStruct((8, LANES), F32)],
            input_output_aliases={i: 2 * ng + i for i in range(2 * n)},
            compiler_params=pltpu.CompilerParams(has_side_effects=EFFECT),
        )(*[_in_hbm(a) for a in srcs + lands], *after)
        self.sems = [outs[2 * gi:2 * gi + 2] for gi in range(ng)]
        thru = outs[2 * ng:2 * ng + 2 * n]
        self.srcs = [thru[offsets[gi]:offsets[gi] + sizes[gi]] for gi in range(ng)]
        self.lands = [thru[n + offsets[gi]:n + offsets[gi] + sizes[gi]] for gi in range(ng)]
        self.token = outs[-1]

    def _copy(self, src, land, send_sems, recv_sems, wi, k):
        to = _peer(k)
        if self.scatter:
            src_ref, dst_ref = src.at[_linear(to)], land.at[k - 1]
        else:
            src_ref, dst_ref = src, land.at[_linear(_peer(0))]
        return pltpu.make_async_remote_copy(
            src_ref=src_ref, dst_ref=dst_ref, send_sem=send_sems.at[(N_DEV - 1) * wi + k - 1],
            recv_sem=recv_sems.at[(N_DEV - 1) * wi + k - 1], device_id=to, device_id_type=MESH)

    def wait(self, gi, after):
        n = self.sizes[gi]
        copy = self._copy

        def body(*refs):
            src, land = refs[:n], refs[n:2 * n]
            send_sems, recv_sems = refs[2 * n], refs[2 * n + 1]
            for wi in range(n):
                for k in range(1, N_DEV):
                    cp = copy(src[wi], land[wi], send_sems, recv_sems, wi, k)
                    cp.wait_send()
                    cp.wait_recv()

        arrays = list(self.srcs[gi]) + list(self.lands[gi])
        outs = pl.pallas_call(
            body, name=f"{self.name}_wait{gi}",
            in_specs=[HBM_SPEC] * (2 * n) + [SEM_SPEC, SEM_SPEC] + [ANY_SPEC] * len(after),
            out_specs=[HBM_SPEC] * (2 * n),
            out_shape=[pltpu.HBM(a.shape, a.dtype) for a in arrays],
            input_output_aliases={i: i for i in range(2 * n)},
            compiler_params=pltpu.CompilerParams(has_side_effects=EFFECT),
        )(*arrays, *self.sems[gi], *after)
        return outs[:n], outs[n:]


class _TwoHopGather:
    NEAR = (1, 2, 4, 6)
    FAR = (2, 4, 6)

    def __init__(self, name, src):
        self.name = name
        land = lax.empty((N_DEV,) + src.shape, src.dtype)
        land = lax.dynamic_update_slice_in_dim(land, src[None], _linear(_peer(0)), axis=0)
        hop = self._hop

        def body(src_ref, land_ref, send_sems, recv_sems, src_out, land_out, token):
            for i, k in enumerate(self.NEAR):
                hop(src_ref, land_ref, send_sems, recv_sems, i, k).start()
            token[...] = jnp.zeros_like(token)

        sem = pltpu.SemaphoreType.DMA((len(self.NEAR),))
        *self.sems, self.src, self.land, self.token = pl.pallas_call(
            body, name=name + "_start", in_specs=[HBM_SPEC] * 2,
            out_specs=[SEM_SPEC] * 2 + [HBM_SPEC] * 2 + [pl.BlockSpec(memory_space=pltpu.VMEM)],
            out_shape=[sem, sem, pltpu.HBM(src.shape, src.dtype), pltpu.HBM(land.shape, land.dtype),
                       jax.ShapeDtypeStruct((8, LANES), F32)],
            input_output_aliases={0: 2, 1: 3},
            compiler_params=pltpu.CompilerParams(has_side_effects=EFFECT),
        )(_in_hbm(src), _in_hbm(land))

    @staticmethod
    def _hop(src, land, send_sems, recv_sems, i, k):
        return pltpu.make_async_remote_copy(
            src_ref=src, dst_ref=land.at[_linear(_peer(0))], send_sem=send_sems.at[i], recv_sem=recv_sems.at[i],
            device_id=_peer(k), device_id_type=MESH)

    def wait(self, after):
        hop = self._hop

        def landed(src, land, send_sems, recv_sems, *rest):
            for i, k in enumerate(self.NEAR):
                cp = hop(src, land, send_sems, recv_sems, i, k)
                cp.wait_send()
                cp.wait_recv()

        src, land = pl.pallas_call(
            landed, name=self.name + "_wait0",
            in_specs=[HBM_SPEC] * 2 + [SEM_SPEC] * 2 + [ANY_SPEC] * len(after), out_specs=[HBM_SPEC] * 2,
            out_shape=[pltpu.HBM(self.src.shape, self.src.dtype), pltpu.HBM(self.land.shape, self.land.dtype)],
            input_output_aliases={0: 0, 1: 1},
            compiler_params=pltpu.CompilerParams(has_side_effects=EFFECT),
        )(self.src, self.land, *self.sems, *after)

        def pass_on(land, land_out, send2, recv2):
            onward = []
            for i, k in enumerate(self.FAR):
                slot = _linear(_peer(k))
                onward.append(pltpu.make_async_remote_copy(
                    src_ref=land.at[slot], dst_ref=land.at[slot], send_sem=send2.at[i], recv_sem=recv2.at[i],
                    device_id=_peer(1), device_id_type=MESH))
                onward[-1].start()
            for cp in onward:
                cp.wait_send()
                cp.wait_recv()

        sem = pltpu.SemaphoreType.DMA((len(self.FAR),))
        return pl.pallas_call(
            pass_on, name=self.name + "_pass_on", in_specs=[HBM_SPEC], out_specs=HBM_SPEC,
            out_shape=pltpu.HBM(land.shape, land.dtype), input_output_aliases={0: 0}, scratch_shapes=[sem, sem],
            compiler_params=pltpu.CompilerParams(has_side_effects=EFFECT),
        )(land)


def _rotary_tables(positions):
    rot_dim = HEAD_DIM // 4
    inv_freq = ROPE_THETA ** (-jnp.arange(0, rot_dim, 2, dtype=F32) / rot_dim)
    ang = positions.astype(F32)[:, None] * inv_freq
    cs = jnp.concatenate([jnp.cos(ang), jnp.sin(ang)], axis=1)
    dim = jnp.arange(LANES) % HEAD_DIM
    first, second = dim < ROT_SHIFT, (dim >= ROT_SHIFT) & (dim < rot_dim)
    src = jnp.arange(2 * ROT_SHIFT)[:, None]
    angle = (dim % ROT_SHIFT)[None, :]
    c = jnp.where((first | second)[None, :] & (src == angle), 1.0, 0.0)
    sa = jnp.where(second[None, :] & (src == angle + ROT_SHIFT), 1.0, 0.0)
    sb = jnp.where(first[None, :] & (src == angle + ROT_SHIFT), -1.0, 0.0)
    spread = jnp.concatenate([c, sa, sb], axis=1).astype(F32)
    base = jnp.concatenate([jnp.where(first | second, 0.0, 1.0), jnp.zeros((2 * LANES,))]).astype(F32)[None, :]
    return jnp.dot(cs, spread, precision=lax.Precision.HIGHEST, preferred_element_type=F32) + base


def _block_diag(pool_w):
    gc = pool_w.shape[-1]
    out = jnp.zeros((POOL_WIDTH, POOL_WIDTH), pool_w.dtype)
    for grp in range(pool_w.shape[0]):
        out = lax.dynamic_update_slice(out, pool_w[grp], (grp * gc, grp * gc))
    return out


def _diag_blocks(a):
    gc = POOL_WIDTH // len(POOL_WINDOWS)
    return jnp.stack([a[grp * gc:(grp + 1) * gc, grp * gc:(grp + 1) * gc] for grp in range(len(POOL_WINDOWS))])


def _local_step(x, p, positions, loss_target, norm1, pool_w, pool_scale, norm2, norm3, final_norm, weights, send):
    rc = rsa = rsb = _rotary_tables(positions)
    ones_bd = _block_diag(jnp.ones((4, HEAD_DIM, HEAD_DIM), BF16))
    w_bds = [_block_diag(pool_w[i]).astype(BF16) for i in range(2)]
    saved = []
    h = x
    for i in range(2):
        tag = f"_l{i}"
        g1, g2, g3 = norm1[i:i + 1], norm2[i:i + 1], norm3[i:i + 1]
        w_bd = w_bds[i]
        scale = pool_scale[i:i + 1]
        if i == 0:
            w_in = weights(i, "in", (h, rc, *w_bds, ones_bd))
            hn1, u, *qkv = _normproj_fwd(h, g1, w_in, rc, rsa, rsb, "normproj_fwd" + tag)
        else:
            w_in, (hn1, u, *qkv) = ahead
        qkv = [qkv[3 * grp:3 * grp + 3] for grp in range(3)]
        started = weights(i, "prefetch", (hn1,))
        o, lse = zip(*[_attn_fwd(*qkv[grp], f"attn_fwd{tag}_g{grp}", after=started) for grp in range(3)])
        w_out = weights(i, "out", o)
        h1, a, y = _outproj_fwd(h, u, w_bd, scale, o, lse, w_out, "outproj_fwd" + tag)
        w_up, w_down = weights(i, "mlp", (h1,))
        h2, hn2, r = _mlp_fwd(h1, g2, w_up, w_down, "mlp_fwd" + tag)
        w_gate, w_ple = weights(i, "gate", (h2,))
        h0 = h
        if i == 0:
            w_in_next = weights(1, "in", (h2,))
            h, hn3, gate, pb, *ahead = _gate_fwd(h2, g3, w_gate, p, i, w_ple, "gate_normproj_fwd",
                                                 follow=(norm1[1:2], w_in_next, rc, rsa, rsb))
            ahead = (w_in_next, ahead)
        else:
            hn3 = gate = pb = None
            loss, d_final, *top = _gate_fwd(h2, g3, w_gate, p, i, w_ple, "gate_loss_gate_bwd",
                                            head=(final_norm.reshape(1, D_MODEL), loss_target))
        saved.append(dict(h0=h0, hn1=hn1, qkv=qkv, y=y, o=o, lse=lse, a=a, h1=h1, hn2=hn2, r=r, h2=h2,
                          hn3=hn3, gate=gate, pb=pb, w_bd=w_bd, scale=scale, g1=g1, g2=g2, g3=g3,
                          w_in=w_in, w_out=w_out, w_up=w_up, w_down=w_down, w_gate=w_gate, w_ple=w_ple))

    grads = [None, None]
    sent = ()
    for i in (1, 0):
        tag = f"_l{i}"
        sv = saved[i]
        if i == 1:
            dh2, dg3, dw_gate, dw_ple = top
        else:
            dh2, dg3, dw_gate, dw_ple = _gate_bwd(dh, sv["gate"], sv["pb"], sv["w_ple"], sv["h2"], sv["g3"],
                                                  sv["w_gate"], sv["hn3"], "gate_bwd" + tag, after=sent)
        dh1, dup, dg2, dh2b = _mlp_bwd(dh2, sv["r"], sv["h1"], sv["g2"], sv["w_up"], sv["w_down"], "mlp_bwd" + tag)
        dw_down = _matmul_tn(sv["r"], dh2b, "dw_down" + tag, square_a=True)
        dw_up = _matmul_tn(sv["hn2"], dup, "dw_up" + tag, blocked_out=True)
        dpool, do0, do1, do2, de0, de1, de2, dw_out = _outproj_bwd(dh1, sv["w_out"], sv["o"], sv["lse"], ones_bd,
                                                                   sv["a"], "outproj_bwd" + tag)
        sent = send(i, "main", dict(w_gate=dw_gate, w_ple=dw_ple, w_down=dw_down, w_up=dw_up, w_out=dw_out))
        dqkv = [_attn_bwd(*sv["qkv"][grp], do_g, sv["lse"][grp], de_g, f"attn_bwd{tag}_g{grp}", after=sent)
                for grp, (do_g, de_g) in enumerate(((do0, de0), (do1, de1), (do2, de2)))]
        dq, dk, dv = zip(*dqkv)
        dh, dz, dg1, dw_bd, dscale = _normproj_bwd(dh1, dpool, sv["y"], sv["w_bd"], sv["scale"], dq, dk, dv, rc, rsa, rsb,
                                                   sv["w_in"], sv["h0"], sv["g1"], "normproj_bwd" + tag)
        grads[i] = dict(norm1=dg1, norm2=dg2, norm3=dg3, pool_w=_diag_blocks(dw_bd), pool_scale=dscale)
        small_sent = send(0, "small", (grads, d_final, loss)) if i == 0 else ()
        dw_in = _matmul_tn(dz, sv["hn1"], "dw_in" + tag, tm=N_IN // 2, after=small_sent)
        sent = send(i, "in", dict(w_in=dw_in))
    return dh, sent


def _pack_small(norm1, norm2, norm3, final_norm, pool_scale, pool_w, spare=None):
    spare = jnp.zeros((1, LANES), F32) if spare is None else spare
    scale_row = jnp.concatenate([pool_scale.reshape(1, 2 * POOL_WIDTH), spare,
                                 jnp.zeros((1, D_MODEL - 2 * POOL_WIDTH - LANES), F32)], axis=1)
    return jnp.concatenate([norm1, norm2, norm3, final_norm.reshape(1, D_MODEL), scale_row,
                            pool_w.reshape(32, D_MODEL)], axis=0)


def _chunks_cols(a, cols):
    return a.reshape(a.shape[0], N_DEV, cols).transpose(1, 0, 2)


def _chunks_rows(a, rows):
    return a.reshape(N_DEV, rows, a.shape[1])


BIG = ("w_in", "w_out", "w_up", "w_down", "w_gate", "w_ple")
SMALL = ("norm1", "norm2", "norm3", "final_norm", "pool_scale", "pool_w")
ORDER = ("norm1", "w_in", "pool_w", "pool_scale", "w_out", "norm2", "w_up", "w_down", "norm3", "w_gate", "w_ple",
         "final_norm")


def kernel(x, p, positions, norm1, w_in, pool_w, pool_scale, w_out, norm2, w_up, w_down, norm3, w_gate, w_ple, final_norm, loss_target, m_norm1, m_w_in, m_pool_w, m_pool_scale, m_w_out, m_norm2, m_w_up, m_w_down, m_norm3, m_w_gate, m_w_ple, m_final_norm, v_norm1, v_w_in, v_pool_w, v_pool_scale, v_w_out, v_norm2, v_w_up, v_w_down, v_norm3, v_w_gate, v_w_ple, v_final_norm):
    w = dict(norm1=norm1, w_in=w_in, pool_w=pool_w, pool_scale=pool_scale, w_out=w_out, norm2=norm2, w_up=w_up,
             w_down=w_down, norm3=norm3, w_gate=w_gate, w_ple=w_ple, final_norm=final_norm)
    m = dict(norm1=m_norm1, w_in=m_w_in, pool_w=m_pool_w, pool_scale=m_pool_scale, w_out=m_w_out, norm2=m_norm2,
             w_up=m_w_up, w_down=m_w_down, norm3=m_norm3, w_gate=m_w_gate, w_ple=m_w_ple, final_norm=m_final_norm)
    v = dict(norm1=v_norm1, w_in=v_w_in, pool_w=v_pool_w, pool_scale=v_pool_scale, w_out=v_w_out, norm2=v_norm2,
             w_up=v_w_up, w_down=v_w_down, norm3=v_norm3, w_gate=v_w_gate, w_ple=v_w_ple, final_norm=v_final_norm)
    seq = x.shape[1]

    bf = {n: [w[n][layer].astype(BF16) for layer in range(2)] for n in BIG}
    bf["w_in"] = [a.T for a in bf["w_in"]]
    me = 4 * lax.axis_index("x") + 2 * lax.axis_index("y") + lax.axis_index("c")
    parts = dict(zip(("in", "out", "mlp", "gate"), (("w_in",), ("w_out",), ("w_up", "w_down"), ("w_gate", "w_ple"))))
    first = _TwoHopGather("gather_first", bf["w_in"][0])
    later = [pt for pt in parts if pt != "in"]
    gathers = [_Exchange("gather_l0", [[bf[n][0] for n in parts[pt]] for pt in later], scatter=False,
                         after=(first.token,))]
    unpack = dict(w_in=lambda a: a.reshape(N_IN, D_MODEL),
                  w_out=lambda a: a.reshape(D_MODEL, D_MODEL), w_gate=lambda a: a.reshape(D_MODEL, D_MODEL),
                  w_ple=lambda a: a.transpose(1, 0, 2).reshape(PLE_DIM, D_MODEL), w_up=lambda a: a, w_down=lambda a: a)

    def weights(layer, part, after):
        if part == "prefetch":
            if layer != 0:
                return ()
            gathers.append(_Exchange("gather_l1", [[bf[n][1] for n in parts[pt]] for pt in parts], scatter=False,
                                     after=after))
            return (gathers[1].token,)
        if layer == 0 and part == "in":
            lands = [first.wait((*after, gathers[0].token))]
        elif layer == 0:
            _, lands = gathers[0].wait(later.index(part), after)
        else:
            _, lands = gathers[1].wait(tuple(parts).index(part), after)
        full = [unpack[n](land) for n, land in zip(parts[part], lands)]
        return full if len(full) > 1 else full[0]

    to_chunks = dict(w_in=lambda a: _chunks_rows(a, N_IN // N_DEV),
                     w_out=lambda a: _chunks_rows(a, D_MODEL // N_DEV),
                     w_up=lambda a: a, w_down=lambda a: _chunks_rows(a, FF_BLOCK),
                     w_gate=lambda a: _chunks_rows(a, D_MODEL // N_DEV), w_ple=lambda a: _chunks_cols(a, D_MODEL // N_DEV))
    own = {n: [None, None] for n in BIG}
    scatters = {}

    def send(layer, part, grads):
        if part == "small":
            per_layer, d_final, loss = grads
            pack = _pack_small(
                *[jnp.concatenate([per_layer[0][n], per_layer[1][n]], axis=0) for n in ("norm1", "norm2", "norm3")],
                d_final.reshape(D_MODEL),
                jnp.concatenate([per_layer[0]["pool_scale"], per_layer[1]["pool_scale"]], axis=0),
                jnp.stack([per_layer[0]["pool_w"], per_layer[1]["pool_w"]]), spare=loss)
            scatters["small"] = _Exchange("gather_small", [[pack]], scatter=False)
            return (scatters["small"].token,)
        for n, (g32, _) in grads.items():
            own[n][layer] = to_chunks[n](g32)
        ex = _Exchange(f"scatter_{part}_l{layer}", [[to_chunks[n](g16) for n, (_, g16) in grads.items()]], scatter=True)
        scatters[layer, part] = (tuple(grads), ex)
        return (ex.token,)

    _, pool_w_late = lax.optimization_barrier((first.token, pool_w))
    dx, sent = _local_step(
        x.reshape(seq, D_MODEL), p.reshape(2, seq, PLE_DIM), positions.reshape(seq), loss_target.reshape(seq, D_MODEL),
        norm1, pool_w_late, pool_scale, norm2, norm3, final_norm, weights, send)

    g_out, d_out, m_out, v_out = {}, {}, {}, {}
    my_index = me.reshape(1)
    for part in ("main", "in"):
        recv = {}
        for layer in (1, 0):
            names, ex = scatters[layer, part]
            for n, r in zip(names, ex.wait(0, sent)[1]):
                recv[n, layer] = r
        sent = ()
        for n in names:
            grad = (*own[n], recv[n, 0], recv[n, 1])
            turn = (lambda a: a.transpose(0, 2, 1)) if n == "w_in" else (lambda a: a)
            updated = _adamw_sharded(turn(w[n]), turn(m[n]), turn(v[n]), grad, my_index, "adamw_" + n)
            g_out[n], d_out[n], m_out[n], v_out[n] = map(turn, updated)
            sent += (updated[1],)
    _, (small_g8,) = scatters["small"].wait(0, sent)
    *small, spare = _adamw_small(w, small_g8, m, v, "adamw_small")
    for dst, a in zip((g_out, d_out, m_out, v_out), small):
        dst.update(a)

    return (spare[0, 0], dx.reshape(1, seq, D_MODEL), *[g_out[n] for n in ORDER], *[d_out[n] for n in ORDER],
            *[m_out[n] for n in ORDER], *[v_out[n] for n in ORDER])
```
